```python
import jax
import jax.numpy as jnp
from jax import lax
import numpy as np

D_MODEL = 1024
BATCH = 8
SEQ = 2048
DEPTH = 1

N_MEM = 256
GRID_W = 64
EPS = 1e-6

ATT_HEADS = 8
ATT_KV_HEADS = 2
ATT_HEAD_DIM = 64
ATT_Q_DIM = ATT_HEADS * ATT_HEAD_DIM
ATT_KV_DIM = ATT_KV_HEADS * ATT_HEAD_DIM
ATT_BLOCK = 128
ROPE_THETA = 10000.0
ROPE_PAIRS_PER_AXIS = ATT_HEAD_DIM // 4

HG_HEADS = 4
HG_HEAD_K = 128
HG_HEAD_V = 128
HG_KEY_DIM = HG_HEADS * HG_HEAD_K
HG_VAL_DIM = HG_HEADS * HG_HEAD_V
HG_CHUNK = 32

MIX_WIDTH = ATT_Q_DIM + HG_VAL_DIM
IN_SPLITS = (ATT_Q_DIM, ATT_KV_DIM, ATT_KV_DIM, HG_KEY_DIM, HG_KEY_DIM, HG_KEY_DIM, HG_VAL_DIM, HG_VAL_DIM)
N_IN = ATT_Q_DIM + 2 * ATT_KV_DIM + 3 * HG_KEY_DIM + 2 * HG_VAL_DIM

X_HEADS = 4
X_HEAD_DIM = D_MODEL // X_HEADS

D_FF = 2816
CONV_W = 3

kernel_name = "hymba_axial_gqa_hgrn2_sandwich_convffn"


def rmsnorm(x, g):
    xf = x.astype(jnp.float32)
    y = xf * lax.rsqrt(jnp.mean(xf * xf, axis=-1, keepdims=True) + EPS)
    return (y * g.astype(jnp.float32)).astype(x.dtype)


def split_columns(p, sizes):
    idx = np.cumsum(np.array(sizes))[:-1].tolist()
    return jnp.split(p, idx, axis=-1)


def axial_rope_tables(n):
    rows = n // GRID_W
    r, c = jnp.meshgrid(jnp.arange(rows), jnp.arange(GRID_W), indexing="ij")
    inv = jnp.power(ROPE_THETA, -jnp.arange(ROPE_PAIRS_PER_AXIS, dtype=jnp.float32) / ROPE_PAIRS_PER_AXIS)
    ang = jnp.concatenate([r.reshape(-1, 1).astype(jnp.float32) * inv,
                           c.reshape(-1, 1).astype(jnp.float32) * inv], axis=-1)
    return jnp.cos(ang), jnp.sin(ang)


def apply_rope(x, cos, sin):
    b, n, h, d = x.shape
    xp = x.reshape(b, n, h, d // 2, 2)
    x0, x1 = xp[..., 0], xp[..., 1]
    c = cos[None, :, None, :].astype(x.dtype)
    s = sin[None, :, None, :].astype(x.dtype)
    return jnp.stack([x0 * c - x1 * s, x0 * s + x1 * c], axis=-1).reshape(b, n, h, d)


def axial_gqa_attention(q, k, v):
    b, n = q.shape[0], q.shape[1]
    grp = ATT_HEADS // ATT_KV_HEADS
    nb = n // ATT_BLOCK
    qb = jnp.moveaxis(q.reshape(b, nb, ATT_BLOCK, ATT_KV_HEADS, grp, ATT_HEAD_DIM), 1, 0)
    scale = ATT_HEAD_DIM ** -0.5

    def one_block(qblk):
        s = jnp.einsum("bqhgd,bkhd->bhgqk", qblk, k).astype(jnp.float32) * scale
        p = jax.nn.softmax(s, axis=-1).astype(v.dtype)
        return jnp.einsum("bhgqk,bkhd->bqhgd", p, v)

    o = lax.map(one_block, qb)
    return jnp.moveaxis(o, 0, 1).reshape(b, n, ATT_Q_DIM)


def bidirectional_gated_scan(q, logf_fwd, logf_bwd, v):
    b, n, h, kd = q.shape
    vd = v.shape[-1]
    nc = n // HG_CHUNK

    def flip(a):
        return a[:, ::-1]

    qs = jnp.stack([q, flip(q)])
    lf = jnp.stack([logf_fwd, flip(logf_bwd)])
    vs = jnp.stack([v, flip(v)])
    ks = -jnp.expm1(lf)

    def to_chunks(a):
        a = a.reshape(2, b, nc, HG_CHUNK, h, a.shape[-1])
        return jnp.transpose(a, (2, 0, 1, 4, 3, 5))

    tri = jnp.tril(jnp.ones((HG_CHUNK, HG_CHUNK), dtype=bool))[:, :, None]

    def step(S, xs):
        qx, kx, vx, lx = xs
        bcum = jnp.cumsum(lx, axis=-2)
        diff = bcum[..., :, None, :] - bcum[..., None, :, :]
        dec = jnp.exp(jnp.where(tri, diff, -jnp.inf))
        a = jnp.einsum("...tk,...sk,...tsk->...ts", qx, kx, dec)
        o_intra = jnp.einsum("...ts,...sv->...tv", a, vx)
        o_inter = jnp.einsum("...tk,...kv->...tv", qx * jnp.exp(bcum), S)
        b_last = bcum[..., -1:, :]
        k_dec = kx * jnp.exp(b_last - bcum)
        S_new = jnp.exp(b_last[..., 0, :])[..., :, None] * S + jnp.einsum("...sk,...sv->...kv", k_dec, vx)
        return S_new, o_intra + o_inter

    S0 = jnp.zeros((2, b, h, kd, vd), jnp.float32)
    _, out = lax.scan(step, S0, (to_chunks(qs), to_chunks(ks), to_chunks(vs), to_chunks(lf)))
    out = jnp.transpose(out, (1, 2, 0, 4, 3, 5)).reshape(2, b, n, h, vd)
    return out[0] + flip(out[1])


def hgrn2_group(q, zf_fwd, zf_bwd, i_in, g, lb, out_g):
    b, n = q.shape[0], q.shape[1]

    def heads(a, d):
        return a.reshape(b, n, HG_HEADS, d)

    def log_forget(z, lb_dir):
        f = lb_dir + (1.0 - lb_dir) * jax.nn.sigmoid(z.astype(jnp.float32))
        return heads(jnp.log(f), HG_HEAD_K)

    qf = heads(jax.nn.silu(q.astype(jnp.float32)), HG_HEAD_K)
    o = bidirectional_gated_scan(qf, log_forget(zf_fwd, lb[0]), log_forget(zf_bwd, lb[1]),
                                 heads(i_in.astype(jnp.float32), HG_HEAD_V))
    o = rmsnorm(o, out_g).reshape(b, n, HG_VAL_DIM)
    return (o * jax.nn.silu(g.astype(jnp.float32))).astype(g.dtype)


def memory_cross_attention(h, m, wq, wkv, wo):
    b, n = h.shape[0], h.shape[1]
    nm = m.shape[1]
    q = (h @ wq).reshape(b, n, X_HEADS, X_HEAD_DIM)
    kv = (m @ wkv).reshape(b, nm, 2, X_HEADS, X_HEAD_DIM)
    k, v = kv[:, :, 0], kv[:, :, 1]
    s = jnp.einsum("bqhd,bkhd->bhqk", q, k).astype(jnp.float32) * (X_HEAD_DIM ** -0.5)
    p = jax.nn.softmax(s, axis=-1).astype(v.dtype)
    o = jnp.einsum("bhqk,bkhd->bqhd", p, v).reshape(b, n, X_HEADS * X_HEAD_DIM)
    return o @ wo


def conv_ffn(h, w_up, conv_w, conv_b, w_down):
    n = h.shape[1]
    u = h @ w_up
    half = CONV_W // 2
    up = jnp.pad(u, ((0, 0), (half, half), (0, 0)))
    acc = conv_b
    for j in range(CONV_W):
        acc = acc + up[:, j:j + n] * conv_w[j]
    gate, val = jnp.split(acc, 2, axis=-1)
    return (jax.nn.silu(gate) * val) @ w_down


def _fwd_setup_inputs(seed: int = 0) -> dict:
    key = jax.random.key(seed)
    ks = jax.random.split(key, 22)
    f32 = jnp.float32
    L = DEPTH

    def nrm(k, shape, scale):
        return jax.random.normal(k, shape, f32) * scale

    def gain(k, shape):
        return 1.0 + nrm(k, shape, 0.05)

    return {
        "x": nrm(ks[0], (BATCH, SEQ, D_MODEL), 1.0),
        "mem": nrm(ks[1], (BATCH, N_MEM, D_MODEL), 1.0),
        "pre_mix_g": gain(ks[2], (L, D_MODEL)),
        "w_in": nrm(ks[3], (L, D_MODEL, N_IN), D_MODEL ** -0.5),
        "q_norm_g": gain(ks[4], (L, ATT_HEAD_DIM)),
        "k_norm_g": gain(ks[5], (L, ATT_HEAD_DIM)),
        "hg_lb": nrm(ks[6], (2, L + 1, HG_KEY_DIM), 0.5),
        "hg_out_norm_g": gain(ks[7], (L, HG_HEAD_V)),
        "w_out": nrm(ks[8], (L, MIX_WIDTH, D_MODEL), MIX_WIDTH ** -0.5),
        "post_mix_g": gain(ks[9], (L, D_MODEL)),
        "pre_x_g": gain(ks[10], (L, D_MODEL)),
        "mem_norm_g": gain(ks[11], (L, D_MODEL)),
        "w_xq": nrm(ks[12], (L, D_MODEL, D_MODEL), D_MODEL ** -0.5),
        "w_xkv": nrm(ks[13], (L, D_MODEL, 2 * D_MODEL), D_MODEL ** -0.5),
        "w_xo": nrm(ks[14], (L, D_MODEL, D_MODEL), D_MODEL ** -0.5),
        "post_x_g": gain(ks[15], (L, D_MODEL)),
        "pre_ffn_g": gain(ks[16], (L, D_MODEL)),
        "w_up": nrm(ks[17], (L, D_MODEL, 2 * D_FF), D_MODEL ** -0.5),
        "conv_w": nrm(ks[18], (L, CONV_W, 2 * D_FF), CONV_W ** -0.5),
        "conv_b": nrm(ks[19], (L, 2 * D_FF), 0.02),
        "w_down": nrm(ks[20], (L, D_FF, D_MODEL), D_FF ** -0.5),
        "post_ffn_g": gain(ks[21], (L, D_MODEL)),
    }


def _fwd_reference(x, mem, pre_mix_g, w_in, q_norm_g, k_norm_g, hg_lb, hg_out_norm_g, w_out, post_mix_g,
              pre_x_g, mem_norm_g, w_xq, w_xkv, w_xo, post_x_g, pre_ffn_g, w_up, conv_w, conv_b,
              w_down, post_ffn_g):
    b, n = x.shape[0], x.shape[1]
    cos, sin = axial_rope_tables(n)
    lb_all = jnp.cumsum(jax.nn.softmax(hg_lb.astype(jnp.float32), axis=1), axis=1)
    for l in range(DEPTH):
        h = rmsnorm(x, pre_mix_g[l])
        aq, ak, av, hq, hf_fwd, hf_bwd, hi, hg = split_columns(h @ w_in[l], IN_SPLITS)
        aq = rmsnorm(aq.reshape(b, n, ATT_HEADS, ATT_HEAD_DIM), q_norm_g[l])
        ak = rmsnorm(ak.reshape(b, n, ATT_KV_HEADS, ATT_HEAD_DIM), k_norm_g[l])
        av = av.reshape(b, n, ATT_KV_HEADS, ATT_HEAD_DIM)
        att = axial_gqa_attention(apply_rope(aq, cos, sin), apply_rope(ak, cos, sin), av)
        rec = hgrn2_group(hq, hf_fwd, hf_bwd, hi, hg, lb_all[:, l], hg_out_norm_g[l])
        mixed = jnp.concatenate([att, rec], axis=-1) @ w_out[l]
        x = x + rmsnorm(mixed, post_mix_g[l])
        h = rmsnorm(x, pre_x_g[l])
        m = rmsnorm(mem, mem_norm_g[l])
        x = x + rmsnorm(memory_cross_attention(h, m, w_xq[l], w_xkv[l], w_xo[l]), post_x_g[l])
        h = rmsnorm(x, pre_ffn_g[l])
        x = x + rmsnorm(conv_ffn(h, w_up[l], conv_w[l], conv_b[l], w_down[l]), post_ffn_g[l])
    return x


import jax as _jax
import jax.numpy as _jnp

TWIN_FORMAT = 'train_step'
FWD_PARAMS = ['x', 'mem', 'pre_mix_g', 'w_in', 'q_norm_g', 'k_norm_g', 'hg_lb', 'hg_out_norm_g', 'w_out', 'post_mix_g', 'pre_x_g', 'mem_norm_g', 'w_xq', 'w_xkv', 'w_xo', 'post_x_g', 'pre_ffn_g', 'w_up', 'conv_w', 'conv_b', 'w_down', 'post_ffn_g']
TWIN_WEIGHTS = ['pre_mix_g', 'w_in', 'q_norm_g', 'k_norm_g', 'hg_lb', 'hg_out_norm_g', 'w_out', 'post_mix_g', 'pre_x_g', 'mem_norm_g', 'w_xq', 'w_xkv', 'w_xo', 'post_x_g', 'pre_ffn_g', 'w_up', 'conv_w', 'conv_b', 'w_down', 'post_ffn_g']
TWIN_DIFF_INPUT = 'x'
TWIN_INPUTS = ['x', 'mem', 'pre_mix_g', 'w_in', 'q_norm_g', 'k_norm_g', 'hg_lb', 'hg_out_norm_g', 'w_out', 'post_mix_g', 'pre_x_g', 'mem_norm_g', 'w_xq', 'w_xkv', 'w_xo', 'post_x_g', 'pre_ffn_g', 'w_up', 'conv_w', 'conv_b', 'w_down', 'post_ffn_g', 'loss_target', 'm_pre_mix_g', 'm_w_in', 'm_q_norm_g', 'm_k_norm_g', 'm_hg_lb', 'm_hg_out_norm_g', 'm_w_out', 'm_post_mix_g', 'm_pre_x_g', 'm_mem_norm_g', 'm_w_xq', 'm_w_xkv', 'm_w_xo', 'm_post_x_g', 'm_pre_ffn_g', 'm_w_up', 'm_conv_w', 'm_conv_b', 'm_w_down', 'm_post_ffn_g', 'v_pre_mix_g', 'v_w_in', 'v_q_norm_g', 'v_k_norm_g', 'v_hg_lb', 'v_hg_out_norm_g', 'v_w_out', 'v_post_mix_g', 'v_pre_x_g', 'v_mem_norm_g', 'v_w_xq', 'v_w_xkv', 'v_w_xo', 'v_post_x_g', 'v_pre_ffn_g', 'v_w_up', 'v_conv_w', 'v_conv_b', 'v_w_down', 'v_post_ffn_g']
TWIN_OUTPUTS = ['loss', 'grad_x', 'grad_pre_mix_g', 'grad_w_in', 'grad_q_norm_g', 'grad_k_norm_g', 'grad_hg_lb', 'grad_hg_out_norm_g', 'grad_w_out', 'grad_post_mix_g', 'grad_pre_x_g', 'grad_mem_norm_g', 'grad_w_xq', 'grad_w_xkv', 'grad_w_xo', 'grad_post_x_g', 'grad_pre_ffn_g', 'grad_w_up', 'grad_conv_w', 'grad_conv_b', 'grad_w_down', 'grad_post_ffn_g', 'delta_pre_mix_g', 'delta_w_in', 'delta_q_norm_g', 'delta_k_norm_g', 'delta_hg_lb', 'delta_hg_out_norm_g', 'delta_w_out', 'delta_post_mix_g', 'delta_pre_x_g', 'delta_mem_norm_g', 'delta_w_xq', 'delta_w_xkv', 'delta_w_xo', 'delta_post_x_g', 'delta_pre_ffn_g', 'delta_w_up', 'delta_conv_w', 'delta_conv_b', 'delta_w_down', 'delta_post_ffn_g', 'new_m_pre_mix_g', 'new_m_w_in', 'new_m_q_norm_g', 'new_m_k_norm_g', 'new_m_hg_lb', 'new_m_hg_out_norm_g', 'new_m_w_out', 'new_m_post_mix_g', 'new_m_pre_x_g', 'new_m_mem_norm_g', 'new_m_w_xq', 'new_m_w_xkv', 'new_m_w_xo', 'new_m_post_x_g', 'new_m_pre_ffn_g', 'new_m_w_up', 'new_m_conv_w', 'new_m_conv_b', 'new_m_w_down', 'new_m_post_ffn_g', 'new_v_pre_mix_g', 'new_v_w_in', 'new_v_q_norm_g', 'new_v_k_norm_g', 'new_v_hg_lb', 'new_v_hg_out_norm_g', 'new_v_w_out', 'new_v_post_mix_g', 'new_v_pre_x_g', 'new_v_mem_norm_g', 'new_v_w_xq', 'new_v_w_xkv', 'new_v_w_xo', 'new_v_post_x_g', 'new_v_pre_ffn_g', 'new_v_w_up', 'new_v_conv_w', 'new_v_conv_b', 'new_v_w_down', 'new_v_post_ffn_g']
TWIN_LEAF_KINDS = {'loss': 'loss', 'grad_x': 'grad_x', 'grad_pre_mix_g': 'grad_w', 'grad_w_in': 'grad_w', 'grad_q_norm_g': 'grad_w', 'grad_k_norm_g': 'grad_w', 'grad_hg_lb': 'grad_w', 'grad_hg_out_norm_g': 'grad_w', 'grad_w_out': 'grad_w', 'grad_post_mix_g': 'grad_w', 'grad_pre_x_g': 'grad_w', 'grad_mem_norm_g': 'grad_w', 'grad_w_xq': 'grad_w', 'grad_w_xkv': 'grad_w', 'grad_w_xo': 'grad_w', 'grad_post_x_g': 'grad_w', 'grad_pre_ffn_g': 'grad_w', 'grad_w_up': 'grad_w', 'grad_conv_w': 'grad_w', 'grad_conv_b': 'grad_w', 'grad_w_down': 'grad_w', 'grad_post_ffn_g': 'grad_w', 'delta_pre_mix_g': 'delta_w', 'delta_w_in': 'delta_w', 'delta_q_norm_g': 'delta_w', 'delta_k_norm_g': 'delta_w', 'delta_hg_lb': 'delta_w', 'delta_hg_out_norm_g': 'delta_w', 'delta_w_out': 'delta_w', 'delta_post_mix_g': 'delta_w', 'delta_pre_x_g': 'delta_w', 'delta_mem_norm_g': 'delta_w', 'delta_w_xq': 'delta_w', 'delta_w_xkv': 'delta_w', 'delta_w_xo': 'delta_w', 'delta_post_x_g': 'delta_w', 'delta_pre_ffn_g': 'delta_w', 'delta_w_up': 'delta_w', 'delta_conv_w': 'delta_w', 'delta_conv_b': 'delta_w', 'delta_w_down': 'delta_w', 'delta_post_ffn_g': 'delta_w', 'new_m_pre_mix_g': 'new_m', 'new_m_w_in': 'new_m', 'new_m_q_norm_g': 'new_m', 'new_m_k_norm_g': 'new_m', 'new_m_hg_lb': 'new_m', 'new_m_hg_out_norm_g': 'new_m', 'new_m_w_out': 'new_m', 'new_m_post_mix_g': 'new_m', 'new_m_pre_x_g': 'new_m', 'new_m_mem_norm_g': 'new_m', 'new_m_w_xq': 'new_m', 'new_m_w_xkv': 'new_m', 'new_m_w_xo': 'new_m', 'new_m_post_x_g': 'new_m', 'new_m_pre_ffn_g': 'new_m', 'new_m_w_up': 'new_m', 'new_m_conv_w': 'new_m', 'new_m_conv_b': 'new_m', 'new_m_w_down': 'new_m', 'new_m_post_ffn_g': 'new_m', 'new_v_pre_mix_g': 'new_v', 'new_v_w_in': 'new_v', 'new_v_q_norm_g': 'new_v', 'new_v_k_norm_g': 'new_v', 'new_v_hg_lb': 'new_v', 'new_v_hg_out_norm_g': 'new_v', 'new_v_w_out': 'new_v', 'new_v_post_mix_g': 'new_v', 'new_v_pre_x_g': 'new_v', 'new_v_mem_norm_g': 'new_v', 'new_v_w_xq': 'new_v', 'new_v_w_xkv': 'new_v', 'new_v_w_xo': 'new_v', 'new_v_post_x_g': 'new_v', 'new_v_pre_ffn_g': 'new_v', 'new_v_w_up': 'new_v', 'new_v_conv_w': 'new_v', 'new_v_conv_b': 'new_v', 'new_v_w_down': 'new_v', 'new_v_post_ffn_g': 'new_v'}


def _forward(args):
    return _fwd_reference(*[args[k] for k in FWD_PARAMS])


def _output_shape():
    out = _jax.eval_shape(lambda: _forward(_fwd_setup_inputs(0)))
    return out.shape, out.dtype

N_MICROBATCH = 1
ADAM_LR = 0.001
ADAM_B1 = 0.9
ADAM_B2 = 0.999
ADAM_EPS = 1e-08
ADAM_WD = 0.01
ADAM_STEP = 10
PER_EXAMPLE_BATCH_AXIS = {'x': 0, 'mem': 0, 'loss_target': 0}
SHARED_INPUTS = []
_WEIGHT_DTYPES = {'pre_mix_g': _jnp.float32, 'w_in': _jnp.float32, 'q_norm_g': _jnp.float32, 'k_norm_g': _jnp.float32, 'hg_lb': _jnp.float32, 'hg_out_norm_g': _jnp.float32, 'w_out': _jnp.float32, 'post_mix_g': _jnp.float32, 'pre_x_g': _jnp.float32, 'mem_norm_g': _jnp.float32, 'w_xq': _jnp.float32, 'w_xkv': _jnp.float32, 'w_xo': _jnp.float32, 'post_x_g': _jnp.float32, 'pre_ffn_g': _jnp.float32, 'w_up': _jnp.float32, 'conv_w': _jnp.float32, 'conv_b': _jnp.float32, 'w_down': _jnp.float32, 'post_ffn_g': _jnp.float32}
MOMENT_SCALE = {'pre_mix_g': 6.480095e-01, 'w_in': 3.466369e-01, 'q_norm_g': 2.548348e-01, 'k_norm_g': 2.628589e-01, 'hg_lb': 2.849345e-02, 'hg_out_norm_g': 1.687252e+00, 'w_out': 5.133002e-01, 'post_mix_g': 1.600671e+01, 'pre_x_g': 4.061215e-01, 'mem_norm_g': 1.202978e+00, 'w_xq': 4.195691e-01, 'w_xkv': 8.163475e-01, 'w_xo': 1.126096e+00, 'post_x_g': 1.647897e+01, 'pre_ffn_g': 8.014605e-01, 'w_up': 3.561799e-01, 'conv_w': 4.161598e-01, 'conv_b': 1.071061e+00, 'w_down': 6.931750e-01, 'post_ffn_g': 1.602190e+01}


def _to_microbatches(a, axis):
    t = _jnp.moveaxis(a, axis, 0)
    t = t.reshape((N_MICROBATCH, t.shape[0] // N_MICROBATCH) + t.shape[1:])
    return _jnp.moveaxis(t, 1, axis + 1)


def setup_inputs(seed: int = 0) -> dict:
    inp = _fwd_setup_inputs(seed)
    key = _jax.random.fold_in(_jax.random.key(seed), 7919)
    shape, _ = _output_shape()
    out = dict(inp)
    out["loss_target"] = _jax.random.normal(_jax.random.fold_in(key, 0), shape, _jnp.float32)
    for i, name in enumerate(TWIN_WEIGHTS):
        w = inp[name].astype(_jnp.float32)
        if MOMENT_SCALE is None:
            s = _jnp.sqrt(_jnp.mean(_jnp.square(w)) + 1e-30)
        else:
            s = MOMENT_SCALE[name]
        km, kv = _jax.random.split(_jax.random.fold_in(key, i + 1))
        out[name] = w
        out["m_" + name] = s * _jax.random.normal(km, w.shape, _jnp.float32)
        out["v_" + name] = (s * s) * _jax.random.uniform(kv, w.shape, _jnp.float32, 0.5, 1.5)
    if N_MICROBATCH > 1:
        for name, axis in PER_EXAMPLE_BATCH_AXIS.items():
            out[name] = _to_microbatches(out[name], axis)
    return {'x': out['x'], 'mem': out['mem'], 'pre_mix_g': out['pre_mix_g'], 'w_in': out['w_in'], 'q_norm_g': out['q_norm_g'], 'k_norm_g': out['k_norm_g'], 'hg_lb': out['hg_lb'], 'hg_out_norm_g': out['hg_out_norm_g'], 'w_out': out['w_out'], 'post_mix_g': out['post_mix_g'], 'pre_x_g': out['pre_x_g'], 'mem_norm_g': out['mem_norm_g'], 'w_xq': out['w_xq'], 'w_xkv': out['w_xkv'], 'w_xo': out['w_xo'], 'post_x_g': out['post_x_g'], 'pre_ffn_g': out['pre_ffn_g'], 'w_up': out['w_up'], 'conv_w': out['conv_w'], 'conv_b': out['conv_b'], 'w_down': out['w_down'], 'post_ffn_g': out['post_ffn_g'], 'loss_target': out['loss_target'], 'm_pre_mix_g': out['m_pre_mix_g'], 'm_w_in': out['m_w_in'], 'm_q_norm_g': out['m_q_norm_g'], 'm_k_norm_g': out['m_k_norm_g'], 'm_hg_lb': out['m_hg_lb'], 'm_hg_out_norm_g': out['m_hg_out_norm_g'], 'm_w_out': out['m_w_out'], 'm_post_mix_g': out['m_post_mix_g'], 'm_pre_x_g': out['m_pre_x_g'], 'm_mem_norm_g': out['m_mem_norm_g'], 'm_w_xq': out['m_w_xq'], 'm_w_xkv': out['m_w_xkv'], 'm_w_xo': out['m_w_xo'], 'm_post_x_g': out['m_post_x_g'], 'm_pre_ffn_g': out['m_pre_ffn_g'], 'm_w_up': out['m_w_up'], 'm_conv_w': out['m_conv_w'], 'm_conv_b': out['m_conv_b'], 'm_w_down': out['m_w_down'], 'm_post_ffn_g': out['m_post_ffn_g'], 'v_pre_mix_g': out['v_pre_mix_g'], 'v_w_in': out['v_w_in'], 'v_q_norm_g': out['v_q_norm_g'], 'v_k_norm_g': out['v_k_norm_g'], 'v_hg_lb': out['v_hg_lb'], 'v_hg_out_norm_g': out['v_hg_out_norm_g'], 'v_w_out': out['v_w_out'], 'v_post_mix_g': out['v_post_mix_g'], 'v_pre_x_g': out['v_pre_x_g'], 'v_mem_norm_g': out['v_mem_norm_g'], 'v_w_xq': out['v_w_xq'], 'v_w_xkv': out['v_w_xkv'], 'v_w_xo': out['v_w_xo'], 'v_post_x_g': out['v_post_x_g'], 'v_pre_ffn_g': out['v_pre_ffn_g'], 'v_w_up': out['v_w_up'], 'v_conv_w': out['v_conv_w'], 'v_conv_b': out['v_conv_b'], 'v_w_down': out['v_w_down'], 'v_post_ffn_g': out['v_post_ffn_g']}


def _loss(weights, diff, rest, loss_target):
    with _jax.named_scope("forward"):
        args = {**rest, TWIN_DIFF_INPUT: diff, **{k: w.astype(_WEIGHT_DTYPES[k]) for k, w in weights.items()}}
        y = _forward(args)
    with _jax.named_scope("loss_head"):
        err = _jnp.square(y.astype(_jnp.float32) - loss_target)
        return 0.5 * _jnp.sum(_jnp.mean(err, axis=-1)) if err.ndim else 0.5 * err


def _adamw(w, g, m, v):
    m = ADAM_B1 * m + (1.0 - ADAM_B1) * g
    v = ADAM_B2 * v + (1.0 - ADAM_B2) * _jnp.square(g)
    m_hat = m / (1.0 - ADAM_B1 ** ADAM_STEP)
    v_hat = v / (1.0 - ADAM_B2 ** ADAM_STEP)
    delta = -ADAM_LR * (m_hat / (_jnp.sqrt(v_hat) + ADAM_EPS) + ADAM_WD * w)
    return delta, m, v


def reference(x, mem, pre_mix_g, w_in, q_norm_g, k_norm_g, hg_lb, hg_out_norm_g, w_out, post_mix_g, pre_x_g, mem_norm_g, w_xq, w_xkv, w_xo, post_x_g, pre_ffn_g, w_up, conv_w, conv_b, w_down, post_ffn_g, loss_target, m_pre_mix_g, m_w_in, m_q_norm_g, m_k_norm_g, m_hg_lb, m_hg_out_norm_g, m_w_out, m_post_mix_g, m_pre_x_g, m_mem_norm_g, m_w_xq, m_w_xkv, m_w_xo, m_post_x_g, m_pre_ffn_g, m_w_up, m_conv_w, m_conv_b, m_w_down, m_post_ffn_g, v_pre_mix_g, v_w_in, v_q_norm_g, v_k_norm_g, v_hg_lb, v_hg_out_norm_g, v_w_out, v_post_mix_g, v_pre_x_g, v_mem_norm_g, v_w_xq, v_w_xkv, v_w_xo, v_post_x_g, v_pre_ffn_g, v_w_up, v_conv_w, v_conv_b, v_w_down, v_post_ffn_g):
    given = dict(x=x, mem=mem, pre_mix_g=pre_mix_g, w_in=w_in, q_norm_g=q_norm_g, k_norm_g=k_norm_g, hg_lb=hg_lb, hg_out_norm_g=hg_out_norm_g, w_out=w_out, post_mix_g=post_mix_g, pre_x_g=pre_x_g, mem_norm_g=mem_norm_g, w_xq=w_xq, w_xkv=w_xkv, w_xo=w_xo, post_x_g=post_x_g, pre_ffn_g=pre_ffn_g, w_up=w_up, conv_w=conv_w, conv_b=conv_b, w_down=w_down, post_ffn_g=post_ffn_g, loss_target=loss_target, m_pre_mix_g=m_pre_mix_g, m_w_in=m_w_in, m_q_norm_g=m_q_norm_g, m_k_norm_g=m_k_norm_g, m_hg_lb=m_hg_lb, m_hg_out_norm_g=m_hg_out_norm_g, m_w_out=m_w_out, m_post_mix_g=m_post_mix_g, m_pre_x_g=m_pre_x_g, m_mem_norm_g=m_mem_norm_g, m_w_xq=m_w_xq, m_w_xkv=m_w_xkv, m_w_xo=m_w_xo, m_post_x_g=m_post_x_g, m_pre_ffn_g=m_pre_ffn_g, m_w_up=m_w_up, m_conv_w=m_conv_w, m_conv_b=m_conv_b, m_w_down=m_w_down, m_post_ffn_g=m_post_ffn_g, v_pre_mix_g=v_pre_mix_g, v_w_in=v_w_in, v_q_norm_g=v_q_norm_g, v_k_norm_g=v_k_norm_g, v_hg_lb=v_hg_lb, v_hg_out_norm_g=v_hg_out_norm_g, v_w_out=v_w_out, v_post_mix_g=v_post_mix_g, v_pre_x_g=v_pre_x_g, v_mem_norm_g=v_mem_norm_g, v_w_xq=v_w_xq, v_w_xkv=v_w_xkv, v_w_xo=v_w_xo, v_post_x_g=v_post_x_g, v_pre_ffn_g=v_pre_ffn_g, v_w_up=v_w_up, v_conv_w=v_conv_w, v_conv_b=v_conv_b, v_w_down=v_w_down, v_post_ffn_g=v_post_ffn_g)
    weights = {n: given[n] for n in TWIN_WEIGHTS}
    shared = {n: given[n] for n in SHARED_INPUTS}
    per_example = {n: given[n] for n in ['x', 'mem']}
    grad_fn = _jax.value_and_grad(_loss, argnums=(0, 1))

    def one_microbatch(ex, loss_target):
        ex = dict(ex)
        diff = ex.pop(TWIN_DIFF_INPUT)
        return grad_fn(weights, diff, {**shared, **ex}, loss_target)

    if N_MICROBATCH == 1:
        loss, (grad_w, grad_x) = one_microbatch(per_example, given["loss_target"])
    else:
        def body(carry, xs):
            loss_sum, grad_sum = carry
            l_k, (gw_k, gx_k) = one_microbatch(xs[0], xs[1])
            with _jax.named_scope("update"):
                return (loss_sum + l_k, _jax.tree.map(_jnp.add, grad_sum, gw_k)), gx_k

        init = (_jnp.zeros((), _jnp.float32), _jax.tree.map(_jnp.zeros_like, weights))
        (loss, grad_w), grad_x = _jax.lax.scan(body, init, (per_example, given["loss_target"]))
    with _jax.named_scope("update"):
        delta_w, new_m, new_v = {}, {}, {}
        for n in TWIN_WEIGHTS:
            delta_w[n], new_m[n], new_v[n] = _adamw(weights[n], grad_w[n], given["m_" + n], given["v_" + n])
    return (loss, grad_x, *[grad_w[n] for n in TWIN_WEIGHTS], *[delta_w[n] for n in TWIN_WEIGHTS],
            *[new_m[n] for n in TWIN_WEIGHTS], *[new_v[n] for n in TWIN_WEIGHTS])
```

```python
import jax
import jax.numpy as jnp
from jax import lax
from jax.experimental import pallas as pl
from jax.experimental.pallas import tpu as pltpu

F32 = jnp.float32
BF16 = jnp.bfloat16

D_MODEL = 1024
GRID_W = 64
EPS = 1e-6
ATT_HEADS = 8
ATT_HEAD_DIM = 64
ATT_Q_DIM = 512
ATT_KV_DIM = 128
ROPE_THETA = 10000.0
HG_HEADS = 4
HG_DIM = 512
HG_CHUNK = 32
HG_CHUNK_LOG2 = 5
HG_BLOCK = 256
N_IN = 3328
X_HEADS = 4
X_HEAD_DIM = 256
D_FF = 2816
N_DEV = 8
LANE = 128
ADAM_LR = 0.001
ADAM_B1 = 0.9
ADAM_B2 = 0.999
ADAM_EPS = 1e-08
ADAM_WD = 0.01
ADAM_STEP = 10
VMEM_LIMIT = 56 * 1024 * 1024

MESH_T = pl.DeviceIdType.MESH


def _params(**kw):
    return pltpu.CompilerParams(vmem_limit_bytes=VMEM_LIMIT, **kw)


def _dot(a, b, ca, cb):
    return lax.dot_general(a, b, (((ca,), (cb,)), ((), ())), preferred_element_type=F32)


def _bf(x):
    return x.astype(BF16)


def _sigmoid(x):
    return 1.0 / (1.0 + jnp.exp(-x))


def _rms_fwd(x, g):
    r = lax.rsqrt(jnp.mean(x * x, axis=-1, keepdims=True) + EPS)
    return x * r * g


def _rms_bwd(dy, x, g):
    r = lax.rsqrt(jnp.mean(x * x, axis=-1, keepdims=True) + EPS)
    xh = x * r
    dg = jnp.sum(dy * xh, axis=0, keepdims=True)
    t = dy * g
    dx = r * (t - xh * jnp.mean(t * xh, axis=-1, keepdims=True))
    return dx, dg


def _full(shape):
    nd = len(shape)
    return pl.BlockSpec(shape, lambda *a: (0,) * nd)


def _norm_mm(x, g, w, trans, tn, name, tm=512):
    t, d = x.shape
    n = w.shape[0] if trans else w.shape[1]
    tm = min(tm, t)

    def body(x_ref, g_ref, w_ref, h_ref, p_ref):
        h = _bf(_rms_fwd(x_ref[...], g_ref[...]))
        h_ref[...] = h
        p_ref[...] = _dot(h, w_ref[...], 1, 1 if trans else 0)

    w_spec = pl.BlockSpec((tn, d), lambda i, j: (j, 0)) if trans else pl.BlockSpec((d, tn), lambda i, j: (0, j))
    return pl.pallas_call(
        body, name=name, grid=(t // tm, n // tn),
        in_specs=[pl.BlockSpec((tm, d), lambda i, j: (i, 0)), _full((1, d)), w_spec],
        out_specs=[pl.BlockSpec((tm, d), lambda i, j: (i, 0)), pl.BlockSpec((tm, tn), lambda i, j: (i, j))],
        out_shape=[jax.ShapeDtypeStruct((t, d), BF16), jax.ShapeDtypeStruct((t, n), F32)],
        compiler_params=_params(),
    )(x, g, w)


def _mm_postnorm_res(a, w, g, res, name, tm=256):
    t, k = a.shape
    d = w.shape[1]

    def body(a_ref, w_ref, g_ref, res_ref, y_ref, o_ref):
        y = _dot(a_ref[...], w_ref[...], 1, 0)
        y_ref[...] = y
        o_ref[...] = res_ref[...] + _rms_fwd(y, g_ref[...])

    row = lambda width: pl.BlockSpec((tm, width), lambda i: (i, 0))
    return pl.pallas_call(
        body, name=name, grid=(t // tm,),
        in_specs=[row(k), _full((k, d)), _full((1, d)), row(d)],
        out_specs=[row(d), row(d)],
        out_shape=[jax.ShapeDtypeStruct((t, d), F32)] * 2,
        compiler_params=_params(),
    )(a, w, g, res)


def _mm_postnorm_res_loss(a, w, g, res, tgt, name, tm=256):
    t, k = a.shape
    d = w.shape[1]

    def body(a_ref, w_ref, g_ref, res_ref, tgt_ref, y_ref, dout_ref, loss_ref):
        @pl.when(pl.program_id(0) == 0)
        def _():
            loss_ref[...] = jnp.zeros_like(loss_ref)

        y = _dot(a_ref[...], w_ref[...], 1, 0)
        y_ref[...] = y
        diff = res_ref[...] + _rms_fwd(y, g_ref[...]) - tgt_ref[...]
        dout_ref[...] = diff * (1.0 / d)
        part = jnp.sum(jnp.sum(diff * diff, axis=-1, keepdims=True), axis=0, keepdims=True)
        loss_ref[...] += (0.5 / d) * part

    row = lambda width: pl.BlockSpec((tm, width), lambda i: (i, 0))
    return pl.pallas_call(
        body, name=name, grid=(t // tm,),
        in_specs=[row(k), _full((k, d)), _full((1, d)), row(d), row(d)],
        out_specs=[row(d), row(d), _full((1, 1))],
        out_shape=[jax.ShapeDtypeStruct((t, d), F32)] * 2 + [jax.ShapeDtypeStruct((1, 1), F32)],
        compiler_params=_params(),
    )(a, w, g, res, tgt)


def _postnorm_bwd_mm(dout, y, g, w, name, tm=256):
    t, d = y.shape
    k = w.shape[0]

    def body(dout_ref, y_ref, g_ref, w_ref, dy_ref, da_ref, dg_ref):
        @pl.when(pl.program_id(0) == 0)
        def _():
            dg_ref[...] = jnp.zeros_like(dg_ref)

        dy, dg = _rms_bwd(dout_ref[...], y_ref[...], g_ref[...])
        dg_ref[...] += dg
        dyb = _bf(dy)
        dy_ref[...] = dyb
        da_ref[...] = _dot(dyb, w_ref[...], 1, 1)

    row = lambda width: pl.BlockSpec((tm, width), lambda i: (i, 0))
    return pl.pallas_call(
        body, name=name, grid=(t // tm,),
        in_specs=[row(d), row(d), _full((1, d)), _full((k, d))],
        out_specs=[row(d), row(k), _full((1, d))],
        out_shape=[jax.ShapeDtypeStruct((t, d), BF16), jax.ShapeDtypeStruct((t, k), F32), jax.ShapeDtypeStruct((1, d), F32)],
        compiler_params=_params(),
    )(dout, y, g, w)


def _mm_prenorm_bwd(dp, w, trans, x, g, dres, tk, name, tm=512):
    t, n = dp.shape
    d = x.shape[1]
    tm = min(tm, t)
    nk = n // tk

    def body(dp_ref, w_ref, x_ref, g_ref, dres_ref, dx_ref, dg_ref, acc_ref):
        i, kk = pl.program_id(0), pl.program_id(1)

        @pl.when((i == 0) & (kk == 0))
        def _():
            dg_ref[...] = jnp.zeros_like(dg_ref)

        part = _dot(dp_ref[...], w_ref[...], 1, 0 if trans else 1)

        @pl.when(kk == 0)
        def _():
            acc_ref[...] = part

        @pl.when(kk > 0)
        def _():
            acc_ref[...] += part

        @pl.when(kk == nk - 1)
        def _():
            dx, dg = _rms_bwd(acc_ref[...], x_ref[...], g_ref[...])
            dg_ref[...] += dg
            dx_ref[...] = dres_ref[...] + dx

    w_spec = pl.BlockSpec((tk, d), lambda i, kk: (kk, 0)) if trans else pl.BlockSpec((d, tk), lambda i, kk: (0, kk))
    row = pl.BlockSpec((tm, d), lambda i, kk: (i, 0))
    return pl.pallas_call(
        body, name=name, grid=(t // tm, nk),
        in_specs=[pl.BlockSpec((tm, tk), lambda i, kk: (i, kk)), w_spec, row, _full((1, d)), row],
        out_specs=[row, _full((1, d))],
        out_shape=[jax.ShapeDtypeStruct((t, d), F32), jax.ShapeDtypeStruct((1, d), F32)],
        scratch_shapes=[pltpu.VMEM((tm, d), F32)],
        compiler_params=_params(),
    )(dp, w, x, g, dres)


def _dw(a, b, name, tka=256):
    t, ka = a.shape
    nb = b.shape[1]

    def body(a_ref, b_ref, o_ref):
        o_ref[...] = _bf(_dot(a_ref[...], b_ref[...], 0, 0))

    return pl.pallas_call(
        body, name=name, grid=(ka // tka,),
        in_specs=[pl.BlockSpec((t, tka), lambda i: (0, i)), _full((t, nb))],
        out_specs=pl.BlockSpec((tka, nb), lambda i: (i, 0)),
        out_shape=jax.ShapeDtypeStruct((ka, nb), BF16),
        compiler_params=_params(),
    )(a, b)


def _rope_tables(t):
    pos = jnp.arange(t)
    r = (pos // GRID_W).astype(F32)
    c = (pos % GRID_W).astype(F32)
    npair = ATT_HEAD_DIM // 4
    inv = jnp.power(ROPE_THETA, -jnp.arange(npair, dtype=F32) / npair)
    ang = jnp.concatenate([r[:, None] * inv, c[:, None] * inv], axis=-1)
    cos = jnp.repeat(jnp.cos(ang), 2, axis=-1)
    sin = jnp.repeat(jnp.sin(ang), 2, axis=-1)
    even = (jnp.arange(ATT_HEAD_DIM) % 2) == 0
    sa = jnp.where(even, -sin, 0.0)
    sb = jnp.where(even, 0.0, sin)
    two = lambda a: jnp.tile(a, (1, 2))
    return two(cos), two(sa), two(sb)


def _head_sum_matrix():
    a = jnp.arange(LANE) // ATT_HEAD_DIM
    return (a[:, None] == a[None, :]).astype(BF16)


def _head_mean(v, bd):
    hi = _bf(v)
    lo = _bf(v - hi.astype(F32))
    return (_dot(hi, bd, 1, 0) + _dot(lo, bd, 1, 0)) * (1.0 / ATT_HEAD_DIM)


def _qk_prep(p, gq, gk, tables, bd, tm=512):
    t = p.shape[0]
    tm = min(tm, t)
    cc, sa, sb = tables

    def body(p_ref, gq_ref, gk_ref, cc_ref, sa_ref, sb_ref, bd_ref, q_ref, k_ref):
        cc_, sa_, sb_, bd_ = cc_ref[...], sa_ref[...], sb_ref[...], bd_ref[...]
        low = lax.broadcasted_iota(jnp.int32, (tm, LANE), 1) < ATT_HEAD_DIM

        def normrope(xs, g):
            xn = xs * lax.rsqrt(_head_mean(xs * xs, bd_) + EPS) * g
            return xn * cc_ + pltpu.roll(xn, LANE - 1, 1) * sa_ + pltpu.roll(xn, 1, 1) * sb_

        for j in range(4):
            y = normrope(p_ref[:, j * LANE:(j + 1) * LANE], gq_ref[...]) * (ATT_HEAD_DIM ** -0.5)
            yr = pltpu.roll(y, ATT_HEAD_DIM, 1)
            if j // 2 == 0:
                h0, h1 = jnp.where(low, y, 0.0), jnp.where(low, yr, 0.0)
            else:
                h0, h1 = jnp.where(low, 0.0, yr), jnp.where(low, 0.0, y)
            q_ref[:, (2 * j) * LANE:(2 * j + 1) * LANE] = _bf(h0)
            q_ref[:, (2 * j + 1) * LANE:(2 * j + 2) * LANE] = _bf(h1)
        k_ref[...] = _bf(normrope(p_ref[:, ATT_Q_DIM:ATT_Q_DIM + LANE], gk_ref[...]))

    row = lambda width: pl.BlockSpec((tm, width), lambda i: (i, 0))
    return pl.pallas_call(
        body, name="qk_prep", grid=(t // tm,),
        in_specs=[row(ATT_Q_DIM + LANE), _full((1, LANE)), _full((1, LANE)), row(LANE), row(LANE), row(LANE),
                  _full((LANE, LANE))],
        out_specs=[row(ATT_HEADS * LANE), row(LANE)],
        out_shape=[jax.ShapeDtypeStruct((t, ATT_HEADS * LANE), BF16), jax.ShapeDtypeStruct((t, LANE), BF16)],
        compiler_params=_params(),
    )(p, gq, gk, cc, sa, sb, bd)


def _qk_prep_bwd(p, dq, dk, gq, gk, tables, bd, tm=512):
    t = p.shape[0]
    tm = min(tm, t)
    cc, sa, sb = tables

    def body(p_ref, dq_ref, dk_ref, gq_ref, gk_ref, cc_ref, sa_ref, sb_ref, bd_ref, dp_ref, dgq_ref, dgk_ref):
        @pl.when(pl.program_id(0) == 0)
        def _():
            dgq_ref[...] = jnp.zeros_like(dgq_ref)
            dgk_ref[...] = jnp.zeros_like(dgk_ref)

        cc_, sa_, sb_, bd_ = cc_ref[...], sa_ref[...], sb_ref[...], bd_ref[...]
        low = lax.broadcasted_iota(jnp.int32, (tm, LANE), 1) < ATT_HEAD_DIM

        def bwd(xs, g, dy):
            r = lax.rsqrt(_head_mean(xs * xs, bd_) + EPS)
            xh = xs * r
            dxn = dy * cc_ + pltpu.roll(dy * sa_, 1, 1) + pltpu.roll(dy * sb_, LANE - 1, 1)
            dg = jnp.sum(dxn * xh, axis=0, keepdims=True)
            tt = dxn * g
            return r * (tt - xh * _head_mean(tt * xh, bd_)), dg

        dgq = jnp.zeros((1, LANE), F32)
        for j in range(4):
            d0 = dq_ref[:, (2 * j) * LANE:(2 * j + 1) * LANE]
            d1 = dq_ref[:, (2 * j + 1) * LANE:(2 * j + 2) * LANE]
            if j // 2 == 0:
                dy = jnp.where(low, d0, pltpu.roll(d1, ATT_HEAD_DIM, 1))
            else:
                dy = jnp.where(low, pltpu.roll(d0, ATT_HEAD_DIM, 1), d1)
            dx, dg = bwd(p_ref[:, j * LANE:(j + 1) * LANE], gq_ref[...], dy * (ATT_HEAD_DIM ** -0.5))
            dp_ref[:, j * LANE:(j + 1) * LANE] = _bf(dx)
            dgq = dgq + dg
        dgq_ref[...] += dgq
        dx, dg = bwd(p_ref[:, ATT_Q_DIM:ATT_Q_DIM + LANE], gk_ref[...], dk_ref[...])
        dp_ref[:, ATT_Q_DIM:ATT_Q_DIM + LANE] = _bf(dx)
        dgk_ref[...] += dg

    row = lambda width: pl.BlockSpec((tm, width), lambda i: (i, 0))
    return pl.pallas_call(
        body, name="qk_prep_bwd", grid=(t // tm,),
        in_specs=[row(ATT_Q_DIM + LANE), row(ATT_HEADS * LANE), row(LANE), _full((1, LANE)), _full((1, LANE)),
                  row(LANE), row(LANE), row(LANE), _full((LANE, LANE))],
        out_specs=[row(ATT_Q_DIM + LANE), _full((1, LANE)), _full((1, LANE))],
        out_shape=[jax.ShapeDtypeStruct((t, ATT_Q_DIM + LANE), BF16), jax.ShapeDtypeStruct((1, LANE), F32),
                   jax.ShapeDtypeStruct((1, LANE), F32)],
        compiler_params=_params(),
    )(p, dq, dk, gq, gk, cc, sa, sb, bd)


def _attn_fwd(q, k, p, tq=256):
    t = k.shape[0]
    tq = min(tq, t)
    v_blk = (ATT_Q_DIM + ATT_KV_DIM) // LANE

    def body(q_ref, k_ref, v_ref, o_ref):
        k_ = k_ref[...]
        v = v_ref[...]
        lowk = lax.broadcasted_iota(jnp.int32, (t, LANE), 1) < ATT_HEAD_DIM
        vm = (_bf(jnp.where(lowk, v, 0.0)), _bf(jnp.where(lowk, 0.0, v)))
        for j in range(4):
            kvh = j // 2
            acc = None
            for sub in range(2):
                h = 2 * j + sub
                s = _dot(q_ref[:, h * LANE:(h + 1) * LANE], k_, 1, 1)
                e = jnp.exp(s - jnp.max(s, axis=-1, keepdims=True))
                o = _dot(_bf(e), vm[kvh], 1, 0) * (1.0 / jnp.sum(e, axis=-1, keepdims=True))
                if sub != kvh:
                    o = pltpu.roll(o, ATT_HEAD_DIM, 1)
                acc = o if acc is None else acc + o
            o_ref[:, j * LANE:(j + 1) * LANE] = _bf(acc)

    return pl.pallas_call(
        body, name="attn_fwd", grid=(t // tq,),
        in_specs=[pl.BlockSpec((tq, ATT_HEADS * LANE), lambda i: (i, 0)), _full((t, LANE)),
                  pl.BlockSpec((t, LANE), lambda i: (0, v_blk))],
        out_specs=pl.BlockSpec((tq, ATT_Q_DIM), lambda i: (i, 0)),
        out_shape=jax.ShapeDtypeStruct((t, ATT_Q_DIM), BF16),
        compiler_params=_params(),
    )(q, k, p)


def _attn_bwd(q, k, p, dcat, tq=256):
    t = k.shape[0]
    tq = min(tq, t)
    v_blk = (ATT_Q_DIM + ATT_KV_DIM) // LANE

    def body(q_ref, k_ref, v_ref, do_ref, dq_ref, dk_ref, dv_ref):
        @pl.when(pl.program_id(0) == 0)
        def _():
            dk_ref[...] = jnp.zeros_like(dk_ref)
            dv_ref[...] = jnp.zeros_like(dv_ref)

        k_ = k_ref[...]
        vb = _bf(v_ref[...])
        lowq = lax.broadcasted_iota(jnp.int32, (tq, LANE), 1) < ATT_HEAD_DIM
        dk_acc = jnp.zeros((t, LANE), F32)
        dv_acc = jnp.zeros((t, LANE), F32)
        for j in range(4):
            kvh = j // 2
            dop = do_ref[:, j * LANE:(j + 1) * LANE]
            for sub in range(2):
                h = 2 * j + sub
                src = dop if sub == kvh else pltpu.roll(dop, ATT_HEAD_DIM, 1)
                do_h = _bf(jnp.where(lowq, src, 0.0) if kvh == 0 else jnp.where(lowq, 0.0, src))
                qh = q_ref[:, h * LANE:(h + 1) * LANE]
                s = _dot(qh, k_, 1, 1)
                e = jnp.exp(s - jnp.max(s, axis=-1, keepdims=True))
                pr = e * (1.0 / jnp.sum(e, axis=-1, keepdims=True))
                dpr = _dot(do_h, vb, 1, 1)
                ds = _bf(pr * (dpr - jnp.sum(pr * dpr, axis=-1, keepdims=True)))
                dq_ref[:, h * LANE:(h + 1) * LANE] = _dot(ds, k_, 1, 0)
                dk_acc = dk_acc + _dot(ds, qh, 0, 0)
                dv_acc = dv_acc + _dot(_bf(pr), do_h, 0, 0)
        dk_ref[...] += dk_acc
        dv_ref[...] += dv_acc

    return pl.pallas_call(
        body, name="attn_bwd", grid=(t // tq,),
        in_specs=[pl.BlockSpec((tq, ATT_HEADS * LANE), lambda i: (i, 0)), _full((t, LANE)),
                  pl.BlockSpec((t, LANE), lambda i: (0, v_blk)), pl.BlockSpec((tq, ATT_Q_DIM), lambda i: (i, 0))],
        out_specs=[pl.BlockSpec((tq, ATT_HEADS * LANE), lambda i: (i, 0)), _full((t, LANE)), _full((t, LANE))],
        out_shape=[jax.ShapeDtypeStruct((t, ATT_HEADS * LANE), F32), jax.ShapeDtypeStruct((t, LANE), F32),
                   jax.ShapeDtypeStruct((t, LANE), F32)],
        compiler_params=_params(),
    )(q, k, p, dcat)


def _lower_bounds(a0, a1):
    def body(a0_ref, a1_ref, lb_ref):
        m = jnp.maximum(a0_ref[...], a1_ref[...])
        e0, e1 = jnp.exp(a0_ref[...] - m), jnp.exp(a1_ref[...] - m)
        lb_ref[...] = e0 / (e0 + e1)

    return pl.pallas_call(body, name="lower_bounds", out_shape=jax.ShapeDtypeStruct(a0.shape, F32))(a0, a1)


def _lower_bounds_bwd(a0, a1, dlb):
    def body(a0_ref, a1_ref, dlb_ref, d0_ref, d1_ref):
        m = jnp.maximum(a0_ref[...], a1_ref[...])
        e0, e1 = jnp.exp(a0_ref[...] - m), jnp.exp(a1_ref[...] - m)
        lb = e0 / (e0 + e1)
        d0 = dlb_ref[...] * lb * (1.0 - lb)
        d0_ref[...] = d0
        d1_ref[...] = -d0

    return pl.pallas_call(body, name="lower_bounds_bwd", out_shape=[jax.ShapeDtypeStruct(a0.shape, F32)] * 2)(a0, a1, dlb)


def _chunk_scan(x, pos, down):
    n = x.shape[0]
    s = 1
    while s < HG_CHUNK:
        if down:
            x = x + jnp.where(pos >= s, pltpu.roll(x, s, 0), 0.0)
        else:
            x = x + jnp.where(pos < HG_CHUNK - s, pltpu.roll(x, n - s, 0), 0.0)
        s *= 2
    return x


def _chunk_concat(x, cid, nc):
    return jnp.concatenate([_bf(jnp.where(cid == c, x, 0.0)) for c in range(nc)], axis=1)


def _chunk_pick(x, cid, nc):
    out = jnp.where(cid == 0, x[:, :LANE], 0.0)
    for c in range(1, nc):
        out = out + jnp.where(cid == c, x[:, c * LANE:(c + 1) * LANE], 0.0)
    return out


def _hgrn_gates(hq, z, lb):
    sq = _sigmoid(hq)
    sig = _sigmoid(z)
    f = lb + (1.0 - lb) * sig
    return hq * sq, sq, sig, f, jnp.log(f)


def _hgrn_local(q, f, g, v, pos, cid, amask, rev, nc):
    k = 1.0 - f
    ba = _chunk_scan(g, pos, not rev)
    bb = _chunk_scan(g, pos, rev)
    eq = 0.5 * (ba - bb + g)
    ex_q, ex_k, ex_i, ex_d = jnp.exp(eq), jnp.exp(-eq), jnp.exp(ba), jnp.exp(bb - g)
    qb, kb, qi, kd = q * ex_q, k * ex_k, q * ex_i, k * ex_d
    dvec = jnp.exp(ba + bb - g)
    a = jnp.where(amask, _dot(_bf(qb), _bf(kb), 1, 1), 0.0)
    kd_m = _chunk_concat(kd, cid, nc)
    qi_m = _chunk_concat(qi, cid, nc)
    ut_all = _dot(_bf(v), kd_m, 0, 0)
    return dict(k=k, ex_q=ex_q, ex_k=ex_k, ex_i=ex_i, ex_d=ex_d, qb=qb, kb=kb, qi=qi, kd=kd, dvec=dvec, a=a,
                kd_m=kd_m, qi_m=qi_m, ut_all=ut_all)


def _hgrn_masks(n, rev):
    row = lax.broadcasted_iota(jnp.int32, (n, LANE), 0)
    ti = lax.broadcasted_iota(jnp.int32, (n, n), 0)
    si = lax.broadcasted_iota(jnp.int32, (n, n), 1)
    tri = (si >= ti) if rev else (si <= ti)
    same = jnp.right_shift(ti, HG_CHUNK_LOG2) == jnp.right_shift(si, HG_CHUNK_LOG2)
    return jnp.bitwise_and(row, HG_CHUNK - 1), jnp.right_shift(row, HG_CHUNK_LOG2), same & tri


def _hgrn_specs(t, nb, rev, bwd):
    n = min(HG_BLOCK, t)
    if rev != bwd:
        blk = lambda i: nb - 1 - i
    else:
        blk = lambda i: i
    col = lambda base: pl.BlockSpec((n, 2 * LANE), lambda hp, i: (blk(i), base + hp))
    return n, blk, col


def _hgrn_fwd(p, lb, rev):
    t = p.shape[0]
    nb = t // min(HG_BLOCK, t)
    n, blk, col = _hgrn_specs(t, nb, rev, False)
    nc = n // HG_CHUNK
    z_base = 7 if rev else 5

    def body(hq_ref, z_ref, hi_ref, lb_ref, o_ref, ssave_ref, st_scr):
        @pl.when(pl.program_id(1) == 0)
        def _():
            st_scr[...] = jnp.zeros_like(st_scr)

        pos, cid, amask = _hgrn_masks(n, rev)
        order = list(range(nc))[::-1] if rev else list(range(nc))
        for hh in range(2):
            sl = slice(hh * LANE, (hh + 1) * LANE)
            q, _, _, f, g = _hgrn_gates(hq_ref[:, sl], z_ref[:, sl], lb_ref[0:1, sl])
            v = hi_ref[:, sl]
            c_ = _hgrn_local(q, f, g, v, pos, cid, amask, rev, nc)
            st = st_scr[hh]
            ssave_ref[0, sl, :] = st
            cols = [None] * nc
            for c in order:
                cols[c] = st
                st = st * c_["dvec"][c * HG_CHUNK:c * HG_CHUNK + 1, :] + c_["ut_all"][:, c * LANE:(c + 1) * LANE]
            st_scr[hh] = st
            st_all = _bf(jnp.concatenate(cols, axis=1))
            o_ref[:, sl] = _dot(_bf(c_["a"]), _bf(v), 1, 0) + _dot(c_["qi_m"], st_all, 1, 1)

    return pl.pallas_call(
        body, name="hgrn_fwd_rev" if rev else "hgrn_fwd", grid=(2, nb),
        in_specs=[col(3), col(z_base), col(9), pl.BlockSpec((1, 2 * LANE), lambda hp, i: (0, hp))],
        out_specs=[pl.BlockSpec((n, 2 * LANE), lambda hp, i: (blk(i), hp)),
                   pl.BlockSpec((1, 2 * LANE, LANE), lambda hp, i: (blk(i), hp, 0))],
        out_shape=[jax.ShapeDtypeStruct((t, HG_DIM), F32), jax.ShapeDtypeStruct((nb, HG_DIM, LANE), F32)],
        scratch_shapes=[pltpu.VMEM((2, LANE, LANE), F32)],
        compiler_params=_params(),
    )(p, p, p, lb)


def _hgrn_bwd(p, lb, do, ssave, rev):
    t = p.shape[0]
    nb = t // min(HG_BLOCK, t)
    n, blk, col = _hgrn_specs(t, nb, rev, True)
    nc = n // HG_CHUNK
    z_base = 7 if rev else 5

    def body(hq_ref, z_ref, hi_ref, lb_ref, do_ref, ssave_ref, dhq_ref, dz_ref, dhi_ref, dlb_ref, dst_scr):
        @pl.when(pl.program_id(1) == 0)
        def _():
            dst_scr[...] = jnp.zeros_like(dst_scr)
            dlb_ref[...] = jnp.zeros_like(dlb_ref)

        pos, cid, amask = _hgrn_masks(n, rev)
        order = list(range(nc))[::-1] if rev else list(range(nc))
        for hh in range(2):
            sl = slice(hh * LANE, (hh + 1) * LANE)
            hq, lbv = hq_ref[:, sl], lb_ref[0:1, sl]
            q, sq, sig, f, g = _hgrn_gates(hq, z_ref[:, sl], lbv)
            v = hi_ref[:, sl]
            c_ = _hgrn_local(q, f, g, v, pos, cid, amask, rev, nc)
            dvec, ut_all = c_["dvec"], c_["ut_all"]
            drow = lambda c: dvec[c * HG_CHUNK:c * HG_CHUNK + 1, :]
            st = ssave_ref[0, sl, :]
            cols = [None] * nc
            for c in order:
                cols[c] = st
                st = st * drow(c) + ut_all[:, c * LANE:(c + 1) * LANE]
            dob, vb = _bf(do_ref[:, sl]), _bf(v)
            gt_all = _dot(dob, c_["qi_m"], 0, 0)
            dcur = dst_scr[hh]
            dnext = [None] * nc
            ddrow = [None] * nc
            for c in order[::-1]:
                dnext[c] = dcur
                ddrow[c] = jnp.sum(cols[c] * dcur, axis=0, keepdims=True) * drow(c)
                dcur = dcur * drow(c) + gt_all[:, c * LANE:(c + 1) * LANE]
            dst_scr[hh] = dcur
            dsn_all = _bf(jnp.concatenate(dnext, axis=1))
            st_all = _bf(jnp.concatenate(cols, axis=1))
            da = _bf(jnp.where(amask, _dot(dob, vb, 1, 1), 0.0))
            dv = _dot(_bf(c_["a"]), dob, 0, 0) + _dot(c_["kd_m"], dsn_all, 1, 1)
            dqb = _dot(da, _bf(c_["kb"]), 1, 0)
            dkb = _dot(da, _bf(c_["qb"]), 0, 0)
            dqi = _chunk_pick(_dot(dob, st_all, 1, 0), cid, nc)
            dkd = _chunk_pick(_dot(vb, dsn_all, 1, 0), cid, nc)
            dq = dqb * c_["ex_q"] + dqi * c_["ex_i"]
            dk = dkb * c_["ex_k"] + dkd * c_["ex_d"]
            e = dqb * c_["qb"] - dkb * c_["kb"] + dqi * c_["qi"]
            w = dkd * c_["kd"]
            dtot = jnp.where(cid == 0, ddrow[0], 0.0)
            for c in range(1, nc):
                dtot = dtot + jnp.where(cid == c, ddrow[c], 0.0)
            dg = _chunk_scan(e, pos, rev) + (_chunk_scan(w, pos, not rev) - w) + dtot
            df = dg / f - dk
            dz_ref[:, sl] = df * (1.0 - lbv) * sig * (1.0 - sig)
            dlb_ref[0:1, sl] += jnp.sum(df * (1.0 - sig), axis=0, keepdims=True)
            dhq_ref[:, sl] = dq * (sq * (1.0 + hq * (1.0 - sq)))
            dhi_ref[:, sl] = dv

    out = pl.BlockSpec((n, 2 * LANE), lambda hp, i: (blk(i), hp))
    return pl.pallas_call(
        body, name="hgrn_bwd_rev" if rev else "hgrn_bwd", grid=(2, nb),
        in_specs=[col(3), col(z_base), col(9), pl.BlockSpec((1, 2 * LANE), lambda hp, i: (0, hp)), out,
                  pl.BlockSpec((1, 2 * LANE, LANE), lambda hp, i: (blk(i), hp, 0))],
        out_specs=[out, out, out, pl.BlockSpec((1, 2 * LANE), lambda hp, i: (0, hp))],
        out_shape=[jax.ShapeDtypeStruct((t, HG_DIM), F32)] * 3 + [jax.ShapeDtypeStruct((1, HG_DIM), F32)],
        scratch_shapes=[pltpu.VMEM((2, LANE, LANE), F32)],
        compiler_params=_params(),
    )(p, p, p, lb, do, ssave)


def _mix_out(att, of, ob, p, gout, w_out, g_post, x, tm=256):
    t, d = x.shape
    tm = min(tm, t)

    def body(att_ref, of_ref, ob_ref, hg0_ref, hg1_ref, gout_ref, w_ref, g_ref, x_ref, cat_ref, y_ref, x1_ref):
        cat_ref[:, :ATT_Q_DIM] = att_ref[...]
        for h in range(HG_HEADS):
            sl = slice(h * LANE, (h + 1) * LANE)
            hg_ref = hg0_ref if h < 2 else hg1_ref
            hg = hg_ref[:, (h % 2) * LANE:(h % 2 + 1) * LANE]
            nrm = _rms_fwd(of_ref[:, sl] + ob_ref[:, sl], gout_ref[...])
            cat_ref[:, ATT_Q_DIM + h * LANE:ATT_Q_DIM + (h + 1) * LANE] = _bf(nrm * hg * _sigmoid(hg))
        y = _dot(cat_ref[...], w_ref[...], 1, 0)
        y_ref[...] = y
        x1_ref[...] = x_ref[...] + _rms_fwd(y, g_ref[...])

    row = lambda width: pl.BlockSpec((tm, width), lambda i: (i, 0))
    return pl.pallas_call(
        body, name="mix_out", grid=(t // tm,),
        in_specs=[row(ATT_Q_DIM), row(HG_DIM), row(HG_DIM), pl.BlockSpec((tm, 2 * LANE), lambda i: (i, 11)),
                  pl.BlockSpec((tm, 2 * LANE), lambda i: (i, 12)), _full((1, LANE)), _full((d, d)), _full((1, d)), row(d)],
        out_specs=[row(d), row(d), row(d)],
        out_shape=[jax.ShapeDtypeStruct((t, d), BF16), jax.ShapeDtypeStruct((t, d), F32), jax.ShapeDtypeStruct((t, d), F32)],
        compiler_params=_params(),
    )(att, of, ob, p, p, gout, w_out, g_post, x)


def _rec_bwd(dcat, of, ob, p, gout, tm=256):
    t = of.shape[0]
    tm = min(tm, t)

    def body(dc_ref, of_ref, ob_ref, hg0_ref, hg1_ref, gout_ref, do_ref, dhg_ref, dgo_ref):
        @pl.when(pl.program_id(0) == 0)
        def _():
            dgo_ref[...] = jnp.zeros_like(dgo_ref)

        dgo = jnp.zeros((1, LANE), F32)
        for h in range(HG_HEADS):
            sl = slice(h * LANE, (h + 1) * LANE)
            hg_ref = hg0_ref if h < 2 else hg1_ref
            hg = hg_ref[:, (h % 2) * LANE:(h % 2 + 1) * LANE]
            o = of_ref[:, sl] + ob_ref[:, sl]
            sg = _sigmoid(hg)
            drec = dc_ref[:, sl]
            dhg_ref[:, sl] = drec * _rms_fwd(o, gout_ref[...]) * (sg * (1.0 + hg * (1.0 - sg)))
            do, dg = _rms_bwd(drec * hg * sg, o, gout_ref[...])
            do_ref[:, sl] = do
            dgo = dgo + dg
        dgo_ref[...] += dgo

    row = lambda width: pl.BlockSpec((tm, width), lambda i: (i, 0))
    return pl.pallas_call(
        body, name="rec_bwd", grid=(t // tm,),
        in_specs=[pl.BlockSpec((tm, HG_DIM), lambda i: (i, 1)), row(HG_DIM), row(HG_DIM),
                  pl.BlockSpec((tm, 2 * LANE), lambda i: (i, 11)), pl.BlockSpec((tm, 2 * LANE), lambda i: (i, 12)),
                  _full((1, LANE))],
        out_specs=[row(HG_DIM), row(HG_DIM), _full((1, LANE))],
        out_shape=[jax.ShapeDtypeStruct((t, HG_DIM), F32), jax.ShapeDtypeStruct((t, HG_DIM), F32),
                   jax.ShapeDtypeStruct((1, LANE), F32)],
        compiler_params=_params(),
    )(dcat, of, ob, p, p, gout)


def _assemble_dp(dp_qk, dv, dhq_f, dhq_b, dz_f, dz_b, dhi_f, dhi_b, dhg, tm=512):
    t = dv.shape[0]
    tm = min(tm, t)
    qk_w = ATT_Q_DIM + LANE

    def body(qk_ref, dv_ref, hqf_ref, hqb_ref, zf_ref, zb_ref, hif_ref, hib_ref, hg_ref, dp_ref):
        o = 0
        dp_ref[:, o:o + qk_w] = qk_ref[...]
        o += qk_w
        dp_ref[:, o:o + LANE] = _bf(dv_ref[...])
        o += LANE
        for val in (hqf_ref[...] + hqb_ref[...], zf_ref[...], zb_ref[...], hif_ref[...] + hib_ref[...], hg_ref[...]):
            dp_ref[:, o:o + HG_DIM] = _bf(val)
            o += HG_DIM

    row = lambda width: pl.BlockSpec((tm, width), lambda i: (i, 0))
    return pl.pallas_call(
        body, name="assemble_dp", grid=(t // tm,),
        in_specs=[row(qk_w), row(LANE)] + [row(HG_DIM)] * 7,
        out_specs=row(N_IN),
        out_shape=jax.ShapeDtypeStruct((t, N_IN), BF16),
        compiler_params=_params(),
    )(dp_qk, dv, dhq_f, dhq_b, dz_f, dz_b, dhi_f, dhi_b, dhg)


def _xattn_fwd(q, kv, tq=512):
    t, d = q.shape
    tq = min(tq, t)
    nm = kv.shape[0]

    def body(q_ref, kv_ref, o_ref):
        for h in range(X_HEADS):
            sl = slice(h * X_HEAD_DIM, (h + 1) * X_HEAD_DIM)
            s = _dot(_bf(q_ref[:, sl]), _bf(kv_ref[:, sl]), 1, 1) * (X_HEAD_DIM ** -0.5)
            e = jnp.exp(s - jnp.max(s, axis=-1, keepdims=True))
            pr = e * (1.0 / jnp.sum(e, axis=-1, keepdims=True))
            o_ref[:, sl] = _bf(_dot(_bf(pr), _bf(kv_ref[:, d + h * X_HEAD_DIM:d + (h + 1) * X_HEAD_DIM]), 1, 0))

    return pl.pallas_call(
        body, name="xattn_fwd", grid=(t // tq,),
        in_specs=[pl.BlockSpec((tq, d), lambda i: (i, 0)), _full((nm, 2 * d))],
        out_specs=pl.BlockSpec((tq, d), lambda i: (i, 0)),
        out_shape=jax.ShapeDtypeStruct((t, d), BF16),
        compiler_params=_params(),
    )(q, kv)


def _xattn_bwd(q, kv, do, tq=512):
    t, d = q.shape
    tq = min(tq, t)
    nm = kv.shape[0]

    def body(q_ref, kv_ref, do_ref, dq_ref, dkv_ref):
        @pl.when(pl.program_id(0) == 0)
        def _():
            dkv_ref[...] = jnp.zeros_like(dkv_ref)

        for h in range(X_HEADS):
            sl = slice(h * X_HEAD_DIM, (h + 1) * X_HEAD_DIM)
            slv = slice(d + h * X_HEAD_DIM, d + (h + 1) * X_HEAD_DIM)
            qb, kb, vb, dob = _bf(q_ref[:, sl]), _bf(kv_ref[:, sl]), _bf(kv_ref[:, slv]), _bf(do_ref[:, sl])
            s = _dot(qb, kb, 1, 1) * (X_HEAD_DIM ** -0.5)
            e = jnp.exp(s - jnp.max(s, axis=-1, keepdims=True))
            pr = e * (1.0 / jnp.sum(e, axis=-1, keepdims=True))
            dpr = _dot(dob, vb, 1, 1)
            ds = _bf(pr * (dpr - jnp.sum(pr * dpr, axis=-1, keepdims=True)) * (X_HEAD_DIM ** -0.5))
            dq_ref[:, sl] = _bf(_dot(ds, kb, 1, 0))
            dkv_ref[:, sl] += _dot(ds, qb, 0, 0)
            dkv_ref[:, slv] += _dot(_bf(pr), dob, 0, 0)

    return pl.pallas_call(
        body, name="xattn_bwd", grid=(t // tq,),
        in_specs=[pl.BlockSpec((tq, d), lambda i: (i, 0)), _full((nm, 2 * d)), pl.BlockSpec((tq, d), lambda i: (i, 0))],
        out_specs=[pl.BlockSpec((tq, d), lambda i: (i, 0)), _full((nm, 2 * d))],
        out_shape=[jax.ShapeDtypeStruct((t, d), BF16), jax.ShapeDtypeStruct((nm, 2 * d), F32)],
        compiler_params=_params(),
    )(q, kv, do)


CONV_TN = 256


def _shift_rows(u, row, t, delta):
    if delta < 0:
        return jnp.where(row == 0, 0.0, pltpu.roll(u, 1, 0))
    return jnp.where(row == t - 1, 0.0, pltpu.roll(u, t - 1, 0))


def _conv_gate_fwd(u, conv_w, conv_b):
    t = u.shape[0]
    nt = D_FF // CONV_TN

    def body(ug_ref, uv_ref, wg_ref, wv_ref, bg_ref, bv_ref, a_ref):
        row = lax.broadcasted_iota(jnp.int32, (t, CONV_TN), 0)

        def conv(u_ref, w_ref, b_ref):
            uu = u_ref[...]
            return (b_ref[...] + _shift_rows(uu, row, t, -1) * w_ref[0:1, :] + uu * w_ref[1:2, :]
                    + _shift_rows(uu, row, t, 1) * w_ref[2:3, :])

        gate = conv(ug_ref, wg_ref, bg_ref)
        a_ref[...] = _bf(gate * _sigmoid(gate) * conv(uv_ref, wv_ref, bv_ref))

    col = lambda rows, off: pl.BlockSpec((rows, CONV_TN), lambda j: (0, j + off))
    return pl.pallas_call(
        body, name="conv_gate_fwd", grid=(nt,),
        in_specs=[col(t, 0), col(t, nt), col(3, 0), col(3, nt), col(1, 0), col(1, nt)],
        out_specs=col(t, 0),
        out_shape=jax.ShapeDtypeStruct((t, D_FF), BF16),
        compiler_params=_params(),
    )(u, u, conv_w, conv_w, conv_b, conv_b)


def _conv_gate_bwd(u, conv_w, conv_b, da):
    t = u.shape[0]
    nt = D_FF // CONV_TN

    def body(ug_ref, uv_ref, wg_ref, wv_ref, bg_ref, bv_ref, da_ref, du_ref, dw_ref, db_ref):
        half = pl.program_id(0)
        row = lax.broadcasted_iota(jnp.int32, (t, CONV_TN), 0)

        def conv(uu, w_ref, b_ref):
            return (b_ref[...] + _shift_rows(uu, row, t, -1) * w_ref[0:1, :] + uu * w_ref[1:2, :]
                    + _shift_rows(uu, row, t, 1) * w_ref[2:3, :])

        ug, uv = ug_ref[...], uv_ref[...]
        gate, val = conv(ug, wg_ref, bg_ref), conv(uv, wv_ref, bv_ref)
        sg = _sigmoid(gate)
        da_ = da_ref[...]
        is_gate = half == 0
        dacc = jnp.where(is_gate, da_ * val * (sg * (1.0 + gate * (1.0 - sg))), da_ * gate * sg)
        uu = jnp.where(is_gate, ug, uv)
        w0 = jnp.where(is_gate, wg_ref[0:1, :], wv_ref[0:1, :])
        w1 = jnp.where(is_gate, wg_ref[1:2, :], wv_ref[1:2, :])
        w2 = jnp.where(is_gate, wg_ref[2:3, :], wv_ref[2:3, :])
        du_ref[...] = _bf(_shift_rows(dacc, row, t, 1) * w0 + dacc * w1 + _shift_rows(dacc, row, t, -1) * w2)
        dw_ref[0:1, :] = jnp.sum(dacc * _shift_rows(uu, row, t, -1), axis=0, keepdims=True)
        dw_ref[1:2, :] = jnp.sum(dacc * uu, axis=0, keepdims=True)
        dw_ref[2:3, :] = jnp.sum(dacc * _shift_rows(uu, row, t, 1), axis=0, keepdims=True)
        db_ref[...] = jnp.sum(dacc, axis=0, keepdims=True)

    col = lambda rows, off: pl.BlockSpec((rows, CONV_TN), lambda hf, j: (0, j + off))
    out = lambda rows: pl.BlockSpec((rows, CONV_TN), lambda hf, j: (0, hf * nt + j))
    return pl.pallas_call(
        body, name="conv_gate_bwd", grid=(2, nt),
        in_specs=[col(t, 0), col(t, nt), col(3, 0), col(3, nt), col(1, 0), col(1, nt), col(t, 0)],
        out_specs=[out(t), out(3), out(1)],
        out_shape=[jax.ShapeDtypeStruct((t, 2 * D_FF), BF16), jax.ShapeDtypeStruct((3, 2 * D_FF), F32),
                   jax.ShapeDtypeStruct((1, 2 * D_FF), F32)],
        compiler_params=_params(),
    )(u, u, conv_w, conv_w, conv_b, conv_b, da)


def _sum_parts(parts, name, tr=128):
    _, r, c = parts.shape
    tr = tr if r % tr == 0 else r

    def body(p_ref, o_ref):
        acc = p_ref[0].astype(F32)
        for i in range(1, N_DEV):
            acc = acc + p_ref[i].astype(F32)
        o_ref[...] = acc

    return pl.pallas_call(
        body, name=name, grid=(r // tr,),
        in_specs=[pl.BlockSpec((N_DEV, tr, c), lambda i: (0, i, 0))],
        out_specs=pl.BlockSpec((tr, c), lambda i: (i, 0)),
        out_shape=jax.ShapeDtypeStruct((r, c), F32),
        compiler_params=_params(),
    )(parts)


def _adamw(w, g, m, v, name, tr=128):
    r, c = w.shape
    tr = tr if r % tr == 0 else r

    def body(w_ref, g_ref, m_ref, v_ref, d_ref, nm_ref, nv_ref):
        g_ = g_ref[...]
        m_ = ADAM_B1 * m_ref[...] + (1.0 - ADAM_B1) * g_
        v_ = ADAM_B2 * v_ref[...] + (1.0 - ADAM_B2) * (g_ * g_)
        m_hat = m_ / (1.0 - ADAM_B1 ** ADAM_STEP)
        v_hat = v_ / (1.0 - ADAM_B2 ** ADAM_STEP)
        d_ref[...] = -ADAM_LR * (m_hat / (jnp.sqrt(v_hat) + ADAM_EPS) + ADAM_WD * w_ref[...])
        nm_ref[...] = m_
        nv_ref[...] = v_

    blk = pl.BlockSpec((tr, c), lambda i: (i, 0))
    return pl.pallas_call(
        body, name=name, grid=(r // tr,),
        in_specs=[blk] * 4, out_specs=[blk] * 3,
        out_shape=[jax.ShapeDtypeStruct((r, c), F32)] * 3,
        compiler_params=_params(),
    )(w, g, m, v)


def _mesh_pos():
    return lax.axis_index("x"), lax.axis_index("y"), lax.axis_index("c")


def _dev_index(px, py, pc):
    return 4 * px + 2 * py + pc


def _all_gather(arrs, name):
    n = len(arrs)

    def body(*refs):
        ins, outs = refs[:n], refs[n:2 * n]
        send_sems, recv_sems, local_sems = refs[2 * n:]
        x, y, c = _mesh_pos()
        me, sibling = (x, y, c), (x, y, 1 - c)
        chips = [(1 - x, y), (x, 1 - y), (1 - x, 1 - y)]

        def copy(k, a, block, to, src=None):
            dst = outs[a].at[_dev_index(*block)]
            return pltpu.make_async_remote_copy(
                src_ref=dst if src is None else src, dst_ref=dst, send_sem=send_sems.at[k, a], recv_sem=recv_sems.at[k, a],
                device_id=to, device_id_type=MESH_T)

        mine = [pltpu.make_async_copy(ins[a], outs[a].at[_dev_index(*me)], local_sems.at[a]) for a in range(n)]
        for cp in mine:
            cp.start()
        first = [copy(0, a, me, sibling, src=ins[a]) for a in range(n)]
        first += [copy(1 + j, a, me, (*chip, c), src=ins[a]) for j, chip in enumerate(chips) for a in range(n)]
        for cp in first:
            cp.start()
        passed = []
        for j, chip in enumerate(chips):
            for a in range(n):
                copy(1 + j, a, (*chip, c), me).wait_recv()
                fwd = copy(4 + j, a, (*chip, c), sibling)
                fwd.start()
                passed.append(fwd)
        for a in range(n):
            copy(0, a, sibling, me).wait_recv()
        for j, chip in enumerate(chips):
            for a in range(n):
                copy(4 + j, a, (*chip, 1 - c), me).wait_recv()
        for cp in first + passed:
            cp.wait_send()
        for cp in mine:
            cp.wait()

    any_spec = pl.BlockSpec(memory_space=pl.ANY)
    return pl.pallas_call(
        body, name=name,
        in_specs=[any_spec] * n, out_specs=[any_spec] * n,
        out_shape=[jax.ShapeDtypeStruct((N_DEV,) + a.shape, a.dtype) for a in arrs],
        scratch_shapes=[pltpu.SemaphoreType.DMA((7, n)), pltpu.SemaphoreType.DMA((7, n)), pltpu.SemaphoreType.DMA((n,))],
    )(*arrs)


def _exchange_parts(arrs, name):
    n = len(arrs)

    def body(*refs):
        ins, outs = refs[:n], refs[n:2 * n]
        send_sems, recv_sems, local_sems = refs[2 * n:]
        x, y, c = _mesh_pos()
        me_i = _dev_index(x, y, c)
        mine = [pltpu.make_async_copy(ins[a].at[me_i], outs[a].at[me_i], local_sems.at[a]) for a in range(n)]
        for cp in mine:
            cp.start()
        copies = []
        for k in range(1, N_DEV):
            px = 1 - x if k & 4 else x
            py = 1 - y if k & 2 else y
            pc = 1 - c if k & 1 else c
            peer_i = _dev_index(px, py, pc)
            for a in range(n):
                cp = pltpu.make_async_remote_copy(
                    src_ref=ins[a].at[peer_i], dst_ref=outs[a].at[me_i], send_sem=send_sems.at[k - 1, a],
                    recv_sem=recv_sems.at[k - 1, a], device_id=(px, py, pc), device_id_type=MESH_T)
                cp.start()
                copies.append(cp)
        for cp in copies:
            cp.wait_recv()
        for cp in copies:
            cp.wait_send()
        for cp in mine:
            cp.wait()

    any_spec = pl.BlockSpec(memory_space=pl.ANY)
    return pl.pallas_call(
        body, name=name,
        in_specs=[any_spec] * n, out_specs=[any_spec] * n,
        out_shape=[jax.ShapeDtypeStruct(a.shape, a.dtype) for a in arrs],
        scratch_shapes=[pltpu.SemaphoreType.DMA((7, n)), pltpu.SemaphoreType.DMA((7, n)), pltpu.SemaphoreType.DMA((n,))],
    )(*arrs)


def _local_step(x, mem, tgt, wts):
    tables = _rope_tables(x.shape[0])
    bd = _head_sum_matrix()
    two = lambda g: jnp.tile(g, (1, 2))
    gq2, gk2 = two(wts["q_norm_g"]), two(wts["k_norm_g"])
    lb_a0, lb_a1 = wts["hg_lb"][:, 0, :], wts["hg_lb"][:, 1, :]

    h1, p = _norm_mm(x, wts["pre_mix_g"], wts["w_in_t"], True, 1664, "in_proj")
    qr, kr = _qk_prep(p, gq2, gk2, tables, bd)
    att = _attn_fwd(qr, kr, p)
    lb = _lower_bounds(lb_a0, lb_a1)
    of, s_f = _hgrn_fwd(p, lb[0:1], False)
    ob, s_b = _hgrn_fwd(p, lb[1:2], True)
    cat, mixed, x1 = _mix_out(att, of, ob, p, wts["hg_out_norm_g"], wts["w_out"], wts["post_mix_g"], x)

    h2, q2 = _norm_mm(x1, wts["pre_x_g"], wts["w_xq"], False, 1024, "xq_proj")
    mn, kv = _norm_mm(mem, wts["mem_norm_g"], wts["w_xkv_t"], True, 1024, "xkv_proj")
    o2 = _xattn_fwd(q2, kv)
    y2, x2 = _mm_postnorm_res(o2, wts["w_xo"], wts["post_x_g"], x1, "xo_proj")

    h3, u = _norm_mm(x2, wts["pre_ffn_g"], wts["w_up_t"], True, 1408, "up_proj")
    a = _conv_gate_fwd(u, wts["conv_w"], wts["conv_b"])
    y3, dx3, loss = _mm_postnorm_res_loss(a, wts["w_down"], wts["post_ffn_g"], x2, tgt, "down_proj")

    g = {}
    dy3, da, g["post_ffn_g"] = _postnorm_bwd_mm(dx3, y3, wts["post_ffn_g"], wts["w_down"], "down_bwd")
    g["w_down"] = _dw(a, dy3, "dw_down")
    du, g["conv_w"], g["conv_b"] = _conv_gate_bwd(u, wts["conv_w"], wts["conv_b"], da)
    g["w_up_t"] = _dw(du, h3, "dw_up")
    dx2, g["pre_ffn_g"] = _mm_prenorm_bwd(du, wts["w_up_t"], True, x2, wts["pre_ffn_g"], dx3, 1408, "up_bwd")

    dy2, do2, g["post_x_g"] = _postnorm_bwd_mm(dx2, y2, wts["post_x_g"], wts["w_xo"], "xo_bwd")
    g["w_xo"] = _dw(o2, dy2, "dw_xo")
    dq2, dkv = _xattn_bwd(q2, kv, do2)
    g["w_xq"] = _dw(h2, dq2, "dw_xq")
    dkvb = _bf(dkv)
    g["w_xkv_t"] = _dw(dkvb, mn, "dw_xkv")
    _, g["mem_norm_g"] = _mm_prenorm_bwd(dkvb, wts["w_xkv_t"], True, mem, wts["mem_norm_g"], jnp.zeros_like(mem), 2048,
                                         "xkv_bwd")
    dx1, g["pre_x_g"] = _mm_prenorm_bwd(dq2, wts["w_xq"], False, x1, wts["pre_x_g"], dx2, 1024, "xq_bwd")

    dmixed, dcat, g["post_mix_g"] = _postnorm_bwd_mm(dx1, mixed, wts["post_mix_g"], wts["w_out"], "out_bwd")
    g["w_out"] = _dw(cat, dmixed, "dw_out")
    do, dhg, g["hg_out_norm_g"] = _rec_bwd(dcat, of, ob, p, wts["hg_out_norm_g"])
    dhq_f, dz_f, dhi_f, dlb_f = _hgrn_bwd(p, lb[0:1], do, s_f, False)
    dhq_b, dz_b, dhi_b, dlb_b = _hgrn_bwd(p, lb[1:2], do, s_b, True)
    d_a0, d_a1 = _lower_bounds_bwd(lb_a0, lb_a1, jnp.concatenate([dlb_f, dlb_b], axis=0))
    g["hg_lb"] = jnp.stack([d_a0, d_a1], axis=1)
    dqr, dkr, dv = _attn_bwd(qr, kr, p, dcat)
    dp_qk, dgq, dgk = _qk_prep_bwd(p, dqr, dkr, gq2, gk2, tables, bd)
    fold = lambda v2: v2[:, :ATT_HEAD_DIM] + v2[:, ATT_HEAD_DIM:]
    g["q_norm_g"], g["k_norm_g"] = fold(dgq), fold(dgk)
    dp = _assemble_dp(dp_qk, dv, dhq_f, dhq_b, dz_f, dz_b, dhi_f, dhi_b, dhg)
    g["w_in_t"] = _dw(dp, h1, "dw_in")
    dx, g["pre_mix_g"] = _mm_prenorm_bwd(dp, wts["w_in_t"], True, x, wts["pre_mix_g"], dx1, 1664, "in_bwd")
    return loss, dx, g


_COL_SHARDED = ("w_in", "w_xkv", "w_up")
_ROW_SHARDED = ("w_out", "w_xq", "w_xo", "w_down")
_REPLICATED = ("pre_mix_g", "q_norm_g", "k_norm_g", "hg_out_norm_g", "post_mix_g", "pre_x_g", "mem_norm_g", "post_x_g",
               "pre_ffn_g", "conv_b", "post_ffn_g")
_WEIGHTS = ("pre_mix_g", "w_in", "q_norm_g", "k_norm_g", "hg_lb", "hg_out_norm_g", "w_out", "post_mix_g", "pre_x_g",
            "mem_norm_g", "w_xq", "w_xkv", "w_xo", "post_x_g", "pre_ffn_g", "w_up", "conv_w", "conv_b", "w_down",
            "post_ffn_g")
PACK_W = 1024


def _pack_rows(vals):
    rows, spans, r0 = [], [], 0
    for val in vals:
        flat = val.reshape(-1).astype(F32)
        nr = -(-flat.shape[0] // (8 * PACK_W)) * 8
        rows.append(jnp.pad(flat, (0, nr * PACK_W - flat.shape[0])).reshape(nr, PACK_W))
        spans.append((r0, nr))
        r0 += nr
    return jnp.concatenate(rows, axis=0), spans


def _unpack_rows(packed, spans, shapes):
    out = []
    for (r0, nr), shp in zip(spans, shapes):
        size = 1
        for s in shp:
            size *= s
        out.append(packed[r0:r0 + nr].reshape(-1)[:size].reshape(shp))
    return out


def kernel(x, mem, pre_mix_g, w_in, q_norm_g, k_norm_g, hg_lb, hg_out_norm_g, w_out, post_mix_g, pre_x_g, mem_norm_g, w_xq, w_xkv, w_xo, post_x_g, pre_ffn_g, w_up, conv_w, conv_b, w_down, post_ffn_g, loss_target, m_pre_mix_g, m_w_in, m_q_norm_g, m_k_norm_g, m_hg_lb, m_hg_out_norm_g, m_w_out, m_post_mix_g, m_pre_x_g, m_mem_norm_g, m_w_xq, m_w_xkv, m_w_xo, m_post_x_g, m_pre_ffn_g, m_w_up, m_conv_w, m_conv_b, m_w_down, m_post_ffn_g, v_pre_mix_g, v_w_in, v_q_norm_g, v_k_norm_g, v_hg_lb, v_hg_out_norm_g, v_w_out, v_post_mix_g, v_pre_x_g, v_mem_norm_g, v_w_xq, v_w_xkv, v_w_xo, v_post_x_g, v_pre_ffn_g, v_w_up, v_conv_w, v_conv_b, v_w_down, v_post_ffn_g):
    args = dict(locals())
    w = {n: args[n] for n in _WEIGHTS}
    m = {n: args["m_" + n] for n in _WEIGHTS}
    v = {n: args["v_" + n] for n in _WEIGHTS}
    me = _dev_index(*_mesh_pos())

    send = [_bf(w[n][0].T) for n in _COL_SHARDED] + [_bf(w[n][0]) for n in _ROW_SHARDED]
    send += [w["hg_lb"].reshape(4, -1), w["conv_w"][0]]
    got = _all_gather(send, "gather_weights")
    wts = {}
    for n, a in zip(_COL_SHARDED, got[:3]):
        wts[n + "_t"] = a.reshape(-1, a.shape[-1])
    for n, a in zip(_ROW_SHARDED, got[3:7]):
        wts[n] = a.reshape(-1, a.shape[-1])
    wts["hg_lb"] = jnp.transpose(got[7].reshape(N_DEV, 2, 2, -1), (1, 2, 0, 3)).reshape(2, 2, HG_DIM)
    wts["conv_w"] = jnp.transpose(got[8], (1, 0, 2)).reshape(3, 2 * D_FF)
    for n in _REPLICATED:
        wts[n] = w[n]

    loss, grad_x, g = _local_step(x[0], mem[0], loss_target[0], wts)
    loss = lax.psum(loss[0, 0], ("x", "y", "c"))

    big = [n + "_t" for n in _COL_SHARDED] + list(_ROW_SHARDED)
    parts = _exchange_parts([g[n].reshape(N_DEV, -1, g[n].shape[-1]) for n in big], "exchange_grads")
    grads = {}
    for n, prt in zip(_COL_SHARDED, parts[:3]):
        grads[n] = _sum_parts(prt, "sum_" + n).T[None]
    for n, prt in zip(_ROW_SHARDED, parts[3:]):
        grads[n] = _sum_parts(prt, "sum_" + n)[None]

    small = list(_REPLICATED) + ["hg_lb", "conv_w"]
    packed, spans = _pack_rows([g[n] for n in small])
    (all_small,) = _all_gather([packed], "gather_small_grads")
    summed = _unpack_rows(_sum_parts(all_small, "sum_small", tr=8), spans, [g[n].shape for n in small])
    for n, s in zip(small, summed):
        grads[n] = s
    grads["hg_lb"] = lax.dynamic_slice_in_dim(grads["hg_lb"], me * (HG_DIM // N_DEV), HG_DIM // N_DEV, axis=2)
    grads["conv_w"] = lax.dynamic_slice_in_dim(grads["conv_w"], me * (2 * D_FF // N_DEV), 2 * D_FF // N_DEV, axis=1)[None]
    for n in _REPLICATED:
        grads[n] = grads[n].reshape(w[n].shape)

    delta, new_m, new_v = {}, {}, {}
    for n in _COL_SHARDED + _ROW_SHARDED + ("hg_lb", "conv_w"):
        shp = w[n].shape
        flat = lambda a: a.reshape(-1, shp[-1])
        d_, m_, v_ = _adamw(flat(w[n]), flat(grads[n]), flat(m[n]), flat(v[n]), "adamw_" + n)
        delta[n], new_m[n], new_v[n] = d_.reshape(shp), m_.reshape(shp), v_.reshape(shp)
    pk = lambda d: _pack_rows([d[n] for n in _REPLICATED])
    (pw, rspans), (pg, _), (pm, _), (pv, _) = pk(w), pk(grads), pk(m), pk(v)
    shapes = [w[n].shape for n in _REPLICATED]
    for dst, packed_out in zip((delta, new_m, new_v), _adamw(pw, pg, pm, pv, "adamw_small", tr=8)):
        for n, val in zip(_REPLICATED, _unpack_rows(packed_out, rspans, shapes)):
            dst[n] = val

    return (loss, grad_x[None], *[grads[n] for n in _WEIGHTS], *[delta[n] for n in _WEIGHTS],
            *[new_m[n] for n in _WEIGHTS], *[new_v[n] for n in _WEIGHTS])
```

```python
import jax
import jax.numpy as jnp
from jax import lax
from jax.experimental import pallas as pl
from jax.experimental.pallas import tpu as pltpu

F32 = jnp.float32
BF16 = jnp.bfloat16

D_MODEL = 1024
GRID_W = 64
EPS = 1e-6
ATT_HEADS = 8
ATT_HEAD_DIM = 64
ATT_Q_DIM = 512
ATT_KV_DIM = 128
ROPE_THETA = 10000.0
HG_HEADS = 4
HG_DIM = 512
HG_CHUNK = 32
HG_CHUNK_LOG2 = 5
HG_BLOCK = 256
N_IN = 3328
X_HEADS = 4
X_HEAD_DIM = 256
D_FF = 2816
N_DEV = 8
LANE = 128
ADAM_LR = 0.001
ADAM_B1 = 0.9
ADAM_B2 = 0.999
ADAM_EPS = 1e-08
ADAM_WD = 0.01
ADAM_STEP = 10
VMEM_LIMIT = 56 * 1024 * 1024

MESH_T = pl.DeviceIdType.MESH


def _params(**kw):
    return pltpu.CompilerParams(vmem_limit_bytes=VMEM_LIMIT, **kw)


def _dot(a, b, ca, cb):
    return lax.dot_general(a, b, (((ca,), (cb,)), ((), ())), preferred_element_type=F32)


def _bf(x):
    return x.astype(BF16)


def _sigmoid(x):
    return 1.0 / (1.0 + jnp.exp(-x))


def _rms_fwd(x, g):
    r = lax.rsqrt(jnp.mean(x * x, axis=-1, keepdims=True) + EPS)
    return x * r * g


def _rms_bwd(dy, x, g):
    r = lax.rsqrt(jnp.mean(x * x, axis=-1, keepdims=True) + EPS)
    xh = x * r
    dg = jnp.sum(dy * xh, axis=0, keepdims=True)
    t = dy * g
    dx = r * (t - xh * jnp.mean(t * xh, axis=-1, keepdims=True))
    return dx, dg


def _full(shape):
    nd = len(shape)
    return pl.BlockSpec(shape, lambda *a: (0,) * nd)


def _norm_mm(x, g, w, trans, tn, name, tm=512):
    t, d = x.shape
    n = w.shape[0] if trans else w.shape[1]
    tm = min(tm, t)

    def body(x_ref, g_ref, w_ref, h_ref, p_ref):
        h = _bf(_rms_fwd(x_ref[...], g_ref[...]))
        h_ref[...] = h
        p_ref[...] = _dot(h, w_ref[...], 1, 1 if trans else 0)

    w_spec = pl.BlockSpec((tn, d), lambda i, j: (j, 0)) if trans else pl.BlockSpec((d, tn), lambda i, j: (0, j))
    return pl.pallas_call(
        body, name=name, grid=(t // tm, n // tn),
        in_specs=[pl.BlockSpec((tm, d), lambda i, j: (i, 0)), _full((1, d)), w_spec],
        out_specs=[pl.BlockSpec((tm, d), lambda i, j: (i, 0)), pl.BlockSpec((tm, tn), lambda i, j: (i, j))],
        out_shape=[jax.ShapeDtypeStruct((t, d), BF16), jax.ShapeDtypeStruct((t, n), F32)],
        compiler_params=_params(),
    )(x, g, w)


def _mm_postnorm_res(a, w, g, res, name, tm=256):
    t, k = a.shape
    d = w.shape[1]

    def body(a_ref, w_ref, g_ref, res_ref, y_ref, o_ref):
        y = _dot(a_ref[...], w_ref[...], 1, 0)
        y_ref[...] = y
        o_ref[...] = res_ref[...] + _rms_fwd(y, g_ref[...])

    row = lambda width: pl.BlockSpec((tm, width), lambda i: (i, 0))
    return pl.pallas_call(
        body, name=name, grid=(t // tm,),
        in_specs=[row(k), _full((k, d)), _full((1, d)), row(d)],
        out_specs=[row(d), row(d)],
        out_shape=[jax.ShapeDtypeStruct((t, d), F32)] * 2,
        compiler_params=_params(),
    )(a, w, g, res)


def _mm_postnorm_res_loss(a, w, g, res, tgt, name, tm=256):
    t, k = a.shape
    d = w.shape[1]

    def body(a_ref, w_ref, g_ref, res_ref, tgt_ref, y_ref, dout_ref, loss_ref):
        @pl.when(pl.program_id(0) == 0)
        def _():
            loss_ref[...] = jnp.zeros_like(loss_ref)

        y = _dot(a_ref[...], w_ref[...], 1, 0)
        y_ref[...] = y
        diff = res_ref[...] + _rms_fwd(y, g_ref[...]) - tgt_ref[...]
        dout_ref[...] = diff * (1.0 / d)
        part = jnp.sum(jnp.sum(diff * diff, axis=-1, keepdims=True), axis=0, keepdims=True)
        loss_ref[...] += (0.5 / d) * part

    row = lambda width: pl.BlockSpec((tm, width), lambda i: (i, 0))
    return pl.pallas_call(
        body, name=name, grid=(t // tm,),
        in_specs=[row(k), _full((k, d)), _full((1, d)), row(d), row(d)],
        out_specs=[row(d), row(d), _full((1, 1))],
        out_shape=[jax.ShapeDtypeStruct((t, d), F32)] * 2 + [jax.ShapeDtypeStruct((1, 1), F32)],
        compiler_params=_params(),
    )(a, w, g, res, tgt)


def _postnorm_bwd_mm(dout, y, g, w, name, tm=256):
    t, d = y.shape
    k = w.shape[0]

    def body(dout_ref, y_ref, g_ref, w_ref, dy_ref, da_ref, dg_ref):
        @pl.when(pl.program_id(0) == 0)
        def _():
            dg_ref[...] = jnp.zeros_like(dg_ref)

        dy, dg = _rms_bwd(dout_ref[...], y_ref[...], g_ref[...])
        dg_ref[...] += dg
        dyb = _bf(dy)
        dy_ref[...] = dyb
        da_ref[...] = _dot(dyb, w_ref[...], 1, 1)

    row = lambda width: pl.BlockSpec((tm, width), lambda i: (i, 0))
    return pl.pallas_call(
        body, name=name, grid=(t // tm,),
        in_specs=[row(d), row(d), _full((1, d)), _full((k, d))],
        out_specs=[row(d), row(k), _full((1, d))],
        out_shape=[jax.ShapeDtypeStruct((t, d), BF16), jax.ShapeDtypeStruct((t, k), F32), jax.ShapeDtypeStruct((1, d), F32)],
        compiler_params=_params(),
    )(dout, y, g, w)


def _mm_prenorm_bwd(dp, w, trans, x, g, dres, tk, name, comm=None, tm=512):
    dps = list(dp) if isinstance(dp, (list, tuple)) else [dp]
    nparts = len(dps)
    t, n_each = dps[0].shape
    d = x.shape[1]
    tm = min(tm, t)
    nk_each = n_each // tk
    nk = nk_each * nparts

    def body(*refs):
        dp_refs = refs[:nparts]
        w_ref, x_ref, g_ref, dres_ref, dx_ref, dg_ref, acc_ref = refs[nparts:]
        i, kk = pl.program_id(0), pl.program_id(1)

        @pl.when((i == 0) & (kk == 0))
        def _():
            dg_ref[...] = jnp.zeros_like(dg_ref)

        @pl.when(kk == 0)
        def _():
            acc_ref[...] = jnp.zeros_like(acc_ref)

        for part, dp_ref in enumerate(dp_refs):
            @pl.when((kk >= part * nk_each) & (kk < (part + 1) * nk_each))
            def _():
                acc_ref[...] += _dot(dp_ref[...], w_ref[...], 1, 0 if trans else 1)

        @pl.when(kk == nk - 1)
        def _():
            dx, dg = _rms_bwd(acc_ref[...], x_ref[...], g_ref[...])
            dg_ref[...] += dg
            dx_ref[...] = dres_ref[...] + dx

    dp_spec = lambda part: pl.BlockSpec((tm, tk), lambda i, kk: (i, jnp.clip(kk - part * nk_each, 0, nk_each - 1)))
    w_spec = pl.BlockSpec((tk, d), lambda i, kk: (kk, 0)) if trans else pl.BlockSpec((d, tk), lambda i, kk: (0, kk))
    row = pl.BlockSpec((tm, d), lambda i, kk: (i, 0))
    return _call(
        body, (*dps, w, x, g, dres), name=name, grid=(t // tm, nk),
        in_specs=[dp_spec(part) for part in range(nparts)] + [w_spec, row, _full((1, d)), row],
        out_specs=[row, _full((1, d))],
        out_shape=[jax.ShapeDtypeStruct((t, d), F32), jax.ShapeDtypeStruct((1, d), F32)],
        scratch_shapes=[pltpu.VMEM((tm, d), F32)], comm=comm)


def _dw(a, b, name, tka=256):
    parts = list(a) if isinstance(a, (list, tuple)) else [a]
    nparts = len(parts)
    t, ka_each = parts[0].shape
    nb = b.shape[1]
    nt_each = ka_each // tka

    def body(*refs):
        a_refs, b_ref, o_ref = refs[:nparts], refs[nparts], refs[nparts + 1]
        i = pl.program_id(0)
        for part, a_ref in enumerate(a_refs):
            @pl.when((i >= part * nt_each) & (i < (part + 1) * nt_each))
            def _():
                o_ref[...] = _bf(_dot(a_ref[...], b_ref[...], 0, 0))

    a_spec = lambda part: pl.BlockSpec((t, tka), lambda i: (0, jnp.clip(i - part * nt_each, 0, nt_each - 1)))
    return pl.pallas_call(
        body, name=name, grid=(nt_each * nparts,),
        in_specs=[a_spec(part) for part in range(nparts)] + [_full((t, nb))],
        out_specs=pl.BlockSpec((tka, nb), lambda i: (i, 0)),
        out_shape=jax.ShapeDtypeStruct((nparts * ka_each, nb), BF16),
        compiler_params=_params(),
    )(*parts, b)


def _rope_tables(t):
    pos = jnp.arange(t)
    r = (pos // GRID_W).astype(F32)
    c = (pos % GRID_W).astype(F32)
    npair = ATT_HEAD_DIM // 4
    inv = jnp.power(ROPE_THETA, -jnp.arange(npair, dtype=F32) / npair)
    ang = jnp.concatenate([r[:, None] * inv, c[:, None] * inv], axis=-1)
    cos = jnp.repeat(jnp.cos(ang), 2, axis=-1)
    sin = jnp.repeat(jnp.sin(ang), 2, axis=-1)
    even = (jnp.arange(ATT_HEAD_DIM) % 2) == 0
    sa = jnp.where(even, -sin, 0.0)
    sb = jnp.where(even, 0.0, sin)
    two = lambda a: jnp.tile(a, (1, 2))
    return two(cos), two(sa), two(sb)


def _head_sum_matrix():
    a = jnp.arange(LANE) // ATT_HEAD_DIM
    return (a[:, None] == a[None, :]).astype(BF16)


def _head_mean(v, bd):
    hi = _bf(v)
    lo = _bf(v - hi.astype(F32))
    return (_dot(hi, bd, 1, 0) + _dot(lo, bd, 1, 0)) * (1.0 / ATT_HEAD_DIM)


def _qk_prep(p, gq, gk, tables, bd, tm=512):
    t = p.shape[0]
    tm = min(tm, t)
    cc, sa, sb = tables

    def body(p_ref, gq_ref, gk_ref, cc_ref, sa_ref, sb_ref, bd_ref, q_ref, k_ref):
        cc_, sa_, sb_, bd_ = cc_ref[...], sa_ref[...], sb_ref[...], bd_ref[...]
        low = lax.broadcasted_iota(jnp.int32, (tm, LANE), 1) < ATT_HEAD_DIM

        def normrope(xs, g):
            xn = xs * lax.rsqrt(_head_mean(xs * xs, bd_) + EPS) * g
            return xn * cc_ + pltpu.roll(xn, LANE - 1, 1) * sa_ + pltpu.roll(xn, 1, 1) * sb_

        for j in range(4):
            y = normrope(p_ref[:, j * LANE:(j + 1) * LANE], gq_ref[...]) * (ATT_HEAD_DIM ** -0.5)
            yr = pltpu.roll(y, ATT_HEAD_DIM, 1)
            if j // 2 == 0:
                h0, h1 = jnp.where(low, y, 0.0), jnp.where(low, yr, 0.0)
            else:
                h0, h1 = jnp.where(low, 0.0, yr), jnp.where(low, 0.0, y)
            q_ref[:, (2 * j) * LANE:(2 * j + 1) * LANE] = _bf(h0)
            q_ref[:, (2 * j + 1) * LANE:(2 * j + 2) * LANE] = _bf(h1)
        k_ref[...] = _bf(normrope(p_ref[:, ATT_Q_DIM:ATT_Q_DIM + LANE], gk_ref[...]))

    row = lambda width: pl.BlockSpec((tm, width), lambda i: (i, 0))
    return pl.pallas_call(
        body, name="qk_prep", grid=(t // tm,),
        in_specs=[row(ATT_Q_DIM + LANE), _full((1, LANE)), _full((1, LANE)), row(LANE), row(LANE), row(LANE),
                  _full((LANE, LANE))],
        out_specs=[row(ATT_HEADS * LANE), row(LANE)],
        out_shape=[jax.ShapeDtypeStruct((t, ATT_HEADS * LANE), BF16), jax.ShapeDtypeStruct((t, LANE), BF16)],
        compiler_params=_params(),
    )(p, gq, gk, cc, sa, sb, bd)


def _qk_prep_bwd(p, dq, dk, gq, gk, tables, bd, tm=512):
    t = p.shape[0]
    tm = min(tm, t)
    cc, sa, sb = tables

    def body(p_ref, dq_ref, dk_ref, gq_ref, gk_ref, cc_ref, sa_ref, sb_ref, bd_ref, dp_ref, dgq_ref, dgk_ref):
        @pl.when(pl.program_id(0) == 0)
        def _():
            dgq_ref[...] = jnp.zeros_like(dgq_ref)
            dgk_ref[...] = jnp.zeros_like(dgk_ref)

        cc_, sa_, sb_, bd_ = cc_ref[...], sa_ref[...], sb_ref[...], bd_ref[...]
        low = lax.broadcasted_iota(jnp.int32, (tm, LANE), 1) < ATT_HEAD_DIM

        def bwd(xs, g, dy):
            r = lax.rsqrt(_head_mean(xs * xs, bd_) + EPS)
            xh = xs * r
            dxn = dy * cc_ + pltpu.roll(dy * sa_, 1, 1) + pltpu.roll(dy * sb_, LANE - 1, 1)
            dg = jnp.sum(dxn * xh, axis=0, keepdims=True)
            tt = dxn * g
            return r * (tt - xh * _head_mean(tt * xh, bd_)), dg

        dgq = jnp.zeros((1, LANE), F32)
        for j in range(4):
            d0 = dq_ref[:, (2 * j) * LANE:(2 * j + 1) * LANE]
            d1 = dq_ref[:, (2 * j + 1) * LANE:(2 * j + 2) * LANE]
            if j // 2 == 0:
                dy = jnp.where(low, d0, pltpu.roll(d1, ATT_HEAD_DIM, 1))
            else:
                dy = jnp.where(low, pltpu.roll(d0, ATT_HEAD_DIM, 1), d1)
            dx, dg = bwd(p_ref[:, j * LANE:(j + 1) * LANE], gq_ref[...], dy * (ATT_HEAD_DIM ** -0.5))
            dp_ref[:, j * LANE:(j + 1) * LANE] = _bf(dx)
            dgq = dgq + dg
        dgq_ref[...] += dgq
        dx, dg = bwd(p_ref[:, ATT_Q_DIM:ATT_Q_DIM + LANE], gk_ref[...], dk_ref[...])
        dp_ref[:, ATT_Q_DIM:ATT_Q_DIM + LANE] = _bf(dx)
        dgk_ref[...] += dg

    row = lambda width: pl.BlockSpec((tm, width), lambda i: (i, 0))
    return pl.pallas_call(
        body, name="qk_prep_bwd", grid=(t // tm,),
        in_specs=[row(ATT_Q_DIM + LANE), row(ATT_HEADS * LANE), row(LANE), _full((1, LANE)), _full((1, LANE)),
                  row(LANE), row(LANE), row(LANE), _full((LANE, LANE))],
        out_specs=[row(ATT_Q_DIM + LANE), _full((1, LANE)), _full((1, LANE))],
        out_shape=[jax.ShapeDtypeStruct((t, ATT_Q_DIM + LANE), BF16), jax.ShapeDtypeStruct((1, LANE), F32),
                   jax.ShapeDtypeStruct((1, LANE), F32)],
        compiler_params=_params(),
    )(p, dq, dk, gq, gk, cc, sa, sb, bd)


def _attn_fwd(q, k, p, comm=None, tq=256):
    t = k.shape[0]
    tq = min(tq, t)
    v_blk = (ATT_Q_DIM + ATT_KV_DIM) // LANE

    def body(q_ref, k_ref, v_ref, o_ref):
        k_ = k_ref[...]
        v = v_ref[...]
        lowk = lax.broadcasted_iota(jnp.int32, (t, LANE), 1) < ATT_HEAD_DIM
        vm = (_bf(jnp.where(lowk, v, 0.0)), _bf(jnp.where(lowk, 0.0, v)))
        for j in range(4):
            kvh = j // 2
            acc = None
            for sub in range(2):
                h = 2 * j + sub
                s = _dot(q_ref[:, h * LANE:(h + 1) * LANE], k_, 1, 1)
                e = jnp.exp(s - jnp.max(s, axis=-1, keepdims=True))
                o = _dot(_bf(e), vm[kvh], 1, 0) * (1.0 / jnp.sum(e, axis=-1, keepdims=True))
                if sub != kvh:
                    o = pltpu.roll(o, ATT_HEAD_DIM, 1)
                acc = o if acc is None else acc + o
            o_ref[:, j * LANE:(j + 1) * LANE] = _bf(acc)

    (att,), got = _call(
        body, (q, k, p), name="attn_fwd", grid=(t // tq,),
        in_specs=[pl.BlockSpec((tq, ATT_HEADS * LANE), lambda i: (i, 0)), _full((t, LANE)),
                  pl.BlockSpec((t, LANE), lambda i: (0, v_blk))],
        out_specs=[pl.BlockSpec((tq, ATT_Q_DIM), lambda i: (i, 0))],
        out_shape=[jax.ShapeDtypeStruct((t, ATT_Q_DIM), BF16)], comm=comm)
    return att, got


def _attn_bwd(q, k, p, dcat, comm=None, tq=256):
    t = k.shape[0]
    tq = min(tq, t)
    v_blk = (ATT_Q_DIM + ATT_KV_DIM) // LANE

    def body(q_ref, k_ref, v_ref, do_ref, dq_ref, dk_ref, dv_ref):
        @pl.when(pl.program_id(0) == 0)
        def _():
            dk_ref[...] = jnp.zeros_like(dk_ref)
            dv_ref[...] = jnp.zeros_like(dv_ref)

        k_ = k_ref[...]
        vb = _bf(v_ref[...])
        lowq = lax.broadcasted_iota(jnp.int32, (tq, LANE), 1) < ATT_HEAD_DIM
        dk_acc = jnp.zeros((t, LANE), F32)
        dv_acc = jnp.zeros((t, LANE), F32)
        for j in range(4):
            kvh = j // 2
            dop = do_ref[:, j * LANE:(j + 1) * LANE]
            for sub in range(2):
                h = 2 * j + sub
                src = dop if sub == kvh else pltpu.roll(dop, ATT_HEAD_DIM, 1)
                do_h = _bf(jnp.where(lowq, src, 0.0) if kvh == 0 else jnp.where(lowq, 0.0, src))
                qh = q_ref[:, h * LANE:(h + 1) * LANE]
                s = _dot(qh, k_, 1, 1)
                e = jnp.exp(s - jnp.max(s, axis=-1, keepdims=True))
                pr = e * (1.0 / jnp.sum(e, axis=-1, keepdims=True))
                dpr = _dot(do_h, vb, 1, 1)
                ds = _bf(pr * (dpr - jnp.sum(pr * dpr, axis=-1, keepdims=True)))
                dq_ref[:, h * LANE:(h + 1) * LANE] = _dot(ds, k_, 1, 0)
                dk_acc = dk_acc + _dot(ds, qh, 0, 0)
                dv_acc = dv_acc + _dot(_bf(pr), do_h, 0, 0)
        dk_ref[...] += dk_acc
        dv_ref[...] += dv_acc

    return _call(
        body, (q, k, p, dcat), name="attn_bwd", grid=(t // tq,),
        in_specs=[pl.BlockSpec((tq, ATT_HEADS * LANE), lambda i: (i, 0)), _full((t, LANE)),
                  pl.BlockSpec((t, LANE), lambda i: (0, v_blk)), pl.BlockSpec((tq, ATT_Q_DIM), lambda i: (i, 0))],
        out_specs=[pl.BlockSpec((tq, ATT_HEADS * LANE), lambda i: (i, 0)), _full((t, LANE)), _full((t, LANE))],
        out_shape=[jax.ShapeDtypeStruct((t, ATT_HEADS * LANE), F32), jax.ShapeDtypeStruct((t, LANE), F32),
                   jax.ShapeDtypeStruct((t, LANE), F32)], comm=comm)


def _lower_bounds(a0, a1):
    def body(a0_ref, a1_ref, lb_ref):
        m = jnp.maximum(a0_ref[...], a1_ref[...])
        e0, e1 = jnp.exp(a0_ref[...] - m), jnp.exp(a1_ref[...] - m)
        lb_ref[...] = e0 / (e0 + e1)

    return pl.pallas_call(body, name="lower_bounds", out_shape=jax.ShapeDtypeStruct(a0.shape, F32))(a0, a1)


def _lower_bounds_bwd(a0, a1, dlb):
    def body(a0_ref, a1_ref, dlb_ref, d0_ref, d1_ref):
        m = jnp.maximum(a0_ref[...], a1_ref[...])
        e0, e1 = jnp.exp(a0_ref[...] - m), jnp.exp(a1_ref[...] - m)
        lb = e0 / (e0 + e1)
        d0 = dlb_ref[...] * lb * (1.0 - lb)
        d0_ref[...] = d0
        d1_ref[...] = -d0

    return pl.pallas_call(body, name="lower_bounds_bwd", out_shape=[jax.ShapeDtypeStruct(a0.shape, F32)] * 2)(a0, a1, dlb)


def _chunk_scan(x, pos, down):
    n = x.shape[0]
    s = 1
    while s < HG_CHUNK:
        if down:
            x = x + jnp.where(pos >= s, pltpu.roll(x, s, 0), 0.0)
        else:
            x = x + jnp.where(pos < HG_CHUNK - s, pltpu.roll(x, n - s, 0), 0.0)
        s *= 2
    return x


def _chunk_concat(x, cid, nc):
    return jnp.concatenate([_bf(jnp.where(cid == c, x, 0.0)) for c in range(nc)], axis=1)


def _chunk_pick(x, cid, nc):
    out = jnp.where(cid == 0, x[:, :LANE], 0.0)
    for c in range(1, nc):
        out = out + jnp.where(cid == c, x[:, c * LANE:(c + 1) * LANE], 0.0)
    return out


def _hgrn_gates(hq, z, lb):
    sq = _sigmoid(hq)
    sig = _sigmoid(z)
    f = lb + (1.0 - lb) * sig
    return hq * sq, sq, sig, f, jnp.log(f)


def _hgrn_local(q, f, g, v, pos, cid, amask, rev, nc):
    k = 1.0 - f
    ba = _chunk_scan(g, pos, not rev)
    bb = _chunk_scan(g, pos, rev)
    eq = 0.5 * (ba - bb + g)
    ex_q, ex_k, ex_i, ex_d = jnp.exp(eq), jnp.exp(-eq), jnp.exp(ba), jnp.exp(bb - g)
    qb, kb, qi, kd = q * ex_q, k * ex_k, q * ex_i, k * ex_d
    dvec = jnp.exp(ba + bb - g)
    a = jnp.where(amask, _dot(_bf(qb), _bf(kb), 1, 1), 0.0)
    kd_m = _chunk_concat(kd, cid, nc)
    qi_m = _chunk_concat(qi, cid, nc)
    ut_all = _dot(_bf(v), kd_m, 0, 0)
    return dict(k=k, ex_q=ex_q, ex_k=ex_k, ex_i=ex_i, ex_d=ex_d, qb=qb, kb=kb, qi=qi, kd=kd, dvec=dvec, a=a,
                kd_m=kd_m, qi_m=qi_m, ut_all=ut_all)


def _hgrn_masks(n, rev):
    row = lax.broadcasted_iota(jnp.int32, (n, LANE), 0)
    ti = lax.broadcasted_iota(jnp.int32, (n, n), 0)
    si = lax.broadcasted_iota(jnp.int32, (n, n), 1)
    tri = (si >= ti) if rev else (si <= ti)
    same = jnp.right_shift(ti, HG_CHUNK_LOG2) == jnp.right_shift(si, HG_CHUNK_LOG2)
    return jnp.bitwise_and(row, HG_CHUNK - 1), jnp.right_shift(row, HG_CHUNK_LOG2), same & tri


def _hgrn_specs(t, nb, rev, bwd):
    n = min(HG_BLOCK, t)
    if rev != bwd:
        blk = lambda i: nb - 1 - i
    else:
        blk = lambda i: i
    col = lambda base: pl.BlockSpec((n, 2 * LANE), lambda hp, i: (blk(i), base + hp))
    return n, blk, col


def _hgrn_fwd(p, lb, rev, comm=None):
    t = p.shape[0]
    nb = t // min(HG_BLOCK, t)
    n, blk, col = _hgrn_specs(t, nb, rev, False)
    nc = n // HG_CHUNK
    z_base = 7 if rev else 5

    def body(hq_ref, z_ref, hi_ref, lb_ref, o_ref, ssave_ref, st_scr):
        @pl.when(pl.program_id(1) == 0)
        def _():
            st_scr[...] = jnp.zeros_like(st_scr)

        pos, cid, amask = _hgrn_masks(n, rev)
        order = list(range(nc))[::-1] if rev else list(range(nc))
        for hh in range(2):
            sl = slice(hh * LANE, (hh + 1) * LANE)
            q, _, _, f, g = _hgrn_gates(hq_ref[:, sl], z_ref[:, sl], lb_ref[0:1, sl])
            v = hi_ref[:, sl]
            c_ = _hgrn_local(q, f, g, v, pos, cid, amask, rev, nc)
            st = st_scr[hh]
            ssave_ref[0, sl, :] = st
            cols = [None] * nc
            for c in order:
                cols[c] = st
                st = st * c_["dvec"][c * HG_CHUNK:c * HG_CHUNK + 1, :] + c_["ut_all"][:, c * LANE:(c + 1) * LANE]
            st_scr[hh] = st
            st_all = _bf(jnp.concatenate(cols, axis=1))
            o_ref[:, sl] = _dot(_bf(c_["a"]), _bf(v), 1, 0) + _dot(c_["qi_m"], st_all, 1, 1)

    return _call(
        body, (p, p, p, lb), name="hgrn_fwd_rev" if rev else "hgrn_fwd", grid=(2, nb),
        in_specs=[col(3), col(z_base), col(9), pl.BlockSpec((1, 2 * LANE), lambda hp, i: (0, hp))],
        out_specs=[pl.BlockSpec((n, 2 * LANE), lambda hp, i: (blk(i), hp)),
                   pl.BlockSpec((1, 2 * LANE, LANE), lambda hp, i: (blk(i), hp, 0))],
        out_shape=[jax.ShapeDtypeStruct((t, HG_DIM), F32), jax.ShapeDtypeStruct((nb, HG_DIM, LANE), F32)],
        scratch_shapes=[pltpu.VMEM((2, LANE, LANE), F32)], comm=comm)


def _hgrn_bwd(p, lb, do, ssave, rev, comm=None):
    t = p.shape[0]
    nb = t // min(HG_BLOCK, t)
    n, blk, col = _hgrn_specs(t, nb, rev, True)
    nc = n // HG_CHUNK
    z_base = 7 if rev else 5

    def body(hq_ref, z_ref, hi_ref, lb_ref, do_ref, ssave_ref, dhq_ref, dz_ref, dhi_ref, dlb_ref, dst_scr):
        @pl.when(pl.program_id(1) == 0)
        def _():
            dst_scr[...] = jnp.zeros_like(dst_scr)
            dlb_ref[...] = jnp.zeros_like(dlb_ref)

        pos, cid, amask = _hgrn_masks(n, rev)
        order = list(range(nc))[::-1] if rev else list(range(nc))
        for hh in range(2):
            sl = slice(hh * LANE, (hh + 1) * LANE)
            hq, lbv = hq_ref[:, sl], lb_ref[0:1, sl]
            q, sq, sig, f, g = _hgrn_gates(hq, z_ref[:, sl], lbv)
            v = hi_ref[:, sl]
            c_ = _hgrn_local(q, f, g, v, pos, cid, amask, rev, nc)
            dvec, ut_all = c_["dvec"], c_["ut_all"]
            drow = lambda c: dvec[c * HG_CHUNK:c * HG_CHUNK + 1, :]
            st = ssave_ref[0, sl, :]
            cols = [None] * nc
            for c in order:
                cols[c] = st
                st = st * drow(c) + ut_all[:, c * LANE:(c + 1) * LANE]
            dob, vb = _bf(do_ref[:, sl]), _bf(v)
            gt_all = _dot(dob, c_["qi_m"], 0, 0)
            dcur = dst_scr[hh]
            dnext = [None] * nc
            ddrow = [None] * nc
            for c in order[::-1]:
                dnext[c] = dcur
                ddrow[c] = jnp.sum(cols[c] * dcur, axis=0, keepdims=True) * drow(c)
                dcur = dcur * drow(c) + gt_all[:, c * LANE:(c + 1) * LANE]
            dst_scr[hh] = dcur
            dsn_all = _bf(jnp.concatenate(dnext, axis=1))
            st_all = _bf(jnp.concatenate(cols, axis=1))
            da = _bf(jnp.where(amask, _dot(dob, vb, 1, 1), 0.0))
            dv = _dot(_bf(c_["a"]), dob, 0, 0) + _dot(c_["kd_m"], dsn_all, 1, 1)
            dqb = _dot(da, _bf(c_["kb"]), 1, 0)
            dkb = _dot(da, _bf(c_["qb"]), 0, 0)
            dqi = _chunk_pick(_dot(dob, st_all, 1, 0), cid, nc)
            dkd = _chunk_pick(_dot(vb, dsn_all, 1, 0), cid, nc)
            dq = dqb * c_["ex_q"] + dqi * c_["ex_i"]
            dk = dkb * c_["ex_k"] + dkd * c_["ex_d"]
            e = dqb * c_["qb"] - dkb * c_["kb"] + dqi * c_["qi"]
            w = dkd * c_["kd"]
            dtot = jnp.where(cid == 0, ddrow[0], 0.0)
            for c in range(1, nc):
                dtot = dtot + jnp.where(cid == c, ddrow[c], 0.0)
            dg = _chunk_scan(e, pos, rev) + (_chunk_scan(w, pos, not rev) - w) + dtot
            df = dg / f - dk
            dz_ref[:, sl] = df * (1.0 - lbv) * sig * (1.0 - sig)
            dlb_ref[0:1, sl] += jnp.sum(df * (1.0 - sig), axis=0, keepdims=True)
            dhq_ref[:, sl] = dq * (sq * (1.0 + hq * (1.0 - sq)))
            dhi_ref[:, sl] = dv

    out = pl.BlockSpec((n, 2 * LANE), lambda hp, i: (blk(i), hp))
    return _call(
        body, (p, p, p, lb, do, ssave), name="hgrn_bwd_rev" if rev else "hgrn_bwd", grid=(2, nb),
        in_specs=[col(3), col(z_base), col(9), pl.BlockSpec((1, 2 * LANE), lambda hp, i: (0, hp)), out,
                  pl.BlockSpec((1, 2 * LANE, LANE), lambda hp, i: (blk(i), hp, 0))],
        out_specs=[out, out, out, pl.BlockSpec((1, 2 * LANE), lambda hp, i: (0, hp))],
        out_shape=[jax.ShapeDtypeStruct((t, HG_DIM), F32)] * 3 + [jax.ShapeDtypeStruct((1, HG_DIM), F32)],
        scratch_shapes=[pltpu.VMEM((2, LANE, LANE), F32)], comm=comm)


def _mix_out(att, of, ob, p, gout, w_out, g_post, x, comm=None, tm=256):
    t, d = x.shape
    tm = min(tm, t)

    def body(att_ref, of_ref, ob_ref, hg0_ref, hg1_ref, gout_ref, w_ref, g_ref, x_ref, cat_ref, y_ref, x1_ref):
        cat_ref[:, :ATT_Q_DIM] = att_ref[...]
        for h in range(HG_HEADS):
            sl = slice(h * LANE, (h + 1) * LANE)
            hg_ref = hg0_ref if h < 2 else hg1_ref
            hg = hg_ref[:, (h % 2) * LANE:(h % 2 + 1) * LANE]
            nrm = _rms_fwd(of_ref[:, sl] + ob_ref[:, sl], gout_ref[...])
            cat_ref[:, ATT_Q_DIM + h * LANE:ATT_Q_DIM + (h + 1) * LANE] = _bf(nrm * hg * _sigmoid(hg))
        y = _dot(cat_ref[...], w_ref[...], 1, 0)
        y_ref[...] = y
        x1_ref[...] = x_ref[...] + _rms_fwd(y, g_ref[...])

    row = lambda width: pl.BlockSpec((tm, width), lambda i: (i, 0))
    return _call(
        body, (att, of, ob, p, p, gout, w_out, g_post, x), name="mix_out", grid=(t // tm,),
        in_specs=[row(ATT_Q_DIM), row(HG_DIM), row(HG_DIM), pl.BlockSpec((tm, 2 * LANE), lambda i: (i, 11)),
                  pl.BlockSpec((tm, 2 * LANE), lambda i: (i, 12)), _full((1, LANE)), _full((d, d)), _full((1, d)), row(d)],
        out_specs=[row(d), row(d), row(d)],
        out_shape=[jax.ShapeDtypeStruct((t, d), BF16), jax.ShapeDtypeStruct((t, d), F32), jax.ShapeDtypeStruct((t, d), F32)],
        comm=comm)


def _rec_bwd(dcat, of, ob, p, gout, tm=256):
    t = of.shape[0]
    tm = min(tm, t)

    def body(dc_ref, of_ref, ob_ref, hg0_ref, hg1_ref, gout_ref, do_ref, dhg_ref, dgo_ref):
        @pl.when(pl.program_id(0) == 0)
        def _():
            dgo_ref[...] = jnp.zeros_like(dgo_ref)

        dgo = jnp.zeros((1, LANE), F32)
        for h in range(HG_HEADS):
            sl = slice(h * LANE, (h + 1) * LANE)
            hg_ref = hg0_ref if h < 2 else hg1_ref
            hg = hg_ref[:, (h % 2) * LANE:(h % 2 + 1) * LANE]
            o = of_ref[:, sl] + ob_ref[:, sl]
            sg = _sigmoid(hg)
            drec = dc_ref[:, sl]
            dhg_ref[:, sl] = drec * _rms_fwd(o, gout_ref[...]) * (sg * (1.0 + hg * (1.0 - sg)))
            do, dg = _rms_bwd(drec * hg * sg, o, gout_ref[...])
            do_ref[:, sl] = do
            dgo = dgo + dg
        dgo_ref[...] += dgo

    row = lambda width: pl.BlockSpec((tm, width), lambda i: (i, 0))
    return pl.pallas_call(
        body, name="rec_bwd", grid=(t // tm,),
        in_specs=[pl.BlockSpec((tm, HG_DIM), lambda i: (i, 1)), row(HG_DIM), row(HG_DIM),
                  pl.BlockSpec((tm, 2 * LANE), lambda i: (i, 11)), pl.BlockSpec((tm, 2 * LANE), lambda i: (i, 12)),
                  _full((1, LANE))],
        out_specs=[row(HG_DIM), row(HG_DIM), _full((1, LANE))],
        out_shape=[jax.ShapeDtypeStruct((t, HG_DIM), F32), jax.ShapeDtypeStruct((t, HG_DIM), F32),
                   jax.ShapeDtypeStruct((1, LANE), F32)],
        compiler_params=_params(),
    )(dcat, of, ob, p, p, gout)


def _assemble_dp(dp_qk, dv, dhq_f, dhq_b, dz_f, dz_b, dhi_f, dhi_b, dhg, tm=512):
    t = dv.shape[0]
    tm = min(tm, t)
    qk_w = ATT_Q_DIM + LANE

    def body(qk_ref, dv_ref, hqf_ref, hqb_ref, zf_ref, zb_ref, hif_ref, hib_ref, hg_ref, dp_ref):
        o = 0
        dp_ref[:, o:o + qk_w] = qk_ref[...]
        o += qk_w
        dp_ref[:, o:o + LANE] = _bf(dv_ref[...])
        o += LANE
        for val in (hqf_ref[...] + hqb_ref[...], zf_ref[...], zb_ref[...], hif_ref[...] + hib_ref[...], hg_ref[...]):
            dp_ref[:, o:o + HG_DIM] = _bf(val)
            o += HG_DIM

    row = lambda width: pl.BlockSpec((tm, width), lambda i: (i, 0))
    return pl.pallas_call(
        body, name="assemble_dp", grid=(t // tm,),
        in_specs=[row(qk_w), row(LANE)] + [row(HG_DIM)] * 7,
        out_specs=row(N_IN),
        out_shape=jax.ShapeDtypeStruct((t, N_IN), BF16),
        compiler_params=_params(),
    )(dp_qk, dv, dhq_f, dhq_b, dz_f, dz_b, dhi_f, dhi_b, dhg)


def _xattn_fwd(q, kv, tq=512):
    t, d = q.shape
    tq = min(tq, t)
    nm = kv.shape[0]

    def body(q_ref, kv_ref, o_ref):
        for h in range(X_HEADS):
            sl = slice(h * X_HEAD_DIM, (h + 1) * X_HEAD_DIM)
            s = _dot(_bf(q_ref[:, sl]), _bf(kv_ref[:, sl]), 1, 1) * (X_HEAD_DIM ** -0.5)
            e = jnp.exp(s - jnp.max(s, axis=-1, keepdims=True))
            pr = e * (1.0 / jnp.sum(e, axis=-1, keepdims=True))
            o_ref[:, sl] = _bf(_dot(_bf(pr), _bf(kv_ref[:, d + h * X_HEAD_DIM:d + (h + 1) * X_HEAD_DIM]), 1, 0))

    return pl.pallas_call(
        body, name="xattn_fwd", grid=(t // tq,),
        in_specs=[pl.BlockSpec((tq, d), lambda i: (i, 0)), _full((nm, 2 * d))],
        out_specs=pl.BlockSpec((tq, d), lambda i: (i, 0)),
        out_shape=jax.ShapeDtypeStruct((t, d), BF16),
        compiler_params=_params(),
    )(q, kv)


def _xattn_bwd(q, kv, do, tq=512):
    t, d = q.shape
    tq = min(tq, t)
    nm = kv.shape[0]

    def body(q_ref, kv_ref, do_ref, dq_ref, dkv_ref):
        @pl.when(pl.program_id(0) == 0)
        def _():
            dkv_ref[...] = jnp.zeros_like(dkv_ref)

        for h in range(X_HEADS):
            sl = slice(h * X_HEAD_DIM, (h + 1) * X_HEAD_DIM)
            slv = slice(d + h * X_HEAD_DIM, d + (h + 1) * X_HEAD_DIM)
            qb, kb, vb, dob = _bf(q_ref[:, sl]), _bf(kv_ref[:, sl]), _bf(kv_ref[:, slv]), _bf(do_ref[:, sl])
            s = _dot(qb, kb, 1, 1) * (X_HEAD_DIM ** -0.5)
            e = jnp.exp(s - jnp.max(s, axis=-1, keepdims=True))
            pr = e * (1.0 / jnp.sum(e, axis=-1, keepdims=True))
            dpr = _dot(dob, vb, 1, 1)
            ds = _bf(pr * (dpr - jnp.sum(pr * dpr, axis=-1, keepdims=True)) * (X_HEAD_DIM ** -0.5))
            dq_ref[:, sl] = _bf(_dot(ds, kb, 1, 0))
            dkv_ref[:, sl] += _dot(ds, qb, 0, 0)
            dkv_ref[:, slv] += _dot(_bf(pr), dob, 0, 0)

    return pl.pallas_call(
        body, name="xattn_bwd", grid=(t // tq,),
        in_specs=[pl.BlockSpec((tq, d), lambda i: (i, 0)), _full((nm, 2 * d)), pl.BlockSpec((tq, d), lambda i: (i, 0))],
        out_specs=[pl.BlockSpec((tq, d), lambda i: (i, 0)), _full((nm, 2 * d))],
        out_shape=[jax.ShapeDtypeStruct((t, d), BF16), jax.ShapeDtypeStruct((nm, 2 * d), F32)],
        compiler_params=_params(),
    )(q, kv, do)


CONV_TN = 256


def _shift_rows(u, row, t, delta):
    if delta < 0:
        return jnp.where(row == 0, 0.0, pltpu.roll(u, 1, 0))
    return jnp.where(row == t - 1, 0.0, pltpu.roll(u, t - 1, 0))


def _conv_gate_fwd(u, conv_w, conv_b):
    t = u.shape[0]
    nt = D_FF // CONV_TN

    def body(ug_ref, uv_ref, wg_ref, wv_ref, bg_ref, bv_ref, a_ref):
        row = lax.broadcasted_iota(jnp.int32, (t, CONV_TN), 0)

        def conv(u_ref, w_ref, b_ref):
            uu = u_ref[...]
            return (b_ref[...] + _shift_rows(uu, row, t, -1) * w_ref[0:1, :] + uu * w_ref[1:2, :]
                    + _shift_rows(uu, row, t, 1) * w_ref[2:3, :])

        gate = conv(ug_ref, wg_ref, bg_ref)
        a_ref[...] = _bf(gate * _sigmoid(gate) * conv(uv_ref, wv_ref, bv_ref))

    col = lambda rows, off: pl.BlockSpec((rows, CONV_TN), lambda j: (0, j + off))
    return pl.pallas_call(
        body, name="conv_gate_fwd", grid=(nt,),
        in_specs=[col(t, 0), col(t, nt), col(3, 0), col(3, nt), col(1, 0), col(1, nt)],
        out_specs=col(t, 0),
        out_shape=jax.ShapeDtypeStruct((t, D_FF), BF16),
        compiler_params=_params(),
    )(u, u, conv_w, conv_w, conv_b, conv_b)


def _conv_gate_bwd(u, conv_w, conv_b, da, comm=None):
    t = u.shape[0]
    nt = D_FF // CONV_TN

    def body(ug_ref, uv_ref, wg_ref, wv_ref, bg_ref, bv_ref, da_ref, dug_ref, dwg_ref, dbg_ref, duv_ref, dwv_ref, dbv_ref):
        row = lax.broadcasted_iota(jnp.int32, (t, CONV_TN), 0)
        ug, uv = ug_ref[...], uv_ref[...]
        ug_m, ug_p = _shift_rows(ug, row, t, -1), _shift_rows(ug, row, t, 1)
        uv_m, uv_p = _shift_rows(uv, row, t, -1), _shift_rows(uv, row, t, 1)
        gate = bg_ref[...] + ug_m * wg_ref[0:1, :] + ug * wg_ref[1:2, :] + ug_p * wg_ref[2:3, :]
        val = bv_ref[...] + uv_m * wv_ref[0:1, :] + uv * wv_ref[1:2, :] + uv_p * wv_ref[2:3, :]
        sg = _sigmoid(gate)
        da_ = da_ref[...]

        def emit(dacc, um, uu, up, w_ref, du_ref, dw_ref, db_ref):
            du_ref[...] = _bf(_shift_rows(dacc, row, t, 1) * w_ref[0:1, :] + dacc * w_ref[1:2, :]
                              + _shift_rows(dacc, row, t, -1) * w_ref[2:3, :])
            dw_ref[0:1, :] = jnp.sum(dacc * um, axis=0, keepdims=True)
            dw_ref[1:2, :] = jnp.sum(dacc * uu, axis=0, keepdims=True)
            dw_ref[2:3, :] = jnp.sum(dacc * up, axis=0, keepdims=True)
            db_ref[...] = jnp.sum(dacc, axis=0, keepdims=True)

        emit(da_ * val * (sg * (1.0 + gate * (1.0 - sg))), ug_m, ug, ug_p, wg_ref, dug_ref, dwg_ref, dbg_ref)
        emit(da_ * gate * sg, uv_m, uv, uv_p, wv_ref, duv_ref, dwv_ref, dbv_ref)

    col = lambda rows, off: pl.BlockSpec((rows, CONV_TN), lambda j: (0, j + off))
    half_shapes = [jax.ShapeDtypeStruct((t, D_FF), BF16), jax.ShapeDtypeStruct((3, D_FF), F32),
                   jax.ShapeDtypeStruct((1, D_FF), F32)]
    outs, got = _call(
        body, (u, u, conv_w, conv_w, conv_b, conv_b, da), name="conv_gate_bwd", grid=(nt,),
        in_specs=[col(t, 0), col(t, nt), col(3, 0), col(3, nt), col(1, 0), col(1, nt), col(t, 0)],
        out_specs=[col(t, 0), col(3, 0), col(1, 0)] * 2, out_shape=half_shapes * 2, comm=comm)
    return outs[:3], outs[3:], got


def _sum_parts(parts, name, tr=128):
    _, r, c = parts.shape
    tr = tr if r % tr == 0 else r

    def body(p_ref, o_ref):
        acc = p_ref[0].astype(F32)
        for i in range(1, N_DEV):
            acc = acc + p_ref[i].astype(F32)
        o_ref[...] = acc

    return pl.pallas_call(
        body, name=name, grid=(r // tr,),
        in_specs=[pl.BlockSpec((N_DEV, tr, c), lambda i: (0, i, 0))],
        out_specs=pl.BlockSpec((tr, c), lambda i: (i, 0)),
        out_shape=jax.ShapeDtypeStruct((r, c), F32),
        compiler_params=_params(),
    )(parts)


def _adamw(w, g, m, v, name, tr=128):
    r, c = w.shape
    tr = tr if r % tr == 0 else r

    def body(w_ref, g_ref, m_ref, v_ref, d_ref, nm_ref, nv_ref):
        g_ = g_ref[...]
        m_ = ADAM_B1 * m_ref[...] + (1.0 - ADAM_B1) * g_
        v_ = ADAM_B2 * v_ref[...] + (1.0 - ADAM_B2) * (g_ * g_)
        m_hat = m_ / (1.0 - ADAM_B1 ** ADAM_STEP)
        v_hat = v_ / (1.0 - ADAM_B2 ** ADAM_STEP)
        d_ref[...] = -ADAM_LR * (m_hat / (jnp.sqrt(v_hat) + ADAM_EPS) + ADAM_WD * w_ref[...])
        nm_ref[...] = m_
        nv_ref[...] = v_

    blk = pl.BlockSpec((tr, c), lambda i: (i, 0))
    return pl.pallas_call(
        body, name=name, grid=(r // tr,),
        in_specs=[blk] * 4, out_specs=[blk] * 3,
        out_shape=[jax.ShapeDtypeStruct((r, c), F32)] * 3,
        compiler_params=_params(),
    )(w, g, m, v)


def _mesh_pos():
    return lax.axis_index("x"), lax.axis_index("y"), lax.axis_index("c")


def _dev_index(px, py, pc):
    return 4 * px + 2 * py + pc


class _Gather:
    def __init__(self, arrs):
        self.arrs = list(arrs)
        n = len(self.arrs)
        self.out_shape = [jax.ShapeDtypeStruct((N_DEV,) + a.shape, a.dtype) for a in self.arrs]
        self.sems = [pltpu.SemaphoreType.DMA((7, n)), pltpu.SemaphoreType.DMA((7, n)), pltpu.SemaphoreType.DMA((n,))]

    def _ctx(self, ins, outs, sems):
        send_sems, recv_sems, local_sems = sems
        x, y, c = _mesh_pos()
        chips = [(1 - x, y), (x, 1 - y), (1 - x, 1 - y)]

        def copy(k, a, block, to, src=None):
            dst = outs[a].at[_dev_index(*block)]
            return pltpu.make_async_remote_copy(
                src_ref=dst if src is None else src, dst_ref=dst, send_sem=send_sems.at[k, a], recv_sem=recv_sems.at[k, a],
                device_id=to, device_id_type=MESH_T)

        n = len(ins)
        me, sibling = (x, y, c), (x, y, 1 - c)
        mine = [pltpu.make_async_copy(ins[a], outs[a].at[_dev_index(*me)], local_sems.at[a]) for a in range(n)]
        first = [copy(0, a, me, sibling, src=ins[a]) for a in range(n)]
        first += [copy(1 + j, a, me, (*chip, c), src=ins[a]) for j, chip in enumerate(chips) for a in range(n)]
        passed = [copy(4 + j, a, (*chip, c), sibling) for j, chip in enumerate(chips) for a in range(n)]
        return n, c, me, sibling, chips, copy, mine, first, passed

    def start(self, ins, outs, sems):
        _, _, _, _, _, _, mine, first, _ = self._ctx(ins, outs, sems)
        for cp in mine + first:
            cp.start()

    def forward(self, ins, outs, sems):
        n, c, me, _, chips, copy, _, _, passed = self._ctx(ins, outs, sems)
        for j, chip in enumerate(chips):
            for a in range(n):
                copy(1 + j, a, (*chip, c), me).wait_recv()
                passed[j * n + a].start()

    def finish(self, ins, outs, sems):
        n, c, me, sibling, chips, copy, mine, first, passed = self._ctx(ins, outs, sems)
        for a in range(n):
            copy(0, a, sibling, me).wait_recv()
        for j, chip in enumerate(chips):
            for a in range(n):
                copy(4 + j, a, (*chip, 1 - c), me).wait_recv()
        for cp in first + passed:
            cp.wait_send()
        for cp in mine:
            cp.wait()


class _Exchange:
    def __init__(self, arrs):
        self.arrs = list(arrs)
        n = len(self.arrs)
        self.out_shape = [jax.ShapeDtypeStruct(a.shape, a.dtype) for a in self.arrs]
        self.sems = [pltpu.SemaphoreType.DMA((7, n)), pltpu.SemaphoreType.DMA((7, n)), pltpu.SemaphoreType.DMA((n,))]

    def _copies(self, ins, outs, sems):
        send_sems, recv_sems, local_sems = sems
        x, y, c = _mesh_pos()
        me_i = _dev_index(x, y, c)
        n = len(ins)
        mine = [pltpu.make_async_copy(ins[a].at[me_i], outs[a].at[me_i], local_sems.at[a]) for a in range(n)]
        copies = []
        for k in range(1, N_DEV):
            px = 1 - x if k & 4 else x
            py = 1 - y if k & 2 else y
            pc = 1 - c if k & 1 else c
            peer_i = _dev_index(px, py, pc)
            for a in range(n):
                copies.append(pltpu.make_async_remote_copy(
                    src_ref=ins[a].at[peer_i], dst_ref=outs[a].at[me_i], send_sem=send_sems.at[k - 1, a],
                    recv_sem=recv_sems.at[k - 1, a], device_id=(px, py, pc), device_id_type=MESH_T))
        return mine, copies

    def start(self, ins, outs, sems):
        mine, copies = self._copies(ins, outs, sems)
        for cp in mine + copies:
            cp.start()

    forward = None

    def finish(self, ins, outs, sems):
        mine, copies = self._copies(ins, outs, sems)
        for cp in copies:
            cp.wait_recv()
        for cp in copies:
            cp.wait_send()
        for cp in mine:
            cp.wait()


def _comm_alone(comm, name):
    n = len(comm.arrs)

    def body(*refs):
        ins, outs, sems = refs[:n], refs[n:2 * n], refs[2 * n:]
        comm.start(ins, outs, sems)
        if comm.forward is not None:
            comm.forward(ins, outs, sems)
        comm.finish(ins, outs, sems)

    any_spec = pl.BlockSpec(memory_space=pl.ANY)
    return pl.pallas_call(body, name=name, in_specs=[any_spec] * n, out_specs=[any_spec] * n, out_shape=comm.out_shape,
                          scratch_shapes=comm.sems)(*comm.arrs)


def _call(body, operands, *, name, grid, in_specs, out_specs, out_shape, scratch_shapes=(), comm=None):
    if comm is None:
        outs = pl.pallas_call(body, name=name, grid=grid, in_specs=in_specs, out_specs=out_specs, out_shape=out_shape,
                              scratch_shapes=list(scratch_shapes), compiler_params=_params())(*operands)
        return list(outs), []
    ni, no, ns = len(in_specs), len(out_specs), len(scratch_shapes)
    ci, co = len(comm.arrs), len(comm.out_shape)
    total = 1
    for gdim in grid:
        total *= gdim

    def hosted(*refs):
        o0 = ni + ci
        s0 = o0 + no + co
        ins, cins = refs[:ni], refs[ni:o0]
        outs, couts = refs[o0:o0 + no], refs[o0 + no:s0]
        scr, csems = refs[s0:s0 + ns], refs[s0 + ns:]
        step = 0
        for axis, gdim in enumerate(grid):
            step = step * gdim + pl.program_id(axis)

        @pl.when(step == 0)
        def _():
            comm.start(cins, couts, csems)

        body(*ins, *outs, *scr)

        @pl.when(step == total - 1)
        def _():
            if comm.forward is not None:
                comm.forward(cins, couts, csems)
            comm.finish(cins, couts, csems)

    any_spec = pl.BlockSpec(memory_space=pl.ANY)
    res = pl.pallas_call(
        hosted, name=name, grid=grid, in_specs=list(in_specs) + [any_spec] * ci, out_specs=list(out_specs) + [any_spec] * co,
        out_shape=list(out_shape) + comm.out_shape, scratch_shapes=list(scratch_shapes) + comm.sems,
        compiler_params=_params())(*operands, *comm.arrs)
    return list(res[:no]), list(res[no:])


def _rows(a):
    return a.reshape(-1, a.shape[-1])


def _slots(a):
    return a.reshape(N_DEV, -1, a.shape[-1])


def _local_step(x, mem, tgt, w, wire):
    tables = _rope_tables(x.shape[0])
    bd = _head_sum_matrix()
    two = lambda g: jnp.tile(g, (1, 2))
    gq2, gk2 = two(w["q_norm_g"]), two(w["k_norm_g"])

    w_in_t, hg_lb, conv_w = _comm_alone(_Gather([wire["w_in"], w["hg_lb"].reshape(4, -1), w["conv_w"][0]]), "gather_w_in")
    w_in_t = _rows(w_in_t)
    hg_lb = jnp.transpose(hg_lb.reshape(N_DEV, 2, 2, -1), (1, 2, 0, 3)).reshape(2, 2, HG_DIM)
    conv_w = jnp.transpose(conv_w, (1, 0, 2)).reshape(3, 2 * D_FF)
    lb_a0, lb_a1 = hg_lb[:, 0, :], hg_lb[:, 1, :]

    h1, p = _norm_mm(x, w["pre_mix_g"], w_in_t, True, 1664, "in_proj")
    qr, kr = _qk_prep(p, gq2, gk2, tables, bd)
    att, (w_up_t,) = _attn_fwd(qr, kr, p, _Gather([wire["w_up"]]))
    lb = _lower_bounds(lb_a0, lb_a1)
    (of, s_f), (w_out, w_xq) = _hgrn_fwd(p, lb[0:1], False, _Gather([wire["w_out"], wire["w_xq"]]))
    (ob, s_b), (w_xkv_t, w_xo) = _hgrn_fwd(p, lb[1:2], True, _Gather([wire["w_xkv"], wire["w_xo"]]))
    w_up_t, w_out, w_xq, w_xkv_t, w_xo = _rows(w_up_t), _rows(w_out), _rows(w_xq), _rows(w_xkv_t), _rows(w_xo)
    (cat, mixed, x1), (w_down,) = _mix_out(att, of, ob, p, w["hg_out_norm_g"], w_out, w["post_mix_g"], x,
                                           _Gather([wire["w_down"]]))
    w_down = _rows(w_down)

    h2, q2 = _norm_mm(x1, w["pre_x_g"], w_xq, False, 1024, "xq_proj")
    mn, kv = _norm_mm(mem, w["mem_norm_g"], w_xkv_t, True, 1024, "xkv_proj")
    o2 = _xattn_fwd(q2, kv)
    y2, x2 = _mm_postnorm_res(o2, w_xo, w["post_x_g"], x1, "xo_proj")

    h3, u = _norm_mm(x2, w["pre_ffn_g"], w_up_t, True, 1408, "up_proj")
    a = _conv_gate_fwd(u, conv_w, w["conv_b"])
    y3, dx3, loss = _mm_postnorm_res_loss(a, w_down, w["post_ffn_g"], x2, tgt, "down_proj")

    g, parts = {}, {}
    dy3, da, g["post_ffn_g"] = _postnorm_bwd_mm(dx3, y3, w["post_ffn_g"], w_down, "down_bwd")
    dw_down = _dw(a, dy3, "dw_down")
    (du_g, dcw_g, dcb_g), (du_v, dcw_v, dcb_v), (parts["w_down"],) = _conv_gate_bwd(
        u, conv_w, w["conv_b"], da, _Exchange([_slots(dw_down)]))
    g["conv_w"] = jnp.concatenate([dcw_g, dcw_v], axis=1)
    g["conv_b"] = jnp.concatenate([dcb_g, dcb_v], axis=1)
    dw_up = _dw([du_g, du_v], h3, "dw_up")
    (dx2, g["pre_ffn_g"]), _ = _mm_prenorm_bwd([du_g, du_v], w_up_t, True, x2, w["pre_ffn_g"], dx3, 1408, "up_bwd")

    dy2, do2, g["post_x_g"] = _postnorm_bwd_mm(dx2, y2, w["post_x_g"], w_xo, "xo_bwd")
    dw_xo = _dw(o2, dy2, "dw_xo")
    dq2, dkv = _xattn_bwd(q2, kv, do2)
    dw_xq = _dw(h2, dq2, "dw_xq")
    dkvb = _bf(dkv)
    dw_xkv = _dw(dkvb, mn, "dw_xkv")
    (_, g["mem_norm_g"]), _ = _mm_prenorm_bwd(dkvb, w_xkv_t, True, mem, w["mem_norm_g"], jnp.zeros_like(mem), 2048, "xkv_bwd")
    (dx1, g["pre_x_g"]), _ = _mm_prenorm_bwd(dq2, w_xq, False, x1, w["pre_x_g"], dx2, 1024, "xq_bwd")

    dmixed, dcat, g["post_mix_g"] = _postnorm_bwd_mm(dx1, mixed, w["post_mix_g"], w_out, "out_bwd")
    dw_out = _dw(cat, dmixed, "dw_out")
    do, dhg, g["hg_out_norm_g"] = _rec_bwd(dcat, of, ob, p, w["hg_out_norm_g"])
    (dhq_f, dz_f, dhi_f, dlb_f), (parts["w_xo"], parts["w_xq"]) = _hgrn_bwd(
        p, lb[0:1], do, s_f, False, _Exchange([_slots(dw_xo), _slots(dw_xq)]))
    (dhq_b, dz_b, dhi_b, dlb_b), (parts["w_xkv"], parts["w_out"]) = _hgrn_bwd(
        p, lb[1:2], do, s_b, True, _Exchange([_slots(dw_xkv), _slots(dw_out)]))
    d_a0, d_a1 = _lower_bounds_bwd(lb_a0, lb_a1, jnp.concatenate([dlb_f, dlb_b], axis=0))
    g["hg_lb"] = jnp.stack([d_a0, d_a1], axis=1)
    (dqr, dkr, dv), (parts["w_up"],) = _attn_bwd(qr, kr, p, dcat, _Exchange([_slots(dw_up)]))
    dp_qk, dgq, dgk = _qk_prep_bwd(p, dqr, dkr, gq2, gk2, tables, bd)
    g["q_norm_g"], g["k_norm_g"] = dgq, dgk
    dp = _assemble_dp(dp_qk, dv, dhq_f, dhq_b, dz_f, dz_b, dhi_f, dhi_b, dhg)
    dw_in = _dw(dp, h1, "dw_in")
    (dx, g["pre_mix_g"]), (parts["w_in"],) = _mm_prenorm_bwd(dp, w_in_t, True, x, w["pre_mix_g"], dx1, 1664, "in_bwd",
                                                             _Exchange([_slots(dw_in)]))
    return loss, dx, g, parts


_COL_SHARDED = ("w_in", "w_xkv", "w_up")
_ROW_SHARDED = ("w_out", "w_xq", "w_xo", "w_down")
_REPLICATED = ("pre_mix_g", "q_norm_g", "k_norm_g", "hg_out_norm_g", "post_mix_g", "pre_x_g", "mem_norm_g", "post_x_g",
               "pre_ffn_g", "conv_b", "post_ffn_g")
_WEIGHTS = ("pre_mix_g", "w_in", "q_norm_g", "k_norm_g", "hg_lb", "hg_out_norm_g", "w_out", "post_mix_g", "pre_x_g",
            "mem_norm_g", "w_xq", "w_xkv", "w_xo", "post_x_g", "pre_ffn_g", "w_up", "conv_w", "conv_b", "w_down",
            "post_ffn_g")
PACK_W = 1024


def _small_plan(shapes):
    plan, r = [], 0
    for vi, (rows, cols) in enumerate(shapes):
        for i in range(rows):
            for c0 in range(0, cols, PACK_W):
                plan.append((vi, i, c0, min(PACK_W, cols - c0), r))
                r += 1
    return plan, -(-r // 8) * 8


def _pack_small(vals):
    plan, nrows = _small_plan([val.shape for val in vals])

    def body(*refs):
        ins, out = refs[:-1], refs[-1]
        out[...] = jnp.zeros_like(out)
        for vi, i, c0, width, r in plan:
            out[r:r + 1, 0:width] = ins[vi][i:i + 1, c0:c0 + width]

    return pl.pallas_call(body, name="pack_small", out_shape=jax.ShapeDtypeStruct((nrows, PACK_W), F32))(*vals)


def _sum_unpack_small(packs, shapes):
    plan, _ = _small_plan(shapes)

    def body(*refs):
        p_ref, outs = refs[0], refs[1:]
        acc = p_ref[0]
        for i in range(1, N_DEV):
            acc = acc + p_ref[i]
        for vi, i, c0, width, r in plan:
            outs[vi][i:i + 1, c0:c0 + width] = acc[r:r + 1, 0:width]

    return pl.pallas_call(body, name="sum_unpack_small", out_shape=[jax.ShapeDtypeStruct(s, F32) for s in shapes])(packs)


def _adamw_many(ws, gs, ms, vs):
    n = len(ws)

    def body(*refs):
        w_refs, g_refs, m_refs, v_refs = (refs[k * n:(k + 1) * n] for k in range(4))
        d_refs, nm_refs, nv_refs = (refs[(4 + k) * n:(5 + k) * n] for k in range(3))
        for k in range(n):
            g_ = g_refs[k][...]
            m_ = ADAM_B1 * m_refs[k][...] + (1.0 - ADAM_B1) * g_
            v_ = ADAM_B2 * v_refs[k][...] + (1.0 - ADAM_B2) * (g_ * g_)
            m_hat = m_ / (1.0 - ADAM_B1 ** ADAM_STEP)
            v_hat = v_ / (1.0 - ADAM_B2 ** ADAM_STEP)
            d_refs[k][...] = -ADAM_LR * (m_hat / (jnp.sqrt(v_hat) + ADAM_EPS) + ADAM_WD * w_refs[k][...])
            nm_refs[k][...] = m_
            nv_refs[k][...] = v_

    shapes = [jax.ShapeDtypeStruct(a.shape, F32) for a in ws]
    outs = pl.pallas_call(body, name="adamw_small", out_shape=shapes * 3)(*ws, *gs, *ms, *vs)
    return outs[:n], outs[n:2 * n], outs[2 * n:]


def kernel(x, mem, pre_mix_g, w_in, q_norm_g, k_norm_g, hg_lb, hg_out_norm_g, w_out, post_mix_g, pre_x_g, mem_norm_g, w_xq, w_xkv, w_xo, post_x_g, pre_ffn_g, w_up, conv_w, conv_b, w_down, post_ffn_g, loss_target, m_pre_mix_g, m_w_in, m_q_norm_g, m_k_norm_g, m_hg_lb, m_hg_out_norm_g, m_w_out, m_post_mix_g, m_pre_x_g, m_mem_norm_g, m_w_xq, m_w_xkv, m_w_xo, m_post_x_g, m_pre_ffn_g, m_w_up, m_conv_w, m_conv_b, m_w_down, m_post_ffn_g, v_pre_mix_g, v_w_in, v_q_norm_g, v_k_norm_g, v_hg_lb, v_hg_out_norm_g, v_w_out, v_post_mix_g, v_pre_x_g, v_mem_norm_g, v_w_xq, v_w_xkv, v_w_xo, v_post_x_g, v_pre_ffn_g, v_w_up, v_conv_w, v_conv_b, v_w_down, v_post_ffn_g):
    args = dict(locals())
    w = {n: args[n] for n in _WEIGHTS}
    m = {n: args["m_" + n] for n in _WEIGHTS}
    v = {n: args["v_" + n] for n in _WEIGHTS}
    me = _dev_index(*_mesh_pos())

    wire = {n: _bf(w[n][0].T) for n in _COL_SHARDED}
    wire.update({n: _bf(w[n][0]) for n in _ROW_SHARDED})

    loss, grad_x, g, parts = _local_step(x[0], mem[0], loss_target[0], w, wire)

    grads = {}
    for n in _COL_SHARDED:
        grads[n] = _sum_parts(parts[n], "sum_" + n).T[None]
    for n in _ROW_SHARDED:
        grads[n] = _sum_parts(parts[n], "sum_" + n)[None]

    small = list(_REPLICATED) + ["hg_lb", "conv_w"]
    vals = [g[n] for n in _REPLICATED] + [g["hg_lb"].reshape(4, HG_DIM), g["conv_w"], jnp.pad(loss, ((0, 0), (0, LANE - 1)))]
    shapes = [val.shape for val in vals]
    (packs,) = _comm_alone(_Gather([_pack_small(vals)]), "gather_small_grads")
    summed = _sum_unpack_small(packs, shapes)
    loss = summed[-1][0, 0]
    for n, s in zip(small, summed[:-1]):
        grads[n] = s
    fold = lambda v2: v2[:, :ATT_HEAD_DIM] + v2[:, ATT_HEAD_DIM:]
    grads["q_norm_g"], grads["k_norm_g"] = fold(grads["q_norm_g"]), fold(grads["k_norm_g"])
    grads["hg_lb"] = lax.dynamic_slice_in_dim(grads["hg_lb"].reshape(2, 2, HG_DIM), me * (HG_DIM // N_DEV),
                                              HG_DIM // N_DEV, axis=2)
    grads["conv_w"] = lax.dynamic_slice_in_dim(grads["conv_w"], me * (2 * D_FF // N_DEV), 2 * D_FF // N_DEV, axis=1)[None]

    delta, new_m, new_v = {}, {}, {}
    for n in _COL_SHARDED + _ROW_SHARDED:
        shp = w[n].shape
        flat = lambda a: a.reshape(-1, shp[-1])
        d_, m_, v_ = _adamw(flat(w[n]), flat(grads[n]), flat(m[n]), flat(v[n]), "adamw_" + n)
        delta[n], new_m[n], new_v[n] = d_.reshape(shp), m_.reshape(shp), v_.reshape(shp)
    flat2 = lambda a: a.reshape(-1, a.shape[-1])
    outs = _adamw_many(*[[flat2(d[n]) for n in small] for d in (w, grads, m, v)])
    for dst, vals in zip((delta, new_m, new_v), outs):
        for n, val in zip(small, vals):
            dst[n] = val.reshape(w[n].shape)

    return (loss, grad_x[None], *[grads[n] for n in _WEIGHTS], *[delta[n] for n in _WEIGHTS],
            *[new_m[n] for n in _WEIGHTS], *[new_v[n] for n in _WEIGHTS])
```

```python
import jax
import jax.numpy as jnp
from jax import lax
from jax.experimental import pallas as pl
from jax.experimental.pallas import tpu as pltpu

F32 = jnp.float32
BF16 = jnp.bfloat16

D_MODEL = 1024
GRID_W = 64
EPS = 1e-6
ATT_HEADS = 8
ATT_HEAD_DIM = 64
ATT_Q_DIM = 512
ATT_KV_DIM = 128
ROPE_THETA = 10000.0
HG_HEADS = 4
HG_DIM = 512
HG_CHUNK = 32
HG_CHUNK_LOG2 = 5
HG_BLOCK = 256
N_IN = 3328
X_HEADS = 4
X_HEAD_DIM = 256
D_FF = 2816
N_DEV = 8
LANE = 128
ADAM_LR = 0.001
ADAM_B1 = 0.9
ADAM_B2 = 0.999
ADAM_EPS = 1e-08
ADAM_WD = 0.01
ADAM_STEP = 10
VMEM_LIMIT = 56 * 1024 * 1024

MESH_T = pl.DeviceIdType.MESH


def _params(**kw):
    return pltpu.CompilerParams(vmem_limit_bytes=VMEM_LIMIT, **kw)


def _dot(a, b, ca, cb):
    return lax.dot_general(a, b, (((ca,), (cb,)), ((), ())), preferred_element_type=F32)


def _bf(x):
    return x.astype(BF16)


def _sigmoid(x):
    return 1.0 / (1.0 + jnp.exp(-x))


def _rms_fwd(x, g):
    r = lax.rsqrt(jnp.mean(x * x, axis=-1, keepdims=True) + EPS)
    return x * r * g


def _rms_bwd(dy, x, g):
    r = lax.rsqrt(jnp.mean(x * x, axis=-1, keepdims=True) + EPS)
    xh = x * r
    dg = jnp.sum(dy * xh, axis=0, keepdims=True)
    t = dy * g
    dx = r * (t - xh * jnp.mean(t * xh, axis=-1, keepdims=True))
    return dx, dg


def _full(shape):
    nd = len(shape)
    return pl.BlockSpec(shape, lambda *a: (0,) * nd)


def _norm_mm(x, g, w, trans, tn, name, comm=None, tm=512):
    t, d = x.shape
    n = w.shape[0] if trans else w.shape[1]
    tm = min(tm, t)

    def body(x_ref, g_ref, w_ref, h_ref, p_ref):
        h = _bf(_rms_fwd(x_ref[...], g_ref[...]))
        h_ref[...] = h
        p_ref[...] = _dot(h, w_ref[...], 1, 1 if trans else 0)

    w_spec = pl.BlockSpec((tn, d), lambda i, j: (j, 0)) if trans else pl.BlockSpec((d, tn), lambda i, j: (0, j))
    (h, p), got = _call(
        body, (x, g, w), name=name, grid=(t // tm, n // tn),
        in_specs=[pl.BlockSpec((tm, d), lambda i, j: (i, 0)), _full((1, d)), w_spec],
        out_specs=[pl.BlockSpec((tm, d), lambda i, j: (i, 0)), pl.BlockSpec((tm, tn), lambda i, j: (i, j))],
        out_shape=[jax.ShapeDtypeStruct((t, d), BF16), jax.ShapeDtypeStruct((t, n), F32)], comm=comm)
    return (h, p, got) if comm is not None else (h, p)


def _mm_postnorm_res(a, w, g, res, name, tm=256):
    t, k = a.shape
    d = w.shape[1]

    def body(a_ref, w_ref, g_ref, res_ref, y_ref, o_ref):
        y = _dot(a_ref[...], w_ref[...], 1, 0)
        y_ref[...] = y
        o_ref[...] = res_ref[...] + _rms_fwd(y, g_ref[...])

    row = lambda width: pl.BlockSpec((tm, width), lambda i: (i, 0))
    return pl.pallas_call(
        body, name=name, grid=(t // tm,),
        in_specs=[row(k), _full((k, d)), _full((1, d)), row(d)],
        out_specs=[row(d), row(d)],
        out_shape=[jax.ShapeDtypeStruct((t, d), F32)] * 2,
        compiler_params=_params(),
    )(a, w, g, res)


def _mm_postnorm_res_loss(a, w, g, res, tgt, name, tm=256):
    t, k = a.shape
    d = w.shape[1]

    def body(a_ref, w_ref, g_ref, res_ref, tgt_ref, y_ref, dout_ref, loss_ref):
        @pl.when(pl.program_id(0) == 0)
        def _():
            loss_ref[...] = jnp.zeros_like(loss_ref)

        y = _dot(a_ref[...], w_ref[...], 1, 0)
        y_ref[...] = y
        diff = res_ref[...] + _rms_fwd(y, g_ref[...]) - tgt_ref[...]
        dout_ref[...] = diff * (1.0 / d)
        part = jnp.sum(jnp.sum(diff * diff, axis=-1, keepdims=True), axis=0, keepdims=True)
        loss_ref[...] += (0.5 / d) * part

    row = lambda width: pl.BlockSpec((tm, width), lambda i: (i, 0))
    return pl.pallas_call(
        body, name=name, grid=(t // tm,),
        in_specs=[row(k), _full((k, d)), _full((1, d)), row(d), row(d)],
        out_specs=[row(d), row(d), _full((1, 1))],
        out_shape=[jax.ShapeDtypeStruct((t, d), F32)] * 2 + [jax.ShapeDtypeStruct((1, 1), F32)],
        compiler_params=_params(),
    )(a, w, g, res, tgt)


def _postnorm_bwd_mm(dout, y, g, w, name, tm=256):
    t, d = y.shape
    k = w.shape[0]

    def body(dout_ref, y_ref, g_ref, w_ref, dy_ref, da_ref, dg_ref):
        @pl.when(pl.program_id(0) == 0)
        def _():
            dg_ref[...] = jnp.zeros_like(dg_ref)

        dy, dg = _rms_bwd(dout_ref[...], y_ref[...], g_ref[...])
        dg_ref[...] += dg
        dyb = _bf(dy)
        dy_ref[...] = dyb
        da_ref[...] = _dot(dyb, w_ref[...], 1, 1)

    row = lambda width: pl.BlockSpec((tm, width), lambda i: (i, 0))
    return pl.pallas_call(
        body, name=name, grid=(t // tm,),
        in_specs=[row(d), row(d), _full((1, d)), _full((k, d))],
        out_specs=[row(d), row(k), _full((1, d))],
        out_shape=[jax.ShapeDtypeStruct((t, d), BF16), jax.ShapeDtypeStruct((t, k), F32), jax.ShapeDtypeStruct((1, d), F32)],
        compiler_params=_params(),
    )(dout, y, g, w)


def _mm_prenorm_bwd(dp, w, trans, x, g, dres, tk, name, comm=None, tm=512):
    dps = list(dp) if isinstance(dp, (list, tuple)) else [dp]
    nparts = len(dps)
    t, n_each = dps[0].shape
    d = x.shape[1]
    tm = min(tm, t)
    nk_each = n_each // tk
    nk = nk_each * nparts

    def body(*refs):
        dp_refs = refs[:nparts]
        w_ref, x_ref, g_ref, dres_ref, dx_ref, dg_ref, acc_ref = refs[nparts:]
        i, kk = pl.program_id(0), pl.program_id(1)

        @pl.when((i == 0) & (kk == 0))
        def _():
            dg_ref[...] = jnp.zeros_like(dg_ref)

        @pl.when(kk == 0)
        def _():
            acc_ref[...] = jnp.zeros_like(acc_ref)

        for part, dp_ref in enumerate(dp_refs):
            @pl.when((kk >= part * nk_each) & (kk < (part + 1) * nk_each))
            def _():
                acc_ref[...] += _dot(dp_ref[...], w_ref[...], 1, 0 if trans else 1)

        @pl.when(kk == nk - 1)
        def _():
            dx, dg = _rms_bwd(acc_ref[...], x_ref[...], g_ref[...])
            dg_ref[...] += dg
            dx_ref[...] = dres_ref[...] + dx

    dp_spec = lambda part: pl.BlockSpec((tm, tk), lambda i, kk: (i, jnp.clip(kk - part * nk_each, 0, nk_each - 1)))
    w_spec = pl.BlockSpec((tk, d), lambda i, kk: (kk, 0)) if trans else pl.BlockSpec((d, tk), lambda i, kk: (0, kk))
    row = pl.BlockSpec((tm, d), lambda i, kk: (i, 0))
    return _call(
        body, (*dps, w, x, g, dres), name=name, grid=(t // tm, nk),
        in_specs=[dp_spec(part) for part in range(nparts)] + [w_spec, row, _full((1, d)), row],
        out_specs=[row, _full((1, d))],
        out_shape=[jax.ShapeDtypeStruct((t, d), F32), jax.ShapeDtypeStruct((1, d), F32)],
        scratch_shapes=[pltpu.VMEM((tm, d), F32)], comm=comm)


def _dw(a, b, name, bcol=None, tka=256):
    parts = list(a) if isinstance(a, (list, tuple)) else [a]
    nparts = len(parts)
    t, ka_each = parts[0].shape
    bci, nb = (0, b.shape[1]) if bcol is None else bcol
    nt_each = ka_each // tka

    def body(*refs):
        a_refs, b_ref, o_ref = refs[:nparts], refs[nparts], refs[nparts + 1]
        i = pl.program_id(0)
        for part, a_ref in enumerate(a_refs):
            @pl.when((i >= part * nt_each) & (i < (part + 1) * nt_each))
            def _():
                o_ref[...] = _bf(_dot(a_ref[...], b_ref[...], 0, 0))

    a_spec = lambda part: pl.BlockSpec((t, tka), lambda i: (0, jnp.clip(i - part * nt_each, 0, nt_each - 1)))
    return pl.pallas_call(
        body, name=name, grid=(nt_each * nparts,),
        in_specs=[a_spec(part) for part in range(nparts)] + [pl.BlockSpec((t, nb), lambda i: (0, bci))],
        out_specs=pl.BlockSpec((tka, nb), lambda i: (i, 0)),
        out_shape=jax.ShapeDtypeStruct((nparts * ka_each, nb), BF16),
        compiler_params=_params(),
    )(*parts, b)


def _rope_tables(t):
    pos = jnp.arange(t)
    r = (pos // GRID_W).astype(F32)
    c = (pos % GRID_W).astype(F32)
    npair = ATT_HEAD_DIM // 4
    inv = jnp.power(ROPE_THETA, -jnp.arange(npair, dtype=F32) / npair)
    ang = jnp.concatenate([r[:, None] * inv, c[:, None] * inv], axis=-1)
    cos = jnp.repeat(jnp.cos(ang), 2, axis=-1)
    sin = jnp.repeat(jnp.sin(ang), 2, axis=-1)
    even = (jnp.arange(ATT_HEAD_DIM) % 2) == 0
    sa = jnp.where(even, -sin, 0.0)
    sb = jnp.where(even, 0.0, sin)
    two = lambda a: jnp.tile(a, (1, 2))
    return two(cos), two(sa), two(sb)


def _head_sum_matrix():
    a = jnp.arange(LANE) // ATT_HEAD_DIM
    return (a[:, None] == a[None, :]).astype(BF16)


def _head_mean(v, bd):
    hi = _bf(v)
    lo = _bf(v - hi.astype(F32))
    return (_dot(hi, bd, 1, 0) + _dot(lo, bd, 1, 0)) * (1.0 / ATT_HEAD_DIM)


def _qk_prep(p, gq, gk, tables, bd, tm=512):
    t = p.shape[0]
    tm = min(tm, t)
    cc, sa, sb = tables

    def body(p_ref, gq_ref, gk_ref, cc_ref, sa_ref, sb_ref, bd_ref, q_ref, k_ref):
        cc_, sa_, sb_, bd_ = cc_ref[...], sa_ref[...], sb_ref[...], bd_ref[...]
        low = lax.broadcasted_iota(jnp.int32, (tm, LANE), 1) < ATT_HEAD_DIM

        def normrope(xs, g):
            xn = xs * lax.rsqrt(_head_mean(xs * xs, bd_) + EPS) * g
            return xn * cc_ + pltpu.roll(xn, LANE - 1, 1) * sa_ + pltpu.roll(xn, 1, 1) * sb_

        for j in range(4):
            y = normrope(p_ref[:, j * LANE:(j + 1) * LANE], gq_ref[...]) * (ATT_HEAD_DIM ** -0.5)
            yr = pltpu.roll(y, ATT_HEAD_DIM, 1)
            if j // 2 == 0:
                h0, h1 = jnp.where(low, y, 0.0), jnp.where(low, yr, 0.0)
            else:
                h0, h1 = jnp.where(low, 0.0, yr), jnp.where(low, 0.0, y)
            q_ref[:, (2 * j) * LANE:(2 * j + 1) * LANE] = _bf(h0)
            q_ref[:, (2 * j + 1) * LANE:(2 * j + 2) * LANE] = _bf(h1)
        k_ref[...] = _bf(normrope(p_ref[:, ATT_Q_DIM:ATT_Q_DIM + LANE], gk_ref[...]))

    row = lambda width: pl.BlockSpec((tm, width), lambda i: (i, 0))
    return pl.pallas_call(
        body, name="qk_prep", grid=(t // tm,),
        in_specs=[row(ATT_Q_DIM + LANE), _full((1, LANE)), _full((1, LANE)), row(LANE), row(LANE), row(LANE),
                  _full((LANE, LANE))],
        out_specs=[row(ATT_HEADS * LANE), row(LANE)],
        out_shape=[jax.ShapeDtypeStruct((t, ATT_HEADS * LANE), BF16), jax.ShapeDtypeStruct((t, LANE), BF16)],
        compiler_params=_params(),
    )(p, gq, gk, cc, sa, sb, bd)


def _qk_prep_bwd(p, dq, dk, gq, gk, tables, bd, tm=512):
    t = p.shape[0]
    tm = min(tm, t)
    cc, sa, sb = tables

    def body(p_ref, dq_ref, dk_ref, gq_ref, gk_ref, cc_ref, sa_ref, sb_ref, bd_ref, dp_ref, dgq_ref, dgk_ref):
        @pl.when(pl.program_id(0) == 0)
        def _():
            dgq_ref[...] = jnp.zeros_like(dgq_ref)
            dgk_ref[...] = jnp.zeros_like(dgk_ref)

        cc_, sa_, sb_, bd_ = cc_ref[...], sa_ref[...], sb_ref[...], bd_ref[...]
        low = lax.broadcasted_iota(jnp.int32, (tm, LANE), 1) < ATT_HEAD_DIM

        def bwd(xs, g, dy):
            r = lax.rsqrt(_head_mean(xs * xs, bd_) + EPS)
            xh = xs * r
            dxn = dy * cc_ + pltpu.roll(dy * sa_, 1, 1) + pltpu.roll(dy * sb_, LANE - 1, 1)
            dg = jnp.sum(dxn * xh, axis=0, keepdims=True)
            tt = dxn * g
            return r * (tt - xh * _head_mean(tt * xh, bd_)), dg

        dgq = jnp.zeros((1, LANE), F32)
        for j in range(4):
            d0 = dq_ref[:, (2 * j) * LANE:(2 * j + 1) * LANE]
            d1 = dq_ref[:, (2 * j + 1) * LANE:(2 * j + 2) * LANE]
            if j // 2 == 0:
                dy = jnp.where(low, d0, pltpu.roll(d1, ATT_HEAD_DIM, 1))
            else:
                dy = jnp.where(low, pltpu.roll(d0, ATT_HEAD_DIM, 1), d1)
            dx, dg = bwd(p_ref[:, j * LANE:(j + 1) * LANE], gq_ref[...], dy * (ATT_HEAD_DIM ** -0.5))
            dp_ref[:, j * LANE:(j + 1) * LANE] = _bf(dx)
            dgq = dgq + dg
        dgq_ref[...] += dgq
        dx, dg = bwd(p_ref[:, ATT_Q_DIM:ATT_Q_DIM + LANE], gk_ref[...], dk_ref[...])
        dp_ref[:, ATT_Q_DIM:ATT_Q_DIM + LANE] = _bf(dx)
        dgk_ref[...] += dg

    row = lambda width: pl.BlockSpec((tm, width), lambda i: (i, 0))
    return pl.pallas_call(
        body, name="qk_prep_bwd", grid=(t // tm,),
        in_specs=[row(ATT_Q_DIM + LANE), row(ATT_HEADS * LANE), row(LANE), _full((1, LANE)), _full((1, LANE)),
                  row(LANE), row(LANE), row(LANE), _full((LANE, LANE))],
        out_specs=[row(ATT_Q_DIM + LANE), _full((1, LANE)), _full((1, LANE))],
        out_shape=[jax.ShapeDtypeStruct((t, ATT_Q_DIM + LANE), BF16), jax.ShapeDtypeStruct((1, LANE), F32),
                   jax.ShapeDtypeStruct((1, LANE), F32)],
        compiler_params=_params(),
    )(p, dq, dk, gq, gk, cc, sa, sb, bd)


def _attn_fwd(q, k, p, comm=None, tq=256):
    t = k.shape[0]
    tq = min(tq, t)
    v_blk = (ATT_Q_DIM + ATT_KV_DIM) // LANE

    def body(q_ref, k_ref, v_ref, o_ref, o32_ref, lse_ref):
        k_ = k_ref[...]
        v = v_ref[...]
        lowk = lax.broadcasted_iota(jnp.int32, (t, LANE), 1) < ATT_HEAD_DIM
        vm = (_bf(jnp.where(lowk, v, 0.0)), _bf(jnp.where(lowk, 0.0, v)))
        for j in range(4):
            kvh = j // 2
            acc = None
            for sub in range(2):
                h = 2 * j + sub
                s = _dot(q_ref[:, h * LANE:(h + 1) * LANE], k_, 1, 1)
                mx = jnp.max(s, axis=-1, keepdims=True)
                e = jnp.exp(s - mx)
                l = jnp.sum(e, axis=-1, keepdims=True)
                lse_ref[h] = mx + jnp.log(l)
                o = _dot(_bf(e), vm[kvh], 1, 0) * (1.0 / l)
                if sub != kvh:
                    o = pltpu.roll(o, ATT_HEAD_DIM, 1)
                acc = o if acc is None else acc + o
            o32_ref[:, j * LANE:(j + 1) * LANE] = acc
            o_ref[:, j * LANE:(j + 1) * LANE] = _bf(acc)

    row = pl.BlockSpec((tq, ATT_Q_DIM), lambda i: (i, 0))
    outs, got = _call(
        body, (q, k, p), name="attn_fwd", grid=(t // tq,),
        in_specs=[pl.BlockSpec((tq, ATT_HEADS * LANE), lambda i: (i, 0)), _full((t, LANE)),
                  pl.BlockSpec((t, LANE), lambda i: (0, v_blk))],
        out_specs=[row, row, pl.BlockSpec((ATT_HEADS, tq, 1), lambda i: (0, i, 0))],
        out_shape=[jax.ShapeDtypeStruct((t, ATT_Q_DIM), BF16), jax.ShapeDtypeStruct((t, ATT_Q_DIM), F32),
                   jax.ShapeDtypeStruct((ATT_HEADS, t, 1), F32)], comm=comm)
    return outs, got


def _attn_bwd(q, k, p, dcat, o32, lse, comm=None, tq=256):
    t = k.shape[0]
    tq = min(tq, t)
    v_blk = (ATT_Q_DIM + ATT_KV_DIM) // LANE

    def body(q_ref, k_ref, v_ref, do_ref, o_ref, lse_ref, dq_ref, dk_ref, dv_ref):
        @pl.when(pl.program_id(0) == 0)
        def _():
            dk_ref[...] = jnp.zeros_like(dk_ref)
            dv_ref[...] = jnp.zeros_like(dv_ref)

        k_ = k_ref[...]
        vb = _bf(v_ref[...])
        lowq = lax.broadcasted_iota(jnp.int32, (tq, LANE), 1) < ATT_HEAD_DIM
        dk_acc = jnp.zeros((t, LANE), F32)
        dv_acc = jnp.zeros((t, LANE), F32)
        for j in range(4):
            kvh = j // 2
            dop = do_ref[:, j * LANE:(j + 1) * LANE]
            prod = dop * o_ref[:, j * LANE:(j + 1) * LANE]
            d_low = jnp.sum(jnp.where(lowq, prod, 0.0), axis=-1, keepdims=True)
            d_sub = (d_low, jnp.sum(prod, axis=-1, keepdims=True) - d_low)
            for sub in range(2):
                h = 2 * j + sub
                src = dop if sub == kvh else pltpu.roll(dop, ATT_HEAD_DIM, 1)
                do_h = _bf(jnp.where(lowq, src, 0.0) if kvh == 0 else jnp.where(lowq, 0.0, src))
                qh = q_ref[:, h * LANE:(h + 1) * LANE]
                pr = jnp.exp(_dot(qh, k_, 1, 1) - lse_ref[h])
                ds = _bf(pr * (_dot(do_h, vb, 1, 1) - d_sub[sub]))
                dq_ref[:, h * LANE:(h + 1) * LANE] = _dot(ds, k_, 1, 0)
                dk_acc = dk_acc + _dot(ds, qh, 0, 0)
                dv_acc = dv_acc + _dot(_bf(pr), do_h, 0, 0)
        dk_ref[...] += dk_acc
        dv_ref[...] += dv_acc

    row = pl.BlockSpec((tq, ATT_Q_DIM), lambda i: (i, 0))
    return _call(
        body, (q, k, p, dcat, o32, lse), name="attn_bwd", grid=(t // tq,),
        in_specs=[pl.BlockSpec((tq, ATT_HEADS * LANE), lambda i: (i, 0)), _full((t, LANE)),
                  pl.BlockSpec((t, LANE), lambda i: (0, v_blk)), row, row,
                  pl.BlockSpec((ATT_HEADS, tq, 1), lambda i: (0, i, 0))],
        out_specs=[pl.BlockSpec((tq, ATT_HEADS * LANE), lambda i: (i, 0)), _full((t, LANE)), _full((t, LANE))],
        out_shape=[jax.ShapeDtypeStruct((t, ATT_HEADS * LANE), F32), jax.ShapeDtypeStruct((t, LANE), F32),
                   jax.ShapeDtypeStruct((t, LANE), F32)], comm=comm)


def _lower_bounds(a0, a1):
    def body(a0_ref, a1_ref, lb_ref):
        m = jnp.maximum(a0_ref[...], a1_ref[...])
        e0, e1 = jnp.exp(a0_ref[...] - m), jnp.exp(a1_ref[...] - m)
        lb_ref[...] = e0 / (e0 + e1)

    return pl.pallas_call(body, name="lower_bounds", out_shape=jax.ShapeDtypeStruct(a0.shape, F32))(a0, a1)


def _lower_bounds_bwd(a0, a1, dlb):
    def body(a0_ref, a1_ref, dlb_ref, d0_ref, d1_ref):
        m = jnp.maximum(a0_ref[...], a1_ref[...])
        e0, e1 = jnp.exp(a0_ref[...] - m), jnp.exp(a1_ref[...] - m)
        lb = e0 / (e0 + e1)
        d0 = dlb_ref[...] * lb * (1.0 - lb)
        d0_ref[...] = d0
        d1_ref[...] = -d0

    return pl.pallas_call(body, name="lower_bounds_bwd", out_shape=[jax.ShapeDtypeStruct(a0.shape, F32)] * 2)(a0, a1, dlb)


def _chunk_scan(x, pos, down):
    n = x.shape[0]
    s = 1
    while s < HG_CHUNK:
        if down:
            x = x + jnp.where(pos >= s, pltpu.roll(x, s, 0), 0.0)
        else:
            x = x + jnp.where(pos < HG_CHUNK - s, pltpu.roll(x, n - s, 0), 0.0)
        s *= 2
    return x


def _chunk_concat(x, cid, nc):
    return jnp.concatenate([_bf(jnp.where(cid == c, x, 0.0)) for c in range(nc)], axis=1)


def _chunk_pick(x, cid, nc):
    out = jnp.where(cid == 0, x[:, :LANE], 0.0)
    for c in range(1, nc):
        out = out + jnp.where(cid == c, x[:, c * LANE:(c + 1) * LANE], 0.0)
    return out


def _hgrn_gates(hq, z, lb):
    sq = _sigmoid(hq)
    sig = _sigmoid(z)
    f = lb + (1.0 - lb) * sig
    return hq * sq, sq, sig, f, jnp.log(f)


def _hgrn_local(q, f, g, v, pos, cid, amask, rev, nc):
    k = 1.0 - f
    ba = _chunk_scan(g, pos, not rev)
    bb = _chunk_scan(g, pos, rev)
    eq = 0.5 * (ba - bb + g)
    ex_q, ex_k, ex_i, ex_d = jnp.exp(eq), jnp.exp(-eq), jnp.exp(ba), jnp.exp(bb - g)
    qb, kb, qi, kd = q * ex_q, k * ex_k, q * ex_i, k * ex_d
    dvec = jnp.exp(ba + bb - g)
    a = jnp.where(amask, _dot(_bf(qb), _bf(kb), 1, 1), 0.0)
    kd_m = _chunk_concat(kd, cid, nc)
    qi_m = _chunk_concat(qi, cid, nc)
    ut_all = _dot(_bf(v), kd_m, 0, 0)
    return dict(k=k, ex_q=ex_q, ex_k=ex_k, ex_i=ex_i, ex_d=ex_d, qb=qb, kb=kb, qi=qi, kd=kd, dvec=dvec, a=a,
                kd_m=kd_m, qi_m=qi_m, ut_all=ut_all)


def _hgrn_masks(n, rev):
    row = lax.broadcasted_iota(jnp.int32, (n, LANE), 0)
    ti = lax.broadcasted_iota(jnp.int32, (n, n), 0)
    si = lax.broadcasted_iota(jnp.int32, (n, n), 1)
    tri = (si >= ti) if rev else (si <= ti)
    same = jnp.right_shift(ti, HG_CHUNK_LOG2) == jnp.right_shift(si, HG_CHUNK_LOG2)
    return jnp.bitwise_and(row, HG_CHUNK - 1), jnp.right_shift(row, HG_CHUNK_LOG2), same & tri


def _hgrn_specs(t, nb, rev, bwd):
    n = min(HG_BLOCK, t)
    if rev != bwd:
        blk = lambda i: nb - 1 - i
    else:
        blk = lambda i: i
    col = lambda base: pl.BlockSpec((n, 2 * LANE), lambda hp, i: (blk(i), base + hp))
    return n, blk, col


def _hgrn_fwd(p, lb, rev, comm=None):
    t = p.shape[0]
    nb = t // min(HG_BLOCK, t)
    n, blk, col = _hgrn_specs(t, nb, rev, False)
    nc = n // HG_CHUNK
    z_base = 7 if rev else 5

    def body(hq_ref, z_ref, hi_ref, lb_ref, o_ref, ssave_ref, st_scr):
        @pl.when(pl.program_id(1) == 0)
        def _():
            st_scr[...] = jnp.zeros_like(st_scr)

        pos, cid, amask = _hgrn_masks(n, rev)
        order = list(range(nc))[::-1] if rev else list(range(nc))
        for hh in range(2):
            sl = slice(hh * LANE, (hh + 1) * LANE)
            q, _, _, f, g = _hgrn_gates(hq_ref[:, sl], z_ref[:, sl], lb_ref[0:1, sl])
            v = hi_ref[:, sl]
            c_ = _hgrn_local(q, f, g, v, pos, cid, amask, rev, nc)
            st = st_scr[hh]
            ssave_ref[0, sl, :] = st
            cols = [None] * nc
            for c in order:
                cols[c] = st
                st = st * c_["dvec"][c * HG_CHUNK:c * HG_CHUNK + 1, :] + c_["ut_all"][:, c * LANE:(c + 1) * LANE]
            st_scr[hh] = st
            st_all = _bf(jnp.concatenate(cols, axis=1))
            o_ref[:, sl] = _dot(_bf(c_["a"]), _bf(v), 1, 0) + _dot(c_["qi_m"], st_all, 1, 1)

    return _call(
        body, (p, p, p, lb), name="hgrn_fwd_rev" if rev else "hgrn_fwd", grid=(2, nb),
        in_specs=[col(3), col(z_base), col(9), pl.BlockSpec((1, 2 * LANE), lambda hp, i: (0, hp))],
        out_specs=[pl.BlockSpec((n, 2 * LANE), lambda hp, i: (blk(i), hp)),
                   pl.BlockSpec((1, 2 * LANE, LANE), lambda hp, i: (blk(i), hp, 0))],
        out_shape=[jax.ShapeDtypeStruct((t, HG_DIM), F32), jax.ShapeDtypeStruct((nb, HG_DIM, LANE), F32)],
        scratch_shapes=[pltpu.VMEM((2, LANE, LANE), F32)], comm=comm)


def _hgrn_bwd(p, lb, do, ssave, rev, comm=None):
    t = p.shape[0]
    nb = t // min(HG_BLOCK, t)
    n, blk, col = _hgrn_specs(t, nb, rev, True)
    nc = n // HG_CHUNK
    z_base = 7 if rev else 5

    def body(hq_ref, z_ref, hi_ref, lb_ref, do_ref, ssave_ref, dhq_ref, dz_ref, dhi_ref, dlb_ref, dst_scr):
        @pl.when(pl.program_id(1) == 0)
        def _():
            dst_scr[...] = jnp.zeros_like(dst_scr)
            dlb_ref[...] = jnp.zeros_like(dlb_ref)

        pos, cid, amask = _hgrn_masks(n, rev)
        order = list(range(nc))[::-1] if rev else list(range(nc))
        for hh in range(2):
            sl = slice(hh * LANE, (hh + 1) * LANE)
            hq, lbv = hq_ref[:, sl], lb_ref[0:1, sl]
            q, sq, sig, f, g = _hgrn_gates(hq, z_ref[:, sl], lbv)
            v = hi_ref[:, sl]
            c_ = _hgrn_local(q, f, g, v, pos, cid, amask, rev, nc)
            dvec, ut_all = c_["dvec"], c_["ut_all"]
            drow = lambda c: dvec[c * HG_CHUNK:c * HG_CHUNK + 1, :]
            st = ssave_ref[0, sl, :]
            cols = [None] * nc
            for c in order:
                cols[c] = st
                st = st * drow(c) + ut_all[:, c * LANE:(c + 1) * LANE]
            dob, vb = _bf(do_ref[:, sl]), _bf(v)
            gt_all = _dot(dob, c_["qi_m"], 0, 0)
            dcur = dst_scr[hh]
            dnext = [None] * nc
            ddrow = [None] * nc
            for c in order[::-1]:
                dnext[c] = dcur
                ddrow[c] = jnp.sum(cols[c] * dcur, axis=0, keepdims=True) * drow(c)
                dcur = dcur * drow(c) + gt_all[:, c * LANE:(c + 1) * LANE]
            dst_scr[hh] = dcur
            dsn_all = _bf(jnp.concatenate(dnext, axis=1))
            st_all = _bf(jnp.concatenate(cols, axis=1))
            da = _bf(jnp.where(amask, _dot(dob, vb, 1, 1), 0.0))
            dv = _dot(_bf(c_["a"]), dob, 0, 0) + _dot(c_["kd_m"], dsn_all, 1, 1)
            dqb = _dot(da, _bf(c_["kb"]), 1, 0)
            dkb = _dot(da, _bf(c_["qb"]), 0, 0)
            dqi = _chunk_pick(_dot(dob, st_all, 1, 0), cid, nc)
            dkd = _chunk_pick(_dot(vb, dsn_all, 1, 0), cid, nc)
            dq = dqb * c_["ex_q"] + dqi * c_["ex_i"]
            dk = dkb * c_["ex_k"] + dkd * c_["ex_d"]
            e = dqb * c_["qb"] - dkb * c_["kb"] + dqi * c_["qi"]
            w = dkd * c_["kd"]
            dtot = jnp.where(cid == 0, ddrow[0], 0.0)
            for c in range(1, nc):
                dtot = dtot + jnp.where(cid == c, ddrow[c], 0.0)
            dg = _chunk_scan(e, pos, rev) + (_chunk_scan(w, pos, not rev) - w) + dtot
            df = dg / f - dk
            dz_ref[:, sl] = df * (1.0 - lbv) * sig * (1.0 - sig)
            dlb_ref[0:1, sl] += jnp.sum(df * (1.0 - sig), axis=0, keepdims=True)
            dhq_ref[:, sl] = dq * (sq * (1.0 + hq * (1.0 - sq)))
            dhi_ref[:, sl] = dv

    out = pl.BlockSpec((n, 2 * LANE), lambda hp, i: (blk(i), hp))
    return _call(
        body, (p, p, p, lb, do, ssave), name="hgrn_bwd_rev" if rev else "hgrn_bwd", grid=(2, nb),
        in_specs=[col(3), col(z_base), col(9), pl.BlockSpec((1, 2 * LANE), lambda hp, i: (0, hp)), out,
                  pl.BlockSpec((1, 2 * LANE, LANE), lambda hp, i: (blk(i), hp, 0))],
        out_specs=[out, out, out, pl.BlockSpec((1, 2 * LANE), lambda hp, i: (0, hp))],
        out_shape=[jax.ShapeDtypeStruct((t, HG_DIM), F32)] * 3 + [jax.ShapeDtypeStruct((1, HG_DIM), F32)],
        scratch_shapes=[pltpu.VMEM((2, LANE, LANE), F32)], comm=comm)


def _mix_out(att, of, ob, p, gout, w_out, g_post, x, comm=None, tm=256):
    t, d = x.shape
    tm = min(tm, t)

    def body(att_ref, of_ref, ob_ref, hg0_ref, hg1_ref, gout_ref, w_ref, g_ref, x_ref, cat_ref, y_ref, x1_ref):
        cat_ref[:, :ATT_Q_DIM] = att_ref[...]
        for h in range(HG_HEADS):
            sl = slice(h * LANE, (h + 1) * LANE)
            hg_ref = hg0_ref if h < 2 else hg1_ref
            hg = hg_ref[:, (h % 2) * LANE:(h % 2 + 1) * LANE]
            nrm = _rms_fwd(of_ref[:, sl] + ob_ref[:, sl], gout_ref[...])
            cat_ref[:, ATT_Q_DIM + h * LANE:ATT_Q_DIM + (h + 1) * LANE] = _bf(nrm * hg * _sigmoid(hg))
        y = _dot(cat_ref[...], w_ref[...], 1, 0)
        y_ref[...] = y
        x1_ref[...] = x_ref[...] + _rms_fwd(y, g_ref[...])

    row = lambda width: pl.BlockSpec((tm, width), lambda i: (i, 0))
    return _call(
        body, (att, of, ob, p, p, gout, w_out, g_post, x), name="mix_out", grid=(t // tm,),
        in_specs=[row(ATT_Q_DIM), row(HG_DIM), row(HG_DIM), pl.BlockSpec((tm, 2 * LANE), lambda i: (i, 11)),
                  pl.BlockSpec((tm, 2 * LANE), lambda i: (i, 12)), _full((1, LANE)), _full((d, d)), _full((1, d)), row(d)],
        out_specs=[row(d), row(d), row(d)],
        out_shape=[jax.ShapeDtypeStruct((t, d), BF16), jax.ShapeDtypeStruct((t, d), F32), jax.ShapeDtypeStruct((t, d), F32)],
        comm=comm)


def _rec_bwd(dcat, of, ob, p, gout, tm=256):
    t = of.shape[0]
    tm = min(tm, t)

    def body(dc_ref, of_ref, ob_ref, hg0_ref, hg1_ref, gout_ref, do_ref, dhg_ref, dgo_ref):
        @pl.when(pl.program_id(0) == 0)
        def _():
            dgo_ref[...] = jnp.zeros_like(dgo_ref)

        dgo = jnp.zeros((1, LANE), F32)
        for h in range(HG_HEADS):
            sl = slice(h * LANE, (h + 1) * LANE)
            hg_ref = hg0_ref if h < 2 else hg1_ref
            hg = hg_ref[:, (h % 2) * LANE:(h % 2 + 1) * LANE]
            o = of_ref[:, sl] + ob_ref[:, sl]
            sg = _sigmoid(hg)
            drec = dc_ref[:, sl]
            dhg_ref[:, sl] = drec * _rms_fwd(o, gout_ref[...]) * (sg * (1.0 + hg * (1.0 - sg)))
            do, dg = _rms_bwd(drec * hg * sg, o, gout_ref[...])
            do_ref[:, sl] = do
            dgo = dgo + dg
        dgo_ref[...] += dgo

    row = lambda width: pl.BlockSpec((tm, width), lambda i: (i, 0))
    return pl.pallas_call(
        body, name="rec_bwd", grid=(t // tm,),
        in_specs=[pl.BlockSpec((tm, HG_DIM), lambda i: (i, 1)), row(HG_DIM), row(HG_DIM),
                  pl.BlockSpec((tm, 2 * LANE), lambda i: (i, 11)), pl.BlockSpec((tm, 2 * LANE), lambda i: (i, 12)),
                  _full((1, LANE))],
        out_specs=[row(HG_DIM), row(HG_DIM), _full((1, LANE))],
        out_shape=[jax.ShapeDtypeStruct((t, HG_DIM), F32), jax.ShapeDtypeStruct((t, HG_DIM), F32),
                   jax.ShapeDtypeStruct((1, LANE), F32)],
        compiler_params=_params(),
    )(dcat, of, ob, p, p, gout)


def _assemble_dp(dp_qk, dv, dhq_f, dhq_b, dz_f, dz_b, dhi_f, dhi_b, dhg, tm=512):
    t = dv.shape[0]
    tm = min(tm, t)
    qk_w = ATT_Q_DIM + LANE

    def body(qk_ref, dv_ref, hqf_ref, hqb_ref, zf_ref, zb_ref, hif_ref, hib_ref, hg_ref, dp_ref):
        o = 0
        dp_ref[:, o:o + qk_w] = qk_ref[...]
        o += qk_w
        dp_ref[:, o:o + LANE] = _bf(dv_ref[...])
        o += LANE
        for val in (hqf_ref[...] + hqb_ref[...], zf_ref[...], zb_ref[...], hif_ref[...] + hib_ref[...], hg_ref[...]):
            dp_ref[:, o:o + HG_DIM] = _bf(val)
            o += HG_DIM

    row = lambda width: pl.BlockSpec((tm, width), lambda i: (i, 0))
    return pl.pallas_call(
        body, name="assemble_dp", grid=(t // tm,),
        in_specs=[row(qk_w), row(LANE)] + [row(HG_DIM)] * 7,
        out_specs=row(N_IN),
        out_shape=jax.ShapeDtypeStruct((t, N_IN), BF16),
        compiler_params=_params(),
    )(dp_qk, dv, dhq_f, dhq_b, dz_f, dz_b, dhi_f, dhi_b, dhg)


def _xattn_fwd(q, kv, tq=512):
    t, d = q.shape
    tq = min(tq, t)
    nm = kv.shape[0]

    def body(q_ref, kv_ref, o_ref):
        for h in range(X_HEADS):
            sl = slice(h * X_HEAD_DIM, (h + 1) * X_HEAD_DIM)
            s = _dot(_bf(q_ref[:, sl]), _bf(kv_ref[:, sl]), 1, 1) * (X_HEAD_DIM ** -0.5)
            e = jnp.exp(s - jnp.max(s, axis=-1, keepdims=True))
            pr = e * (1.0 / jnp.sum(e, axis=-1, keepdims=True))
            o_ref[:, sl] = _bf(_dot(_bf(pr), _bf(kv_ref[:, d + h * X_HEAD_DIM:d + (h + 1) * X_HEAD_DIM]), 1, 0))

    return pl.pallas_call(
        body, name="xattn_fwd", grid=(t // tq,),
        in_specs=[pl.BlockSpec((tq, d), lambda i: (i, 0)), _full((nm, 2 * d))],
        out_specs=pl.BlockSpec((tq, d), lambda i: (i, 0)),
        out_shape=jax.ShapeDtypeStruct((t, d), BF16),
        compiler_params=_params(),
    )(q, kv)


def _xattn_bwd(q, kv, do, tq=512):
    t, d = q.shape
    tq = min(tq, t)
    nm = kv.shape[0]

    def body(q_ref, kv_ref, do_ref, dq_ref, dkv_ref):
        @pl.when(pl.program_id(0) == 0)
        def _():
            dkv_ref[...] = jnp.zeros_like(dkv_ref)

        for h in range(X_HEADS):
            sl = slice(h * X_HEAD_DIM, (h + 1) * X_HEAD_DIM)
            slv = slice(d + h * X_HEAD_DIM, d + (h + 1) * X_HEAD_DIM)
            qb, kb, vb, dob = _bf(q_ref[:, sl]), _bf(kv_ref[:, sl]), _bf(kv_ref[:, slv]), _bf(do_ref[:, sl])
            s = _dot(qb, kb, 1, 1) * (X_HEAD_DIM ** -0.5)
            e = jnp.exp(s - jnp.max(s, axis=-1, keepdims=True))
            pr = e * (1.0 / jnp.sum(e, axis=-1, keepdims=True))
            dpr = _dot(dob, vb, 1, 1)
            ds = _bf(pr * (dpr - jnp.sum(pr * dpr, axis=-1, keepdims=True)) * (X_HEAD_DIM ** -0.5))
            dq_ref[:, sl] = _bf(_dot(ds, kb, 1, 0))
            dkv_ref[:, sl] += _dot(ds, qb, 0, 0)
            dkv_ref[:, slv] += _dot(_bf(pr), dob, 0, 0)

    return pl.pallas_call(
        body, name="xattn_bwd", grid=(t // tq,),
        in_specs=[pl.BlockSpec((tq, d), lambda i: (i, 0)), _full((nm, 2 * d)), pl.BlockSpec((tq, d), lambda i: (i, 0))],
        out_specs=[pl.BlockSpec((tq, d), lambda i: (i, 0)), _full((nm, 2 * d))],
        out_shape=[jax.ShapeDtypeStruct((t, d), BF16), jax.ShapeDtypeStruct((nm, 2 * d), F32)],
        compiler_params=_params(),
    )(q, kv, do)


CONV_TN = 256


def _shift_rows(u, row, t, delta):
    if delta < 0:
        return jnp.where(row == 0, 0.0, pltpu.roll(u, 1, 0))
    return jnp.where(row == t - 1, 0.0, pltpu.roll(u, t - 1, 0))


def _conv_gate_fwd(u, conv_w, conv_b):
    t = u.shape[0]
    nt = D_FF // CONV_TN

    def body(ug_ref, uv_ref, wg_ref, wv_ref, bg_ref, bv_ref, a_ref):
        row = lax.broadcasted_iota(jnp.int32, (t, CONV_TN), 0)

        def conv(u_ref, w_ref, b_ref):
            uu = u_ref[...]
            return (b_ref[...] + _shift_rows(uu, row, t, -1) * w_ref[0:1, :] + uu * w_ref[1:2, :]
                    + _shift_rows(uu, row, t, 1) * w_ref[2:3, :])

        gate = conv(ug_ref, wg_ref, bg_ref)
        a_ref[...] = _bf(gate * _sigmoid(gate) * conv(uv_ref, wv_ref, bv_ref))

    col = lambda rows, off: pl.BlockSpec((rows, CONV_TN), lambda j: (0, j + off))
    return pl.pallas_call(
        body, name="conv_gate_fwd", grid=(nt,),
        in_specs=[col(t, 0), col(t, nt), col(3, 0), col(3, nt), col(1, 0), col(1, nt)],
        out_specs=col(t, 0),
        out_shape=jax.ShapeDtypeStruct((t, D_FF), BF16),
        compiler_params=_params(),
    )(u, u, conv_w, conv_w, conv_b, conv_b)


def _conv_gate_bwd(u, conv_w, conv_b, da, comm=None):
    t = u.shape[0]
    nt = D_FF // CONV_TN

    def body(ug_ref, uv_ref, wg_ref, wv_ref, bg_ref, bv_ref, da_ref, dug_ref, dwg_ref, dbg_ref, duv_ref, dwv_ref, dbv_ref):
        row = lax.broadcasted_iota(jnp.int32, (t, CONV_TN), 0)
        ug, uv = ug_ref[...], uv_ref[...]
        ug_m, ug_p = _shift_rows(ug, row, t, -1), _shift_rows(ug, row, t, 1)
        uv_m, uv_p = _shift_rows(uv, row, t, -1), _shift_rows(uv, row, t, 1)
        gate = bg_ref[...] + ug_m * wg_ref[0:1, :] + ug * wg_ref[1:2, :] + ug_p * wg_ref[2:3, :]
        val = bv_ref[...] + uv_m * wv_ref[0:1, :] + uv * wv_ref[1:2, :] + uv_p * wv_ref[2:3, :]
        sg = _sigmoid(gate)
        da_ = da_ref[...]

        def emit(dacc, um, uu, up, w_ref, du_ref, dw_ref, db_ref):
            du_ref[...] = _bf(_shift_rows(dacc, row, t, 1) * w_ref[0:1, :] + dacc * w_ref[1:2, :]
                              + _shift_rows(dacc, row, t, -1) * w_ref[2:3, :])
            dw_ref[0:1, :] = jnp.sum(dacc * um, axis=0, keepdims=True)
            dw_ref[1:2, :] = jnp.sum(dacc * uu, axis=0, keepdims=True)
            dw_ref[2:3, :] = jnp.sum(dacc * up, axis=0, keepdims=True)
            db_ref[...] = jnp.sum(dacc, axis=0, keepdims=True)

        emit(da_ * val * (sg * (1.0 + gate * (1.0 - sg))), ug_m, ug, ug_p, wg_ref, dug_ref, dwg_ref, dbg_ref)
        emit(da_ * gate * sg, uv_m, uv, uv_p, wv_ref, duv_ref, dwv_ref, dbv_ref)

    col = lambda rows, off: pl.BlockSpec((rows, CONV_TN), lambda j: (0, j + off))
    half_shapes = [jax.ShapeDtypeStruct((t, D_FF), BF16), jax.ShapeDtypeStruct((3, D_FF), F32),
                   jax.ShapeDtypeStruct((1, D_FF), F32)]
    outs, got = _call(
        body, (u, u, conv_w, conv_w, conv_b, conv_b, da), name="conv_gate_bwd", grid=(nt,),
        in_specs=[col(t, 0), col(t, nt), col(3, 0), col(3, nt), col(1, 0), col(1, nt), col(t, 0)],
        out_specs=[col(t, 0), col(3, 0), col(1, 0)] * 2, out_shape=half_shapes * 2, comm=comm)
    return outs[:3], outs[3:], got


def _row_tile(r, cap):
    best = r
    for cand in range(16, cap + 1, 16):
        if r % cand == 0:
            best = cand
    return best


def _sum_parts(parts, name, tr=256):
    cols = list(parts) if isinstance(parts, (list, tuple)) else [parts]
    _, r, c = cols[0].shape
    tr = _row_tile(r, tr)

    def body(*refs):
        o_ref = refs[-1]
        for k, p_ref in enumerate(refs[:-1]):
            acc = p_ref[0].astype(F32)
            for i in range(1, N_DEV):
                acc = acc + p_ref[i].astype(F32)
            o_ref[:, k * c:(k + 1) * c] = acc

    return pl.pallas_call(
        body, name=name, grid=(r // tr,),
        in_specs=[pl.BlockSpec((N_DEV, tr, c), lambda i: (0, i, 0))] * len(cols),
        out_specs=pl.BlockSpec((tr, c * len(cols)), lambda i: (i, 0)),
        out_shape=jax.ShapeDtypeStruct((r, c * len(cols)), F32),
        compiler_params=_params(),
    )(*cols)


def _adamw(w, g, m, v, name, tr=256):
    r, c = w.shape
    tr = _row_tile(r, tr)

    def body(w_ref, g_ref, m_ref, v_ref, d_ref, nm_ref, nv_ref):
        g_ = g_ref[...]
        m_ = ADAM_B1 * m_ref[...] + (1.0 - ADAM_B1) * g_
        v_ = ADAM_B2 * v_ref[...] + (1.0 - ADAM_B2) * (g_ * g_)
        m_hat = m_ / (1.0 - ADAM_B1 ** ADAM_STEP)
        v_hat = v_ / (1.0 - ADAM_B2 ** ADAM_STEP)
        d_ref[...] = -ADAM_LR * (m_hat / (jnp.sqrt(v_hat) + ADAM_EPS) + ADAM_WD * w_ref[...])
        nm_ref[...] = m_
        nv_ref[...] = v_

    blk = pl.BlockSpec((tr, c), lambda i: (i, 0))
    return pl.pallas_call(
        body, name=name, grid=(r // tr,),
        in_specs=[blk] * 4, out_specs=[blk] * 3,
        out_shape=[jax.ShapeDtypeStruct((r, c), F32)] * 3,
        compiler_params=_params(),
    )(w, g, m, v)


def _mesh_pos():
    return lax.axis_index("x"), lax.axis_index("y"), lax.axis_index("c")


def _dev_index(px, py, pc):
    return 4 * px + 2 * py + pc


class _Gather:
    def __init__(self, arrs):
        self.arrs = list(arrs)
        n = len(self.arrs)
        self.out_shape = [jax.ShapeDtypeStruct((N_DEV,) + a.shape, a.dtype) for a in self.arrs]
        self.sems = [pltpu.SemaphoreType.DMA((7, n)), pltpu.SemaphoreType.DMA((7, n)), pltpu.SemaphoreType.DMA((n,))]

    def _ctx(self, ins, outs, sems):
        send_sems, recv_sems, local_sems = sems
        x, y, c = _mesh_pos()
        chips = [(1 - x, y), (x, 1 - y), (1 - x, 1 - y)]

        def copy(k, a, block, to, src=None):
            dst = outs[a].at[_dev_index(*block)]
            return pltpu.make_async_remote_copy(
                src_ref=dst if src is None else src, dst_ref=dst, send_sem=send_sems.at[k, a], recv_sem=recv_sems.at[k, a],
                device_id=to, device_id_type=MESH_T)

        n = len(ins)
        me, sibling = (x, y, c), (x, y, 1 - c)
        mine = [pltpu.make_async_copy(ins[a], outs[a].at[_dev_index(*me)], local_sems.at[a]) for a in range(n)]
        first = [copy(0, a, me, sibling, src=ins[a]) for a in range(n)]
        first += [copy(1 + j, a, me, (*chip, c), src=ins[a]) for j, chip in enumerate(chips) for a in range(n)]
        passed = [copy(4 + j, a, (*chip, c), sibling) for j, chip in enumerate(chips) for a in range(n)]
        return n, c, me, sibling, chips, copy, mine, first, passed

    def start(self, ins, outs, sems):
        _, _, _, _, _, _, mine, first, _ = self._ctx(ins, outs, sems)
        for cp in mine + first:
            cp.start()

    def forward(self, ins, outs, sems):
        n, c, me, _, chips, copy, _, _, passed = self._ctx(ins, outs, sems)
        for j, chip in enumerate(chips):
            for a in range(n):
                copy(1 + j, a, (*chip, c), me).wait_recv()
                passed[j * n + a].start()

    def finish(self, ins, outs, sems):
        n, c, me, sibling, chips, copy, mine, first, passed = self._ctx(ins, outs, sems)
        for a in range(n):
            copy(0, a, sibling, me).wait_recv()
        for j, chip in enumerate(chips):
            for a in range(n):
                copy(4 + j, a, (*chip, 1 - c), me).wait_recv()
        for cp in first + passed:
            cp.wait_send()
        for cp in mine:
            cp.wait()


class _Exchange:
    def __init__(self, arrs):
        self.arrs = list(arrs)
        n = len(self.arrs)
        self.out_shape = [jax.ShapeDtypeStruct(a.shape, a.dtype) for a in self.arrs]
        self.sems = [pltpu.SemaphoreType.DMA((7, n)), pltpu.SemaphoreType.DMA((7, n)), pltpu.SemaphoreType.DMA((n,))]

    def _copies(self, ins, outs, sems):
        send_sems, recv_sems, local_sems = sems
        x, y, c = _mesh_pos()
        me_i = _dev_index(x, y, c)
        n = len(ins)
        mine = [pltpu.make_async_copy(ins[a].at[me_i], outs[a].at[me_i], local_sems.at[a]) for a in range(n)]
        copies = []
        for k in range(1, N_DEV):
            px = 1 - x if k & 4 else x
            py = 1 - y if k & 2 else y
            pc = 1 - c if k & 1 else c
            peer_i = _dev_index(px, py, pc)
            for a in range(n):
                copies.append(pltpu.make_async_remote_copy(
                    src_ref=ins[a].at[peer_i], dst_ref=outs[a].at[me_i], send_sem=send_sems.at[k - 1, a],
                    recv_sem=recv_sems.at[k - 1, a], device_id=(px, py, pc), device_id_type=MESH_T))
        return mine, copies

    def start(self, ins, outs, sems):
        mine, copies = self._copies(ins, outs, sems)
        for cp in mine + copies:
            cp.start()

    forward = None

    def finish(self, ins, outs, sems):
        mine, copies = self._copies(ins, outs, sems)
        for cp in copies:
            cp.wait_recv()
        for cp in copies:
            cp.wait_send()
        for cp in mine:
            cp.wait()


def _comm_alone(comm, name):
    n = len(comm.arrs)

    def body(*refs):
        ins, outs, sems = refs[:n], refs[n:2 * n], refs[2 * n:]
        comm.start(ins, outs, sems)
        if comm.forward is not None:
            comm.forward(ins, outs, sems)
        comm.finish(ins, outs, sems)

    any_spec = pl.BlockSpec(memory_space=pl.ANY)
    return pl.pallas_call(body, name=name, in_specs=[any_spec] * n, out_specs=[any_spec] * n, out_shape=comm.out_shape,
                          scratch_shapes=comm.sems)(*comm.arrs)


def _call(body, operands, *, name, grid, in_specs, out_specs, out_shape, scratch_shapes=(), comm=None):
    if comm is None:
        outs = pl.pallas_call(body, name=name, grid=grid, in_specs=in_specs, out_specs=out_specs, out_shape=out_shape,
                              scratch_shapes=list(scratch_shapes), compiler_params=_params())(*operands)
        return list(outs), []
    ni, no, ns = len(in_specs), len(out_specs), len(scratch_shapes)
    ci, co = len(comm.arrs), len(comm.out_shape)
    total = 1
    for gdim in grid:
        total *= gdim

    def hosted(*refs):
        o0 = ni + ci
        s0 = o0 + no + co
        ins, cins = refs[:ni], refs[ni:o0]
        outs, couts = refs[o0:o0 + no], refs[o0 + no:s0]
        scr, csems = refs[s0:s0 + ns], refs[s0 + ns:]
        step = 0
        for axis, gdim in enumerate(grid):
            step = step * gdim + pl.program_id(axis)

        @pl.when(step == 0)
        def _():
            comm.start(cins, couts, csems)

        if comm.forward is not None:
            @pl.when(step == (3 * total) // 4)
            def _():
                comm.forward(cins, couts, csems)

        body(*ins, *outs, *scr)

        @pl.when(step == total - 1)
        def _():
            comm.finish(cins, couts, csems)

    any_spec = pl.BlockSpec(memory_space=pl.ANY)
    res = pl.pallas_call(
        hosted, name=name, grid=grid, in_specs=list(in_specs) + [any_spec] * ci, out_specs=list(out_specs) + [any_spec] * co,
        out_shape=list(out_shape) + comm.out_shape, scratch_shapes=list(scratch_shapes) + comm.sems,
        compiler_params=_params())(*operands, *comm.arrs)
    return list(res[:no]), list(res[no:])


def _rows(a):
    return a.reshape(-1, a.shape[-1])


def _slots(a):
    return a.reshape(N_DEV, -1, a.shape[-1])


def _local_step(x, mem, tgt, w, wire):
    tables = _rope_tables(x.shape[0])
    bd = _head_sum_matrix()
    two = lambda g: jnp.tile(g, (1, 2))
    gq2, gk2 = two(w["q_norm_g"]), two(w["k_norm_g"])

    w_in_t, hg_lb, conv_w = _comm_alone(_Gather([wire["w_in"], w["hg_lb"].reshape(4, -1), w["conv_w"][0]]), "gather_w_in")
    w_in_t = _rows(w_in_t)
    hg_lb = jnp.transpose(hg_lb.reshape(N_DEV, 2, 2, -1), (1, 2, 0, 3)).reshape(2, 2, HG_DIM)
    conv_w = jnp.transpose(conv_w, (1, 0, 2)).reshape(3, 2 * D_FF)
    lb_a0, lb_a1 = hg_lb[:, 0, :], hg_lb[:, 1, :]

    h1, p, (w_out, w_xq) = _norm_mm(x, w["pre_mix_g"], w_in_t, True, 1664, "in_proj", _Gather([wire["w_out"], wire["w_xq"]]))
    qr, kr = _qk_prep(p, gq2, gk2, tables, bd)
    (att, att32, lse), (w_up_t,) = _attn_fwd(qr, kr, p, _Gather([wire["w_up"]]))
    lb = _lower_bounds(lb_a0, lb_a1)
    (of, s_f), (w_xkv_t, w_xo) = _hgrn_fwd(p, lb[0:1], False, _Gather([wire["w_xkv"], wire["w_xo"]]))
    (ob, s_b), (w_down,) = _hgrn_fwd(p, lb[1:2], True, _Gather([wire["w_down"]]))
    w_up_t, w_out, w_xq, w_xkv_t, w_xo, w_down = (_rows(a) for a in (w_up_t, w_out, w_xq, w_xkv_t, w_xo, w_down))
    (cat, mixed, x1), _ = _mix_out(att, of, ob, p, w["hg_out_norm_g"], w_out, w["post_mix_g"], x)

    h2, q2 = _norm_mm(x1, w["pre_x_g"], w_xq, False, 1024, "xq_proj")
    mn, kv = _norm_mm(mem, w["mem_norm_g"], w_xkv_t, True, 1024, "xkv_proj")
    o2 = _xattn_fwd(q2, kv)
    y2, x2 = _mm_postnorm_res(o2, w_xo, w["post_x_g"], x1, "xo_proj")

    h3, u = _norm_mm(x2, w["pre_ffn_g"], w_up_t, True, 1408, "up_proj")
    a = _conv_gate_fwd(u, conv_w, w["conv_b"])
    y3, dx3, loss = _mm_postnorm_res_loss(a, w_down, w["post_ffn_g"], x2, tgt, "down_proj")

    g, parts = {}, {}
    dy3, da, g["post_ffn_g"] = _postnorm_bwd_mm(dx3, y3, w["post_ffn_g"], w_down, "down_bwd")
    dw_down = _dw(a, dy3, "dw_down")
    (du_g, dcw_g, dcb_g), (du_v, dcw_v, dcb_v), (parts["w_down"],) = _conv_gate_bwd(
        u, conv_w, w["conv_b"], da, _Exchange([_slots(dw_down)]))
    g["conv_w"] = jnp.concatenate([dcw_g, dcw_v], axis=1)
    g["conv_b"] = jnp.concatenate([dcb_g, dcb_v], axis=1)
    half = D_MODEL // 2
    dw_up_a = _dw([du_g, du_v], h3, "dw_up_a", bcol=(0, half))
    dw_up_b = _dw([du_g, du_v], h3, "dw_up_b", bcol=(1, half))
    (dx2, g["pre_ffn_g"]), (up_a,) = _mm_prenorm_bwd([du_g, du_v], w_up_t, True, x2, w["pre_ffn_g"], dx3, 1408, "up_bwd",
                                                     _Exchange([_slots(dw_up_a)]))

    dy2, do2, g["post_x_g"] = _postnorm_bwd_mm(dx2, y2, w["post_x_g"], w_xo, "xo_bwd")
    dw_xo = _dw(o2, dy2, "dw_xo")
    dq2, dkv = _xattn_bwd(q2, kv, do2)
    dw_xq = _dw(h2, dq2, "dw_xq")
    dkvb = _bf(dkv)
    dw_xkv = _dw(dkvb, mn, "dw_xkv")
    (_, g["mem_norm_g"]), _ = _mm_prenorm_bwd(dkvb, w_xkv_t, True, mem, w["mem_norm_g"], jnp.zeros_like(mem), 2048, "xkv_bwd")
    (dx1, g["pre_x_g"]), _ = _mm_prenorm_bwd(dq2, w_xq, False, x1, w["pre_x_g"], dx2, 1024, "xq_bwd")

    dmixed, dcat, g["post_mix_g"] = _postnorm_bwd_mm(dx1, mixed, w["post_mix_g"], w_out, "out_bwd")
    dw_out = _dw(cat, dmixed, "dw_out")
    do, dhg, g["hg_out_norm_g"] = _rec_bwd(dcat, of, ob, p, w["hg_out_norm_g"])
    (dhq_f, dz_f, dhi_f, dlb_f), (up_b,) = _hgrn_bwd(p, lb[0:1], do, s_f, False, _Exchange([_slots(dw_up_b)]))
    parts["w_up"] = [up_a, up_b]
    (dhq_b, dz_b, dhi_b, dlb_b), (parts["w_out"],) = _hgrn_bwd(p, lb[1:2], do, s_b, True, _Exchange([_slots(dw_out)]))
    d_a0, d_a1 = _lower_bounds_bwd(lb_a0, lb_a1, jnp.concatenate([dlb_f, dlb_b], axis=0))
    g["hg_lb"] = jnp.stack([d_a0, d_a1], axis=1)
    (dqr, dkr, dv), (parts["w_xo"], parts["w_xq"], parts["w_xkv"]) = _attn_bwd(
        qr, kr, p, dcat, att32, lse, _Exchange([_slots(dw_xo), _slots(dw_xq), _slots(dw_xkv)]))
    dp_qk, dgq, dgk = _qk_prep_bwd(p, dqr, dkr, gq2, gk2, tables, bd)
    g["q_norm_g"], g["k_norm_g"] = dgq, dgk
    dp = _assemble_dp(dp_qk, dv, dhq_f, dhq_b, dz_f, dz_b, dhi_f, dhi_b, dhg)
    dw_in = _dw(dp, h1, "dw_in")
    (dx, g["pre_mix_g"]), (parts["w_in"],) = _mm_prenorm_bwd(dp, w_in_t, True, x, w["pre_mix_g"], dx1, 1664, "in_bwd",
                                                             _Exchange([_slots(dw_in)]))
    return loss, dx, g, parts


_COL_SHARDED = ("w_in", "w_xkv", "w_up")
_ROW_SHARDED = ("w_out", "w_xq", "w_xo", "w_down")
_REPLICATED = ("pre_mix_g", "q_norm_g", "k_norm_g", "hg_out_norm_g", "post_mix_g", "pre_x_g", "mem_norm_g", "post_x_g",
               "pre_ffn_g", "conv_b", "post_ffn_g")
_WEIGHTS = ("pre_mix_g", "w_in", "q_norm_g", "k_norm_g", "hg_lb", "hg_out_norm_g", "w_out", "post_mix_g", "pre_x_g",
            "mem_norm_g", "w_xq", "w_xkv", "w_xo", "post_x_g", "pre_ffn_g", "w_up", "conv_w", "conv_b", "w_down",
            "post_ffn_g")
_ADAM_TRANSPOSED = ("w_in", "w_up")
PACK_W = 1024


def _small_plan(shapes):
    plan, r = [], 0
    for vi, (rows, cols) in enumerate(shapes):
        for i in range(rows):
            for c0 in range(0, cols, PACK_W):
                plan.append((vi, i, c0, min(PACK_W, cols - c0), r))
                r += 1
    return plan, -(-r // 8) * 8


def _pack_small(vals):
    plan, nrows = _small_plan([val.shape for val in vals])

    def body(*refs):
        ins, out = refs[:-1], refs[-1]
        out[...] = jnp.zeros_like(out)
        for vi, i, c0, width, r in plan:
            out[r:r + 1, 0:width] = ins[vi][i:i + 1, c0:c0 + width]

    return pl.pallas_call(body, name="pack_small", out_shape=jax.ShapeDtypeStruct((nrows, PACK_W), F32))(*vals)


def _sum_unpack_small(packs, shapes):
    plan, _ = _small_plan(shapes)

    def body(*refs):
        p_ref, outs = refs[0], refs[1:]
        acc = p_ref[0]
        for i in range(1, N_DEV):
            acc = acc + p_ref[i]
        for vi, i, c0, width, r in plan:
            outs[vi][i:i + 1, c0:c0 + width] = acc[r:r + 1, 0:width]

    return pl.pallas_call(body, name="sum_unpack_small", out_shape=[jax.ShapeDtypeStruct(s, F32) for s in shapes])(packs)


def _adamw_many(ws, gs, ms, vs):
    n = len(ws)

    def body(*refs):
        w_refs, g_refs, m_refs, v_refs = (refs[k * n:(k + 1) * n] for k in range(4))
        d_refs, nm_refs, nv_refs = (refs[(4 + k) * n:(5 + k) * n] for k in range(3))
        for k in range(n):
            g_ = g_refs[k][...]
            m_ = ADAM_B1 * m_refs[k][...] + (1.0 - ADAM_B1) * g_
            v_ = ADAM_B2 * v_refs[k][...] + (1.0 - ADAM_B2) * (g_ * g_)
            m_hat = m_ / (1.0 - ADAM_B1 ** ADAM_STEP)
            v_hat = v_ / (1.0 - ADAM_B2 ** ADAM_STEP)
            d_refs[k][...] = -ADAM_LR * (m_hat / (jnp.sqrt(v_hat) + ADAM_EPS) + ADAM_WD * w_refs[k][...])
            nm_refs[k][...] = m_
            nv_refs[k][...] = v_

    shapes = [jax.ShapeDtypeStruct(a.shape, F32) for a in ws]
    outs = pl.pallas_call(body, name="adamw_small", out_shape=shapes * 3)(*ws, *gs, *ms, *vs)
    return outs[:n], outs[n:2 * n], outs[2 * n:]


def kernel(x, mem, pre_mix_g, w_in, q_norm_g, k_norm_g, hg_lb, hg_out_norm_g, w_out, post_mix_g, pre_x_g, mem_norm_g, w_xq, w_xkv, w_xo, post_x_g, pre_ffn_g, w_up, conv_w, conv_b, w_down, post_ffn_g, loss_target, m_pre_mix_g, m_w_in, m_q_norm_g, m_k_norm_g, m_hg_lb, m_hg_out_norm_g, m_w_out, m_post_mix_g, m_pre_x_g, m_mem_norm_g, m_w_xq, m_w_xkv, m_w_xo, m_post_x_g, m_pre_ffn_g, m_w_up, m_conv_w, m_conv_b, m_w_down, m_post_ffn_g, v_pre_mix_g, v_w_in, v_q_norm_g, v_k_norm_g, v_hg_lb, v_hg_out_norm_g, v_w_out, v_post_mix_g, v_pre_x_g, v_mem_norm_g, v_w_xq, v_w_xkv, v_w_xo, v_post_x_g, v_pre_ffn_g, v_w_up, v_conv_w, v_conv_b, v_w_down, v_post_ffn_g):
    args = dict(locals())
    w = {n: args[n] for n in _WEIGHTS}
    m = {n: args["m_" + n] for n in _WEIGHTS}
    v = {n: args["v_" + n] for n in _WEIGHTS}
    me = _dev_index(*_mesh_pos())

    wire = {n: _bf(w[n][0].T) for n in _COL_SHARDED}
    wire.update({n: _bf(w[n][0]) for n in _ROW_SHARDED})

    loss, grad_x, g, parts = _local_step(x[0], mem[0], loss_target[0], w, wire)

    grads, grads_t = {}, {}
    for n in _COL_SHARDED:
        grads_t[n] = _sum_parts(parts[n], "sum_" + n)
        grads[n] = grads_t[n].T[None]
    for n in _ROW_SHARDED:
        grads[n] = _sum_parts(parts[n], "sum_" + n)[None]

    small = list(_REPLICATED) + ["hg_lb", "conv_w"]
    vals = [g[n] for n in _REPLICATED] + [g["hg_lb"].reshape(4, HG_DIM), g["conv_w"], jnp.pad(loss, ((0, 0), (0, LANE - 1)))]
    shapes = [val.shape for val in vals]
    (packs,) = _comm_alone(_Gather([_pack_small(vals)]), "gather_small_grads")
    summed = _sum_unpack_small(packs, shapes)
    loss = summed[-1][0, 0]
    for n, s in zip(small, summed[:-1]):
        grads[n] = s
    fold = lambda v2: v2[:, :ATT_HEAD_DIM] + v2[:, ATT_HEAD_DIM:]
    grads["q_norm_g"], grads["k_norm_g"] = fold(grads["q_norm_g"]), fold(grads["k_norm_g"])
    grads["hg_lb"] = lax.dynamic_slice_in_dim(grads["hg_lb"].reshape(2, 2, HG_DIM), me * (HG_DIM // N_DEV),
                                              HG_DIM // N_DEV, axis=2)
    grads["conv_w"] = lax.dynamic_slice_in_dim(grads["conv_w"], me * (2 * D_FF // N_DEV), 2 * D_FF // N_DEV, axis=1)[None]

    delta, new_m, new_v = {}, {}, {}
    for n in _COL_SHARDED + _ROW_SHARDED:
        if n in _ADAM_TRANSPOSED:
            d_, m_, v_ = _adamw(w[n][0].T, grads_t[n], m[n][0].T, v[n][0].T, "adamw_" + n)
            delta[n], new_m[n], new_v[n] = d_.T[None], m_.T[None], v_.T[None]
        else:
            d_, m_, v_ = _adamw(w[n][0], grads[n][0], m[n][0], v[n][0], "adamw_" + n)
            delta[n], new_m[n], new_v[n] = d_[None], m_[None], v_[None]
    flat2 = lambda a: a.reshape(-1, a.shape[-1])
    outs = _adamw_many(*[[flat2(d[n]) for n in small] for d in (w, grads, m, v)])
    for dst, vals in zip((delta, new_m, new_v), outs):
        for n, val in zip(small, vals):
            dst[n] = val.reshape(w[n].shape)

    return (loss, grad_x[None], *[grads[n] for n in _WEIGHTS], *[delta[n] for n in _WEIGHTS],
            *[new_m[n] for n in _WEIGHTS], *[new_v[n] for n in _WEIGHTS])
```

```python
import jax
import jax.numpy as jnp
from jax import lax
from jax.experimental import pallas as pl
from jax.experimental.pallas import tpu as pltpu

F32 = jnp.float32
BF16 = jnp.bfloat16

D_MODEL = 1024
GRID_W = 64
EPS = 1e-6
ATT_HEADS = 8
ATT_HEAD_DIM = 64
ATT_Q_DIM = 512
ATT_KV_DIM = 128
ROPE_THETA = 10000.0
HG_HEADS = 4
HG_DIM = 512
HG_CHUNK = 32
HG_CHUNK_LOG2 = 5
HG_BLOCK = 256
N_IN = 3328
X_HEADS = 4
X_HEAD_DIM = 256
D_FF = 2816
N_DEV = 8
LANE = 128
ADAM_LR = 0.001
ADAM_B1 = 0.9
ADAM_B2 = 0.999
ADAM_EPS = 1e-08
ADAM_WD = 0.01
ADAM_STEP = 10
VMEM_LIMIT = 56 * 1024 * 1024

MESH_T = pl.DeviceIdType.MESH


def _params(**kw):
    return pltpu.CompilerParams(vmem_limit_bytes=VMEM_LIMIT, **kw)


def _dot(a, b, ca, cb):
    return lax.dot_general(a, b, (((ca,), (cb,)), ((), ())), preferred_element_type=F32)


def _bf(x):
    return x.astype(BF16)


def _sigmoid(x):
    return 1.0 / (1.0 + jnp.exp(-x))


def _rms_fwd(x, g):
    r = lax.rsqrt(jnp.mean(x * x, axis=-1, keepdims=True) + EPS)
    return x * r * g


def _rms_bwd(dy, x, g):
    r = lax.rsqrt(jnp.mean(x * x, axis=-1, keepdims=True) + EPS)
    xh = x * r
    dg = jnp.sum(dy * xh, axis=0, keepdims=True)
    t = dy * g
    dx = r * (t - xh * jnp.mean(t * xh, axis=-1, keepdims=True))
    return dx, dg


def _full(shape):
    nd = len(shape)
    return pl.BlockSpec(shape, lambda *a: (0,) * nd)


def _norm_mm(x, g, w, trans, tn, name, comm=None, tm=512):
    t, d = x.shape
    n = w.shape[0] if trans else w.shape[1]
    tm = min(tm, t)

    def body(x_ref, g_ref, w_ref, h_ref, p_ref):
        h = _bf(_rms_fwd(x_ref[...], g_ref[...]))
        h_ref[...] = h
        p_ref[...] = _dot(h, w_ref[...], 1, 1 if trans else 0)

    w_spec = pl.BlockSpec((tn, d), lambda i, j: (j, 0)) if trans else pl.BlockSpec((d, tn), lambda i, j: (0, j))
    (h, p), got = _call(
        body, (x, g, w), name=name, grid=(t // tm, n // tn),
        in_specs=[pl.BlockSpec((tm, d), lambda i, j: (i, 0)), _full((1, d)), w_spec],
        out_specs=[pl.BlockSpec((tm, d), lambda i, j: (i, 0)), pl.BlockSpec((tm, tn), lambda i, j: (i, j))],
        out_shape=[jax.ShapeDtypeStruct((t, d), BF16), jax.ShapeDtypeStruct((t, n), F32)], comm=comm)
    return (h, p, got) if comm is not None else (h, p)


def _mm_postnorm_res(a, w, g, res, name, tm=256):
    t, k = a.shape
    d = w.shape[1]

    def body(a_ref, w_ref, g_ref, res_ref, y_ref, o_ref):
        y = _dot(a_ref[...], w_ref[...], 1, 0)
        y_ref[...] = y
        o_ref[...] = res_ref[...] + _rms_fwd(y, g_ref[...])

    row = lambda width: pl.BlockSpec((tm, width), lambda i: (i, 0))
    return pl.pallas_call(
        body, name=name, grid=(t // tm,),
        in_specs=[row(k), _full((k, d)), _full((1, d)), row(d)],
        out_specs=[row(d), row(d)],
        out_shape=[jax.ShapeDtypeStruct((t, d), F32)] * 2,
        compiler_params=_params(),
    )(a, w, g, res)


def _mm_postnorm_res_loss(a, w, g, res, tgt, name, tm=256):
    t, k = a.shape
    d = w.shape[1]

    def body(a_ref, w_ref, g_ref, res_ref, tgt_ref, y_ref, dout_ref, loss_ref):
        @pl.when(pl.program_id(0) == 0)
        def _():
            loss_ref[...] = jnp.zeros_like(loss_ref)

        y = _dot(a_ref[...], w_ref[...], 1, 0)
        y_ref[...] = y
        diff = res_ref[...] + _rms_fwd(y, g_ref[...]) - tgt_ref[...]
        dout_ref[...] = diff * (1.0 / d)
        part = jnp.sum(jnp.sum(diff * diff, axis=-1, keepdims=True), axis=0, keepdims=True)
        loss_ref[...] += (0.5 / d) * part

    row = lambda width: pl.BlockSpec((tm, width), lambda i: (i, 0))
    return pl.pallas_call(
        body, name=name, grid=(t // tm,),
        in_specs=[row(k), _full((k, d)), _full((1, d)), row(d), row(d)],
        out_specs=[row(d), row(d), _full((1, 1))],
        out_shape=[jax.ShapeDtypeStruct((t, d), F32)] * 2 + [jax.ShapeDtypeStruct((1, 1), F32)],
        compiler_params=_params(),
    )(a, w, g, res, tgt)


def _postnorm_bwd_mm(dout, y, g, w, name, tm=256):
    t, d = y.shape
    k = w.shape[0]

    def body(dout_ref, y_ref, g_ref, w_ref, dy_ref, da_ref, dg_ref):
        @pl.when(pl.program_id(0) == 0)
        def _():
            dg_ref[...] = jnp.zeros_like(dg_ref)

        dy, dg = _rms_bwd(dout_ref[...], y_ref[...], g_ref[...])
        dg_ref[...] += dg
        dyb = _bf(dy)
        dy_ref[...] = dyb
        da_ref[...] = _dot(dyb, w_ref[...], 1, 1)

    row = lambda width: pl.BlockSpec((tm, width), lambda i: (i, 0))
    return pl.pallas_call(
        body, name=name, grid=(t // tm,),
        in_specs=[row(d), row(d), _full((1, d)), _full((k, d))],
        out_specs=[row(d), row(k), _full((1, d))],
        out_shape=[jax.ShapeDtypeStruct((t, d), BF16), jax.ShapeDtypeStruct((t, k), F32), jax.ShapeDtypeStruct((1, d), F32)],
        compiler_params=_params(),
    )(dout, y, g, w)


def _mm_prenorm_bwd(dp, w, trans, x, g, dres, tk, name, comm=None, tm=512):
    dps = list(dp) if isinstance(dp, (list, tuple)) else [dp]
    nparts = len(dps)
    t, n_each = dps[0].shape
    d = x.shape[1]
    tm = min(tm, t)
    nk_each = n_each // tk
    nk = nk_each * nparts

    def body(*refs):
        dp_refs = refs[:nparts]
        w_ref, x_ref, g_ref, dres_ref, dx_ref, dg_ref, acc_ref = refs[nparts:]
        i, kk = pl.program_id(0), pl.program_id(1)

        @pl.when((i == 0) & (kk == 0))
        def _():
            dg_ref[...] = jnp.zeros_like(dg_ref)

        @pl.when(kk == 0)
        def _():
            acc_ref[...] = jnp.zeros_like(acc_ref)

        for part, dp_ref in enumerate(dp_refs):
            @pl.when((kk >= part * nk_each) & (kk < (part + 1) * nk_each))
            def _():
                acc_ref[...] += _dot(dp_ref[...], w_ref[...], 1, 0 if trans else 1)

        @pl.when(kk == nk - 1)
        def _():
            dx, dg = _rms_bwd(acc_ref[...], x_ref[...], g_ref[...])
            dg_ref[...] += dg
            dx_ref[...] = dres_ref[...] + dx

    dp_spec = lambda part: pl.BlockSpec((tm, tk), lambda i, kk: (i, jnp.clip(kk - part * nk_each, 0, nk_each - 1)))
    w_spec = pl.BlockSpec((tk, d), lambda i, kk: (kk, 0)) if trans else pl.BlockSpec((d, tk), lambda i, kk: (0, kk))
    row = pl.BlockSpec((tm, d), lambda i, kk: (i, 0))
    return _call(
        body, (*dps, w, x, g, dres), name=name, grid=(t // tm, nk),
        in_specs=[dp_spec(part) for part in range(nparts)] + [w_spec, row, _full((1, d)), row],
        out_specs=[row, _full((1, d))],
        out_shape=[jax.ShapeDtypeStruct((t, d), F32), jax.ShapeDtypeStruct((1, d), F32)],
        scratch_shapes=[pltpu.VMEM((tm, d), F32)], comm=comm)


def _dw(a, b, name, bcol=None, tka=256):
    parts = list(a) if isinstance(a, (list, tuple)) else [a]
    nparts = len(parts)
    t, ka_each = parts[0].shape
    bci, nb = (0, b.shape[1]) if bcol is None else bcol
    nt_each = ka_each // tka

    def body(*refs):
        a_refs, b_ref, o_ref = refs[:nparts], refs[nparts], refs[nparts + 1]
        i = pl.program_id(0)
        for part, a_ref in enumerate(a_refs):
            @pl.when((i >= part * nt_each) & (i < (part + 1) * nt_each))
            def _():
                o_ref[...] = _bf(_dot(a_ref[...], b_ref[...], 0, 0))

    a_spec = lambda part: pl.BlockSpec((t, tka), lambda i: (0, jnp.clip(i - part * nt_each, 0, nt_each - 1)))
    return pl.pallas_call(
        body, name=name, grid=(nt_each * nparts,),
        in_specs=[a_spec(part) for part in range(nparts)] + [pl.BlockSpec((t, nb), lambda i: (0, bci))],
        out_specs=pl.BlockSpec((tka, nb), lambda i: (i, 0)),
        out_shape=jax.ShapeDtypeStruct((nparts * ka_each, nb), BF16),
        compiler_params=_params(),
    )(*parts, b)


def _rope_tables(t):
    pos = jnp.arange(t)
    r = (pos // GRID_W).astype(F32)
    c = (pos % GRID_W).astype(F32)
    npair = ATT_HEAD_DIM // 4
    inv = jnp.power(ROPE_THETA, -jnp.arange(npair, dtype=F32) / npair)
    ang = jnp.concatenate([r[:, None] * inv, c[:, None] * inv], axis=-1)
    cos = jnp.repeat(jnp.cos(ang), 2, axis=-1)
    sin = jnp.repeat(jnp.sin(ang), 2, axis=-1)
    even = (jnp.arange(ATT_HEAD_DIM) % 2) == 0
    sa = jnp.where(even, -sin, 0.0)
    sb = jnp.where(even, 0.0, sin)
    two = lambda a: jnp.tile(a, (1, 2))
    return two(cos), two(sa), two(sb)


def _head_sum_matrix():
    a = jnp.arange(LANE) // ATT_HEAD_DIM
    return (a[:, None] == a[None, :]).astype(BF16)


def _head_mean(v, bd):
    hi = _bf(v)
    lo = _bf(v - hi.astype(F32))
    return (_dot(hi, bd, 1, 0) + _dot(lo, bd, 1, 0)) * (1.0 / ATT_HEAD_DIM)


def _qk_prep(p, gq, gk, tables, bd, tm=512):
    t = p.shape[0]
    tm = min(tm, t)
    cc, sa, sb = tables

    def body(p_ref, gq_ref, gk_ref, cc_ref, sa_ref, sb_ref, bd_ref, q_ref, k_ref):
        cc_, sa_, sb_, bd_ = cc_ref[...], sa_ref[...], sb_ref[...], bd_ref[...]
        low = lax.broadcasted_iota(jnp.int32, (tm, LANE), 1) < ATT_HEAD_DIM

        def normrope(xs, g):
            xn = xs * lax.rsqrt(_head_mean(xs * xs, bd_) + EPS) * g
            return xn * cc_ + pltpu.roll(xn, LANE - 1, 1) * sa_ + pltpu.roll(xn, 1, 1) * sb_

        for j in range(4):
            y = normrope(p_ref[:, j * LANE:(j + 1) * LANE], gq_ref[...]) * (ATT_HEAD_DIM ** -0.5)
            yr = pltpu.roll(y, ATT_HEAD_DIM, 1)
            if j // 2 == 0:
                h0, h1 = jnp.where(low, y, 0.0), jnp.where(low, yr, 0.0)
            else:
                h0, h1 = jnp.where(low, 0.0, yr), jnp.where(low, 0.0, y)
            q_ref[:, (2 * j) * LANE:(2 * j + 1) * LANE] = _bf(h0)
            q_ref[:, (2 * j + 1) * LANE:(2 * j + 2) * LANE] = _bf(h1)
        k_ref[...] = _bf(normrope(p_ref[:, ATT_Q_DIM:ATT_Q_DIM + LANE], gk_ref[...]))

    row = lambda width: pl.BlockSpec((tm, width), lambda i: (i, 0))
    return pl.pallas_call(
        body, name="qk_prep", grid=(t // tm,),
        in_specs=[row(ATT_Q_DIM + LANE), _full((1, LANE)), _full((1, LANE)), row(LANE), row(LANE), row(LANE),
                  _full((LANE, LANE))],
        out_specs=[row(ATT_HEADS * LANE), row(LANE)],
        out_shape=[jax.ShapeDtypeStruct((t, ATT_HEADS * LANE), BF16), jax.ShapeDtypeStruct((t, LANE), BF16)],
        compiler_params=_params(),
    )(p, gq, gk, cc, sa, sb, bd)


def _qk_prep_bwd(p, dq, dk, gq, gk, tables, bd, tm=512):
    t = p.shape[0]
    tm = min(tm, t)
    cc, sa, sb = tables

    def body(p_ref, dq_ref, dk_ref, gq_ref, gk_ref, cc_ref, sa_ref, sb_ref, bd_ref, dp_ref, dgq_ref, dgk_ref):
        @pl.when(pl.program_id(0) == 0)
        def _():
            dgq_ref[...] = jnp.zeros_like(dgq_ref)
            dgk_ref[...] = jnp.zeros_like(dgk_ref)

        cc_, sa_, sb_, bd_ = cc_ref[...], sa_ref[...], sb_ref[...], bd_ref[...]
        low = lax.broadcasted_iota(jnp.int32, (tm, LANE), 1) < ATT_HEAD_DIM

        def bwd(xs, g, dy):
            r = lax.rsqrt(_head_mean(xs * xs, bd_) + EPS)
            xh = xs * r
            dxn = dy * cc_ + pltpu.roll(dy * sa_, 1, 1) + pltpu.roll(dy * sb_, LANE - 1, 1)
            dg = jnp.sum(dxn * xh, axis=0, keepdims=True)
            tt = dxn * g
            return r * (tt - xh * _head_mean(tt * xh, bd_)), dg

        dgq = jnp.zeros((1, LANE), F32)
        for j in range(4):
            d0 = dq_ref[:, (2 * j) * LANE:(2 * j + 1) * LANE]
            d1 = dq_ref[:, (2 * j + 1) * LANE:(2 * j + 2) * LANE]
            if j // 2 == 0:
                dy = jnp.where(low, d0, pltpu.roll(d1, ATT_HEAD_DIM, 1))
            else:
                dy = jnp.where(low, pltpu.roll(d0, ATT_HEAD_DIM, 1), d1)
            dx, dg = bwd(p_ref[:, j * LANE:(j + 1) * LANE], gq_ref[...], dy * (ATT_HEAD_DIM ** -0.5))
            dp_ref[:, j * LANE:(j + 1) * LANE] = _bf(dx)
            dgq = dgq + dg
        dgq_ref[...] += dgq
        dx, dg = bwd(p_ref[:, ATT_Q_DIM:ATT_Q_DIM + LANE], gk_ref[...], dk_ref[...])
        dp_ref[:, ATT_Q_DIM:ATT_Q_DIM + LANE] = _bf(dx)
        dgk_ref[...] += dg

    row = lambda width: pl.BlockSpec((tm, width), lambda i: (i, 0))
    return pl.pallas_call(
        body, name="qk_prep_bwd", grid=(t // tm,),
        in_specs=[row(ATT_Q_DIM + LANE), row(ATT_HEADS * LANE), row(LANE), _full((1, LANE)), _full((1, LANE)),
                  row(LANE), row(LANE), row(LANE), _full((LANE, LANE))],
        out_specs=[row(ATT_Q_DIM + LANE), _full((1, LANE)), _full((1, LANE))],
        out_shape=[jax.ShapeDtypeStruct((t, ATT_Q_DIM + LANE), BF16), jax.ShapeDtypeStruct((1, LANE), F32),
                   jax.ShapeDtypeStruct((1, LANE), F32)],
        compiler_params=_params(),
    )(p, dq, dk, gq, gk, cc, sa, sb, bd)


def _attn_fwd(q, k, p, comm=None, tq=256):
    t = k.shape[0]
    tq = min(tq, t)
    v_blk = (ATT_Q_DIM + ATT_KV_DIM) // LANE

    def body(q_ref, k_ref, v_ref, o_ref, o32_ref, lse_ref):
        k_ = k_ref[...]
        v = v_ref[...]
        lowk = lax.broadcasted_iota(jnp.int32, (t, LANE), 1) < ATT_HEAD_DIM
        vm = (_bf(jnp.where(lowk, v, 0.0)), _bf(jnp.where(lowk, 0.0, v)))
        for j in range(4):
            kvh = j // 2
            acc = None
            for sub in range(2):
                h = 2 * j + sub
                s = _dot(q_ref[:, h * LANE:(h + 1) * LANE], k_, 1, 1)
                mx = jnp.max(s, axis=-1, keepdims=True)
                e = jnp.exp(s - mx)
                l = jnp.sum(e, axis=-1, keepdims=True)
                lse_ref[h] = mx + jnp.log(l)
                o = _dot(_bf(e), vm[kvh], 1, 0) * (1.0 / l)
                if sub != kvh:
                    o = pltpu.roll(o, ATT_HEAD_DIM, 1)
                acc = o if acc is None else acc + o
            o32_ref[:, j * LANE:(j + 1) * LANE] = acc
            o_ref[:, j * LANE:(j + 1) * LANE] = _bf(acc)

    row = pl.BlockSpec((tq, ATT_Q_DIM), lambda i: (i, 0))
    outs, got = _call(
        body, (q, k, p), name="attn_fwd", grid=(t // tq,),
        in_specs=[pl.BlockSpec((tq, ATT_HEADS * LANE), lambda i: (i, 0)), _full((t, LANE)),
                  pl.BlockSpec((t, LANE), lambda i: (0, v_blk))],
        out_specs=[row, row, pl.BlockSpec((ATT_HEADS, tq, 1), lambda i: (0, i, 0))],
        out_shape=[jax.ShapeDtypeStruct((t, ATT_Q_DIM), BF16), jax.ShapeDtypeStruct((t, ATT_Q_DIM), F32),
                   jax.ShapeDtypeStruct((ATT_HEADS, t, 1), F32)], comm=comm)
    return outs, got


def _attn_bwd(q, k, p, dcat, o32, lse, comm=None, tq=256):
    t = k.shape[0]
    tq = min(tq, t)
    v_blk = (ATT_Q_DIM + ATT_KV_DIM) // LANE

    def body(q_ref, k_ref, v_ref, do_ref, o_ref, lse_ref, dq_ref, dk_ref, dv_ref):
        @pl.when(pl.program_id(0) == 0)
        def _():
            dk_ref[...] = jnp.zeros_like(dk_ref)
            dv_ref[...] = jnp.zeros_like(dv_ref)

        k_ = k_ref[...]
        vb = _bf(v_ref[...])
        lowq = lax.broadcasted_iota(jnp.int32, (tq, LANE), 1) < ATT_HEAD_DIM
        dk_acc = jnp.zeros((t, LANE), F32)
        dv_acc = jnp.zeros((t, LANE), F32)
        for j in range(4):
            kvh = j // 2
            dop = do_ref[:, j * LANE:(j + 1) * LANE]
            prod = dop * o_ref[:, j * LANE:(j + 1) * LANE]
            d_low = jnp.sum(jnp.where(lowq, prod, 0.0), axis=-1, keepdims=True)
            d_sub = (d_low, jnp.sum(prod, axis=-1, keepdims=True) - d_low)
            for sub in range(2):
                h = 2 * j + sub
                src = dop if sub == kvh else pltpu.roll(dop, ATT_HEAD_DIM, 1)
                do_h = _bf(jnp.where(lowq, src, 0.0) if kvh == 0 else jnp.where(lowq, 0.0, src))
                qh = q_ref[:, h * LANE:(h + 1) * LANE]
                pr = jnp.exp(_dot(qh, k_, 1, 1) - lse_ref[h])
                ds = _bf(pr * (_dot(do_h, vb, 1, 1) - d_sub[sub]))
                dq_ref[:, h * LANE:(h + 1) * LANE] = _dot(ds, k_, 1, 0)
                dk_acc = dk_acc + _dot(ds, qh, 0, 0)
                dv_acc = dv_acc + _dot(_bf(pr), do_h, 0, 0)
        dk_ref[...] += dk_acc
        dv_ref[...] += dv_acc

    row = pl.BlockSpec((tq, ATT_Q_DIM), lambda i: (i, 0))
    return _call(
        body, (q, k, p, dcat, o32, lse), name="attn_bwd", grid=(t // tq,),
        in_specs=[pl.BlockSpec((tq, ATT_HEADS * LANE), lambda i: (i, 0)), _full((t, LANE)),
                  pl.BlockSpec((t, LANE), lambda i: (0, v_blk)), row, row,
                  pl.BlockSpec((ATT_HEADS, tq, 1), lambda i: (0, i, 0))],
        out_specs=[pl.BlockSpec((tq, ATT_HEADS * LANE), lambda i: (i, 0)), _full((t, LANE)), _full((t, LANE))],
        out_shape=[jax.ShapeDtypeStruct((t, ATT_HEADS * LANE), F32), jax.ShapeDtypeStruct((t, LANE), F32),
                   jax.ShapeDtypeStruct((t, LANE), F32)], comm=comm)


def _lower_bounds(a0, a1):
    def body(a0_ref, a1_ref, lb_ref):
        m = jnp.maximum(a0_ref[...], a1_ref[...])
        e0, e1 = jnp.exp(a0_ref[...] - m), jnp.exp(a1_ref[...] - m)
        lb_ref[...] = e0 / (e0 + e1)

    return pl.pallas_call(body, name="lower_bounds", out_shape=jax.ShapeDtypeStruct(a0.shape, F32))(a0, a1)


def _lower_bounds_bwd(a0, a1, dlb):
    def body(a0_ref, a1_ref, dlb_ref, d0_ref, d1_ref):
        m = jnp.maximum(a0_ref[...], a1_ref[...])
        e0, e1 = jnp.exp(a0_ref[...] - m), jnp.exp(a1_ref[...] - m)
        lb = e0 / (e0 + e1)
        d0 = dlb_ref[...] * lb * (1.0 - lb)
        d0_ref[...] = d0
        d1_ref[...] = -d0

    return pl.pallas_call(body, name="lower_bounds_bwd", out_shape=[jax.ShapeDtypeStruct(a0.shape, F32)] * 2)(a0, a1, dlb)


def _chunk_scan(x, pos, down):
    n = x.shape[0]
    s = 1
    while s < HG_CHUNK:
        if down:
            x = x + jnp.where(pos >= s, pltpu.roll(x, s, 0), 0.0)
        else:
            x = x + jnp.where(pos < HG_CHUNK - s, pltpu.roll(x, n - s, 0), 0.0)
        s *= 2
    return x


def _chunk_concat(x, cid, nc):
    return jnp.concatenate([_bf(jnp.where(cid == c, x, 0.0)) for c in range(nc)], axis=1)


def _chunk_pick(x, cid, nc):
    out = jnp.where(cid == 0, x[:, :LANE], 0.0)
    for c in range(1, nc):
        out = out + jnp.where(cid == c, x[:, c * LANE:(c + 1) * LANE], 0.0)
    return out


def _hgrn_gates(hq, z, lb):
    sq = _sigmoid(hq)
    sig = _sigmoid(z)
    f = lb + (1.0 - lb) * sig
    return hq * sq, sq, sig, f, jnp.log(f)


def _hgrn_local(q, f, g, v, pos, cid, amask, rev, nc):
    k = 1.0 - f
    ba = _chunk_scan(g, pos, not rev)
    bb = _chunk_scan(g, pos, rev)
    eq = 0.5 * (ba - bb + g)
    ex_q, ex_k, ex_i, ex_d = jnp.exp(eq), jnp.exp(-eq), jnp.exp(ba), jnp.exp(bb - g)
    qb, kb, qi, kd = q * ex_q, k * ex_k, q * ex_i, k * ex_d
    dvec = jnp.exp(ba + bb - g)
    a = jnp.where(amask, _dot(_bf(qb), _bf(kb), 1, 1), 0.0)
    kd_m = _chunk_concat(kd, cid, nc)
    qi_m = _chunk_concat(qi, cid, nc)
    ut_all = _dot(_bf(v), kd_m, 0, 0)
    return dict(k=k, ex_q=ex_q, ex_k=ex_k, ex_i=ex_i, ex_d=ex_d, qb=qb, kb=kb, qi=qi, kd=kd, dvec=dvec, a=a,
                kd_m=kd_m, qi_m=qi_m, ut_all=ut_all)


def _hgrn_masks(n, rev):
    row = lax.broadcasted_iota(jnp.int32, (n, LANE), 0)
    ti = lax.broadcasted_iota(jnp.int32, (n, n), 0)
    si = lax.broadcasted_iota(jnp.int32, (n, n), 1)
    tri = (si >= ti) if rev else (si <= ti)
    same = jnp.right_shift(ti, HG_CHUNK_LOG2) == jnp.right_shift(si, HG_CHUNK_LOG2)
    return jnp.bitwise_and(row, HG_CHUNK - 1), jnp.right_shift(row, HG_CHUNK_LOG2), same & tri


def _hgrn_specs(t, nb, rev, bwd):
    n = min(HG_BLOCK, t)
    if rev != bwd:
        blk = lambda i: nb - 1 - i
    else:
        blk = lambda i: i
    col = lambda base: pl.BlockSpec((n, 2 * LANE), lambda hp, i: (blk(i), base + hp))
    return n, blk, col


def _hgrn_fwd(p, lb, rev, comm=None):
    t = p.shape[0]
    nb = t // min(HG_BLOCK, t)
    n, blk, col = _hgrn_specs(t, nb, rev, False)
    nc = n // HG_CHUNK
    z_base = 7 if rev else 5

    def body(hq_ref, z_ref, hi_ref, lb_ref, o_ref, ssave_ref, st_scr):
        @pl.when(pl.program_id(1) == 0)
        def _():
            st_scr[...] = jnp.zeros_like(st_scr)

        pos, cid, amask = _hgrn_masks(n, rev)
        order = list(range(nc))[::-1] if rev else list(range(nc))
        for hh in range(2):
            sl = slice(hh * LANE, (hh + 1) * LANE)
            q, _, _, f, g = _hgrn_gates(hq_ref[:, sl], z_ref[:, sl], lb_ref[0:1, sl])
            v = hi_ref[:, sl]
            c_ = _hgrn_local(q, f, g, v, pos, cid, amask, rev, nc)
            st = st_scr[hh]
            ssave_ref[0, sl, :] = st
            cols = [None] * nc
            for c in order:
                cols[c] = st
                st = st * c_["dvec"][c * HG_CHUNK:c * HG_CHUNK + 1, :] + c_["ut_all"][:, c * LANE:(c + 1) * LANE]
            st_scr[hh] = st
            st_all = _bf(jnp.concatenate(cols, axis=1))
            o_ref[:, sl] = _dot(_bf(c_["a"]), _bf(v), 1, 0) + _dot(c_["qi_m"], st_all, 1, 1)

    return _call(
        body, (p, p, p, lb), name="hgrn_fwd_rev" if rev else "hgrn_fwd", grid=(2, nb),
        in_specs=[col(3), col(z_base), col(9), pl.BlockSpec((1, 2 * LANE), lambda hp, i: (0, hp))],
        out_specs=[pl.BlockSpec((n, 2 * LANE), lambda hp, i: (blk(i), hp)),
                   pl.BlockSpec((1, 2 * LANE, LANE), lambda hp, i: (blk(i), hp, 0))],
        out_shape=[jax.ShapeDtypeStruct((t, HG_DIM), F32), jax.ShapeDtypeStruct((nb, HG_DIM, LANE), F32)],
        scratch_shapes=[pltpu.VMEM((2, LANE, LANE), F32)], comm=comm)


def _hgrn_bwd(p, lb, do, ssave, rev, comm=None):
    t = p.shape[0]
    nb = t // min(HG_BLOCK, t)
    n, blk, col = _hgrn_specs(t, nb, rev, True)
    nc = n // HG_CHUNK
    z_base = 7 if rev else 5

    def body(hq_ref, z_ref, hi_ref, lb_ref, do_ref, ssave_ref, dhq_ref, dz_ref, dhi_ref, dlb_ref, dst_scr):
        @pl.when(pl.program_id(1) == 0)
        def _():
            dst_scr[...] = jnp.zeros_like(dst_scr)
            dlb_ref[...] = jnp.zeros_like(dlb_ref)

        pos, cid, amask = _hgrn_masks(n, rev)
        order = list(range(nc))[::-1] if rev else list(range(nc))
        for hh in range(2):
            sl = slice(hh * LANE, (hh + 1) * LANE)
            hq, lbv = hq_ref[:, sl], lb_ref[0:1, sl]
            q, sq, sig, f, g = _hgrn_gates(hq, z_ref[:, sl], lbv)
            v = hi_ref[:, sl]
            c_ = _hgrn_local(q, f, g, v, pos, cid, amask, rev, nc)
            dvec, ut_all = c_["dvec"], c_["ut_all"]
            drow = lambda c: dvec[c * HG_CHUNK:c * HG_CHUNK + 1, :]
            st = ssave_ref[0, sl, :]
            cols = [None] * nc
            for c in order:
                cols[c] = st
                st = st * drow(c) + ut_all[:, c * LANE:(c + 1) * LANE]
            dob, vb = _bf(do_ref[:, sl]), _bf(v)
            gt_all = _dot(dob, c_["qi_m"], 0, 0)
            dcur = dst_scr[hh]
            dnext = [None] * nc
            ddrow = [None] * nc
            for c in order[::-1]:
                dnext[c] = dcur
                ddrow[c] = jnp.sum(cols[c] * dcur, axis=0, keepdims=True) * drow(c)
                dcur = dcur * drow(c) + gt_all[:, c * LANE:(c + 1) * LANE]
            dst_scr[hh] = dcur
            dsn_all = _bf(jnp.concatenate(dnext, axis=1))
            st_all = _bf(jnp.concatenate(cols, axis=1))
            da = _bf(jnp.where(amask, _dot(dob, vb, 1, 1), 0.0))
            dv = _dot(_bf(c_["a"]), dob, 0, 0) + _dot(c_["kd_m"], dsn_all, 1, 1)
            dqb = _dot(da, _bf(c_["kb"]), 1, 0)
            dkb = _dot(da, _bf(c_["qb"]), 0, 0)
            dqi = _chunk_pick(_dot(dob, st_all, 1, 0), cid, nc)
            dkd = _chunk_pick(_dot(vb, dsn_all, 1, 0), cid, nc)
            dq = dqb * c_["ex_q"] + dqi * c_["ex_i"]
            dk = dkb * c_["ex_k"] + dkd * c_["ex_d"]
            e = dqb * c_["qb"] - dkb * c_["kb"] + dqi * c_["qi"]
            w = dkd * c_["kd"]
            dtot = jnp.where(cid == 0, ddrow[0], 0.0)
            for c in range(1, nc):
                dtot = dtot + jnp.where(cid == c, ddrow[c], 0.0)
            dg = _chunk_scan(e, pos, rev) + (_chunk_scan(w, pos, not rev) - w) + dtot
            df = dg / f - dk
            dz_ref[:, sl] = df * (1.0 - lbv) * sig * (1.0 - sig)
            dlb_ref[0:1, sl] += jnp.sum(df * (1.0 - sig), axis=0, keepdims=True)
            dhq_ref[:, sl] = dq * (sq * (1.0 + hq * (1.0 - sq)))
            dhi_ref[:, sl] = dv

    out = pl.BlockSpec((n, 2 * LANE), lambda hp, i: (blk(i), hp))
    return _call(
        body, (p, p, p, lb, do, ssave), name="hgrn_bwd_rev" if rev else "hgrn_bwd", grid=(2, nb),
        in_specs=[col(3), col(z_base), col(9), pl.BlockSpec((1, 2 * LANE), lambda hp, i: (0, hp)), out,
                  pl.BlockSpec((1, 2 * LANE, LANE), lambda hp, i: (blk(i), hp, 0))],
        out_specs=[out, out, out, pl.BlockSpec((1, 2 * LANE), lambda hp, i: (0, hp))],
        out_shape=[jax.ShapeDtypeStruct((t, HG_DIM), F32)] * 3 + [jax.ShapeDtypeStruct((1, HG_DIM), F32)],
        scratch_shapes=[pltpu.VMEM((2, LANE, LANE), F32)], comm=comm)


def _mix_out(att, of, ob, p, gout, w_out, g_post, x, comm=None, tm=256):
    t, d = x.shape
    tm = min(tm, t)

    def body(att_ref, of_ref, ob_ref, hg0_ref, hg1_ref, gout_ref, w_ref, g_ref, x_ref, cat_ref, y_ref, x1_ref):
        cat_ref[:, :ATT_Q_DIM] = att_ref[...]
        for h in range(HG_HEADS):
            sl = slice(h * LANE, (h + 1) * LANE)
            hg_ref = hg0_ref if h < 2 else hg1_ref
            hg = hg_ref[:, (h % 2) * LANE:(h % 2 + 1) * LANE]
            nrm = _rms_fwd(of_ref[:, sl] + ob_ref[:, sl], gout_ref[...])
            cat_ref[:, ATT_Q_DIM + h * LANE:ATT_Q_DIM + (h + 1) * LANE] = _bf(nrm * hg * _sigmoid(hg))
        y = _dot(cat_ref[...], w_ref[...], 1, 0)
        y_ref[...] = y
        x1_ref[...] = x_ref[...] + _rms_fwd(y, g_ref[...])

    row = lambda width: pl.BlockSpec((tm, width), lambda i: (i, 0))
    return _call(
        body, (att, of, ob, p, p, gout, w_out, g_post, x), name="mix_out", grid=(t // tm,),
        in_specs=[row(ATT_Q_DIM), row(HG_DIM), row(HG_DIM), pl.BlockSpec((tm, 2 * LANE), lambda i: (i, 11)),
                  pl.BlockSpec((tm, 2 * LANE), lambda i: (i, 12)), _full((1, LANE)), _full((d, d)), _full((1, d)), row(d)],
        out_specs=[row(d), row(d), row(d)],
        out_shape=[jax.ShapeDtypeStruct((t, d), BF16), jax.ShapeDtypeStruct((t, d), F32), jax.ShapeDtypeStruct((t, d), F32)],
        comm=comm)


def _rec_bwd(dcat, of, ob, p, gout, tm=256):
    t = of.shape[0]
    tm = min(tm, t)

    def body(dc_ref, of_ref, ob_ref, hg0_ref, hg1_ref, gout_ref, do_ref, dhg_ref, dgo_ref):
        @pl.when(pl.program_id(0) == 0)
        def _():
            dgo_ref[...] = jnp.zeros_like(dgo_ref)

        dgo = jnp.zeros((1, LANE), F32)
        for h in range(HG_HEADS):
            sl = slice(h * LANE, (h + 1) * LANE)
            hg_ref = hg0_ref if h < 2 else hg1_ref
            hg = hg_ref[:, (h % 2) * LANE:(h % 2 + 1) * LANE]
            o = of_ref[:, sl] + ob_ref[:, sl]
            sg = _sigmoid(hg)
            drec = dc_ref[:, sl]
            dhg_ref[:, sl] = drec * _rms_fwd(o, gout_ref[...]) * (sg * (1.0 + hg * (1.0 - sg)))
            do, dg = _rms_bwd(drec * hg * sg, o, gout_ref[...])
            do_ref[:, sl] = do
            dgo = dgo + dg
        dgo_ref[...] += dgo

    row = lambda width: pl.BlockSpec((tm, width), lambda i: (i, 0))
    return pl.pallas_call(
        body, name="rec_bwd", grid=(t // tm,),
        in_specs=[pl.BlockSpec((tm, HG_DIM), lambda i: (i, 1)), row(HG_DIM), row(HG_DIM),
                  pl.BlockSpec((tm, 2 * LANE), lambda i: (i, 11)), pl.BlockSpec((tm, 2 * LANE), lambda i: (i, 12)),
                  _full((1, LANE))],
        out_specs=[row(HG_DIM), row(HG_DIM), _full((1, LANE))],
        out_shape=[jax.ShapeDtypeStruct((t, HG_DIM), F32), jax.ShapeDtypeStruct((t, HG_DIM), F32),
                   jax.ShapeDtypeStruct((1, LANE), F32)],
        compiler_params=_params(),
    )(dcat, of, ob, p, p, gout)


def _assemble_dp(dp_qk, dv, dhq_f, dhq_b, dz_f, dz_b, dhi_f, dhi_b, dhg, tm=512):
    t = dv.shape[0]
    tm = min(tm, t)
    qk_w = ATT_Q_DIM + LANE

    def body(qk_ref, dv_ref, hqf_ref, hqb_ref, zf_ref, zb_ref, hif_ref, hib_ref, hg_ref, dp_ref):
        o = 0
        dp_ref[:, o:o + qk_w] = qk_ref[...]
        o += qk_w
        dp_ref[:, o:o + LANE] = _bf(dv_ref[...])
        o += LANE
        for val in (hqf_ref[...] + hqb_ref[...], zf_ref[...], zb_ref[...], hif_ref[...] + hib_ref[...], hg_ref[...]):
            dp_ref[:, o:o + HG_DIM] = _bf(val)
            o += HG_DIM

    row = lambda width: pl.BlockSpec((tm, width), lambda i: (i, 0))
    return pl.pallas_call(
        body, name="assemble_dp", grid=(t // tm,),
        in_specs=[row(qk_w), row(LANE)] + [row(HG_DIM)] * 7,
        out_specs=row(N_IN),
        out_shape=jax.ShapeDtypeStruct((t, N_IN), BF16),
        compiler_params=_params(),
    )(dp_qk, dv, dhq_f, dhq_b, dz_f, dz_b, dhi_f, dhi_b, dhg)


def _xattn_fwd(q, kv, tq=512):
    t, d = q.shape
    tq = min(tq, t)
    nm = kv.shape[0]

    def body(q_ref, kv_ref, o_ref):
        for h in range(X_HEADS):
            sl = slice(h * X_HEAD_DIM, (h + 1) * X_HEAD_DIM)
            s = _dot(_bf(q_ref[:, sl]), _bf(kv_ref[:, sl]), 1, 1) * (X_HEAD_DIM ** -0.5)
            e = jnp.exp(s - jnp.max(s, axis=-1, keepdims=True))
            pr = e * (1.0 / jnp.sum(e, axis=-1, keepdims=True))
            o_ref[:, sl] = _bf(_dot(_bf(pr), _bf(kv_ref[:, d + h * X_HEAD_DIM:d + (h + 1) * X_HEAD_DIM]), 1, 0))

    return pl.pallas_call(
        body, name="xattn_fwd", grid=(t // tq,),
        in_specs=[pl.BlockSpec((tq, d), lambda i: (i, 0)), _full((nm, 2 * d))],
        out_specs=pl.BlockSpec((tq, d), lambda i: (i, 0)),
        out_shape=jax.ShapeDtypeStruct((t, d), BF16),
        compiler_params=_params(),
    )(q, kv)


def _xattn_bwd(q, kv, do, tq=512):
    t, d = q.shape
    tq = min(tq, t)
    nm = kv.shape[0]

    def body(q_ref, kv_ref, do_ref, dq_ref, dkv_ref):
        @pl.when(pl.program_id(0) == 0)
        def _():
            dkv_ref[...] = jnp.zeros_like(dkv_ref)

        for h in range(X_HEADS):
            sl = slice(h * X_HEAD_DIM, (h + 1) * X_HEAD_DIM)
            slv = slice(d + h * X_HEAD_DIM, d + (h + 1) * X_HEAD_DIM)
            qb, kb, vb, dob = _bf(q_ref[:, sl]), _bf(kv_ref[:, sl]), _bf(kv_ref[:, slv]), _bf(do_ref[:, sl])
            s = _dot(qb, kb, 1, 1) * (X_HEAD_DIM ** -0.5)
            e = jnp.exp(s - jnp.max(s, axis=-1, keepdims=True))
            pr = e * (1.0 / jnp.sum(e, axis=-1, keepdims=True))
            dpr = _dot(dob, vb, 1, 1)
            ds = _bf(pr * (dpr - jnp.sum(pr * dpr, axis=-1, keepdims=True)) * (X_HEAD_DIM ** -0.5))
            dq_ref[:, sl] = _bf(_dot(ds, kb, 1, 0))
            dkv_ref[:, sl] += _dot(ds, qb, 0, 0)
            dkv_ref[:, slv] += _dot(_bf(pr), dob, 0, 0)

    return pl.pallas_call(
        body, name="xattn_bwd", grid=(t // tq,),
        in_specs=[pl.BlockSpec((tq, d), lambda i: (i, 0)), _full((nm, 2 * d)), pl.BlockSpec((tq, d), lambda i: (i, 0))],
        out_specs=[pl.BlockSpec((tq, d), lambda i: (i, 0)), _full((nm, 2 * d))],
        out_shape=[jax.ShapeDtypeStruct((t, d), BF16), jax.ShapeDtypeStruct((nm, 2 * d), F32)],
        compiler_params=_params(),
    )(q, kv, do)


CONV_TN = 256


def _shift_rows(u, row, t, delta):
    if delta < 0:
        return jnp.where(row == 0, 0.0, pltpu.roll(u, 1, 0))
    return jnp.where(row == t - 1, 0.0, pltpu.roll(u, t - 1, 0))


def _conv_gate_fwd(u, conv_w, conv_b):
    t = u.shape[0]
    nt = D_FF // CONV_TN

    def body(ug_ref, uv_ref, wg_ref, wv_ref, bg_ref, bv_ref, a_ref):
        row = lax.broadcasted_iota(jnp.int32, (t, CONV_TN), 0)

        def conv(u_ref, w_ref, b_ref):
            uu = u_ref[...]
            return (b_ref[...] + _shift_rows(uu, row, t, -1) * w_ref[0:1, :] + uu * w_ref[1:2, :]
                    + _shift_rows(uu, row, t, 1) * w_ref[2:3, :])

        gate = conv(ug_ref, wg_ref, bg_ref)
        a_ref[...] = _bf(gate * _sigmoid(gate) * conv(uv_ref, wv_ref, bv_ref))

    col = lambda rows, off: pl.BlockSpec((rows, CONV_TN), lambda j: (0, j + off))
    return pl.pallas_call(
        body, name="conv_gate_fwd", grid=(nt,),
        in_specs=[col(t, 0), col(t, nt), col(3, 0), col(3, nt), col(1, 0), col(1, nt)],
        out_specs=col(t, 0),
        out_shape=jax.ShapeDtypeStruct((t, D_FF), BF16),
        compiler_params=_params(),
    )(u, u, conv_w, conv_w, conv_b, conv_b)


def _conv_gate_bwd(u, conv_w, conv_b, da, comm=None):
    t = u.shape[0]
    nt = D_FF // CONV_TN

    def body(ug_ref, uv_ref, wg_ref, wv_ref, bg_ref, bv_ref, da_ref, dug_ref, dwg_ref, dbg_ref, duv_ref, dwv_ref, dbv_ref):
        row = lax.broadcasted_iota(jnp.int32, (t, CONV_TN), 0)
        ug, uv = ug_ref[...], uv_ref[...]
        ug_m, ug_p = _shift_rows(ug, row, t, -1), _shift_rows(ug, row, t, 1)
        uv_m, uv_p = _shift_rows(uv, row, t, -1), _shift_rows(uv, row, t, 1)
        gate = bg_ref[...] + ug_m * wg_ref[0:1, :] + ug * wg_ref[1:2, :] + ug_p * wg_ref[2:3, :]
        val = bv_ref[...] + uv_m * wv_ref[0:1, :] + uv * wv_ref[1:2, :] + uv_p * wv_ref[2:3, :]
        sg = _sigmoid(gate)
        da_ = da_ref[...]

        def emit(dacc, um, uu, up, w_ref, du_ref, dw_ref, db_ref):
            du_ref[...] = _bf(_shift_rows(dacc, row, t, 1) * w_ref[0:1, :] + dacc * w_ref[1:2, :]
                              + _shift_rows(dacc, row, t, -1) * w_ref[2:3, :])
            dw_ref[0:1, :] = jnp.sum(dacc * um, axis=0, keepdims=True)
            dw_ref[1:2, :] = jnp.sum(dacc * uu, axis=0, keepdims=True)
            dw_ref[2:3, :] = jnp.sum(dacc * up, axis=0, keepdims=True)
            db_ref[...] = jnp.sum(dacc, axis=0, keepdims=True)

        emit(da_ * val * (sg * (1.0 + gate * (1.0 - sg))), ug_m, ug, ug_p, wg_ref, dug_ref, dwg_ref, dbg_ref)
        emit(da_ * gate * sg, uv_m, uv, uv_p, wv_ref, duv_ref, dwv_ref, dbv_ref)

    col = lambda rows, off: pl.BlockSpec((rows, CONV_TN), lambda j: (0, j + off))
    half_shapes = [jax.ShapeDtypeStruct((t, D_FF), BF16), jax.ShapeDtypeStruct((3, D_FF), F32),
                   jax.ShapeDtypeStruct((1, D_FF), F32)]
    outs, got = _call(
        body, (u, u, conv_w, conv_w, conv_b, conv_b, da), name="conv_gate_bwd", grid=(nt,),
        in_specs=[col(t, 0), col(t, nt), col(3, 0), col(3, nt), col(1, 0), col(1, nt), col(t, 0)],
        out_specs=[col(t, 0), col(3, 0), col(1, 0)] * 2, out_shape=half_shapes * 2, comm=comm)
    return outs[:3], outs[3:], got


def _row_tile(r, cap):
    best = r
    for cand in range(16, cap + 1, 16):
        if r % cand == 0:
            best = cand
    return best


def _sum_parts(parts, name, tr=256):
    cols = list(parts) if isinstance(parts, (list, tuple)) else [parts]
    _, r, c = cols[0].shape
    tr = _row_tile(r, tr)

    def body(*refs):
        o_ref = refs[-1]
        for k, p_ref in enumerate(refs[:-1]):
            acc = p_ref[0].astype(F32)
            for i in range(1, N_DEV):
                acc = acc + p_ref[i].astype(F32)
            o_ref[:, k * c:(k + 1) * c] = acc

    return pl.pallas_call(
        body, name=name, grid=(r // tr,),
        in_specs=[pl.BlockSpec((N_DEV, tr, c), lambda i: (0, i, 0))] * len(cols),
        out_specs=pl.BlockSpec((tr, c * len(cols)), lambda i: (i, 0)),
        out_shape=jax.ShapeDtypeStruct((r, c * len(cols)), F32),
        compiler_params=_params(),
    )(*cols)


def _adamw(w, g, m, v, name, tr=256):
    r, c = w.shape
    tr = _row_tile(r, tr)

    def body(w_ref, g_ref, m_ref, v_ref, d_ref, nm_ref, nv_ref):
        g_ = g_ref[...]
        m_ = ADAM_B1 * m_ref[...] + (1.0 - ADAM_B1) * g_
        v_ = ADAM_B2 * v_ref[...] + (1.0 - ADAM_B2) * (g_ * g_)
        m_hat = m_ / (1.0 - ADAM_B1 ** ADAM_STEP)
        v_hat = v_ / (1.0 - ADAM_B2 ** ADAM_STEP)
        d_ref[...] = -ADAM_LR * (m_hat / (jnp.sqrt(v_hat) + ADAM_EPS) + ADAM_WD * w_ref[...])
        nm_ref[...] = m_
        nv_ref[...] = v_

    blk = pl.BlockSpec((tr, c), lambda i: (i, 0))
    return pl.pallas_call(
        body, name=name, grid=(r // tr,),
        in_specs=[blk] * 4, out_specs=[blk] * 3,
        out_shape=[jax.ShapeDtypeStruct((r, c), F32)] * 3,
        compiler_params=_params(),
    )(w, g, m, v)


def _mesh_pos():
    return lax.axis_index("x"), lax.axis_index("y"), lax.axis_index("c")


def _dev_index(px, py, pc):
    return 4 * px + 2 * py + pc


class _Gather:
    def __init__(self, arrs):
        self.arrs = list(arrs)
        n = len(self.arrs)
        self.out_shape = [jax.ShapeDtypeStruct((N_DEV,) + a.shape, a.dtype) for a in self.arrs]
        self.sems = [pltpu.SemaphoreType.DMA((7, n)), pltpu.SemaphoreType.DMA((7, n)), pltpu.SemaphoreType.DMA((n,))]

    def _ctx(self, ins, outs, sems):
        send_sems, recv_sems, local_sems = sems
        x, y, c = _mesh_pos()
        chips = [(1 - x, y), (x, 1 - y), (1 - x, 1 - y)]

        def copy(k, a, block, to, src=None):
            dst = outs[a].at[_dev_index(*block)]
            return pltpu.make_async_remote_copy(
                src_ref=dst if src is None else src, dst_ref=dst, send_sem=send_sems.at[k, a], recv_sem=recv_sems.at[k, a],
                device_id=to, device_id_type=MESH_T)

        n = len(ins)
        me, sibling = (x, y, c), (x, y, 1 - c)
        mine = [pltpu.make_async_copy(ins[a], outs[a].at[_dev_index(*me)], local_sems.at[a]) for a in range(n)]
        first = [copy(0, a, me, sibling, src=ins[a]) for a in range(n)]
        first += [copy(1 + j, a, me, (*chip, c), src=ins[a]) for j, chip in enumerate(chips) for a in range(n)]
        passed = [copy(4 + j, a, (*chip, c), sibling) for j, chip in enumerate(chips) for a in range(n)]
        return n, c, me, sibling, chips, copy, mine, first, passed

    def start(self, ins, outs, sems):
        _, _, _, _, _, _, mine, first, _ = self._ctx(ins, outs, sems)
        for cp in mine + first:
            cp.start()

    def forward(self, ins, outs, sems):
        n, c, me, _, chips, copy, _, _, passed = self._ctx(ins, outs, sems)
        for j, chip in enumerate(chips):
            for a in range(n):
                copy(1 + j, a, (*chip, c), me).wait_recv()
                passed[j * n + a].start()

    def finish(self, ins, outs, sems):
        n, c, me, sibling, chips, copy, mine, first, passed = self._ctx(ins, outs, sems)
        for a in range(n):
            copy(0, a, sibling, me).wait_recv()
        for j, chip in enumerate(chips):
            for a in range(n):
                copy(4 + j, a, (*chip, 1 - c), me).wait_recv()
        for cp in first + passed:
            cp.wait_send()
        for cp in mine:
            cp.wait()


class _Exchange:
    def __init__(self, arrs):
        self.arrs = list(arrs)
        n = len(self.arrs)
        self.out_shape = [jax.ShapeDtypeStruct(a.shape, a.dtype) for a in self.arrs]
        self.sems = [pltpu.SemaphoreType.DMA((7, n)), pltpu.SemaphoreType.DMA((7, n)), pltpu.SemaphoreType.DMA((n,))]

    def _copies(self, ins, outs, sems):
        send_sems, recv_sems, local_sems = sems
        x, y, c = _mesh_pos()
        me_i = _dev_index(x, y, c)
        n = len(ins)
        mine = [pltpu.make_async_copy(ins[a].at[me_i], outs[a].at[me_i], local_sems.at[a]) for a in range(n)]
        copies = []
        for k in range(1, N_DEV):
            px = 1 - x if k & 4 else x
            py = 1 - y if k & 2 else y
            pc = 1 - c if k & 1 else c
            peer_i = _dev_index(px, py, pc)
            for a in range(n):
                copies.append(pltpu.make_async_remote_copy(
                    src_ref=ins[a].at[peer_i], dst_ref=outs[a].at[me_i], send_sem=send_sems.at[k - 1, a],
                    recv_sem=recv_sems.at[k - 1, a], device_id=(px, py, pc), device_id_type=MESH_T))
        return mine, copies

    def start(self, ins, outs, sems):
        mine, copies = self._copies(ins, outs, sems)
        for cp in mine + copies:
            cp.start()

    forward = None

    def finish(self, ins, outs, sems):
        mine, copies = self._copies(ins, outs, sems)
        for cp in copies:
            cp.wait_recv()
        for cp in copies:
            cp.wait_send()
        for cp in mine:
            cp.wait()


def _comm_alone(comm, name):
    n = len(comm.arrs)

    def body(*refs):
        ins, outs, sems = refs[:n], refs[n:2 * n], refs[2 * n:]
        comm.start(ins, outs, sems)
        if comm.forward is not None:
            comm.forward(ins, outs, sems)
        comm.finish(ins, outs, sems)

    any_spec = pl.BlockSpec(memory_space=pl.ANY)
    return pl.pallas_call(body, name=name, in_specs=[any_spec] * n, out_specs=[any_spec] * n, out_shape=comm.out_shape,
                          scratch_shapes=comm.sems)(*comm.arrs)


def _peers(x, y, c):
    out = []
    for k in range(1, N_DEV):
        pos = (1 - x if k & 4 else x, 1 - y if k & 2 else y, 1 - c if k & 1 else c)
        out.append((k, pos, _dev_index(*pos)))
    return out


def _exchange_begin(arrs, tag):
    n = len(arrs)

    def own_body(*refs):
        ins, outs, sems = refs[:n], refs[n:2 * n], refs[2 * n]
        me_i = _dev_index(*_mesh_pos())
        cps = [pltpu.make_async_copy(ins[a].at[me_i], outs[a].at[me_i], sems.at[a]) for a in range(n)]
        for cp in cps:
            cp.start()
        for cp in cps:
            cp.wait()

    any_spec = pl.BlockSpec(memory_space=pl.ANY)
    lands = pl.pallas_call(own_body, name="own_slots_" + tag, in_specs=[any_spec] * n, out_specs=[any_spec] * n,
                           out_shape=[jax.ShapeDtypeStruct(a.shape, a.dtype) for a in arrs],
                           scratch_shapes=[pltpu.SemaphoreType.DMA((n,))])(*arrs)

    def start_body(*refs):
        ins, land = refs[:n], refs[n:2 * n]
        send_sems, recv_sems, token = refs[2 * n], refs[2 * n + 1], refs[-1]
        x, y, c = _mesh_pos()
        me_i = _dev_index(x, y, c)
        for k, pos, peer_i in _peers(x, y, c):
            for a in range(n):
                pltpu.make_async_remote_copy(
                    src_ref=ins[a].at[peer_i], dst_ref=land[a].at[me_i], send_sem=send_sems.at[(k - 1) * n + a],
                    recv_sem=recv_sems.at[(k - 1) * n + a], device_id=pos, device_id_type=MESH_T).start()
        token[...] = jnp.zeros_like(token)

    hbm = pl.BlockSpec(memory_space=pltpu.HBM)
    sem = pl.BlockSpec(memory_space=pltpu.SEMAPHORE)
    thru = [pltpu.HBM(a.shape, a.dtype) for a in arrs]
    outs = pl.pallas_call(
        start_body, name="exchange_start_" + tag,
        out_shape=[pltpu.SemaphoreType.DMA((7 * n,)), pltpu.SemaphoreType.DMA((7 * n,))] + thru + thru
        + [jax.ShapeDtypeStruct((8, LANE), F32)],
        in_specs=[hbm] * (2 * n), out_specs=[sem, sem] + [hbm] * (2 * n) + [pl.BlockSpec(memory_space=pltpu.VMEM)],
        input_output_aliases={i: 2 + i for i in range(2 * n)},
        compiler_params=pltpu.CompilerParams(has_side_effects=pltpu.SideEffectType.DATAFLOW_SIDE_EFFECTING),
    )(*[pltpu.with_memory_space_constraint(a, pltpu.HBM) for a in list(arrs) + list(lands)])
    return (tag, n, outs[0], outs[1], outs[2:2 + n], outs[2 + n:2 + 2 * n]), outs[-1][0, 0]


def _exchange_end(handle, after):
    tag, n, send_sems, recv_sems, srcs, lands = handle

    def body(*refs):
        ins, land = refs[:n], refs[n:2 * n]
        send_sems_, recv_sems_ = refs[2 * n], refs[2 * n + 1]
        x, y, c = _mesh_pos()
        me_i = _dev_index(x, y, c)
        for k, pos, peer_i in _peers(x, y, c):
            for a in range(n):
                cp = pltpu.make_async_remote_copy(
                    src_ref=ins[a].at[peer_i], dst_ref=land[a].at[me_i], send_sem=send_sems_.at[(k - 1) * n + a],
                    recv_sem=recv_sems_.at[(k - 1) * n + a], device_id=pos, device_id_type=MESH_T)
                cp.wait_send()
                cp.wait_recv()

    hbm = pl.BlockSpec(memory_space=pltpu.HBM)
    sem = pl.BlockSpec(memory_space=pltpu.SEMAPHORE)
    outs = pl.pallas_call(
        body, name="exchange_end_" + tag, out_shape=[pltpu.HBM(a.shape, a.dtype) for a in list(srcs) + list(lands)],
        in_specs=[hbm] * (2 * n) + [sem, sem, pl.BlockSpec(memory_space=pl.ANY)], out_specs=[hbm] * (2 * n),
        input_output_aliases={i: i for i in range(2 * n)},
        compiler_params=pltpu.CompilerParams(has_side_effects=pltpu.SideEffectType.DATAFLOW_SIDE_EFFECTING),
    )(*srcs, *lands, send_sems, recv_sems, after)
    return list(outs[n:])


def _call(body, operands, *, name, grid, in_specs, out_specs, out_shape, scratch_shapes=(), comm=None):
    if comm is None:
        outs = pl.pallas_call(body, name=name, grid=grid, in_specs=in_specs, out_specs=out_specs, out_shape=out_shape,
                              scratch_shapes=list(scratch_shapes), compiler_params=_params())(*operands)
        return list(outs), []
    ni, no, ns = len(in_specs), len(out_specs), len(scratch_shapes)
    ci, co = len(comm.arrs), len(comm.out_shape)
    total = 1
    for gdim in grid:
        total *= gdim

    def hosted(*refs):
        o0 = ni + ci
        s0 = o0 + no + co
        ins, cins = refs[:ni], refs[ni:o0]
        outs, couts = refs[o0:o0 + no], refs[o0 + no:s0]
        scr, csems = refs[s0:s0 + ns], refs[s0 + ns:]
        step = 0
        for axis, gdim in enumerate(grid):
            step = step * gdim + pl.program_id(axis)

        @pl.when(step == 0)
        def _():
            comm.start(cins, couts, csems)

        if comm.forward is not None:
            @pl.when(step == (3 * total) // 4)
            def _():
                comm.forward(cins, couts, csems)

        body(*ins, *outs, *scr)

        @pl.when(step == total - 1)
        def _():
            comm.finish(cins, couts, csems)

    any_spec = pl.BlockSpec(memory_space=pl.ANY)
    res = pl.pallas_call(
        hosted, name=name, grid=grid, in_specs=list(in_specs) + [any_spec] * ci, out_specs=list(out_specs) + [any_spec] * co,
        out_shape=list(out_shape) + comm.out_shape, scratch_shapes=list(scratch_shapes) + comm.sems,
        compiler_params=_params())(*operands, *comm.arrs)
    return list(res[:no]), list(res[no:])


def _rows(a):
    return a.reshape(-1, a.shape[-1])


def _slots(a):
    return a.reshape(N_DEV, -1, a.shape[-1])


def _local_step(x, mem, tgt, w, wire):
    tables = _rope_tables(x.shape[0])
    bd = _head_sum_matrix()
    two = lambda g: jnp.tile(g, (1, 2))
    gq2, gk2 = two(w["q_norm_g"]), two(w["k_norm_g"])

    w_in_t, hg_lb, conv_w = _comm_alone(_Gather([wire["w_in"], w["hg_lb"].reshape(4, -1), w["conv_w"][0]]), "gather_w_in")
    w_in_t = _rows(w_in_t)
    hg_lb = jnp.transpose(hg_lb.reshape(N_DEV, 2, 2, -1), (1, 2, 0, 3)).reshape(2, 2, HG_DIM)
    conv_w = jnp.transpose(conv_w, (1, 0, 2)).reshape(3, 2 * D_FF)
    lb_a0, lb_a1 = hg_lb[:, 0, :], hg_lb[:, 1, :]

    h1, p, (w_out, w_xq) = _norm_mm(x, w["pre_mix_g"], w_in_t, True, 1664, "in_proj", _Gather([wire["w_out"], wire["w_xq"]]))
    qr, kr = _qk_prep(p, gq2, gk2, tables, bd)
    (att, att32, lse), (w_up_t,) = _attn_fwd(qr, kr, p, _Gather([wire["w_up"]]))
    lb = _lower_bounds(lb_a0, lb_a1)
    (of, s_f), (w_xkv_t, w_xo) = _hgrn_fwd(p, lb[0:1], False, _Gather([wire["w_xkv"], wire["w_xo"]]))
    (ob, s_b), (w_down,) = _hgrn_fwd(p, lb[1:2], True, _Gather([wire["w_down"]]))
    w_up_t, w_out, w_xq, w_xkv_t, w_xo, w_down = (_rows(a) for a in (w_up_t, w_out, w_xq, w_xkv_t, w_xo, w_down))
    (cat, mixed, x1), _ = _mix_out(att, of, ob, p, w["hg_out_norm_g"], w_out, w["post_mix_g"], x)

    h2, q2 = _norm_mm(x1, w["pre_x_g"], w_xq, False, 1024, "xq_proj")
    mn, kv = _norm_mm(mem, w["mem_norm_g"], w_xkv_t, True, 1024, "xkv_proj")
    o2 = _xattn_fwd(q2, kv)
    y2, x2 = _mm_postnorm_res(o2, w_xo, w["post_x_g"], x1, "xo_proj")

    h3, u = _norm_mm(x2, w["pre_ffn_g"], w_up_t, True, 1408, "up_proj")
    a = _conv_gate_fwd(u, conv_w, w["conv_b"])
    y3, dx3, loss = _mm_postnorm_res_loss(a, w_down, w["post_ffn_g"], x2, tgt, "down_proj")

    g, pending = {}, {}
    dy3, da, g["post_ffn_g"] = _postnorm_bwd_mm(dx3, y3, w["post_ffn_g"], w_down, "down_bwd")
    pending["down"], zero = _exchange_begin([_slots(_dw(a, dy3, "dw_down"))], "down")
    (du_g, dcw_g, dcb_g), (du_v, dcw_v, dcb_v), _ = _conv_gate_bwd(u, conv_w, w["conv_b"] + zero, da)
    g["conv_w"] = jnp.concatenate([dcw_g, dcw_v], axis=1)
    g["conv_b"] = jnp.concatenate([dcb_g, dcb_v], axis=1)
    pending["up"], zero = _exchange_begin([_slots(_dw([du_g, du_v], h3, "dw_up"))], "up")
    (dx2, g["pre_ffn_g"]), _ = _mm_prenorm_bwd([du_g, du_v], w_up_t, True, x2, w["pre_ffn_g"] + zero, dx3, 1408, "up_bwd")

    dy2, do2, g["post_x_g"] = _postnorm_bwd_mm(dx2, y2, w["post_x_g"], w_xo, "xo_bwd")
    dw_xo = _dw(o2, dy2, "dw_xo")
    dq2, dkv = _xattn_bwd(q2, kv, do2)
    dw_xq = _dw(h2, dq2, "dw_xq")
    dkvb = _bf(dkv)
    dw_xkv = _dw(dkvb, mn, "dw_xkv")
    pending["x"], zero = _exchange_begin([_slots(dw_xo), _slots(dw_xq), _slots(dw_xkv)], "x")
    (_, g["mem_norm_g"]), _ = _mm_prenorm_bwd(dkvb, w_xkv_t, True, mem, w["mem_norm_g"] + zero, jnp.zeros_like(mem), 2048,
                                              "xkv_bwd")
    (dx1, g["pre_x_g"]), _ = _mm_prenorm_bwd(dq2, w_xq, False, x1, w["pre_x_g"], dx2, 1024, "xq_bwd")

    dmixed, dcat, g["post_mix_g"] = _postnorm_bwd_mm(dx1, mixed, w["post_mix_g"], w_out, "out_bwd")
    pending["out"], zero = _exchange_begin([_slots(_dw(cat, dmixed, "dw_out"))], "out")
    do, dhg, g["hg_out_norm_g"] = _rec_bwd(dcat, of, ob, p, w["hg_out_norm_g"] + zero)
    (dhq_f, dz_f, dhi_f, dlb_f), _ = _hgrn_bwd(p, lb[0:1], do, s_f, False)
    (dhq_b, dz_b, dhi_b, dlb_b), _ = _hgrn_bwd(p, lb[1:2], do, s_b, True)
    d_a0, d_a1 = _lower_bounds_bwd(lb_a0, lb_a1, jnp.concatenate([dlb_f, dlb_b], axis=0))
    g["hg_lb"] = jnp.stack([d_a0, d_a1], axis=1)
    (dqr, dkr, dv), _ = _attn_bwd(qr, kr, p, dcat, att32, lse)
    dp_qk, dgq, dgk = _qk_prep_bwd(p, dqr, dkr, gq2, gk2, tables, bd)
    g["q_norm_g"], g["k_norm_g"] = dgq, dgk
    dp = _assemble_dp(dp_qk, dv, dhq_f, dhq_b, dz_f, dz_b, dhi_f, dhi_b, dhg)
    pending["in"], zero = _exchange_begin([_slots(_dw(dp, h1, "dw_in"))], "in")
    (dx, g["pre_mix_g"]), _ = _mm_prenorm_bwd(dp, w_in_t, True, x, w["pre_mix_g"] + zero, dx1, 1664, "in_bwd")
    return loss, dx, g, pending


_COL_SHARDED = ("w_in", "w_xkv", "w_up")
_ROW_SHARDED = ("w_out", "w_xq", "w_xo", "w_down")
_REPLICATED = ("pre_mix_g", "q_norm_g", "k_norm_g", "hg_out_norm_g", "post_mix_g", "pre_x_g", "mem_norm_g", "post_x_g",
               "pre_ffn_g", "conv_b", "post_ffn_g")
_WEIGHTS = ("pre_mix_g", "w_in", "q_norm_g", "k_norm_g", "hg_lb", "hg_out_norm_g", "w_out", "post_mix_g", "pre_x_g",
            "mem_norm_g", "w_xq", "w_xkv", "w_xo", "post_x_g", "pre_ffn_g", "w_up", "conv_w", "conv_b", "w_down",
            "post_ffn_g")
_ADAM_TRANSPOSED = ("w_in", "w_up")
PACK_W = 1024


def _small_plan(shapes):
    plan, r = [], 0
    for vi, (rows, cols) in enumerate(shapes):
        for i in range(rows):
            for c0 in range(0, cols, PACK_W):
                plan.append((vi, i, c0, min(PACK_W, cols - c0), r))
                r += 1
    return plan, -(-r // 8) * 8


def _pack_small(vals):
    plan, nrows = _small_plan([val.shape for val in vals])

    def body(*refs):
        ins, out = refs[:-1], refs[-1]
        out[...] = jnp.zeros_like(out)
        for vi, i, c0, width, r in plan:
            out[r:r + 1, 0:width] = ins[vi][i:i + 1, c0:c0 + width]

    return pl.pallas_call(body, name="pack_small", out_shape=jax.ShapeDtypeStruct((nrows, PACK_W), F32))(*vals)


def _sum_unpack_small(packs, shapes):
    plan, _ = _small_plan(shapes)

    def body(*refs):
        p_ref, outs = refs[0], refs[1:]
        acc = p_ref[0]
        for i in range(1, N_DEV):
            acc = acc + p_ref[i]
        for vi, i, c0, width, r in plan:
            outs[vi][i:i + 1, c0:c0 + width] = acc[r:r + 1, 0:width]

    return pl.pallas_call(body, name="sum_unpack_small", out_shape=[jax.ShapeDtypeStruct(s, F32) for s in shapes])(packs)


def _adamw_many(ws, gs, ms, vs):
    n = len(ws)

    def body(*refs):
        w_refs, g_refs, m_refs, v_refs = (refs[k * n:(k + 1) * n] for k in range(4))
        d_refs, nm_refs, nv_refs = (refs[(4 + k) * n:(5 + k) * n] for k in range(3))
        for k in range(n):
            g_ = g_refs[k][...]
            m_ = ADAM_B1 * m_refs[k][...] + (1.0 - ADAM_B1) * g_
            v_ = ADAM_B2 * v_refs[k][...] + (1.0 - ADAM_B2) * (g_ * g_)
            m_hat = m_ / (1.0 - ADAM_B1 ** ADAM_STEP)
            v_hat = v_ / (1.0 - ADAM_B2 ** ADAM_STEP)
            d_refs[k][...] = -ADAM_LR * (m_hat / (jnp.sqrt(v_hat) + ADAM_EPS) + ADAM_WD * w_refs[k][...])
            nm_refs[k][...] = m_
            nv_refs[k][...] = v_

    shapes = [jax.ShapeDtypeStruct(a.shape, F32) for a in ws]
    outs = pl.pallas_call(body, name="adamw_small", out_shape=shapes * 3)(*ws, *gs, *ms, *vs)
    return outs[:n], outs[n:2 * n], outs[2 * n:]


def kernel(x, mem, pre_mix_g, w_in, q_norm_g, k_norm_g, hg_lb, hg_out_norm_g, w_out, post_mix_g, pre_x_g, mem_norm_g, w_xq, w_xkv, w_xo, post_x_g, pre_ffn_g, w_up, conv_w, conv_b, w_down, post_ffn_g, loss_target, m_pre_mix_g, m_w_in, m_q_norm_g, m_k_norm_g, m_hg_lb, m_hg_out_norm_g, m_w_out, m_post_mix_g, m_pre_x_g, m_mem_norm_g, m_w_xq, m_w_xkv, m_w_xo, m_post_x_g, m_pre_ffn_g, m_w_up, m_conv_w, m_conv_b, m_w_down, m_post_ffn_g, v_pre_mix_g, v_w_in, v_q_norm_g, v_k_norm_g, v_hg_lb, v_hg_out_norm_g, v_w_out, v_post_mix_g, v_pre_x_g, v_mem_norm_g, v_w_xq, v_w_xkv, v_w_xo, v_post_x_g, v_pre_ffn_g, v_w_up, v_conv_w, v_conv_b, v_w_down, v_post_ffn_g):
    args = dict(locals())
    w = {n: args[n] for n in _WEIGHTS}
    m = {n: args["m_" + n] for n in _WEIGHTS}
    v = {n: args["v_" + n] for n in _WEIGHTS}
    me = _dev_index(*_mesh_pos())

    wire = {n: _bf(w[n][0].T) for n in _COL_SHARDED}
    wire.update({n: _bf(w[n][0]) for n in _ROW_SHARDED})

    loss, grad_x, g, pending = _local_step(x[0], mem[0], loss_target[0], w, wire)

    grads, delta, new_m, new_v = {}, {}, {}, {}

    def update(n, parts):
        gsum = _sum_parts(parts, "sum_" + n)
        if n in _ADAM_TRANSPOSED:
            grads[n] = gsum.T[None]
            d_, m_, v_ = _adamw(w[n][0].T, gsum, m[n][0].T, v[n][0].T, "adamw_" + n)
            delta[n], new_m[n], new_v[n] = d_.T[None], m_.T[None], v_.T[None]
        else:
            gsum = gsum.T if n in _COL_SHARDED else gsum
            grads[n] = gsum[None]
            d_, m_, v_ = _adamw(w[n][0], gsum, m[n][0], v[n][0], "adamw_" + n)
            delta[n], new_m[n], new_v[n] = d_[None], m_[None], v_[None]

    after = grad_x
    for tag, names in (("down", ["w_down"]), ("up", ["w_up"]), ("x", ["w_xo", "w_xq", "w_xkv"]), ("out", ["w_out"]),
                       ("in", ["w_in"])):
        for n, parts in zip(names, _exchange_end(pending[tag], after)):
            update(n, parts)
            after = new_v[n]

    small = list(_REPLICATED) + ["hg_lb", "conv_w"]
    vals = [g[n] for n in _REPLICATED] + [g["hg_lb"].reshape(4, HG_DIM), g["conv_w"], jnp.pad(loss, ((0, 0), (0, LANE - 1)))]
    shapes = [val.shape for val in vals]
    (packs,) = _comm_alone(_Gather([_pack_small(vals)]), "gather_small_grads")
    summed = _sum_unpack_small(packs, shapes)
    loss = summed[-1][0, 0]
    for n, s in zip(small, summed[:-1]):
        grads[n] = s
    fold = lambda v2: v2[:, :ATT_HEAD_DIM] + v2[:, ATT_HEAD_DIM:]
    grads["q_norm_g"], grads["k_norm_g"] = fold(grads["q_norm_g"]), fold(grads["k_norm_g"])
    grads["hg_lb"] = lax.dynamic_slice_in_dim(grads["hg_lb"].reshape(2, 2, HG_DIM), me * (HG_DIM // N_DEV),
                                              HG_DIM // N_DEV, axis=2)
    grads["conv_w"] = lax.dynamic_slice_in_dim(grads["conv_w"], me * (2 * D_FF // N_DEV), 2 * D_FF // N_DEV, axis=1)[None]

    flat2 = lambda a: a.reshape(-1, a.shape[-1])
    outs = _adamw_many(*[[flat2(d[n]) for n in small] for d in (w, grads, m, v)])
    for dst, vals in zip((delta, new_m, new_v), outs):
        for n, val in zip(small, vals):
            dst[n] = val.reshape(w[n].shape)

    return (loss, grad_x[None], *[grads[n] for n in _WEIGHTS], *[delta[n] for n in _WEIGHTS],
            *[new_m[n] for n in _WEIGHTS], *[new_v[n] for n in _WEIGHTS])
```

```python
import jax
import jax.numpy as jnp
from jax import lax
from jax.experimental import pallas as pl
from jax.experimental.pallas import tpu as pltpu

F32 = jnp.float32
BF16 = jnp.bfloat16

D_MODEL = 1024
GRID_W = 64
EPS = 1e-6
ATT_HEADS = 8
ATT_HEAD_DIM = 64
ATT_Q_DIM = 512
ATT_KV_DIM = 128
ROPE_THETA = 10000.0
HG_HEADS = 4
HG_DIM = 512
HG_CHUNK = 32
HG_CHUNK_LOG2 = 5
HG_BLOCK = 256
N_IN = 3328
X_HEADS = 4
X_HEAD_DIM = 256
D_FF = 2816
N_DEV = 8
LANE = 128
ADAM_LR = 0.001
ADAM_B1 = 0.9
ADAM_B2 = 0.999
ADAM_EPS = 1e-08
ADAM_WD = 0.01
ADAM_STEP = 10
VMEM_LIMIT = 56 * 1024 * 1024

MESH_T = pl.DeviceIdType.MESH


def _params(**kw):
    return pltpu.CompilerParams(vmem_limit_bytes=VMEM_LIMIT, **kw)


def _dot(a, b, ca, cb):
    return lax.dot_general(a, b, (((ca,), (cb,)), ((), ())), preferred_element_type=F32)


def _bf(x):
    return x.astype(BF16)


def _sigmoid(x):
    return 1.0 / (1.0 + jnp.exp(-x))


def _rms_fwd(x, g):
    r = lax.rsqrt(jnp.mean(x * x, axis=-1, keepdims=True) + EPS)
    return x * r * g


def _rms_bwd(dy, x, g):
    r = lax.rsqrt(jnp.mean(x * x, axis=-1, keepdims=True) + EPS)
    xh = x * r
    dg = jnp.sum(dy * xh, axis=0, keepdims=True)
    t = dy * g
    dx = r * (t - xh * jnp.mean(t * xh, axis=-1, keepdims=True))
    return dx, dg


def _full(shape):
    nd = len(shape)
    return pl.BlockSpec(shape, lambda *a: (0,) * nd)


def _norm_mm(x, g, w, trans, tn, name, comm=None, tm=512):
    t, d = x.shape
    n = w.shape[0] if trans else w.shape[1]
    tm = min(tm, t)

    def body(x_ref, g_ref, w_ref, h_ref, p_ref):
        h = _bf(_rms_fwd(x_ref[...], g_ref[...]))
        h_ref[...] = h
        p_ref[...] = _dot(h, w_ref[...], 1, 1 if trans else 0)

    w_spec = pl.BlockSpec((tn, d), lambda i, j: (j, 0)) if trans else pl.BlockSpec((d, tn), lambda i, j: (0, j))
    (h, p), got = _call(
        body, (x, g, w), name=name, grid=(t // tm, n // tn),
        in_specs=[pl.BlockSpec((tm, d), lambda i, j: (i, 0)), _full((1, d)), w_spec],
        out_specs=[pl.BlockSpec((tm, d), lambda i, j: (i, 0)), pl.BlockSpec((tm, tn), lambda i, j: (i, j))],
        out_shape=[jax.ShapeDtypeStruct((t, d), BF16), jax.ShapeDtypeStruct((t, n), F32)], comm=comm)
    return (h, p, got) if comm is not None else (h, p)


def _mm_postnorm_res(a, w, g, res, name, tm=256):
    t, k = a.shape
    d = w.shape[1]

    def body(a_ref, w_ref, g_ref, res_ref, y_ref, o_ref):
        y = _dot(a_ref[...], w_ref[...], 1, 0)
        y_ref[...] = y
        o_ref[...] = res_ref[...] + _rms_fwd(y, g_ref[...])

    row = lambda width: pl.BlockSpec((tm, width), lambda i: (i, 0))
    return pl.pallas_call(
        body, name=name, grid=(t // tm,),
        in_specs=[row(k), _full((k, d)), _full((1, d)), row(d)],
        out_specs=[row(d), row(d)],
        out_shape=[jax.ShapeDtypeStruct((t, d), F32)] * 2,
        compiler_params=_params(),
    )(a, w, g, res)


def _mm_postnorm_res_loss(a, w, g, res, tgt, name, tm=256):
    t, k = a.shape
    d = w.shape[1]

    def body(a_ref, w_ref, g_ref, res_ref, tgt_ref, y_ref, dout_ref, loss_ref):
        @pl.when(pl.program_id(0) == 0)
        def _():
            loss_ref[...] = jnp.zeros_like(loss_ref)

        y = _dot(a_ref[...], w_ref[...], 1, 0)
        y_ref[...] = y
        diff = res_ref[...] + _rms_fwd(y, g_ref[...]) - tgt_ref[...]
        dout_ref[...] = diff * (1.0 / d)
        part = jnp.sum(jnp.sum(diff * diff, axis=-1, keepdims=True), axis=0, keepdims=True)
        loss_ref[...] += (0.5 / d) * part

    row = lambda width: pl.BlockSpec((tm, width), lambda i: (i, 0))
    return pl.pallas_call(
        body, name=name, grid=(t // tm,),
        in_specs=[row(k), _full((k, d)), _full((1, d)), row(d), row(d)],
        out_specs=[row(d), row(d), _full((1, 1))],
        out_shape=[jax.ShapeDtypeStruct((t, d), F32)] * 2 + [jax.ShapeDtypeStruct((1, 1), F32)],
        compiler_params=_params(),
    )(a, w, g, res, tgt)


def _postnorm_bwd_mm(dout, y, g, w, name, tm=256):
    t, d = y.shape
    k = w.shape[0]

    def body(dout_ref, y_ref, g_ref, w_ref, dy_ref, da_ref, dg_ref):
        @pl.when(pl.program_id(0) == 0)
        def _():
            dg_ref[...] = jnp.zeros_like(dg_ref)

        dy, dg = _rms_bwd(dout_ref[...], y_ref[...], g_ref[...])
        dg_ref[...] += dg
        dyb = _bf(dy)
        dy_ref[...] = dyb
        da_ref[...] = _dot(dyb, w_ref[...], 1, 1)

    row = lambda width: pl.BlockSpec((tm, width), lambda i: (i, 0))
    return pl.pallas_call(
        body, name=name, grid=(t // tm,),
        in_specs=[row(d), row(d), _full((1, d)), _full((k, d))],
        out_specs=[row(d), row(k), _full((1, d))],
        out_shape=[jax.ShapeDtypeStruct((t, d), BF16), jax.ShapeDtypeStruct((t, k), F32), jax.ShapeDtypeStruct((1, d), F32)],
        compiler_params=_params(),
    )(dout, y, g, w)


def _mm_prenorm_bwd(dp, w, trans, x, g, dres, tk, name, comm=None, tm=512):
    dps = list(dp) if isinstance(dp, (list, tuple)) else [dp]
    nparts = len(dps)
    t, n_each = dps[0].shape
    d = x.shape[1]
    tm = min(tm, t)
    nk_each = n_each // tk
    nk = nk_each * nparts

    def body(*refs):
        dp_refs = refs[:nparts]
        w_ref, x_ref, g_ref, dres_ref, dx_ref, dg_ref, acc_ref = refs[nparts:]
        i, kk = pl.program_id(0), pl.program_id(1)

        @pl.when((i == 0) & (kk == 0))
        def _():
            dg_ref[...] = jnp.zeros_like(dg_ref)

        @pl.when(kk == 0)
        def _():
            acc_ref[...] = jnp.zeros_like(acc_ref)

        for part, dp_ref in enumerate(dp_refs):
            @pl.when((kk >= part * nk_each) & (kk < (part + 1) * nk_each))
            def _():
                acc_ref[...] += _dot(dp_ref[...], w_ref[...], 1, 0 if trans else 1)

        @pl.when(kk == nk - 1)
        def _():
            dx, dg = _rms_bwd(acc_ref[...], x_ref[...], g_ref[...])
            dg_ref[...] += dg
            dx_ref[...] = dres_ref[...] + dx

    dp_spec = lambda part: pl.BlockSpec((tm, tk), lambda i, kk: (i, jnp.clip(kk - part * nk_each, 0, nk_each - 1)))
    w_spec = pl.BlockSpec((tk, d), lambda i, kk: (kk, 0)) if trans else pl.BlockSpec((d, tk), lambda i, kk: (0, kk))
    row = pl.BlockSpec((tm, d), lambda i, kk: (i, 0))
    return _call(
        body, (*dps, w, x, g, dres), name=name, grid=(t // tm, nk),
        in_specs=[dp_spec(part) for part in range(nparts)] + [w_spec, row, _full((1, d)), row],
        out_specs=[row, _full((1, d))],
        out_shape=[jax.ShapeDtypeStruct((t, d), F32), jax.ShapeDtypeStruct((1, d), F32)],
        scratch_shapes=[pltpu.VMEM((tm, d), F32)], comm=comm)


def _dw(a, b, name, bcol=None, tka=256):
    parts = list(a) if isinstance(a, (list, tuple)) else [a]
    nparts = len(parts)
    t, ka_each = parts[0].shape
    bci, nb = (0, b.shape[1]) if bcol is None else bcol
    nt_each = ka_each // tka

    def body(*refs):
        a_refs, b_ref, o_ref = refs[:nparts], refs[nparts], refs[nparts + 1]
        i = pl.program_id(0)
        for part, a_ref in enumerate(a_refs):
            @pl.when((i >= part * nt_each) & (i < (part + 1) * nt_each))
            def _():
                o_ref[...] = _bf(_dot(a_ref[...], b_ref[...], 0, 0))

    a_spec = lambda part: pl.BlockSpec((t, tka), lambda i: (0, jnp.clip(i - part * nt_each, 0, nt_each - 1)))
    return pl.pallas_call(
        body, name=name, grid=(nt_each * nparts,),
        in_specs=[a_spec(part) for part in range(nparts)] + [pl.BlockSpec((t, nb), lambda i: (0, bci))],
        out_specs=pl.BlockSpec((tka, nb), lambda i: (i, 0)),
        out_shape=jax.ShapeDtypeStruct((nparts * ka_each, nb), BF16),
        compiler_params=_params(),
    )(*parts, b)


def _rope_tables(t):
    pos = jnp.arange(t)
    r = (pos // GRID_W).astype(F32)
    c = (pos % GRID_W).astype(F32)
    npair = ATT_HEAD_DIM // 4
    inv = jnp.power(ROPE_THETA, -jnp.arange(npair, dtype=F32) / npair)
    ang = jnp.concatenate([r[:, None] * inv, c[:, None] * inv], axis=-1)
    cos = jnp.repeat(jnp.cos(ang), 2, axis=-1)
    sin = jnp.repeat(jnp.sin(ang), 2, axis=-1)
    even = (jnp.arange(ATT_HEAD_DIM) % 2) == 0
    sa = jnp.where(even, -sin, 0.0)
    sb = jnp.where(even, 0.0, sin)
    two = lambda a: jnp.tile(a, (1, 2))
    return two(cos), two(sa), two(sb)


def _head_sum_matrix():
    a = jnp.arange(LANE) // ATT_HEAD_DIM
    return (a[:, None] == a[None, :]).astype(BF16)


def _head_mean(v, bd):
    hi = _bf(v)
    lo = _bf(v - hi.astype(F32))
    return (_dot(hi, bd, 1, 0) + _dot(lo, bd, 1, 0)) * (1.0 / ATT_HEAD_DIM)


def _qk_prep(p, gq, gk, tables, bd, tm=512):
    t = p.shape[0]
    tm = min(tm, t)
    cc, sa, sb = tables

    def body(p_ref, gq_ref, gk_ref, cc_ref, sa_ref, sb_ref, bd_ref, q_ref, k_ref):
        cc_, sa_, sb_, bd_ = cc_ref[...], sa_ref[...], sb_ref[...], bd_ref[...]
        low = lax.broadcasted_iota(jnp.int32, (tm, LANE), 1) < ATT_HEAD_DIM

        def normrope(xs, g):
            xn = xs * lax.rsqrt(_head_mean(xs * xs, bd_) + EPS) * g
            return xn * cc_ + pltpu.roll(xn, LANE - 1, 1) * sa_ + pltpu.roll(xn, 1, 1) * sb_

        for j in range(4):
            y = normrope(p_ref[:, j * LANE:(j + 1) * LANE], gq_ref[...]) * (ATT_HEAD_DIM ** -0.5)
            yr = pltpu.roll(y, ATT_HEAD_DIM, 1)
            if j // 2 == 0:
                h0, h1 = jnp.where(low, y, 0.0), jnp.where(low, yr, 0.0)
            else:
                h0, h1 = jnp.where(low, 0.0, yr), jnp.where(low, 0.0, y)
            q_ref[:, (2 * j) * LANE:(2 * j + 1) * LANE] = _bf(h0)
            q_ref[:, (2 * j + 1) * LANE:(2 * j + 2) * LANE] = _bf(h1)
        k_ref[...] = _bf(normrope(p_ref[:, ATT_Q_DIM:ATT_Q_DIM + LANE], gk_ref[...]))

    row = lambda width: pl.BlockSpec((tm, width), lambda i: (i, 0))
    return pl.pallas_call(
        body, name="qk_prep", grid=(t // tm,),
        in_specs=[row(ATT_Q_DIM + LANE), _full((1, LANE)), _full((1, LANE)), row(LANE), row(LANE), row(LANE),
                  _full((LANE, LANE))],
        out_specs=[row(ATT_HEADS * LANE), row(LANE)],
        out_shape=[jax.ShapeDtypeStruct((t, ATT_HEADS * LANE), BF16), jax.ShapeDtypeStruct((t, LANE), BF16)],
        compiler_params=_params(),
    )(p, gq, gk, cc, sa, sb, bd)


def _qk_prep_bwd(p, dq, dk, gq, gk, tables, bd, tm=512):
    t = p.shape[0]
    tm = min(tm, t)
    cc, sa, sb = tables

    def body(p_ref, dq_ref, dk_ref, gq_ref, gk_ref, cc_ref, sa_ref, sb_ref, bd_ref, dp_ref, dgq_ref, dgk_ref):
        @pl.when(pl.program_id(0) == 0)
        def _():
            dgq_ref[...] = jnp.zeros_like(dgq_ref)
            dgk_ref[...] = jnp.zeros_like(dgk_ref)

        cc_, sa_, sb_, bd_ = cc_ref[...], sa_ref[...], sb_ref[...], bd_ref[...]
        low = lax.broadcasted_iota(jnp.int32, (tm, LANE), 1) < ATT_HEAD_DIM

        def bwd(xs, g, dy):
            r = lax.rsqrt(_head_mean(xs * xs, bd_) + EPS)
            xh = xs * r
            dxn = dy * cc_ + pltpu.roll(dy * sa_, 1, 1) + pltpu.roll(dy * sb_, LANE - 1, 1)
            dg = jnp.sum(dxn * xh, axis=0, keepdims=True)
            tt = dxn * g
            return r * (tt - xh * _head_mean(tt * xh, bd_)), dg

        dgq = jnp.zeros((1, LANE), F32)
        for j in range(4):
            d0 = dq_ref[:, (2 * j) * LANE:(2 * j + 1) * LANE]
            d1 = dq_ref[:, (2 * j + 1) * LANE:(2 * j + 2) * LANE]
            if j // 2 == 0:
                dy = jnp.where(low, d0, pltpu.roll(d1, ATT_HEAD_DIM, 1))
            else:
                dy = jnp.where(low, pltpu.roll(d0, ATT_HEAD_DIM, 1), d1)
            dx, dg = bwd(p_ref[:, j * LANE:(j + 1) * LANE], gq_ref[...], dy * (ATT_HEAD_DIM ** -0.5))
            dp_ref[:, j * LANE:(j + 1) * LANE] = _bf(dx)
            dgq = dgq + dg
        dgq_ref[...] += dgq
        dx, dg = bwd(p_ref[:, ATT_Q_DIM:ATT_Q_DIM + LANE], gk_ref[...], dk_ref[...])
        dp_ref[:, ATT_Q_DIM:ATT_Q_DIM + LANE] = _bf(dx)
        dgk_ref[...] += dg

    row = lambda width: pl.BlockSpec((tm, width), lambda i: (i, 0))
    return pl.pallas_call(
        body, name="qk_prep_bwd", grid=(t // tm,),
        in_specs=[row(ATT_Q_DIM + LANE), row(ATT_HEADS * LANE), row(LANE), _full((1, LANE)), _full((1, LANE)),
                  row(LANE), row(LANE), row(LANE), _full((LANE, LANE))],
        out_specs=[row(ATT_Q_DIM + LANE), _full((1, LANE)), _full((1, LANE))],
        out_shape=[jax.ShapeDtypeStruct((t, ATT_Q_DIM + LANE), BF16), jax.ShapeDtypeStruct((1, LANE), F32),
                   jax.ShapeDtypeStruct((1, LANE), F32)],
        compiler_params=_params(),
    )(p, dq, dk, gq, gk, cc, sa, sb, bd)


def _attn_fwd(q, k, p, comm=None, tq=256):
    t = k.shape[0]
    tq = min(tq, t)
    v_blk = (ATT_Q_DIM + ATT_KV_DIM) // LANE

    def body(q_ref, k_ref, v_ref, o_ref, o32_ref, lse_ref):
        k_ = k_ref[...]
        v = v_ref[...]
        lowk = lax.broadcasted_iota(jnp.int32, (t, LANE), 1) < ATT_HEAD_DIM
        vm = (_bf(jnp.where(lowk, v, 0.0)), _bf(jnp.where(lowk, 0.0, v)))
        for j in range(4):
            kvh = j // 2
            acc = None
            for sub in range(2):
                h = 2 * j + sub
                s = _dot(q_ref[:, h * LANE:(h + 1) * LANE], k_, 1, 1)
                mx = jnp.max(s, axis=-1, keepdims=True)
                e = jnp.exp(s - mx)
                l = jnp.sum(e, axis=-1, keepdims=True)
                lse_ref[h] = mx + jnp.log(l)
                o = _dot(_bf(e), vm[kvh], 1, 0) * (1.0 / l)
                if sub != kvh:
                    o = pltpu.roll(o, ATT_HEAD_DIM, 1)
                acc = o if acc is None else acc + o
            o32_ref[:, j * LANE:(j + 1) * LANE] = acc
            o_ref[:, j * LANE:(j + 1) * LANE] = _bf(acc)

    row = pl.BlockSpec((tq, ATT_Q_DIM), lambda i: (i, 0))
    outs, got = _call(
        body, (q, k, p), name="attn_fwd", grid=(t // tq,),
        in_specs=[pl.BlockSpec((tq, ATT_HEADS * LANE), lambda i: (i, 0)), _full((t, LANE)),
                  pl.BlockSpec((t, LANE), lambda i: (0, v_blk))],
        out_specs=[row, row, pl.BlockSpec((ATT_HEADS, tq, 1), lambda i: (0, i, 0))],
        out_shape=[jax.ShapeDtypeStruct((t, ATT_Q_DIM), BF16), jax.ShapeDtypeStruct((t, ATT_Q_DIM), F32),
                   jax.ShapeDtypeStruct((ATT_HEADS, t, 1), F32)], comm=comm)
    return outs, got


def _attn_bwd(q, k, p, dcat, o32, lse, comm=None, tq=256):
    t = k.shape[0]
    tq = min(tq, t)
    v_blk = (ATT_Q_DIM + ATT_KV_DIM) // LANE

    def body(q_ref, k_ref, v_ref, do_ref, o_ref, lse_ref, dq_ref, dk_ref, dv_ref):
        @pl.when(pl.program_id(0) == 0)
        def _():
            dk_ref[...] = jnp.zeros_like(dk_ref)
            dv_ref[...] = jnp.zeros_like(dv_ref)

        k_ = k_ref[...]
        vb = _bf(v_ref[...])
        lowq = lax.broadcasted_iota(jnp.int32, (tq, LANE), 1) < ATT_HEAD_DIM
        dk_acc = jnp.zeros((t, LANE), F32)
        dv_acc = jnp.zeros((t, LANE), F32)
        for j in range(4):
            kvh = j // 2
            dop = do_ref[:, j * LANE:(j + 1) * LANE]
            prod = dop * o_ref[:, j * LANE:(j + 1) * LANE]
            d_low = jnp.sum(jnp.where(lowq, prod, 0.0), axis=-1, keepdims=True)
            d_sub = (d_low, jnp.sum(prod, axis=-1, keepdims=True) - d_low)
            for sub in range(2):
                h = 2 * j + sub
                src = dop if sub == kvh else pltpu.roll(dop, ATT_HEAD_DIM, 1)
                do_h = _bf(jnp.where(lowq, src, 0.0) if kvh == 0 else jnp.where(lowq, 0.0, src))
                qh = q_ref[:, h * LANE:(h + 1) * LANE]
                pr = jnp.exp(_dot(qh, k_, 1, 1) - lse_ref[h])
                ds = _bf(pr * (_dot(do_h, vb, 1, 1) - d_sub[sub]))
                dq_ref[:, h * LANE:(h + 1) * LANE] = _dot(ds, k_, 1, 0)
                dk_acc = dk_acc + _dot(ds, qh, 0, 0)
                dv_acc = dv_acc + _dot(_bf(pr), do_h, 0, 0)
        dk_ref[...] += dk_acc
        dv_ref[...] += dv_acc

    row = pl.BlockSpec((tq, ATT_Q_DIM), lambda i: (i, 0))
    return _call(
        body, (q, k, p, dcat, o32, lse), name="attn_bwd", grid=(t // tq,),
        in_specs=[pl.BlockSpec((tq, ATT_HEADS * LANE), lambda i: (i, 0)), _full((t, LANE)),
                  pl.BlockSpec((t, LANE), lambda i: (0, v_blk)), row, row,
                  pl.BlockSpec((ATT_HEADS, tq, 1), lambda i: (0, i, 0))],
        out_specs=[pl.BlockSpec((tq, ATT_HEADS * LANE), lambda i: (i, 0)), _full((t, LANE)), _full((t, LANE))],
        out_shape=[jax.ShapeDtypeStruct((t, ATT_HEADS * LANE), F32), jax.ShapeDtypeStruct((t, LANE), F32),
                   jax.ShapeDtypeStruct((t, LANE), F32)], comm=comm)


def _lower_bounds(a0, a1):
    def body(a0_ref, a1_ref, lb_ref):
        m = jnp.maximum(a0_ref[...], a1_ref[...])
        e0, e1 = jnp.exp(a0_ref[...] - m), jnp.exp(a1_ref[...] - m)
        lb_ref[...] = e0 / (e0 + e1)

    return pl.pallas_call(body, name="lower_bounds", out_shape=jax.ShapeDtypeStruct(a0.shape, F32))(a0, a1)


def _lower_bounds_bwd(a0, a1, dlb):
    def body(a0_ref, a1_ref, dlb_ref, d0_ref, d1_ref):
        m = jnp.maximum(a0_ref[...], a1_ref[...])
        e0, e1 = jnp.exp(a0_ref[...] - m), jnp.exp(a1_ref[...] - m)
        lb = e0 / (e0 + e1)
        d0 = dlb_ref[...] * lb * (1.0 - lb)
        d0_ref[...] = d0
        d1_ref[...] = -d0

    return pl.pallas_call(body, name="lower_bounds_bwd", out_shape=[jax.ShapeDtypeStruct(a0.shape, F32)] * 2)(a0, a1, dlb)


def _chunk_scan(x, pos, down):
    n = x.shape[0]
    s = 1
    while s < HG_CHUNK:
        if down:
            x = x + jnp.where(pos >= s, pltpu.roll(x, s, 0), 0.0)
        else:
            x = x + jnp.where(pos < HG_CHUNK - s, pltpu.roll(x, n - s, 0), 0.0)
        s *= 2
    return x


def _chunk_concat(x, cid, nc):
    return jnp.concatenate([_bf(jnp.where(cid == c, x, 0.0)) for c in range(nc)], axis=1)


def _chunk_pick(x, cid, nc):
    out = jnp.where(cid == 0, x[:, :LANE], 0.0)
    for c in range(1, nc):
        out = out + jnp.where(cid == c, x[:, c * LANE:(c + 1) * LANE], 0.0)
    return out


def _hgrn_gates(hq, z, lb):
    sq = _sigmoid(hq)
    sig = _sigmoid(z)
    f = lb + (1.0 - lb) * sig
    return hq * sq, sq, sig, f, jnp.log(f)


def _hgrn_local(q, f, g, v, pos, cid, amask, rev, nc):
    k = 1.0 - f
    ba = _chunk_scan(g, pos, not rev)
    bb = _chunk_scan(g, pos, rev)
    eq = 0.5 * (ba - bb + g)
    ex_q, ex_k, ex_i, ex_d = jnp.exp(eq), jnp.exp(-eq), jnp.exp(ba), jnp.exp(bb - g)
    qb, kb, qi, kd = q * ex_q, k * ex_k, q * ex_i, k * ex_d
    dvec = jnp.exp(ba + bb - g)
    a = jnp.where(amask, _dot(_bf(qb), _bf(kb), 1, 1), 0.0)
    kd_m = _chunk_concat(kd, cid, nc)
    qi_m = _chunk_concat(qi, cid, nc)
    ut_all = _dot(_bf(v), kd_m, 0, 0)
    return dict(k=k, ex_q=ex_q, ex_k=ex_k, ex_i=ex_i, ex_d=ex_d, qb=qb, kb=kb, qi=qi, kd=kd, dvec=dvec, a=a,
                kd_m=kd_m, qi_m=qi_m, ut_all=ut_all)


def _hgrn_masks(n, rev):
    row = lax.broadcasted_iota(jnp.int32, (n, LANE), 0)
    ti = lax.broadcasted_iota(jnp.int32, (n, n), 0)
    si = lax.broadcasted_iota(jnp.int32, (n, n), 1)
    tri = (si >= ti) if rev else (si <= ti)
    same = jnp.right_shift(ti, HG_CHUNK_LOG2) == jnp.right_shift(si, HG_CHUNK_LOG2)
    return jnp.bitwise_and(row, HG_CHUNK - 1), jnp.right_shift(row, HG_CHUNK_LOG2), same & tri


def _hgrn_specs(t, nb, rev, bwd):
    n = min(HG_BLOCK, t)
    if rev != bwd:
        blk = lambda i: nb - 1 - i
    else:
        blk = lambda i: i
    col = lambda base: pl.BlockSpec((n, 2 * LANE), lambda hp, i: (blk(i), base + hp))
    return n, blk, col


def _hgrn_fwd(p, lb, rev, comm=None):
    t = p.shape[0]
    nb = t // min(HG_BLOCK, t)
    n, blk, col = _hgrn_specs(t, nb, rev, False)
    nc = n // HG_CHUNK
    z_base = 7 if rev else 5

    def body(hq_ref, z_ref, hi_ref, lb_ref, o_ref, ssave_ref, st_scr):
        @pl.when(pl.program_id(1) == 0)
        def _():
            st_scr[...] = jnp.zeros_like(st_scr)

        pos, cid, amask = _hgrn_masks(n, rev)
        order = list(range(nc))[::-1] if rev else list(range(nc))
        for hh in range(2):
            sl = slice(hh * LANE, (hh + 1) * LANE)
            q, _, _, f, g = _hgrn_gates(hq_ref[:, sl], z_ref[:, sl], lb_ref[0:1, sl])
            v = hi_ref[:, sl]
            c_ = _hgrn_local(q, f, g, v, pos, cid, amask, rev, nc)
            st = st_scr[hh]
            ssave_ref[0, sl, :] = st
            cols = [None] * nc
            for c in order:
                cols[c] = st
                st = st * c_["dvec"][c * HG_CHUNK:c * HG_CHUNK + 1, :] + c_["ut_all"][:, c * LANE:(c + 1) * LANE]
            st_scr[hh] = st
            st_all = _bf(jnp.concatenate(cols, axis=1))
            o_ref[:, sl] = _dot(_bf(c_["a"]), _bf(v), 1, 0) + _dot(c_["qi_m"], st_all, 1, 1)

    return _call(
        body, (p, p, p, lb), name="hgrn_fwd_rev" if rev else "hgrn_fwd", grid=(2, nb),
        in_specs=[col(3), col(z_base), col(9), pl.BlockSpec((1, 2 * LANE), lambda hp, i: (0, hp))],
        out_specs=[pl.BlockSpec((n, 2 * LANE), lambda hp, i: (blk(i), hp)),
                   pl.BlockSpec((1, 2 * LANE, LANE), lambda hp, i: (blk(i), hp, 0))],
        out_shape=[jax.ShapeDtypeStruct((t, HG_DIM), F32), jax.ShapeDtypeStruct((nb, HG_DIM, LANE), F32)],
        scratch_shapes=[pltpu.VMEM((2, LANE, LANE), F32)], comm=comm)


def _hgrn_bwd(p, lb, do, ssave, rev, comm=None):
    t = p.shape[0]
    nb = t // min(HG_BLOCK, t)
    n, blk, col = _hgrn_specs(t, nb, rev, True)
    nc = n // HG_CHUNK
    z_base = 7 if rev else 5

    def body(hq_ref, z_ref, hi_ref, lb_ref, do_ref, ssave_ref, dhq_ref, dz_ref, dhi_ref, dlb_ref, dst_scr):
        @pl.when(pl.program_id(1) == 0)
        def _():
            dst_scr[...] = jnp.zeros_like(dst_scr)
            dlb_ref[...] = jnp.zeros_like(dlb_ref)

        pos, cid, amask = _hgrn_masks(n, rev)
        order = list(range(nc))[::-1] if rev else list(range(nc))
        for hh in range(2):
            sl = slice(hh * LANE, (hh + 1) * LANE)
            hq, lbv = hq_ref[:, sl], lb_ref[0:1, sl]
            q, sq, sig, f, g = _hgrn_gates(hq, z_ref[:, sl], lbv)
            v = hi_ref[:, sl]
            c_ = _hgrn_local(q, f, g, v, pos, cid, amask, rev, nc)
            dvec, ut_all = c_["dvec"], c_["ut_all"]
            drow = lambda c: dvec[c * HG_CHUNK:c * HG_CHUNK + 1, :]
            st = ssave_ref[0, sl, :]
            cols = [None] * nc
            for c in order:
                cols[c] = st
                st = st * drow(c) + ut_all[:, c * LANE:(c + 1) * LANE]
            dob, vb = _bf(do_ref[:, sl]), _bf(v)
            gt_all = _dot(dob, c_["qi_m"], 0, 0)
            dcur = dst_scr[hh]
            dnext = [None] * nc
            ddrow = [None] * nc
            for c in order[::-1]:
                dnext[c] = dcur
                ddrow[c] = jnp.sum(cols[c] * dcur, axis=0, keepdims=True) * drow(c)
                dcur = dcur * drow(c) + gt_all[:, c * LANE:(c + 1) * LANE]
            dst_scr[hh] = dcur
            dsn_all = _bf(jnp.concatenate(dnext, axis=1))
            st_all = _bf(jnp.concatenate(cols, axis=1))
            da = _bf(jnp.where(amask, _dot(dob, vb, 1, 1), 0.0))
            dv = _dot(_bf(c_["a"]), dob, 0, 0) + _dot(c_["kd_m"], dsn_all, 1, 1)
            dqb = _dot(da, _bf(c_["kb"]), 1, 0)
            dkb = _dot(da, _bf(c_["qb"]), 0, 0)
            dqi = _chunk_pick(_dot(dob, st_all, 1, 0), cid, nc)
            dkd = _chunk_pick(_dot(vb, dsn_all, 1, 0), cid, nc)
            dq = dqb * c_["ex_q"] + dqi * c_["ex_i"]
            dk = dkb * c_["ex_k"] + dkd * c_["ex_d"]
            e = dqb * c_["qb"] - dkb * c_["kb"] + dqi * c_["qi"]
            w = dkd * c_["kd"]
            dtot = jnp.where(cid == 0, ddrow[0], 0.0)
            for c in range(1, nc):
                dtot = dtot + jnp.where(cid == c, ddrow[c], 0.0)
            dg = _chunk_scan(e, pos, rev) + (_chunk_scan(w, pos, not rev) - w) + dtot
            df = dg / f - dk
            dz_ref[:, sl] = df * (1.0 - lbv) * sig * (1.0 - sig)
            dlb_ref[0:1, sl] += jnp.sum(df * (1.0 - sig), axis=0, keepdims=True)
            dhq_ref[:, sl] = dq * (sq * (1.0 + hq * (1.0 - sq)))
            dhi_ref[:, sl] = dv

    out = pl.BlockSpec((n, 2 * LANE), lambda hp, i: (blk(i), hp))
    return _call(
        body, (p, p, p, lb, do, ssave), name="hgrn_bwd_rev" if rev else "hgrn_bwd", grid=(2, nb),
        in_specs=[col(3), col(z_base), col(9), pl.BlockSpec((1, 2 * LANE), lambda hp, i: (0, hp)), out,
                  pl.BlockSpec((1, 2 * LANE, LANE), lambda hp, i: (blk(i), hp, 0))],
        out_specs=[out, out, out, pl.BlockSpec((1, 2 * LANE), lambda hp, i: (0, hp))],
        out_shape=[jax.ShapeDtypeStruct((t, HG_DIM), F32)] * 3 + [jax.ShapeDtypeStruct((1, HG_DIM), F32)],
        scratch_shapes=[pltpu.VMEM((2, LANE, LANE), F32)], comm=comm)


def _mix_out(att, of, ob, p, gout, w_out, g_post, x, comm=None, tm=256):
    t, d = x.shape
    tm = min(tm, t)

    def body(att_ref, of_ref, ob_ref, hg0_ref, hg1_ref, gout_ref, w_ref, g_ref, x_ref, cat_ref, y_ref, x1_ref):
        cat_ref[:, :ATT_Q_DIM] = att_ref[...]
        for h in range(HG_HEADS):
            sl = slice(h * LANE, (h + 1) * LANE)
            hg_ref = hg0_ref if h < 2 else hg1_ref
            hg = hg_ref[:, (h % 2) * LANE:(h % 2 + 1) * LANE]
            nrm = _rms_fwd(of_ref[:, sl] + ob_ref[:, sl], gout_ref[...])
            cat_ref[:, ATT_Q_DIM + h * LANE:ATT_Q_DIM + (h + 1) * LANE] = _bf(nrm * hg * _sigmoid(hg))
        y = _dot(cat_ref[...], w_ref[...], 1, 0)
        y_ref[...] = y
        x1_ref[...] = x_ref[...] + _rms_fwd(y, g_ref[...])

    row = lambda width: pl.BlockSpec((tm, width), lambda i: (i, 0))
    return _call(
        body, (att, of, ob, p, p, gout, w_out, g_post, x), name="mix_out", grid=(t // tm,),
        in_specs=[row(ATT_Q_DIM), row(HG_DIM), row(HG_DIM), pl.BlockSpec((tm, 2 * LANE), lambda i: (i, 11)),
                  pl.BlockSpec((tm, 2 * LANE), lambda i: (i, 12)), _full((1, LANE)), _full((d, d)), _full((1, d)), row(d)],
        out_specs=[row(d), row(d), row(d)],
        out_shape=[jax.ShapeDtypeStruct((t, d), BF16), jax.ShapeDtypeStruct((t, d), F32), jax.ShapeDtypeStruct((t, d), F32)],
        comm=comm)


def _rec_bwd(dcat, of, ob, p, gout, tm=256):
    t = of.shape[0]
    tm = min(tm, t)

    def body(dc_ref, of_ref, ob_ref, hg0_ref, hg1_ref, gout_ref, do_ref, dhg_ref, dgo_ref):
        @pl.when(pl.program_id(0) == 0)
        def _():
            dgo_ref[...] = jnp.zeros_like(dgo_ref)

        dgo = jnp.zeros((1, LANE), F32)
        for h in range(HG_HEADS):
            sl = slice(h * LANE, (h + 1) * LANE)
            hg_ref = hg0_ref if h < 2 else hg1_ref
            hg = hg_ref[:, (h % 2) * LANE:(h % 2 + 1) * LANE]
            o = of_ref[:, sl] + ob_ref[:, sl]
            sg = _sigmoid(hg)
            drec = dc_ref[:, sl]
            dhg_ref[:, sl] = drec * _rms_fwd(o, gout_ref[...]) * (sg * (1.0 + hg * (1.0 - sg)))
            do, dg = _rms_bwd(drec * hg * sg, o, gout_ref[...])
            do_ref[:, sl] = do
            dgo = dgo + dg
        dgo_ref[...] += dgo

    row = lambda width: pl.BlockSpec((tm, width), lambda i: (i, 0))
    return pl.pallas_call(
        body, name="rec_bwd", grid=(t // tm,),
        in_specs=[pl.BlockSpec((tm, HG_DIM), lambda i: (i, 1)), row(HG_DIM), row(HG_DIM),
                  pl.BlockSpec((tm, 2 * LANE), lambda i: (i, 11)), pl.BlockSpec((tm, 2 * LANE), lambda i: (i, 12)),
                  _full((1, LANE))],
        out_specs=[row(HG_DIM), row(HG_DIM), _full((1, LANE))],
        out_shape=[jax.ShapeDtypeStruct((t, HG_DIM), F32), jax.ShapeDtypeStruct((t, HG_DIM), F32),
                   jax.ShapeDtypeStruct((1, LANE), F32)],
        compiler_params=_params(),
    )(dcat, of, ob, p, p, gout)


def _assemble_dp(dp_qk, dv, dhq_f, dhq_b, dz_f, dz_b, dhi_f, dhi_b, dhg, tm=512):
    t = dv.shape[0]
    tm = min(tm, t)
    qk_w = ATT_Q_DIM + LANE

    def body(qk_ref, dv_ref, hqf_ref, hqb_ref, zf_ref, zb_ref, hif_ref, hib_ref, hg_ref, dp_ref):
        o = 0
        dp_ref[:, o:o + qk_w] = qk_ref[...]
        o += qk_w
        dp_ref[:, o:o + LANE] = _bf(dv_ref[...])
        o += LANE
        for val in (hqf_ref[...] + hqb_ref[...], zf_ref[...], zb_ref[...], hif_ref[...] + hib_ref[...], hg_ref[...]):
            dp_ref[:, o:o + HG_DIM] = _bf(val)
            o += HG_DIM

    row = lambda width: pl.BlockSpec((tm, width), lambda i: (i, 0))
    return pl.pallas_call(
        body, name="assemble_dp", grid=(t // tm,),
        in_specs=[row(qk_w), row(LANE)] + [row(HG_DIM)] * 7,
        out_specs=row(N_IN),
        out_shape=jax.ShapeDtypeStruct((t, N_IN), BF16),
        compiler_params=_params(),
    )(dp_qk, dv, dhq_f, dhq_b, dz_f, dz_b, dhi_f, dhi_b, dhg)


def _xattn_fwd(q, kv, tq=512):
    t, d = q.shape
    tq = min(tq, t)
    nm = kv.shape[0]

    def body(q_ref, kv_ref, o_ref):
        for h in range(X_HEADS):
            sl = slice(h * X_HEAD_DIM, (h + 1) * X_HEAD_DIM)
            s = _dot(_bf(q_ref[:, sl]), _bf(kv_ref[:, sl]), 1, 1) * (X_HEAD_DIM ** -0.5)
            e = jnp.exp(s - jnp.max(s, axis=-1, keepdims=True))
            pr = e * (1.0 / jnp.sum(e, axis=-1, keepdims=True))
            o_ref[:, sl] = _bf(_dot(_bf(pr), _bf(kv_ref[:, d + h * X_HEAD_DIM:d + (h + 1) * X_HEAD_DIM]), 1, 0))

    return pl.pallas_call(
        body, name="xattn_fwd", grid=(t // tq,),
        in_specs=[pl.BlockSpec((tq, d), lambda i: (i, 0)), _full((nm, 2 * d))],
        out_specs=pl.BlockSpec((tq, d), lambda i: (i, 0)),
        out_shape=jax.ShapeDtypeStruct((t, d), BF16),
        compiler_params=_params(),
    )(q, kv)


def _xattn_bwd(q, kv, do, tq=512):
    t, d = q.shape
    tq = min(tq, t)
    nm = kv.shape[0]

    def body(q_ref, kv_ref, do_ref, dq_ref, dkv_ref):
        @pl.when(pl.program_id(0) == 0)
        def _():
            dkv_ref[...] = jnp.zeros_like(dkv_ref)

        for h in range(X_HEADS):
            sl = slice(h * X_HEAD_DIM, (h + 1) * X_HEAD_DIM)
            slv = slice(d + h * X_HEAD_DIM, d + (h + 1) * X_HEAD_DIM)
            qb, kb, vb, dob = _bf(q_ref[:, sl]), _bf(kv_ref[:, sl]), _bf(kv_ref[:, slv]), _bf(do_ref[:, sl])
            s = _dot(qb, kb, 1, 1) * (X_HEAD_DIM ** -0.5)
            e = jnp.exp(s - jnp.max(s, axis=-1, keepdims=True))
            pr = e * (1.0 / jnp.sum(e, axis=-1, keepdims=True))
            dpr = _dot(dob, vb, 1, 1)
            ds = _bf(pr * (dpr - jnp.sum(pr * dpr, axis=-1, keepdims=True)) * (X_HEAD_DIM ** -0.5))
            dq_ref[:, sl] = _bf(_dot(ds, kb, 1, 0))
            dkv_ref[:, sl] += _dot(ds, qb, 0, 0)
            dkv_ref[:, slv] += _dot(_bf(pr), dob, 0, 0)

    return pl.pallas_call(
        body, name="xattn_bwd", grid=(t // tq,),
        in_specs=[pl.BlockSpec((tq, d), lambda i: (i, 0)), _full((nm, 2 * d)), pl.BlockSpec((tq, d), lambda i: (i, 0))],
        out_specs=[pl.BlockSpec((tq, d), lambda i: (i, 0)), _full((nm, 2 * d))],
        out_shape=[jax.ShapeDtypeStruct((t, d), BF16), jax.ShapeDtypeStruct((nm, 2 * d), F32)],
        compiler_params=_params(),
    )(q, kv, do)


CONV_TN = 256


def _shift_rows(u, row, t, delta):
    if delta < 0:
        return jnp.where(row == 0, 0.0, pltpu.roll(u, 1, 0))
    return jnp.where(row == t - 1, 0.0, pltpu.roll(u, t - 1, 0))


def _conv_gate_fwd(u, conv_w, conv_b):
    t = u.shape[0]
    nt = D_FF // CONV_TN

    def body(ug_ref, uv_ref, wg_ref, wv_ref, bg_ref, bv_ref, a_ref):
        row = lax.broadcasted_iota(jnp.int32, (t, CONV_TN), 0)

        def conv(u_ref, w_ref, b_ref):
            uu = u_ref[...]
            return (b_ref[...] + _shift_rows(uu, row, t, -1) * w_ref[0:1, :] + uu * w_ref[1:2, :]
                    + _shift_rows(uu, row, t, 1) * w_ref[2:3, :])

        gate = conv(ug_ref, wg_ref, bg_ref)
        a_ref[...] = _bf(gate * _sigmoid(gate) * conv(uv_ref, wv_ref, bv_ref))

    col = lambda rows, off: pl.BlockSpec((rows, CONV_TN), lambda j: (0, j + off))
    return pl.pallas_call(
        body, name="conv_gate_fwd", grid=(nt,),
        in_specs=[col(t, 0), col(t, nt), col(3, 0), col(3, nt), col(1, 0), col(1, nt)],
        out_specs=col(t, 0),
        out_shape=jax.ShapeDtypeStruct((t, D_FF), BF16),
        compiler_params=_params(),
    )(u, u, conv_w, conv_w, conv_b, conv_b)


def _conv_gate_bwd(u, conv_w, conv_b, da, comm=None):
    t = u.shape[0]
    nt = D_FF // CONV_TN

    def body(ug_ref, uv_ref, wg_ref, wv_ref, bg_ref, bv_ref, da_ref, dug_ref, dwg_ref, dbg_ref, duv_ref, dwv_ref, dbv_ref):
        row = lax.broadcasted_iota(jnp.int32, (t, CONV_TN), 0)
        ug, uv = ug_ref[...], uv_ref[...]
        ug_m, ug_p = _shift_rows(ug, row, t, -1), _shift_rows(ug, row, t, 1)
        uv_m, uv_p = _shift_rows(uv, row, t, -1), _shift_rows(uv, row, t, 1)
        gate = bg_ref[...] + ug_m * wg_ref[0:1, :] + ug * wg_ref[1:2, :] + ug_p * wg_ref[2:3, :]
        val = bv_ref[...] + uv_m * wv_ref[0:1, :] + uv * wv_ref[1:2, :] + uv_p * wv_ref[2:3, :]
        sg = _sigmoid(gate)
        da_ = da_ref[...]

        def emit(dacc, um, uu, up, w_ref, du_ref, dw_ref, db_ref):
            du_ref[...] = _bf(_shift_rows(dacc, row, t, 1) * w_ref[0:1, :] + dacc * w_ref[1:2, :]
                              + _shift_rows(dacc, row, t, -1) * w_ref[2:3, :])
            dw_ref[0:1, :] = jnp.sum(dacc * um, axis=0, keepdims=True)
            dw_ref[1:2, :] = jnp.sum(dacc * uu, axis=0, keepdims=True)
            dw_ref[2:3, :] = jnp.sum(dacc * up, axis=0, keepdims=True)
            db_ref[...] = jnp.sum(dacc, axis=0, keepdims=True)

        emit(da_ * val * (sg * (1.0 + gate * (1.0 - sg))), ug_m, ug, ug_p, wg_ref, dug_ref, dwg_ref, dbg_ref)
        emit(da_ * gate * sg, uv_m, uv, uv_p, wv_ref, duv_ref, dwv_ref, dbv_ref)

    col = lambda rows, off: pl.BlockSpec((rows, CONV_TN), lambda j: (0, j + off))
    half_shapes = [jax.ShapeDtypeStruct((t, D_FF), BF16), jax.ShapeDtypeStruct((3, D_FF), F32),
                   jax.ShapeDtypeStruct((1, D_FF), F32)]
    outs, got = _call(
        body, (u, u, conv_w, conv_w, conv_b, conv_b, da), name="conv_gate_bwd", grid=(nt,),
        in_specs=[col(t, 0), col(t, nt), col(3, 0), col(3, nt), col(1, 0), col(1, nt), col(t, 0)],
        out_specs=[col(t, 0), col(3, 0), col(1, 0)] * 2, out_shape=half_shapes * 2, comm=comm)
    return outs[:3], outs[3:], got


def _row_tile(r, cap):
    best = r
    for cand in range(16, cap + 1, 16):
        if r % cand == 0:
            best = cand
    return best


def _sum_parts(own, got, me, name, tr=256):
    _, r, c = got.shape
    tr = _row_tile(r, tr)

    def body(me_ref, own_ref, got_ref, o_ref):
        mine = own_ref[...].astype(F32)
        acc = None
        for i in range(N_DEV):
            term = jnp.where(me_ref[0] == i, mine, got_ref[i].astype(F32))
            acc = term if acc is None else acc + term
        o_ref[...] = acc

    return pl.pallas_call(
        body, name=name,
        grid_spec=pltpu.PrefetchScalarGridSpec(
            num_scalar_prefetch=1, grid=(r // tr,),
            in_specs=[pl.BlockSpec((None, tr, c), lambda i, me_ref: (me_ref[0], i, 0)),
                      pl.BlockSpec((N_DEV, tr, c), lambda i, me_ref: (0, i, 0))],
            out_specs=pl.BlockSpec((tr, c), lambda i, me_ref: (i, 0))),
        out_shape=jax.ShapeDtypeStruct((r, c), F32),
        compiler_params=_params(),
    )(me, own, got)


def _adamw(w, g, m, v, name, tr=256):
    r, c = w.shape
    tr = _row_tile(r, tr)

    def body(w_ref, g_ref, m_ref, v_ref, d_ref, nm_ref, nv_ref):
        g_ = g_ref[...]
        m_ = ADAM_B1 * m_ref[...] + (1.0 - ADAM_B1) * g_
        v_ = ADAM_B2 * v_ref[...] + (1.0 - ADAM_B2) * (g_ * g_)
        m_hat = m_ / (1.0 - ADAM_B1 ** ADAM_STEP)
        v_hat = v_ / (1.0 - ADAM_B2 ** ADAM_STEP)
        d_ref[...] = -ADAM_LR * (m_hat / (jnp.sqrt(v_hat) + ADAM_EPS) + ADAM_WD * w_ref[...])
        nm_ref[...] = m_
        nv_ref[...] = v_

    blk = pl.BlockSpec((tr, c), lambda i: (i, 0))
    return pl.pallas_call(
        body, name=name, grid=(r // tr,),
        in_specs=[blk] * 4, out_specs=[blk] * 3,
        out_shape=[jax.ShapeDtypeStruct((r, c), F32)] * 3,
        compiler_params=_params(),
    )(w, g, m, v)


def _mesh_pos():
    return lax.axis_index("x"), lax.axis_index("y"), lax.axis_index("c")


def _dev_index(px, py, pc):
    return 4 * px + 2 * py + pc


class _Gather:
    def __init__(self, arrs):
        self.arrs = list(arrs)
        n = len(self.arrs)
        self.out_shape = [jax.ShapeDtypeStruct((N_DEV,) + a.shape, a.dtype) for a in self.arrs]
        self.sems = [pltpu.SemaphoreType.DMA((7, n)), pltpu.SemaphoreType.DMA((7, n)), pltpu.SemaphoreType.DMA((n,))]

    def _ctx(self, ins, outs, sems):
        send_sems, recv_sems, local_sems = sems
        x, y, c = _mesh_pos()
        chips = [(1 - x, y), (x, 1 - y), (1 - x, 1 - y)]

        def copy(k, a, block, to, src=None):
            dst = outs[a].at[_dev_index(*block)]
            return pltpu.make_async_remote_copy(
                src_ref=dst if src is None else src, dst_ref=dst, send_sem=send_sems.at[k, a], recv_sem=recv_sems.at[k, a],
                device_id=to, device_id_type=MESH_T)

        n = len(ins)
        me, sibling = (x, y, c), (x, y, 1 - c)
        mine = [pltpu.make_async_copy(ins[a], outs[a].at[_dev_index(*me)], local_sems.at[a]) for a in range(n)]
        first = [copy(0, a, me, sibling, src=ins[a]) for a in range(n)]
        first += [copy(1 + j, a, me, (*chip, c), src=ins[a]) for j, chip in enumerate(chips) for a in range(n)]
        passed = [copy(4 + j, a, (*chip, c), sibling) for j, chip in enumerate(chips) for a in range(n)]
        return n, c, me, sibling, chips, copy, mine, first, passed

    def start(self, ins, outs, sems):
        _, _, _, _, _, _, mine, first, _ = self._ctx(ins, outs, sems)
        for cp in mine + first:
            cp.start()

    def forward(self, ins, outs, sems):
        n, c, me, _, chips, copy, _, _, passed = self._ctx(ins, outs, sems)
        for j, chip in enumerate(chips):
            for a in range(n):
                copy(1 + j, a, (*chip, c), me).wait_recv()
                passed[j * n + a].start()

    def finish(self, ins, outs, sems):
        n, c, me, sibling, chips, copy, mine, first, passed = self._ctx(ins, outs, sems)
        for a in range(n):
            copy(0, a, sibling, me).wait_recv()
        for j, chip in enumerate(chips):
            for a in range(n):
                copy(4 + j, a, (*chip, 1 - c), me).wait_recv()
        for cp in first + passed:
            cp.wait_send()
        for cp in mine:
            cp.wait()


class _Exchange:
    def __init__(self, arrs):
        self.arrs = list(arrs)
        n = len(self.arrs)
        self.out_shape = [jax.ShapeDtypeStruct(a.shape, a.dtype) for a in self.arrs]
        self.sems = [pltpu.SemaphoreType.DMA((7, n)), pltpu.SemaphoreType.DMA((7, n)), pltpu.SemaphoreType.DMA((n,))]

    def _copies(self, ins, outs, sems):
        send_sems, recv_sems, local_sems = sems
        x, y, c = _mesh_pos()
        me_i = _dev_index(x, y, c)
        n = len(ins)
        mine = [pltpu.make_async_copy(ins[a].at[me_i], outs[a].at[me_i], local_sems.at[a]) for a in range(n)]
        copies = []
        for k in range(1, N_DEV):
            px = 1 - x if k & 4 else x
            py = 1 - y if k & 2 else y
            pc = 1 - c if k & 1 else c
            peer_i = _dev_index(px, py, pc)
            for a in range(n):
                copies.append(pltpu.make_async_remote_copy(
                    src_ref=ins[a].at[peer_i], dst_ref=outs[a].at[me_i], send_sem=send_sems.at[k - 1, a],
                    recv_sem=recv_sems.at[k - 1, a], device_id=(px, py, pc), device_id_type=MESH_T))
        return mine, copies

    def start(self, ins, outs, sems):
        mine, copies = self._copies(ins, outs, sems)
        for cp in mine + copies:
            cp.start()

    forward = None

    def finish(self, ins, outs, sems):
        mine, copies = self._copies(ins, outs, sems)
        for cp in copies:
            cp.wait_recv()
        for cp in copies:
            cp.wait_send()
        for cp in mine:
            cp.wait()


def _comm_alone(comm, name):
    n = len(comm.arrs)

    def body(*refs):
        ins, outs, sems = refs[:n], refs[n:2 * n], refs[2 * n:]
        comm.start(ins, outs, sems)
        if comm.forward is not None:
            comm.forward(ins, outs, sems)
        comm.finish(ins, outs, sems)

    any_spec = pl.BlockSpec(memory_space=pl.ANY)
    return pl.pallas_call(body, name=name, in_specs=[any_spec] * n, out_specs=[any_spec] * n, out_shape=comm.out_shape,
                          scratch_shapes=comm.sems)(*comm.arrs)


def _peers(x, y, c):
    out = []
    for k in range(1, N_DEV):
        pos = (1 - x if k & 4 else x, 1 - y if k & 2 else y, 1 - c if k & 1 else c)
        out.append((k, pos, _dev_index(*pos)))
    return out


def _exchange_begin(arrs, tag):
    n = len(arrs)
    lands = [lax.empty(a.shape, a.dtype) for a in arrs]

    def start_body(*refs):
        ins, land = refs[:n], refs[n:2 * n]
        send_sems, recv_sems, token = refs[2 * n], refs[2 * n + 1], refs[-1]
        x, y, c = _mesh_pos()
        me_i = _dev_index(x, y, c)
        for k, pos, peer_i in _peers(x, y, c):
            for a in range(n):
                pltpu.make_async_remote_copy(
                    src_ref=ins[a].at[peer_i], dst_ref=land[a].at[me_i], send_sem=send_sems.at[(k - 1) * n + a],
                    recv_sem=recv_sems.at[(k - 1) * n + a], device_id=pos, device_id_type=MESH_T).start()
        token[...] = jnp.zeros_like(token)

    hbm = pl.BlockSpec(memory_space=pltpu.HBM)
    sem = pl.BlockSpec(memory_space=pltpu.SEMAPHORE)
    thru = [pltpu.HBM(a.shape, a.dtype) for a in arrs]
    outs = pl.pallas_call(
        start_body, name="exchange_start_" + tag,
        out_shape=[pltpu.SemaphoreType.DMA((7 * n,)), pltpu.SemaphoreType.DMA((7 * n,))] + thru + thru
        + [jax.ShapeDtypeStruct((8, LANE), F32)],
        in_specs=[hbm] * (2 * n), out_specs=[sem, sem] + [hbm] * (2 * n) + [pl.BlockSpec(memory_space=pltpu.VMEM)],
        input_output_aliases={i: 2 + i for i in range(2 * n)},
        compiler_params=pltpu.CompilerParams(has_side_effects=pltpu.SideEffectType.DATAFLOW_SIDE_EFFECTING),
    )(*[pltpu.with_memory_space_constraint(a, pltpu.HBM) for a in list(arrs) + list(lands)])
    return (tag, n, outs[0], outs[1], outs[2:2 + n], outs[2 + n:2 + 2 * n]), outs[-1][0, 0]


def _exchange_end(handle, after):
    tag, n, send_sems, recv_sems, srcs, lands = handle

    def body(*refs):
        ins, land = refs[:n], refs[n:2 * n]
        send_sems_, recv_sems_ = refs[2 * n], refs[2 * n + 1]
        x, y, c = _mesh_pos()
        me_i = _dev_index(x, y, c)
        for k, pos, peer_i in _peers(x, y, c):
            for a in range(n):
                cp = pltpu.make_async_remote_copy(
                    src_ref=ins[a].at[peer_i], dst_ref=land[a].at[me_i], send_sem=send_sems_.at[(k - 1) * n + a],
                    recv_sem=recv_sems_.at[(k - 1) * n + a], device_id=pos, device_id_type=MESH_T)
                cp.wait_send()
                cp.wait_recv()

    hbm = pl.BlockSpec(memory_space=pltpu.HBM)
    sem = pl.BlockSpec(memory_space=pltpu.SEMAPHORE)
    outs = pl.pallas_call(
        body, name="exchange_end_" + tag, out_shape=[pltpu.HBM(a.shape, a.dtype) for a in list(srcs) + list(lands)],
        in_specs=[hbm] * (2 * n) + [sem, sem, pl.BlockSpec(memory_space=pl.ANY)], out_specs=[hbm] * (2 * n),
        input_output_aliases={i: i for i in range(2 * n)},
        compiler_params=pltpu.CompilerParams(has_side_effects=pltpu.SideEffectType.DATAFLOW_SIDE_EFFECTING),
    )(*srcs, *lands, send_sems, recv_sems, after)
    return list(zip(outs[:n], outs[n:]))


def _call(body, operands, *, name, grid, in_specs, out_specs, out_shape, scratch_shapes=(), comm=None):
    if comm is None:
        outs = pl.pallas_call(body, name=name, grid=grid, in_specs=in_specs, out_specs=out_specs, out_shape=out_shape,
                              scratch_shapes=list(scratch_shapes), compiler_params=_params())(*operands)
        return list(outs), []
    ni, no, ns = len(in_specs), len(out_specs), len(scratch_shapes)
    ci, co = len(comm.arrs), len(comm.out_shape)
    total = 1
    for gdim in grid:
        total *= gdim

    def hosted(*refs):
        o0 = ni + ci
        s0 = o0 + no + co
        ins, cins = refs[:ni], refs[ni:o0]
        outs, couts = refs[o0:o0 + no], refs[o0 + no:s0]
        scr, csems = refs[s0:s0 + ns], refs[s0 + ns:]
        step = 0
        for axis, gdim in enumerate(grid):
            step = step * gdim + pl.program_id(axis)

        @pl.when(step == 0)
        def _():
            comm.start(cins, couts, csems)

        if comm.forward is not None:
            @pl.when(step == (3 * total) // 4)
            def _():
                comm.forward(cins, couts, csems)

        body(*ins, *outs, *scr)

        @pl.when(step == total - 1)
        def _():
            comm.finish(cins, couts, csems)

    any_spec = pl.BlockSpec(memory_space=pl.ANY)
    res = pl.pallas_call(
        hosted, name=name, grid=grid, in_specs=list(in_specs) + [any_spec] * ci, out_specs=list(out_specs) + [any_spec] * co,
        out_shape=list(out_shape) + comm.out_shape, scratch_shapes=list(scratch_shapes) + comm.sems,
        compiler_params=_params())(*operands, *comm.arrs)
    return list(res[:no]), list(res[no:])


def _rows(a):
    return a.reshape(-1, a.shape[-1])


def _slots(a):
    return a.reshape(N_DEV, -1, a.shape[-1])


def _local_step(x, mem, tgt, w, wire):
    tables = _rope_tables(x.shape[0])
    bd = _head_sum_matrix()
    two = lambda g: jnp.tile(g, (1, 2))
    gq2, gk2 = two(w["q_norm_g"]), two(w["k_norm_g"])

    w_in_t, hg_lb, conv_w = _comm_alone(_Gather([wire["w_in"], w["hg_lb"].reshape(4, -1), w["conv_w"][0]]), "gather_w_in")
    w_in_t = _rows(w_in_t)
    hg_lb = jnp.transpose(hg_lb.reshape(N_DEV, 2, 2, -1), (1, 2, 0, 3)).reshape(2, 2, HG_DIM)
    conv_w = jnp.transpose(conv_w, (1, 0, 2)).reshape(3, 2 * D_FF)
    lb_a0, lb_a1 = hg_lb[:, 0, :], hg_lb[:, 1, :]

    h1, p, (w_out, w_xq) = _norm_mm(x, w["pre_mix_g"], w_in_t, True, 1664, "in_proj", _Gather([wire["w_out"], wire["w_xq"]]))
    qr, kr = _qk_prep(p, gq2, gk2, tables, bd)
    (att, att32, lse), (w_up_t,) = _attn_fwd(qr, kr, p, _Gather([wire["w_up"]]))
    lb = _lower_bounds(lb_a0, lb_a1)
    (of, s_f), (w_xkv_t, w_xo) = _hgrn_fwd(p, lb[0:1], False, _Gather([wire["w_xkv"], wire["w_xo"]]))
    (ob, s_b), (w_down,) = _hgrn_fwd(p, lb[1:2], True, _Gather([wire["w_down"]]))
    w_up_t, w_out, w_xq, w_xkv_t, w_xo, w_down = (_rows(a) for a in (w_up_t, w_out, w_xq, w_xkv_t, w_xo, w_down))
    (cat, mixed, x1), _ = _mix_out(att, of, ob, p, w["hg_out_norm_g"], w_out, w["post_mix_g"], x)

    h2, q2 = _norm_mm(x1, w["pre_x_g"], w_xq, False, 1024, "xq_proj")
    mn, kv = _norm_mm(mem, w["mem_norm_g"], w_xkv_t, True, 1024, "xkv_proj")
    o2 = _xattn_fwd(q2, kv)
    y2, x2 = _mm_postnorm_res(o2, w_xo, w["post_x_g"], x1, "xo_proj")

    h3, u = _norm_mm(x2, w["pre_ffn_g"], w_up_t, True, 1408, "up_proj")
    a = _conv_gate_fwd(u, conv_w, w["conv_b"])
    y3, dx3, loss = _mm_postnorm_res_loss(a, w_down, w["post_ffn_g"], x2, tgt, "down_proj")

    g, pending = {}, {}
    dy3, da, g["post_ffn_g"] = _postnorm_bwd_mm(dx3, y3, w["post_ffn_g"], w_down, "down_bwd")
    pending["down"], zero = _exchange_begin([_slots(_dw(a, dy3, "dw_down"))], "down")
    (du_g, dcw_g, dcb_g), (du_v, dcw_v, dcb_v), _ = _conv_gate_bwd(u, conv_w, w["conv_b"] + zero, da)
    g["conv_w"] = jnp.concatenate([dcw_g, dcw_v], axis=1)
    g["conv_b"] = jnp.concatenate([dcb_g, dcb_v], axis=1)
    pending["up"], zero = _exchange_begin([_slots(_dw([du_g, du_v], h3, "dw_up"))], "up")
    (dx2, g["pre_ffn_g"]), _ = _mm_prenorm_bwd([du_g, du_v], w_up_t, True, x2, w["pre_ffn_g"] + zero, dx3, 1408, "up_bwd")

    dy2, do2, g["post_x_g"] = _postnorm_bwd_mm(dx2, y2, w["post_x_g"], w_xo, "xo_bwd")
    dw_xo = _dw(o2, dy2, "dw_xo")
    dq2, dkv = _xattn_bwd(q2, kv, do2)
    dw_xq = _dw(h2, dq2, "dw_xq")
    dkvb = _bf(dkv)
    dw_xkv = _dw(dkvb, mn, "dw_xkv")
    pending["x"], zero = _exchange_begin([_slots(dw_xo), _slots(dw_xq), _slots(dw_xkv)], "x")
    (_, g["mem_norm_g"]), _ = _mm_prenorm_bwd(dkvb, w_xkv_t, True, mem, w["mem_norm_g"], jnp.zeros_like(mem), 2048, "xkv_bwd")
    (dx1, g["pre_x_g"]), _ = _mm_prenorm_bwd(dq2, w_xq, False, x1, w["pre_x_g"] + zero, dx2, 1024, "xq_bwd")

    dmixed, dcat, g["post_mix_g"] = _postnorm_bwd_mm(dx1, mixed, w["post_mix_g"], w_out, "out_bwd")
    pending["out"], zero = _exchange_begin([_slots(_dw(cat, dmixed, "dw_out"))], "out")
    do, dhg, g["hg_out_norm_g"] = _rec_bwd(dcat, of, ob, p, w["hg_out_norm_g"] + zero)
    (dhq_f, dz_f, dhi_f, dlb_f), _ = _hgrn_bwd(p, lb[0:1], do, s_f, False)
    (dhq_b, dz_b, dhi_b, dlb_b), _ = _hgrn_bwd(p, lb[1:2], do, s_b, True)
    d_a0, d_a1 = _lower_bounds_bwd(lb_a0, lb_a1, jnp.concatenate([dlb_f, dlb_b], axis=0))
    g["hg_lb"] = jnp.stack([d_a0, d_a1], axis=1)
    (dqr, dkr, dv), _ = _attn_bwd(qr, kr, p, dcat, att32, lse)
    dp_qk, dgq, dgk = _qk_prep_bwd(p, dqr, dkr, gq2, gk2, tables, bd)
    g["q_norm_g"], g["k_norm_g"] = dgq, dgk
    dp = _assemble_dp(dp_qk, dv, dhq_f, dhq_b, dz_f, dz_b, dhi_f, dhi_b, dhg)
    pending["in"], zero = _exchange_begin([_slots(_dw(dp, h1, "dw_in"))], "in")
    (dx, g["pre_mix_g"]), _ = _mm_prenorm_bwd(dp, w_in_t, True, x, w["pre_mix_g"] + zero, dx1, 1664, "in_bwd")
    return loss, dx, g, pending


_COL_SHARDED = ("w_in", "w_xkv", "w_up")
_ROW_SHARDED = ("w_out", "w_xq", "w_xo", "w_down")
_REPLICATED = ("pre_mix_g", "q_norm_g", "k_norm_g", "hg_out_norm_g", "post_mix_g", "pre_x_g", "mem_norm_g", "post_x_g",
               "pre_ffn_g", "conv_b", "post_ffn_g")
_WEIGHTS = ("pre_mix_g", "w_in", "q_norm_g", "k_norm_g", "hg_lb", "hg_out_norm_g", "w_out", "post_mix_g", "pre_x_g",
            "mem_norm_g", "w_xq", "w_xkv", "w_xo", "post_x_g", "pre_ffn_g", "w_up", "conv_w", "conv_b", "w_down",
            "post_ffn_g")
_ADAM_TRANSPOSED = ("w_in", "w_up")
PACK_W = 1024


def _small_plan(shapes):
    plan, r = [], 0
    for vi, (rows, cols) in enumerate(shapes):
        for i in range(rows):
            for c0 in range(0, cols, PACK_W):
                plan.append((vi, i, c0, min(PACK_W, cols - c0), r))
                r += 1
    return plan, -(-r // 8) * 8


def _pack_small(vals):
    plan, nrows = _small_plan([val.shape for val in vals])

    def body(*refs):
        ins, out = refs[:-1], refs[-1]
        out[...] = jnp.zeros_like(out)
        for vi, i, c0, width, r in plan:
            out[r:r + 1, 0:width] = ins[vi][i:i + 1, c0:c0 + width]

    return pl.pallas_call(body, name="pack_small", out_shape=jax.ShapeDtypeStruct((nrows, PACK_W), F32))(*vals)


def _sum_unpack_small(packs, shapes):
    plan, _ = _small_plan(shapes)

    def body(*refs):
        p_ref, outs = refs[0], refs[1:]
        acc = p_ref[0]
        for i in range(1, N_DEV):
            acc = acc + p_ref[i]
        for vi, i, c0, width, r in plan:
            outs[vi][i:i + 1, c0:c0 + width] = acc[r:r + 1, 0:width]

    return pl.pallas_call(body, name="sum_unpack_small", out_shape=[jax.ShapeDtypeStruct(s, F32) for s in shapes])(packs)


def _adamw_many(ws, gs, ms, vs):
    n = len(ws)

    def body(*refs):
        w_refs, g_refs, m_refs, v_refs = (refs[k * n:(k + 1) * n] for k in range(4))
        d_refs, nm_refs, nv_refs = (refs[(4 + k) * n:(5 + k) * n] for k in range(3))
        for k in range(n):
            g_ = g_refs[k][...]
            m_ = ADAM_B1 * m_refs[k][...] + (1.0 - ADAM_B1) * g_
            v_ = ADAM_B2 * v_refs[k][...] + (1.0 - ADAM_B2) * (g_ * g_)
            m_hat = m_ / (1.0 - ADAM_B1 ** ADAM_STEP)
            v_hat = v_ / (1.0 - ADAM_B2 ** ADAM_STEP)
            d_refs[k][...] = -ADAM_LR * (m_hat / (jnp.sqrt(v_hat) + ADAM_EPS) + ADAM_WD * w_refs[k][...])
            nm_refs[k][...] = m_
            nv_refs[k][...] = v_

    shapes = [jax.ShapeDtypeStruct(a.shape, F32) for a in ws]
    outs = pl.pallas_call(body, name="adamw_small", out_shape=shapes * 3)(*ws, *gs, *ms, *vs)
    return outs[:n], outs[n:2 * n], outs[2 * n:]


def kernel(x, mem, pre_mix_g, w_in, q_norm_g, k_norm_g, hg_lb, hg_out_norm_g, w_out, post_mix_g, pre_x_g, mem_norm_g, w_xq, w_xkv, w_xo, post_x_g, pre_ffn_g, w_up, conv_w, conv_b, w_down, post_ffn_g, loss_target, m_pre_mix_g, m_w_in, m_q_norm_g, m_k_norm_g, m_hg_lb, m_hg_out_norm_g, m_w_out, m_post_mix_g, m_pre_x_g, m_mem_norm_g, m_w_xq, m_w_xkv, m_w_xo, m_post_x_g, m_pre_ffn_g, m_w_up, m_conv_w, m_conv_b, m_w_down, m_post_ffn_g, v_pre_mix_g, v_w_in, v_q_norm_g, v_k_norm_g, v_hg_lb, v_hg_out_norm_g, v_w_out, v_post_mix_g, v_pre_x_g, v_mem_norm_g, v_w_xq, v_w_xkv, v_w_xo, v_post_x_g, v_pre_ffn_g, v_w_up, v_conv_w, v_conv_b, v_w_down, v_post_ffn_g):
    args = dict(locals())
    w = {n: args[n] for n in _WEIGHTS}
    m = {n: args["m_" + n] for n in _WEIGHTS}
    v = {n: args["v_" + n] for n in _WEIGHTS}
    me = _dev_index(*_mesh_pos())

    wire = {n: _bf(w[n][0].T) for n in _COL_SHARDED}
    wire.update({n: _bf(w[n][0]) for n in _ROW_SHARDED})

    loss, grad_x, g, pending = _local_step(x[0], mem[0], loss_target[0], w, wire)

    grads, delta, new_m, new_v = {}, {}, {}, {}

    me_arr = jnp.reshape(me, (1,)).astype(jnp.int32)

    def update(n, parts):
        gsum = _sum_parts(*parts, me_arr, "sum_" + n)
        if n in _ADAM_TRANSPOSED:
            grads[n] = gsum.T[None]
            d_, m_, v_ = _adamw(w[n][0].T, gsum, m[n][0].T, v[n][0].T, "adamw_" + n)
            delta[n], new_m[n], new_v[n] = d_.T[None], m_.T[None], v_.T[None]
        else:
            gsum = gsum.T if n in _COL_SHARDED else gsum
            grads[n] = gsum[None]
            d_, m_, v_ = _adamw(w[n][0], gsum, m[n][0], v[n][0], "adamw_" + n)
            delta[n], new_m[n], new_v[n] = d_[None], m_[None], v_[None]

    after = grad_x
    for tag, names in (("down", ["w_down"]), ("up", ["w_up"]), ("x", ["w_xo", "w_xq", "w_xkv"]), ("out", ["w_out"]),
                       ("in", ["w_in"])):
        for n, parts in zip(names, _exchange_end(pending[tag], after)):
            update(n, parts)
            after = new_v[n]

    small = list(_REPLICATED) + ["hg_lb", "conv_w"]
    vals = [g[n] for n in _REPLICATED] + [g["hg_lb"].reshape(4, HG_DIM), g["conv_w"], jnp.pad(loss, ((0, 0), (0, LANE - 1)))]
    shapes = [val.shape for val in vals]
    (packs,) = _comm_alone(_Gather([_pack_small(vals)]), "gather_small_grads")
    summed = _sum_unpack_small(packs, shapes)
    loss = summed[-1][0, 0]
    for n, s in zip(small, summed[:-1]):
        grads[n] = s
    fold = lambda v2: v2[:, :ATT_HEAD_DIM] + v2[:, ATT_HEAD_DIM:]
    grads["q_norm_g"], grads["k_norm_g"] = fold(grads["q_norm_g"]), fold(grads["k_norm_g"])
    grads["hg_lb"] = lax.dynamic_slice_in_dim(grads["hg_lb"].reshape(2, 2, HG_DIM), me * (HG_DIM // N_DEV),
                                              HG_DIM // N_DEV, axis=2)
    grads["conv_w"] = lax.dynamic_slice_in_dim(grads["conv_w"], me * (2 * D_FF // N_DEV), 2 * D_FF // N_DEV, axis=1)[None]

    flat2 = lambda a: a.reshape(-1, a.shape[-1])
    outs = _adamw_many(*[[flat2(d[n]) for n in small] for d in (w, grads, m, v)])
    for dst, vals in zip((delta, new_m, new_v), outs):
        for n, val in zip(small, vals):
            dst[n] = val.reshape(w[n].shape)

    return (loss, grad_x[None], *[grads[n] for n in _WEIGHTS], *[delta[n] for n in _WEIGHTS],
            *[new_m[n] for n in _WEIGHTS], *[new_v[n] for n in _WEIGHTS])
```

```python
import jax
import jax.numpy as jnp
from jax import lax
from jax.experimental import pallas as pl
from jax.experimental.pallas import tpu as pltpu

F32 = jnp.float32
BF16 = jnp.bfloat16

D_MODEL = 1024
GRID_W = 64
EPS = 1e-6
ATT_HEADS = 8
ATT_HEAD_DIM = 64
ATT_Q_DIM = 512
ATT_KV_DIM = 128
ROPE_THETA = 10000.0
HG_HEADS = 4
HG_DIM = 512
HG_CHUNK = 32
HG_CHUNK_LOG2 = 5
HG_BLOCK = 256
N_IN = 3328
X_HEADS = 4
X_HEAD_DIM = 256
D_FF = 2816
N_DEV = 8
LANE = 128
ADAM_LR = 0.001
ADAM_B1 = 0.9
ADAM_B2 = 0.999
ADAM_EPS = 1e-08
ADAM_WD = 0.01
ADAM_STEP = 10
VMEM_LIMIT = 56 * 1024 * 1024

MESH_T = pl.DeviceIdType.MESH


def _params(**kw):
    return pltpu.CompilerParams(vmem_limit_bytes=VMEM_LIMIT, **kw)


def _dot(a, b, ca, cb):
    return lax.dot_general(a, b, (((ca,), (cb,)), ((), ())), preferred_element_type=F32)


def _bf(x):
    return x.astype(BF16)


def _sigmoid(x):
    return 1.0 / (1.0 + jnp.exp(-x))


def _rms_fwd(x, g):
    r = lax.rsqrt(jnp.mean(x * x, axis=-1, keepdims=True) + EPS)
    return x * r * g


def _rms_bwd(dy, x, g):
    r = lax.rsqrt(jnp.mean(x * x, axis=-1, keepdims=True) + EPS)
    xh = x * r
    dg = jnp.sum(dy * xh, axis=0, keepdims=True)
    t = dy * g
    dx = r * (t - xh * jnp.mean(t * xh, axis=-1, keepdims=True))
    return dx, dg


def _full(shape):
    nd = len(shape)
    return pl.BlockSpec(shape, lambda *a: (0,) * nd)


def _norm_mm(x, g, w, trans, tn, name, comm=None, tm=512):
    t, d = x.shape
    n = w.shape[0] if trans else w.shape[1]
    tm = min(tm, t)

    def body(x_ref, g_ref, w_ref, h_ref, p_ref):
        h = _bf(_rms_fwd(x_ref[...], g_ref[...]))
        h_ref[...] = h
        p_ref[...] = _dot(h, w_ref[...], 1, 1 if trans else 0)

    w_spec = pl.BlockSpec((tn, d), lambda i, j: (j, 0)) if trans else pl.BlockSpec((d, tn), lambda i, j: (0, j))
    (h, p), got = _call(
        body, (x, g, w), name=name, grid=(t // tm, n // tn),
        in_specs=[pl.BlockSpec((tm, d), lambda i, j: (i, 0)), _full((1, d)), w_spec],
        out_specs=[pl.BlockSpec((tm, d), lambda i, j: (i, 0)), pl.BlockSpec((tm, tn), lambda i, j: (i, j))],
        out_shape=[jax.ShapeDtypeStruct((t, d), BF16), jax.ShapeDtypeStruct((t, n), F32)], comm=comm)
    return (h, p, got) if comm is not None else (h, p)


def _mm_postnorm_res(a, w, g, res, name, tm=256):
    t, k = a.shape
    d = w.shape[1]

    def body(a_ref, w_ref, g_ref, res_ref, y_ref, o_ref):
        y = _dot(a_ref[...], w_ref[...], 1, 0)
        y_ref[...] = y
        o_ref[...] = res_ref[...] + _rms_fwd(y, g_ref[...])

    row = lambda width: pl.BlockSpec((tm, width), lambda i: (i, 0))
    return pl.pallas_call(
        body, name=name, grid=(t // tm,),
        in_specs=[row(k), _full((k, d)), _full((1, d)), row(d)],
        out_specs=[row(d), row(d)],
        out_shape=[jax.ShapeDtypeStruct((t, d), F32)] * 2,
        compiler_params=_params(),
    )(a, w, g, res)


def _mm_postnorm_res_loss(a, w, g, res, tgt, name, tm=256):
    t, k = a.shape
    d = w.shape[1]

    def body(a_ref, w_ref, g_ref, res_ref, tgt_ref, y_ref, dout_ref, loss_ref):
        @pl.when(pl.program_id(0) == 0)
        def _():
            loss_ref[...] = jnp.zeros_like(loss_ref)

        y = _dot(a_ref[...], w_ref[...], 1, 0)
        y_ref[...] = y
        diff = res_ref[...] + _rms_fwd(y, g_ref[...]) - tgt_ref[...]
        dout_ref[...] = diff * (1.0 / d)
        part = jnp.sum(jnp.sum(diff * diff, axis=-1, keepdims=True), axis=0, keepdims=True)
        loss_ref[...] += (0.5 / d) * part

    row = lambda width: pl.BlockSpec((tm, width), lambda i: (i, 0))
    return pl.pallas_call(
        body, name=name, grid=(t // tm,),
        in_specs=[row(k), _full((k, d)), _full((1, d)), row(d), row(d)],
        out_specs=[row(d), row(d), _full((1, 1))],
        out_shape=[jax.ShapeDtypeStruct((t, d), F32)] * 2 + [jax.ShapeDtypeStruct((1, 1), F32)],
        compiler_params=_params(),
    )(a, w, g, res, tgt)


def _postnorm_bwd_mm(dout, y, g, w, name, tm=256):
    t, d = y.shape
    k = w.shape[0]

    def body(dout_ref, y_ref, g_ref, w_ref, dy_ref, da_ref, dg_ref):
        @pl.when(pl.program_id(0) == 0)
        def _():
            dg_ref[...] = jnp.zeros_like(dg_ref)

        dy, dg = _rms_bwd(dout_ref[...], y_ref[...], g_ref[...])
        dg_ref[...] += dg
        dyb = _bf(dy)
        dy_ref[...] = dyb
        da_ref[...] = _dot(dyb, w_ref[...], 1, 1)

    row = lambda width: pl.BlockSpec((tm, width), lambda i: (i, 0))
    return pl.pallas_call(
        body, name=name, grid=(t // tm,),
        in_specs=[row(d), row(d), _full((1, d)), _full((k, d))],
        out_specs=[row(d), row(k), _full((1, d))],
        out_shape=[jax.ShapeDtypeStruct((t, d), BF16), jax.ShapeDtypeStruct((t, k), F32), jax.ShapeDtypeStruct((1, d), F32)],
        compiler_params=_params(),
    )(dout, y, g, w)


def _mm_prenorm_bwd(dp, w, trans, x, g, dres, tk, name, comm=None, tm=512):
    dps = list(dp) if isinstance(dp, (list, tuple)) else [dp]
    nparts = len(dps)
    t, n_each = dps[0].shape
    d = x.shape[1]
    tm = min(tm, t)
    nk_each = n_each // tk
    nk = nk_each * nparts

    def body(*refs):
        dp_refs = refs[:nparts]
        w_ref, x_ref, g_ref, dres_ref, dx_ref, dg_ref, acc_ref = refs[nparts:]
        i, kk = pl.program_id(0), pl.program_id(1)

        @pl.when((i == 0) & (kk == 0))
        def _():
            dg_ref[...] = jnp.zeros_like(dg_ref)

        @pl.when(kk == 0)
        def _():
            acc_ref[...] = jnp.zeros_like(acc_ref)

        for part, dp_ref in enumerate(dp_refs):
            @pl.when((kk >= part * nk_each) & (kk < (part + 1) * nk_each))
            def _():
                acc_ref[...] += _dot(dp_ref[...], w_ref[...], 1, 0 if trans else 1)

        @pl.when(kk == nk - 1)
        def _():
            dx, dg = _rms_bwd(acc_ref[...], x_ref[...], g_ref[...])
            dg_ref[...] += dg
            dx_ref[...] = dres_ref[...] + dx

    dp_spec = lambda part: pl.BlockSpec((tm, tk), lambda i, kk: (i, jnp.clip(kk - part * nk_each, 0, nk_each - 1)))
    w_spec = pl.BlockSpec((tk, d), lambda i, kk: (kk, 0)) if trans else pl.BlockSpec((d, tk), lambda i, kk: (0, kk))
    row = pl.BlockSpec((tm, d), lambda i, kk: (i, 0))
    return _call(
        body, (*dps, w, x, g, dres), name=name, grid=(t // tm, nk),
        in_specs=[dp_spec(part) for part in range(nparts)] + [w_spec, row, _full((1, d)), row],
        out_specs=[row, _full((1, d))],
        out_shape=[jax.ShapeDtypeStruct((t, d), F32), jax.ShapeDtypeStruct((1, d), F32)],
        scratch_shapes=[pltpu.VMEM((tm, d), F32)], comm=comm)


def _dw(a, b, name, bcol=None, tka=256):
    parts = list(a) if isinstance(a, (list, tuple)) else [a]
    nparts = len(parts)
    t, ka_each = parts[0].shape
    bci, nb = (0, b.shape[1]) if bcol is None else bcol
    nt_each = ka_each // tka

    def body(*refs):
        a_refs, b_ref, o_ref = refs[:nparts], refs[nparts], refs[nparts + 1]
        i = pl.program_id(0)
        for part, a_ref in enumerate(a_refs):
            @pl.when((i >= part * nt_each) & (i < (part + 1) * nt_each))
            def _():
                o_ref[...] = _bf(_dot(a_ref[...], b_ref[...], 0, 0))

    a_spec = lambda part: pl.BlockSpec((t, tka), lambda i: (0, jnp.clip(i - part * nt_each, 0, nt_each - 1)))
    return pl.pallas_call(
        body, name=name, grid=(nt_each * nparts,),
        in_specs=[a_spec(part) for part in range(nparts)] + [pl.BlockSpec((t, nb), lambda i: (0, bci))],
        out_specs=pl.BlockSpec((tka, nb), lambda i: (i, 0)),
        out_shape=jax.ShapeDtypeStruct((nparts * ka_each, nb), BF16),
        compiler_params=_params(),
    )(*parts, b)


def _rope_tables(t):
    pos = jnp.arange(t)
    r = (pos // GRID_W).astype(F32)
    c = (pos % GRID_W).astype(F32)
    npair = ATT_HEAD_DIM // 4
    inv = jnp.power(ROPE_THETA, -jnp.arange(npair, dtype=F32) / npair)
    ang = jnp.concatenate([r[:, None] * inv, c[:, None] * inv], axis=-1)
    cos = jnp.repeat(jnp.cos(ang), 2, axis=-1)
    sin = jnp.repeat(jnp.sin(ang), 2, axis=-1)
    even = (jnp.arange(ATT_HEAD_DIM) % 2) == 0
    sa = jnp.where(even, -sin, 0.0)
    sb = jnp.where(even, 0.0, sin)
    two = lambda a: jnp.tile(a, (1, 2))
    return two(cos), two(sa), two(sb)


def _head_sum_matrix():
    a = jnp.arange(LANE) // ATT_HEAD_DIM
    return (a[:, None] == a[None, :]).astype(BF16)


def _head_mean(v, bd):
    hi = _bf(v)
    lo = _bf(v - hi.astype(F32))
    return (_dot(hi, bd, 1, 0) + _dot(lo, bd, 1, 0)) * (1.0 / ATT_HEAD_DIM)


def _qk_prep(p, gq, gk, tables, bd, tm=512):
    t = p.shape[0]
    tm = min(tm, t)
    cc, sa, sb = tables

    def body(p_ref, gq_ref, gk_ref, cc_ref, sa_ref, sb_ref, bd_ref, q_ref, k_ref):
        cc_, sa_, sb_, bd_ = cc_ref[...], sa_ref[...], sb_ref[...], bd_ref[...]
        low = lax.broadcasted_iota(jnp.int32, (tm, LANE), 1) < ATT_HEAD_DIM

        def normrope(xs, g):
            xn = xs * lax.rsqrt(_head_mean(xs * xs, bd_) + EPS) * g
            return xn * cc_ + pltpu.roll(xn, LANE - 1, 1) * sa_ + pltpu.roll(xn, 1, 1) * sb_

        for j in range(4):
            y = normrope(p_ref[:, j * LANE:(j + 1) * LANE], gq_ref[...]) * (ATT_HEAD_DIM ** -0.5)
            yr = pltpu.roll(y, ATT_HEAD_DIM, 1)
            if j // 2 == 0:
                h0, h1 = jnp.where(low, y, 0.0), jnp.where(low, yr, 0.0)
            else:
                h0, h1 = jnp.where(low, 0.0, yr), jnp.where(low, 0.0, y)
            q_ref[:, (2 * j) * LANE:(2 * j + 1) * LANE] = _bf(h0)
            q_ref[:, (2 * j + 1) * LANE:(2 * j + 2) * LANE] = _bf(h1)
        k_ref[...] = _bf(normrope(p_ref[:, ATT_Q_DIM:ATT_Q_DIM + LANE], gk_ref[...]))

    row = lambda width: pl.BlockSpec((tm, width), lambda i: (i, 0))
    return pl.pallas_call(
        body, name="qk_prep", grid=(t // tm,),
        in_specs=[row(ATT_Q_DIM + LANE), _full((1, LANE)), _full((1, LANE)), row(LANE), row(LANE), row(LANE),
                  _full((LANE, LANE))],
        out_specs=[row(ATT_HEADS * LANE), row(LANE)],
        out_shape=[jax.ShapeDtypeStruct((t, ATT_HEADS * LANE), BF16), jax.ShapeDtypeStruct((t, LANE), BF16)],
        compiler_params=_params(),
    )(p, gq, gk, cc, sa, sb, bd)


def _qk_prep_bwd(p, dq, dk, gq, gk, tables, bd, tm=512):
    t = p.shape[0]
    tm = min(tm, t)
    cc, sa, sb = tables

    def body(p_ref, dq_ref, dk_ref, gq_ref, gk_ref, cc_ref, sa_ref, sb_ref, bd_ref, dp_ref, dgq_ref, dgk_ref):
        @pl.when(pl.program_id(0) == 0)
        def _():
            dgq_ref[...] = jnp.zeros_like(dgq_ref)
            dgk_ref[...] = jnp.zeros_like(dgk_ref)

        cc_, sa_, sb_, bd_ = cc_ref[...], sa_ref[...], sb_ref[...], bd_ref[...]
        low = lax.broadcasted_iota(jnp.int32, (tm, LANE), 1) < ATT_HEAD_DIM

        def bwd(xs, g, dy):
            r = lax.rsqrt(_head_mean(xs * xs, bd_) + EPS)
            xh = xs * r
            dxn = dy * cc_ + pltpu.roll(dy * sa_, 1, 1) + pltpu.roll(dy * sb_, LANE - 1, 1)
            dg = jnp.sum(dxn * xh, axis=0, keepdims=True)
            tt = dxn * g
            return r * (tt - xh * _head_mean(tt * xh, bd_)), dg

        dgq = jnp.zeros((1, LANE), F32)
        for j in range(4):
            d0 = dq_ref[:, (2 * j) * LANE:(2 * j + 1) * LANE]
            d1 = dq_ref[:, (2 * j + 1) * LANE:(2 * j + 2) * LANE]
            if j // 2 == 0:
                dy = jnp.where(low, d0, pltpu.roll(d1, ATT_HEAD_DIM, 1))
            else:
                dy = jnp.where(low, pltpu.roll(d0, ATT_HEAD_DIM, 1), d1)
            dx, dg = bwd(p_ref[:, j * LANE:(j + 1) * LANE], gq_ref[...], dy * (ATT_HEAD_DIM ** -0.5))
            dp_ref[:, j * LANE:(j + 1) * LANE] = _bf(dx)
            dgq = dgq + dg
        dgq_ref[...] += dgq
        dx, dg = bwd(p_ref[:, ATT_Q_DIM:ATT_Q_DIM + LANE], gk_ref[...], dk_ref[...])
        dp_ref[:, ATT_Q_DIM:ATT_Q_DIM + LANE] = _bf(dx)
        dgk_ref[...] += dg

    row = lambda width: pl.BlockSpec((tm, width), lambda i: (i, 0))
    return pl.pallas_call(
        body, name="qk_prep_bwd", grid=(t // tm,),
        in_specs=[row(ATT_Q_DIM + LANE), row(ATT_HEADS * LANE), row(LANE), _full((1, LANE)), _full((1, LANE)),
                  row(LANE), row(LANE), row(LANE), _full((LANE, LANE))],
        out_specs=[row(ATT_Q_DIM + LANE), _full((1, LANE)), _full((1, LANE))],
        out_shape=[jax.ShapeDtypeStruct((t, ATT_Q_DIM + LANE), BF16), jax.ShapeDtypeStruct((1, LANE), F32),
                   jax.ShapeDtypeStruct((1, LANE), F32)],
        compiler_params=_params(),
    )(p, dq, dk, gq, gk, cc, sa, sb, bd)


def _attn_fwd(q, k, p, comm=None, tq=256):
    t = k.shape[0]
    tq = min(tq, t)
    v_blk = (ATT_Q_DIM + ATT_KV_DIM) // LANE

    def body(q_ref, k_ref, v_ref, o_ref, o32_ref, lse_ref):
        k_ = k_ref[...]
        v = v_ref[...]
        lowk = lax.broadcasted_iota(jnp.int32, (t, LANE), 1) < ATT_HEAD_DIM
        vm = (_bf(jnp.where(lowk, v, 0.0)), _bf(jnp.where(lowk, 0.0, v)))
        for j in range(4):
            kvh = j // 2
            acc = None
            for sub in range(2):
                h = 2 * j + sub
                s = _dot(q_ref[:, h * LANE:(h + 1) * LANE], k_, 1, 1)
                mx = jnp.max(s, axis=-1, keepdims=True)
                e = jnp.exp(s - mx)
                l = jnp.sum(e, axis=-1, keepdims=True)
                lse_ref[h] = mx + jnp.log(l)
                o = _dot(_bf(e), vm[kvh], 1, 0) * (1.0 / l)
                if sub != kvh:
                    o = pltpu.roll(o, ATT_HEAD_DIM, 1)
                acc = o if acc is None else acc + o
            o32_ref[:, j * LANE:(j + 1) * LANE] = acc
            o_ref[:, j * LANE:(j + 1) * LANE] = _bf(acc)

    row = pl.BlockSpec((tq, ATT_Q_DIM), lambda i: (i, 0))
    outs, got = _call(
        body, (q, k, p), name="attn_fwd", grid=(t // tq,),
        in_specs=[pl.BlockSpec((tq, ATT_HEADS * LANE), lambda i: (i, 0)), _full((t, LANE)),
                  pl.BlockSpec((t, LANE), lambda i: (0, v_blk))],
        out_specs=[row, row, pl.BlockSpec((ATT_HEADS, tq, 1), lambda i: (0, i, 0))],
        out_shape=[jax.ShapeDtypeStruct((t, ATT_Q_DIM), BF16), jax.ShapeDtypeStruct((t, ATT_Q_DIM), F32),
                   jax.ShapeDtypeStruct((ATT_HEADS, t, 1), F32)], comm=comm)
    return outs, got


def _attn_bwd(q, k, p, dcat, o32, lse, comm=None, tq=256):
    t = k.shape[0]
    tq = min(tq, t)
    v_blk = (ATT_Q_DIM + ATT_KV_DIM) // LANE

    def body(q_ref, k_ref, v_ref, do_ref, o_ref, lse_ref, dq_ref, dk_ref, dv_ref):
        @pl.when(pl.program_id(0) == 0)
        def _():
            dk_ref[...] = jnp.zeros_like(dk_ref)
            dv_ref[...] = jnp.zeros_like(dv_ref)

        k_ = k_ref[...]
        vb = _bf(v_ref[...])
        lowq = lax.broadcasted_iota(jnp.int32, (tq, LANE), 1) < ATT_HEAD_DIM
        dk_acc = jnp.zeros((t, LANE), F32)
        dv_acc = jnp.zeros((t, LANE), F32)
        for j in range(4):
            kvh = j // 2
            dop = do_ref[:, j * LANE:(j + 1) * LANE]
            prod = dop * o_ref[:, j * LANE:(j + 1) * LANE]
            d_low = jnp.sum(jnp.where(lowq, prod, 0.0), axis=-1, keepdims=True)
            d_sub = (d_low, jnp.sum(prod, axis=-1, keepdims=True) - d_low)
            for sub in range(2):
                h = 2 * j + sub
                src = dop if sub == kvh else pltpu.roll(dop, ATT_HEAD_DIM, 1)
                do_h = _bf(jnp.where(lowq, src, 0.0) if kvh == 0 else jnp.where(lowq, 0.0, src))
                qh = q_ref[:, h * LANE:(h + 1) * LANE]
                pr = jnp.exp(_dot(qh, k_, 1, 1) - lse_ref[h])
                ds = _bf(pr * (_dot(do_h, vb, 1, 1) - d_sub[sub]))
                dq_ref[:, h * LANE:(h + 1) * LANE] = _dot(ds, k_, 1, 0)
                dk_acc = dk_acc + _dot(ds, qh, 0, 0)
                dv_acc = dv_acc + _dot(_bf(pr), do_h, 0, 0)
        dk_ref[...] += dk_acc
        dv_ref[...] += dv_acc

    row = pl.BlockSpec((tq, ATT_Q_DIM), lambda i: (i, 0))
    return _call(
        body, (q, k, p, dcat, o32, lse), name="attn_bwd", grid=(t // tq,),
        in_specs=[pl.BlockSpec((tq, ATT_HEADS * LANE), lambda i: (i, 0)), _full((t, LANE)),
                  pl.BlockSpec((t, LANE), lambda i: (0, v_blk)), row, row,
                  pl.BlockSpec((ATT_HEADS, tq, 1), lambda i: (0, i, 0))],
        out_specs=[pl.BlockSpec((tq, ATT_HEADS * LANE), lambda i: (i, 0)), _full((t, LANE)), _full((t, LANE))],
        out_shape=[jax.ShapeDtypeStruct((t, ATT_HEADS * LANE), F32), jax.ShapeDtypeStruct((t, LANE), F32),
                   jax.ShapeDtypeStruct((t, LANE), F32)], comm=comm)


def _lower_bounds(a0, a1):
    def body(a0_ref, a1_ref, lb_ref):
        m = jnp.maximum(a0_ref[...], a1_ref[...])
        e0, e1 = jnp.exp(a0_ref[...] - m), jnp.exp(a1_ref[...] - m)
        lb_ref[...] = e0 / (e0 + e1)

    return pl.pallas_call(body, name="lower_bounds", out_shape=jax.ShapeDtypeStruct(a0.shape, F32))(a0, a1)


def _lower_bounds_bwd(a0, a1, dlb):
    def body(a0_ref, a1_ref, dlb_ref, d0_ref, d1_ref):
        m = jnp.maximum(a0_ref[...], a1_ref[...])
        e0, e1 = jnp.exp(a0_ref[...] - m), jnp.exp(a1_ref[...] - m)
        lb = e0 / (e0 + e1)
        d0 = dlb_ref[...] * lb * (1.0 - lb)
        d0_ref[...] = d0
        d1_ref[...] = -d0

    return pl.pallas_call(body, name="lower_bounds_bwd", out_shape=[jax.ShapeDtypeStruct(a0.shape, F32)] * 2)(a0, a1, dlb)


def _chunk_scan(x, pos, down):
    n = x.shape[0]
    s = 1
    while s < HG_CHUNK:
        if down:
            x = x + jnp.where(pos >= s, pltpu.roll(x, s, 0), 0.0)
        else:
            x = x + jnp.where(pos < HG_CHUNK - s, pltpu.roll(x, n - s, 0), 0.0)
        s *= 2
    return x


def _chunk_concat(x, cid, nc):
    return jnp.concatenate([_bf(jnp.where(cid == c, x, 0.0)) for c in range(nc)], axis=1)


def _chunk_pick(x, cid, nc):
    out = jnp.where(cid == 0, x[:, :LANE], 0.0)
    for c in range(1, nc):
        out = out + jnp.where(cid == c, x[:, c * LANE:(c + 1) * LANE], 0.0)
    return out


def _hgrn_gates(hq, z, lb):
    sq = _sigmoid(hq)
    sig = _sigmoid(z)
    f = lb + (1.0 - lb) * sig
    return hq * sq, sq, sig, f, jnp.log(f)


def _hgrn_local(q, f, g, v, pos, cid, amask, rev, nc):
    k = 1.0 - f
    ba = _chunk_scan(g, pos, not rev)
    bb = _chunk_scan(g, pos, rev)
    eq = 0.5 * (ba - bb + g)
    ex_q, ex_k, ex_i, ex_d = jnp.exp(eq), jnp.exp(-eq), jnp.exp(ba), jnp.exp(bb - g)
    qb, kb, qi, kd = q * ex_q, k * ex_k, q * ex_i, k * ex_d
    dvec = jnp.exp(ba + bb - g)
    a = jnp.where(amask, _dot(_bf(qb), _bf(kb), 1, 1), 0.0)
    kd_m = _chunk_concat(kd, cid, nc)
    qi_m = _chunk_concat(qi, cid, nc)
    ut_all = _dot(_bf(v), kd_m, 0, 0)
    return dict(k=k, ex_q=ex_q, ex_k=ex_k, ex_i=ex_i, ex_d=ex_d, qb=qb, kb=kb, qi=qi, kd=kd, dvec=dvec, a=a,
                kd_m=kd_m, qi_m=qi_m, ut_all=ut_all)


def _hgrn_masks(n, rev):
    row = lax.broadcasted_iota(jnp.int32, (n, LANE), 0)
    ti = lax.broadcasted_iota(jnp.int32, (n, n), 0)
    si = lax.broadcasted_iota(jnp.int32, (n, n), 1)
    tri = (si >= ti) if rev else (si <= ti)
    same = jnp.right_shift(ti, HG_CHUNK_LOG2) == jnp.right_shift(si, HG_CHUNK_LOG2)
    return jnp.bitwise_and(row, HG_CHUNK - 1), jnp.right_shift(row, HG_CHUNK_LOG2), same & tri


def _hgrn_specs(t, nb, rev, bwd):
    n = min(HG_BLOCK, t)
    if rev != bwd:
        blk = lambda i: nb - 1 - i
    else:
        blk = lambda i: i
    col = lambda base: pl.BlockSpec((n, 2 * LANE), lambda hp, i: (blk(i), base + hp))
    return n, blk, col


def _hgrn_fwd(p, lb, rev, comm=None):
    t = p.shape[0]
    nb = t // min(HG_BLOCK, t)
    n, blk, col = _hgrn_specs(t, nb, rev, False)
    nc = n // HG_CHUNK
    z_base = 7 if rev else 5

    def body(hq_ref, z_ref, hi_ref, lb_ref, o_ref, ssave_ref, st_scr):
        @pl.when(pl.program_id(1) == 0)
        def _():
            st_scr[...] = jnp.zeros_like(st_scr)

        pos, cid, amask = _hgrn_masks(n, rev)
        order = list(range(nc))[::-1] if rev else list(range(nc))
        for hh in range(2):
            sl = slice(hh * LANE, (hh + 1) * LANE)
            q, _, _, f, g = _hgrn_gates(hq_ref[:, sl], z_ref[:, sl], lb_ref[0:1, sl])
            v = hi_ref[:, sl]
            c_ = _hgrn_local(q, f, g, v, pos, cid, amask, rev, nc)
            st = st_scr[hh]
            ssave_ref[0, sl, :] = st
            cols = [None] * nc
            for c in order:
                cols[c] = st
                st = st * c_["dvec"][c * HG_CHUNK:c * HG_CHUNK + 1, :] + c_["ut_all"][:, c * LANE:(c + 1) * LANE]
            st_scr[hh] = st
            st_all = _bf(jnp.concatenate(cols, axis=1))
            o_ref[:, sl] = _dot(_bf(c_["a"]), _bf(v), 1, 0) + _dot(c_["qi_m"], st_all, 1, 1)

    return _call(
        body, (p, p, p, lb), name="hgrn_fwd_rev" if rev else "hgrn_fwd", grid=(2, nb),
        in_specs=[col(3), col(z_base), col(9), pl.BlockSpec((1, 2 * LANE), lambda hp, i: (0, hp))],
        out_specs=[pl.BlockSpec((n, 2 * LANE), lambda hp, i: (blk(i), hp)),
                   pl.BlockSpec((1, 2 * LANE, LANE), lambda hp, i: (blk(i), hp, 0))],
        out_shape=[jax.ShapeDtypeStruct((t, HG_DIM), F32), jax.ShapeDtypeStruct((nb, HG_DIM, LANE), F32)],
        scratch_shapes=[pltpu.VMEM((2, LANE, LANE), F32)], comm=comm)


def _hgrn_bwd(p, lb, do, ssave, rev, comm=None):
    t = p.shape[0]
    nb = t // min(HG_BLOCK, t)
    n, blk, col = _hgrn_specs(t, nb, rev, True)
    nc = n // HG_CHUNK
    z_base = 7 if rev else 5

    def body(hq_ref, z_ref, hi_ref, lb_ref, do_ref, ssave_ref, dhq_ref, dz_ref, dhi_ref, dlb_ref, dst_scr):
        @pl.when(pl.program_id(1) == 0)
        def _():
            dst_scr[...] = jnp.zeros_like(dst_scr)
            dlb_ref[...] = jnp.zeros_like(dlb_ref)

        pos, cid, amask = _hgrn_masks(n, rev)
        order = list(range(nc))[::-1] if rev else list(range(nc))
        for hh in range(2):
            sl = slice(hh * LANE, (hh + 1) * LANE)
            hq, lbv = hq_ref[:, sl], lb_ref[0:1, sl]
            q, sq, sig, f, g = _hgrn_gates(hq, z_ref[:, sl], lbv)
            v = hi_ref[:, sl]
            c_ = _hgrn_local(q, f, g, v, pos, cid, amask, rev, nc)
            dvec, ut_all = c_["dvec"], c_["ut_all"]
            drow = lambda c: dvec[c * HG_CHUNK:c * HG_CHUNK + 1, :]
            st = ssave_ref[0, sl, :]
            cols = [None] * nc
            for c in order:
                cols[c] = st
                st = st * drow(c) + ut_all[:, c * LANE:(c + 1) * LANE]
            dob, vb = _bf(do_ref[:, sl]), _bf(v)
            gt_all = _dot(dob, c_["qi_m"], 0, 0)
            dcur = dst_scr[hh]
            dnext = [None] * nc
            ddrow = [None] * nc
            for c in order[::-1]:
                dnext[c] = dcur
                ddrow[c] = jnp.sum(cols[c] * dcur, axis=0, keepdims=True) * drow(c)
                dcur = dcur * drow(c) + gt_all[:, c * LANE:(c + 1) * LANE]
            dst_scr[hh] = dcur
            dsn_all = _bf(jnp.concatenate(dnext, axis=1))
            st_all = _bf(jnp.concatenate(cols, axis=1))
            da = _bf(jnp.where(amask, _dot(dob, vb, 1, 1), 0.0))
            dv = _dot(_bf(c_["a"]), dob, 0, 0) + _dot(c_["kd_m"], dsn_all, 1, 1)
            dqb = _dot(da, _bf(c_["kb"]), 1, 0)
            dkb = _dot(da, _bf(c_["qb"]), 0, 0)
            dqi = _chunk_pick(_dot(dob, st_all, 1, 0), cid, nc)
            dkd = _chunk_pick(_dot(vb, dsn_all, 1, 0), cid, nc)
            dq = dqb * c_["ex_q"] + dqi * c_["ex_i"]
            dk = dkb * c_["ex_k"] + dkd * c_["ex_d"]
            e = dqb * c_["qb"] - dkb * c_["kb"] + dqi * c_["qi"]
            w = dkd * c_["kd"]
            dtot = jnp.where(cid == 0, ddrow[0], 0.0)
            for c in range(1, nc):
                dtot = dtot + jnp.where(cid == c, ddrow[c], 0.0)
            dg = _chunk_scan(e, pos, rev) + (_chunk_scan(w, pos, not rev) - w) + dtot
            df = dg / f - dk
            dz_ref[:, sl] = df * (1.0 - lbv) * sig * (1.0 - sig)
            dlb_ref[0:1, sl] += jnp.sum(df * (1.0 - sig), axis=0, keepdims=True)
            dhq_ref[:, sl] = dq * (sq * (1.0 + hq * (1.0 - sq)))
            dhi_ref[:, sl] = dv

    out = pl.BlockSpec((n, 2 * LANE), lambda hp, i: (blk(i), hp))
    return _call(
        body, (p, p, p, lb, do, ssave), name="hgrn_bwd_rev" if rev else "hgrn_bwd", grid=(2, nb),
        in_specs=[col(3), col(z_base), col(9), pl.BlockSpec((1, 2 * LANE), lambda hp, i: (0, hp)), out,
                  pl.BlockSpec((1, 2 * LANE, LANE), lambda hp, i: (blk(i), hp, 0))],
        out_specs=[out, out, out, pl.BlockSpec((1, 2 * LANE), lambda hp, i: (0, hp))],
        out_shape=[jax.ShapeDtypeStruct((t, HG_DIM), F32)] * 3 + [jax.ShapeDtypeStruct((1, HG_DIM), F32)],
        scratch_shapes=[pltpu.VMEM((2, LANE, LANE), F32)], comm=comm)


def _mix_out(att, of, ob, p, gout, w_out, g_post, x, comm=None, tm=256):
    t, d = x.shape
    tm = min(tm, t)

    def body(att_ref, of_ref, ob_ref, hg0_ref, hg1_ref, gout_ref, w_ref, g_ref, x_ref, cat_ref, y_ref, x1_ref):
        cat_ref[:, :ATT_Q_DIM] = att_ref[...]
        for h in range(HG_HEADS):
            sl = slice(h * LANE, (h + 1) * LANE)
            hg_ref = hg0_ref if h < 2 else hg1_ref
            hg = hg_ref[:, (h % 2) * LANE:(h % 2 + 1) * LANE]
            nrm = _rms_fwd(of_ref[:, sl] + ob_ref[:, sl], gout_ref[...])
            cat_ref[:, ATT_Q_DIM + h * LANE:ATT_Q_DIM + (h + 1) * LANE] = _bf(nrm * hg * _sigmoid(hg))
        y = _dot(cat_ref[...], w_ref[...], 1, 0)
        y_ref[...] = y
        x1_ref[...] = x_ref[...] + _rms_fwd(y, g_ref[...])

    row = lambda width: pl.BlockSpec((tm, width), lambda i: (i, 0))
    return _call(
        body, (att, of, ob, p, p, gout, w_out, g_post, x), name="mix_out", grid=(t // tm,),
        in_specs=[row(ATT_Q_DIM), row(HG_DIM), row(HG_DIM), pl.BlockSpec((tm, 2 * LANE), lambda i: (i, 11)),
                  pl.BlockSpec((tm, 2 * LANE), lambda i: (i, 12)), _full((1, LANE)), _full((d, d)), _full((1, d)), row(d)],
        out_specs=[row(d), row(d), row(d)],
        out_shape=[jax.ShapeDtypeStruct((t, d), BF16), jax.ShapeDtypeStruct((t, d), F32), jax.ShapeDtypeStruct((t, d), F32)],
        comm=comm)


def _rec_bwd(dcat, of, ob, p, gout, tm=256):
    t = of.shape[0]
    tm = min(tm, t)

    def body(dc_ref, of_ref, ob_ref, hg0_ref, hg1_ref, gout_ref, do_ref, dhg_ref, dgo_ref):
        @pl.when(pl.program_id(0) == 0)
        def _():
            dgo_ref[...] = jnp.zeros_like(dgo_ref)

        dgo = jnp.zeros((1, LANE), F32)
        for h in range(HG_HEADS):
            sl = slice(h * LANE, (h + 1) * LANE)
            hg_ref = hg0_ref if h < 2 else hg1_ref
            hg = hg_ref[:, (h % 2) * LANE:(h % 2 + 1) * LANE]
            o = of_ref[:, sl] + ob_ref[:, sl]
            sg = _sigmoid(hg)
            drec = dc_ref[:, sl]
            dhg_ref[:, sl] = drec * _rms_fwd(o, gout_ref[...]) * (sg * (1.0 + hg * (1.0 - sg)))
            do, dg = _rms_bwd(drec * hg * sg, o, gout_ref[...])
            do_ref[:, sl] = do
            dgo = dgo + dg
        dgo_ref[...] += dgo

    row = lambda width: pl.BlockSpec((tm, width), lambda i: (i, 0))
    return pl.pallas_call(
        body, name="rec_bwd", grid=(t // tm,),
        in_specs=[pl.BlockSpec((tm, HG_DIM), lambda i: (i, 1)), row(HG_DIM), row(HG_DIM),
                  pl.BlockSpec((tm, 2 * LANE), lambda i: (i, 11)), pl.BlockSpec((tm, 2 * LANE), lambda i: (i, 12)),
                  _full((1, LANE))],
        out_specs=[row(HG_DIM), row(HG_DIM), _full((1, LANE))],
        out_shape=[jax.ShapeDtypeStruct((t, HG_DIM), F32), jax.ShapeDtypeStruct((t, HG_DIM), F32),
                   jax.ShapeDtypeStruct((1, LANE), F32)],
        compiler_params=_params(),
    )(dcat, of, ob, p, p, gout)


def _assemble_dp(dp_qk, dv, dhq_f, dhq_b, dz_f, dz_b, dhi_f, dhi_b, dhg, tm=512):
    t = dv.shape[0]
    tm = min(tm, t)
    qk_w = ATT_Q_DIM + LANE

    def body(qk_ref, dv_ref, hqf_ref, hqb_ref, zf_ref, zb_ref, hif_ref, hib_ref, hg_ref, dp_ref):
        o = 0
        dp_ref[:, o:o + qk_w] = qk_ref[...]
        o += qk_w
        dp_ref[:, o:o + LANE] = _bf(dv_ref[...])
        o += LANE
        for val in (hqf_ref[...] + hqb_ref[...], zf_ref[...], zb_ref[...], hif_ref[...] + hib_ref[...], hg_ref[...]):
            dp_ref[:, o:o + HG_DIM] = _bf(val)
            o += HG_DIM

    row = lambda width: pl.BlockSpec((tm, width), lambda i: (i, 0))
    return pl.pallas_call(
        body, name="assemble_dp", grid=(t // tm,),
        in_specs=[row(qk_w), row(LANE)] + [row(HG_DIM)] * 7,
        out_specs=row(N_IN),
        out_shape=jax.ShapeDtypeStruct((t, N_IN), BF16),
        compiler_params=_params(),
    )(dp_qk, dv, dhq_f, dhq_b, dz_f, dz_b, dhi_f, dhi_b, dhg)


def _xattn_fwd(q, kv, tq=512):
    t, d = q.shape
    tq = min(tq, t)
    nm = kv.shape[0]

    def body(q_ref, kv_ref, o_ref):
        for h in range(X_HEADS):
            sl = slice(h * X_HEAD_DIM, (h + 1) * X_HEAD_DIM)
            s = _dot(_bf(q_ref[:, sl]), _bf(kv_ref[:, sl]), 1, 1) * (X_HEAD_DIM ** -0.5)
            e = jnp.exp(s - jnp.max(s, axis=-1, keepdims=True))
            pr = e * (1.0 / jnp.sum(e, axis=-1, keepdims=True))
            o_ref[:, sl] = _bf(_dot(_bf(pr), _bf(kv_ref[:, d + h * X_HEAD_DIM:d + (h + 1) * X_HEAD_DIM]), 1, 0))

    return pl.pallas_call(
        body, name="xattn_fwd", grid=(t // tq,),
        in_specs=[pl.BlockSpec((tq, d), lambda i: (i, 0)), _full((nm, 2 * d))],
        out_specs=pl.BlockSpec((tq, d), lambda i: (i, 0)),
        out_shape=jax.ShapeDtypeStruct((t, d), BF16),
        compiler_params=_params(),
    )(q, kv)


def _xattn_bwd(q, kv, do, tq=512):
    t, d = q.shape
    tq = min(tq, t)
    nm = kv.shape[0]

    def body(q_ref, kv_ref, do_ref, dq_ref, dkv_ref):
        @pl.when(pl.program_id(0) == 0)
        def _():
            dkv_ref[...] = jnp.zeros_like(dkv_ref)

        for h in range(X_HEADS):
            sl = slice(h * X_HEAD_DIM, (h + 1) * X_HEAD_DIM)
            slv = slice(d + h * X_HEAD_DIM, d + (h + 1) * X_HEAD_DIM)
            qb, kb, vb, dob = _bf(q_ref[:, sl]), _bf(kv_ref[:, sl]), _bf(kv_ref[:, slv]), _bf(do_ref[:, sl])
            s = _dot(qb, kb, 1, 1) * (X_HEAD_DIM ** -0.5)
            e = jnp.exp(s - jnp.max(s, axis=-1, keepdims=True))
            pr = e * (1.0 / jnp.sum(e, axis=-1, keepdims=True))
            dpr = _dot(dob, vb, 1, 1)
            ds = _bf(pr * (dpr - jnp.sum(pr * dpr, axis=-1, keepdims=True)) * (X_HEAD_DIM ** -0.5))
            dq_ref[:, sl] = _bf(_dot(ds, kb, 1, 0))
            dkv_ref[:, sl] += _dot(ds, qb, 0, 0)
            dkv_ref[:, slv] += _dot(_bf(pr), dob, 0, 0)

    return pl.pallas_call(
        body, name="xattn_bwd", grid=(t // tq,),
        in_specs=[pl.BlockSpec((tq, d), lambda i: (i, 0)), _full((nm, 2 * d)), pl.BlockSpec((tq, d), lambda i: (i, 0))],
        out_specs=[pl.BlockSpec((tq, d), lambda i: (i, 0)), _full((nm, 2 * d))],
        out_shape=[jax.ShapeDtypeStruct((t, d), BF16), jax.ShapeDtypeStruct((nm, 2 * d), F32)],
        compiler_params=_params(),
    )(q, kv, do)


CONV_TN = 256


def _shift_rows(u, row, t, delta):
    if delta < 0:
        return jnp.where(row == 0, 0.0, pltpu.roll(u, 1, 0))
    return jnp.where(row == t - 1, 0.0, pltpu.roll(u, t - 1, 0))


def _conv_gate_fwd(u, conv_w, conv_b):
    t = u.shape[0]
    nt = D_FF // CONV_TN

    def body(ug_ref, uv_ref, wg_ref, wv_ref, bg_ref, bv_ref, a_ref):
        row = lax.broadcasted_iota(jnp.int32, (t, CONV_TN), 0)

        def conv(u_ref, w_ref, b_ref):
            uu = u_ref[...]
            return (b_ref[...] + _shift_rows(uu, row, t, -1) * w_ref[0:1, :] + uu * w_ref[1:2, :]
                    + _shift_rows(uu, row, t, 1) * w_ref[2:3, :])

        gate = conv(ug_ref, wg_ref, bg_ref)
        a_ref[...] = _bf(gate * _sigmoid(gate) * conv(uv_ref, wv_ref, bv_ref))

    col = lambda rows, off: pl.BlockSpec((rows, CONV_TN), lambda j: (0, j + off))
    return pl.pallas_call(
        body, name="conv_gate_fwd", grid=(nt,),
        in_specs=[col(t, 0), col(t, nt), col(3, 0), col(3, nt), col(1, 0), col(1, nt)],
        out_specs=col(t, 0),
        out_shape=jax.ShapeDtypeStruct((t, D_FF), BF16),
        compiler_params=_params(),
    )(u, u, conv_w, conv_w, conv_b, conv_b)


def _conv_gate_bwd(u, conv_w, conv_b, da, comm=None):
    t = u.shape[0]
    nt = D_FF // CONV_TN

    def body(ug_ref, uv_ref, wg_ref, wv_ref, bg_ref, bv_ref, da_ref, dug_ref, dwg_ref, dbg_ref, duv_ref, dwv_ref, dbv_ref):
        row = lax.broadcasted_iota(jnp.int32, (t, CONV_TN), 0)
        ug, uv = ug_ref[...], uv_ref[...]
        ug_m, ug_p = _shift_rows(ug, row, t, -1), _shift_rows(ug, row, t, 1)
        uv_m, uv_p = _shift_rows(uv, row, t, -1), _shift_rows(uv, row, t, 1)
        gate = bg_ref[...] + ug_m * wg_ref[0:1, :] + ug * wg_ref[1:2, :] + ug_p * wg_ref[2:3, :]
        val = bv_ref[...] + uv_m * wv_ref[0:1, :] + uv * wv_ref[1:2, :] + uv_p * wv_ref[2:3, :]
        sg = _sigmoid(gate)
        da_ = da_ref[...]

        def emit(dacc, um, uu, up, w_ref, du_ref, dw_ref, db_ref):
            du_ref[...] = _bf(_shift_rows(dacc, row, t, 1) * w_ref[0:1, :] + dacc * w_ref[1:2, :]
                              + _shift_rows(dacc, row, t, -1) * w_ref[2:3, :])
            dw_ref[0:1, :] = jnp.sum(dacc * um, axis=0, keepdims=True)
            dw_ref[1:2, :] = jnp.sum(dacc * uu, axis=0, keepdims=True)
            dw_ref[2:3, :] = jnp.sum(dacc * up, axis=0, keepdims=True)
            db_ref[...] = jnp.sum(dacc, axis=0, keepdims=True)

        emit(da_ * val * (sg * (1.0 + gate * (1.0 - sg))), ug_m, ug, ug_p, wg_ref, dug_ref, dwg_ref, dbg_ref)
        emit(da_ * gate * sg, uv_m, uv, uv_p, wv_ref, duv_ref, dwv_ref, dbv_ref)

    col = lambda rows, off: pl.BlockSpec((rows, CONV_TN), lambda j: (0, j + off))
    half_shapes = [jax.ShapeDtypeStruct((t, D_FF), BF16), jax.ShapeDtypeStruct((3, D_FF), F32),
                   jax.ShapeDtypeStruct((1, D_FF), F32)]
    outs, got = _call(
        body, (u, u, conv_w, conv_w, conv_b, conv_b, da), name="conv_gate_bwd", grid=(nt,),
        in_specs=[col(t, 0), col(t, nt), col(3, 0), col(3, nt), col(1, 0), col(1, nt), col(t, 0)],
        out_specs=[col(t, 0), col(3, 0), col(1, 0)] * 2, out_shape=half_shapes * 2, comm=comm)
    return outs[:3], outs[3:], got


def _row_tile(r, cap):
    best = r
    for cand in range(16, cap + 1, 16):
        if r % cand == 0:
            best = cand
    return best


def _sum_parts(own, got, me, name, tr=256):
    _, r, c = got.shape
    tr = _row_tile(r, tr)

    def body(me_ref, own_ref, got_ref, o_ref):
        mine = own_ref[...].astype(F32)
        acc = None
        for i in range(N_DEV):
            term = jnp.where(me_ref[0] == i, mine, got_ref[i].astype(F32))
            acc = term if acc is None else acc + term
        o_ref[...] = acc

    return pl.pallas_call(
        body, name=name,
        grid_spec=pltpu.PrefetchScalarGridSpec(
            num_scalar_prefetch=1, grid=(r // tr,),
            in_specs=[pl.BlockSpec((None, tr, c), lambda i, me_ref: (me_ref[0], i, 0)),
                      pl.BlockSpec((N_DEV, tr, c), lambda i, me_ref: (0, i, 0))],
            out_specs=pl.BlockSpec((tr, c), lambda i, me_ref: (i, 0))),
        out_shape=jax.ShapeDtypeStruct((r, c), F32),
        compiler_params=_params(),
    )(me, own, got)


def _adamw(w, g, m, v, name, tr=256):
    r, c = w.shape
    tr = _row_tile(r, tr)

    def body(w_ref, g_ref, m_ref, v_ref, d_ref, nm_ref, nv_ref):
        g_ = g_ref[...]
        m_ = ADAM_B1 * m_ref[...] + (1.0 - ADAM_B1) * g_
        v_ = ADAM_B2 * v_ref[...] + (1.0 - ADAM_B2) * (g_ * g_)
        m_hat = m_ / (1.0 - ADAM_B1 ** ADAM_STEP)
        v_hat = v_ / (1.0 - ADAM_B2 ** ADAM_STEP)
        d_ref[...] = -ADAM_LR * (m_hat / (jnp.sqrt(v_hat) + ADAM_EPS) + ADAM_WD * w_ref[...])
        nm_ref[...] = m_
        nv_ref[...] = v_

    blk = pl.BlockSpec((tr, c), lambda i: (i, 0))
    return pl.pallas_call(
        body, name=name, grid=(r // tr,),
        in_specs=[blk] * 4, out_specs=[blk] * 3,
        out_shape=[jax.ShapeDtypeStruct((r, c), F32)] * 3,
        compiler_params=_params(),
    )(w, g, m, v)


def _mesh_pos():
    return lax.axis_index("x"), lax.axis_index("y"), lax.axis_index("c")


def _dev_index(px, py, pc):
    return 4 * px + 2 * py + pc


class _Gather:
    def __init__(self, arrs):
        self.arrs = list(arrs)
        n = len(self.arrs)
        self.out_shape = [jax.ShapeDtypeStruct((N_DEV,) + a.shape, a.dtype) for a in self.arrs]
        self.sems = [pltpu.SemaphoreType.DMA((7, n)), pltpu.SemaphoreType.DMA((7, n)), pltpu.SemaphoreType.DMA((n,))]

    def _ctx(self, ins, outs, sems):
        send_sems, recv_sems, local_sems = sems
        x, y, c = _mesh_pos()
        chips = [(1 - x, y), (x, 1 - y), (1 - x, 1 - y)]

        def copy(k, a, block, to, src=None):
            dst = outs[a].at[_dev_index(*block)]
            return pltpu.make_async_remote_copy(
                src_ref=dst if src is None else src, dst_ref=dst, send_sem=send_sems.at[k, a], recv_sem=recv_sems.at[k, a],
                device_id=to, device_id_type=MESH_T)

        n = len(ins)
        me, sibling = (x, y, c), (x, y, 1 - c)
        mine = [pltpu.make_async_copy(ins[a], outs[a].at[_dev_index(*me)], local_sems.at[a]) for a in range(n)]
        first = [copy(0, a, me, sibling, src=ins[a]) for a in range(n)]
        first += [copy(1 + j, a, me, (*chip, c), src=ins[a]) for j, chip in enumerate(chips) for a in range(n)]
        passed = [copy(4 + j, a, (*chip, c), sibling) for j, chip in enumerate(chips) for a in range(n)]
        return n, c, me, sibling, chips, copy, mine, first, passed

    def start(self, ins, outs, sems):
        _, _, _, _, _, _, mine, first, _ = self._ctx(ins, outs, sems)
        for cp in mine + first:
            cp.start()

    def forward(self, ins, outs, sems):
        n, c, me, _, chips, copy, _, _, passed = self._ctx(ins, outs, sems)
        for j, chip in enumerate(chips):
            for a in range(n):
                copy(1 + j, a, (*chip, c), me).wait_recv()
                passed[j * n + a].start()

    def finish(self, ins, outs, sems):
        n, c, me, sibling, chips, copy, mine, first, passed = self._ctx(ins, outs, sems)
        for a in range(n):
            copy(0, a, sibling, me).wait_recv()
        for j, chip in enumerate(chips):
            for a in range(n):
                copy(4 + j, a, (*chip, 1 - c), me).wait_recv()
        for cp in first + passed:
            cp.wait_send()
        for cp in mine:
            cp.wait()


class _Exchange:
    def __init__(self, arrs):
        self.arrs = list(arrs)
        n = len(self.arrs)
        self.out_shape = [jax.ShapeDtypeStruct(a.shape, a.dtype) for a in self.arrs]
        self.sems = [pltpu.SemaphoreType.DMA((7, n)), pltpu.SemaphoreType.DMA((7, n)), pltpu.SemaphoreType.DMA((n,))]

    def _copies(self, ins, outs, sems):
        send_sems, recv_sems, local_sems = sems
        x, y, c = _mesh_pos()
        me_i = _dev_index(x, y, c)
        n = len(ins)
        mine = [pltpu.make_async_copy(ins[a].at[me_i], outs[a].at[me_i], local_sems.at[a]) for a in range(n)]
        copies = []
        for k in range(1, N_DEV):
            px = 1 - x if k & 4 else x
            py = 1 - y if k & 2 else y
            pc = 1 - c if k & 1 else c
            peer_i = _dev_index(px, py, pc)
            for a in range(n):
                copies.append(pltpu.make_async_remote_copy(
                    src_ref=ins[a].at[peer_i], dst_ref=outs[a].at[me_i], send_sem=send_sems.at[k - 1, a],
                    recv_sem=recv_sems.at[k - 1, a], device_id=(px, py, pc), device_id_type=MESH_T))
        return mine, copies

    def start(self, ins, outs, sems):
        mine, copies = self._copies(ins, outs, sems)
        for cp in mine + copies:
            cp.start()

    forward = None

    def finish(self, ins, outs, sems):
        mine, copies = self._copies(ins, outs, sems)
        for cp in copies:
            cp.wait_recv()
        for cp in copies:
            cp.wait_send()
        for cp in mine:
            cp.wait()


def _comm_alone(comm, name):
    n = len(comm.arrs)

    def body(*refs):
        ins, outs, sems = refs[:n], refs[n:2 * n], refs[2 * n:]
        comm.start(ins, outs, sems)
        if comm.forward is not None:
            comm.forward(ins, outs, sems)
        comm.finish(ins, outs, sems)

    any_spec = pl.BlockSpec(memory_space=pl.ANY)
    return pl.pallas_call(body, name=name, in_specs=[any_spec] * n, out_specs=[any_spec] * n, out_shape=comm.out_shape,
                          scratch_shapes=comm.sems)(*comm.arrs)


def _peers(x, y, c):
    out = []
    for k in range(1, N_DEV):
        pos = (1 - x if k & 4 else x, 1 - y if k & 2 else y, 1 - c if k & 1 else c)
        out.append((k, pos, _dev_index(*pos)))
    return out


def _exchange_begin(arrs, tag):
    n = len(arrs)
    lands = [lax.empty(a.shape, a.dtype) for a in arrs]

    def start_body(*refs):
        ins, land = refs[:n], refs[n:2 * n]
        send_sems, recv_sems, token = refs[2 * n], refs[2 * n + 1], refs[-1]
        x, y, c = _mesh_pos()
        me_i = _dev_index(x, y, c)
        for k, pos, peer_i in _peers(x, y, c):
            for a in range(n):
                pltpu.make_async_remote_copy(
                    src_ref=ins[a].at[peer_i], dst_ref=land[a].at[me_i], send_sem=send_sems.at[(k - 1) * n + a],
                    recv_sem=recv_sems.at[(k - 1) * n + a], device_id=pos, device_id_type=MESH_T).start()
        token[...] = jnp.zeros_like(token)

    hbm = pl.BlockSpec(memory_space=pltpu.HBM)
    sem = pl.BlockSpec(memory_space=pltpu.SEMAPHORE)
    thru = [pltpu.HBM(a.shape, a.dtype) for a in arrs]
    outs = pl.pallas_call(
        start_body, name="exchange_start_" + tag,
        out_shape=[pltpu.SemaphoreType.DMA((7 * n,)), pltpu.SemaphoreType.DMA((7 * n,))] + thru + thru
        + [jax.ShapeDtypeStruct((8, LANE), F32)],
        in_specs=[hbm] * (2 * n), out_specs=[sem, sem] + [hbm] * (2 * n) + [pl.BlockSpec(memory_space=pltpu.VMEM)],
        input_output_aliases={i: 2 + i for i in range(2 * n)},
        compiler_params=pltpu.CompilerParams(has_side_effects=pltpu.SideEffectType.DATAFLOW_SIDE_EFFECTING),
    )(*[pltpu.with_memory_space_constraint(a, pltpu.HBM) for a in list(arrs) + list(lands)])
    return (tag, n, outs[0], outs[1], outs[2:2 + n], outs[2 + n:2 + 2 * n]), outs[-1][0, 0]


def _exchange_end(handle, after):
    tag, n, send_sems, recv_sems, srcs, lands = handle

    def body(*refs):
        ins, land = refs[:n], refs[n:2 * n]
        send_sems_, recv_sems_ = refs[2 * n], refs[2 * n + 1]
        x, y, c = _mesh_pos()
        me_i = _dev_index(x, y, c)
        for k, pos, peer_i in _peers(x, y, c):
            for a in range(n):
                cp = pltpu.make_async_remote_copy(
                    src_ref=ins[a].at[peer_i], dst_ref=land[a].at[me_i], send_sem=send_sems_.at[(k - 1) * n + a],
                    recv_sem=recv_sems_.at[(k - 1) * n + a], device_id=pos, device_id_type=MESH_T)
                cp.wait_send()
                cp.wait_recv()

    hbm = pl.BlockSpec(memory_space=pltpu.HBM)
    sem = pl.BlockSpec(memory_space=pltpu.SEMAPHORE)
    outs = pl.pallas_call(
        body, name="exchange_end_" + tag, out_shape=[pltpu.HBM(a.shape, a.dtype) for a in list(srcs) + list(lands)],
        in_specs=[hbm] * (2 * n) + [sem, sem, pl.BlockSpec(memory_space=pl.ANY)], out_specs=[hbm] * (2 * n),
        input_output_aliases={i: i for i in range(2 * n)},
        compiler_params=pltpu.CompilerParams(has_side_effects=pltpu.SideEffectType.DATAFLOW_SIDE_EFFECTING),
    )(*srcs, *lands, send_sems, recv_sems, after)
    return list(zip(outs[:n], outs[n:]))


def _gather_begin(arrs, tag):
    n = len(arrs)
    me = _dev_index(*_mesh_pos())
    lands = [lax.dynamic_update_slice(lax.empty((N_DEV,) + a.shape, a.dtype), a[None], (me,) + (0,) * a.ndim) for a in arrs]

    def start_body(*refs):
        ins, land = refs[:n], refs[n:2 * n]
        send_sems, recv_sems, token = refs[2 * n], refs[2 * n + 1], refs[-1]
        x, y, c = _mesh_pos()
        me_i = _dev_index(x, y, c)
        for a in range(n):
            for k, pos, _ in _peers(x, y, c):
                pltpu.make_async_remote_copy(
                    src_ref=ins[a], dst_ref=land[a].at[me_i], send_sem=send_sems.at[(k - 1) * n + a],
                    recv_sem=recv_sems.at[(k - 1) * n + a], device_id=pos, device_id_type=MESH_T).start()
        token[...] = jnp.zeros_like(token)

    hbm = pl.BlockSpec(memory_space=pltpu.HBM)
    sem = pl.BlockSpec(memory_space=pltpu.SEMAPHORE)
    thru = [pltpu.HBM(a.shape, a.dtype) for a in list(arrs) + lands]
    outs = pl.pallas_call(
        start_body, name="gather_start_" + tag,
        out_shape=[pltpu.SemaphoreType.DMA((7 * n,)), pltpu.SemaphoreType.DMA((7 * n,))] + thru
        + [jax.ShapeDtypeStruct((8, LANE), F32)],
        in_specs=[hbm] * (2 * n), out_specs=[sem, sem] + [hbm] * (2 * n) + [pl.BlockSpec(memory_space=pltpu.VMEM)],
        input_output_aliases={i: 2 + i for i in range(2 * n)},
        compiler_params=pltpu.CompilerParams(has_side_effects=pltpu.SideEffectType.DATAFLOW_SIDE_EFFECTING),
    )(*[pltpu.with_memory_space_constraint(a, pltpu.HBM) for a in list(arrs) + lands])
    return (tag, n, outs[0], outs[1], outs[2:2 + n], outs[2 + n:2 + 2 * n]), outs[-1][0, 0]


def _gather_end(handle, which, after):
    tag, n, send_sems, recv_sems, srcs, lands = handle
    m = len(which)

    def body(*refs):
        ins, land = refs[:m], refs[m:2 * m]
        send_sems_, recv_sems_ = refs[2 * m], refs[2 * m + 1]
        x, y, c = _mesh_pos()
        me_i = _dev_index(x, y, c)
        for j, a in enumerate(which):
            for k, pos, _ in _peers(x, y, c):
                cp = pltpu.make_async_remote_copy(
                    src_ref=ins[j], dst_ref=land[j].at[me_i], send_sem=send_sems_.at[(k - 1) * n + a],
                    recv_sem=recv_sems_.at[(k - 1) * n + a], device_id=pos, device_id_type=MESH_T)
                cp.wait_send()
                cp.wait_recv()

    hbm = pl.BlockSpec(memory_space=pltpu.HBM)
    sem = pl.BlockSpec(memory_space=pltpu.SEMAPHORE)
    ops = [srcs[a] for a in which] + [lands[a] for a in which]
    outs = pl.pallas_call(
        body, name="gather_end_%s_%s" % (tag, "_".join(str(a) for a in which)),
        out_shape=[pltpu.HBM(a.shape, a.dtype) for a in ops],
        in_specs=[hbm] * (2 * m) + [sem, sem, pl.BlockSpec(memory_space=pl.ANY)], out_specs=[hbm] * (2 * m),
        input_output_aliases={i: i for i in range(2 * m)},
        compiler_params=pltpu.CompilerParams(has_side_effects=pltpu.SideEffectType.DATAFLOW_SIDE_EFFECTING),
    )(*ops, send_sems, recv_sems, after)
    return list(outs[m:])


def _call(body, operands, *, name, grid, in_specs, out_specs, out_shape, scratch_shapes=(), comm=None):
    if comm is None:
        outs = pl.pallas_call(body, name=name, grid=grid, in_specs=in_specs, out_specs=out_specs, out_shape=out_shape,
                              scratch_shapes=list(scratch_shapes), compiler_params=_params())(*operands)
        return list(outs), []
    ni, no, ns = len(in_specs), len(out_specs), len(scratch_shapes)
    ci, co = len(comm.arrs), len(comm.out_shape)
    total = 1
    for gdim in grid:
        total *= gdim

    def hosted(*refs):
        o0 = ni + ci
        s0 = o0 + no + co
        ins, cins = refs[:ni], refs[ni:o0]
        outs, couts = refs[o0:o0 + no], refs[o0 + no:s0]
        scr, csems = refs[s0:s0 + ns], refs[s0 + ns:]
        step = 0
        for axis, gdim in enumerate(grid):
            step = step * gdim + pl.program_id(axis)

        @pl.when(step == 0)
        def _():
            comm.start(cins, couts, csems)

        if comm.forward is not None:
            @pl.when(step == (3 * total) // 4)
            def _():
                comm.forward(cins, couts, csems)

        body(*ins, *outs, *scr)

        @pl.when(step == total - 1)
        def _():
            comm.finish(cins, couts, csems)

    any_spec = pl.BlockSpec(memory_space=pl.ANY)
    res = pl.pallas_call(
        hosted, name=name, grid=grid, in_specs=list(in_specs) + [any_spec] * ci, out_specs=list(out_specs) + [any_spec] * co,
        out_shape=list(out_shape) + comm.out_shape, scratch_shapes=list(scratch_shapes) + comm.sems,
        compiler_params=_params())(*operands, *comm.arrs)
    return list(res[:no]), list(res[no:])


def _rows(a):
    return a.reshape(-1, a.shape[-1])


def _slots(a):
    return a.reshape(N_DEV, -1, a.shape[-1])


def _local_step(x, mem, tgt, w, wire):
    tables = _rope_tables(x.shape[0])
    bd = _head_sum_matrix()
    two = lambda g: jnp.tile(g, (1, 2))
    gq2, gk2 = two(w["q_norm_g"]), two(w["k_norm_g"])

    w_in_t, hg_lb, conv_w = _comm_alone(_Gather([wire["w_in"], w["hg_lb"].reshape(4, -1), w["conv_w"][0]]), "gather_w_in")
    w_in_t = _rows(w_in_t)
    hg_lb = jnp.transpose(hg_lb.reshape(N_DEV, 2, 2, -1), (1, 2, 0, 3)).reshape(2, 2, HG_DIM)
    conv_w = jnp.transpose(conv_w, (1, 0, 2)).reshape(3, 2 * D_FF)
    lb_a0, lb_a1 = hg_lb[:, 0, :], hg_lb[:, 1, :]

    order = ("w_out", "w_xq", "w_xkv", "w_xo", "w_up", "w_down")
    fetch, zero = _gather_begin([wire[n] for n in order], "w")
    take = lambda n, after: _rows(_gather_end(fetch, [order.index(n)], after)[0])

    h1, p = _norm_mm(x, w["pre_mix_g"] + zero, w_in_t, True, 1664, "in_proj")
    qr, kr = _qk_prep(p, gq2, gk2, tables, bd)
    (att, att32, lse), _ = _attn_fwd(qr, kr, p)
    lb = _lower_bounds(lb_a0, lb_a1)
    (of, s_f), _ = _hgrn_fwd(p, lb[0:1], False)
    (ob, s_b), _ = _hgrn_fwd(p, lb[1:2], True)
    w_out = take("w_out", ob)
    (cat, mixed, x1), _ = _mix_out(att, of, ob, p, w["hg_out_norm_g"], w_out, w["post_mix_g"], x)

    w_xq = take("w_xq", x1)
    h2, q2 = _norm_mm(x1, w["pre_x_g"], w_xq, False, 1024, "xq_proj")
    w_xkv_t = take("w_xkv", q2)
    mn, kv = _norm_mm(mem, w["mem_norm_g"], w_xkv_t, True, 1024, "xkv_proj")
    o2 = _xattn_fwd(q2, kv)
    w_xo = take("w_xo", o2)
    y2, x2 = _mm_postnorm_res(o2, w_xo, w["post_x_g"], x1, "xo_proj")

    w_up_t = take("w_up", x2)
    h3, u = _norm_mm(x2, w["pre_ffn_g"], w_up_t, True, 1408, "up_proj")
    a = _conv_gate_fwd(u, conv_w, w["conv_b"])
    w_down = take("w_down", a)
    y3, dx3, loss = _mm_postnorm_res_loss(a, w_down, w["post_ffn_g"], x2, tgt, "down_proj")

    g, pending = {}, {}
    dy3, da, g["post_ffn_g"] = _postnorm_bwd_mm(dx3, y3, w["post_ffn_g"], w_down, "down_bwd")
    pending["down"], zero = _exchange_begin([_slots(_dw(a, dy3, "dw_down"))], "down")
    (du_g, dcw_g, dcb_g), (du_v, dcw_v, dcb_v), _ = _conv_gate_bwd(u, conv_w, w["conv_b"] + zero, da)
    g["conv_w"] = jnp.concatenate([dcw_g, dcw_v], axis=1)
    g["conv_b"] = jnp.concatenate([dcb_g, dcb_v], axis=1)
    pending["up"], zero = _exchange_begin([_slots(_dw([du_g, du_v], h3, "dw_up"))], "up")
    (dx2, g["pre_ffn_g"]), _ = _mm_prenorm_bwd([du_g, du_v], w_up_t, True, x2, w["pre_ffn_g"] + zero, dx3, 1408, "up_bwd")

    dy2, do2, g["post_x_g"] = _postnorm_bwd_mm(dx2, y2, w["post_x_g"], w_xo, "xo_bwd")
    dw_xo = _dw(o2, dy2, "dw_xo")
    dq2, dkv = _xattn_bwd(q2, kv, do2)
    dw_xq = _dw(h2, dq2, "dw_xq")
    dkvb = _bf(dkv)
    dw_xkv = _dw(dkvb, mn, "dw_xkv")
    pending["x"], zero = _exchange_begin([_slots(dw_xo), _slots(dw_xq), _slots(dw_xkv)], "x")
    (_, g["mem_norm_g"]), _ = _mm_prenorm_bwd(dkvb, w_xkv_t, True, mem, w["mem_norm_g"], jnp.zeros_like(mem), 2048, "xkv_bwd")
    (dx1, g["pre_x_g"]), _ = _mm_prenorm_bwd(dq2, w_xq, False, x1, w["pre_x_g"] + zero, dx2, 1024, "xq_bwd")

    dmixed, dcat, g["post_mix_g"] = _postnorm_bwd_mm(dx1, mixed, w["post_mix_g"], w_out, "out_bwd")
    pending["out"], zero = _exchange_begin([_slots(_dw(cat, dmixed, "dw_out"))], "out")
    do, dhg, g["hg_out_norm_g"] = _rec_bwd(dcat, of, ob, p, w["hg_out_norm_g"] + zero)
    (dhq_f, dz_f, dhi_f, dlb_f), _ = _hgrn_bwd(p, lb[0:1], do, s_f, False)
    (dhq_b, dz_b, dhi_b, dlb_b), _ = _hgrn_bwd(p, lb[1:2], do, s_b, True)
    d_a0, d_a1 = _lower_bounds_bwd(lb_a0, lb_a1, jnp.concatenate([dlb_f, dlb_b], axis=0))
    g["hg_lb"] = jnp.stack([d_a0, d_a1], axis=1)
    (dqr, dkr, dv), _ = _attn_bwd(qr, kr, p, dcat, att32, lse)
    dp_qk, dgq, dgk = _qk_prep_bwd(p, dqr, dkr, gq2, gk2, tables, bd)
    g["q_norm_g"], g["k_norm_g"] = dgq, dgk
    dp = _assemble_dp(dp_qk, dv, dhq_f, dhq_b, dz_f, dz_b, dhi_f, dhi_b, dhg)
    pending["in"], zero = _exchange_begin([_slots(_dw(dp, h1, "dw_in"))], "in")
    (dx, g["pre_mix_g"]), _ = _mm_prenorm_bwd(dp, w_in_t, True, x, w["pre_mix_g"] + zero, dx1, 1664, "in_bwd")
    return loss, dx, g, pending


_COL_SHARDED = ("w_in", "w_xkv", "w_up")
_ROW_SHARDED = ("w_out", "w_xq", "w_xo", "w_down")
_REPLICATED = ("pre_mix_g", "q_norm_g", "k_norm_g", "hg_out_norm_g", "post_mix_g", "pre_x_g", "mem_norm_g", "post_x_g",
               "pre_ffn_g", "conv_b", "post_ffn_g")
_WEIGHTS = ("pre_mix_g", "w_in", "q_norm_g", "k_norm_g", "hg_lb", "hg_out_norm_g", "w_out", "post_mix_g", "pre_x_g",
            "mem_norm_g", "w_xq", "w_xkv", "w_xo", "post_x_g", "pre_ffn_g", "w_up", "conv_w", "conv_b", "w_down",
            "post_ffn_g")
_ADAM_TRANSPOSED = ("w_in", "w_up")
PACK_W = 1024


def _small_plan(shapes):
    plan, r = [], 0
    for vi, (rows, cols) in enumerate(shapes):
        for i in range(rows):
            for c0 in range(0, cols, PACK_W):
                plan.append((vi, i, c0, min(PACK_W, cols - c0), r))
                r += 1
    return plan, -(-r // 8) * 8


def _pack_small(vals):
    plan, nrows = _small_plan([val.shape for val in vals])

    def body(*refs):
        ins, out = refs[:-1], refs[-1]
        out[...] = jnp.zeros_like(out)
        for vi, i, c0, width, r in plan:
            out[r:r + 1, 0:width] = ins[vi][i:i + 1, c0:c0 + width]

    return pl.pallas_call(body, name="pack_small", out_shape=jax.ShapeDtypeStruct((nrows, PACK_W), F32))(*vals)


def _sum_unpack_small(packs, shapes):
    plan, _ = _small_plan(shapes)

    def body(*refs):
        p_ref, outs = refs[0], refs[1:]
        acc = p_ref[0]
        for i in range(1, N_DEV):
            acc = acc + p_ref[i]
        for vi, i, c0, width, r in plan:
            outs[vi][i:i + 1, c0:c0 + width] = acc[r:r + 1, 0:width]

    return pl.pallas_call(body, name="sum_unpack_small", out_shape=[jax.ShapeDtypeStruct(s, F32) for s in shapes])(packs)


def _adamw_many(ws, gs, ms, vs):
    n = len(ws)

    def body(*refs):
        w_refs, g_refs, m_refs, v_refs = (refs[k * n:(k + 1) * n] for k in range(4))
        d_refs, nm_refs, nv_refs = (refs[(4 + k) * n:(5 + k) * n] for k in range(3))
        for k in range(n):
            g_ = g_refs[k][...]
            m_ = ADAM_B1 * m_refs[k][...] + (1.0 - ADAM_B1) * g_
            v_ = ADAM_B2 * v_refs[k][...] + (1.0 - ADAM_B2) * (g_ * g_)
            m_hat = m_ / (1.0 - ADAM_B1 ** ADAM_STEP)
            v_hat = v_ / (1.0 - ADAM_B2 ** ADAM_STEP)
            d_refs[k][...] = -ADAM_LR * (m_hat / (jnp.sqrt(v_hat) + ADAM_EPS) + ADAM_WD * w_refs[k][...])
            nm_refs[k][...] = m_
            nv_refs[k][...] = v_

    shapes = [jax.ShapeDtypeStruct(a.shape, F32) for a in ws]
    outs = pl.pallas_call(body, name="adamw_small", out_shape=shapes * 3)(*ws, *gs, *ms, *vs)
    return outs[:n], outs[n:2 * n], outs[2 * n:]


def kernel(x, mem, pre_mix_g, w_in, q_norm_g, k_norm_g, hg_lb, hg_out_norm_g, w_out, post_mix_g, pre_x_g, mem_norm_g, w_xq, w_xkv, w_xo, post_x_g, pre_ffn_g, w_up, conv_w, conv_b, w_down, post_ffn_g, loss_target, m_pre_mix_g, m_w_in, m_q_norm_g, m_k_norm_g, m_hg_lb, m_hg_out_norm_g, m_w_out, m_post_mix_g, m_pre_x_g, m_mem_norm_g, m_w_xq, m_w_xkv, m_w_xo, m_post_x_g, m_pre_ffn_g, m_w_up, m_conv_w, m_conv_b, m_w_down, m_post_ffn_g, v_pre_mix_g, v_w_in, v_q_norm_g, v_k_norm_g, v_hg_lb, v_hg_out_norm_g, v_w_out, v_post_mix_g, v_pre_x_g, v_mem_norm_g, v_w_xq, v_w_xkv, v_w_xo, v_post_x_g, v_pre_ffn_g, v_w_up, v_conv_w, v_conv_b, v_w_down, v_post_ffn_g):
    args = dict(locals())
    w = {n: args[n] for n in _WEIGHTS}
    m = {n: args["m_" + n] for n in _WEIGHTS}
    v = {n: args["v_" + n] for n in _WEIGHTS}
    me = _dev_index(*_mesh_pos())

    wire = {n: _bf(w[n][0].T) for n in _COL_SHARDED}
    wire.update({n: _bf(w[n][0]) for n in _ROW_SHARDED})

    loss, grad_x, g, pending = _local_step(x[0], mem[0], loss_target[0], w, wire)

    grads, delta, new_m, new_v = {}, {}, {}, {}

    me_arr = jnp.reshape(me, (1,)).astype(jnp.int32)

    def update(n, parts):
        gsum = _sum_parts(*parts, me_arr, "sum_" + n)
        if n in _ADAM_TRANSPOSED:
            grads[n] = gsum.T[None]
            d_, m_, v_ = _adamw(w[n][0].T, gsum, m[n][0].T, v[n][0].T, "adamw_" + n)
            delta[n], new_m[n], new_v[n] = d_.T[None], m_.T[None], v_.T[None]
        else:
            gsum = gsum.T if n in _COL_SHARDED else gsum
            grads[n] = gsum[None]
            d_, m_, v_ = _adamw(w[n][0], gsum, m[n][0], v[n][0], "adamw_" + n)
            delta[n], new_m[n], new_v[n] = d_[None], m_[None], v_[None]

    after = grad_x
    for tag, names in (("down", ["w_down"]), ("up", ["w_up"]), ("x", ["w_xo", "w_xq", "w_xkv"]), ("out", ["w_out"]),
                       ("in", ["w_in"])):
        for n, parts in zip(names, _exchange_end(pending[tag], after)):
            update(n, parts)
            after = new_v[n]

    small = list(_REPLICATED) + ["hg_lb", "conv_w"]
    vals = [g[n] for n in _REPLICATED] + [g["hg_lb"].reshape(4, HG_DIM), g["conv_w"], jnp.pad(loss, ((0, 0), (0, LANE - 1)))]
    shapes = [val.shape for val in vals]
    (packs,) = _comm_alone(_Gather([_pack_small(vals)]), "gather_small_grads")
    summed = _sum_unpack_small(packs, shapes)
    loss = summed[-1][0, 0]
    for n, s in zip(small, summed[:-1]):
        grads[n] = s
    fold = lambda v2: v2[:, :ATT_HEAD_DIM] + v2[:, ATT_HEAD_DIM:]
    grads["q_norm_g"], grads["k_norm_g"] = fold(grads["q_norm_g"]), fold(grads["k_norm_g"])
    grads["hg_lb"] = lax.dynamic_slice_in_dim(grads["hg_lb"].reshape(2, 2, HG_DIM), me * (HG_DIM // N_DEV),
                                              HG_DIM // N_DEV, axis=2)
    grads["conv_w"] = lax.dynamic_slice_in_dim(grads["conv_w"], me * (2 * D_FF // N_DEV), 2 * D_FF // N_DEV, axis=1)[None]

    flat2 = lambda a: a.reshape(-1, a.shape[-1])
    outs = _adamw_many(*[[flat2(d[n]) for n in small] for d in (w, grads, m, v)])
    for dst, vals in zip((delta, new_m, new_v), outs):
        for n, val in zip(small, vals):
            dst[n] = val.reshape(w[n].shape)

    return (loss, grad_x[None], *[grads[n] for n in _WEIGHTS], *[delta[n] for n in _WEIGHTS],
            *[new_m[n] for n in _WEIGHTS], *[new_v[n] for n in _WEIGHTS])
```

```python
import jax
import jax.numpy as jnp
from jax import lax
from jax.experimental import pallas as pl
from jax.experimental.pallas import tpu as pltpu

F32 = jnp.float32
BF16 = jnp.bfloat16

D_MODEL = 1024
GRID_W = 64
EPS = 1e-6
ATT_HEADS = 8
ATT_HEAD_DIM = 64
ATT_Q_DIM = 512
ATT_KV_DIM = 128
ROPE_THETA = 10000.0
HG_HEADS = 4
HG_DIM = 512
HG_CHUNK = 32
HG_CHUNK_LOG2 = 5
HG_BLOCK = 256
N_IN = 3328
X_HEADS = 4
X_HEAD_DIM = 256
D_FF = 2816
N_DEV = 8
LANE = 128
ADAM_LR = 0.001
ADAM_B1 = 0.9
ADAM_B2 = 0.999
ADAM_EPS = 1e-08
ADAM_WD = 0.01
ADAM_STEP = 10
VMEM_LIMIT = 56 * 1024 * 1024

MESH_T = pl.DeviceIdType.MESH


def _params(**kw):
    return pltpu.CompilerParams(vmem_limit_bytes=VMEM_LIMIT, **kw)


def _dot(a, b, ca, cb):
    return lax.dot_general(a, b, (((ca,), (cb,)), ((), ())), preferred_element_type=F32)


def _bf(x):
    return x.astype(BF16)


def _sigmoid(x):
    return 1.0 / (1.0 + jnp.exp(-x))


def _rms_fwd(x, g):
    r = lax.rsqrt(jnp.mean(x * x, axis=-1, keepdims=True) + EPS)
    return x * r * g


def _rms_bwd(dy, x, g):
    r = lax.rsqrt(jnp.mean(x * x, axis=-1, keepdims=True) + EPS)
    xh = x * r
    dg = jnp.sum(dy * xh, axis=0, keepdims=True)
    t = dy * g
    dx = r * (t - xh * jnp.mean(t * xh, axis=-1, keepdims=True))
    return dx, dg


def _full(shape):
    nd = len(shape)
    return pl.BlockSpec(shape, lambda *a: (0,) * nd)


def _norm_mm(x, g, w, trans, tn, name, comm=None, tm=512):
    t, d = x.shape
    n = w.shape[0] if trans else w.shape[1]
    tm = min(tm, t)

    def body(x_ref, g_ref, w_ref, h_ref, p_ref):
        h = _bf(_rms_fwd(x_ref[...], g_ref[...]))
        h_ref[...] = h
        p_ref[...] = _dot(h, w_ref[...], 1, 1 if trans else 0)

    w_spec = pl.BlockSpec((tn, d), lambda i, j: (j, 0)) if trans else pl.BlockSpec((d, tn), lambda i, j: (0, j))
    (h, p), got = _call(
        body, (x, g, w), name=name, grid=(t // tm, n // tn),
        in_specs=[pl.BlockSpec((tm, d), lambda i, j: (i, 0)), _full((1, d)), w_spec],
        out_specs=[pl.BlockSpec((tm, d), lambda i, j: (i, 0)), pl.BlockSpec((tm, tn), lambda i, j: (i, j))],
        out_shape=[jax.ShapeDtypeStruct((t, d), BF16), jax.ShapeDtypeStruct((t, n), F32)], comm=comm)
    return (h, p, got) if comm is not None else (h, p)


def _mm_postnorm_res(a, w, g, res, name, tm=256):
    t, k = a.shape
    d = w.shape[1]

    def body(a_ref, w_ref, g_ref, res_ref, y_ref, o_ref):
        y = _dot(a_ref[...], w_ref[...], 1, 0)
        y_ref[...] = y
        o_ref[...] = res_ref[...] + _rms_fwd(y, g_ref[...])

    row = lambda width: pl.BlockSpec((tm, width), lambda i: (i, 0))
    return pl.pallas_call(
        body, name=name, grid=(t // tm,),
        in_specs=[row(k), _full((k, d)), _full((1, d)), row(d)],
        out_specs=[row(d), row(d)],
        out_shape=[jax.ShapeDtypeStruct((t, d), F32)] * 2,
        compiler_params=_params(),
    )(a, w, g, res)


def _mm_postnorm_res_loss(a, w, g, res, tgt, name, tm=256):
    t, k = a.shape
    d = w.shape[1]

    def body(a_ref, w_ref, g_ref, res_ref, tgt_ref, y_ref, dout_ref, loss_ref):
        @pl.when(pl.program_id(0) == 0)
        def _():
            loss_ref[...] = jnp.zeros_like(loss_ref)

        y = _dot(a_ref[...], w_ref[...], 1, 0)
        y_ref[...] = y
        diff = res_ref[...] + _rms_fwd(y, g_ref[...]) - tgt_ref[...]
        dout_ref[...] = diff * (1.0 / d)
        part = jnp.sum(jnp.sum(diff * diff, axis=-1, keepdims=True), axis=0, keepdims=True)
        loss_ref[...] += (0.5 / d) * part

    row = lambda width: pl.BlockSpec((tm, width), lambda i: (i, 0))
    return pl.pallas_call(
        body, name=name, grid=(t // tm,),
        in_specs=[row(k), _full((k, d)), _full((1, d)), row(d), row(d)],
        out_specs=[row(d), row(d), _full((1, 1))],
        out_shape=[jax.ShapeDtypeStruct((t, d), F32)] * 2 + [jax.ShapeDtypeStruct((1, 1), F32)],
        compiler_params=_params(),
    )(a, w, g, res, tgt)


def _postnorm_bwd_mm(dout, y, g, w, name, tm=256):
    t, d = y.shape
    k = w.shape[0]

    def body(dout_ref, y_ref, g_ref, w_ref, dy_ref, da_ref, dg_ref):
        @pl.when(pl.program_id(0) == 0)
        def _():
            dg_ref[...] = jnp.zeros_like(dg_ref)

        dy, dg = _rms_bwd(dout_ref[...], y_ref[...], g_ref[...])
        dg_ref[...] += dg
        dyb = _bf(dy)
        dy_ref[...] = dyb
        da_ref[...] = _dot(dyb, w_ref[...], 1, 1)

    row = lambda width: pl.BlockSpec((tm, width), lambda i: (i, 0))
    return pl.pallas_call(
        body, name=name, grid=(t // tm,),
        in_specs=[row(d), row(d), _full((1, d)), _full((k, d))],
        out_specs=[row(d), row(k), _full((1, d))],
        out_shape=[jax.ShapeDtypeStruct((t, d), BF16), jax.ShapeDtypeStruct((t, k), F32), jax.ShapeDtypeStruct((1, d), F32)],
        compiler_params=_params(),
    )(dout, y, g, w)


def _mm_prenorm_bwd(dp, w, trans, x, g, dres, tk, name, comm=None, tm=512):
    dps = list(dp) if isinstance(dp, (list, tuple)) else [dp]
    nparts = len(dps)
    t, n_each = dps[0].shape
    d = x.shape[1]
    tm = min(tm, t)
    nk_each = n_each // tk
    nk = nk_each * nparts

    def body(*refs):
        dp_refs = refs[:nparts]
        w_ref, x_ref, g_ref, dres_ref, dx_ref, dg_ref, acc_ref = refs[nparts:]
        i, kk = pl.program_id(0), pl.program_id(1)

        @pl.when((i == 0) & (kk == 0))
        def _():
            dg_ref[...] = jnp.zeros_like(dg_ref)

        @pl.when(kk == 0)
        def _():
            acc_ref[...] = jnp.zeros_like(acc_ref)

        for part, dp_ref in enumerate(dp_refs):
            @pl.when((kk >= part * nk_each) & (kk < (part + 1) * nk_each))
            def _():
                acc_ref[...] += _dot(dp_ref[...], w_ref[...], 1, 0 if trans else 1)

        @pl.when(kk == nk - 1)
        def _():
            dx, dg = _rms_bwd(acc_ref[...], x_ref[...], g_ref[...])
            dg_ref[...] += dg
            dx_ref[...] = dres_ref[...] + dx

    dp_spec = lambda part: pl.BlockSpec((tm, tk), lambda i, kk: (i, jnp.clip(kk - part * nk_each, 0, nk_each - 1)))
    w_spec = pl.BlockSpec((tk, d), lambda i, kk: (kk, 0)) if trans else pl.BlockSpec((d, tk), lambda i, kk: (0, kk))
    row = pl.BlockSpec((tm, d), lambda i, kk: (i, 0))
    return _call(
        body, (*dps, w, x, g, dres), name=name, grid=(t // tm, nk),
        in_specs=[dp_spec(part) for part in range(nparts)] + [w_spec, row, _full((1, d)), row],
        out_specs=[row, _full((1, d))],
        out_shape=[jax.ShapeDtypeStruct((t, d), F32), jax.ShapeDtypeStruct((1, d), F32)],
        scratch_shapes=[pltpu.VMEM((tm, d), F32)], comm=comm)


def _dw(a, b, name, bcol=None, tka=256):
    parts = list(a) if isinstance(a, (list, tuple)) else [a]
    nparts = len(parts)
    t, ka_each = parts[0].shape
    bci, nb = (0, b.shape[1]) if bcol is None else bcol
    nt_each = ka_each // tka

    def body(*refs):
        a_refs, b_ref, o_ref = refs[:nparts], refs[nparts], refs[nparts + 1]
        i = pl.program_id(0)
        for part, a_ref in enumerate(a_refs):
            @pl.when((i >= part * nt_each) & (i < (part + 1) * nt_each))
            def _():
                o_ref[...] = _bf(_dot(a_ref[...], b_ref[...], 0, 0))

    a_spec = lambda part: pl.BlockSpec((t, tka), lambda i: (0, jnp.clip(i - part * nt_each, 0, nt_each - 1)))
    return pl.pallas_call(
        body, name=name, grid=(nt_each * nparts,),
        in_specs=[a_spec(part) for part in range(nparts)] + [pl.BlockSpec((t, nb), lambda i: (0, bci))],
        out_specs=pl.BlockSpec((tka, nb), lambda i: (i, 0)),
        out_shape=jax.ShapeDtypeStruct((nparts * ka_each, nb), BF16),
        compiler_params=_params(),
    )(*parts, b)


def _rope_tables(t):
    pos = jnp.arange(t)
    r = (pos // GRID_W).astype(F32)
    c = (pos % GRID_W).astype(F32)
    npair = ATT_HEAD_DIM // 4
    inv = jnp.power(ROPE_THETA, -jnp.arange(npair, dtype=F32) / npair)
    ang = jnp.concatenate([r[:, None] * inv, c[:, None] * inv], axis=-1)
    cos = jnp.repeat(jnp.cos(ang), 2, axis=-1)
    sin = jnp.repeat(jnp.sin(ang), 2, axis=-1)
    even = (jnp.arange(ATT_HEAD_DIM) % 2) == 0
    sa = jnp.where(even, -sin, 0.0)
    sb = jnp.where(even, 0.0, sin)
    two = lambda a: jnp.tile(a, (1, 2))
    return two(cos), two(sa), two(sb)


def _head_sum_matrix():
    a = jnp.arange(LANE) // ATT_HEAD_DIM
    return (a[:, None] == a[None, :]).astype(BF16)


def _head_mean(v, bd):
    hi = _bf(v)
    lo = _bf(v - hi.astype(F32))
    return (_dot(hi, bd, 1, 0) + _dot(lo, bd, 1, 0)) * (1.0 / ATT_HEAD_DIM)


def _qk_prep(p, gq, gk, tables, bd, tm=512):
    t = p.shape[0]
    tm = min(tm, t)
    cc, sa, sb = tables

    def body(p_ref, gq_ref, gk_ref, cc_ref, sa_ref, sb_ref, bd_ref, q_ref, k_ref):
        cc_, sa_, sb_, bd_ = cc_ref[...], sa_ref[...], sb_ref[...], bd_ref[...]
        low = lax.broadcasted_iota(jnp.int32, (tm, LANE), 1) < ATT_HEAD_DIM

        def normrope(xs, g):
            xn = xs * lax.rsqrt(_head_mean(xs * xs, bd_) + EPS) * g
            return xn * cc_ + pltpu.roll(xn, LANE - 1, 1) * sa_ + pltpu.roll(xn, 1, 1) * sb_

        for j in range(4):
            y = normrope(p_ref[:, j * LANE:(j + 1) * LANE], gq_ref[...]) * (ATT_HEAD_DIM ** -0.5)
            yr = pltpu.roll(y, ATT_HEAD_DIM, 1)
            if j // 2 == 0:
                h0, h1 = jnp.where(low, y, 0.0), jnp.where(low, yr, 0.0)
            else:
                h0, h1 = jnp.where(low, 0.0, yr), jnp.where(low, 0.0, y)
            q_ref[:, (2 * j) * LANE:(2 * j + 1) * LANE] = _bf(h0)
            q_ref[:, (2 * j + 1) * LANE:(2 * j + 2) * LANE] = _bf(h1)
        k_ref[...] = _bf(normrope(p_ref[:, ATT_Q_DIM:ATT_Q_DIM + LANE], gk_ref[...]))

    row = lambda width: pl.BlockSpec((tm, width), lambda i: (i, 0))
    return pl.pallas_call(
        body, name="qk_prep", grid=(t // tm,),
        in_specs=[row(ATT_Q_DIM + LANE), _full((1, LANE)), _full((1, LANE)), row(LANE), row(LANE), row(LANE),
                  _full((LANE, LANE))],
        out_specs=[row(ATT_HEADS * LANE), row(LANE)],
        out_shape=[jax.ShapeDtypeStruct((t, ATT_HEADS * LANE), BF16), jax.ShapeDtypeStruct((t, LANE), BF16)],
        compiler_params=_params(),
    )(p, gq, gk, cc, sa, sb, bd)


def _qk_prep_bwd(p, dq, dk, gq, gk, tables, bd, tm=512):
    t = p.shape[0]
    tm = min(tm, t)
    cc, sa, sb = tables

    def body(p_ref, dq_ref, dk_ref, gq_ref, gk_ref, cc_ref, sa_ref, sb_ref, bd_ref, dp_ref, dgq_ref, dgk_ref):
        @pl.when(pl.program_id(0) == 0)
        def _():
            dgq_ref[...] = jnp.zeros_like(dgq_ref)
            dgk_ref[...] = jnp.zeros_like(dgk_ref)

        cc_, sa_, sb_, bd_ = cc_ref[...], sa_ref[...], sb_ref[...], bd_ref[...]
        low = lax.broadcasted_iota(jnp.int32, (tm, LANE), 1) < ATT_HEAD_DIM

        def bwd(xs, g, dy):
            r = lax.rsqrt(_head_mean(xs * xs, bd_) + EPS)
            xh = xs * r
            dxn = dy * cc_ + pltpu.roll(dy * sa_, 1, 1) + pltpu.roll(dy * sb_, LANE - 1, 1)
            dg = jnp.sum(dxn * xh, axis=0, keepdims=True)
            tt = dxn * g
            return r * (tt - xh * _head_mean(tt * xh, bd_)), dg

        dgq = jnp.zeros((1, LANE), F32)
        for j in range(4):
            d0 = dq_ref[:, (2 * j) * LANE:(2 * j + 1) * LANE]
            d1 = dq_ref[:, (2 * j + 1) * LANE:(2 * j + 2) * LANE]
            if j // 2 == 0:
                dy = jnp.where(low, d0, pltpu.roll(d1, ATT_HEAD_DIM, 1))
            else:
                dy = jnp.where(low, pltpu.roll(d0, ATT_HEAD_DIM, 1), d1)
            dx, dg = bwd(p_ref[:, j * LANE:(j + 1) * LANE], gq_ref[...], dy * (ATT_HEAD_DIM ** -0.5))
            dp_ref[:, j * LANE:(j + 1) * LANE] = _bf(dx)
            dgq = dgq + dg
        dgq_ref[...] += dgq
        dx, dg = bwd(p_ref[:, ATT_Q_DIM:ATT_Q_DIM + LANE], gk_ref[...], dk_ref[...])
        dp_ref[:, ATT_Q_DIM:ATT_Q_DIM + LANE] = _bf(dx)
        dgk_ref[...] += dg

    row = lambda width: pl.BlockSpec((tm, width), lambda i: (i, 0))
    return pl.pallas_call(
        body, name="qk_prep_bwd", grid=(t // tm,),
        in_specs=[row(ATT_Q_DIM + LANE), row(ATT_HEADS * LANE), row(LANE), _full((1, LANE)), _full((1, LANE)),
                  row(LANE), row(LANE), row(LANE), _full((LANE, LANE))],
        out_specs=[row(ATT_Q_DIM + LANE), _full((1, LANE)), _full((1, LANE))],
        out_shape=[jax.ShapeDtypeStruct((t, ATT_Q_DIM + LANE), BF16), jax.ShapeDtypeStruct((1, LANE), F32),
                   jax.ShapeDtypeStruct((1, LANE), F32)],
        compiler_params=_params(),
    )(p, dq, dk, gq, gk, cc, sa, sb, bd)


def _attn_fwd(q, k, p, comm=None, tq=256):
    t = k.shape[0]
    tq = min(tq, t)
    v_blk = (ATT_Q_DIM + ATT_KV_DIM) // LANE

    def body(q_ref, k_ref, v_ref, o_ref, o32_ref, lse_ref):
        k_ = k_ref[...]
        v = v_ref[...]
        lowk = lax.broadcasted_iota(jnp.int32, (t, LANE), 1) < ATT_HEAD_DIM
        vm = (_bf(jnp.where(lowk, v, 0.0)), _bf(jnp.where(lowk, 0.0, v)))
        for j in range(4):
            kvh = j // 2
            acc = None
            for sub in range(2):
                h = 2 * j + sub
                s = _dot(q_ref[:, h * LANE:(h + 1) * LANE], k_, 1, 1)
                mx = jnp.max(s, axis=-1, keepdims=True)
                e = jnp.exp(s - mx)
                l = jnp.sum(e, axis=-1, keepdims=True)
                lse_ref[h] = mx + jnp.log(l)
                o = _dot(_bf(e), vm[kvh], 1, 0) * (1.0 / l)
                if sub != kvh:
                    o = pltpu.roll(o, ATT_HEAD_DIM, 1)
                acc = o if acc is None else acc + o
            o32_ref[:, j * LANE:(j + 1) * LANE] = acc
            o_ref[:, j * LANE:(j + 1) * LANE] = _bf(acc)

    row = pl.BlockSpec((tq, ATT_Q_DIM), lambda i: (i, 0))
    outs, got = _call(
        body, (q, k, p), name="attn_fwd", grid=(t // tq,),
        in_specs=[pl.BlockSpec((tq, ATT_HEADS * LANE), lambda i: (i, 0)), _full((t, LANE)),
                  pl.BlockSpec((t, LANE), lambda i: (0, v_blk))],
        out_specs=[row, row, pl.BlockSpec((ATT_HEADS, tq, 1), lambda i: (0, i, 0))],
        out_shape=[jax.ShapeDtypeStruct((t, ATT_Q_DIM), BF16), jax.ShapeDtypeStruct((t, ATT_Q_DIM), F32),
                   jax.ShapeDtypeStruct((ATT_HEADS, t, 1), F32)], comm=comm)
    return outs, got


def _attn_bwd(q, k, p, dcat, o32, lse, comm=None, tq=256):
    t = k.shape[0]
    tq = min(tq, t)
    v_blk = (ATT_Q_DIM + ATT_KV_DIM) // LANE

    def body(q_ref, k_ref, v_ref, do_ref, o_ref, lse_ref, dq_ref, dk_ref, dv_ref):
        @pl.when(pl.program_id(0) == 0)
        def _():
            dk_ref[...] = jnp.zeros_like(dk_ref)
            dv_ref[...] = jnp.zeros_like(dv_ref)

        k_ = k_ref[...]
        vb = _bf(v_ref[...])
        lowq = lax.broadcasted_iota(jnp.int32, (tq, LANE), 1) < ATT_HEAD_DIM
        dk_acc = jnp.zeros((t, LANE), F32)
        dv_acc = jnp.zeros((t, LANE), F32)
        for j in range(4):
            kvh = j // 2
            dop = do_ref[:, j * LANE:(j + 1) * LANE]
            prod = dop * o_ref[:, j * LANE:(j + 1) * LANE]
            d_low = jnp.sum(jnp.where(lowq, prod, 0.0), axis=-1, keepdims=True)
            d_sub = (d_low, jnp.sum(prod, axis=-1, keepdims=True) - d_low)
            for sub in range(2):
                h = 2 * j + sub
                src = dop if sub == kvh else pltpu.roll(dop, ATT_HEAD_DIM, 1)
                do_h = _bf(jnp.where(lowq, src, 0.0) if kvh == 0 else jnp.where(lowq, 0.0, src))
                qh = q_ref[:, h * LANE:(h + 1) * LANE]
                pr = jnp.exp(_dot(qh, k_, 1, 1) - lse_ref[h])
                ds = _bf(pr * (_dot(do_h, vb, 1, 1) - d_sub[sub]))
                dq_ref[:, h * LANE:(h + 1) * LANE] = _dot(ds, k_, 1, 0)
                dk_acc = dk_acc + _dot(ds, qh, 0, 0)
                dv_acc = dv_acc + _dot(_bf(pr), do_h, 0, 0)
        dk_ref[...] += dk_acc
        dv_ref[...] += dv_acc

    row = pl.BlockSpec((tq, ATT_Q_DIM), lambda i: (i, 0))
    return _call(
        body, (q, k, p, dcat, o32, lse), name="attn_bwd", grid=(t // tq,),
        in_specs=[pl.BlockSpec((tq, ATT_HEADS * LANE), lambda i: (i, 0)), _full((t, LANE)),
                  pl.BlockSpec((t, LANE), lambda i: (0, v_blk)), row, row,
                  pl.BlockSpec((ATT_HEADS, tq, 1), lambda i: (0, i, 0))],
        out_specs=[pl.BlockSpec((tq, ATT_HEADS * LANE), lambda i: (i, 0)), _full((t, LANE)), _full((t, LANE))],
        out_shape=[jax.ShapeDtypeStruct((t, ATT_HEADS * LANE), F32), jax.ShapeDtypeStruct((t, LANE), F32),
                   jax.ShapeDtypeStruct((t, LANE), F32)], comm=comm)


def _lower_bounds(a0, a1):
    def body(a0_ref, a1_ref, lb_ref):
        m = jnp.maximum(a0_ref[...], a1_ref[...])
        e0, e1 = jnp.exp(a0_ref[...] - m), jnp.exp(a1_ref[...] - m)
        lb_ref[...] = e0 / (e0 + e1)

    return pl.pallas_call(body, name="lower_bounds", out_shape=jax.ShapeDtypeStruct(a0.shape, F32))(a0, a1)


def _lower_bounds_bwd(a0, a1, dlb):
    def body(a0_ref, a1_ref, dlb_ref, d0_ref, d1_ref):
        m = jnp.maximum(a0_ref[...], a1_ref[...])
        e0, e1 = jnp.exp(a0_ref[...] - m), jnp.exp(a1_ref[...] - m)
        lb = e0 / (e0 + e1)
        d0 = dlb_ref[...] * lb * (1.0 - lb)
        d0_ref[...] = d0
        d1_ref[...] = -d0

    return pl.pallas_call(body, name="lower_bounds_bwd", out_shape=[jax.ShapeDtypeStruct(a0.shape, F32)] * 2)(a0, a1, dlb)


def _chunk_scan(x, pos, down):
    n = x.shape[0]
    s = 1
    while s < HG_CHUNK:
        if down:
            x = x + jnp.where(pos >= s, pltpu.roll(x, s, 0), 0.0)
        else:
            x = x + jnp.where(pos < HG_CHUNK - s, pltpu.roll(x, n - s, 0), 0.0)
        s *= 2
    return x


def _chunk_concat(x, cid, nc):
    return jnp.concatenate([_bf(jnp.where(cid == c, x, 0.0)) for c in range(nc)], axis=1)


def _chunk_pick(x, cid, nc):
    out = jnp.where(cid == 0, x[:, :LANE], 0.0)
    for c in range(1, nc):
        out = out + jnp.where(cid == c, x[:, c * LANE:(c + 1) * LANE], 0.0)
    return out


def _hgrn_gates(hq, z, lb):
    sq = _sigmoid(hq)
    sig = _sigmoid(z)
    f = lb + (1.0 - lb) * sig
    return hq * sq, sq, sig, f, jnp.log(f)


def _hgrn_local(q, f, g, v, pos, cid, amask, rev, nc):
    k = 1.0 - f
    ba = _chunk_scan(g, pos, not rev)
    bb = _chunk_scan(g, pos, rev)
    eq = 0.5 * (ba - bb + g)
    ex_q, ex_k, ex_i, ex_d = jnp.exp(eq), jnp.exp(-eq), jnp.exp(ba), jnp.exp(bb - g)
    qb, kb, qi, kd = q * ex_q, k * ex_k, q * ex_i, k * ex_d
    dvec = jnp.exp(ba + bb - g)
    a = jnp.where(amask, _dot(_bf(qb), _bf(kb), 1, 1), 0.0)
    kd_m = _chunk_concat(kd, cid, nc)
    qi_m = _chunk_concat(qi, cid, nc)
    ut_all = _dot(_bf(v), kd_m, 0, 0)
    return dict(k=k, ex_q=ex_q, ex_k=ex_k, ex_i=ex_i, ex_d=ex_d, qb=qb, kb=kb, qi=qi, kd=kd, dvec=dvec, a=a,
                kd_m=kd_m, qi_m=qi_m, ut_all=ut_all)


def _hgrn_masks(n, rev):
    row = lax.broadcasted_iota(jnp.int32, (n, LANE), 0)
    ti = lax.broadcasted_iota(jnp.int32, (n, n), 0)
    si = lax.broadcasted_iota(jnp.int32, (n, n), 1)
    tri = (si >= ti) if rev else (si <= ti)
    same = jnp.right_shift(ti, HG_CHUNK_LOG2) == jnp.right_shift(si, HG_CHUNK_LOG2)
    return jnp.bitwise_and(row, HG_CHUNK - 1), jnp.right_shift(row, HG_CHUNK_LOG2), same & tri


def _hgrn_specs(t, nb, rev, bwd):
    n = min(HG_BLOCK, t)
    if rev != bwd:
        blk = lambda i: nb - 1 - i
    else:
        blk = lambda i: i
    col = lambda base: pl.BlockSpec((n, 2 * LANE), lambda hp, i: (blk(i), base + hp))
    return n, blk, col


def _hgrn_fwd(p, lb, rev, comm=None):
    t = p.shape[0]
    nb = t // min(HG_BLOCK, t)
    n, blk, col = _hgrn_specs(t, nb, rev, False)
    nc = n // HG_CHUNK
    z_base = 7 if rev else 5

    def body(hq_ref, z_ref, hi_ref, lb_ref, o_ref, ssave_ref, st_scr):
        @pl.when(pl.program_id(1) == 0)
        def _():
            st_scr[...] = jnp.zeros_like(st_scr)

        pos, cid, amask = _hgrn_masks(n, rev)
        order = list(range(nc))[::-1] if rev else list(range(nc))
        for hh in range(2):
            sl = slice(hh * LANE, (hh + 1) * LANE)
            q, _, _, f, g = _hgrn_gates(hq_ref[:, sl], z_ref[:, sl], lb_ref[0:1, sl])
            v = hi_ref[:, sl]
            c_ = _hgrn_local(q, f, g, v, pos, cid, amask, rev, nc)
            st = st_scr[hh]
            ssave_ref[0, sl, :] = st
            cols = [None] * nc
            for c in order:
                cols[c] = st
                st = st * c_["dvec"][c * HG_CHUNK:c * HG_CHUNK + 1, :] + c_["ut_all"][:, c * LANE:(c + 1) * LANE]
            st_scr[hh] = st
            st_all = _bf(jnp.concatenate(cols, axis=1))
            o_ref[:, sl] = _dot(_bf(c_["a"]), _bf(v), 1, 0) + _dot(c_["qi_m"], st_all, 1, 1)

    return _call(
        body, (p, p, p, lb), name="hgrn_fwd_rev" if rev else "hgrn_fwd", grid=(2, nb),
        in_specs=[col(3), col(z_base), col(9), pl.BlockSpec((1, 2 * LANE), lambda hp, i: (0, hp))],
        out_specs=[pl.BlockSpec((n, 2 * LANE), lambda hp, i: (blk(i), hp)),
                   pl.BlockSpec((1, 2 * LANE, LANE), lambda hp, i: (blk(i), hp, 0))],
        out_shape=[jax.ShapeDtypeStruct((t, HG_DIM), F32), jax.ShapeDtypeStruct((nb, HG_DIM, LANE), F32)],
        scratch_shapes=[pltpu.VMEM((2, LANE, LANE), F32)], comm=comm)


def _hgrn_bwd(p, lb, do, ssave, rev, comm=None):
    t = p.shape[0]
    nb = t // min(HG_BLOCK, t)
    n, blk, col = _hgrn_specs(t, nb, rev, True)
    nc = n // HG_CHUNK
    z_base = 7 if rev else 5

    def body(hq_ref, z_ref, hi_ref, lb_ref, do_ref, ssave_ref, dhq_ref, dz_ref, dhi_ref, dlb_ref, dst_scr):
        @pl.when(pl.program_id(1) == 0)
        def _():
            dst_scr[...] = jnp.zeros_like(dst_scr)
            dlb_ref[...] = jnp.zeros_like(dlb_ref)

        pos, cid, amask = _hgrn_masks(n, rev)
        order = list(range(nc))[::-1] if rev else list(range(nc))
        for hh in range(2):
            sl = slice(hh * LANE, (hh + 1) * LANE)
            hq, lbv = hq_ref[:, sl], lb_ref[0:1, sl]
            q, sq, sig, f, g = _hgrn_gates(hq, z_ref[:, sl], lbv)
            v = hi_ref[:, sl]
            c_ = _hgrn_local(q, f, g, v, pos, cid, amask, rev, nc)
            dvec, ut_all = c_["dvec"], c_["ut_all"]
            drow = lambda c: dvec[c * HG_CHUNK:c * HG_CHUNK + 1, :]
            st = ssave_ref[0, sl, :]
            cols = [None] * nc
            for c in order:
                cols[c] = st
                st = st * drow(c) + ut_all[:, c * LANE:(c + 1) * LANE]
            dob, vb = _bf(do_ref[:, sl]), _bf(v)
            gt_all = _dot(dob, c_["qi_m"], 0, 0)
            dcur = dst_scr[hh]
            dnext = [None] * nc
            ddrow = [None] * nc
            for c in order[::-1]:
                dnext[c] = dcur
                ddrow[c] = jnp.sum(cols[c] * dcur, axis=0, keepdims=True) * drow(c)
                dcur = dcur * drow(c) + gt_all[:, c * LANE:(c + 1) * LANE]
            dst_scr[hh] = dcur
            dsn_all = _bf(jnp.concatenate(dnext, axis=1))
            st_all = _bf(jnp.concatenate(cols, axis=1))
            da = _bf(jnp.where(amask, _dot(dob, vb, 1, 1), 0.0))
            dv = _dot(_bf(c_["a"]), dob, 0, 0) + _dot(c_["kd_m"], dsn_all, 1, 1)
            dqb = _dot(da, _bf(c_["kb"]), 1, 0)
            dkb = _dot(da, _bf(c_["qb"]), 0, 0)
            dqi = _chunk_pick(_dot(dob, st_all, 1, 0), cid, nc)
            dkd = _chunk_pick(_dot(vb, dsn_all, 1, 0), cid, nc)
            dq = dqb * c_["ex_q"] + dqi * c_["ex_i"]
            dk = dkb * c_["ex_k"] + dkd * c_["ex_d"]
            e = dqb * c_["qb"] - dkb * c_["kb"] + dqi * c_["qi"]
            w = dkd * c_["kd"]
            dtot = jnp.where(cid == 0, ddrow[0], 0.0)
            for c in range(1, nc):
                dtot = dtot + jnp.where(cid == c, ddrow[c], 0.0)
            dg = _chunk_scan(e, pos, rev) + (_chunk_scan(w, pos, not rev) - w) + dtot
            df = dg / f - dk
            dz_ref[:, sl] = df * (1.0 - lbv) * sig * (1.0 - sig)
            dlb_ref[0:1, sl] += jnp.sum(df * (1.0 - sig), axis=0, keepdims=True)
            dhq_ref[:, sl] = dq * (sq * (1.0 + hq * (1.0 - sq)))
            dhi_ref[:, sl] = dv

    out = pl.BlockSpec((n, 2 * LANE), lambda hp, i: (blk(i), hp))
    return _call(
        body, (p, p, p, lb, do, ssave), name="hgrn_bwd_rev" if rev else "hgrn_bwd", grid=(2, nb),
        in_specs=[col(3), col(z_base), col(9), pl.BlockSpec((1, 2 * LANE), lambda hp, i: (0, hp)), out,
                  pl.BlockSpec((1, 2 * LANE, LANE), lambda hp, i: (blk(i), hp, 0))],
        out_specs=[out, out, out, pl.BlockSpec((1, 2 * LANE), lambda hp, i: (0, hp))],
        out_shape=[jax.ShapeDtypeStruct((t, HG_DIM), F32)] * 3 + [jax.ShapeDtypeStruct((1, HG_DIM), F32)],
        scratch_shapes=[pltpu.VMEM((2, LANE, LANE), F32)], comm=comm)


def _mix_out(att, of, ob, p, gout, w_out, g_post, x, comm=None, tm=256):
    t, d = x.shape
    tm = min(tm, t)

    def body(att_ref, of_ref, ob_ref, hg0_ref, hg1_ref, gout_ref, w_ref, g_ref, x_ref, cat_ref, y_ref, x1_ref):
        cat_ref[:, :ATT_Q_DIM] = att_ref[...]
        for h in range(HG_HEADS):
            sl = slice(h * LANE, (h + 1) * LANE)
            hg_ref = hg0_ref if h < 2 else hg1_ref
            hg = hg_ref[:, (h % 2) * LANE:(h % 2 + 1) * LANE]
            nrm = _rms_fwd(of_ref[:, sl] + ob_ref[:, sl], gout_ref[...])
            cat_ref[:, ATT_Q_DIM + h * LANE:ATT_Q_DIM + (h + 1) * LANE] = _bf(nrm * hg * _sigmoid(hg))
        y = _dot(cat_ref[...], w_ref[...], 1, 0)
        y_ref[...] = y
        x1_ref[...] = x_ref[...] + _rms_fwd(y, g_ref[...])

    row = lambda width: pl.BlockSpec((tm, width), lambda i: (i, 0))
    return _call(
        body, (att, of, ob, p, p, gout, w_out, g_post, x), name="mix_out", grid=(t // tm,),
        in_specs=[row(ATT_Q_DIM), row(HG_DIM), row(HG_DIM), pl.BlockSpec((tm, 2 * LANE), lambda i: (i, 11)),
                  pl.BlockSpec((tm, 2 * LANE), lambda i: (i, 12)), _full((1, LANE)), _full((d, d)), _full((1, d)), row(d)],
        out_specs=[row(d), row(d), row(d)],
        out_shape=[jax.ShapeDtypeStruct((t, d), BF16), jax.ShapeDtypeStruct((t, d), F32), jax.ShapeDtypeStruct((t, d), F32)],
        comm=comm)


def _rec_bwd(dcat, of, ob, p, gout, tm=256):
    t = of.shape[0]
    tm = min(tm, t)

    def body(dc_ref, of_ref, ob_ref, hg0_ref, hg1_ref, gout_ref, do_ref, dhg_ref, dgo_ref):
        @pl.when(pl.program_id(0) == 0)
        def _():
            dgo_ref[...] = jnp.zeros_like(dgo_ref)

        dgo = jnp.zeros((1, LANE), F32)
        for h in range(HG_HEADS):
            sl = slice(h * LANE, (h + 1) * LANE)
            hg_ref = hg0_ref if h < 2 else hg1_ref
            hg = hg_ref[:, (h % 2) * LANE:(h % 2 + 1) * LANE]
            o = of_ref[:, sl] + ob_ref[:, sl]
            sg = _sigmoid(hg)
            drec = dc_ref[:, sl]
            dhg_ref[:, sl] = drec * _rms_fwd(o, gout_ref[...]) * (sg * (1.0 + hg * (1.0 - sg)))
            do, dg = _rms_bwd(drec * hg * sg, o, gout_ref[...])
            do_ref[:, sl] = do
            dgo = dgo + dg
        dgo_ref[...] += dgo

    row = lambda width: pl.BlockSpec((tm, width), lambda i: (i, 0))
    return pl.pallas_call(
        body, name="rec_bwd", grid=(t // tm,),
        in_specs=[pl.BlockSpec((tm, HG_DIM), lambda i: (i, 1)), row(HG_DIM), row(HG_DIM),
                  pl.BlockSpec((tm, 2 * LANE), lambda i: (i, 11)), pl.BlockSpec((tm, 2 * LANE), lambda i: (i, 12)),
                  _full((1, LANE))],
        out_specs=[row(HG_DIM), row(HG_DIM), _full((1, LANE))],
        out_shape=[jax.ShapeDtypeStruct((t, HG_DIM), F32), jax.ShapeDtypeStruct((t, HG_DIM), F32),
                   jax.ShapeDtypeStruct((1, LANE), F32)],
        compiler_params=_params(),
    )(dcat, of, ob, p, p, gout)


def _assemble_dp(dp_qk, dv, dhq_f, dhq_b, dz_f, dz_b, dhi_f, dhi_b, dhg, tm=512):
    t = dv.shape[0]
    tm = min(tm, t)
    qk_w = ATT_Q_DIM + LANE

    def body(qk_ref, dv_ref, hqf_ref, hqb_ref, zf_ref, zb_ref, hif_ref, hib_ref, hg_ref, dp_ref):
        o = 0
        dp_ref[:, o:o + qk_w] = qk_ref[...]
        o += qk_w
        dp_ref[:, o:o + LANE] = _bf(dv_ref[...])
        o += LANE
        for val in (hqf_ref[...] + hqb_ref[...], zf_ref[...], zb_ref[...], hif_ref[...] + hib_ref[...], hg_ref[...]):
            dp_ref[:, o:o + HG_DIM] = _bf(val)
            o += HG_DIM

    row = lambda width: pl.BlockSpec((tm, width), lambda i: (i, 0))
    return pl.pallas_call(
        body, name="assemble_dp", grid=(t // tm,),
        in_specs=[row(qk_w), row(LANE)] + [row(HG_DIM)] * 7,
        out_specs=row(N_IN),
        out_shape=jax.ShapeDtypeStruct((t, N_IN), BF16),
        compiler_params=_params(),
    )(dp_qk, dv, dhq_f, dhq_b, dz_f, dz_b, dhi_f, dhi_b, dhg)


def _xattn_fwd(q, kv, tq=512):
    t, d = q.shape
    tq = min(tq, t)
    nm = kv.shape[0]

    def body(q_ref, kv_ref, o_ref):
        for h in range(X_HEADS):
            sl = slice(h * X_HEAD_DIM, (h + 1) * X_HEAD_DIM)
            s = _dot(_bf(q_ref[:, sl]), _bf(kv_ref[:, sl]), 1, 1) * (X_HEAD_DIM ** -0.5)
            e = jnp.exp(s - jnp.max(s, axis=-1, keepdims=True))
            pr = e * (1.0 / jnp.sum(e, axis=-1, keepdims=True))
            o_ref[:, sl] = _bf(_dot(_bf(pr), _bf(kv_ref[:, d + h * X_HEAD_DIM:d + (h + 1) * X_HEAD_DIM]), 1, 0))

    return pl.pallas_call(
        body, name="xattn_fwd", grid=(t // tq,),
        in_specs=[pl.BlockSpec((tq, d), lambda i: (i, 0)), _full((nm, 2 * d))],
        out_specs=pl.BlockSpec((tq, d), lambda i: (i, 0)),
        out_shape=jax.ShapeDtypeStruct((t, d), BF16),
        compiler_params=_params(),
    )(q, kv)


def _xattn_bwd(q, kv, do, tq=512):
    t, d = q.shape
    tq = min(tq, t)
    nm = kv.shape[0]

    def body(q_ref, kv_ref, do_ref, dq_ref, dkv_ref):
        @pl.when(pl.program_id(0) == 0)
        def _():
            dkv_ref[...] = jnp.zeros_like(dkv_ref)

        for h in range(X_HEADS):
            sl = slice(h * X_HEAD_DIM, (h + 1) * X_HEAD_DIM)
            slv = slice(d + h * X_HEAD_DIM, d + (h + 1) * X_HEAD_DIM)
            qb, kb, vb, dob = _bf(q_ref[:, sl]), _bf(kv_ref[:, sl]), _bf(kv_ref[:, slv]), _bf(do_ref[:, sl])
            s = _dot(qb, kb, 1, 1) * (X_HEAD_DIM ** -0.5)
            e = jnp.exp(s - jnp.max(s, axis=-1, keepdims=True))
            pr = e * (1.0 / jnp.sum(e, axis=-1, keepdims=True))
            dpr = _dot(dob, vb, 1, 1)
            ds = _bf(pr * (dpr - jnp.sum(pr * dpr, axis=-1, keepdims=True)) * (X_HEAD_DIM ** -0.5))
            dq_ref[:, sl] = _bf(_dot(ds, kb, 1, 0))
            dkv_ref[:, sl] += _dot(ds, qb, 0, 0)
            dkv_ref[:, slv] += _dot(_bf(pr), dob, 0, 0)

    return pl.pallas_call(
        body, name="xattn_bwd", grid=(t // tq,),
        in_specs=[pl.BlockSpec((tq, d), lambda i: (i, 0)), _full((nm, 2 * d)), pl.BlockSpec((tq, d), lambda i: (i, 0))],
        out_specs=[pl.BlockSpec((tq, d), lambda i: (i, 0)), _full((nm, 2 * d))],
        out_shape=[jax.ShapeDtypeStruct((t, d), BF16), jax.ShapeDtypeStruct((nm, 2 * d), F32)],
        compiler_params=_params(),
    )(q, kv, do)


CONV_TN = 256


def _shift_rows(u, row, t, delta):
    if delta < 0:
        return jnp.where(row == 0, 0.0, pltpu.roll(u, 1, 0))
    return jnp.where(row == t - 1, 0.0, pltpu.roll(u, t - 1, 0))


def _conv_gate_fwd(u, conv_w, conv_b):
    t = u.shape[0]
    nt = D_FF // CONV_TN

    def body(ug_ref, uv_ref, wg_ref, wv_ref, bg_ref, bv_ref, a_ref):
        row = lax.broadcasted_iota(jnp.int32, (t, CONV_TN), 0)

        def conv(u_ref, w_ref, b_ref):
            uu = u_ref[...]
            return (b_ref[...] + _shift_rows(uu, row, t, -1) * w_ref[0:1, :] + uu * w_ref[1:2, :]
                    + _shift_rows(uu, row, t, 1) * w_ref[2:3, :])

        gate = conv(ug_ref, wg_ref, bg_ref)
        a_ref[...] = _bf(gate * _sigmoid(gate) * conv(uv_ref, wv_ref, bv_ref))

    col = lambda rows, off: pl.BlockSpec((rows, CONV_TN), lambda j: (0, j + off))
    return pl.pallas_call(
        body, name="conv_gate_fwd", grid=(nt,),
        in_specs=[col(t, 0), col(t, nt), col(3, 0), col(3, nt), col(1, 0), col(1, nt)],
        out_specs=col(t, 0),
        out_shape=jax.ShapeDtypeStruct((t, D_FF), BF16),
        compiler_params=_params(),
    )(u, u, conv_w, conv_w, conv_b, conv_b)


def _conv_gate_bwd(u, conv_w, conv_b, da, comm=None):
    t = u.shape[0]
    nt = D_FF // CONV_TN

    def body(ug_ref, uv_ref, wg_ref, wv_ref, bg_ref, bv_ref, da_ref, dug_ref, dwg_ref, dbg_ref, duv_ref, dwv_ref, dbv_ref):
        row = lax.broadcasted_iota(jnp.int32, (t, CONV_TN), 0)
        ug, uv = ug_ref[...], uv_ref[...]
        ug_m, ug_p = _shift_rows(ug, row, t, -1), _shift_rows(ug, row, t, 1)
        uv_m, uv_p = _shift_rows(uv, row, t, -1), _shift_rows(uv, row, t, 1)
        gate = bg_ref[...] + ug_m * wg_ref[0:1, :] + ug * wg_ref[1:2, :] + ug_p * wg_ref[2:3, :]
        val = bv_ref[...] + uv_m * wv_ref[0:1, :] + uv * wv_ref[1:2, :] + uv_p * wv_ref[2:3, :]
        sg = _sigmoid(gate)
        da_ = da_ref[...]

        def emit(dacc, um, uu, up, w_ref, du_ref, dw_ref, db_ref):
            du_ref[...] = _bf(_shift_rows(dacc, row, t, 1) * w_ref[0:1, :] + dacc * w_ref[1:2, :]
                              + _shift_rows(dacc, row, t, -1) * w_ref[2:3, :])
            dw_ref[0:1, :] = jnp.sum(dacc * um, axis=0, keepdims=True)
            dw_ref[1:2, :] = jnp.sum(dacc * uu, axis=0, keepdims=True)
            dw_ref[2:3, :] = jnp.sum(dacc * up, axis=0, keepdims=True)
            db_ref[...] = jnp.sum(dacc, axis=0, keepdims=True)

        emit(da_ * val * (sg * (1.0 + gate * (1.0 - sg))), ug_m, ug, ug_p, wg_ref, dug_ref, dwg_ref, dbg_ref)
        emit(da_ * gate * sg, uv_m, uv, uv_p, wv_ref, duv_ref, dwv_ref, dbv_ref)

    col = lambda rows, off: pl.BlockSpec((rows, CONV_TN), lambda j: (0, j + off))
    half_shapes = [jax.ShapeDtypeStruct((t, D_FF), BF16), jax.ShapeDtypeStruct((3, D_FF), F32),
                   jax.ShapeDtypeStruct((1, D_FF), F32)]
    outs, got = _call(
        body, (u, u, conv_w, conv_w, conv_b, conv_b, da), name="conv_gate_bwd", grid=(nt,),
        in_specs=[col(t, 0), col(t, nt), col(3, 0), col(3, nt), col(1, 0), col(1, nt), col(t, 0)],
        out_specs=[col(t, 0), col(3, 0), col(1, 0)] * 2, out_shape=half_shapes * 2, comm=comm)
    return outs[:3], outs[3:], got


def _row_tile(r, cap):
    best = r
    for cand in range(16, cap + 1, 16):
        if r % cand == 0:
            best = cand
    return best


def _sum_parts(own, got, me, name, tr=256):
    _, r, c = got.shape
    tr = _row_tile(r, tr)

    def body(me_ref, own_ref, got_ref, o_ref):
        mine = own_ref[...].astype(F32)
        acc = None
        for i in range(N_DEV):
            term = jnp.where(me_ref[0] == i, mine, got_ref[i].astype(F32))
            acc = term if acc is None else acc + term
        o_ref[...] = acc

    return pl.pallas_call(
        body, name=name,
        grid_spec=pltpu.PrefetchScalarGridSpec(
            num_scalar_prefetch=1, grid=(r // tr,),
            in_specs=[pl.BlockSpec((None, tr, c), lambda i, me_ref: (me_ref[0], i, 0)),
                      pl.BlockSpec((N_DEV, tr, c), lambda i, me_ref: (0, i, 0))],
            out_specs=pl.BlockSpec((tr, c), lambda i, me_ref: (i, 0))),
        out_shape=jax.ShapeDtypeStruct((r, c), F32),
        compiler_params=_params(),
    )(me, own, got)


def _adamw(w, g, m, v, name, tr=256):
    r, c = w.shape
    tr = _row_tile(r, tr)

    def body(w_ref, g_ref, m_ref, v_ref, d_ref, nm_ref, nv_ref):
        g_ = g_ref[...]
        m_ = ADAM_B1 * m_ref[...] + (1.0 - ADAM_B1) * g_
        v_ = ADAM_B2 * v_ref[...] + (1.0 - ADAM_B2) * (g_ * g_)
        m_hat = m_ / (1.0 - ADAM_B1 ** ADAM_STEP)
        v_hat = v_ / (1.0 - ADAM_B2 ** ADAM_STEP)
        d_ref[...] = -ADAM_LR * (m_hat / (jnp.sqrt(v_hat) + ADAM_EPS) + ADAM_WD * w_ref[...])
        nm_ref[...] = m_
        nv_ref[...] = v_

    blk = pl.BlockSpec((tr, c), lambda i: (i, 0))
    return pl.pallas_call(
        body, name=name, grid=(r // tr,),
        in_specs=[blk] * 4, out_specs=[blk] * 3,
        out_shape=[jax.ShapeDtypeStruct((r, c), F32)] * 3,
        compiler_params=_params(),
    )(w, g, m, v)


def _mesh_pos():
    return lax.axis_index("x"), lax.axis_index("y"), lax.axis_index("c")


def _dev_index(px, py, pc):
    return 4 * px + 2 * py + pc


class _Gather:
    def __init__(self, arrs):
        self.arrs = list(arrs)
        n = len(self.arrs)
        self.out_shape = [jax.ShapeDtypeStruct((N_DEV,) + a.shape, a.dtype) for a in self.arrs]
        self.sems = [pltpu.SemaphoreType.DMA((7, n)), pltpu.SemaphoreType.DMA((7, n)), pltpu.SemaphoreType.DMA((n,))]

    def _ctx(self, ins, outs, sems):
        send_sems, recv_sems, local_sems = sems
        x, y, c = _mesh_pos()
        chips = [(1 - x, y), (x, 1 - y), (1 - x, 1 - y)]

        def copy(k, a, block, to, src=None):
            dst = outs[a].at[_dev_index(*block)]
            return pltpu.make_async_remote_copy(
                src_ref=dst if src is None else src, dst_ref=dst, send_sem=send_sems.at[k, a], recv_sem=recv_sems.at[k, a],
                device_id=to, device_id_type=MESH_T)

        n = len(ins)
        me, sibling = (x, y, c), (x, y, 1 - c)
        mine = [pltpu.make_async_copy(ins[a], outs[a].at[_dev_index(*me)], local_sems.at[a]) for a in range(n)]
        first = [copy(0, a, me, sibling, src=ins[a]) for a in range(n)]
        first += [copy(1 + j, a, me, (*chip, c), src=ins[a]) for j, chip in enumerate(chips) for a in range(n)]
        passed = [copy(4 + j, a, (*chip, c), sibling) for j, chip in enumerate(chips) for a in range(n)]
        return n, c, me, sibling, chips, copy, mine, first, passed

    def start(self, ins, outs, sems):
        _, _, _, _, _, _, mine, first, _ = self._ctx(ins, outs, sems)
        for cp in mine + first:
            cp.start()

    def forward(self, ins, outs, sems):
        n, c, me, _, chips, copy, _, _, passed = self._ctx(ins, outs, sems)
        for j, chip in enumerate(chips):
            for a in range(n):
                copy(1 + j, a, (*chip, c), me).wait_recv()
                passed[j * n + a].start()

    def finish(self, ins, outs, sems):
        n, c, me, sibling, chips, copy, mine, first, passed = self._ctx(ins, outs, sems)
        for a in range(n):
            copy(0, a, sibling, me).wait_recv()
        for j, chip in enumerate(chips):
            for a in range(n):
                copy(4 + j, a, (*chip, 1 - c), me).wait_recv()
        for cp in first + passed:
            cp.wait_send()
        for cp in mine:
            cp.wait()


class _Exchange:
    def __init__(self, arrs):
        self.arrs = list(arrs)
        n = len(self.arrs)
        self.out_shape = [jax.ShapeDtypeStruct(a.shape, a.dtype) for a in self.arrs]
        self.sems = [pltpu.SemaphoreType.DMA((7, n)), pltpu.SemaphoreType.DMA((7, n)), pltpu.SemaphoreType.DMA((n,))]

    def _copies(self, ins, outs, sems):
        send_sems, recv_sems, local_sems = sems
        x, y, c = _mesh_pos()
        me_i = _dev_index(x, y, c)
        n = len(ins)
        mine = [pltpu.make_async_copy(ins[a].at[me_i], outs[a].at[me_i], local_sems.at[a]) for a in range(n)]
        copies = []
        for k in range(1, N_DEV):
            px = 1 - x if k & 4 else x
            py = 1 - y if k & 2 else y
            pc = 1 - c if k & 1 else c
            peer_i = _dev_index(px, py, pc)
            for a in range(n):
                copies.append(pltpu.make_async_remote_copy(
                    src_ref=ins[a].at[peer_i], dst_ref=outs[a].at[me_i], send_sem=send_sems.at[k - 1, a],
                    recv_sem=recv_sems.at[k - 1, a], device_id=(px, py, pc), device_id_type=MESH_T))
        return mine, copies

    def start(self, ins, outs, sems):
        mine, copies = self._copies(ins, outs, sems)
        for cp in mine + copies:
            cp.start()

    forward = None

    def finish(self, ins, outs, sems):
        mine, copies = self._copies(ins, outs, sems)
        for cp in copies:
            cp.wait_recv()
        for cp in copies:
            cp.wait_send()
        for cp in mine:
            cp.wait()


def _comm_alone(comm, name):
    n = len(comm.arrs)

    def body(*refs):
        ins, outs, sems = refs[:n], refs[n:2 * n], refs[2 * n:]
        comm.start(ins, outs, sems)
        if comm.forward is not None:
            comm.forward(ins, outs, sems)
        comm.finish(ins, outs, sems)

    any_spec = pl.BlockSpec(memory_space=pl.ANY)
    return pl.pallas_call(body, name=name, in_specs=[any_spec] * n, out_specs=[any_spec] * n, out_shape=comm.out_shape,
                          scratch_shapes=comm.sems)(*comm.arrs)


def _peers(x, y, c):
    out = []
    for k in range(1, N_DEV):
        pos = (1 - x if k & 4 else x, 1 - y if k & 2 else y, 1 - c if k & 1 else c)
        out.append((k, pos, _dev_index(*pos)))
    return out


def _exchange_begin(arrs, tag):
    n = len(arrs)
    lands = [lax.empty(a.shape, a.dtype) for a in arrs]

    def start_body(*refs):
        ins, land = refs[:n], refs[n:2 * n]
        send_sems, recv_sems, token = refs[2 * n], refs[2 * n + 1], refs[-1]
        x, y, c = _mesh_pos()
        me_i = _dev_index(x, y, c)
        for k, pos, peer_i in _peers(x, y, c):
            for a in range(n):
                pltpu.make_async_remote_copy(
                    src_ref=ins[a].at[peer_i], dst_ref=land[a].at[me_i], send_sem=send_sems.at[(k - 1) * n + a],
                    recv_sem=recv_sems.at[(k - 1) * n + a], device_id=pos, device_id_type=MESH_T).start()
        token[...] = jnp.zeros_like(token)

    hbm = pl.BlockSpec(memory_space=pltpu.HBM)
    sem = pl.BlockSpec(memory_space=pltpu.SEMAPHORE)
    thru = [pltpu.HBM(a.shape, a.dtype) for a in arrs]
    outs = pl.pallas_call(
        start_body, name="exchange_start_" + tag,
        out_shape=[pltpu.SemaphoreType.DMA((7 * n,)), pltpu.SemaphoreType.DMA((7 * n,))] + thru + thru
        + [jax.ShapeDtypeStruct((8, LANE), F32)],
        in_specs=[hbm] * (2 * n), out_specs=[sem, sem] + [hbm] * (2 * n) + [pl.BlockSpec(memory_space=pltpu.VMEM)],
        input_output_aliases={i: 2 + i for i in range(2 * n)},
        compiler_params=pltpu.CompilerParams(has_side_effects=pltpu.SideEffectType.DATAFLOW_SIDE_EFFECTING),
    )(*[pltpu.with_memory_space_constraint(a, pltpu.HBM) for a in list(arrs) + list(lands)])
    return (tag, n, outs[0], outs[1], outs[2:2 + n], outs[2 + n:2 + 2 * n]), outs[-1][0, 0]


def _exchange_end(handle, after):
    tag, n, send_sems, recv_sems, srcs, lands = handle

    def body(*refs):
        ins, land = refs[:n], refs[n:2 * n]
        send_sems_, recv_sems_ = refs[2 * n], refs[2 * n + 1]
        x, y, c = _mesh_pos()
        me_i = _dev_index(x, y, c)
        for k, pos, peer_i in _peers(x, y, c):
            for a in range(n):
                cp = pltpu.make_async_remote_copy(
                    src_ref=ins[a].at[peer_i], dst_ref=land[a].at[me_i], send_sem=send_sems_.at[(k - 1) * n + a],
                    recv_sem=recv_sems_.at[(k - 1) * n + a], device_id=pos, device_id_type=MESH_T)
                cp.wait_send()
                cp.wait_recv()

    hbm = pl.BlockSpec(memory_space=pltpu.HBM)
    sem = pl.BlockSpec(memory_space=pltpu.SEMAPHORE)
    outs = pl.pallas_call(
        body, name="exchange_end_" + tag, out_shape=[pltpu.HBM(a.shape, a.dtype) for a in list(srcs) + list(lands)],
        in_specs=[hbm] * (2 * n) + [sem, sem, pl.BlockSpec(memory_space=pl.ANY)], out_specs=[hbm] * (2 * n),
        input_output_aliases={i: i for i in range(2 * n)},
        compiler_params=pltpu.CompilerParams(has_side_effects=pltpu.SideEffectType.DATAFLOW_SIDE_EFFECTING),
    )(*srcs, *lands, send_sems, recv_sems, after)
    return list(zip(outs[:n], outs[n:]))


def _gather_begin(arrs, tag):
    n = len(arrs)
    me = _dev_index(*_mesh_pos())
    lands = [lax.dynamic_update_slice(lax.empty((N_DEV,) + a.shape, a.dtype), a[None], (me,) + (0,) * a.ndim) for a in arrs]

    def start_body(*refs):
        ins, land = refs[:n], refs[n:2 * n]
        send_sems, recv_sems, token = refs[2 * n], refs[2 * n + 1], refs[-1]
        x, y, c = _mesh_pos()
        me_i = _dev_index(x, y, c)
        for a in range(n):
            for k, pos, _ in _peers(x, y, c):
                pltpu.make_async_remote_copy(
                    src_ref=ins[a], dst_ref=land[a].at[me_i], send_sem=send_sems.at[(k - 1) * n + a],
                    recv_sem=recv_sems.at[(k - 1) * n + a], device_id=pos, device_id_type=MESH_T).start()
        token[...] = jnp.zeros_like(token)

    hbm = pl.BlockSpec(memory_space=pltpu.HBM)
    sem = pl.BlockSpec(memory_space=pltpu.SEMAPHORE)
    thru = [pltpu.HBM(a.shape, a.dtype) for a in list(arrs) + lands]
    outs = pl.pallas_call(
        start_body, name="gather_start_" + tag,
        out_shape=[pltpu.SemaphoreType.DMA((7 * n,)), pltpu.SemaphoreType.DMA((7 * n,))] + thru
        + [jax.ShapeDtypeStruct((8, LANE), F32)],
        in_specs=[hbm] * (2 * n), out_specs=[sem, sem] + [hbm] * (2 * n) + [pl.BlockSpec(memory_space=pltpu.VMEM)],
        input_output_aliases={i: 2 + i for i in range(2 * n)},
        compiler_params=pltpu.CompilerParams(has_side_effects=pltpu.SideEffectType.DATAFLOW_SIDE_EFFECTING),
    )(*[pltpu.with_memory_space_constraint(a, pltpu.HBM) for a in list(arrs) + lands])
    return (tag, n, outs[0], outs[1], outs[2:2 + n], outs[2 + n:2 + 2 * n]), outs[-1][0, 0]


def _gather_end(handle, which, after):
    tag, n, send_sems, recv_sems, srcs, lands = handle
    m = len(which)

    def body(*refs):
        ins, land = refs[:m], refs[m:2 * m]
        send_sems_, recv_sems_ = refs[2 * m], refs[2 * m + 1]
        x, y, c = _mesh_pos()
        me_i = _dev_index(x, y, c)
        for j, a in enumerate(which):
            for k, pos, _ in _peers(x, y, c):
                cp = pltpu.make_async_remote_copy(
                    src_ref=ins[j], dst_ref=land[j].at[me_i], send_sem=send_sems_.at[(k - 1) * n + a],
                    recv_sem=recv_sems_.at[(k - 1) * n + a], device_id=pos, device_id_type=MESH_T)
                cp.wait_send()
                cp.wait_recv()

    hbm = pl.BlockSpec(memory_space=pltpu.HBM)
    sem = pl.BlockSpec(memory_space=pltpu.SEMAPHORE)
    ops = [srcs[a] for a in which] + [lands[a] for a in which]
    outs = pl.pallas_call(
        body, name="gather_end_%s_%s" % (tag, "_".join(str(a) for a in which)),
        out_shape=[pltpu.HBM(a.shape, a.dtype) for a in ops],
        in_specs=[hbm] * (2 * m) + [sem, sem, pl.BlockSpec(memory_space=pl.ANY)], out_specs=[hbm] * (2 * m),
        input_output_aliases={i: i for i in range(2 * m)},
        compiler_params=pltpu.CompilerParams(has_side_effects=pltpu.SideEffectType.DATAFLOW_SIDE_EFFECTING),
    )(*ops, send_sems, recv_sems, after)
    return list(outs[m:])


def _call(body, operands, *, name, grid, in_specs, out_specs, out_shape, scratch_shapes=(), comm=None):
    if comm is None:
        outs = pl.pallas_call(body, name=name, grid=grid, in_specs=in_specs, out_specs=out_specs, out_shape=out_shape,
                              scratch_shapes=list(scratch_shapes), compiler_params=_params())(*operands)
        return list(outs), []
    ni, no, ns = len(in_specs), len(out_specs), len(scratch_shapes)
    ci, co = len(comm.arrs), len(comm.out_shape)
    total = 1
    for gdim in grid:
        total *= gdim

    def hosted(*refs):
        o0 = ni + ci
        s0 = o0 + no + co
        ins, cins = refs[:ni], refs[ni:o0]
        outs, couts = refs[o0:o0 + no], refs[o0 + no:s0]
        scr, csems = refs[s0:s0 + ns], refs[s0 + ns:]
        step = 0
        for axis, gdim in enumerate(grid):
            step = step * gdim + pl.program_id(axis)

        @pl.when(step == 0)
        def _():
            comm.start(cins, couts, csems)

        if comm.forward is not None:
            @pl.when(step == (3 * total) // 4)
            def _():
                comm.forward(cins, couts, csems)

        body(*ins, *outs, *scr)

        @pl.when(step == total - 1)
        def _():
            comm.finish(cins, couts, csems)

    any_spec = pl.BlockSpec(memory_space=pl.ANY)
    res = pl.pallas_call(
        hosted, name=name, grid=grid, in_specs=list(in_specs) + [any_spec] * ci, out_specs=list(out_specs) + [any_spec] * co,
        out_shape=list(out_shape) + comm.out_shape, scratch_shapes=list(scratch_shapes) + comm.sems,
        compiler_params=_params())(*operands, *comm.arrs)
    return list(res[:no]), list(res[no:])


def _rows(a):
    return a.reshape(-1, a.shape[-1])


def _slots(a):
    return a.reshape(N_DEV, -1, a.shape[-1])


def _local_step(x, mem, tgt, w, wire):
    tables = _rope_tables(x.shape[0])
    bd = _head_sum_matrix()
    two = lambda g: jnp.tile(g, (1, 2))
    gq2, gk2 = two(w["q_norm_g"]), two(w["k_norm_g"])

    w_in_t, hg_lb, conv_w = _comm_alone(_Gather([wire["w_in"], w["hg_lb"].reshape(4, -1), w["conv_w"][0]]), "gather_w_in")
    w_in_t = _rows(w_in_t)
    hg_lb = jnp.transpose(hg_lb.reshape(N_DEV, 2, 2, -1), (1, 2, 0, 3)).reshape(2, 2, HG_DIM)
    conv_w = jnp.transpose(conv_w, (1, 0, 2)).reshape(3, 2 * D_FF)
    lb_a0, lb_a1 = hg_lb[:, 0, :], hg_lb[:, 1, :]

    order = ("w_out", "w_xq", "w_xkv", "w_xo", "w_up", "w_down")
    w_in_t, *later = lax.optimization_barrier((w_in_t, *[wire[n] for n in order]))
    fetch, zero = _gather_begin(later, "w")
    take = lambda n, after: _rows(_gather_end(fetch, [order.index(n)], after)[0])

    h1, p = _norm_mm(x, w["pre_mix_g"] + zero, w_in_t, True, 1664, "in_proj")
    qr, kr = _qk_prep(p, gq2, gk2, tables, bd)
    (att, att32, lse), _ = _attn_fwd(qr, kr, p)
    lb = _lower_bounds(lb_a0, lb_a1)
    (of, s_f), _ = _hgrn_fwd(p, lb[0:1], False)
    (ob, s_b), _ = _hgrn_fwd(p, lb[1:2], True)
    w_out = take("w_out", ob)
    (cat, mixed, x1), _ = _mix_out(att, of, ob, p, w["hg_out_norm_g"], w_out, w["post_mix_g"], x)

    w_xq = take("w_xq", x1)
    h2, q2 = _norm_mm(x1, w["pre_x_g"], w_xq, False, 1024, "xq_proj")
    w_xkv_t = take("w_xkv", q2)
    mn, kv = _norm_mm(mem, w["mem_norm_g"], w_xkv_t, True, 1024, "xkv_proj")
    o2 = _xattn_fwd(q2, kv)
    w_xo = take("w_xo", o2)
    y2, x2 = _mm_postnorm_res(o2, w_xo, w["post_x_g"], x1, "xo_proj")

    w_up_t = take("w_up", x2)
    h3, u = _norm_mm(x2, w["pre_ffn_g"], w_up_t, True, 1408, "up_proj")
    a = _conv_gate_fwd(u, conv_w, w["conv_b"])
    w_down = take("w_down", a)
    y3, dx3, loss = _mm_postnorm_res_loss(a, w_down, w["post_ffn_g"], x2, tgt, "down_proj")

    g, pending = {}, {}
    dy3, da, g["post_ffn_g"] = _postnorm_bwd_mm(dx3, y3, w["post_ffn_g"], w_down, "down_bwd")
    pending["down"], zero = _exchange_begin([_slots(_dw(a, dy3, "dw_down"))], "down")
    (du_g, dcw_g, dcb_g), (du_v, dcw_v, dcb_v), _ = _conv_gate_bwd(u, conv_w, w["conv_b"] + zero, da)
    g["conv_w"] = jnp.concatenate([dcw_g, dcw_v], axis=1)
    g["conv_b"] = jnp.concatenate([dcb_g, dcb_v], axis=1)
    pending["up"], zero = _exchange_begin([_slots(_dw([du_g, du_v], h3, "dw_up"))], "up")
    (dx2, g["pre_ffn_g"]), _ = _mm_prenorm_bwd([du_g, du_v], w_up_t, True, x2, w["pre_ffn_g"] + zero, dx3, 1408, "up_bwd")

    dy2, do2, g["post_x_g"] = _postnorm_bwd_mm(dx2, y2, w["post_x_g"], w_xo, "xo_bwd")
    dw_xo = _dw(o2, dy2, "dw_xo")
    dq2, dkv = _xattn_bwd(q2, kv, do2)
    dw_xq = _dw(h2, dq2, "dw_xq")
    dkvb = _bf(dkv)
    dw_xkv = _dw(dkvb, mn, "dw_xkv")
    pending["x"], zero = _exchange_begin([_slots(dw_xo), _slots(dw_xq), _slots(dw_xkv)], "x")
    (_, g["mem_norm_g"]), _ = _mm_prenorm_bwd(dkvb, w_xkv_t, True, mem, w["mem_norm_g"], jnp.zeros_like(mem), 2048, "xkv_bwd")
    (dx1, g["pre_x_g"]), _ = _mm_prenorm_bwd(dq2, w_xq, False, x1, w["pre_x_g"] + zero, dx2, 1024, "xq_bwd")

    dmixed, dcat, g["post_mix_g"] = _postnorm_bwd_mm(dx1, mixed, w["post_mix_g"], w_out, "out_bwd")
    pending["out"], zero = _exchange_begin([_slots(_dw(cat, dmixed, "dw_out"))], "out")
    do, dhg, g["hg_out_norm_g"] = _rec_bwd(dcat, of, ob, p, w["hg_out_norm_g"] + zero)
    (dhq_f, dz_f, dhi_f, dlb_f), _ = _hgrn_bwd(p, lb[0:1], do, s_f, False)
    (dhq_b, dz_b, dhi_b, dlb_b), _ = _hgrn_bwd(p, lb[1:2], do, s_b, True)
    d_a0, d_a1 = _lower_bounds_bwd(lb_a0, lb_a1, jnp.concatenate([dlb_f, dlb_b], axis=0))
    g["hg_lb"] = jnp.stack([d_a0, d_a1], axis=1)
    (dqr, dkr, dv), _ = _attn_bwd(qr, kr, p, dcat, att32, lse)
    dp_qk, dgq, dgk = _qk_prep_bwd(p, dqr, dkr, gq2, gk2, tables, bd)
    g["q_norm_g"], g["k_norm_g"] = dgq, dgk
    dp = _assemble_dp(dp_qk, dv, dhq_f, dhq_b, dz_f, dz_b, dhi_f, dhi_b, dhg)
    pending["in"], zero = _exchange_begin([_slots(_dw(dp, h1, "dw_in"))], "in")
    (dx, g["pre_mix_g"]), _ = _mm_prenorm_bwd(dp, w_in_t, True, x, w["pre_mix_g"] + zero, dx1, 1664, "in_bwd")
    return loss, dx, g, pending


_COL_SHARDED = ("w_in", "w_xkv", "w_up")
_ROW_SHARDED = ("w_out", "w_xq", "w_xo", "w_down")
_REPLICATED = ("pre_mix_g", "q_norm_g", "k_norm_g", "hg_out_norm_g", "post_mix_g", "pre_x_g", "mem_norm_g", "post_x_g",
               "pre_ffn_g", "conv_b", "post_ffn_g")
_WEIGHTS = ("pre_mix_g", "w_in", "q_norm_g", "k_norm_g", "hg_lb", "hg_out_norm_g", "w_out", "post_mix_g", "pre_x_g",
            "mem_norm_g", "w_xq", "w_xkv", "w_xo", "post_x_g", "pre_ffn_g", "w_up", "conv_w", "conv_b", "w_down",
            "post_ffn_g")
_ADAM_TRANSPOSED = ("w_in", "w_up")
PACK_W = 1024


def _small_plan(shapes):
    plan, r = [], 0
    for vi, (rows, cols) in enumerate(shapes):
        for i in range(rows):
            for c0 in range(0, cols, PACK_W):
                plan.append((vi, i, c0, min(PACK_W, cols - c0), r))
                r += 1
    return plan, -(-r // 8) * 8


def _pack_small(vals):
    plan, nrows = _small_plan([val.shape for val in vals])

    def body(*refs):
        ins, out = refs[:-1], refs[-1]
        out[...] = jnp.zeros_like(out)
        for vi, i, c0, width, r in plan:
            out[r:r + 1, 0:width] = ins[vi][i:i + 1, c0:c0 + width]

    return pl.pallas_call(body, name="pack_small", out_shape=jax.ShapeDtypeStruct((nrows, PACK_W), F32))(*vals)


def _sum_unpack_small(packs, shapes):
    plan, _ = _small_plan(shapes)

    def body(*refs):
        p_ref, outs = refs[0], refs[1:]
        acc = p_ref[0]
        for i in range(1, N_DEV):
            acc = acc + p_ref[i]
        for vi, i, c0, width, r in plan:
            outs[vi][i:i + 1, c0:c0 + width] = acc[r:r + 1, 0:width]

    return pl.pallas_call(body, name="sum_unpack_small", out_shape=[jax.ShapeDtypeStruct(s, F32) for s in shapes])(packs)


def _adamw_many(ws, gs, ms, vs):
    n = len(ws)

    def body(*refs):
        w_refs, g_refs, m_refs, v_refs = (refs[k * n:(k + 1) * n] for k in range(4))
        d_refs, nm_refs, nv_refs = (refs[(4 + k) * n:(5 + k) * n] for k in range(3))
        for k in range(n):
            g_ = g_refs[k][...]
            m_ = ADAM_B1 * m_refs[k][...] + (1.0 - ADAM_B1) * g_
            v_ = ADAM_B2 * v_refs[k][...] + (1.0 - ADAM_B2) * (g_ * g_)
            m_hat = m_ / (1.0 - ADAM_B1 ** ADAM_STEP)
            v_hat = v_ / (1.0 - ADAM_B2 ** ADAM_STEP)
            d_refs[k][...] = -ADAM_LR * (m_hat / (jnp.sqrt(v_hat) + ADAM_EPS) + ADAM_WD * w_refs[k][...])
            nm_refs[k][...] = m_
            nv_refs[k][...] = v_

    shapes = [jax.ShapeDtypeStruct(a.shape, F32) for a in ws]
    outs = pl.pallas_call(body, name="adamw_small", out_shape=shapes * 3)(*ws, *gs, *ms, *vs)
    return outs[:n], outs[n:2 * n], outs[2 * n:]


def kernel(x, mem, pre_mix_g, w_in, q_norm_g, k_norm_g, hg_lb, hg_out_norm_g, w_out, post_mix_g, pre_x_g, mem_norm_g, w_xq, w_xkv, w_xo, post_x_g, pre_ffn_g, w_up, conv_w, conv_b, w_down, post_ffn_g, loss_target, m_pre_mix_g, m_w_in, m_q_norm_g, m_k_norm_g, m_hg_lb, m_hg_out_norm_g, m_w_out, m_post_mix_g, m_pre_x_g, m_mem_norm_g, m_w_xq, m_w_xkv, m_w_xo, m_post_x_g, m_pre_ffn_g, m_w_up, m_conv_w, m_conv_b, m_w_down, m_post_ffn_g, v_pre_mix_g, v_w_in, v_q_norm_g, v_k_norm_g, v_hg_lb, v_hg_out_norm_g, v_w_out, v_post_mix_g, v_pre_x_g, v_mem_norm_g, v_w_xq, v_w_xkv, v_w_xo, v_post_x_g, v_pre_ffn_g, v_w_up, v_conv_w, v_conv_b, v_w_down, v_post_ffn_g):
    args = dict(locals())
    w = {n: args[n] for n in _WEIGHTS}
    m = {n: args["m_" + n] for n in _WEIGHTS}
    v = {n: args["v_" + n] for n in _WEIGHTS}
    me = _dev_index(*_mesh_pos())

    wire = {n: _bf(w[n][0].T) for n in _COL_SHARDED}
    wire.update({n: _bf(w[n][0]) for n in _ROW_SHARDED})

    loss, grad_x, g, pending = _local_step(x[0], mem[0], loss_target[0], w, wire)

    grads, delta, new_m, new_v = {}, {}, {}, {}

    me_arr = jnp.reshape(me, (1,)).astype(jnp.int32)

    def update(n, parts):
        gsum = _sum_parts(*parts, me_arr, "sum_" + n)
        if n in _ADAM_TRANSPOSED:
            grads[n] = gsum.T[None]
            d_, m_, v_ = _adamw(w[n][0].T, gsum, m[n][0].T, v[n][0].T, "adamw_" + n)
            delta[n], new_m[n], new_v[n] = d_.T[None], m_.T[None], v_.T[None]
        else:
            gsum = gsum.T if n in _COL_SHARDED else gsum
            grads[n] = gsum[None]
            d_, m_, v_ = _adamw(w[n][0], gsum, m[n][0], v[n][0], "adamw_" + n)
            delta[n], new_m[n], new_v[n] = d_[None], m_[None], v_[None]

    after = grad_x
    for tag, names in (("down", ["w_down"]), ("up", ["w_up"]), ("x", ["w_xo", "w_xq", "w_xkv"]), ("out", ["w_out"]),
                       ("in", ["w_in"])):
        for n, parts in zip(names, _exchange_end(pending[tag], after)):
            update(n, parts)
            after = new_v[n]

    small = list(_REPLICATED) + ["hg_lb", "conv_w"]
    vals = [g[n] for n in _REPLICATED] + [g["hg_lb"].reshape(4, HG_DIM), g["conv_w"], jnp.pad(loss, ((0, 0), (0, LANE - 1)))]
    shapes = [val.shape for val in vals]
    (packs,) = _comm_alone(_Gather([_pack_small(vals)]), "gather_small_grads")
    summed = _sum_unpack_small(packs, shapes)
    loss = summed[-1][0, 0]
    for n, s in zip(small, summed[:-1]):
        grads[n] = s
    fold = lambda v2: v2[:, :ATT_HEAD_DIM] + v2[:, ATT_HEAD_DIM:]
    grads["q_norm_g"], grads["k_norm_g"] = fold(grads["q_norm_g"]), fold(grads["k_norm_g"])
    grads["hg_lb"] = lax.dynamic_slice_in_dim(grads["hg_lb"].reshape(2, 2, HG_DIM), me * (HG_DIM // N_DEV),
                                              HG_DIM // N_DEV, axis=2)
    grads["conv_w"] = lax.dynamic_slice_in_dim(grads["conv_w"], me * (2 * D_FF // N_DEV), 2 * D_FF // N_DEV, axis=1)[None]

    flat2 = lambda a: a.reshape(-1, a.shape[-1])
    outs = _adamw_many(*[[flat2(d[n]) for n in small] for d in (w, grads, m, v)])
    for dst, vals in zip((delta, new_m, new_v), outs):
        for n, val in zip(small, vals):
            dst[n] = val.reshape(w[n].shape)

    return (loss, grad_x[None], *[grads[n] for n in _WEIGHTS], *[delta[n] for n in _WEIGHTS],
            *[new_m[n] for n in _WEIGHTS], *[new_v[n] for n in _WEIGHTS])
```

```python
import jax
import jax.numpy as jnp
from jax import lax
from jax.experimental import pallas as pl
from jax.experimental.pallas import tpu as pltpu

F32 = jnp.float32
BF16 = jnp.bfloat16

D_MODEL = 1024
GRID_W = 64
EPS = 1e-6
ATT_HEADS = 8
ATT_HEAD_DIM = 64
ATT_Q_DIM = 512
ATT_KV_DIM = 128
ROPE_THETA = 10000.0
HG_HEADS = 4
HG_DIM = 512
HG_CHUNK = 32
HG_CHUNK_LOG2 = 5
HG_BLOCK = 128
N_IN = 3328
X_HEADS = 4
X_HEAD_DIM = 256
D_FF = 2816
N_DEV = 8
LANE = 128
ADAM_LR = 0.001
ADAM_B1 = 0.9
ADAM_B2 = 0.999
ADAM_EPS = 1e-08
ADAM_WD = 0.01
ADAM_STEP = 10
VMEM_LIMIT = 56 * 1024 * 1024

MESH_T = pl.DeviceIdType.MESH


def _params(**kw):
    return pltpu.CompilerParams(vmem_limit_bytes=VMEM_LIMIT, **kw)


def _dot(a, b, ca, cb):
    return lax.dot_general(a, b, (((ca,), (cb,)), ((), ())), preferred_element_type=F32)


def _bf(x):
    return x.astype(BF16)


def _sigmoid(x):
    return 1.0 / (1.0 + jnp.exp(-x))


def _rms_fwd(x, g):
    r = lax.rsqrt(jnp.mean(x * x, axis=-1, keepdims=True) + EPS)
    return x * r * g


def _rms_bwd(dy, x, g):
    r = lax.rsqrt(jnp.mean(x * x, axis=-1, keepdims=True) + EPS)
    xh = x * r
    dg = jnp.sum(dy * xh, axis=0, keepdims=True)
    t = dy * g
    dx = r * (t - xh * jnp.mean(t * xh, axis=-1, keepdims=True))
    return dx, dg


def _full(shape):
    nd = len(shape)
    return pl.BlockSpec(shape, lambda *a: (0,) * nd)


def _norm_mm(x, g, w, trans, tn, name, comm=None, tm=512):
    t, d = x.shape
    n = w.shape[0] if trans else w.shape[1]
    tm = min(tm, t)

    def body(x_ref, g_ref, w_ref, h_ref, p_ref):
        h = _bf(_rms_fwd(x_ref[...], g_ref[...]))
        h_ref[...] = h
        p_ref[...] = _dot(h, w_ref[...], 1, 1 if trans else 0)

    w_spec = pl.BlockSpec((tn, d), lambda i, j: (j, 0)) if trans else pl.BlockSpec((d, tn), lambda i, j: (0, j))
    (h, p), got = _call(
        body, (x, g, w), name=name, grid=(t // tm, n // tn),
        in_specs=[pl.BlockSpec((tm, d), lambda i, j: (i, 0)), _full((1, d)), w_spec],
        out_specs=[pl.BlockSpec((tm, d), lambda i, j: (i, 0)), pl.BlockSpec((tm, tn), lambda i, j: (i, j))],
        out_shape=[jax.ShapeDtypeStruct((t, d), BF16), jax.ShapeDtypeStruct((t, n), F32)], comm=comm)
    return (h, p, got) if comm is not None else (h, p)


def _mm_postnorm_res(a, w, g, res, name, tm=256):
    t, k = a.shape
    d = w.shape[1]

    def body(a_ref, w_ref, g_ref, res_ref, y_ref, o_ref):
        y = _dot(a_ref[...], w_ref[...], 1, 0)
        y_ref[...] = y
        o_ref[...] = res_ref[...] + _rms_fwd(y, g_ref[...])

    row = lambda width: pl.BlockSpec((tm, width), lambda i: (i, 0))
    return pl.pallas_call(
        body, name=name, grid=(t // tm,),
        in_specs=[row(k), _full((k, d)), _full((1, d)), row(d)],
        out_specs=[row(d), row(d)],
        out_shape=[jax.ShapeDtypeStruct((t, d), F32)] * 2,
        compiler_params=_params(),
    )(a, w, g, res)


def _mm_postnorm_res_loss(a, w, g, res, tgt, name, tm=256):
    t, k = a.shape
    d = w.shape[1]

    def body(a_ref, w_ref, g_ref, res_ref, tgt_ref, y_ref, dout_ref, loss_ref):
        @pl.when(pl.program_id(0) == 0)
        def _():
            loss_ref[...] = jnp.zeros_like(loss_ref)

        y = _dot(a_ref[...], w_ref[...], 1, 0)
        y_ref[...] = y
        diff = res_ref[...] + _rms_fwd(y, g_ref[...]) - tgt_ref[...]
        dout_ref[...] = diff * (1.0 / d)
        part = jnp.sum(jnp.sum(diff * diff, axis=-1, keepdims=True), axis=0, keepdims=True)
        loss_ref[...] += (0.5 / d) * part

    row = lambda width: pl.BlockSpec((tm, width), lambda i: (i, 0))
    return pl.pallas_call(
        body, name=name, grid=(t // tm,),
        in_specs=[row(k), _full((k, d)), _full((1, d)), row(d), row(d)],
        out_specs=[row(d), row(d), _full((1, 1))],
        out_shape=[jax.ShapeDtypeStruct((t, d), F32)] * 2 + [jax.ShapeDtypeStruct((1, 1), F32)],
        compiler_params=_params(),
    )(a, w, g, res, tgt)


def _postnorm_bwd_mm(dout, y, g, w, name, tm=256):
    t, d = y.shape
    k = w.shape[0]

    def body(dout_ref, y_ref, g_ref, w_ref, dy_ref, da_ref, dg_ref):
        @pl.when(pl.program_id(0) == 0)
        def _():
            dg_ref[...] = jnp.zeros_like(dg_ref)

        dy, dg = _rms_bwd(dout_ref[...], y_ref[...], g_ref[...])
        dg_ref[...] += dg
        dyb = _bf(dy)
        dy_ref[...] = dyb
        da_ref[...] = _dot(dyb, w_ref[...], 1, 1)

    row = lambda width: pl.BlockSpec((tm, width), lambda i: (i, 0))
    return pl.pallas_call(
        body, name=name, grid=(t // tm,),
        in_specs=[row(d), row(d), _full((1, d)), _full((k, d))],
        out_specs=[row(d), row(k), _full((1, d))],
        out_shape=[jax.ShapeDtypeStruct((t, d), BF16), jax.ShapeDtypeStruct((t, k), F32), jax.ShapeDtypeStruct((1, d), F32)],
        compiler_params=_params(),
    )(dout, y, g, w)


def _mm_prenorm_bwd(dp, w, trans, x, g, dres, tk, name, comm=None, tm=512):
    dps = list(dp) if isinstance(dp, (list, tuple)) else [dp]
    nparts = len(dps)
    t, n_each = dps[0].shape
    d = x.shape[1]
    tm = min(tm, t)
    nk_each = n_each // tk
    nk = nk_each * nparts

    def body(*refs):
        dp_refs = refs[:nparts]
        w_ref, x_ref, g_ref, dres_ref, dx_ref, dg_ref, acc_ref = refs[nparts:]
        i, kk = pl.program_id(0), pl.program_id(1)

        @pl.when((i == 0) & (kk == 0))
        def _():
            dg_ref[...] = jnp.zeros_like(dg_ref)

        @pl.when(kk == 0)
        def _():
            acc_ref[...] = jnp.zeros_like(acc_ref)

        for part, dp_ref in enumerate(dp_refs):
            @pl.when((kk >= part * nk_each) & (kk < (part + 1) * nk_each))
            def _():
                acc_ref[...] += _dot(dp_ref[...], w_ref[...], 1, 0 if trans else 1)

        @pl.when(kk == nk - 1)
        def _():
            dx, dg = _rms_bwd(acc_ref[...], x_ref[...], g_ref[...])
            dg_ref[...] += dg
            dx_ref[...] = dres_ref[...] + dx

    dp_spec = lambda part: pl.BlockSpec((tm, tk), lambda i, kk: (i, jnp.clip(kk - part * nk_each, 0, nk_each - 1)))
    w_spec = pl.BlockSpec((tk, d), lambda i, kk: (kk, 0)) if trans else pl.BlockSpec((d, tk), lambda i, kk: (0, kk))
    row = pl.BlockSpec((tm, d), lambda i, kk: (i, 0))
    return _call(
        body, (*dps, w, x, g, dres), name=name, grid=(t // tm, nk),
        in_specs=[dp_spec(part) for part in range(nparts)] + [w_spec, row, _full((1, d)), row],
        out_specs=[row, _full((1, d))],
        out_shape=[jax.ShapeDtypeStruct((t, d), F32), jax.ShapeDtypeStruct((1, d), F32)],
        scratch_shapes=[pltpu.VMEM((tm, d), F32)], comm=comm)


def _dw(a, b, name, bcol=None, tka=256):
    parts = list(a) if isinstance(a, (list, tuple)) else [a]
    nparts = len(parts)
    t, ka_each = parts[0].shape
    bci, nb = (0, b.shape[1]) if bcol is None else bcol
    nt_each = ka_each // tka

    def body(*refs):
        a_refs, b_ref, o_ref = refs[:nparts], refs[nparts], refs[nparts + 1]
        i = pl.program_id(0)
        for part, a_ref in enumerate(a_refs):
            @pl.when((i >= part * nt_each) & (i < (part + 1) * nt_each))
            def _():
                o_ref[...] = _bf(_dot(a_ref[...], b_ref[...], 0, 0))

    a_spec = lambda part: pl.BlockSpec((t, tka), lambda i: (0, jnp.clip(i - part * nt_each, 0, nt_each - 1)))
    return pl.pallas_call(
        body, name=name, grid=(nt_each * nparts,),
        in_specs=[a_spec(part) for part in range(nparts)] + [pl.BlockSpec((t, nb), lambda i: (0, bci))],
        out_specs=pl.BlockSpec((tka, nb), lambda i: (i, 0)),
        out_shape=jax.ShapeDtypeStruct((nparts * ka_each, nb), BF16),
        compiler_params=_params(),
    )(*parts, b)


def _rope_tables(t):
    pos = jnp.arange(t)
    r = (pos // GRID_W).astype(F32)
    c = (pos % GRID_W).astype(F32)
    npair = ATT_HEAD_DIM // 4
    inv = jnp.power(ROPE_THETA, -jnp.arange(npair, dtype=F32) / npair)
    ang = jnp.concatenate([r[:, None] * inv, c[:, None] * inv], axis=-1)
    cos = jnp.repeat(jnp.cos(ang), 2, axis=-1)
    sin = jnp.repeat(jnp.sin(ang), 2, axis=-1)
    even = (jnp.arange(ATT_HEAD_DIM) % 2) == 0
    sa = jnp.where(even, -sin, 0.0)
    sb = jnp.where(even, 0.0, sin)
    two = lambda a: jnp.tile(a, (1, 2))
    return two(cos), two(sa), two(sb)


def _head_sum_matrix():
    a = jnp.arange(LANE) // ATT_HEAD_DIM
    return (a[:, None] == a[None, :]).astype(BF16)


def _head_mean(v, bd):
    hi = _bf(v)
    lo = _bf(v - hi.astype(F32))
    return (_dot(hi, bd, 1, 0) + _dot(lo, bd, 1, 0)) * (1.0 / ATT_HEAD_DIM)


def _qk_prep(p, gq, gk, tables, bd, tm=512):
    t = p.shape[0]
    tm = min(tm, t)
    cc, sa, sb = tables

    def body(p_ref, gq_ref, gk_ref, cc_ref, sa_ref, sb_ref, bd_ref, q_ref, k_ref):
        cc_, sa_, sb_, bd_ = cc_ref[...], sa_ref[...], sb_ref[...], bd_ref[...]
        low = lax.broadcasted_iota(jnp.int32, (tm, LANE), 1) < ATT_HEAD_DIM

        def normrope(xs, g):
            xn = xs * lax.rsqrt(_head_mean(xs * xs, bd_) + EPS) * g
            return xn * cc_ + pltpu.roll(xn, LANE - 1, 1) * sa_ + pltpu.roll(xn, 1, 1) * sb_

        for j in range(4):
            y = normrope(p_ref[:, j * LANE:(j + 1) * LANE], gq_ref[...]) * (ATT_HEAD_DIM ** -0.5)
            yr = pltpu.roll(y, ATT_HEAD_DIM, 1)
            if j // 2 == 0:
                h0, h1 = jnp.where(low, y, 0.0), jnp.where(low, yr, 0.0)
            else:
                h0, h1 = jnp.where(low, 0.0, yr), jnp.where(low, 0.0, y)
            q_ref[:, (2 * j) * LANE:(2 * j + 1) * LANE] = _bf(h0)
            q_ref[:, (2 * j + 1) * LANE:(2 * j + 2) * LANE] = _bf(h1)
        k_ref[...] = _bf(normrope(p_ref[:, ATT_Q_DIM:ATT_Q_DIM + LANE], gk_ref[...]))

    row = lambda width: pl.BlockSpec((tm, width), lambda i: (i, 0))
    return pl.pallas_call(
        body, name="qk_prep", grid=(t // tm,),
        in_specs=[row(ATT_Q_DIM + LANE), _full((1, LANE)), _full((1, LANE)), row(LANE), row(LANE), row(LANE),
                  _full((LANE, LANE))],
        out_specs=[row(ATT_HEADS * LANE), row(LANE)],
        out_shape=[jax.ShapeDtypeStruct((t, ATT_HEADS * LANE), BF16), jax.ShapeDtypeStruct((t, LANE), BF16)],
        compiler_params=_params(),
    )(p, gq, gk, cc, sa, sb, bd)


def _qk_prep_bwd(p, dq, dk, gq, gk, tables, bd, tm=512):
    t = p.shape[0]
    tm = min(tm, t)
    cc, sa, sb = tables

    def body(p_ref, dq_ref, dk_ref, gq_ref, gk_ref, cc_ref, sa_ref, sb_ref, bd_ref, dp_ref, dgq_ref, dgk_ref):
        @pl.when(pl.program_id(0) == 0)
        def _():
            dgq_ref[...] = jnp.zeros_like(dgq_ref)
            dgk_ref[...] = jnp.zeros_like(dgk_ref)

        cc_, sa_, sb_, bd_ = cc_ref[...], sa_ref[...], sb_ref[...], bd_ref[...]
        low = lax.broadcasted_iota(jnp.int32, (tm, LANE), 1) < ATT_HEAD_DIM

        def bwd(xs, g, dy):
            r = lax.rsqrt(_head_mean(xs * xs, bd_) + EPS)
            xh = xs * r
            dxn = dy * cc_ + pltpu.roll(dy * sa_, 1, 1) + pltpu.roll(dy * sb_, LANE - 1, 1)
            dg = jnp.sum(dxn * xh, axis=0, keepdims=True)
            tt = dxn * g
            return r * (tt - xh * _head_mean(tt * xh, bd_)), dg

        dgq = jnp.zeros((1, LANE), F32)
        for j in range(4):
            d0 = dq_ref[:, (2 * j) * LANE:(2 * j + 1) * LANE]
            d1 = dq_ref[:, (2 * j + 1) * LANE:(2 * j + 2) * LANE]
            if j // 2 == 0:
                dy = jnp.where(low, d0, pltpu.roll(d1, ATT_HEAD_DIM, 1))
            else:
                dy = jnp.where(low, pltpu.roll(d0, ATT_HEAD_DIM, 1), d1)
            dx, dg = bwd(p_ref[:, j * LANE:(j + 1) * LANE], gq_ref[...], dy * (ATT_HEAD_DIM ** -0.5))
            dp_ref[:, j * LANE:(j + 1) * LANE] = _bf(dx)
            dgq = dgq + dg
        dgq_ref[...] += dgq
        dx, dg = bwd(p_ref[:, ATT_Q_DIM:ATT_Q_DIM + LANE], gk_ref[...], dk_ref[...])
        dp_ref[:, ATT_Q_DIM:ATT_Q_DIM + LANE] = _bf(dx)
        dgk_ref[...] += dg

    row = lambda width: pl.BlockSpec((tm, width), lambda i: (i, 0))
    return pl.pallas_call(
        body, name="qk_prep_bwd", grid=(t // tm,),
        in_specs=[row(ATT_Q_DIM + LANE), row(ATT_HEADS * LANE), row(LANE), _full((1, LANE)), _full((1, LANE)),
                  row(LANE), row(LANE), row(LANE), _full((LANE, LANE))],
        out_specs=[row(ATT_Q_DIM + LANE), _full((1, LANE)), _full((1, LANE))],
        out_shape=[jax.ShapeDtypeStruct((t, ATT_Q_DIM + LANE), BF16), jax.ShapeDtypeStruct((1, LANE), F32),
                   jax.ShapeDtypeStruct((1, LANE), F32)],
        compiler_params=_params(),
    )(p, dq, dk, gq, gk, cc, sa, sb, bd)


def _attn_fwd(q, k, p, comm=None, tq=256):
    t = k.shape[0]
    tq = min(tq, t)
    v_blk = (ATT_Q_DIM + ATT_KV_DIM) // LANE

    def body(q_ref, k_ref, v_ref, o_ref, o32_ref, lse_ref):
        k_ = k_ref[...]
        v = v_ref[...]
        lowk = lax.broadcasted_iota(jnp.int32, (t, LANE), 1) < ATT_HEAD_DIM
        vm = (_bf(jnp.where(lowk, v, 0.0)), _bf(jnp.where(lowk, 0.0, v)))
        for j in range(4):
            kvh = j // 2
            acc = None
            for sub in range(2):
                h = 2 * j + sub
                s = _dot(q_ref[:, h * LANE:(h + 1) * LANE], k_, 1, 1)
                mx = jnp.max(s, axis=-1, keepdims=True)
                e = jnp.exp(s - mx)
                l = jnp.sum(e, axis=-1, keepdims=True)
                lse_ref[h] = mx + jnp.log(l)
                o = _dot(_bf(e), vm[kvh], 1, 0) * (1.0 / l)
                if sub != kvh:
                    o = pltpu.roll(o, ATT_HEAD_DIM, 1)
                acc = o if acc is None else acc + o
            o32_ref[:, j * LANE:(j + 1) * LANE] = acc
            o_ref[:, j * LANE:(j + 1) * LANE] = _bf(acc)

    row = pl.BlockSpec((tq, ATT_Q_DIM), lambda i: (i, 0))
    outs, got = _call(
        body, (q, k, p), name="attn_fwd", grid=(t // tq,),
        in_specs=[pl.BlockSpec((tq, ATT_HEADS * LANE), lambda i: (i, 0)), _full((t, LANE)),
                  pl.BlockSpec((t, LANE), lambda i: (0, v_blk))],
        out_specs=[row, row, pl.BlockSpec((ATT_HEADS, tq, 1), lambda i: (0, i, 0))],
        out_shape=[jax.ShapeDtypeStruct((t, ATT_Q_DIM), BF16), jax.ShapeDtypeStruct((t, ATT_Q_DIM), F32),
                   jax.ShapeDtypeStruct((ATT_HEADS, t, 1), F32)], comm=comm)
    return outs, got


def _attn_bwd(q, k, p, dcat, o32, lse, comm=None, tq=256):
    t = k.shape[0]
    tq = min(tq, t)
    v_blk = (ATT_Q_DIM + ATT_KV_DIM) // LANE

    def body(q_ref, k_ref, v_ref, do_ref, o_ref, lse_ref, dq_ref, dk_ref, dv_ref):
        @pl.when(pl.program_id(0) == 0)
        def _():
            dk_ref[...] = jnp.zeros_like(dk_ref)
            dv_ref[...] = jnp.zeros_like(dv_ref)

        k_ = k_ref[...]
        vb = _bf(v_ref[...])
        lowq = lax.broadcasted_iota(jnp.int32, (tq, LANE), 1) < ATT_HEAD_DIM
        dk_acc = jnp.zeros((t, LANE), F32)
        dv_acc = jnp.zeros((t, LANE), F32)
        for j in range(4):
            kvh = j // 2
            dop = do_ref[:, j * LANE:(j + 1) * LANE]
            prod = dop * o_ref[:, j * LANE:(j + 1) * LANE]
            d_low = jnp.sum(jnp.where(lowq, prod, 0.0), axis=-1, keepdims=True)
            d_sub = (d_low, jnp.sum(prod, axis=-1, keepdims=True) - d_low)
            for sub in range(2):
                h = 2 * j + sub
                src = dop if sub == kvh else pltpu.roll(dop, ATT_HEAD_DIM, 1)
                do_h = _bf(jnp.where(lowq, src, 0.0) if kvh == 0 else jnp.where(lowq, 0.0, src))
                qh = q_ref[:, h * LANE:(h + 1) * LANE]
                pr = jnp.exp(_bf(_dot(qh, k_, 1, 1) - lse_ref[h]))
                ds = pr * _bf(_dot(do_h, vb, 1, 1) - d_sub[sub])
                dq_ref[:, h * LANE:(h + 1) * LANE] = _dot(ds, k_, 1, 0)
                dk_acc = dk_acc + _dot(ds, qh, 0, 0)
                dv_acc = dv_acc + _dot(pr, do_h, 0, 0)
        dk_ref[...] += dk_acc
        dv_ref[...] += dv_acc

    row = pl.BlockSpec((tq, ATT_Q_DIM), lambda i: (i, 0))
    return _call(
        body, (q, k, p, dcat, o32, lse), name="attn_bwd", grid=(t // tq,),
        in_specs=[pl.BlockSpec((tq, ATT_HEADS * LANE), lambda i: (i, 0)), _full((t, LANE)),
                  pl.BlockSpec((t, LANE), lambda i: (0, v_blk)), row, row,
                  pl.BlockSpec((ATT_HEADS, tq, 1), lambda i: (0, i, 0))],
        out_specs=[pl.BlockSpec((tq, ATT_HEADS * LANE), lambda i: (i, 0)), _full((t, LANE)), _full((t, LANE))],
        out_shape=[jax.ShapeDtypeStruct((t, ATT_HEADS * LANE), F32), jax.ShapeDtypeStruct((t, LANE), F32),
                   jax.ShapeDtypeStruct((t, LANE), F32)], comm=comm)


def _lower_bounds(a0, a1):
    def body(a0_ref, a1_ref, lb_ref):
        m = jnp.maximum(a0_ref[...], a1_ref[...])
        e0, e1 = jnp.exp(a0_ref[...] - m), jnp.exp(a1_ref[...] - m)
        lb_ref[...] = e0 / (e0 + e1)

    return pl.pallas_call(body, name="lower_bounds", out_shape=jax.ShapeDtypeStruct(a0.shape, F32))(a0, a1)


def _lower_bounds_bwd(a0, a1, dlb):
    def body(a0_ref, a1_ref, dlb_ref, d0_ref, d1_ref):
        m = jnp.maximum(a0_ref[...], a1_ref[...])
        e0, e1 = jnp.exp(a0_ref[...] - m), jnp.exp(a1_ref[...] - m)
        lb = e0 / (e0 + e1)
        d0 = dlb_ref[...] * lb * (1.0 - lb)
        d0_ref[...] = d0
        d1_ref[...] = -d0

    return pl.pallas_call(body, name="lower_bounds_bwd", out_shape=[jax.ShapeDtypeStruct(a0.shape, F32)] * 2)(a0, a1, dlb)


def _chunk_scan(x, pos, down):
    n = x.shape[0]
    s = 1
    while s < HG_CHUNK:
        if down:
            x = x + jnp.where(pos >= s, pltpu.roll(x, s, 0), 0.0)
        else:
            x = x + jnp.where(pos < HG_CHUNK - s, pltpu.roll(x, n - s, 0), 0.0)
        s *= 2
    return x


def _chunk_concat(x, cid, nc):
    return jnp.concatenate([_bf(jnp.where(cid == c, x, 0.0)) for c in range(nc)], axis=1)


def _chunk_pick(x, cid, nc):
    out = jnp.where(cid == 0, x[:, :LANE], 0.0)
    for c in range(1, nc):
        out = out + jnp.where(cid == c, x[:, c * LANE:(c + 1) * LANE], 0.0)
    return out


def _hgrn_gates(hq, z, lb):
    sq = _sigmoid(hq)
    sig = _sigmoid(z)
    f = lb + (1.0 - lb) * sig
    return hq * sq, sq, sig, f, jnp.log(f)


def _hgrn_local(q, f, g, v, pos, cid, amask, rev, nc):
    k = 1.0 - f
    ba = _chunk_scan(g, pos, not rev)
    bb = _chunk_scan(g, pos, rev)
    eq = 0.5 * (ba - bb + g)
    ex_q, ex_k, ex_i, ex_d = jnp.exp(eq), jnp.exp(-eq), jnp.exp(ba), jnp.exp(bb - g)
    qb, kb, qi, kd = q * ex_q, k * ex_k, q * ex_i, k * ex_d
    dvec = jnp.exp(ba + bb - g)
    a = jnp.where(amask, _dot(_bf(qb), _bf(kb), 1, 1), 0.0)
    kd_m = _chunk_concat(kd, cid, nc)
    qi_m = _chunk_concat(qi, cid, nc)
    ut_all = _dot(_bf(v), kd_m, 0, 0)
    return dict(k=k, ex_q=ex_q, ex_k=ex_k, ex_i=ex_i, ex_d=ex_d, qb=qb, kb=kb, qi=qi, kd=kd, dvec=dvec, a=a,
                kd_m=kd_m, qi_m=qi_m, ut_all=ut_all)


def _hgrn_masks(n, rev):
    row = lax.broadcasted_iota(jnp.int32, (n, LANE), 0)
    ti = lax.broadcasted_iota(jnp.int32, (n, n), 0)
    si = lax.broadcasted_iota(jnp.int32, (n, n), 1)
    tri = (si >= ti) if rev else (si <= ti)
    same = jnp.right_shift(ti, HG_CHUNK_LOG2) == jnp.right_shift(si, HG_CHUNK_LOG2)
    return jnp.bitwise_and(row, HG_CHUNK - 1), jnp.right_shift(row, HG_CHUNK_LOG2), same & tri


def _hgrn_specs(t, nb, rev, bwd):
    n = min(HG_BLOCK, t)
    if rev != bwd:
        blk = lambda i: nb - 1 - i
    else:
        blk = lambda i: i
    col = lambda base: pl.BlockSpec((n, 2 * LANE), lambda hp, i: (blk(i), base + hp))
    return n, blk, col


def _hgrn_fwd(p, lb, rev, comm=None):
    t = p.shape[0]
    nb = t // min(HG_BLOCK, t)
    n, blk, col = _hgrn_specs(t, nb, rev, False)
    nc = n // HG_CHUNK
    z_base = 7 if rev else 5

    def body(hq_ref, z_ref, hi_ref, lb_ref, o_ref, ssave_ref, st_scr):
        @pl.when(pl.program_id(1) == 0)
        def _():
            st_scr[...] = jnp.zeros_like(st_scr)

        pos, cid, amask = _hgrn_masks(n, rev)
        order = list(range(nc))[::-1] if rev else list(range(nc))
        for hh in range(2):
            sl = slice(hh * LANE, (hh + 1) * LANE)
            q, _, _, f, g = _hgrn_gates(hq_ref[:, sl], z_ref[:, sl], lb_ref[0:1, sl])
            v = hi_ref[:, sl]
            c_ = _hgrn_local(q, f, g, v, pos, cid, amask, rev, nc)
            st = st_scr[hh]
            ssave_ref[0, sl, :] = st
            cols = [None] * nc
            for c in order:
                cols[c] = st
                st = st * c_["dvec"][c * HG_CHUNK:c * HG_CHUNK + 1, :] + c_["ut_all"][:, c * LANE:(c + 1) * LANE]
            st_scr[hh] = st
            st_all = _bf(jnp.concatenate(cols, axis=1))
            o_ref[:, sl] = _dot(_bf(c_["a"]), _bf(v), 1, 0) + _dot(c_["qi_m"], st_all, 1, 1)

    return _call(
        body, (p, p, p, lb), name="hgrn_fwd_rev" if rev else "hgrn_fwd", grid=(2, nb),
        in_specs=[col(3), col(z_base), col(9), pl.BlockSpec((1, 2 * LANE), lambda hp, i: (0, hp))],
        out_specs=[pl.BlockSpec((n, 2 * LANE), lambda hp, i: (blk(i), hp)),
                   pl.BlockSpec((1, 2 * LANE, LANE), lambda hp, i: (blk(i), hp, 0))],
        out_shape=[jax.ShapeDtypeStruct((t, HG_DIM), F32), jax.ShapeDtypeStruct((nb, HG_DIM, LANE), F32)],
        scratch_shapes=[pltpu.VMEM((2, LANE, LANE), F32)], comm=comm)


def _hgrn_bwd(p, lb, do, ssave, rev, comm=None):
    t = p.shape[0]
    nb = t // min(HG_BLOCK, t)
    n, blk, col = _hgrn_specs(t, nb, rev, True)
    nc = n // HG_CHUNK
    z_base = 7 if rev else 5

    def body(hq_ref, z_ref, hi_ref, lb_ref, do_ref, ssave_ref, dhq_ref, dz_ref, dhi_ref, dlb_ref, dst_scr):
        @pl.when(pl.program_id(1) == 0)
        def _():
            dst_scr[...] = jnp.zeros_like(dst_scr)
            dlb_ref[...] = jnp.zeros_like(dlb_ref)

        pos, cid, amask = _hgrn_masks(n, rev)
        order = list(range(nc))[::-1] if rev else list(range(nc))
        for hh in range(2):
            sl = slice(hh * LANE, (hh + 1) * LANE)
            hq, lbv = hq_ref[:, sl], lb_ref[0:1, sl]
            q, sq, sig, f, g = _hgrn_gates(hq, z_ref[:, sl], lbv)
            v = hi_ref[:, sl]
            c_ = _hgrn_local(q, f, g, v, pos, cid, amask, rev, nc)
            dvec, ut_all = c_["dvec"], c_["ut_all"]
            drow = lambda c: dvec[c * HG_CHUNK:c * HG_CHUNK + 1, :]
            st = ssave_ref[0, sl, :]
            cols = [None] * nc
            for c in order:
                cols[c] = st
                st = st * drow(c) + ut_all[:, c * LANE:(c + 1) * LANE]
            dob, vb = _bf(do_ref[:, sl]), _bf(v)
            gt_all = _dot(dob, c_["qi_m"], 0, 0)
            dcur = dst_scr[hh]
            dnext = [None] * nc
            ddrow = [None] * nc
            for c in order[::-1]:
                dnext[c] = dcur
                ddrow[c] = jnp.sum(cols[c] * dcur, axis=0, keepdims=True) * drow(c)
                dcur = dcur * drow(c) + gt_all[:, c * LANE:(c + 1) * LANE]
            dst_scr[hh] = dcur
            dsn_all = _bf(jnp.concatenate(dnext, axis=1))
            st_all = _bf(jnp.concatenate(cols, axis=1))
            da = _bf(jnp.where(amask, _dot(dob, vb, 1, 1), 0.0))
            dv = _dot(_bf(c_["a"]), dob, 0, 0) + _dot(c_["kd_m"], dsn_all, 1, 1)
            dqb = _dot(da, _bf(c_["kb"]), 1, 0)
            dkb = _dot(da, _bf(c_["qb"]), 0, 0)
            dqi = _chunk_pick(_dot(dob, st_all, 1, 0), cid, nc)
            dkd = _chunk_pick(_dot(vb, dsn_all, 1, 0), cid, nc)
            dq = dqb * c_["ex_q"] + dqi * c_["ex_i"]
            dk = dkb * c_["ex_k"] + dkd * c_["ex_d"]
            e = dqb * c_["qb"] - dkb * c_["kb"] + dqi * c_["qi"]
            w = dkd * c_["kd"]
            dtot = jnp.where(cid == 0, ddrow[0], 0.0)
            for c in range(1, nc):
                dtot = dtot + jnp.where(cid == c, ddrow[c], 0.0)
            dg = _chunk_scan(e, pos, rev) + (_chunk_scan(w, pos, not rev) - w) + dtot
            df = dg / f - dk
            dz_ref[:, sl] = df * (1.0 - lbv) * sig * (1.0 - sig)
            dlb_ref[0:1, sl] += jnp.sum(df * (1.0 - sig), axis=0, keepdims=True)
            dhq_ref[:, sl] = dq * (sq * (1.0 + hq * (1.0 - sq)))
            dhi_ref[:, sl] = dv

    out = pl.BlockSpec((n, 2 * LANE), lambda hp, i: (blk(i), hp))
    return _call(
        body, (p, p, p, lb, do, ssave), name="hgrn_bwd_rev" if rev else "hgrn_bwd", grid=(2, nb),
        in_specs=[col(3), col(z_base), col(9), pl.BlockSpec((1, 2 * LANE), lambda hp, i: (0, hp)), out,
                  pl.BlockSpec((1, 2 * LANE, LANE), lambda hp, i: (blk(i), hp, 0))],
        out_specs=[out, out, out, pl.BlockSpec((1, 2 * LANE), lambda hp, i: (0, hp))],
        out_shape=[jax.ShapeDtypeStruct((t, HG_DIM), F32)] * 3 + [jax.ShapeDtypeStruct((1, HG_DIM), F32)],
        scratch_shapes=[pltpu.VMEM((2, LANE, LANE), F32)], comm=comm)


def _mix_out(att, of, ob, p, gout, w_out, g_post, x, comm=None, tm=256):
    t, d = x.shape
    tm = min(tm, t)

    def body(att_ref, of_ref, ob_ref, hg0_ref, hg1_ref, gout_ref, w_ref, g_ref, x_ref, cat_ref, y_ref, x1_ref):
        cat_ref[:, :ATT_Q_DIM] = att_ref[...]
        for h in range(HG_HEADS):
            sl = slice(h * LANE, (h + 1) * LANE)
            hg_ref = hg0_ref if h < 2 else hg1_ref
            hg = hg_ref[:, (h % 2) * LANE:(h % 2 + 1) * LANE]
            nrm = _rms_fwd(of_ref[:, sl] + ob_ref[:, sl], gout_ref[...])
            cat_ref[:, ATT_Q_DIM + h * LANE:ATT_Q_DIM + (h + 1) * LANE] = _bf(nrm * hg * _sigmoid(hg))
        y = _dot(cat_ref[...], w_ref[...], 1, 0)
        y_ref[...] = y
        x1_ref[...] = x_ref[...] + _rms_fwd(y, g_ref[...])

    row = lambda width: pl.BlockSpec((tm, width), lambda i: (i, 0))
    return _call(
        body, (att, of, ob, p, p, gout, w_out, g_post, x), name="mix_out", grid=(t // tm,),
        in_specs=[row(ATT_Q_DIM), row(HG_DIM), row(HG_DIM), pl.BlockSpec((tm, 2 * LANE), lambda i: (i, 11)),
                  pl.BlockSpec((tm, 2 * LANE), lambda i: (i, 12)), _full((1, LANE)), _full((d, d)), _full((1, d)), row(d)],
        out_specs=[row(d), row(d), row(d)],
        out_shape=[jax.ShapeDtypeStruct((t, d), BF16), jax.ShapeDtypeStruct((t, d), F32), jax.ShapeDtypeStruct((t, d), F32)],
        comm=comm)


def _rec_bwd(dcat, of, ob, p, gout, tm=256):
    t = of.shape[0]
    tm = min(tm, t)

    def body(dc_ref, of_ref, ob_ref, hg0_ref, hg1_ref, gout_ref, do_ref, dhg_ref, dgo_ref):
        @pl.when(pl.program_id(0) == 0)
        def _():
            dgo_ref[...] = jnp.zeros_like(dgo_ref)

        dgo = jnp.zeros((1, LANE), F32)
        for h in range(HG_HEADS):
            sl = slice(h * LANE, (h + 1) * LANE)
            hg_ref = hg0_ref if h < 2 else hg1_ref
            hg = hg_ref[:, (h % 2) * LANE:(h % 2 + 1) * LANE]
            o = of_ref[:, sl] + ob_ref[:, sl]
            sg = _sigmoid(hg)
            drec = dc_ref[:, sl]
            dhg_ref[:, sl] = drec * _rms_fwd(o, gout_ref[...]) * (sg * (1.0 + hg * (1.0 - sg)))
            do, dg = _rms_bwd(drec * hg * sg, o, gout_ref[...])
            do_ref[:, sl] = do
            dgo = dgo + dg
        dgo_ref[...] += dgo

    row = lambda width: pl.BlockSpec((tm, width), lambda i: (i, 0))
    return pl.pallas_call(
        body, name="rec_bwd", grid=(t // tm,),
        in_specs=[pl.BlockSpec((tm, HG_DIM), lambda i: (i, 1)), row(HG_DIM), row(HG_DIM),
                  pl.BlockSpec((tm, 2 * LANE), lambda i: (i, 11)), pl.BlockSpec((tm, 2 * LANE), lambda i: (i, 12)),
                  _full((1, LANE))],
        out_specs=[row(HG_DIM), row(HG_DIM), _full((1, LANE))],
        out_shape=[jax.ShapeDtypeStruct((t, HG_DIM), F32), jax.ShapeDtypeStruct((t, HG_DIM), F32),
                   jax.ShapeDtypeStruct((1, LANE), F32)],
        compiler_params=_params(),
    )(dcat, of, ob, p, p, gout)


def _assemble_dp(dp_qk, dv, dhq_f, dhq_b, dz_f, dz_b, dhi_f, dhi_b, dhg, tm=512):
    t = dv.shape[0]
    tm = min(tm, t)
    qk_w = ATT_Q_DIM + LANE

    def body(qk_ref, dv_ref, hqf_ref, hqb_ref, zf_ref, zb_ref, hif_ref, hib_ref, hg_ref, dp_ref):
        o = 0
        dp_ref[:, o:o + qk_w] = qk_ref[...]
        o += qk_w
        dp_ref[:, o:o + LANE] = _bf(dv_ref[...])
        o += LANE
        for val in (hqf_ref[...] + hqb_ref[...], zf_ref[...], zb_ref[...], hif_ref[...] + hib_ref[...], hg_ref[...]):
            dp_ref[:, o:o + HG_DIM] = _bf(val)
            o += HG_DIM

    row = lambda width: pl.BlockSpec((tm, width), lambda i: (i, 0))
    return pl.pallas_call(
        body, name="assemble_dp", grid=(t // tm,),
        in_specs=[row(qk_w), row(LANE)] + [row(HG_DIM)] * 7,
        out_specs=row(N_IN),
        out_shape=jax.ShapeDtypeStruct((t, N_IN), BF16),
        compiler_params=_params(),
    )(dp_qk, dv, dhq_f, dhq_b, dz_f, dz_b, dhi_f, dhi_b, dhg)


def _xattn_fwd(q, kv, tq=512):
    t, d = q.shape
    tq = min(tq, t)
    nm = kv.shape[0]

    def body(q_ref, kv_ref, o_ref):
        for h in range(X_HEADS):
            sl = slice(h * X_HEAD_DIM, (h + 1) * X_HEAD_DIM)
            s = _dot(_bf(q_ref[:, sl]), _bf(kv_ref[:, sl]), 1, 1) * (X_HEAD_DIM ** -0.5)
            e = jnp.exp(s - jnp.max(s, axis=-1, keepdims=True))
            pr = e * (1.0 / jnp.sum(e, axis=-1, keepdims=True))
            o_ref[:, sl] = _bf(_dot(_bf(pr), _bf(kv_ref[:, d + h * X_HEAD_DIM:d + (h + 1) * X_HEAD_DIM]), 1, 0))

    return pl.pallas_call(
        body, name="xattn_fwd", grid=(t // tq,),
        in_specs=[pl.BlockSpec((tq, d), lambda i: (i, 0)), _full((nm, 2 * d))],
        out_specs=pl.BlockSpec((tq, d), lambda i: (i, 0)),
        out_shape=jax.ShapeDtypeStruct((t, d), BF16),
        compiler_params=_params(),
    )(q, kv)


def _xattn_bwd(q, kv, do, tq=512):
    t, d = q.shape
    tq = min(tq, t)
    nm = kv.shape[0]

    def body(q_ref, kv_ref, do_ref, dq_ref, dkv_ref):
        @pl.when(pl.program_id(0) == 0)
        def _():
            dkv_ref[...] = jnp.zeros_like(dkv_ref)

        for h in range(X_HEADS):
            sl = slice(h * X_HEAD_DIM, (h + 1) * X_HEAD_DIM)
            slv = slice(d + h * X_HEAD_DIM, d + (h + 1) * X_HEAD_DIM)
            qb, kb, vb, dob = _bf(q_ref[:, sl]), _bf(kv_ref[:, sl]), _bf(kv_ref[:, slv]), _bf(do_ref[:, sl])
            s = _dot(qb, kb, 1, 1) * (X_HEAD_DIM ** -0.5)
            e = jnp.exp(s - jnp.max(s, axis=-1, keepdims=True))
            pr = e * (1.0 / jnp.sum(e, axis=-1, keepdims=True))
            dpr = _dot(dob, vb, 1, 1)
            ds = _bf(pr * (dpr - jnp.sum(pr * dpr, axis=-1, keepdims=True)) * (X_HEAD_DIM ** -0.5))
            dq_ref[:, sl] = _bf(_dot(ds, kb, 1, 0))
            dkv_ref[:, sl] += _dot(ds, qb, 0, 0)
            dkv_ref[:, slv] += _dot(_bf(pr), dob, 0, 0)

    return pl.pallas_call(
        body, name="xattn_bwd", grid=(t // tq,),
        in_specs=[pl.BlockSpec((tq, d), lambda i: (i, 0)), _full((nm, 2 * d)), pl.BlockSpec((tq, d), lambda i: (i, 0))],
        out_specs=[pl.BlockSpec((tq, d), lambda i: (i, 0)), _full((nm, 2 * d))],
        out_shape=[jax.ShapeDtypeStruct((t, d), BF16), jax.ShapeDtypeStruct((nm, 2 * d), F32)],
        compiler_params=_params(),
    )(q, kv, do)


CONV_TN = 256


def _shift_rows(u, row, t, delta):
    if delta < 0:
        return jnp.where(row == 0, 0.0, pltpu.roll(u, 1, 0))
    return jnp.where(row == t - 1, 0.0, pltpu.roll(u, t - 1, 0))


def _conv_gate_fwd(u, conv_w, conv_b):
    t = u.shape[0]
    nt = D_FF // CONV_TN

    def body(ug_ref, uv_ref, wg_ref, wv_ref, bg_ref, bv_ref, a_ref):
        row = lax.broadcasted_iota(jnp.int32, (t, CONV_TN), 0)

        def conv(u_ref, w_ref, b_ref):
            uu = u_ref[...]
            return (b_ref[...] + _shift_rows(uu, row, t, -1) * w_ref[0:1, :] + uu * w_ref[1:2, :]
                    + _shift_rows(uu, row, t, 1) * w_ref[2:3, :])

        gate = conv(ug_ref, wg_ref, bg_ref)
        a_ref[...] = _bf(gate * _sigmoid(gate) * conv(uv_ref, wv_ref, bv_ref))

    col = lambda rows, off: pl.BlockSpec((rows, CONV_TN), lambda j: (0, j + off))
    return pl.pallas_call(
        body, name="conv_gate_fwd", grid=(nt,),
        in_specs=[col(t, 0), col(t, nt), col(3, 0), col(3, nt), col(1, 0), col(1, nt)],
        out_specs=col(t, 0),
        out_shape=jax.ShapeDtypeStruct((t, D_FF), BF16),
        compiler_params=_params(),
    )(u, u, conv_w, conv_w, conv_b, conv_b)


def _conv_gate_bwd(u, conv_w, conv_b, da, comm=None):
    t = u.shape[0]
    nt = D_FF // CONV_TN

    def body(ug_ref, uv_ref, wg_ref, wv_ref, bg_ref, bv_ref, da_ref, dug_ref, dwg_ref, dbg_ref, duv_ref, dwv_ref, dbv_ref):
        row = lax.broadcasted_iota(jnp.int32, (t, CONV_TN), 0)
        ug, uv = ug_ref[...], uv_ref[...]
        ug_m, ug_p = _shift_rows(ug, row, t, -1), _shift_rows(ug, row, t, 1)
        uv_m, uv_p = _shift_rows(uv, row, t, -1), _shift_rows(uv, row, t, 1)
        gate = bg_ref[...] + ug_m * wg_ref[0:1, :] + ug * wg_ref[1:2, :] + ug_p * wg_ref[2:3, :]
        val = bv_ref[...] + uv_m * wv_ref[0:1, :] + uv * wv_ref[1:2, :] + uv_p * wv_ref[2:3, :]
        sg = _sigmoid(gate)
        da_ = da_ref[...]

        def emit(dacc, um, uu, up, w_ref, du_ref, dw_ref, db_ref):
            du_ref[...] = _bf(_shift_rows(dacc, row, t, 1) * w_ref[0:1, :] + dacc * w_ref[1:2, :]
                              + _shift_rows(dacc, row, t, -1) * w_ref[2:3, :])
            dw_ref[0:1, :] = jnp.sum(dacc * um, axis=0, keepdims=True)
            dw_ref[1:2, :] = jnp.sum(dacc * uu, axis=0, keepdims=True)
            dw_ref[2:3, :] = jnp.sum(dacc * up, axis=0, keepdims=True)
            db_ref[...] = jnp.sum(dacc, axis=0, keepdims=True)

        emit(da_ * val * (sg * (1.0 + gate * (1.0 - sg))), ug_m, ug, ug_p, wg_ref, dug_ref, dwg_ref, dbg_ref)
        emit(da_ * gate * sg, uv_m, uv, uv_p, wv_ref, duv_ref, dwv_ref, dbv_ref)

    col = lambda rows, off: pl.BlockSpec((rows, CONV_TN), lambda j: (0, j + off))
    half_shapes = [jax.ShapeDtypeStruct((t, D_FF), BF16), jax.ShapeDtypeStruct((3, D_FF), F32),
                   jax.ShapeDtypeStruct((1, D_FF), F32)]
    outs, got = _call(
        body, (u, u, conv_w, conv_w, conv_b, conv_b, da), name="conv_gate_bwd", grid=(nt,),
        in_specs=[col(t, 0), col(t, nt), col(3, 0), col(3, nt), col(1, 0), col(1, nt), col(t, 0)],
        out_specs=[col(t, 0), col(3, 0), col(1, 0)] * 2, out_shape=half_shapes * 2, comm=comm)
    return outs[:3], outs[3:], got


def _row_tile(r, cap):
    best = r
    for cand in range(16, cap + 1, 16):
        if r % cand == 0:
            best = cand
    return best


def _sum_parts(own, got, me, name, tr=256):
    _, r, c = got.shape
    tr = _row_tile(r, tr)

    def body(me_ref, own_ref, got_ref, o_ref):
        mine = own_ref[...].astype(F32)
        acc = None
        for i in range(N_DEV):
            term = jnp.where(me_ref[0] == i, mine, got_ref[i].astype(F32))
            acc = term if acc is None else acc + term
        o_ref[...] = acc

    return pl.pallas_call(
        body, name=name,
        grid_spec=pltpu.PrefetchScalarGridSpec(
            num_scalar_prefetch=1, grid=(r // tr,),
            in_specs=[pl.BlockSpec((None, tr, c), lambda i, me_ref: (me_ref[0], i, 0)),
                      pl.BlockSpec((N_DEV, tr, c), lambda i, me_ref: (0, i, 0))],
            out_specs=pl.BlockSpec((tr, c), lambda i, me_ref: (i, 0))),
        out_shape=jax.ShapeDtypeStruct((r, c), F32),
        compiler_params=_params(),
    )(me, own, got)


def _adamw(w, g, m, v, name, tr=256):
    r, c = w.shape
    tr = _row_tile(r, tr)

    def body(w_ref, g_ref, m_ref, v_ref, d_ref, nm_ref, nv_ref):
        g_ = g_ref[...]
        m_ = ADAM_B1 * m_ref[...] + (1.0 - ADAM_B1) * g_
        v_ = ADAM_B2 * v_ref[...] + (1.0 - ADAM_B2) * (g_ * g_)
        m_hat = m_ / (1.0 - ADAM_B1 ** ADAM_STEP)
        v_hat = v_ / (1.0 - ADAM_B2 ** ADAM_STEP)
        d_ref[...] = -ADAM_LR * (m_hat / (jnp.sqrt(v_hat) + ADAM_EPS) + ADAM_WD * w_ref[...])
        nm_ref[...] = m_
        nv_ref[...] = v_

    blk = pl.BlockSpec((tr, c), lambda i: (i, 0))
    return pl.pallas_call(
        body, name=name, grid=(r // tr,),
        in_specs=[blk] * 4, out_specs=[blk] * 3,
        out_shape=[jax.ShapeDtypeStruct((r, c), F32)] * 3,
        compiler_params=_params(),
    )(w, g, m, v)


def _mesh_pos():
    return lax.axis_index("x"), lax.axis_index("y"), lax.axis_index("c")


def _dev_index(px, py, pc):
    return 4 * px + 2 * py + pc


class _Gather:
    def __init__(self, arrs):
        self.arrs = list(arrs)
        n = len(self.arrs)
        self.out_shape = [jax.ShapeDtypeStruct((N_DEV,) + a.shape, a.dtype) for a in self.arrs]
        self.sems = [pltpu.SemaphoreType.DMA((7, n)), pltpu.SemaphoreType.DMA((7, n)), pltpu.SemaphoreType.DMA((n,))]

    def _ctx(self, ins, outs, sems):
        send_sems, recv_sems, local_sems = sems
        x, y, c = _mesh_pos()
        chips = [(1 - x, y), (x, 1 - y), (1 - x, 1 - y)]

        def copy(k, a, block, to, src=None):
            dst = outs[a].at[_dev_index(*block)]
            return pltpu.make_async_remote_copy(
                src_ref=dst if src is None else src, dst_ref=dst, send_sem=send_sems.at[k, a], recv_sem=recv_sems.at[k, a],
                device_id=to, device_id_type=MESH_T)

        n = len(ins)
        me, sibling = (x, y, c), (x, y, 1 - c)
        mine = [pltpu.make_async_copy(ins[a], outs[a].at[_dev_index(*me)], local_sems.at[a]) for a in range(n)]
        first = [copy(0, a, me, sibling, src=ins[a]) for a in range(n)]
        first += [copy(1 + j, a, me, (*chip, c), src=ins[a]) for j, chip in enumerate(chips) for a in range(n)]
        passed = [copy(4 + j, a, (*chip, c), sibling) for j, chip in enumerate(chips) for a in range(n)]
        return n, c, me, sibling, chips, copy, mine, first, passed

    def start(self, ins, outs, sems):
        _, _, _, _, _, _, mine, first, _ = self._ctx(ins, outs, sems)
        for cp in mine + first:
            cp.start()

    def forward(self, ins, outs, sems):
        n, c, me, _, chips, copy, _, _, passed = self._ctx(ins, outs, sems)
        for j, chip in enumerate(chips):
            for a in range(n):
                copy(1 + j, a, (*chip, c), me).wait_recv()
                passed[j * n + a].start()

    def finish(self, ins, outs, sems):
        n, c, me, sibling, chips, copy, mine, first, passed = self._ctx(ins, outs, sems)
        for a in range(n):
            copy(0, a, sibling, me).wait_recv()
        for j, chip in enumerate(chips):
            for a in range(n):
                copy(4 + j, a, (*chip, 1 - c), me).wait_recv()
        for cp in first + passed:
            cp.wait_send()
        for cp in mine:
            cp.wait()


class _Exchange:
    def __init__(self, arrs):
        self.arrs = list(arrs)
        n = len(self.arrs)
        self.out_shape = [jax.ShapeDtypeStruct(a.shape, a.dtype) for a in self.arrs]
        self.sems = [pltpu.SemaphoreType.DMA((7, n)), pltpu.SemaphoreType.DMA((7, n)), pltpu.SemaphoreType.DMA((n,))]

    def _copies(self, ins, outs, sems):
        send_sems, recv_sems, local_sems = sems
        x, y, c = _mesh_pos()
        me_i = _dev_index(x, y, c)
        n = len(ins)
        mine = [pltpu.make_async_copy(ins[a].at[me_i], outs[a].at[me_i], local_sems.at[a]) for a in range(n)]
        copies = []
        for k in range(1, N_DEV):
            px = 1 - x if k & 4 else x
            py = 1 - y if k & 2 else y
            pc = 1 - c if k & 1 else c
            peer_i = _dev_index(px, py, pc)
            for a in range(n):
                copies.append(pltpu.make_async_remote_copy(
                    src_ref=ins[a].at[peer_i], dst_ref=outs[a].at[me_i], send_sem=send_sems.at[k - 1, a],
                    recv_sem=recv_sems.at[k - 1, a], device_id=(px, py, pc), device_id_type=MESH_T))
        return mine, copies

    def start(self, ins, outs, sems):
        mine, copies = self._copies(ins, outs, sems)
        for cp in mine + copies:
            cp.start()

    forward = None

    def finish(self, ins, outs, sems):
        mine, copies = self._copies(ins, outs, sems)
        for cp in copies:
            cp.wait_recv()
        for cp in copies:
            cp.wait_send()
        for cp in mine:
            cp.wait()


def _comm_alone(comm, name):
    n = len(comm.arrs)

    def body(*refs):
        ins, outs, sems = refs[:n], refs[n:2 * n], refs[2 * n:]
        comm.start(ins, outs, sems)
        if comm.forward is not None:
            comm.forward(ins, outs, sems)
        comm.finish(ins, outs, sems)

    any_spec = pl.BlockSpec(memory_space=pl.ANY)
    return pl.pallas_call(body, name=name, in_specs=[any_spec] * n, out_specs=[any_spec] * n, out_shape=comm.out_shape,
                          scratch_shapes=comm.sems)(*comm.arrs)


def _peers(x, y, c):
    out = []
    for k in range(1, N_DEV):
        pos = (1 - x if k & 4 else x, 1 - y if k & 2 else y, 1 - c if k & 1 else c)
        out.append((k, pos, _dev_index(*pos)))
    return out


def _exchange_begin(arrs, tag):
    n = len(arrs)
    lands = [lax.empty(a.shape, a.dtype) for a in arrs]

    def start_body(*refs):
        ins, land = refs[:n], refs[n:2 * n]
        send_sems, recv_sems, token = refs[2 * n], refs[2 * n + 1], refs[-1]
        x, y, c = _mesh_pos()
        me_i = _dev_index(x, y, c)
        for k, pos, peer_i in _peers(x, y, c):
            for a in range(n):
                pltpu.make_async_remote_copy(
                    src_ref=ins[a].at[peer_i], dst_ref=land[a].at[me_i], send_sem=send_sems.at[(k - 1) * n + a],
                    recv_sem=recv_sems.at[(k - 1) * n + a], device_id=pos, device_id_type=MESH_T).start()
        token[...] = jnp.zeros_like(token)

    hbm = pl.BlockSpec(memory_space=pltpu.HBM)
    sem = pl.BlockSpec(memory_space=pltpu.SEMAPHORE)
    thru = [pltpu.HBM(a.shape, a.dtype) for a in arrs]
    outs = pl.pallas_call(
        start_body, name="exchange_start_" + tag,
        out_shape=[pltpu.SemaphoreType.DMA((7 * n,)), pltpu.SemaphoreType.DMA((7 * n,))] + thru + thru
        + [jax.ShapeDtypeStruct((8, LANE), F32)],
        in_specs=[hbm] * (2 * n), out_specs=[sem, sem] + [hbm] * (2 * n) + [pl.BlockSpec(memory_space=pltpu.VMEM)],
        input_output_aliases={i: 2 + i for i in range(2 * n)},
        compiler_params=pltpu.CompilerParams(has_side_effects=pltpu.SideEffectType.DATAFLOW_SIDE_EFFECTING),
    )(*[pltpu.with_memory_space_constraint(a, pltpu.HBM) for a in list(arrs) + list(lands)])
    return (tag, n, outs[0], outs[1], outs[2:2 + n], outs[2 + n:2 + 2 * n]), outs[-1][0, 0]


def _exchange_end(handle, after):
    tag, n, send_sems, recv_sems, srcs, lands = handle

    def body(*refs):
        ins, land = refs[:n], refs[n:2 * n]
        send_sems_, recv_sems_ = refs[2 * n], refs[2 * n + 1]
        x, y, c = _mesh_pos()
        me_i = _dev_index(x, y, c)
        for k, pos, peer_i in _peers(x, y, c):
            for a in range(n):
                cp = pltpu.make_async_remote_copy(
                    src_ref=ins[a].at[peer_i], dst_ref=land[a].at[me_i], send_sem=send_sems_.at[(k - 1) * n + a],
                    recv_sem=recv_sems_.at[(k - 1) * n + a], device_id=pos, device_id_type=MESH_T)
                cp.wait_send()
                cp.wait_recv()

    hbm = pl.BlockSpec(memory_space=pltpu.HBM)
    sem = pl.BlockSpec(memory_space=pltpu.SEMAPHORE)
    outs = pl.pallas_call(
        body, name="exchange_end_" + tag, out_shape=[pltpu.HBM(a.shape, a.dtype) for a in list(srcs) + list(lands)],
        in_specs=[hbm] * (2 * n) + [sem, sem, pl.BlockSpec(memory_space=pl.ANY)], out_specs=[hbm] * (2 * n),
        input_output_aliases={i: i for i in range(2 * n)},
        compiler_params=pltpu.CompilerParams(has_side_effects=pltpu.SideEffectType.DATAFLOW_SIDE_EFFECTING),
    )(*srcs, *lands, send_sems, recv_sems, after)
    return list(zip(outs[:n], outs[n:]))


def _gather_begin(arrs, tag):
    n = len(arrs)
    me = _dev_index(*_mesh_pos())
    lands = [lax.dynamic_update_slice(lax.empty((N_DEV,) + a.shape, a.dtype), a[None], (me,) + (0,) * a.ndim) for a in arrs]

    def start_body(*refs):
        ins, land = refs[:n], refs[n:2 * n]
        send_sems, recv_sems, token = refs[2 * n], refs[2 * n + 1], refs[-1]
        x, y, c = _mesh_pos()
        me_i = _dev_index(x, y, c)
        for a in range(n):
            for k, pos, _ in _peers(x, y, c):
                pltpu.make_async_remote_copy(
                    src_ref=ins[a], dst_ref=land[a].at[me_i], send_sem=send_sems.at[(k - 1) * n + a],
                    recv_sem=recv_sems.at[(k - 1) * n + a], device_id=pos, device_id_type=MESH_T).start()
        token[...] = jnp.zeros_like(token)

    hbm = pl.BlockSpec(memory_space=pltpu.HBM)
    sem = pl.BlockSpec(memory_space=pltpu.SEMAPHORE)
    thru = [pltpu.HBM(a.shape, a.dtype) for a in list(arrs) + lands]
    outs = pl.pallas_call(
        start_body, name="gather_start_" + tag,
        out_shape=[pltpu.SemaphoreType.DMA((7 * n,)), pltpu.SemaphoreType.DMA((7 * n,))] + thru
        + [jax.ShapeDtypeStruct((8, LANE), F32)],
        in_specs=[hbm] * (2 * n), out_specs=[sem, sem] + [hbm] * (2 * n) + [pl.BlockSpec(memory_space=pltpu.VMEM)],
        input_output_aliases={i: 2 + i for i in range(2 * n)},
        compiler_params=pltpu.CompilerParams(has_side_effects=pltpu.SideEffectType.DATAFLOW_SIDE_EFFECTING),
    )(*[pltpu.with_memory_space_constraint(a, pltpu.HBM) for a in list(arrs) + lands])
    return (tag, n, outs[0], outs[1], outs[2:2 + n], outs[2 + n:2 + 2 * n]), outs[-1][0, 0]


def _gather_end(handle, which, after):
    tag, n, send_sems, recv_sems, srcs, lands = handle
    m = len(which)

    def body(*refs):
        ins, land = refs[:m], refs[m:2 * m]
        send_sems_, recv_sems_ = refs[2 * m], refs[2 * m + 1]
        x, y, c = _mesh_pos()
        me_i = _dev_index(x, y, c)
        for j, a in enumerate(which):
            for k, pos, _ in _peers(x, y, c):
                cp = pltpu.make_async_remote_copy(
                    src_ref=ins[j], dst_ref=land[j].at[me_i], send_sem=send_sems_.at[(k - 1) * n + a],
                    recv_sem=recv_sems_.at[(k - 1) * n + a], device_id=pos, device_id_type=MESH_T)
                cp.wait_send()
                cp.wait_recv()

    hbm = pl.BlockSpec(memory_space=pltpu.HBM)
    sem = pl.BlockSpec(memory_space=pltpu.SEMAPHORE)
    ops = [srcs[a] for a in which] + [lands[a] for a in which]
    outs = pl.pallas_call(
        body, name="gather_end_%s_%s" % (tag, "_".join(str(a) for a in which)),
        out_shape=[pltpu.HBM(a.shape, a.dtype) for a in ops],
        in_specs=[hbm] * (2 * m) + [sem, sem, pl.BlockSpec(memory_space=pl.ANY)], out_specs=[hbm] * (2 * m),
        input_output_aliases={i: i for i in range(2 * m)},
        compiler_params=pltpu.CompilerParams(has_side_effects=pltpu.SideEffectType.DATAFLOW_SIDE_EFFECTING),
    )(*ops, send_sems, recv_sems, after)
    return list(outs[m:])


def _call(body, operands, *, name, grid, in_specs, out_specs, out_shape, scratch_shapes=(), comm=None):
    if comm is None:
        outs = pl.pallas_call(body, name=name, grid=grid, in_specs=in_specs, out_specs=out_specs, out_shape=out_shape,
                              scratch_shapes=list(scratch_shapes), compiler_params=_params())(*operands)
        return list(outs), []
    ni, no, ns = len(in_specs), len(out_specs), len(scratch_shapes)
    ci, co = len(comm.arrs), len(comm.out_shape)
    total = 1
    for gdim in grid:
        total *= gdim

    def hosted(*refs):
        o0 = ni + ci
        s0 = o0 + no + co
        ins, cins = refs[:ni], refs[ni:o0]
        outs, couts = refs[o0:o0 + no], refs[o0 + no:s0]
        scr, csems = refs[s0:s0 + ns], refs[s0 + ns:]
        step = 0
        for axis, gdim in enumerate(grid):
            step = step * gdim + pl.program_id(axis)

        @pl.when(step == 0)
        def _():
            comm.start(cins, couts, csems)

        if comm.forward is not None:
            @pl.when(step == (3 * total) // 4)
            def _():
                comm.forward(cins, couts, csems)

        body(*ins, *outs, *scr)

        @pl.when(step == total - 1)
        def _():
            comm.finish(cins, couts, csems)

    any_spec = pl.BlockSpec(memory_space=pl.ANY)
    res = pl.pallas_call(
        hosted, name=name, grid=grid, in_specs=list(in_specs) + [any_spec] * ci, out_specs=list(out_specs) + [any_spec] * co,
        out_shape=list(out_shape) + comm.out_shape, scratch_shapes=list(scratch_shapes) + comm.sems,
        compiler_params=_params())(*operands, *comm.arrs)
    return list(res[:no]), list(res[no:])


def _rows(a):
    return a.reshape(-1, a.shape[-1])


def _slots(a):
    return a.reshape(N_DEV, -1, a.shape[-1])


def _local_step(x, mem, tgt, w, wire):
    tables = _rope_tables(x.shape[0])
    bd = _head_sum_matrix()
    two = lambda g: jnp.tile(g, (1, 2))
    gq2, gk2 = two(w["q_norm_g"]), two(w["k_norm_g"])

    w_in_t, hg_lb, conv_w = _comm_alone(_Gather([wire["w_in"], w["hg_lb"].reshape(4, -1), w["conv_w"][0]]), "gather_w_in")
    w_in_t = _rows(w_in_t)
    hg_lb = jnp.transpose(hg_lb.reshape(N_DEV, 2, 2, -1), (1, 2, 0, 3)).reshape(2, 2, HG_DIM)
    conv_w = jnp.transpose(conv_w, (1, 0, 2)).reshape(3, 2 * D_FF)
    lb_a0, lb_a1 = hg_lb[:, 0, :], hg_lb[:, 1, :]

    order = ("w_out", "w_xq", "w_xkv", "w_xo", "w_up", "w_down")
    w_in_t, *later = lax.optimization_barrier((w_in_t, *[wire[n] for n in order]))
    fetch, zero = _gather_begin(later, "w")
    take = lambda n, after: _rows(_gather_end(fetch, [order.index(n)], after)[0])

    h1, p = _norm_mm(x, w["pre_mix_g"] + zero, w_in_t, True, 1664, "in_proj")
    qr, kr = _qk_prep(p, gq2, gk2, tables, bd)
    (att, att32, lse), _ = _attn_fwd(qr, kr, p)
    lb = _lower_bounds(lb_a0, lb_a1)
    (of, s_f), _ = _hgrn_fwd(p, lb[0:1], False)
    (ob, s_b), _ = _hgrn_fwd(p, lb[1:2], True)
    w_out = take("w_out", ob)
    (cat, mixed, x1), _ = _mix_out(att, of, ob, p, w["hg_out_norm_g"], w_out, w["post_mix_g"], x)

    w_xq = take("w_xq", x1)
    h2, q2 = _norm_mm(x1, w["pre_x_g"], w_xq, False, 1024, "xq_proj")
    w_xkv_t = take("w_xkv", q2)
    mn, kv = _norm_mm(mem, w["mem_norm_g"], w_xkv_t, True, 1024, "xkv_proj")
    o2 = _xattn_fwd(q2, kv)
    w_xo = take("w_xo", o2)
    y2, x2 = _mm_postnorm_res(o2, w_xo, w["post_x_g"], x1, "xo_proj")

    w_up_t = take("w_up", x2)
    h3, u = _norm_mm(x2, w["pre_ffn_g"], w_up_t, True, 1408, "up_proj")
    a = _conv_gate_fwd(u, conv_w, w["conv_b"])
    w_down = take("w_down", a)
    y3, dx3, loss = _mm_postnorm_res_loss(a, w_down, w["post_ffn_g"], x2, tgt, "down_proj")

    g, pending = {}, {}
    dy3, da, g["post_ffn_g"] = _postnorm_bwd_mm(dx3, y3, w["post_ffn_g"], w_down, "down_bwd")
    pending["down"], zero = _exchange_begin([_slots(_dw(a, dy3, "dw_down"))], "down")
    (du_g, dcw_g, dcb_g), (du_v, dcw_v, dcb_v), _ = _conv_gate_bwd(u, conv_w, w["conv_b"] + zero, da)
    g["conv_w"] = jnp.concatenate([dcw_g, dcw_v], axis=1)
    g["conv_b"] = jnp.concatenate([dcb_g, dcb_v], axis=1)
    pending["up"], zero = _exchange_begin([_slots(_dw([du_g, du_v], h3, "dw_up"))], "up")
    (dx2, g["pre_ffn_g"]), _ = _mm_prenorm_bwd([du_g, du_v], w_up_t, True, x2, w["pre_ffn_g"] + zero, dx3, 1408, "up_bwd")

    dy2, do2, g["post_x_g"] = _postnorm_bwd_mm(dx2, y2, w["post_x_g"], w_xo, "xo_bwd")
    dw_xo = _dw(o2, dy2, "dw_xo")
    dq2, dkv = _xattn_bwd(q2, kv, do2)
    dw_xq = _dw(h2, dq2, "dw_xq")
    dkvb = _bf(dkv)
    dw_xkv = _dw(dkvb, mn, "dw_xkv")
    pending["x"], zero = _exchange_begin([_slots(dw_xo), _slots(dw_xq), _slots(dw_xkv)], "x")
    (_, g["mem_norm_g"]), _ = _mm_prenorm_bwd(dkvb, w_xkv_t, True, mem, w["mem_norm_g"], jnp.zeros_like(mem), 2048, "xkv_bwd")
    (dx1, g["pre_x_g"]), _ = _mm_prenorm_bwd(dq2, w_xq, False, x1, w["pre_x_g"] + zero, dx2, 1024, "xq_bwd")

    dmixed, dcat, g["post_mix_g"] = _postnorm_bwd_mm(dx1, mixed, w["post_mix_g"], w_out, "out_bwd")
    pending["out"], zero = _exchange_begin([_slots(_dw(cat, dmixed, "dw_out"))], "out")
    do, dhg, g["hg_out_norm_g"] = _rec_bwd(dcat, of, ob, p, w["hg_out_norm_g"] + zero)
    (dhq_f, dz_f, dhi_f, dlb_f), _ = _hgrn_bwd(p, lb[0:1], do, s_f, False)
    (dhq_b, dz_b, dhi_b, dlb_b), _ = _hgrn_bwd(p, lb[1:2], do, s_b, True)
    d_a0, d_a1 = _lower_bounds_bwd(lb_a0, lb_a1, jnp.concatenate([dlb_f, dlb_b], axis=0))
    g["hg_lb"] = jnp.stack([d_a0, d_a1], axis=1)
    (dqr, dkr, dv), _ = _attn_bwd(qr, kr, p, dcat, att32, lse)
    dp_qk, dgq, dgk = _qk_prep_bwd(p, dqr, dkr, gq2, gk2, tables, bd)
    g["q_norm_g"], g["k_norm_g"] = dgq, dgk
    dp = _assemble_dp(dp_qk, dv, dhq_f, dhq_b, dz_f, dz_b, dhi_f, dhi_b, dhg)
    pending["in"], zero = _exchange_begin([_slots(_dw(dp, h1, "dw_in"))], "in")
    (dx, g["pre_mix_g"]), _ = _mm_prenorm_bwd(dp, w_in_t, True, x, w["pre_mix_g"] + zero, dx1, 1664, "in_bwd")
    return loss, dx, g, pending


_COL_SHARDED = ("w_in", "w_xkv", "w_up")
_ROW_SHARDED = ("w_out", "w_xq", "w_xo", "w_down")
_REPLICATED = ("pre_mix_g", "q_norm_g", "k_norm_g", "hg_out_norm_g", "post_mix_g", "pre_x_g", "mem_norm_g", "post_x_g",
               "pre_ffn_g", "conv_b", "post_ffn_g")
_WEIGHTS = ("pre_mix_g", "w_in", "q_norm_g", "k_norm_g", "hg_lb", "hg_out_norm_g", "w_out", "post_mix_g", "pre_x_g",
            "mem_norm_g", "w_xq", "w_xkv", "w_xo", "post_x_g", "pre_ffn_g", "w_up", "conv_w", "conv_b", "w_down",
            "post_ffn_g")
_ADAM_TRANSPOSED = ("w_in", "w_up")
PACK_W = 1024


def _small_plan(shapes):
    plan, r = [], 0
    for vi, (rows, cols) in enumerate(shapes):
        for i in range(rows):
            for c0 in range(0, cols, PACK_W):
                plan.append((vi, i, c0, min(PACK_W, cols - c0), r))
                r += 1
    return plan, -(-r // 8) * 8


def _pack_small(vals):
    plan, nrows = _small_plan([val.shape for val in vals])

    def body(*refs):
        ins, out = refs[:-1], refs[-1]
        out[...] = jnp.zeros_like(out)
        for vi, i, c0, width, r in plan:
            out[r:r + 1, 0:width] = ins[vi][i:i + 1, c0:c0 + width]

    return pl.pallas_call(body, name="pack_small", out_shape=jax.ShapeDtypeStruct((nrows, PACK_W), F32))(*vals)


def _sum_unpack_small(packs, shapes):
    plan, _ = _small_plan(shapes)

    def body(*refs):
        p_ref, outs = refs[0], refs[1:]
        acc = p_ref[0]
        for i in range(1, N_DEV):
            acc = acc + p_ref[i]
        for vi, i, c0, width, r in plan:
            outs[vi][i:i + 1, c0:c0 + width] = acc[r:r + 1, 0:width]

    return pl.pallas_call(body, name="sum_unpack_small", out_shape=[jax.ShapeDtypeStruct(s, F32) for s in shapes])(packs)


def _adamw_many(ws, gs, ms, vs):
    n = len(ws)

    def body(*refs):
        w_refs, g_refs, m_refs, v_refs = (refs[k * n:(k + 1) * n] for k in range(4))
        d_refs, nm_refs, nv_refs = (refs[(4 + k) * n:(5 + k) * n] for k in range(3))
        for k in range(n):
            g_ = g_refs[k][...]
            m_ = ADAM_B1 * m_refs[k][...] + (1.0 - ADAM_B1) * g_
            v_ = ADAM_B2 * v_refs[k][...] + (1.0 - ADAM_B2) * (g_ * g_)
            m_hat = m_ / (1.0 - ADAM_B1 ** ADAM_STEP)
            v_hat = v_ / (1.0 - ADAM_B2 ** ADAM_STEP)
            d_refs[k][...] = -ADAM_LR * (m_hat / (jnp.sqrt(v_hat) + ADAM_EPS) + ADAM_WD * w_refs[k][...])
            nm_refs[k][...] = m_
            nv_refs[k][...] = v_

    shapes = [jax.ShapeDtypeStruct(a.shape, F32) for a in ws]
    outs = pl.pallas_call(body, name="adamw_small", out_shape=shapes * 3)(*ws, *gs, *ms, *vs)
    return outs[:n], outs[n:2 * n], outs[2 * n:]


def kernel(x, mem, pre_mix_g, w_in, q_norm_g, k_norm_g, hg_lb, hg_out_norm_g, w_out, post_mix_g, pre_x_g, mem_norm_g, w_xq, w_xkv, w_xo, post_x_g, pre_ffn_g, w_up, conv_w, conv_b, w_down, post_ffn_g, loss_target, m_pre_mix_g, m_w_in, m_q_norm_g, m_k_norm_g, m_hg_lb, m_hg_out_norm_g, m_w_out, m_post_mix_g, m_pre_x_g, m_mem_norm_g, m_w_xq, m_w_xkv, m_w_xo, m_post_x_g, m_pre_ffn_g, m_w_up, m_conv_w, m_conv_b, m_w_down, m_post_ffn_g, v_pre_mix_g, v_w_in, v_q_norm_g, v_k_norm_g, v_hg_lb, v_hg_out_norm_g, v_w_out, v_post_mix_g, v_pre_x_g, v_mem_norm_g, v_w_xq, v_w_xkv, v_w_xo, v_post_x_g, v_pre_ffn_g, v_w_up, v_conv_w, v_conv_b, v_w_down, v_post_ffn_g):
    args = dict(locals())
    w = {n: args[n] for n in _WEIGHTS}
    m = {n: args["m_" + n] for n in _WEIGHTS}
    v = {n: args["v_" + n] for n in _WEIGHTS}
    me = _dev_index(*_mesh_pos())

    wire = {n: _bf(w[n][0].T) for n in _COL_SHARDED}
    wire.update({n: _bf(w[n][0]) for n in _ROW_SHARDED})

    loss, grad_x, g, pending = _local_step(x[0], mem[0], loss_target[0], w, wire)

    grads, delta, new_m, new_v = {}, {}, {}, {}

    me_arr = jnp.reshape(me, (1,)).astype(jnp.int32)

    def update(n, parts):
        gsum = _sum_parts(*parts, me_arr, "sum_" + n)
        if n in _ADAM_TRANSPOSED:
            grads[n] = gsum.T[None]
            d_, m_, v_ = _adamw(w[n][0].T, gsum, m[n][0].T, v[n][0].T, "adamw_" + n)
            delta[n], new_m[n], new_v[n] = d_.T[None], m_.T[None], v_.T[None]
        else:
            gsum = gsum.T if n in _COL_SHARDED else gsum
            grads[n] = gsum[None]
            d_, m_, v_ = _adamw(w[n][0], gsum, m[n][0], v[n][0], "adamw_" + n)
            delta[n], new_m[n], new_v[n] = d_[None], m_[None], v_[None]

    after = grad_x
    for tag, names in (("down", ["w_down"]), ("up", ["w_up"]), ("x", ["w_xo", "w_xq", "w_xkv"]), ("out", ["w_out"]),
                       ("in", ["w_in"])):
        for n, parts in zip(names, _exchange_end(pending[tag], after)):
            update(n, parts)
            after = new_v[n]

    small = list(_REPLICATED) + ["hg_lb", "conv_w"]
    vals = [g[n] for n in _REPLICATED] + [g["hg_lb"].reshape(4, HG_DIM), g["conv_w"], jnp.pad(loss, ((0, 0), (0, LANE - 1)))]
    shapes = [val.shape for val in vals]
    (packs,) = _comm_alone(_Gather([_pack_small(vals)]), "gather_small_grads")
    summed = _sum_unpack_small(packs, shapes)
    loss = summed[-1][0, 0]
    for n, s in zip(small, summed[:-1]):
        grads[n] = s
    fold = lambda v2: v2[:, :ATT_HEAD_DIM] + v2[:, ATT_HEAD_DIM:]
    grads["q_norm_g"], grads["k_norm_g"] = fold(grads["q_norm_g"]), fold(grads["k_norm_g"])
    grads["hg_lb"] = lax.dynamic_slice_in_dim(grads["hg_lb"].reshape(2, 2, HG_DIM), me * (HG_DIM // N_DEV),
                                              HG_DIM // N_DEV, axis=2)
    grads["conv_w"] = lax.dynamic_slice_in_dim(grads["conv_w"], me * (2 * D_FF // N_DEV), 2 * D_FF // N_DEV, axis=1)[None]

    flat2 = lambda a: a.reshape(-1, a.shape[-1])
    outs = _adamw_many(*[[flat2(d[n]) for n in small] for d in (w, grads, m, v)])
    for dst, vals in zip((delta, new_m, new_v), outs):
        for n, val in zip(small, vals):
            dst[n] = val.reshape(w[n].shape)

    return (loss, grad_x[None], *[grads[n] for n in _WEIGHTS], *[delta[n] for n in _WEIGHTS],
            *[new_m[n] for n in _WEIGHTS], *[new_v[n] for n in _WEIGHTS])
```

```python
import jax
import jax.numpy as jnp
from jax import lax
from jax.experimental import pallas as pl
from jax.experimental.pallas import tpu as pltpu

F32 = jnp.float32
BF16 = jnp.bfloat16

D_MODEL = 1024
GRID_W = 64
EPS = 1e-6
ATT_HEADS = 8
ATT_HEAD_DIM = 64
ATT_Q_DIM = 512
ATT_KV_DIM = 128
ROPE_THETA = 10000.0
HG_HEADS = 4
HG_DIM = 512
HG_CHUNK = 32
HG_CHUNK_LOG2 = 5
HG_BLOCK_FWD = 256
HG_BLOCK_BWD = 128
N_IN = 3328
X_HEADS = 4
X_HEAD_DIM = 256
D_FF = 2816
N_DEV = 8
LANE = 128
ADAM_LR = 0.001
ADAM_B1 = 0.9
ADAM_B2 = 0.999
ADAM_EPS = 1e-08
ADAM_WD = 0.01
ADAM_STEP = 10
VMEM_LIMIT = 56 * 1024 * 1024

MESH_T = pl.DeviceIdType.MESH


def _params(**kw):
    return pltpu.CompilerParams(vmem_limit_bytes=VMEM_LIMIT, **kw)


def _dot(a, b, ca, cb):
    return lax.dot_general(a, b, (((ca,), (cb,)), ((), ())), preferred_element_type=F32)


def _bf(x):
    return x.astype(BF16)


def _sigmoid(x):
    return 1.0 / (1.0 + jnp.exp(-x))


def _rms_fwd(x, g):
    r = lax.rsqrt(jnp.mean(x * x, axis=-1, keepdims=True) + EPS)
    return x * r * g


def _rms_bwd(dy, x, g):
    r = lax.rsqrt(jnp.mean(x * x, axis=-1, keepdims=True) + EPS)
    xh = x * r
    dg = jnp.sum(dy * xh, axis=0, keepdims=True)
    t = dy * g
    dx = r * (t - xh * jnp.mean(t * xh, axis=-1, keepdims=True))
    return dx, dg


def _full(shape):
    nd = len(shape)
    return pl.BlockSpec(shape, lambda *a: (0,) * nd)


def _norm_mm(x, g, w, trans, tn, name, comm=None, tm=512):
    t, d = x.shape
    n = w.shape[0] if trans else w.shape[1]
    tm = min(tm, t)

    def body(x_ref, g_ref, w_ref, h_ref, p_ref):
        h = _bf(_rms_fwd(x_ref[...], g_ref[...]))
        h_ref[...] = h
        p_ref[...] = _dot(h, w_ref[...], 1, 1 if trans else 0)

    w_spec = pl.BlockSpec((tn, d), lambda i, j: (j, 0)) if trans else pl.BlockSpec((d, tn), lambda i, j: (0, j))
    (h, p), got = _call(
        body, (x, g, w), name=name, grid=(t // tm, n // tn),
        in_specs=[pl.BlockSpec((tm, d), lambda i, j: (i, 0)), _full((1, d)), w_spec],
        out_specs=[pl.BlockSpec((tm, d), lambda i, j: (i, 0)), pl.BlockSpec((tm, tn), lambda i, j: (i, j))],
        out_shape=[jax.ShapeDtypeStruct((t, d), BF16), jax.ShapeDtypeStruct((t, n), F32)], comm=comm)
    return (h, p, got) if comm is not None else (h, p)


def _mm_postnorm_res(a, w, g, res, name, tm=256):
    t, k = a.shape
    d = w.shape[1]

    def body(a_ref, w_ref, g_ref, res_ref, y_ref, o_ref):
        y = _dot(a_ref[...], w_ref[...], 1, 0)
        y_ref[...] = y
        o_ref[...] = res_ref[...] + _rms_fwd(y, g_ref[...])

    row = lambda width: pl.BlockSpec((tm, width), lambda i: (i, 0))
    return pl.pallas_call(
        body, name=name, grid=(t // tm,),
        in_specs=[row(k), _full((k, d)), _full((1, d)), row(d)],
        out_specs=[row(d), row(d)],
        out_shape=[jax.ShapeDtypeStruct((t, d), F32)] * 2,
        compiler_params=_params(),
    )(a, w, g, res)


def _mm_postnorm_res_loss(a, w, g, res, tgt, name, tm=256):
    t, k = a.shape
    d = w.shape[1]

    def body(a_ref, w_ref, g_ref, res_ref, tgt_ref, y_ref, dout_ref, loss_ref):
        @pl.when(pl.program_id(0) == 0)
        def _():
            loss_ref[...] = jnp.zeros_like(loss_ref)

        y = _dot(a_ref[...], w_ref[...], 1, 0)
        y_ref[...] = y
        diff = res_ref[...] + _rms_fwd(y, g_ref[...]) - tgt_ref[...]
        dout_ref[...] = diff * (1.0 / d)
        part = jnp.sum(jnp.sum(diff * diff, axis=-1, keepdims=True), axis=0, keepdims=True)
        loss_ref[...] += (0.5 / d) * part

    row = lambda width: pl.BlockSpec((tm, width), lambda i: (i, 0))
    return pl.pallas_call(
        body, name=name, grid=(t // tm,),
        in_specs=[row(k), _full((k, d)), _full((1, d)), row(d), row(d)],
        out_specs=[row(d), row(d), _full((1, 1))],
        out_shape=[jax.ShapeDtypeStruct((t, d), F32)] * 2 + [jax.ShapeDtypeStruct((1, 1), F32)],
        compiler_params=_params(),
    )(a, w, g, res, tgt)


def _postnorm_bwd_mm(dout, y, g, w, name, da_dtype=F32, tm=256):
    t, d = y.shape
    k = w.shape[0]

    def body(dout_ref, y_ref, g_ref, w_ref, dy_ref, da_ref, dg_ref):
        @pl.when(pl.program_id(0) == 0)
        def _():
            dg_ref[...] = jnp.zeros_like(dg_ref)

        dy, dg = _rms_bwd(dout_ref[...], y_ref[...], g_ref[...])
        dg_ref[...] += dg
        dyb = _bf(dy)
        dy_ref[...] = dyb
        da_ref[...] = _dot(dyb, w_ref[...], 1, 1).astype(da_dtype)

    row = lambda width: pl.BlockSpec((tm, width), lambda i: (i, 0))
    return pl.pallas_call(
        body, name=name, grid=(t // tm,),
        in_specs=[row(d), row(d), _full((1, d)), _full((k, d))],
        out_specs=[row(d), row(k), _full((1, d))],
        out_shape=[jax.ShapeDtypeStruct((t, d), BF16), jax.ShapeDtypeStruct((t, k), da_dtype), jax.ShapeDtypeStruct((1, d), F32)],
        compiler_params=_params(),
    )(dout, y, g, w)


def _mm_prenorm_bwd(dp, w, trans, x, g, dres, name, comm=None, tm=512):
    dps = list(dp) if isinstance(dp, (list, tuple)) else [dp]
    nparts = len(dps)
    t, n_each = dps[0].shape
    d = x.shape[1]
    tm = min(tm, t)

    def body(*refs):
        dp_refs = refs[:nparts]
        w_ref, x_ref, g_ref, dres_ref, dx_ref, dg_ref = refs[nparts:]

        @pl.when(pl.program_id(0) == 0)
        def _():
            dg_ref[...] = jnp.zeros_like(dg_ref)

        dh = None
        for part, dp_ref in enumerate(dp_refs):
            cols = slice(part * n_each, (part + 1) * n_each)
            term = _dot(dp_ref[...], w_ref[cols, :], 1, 0) if trans else _dot(dp_ref[...], w_ref[:, cols], 1, 1)
            dh = term if dh is None else dh + term
        dx, dg = _rms_bwd(dh, x_ref[...], g_ref[...])
        dg_ref[...] += dg
        dx_ref[...] = dres_ref[...] + dx

    row = lambda width: pl.BlockSpec((tm, width), lambda i: (i, 0))
    return _call(
        body, (*dps, w, x, g, dres), name=name, grid=(t // tm,),
        in_specs=[row(n_each)] * nparts + [_full(w.shape), row(d), _full((1, d)), row(d)],
        out_specs=[row(d), _full((1, d))],
        out_shape=[jax.ShapeDtypeStruct((t, d), F32), jax.ShapeDtypeStruct((1, d), F32)], comm=comm)


def _dw(a, b, name, bcol=None, tka=256):
    parts = list(a) if isinstance(a, (list, tuple)) else [a]
    nparts = len(parts)
    t, ka_each = parts[0].shape
    bci, nb = (0, b.shape[1]) if bcol is None else bcol
    nt_each = ka_each // tka

    def body(*refs):
        a_refs, b_ref, o_ref = refs[:nparts], refs[nparts], refs[nparts + 1]
        i = pl.program_id(0)
        for part, a_ref in enumerate(a_refs):
            @pl.when((i >= part * nt_each) & (i < (part + 1) * nt_each))
            def _():
                o_ref[...] = _bf(_dot(a_ref[...], b_ref[...], 0, 0))

    a_spec = lambda part: pl.BlockSpec((t, tka), lambda i: (0, jnp.clip(i - part * nt_each, 0, nt_each - 1)))
    return pl.pallas_call(
        body, name=name, grid=(nt_each * nparts,),
        in_specs=[a_spec(part) for part in range(nparts)] + [pl.BlockSpec((t, nb), lambda i: (0, bci))],
        out_specs=pl.BlockSpec((tka, nb), lambda i: (i, 0)),
        out_shape=jax.ShapeDtypeStruct((nparts * ka_each, nb), BF16),
        compiler_params=_params(),
    )(*parts, b)


def _rope_tables(t):
    pos = jnp.arange(t)
    r = (pos // GRID_W).astype(F32)
    c = (pos % GRID_W).astype(F32)
    npair = ATT_HEAD_DIM // 4
    inv = jnp.power(ROPE_THETA, -jnp.arange(npair, dtype=F32) / npair)
    ang = jnp.concatenate([r[:, None] * inv, c[:, None] * inv], axis=-1)
    cos = jnp.repeat(jnp.cos(ang), 2, axis=-1)
    sin = jnp.repeat(jnp.sin(ang), 2, axis=-1)
    even = (jnp.arange(ATT_HEAD_DIM) % 2) == 0
    sa = jnp.where(even, -sin, 0.0)
    sb = jnp.where(even, 0.0, sin)
    two = lambda a: jnp.tile(a, (1, 2))
    return two(cos), two(sa), two(sb)


def _head_sum_matrix():
    a = jnp.arange(LANE) // ATT_HEAD_DIM
    return (a[:, None] == a[None, :]).astype(BF16)


def _head_mean(v, bd):
    hi = _bf(v)
    lo = _bf(v - hi.astype(F32))
    return (_dot(hi, bd, 1, 0) + _dot(lo, bd, 1, 0)) * (1.0 / ATT_HEAD_DIM)


def _qk_prep(p, gq, gk, tables, bd, tm=512):
    t = p.shape[0]
    tm = min(tm, t)
    cc, sa, sb = tables

    def body(p_ref, gq_ref, gk_ref, cc_ref, sa_ref, sb_ref, bd_ref, q_ref, k_ref):
        cc_, sa_, sb_, bd_ = cc_ref[...], sa_ref[...], sb_ref[...], bd_ref[...]
        low = lax.broadcasted_iota(jnp.int32, (tm, LANE), 1) < ATT_HEAD_DIM

        def normrope(xs, g):
            xn = xs * lax.rsqrt(_head_mean(xs * xs, bd_) + EPS) * g
            return xn * cc_ + pltpu.roll(xn, LANE - 1, 1) * sa_ + pltpu.roll(xn, 1, 1) * sb_

        for j in range(4):
            y = normrope(p_ref[:, j * LANE:(j + 1) * LANE], gq_ref[...]) * (ATT_HEAD_DIM ** -0.5)
            yr = pltpu.roll(y, ATT_HEAD_DIM, 1)
            if j // 2 == 0:
                h0, h1 = jnp.where(low, y, 0.0), jnp.where(low, yr, 0.0)
            else:
                h0, h1 = jnp.where(low, 0.0, yr), jnp.where(low, 0.0, y)
            q_ref[:, (2 * j) * LANE:(2 * j + 1) * LANE] = _bf(h0)
            q_ref[:, (2 * j + 1) * LANE:(2 * j + 2) * LANE] = _bf(h1)
        k_ref[...] = _bf(normrope(p_ref[:, ATT_Q_DIM:ATT_Q_DIM + LANE], gk_ref[...]))

    row = lambda width: pl.BlockSpec((tm, width), lambda i: (i, 0))
    return pl.pallas_call(
        body, name="qk_prep", grid=(t // tm,),
        in_specs=[row(ATT_Q_DIM + LANE), _full((1, LANE)), _full((1, LANE)), row(LANE), row(LANE), row(LANE),
                  _full((LANE, LANE))],
        out_specs=[row(ATT_HEADS * LANE), row(LANE)],
        out_shape=[jax.ShapeDtypeStruct((t, ATT_HEADS * LANE), BF16), jax.ShapeDtypeStruct((t, LANE), BF16)],
        compiler_params=_params(),
    )(p, gq, gk, cc, sa, sb, bd)


def _qk_prep_bwd(p, dq, dk, gq, gk, tables, bd, tm=512):
    t = p.shape[0]
    tm = min(tm, t)
    cc, sa, sb = tables

    def body(p_ref, dq_ref, dk_ref, gq_ref, gk_ref, cc_ref, sa_ref, sb_ref, bd_ref, dp_ref, dgq_ref, dgk_ref):
        @pl.when(pl.program_id(0) == 0)
        def _():
            dgq_ref[...] = jnp.zeros_like(dgq_ref)
            dgk_ref[...] = jnp.zeros_like(dgk_ref)

        cc_, sa_, sb_, bd_ = cc_ref[...], sa_ref[...], sb_ref[...], bd_ref[...]
        low = lax.broadcasted_iota(jnp.int32, (tm, LANE), 1) < ATT_HEAD_DIM

        def bwd(xs, g, dy):
            r = lax.rsqrt(_head_mean(xs * xs, bd_) + EPS)
            xh = xs * r
            dxn = dy * cc_ + pltpu.roll(dy * sa_, 1, 1) + pltpu.roll(dy * sb_, LANE - 1, 1)
            dg = jnp.sum(dxn * xh, axis=0, keepdims=True)
            tt = dxn * g
            return r * (tt - xh * _head_mean(tt * xh, bd_)), dg

        dgq = jnp.zeros((1, LANE), F32)
        for j in range(4):
            d0 = dq_ref[:, (2 * j) * LANE:(2 * j + 1) * LANE]
            d1 = dq_ref[:, (2 * j + 1) * LANE:(2 * j + 2) * LANE]
            if j // 2 == 0:
                dy = jnp.where(low, d0, pltpu.roll(d1, ATT_HEAD_DIM, 1))
            else:
                dy = jnp.where(low, pltpu.roll(d0, ATT_HEAD_DIM, 1), d1)
            dx, dg = bwd(p_ref[:, j * LANE:(j + 1) * LANE], gq_ref[...], dy * (ATT_HEAD_DIM ** -0.5))
            dp_ref[:, j * LANE:(j + 1) * LANE] = _bf(dx)
            dgq = dgq + dg
        dgq_ref[...] += dgq
        dx, dg = bwd(p_ref[:, ATT_Q_DIM:ATT_Q_DIM + LANE], gk_ref[...], dk_ref[...])
        dp_ref[:, ATT_Q_DIM:ATT_Q_DIM + LANE] = _bf(dx)
        dgk_ref[...] += dg

    row = lambda width: pl.BlockSpec((tm, width), lambda i: (i, 0))
    return pl.pallas_call(
        body, name="qk_prep_bwd", grid=(t // tm,),
        in_specs=[row(ATT_Q_DIM + LANE), row(ATT_HEADS * LANE), row(LANE), _full((1, LANE)), _full((1, LANE)),
                  row(LANE), row(LANE), row(LANE), _full((LANE, LANE))],
        out_specs=[row(ATT_Q_DIM + LANE), _full((1, LANE)), _full((1, LANE))],
        out_shape=[jax.ShapeDtypeStruct((t, ATT_Q_DIM + LANE), BF16), jax.ShapeDtypeStruct((1, LANE), F32),
                   jax.ShapeDtypeStruct((1, LANE), F32)],
        compiler_params=_params(),
    )(p, dq, dk, gq, gk, cc, sa, sb, bd)


def _attn_fwd(q, k, p, comm=None, tq=256):
    t = k.shape[0]
    tq = min(tq, t)
    v_blk = (ATT_Q_DIM + ATT_KV_DIM) // LANE

    def body(q_ref, k_ref, v_ref, o_ref, o32_ref, lse_ref):
        k_ = k_ref[...]
        v = v_ref[...]
        lowk = lax.broadcasted_iota(jnp.int32, (t, LANE), 1) < ATT_HEAD_DIM
        vm = (_bf(jnp.where(lowk, v, 0.0)), _bf(jnp.where(lowk, 0.0, v)))
        for j in range(4):
            kvh = j // 2
            acc = None
            for sub in range(2):
                h = 2 * j + sub
                s = _dot(q_ref[:, h * LANE:(h + 1) * LANE], k_, 1, 1)
                mx = jnp.max(s, axis=-1, keepdims=True)
                e = jnp.exp(s - mx)
                l = jnp.sum(e, axis=-1, keepdims=True)
                lse_ref[h] = mx + jnp.log(l)
                o = _dot(_bf(e), vm[kvh], 1, 0) * (1.0 / l)
                if sub != kvh:
                    o = pltpu.roll(o, ATT_HEAD_DIM, 1)
                acc = o if acc is None else acc + o
            o32_ref[:, j * LANE:(j + 1) * LANE] = acc
            o_ref[:, j * LANE:(j + 1) * LANE] = _bf(acc)

    row = pl.BlockSpec((tq, ATT_Q_DIM), lambda i: (i, 0))
    outs, got = _call(
        body, (q, k, p), name="attn_fwd", grid=(t // tq,),
        in_specs=[pl.BlockSpec((tq, ATT_HEADS * LANE), lambda i: (i, 0)), _full((t, LANE)),
                  pl.BlockSpec((t, LANE), lambda i: (0, v_blk))],
        out_specs=[row, row, pl.BlockSpec((ATT_HEADS, tq, 1), lambda i: (0, i, 0))],
        out_shape=[jax.ShapeDtypeStruct((t, ATT_Q_DIM), BF16), jax.ShapeDtypeStruct((t, ATT_Q_DIM), F32),
                   jax.ShapeDtypeStruct((ATT_HEADS, t, 1), F32)], comm=comm)
    return outs, got


def _attn_bwd(q, k, p, dcat, o32, lse, comm=None, tq=256):
    t = k.shape[0]
    tq = min(tq, t)
    v_blk = (ATT_Q_DIM + ATT_KV_DIM) // LANE

    def body(q_ref, k_ref, v_ref, do_ref, o_ref, lse_ref, dq_ref, dk_ref, dv_ref):
        @pl.when(pl.program_id(0) == 0)
        def _():
            dk_ref[...] = jnp.zeros_like(dk_ref)
            dv_ref[...] = jnp.zeros_like(dv_ref)

        k_ = k_ref[...]
        vb = _bf(v_ref[...])
        lowq = lax.broadcasted_iota(jnp.int32, (tq, LANE), 1) < ATT_HEAD_DIM
        dk_acc = jnp.zeros((t, LANE), F32)
        dv_acc = jnp.zeros((t, LANE), F32)
        for j in range(4):
            kvh = j // 2
            dop = do_ref[:, j * LANE:(j + 1) * LANE]
            prod = dop * o_ref[:, j * LANE:(j + 1) * LANE]
            d_low = jnp.sum(jnp.where(lowq, prod, 0.0), axis=-1, keepdims=True)
            d_sub = (d_low, jnp.sum(prod, axis=-1, keepdims=True) - d_low)
            for sub in range(2):
                h = 2 * j + sub
                src = dop if sub == kvh else pltpu.roll(dop, ATT_HEAD_DIM, 1)
                do_h = _bf(jnp.where(lowq, src, 0.0) if kvh == 0 else jnp.where(lowq, 0.0, src))
                qh = q_ref[:, h * LANE:(h + 1) * LANE]
                pr = jnp.exp(_bf(_dot(qh, k_, 1, 1) - lse_ref[h]))
                ds = pr * _bf(_dot(do_h, vb, 1, 1) - d_sub[sub])
                dq_ref[:, h * LANE:(h + 1) * LANE] = _dot(ds, k_, 1, 0)
                dk_acc = dk_acc + _dot(ds, qh, 0, 0)
                dv_acc = dv_acc + _dot(pr, do_h, 0, 0)
        dk_ref[...] += dk_acc
        dv_ref[...] += dv_acc

    row = pl.BlockSpec((tq, ATT_Q_DIM), lambda i: (i, 0))
    return _call(
        body, (q, k, p, dcat, o32, lse), name="attn_bwd", grid=(t // tq,),
        in_specs=[pl.BlockSpec((tq, ATT_HEADS * LANE), lambda i: (i, 0)), _full((t, LANE)),
                  pl.BlockSpec((t, LANE), lambda i: (0, v_blk)), row, row,
                  pl.BlockSpec((ATT_HEADS, tq, 1), lambda i: (0, i, 0))],
        out_specs=[pl.BlockSpec((tq, ATT_HEADS * LANE), lambda i: (i, 0)), _full((t, LANE)), _full((t, LANE))],
        out_shape=[jax.ShapeDtypeStruct((t, ATT_HEADS * LANE), F32), jax.ShapeDtypeStruct((t, LANE), F32),
                   jax.ShapeDtypeStruct((t, LANE), F32)], comm=comm)


def _lower_bounds(a0, a1):
    def body(a0_ref, a1_ref, lb_ref):
        m = jnp.maximum(a0_ref[...], a1_ref[...])
        e0, e1 = jnp.exp(a0_ref[...] - m), jnp.exp(a1_ref[...] - m)
        lb_ref[...] = e0 / (e0 + e1)

    return pl.pallas_call(body, name="lower_bounds", out_shape=jax.ShapeDtypeStruct(a0.shape, F32))(a0, a1)


def _lower_bounds_bwd(a0, a1, dlb):
    def body(a0_ref, a1_ref, dlb_ref, d0_ref, d1_ref):
        m = jnp.maximum(a0_ref[...], a1_ref[...])
        e0, e1 = jnp.exp(a0_ref[...] - m), jnp.exp(a1_ref[...] - m)
        lb = e0 / (e0 + e1)
        d0 = dlb_ref[...] * lb * (1.0 - lb)
        d0_ref[...] = d0
        d1_ref[...] = -d0

    return pl.pallas_call(body, name="lower_bounds_bwd", out_shape=[jax.ShapeDtypeStruct(a0.shape, F32)] * 2)(a0, a1, dlb)


def _chunk_scan(x, pos, down):
    n = x.shape[0]
    s = 1
    while s < HG_CHUNK:
        if down:
            x = x + jnp.where(pos >= s, pltpu.roll(x, s, 0), 0.0)
        else:
            x = x + jnp.where(pos < HG_CHUNK - s, pltpu.roll(x, n - s, 0), 0.0)
        s *= 2
    return x


def _chunk_concat(x, cid, nc):
    return jnp.concatenate([_bf(jnp.where(cid == c, x, 0.0)) for c in range(nc)], axis=1)


def _chunk_pick(x, cid, nc):
    out = jnp.where(cid == 0, x[:, :LANE], 0.0)
    for c in range(1, nc):
        out = out + jnp.where(cid == c, x[:, c * LANE:(c + 1) * LANE], 0.0)
    return out


def _hgrn_gates(hq, z, lb):
    sq = _sigmoid(hq)
    sig = _sigmoid(z)
    f = lb + (1.0 - lb) * sig
    return hq * sq, sq, sig, f, jnp.log(f)


def _hgrn_local(q, f, g, v, pos, cid, amask, rev, nc):
    k = 1.0 - f
    ba = _chunk_scan(g, pos, not rev)
    bb = _chunk_scan(g, pos, rev)
    eq = 0.5 * (ba - bb + g)
    ex_q, ex_k, ex_i, ex_d = jnp.exp(eq), jnp.exp(-eq), jnp.exp(ba), jnp.exp(bb - g)
    qb, kb, qi, kd = q * ex_q, k * ex_k, q * ex_i, k * ex_d
    dvec = jnp.exp(ba + bb - g)
    a = jnp.where(amask, _dot(_bf(qb), _bf(kb), 1, 1), 0.0)
    kd_m = _chunk_concat(kd, cid, nc)
    qi_m = _chunk_concat(qi, cid, nc)
    ut_all = _dot(_bf(v), kd_m, 0, 0)
    return dict(k=k, ex_q=ex_q, ex_k=ex_k, ex_i=ex_i, ex_d=ex_d, qb=qb, kb=kb, qi=qi, kd=kd, dvec=dvec, a=a,
                kd_m=kd_m, qi_m=qi_m, ut_all=ut_all)


def _hgrn_masks(n, rev):
    row = lax.broadcasted_iota(jnp.int32, (n, LANE), 0)
    ti = lax.broadcasted_iota(jnp.int32, (n, n), 0)
    si = lax.broadcasted_iota(jnp.int32, (n, n), 1)
    tri = (si >= ti) if rev else (si <= ti)
    same = jnp.right_shift(ti, HG_CHUNK_LOG2) == jnp.right_shift(si, HG_CHUNK_LOG2)
    return jnp.bitwise_and(row, HG_CHUNK - 1), jnp.right_shift(row, HG_CHUNK_LOG2), same & tri


def _hgrn_specs(t, rev, bwd):
    n = min(HG_BLOCK_BWD if bwd else HG_BLOCK_FWD, t)
    nb = t // n
    if rev != bwd:
        blk = lambda i: nb - 1 - i
    else:
        blk = lambda i: i
    col = lambda base: pl.BlockSpec((n, 2 * LANE), lambda hp, i: (blk(i), base + hp))
    return n, nb, blk, col


def _hgrn_fwd(p, lb, rev, comm=None):
    t = p.shape[0]
    n, nb, blk, col = _hgrn_specs(t, rev, False)
    nc = n // HG_CHUNK
    sub = n // min(HG_BLOCK_BWD, t)
    z_base = 7 if rev else 5

    def body(hq_ref, z_ref, hi_ref, lb_ref, o_ref, ssave_ref, st_scr):
        @pl.when(pl.program_id(1) == 0)
        def _():
            st_scr[...] = jnp.zeros_like(st_scr)

        pos, cid, amask = _hgrn_masks(n, rev)
        order = list(range(nc))[::-1] if rev else list(range(nc))
        for hh in range(2):
            sl = slice(hh * LANE, (hh + 1) * LANE)
            q, _, _, f, g = _hgrn_gates(hq_ref[:, sl], z_ref[:, sl], lb_ref[0:1, sl])
            v = hi_ref[:, sl]
            c_ = _hgrn_local(q, f, g, v, pos, cid, amask, rev, nc)
            st = st_scr[hh]
            cols = [None] * nc
            for c in order:
                cols[c] = st
                st = st * c_["dvec"][c * HG_CHUNK:c * HG_CHUNK + 1, :] + c_["ut_all"][:, c * LANE:(c + 1) * LANE]
            st_scr[hh] = st
            for s in range(sub):
                ssave_ref[sub - 1 - s if rev else s, sl, :] = cols[order[s * (nc // sub)]]
            st_all = _bf(jnp.concatenate(cols, axis=1))
            o_ref[:, sl] = _dot(_bf(c_["a"]), _bf(v), 1, 0) + _dot(c_["qi_m"], st_all, 1, 1)

    return _call(
        body, (p, p, p, lb), name="hgrn_fwd_rev" if rev else "hgrn_fwd", grid=(2, nb),
        in_specs=[col(3), col(z_base), col(9), pl.BlockSpec((1, 2 * LANE), lambda hp, i: (0, hp))],
        out_specs=[pl.BlockSpec((n, 2 * LANE), lambda hp, i: (blk(i), hp)),
                   pl.BlockSpec((sub, 2 * LANE, LANE), lambda hp, i: (blk(i), hp, 0))],
        out_shape=[jax.ShapeDtypeStruct((t, HG_DIM), F32), jax.ShapeDtypeStruct((nb * sub, HG_DIM, LANE), F32)],
        scratch_shapes=[pltpu.VMEM((2, LANE, LANE), F32)], comm=comm)


def _hgrn_bwd(p, lb, do, ssave, rev, comm=None):
    t = p.shape[0]
    n, nb, blk, col = _hgrn_specs(t, rev, True)
    nc = n // HG_CHUNK
    z_base = 7 if rev else 5

    def body(hq_ref, z_ref, hi_ref, lb_ref, do_ref, ssave_ref, dhq_ref, dz_ref, dhi_ref, dlb_ref, dst_scr):
        @pl.when(pl.program_id(1) == 0)
        def _():
            dst_scr[...] = jnp.zeros_like(dst_scr)
            dlb_ref[...] = jnp.zeros_like(dlb_ref)

        pos, cid, amask = _hgrn_masks(n, rev)
        order = list(range(nc))[::-1] if rev else list(range(nc))
        for hh in range(2):
            sl = slice(hh * LANE, (hh + 1) * LANE)
            hq, lbv = hq_ref[:, sl], lb_ref[0:1, sl]
            q, sq, sig, f, g = _hgrn_gates(hq, z_ref[:, sl], lbv)
            v = hi_ref[:, sl]
            c_ = _hgrn_local(q, f, g, v, pos, cid, amask, rev, nc)
            dvec, ut_all = c_["dvec"], c_["ut_all"]
            drow = lambda c: dvec[c * HG_CHUNK:c * HG_CHUNK + 1, :]
            st = ssave_ref[0, sl, :]
            cols = [None] * nc
            for c in order:
                cols[c] = st
                st = st * drow(c) + ut_all[:, c * LANE:(c + 1) * LANE]
            dob, vb = _bf(do_ref[:, sl]), _bf(v)
            gt_all = _dot(dob, c_["qi_m"], 0, 0)
            dcur = dst_scr[hh]
            dnext = [None] * nc
            ddrow = [None] * nc
            for c in order[::-1]:
                dnext[c] = dcur
                ddrow[c] = jnp.sum(cols[c] * dcur, axis=0, keepdims=True) * drow(c)
                dcur = dcur * drow(c) + gt_all[:, c * LANE:(c + 1) * LANE]
            dst_scr[hh] = dcur
            dsn_all = _bf(jnp.concatenate(dnext, axis=1))
            st_all = _bf(jnp.concatenate(cols, axis=1))
            da = _bf(jnp.where(amask, _dot(dob, vb, 1, 1), 0.0))
            dv = _dot(_bf(c_["a"]), dob, 0, 0) + _dot(c_["kd_m"], dsn_all, 1, 1)
            dqb = _dot(da, _bf(c_["kb"]), 1, 0)
            dkb = _dot(da, _bf(c_["qb"]), 0, 0)
            dqi = _chunk_pick(_dot(dob, st_all, 1, 0), cid, nc)
            dkd = _chunk_pick(_dot(vb, dsn_all, 1, 0), cid, nc)
            dq = dqb * c_["ex_q"] + dqi * c_["ex_i"]
            dk = dkb * c_["ex_k"] + dkd * c_["ex_d"]
            e = dqb * c_["qb"] - dkb * c_["kb"] + dqi * c_["qi"]
            w = dkd * c_["kd"]
            dtot = jnp.where(cid == 0, ddrow[0], 0.0)
            for c in range(1, nc):
                dtot = dtot + jnp.where(cid == c, ddrow[c], 0.0)
            dg = _chunk_scan(e, pos, rev) + (_chunk_scan(w, pos, not rev) - w) + dtot
            df = dg / f - dk
            dz_ref[:, sl] = df * (1.0 - lbv) * sig * (1.0 - sig)
            dlb_ref[0:1, sl] += jnp.sum(df * (1.0 - sig), axis=0, keepdims=True)
            dhq_ref[:, sl] = dq * (sq * (1.0 + hq * (1.0 - sq)))
            dhi_ref[:, sl] = dv

    out = pl.BlockSpec((n, 2 * LANE), lambda hp, i: (blk(i), hp))
    return _call(
        body, (p, p, p, lb, do, ssave), name="hgrn_bwd_rev" if rev else "hgrn_bwd", grid=(2, nb),
        in_specs=[col(3), col(z_base), col(9), pl.BlockSpec((1, 2 * LANE), lambda hp, i: (0, hp)), out,
                  pl.BlockSpec((1, 2 * LANE, LANE), lambda hp, i: (blk(i), hp, 0))],
        out_specs=[out, out, out, pl.BlockSpec((1, 2 * LANE), lambda hp, i: (0, hp))],
        out_shape=[jax.ShapeDtypeStruct((t, HG_DIM), F32)] * 3 + [jax.ShapeDtypeStruct((1, HG_DIM), F32)],
        scratch_shapes=[pltpu.VMEM((2, LANE, LANE), F32)], comm=comm)


def _mix_out(att, of, ob, p, gout, w_out, g_post, x, comm=None, tm=256):
    t, d = x.shape
    tm = min(tm, t)

    def body(att_ref, of_ref, ob_ref, hg0_ref, hg1_ref, gout_ref, w_ref, g_ref, x_ref, cat_ref, y_ref, x1_ref):
        cat_ref[:, :ATT_Q_DIM] = att_ref[...]
        for h in range(HG_HEADS):
            sl = slice(h * LANE, (h + 1) * LANE)
            hg_ref = hg0_ref if h < 2 else hg1_ref
            hg = hg_ref[:, (h % 2) * LANE:(h % 2 + 1) * LANE]
            nrm = _rms_fwd(of_ref[:, sl] + ob_ref[:, sl], gout_ref[...])
            cat_ref[:, ATT_Q_DIM + h * LANE:ATT_Q_DIM + (h + 1) * LANE] = _bf(nrm * hg * _sigmoid(hg))
        y = _dot(cat_ref[...], w_ref[...], 1, 0)
        y_ref[...] = y
        x1_ref[...] = x_ref[...] + _rms_fwd(y, g_ref[...])

    row = lambda width: pl.BlockSpec((tm, width), lambda i: (i, 0))
    return _call(
        body, (att, of, ob, p, p, gout, w_out, g_post, x), name="mix_out", grid=(t // tm,),
        in_specs=[row(ATT_Q_DIM), row(HG_DIM), row(HG_DIM), pl.BlockSpec((tm, 2 * LANE), lambda i: (i, 11)),
                  pl.BlockSpec((tm, 2 * LANE), lambda i: (i, 12)), _full((1, LANE)), _full((d, d)), _full((1, d)), row(d)],
        out_specs=[row(d), row(d), row(d)],
        out_shape=[jax.ShapeDtypeStruct((t, d), BF16), jax.ShapeDtypeStruct((t, d), F32), jax.ShapeDtypeStruct((t, d), F32)],
        comm=comm)


def _rec_bwd(dcat, of, ob, p, gout, tm=256):
    t = of.shape[0]
    tm = min(tm, t)

    def body(dc_ref, of_ref, ob_ref, hg0_ref, hg1_ref, gout_ref, do_ref, dhg_ref, dgo_ref):
        @pl.when(pl.program_id(0) == 0)
        def _():
            dgo_ref[...] = jnp.zeros_like(dgo_ref)

        dgo = jnp.zeros((1, LANE), F32)
        for h in range(HG_HEADS):
            sl = slice(h * LANE, (h + 1) * LANE)
            hg_ref = hg0_ref if h < 2 else hg1_ref
            hg = hg_ref[:, (h % 2) * LANE:(h % 2 + 1) * LANE]
            o = of_ref[:, sl] + ob_ref[:, sl]
            sg = _sigmoid(hg)
            drec = dc_ref[:, sl]
            dhg_ref[:, sl] = drec * _rms_fwd(o, gout_ref[...]) * (sg * (1.0 + hg * (1.0 - sg)))
            do, dg = _rms_bwd(drec * hg * sg, o, gout_ref[...])
            do_ref[:, sl] = do
            dgo = dgo + dg
        dgo_ref[...] += dgo

    row = lambda width: pl.BlockSpec((tm, width), lambda i: (i, 0))
    return pl.pallas_call(
        body, name="rec_bwd", grid=(t // tm,),
        in_specs=[pl.BlockSpec((tm, HG_DIM), lambda i: (i, 1)), row(HG_DIM), row(HG_DIM),
                  pl.BlockSpec((tm, 2 * LANE), lambda i: (i, 11)), pl.BlockSpec((tm, 2 * LANE), lambda i: (i, 12)),
                  _full((1, LANE))],
        out_specs=[row(HG_DIM), row(HG_DIM), _full((1, LANE))],
        out_shape=[jax.ShapeDtypeStruct((t, HG_DIM), F32), jax.ShapeDtypeStruct((t, HG_DIM), F32),
                   jax.ShapeDtypeStruct((1, LANE), F32)],
        compiler_params=_params(),
    )(dcat, of, ob, p, p, gout)


def _assemble_dp(dp_qk, dv, dhq_f, dhq_b, dz_f, dz_b, dhi_f, dhi_b, dhg, tm=512):
    t = dv.shape[0]
    tm = min(tm, t)
    qk_w = ATT_Q_DIM + LANE

    def body(qk_ref, dv_ref, hqf_ref, hqb_ref, zf_ref, zb_ref, hif_ref, hib_ref, hg_ref, dp_ref):
        o = 0
        dp_ref[:, o:o + qk_w] = qk_ref[...]
        o += qk_w
        dp_ref[:, o:o + LANE] = _bf(dv_ref[...])
        o += LANE
        for val in (hqf_ref[...] + hqb_ref[...], zf_ref[...], zb_ref[...], hif_ref[...] + hib_ref[...], hg_ref[...]):
            dp_ref[:, o:o + HG_DIM] = _bf(val)
            o += HG_DIM

    row = lambda width: pl.BlockSpec((tm, width), lambda i: (i, 0))
    return pl.pallas_call(
        body, name="assemble_dp", grid=(t // tm,),
        in_specs=[row(qk_w), row(LANE)] + [row(HG_DIM)] * 7,
        out_specs=row(N_IN),
        out_shape=jax.ShapeDtypeStruct((t, N_IN), BF16),
        compiler_params=_params(),
    )(dp_qk, dv, dhq_f, dhq_b, dz_f, dz_b, dhi_f, dhi_b, dhg)


def _xattn_fwd(q, kv, tq=512):
    t, d = q.shape
    tq = min(tq, t)
    nm = kv.shape[0]

    def body(q_ref, kv_ref, o_ref):
        for h in range(X_HEADS):
            sl = slice(h * X_HEAD_DIM, (h + 1) * X_HEAD_DIM)
            s = _dot(_bf(q_ref[:, sl]), _bf(kv_ref[:, sl]), 1, 1) * (X_HEAD_DIM ** -0.5)
            e = jnp.exp(s - jnp.max(s, axis=-1, keepdims=True))
            pr = e * (1.0 / jnp.sum(e, axis=-1, keepdims=True))
            o_ref[:, sl] = _bf(_dot(_bf(pr), _bf(kv_ref[:, d + h * X_HEAD_DIM:d + (h + 1) * X_HEAD_DIM]), 1, 0))

    return pl.pallas_call(
        body, name="xattn_fwd", grid=(t // tq,),
        in_specs=[pl.BlockSpec((tq, d), lambda i: (i, 0)), _full((nm, 2 * d))],
        out_specs=pl.BlockSpec((tq, d), lambda i: (i, 0)),
        out_shape=jax.ShapeDtypeStruct((t, d), BF16),
        compiler_params=_params(),
    )(q, kv)


def _xattn_bwd(q, kv, do, tq=512):
    t, d = q.shape
    tq = min(tq, t)
    nm = kv.shape[0]

    def body(q_ref, kv_ref, do_ref, dq_ref, dkv_ref):
        @pl.when(pl.program_id(0) == 0)
        def _():
            dkv_ref[...] = jnp.zeros_like(dkv_ref)

        for h in range(X_HEADS):
            sl = slice(h * X_HEAD_DIM, (h + 1) * X_HEAD_DIM)
            slv = slice(d + h * X_HEAD_DIM, d + (h + 1) * X_HEAD_DIM)
            qb, kb, vb, dob = _bf(q_ref[:, sl]), _bf(kv_ref[:, sl]), _bf(kv_ref[:, slv]), _bf(do_ref[:, sl])
            s = _dot(qb, kb, 1, 1) * (X_HEAD_DIM ** -0.5)
            e = jnp.exp(s - jnp.max(s, axis=-1, keepdims=True))
            pr = e * (1.0 / jnp.sum(e, axis=-1, keepdims=True))
            dpr = _dot(dob, vb, 1, 1)
            ds = _bf(pr * (dpr - jnp.sum(pr * dpr, axis=-1, keepdims=True)) * (X_HEAD_DIM ** -0.5))
            dq_ref[:, sl] = _bf(_dot(ds, kb, 1, 0))
            dkv_ref[:, sl] += _dot(ds, qb, 0, 0)
            dkv_ref[:, slv] += _dot(_bf(pr), dob, 0, 0)

    return pl.pallas_call(
        body, name="xattn_bwd", grid=(t // tq,),
        in_specs=[pl.BlockSpec((tq, d), lambda i: (i, 0)), _full((nm, 2 * d)), pl.BlockSpec((tq, d), lambda i: (i, 0))],
        out_specs=[pl.BlockSpec((tq, d), lambda i: (i, 0)), _full((nm, 2 * d))],
        out_shape=[jax.ShapeDtypeStruct((t, d), BF16), jax.ShapeDtypeStruct((nm, 2 * d), F32)],
        compiler_params=_params(),
    )(q, kv, do)


CONV_TN = 256


def _shift_rows(u, row, t, delta):
    if delta < 0:
        return jnp.where(row == 0, 0.0, pltpu.roll(u, 1, 0))
    return jnp.where(row == t - 1, 0.0, pltpu.roll(u, t - 1, 0))


def _conv_gate_fwd(u, conv_w, conv_b):
    t = u.shape[0]
    nt = D_FF // CONV_TN

    def body(ug_ref, uv_ref, wg_ref, wv_ref, bg_ref, bv_ref, a_ref):
        row = lax.broadcasted_iota(jnp.int32, (t, CONV_TN), 0)

        def conv(u_ref, w_ref, b_ref):
            uu = u_ref[...]
            return (b_ref[...] + _shift_rows(uu, row, t, -1) * w_ref[0:1, :] + uu * w_ref[1:2, :]
                    + _shift_rows(uu, row, t, 1) * w_ref[2:3, :])

        gate = conv(ug_ref, wg_ref, bg_ref)
        a_ref[...] = _bf(gate * _sigmoid(gate) * conv(uv_ref, wv_ref, bv_ref))

    col = lambda rows, off: pl.BlockSpec((rows, CONV_TN), lambda j: (0, j + off))
    return pl.pallas_call(
        body, name="conv_gate_fwd", grid=(nt,),
        in_specs=[col(t, 0), col(t, nt), col(3, 0), col(3, nt), col(1, 0), col(1, nt)],
        out_specs=col(t, 0),
        out_shape=jax.ShapeDtypeStruct((t, D_FF), BF16),
        compiler_params=_params(),
    )(u, u, conv_w, conv_w, conv_b, conv_b)


def _conv_gate_bwd(u, conv_w, conv_b, da, comm=None):
    t = u.shape[0]
    nt = D_FF // CONV_TN

    def body(ug_ref, uv_ref, wg_ref, wv_ref, bg_ref, bv_ref, da_ref, dug_ref, dwg_ref, dbg_ref, duv_ref, dwv_ref, dbv_ref):
        row = lax.broadcasted_iota(jnp.int32, (t, CONV_TN), 0)
        ug, uv = ug_ref[...], uv_ref[...]
        ug_m, ug_p = _shift_rows(ug, row, t, -1), _shift_rows(ug, row, t, 1)
        uv_m, uv_p = _shift_rows(uv, row, t, -1), _shift_rows(uv, row, t, 1)
        gate = bg_ref[...] + ug_m * wg_ref[0:1, :] + ug * wg_ref[1:2, :] + ug_p * wg_ref[2:3, :]
        val = bv_ref[...] + uv_m * wv_ref[0:1, :] + uv * wv_ref[1:2, :] + uv_p * wv_ref[2:3, :]
        sg = _sigmoid(gate)
        da_ = da_ref[...].astype(F32)

        def emit(dacc, um, uu, up, w_ref, du_ref, dw_ref, db_ref):
            du_ref[...] = _bf(_shift_rows(dacc, row, t, 1) * w_ref[0:1, :] + dacc * w_ref[1:2, :]
                              + _shift_rows(dacc, row, t, -1) * w_ref[2:3, :])
            dw_ref[0:1, :] = jnp.sum(dacc * um, axis=0, keepdims=True)
            dw_ref[1:2, :] = jnp.sum(dacc * uu, axis=0, keepdims=True)
            dw_ref[2:3, :] = jnp.sum(dacc * up, axis=0, keepdims=True)
            db_ref[...] = jnp.sum(dacc, axis=0, keepdims=True)

        emit(da_ * val * (sg * (1.0 + gate * (1.0 - sg))), ug_m, ug, ug_p, wg_ref, dug_ref, dwg_ref, dbg_ref)
        emit(da_ * gate * sg, uv_m, uv, uv_p, wv_ref, duv_ref, dwv_ref, dbv_ref)

    col = lambda rows, off: pl.BlockSpec((rows, CONV_TN), lambda j: (0, j + off))
    half_shapes = [jax.ShapeDtypeStruct((t, D_FF), BF16), jax.ShapeDtypeStruct((3, D_FF), F32),
                   jax.ShapeDtypeStruct((1, D_FF), F32)]
    outs, got = _call(
        body, (u, u, conv_w, conv_w, conv_b, conv_b, da), name="conv_gate_bwd", grid=(nt,),
        in_specs=[col(t, 0), col(t, nt), col(3, 0), col(3, nt), col(1, 0), col(1, nt), col(t, 0)],
        out_specs=[col(t, 0), col(3, 0), col(1, 0)] * 2, out_shape=half_shapes * 2, comm=comm)
    return outs[:3], outs[3:], got


def _row_tile(r, cap):
    best = r
    for cand in range(16, cap + 1, 16):
        if r % cand == 0:
            best = cand
    return best


def _sum_parts(own, got, me, name, tr=256):
    _, r, c = got.shape
    tr = _row_tile(r, tr)

    def body(me_ref, own_ref, got_ref, o_ref):
        mine = own_ref[...].astype(F32)
        acc = None
        for i in range(N_DEV):
            term = jnp.where(me_ref[0] == i, mine, got_ref[i].astype(F32))
            acc = term if acc is None else acc + term
        o_ref[...] = acc

    return pl.pallas_call(
        body, name=name,
        grid_spec=pltpu.PrefetchScalarGridSpec(
            num_scalar_prefetch=1, grid=(r // tr,),
            in_specs=[pl.BlockSpec((None, tr, c), lambda i, me_ref: (me_ref[0], i, 0)),
                      pl.BlockSpec((N_DEV, tr, c), lambda i, me_ref: (0, i, 0))],
            out_specs=pl.BlockSpec((tr, c), lambda i, me_ref: (i, 0))),
        out_shape=jax.ShapeDtypeStruct((r, c), F32),
        compiler_params=_params(),
    )(me, own, got)


def _adamw(w, g, m, v, name, tr=256):
    r, c = w.shape
    tr = _row_tile(r, tr)

    def body(w_ref, g_ref, m_ref, v_ref, d_ref, nm_ref, nv_ref):
        g_ = g_ref[...]
        m_ = ADAM_B1 * m_ref[...] + (1.0 - ADAM_B1) * g_
        v_ = ADAM_B2 * v_ref[...] + (1.0 - ADAM_B2) * (g_ * g_)
        m_hat = m_ / (1.0 - ADAM_B1 ** ADAM_STEP)
        v_hat = v_ / (1.0 - ADAM_B2 ** ADAM_STEP)
        d_ref[...] = -ADAM_LR * (m_hat / (jnp.sqrt(v_hat) + ADAM_EPS) + ADAM_WD * w_ref[...])
        nm_ref[...] = m_
        nv_ref[...] = v_

    blk = pl.BlockSpec((tr, c), lambda i: (i, 0))
    return pl.pallas_call(
        body, name=name, grid=(r // tr,),
        in_specs=[blk] * 4, out_specs=[blk] * 3,
        out_shape=[jax.ShapeDtypeStruct((r, c), F32)] * 3,
        compiler_params=_params(),
    )(w, g, m, v)


def _mesh_pos():
    return lax.axis_index("x"), lax.axis_index("y"), lax.axis_index("c")


def _dev_index(px, py, pc):
    return 4 * px + 2 * py + pc


class _Gather:
    def __init__(self, arrs):
        self.arrs = list(arrs)
        n = len(self.arrs)
        self.out_shape = [jax.ShapeDtypeStruct((N_DEV,) + a.shape, a.dtype) for a in self.arrs]
        self.sems = [pltpu.SemaphoreType.DMA((7, n)), pltpu.SemaphoreType.DMA((7, n)), pltpu.SemaphoreType.DMA((n,))]

    def _ctx(self, ins, outs, sems):
        send_sems, recv_sems, local_sems = sems
        x, y, c = _mesh_pos()
        chips = [(1 - x, y), (x, 1 - y), (1 - x, 1 - y)]

        def copy(k, a, block, to, src=None):
            dst = outs[a].at[_dev_index(*block)]
            return pltpu.make_async_remote_copy(
                src_ref=dst if src is None else src, dst_ref=dst, send_sem=send_sems.at[k, a], recv_sem=recv_sems.at[k, a],
                device_id=to, device_id_type=MESH_T)

        n = len(ins)
        me, sibling = (x, y, c), (x, y, 1 - c)
        mine = [pltpu.make_async_copy(ins[a], outs[a].at[_dev_index(*me)], local_sems.at[a]) for a in range(n)]
        first = [copy(0, a, me, sibling, src=ins[a]) for a in range(n)]
        first += [copy(1 + j, a, me, (*chip, c), src=ins[a]) for j, chip in enumerate(chips) for a in range(n)]
        passed = [copy(4 + j, a, (*chip, c), sibling) for j, chip in enumerate(chips) for a in range(n)]
        return n, c, me, sibling, chips, copy, mine, first, passed

    def start(self, ins, outs, sems):
        _, _, _, _, _, _, mine, first, _ = self._ctx(ins, outs, sems)
        for cp in mine + first:
            cp.start()

    def forward(self, ins, outs, sems):
        n, c, me, _, chips, copy, _, _, passed = self._ctx(ins, outs, sems)
        for j, chip in enumerate(chips):
            for a in range(n):
                copy(1 + j, a, (*chip, c), me).wait_recv()
                passed[j * n + a].start()

    def finish(self, ins, outs, sems):
        n, c, me, sibling, chips, copy, mine, first, passed = self._ctx(ins, outs, sems)
        for a in range(n):
            copy(0, a, sibling, me).wait_recv()
        for j, chip in enumerate(chips):
            for a in range(n):
                copy(4 + j, a, (*chip, 1 - c), me).wait_recv()
        for cp in first + passed:
            cp.wait_send()
        for cp in mine:
            cp.wait()


class _Exchange:
    def __init__(self, arrs):
        self.arrs = list(arrs)
        n = len(self.arrs)
        self.out_shape = [jax.ShapeDtypeStruct(a.shape, a.dtype) for a in self.arrs]
        self.sems = [pltpu.SemaphoreType.DMA((7, n)), pltpu.SemaphoreType.DMA((7, n)), pltpu.SemaphoreType.DMA((n,))]

    def _copies(self, ins, outs, sems):
        send_sems, recv_sems, local_sems = sems
        x, y, c = _mesh_pos()
        me_i = _dev_index(x, y, c)
        n = len(ins)
        mine = [pltpu.make_async_copy(ins[a].at[me_i], outs[a].at[me_i], local_sems.at[a]) for a in range(n)]
        copies = []
        for k in range(1, N_DEV):
            px = 1 - x if k & 4 else x
            py = 1 - y if k & 2 else y
            pc = 1 - c if k & 1 else c
            peer_i = _dev_index(px, py, pc)
            for a in range(n):
                copies.append(pltpu.make_async_remote_copy(
                    src_ref=ins[a].at[peer_i], dst_ref=outs[a].at[me_i], send_sem=send_sems.at[k - 1, a],
                    recv_sem=recv_sems.at[k - 1, a], device_id=(px, py, pc), device_id_type=MESH_T))
        return mine, copies

    def start(self, ins, outs, sems):
        mine, copies = self._copies(ins, outs, sems)
        for cp in mine + copies:
            cp.start()

    forward = None

    def finish(self, ins, outs, sems):
        mine, copies = self._copies(ins, outs, sems)
        for cp in copies:
            cp.wait_recv()
        for cp in copies:
            cp.wait_send()
        for cp in mine:
            cp.wait()


def _comm_alone(comm, name):
    n = len(comm.arrs)

    def body(*refs):
        ins, outs, sems = refs[:n], refs[n:2 * n], refs[2 * n:]
        comm.start(ins, outs, sems)
        if comm.forward is not None:
            comm.forward(ins, outs, sems)
        comm.finish(ins, outs, sems)

    any_spec = pl.BlockSpec(memory_space=pl.ANY)
    return pl.pallas_call(body, name=name, in_specs=[any_spec] * n, out_specs=[any_spec] * n, out_shape=comm.out_shape,
                          scratch_shapes=comm.sems)(*comm.arrs)


def _peers(x, y, c):
    out = []
    for k in range(1, N_DEV):
        pos = (1 - x if k & 4 else x, 1 - y if k & 2 else y, 1 - c if k & 1 else c)
        out.append((k, pos, _dev_index(*pos)))
    return out


def _exchange_begin(arrs, tag):
    n = len(arrs)
    lands = [lax.empty(a.shape, a.dtype) for a in arrs]

    def start_body(*refs):
        ins, land = refs[:n], refs[n:2 * n]
        send_sems, recv_sems, token = refs[2 * n], refs[2 * n + 1], refs[-1]
        x, y, c = _mesh_pos()
        me_i = _dev_index(x, y, c)
        for k, pos, peer_i in _peers(x, y, c):
            for a in range(n):
                pltpu.make_async_remote_copy(
                    src_ref=ins[a].at[peer_i], dst_ref=land[a].at[me_i], send_sem=send_sems.at[(k - 1) * n + a],
                    recv_sem=recv_sems.at[(k - 1) * n + a], device_id=pos, device_id_type=MESH_T).start()
        token[...] = jnp.zeros_like(token)

    hbm = pl.BlockSpec(memory_space=pltpu.HBM)
    sem = pl.BlockSpec(memory_space=pltpu.SEMAPHORE)
    thru = [pltpu.HBM(a.shape, a.dtype) for a in arrs]
    outs = pl.pallas_call(
        start_body, name="exchange_start_" + tag,
        out_shape=[pltpu.SemaphoreType.DMA((7 * n,)), pltpu.SemaphoreType.DMA((7 * n,))] + thru + thru
        + [jax.ShapeDtypeStruct((8, LANE), F32)],
        in_specs=[hbm] * (2 * n), out_specs=[sem, sem] + [hbm] * (2 * n) + [pl.BlockSpec(memory_space=pltpu.VMEM)],
        input_output_aliases={i: 2 + i for i in range(2 * n)},
        compiler_params=pltpu.CompilerParams(has_side_effects=pltpu.SideEffectType.DATAFLOW_SIDE_EFFECTING),
    )(*[pltpu.with_memory_space_constraint(a, pltpu.HBM) for a in list(arrs) + list(lands)])
    return (tag, n, outs[0], outs[1], outs[2:2 + n], outs[2 + n:2 + 2 * n]), outs[-1][0, 0]


def _exchange_end(handle, after):
    tag, n, send_sems, recv_sems, srcs, lands = handle

    def body(*refs):
        ins, land = refs[:n], refs[n:2 * n]
        send_sems_, recv_sems_ = refs[2 * n], refs[2 * n + 1]
        x, y, c = _mesh_pos()
        me_i = _dev_index(x, y, c)
        for k, pos, peer_i in _peers(x, y, c):
            for a in range(n):
                cp = pltpu.make_async_remote_copy(
                    src_ref=ins[a].at[peer_i], dst_ref=land[a].at[me_i], send_sem=send_sems_.at[(k - 1) * n + a],
                    recv_sem=recv_sems_.at[(k - 1) * n + a], device_id=pos, device_id_type=MESH_T)
                cp.wait_send()
                cp.wait_recv()

    hbm = pl.BlockSpec(memory_space=pltpu.HBM)
    sem = pl.BlockSpec(memory_space=pltpu.SEMAPHORE)
    outs = pl.pallas_call(
        body, name="exchange_end_" + tag, out_shape=[pltpu.HBM(a.shape, a.dtype) for a in list(srcs) + list(lands)],
        in_specs=[hbm] * (2 * n) + [sem, sem, pl.BlockSpec(memory_space=pl.ANY)], out_specs=[hbm] * (2 * n),
        input_output_aliases={i: i for i in range(2 * n)},
        compiler_params=pltpu.CompilerParams(has_side_effects=pltpu.SideEffectType.DATAFLOW_SIDE_EFFECTING),
    )(*srcs, *lands, send_sems, recv_sems, after)
    return list(zip(outs[:n], outs[n:]))


def _gather_begin(arrs, tag):
    n = len(arrs)
    me = _dev_index(*_mesh_pos())
    lands = [lax.dynamic_update_slice(lax.empty((N_DEV,) + a.shape, a.dtype), a[None], (me,) + (0,) * a.ndim) for a in arrs]

    def start_body(*refs):
        ins, land = refs[:n], refs[n:2 * n]
        send_sems, recv_sems, token = refs[2 * n], refs[2 * n + 1], refs[-1]
        x, y, c = _mesh_pos()
        me_i = _dev_index(x, y, c)
        for a in range(n):
            for k, pos, _ in _peers(x, y, c):
                pltpu.make_async_remote_copy(
                    src_ref=ins[a], dst_ref=land[a].at[me_i], send_sem=send_sems.at[(k - 1) * n + a],
                    recv_sem=recv_sems.at[(k - 1) * n + a], device_id=pos, device_id_type=MESH_T).start()
        token[...] = jnp.zeros_like(token)

    hbm = pl.BlockSpec(memory_space=pltpu.HBM)
    sem = pl.BlockSpec(memory_space=pltpu.SEMAPHORE)
    thru = [pltpu.HBM(a.shape, a.dtype) for a in list(arrs) + lands]
    outs = pl.pallas_call(
        start_body, name="gather_start_" + tag,
        out_shape=[pltpu.SemaphoreType.DMA((7 * n,)), pltpu.SemaphoreType.DMA((7 * n,))] + thru
        + [jax.ShapeDtypeStruct((8, LANE), F32)],
        in_specs=[hbm] * (2 * n), out_specs=[sem, sem] + [hbm] * (2 * n) + [pl.BlockSpec(memory_space=pltpu.VMEM)],
        input_output_aliases={i: 2 + i for i in range(2 * n)},
        compiler_params=pltpu.CompilerParams(has_side_effects=pltpu.SideEffectType.DATAFLOW_SIDE_EFFECTING),
    )(*[pltpu.with_memory_space_constraint(a, pltpu.HBM) for a in list(arrs) + lands])
    return (tag, n, outs[0], outs[1], outs[2:2 + n], outs[2 + n:2 + 2 * n]), outs[-1][0, 0]


def _gather_end(handle, which, after):
    tag, n, send_sems, recv_sems, srcs, lands = handle
    m = len(which)

    def body(*refs):
        ins, land = refs[:m], refs[m:2 * m]
        send_sems_, recv_sems_ = refs[2 * m], refs[2 * m + 1]
        x, y, c = _mesh_pos()
        me_i = _dev_index(x, y, c)
        for j, a in enumerate(which):
            for k, pos, _ in _peers(x, y, c):
                cp = pltpu.make_async_remote_copy(
                    src_ref=ins[j], dst_ref=land[j].at[me_i], send_sem=send_sems_.at[(k - 1) * n + a],
                    recv_sem=recv_sems_.at[(k - 1) * n + a], device_id=pos, device_id_type=MESH_T)
                cp.wait_send()
                cp.wait_recv()

    hbm = pl.BlockSpec(memory_space=pltpu.HBM)
    sem = pl.BlockSpec(memory_space=pltpu.SEMAPHORE)
    ops = [srcs[a] for a in which] + [lands[a] for a in which]
    outs = pl.pallas_call(
        body, name="gather_end_%s_%s" % (tag, "_".join(str(a) for a in which)),
        out_shape=[pltpu.HBM(a.shape, a.dtype) for a in ops],
        in_specs=[hbm] * (2 * m) + [sem, sem, pl.BlockSpec(memory_space=pl.ANY)], out_specs=[hbm] * (2 * m),
        input_output_aliases={i: i for i in range(2 * m)},
        compiler_params=pltpu.CompilerParams(has_side_effects=pltpu.SideEffectType.DATAFLOW_SIDE_EFFECTING),
    )(*ops, send_sems, recv_sems, after)
    return list(outs[m:])


def _call(body, operands, *, name, grid, in_specs, out_specs, out_shape, scratch_shapes=(), comm=None):
    if comm is None:
        outs = pl.pallas_call(body, name=name, grid=grid, in_specs=in_specs, out_specs=out_specs, out_shape=out_shape,
                              scratch_shapes=list(scratch_shapes), compiler_params=_params())(*operands)
        return list(outs), []
    ni, no, ns = len(in_specs), len(out_specs), len(scratch_shapes)
    ci, co = len(comm.arrs), len(comm.out_shape)
    total = 1
    for gdim in grid:
        total *= gdim

    def hosted(*refs):
        o0 = ni + ci
        s0 = o0 + no + co
        ins, cins = refs[:ni], refs[ni:o0]
        outs, couts = refs[o0:o0 + no], refs[o0 + no:s0]
        scr, csems = refs[s0:s0 + ns], refs[s0 + ns:]
        step = 0
        for axis, gdim in enumerate(grid):
            step = step * gdim + pl.program_id(axis)

        @pl.when(step == 0)
        def _():
            comm.start(cins, couts, csems)

        if comm.forward is not None:
            @pl.when(step == (3 * total) // 4)
            def _():
                comm.forward(cins, couts, csems)

        body(*ins, *outs, *scr)

        @pl.when(step == total - 1)
        def _():
            comm.finish(cins, couts, csems)

    any_spec = pl.BlockSpec(memory_space=pl.ANY)
    res = pl.pallas_call(
        hosted, name=name, grid=grid, in_specs=list(in_specs) + [any_spec] * ci, out_specs=list(out_specs) + [any_spec] * co,
        out_shape=list(out_shape) + comm.out_shape, scratch_shapes=list(scratch_shapes) + comm.sems,
        compiler_params=_params())(*operands, *comm.arrs)
    return list(res[:no]), list(res[no:])


def _rows(a):
    return a.reshape(-1, a.shape[-1])


def _slots(a):
    return a.reshape(N_DEV, -1, a.shape[-1])


def _local_step(x, mem, tgt, w, wire):
    tables = _rope_tables(x.shape[0])
    bd = _head_sum_matrix()
    two = lambda g: jnp.tile(g, (1, 2))
    gq2, gk2 = two(w["q_norm_g"]), two(w["k_norm_g"])

    w_in_t, hg_lb, conv_w = _comm_alone(_Gather([wire["w_in"], w["hg_lb"].reshape(4, -1), w["conv_w"][0]]), "gather_w_in")
    w_in_t = _rows(w_in_t)
    hg_lb = jnp.transpose(hg_lb.reshape(N_DEV, 2, 2, -1), (1, 2, 0, 3)).reshape(2, 2, HG_DIM)
    conv_w = jnp.transpose(conv_w, (1, 0, 2)).reshape(3, 2 * D_FF)
    lb_a0, lb_a1 = hg_lb[:, 0, :], hg_lb[:, 1, :]

    order = ("w_out", "w_xq", "w_xkv", "w_xo", "w_up", "w_down")
    w_in_t, *later = lax.optimization_barrier((w_in_t, *[wire[n] for n in order]))
    fetch, zero = _gather_begin(later, "w")
    take = lambda n, after: _rows(_gather_end(fetch, [order.index(n)], after)[0])

    h1, p = _norm_mm(x, w["pre_mix_g"] + zero, w_in_t, True, N_IN, "in_proj")
    qr, kr = _qk_prep(p, gq2, gk2, tables, bd)
    (att, att32, lse), _ = _attn_fwd(qr, kr, p)
    lb = _lower_bounds(lb_a0, lb_a1)
    (of, s_f), _ = _hgrn_fwd(p, lb[0:1], False)
    (ob, s_b), _ = _hgrn_fwd(p, lb[1:2], True)
    w_out = take("w_out", ob)
    (cat, mixed, x1), _ = _mix_out(att, of, ob, p, w["hg_out_norm_g"], w_out, w["post_mix_g"], x)

    w_xq = take("w_xq", x1)
    h2, q2 = _norm_mm(x1, w["pre_x_g"], w_xq, False, 1024, "xq_proj")
    w_xkv_t = take("w_xkv", q2)
    mn, kv = _norm_mm(mem, w["mem_norm_g"], w_xkv_t, True, 2 * D_MODEL, "xkv_proj")
    o2 = _xattn_fwd(q2, kv)
    w_xo = take("w_xo", o2)
    y2, x2 = _mm_postnorm_res(o2, w_xo, w["post_x_g"], x1, "xo_proj")

    w_up_t = take("w_up", x2)
    h3, u = _norm_mm(x2, w["pre_ffn_g"], w_up_t, True, 2 * D_FF, "up_proj", tm=256)
    a = _conv_gate_fwd(u, conv_w, w["conv_b"])
    w_down = take("w_down", a)
    y3, dx3, loss = _mm_postnorm_res_loss(a, w_down, w["post_ffn_g"], x2, tgt, "down_proj")

    g, pending = {}, {}
    dy3, da, g["post_ffn_g"] = _postnorm_bwd_mm(dx3, y3, w["post_ffn_g"], w_down, "down_bwd", BF16)
    pending["down"], zero = _exchange_begin([_slots(_dw(a, dy3, "dw_down"))], "down")
    (du_g, dcw_g, dcb_g), (du_v, dcw_v, dcb_v), _ = _conv_gate_bwd(u, conv_w, w["conv_b"] + zero, da)
    g["conv_w"] = jnp.concatenate([dcw_g, dcw_v], axis=1)
    g["conv_b"] = jnp.concatenate([dcb_g, dcb_v], axis=1)
    pending["up"], zero = _exchange_begin([_slots(_dw([du_g, du_v], h3, "dw_up"))], "up")
    (dx2, g["pre_ffn_g"]), _ = _mm_prenorm_bwd([du_g, du_v], w_up_t, True, x2, w["pre_ffn_g"] + zero, dx3, "up_bwd", tm=256)

    dy2, do2, g["post_x_g"] = _postnorm_bwd_mm(dx2, y2, w["post_x_g"], w_xo, "xo_bwd", BF16)
    dw_xo = _dw(o2, dy2, "dw_xo")
    dq2, dkv = _xattn_bwd(q2, kv, do2)
    dw_xq = _dw(h2, dq2, "dw_xq")
    dkvb = _bf(dkv)
    dw_xkv = _dw(dkvb, mn, "dw_xkv")
    pending["x"], zero = _exchange_begin([_slots(dw_xo), _slots(dw_xq), _slots(dw_xkv)], "x")
    (_, g["mem_norm_g"]), _ = _mm_prenorm_bwd(dkvb, w_xkv_t, True, mem, w["mem_norm_g"], jnp.zeros_like(mem), "xkv_bwd")
    (dx1, g["pre_x_g"]), _ = _mm_prenorm_bwd(dq2, w_xq, False, x1, w["pre_x_g"] + zero, dx2, "xq_bwd")

    dmixed, dcat, g["post_mix_g"] = _postnorm_bwd_mm(dx1, mixed, w["post_mix_g"], w_out, "out_bwd")
    pending["out"], zero = _exchange_begin([_slots(_dw(cat, dmixed, "dw_out"))], "out")
    do, dhg, g["hg_out_norm_g"] = _rec_bwd(dcat, of, ob, p, w["hg_out_norm_g"] + zero)
    (dhq_f, dz_f, dhi_f, dlb_f), _ = _hgrn_bwd(p, lb[0:1], do, s_f, False)
    (dhq_b, dz_b, dhi_b, dlb_b), _ = _hgrn_bwd(p, lb[1:2], do, s_b, True)
    d_a0, d_a1 = _lower_bounds_bwd(lb_a0, lb_a1, jnp.concatenate([dlb_f, dlb_b], axis=0))
    g["hg_lb"] = jnp.stack([d_a0, d_a1], axis=1)
    (dqr, dkr, dv), _ = _attn_bwd(qr, kr, p, dcat, att32, lse)
    dp_qk, dgq, dgk = _qk_prep_bwd(p, dqr, dkr, gq2, gk2, tables, bd)
    g["q_norm_g"], g["k_norm_g"] = dgq, dgk
    dp = _assemble_dp(dp_qk, dv, dhq_f, dhq_b, dz_f, dz_b, dhi_f, dhi_b, dhg)
    pending["in"], zero = _exchange_begin([_slots(_dw(dp, h1, "dw_in"))], "in")
    (dx, g["pre_mix_g"]), _ = _mm_prenorm_bwd(dp, w_in_t, True, x, w["pre_mix_g"] + zero, dx1, "in_bwd")
    return loss, dx, g, pending


_COL_SHARDED = ("w_in", "w_xkv", "w_up")
_ROW_SHARDED = ("w_out", "w_xq", "w_xo", "w_down")
_REPLICATED = ("pre_mix_g", "q_norm_g", "k_norm_g", "hg_out_norm_g", "post_mix_g", "pre_x_g", "mem_norm_g", "post_x_g",
               "pre_ffn_g", "conv_b", "post_ffn_g")
_WEIGHTS = ("pre_mix_g", "w_in", "q_norm_g", "k_norm_g", "hg_lb", "hg_out_norm_g", "w_out", "post_mix_g", "pre_x_g",
            "mem_norm_g", "w_xq", "w_xkv", "w_xo", "post_x_g", "pre_ffn_g", "w_up", "conv_w", "conv_b", "w_down",
            "post_ffn_g")
_ADAM_TRANSPOSED = ("w_in", "w_up")
PACK_W = 1024


def _small_plan(shapes):
    plan, r = [], 0
    for vi, (rows, cols) in enumerate(shapes):
        for i in range(rows):
            for c0 in range(0, cols, PACK_W):
                plan.append((vi, i, c0, min(PACK_W, cols - c0), r))
                r += 1
    return plan, -(-r // 8) * 8


def _pack_small(vals):
    plan, nrows = _small_plan([val.shape for val in vals])

    def body(*refs):
        ins, out = refs[:-1], refs[-1]
        out[...] = jnp.zeros_like(out)
        for vi, i, c0, width, r in plan:
            out[r:r + 1, 0:width] = ins[vi][i:i + 1, c0:c0 + width]

    return pl.pallas_call(body, name="pack_small", out_shape=jax.ShapeDtypeStruct((nrows, PACK_W), F32))(*vals)


def _sum_unpack_small(packs, shapes):
    plan, _ = _small_plan(shapes)

    def body(*refs):
        p_ref, outs = refs[0], refs[1:]
        acc = p_ref[0]
        for i in range(1, N_DEV):
            acc = acc + p_ref[i]
        for vi, i, c0, width, r in plan:
            outs[vi][i:i + 1, c0:c0 + width] = acc[r:r + 1, 0:width]

    return pl.pallas_call(body, name="sum_unpack_small", out_shape=[jax.ShapeDtypeStruct(s, F32) for s in shapes])(packs)


def _adamw_many(ws, gs, ms, vs):
    n = len(ws)

    def body(*refs):
        w_refs, g_refs, m_refs, v_refs = (refs[k * n:(k + 1) * n] for k in range(4))
        d_refs, nm_refs, nv_refs = (refs[(4 + k) * n:(5 + k) * n] for k in range(3))
        for k in range(n):
            g_ = g_refs[k][...]
            m_ = ADAM_B1 * m_refs[k][...] + (1.0 - ADAM_B1) * g_
            v_ = ADAM_B2 * v_refs[k][...] + (1.0 - ADAM_B2) * (g_ * g_)
            m_hat = m_ / (1.0 - ADAM_B1 ** ADAM_STEP)
            v_hat = v_ / (1.0 - ADAM_B2 ** ADAM_STEP)
            d_refs[k][...] = -ADAM_LR * (m_hat / (jnp.sqrt(v_hat) + ADAM_EPS) + ADAM_WD * w_refs[k][...])
            nm_refs[k][...] = m_
            nv_refs[k][...] = v_

    shapes = [jax.ShapeDtypeStruct(a.shape, F32) for a in ws]
    outs = pl.pallas_call(body, name="adamw_small", out_shape=shapes * 3)(*ws, *gs, *ms, *vs)
    return outs[:n], outs[n:2 * n], outs[2 * n:]


def kernel(x, mem, pre_mix_g, w_in, q_norm_g, k_norm_g, hg_lb, hg_out_norm_g, w_out, post_mix_g, pre_x_g, mem_norm_g, w_xq, w_xkv, w_xo, post_x_g, pre_ffn_g, w_up, conv_w, conv_b, w_down, post_ffn_g, loss_target, m_pre_mix_g, m_w_in, m_q_norm_g, m_k_norm_g, m_hg_lb, m_hg_out_norm_g, m_w_out, m_post_mix_g, m_pre_x_g, m_mem_norm_g, m_w_xq, m_w_xkv, m_w_xo, m_post_x_g, m_pre_ffn_g, m_w_up, m_conv_w, m_conv_b, m_w_down, m_post_ffn_g, v_pre_mix_g, v_w_in, v_q_norm_g, v_k_norm_g, v_hg_lb, v_hg_out_norm_g, v_w_out, v_post_mix_g, v_pre_x_g, v_mem_norm_g, v_w_xq, v_w_xkv, v_w_xo, v_post_x_g, v_pre_ffn_g, v_w_up, v_conv_w, v_conv_b, v_w_down, v_post_ffn_g):
    args = dict(locals())
    w = {n: args[n] for n in _WEIGHTS}
    m = {n: args["m_" + n] for n in _WEIGHTS}
    v = {n: args["v_" + n] for n in _WEIGHTS}
    me = _dev_index(*_mesh_pos())

    wire = {n: _bf(w[n][0].T) for n in _COL_SHARDED}
    wire.update({n: _bf(w[n][0]) for n in _ROW_SHARDED})

    loss, grad_x, g, pending = _local_step(x[0], mem[0], loss_target[0], w, wire)

    grads, delta, new_m, new_v = {}, {}, {}, {}

    me_arr = jnp.reshape(me, (1,)).astype(jnp.int32)

    def update(n, parts):
        gsum = _sum_parts(*parts, me_arr, "sum_" + n)
        if n in _ADAM_TRANSPOSED:
            grads[n] = gsum.T[None]
            d_, m_, v_ = _adamw(w[n][0].T, gsum, m[n][0].T, v[n][0].T, "adamw_" + n)
            delta[n], new_m[n], new_v[n] = d_.T[None], m_.T[None], v_.T[None]
        else:
            gsum = gsum.T if n in _COL_SHARDED else gsum
            grads[n] = gsum[None]
            d_, m_, v_ = _adamw(w[n][0], gsum, m[n][0], v[n][0], "adamw_" + n)
            delta[n], new_m[n], new_v[n] = d_[None], m_[None], v_[None]

    after = grad_x
    for tag, names in (("down", ["w_down"]), ("up", ["w_up"]), ("x", ["w_xo", "w_xq", "w_xkv"]), ("out", ["w_out"]),
                       ("in", ["w_in"])):
        for n, parts in zip(names, _exchange_end(pending[tag], after)):
            update(n, parts)
            after = new_v[n]

    small = list(_REPLICATED) + ["hg_lb", "conv_w"]
    vals = [g[n] for n in _REPLICATED] + [g["hg_lb"].reshape(4, HG_DIM), g["conv_w"], jnp.pad(loss, ((0, 0), (0, LANE - 1)))]
    shapes = [val.shape for val in vals]
    (packs,) = _comm_alone(_Gather([_pack_small(vals)]), "gather_small_grads")
    summed = _sum_unpack_small(packs, shapes)
    loss = summed[-1][0, 0]
    for n, s in zip(small, summed[:-1]):
        grads[n] = s
    fold = lambda v2: v2[:, :ATT_HEAD_DIM] + v2[:, ATT_HEAD_DIM:]
    grads["q_norm_g"], grads["k_norm_g"] = fold(grads["q_norm_g"]), fold(grads["k_norm_g"])
    grads["hg_lb"] = lax.dynamic_slice_in_dim(grads["hg_lb"].reshape(2, 2, HG_DIM), me * (HG_DIM // N_DEV),
                                              HG_DIM // N_DEV, axis=2)
    grads["conv_w"] = lax.dynamic_slice_in_dim(grads["conv_w"], me * (2 * D_FF // N_DEV), 2 * D_FF // N_DEV, axis=1)[None]

    flat2 = lambda a: a.reshape(-1, a.shape[-1])
    outs = _adamw_many(*[[flat2(d[n]) for n in small] for d in (w, grads, m, v)])
    for dst, vals in zip((delta, new_m, new_v), outs):
        for n, val in zip(small, vals):
            dst[n] = val.reshape(w[n].shape)

    return (loss, grad_x[None], *[grads[n] for n in _WEIGHTS], *[delta[n] for n in _WEIGHTS],
            *[new_m[n] for n in _WEIGHTS], *[new_v[n] for n in _WEIGHTS])
```

```python
import jax
import jax.numpy as jnp
from jax import lax
from jax.experimental import pallas as pl
from jax.experimental.pallas import tpu as pltpu

F32 = jnp.float32
BF16 = jnp.bfloat16

D_MODEL = 1024
GRID_W = 64
EPS = 1e-6
ATT_HEADS = 8
ATT_HEAD_DIM = 64
ATT_Q_DIM = 512
ATT_KV_DIM = 128
ROPE_THETA = 10000.0
HG_HEADS = 4
HG_DIM = 512
HG_CHUNK = 32
HG_CHUNK_LOG2 = 5
HG_BLOCK_FWD = 256
HG_BLOCK_BWD = 128
N_IN = 3328
X_HEADS = 4
X_HEAD_DIM = 256
D_FF = 2816
N_DEV = 8
LANE = 128
ADAM_LR = 0.001
ADAM_B1 = 0.9
ADAM_B2 = 0.999
ADAM_EPS = 1e-08
ADAM_WD = 0.01
ADAM_STEP = 10
VMEM_LIMIT = 56 * 1024 * 1024

MESH_T = pl.DeviceIdType.MESH


def _params(**kw):
    return pltpu.CompilerParams(vmem_limit_bytes=VMEM_LIMIT, **kw)


def _dot(a, b, ca, cb):
    return lax.dot_general(a, b, (((ca,), (cb,)), ((), ())), preferred_element_type=F32)


def _bf(x):
    return x.astype(BF16)


def _sigmoid(x):
    return 1.0 / (1.0 + jnp.exp(-x))


def _rms_fwd(x, g):
    r = lax.rsqrt(jnp.mean(x * x, axis=-1, keepdims=True) + EPS)
    return x * r * g


def _rms_bwd(dy, x, g):
    r = lax.rsqrt(jnp.mean(x * x, axis=-1, keepdims=True) + EPS)
    xh = x * r
    dg = jnp.sum(dy * xh, axis=0, keepdims=True)
    t = dy * g
    dx = r * (t - xh * jnp.mean(t * xh, axis=-1, keepdims=True))
    return dx, dg


def _full(shape):
    nd = len(shape)
    return pl.BlockSpec(shape, lambda *a: (0,) * nd)


def _norm_mm(x, g, w, trans, tn, name, after=(), tm=512):
    t, d = x.shape
    n = w.shape[0] if trans else w.shape[1]
    tm = min(tm, t)

    def body(x_ref, g_ref, w_ref, h_ref, p_ref):
        h = _bf(_rms_fwd(x_ref[...], g_ref[...]))
        h_ref[...] = h
        p_ref[...] = _dot(h, w_ref[...], 1, 1 if trans else 0)

    w_spec = pl.BlockSpec((tn, d), lambda i, j: (j, 0)) if trans else pl.BlockSpec((d, tn), lambda i, j: (0, j))
    return _call(
        body, (x, g, w), name=name, grid=(t // tm, n // tn),
        in_specs=[pl.BlockSpec((tm, d), lambda i, j: (i, 0)), _full((1, d)), w_spec],
        out_specs=[pl.BlockSpec((tm, d), lambda i, j: (i, 0)), pl.BlockSpec((tm, tn), lambda i, j: (i, j))],
        out_shape=[jax.ShapeDtypeStruct((t, d), BF16), jax.ShapeDtypeStruct((t, n), F32)], after=after)


def _mm_postnorm_res(a, w, g, res, name, tm=256):
    t, k = a.shape
    d = w.shape[1]

    def body(a_ref, w_ref, g_ref, res_ref, y_ref, o_ref):
        y = _dot(a_ref[...], w_ref[...], 1, 0)
        y_ref[...] = y
        o_ref[...] = res_ref[...] + _rms_fwd(y, g_ref[...])

    row = lambda width: pl.BlockSpec((tm, width), lambda i: (i, 0))
    return pl.pallas_call(
        body, name=name, grid=(t // tm,),
        in_specs=[row(k), _full((k, d)), _full((1, d)), row(d)],
        out_specs=[row(d), row(d)],
        out_shape=[jax.ShapeDtypeStruct((t, d), F32)] * 2,
        compiler_params=_params(),
    )(a, w, g, res)


def _mm_postnorm_res_loss(a, w, g, res, tgt, name, tm=256):
    t, k = a.shape
    d = w.shape[1]

    def body(a_ref, w_ref, g_ref, res_ref, tgt_ref, y_ref, dout_ref, loss_ref):
        @pl.when(pl.program_id(0) == 0)
        def _():
            loss_ref[...] = jnp.zeros_like(loss_ref)

        y = _dot(a_ref[...], w_ref[...], 1, 0)
        y_ref[...] = y
        diff = res_ref[...] + _rms_fwd(y, g_ref[...]) - tgt_ref[...]
        dout_ref[...] = diff * (1.0 / d)
        part = jnp.sum(jnp.sum(diff * diff, axis=-1, keepdims=True), axis=0, keepdims=True)
        loss_ref[...] += (0.5 / d) * part

    row = lambda width: pl.BlockSpec((tm, width), lambda i: (i, 0))
    return pl.pallas_call(
        body, name=name, grid=(t // tm,),
        in_specs=[row(k), _full((k, d)), _full((1, d)), row(d), row(d)],
        out_specs=[row(d), row(d), _full((1, 1))],
        out_shape=[jax.ShapeDtypeStruct((t, d), F32)] * 2 + [jax.ShapeDtypeStruct((1, 1), F32)],
        compiler_params=_params(),
    )(a, w, g, res, tgt)


def _postnorm_bwd_mm(dout, y, g, w, name, da_dtype=F32, tm=256):
    t, d = y.shape
    k = w.shape[0]

    def body(dout_ref, y_ref, g_ref, w_ref, dy_ref, da_ref, dg_ref):
        @pl.when(pl.program_id(0) == 0)
        def _():
            dg_ref[...] = jnp.zeros_like(dg_ref)

        dy, dg = _rms_bwd(dout_ref[...], y_ref[...], g_ref[...])
        dg_ref[...] += dg
        dyb = _bf(dy)
        dy_ref[...] = dyb
        da_ref[...] = _dot(dyb, w_ref[...], 1, 1).astype(da_dtype)

    row = lambda width: pl.BlockSpec((tm, width), lambda i: (i, 0))
    return pl.pallas_call(
        body, name=name, grid=(t // tm,),
        in_specs=[row(d), row(d), _full((1, d)), _full((k, d))],
        out_specs=[row(d), row(k), _full((1, d))],
        out_shape=[jax.ShapeDtypeStruct((t, d), BF16), jax.ShapeDtypeStruct((t, k), da_dtype), jax.ShapeDtypeStruct((1, d), F32)],
        compiler_params=_params(),
    )(dout, y, g, w)


def _mm_prenorm_bwd(dp, w, trans, x, g, dres, name, after=(), tm=512):
    dps = list(dp) if isinstance(dp, (list, tuple)) else [dp]
    nparts = len(dps)
    t, n_each = dps[0].shape
    d = x.shape[1]
    tm = min(tm, t)

    def body(*refs):
        dp_refs = refs[:nparts]
        w_ref, x_ref, g_ref, dres_ref, dx_ref, dg_ref = refs[nparts:]

        @pl.when(pl.program_id(0) == 0)
        def _():
            dg_ref[...] = jnp.zeros_like(dg_ref)

        dh = None
        for part, dp_ref in enumerate(dp_refs):
            cols = slice(part * n_each, (part + 1) * n_each)
            term = _dot(dp_ref[...], w_ref[cols, :], 1, 0) if trans else _dot(dp_ref[...], w_ref[:, cols], 1, 1)
            dh = term if dh is None else dh + term
        dx, dg = _rms_bwd(dh, x_ref[...], g_ref[...])
        dg_ref[...] += dg
        dx_ref[...] = dres_ref[...] + dx

    row = lambda width: pl.BlockSpec((tm, width), lambda i: (i, 0))
    return _call(
        body, (*dps, w, x, g, dres), name=name, grid=(t // tm,),
        in_specs=[row(n_each)] * nparts + [_full(w.shape), row(d), _full((1, d)), row(d)],
        out_specs=[row(d), _full((1, d))],
        out_shape=[jax.ShapeDtypeStruct((t, d), F32), jax.ShapeDtypeStruct((1, d), F32)], after=after)


def _dw(a, b, name, bcol=None, tka=256):
    parts = list(a) if isinstance(a, (list, tuple)) else [a]
    nparts = len(parts)
    t, ka_each = parts[0].shape
    bci, nb = (0, b.shape[1]) if bcol is None else bcol
    nt_each = ka_each // tka

    def body(*refs):
        a_refs, b_ref, o_ref = refs[:nparts], refs[nparts], refs[nparts + 1]
        i = pl.program_id(0)
        for part, a_ref in enumerate(a_refs):
            @pl.when((i >= part * nt_each) & (i < (part + 1) * nt_each))
            def _():
                o_ref[...] = _bf(_dot(a_ref[...], b_ref[...], 0, 0))

    a_spec = lambda part: pl.BlockSpec((t, tka), lambda i: (0, jnp.clip(i - part * nt_each, 0, nt_each - 1)))
    return pl.pallas_call(
        body, name=name, grid=(nt_each * nparts,),
        in_specs=[a_spec(part) for part in range(nparts)] + [pl.BlockSpec((t, nb), lambda i: (0, bci))],
        out_specs=pl.BlockSpec((tka, nb), lambda i: (i, 0)),
        out_shape=jax.ShapeDtypeStruct((nparts * ka_each, nb), BF16),
        compiler_params=_params(),
    )(*parts, b)


def _rope_tables(t):
    pos = jnp.arange(t)
    r = (pos // GRID_W).astype(F32)
    c = (pos % GRID_W).astype(F32)
    npair = ATT_HEAD_DIM // 4
    inv = jnp.power(ROPE_THETA, -jnp.arange(npair, dtype=F32) / npair)
    ang = jnp.concatenate([r[:, None] * inv, c[:, None] * inv], axis=-1)
    cos = jnp.repeat(jnp.cos(ang), 2, axis=-1)
    sin = jnp.repeat(jnp.sin(ang), 2, axis=-1)
    even = (jnp.arange(ATT_HEAD_DIM) % 2) == 0
    sa = jnp.where(even, -sin, 0.0)
    sb = jnp.where(even, 0.0, sin)
    two = lambda a: jnp.tile(a, (1, 2))
    return two(cos), two(sa), two(sb)


def _head_sum_matrix():
    a = jnp.arange(LANE) // ATT_HEAD_DIM
    return (a[:, None] == a[None, :]).astype(BF16)


def _head_mean(v, bd):
    hi = _bf(v)
    lo = _bf(v - hi.astype(F32))
    return (_dot(hi, bd, 1, 0) + _dot(lo, bd, 1, 0)) * (1.0 / ATT_HEAD_DIM)


def _qk_prep(p, gq, gk, tables, bd, tm=512):
    t = p.shape[0]
    tm = min(tm, t)
    cc, sa, sb = tables

    def body(p_ref, gq_ref, gk_ref, cc_ref, sa_ref, sb_ref, bd_ref, q_ref, k_ref):
        cc_, sa_, sb_, bd_ = cc_ref[...], sa_ref[...], sb_ref[...], bd_ref[...]
        low = lax.broadcasted_iota(jnp.int32, (tm, LANE), 1) < ATT_HEAD_DIM

        def normrope(xs, g):
            xn = xs * lax.rsqrt(_head_mean(xs * xs, bd_) + EPS) * g
            return xn * cc_ + pltpu.roll(xn, LANE - 1, 1) * sa_ + pltpu.roll(xn, 1, 1) * sb_

        for j in range(4):
            y = normrope(p_ref[:, j * LANE:(j + 1) * LANE], gq_ref[...]) * (ATT_HEAD_DIM ** -0.5)
            yr = pltpu.roll(y, ATT_HEAD_DIM, 1)
            if j // 2 == 0:
                h0, h1 = jnp.where(low, y, 0.0), jnp.where(low, yr, 0.0)
            else:
                h0, h1 = jnp.where(low, 0.0, yr), jnp.where(low, 0.0, y)
            q_ref[:, (2 * j) * LANE:(2 * j + 1) * LANE] = _bf(h0)
            q_ref[:, (2 * j + 1) * LANE:(2 * j + 2) * LANE] = _bf(h1)
        k_ref[...] = _bf(normrope(p_ref[:, ATT_Q_DIM:ATT_Q_DIM + LANE], gk_ref[...]))

    row = lambda width: pl.BlockSpec((tm, width), lambda i: (i, 0))
    return pl.pallas_call(
        body, name="qk_prep", grid=(t // tm,),
        in_specs=[row(ATT_Q_DIM + LANE), _full((1, LANE)), _full((1, LANE)), row(LANE), row(LANE), row(LANE),
                  _full((LANE, LANE))],
        out_specs=[row(ATT_HEADS * LANE), row(LANE)],
        out_shape=[jax.ShapeDtypeStruct((t, ATT_HEADS * LANE), BF16), jax.ShapeDtypeStruct((t, LANE), BF16)],
        compiler_params=_params(),
    )(p, gq, gk, cc, sa, sb, bd)


def _qk_prep_bwd(p, dq, dk, gq, gk, tables, bd, tm=512):
    t = p.shape[0]
    tm = min(tm, t)
    cc, sa, sb = tables

    def body(p_ref, dq_ref, dk_ref, gq_ref, gk_ref, cc_ref, sa_ref, sb_ref, bd_ref, dp_ref, dgq_ref, dgk_ref):
        @pl.when(pl.program_id(0) == 0)
        def _():
            dgq_ref[...] = jnp.zeros_like(dgq_ref)
            dgk_ref[...] = jnp.zeros_like(dgk_ref)

        cc_, sa_, sb_, bd_ = cc_ref[...], sa_ref[...], sb_ref[...], bd_ref[...]
        low = lax.broadcasted_iota(jnp.int32, (tm, LANE), 1) < ATT_HEAD_DIM

        def bwd(xs, g, dy):
            r = lax.rsqrt(_head_mean(xs * xs, bd_) + EPS)
            xh = xs * r
            dxn = dy * cc_ + pltpu.roll(dy * sa_, 1, 1) + pltpu.roll(dy * sb_, LANE - 1, 1)
            dg = jnp.sum(dxn * xh, axis=0, keepdims=True)
            tt = dxn * g
            return r * (tt - xh * _head_mean(tt * xh, bd_)), dg

        dgq = jnp.zeros((1, LANE), F32)
        for j in range(4):
            d0 = dq_ref[:, (2 * j) * LANE:(2 * j + 1) * LANE]
            d1 = dq_ref[:, (2 * j + 1) * LANE:(2 * j + 2) * LANE]
            if j // 2 == 0:
                dy = jnp.where(low, d0, pltpu.roll(d1, ATT_HEAD_DIM, 1))
            else:
                dy = jnp.where(low, pltpu.roll(d0, ATT_HEAD_DIM, 1), d1)
            dx, dg = bwd(p_ref[:, j * LANE:(j + 1) * LANE], gq_ref[...], dy * (ATT_HEAD_DIM ** -0.5))
            dp_ref[:, j * LANE:(j + 1) * LANE] = _bf(dx)
            dgq = dgq + dg
        dgq_ref[...] += dgq
        dx, dg = bwd(p_ref[:, ATT_Q_DIM:ATT_Q_DIM + LANE], gk_ref[...], dk_ref[...])
        dp_ref[:, ATT_Q_DIM:ATT_Q_DIM + LANE] = _bf(dx)
        dgk_ref[...] += dg

    row = lambda width: pl.BlockSpec((tm, width), lambda i: (i, 0))
    return pl.pallas_call(
        body, name="qk_prep_bwd", grid=(t // tm,),
        in_specs=[row(ATT_Q_DIM + LANE), row(ATT_HEADS * LANE), row(LANE), _full((1, LANE)), _full((1, LANE)),
                  row(LANE), row(LANE), row(LANE), _full((LANE, LANE))],
        out_specs=[row(ATT_Q_DIM + LANE), _full((1, LANE)), _full((1, LANE))],
        out_shape=[jax.ShapeDtypeStruct((t, ATT_Q_DIM + LANE), BF16), jax.ShapeDtypeStruct((1, LANE), F32),
                   jax.ShapeDtypeStruct((1, LANE), F32)],
        compiler_params=_params(),
    )(p, dq, dk, gq, gk, cc, sa, sb, bd)


def _attn_fwd(q, k, p, tq=256):
    t = k.shape[0]
    tq = min(tq, t)
    v_blk = (ATT_Q_DIM + ATT_KV_DIM) // LANE

    def body(q_ref, k_ref, v_ref, o_ref, o32_ref, lse_ref):
        k_ = k_ref[...]
        v = v_ref[...]
        lane_k = lax.broadcasted_iota(jnp.int32, (t, LANE), 1)
        lane_q = lax.broadcasted_iota(jnp.int32, (tq, LANE), 1)
        lowk, lowq = lane_k < ATT_HEAD_DIM, lane_q < ATT_HEAD_DIM
        ones_lane = (ATT_HEAD_DIM, 0)
        vm = (_bf(jnp.where(lowk, v, jnp.where(lane_k == ones_lane[0], 1.0, 0.0))),
              _bf(jnp.where(lowk, jnp.where(lane_k == ones_lane[1], 1.0, 0.0), v)))
        for j in range(4):
            kvh = j // 2
            acc = None
            for sub in range(2):
                h = 2 * j + sub
                s = _dot(q_ref[:, h * LANE:(h + 1) * LANE], k_, 1, 1)
                mx = jnp.max(s, axis=-1, keepdims=True)
                ov = _dot(jnp.exp(_bf(s - mx)), vm[kvh], 1, 0)
                l = jnp.sum(jnp.where(lane_q == ones_lane[kvh], ov, 0.0), axis=-1, keepdims=True)
                lse_ref[h] = mx + jnp.log(l)
                o = jnp.where(lowq if kvh == 0 else ~lowq, ov, 0.0) * (1.0 / l)
                if sub != kvh:
                    o = pltpu.roll(o, ATT_HEAD_DIM, 1)
                acc = o if acc is None else acc + o
            o32_ref[:, j * LANE:(j + 1) * LANE] = acc
            o_ref[:, j * LANE:(j + 1) * LANE] = _bf(acc)

    row = pl.BlockSpec((tq, ATT_Q_DIM), lambda i: (i, 0))
    return _call(
        body, (q, k, p), name="attn_fwd", grid=(t // tq,),
        in_specs=[pl.BlockSpec((tq, ATT_HEADS * LANE), lambda i: (i, 0)), _full((t, LANE)),
                  pl.BlockSpec((t, LANE), lambda i: (0, v_blk))],
        out_specs=[row, row, pl.BlockSpec((ATT_HEADS, tq, 1), lambda i: (0, i, 0))],
        out_shape=[jax.ShapeDtypeStruct((t, ATT_Q_DIM), BF16), jax.ShapeDtypeStruct((t, ATT_Q_DIM), F32),
                   jax.ShapeDtypeStruct((ATT_HEADS, t, 1), F32)])


def _attn_bwd(q, k, p, dcat, o32, lse, tq=256):
    t = k.shape[0]
    tq = min(tq, t)
    v_blk = (ATT_Q_DIM + ATT_KV_DIM) // LANE

    def body(q_ref, k_ref, v_ref, do_ref, o_ref, lse_ref, dq_ref, dk_ref, dv_ref):
        @pl.when(pl.program_id(0) == 0)
        def _():
            dk_ref[...] = jnp.zeros_like(dk_ref)
            dv_ref[...] = jnp.zeros_like(dv_ref)

        k_ = k_ref[...]
        vb = _bf(v_ref[...])
        lowq = lax.broadcasted_iota(jnp.int32, (tq, LANE), 1) < ATT_HEAD_DIM
        dk_acc = jnp.zeros((t, LANE), F32)
        dv_acc = jnp.zeros((t, LANE), F32)
        for j in range(4):
            kvh = j // 2
            dop = do_ref[:, j * LANE:(j + 1) * LANE]
            prod = dop * o_ref[:, j * LANE:(j + 1) * LANE]
            d_low = jnp.sum(jnp.where(lowq, prod, 0.0), axis=-1, keepdims=True)
            d_sub = (d_low, jnp.sum(prod, axis=-1, keepdims=True) - d_low)
            for sub in range(2):
                h = 2 * j + sub
                src = dop if sub == kvh else pltpu.roll(dop, ATT_HEAD_DIM, 1)
                do_h = _bf(jnp.where(lowq, src, 0.0) if kvh == 0 else jnp.where(lowq, 0.0, src))
                qh = q_ref[:, h * LANE:(h + 1) * LANE]
                pr = jnp.exp(_bf(_dot(qh, k_, 1, 1) - lse_ref[h]))
                ds = pr * _bf(_dot(do_h, vb, 1, 1) - d_sub[sub])
                dq_ref[:, h * LANE:(h + 1) * LANE] = _dot(ds, k_, 1, 0)
                dk_acc = dk_acc + _dot(ds, qh, 0, 0)
                dv_acc = dv_acc + _dot(pr, do_h, 0, 0)
        dk_ref[...] += dk_acc
        dv_ref[...] += dv_acc

    row = pl.BlockSpec((tq, ATT_Q_DIM), lambda i: (i, 0))
    return _call(
        body, (q, k, p, dcat, o32, lse), name="attn_bwd", grid=(t // tq,),
        in_specs=[pl.BlockSpec((tq, ATT_HEADS * LANE), lambda i: (i, 0)), _full((t, LANE)),
                  pl.BlockSpec((t, LANE), lambda i: (0, v_blk)), row, row,
                  pl.BlockSpec((ATT_HEADS, tq, 1), lambda i: (0, i, 0))],
        out_specs=[pl.BlockSpec((tq, ATT_HEADS * LANE), lambda i: (i, 0)), _full((t, LANE)), _full((t, LANE))],
        out_shape=[jax.ShapeDtypeStruct((t, ATT_HEADS * LANE), F32), jax.ShapeDtypeStruct((t, LANE), F32),
                   jax.ShapeDtypeStruct((t, LANE), F32)])


def _lower_bounds(a0, a1):
    def body(a0_ref, a1_ref, lb_ref):
        m = jnp.maximum(a0_ref[...], a1_ref[...])
        e0, e1 = jnp.exp(a0_ref[...] - m), jnp.exp(a1_ref[...] - m)
        lb_ref[...] = e0 / (e0 + e1)

    return pl.pallas_call(body, name="lower_bounds", out_shape=jax.ShapeDtypeStruct(a0.shape, F32))(a0, a1)


def _lower_bounds_bwd(a0, a1, dlb):
    def body(a0_ref, a1_ref, dlb_ref, d0_ref, d1_ref):
        m = jnp.maximum(a0_ref[...], a1_ref[...])
        e0, e1 = jnp.exp(a0_ref[...] - m), jnp.exp(a1_ref[...] - m)
        lb = e0 / (e0 + e1)
        d0 = dlb_ref[...] * lb * (1.0 - lb)
        d0_ref[...] = d0
        d1_ref[...] = -d0

    return pl.pallas_call(body, name="lower_bounds_bwd", out_shape=[jax.ShapeDtypeStruct(a0.shape, F32)] * 2)(a0, a1, dlb)


def _chunk_scan(x, pos, down):
    n = x.shape[0]
    s = 1
    while s < HG_CHUNK:
        if down:
            x = x + jnp.where(pos >= s, pltpu.roll(x, s, 0), 0.0)
        else:
            x = x + jnp.where(pos < HG_CHUNK - s, pltpu.roll(x, n - s, 0), 0.0)
        s *= 2
    return x


def _chunk_concat(x, cid, nc):
    return jnp.concatenate([_bf(jnp.where(cid == c, x, 0.0)) for c in range(nc)], axis=1)


def _chunk_pick(x, cid, nc):
    out = jnp.where(cid == 0, x[:, :LANE], 0.0)
    for c in range(1, nc):
        out = out + jnp.where(cid == c, x[:, c * LANE:(c + 1) * LANE], 0.0)
    return out


def _hgrn_gates(hq, z, lb):
    sq = _sigmoid(hq)
    sig = _sigmoid(z)
    f = lb + (1.0 - lb) * sig
    return hq * sq, sq, sig, f, jnp.log(f)


def _hgrn_local(q, f, g, v, pos, cid, amask, rev, nc):
    k = 1.0 - f
    ba = _chunk_scan(g, pos, not rev)
    bb = _chunk_scan(g, pos, rev)
    eq = 0.5 * (ba - bb + g)
    ex_q, ex_k, ex_i, ex_d = jnp.exp(eq), jnp.exp(-eq), jnp.exp(ba), jnp.exp(bb - g)
    qb, kb, qi, kd = q * ex_q, k * ex_k, q * ex_i, k * ex_d
    dvec = jnp.exp(ba + bb - g)
    a = jnp.where(amask, _dot(_bf(qb), _bf(kb), 1, 1), 0.0)
    kd_m = _chunk_concat(kd, cid, nc)
    qi_m = _chunk_concat(qi, cid, nc)
    ut_all = _dot(_bf(v), kd_m, 0, 0)
    return dict(k=k, ex_q=ex_q, ex_k=ex_k, ex_i=ex_i, ex_d=ex_d, qb=qb, kb=kb, qi=qi, kd=kd, dvec=dvec, a=a,
                kd_m=kd_m, qi_m=qi_m, ut_all=ut_all)


def _hgrn_masks(n, rev):
    row = lax.broadcasted_iota(jnp.int32, (n, LANE), 0)
    ti = lax.broadcasted_iota(jnp.int32, (n, n), 0)
    si = lax.broadcasted_iota(jnp.int32, (n, n), 1)
    tri = (si >= ti) if rev else (si <= ti)
    same = jnp.right_shift(ti, HG_CHUNK_LOG2) == jnp.right_shift(si, HG_CHUNK_LOG2)
    return jnp.bitwise_and(row, HG_CHUNK - 1), jnp.right_shift(row, HG_CHUNK_LOG2), same & tri


def _hgrn_specs(t, rev, bwd):
    n = min(HG_BLOCK_BWD if bwd else HG_BLOCK_FWD, t)
    nb = t // n
    if rev != bwd:
        blk = lambda i: nb - 1 - i
    else:
        blk = lambda i: i
    col = lambda base: pl.BlockSpec((n, 2 * LANE), lambda hp, i: (blk(i), base + hp))
    return n, nb, blk, col


def _hgrn_fwd(p, lb, rev):
    t = p.shape[0]
    n, nb, blk, col = _hgrn_specs(t, rev, False)
    nc = n // HG_CHUNK
    sub = n // min(HG_BLOCK_BWD, t)
    z_base = 7 if rev else 5

    def body(hq_ref, z_ref, hi_ref, lb_ref, o_ref, ssave_ref, st_scr):
        @pl.when(pl.program_id(1) == 0)
        def _():
            st_scr[...] = jnp.zeros_like(st_scr)

        pos, cid, amask = _hgrn_masks(n, rev)
        order = list(range(nc))[::-1] if rev else list(range(nc))
        for hh in range(2):
            sl = slice(hh * LANE, (hh + 1) * LANE)
            q, _, _, f, g = _hgrn_gates(hq_ref[:, sl], z_ref[:, sl], lb_ref[0:1, sl])
            v = hi_ref[:, sl]
            c_ = _hgrn_local(q, f, g, v, pos, cid, amask, rev, nc)
            st = st_scr[hh]
            cols = [None] * nc
            for c in order:
                cols[c] = st
                st = st * c_["dvec"][c * HG_CHUNK:c * HG_CHUNK + 1, :] + c_["ut_all"][:, c * LANE:(c + 1) * LANE]
            st_scr[hh] = st
            for s in range(sub):
                ssave_ref[sub - 1 - s if rev else s, sl, :] = cols[order[s * (nc // sub)]]
            st_all = _bf(jnp.concatenate(cols, axis=1))
            o_ref[:, sl] = _dot(_bf(c_["a"]), _bf(v), 1, 0) + _dot(c_["qi_m"], st_all, 1, 1)

    return _call(
        body, (p, p, p, lb), name="hgrn_fwd_rev" if rev else "hgrn_fwd", grid=(2, nb),
        in_specs=[col(3), col(z_base), col(9), pl.BlockSpec((1, 2 * LANE), lambda hp, i: (0, hp))],
        out_specs=[pl.BlockSpec((n, 2 * LANE), lambda hp, i: (blk(i), hp)),
                   pl.BlockSpec((sub, 2 * LANE, LANE), lambda hp, i: (blk(i), hp, 0))],
        out_shape=[jax.ShapeDtypeStruct((t, HG_DIM), F32), jax.ShapeDtypeStruct((nb * sub, HG_DIM, LANE), F32)],
        scratch_shapes=[pltpu.VMEM((2, LANE, LANE), F32)])


def _hgrn_bwd(p, lb, do, ssave, rev, after=()):
    t = p.shape[0]
    n, nb, blk, col = _hgrn_specs(t, rev, True)
    nc = n // HG_CHUNK
    z_base = 7 if rev else 5

    def body(hq_ref, z_ref, hi_ref, lb_ref, do_ref, ssave_ref, dhq_ref, dz_ref, dhi_ref, dlb_ref, dst_scr):
        @pl.when(pl.program_id(1) == 0)
        def _():
            dst_scr[...] = jnp.zeros_like(dst_scr)
            dlb_ref[...] = jnp.zeros_like(dlb_ref)

        pos, cid, amask = _hgrn_masks(n, rev)
        order = list(range(nc))[::-1] if rev else list(range(nc))
        for hh in range(2):
            sl = slice(hh * LANE, (hh + 1) * LANE)
            hq, lbv = hq_ref[:, sl], lb_ref[0:1, sl]
            q, sq, sig, f, g = _hgrn_gates(hq, z_ref[:, sl], lbv)
            v = hi_ref[:, sl]
            c_ = _hgrn_local(q, f, g, v, pos, cid, amask, rev, nc)
            dvec, ut_all = c_["dvec"], c_["ut_all"]
            drow = lambda c: dvec[c * HG_CHUNK:c * HG_CHUNK + 1, :]
            st = ssave_ref[0, sl, :]
            cols = [None] * nc
            for c in order:
                cols[c] = st
                st = st * drow(c) + ut_all[:, c * LANE:(c + 1) * LANE]
            dob, vb = _bf(do_ref[:, sl]), _bf(v)
            gt_all = _dot(dob, c_["qi_m"], 0, 0)
            dcur = dst_scr[hh]
            dnext = [None] * nc
            ddrow = [None] * nc
            for c in order[::-1]:
                dnext[c] = dcur
                ddrow[c] = jnp.sum(cols[c] * dcur, axis=0, keepdims=True) * drow(c)
                dcur = dcur * drow(c) + gt_all[:, c * LANE:(c + 1) * LANE]
            dst_scr[hh] = dcur
            dsn_all = _bf(jnp.concatenate(dnext, axis=1))
            st_all = _bf(jnp.concatenate(cols, axis=1))
            da = _bf(jnp.where(amask, _dot(dob, vb, 1, 1), 0.0))
            dv = _dot(_bf(c_["a"]), dob, 0, 0) + _dot(c_["kd_m"], dsn_all, 1, 1)
            dqb = _dot(da, _bf(c_["kb"]), 1, 0)
            dkb = _dot(da, _bf(c_["qb"]), 0, 0)
            dqi = _chunk_pick(_dot(dob, st_all, 1, 0), cid, nc)
            dkd = _chunk_pick(_dot(vb, dsn_all, 1, 0), cid, nc)
            dq = dqb * c_["ex_q"] + dqi * c_["ex_i"]
            dk = dkb * c_["ex_k"] + dkd * c_["ex_d"]
            e = dqb * c_["qb"] - dkb * c_["kb"] + dqi * c_["qi"]
            w = dkd * c_["kd"]
            dtot = jnp.where(cid == 0, ddrow[0], 0.0)
            for c in range(1, nc):
                dtot = dtot + jnp.where(cid == c, ddrow[c], 0.0)
            dg = _chunk_scan(e, pos, rev) + (_chunk_scan(w, pos, not rev) - w) + dtot
            df = dg / f - dk
            dz_ref[:, sl] = df * (1.0 - lbv) * sig * (1.0 - sig)
            dlb_ref[0:1, sl] += jnp.sum(df * (1.0 - sig), axis=0, keepdims=True)
            dhq_ref[:, sl] = dq * (sq * (1.0 + hq * (1.0 - sq)))
            dhi_ref[:, sl] = dv

    out = pl.BlockSpec((n, 2 * LANE), lambda hp, i: (blk(i), hp))
    return _call(
        body, (p, p, p, lb, do, ssave), name="hgrn_bwd_rev" if rev else "hgrn_bwd", grid=(2, nb),
        in_specs=[col(3), col(z_base), col(9), pl.BlockSpec((1, 2 * LANE), lambda hp, i: (0, hp)), out,
                  pl.BlockSpec((1, 2 * LANE, LANE), lambda hp, i: (blk(i), hp, 0))],
        out_specs=[out, out, out, pl.BlockSpec((1, 2 * LANE), lambda hp, i: (0, hp))],
        out_shape=[jax.ShapeDtypeStruct((t, HG_DIM), F32)] * 3 + [jax.ShapeDtypeStruct((1, HG_DIM), F32)],
        scratch_shapes=[pltpu.VMEM((2, LANE, LANE), F32)], after=after)


def _mix_out(att, of, ob, p, gout, w_out, g_post, x, tm=256):
    t, d = x.shape
    tm = min(tm, t)

    def body(att_ref, of_ref, ob_ref, hg0_ref, hg1_ref, gout_ref, w_ref, g_ref, x_ref, cat_ref, y_ref, x1_ref):
        cat_ref[:, :ATT_Q_DIM] = att_ref[...]
        for h in range(HG_HEADS):
            sl = slice(h * LANE, (h + 1) * LANE)
            hg_ref = hg0_ref if h < 2 else hg1_ref
            hg = hg_ref[:, (h % 2) * LANE:(h % 2 + 1) * LANE]
            nrm = _rms_fwd(of_ref[:, sl] + ob_ref[:, sl], gout_ref[...])
            cat_ref[:, ATT_Q_DIM + h * LANE:ATT_Q_DIM + (h + 1) * LANE] = _bf(nrm * hg * _sigmoid(hg))
        y = _dot(cat_ref[...], w_ref[...], 1, 0)
        y_ref[...] = y
        x1_ref[...] = x_ref[...] + _rms_fwd(y, g_ref[...])

    row = lambda width: pl.BlockSpec((tm, width), lambda i: (i, 0))
    return _call(
        body, (att, of, ob, p, p, gout, w_out, g_post, x), name="mix_out", grid=(t // tm,),
        in_specs=[row(ATT_Q_DIM), row(HG_DIM), row(HG_DIM), pl.BlockSpec((tm, 2 * LANE), lambda i: (i, 11)),
                  pl.BlockSpec((tm, 2 * LANE), lambda i: (i, 12)), _full((1, LANE)), _full((d, d)), _full((1, d)), row(d)],
        out_specs=[row(d), row(d), row(d)],
        out_shape=[jax.ShapeDtypeStruct((t, d), BF16), jax.ShapeDtypeStruct((t, d), F32), jax.ShapeDtypeStruct((t, d), F32)])


def _rec_bwd(dcat, of, ob, p, gout, tm=256):
    t = of.shape[0]
    tm = min(tm, t)

    def body(dc_ref, of_ref, ob_ref, hg0_ref, hg1_ref, gout_ref, do_ref, dhg_ref, dgo_ref):
        @pl.when(pl.program_id(0) == 0)
        def _():
            dgo_ref[...] = jnp.zeros_like(dgo_ref)

        dgo = jnp.zeros((1, LANE), F32)
        for h in range(HG_HEADS):
            sl = slice(h * LANE, (h + 1) * LANE)
            hg_ref = hg0_ref if h < 2 else hg1_ref
            hg = hg_ref[:, (h % 2) * LANE:(h % 2 + 1) * LANE]
            o = of_ref[:, sl] + ob_ref[:, sl]
            sg = _sigmoid(hg)
            drec = dc_ref[:, sl]
            dhg_ref[:, sl] = drec * _rms_fwd(o, gout_ref[...]) * (sg * (1.0 + hg * (1.0 - sg)))
            do, dg = _rms_bwd(drec * hg * sg, o, gout_ref[...])
            do_ref[:, sl] = do
            dgo = dgo + dg
        dgo_ref[...] += dgo

    row = lambda width: pl.BlockSpec((tm, width), lambda i: (i, 0))
    return pl.pallas_call(
        body, name="rec_bwd", grid=(t // tm,),
        in_specs=[pl.BlockSpec((tm, HG_DIM), lambda i: (i, 1)), row(HG_DIM), row(HG_DIM),
                  pl.BlockSpec((tm, 2 * LANE), lambda i: (i, 11)), pl.BlockSpec((tm, 2 * LANE), lambda i: (i, 12)),
                  _full((1, LANE))],
        out_specs=[row(HG_DIM), row(HG_DIM), _full((1, LANE))],
        out_shape=[jax.ShapeDtypeStruct((t, HG_DIM), F32), jax.ShapeDtypeStruct((t, HG_DIM), F32),
                   jax.ShapeDtypeStruct((1, LANE), F32)],
        compiler_params=_params(),
    )(dcat, of, ob, p, p, gout)


def _assemble_dp(dp_qk, dv, dhq_f, dhq_b, dz_f, dz_b, dhi_f, dhi_b, dhg, tm=512):
    t = dv.shape[0]
    tm = min(tm, t)
    qk_w = ATT_Q_DIM + LANE

    def body(qk_ref, dv_ref, hqf_ref, hqb_ref, zf_ref, zb_ref, hif_ref, hib_ref, hg_ref, dp_ref):
        o = 0
        dp_ref[:, o:o + qk_w] = qk_ref[...]
        o += qk_w
        dp_ref[:, o:o + LANE] = _bf(dv_ref[...])
        o += LANE
        for val in (hqf_ref[...] + hqb_ref[...], zf_ref[...], zb_ref[...], hif_ref[...] + hib_ref[...], hg_ref[...]):
            dp_ref[:, o:o + HG_DIM] = _bf(val)
            o += HG_DIM

    row = lambda width: pl.BlockSpec((tm, width), lambda i: (i, 0))
    return pl.pallas_call(
        body, name="assemble_dp", grid=(t // tm,),
        in_specs=[row(qk_w), row(LANE)] + [row(HG_DIM)] * 7,
        out_specs=row(N_IN),
        out_shape=jax.ShapeDtypeStruct((t, N_IN), BF16),
        compiler_params=_params(),
    )(dp_qk, dv, dhq_f, dhq_b, dz_f, dz_b, dhi_f, dhi_b, dhg)


def _xattn_fwd(q, kv, tq=512):
    t, d = q.shape
    tq = min(tq, t)
    nm = kv.shape[0]

    def body(q_ref, kv_ref, o_ref):
        for h in range(X_HEADS):
            sl = slice(h * X_HEAD_DIM, (h + 1) * X_HEAD_DIM)
            s = _dot(_bf(q_ref[:, sl]), _bf(kv_ref[:, sl]), 1, 1) * (X_HEAD_DIM ** -0.5)
            e = jnp.exp(s - jnp.max(s, axis=-1, keepdims=True))
            pr = e * (1.0 / jnp.sum(e, axis=-1, keepdims=True))
            o_ref[:, sl] = _bf(_dot(_bf(pr), _bf(kv_ref[:, d + h * X_HEAD_DIM:d + (h + 1) * X_HEAD_DIM]), 1, 0))

    return pl.pallas_call(
        body, name="xattn_fwd", grid=(t // tq,),
        in_specs=[pl.BlockSpec((tq, d), lambda i: (i, 0)), _full((nm, 2 * d))],
        out_specs=pl.BlockSpec((tq, d), lambda i: (i, 0)),
        out_shape=jax.ShapeDtypeStruct((t, d), BF16),
        compiler_params=_params(),
    )(q, kv)


def _xattn_bwd(q, kv, do, tq=512):
    t, d = q.shape
    tq = min(tq, t)
    nm = kv.shape[0]

    def body(q_ref, kv_ref, do_ref, dq_ref, dkv_ref):
        @pl.when(pl.program_id(0) == 0)
        def _():
            dkv_ref[...] = jnp.zeros_like(dkv_ref)

        for h in range(X_HEADS):
            sl = slice(h * X_HEAD_DIM, (h + 1) * X_HEAD_DIM)
            slv = slice(d + h * X_HEAD_DIM, d + (h + 1) * X_HEAD_DIM)
            qb, kb, vb, dob = _bf(q_ref[:, sl]), _bf(kv_ref[:, sl]), _bf(kv_ref[:, slv]), _bf(do_ref[:, sl])
            s = _dot(qb, kb, 1, 1) * (X_HEAD_DIM ** -0.5)
            e = jnp.exp(s - jnp.max(s, axis=-1, keepdims=True))
            pr = e * (1.0 / jnp.sum(e, axis=-1, keepdims=True))
            dpr = _dot(dob, vb, 1, 1)
            ds = _bf(pr * (dpr - jnp.sum(pr * dpr, axis=-1, keepdims=True)) * (X_HEAD_DIM ** -0.5))
            dq_ref[:, sl] = _bf(_dot(ds, kb, 1, 0))
            dkv_ref[:, sl] += _dot(ds, qb, 0, 0)
            dkv_ref[:, slv] += _dot(_bf(pr), dob, 0, 0)

    return pl.pallas_call(
        body, name="xattn_bwd", grid=(t // tq,),
        in_specs=[pl.BlockSpec((tq, d), lambda i: (i, 0)), _full((nm, 2 * d)), pl.BlockSpec((tq, d), lambda i: (i, 0))],
        out_specs=[pl.BlockSpec((tq, d), lambda i: (i, 0)), _full((nm, 2 * d))],
        out_shape=[jax.ShapeDtypeStruct((t, d), BF16), jax.ShapeDtypeStruct((nm, 2 * d), F32)],
        compiler_params=_params(),
    )(q, kv, do)


CONV_TN = 256


def _shift_rows(u, row, t, delta):
    if delta < 0:
        return jnp.where(row == 0, 0.0, pltpu.roll(u, 1, 0))
    return jnp.where(row == t - 1, 0.0, pltpu.roll(u, t - 1, 0))


def _conv_gate_fwd(u, conv_w, conv_b):
    t = u.shape[0]
    nt = D_FF // CONV_TN

    def body(ug_ref, uv_ref, wg_ref, wv_ref, bg_ref, bv_ref, a_ref):
        row = lax.broadcasted_iota(jnp.int32, (t, CONV_TN), 0)

        def conv(u_ref, w_ref, b_ref):
            uu = u_ref[...]
            return (b_ref[...] + _shift_rows(uu, row, t, -1) * w_ref[0:1, :] + uu * w_ref[1:2, :]
                    + _shift_rows(uu, row, t, 1) * w_ref[2:3, :])

        gate = conv(ug_ref, wg_ref, bg_ref)
        a_ref[...] = _bf(gate * _sigmoid(gate) * conv(uv_ref, wv_ref, bv_ref))

    col = lambda rows, off: pl.BlockSpec((rows, CONV_TN), lambda j: (0, j + off))
    return pl.pallas_call(
        body, name="conv_gate_fwd", grid=(nt,),
        in_specs=[col(t, 0), col(t, nt), col(3, 0), col(3, nt), col(1, 0), col(1, nt)],
        out_specs=col(t, 0),
        out_shape=jax.ShapeDtypeStruct((t, D_FF), BF16),
        compiler_params=_params(),
    )(u, u, conv_w, conv_w, conv_b, conv_b)


def _conv_gate_bwd(u, conv_w, conv_b, da, after=()):
    t = u.shape[0]
    nt = D_FF // CONV_TN

    def body(ug_ref, uv_ref, wg_ref, wv_ref, bg_ref, bv_ref, da_ref, dug_ref, dwg_ref, dbg_ref, duv_ref, dwv_ref, dbv_ref):
        row = lax.broadcasted_iota(jnp.int32, (t, CONV_TN), 0)
        ug, uv = ug_ref[...], uv_ref[...]
        ug_m, ug_p = _shift_rows(ug, row, t, -1), _shift_rows(ug, row, t, 1)
        uv_m, uv_p = _shift_rows(uv, row, t, -1), _shift_rows(uv, row, t, 1)
        gate = bg_ref[...] + ug_m * wg_ref[0:1, :] + ug * wg_ref[1:2, :] + ug_p * wg_ref[2:3, :]
        val = bv_ref[...] + uv_m * wv_ref[0:1, :] + uv * wv_ref[1:2, :] + uv_p * wv_ref[2:3, :]
        sg = _sigmoid(gate)
        da_ = da_ref[...].astype(F32)

        def emit(dacc, um, uu, up, w_ref, du_ref, dw_ref, db_ref):
            du_ref[...] = _bf(_shift_rows(dacc, row, t, 1) * w_ref[0:1, :] + dacc * w_ref[1:2, :]
                              + _shift_rows(dacc, row, t, -1) * w_ref[2:3, :])
            dw_ref[0:1, :] = jnp.sum(dacc * um, axis=0, keepdims=True)
            dw_ref[1:2, :] = jnp.sum(dacc * uu, axis=0, keepdims=True)
            dw_ref[2:3, :] = jnp.sum(dacc * up, axis=0, keepdims=True)
            db_ref[...] = jnp.sum(dacc, axis=0, keepdims=True)

        emit(da_ * val * (sg * (1.0 + gate * (1.0 - sg))), ug_m, ug, ug_p, wg_ref, dug_ref, dwg_ref, dbg_ref)
        emit(da_ * gate * sg, uv_m, uv, uv_p, wv_ref, duv_ref, dwv_ref, dbv_ref)

    col = lambda rows, off: pl.BlockSpec((rows, CONV_TN), lambda j: (0, j + off))
    half_shapes = [jax.ShapeDtypeStruct((t, D_FF), BF16), jax.ShapeDtypeStruct((3, D_FF), F32),
                   jax.ShapeDtypeStruct((1, D_FF), F32)]
    outs = _call(
        body, (u, u, conv_w, conv_w, conv_b, conv_b, da), name="conv_gate_bwd", grid=(nt,),
        in_specs=[col(t, 0), col(t, nt), col(3, 0), col(3, nt), col(1, 0), col(1, nt), col(t, 0)],
        out_specs=[col(t, 0), col(3, 0), col(1, 0)] * 2, out_shape=half_shapes * 2, after=after)
    return outs[:3], outs[3:]


def _row_tile(r, cap):
    best = r
    for cand in range(16, cap + 1, 16):
        if r % cand == 0:
            best = cand
    return best


def _sum_parts(own, got, me, name, tr=256):
    _, r, c = got.shape
    tr = _row_tile(r, tr)

    def body(me_ref, own_ref, got_ref, o_ref):
        mine = own_ref[...].astype(F32)
        acc = None
        for i in range(N_DEV):
            term = jnp.where(me_ref[0] == i, mine, got_ref[i].astype(F32))
            acc = term if acc is None else acc + term
        o_ref[...] = acc

    return pl.pallas_call(
        body, name=name,
        grid_spec=pltpu.PrefetchScalarGridSpec(
            num_scalar_prefetch=1, grid=(r // tr,),
            in_specs=[pl.BlockSpec((None, tr, c), lambda i, me_ref: (me_ref[0], i, 0)),
                      pl.BlockSpec((N_DEV, tr, c), lambda i, me_ref: (0, i, 0))],
            out_specs=pl.BlockSpec((tr, c), lambda i, me_ref: (i, 0))),
        out_shape=jax.ShapeDtypeStruct((r, c), F32),
        compiler_params=_params(),
    )(me, own, got)


def _adamw_math(w_, g_, m_, v_):
    m_ = ADAM_B1 * m_ + (1.0 - ADAM_B1) * g_
    v_ = ADAM_B2 * v_ + (1.0 - ADAM_B2) * (g_ * g_)
    m_hat = m_ / (1.0 - ADAM_B1 ** ADAM_STEP)
    v_hat = v_ / (1.0 - ADAM_B2 ** ADAM_STEP)
    return -ADAM_LR * (m_hat / (jnp.sqrt(v_hat) + ADAM_EPS) + ADAM_WD * w_), m_, v_


def _sum_adamw(own, got, me, w, m, v, name, tr=256):
    _, r, c = got.shape
    tr = _row_tile(r, tr)

    def body(me_ref, own_ref, got_ref, w_ref, m_ref, v_ref, g_ref, d_ref, nm_ref, nv_ref):
        mine = own_ref[...].astype(F32)
        acc = None
        for i in range(N_DEV):
            term = jnp.where(me_ref[0] == i, mine, got_ref[i].astype(F32))
            acc = term if acc is None else acc + term
        g_ref[...] = acc
        d_ref[...], nm_ref[...], nv_ref[...] = _adamw_math(w_ref[...], acc, m_ref[...], v_ref[...])

    blk = pl.BlockSpec((tr, c), lambda i, me_ref: (i, 0))
    return pl.pallas_call(
        body, name=name,
        grid_spec=pltpu.PrefetchScalarGridSpec(
            num_scalar_prefetch=1, grid=(r // tr,),
            in_specs=[pl.BlockSpec((None, tr, c), lambda i, me_ref: (me_ref[0], i, 0)),
                      pl.BlockSpec((N_DEV, tr, c), lambda i, me_ref: (0, i, 0)), blk, blk, blk],
            out_specs=[blk] * 4),
        out_shape=[jax.ShapeDtypeStruct((r, c), F32)] * 4,
        compiler_params=_params(),
    )(me, own, got, w, m, v)


def _adamw(w, g, m, v, name, tr=256):
    r, c = w.shape
    tr = _row_tile(r, tr)

    def body(w_ref, g_ref, m_ref, v_ref, d_ref, nm_ref, nv_ref):
        d_ref[...], nm_ref[...], nv_ref[...] = _adamw_math(w_ref[...], g_ref[...], m_ref[...], v_ref[...])

    blk = pl.BlockSpec((tr, c), lambda i: (i, 0))
    return pl.pallas_call(
        body, name=name, grid=(r // tr,),
        in_specs=[blk] * 4, out_specs=[blk] * 3,
        out_shape=[jax.ShapeDtypeStruct((r, c), F32)] * 3,
        compiler_params=_params(),
    )(w, g, m, v)


def _mesh_pos():
    return lax.axis_index("x"), lax.axis_index("y"), lax.axis_index("c")


def _dev_index(px, py, pc):
    return 4 * px + 2 * py + pc


class _Gather:
    def __init__(self, arrs):
        self.arrs = list(arrs)
        n = len(self.arrs)
        self.out_shape = [jax.ShapeDtypeStruct((N_DEV,) + a.shape, a.dtype) for a in self.arrs]
        self.sems = [pltpu.SemaphoreType.DMA((7, n)), pltpu.SemaphoreType.DMA((7, n)), pltpu.SemaphoreType.DMA((n,))]

    def _ctx(self, ins, outs, sems):
        send_sems, recv_sems, local_sems = sems
        x, y, c = _mesh_pos()
        chips = [(1 - x, y), (x, 1 - y), (1 - x, 1 - y)]

        def copy(k, a, block, to, src=None):
            dst = outs[a].at[_dev_index(*block)]
            return pltpu.make_async_remote_copy(
                src_ref=dst if src is None else src, dst_ref=dst, send_sem=send_sems.at[k, a], recv_sem=recv_sems.at[k, a],
                device_id=to, device_id_type=MESH_T)

        n = len(ins)
        me, sibling = (x, y, c), (x, y, 1 - c)
        mine = [pltpu.make_async_copy(ins[a], outs[a].at[_dev_index(*me)], local_sems.at[a]) for a in range(n)]
        first = [copy(0, a, me, sibling, src=ins[a]) for a in range(n)]
        first += [copy(1 + j, a, me, (*chip, c), src=ins[a]) for j, chip in enumerate(chips) for a in range(n)]
        passed = [copy(4 + j, a, (*chip, c), sibling) for j, chip in enumerate(chips) for a in range(n)]
        return n, c, me, sibling, chips, copy, mine, first, passed

    def start(self, ins, outs, sems):
        _, _, _, _, _, _, mine, first, _ = self._ctx(ins, outs, sems)
        for cp in mine + first:
            cp.start()

    def forward(self, ins, outs, sems):
        n, c, me, _, chips, copy, _, _, passed = self._ctx(ins, outs, sems)
        for j, chip in enumerate(chips):
            for a in range(n):
                copy(1 + j, a, (*chip, c), me).wait_recv()
                passed[j * n + a].start()

    def finish(self, ins, outs, sems):
        n, c, me, sibling, chips, copy, mine, first, passed = self._ctx(ins, outs, sems)
        for a in range(n):
            copy(0, a, sibling, me).wait_recv()
        for j, chip in enumerate(chips):
            for a in range(n):
                copy(4 + j, a, (*chip, 1 - c), me).wait_recv()
        for cp in first + passed:
            cp.wait_send()
        for cp in mine:
            cp.wait()


def _comm_alone(comm, name):
    n = len(comm.arrs)

    def body(*refs):
        ins, outs, sems = refs[:n], refs[n:2 * n], refs[2 * n:]
        comm.start(ins, outs, sems)
        if comm.forward is not None:
            comm.forward(ins, outs, sems)
        comm.finish(ins, outs, sems)

    any_spec = pl.BlockSpec(memory_space=pl.ANY)
    return pl.pallas_call(body, name=name, in_specs=[any_spec] * n, out_specs=[any_spec] * n, out_shape=comm.out_shape,
                          scratch_shapes=comm.sems)(*comm.arrs)


def _peers(x, y, c):
    out = []
    for k in range(1, N_DEV):
        pos = (1 - x if k & 4 else x, 1 - y if k & 2 else y, 1 - c if k & 1 else c)
        out.append((k, pos, _dev_index(*pos)))
    return out


def _exchange_begin(arrs, tag):
    n = len(arrs)
    lands = [lax.empty(a.shape, a.dtype) for a in arrs]

    def start_body(*refs):
        ins, land = refs[:n], refs[n:2 * n]
        send_sems, recv_sems, token = refs[2 * n], refs[2 * n + 1], refs[-1]
        x, y, c = _mesh_pos()
        me_i = _dev_index(x, y, c)
        for k, pos, peer_i in _peers(x, y, c):
            for a in range(n):
                pltpu.make_async_remote_copy(
                    src_ref=ins[a].at[peer_i], dst_ref=land[a].at[me_i], send_sem=send_sems.at[(k - 1) * n + a],
                    recv_sem=recv_sems.at[(k - 1) * n + a], device_id=pos, device_id_type=MESH_T).start()
        token[...] = jnp.zeros_like(token)

    hbm = pl.BlockSpec(memory_space=pltpu.HBM)
    sem = pl.BlockSpec(memory_space=pltpu.SEMAPHORE)
    thru = [pltpu.HBM(a.shape, a.dtype) for a in arrs]
    outs = pl.pallas_call(
        start_body, name="exchange_start_" + tag,
        out_shape=[pltpu.SemaphoreType.DMA((7 * n,)), pltpu.SemaphoreType.DMA((7 * n,))] + thru + thru
        + [jax.ShapeDtypeStruct((8, LANE), F32)],
        in_specs=[hbm] * (2 * n), out_specs=[sem, sem] + [hbm] * (2 * n) + [pl.BlockSpec(memory_space=pltpu.VMEM)],
        input_output_aliases={i: 2 + i for i in range(2 * n)},
        compiler_params=pltpu.CompilerParams(has_side_effects=pltpu.SideEffectType.DATAFLOW_SIDE_EFFECTING),
    )(*[pltpu.with_memory_space_constraint(a, pltpu.HBM) for a in list(arrs) + list(lands)])
    return (tag, n, outs[0], outs[1], outs[2:2 + n], outs[2 + n:2 + 2 * n]), outs[-1]


def _exchange_end(handle, after):
    tag, n, send_sems, recv_sems, srcs, lands = handle

    def body(*refs):
        ins, land = refs[:n], refs[n:2 * n]
        send_sems_, recv_sems_ = refs[2 * n], refs[2 * n + 1]
        x, y, c = _mesh_pos()
        me_i = _dev_index(x, y, c)
        for k, pos, peer_i in _peers(x, y, c):
            for a in range(n):
                cp = pltpu.make_async_remote_copy(
                    src_ref=ins[a].at[peer_i], dst_ref=land[a].at[me_i], send_sem=send_sems_.at[(k - 1) * n + a],
                    recv_sem=recv_sems_.at[(k - 1) * n + a], device_id=pos, device_id_type=MESH_T)
                cp.wait_send()
                cp.wait_recv()

    hbm = pl.BlockSpec(memory_space=pltpu.HBM)
    sem = pl.BlockSpec(memory_space=pltpu.SEMAPHORE)
    outs = pl.pallas_call(
        body, name="exchange_end_" + tag, out_shape=[pltpu.HBM(a.shape, a.dtype) for a in list(srcs) + list(lands)],
        in_specs=[hbm] * (2 * n) + [sem, sem, pl.BlockSpec(memory_space=pl.ANY)], out_specs=[hbm] * (2 * n),
        input_output_aliases={i: i for i in range(2 * n)},
        compiler_params=pltpu.CompilerParams(has_side_effects=pltpu.SideEffectType.DATAFLOW_SIDE_EFFECTING),
    )(*srcs, *lands, send_sems, recv_sems, after)
    return list(zip(outs[:n], outs[n:]))


def _gather_begin(arrs, tag):
    n = len(arrs)
    me = _dev_index(*_mesh_pos())
    lands = [lax.dynamic_update_slice(lax.empty((N_DEV,) + a.shape, a.dtype), a[None], (me,) + (0,) * a.ndim) for a in arrs]

    def start_body(*refs):
        ins, land = refs[:n], refs[n:2 * n]
        send_sems, recv_sems, token = refs[2 * n], refs[2 * n + 1], refs[-1]
        x, y, c = _mesh_pos()
        me_i = _dev_index(x, y, c)
        for a in range(n):
            for k, pos, _ in _peers(x, y, c):
                pltpu.make_async_remote_copy(
                    src_ref=ins[a], dst_ref=land[a].at[me_i], send_sem=send_sems.at[(k - 1) * n + a],
                    recv_sem=recv_sems.at[(k - 1) * n + a], device_id=pos, device_id_type=MESH_T).start()
        token[...] = jnp.zeros_like(token)

    hbm = pl.BlockSpec(memory_space=pltpu.HBM)
    sem = pl.BlockSpec(memory_space=pltpu.SEMAPHORE)
    thru = [pltpu.HBM(a.shape, a.dtype) for a in list(arrs) + lands]
    outs = pl.pallas_call(
        start_body, name="gather_start_" + tag,
        out_shape=[pltpu.SemaphoreType.DMA((7 * n,)), pltpu.SemaphoreType.DMA((7 * n,))] + thru
        + [jax.ShapeDtypeStruct((8, LANE), F32)],
        in_specs=[hbm] * (2 * n), out_specs=[sem, sem] + [hbm] * (2 * n) + [pl.BlockSpec(memory_space=pltpu.VMEM)],
        input_output_aliases={i: 2 + i for i in range(2 * n)},
        compiler_params=pltpu.CompilerParams(has_side_effects=pltpu.SideEffectType.DATAFLOW_SIDE_EFFECTING),
    )(*[pltpu.with_memory_space_constraint(a, pltpu.HBM) for a in list(arrs) + lands])
    return (tag, n, outs[0], outs[1], outs[2:2 + n], outs[2 + n:2 + 2 * n]), outs[-1]


def _gather_end(handle, which, after):
    tag, n, send_sems, recv_sems, srcs, lands = handle
    m = len(which)

    def body(*refs):
        ins, land = refs[:m], refs[m:2 * m]
        send_sems_, recv_sems_ = refs[2 * m], refs[2 * m + 1]
        x, y, c = _mesh_pos()
        me_i = _dev_index(x, y, c)
        for j, a in enumerate(which):
            for k, pos, _ in _peers(x, y, c):
                cp = pltpu.make_async_remote_copy(
                    src_ref=ins[j], dst_ref=land[j].at[me_i], send_sem=send_sems_.at[(k - 1) * n + a],
                    recv_sem=recv_sems_.at[(k - 1) * n + a], device_id=pos, device_id_type=MESH_T)
                cp.wait_send()
                cp.wait_recv()

    hbm = pl.BlockSpec(memory_space=pltpu.HBM)
    sem = pl.BlockSpec(memory_space=pltpu.SEMAPHORE)
    ops = [srcs[a] for a in which] + [lands[a] for a in which]
    outs = pl.pallas_call(
        body, name="gather_end_%s_%s" % (tag, "_".join(str(a) for a in which)),
        out_shape=[pltpu.HBM(a.shape, a.dtype) for a in ops],
        in_specs=[hbm] * (2 * m) + [sem, sem, pl.BlockSpec(memory_space=pl.ANY)], out_specs=[hbm] * (2 * m),
        input_output_aliases={i: i for i in range(2 * m)},
        compiler_params=pltpu.CompilerParams(has_side_effects=pltpu.SideEffectType.DATAFLOW_SIDE_EFFECTING),
    )(*ops, send_sems, recv_sems, after)
    return list(outs[m:])


def _call(body, operands, *, name, grid, in_specs, out_specs, out_shape, scratch_shapes=(), after=()):
    ni, nd = len(in_specs), len(after)

    def ordered(*refs):
        body(*refs[:ni], *refs[ni + nd:])

    outs = pl.pallas_call(
        ordered, name=name, grid=grid, in_specs=list(in_specs) + [pl.BlockSpec(memory_space=pl.ANY)] * nd,
        out_specs=out_specs, out_shape=out_shape, scratch_shapes=list(scratch_shapes), compiler_params=_params(),
    )(*operands, *after)
    return list(outs)


def _rows(a):
    return a.reshape(-1, a.shape[-1])


def _slots(a):
    return a.reshape(N_DEV, -1, a.shape[-1])


def _local_step(x, mem, tgt, w, wire):
    tables = _rope_tables(x.shape[0])
    bd = _head_sum_matrix()
    two = lambda g: jnp.tile(g, (1, 2))
    gq2, gk2 = two(w["q_norm_g"]), two(w["k_norm_g"])

    w_in_t, hg_lb, conv_w = _comm_alone(_Gather([wire["w_in"], w["hg_lb"].reshape(4, -1), w["conv_w"][0]]), "gather_w_in")
    w_in_t = _rows(w_in_t)
    hg_lb = jnp.transpose(hg_lb.reshape(N_DEV, 2, 2, -1), (1, 2, 0, 3)).reshape(2, 2, HG_DIM)
    conv_w = jnp.transpose(conv_w, (1, 0, 2)).reshape(3, 2 * D_FF)
    lb_a0, lb_a1 = hg_lb[:, 0, :], hg_lb[:, 1, :]

    order = ("w_out", "w_xq", "w_xkv", "w_xo", "w_up", "w_down")
    w_in_t, *later = lax.optimization_barrier((w_in_t, *[wire[n] for n in order]))
    fetch, started = _gather_begin(later, "w")
    take = lambda n, after: _rows(_gather_end(fetch, [order.index(n)], after)[0])

    h1, p = _norm_mm(x, w["pre_mix_g"], w_in_t, True, N_IN, "in_proj", after=[started])
    qr, kr = _qk_prep(p, gq2, gk2, tables, bd)
    att, att32, lse = _attn_fwd(qr, kr, p)
    lb = _lower_bounds(lb_a0, lb_a1)
    of, s_f = _hgrn_fwd(p, lb[0:1], False)
    ob, s_b = _hgrn_fwd(p, lb[1:2], True)
    w_out = take("w_out", ob)
    cat, mixed, x1 = _mix_out(att, of, ob, p, w["hg_out_norm_g"], w_out, w["post_mix_g"], x)

    w_xq = take("w_xq", x1)
    h2, q2 = _norm_mm(x1, w["pre_x_g"], w_xq, False, 1024, "xq_proj")
    w_xkv_t = take("w_xkv", q2)
    mn, kv = _norm_mm(mem, w["mem_norm_g"], w_xkv_t, True, 2 * D_MODEL, "xkv_proj")
    o2 = _xattn_fwd(q2, kv)
    w_xo = take("w_xo", o2)
    y2, x2 = _mm_postnorm_res(o2, w_xo, w["post_x_g"], x1, "xo_proj")

    w_up_t = take("w_up", x2)
    h3, u = _norm_mm(x2, w["pre_ffn_g"], w_up_t, True, 2 * D_FF, "up_proj", tm=256)
    a = _conv_gate_fwd(u, conv_w, w["conv_b"])
    w_down = take("w_down", a)
    y3, dx3, loss = _mm_postnorm_res_loss(a, w_down, w["post_ffn_g"], x2, tgt, "down_proj")

    g, pending = {}, {}
    dy3, da, g["post_ffn_g"] = _postnorm_bwd_mm(dx3, y3, w["post_ffn_g"], w_down, "down_bwd", BF16)
    pending["down"], started = _exchange_begin([_slots(_dw(a, dy3, "dw_down"))], "down")
    (du_g, dcw_g, dcb_g), (du_v, dcw_v, dcb_v) = _conv_gate_bwd(u, conv_w, w["conv_b"], da, after=[started])
    g["conv_w"] = jnp.concatenate([dcw_g, dcw_v], axis=1)
    g["conv_b"] = jnp.concatenate([dcb_g, dcb_v], axis=1)
    pending["up"], started = _exchange_begin([_slots(_dw([du_g, du_v], h3, "dw_up"))], "up")
    dx2, g["pre_ffn_g"] = _mm_prenorm_bwd([du_g, du_v], w_up_t, True, x2, w["pre_ffn_g"], dx3, "up_bwd", after=[started], tm=256)

    dy2, do2, g["post_x_g"] = _postnorm_bwd_mm(dx2, y2, w["post_x_g"], w_xo, "xo_bwd", BF16)
    dw_xo = _dw(o2, dy2, "dw_xo")
    dq2, dkv = _xattn_bwd(q2, kv, do2)
    dw_xq = _dw(h2, dq2, "dw_xq")
    dkvb = _bf(dkv)
    dw_xkv = _dw(dkvb, mn, "dw_xkv")
    pending["x"], started = _exchange_begin([_slots(dw_xo), _slots(dw_xq), _slots(dw_xkv)], "x")
    _, g["mem_norm_g"] = _mm_prenorm_bwd(dkvb, w_xkv_t, True, mem, w["mem_norm_g"], jnp.zeros_like(mem), "xkv_bwd")
    dx1, g["pre_x_g"] = _mm_prenorm_bwd(dq2, w_xq, False, x1, w["pre_x_g"], dx2, "xq_bwd", after=[started])

    dmixed, dcat, g["post_mix_g"] = _postnorm_bwd_mm(dx1, mixed, w["post_mix_g"], w_out, "out_bwd")
    pending["out"], started = _exchange_begin([_slots(_dw(cat, dmixed, "dw_out"))], "out")
    do, dhg, g["hg_out_norm_g"] = _rec_bwd(dcat, of, ob, p, w["hg_out_norm_g"])
    dhq_f, dz_f, dhi_f, dlb_f = _hgrn_bwd(p, lb[0:1], do, s_f, False, after=[started])
    dhq_b, dz_b, dhi_b, dlb_b = _hgrn_bwd(p, lb[1:2], do, s_b, True)
    d_a0, d_a1 = _lower_bounds_bwd(lb_a0, lb_a1, jnp.concatenate([dlb_f, dlb_b], axis=0))
    g["hg_lb"] = jnp.stack([d_a0, d_a1], axis=1)
    dqr, dkr, dv = _attn_bwd(qr, kr, p, dcat, att32, lse)
    dp_qk, dgq, dgk = _qk_prep_bwd(p, dqr, dkr, gq2, gk2, tables, bd)
    g["q_norm_g"], g["k_norm_g"] = dgq, dgk
    dp = _assemble_dp(dp_qk, dv, dhq_f, dhq_b, dz_f, dz_b, dhi_f, dhi_b, dhg)
    pending["in"], started = _exchange_begin([_slots(_dw(dp, h1, "dw_in"))], "in")
    dx, g["pre_mix_g"] = _mm_prenorm_bwd(dp, w_in_t, True, x, w["pre_mix_g"], dx1, "in_bwd", after=[started])
    return loss, dx, g, pending


_COL_SHARDED = ("w_in", "w_xkv", "w_up")
_ROW_SHARDED = ("w_out", "w_xq", "w_xo", "w_down")
_REPLICATED = ("pre_mix_g", "q_norm_g", "k_norm_g", "hg_out_norm_g", "post_mix_g", "pre_x_g", "mem_norm_g", "post_x_g",
               "pre_ffn_g", "conv_b", "post_ffn_g")
_WEIGHTS = ("pre_mix_g", "w_in", "q_norm_g", "k_norm_g", "hg_lb", "hg_out_norm_g", "w_out", "post_mix_g", "pre_x_g",
            "mem_norm_g", "w_xq", "w_xkv", "w_xo", "post_x_g", "pre_ffn_g", "w_up", "conv_w", "conv_b", "w_down",
            "post_ffn_g")
_ADAM_TRANSPOSED = ("w_in", "w_up")
PACK_W = 1024


def _small_plan(shapes):
    plan, r = [], 0
    for vi, (rows, cols) in enumerate(shapes):
        for i in range(rows):
            for c0 in range(0, cols, PACK_W):
                plan.append((vi, i, c0, min(PACK_W, cols - c0), r))
                r += 1
    return plan, -(-r // 8) * 8


def _pack_small(vals):
    plan, nrows = _small_plan([val.shape for val in vals])

    def body(*refs):
        ins, out = refs[:-1], refs[-1]
        out[...] = jnp.zeros_like(out)
        for vi, i, c0, width, r in plan:
            out[r:r + 1, 0:width] = ins[vi][i:i + 1, c0:c0 + width]

    return pl.pallas_call(body, name="pack_small", out_shape=jax.ShapeDtypeStruct((nrows, PACK_W), F32))(*vals)


def _sum_unpack_small(packs, shapes):
    plan, _ = _small_plan(shapes)

    def body(*refs):
        p_ref, outs = refs[0], refs[1:]
        acc = p_ref[0]
        for i in range(1, N_DEV):
            acc = acc + p_ref[i]
        for vi, i, c0, width, r in plan:
            outs[vi][i:i + 1, c0:c0 + width] = acc[r:r + 1, 0:width]

    return pl.pallas_call(body, name="sum_unpack_small", out_shape=[jax.ShapeDtypeStruct(s, F32) for s in shapes])(packs)


def _adamw_many(ws, gs, ms, vs):
    n = len(ws)

    def body(*refs):
        w_refs, g_refs, m_refs, v_refs = (refs[k * n:(k + 1) * n] for k in range(4))
        d_refs, nm_refs, nv_refs = (refs[(4 + k) * n:(5 + k) * n] for k in range(3))
        for k in range(n):
            g_ = g_refs[k][...]
            m_ = ADAM_B1 * m_refs[k][...] + (1.0 - ADAM_B1) * g_
            v_ = ADAM_B2 * v_refs[k][...] + (1.0 - ADAM_B2) * (g_ * g_)
            m_hat = m_ / (1.0 - ADAM_B1 ** ADAM_STEP)
            v_hat = v_ / (1.0 - ADAM_B2 ** ADAM_STEP)
            d_refs[k][...] = -ADAM_LR * (m_hat / (jnp.sqrt(v_hat) + ADAM_EPS) + ADAM_WD * w_refs[k][...])
            nm_refs[k][...] = m_
            nv_refs[k][...] = v_

    shapes = [jax.ShapeDtypeStruct(a.shape, F32) for a in ws]
    outs = pl.pallas_call(body, name="adamw_small", out_shape=shapes * 3)(*ws, *gs, *ms, *vs)
    return outs[:n], outs[n:2 * n], outs[2 * n:]


def kernel(x, mem, pre_mix_g, w_in, q_norm_g, k_norm_g, hg_lb, hg_out_norm_g, w_out, post_mix_g, pre_x_g, mem_norm_g, w_xq, w_xkv, w_xo, post_x_g, pre_ffn_g, w_up, conv_w, conv_b, w_down, post_ffn_g, loss_target, m_pre_mix_g, m_w_in, m_q_norm_g, m_k_norm_g, m_hg_lb, m_hg_out_norm_g, m_w_out, m_post_mix_g, m_pre_x_g, m_mem_norm_g, m_w_xq, m_w_xkv, m_w_xo, m_post_x_g, m_pre_ffn_g, m_w_up, m_conv_w, m_conv_b, m_w_down, m_post_ffn_g, v_pre_mix_g, v_w_in, v_q_norm_g, v_k_norm_g, v_hg_lb, v_hg_out_norm_g, v_w_out, v_post_mix_g, v_pre_x_g, v_mem_norm_g, v_w_xq, v_w_xkv, v_w_xo, v_post_x_g, v_pre_ffn_g, v_w_up, v_conv_w, v_conv_b, v_w_down, v_post_ffn_g):
    args = dict(locals())
    w = {n: args[n] for n in _WEIGHTS}
    m = {n: args["m_" + n] for n in _WEIGHTS}
    v = {n: args["v_" + n] for n in _WEIGHTS}
    me = _dev_index(*_mesh_pos())

    wire = {n: _bf(w[n][0].T) for n in _COL_SHARDED}
    wire.update({n: _bf(w[n][0]) for n in _ROW_SHARDED})

    loss, grad_x, g, pending = _local_step(x[0], mem[0], loss_target[0], w, wire)

    grads, delta, new_m, new_v = {}, {}, {}, {}

    me_arr = jnp.reshape(me, (1,)).astype(jnp.int32)

    def update(n, parts):
        if n in _ADAM_TRANSPOSED:
            outs = _sum_adamw(*parts, me_arr, w[n][0].T, m[n][0].T, v[n][0].T, "update_" + n)
            grads[n], delta[n], new_m[n], new_v[n] = (a.T[None] for a in outs)
        elif n in _COL_SHARDED:
            gsum = _sum_parts(*parts, me_arr, "sum_" + n).T
            grads[n] = gsum[None]
            delta[n], new_m[n], new_v[n] = (a[None] for a in _adamw(w[n][0], gsum, m[n][0], v[n][0], "adamw_" + n))
        else:
            outs = _sum_adamw(*parts, me_arr, w[n][0], m[n][0], v[n][0], "update_" + n)
            grads[n], delta[n], new_m[n], new_v[n] = (a[None] for a in outs)

    after = grad_x
    for tag, names in (("down", ["w_down"]), ("up", ["w_up"]), ("x", ["w_xo", "w_xq", "w_xkv"]), ("out", ["w_out"]),
                       ("in", ["w_in"])):
        for n, parts in zip(names, _exchange_end(pending[tag], after)):
            update(n, parts)
            after = new_v[n]

    small = list(_REPLICATED) + ["hg_lb", "conv_w"]
    vals = [g[n] for n in _REPLICATED] + [g["hg_lb"].reshape(4, HG_DIM), g["conv_w"], jnp.pad(loss, ((0, 0), (0, LANE - 1)))]
    shapes = [val.shape for val in vals]
    (packs,) = _comm_alone(_Gather([_pack_small(vals)]), "gather_small_grads")
    summed = _sum_unpack_small(packs, shapes)
    loss = summed[-1][0, 0]
    for n, s in zip(small, summed[:-1]):
        grads[n] = s
    fold = lambda v2: v2[:, :ATT_HEAD_DIM] + v2[:, ATT_HEAD_DIM:]
    grads["q_norm_g"], grads["k_norm_g"] = fold(grads["q_norm_g"]), fold(grads["k_norm_g"])
    grads["hg_lb"] = lax.dynamic_slice_in_dim(grads["hg_lb"].reshape(2, 2, HG_DIM), me * (HG_DIM // N_DEV),
                                              HG_DIM // N_DEV, axis=2)
    grads["conv_w"] = lax.dynamic_slice_in_dim(grads["conv_w"], me * (2 * D_FF // N_DEV), 2 * D_FF // N_DEV, axis=1)[None]

    flat2 = lambda a: a.reshape(-1, a.shape[-1])
    outs = _adamw_many(*[[flat2(d[n]) for n in small] for d in (w, grads, m, v)])
    for dst, vals in zip((delta, new_m, new_v), outs):
        for n, val in zip(small, vals):
            dst[n] = val.reshape(w[n].shape)

    return (loss, grad_x[None], *[grads[n] for n in _WEIGHTS], *[delta[n] for n in _WEIGHTS],
            *[new_m[n] for n in _WEIGHTS], *[new_v[n] for n in _WEIGHTS])
```

```python
import jax
import jax.numpy as jnp
from jax import lax
from jax.experimental import pallas as pl
from jax.experimental.pallas import tpu as pltpu

F32 = jnp.float32
BF16 = jnp.bfloat16

D_MODEL = 1024
GRID_W = 64
EPS = 1e-6
ATT_HEADS = 8
ATT_HEAD_DIM = 64
ATT_Q_DIM = 512
ATT_KV_DIM = 128
ROPE_THETA = 10000.0
HG_HEADS = 4
HG_DIM = 512
HG_CHUNK = 32
HG_CHUNK_LOG2 = 5
HG_BLOCK_FWD = 256
HG_BLOCK_BWD = 128
N_IN = 3328
X_HEADS = 4
X_HEAD_DIM = 256
D_FF = 2816
N_DEV = 8
LANE = 128
ADAM_LR = 0.001
ADAM_B1 = 0.9
ADAM_B2 = 0.999
ADAM_EPS = 1e-08
ADAM_WD = 0.01
ADAM_STEP = 10
VMEM_LIMIT = 56 * 1024 * 1024

MESH_T = pl.DeviceIdType.MESH


def _params(**kw):
    return pltpu.CompilerParams(vmem_limit_bytes=VMEM_LIMIT, **kw)


def _dot(a, b, ca, cb):
    return lax.dot_general(a, b, (((ca,), (cb,)), ((), ())), preferred_element_type=F32)


def _bf(x):
    return x.astype(BF16)


def _sigmoid(x):
    return 1.0 / (1.0 + jnp.exp(-x))


def _rms_fwd(x, g):
    r = lax.rsqrt(jnp.mean(x * x, axis=-1, keepdims=True) + EPS)
    return x * r * g


def _rms_bwd(dy, x, g):
    r = lax.rsqrt(jnp.mean(x * x, axis=-1, keepdims=True) + EPS)
    xh = x * r
    dg = jnp.sum(dy * xh, axis=0, keepdims=True)
    t = dy * g
    dx = r * (t - xh * jnp.mean(t * xh, axis=-1, keepdims=True))
    return dx, dg


def _full(shape):
    nd = len(shape)
    return pl.BlockSpec(shape, lambda *a: (0,) * nd)


def _norm_mm(x, g, w, trans, tn, name, after=(), tm=512):
    t, d = x.shape
    n = w.shape[0] if trans else w.shape[1]
    tm = min(tm, t)

    def body(x_ref, g_ref, w_ref, h_ref, p_ref):
        h = _bf(_rms_fwd(x_ref[...], g_ref[...]))
        h_ref[...] = h
        p_ref[...] = _dot(h, w_ref[...], 1, 1 if trans else 0)

    w_spec = pl.BlockSpec((tn, d), lambda i, j: (j, 0)) if trans else pl.BlockSpec((d, tn), lambda i, j: (0, j))
    return _call(
        body, (x, g, w), name=name, grid=(t // tm, n // tn),
        in_specs=[pl.BlockSpec((tm, d), lambda i, j: (i, 0)), _full((1, d)), w_spec],
        out_specs=[pl.BlockSpec((tm, d), lambda i, j: (i, 0)), pl.BlockSpec((tm, tn), lambda i, j: (i, j))],
        out_shape=[jax.ShapeDtypeStruct((t, d), BF16), jax.ShapeDtypeStruct((t, n), F32)], after=after)


def _mm_postnorm_res(a, w, g, res, name, tm=256):
    t, k = a.shape
    d = w.shape[1]

    def body(a_ref, w_ref, g_ref, res_ref, y_ref, o_ref):
        y = _dot(a_ref[...], w_ref[...], 1, 0)
        y_ref[...] = y
        o_ref[...] = res_ref[...] + _rms_fwd(y, g_ref[...])

    row = lambda width: pl.BlockSpec((tm, width), lambda i: (i, 0))
    return pl.pallas_call(
        body, name=name, grid=(t // tm,),
        in_specs=[row(k), _full((k, d)), _full((1, d)), row(d)],
        out_specs=[row(d), row(d)],
        out_shape=[jax.ShapeDtypeStruct((t, d), F32)] * 2,
        compiler_params=_params(),
    )(a, w, g, res)


def _mm_postnorm_res_loss(a, w, g, res, tgt, name, tm=256):
    t, k = a.shape
    d = w.shape[1]

    def body(a_ref, w_ref, g_ref, res_ref, tgt_ref, y_ref, dout_ref, loss_ref):
        @pl.when(pl.program_id(0) == 0)
        def _():
            loss_ref[...] = jnp.zeros_like(loss_ref)

        y = _dot(a_ref[...], w_ref[...], 1, 0)
        y_ref[...] = y
        diff = res_ref[...] + _rms_fwd(y, g_ref[...]) - tgt_ref[...]
        dout_ref[...] = diff * (1.0 / d)
        part = jnp.sum(jnp.sum(diff * diff, axis=-1, keepdims=True), axis=0, keepdims=True)
        loss_ref[...] += (0.5 / d) * part

    row = lambda width: pl.BlockSpec((tm, width), lambda i: (i, 0))
    return pl.pallas_call(
        body, name=name, grid=(t // tm,),
        in_specs=[row(k), _full((k, d)), _full((1, d)), row(d), row(d)],
        out_specs=[row(d), row(d), _full((1, 1))],
        out_shape=[jax.ShapeDtypeStruct((t, d), F32)] * 2 + [jax.ShapeDtypeStruct((1, 1), F32)],
        compiler_params=_params(),
    )(a, w, g, res, tgt)


def _postnorm_bwd_mm(dout, y, g, w, name, da_dtype=F32, tm=256):
    t, d = y.shape
    k = w.shape[0]

    def body(dout_ref, y_ref, g_ref, w_ref, dy_ref, da_ref, dg_ref):
        @pl.when(pl.program_id(0) == 0)
        def _():
            dg_ref[...] = jnp.zeros_like(dg_ref)

        dy, dg = _rms_bwd(dout_ref[...], y_ref[...], g_ref[...])
        dg_ref[...] += dg
        dyb = _bf(dy)
        dy_ref[...] = dyb
        da_ref[...] = _dot(dyb, w_ref[...], 1, 1).astype(da_dtype)

    row = lambda width: pl.BlockSpec((tm, width), lambda i: (i, 0))
    return pl.pallas_call(
        body, name=name, grid=(t // tm,),
        in_specs=[row(d), row(d), _full((1, d)), _full((k, d))],
        out_specs=[row(d), row(k), _full((1, d))],
        out_shape=[jax.ShapeDtypeStruct((t, d), BF16), jax.ShapeDtypeStruct((t, k), da_dtype), jax.ShapeDtypeStruct((1, d), F32)],
        compiler_params=_params(),
    )(dout, y, g, w)


def _mm_prenorm_bwd(dp, w, trans, x, g, dres, name, after=(), tm=512):
    dps = list(dp) if isinstance(dp, (list, tuple)) else [dp]
    nparts = len(dps)
    t, n_each = dps[0].shape
    d = x.shape[1]
    tm = min(tm, t)

    def body(*refs):
        dp_refs = refs[:nparts]
        w_ref, x_ref, g_ref, dres_ref, dx_ref, dg_ref = refs[nparts:]

        @pl.when(pl.program_id(0) == 0)
        def _():
            dg_ref[...] = jnp.zeros_like(dg_ref)

        dh = None
        for part, dp_ref in enumerate(dp_refs):
            cols = slice(part * n_each, (part + 1) * n_each)
            term = _dot(dp_ref[...], w_ref[cols, :], 1, 0) if trans else _dot(dp_ref[...], w_ref[:, cols], 1, 1)
            dh = term if dh is None else dh + term
        dx, dg = _rms_bwd(dh, x_ref[...], g_ref[...])
        dg_ref[...] += dg
        dx_ref[...] = dres_ref[...] + dx

    row = lambda width: pl.BlockSpec((tm, width), lambda i: (i, 0))
    return _call(
        body, (*dps, w, x, g, dres), name=name, grid=(t // tm,),
        in_specs=[row(n_each)] * nparts + [_full(w.shape), row(d), _full((1, d)), row(d)],
        out_specs=[row(d), _full((1, d))],
        out_shape=[jax.ShapeDtypeStruct((t, d), F32), jax.ShapeDtypeStruct((1, d), F32)], after=after)


def _dw(a, b, name, bcol=None, tka=256):
    parts = list(a) if isinstance(a, (list, tuple)) else [a]
    nparts = len(parts)
    t, ka_each = parts[0].shape
    bci, nb = (0, b.shape[1]) if bcol is None else bcol
    nt_each = ka_each // tka

    def body(*refs):
        a_refs, b_ref, o_ref = refs[:nparts], refs[nparts], refs[nparts + 1]
        i = pl.program_id(0)
        for part, a_ref in enumerate(a_refs):
            @pl.when((i >= part * nt_each) & (i < (part + 1) * nt_each))
            def _():
                o_ref[...] = _bf(_dot(a_ref[...], b_ref[...], 0, 0))

    a_spec = lambda part: pl.BlockSpec((t, tka), lambda i: (0, jnp.clip(i - part * nt_each, 0, nt_each - 1)))
    return pl.pallas_call(
        body, name=name, grid=(nt_each * nparts,),
        in_specs=[a_spec(part) for part in range(nparts)] + [pl.BlockSpec((t, nb), lambda i: (0, bci))],
        out_specs=pl.BlockSpec((tka, nb), lambda i: (i, 0)),
        out_shape=jax.ShapeDtypeStruct((nparts * ka_each, nb), BF16),
        compiler_params=_params(),
    )(*parts, b)


def _rope_tables(t):
    pos = jnp.arange(t)
    r = (pos // GRID_W).astype(F32)
    c = (pos % GRID_W).astype(F32)
    npair = ATT_HEAD_DIM // 4
    inv = jnp.power(ROPE_THETA, -jnp.arange(npair, dtype=F32) / npair)
    ang = jnp.concatenate([r[:, None] * inv, c[:, None] * inv], axis=-1)
    cos = jnp.repeat(jnp.cos(ang), 2, axis=-1)
    sin = jnp.repeat(jnp.sin(ang), 2, axis=-1)
    even = (jnp.arange(ATT_HEAD_DIM) % 2) == 0
    sa = jnp.where(even, -sin, 0.0)
    sb = jnp.where(even, 0.0, sin)
    two = lambda a: jnp.tile(a, (1, 2))
    return two(cos), two(sa), two(sb)


def _head_sum_matrix():
    a = jnp.arange(LANE) // ATT_HEAD_DIM
    return (a[:, None] == a[None, :]).astype(BF16)


def _head_mean(v, bd):
    hi = _bf(v)
    lo = _bf(v - hi.astype(F32))
    return (_dot(hi, bd, 1, 0) + _dot(lo, bd, 1, 0)) * (1.0 / ATT_HEAD_DIM)


def _qk_prep(p, gq, gk, tables, bd, tm=512):
    t = p.shape[0]
    tm = min(tm, t)
    cc, sa, sb = tables

    def body(p_ref, gq_ref, gk_ref, cc_ref, sa_ref, sb_ref, bd_ref, q_ref, k_ref):
        cc_, sa_, sb_, bd_ = cc_ref[...], sa_ref[...], sb_ref[...], bd_ref[...]
        low = lax.broadcasted_iota(jnp.int32, (tm, LANE), 1) < ATT_HEAD_DIM

        def normrope(xs, g):
            xn = xs * lax.rsqrt(_head_mean(xs * xs, bd_) + EPS) * g
            return xn * cc_ + pltpu.roll(xn, LANE - 1, 1) * sa_ + pltpu.roll(xn, 1, 1) * sb_

        for j in range(4):
            y = normrope(p_ref[:, j * LANE:(j + 1) * LANE], gq_ref[...]) * (ATT_HEAD_DIM ** -0.5)
            yr = pltpu.roll(y, ATT_HEAD_DIM, 1)
            if j // 2 == 0:
                h0, h1 = jnp.where(low, y, 0.0), jnp.where(low, yr, 0.0)
            else:
                h0, h1 = jnp.where(low, 0.0, yr), jnp.where(low, 0.0, y)
            q_ref[:, (2 * j) * LANE:(2 * j + 1) * LANE] = _bf(h0)
            q_ref[:, (2 * j + 1) * LANE:(2 * j + 2) * LANE] = _bf(h1)
        k_ref[...] = _bf(normrope(p_ref[:, ATT_Q_DIM:ATT_Q_DIM + LANE], gk_ref[...]))

    row = lambda width: pl.BlockSpec((tm, width), lambda i: (i, 0))
    return pl.pallas_call(
        body, name="qk_prep", grid=(t // tm,),
        in_specs=[row(ATT_Q_DIM + LANE), _full((1, LANE)), _full((1, LANE)), row(LANE), row(LANE), row(LANE),
                  _full((LANE, LANE))],
        out_specs=[row(ATT_HEADS * LANE), row(LANE)],
        out_shape=[jax.ShapeDtypeStruct((t, ATT_HEADS * LANE), BF16), jax.ShapeDtypeStruct((t, LANE), BF16)],
        compiler_params=_params(),
    )(p, gq, gk, cc, sa, sb, bd)


def _qk_prep_bwd(p, dq, dk, gq, gk, tables, bd, tm=512):
    t = p.shape[0]
    tm = min(tm, t)
    cc, sa, sb = tables

    def body(p_ref, dq_ref, dk_ref, gq_ref, gk_ref, cc_ref, sa_ref, sb_ref, bd_ref, dp_ref, dgq_ref, dgk_ref):
        @pl.when(pl.program_id(0) == 0)
        def _():
            dgq_ref[...] = jnp.zeros_like(dgq_ref)
            dgk_ref[...] = jnp.zeros_like(dgk_ref)

        cc_, sa_, sb_, bd_ = cc_ref[...], sa_ref[...], sb_ref[...], bd_ref[...]
        low = lax.broadcasted_iota(jnp.int32, (tm, LANE), 1) < ATT_HEAD_DIM

        def bwd(xs, g, dy):
            r = lax.rsqrt(_head_mean(xs * xs, bd_) + EPS)
            xh = xs * r
            dxn = dy * cc_ + pltpu.roll(dy * sa_, 1, 1) + pltpu.roll(dy * sb_, LANE - 1, 1)
            dg = jnp.sum(dxn * xh, axis=0, keepdims=True)
            tt = dxn * g
            return r * (tt - xh * _head_mean(tt * xh, bd_)), dg

        dgq = jnp.zeros((1, LANE), F32)
        for j in range(4):
            d0 = dq_ref[:, (2 * j) * LANE:(2 * j + 1) * LANE]
            d1 = dq_ref[:, (2 * j + 1) * LANE:(2 * j + 2) * LANE]
            if j // 2 == 0:
                dy = jnp.where(low, d0, pltpu.roll(d1, ATT_HEAD_DIM, 1))
            else:
                dy = jnp.where(low, pltpu.roll(d0, ATT_HEAD_DIM, 1), d1)
            dx, dg = bwd(p_ref[:, j * LANE:(j + 1) * LANE], gq_ref[...], dy * (ATT_HEAD_DIM ** -0.5))
            dp_ref[:, j * LANE:(j + 1) * LANE] = _bf(dx)
            dgq = dgq + dg
        dgq_ref[...] += dgq
        dx, dg = bwd(p_ref[:, ATT_Q_DIM:ATT_Q_DIM + LANE], gk_ref[...], dk_ref[...])
        dp_ref[:, ATT_Q_DIM:ATT_Q_DIM + LANE] = _bf(dx)
        dgk_ref[...] += dg

    row = lambda width: pl.BlockSpec((tm, width), lambda i: (i, 0))
    return pl.pallas_call(
        body, name="qk_prep_bwd", grid=(t // tm,),
        in_specs=[row(ATT_Q_DIM + LANE), row(ATT_HEADS * LANE), row(LANE), _full((1, LANE)), _full((1, LANE)),
                  row(LANE), row(LANE), row(LANE), _full((LANE, LANE))],
        out_specs=[row(ATT_Q_DIM + LANE), _full((1, LANE)), _full((1, LANE))],
        out_shape=[jax.ShapeDtypeStruct((t, ATT_Q_DIM + LANE), BF16), jax.ShapeDtypeStruct((1, LANE), F32),
                   jax.ShapeDtypeStruct((1, LANE), F32)],
        compiler_params=_params(),
    )(p, dq, dk, gq, gk, cc, sa, sb, bd)


def _attn_fwd(q, k, p, tq=256):
    t = k.shape[0]
    tq = min(tq, t)
    v_blk = (ATT_Q_DIM + ATT_KV_DIM) // LANE

    def body(q_ref, k_ref, v_ref, o_ref, o32_ref, lse_ref):
        k_ = k_ref[...]
        v = v_ref[...]
        lane_k = lax.broadcasted_iota(jnp.int32, (t, LANE), 1)
        lane_q = lax.broadcasted_iota(jnp.int32, (tq, LANE), 1)
        lowk, lowq = lane_k < ATT_HEAD_DIM, lane_q < ATT_HEAD_DIM
        ones_lane = (ATT_HEAD_DIM, 0)
        vm = (_bf(jnp.where(lowk, v, jnp.where(lane_k == ones_lane[0], 1.0, 0.0))),
              _bf(jnp.where(lowk, jnp.where(lane_k == ones_lane[1], 1.0, 0.0), v)))
        for j in range(4):
            kvh = j // 2
            acc = None
            for sub in range(2):
                h = 2 * j + sub
                s = _dot(q_ref[:, h * LANE:(h + 1) * LANE], k_, 1, 1)
                mx = jnp.max(s, axis=-1, keepdims=True)
                ov = _dot(jnp.exp(_bf(s - mx)), vm[kvh], 1, 0)
                l = jnp.sum(jnp.where(lane_q == ones_lane[kvh], ov, 0.0), axis=-1, keepdims=True)
                lse_ref[h] = mx + jnp.log(l)
                o = jnp.where(lowq if kvh == 0 else ~lowq, ov, 0.0) * (1.0 / l)
                if sub != kvh:
                    o = pltpu.roll(o, ATT_HEAD_DIM, 1)
                acc = o if acc is None else acc + o
            o32_ref[:, j * LANE:(j + 1) * LANE] = acc
            o_ref[:, j * LANE:(j + 1) * LANE] = _bf(acc)

    row = pl.BlockSpec((tq, ATT_Q_DIM), lambda i: (i, 0))
    return _call(
        body, (q, k, p), name="attn_fwd", grid=(t // tq,),
        in_specs=[pl.BlockSpec((tq, ATT_HEADS * LANE), lambda i: (i, 0)), _full((t, LANE)),
                  pl.BlockSpec((t, LANE), lambda i: (0, v_blk))],
        out_specs=[row, row, pl.BlockSpec((ATT_HEADS, tq, 1), lambda i: (0, i, 0))],
        out_shape=[jax.ShapeDtypeStruct((t, ATT_Q_DIM), BF16), jax.ShapeDtypeStruct((t, ATT_Q_DIM), F32),
                   jax.ShapeDtypeStruct((ATT_HEADS, t, 1), F32)])


def _attn_bwd(q, k, p, dcat, o32, lse, tq=256):
    t = k.shape[0]
    tq = min(tq, t)
    v_blk = (ATT_Q_DIM + ATT_KV_DIM) // LANE

    def body(q_ref, k_ref, v_ref, do_ref, o_ref, lse_ref, dq_ref, dk_ref, dv_ref):
        @pl.when(pl.program_id(0) == 0)
        def _():
            dk_ref[...] = jnp.zeros_like(dk_ref)
            dv_ref[...] = jnp.zeros_like(dv_ref)

        k_ = k_ref[...]
        vb = _bf(v_ref[...])
        lowq = lax.broadcasted_iota(jnp.int32, (tq, LANE), 1) < ATT_HEAD_DIM
        dk_acc = jnp.zeros((t, LANE), F32)
        dv_acc = jnp.zeros((t, LANE), F32)
        for j in range(4):
            kvh = j // 2
            dop = do_ref[:, j * LANE:(j + 1) * LANE]
            prod = dop * o_ref[:, j * LANE:(j + 1) * LANE]
            d_low = jnp.sum(jnp.where(lowq, prod, 0.0), axis=-1, keepdims=True)
            d_sub = (d_low, jnp.sum(prod, axis=-1, keepdims=True) - d_low)
            for sub in range(2):
                h = 2 * j + sub
                src = dop if sub == kvh else pltpu.roll(dop, ATT_HEAD_DIM, 1)
                do_h = _bf(jnp.where(lowq, src, 0.0) if kvh == 0 else jnp.where(lowq, 0.0, src))
                qh = q_ref[:, h * LANE:(h + 1) * LANE]
                pr = jnp.exp(_bf(_dot(qh, k_, 1, 1) - lse_ref[h]))
                ds = pr * _bf(_dot(do_h, vb, 1, 1) - d_sub[sub])
                dq_ref[:, h * LANE:(h + 1) * LANE] = _dot(ds, k_, 1, 0)
                dk_acc = dk_acc + _dot(ds, qh, 0, 0)
                dv_acc = dv_acc + _dot(pr, do_h, 0, 0)
        dk_ref[...] += dk_acc
        dv_ref[...] += dv_acc

    row = pl.BlockSpec((tq, ATT_Q_DIM), lambda i: (i, 0))
    return _call(
        body, (q, k, p, dcat, o32, lse), name="attn_bwd", grid=(t // tq,),
        in_specs=[pl.BlockSpec((tq, ATT_HEADS * LANE), lambda i: (i, 0)), _full((t, LANE)),
                  pl.BlockSpec((t, LANE), lambda i: (0, v_blk)), row, row,
                  pl.BlockSpec((ATT_HEADS, tq, 1), lambda i: (0, i, 0))],
        out_specs=[pl.BlockSpec((tq, ATT_HEADS * LANE), lambda i: (i, 0)), _full((t, LANE)), _full((t, LANE))],
        out_shape=[jax.ShapeDtypeStruct((t, ATT_HEADS * LANE), F32), jax.ShapeDtypeStruct((t, LANE), F32),
                   jax.ShapeDtypeStruct((t, LANE), F32)])


def _lower_bounds(a0, a1):
    def body(a0_ref, a1_ref, lb_ref):
        m = jnp.maximum(a0_ref[...], a1_ref[...])
        e0, e1 = jnp.exp(a0_ref[...] - m), jnp.exp(a1_ref[...] - m)
        lb_ref[...] = e0 / (e0 + e1)

    return pl.pallas_call(body, name="lower_bounds", out_shape=jax.ShapeDtypeStruct(a0.shape, F32))(a0, a1)


def _lower_bounds_bwd(a0, a1, dlb):
    def body(a0_ref, a1_ref, dlb_ref, d0_ref, d1_ref):
        m = jnp.maximum(a0_ref[...], a1_ref[...])
        e0, e1 = jnp.exp(a0_ref[...] - m), jnp.exp(a1_ref[...] - m)
        lb = e0 / (e0 + e1)
        d0 = dlb_ref[...] * lb * (1.0 - lb)
        d0_ref[...] = d0
        d1_ref[...] = -d0

    return pl.pallas_call(body, name="lower_bounds_bwd", out_shape=[jax.ShapeDtypeStruct(a0.shape, F32)] * 2)(a0, a1, dlb)


def _chunk_scan(x, pos, down):
    n = x.shape[0]
    s = 1
    while s < HG_CHUNK:
        if down:
            x = x + jnp.where(pos >= s, pltpu.roll(x, s, 0), 0.0)
        else:
            x = x + jnp.where(pos < HG_CHUNK - s, pltpu.roll(x, n - s, 0), 0.0)
        s *= 2
    return x


def _chunk_concat(x, cid, nc):
    return jnp.concatenate([_bf(jnp.where(cid == c, x, 0.0)) for c in range(nc)], axis=1)


def _chunk_pick(x, cid, nc):
    out = jnp.where(cid == 0, x[:, :LANE], 0.0)
    for c in range(1, nc):
        out = out + jnp.where(cid == c, x[:, c * LANE:(c + 1) * LANE], 0.0)
    return out


def _hgrn_gates(hq, z, lb):
    sq = _sigmoid(hq)
    sig = _sigmoid(z)
    f = lb + (1.0 - lb) * sig
    return hq * sq, sq, sig, f, jnp.log(f)


def _hgrn_local(q, f, g, v, pos, cid, amask, rev, nc):
    k = 1.0 - f
    ba = _chunk_scan(g, pos, not rev)
    bb = _chunk_scan(g, pos, rev)
    eq = 0.5 * (ba - bb + g)
    ex_q, ex_k, ex_i, ex_d = jnp.exp(eq), jnp.exp(-eq), jnp.exp(ba), jnp.exp(bb - g)
    qb, kb, qi, kd = q * ex_q, k * ex_k, q * ex_i, k * ex_d
    dvec = jnp.exp(ba + bb - g)
    a = jnp.where(amask, _dot(_bf(qb), _bf(kb), 1, 1), 0.0)
    kd_m = _chunk_concat(kd, cid, nc)
    qi_m = _chunk_concat(qi, cid, nc)
    ut_all = _dot(_bf(v), kd_m, 0, 0)
    return dict(k=k, ex_q=ex_q, ex_k=ex_k, ex_i=ex_i, ex_d=ex_d, qb=qb, kb=kb, qi=qi, kd=kd, dvec=dvec, a=a,
                kd_m=kd_m, qi_m=qi_m, ut_all=ut_all)


def _hgrn_masks(n, rev):
    row = lax.broadcasted_iota(jnp.int32, (n, LANE), 0)
    ti = lax.broadcasted_iota(jnp.int32, (n, n), 0)
    si = lax.broadcasted_iota(jnp.int32, (n, n), 1)
    tri = (si >= ti) if rev else (si <= ti)
    same = jnp.right_shift(ti, HG_CHUNK_LOG2) == jnp.right_shift(si, HG_CHUNK_LOG2)
    return jnp.bitwise_and(row, HG_CHUNK - 1), jnp.right_shift(row, HG_CHUNK_LOG2), same & tri


def _hgrn_specs(t, rev, bwd):
    n = min(HG_BLOCK_BWD if bwd else HG_BLOCK_FWD, t)
    nb = t // n
    if rev != bwd:
        blk = lambda i: nb - 1 - i
    else:
        blk = lambda i: i
    col = lambda base: pl.BlockSpec((n, 2 * LANE), lambda hp, i: (blk(i), base + hp))
    return n, nb, blk, col


def _hgrn_fwd(p, lb, rev):
    t = p.shape[0]
    n, nb, blk, col = _hgrn_specs(t, rev, False)
    nc = n // HG_CHUNK
    sub = n // min(HG_BLOCK_BWD, t)
    z_base = 7 if rev else 5

    def body(hq_ref, z_ref, hi_ref, lb_ref, o_ref, ssave_ref, st_scr):
        @pl.when(pl.program_id(1) == 0)
        def _():
            st_scr[...] = jnp.zeros_like(st_scr)

        pos, cid, amask = _hgrn_masks(n, rev)
        order = list(range(nc))[::-1] if rev else list(range(nc))
        for hh in range(2):
            sl = slice(hh * LANE, (hh + 1) * LANE)
            q, _, _, f, g = _hgrn_gates(hq_ref[:, sl], z_ref[:, sl], lb_ref[0:1, sl])
            v = hi_ref[:, sl]
            c_ = _hgrn_local(q, f, g, v, pos, cid, amask, rev, nc)
            st = st_scr[hh]
            cols = [None] * nc
            for c in order:
                cols[c] = st
                st = st * c_["dvec"][c * HG_CHUNK:c * HG_CHUNK + 1, :] + c_["ut_all"][:, c * LANE:(c + 1) * LANE]
            st_scr[hh] = st
            for s in range(sub):
                ssave_ref[sub - 1 - s if rev else s, sl, :] = cols[order[s * (nc // sub)]]
            st_all = _bf(jnp.concatenate(cols, axis=1))
            o_ref[:, sl] = _dot(_bf(c_["a"]), _bf(v), 1, 0) + _dot(c_["qi_m"], st_all, 1, 1)

    return _call(
        body, (p, p, p, lb), name="hgrn_fwd_rev" if rev else "hgrn_fwd", grid=(2, nb),
        in_specs=[col(3), col(z_base), col(9), pl.BlockSpec((1, 2 * LANE), lambda hp, i: (0, hp))],
        out_specs=[pl.BlockSpec((n, 2 * LANE), lambda hp, i: (blk(i), hp)),
                   pl.BlockSpec((sub, 2 * LANE, LANE), lambda hp, i: (blk(i), hp, 0))],
        out_shape=[jax.ShapeDtypeStruct((t, HG_DIM), F32), jax.ShapeDtypeStruct((nb * sub, HG_DIM, LANE), F32)],
        scratch_shapes=[pltpu.VMEM((2, LANE, LANE), F32)])


def _hgrn_bwd(p, lb, do, ssave, rev, after=()):
    t = p.shape[0]
    n, nb, blk, col = _hgrn_specs(t, rev, True)
    nc = n // HG_CHUNK
    z_base = 7 if rev else 5

    def body(hq_ref, z_ref, hi_ref, lb_ref, do_ref, ssave_ref, dhq_ref, dz_ref, dhi_ref, dlb_ref, dst_scr):
        @pl.when(pl.program_id(1) == 0)
        def _():
            dst_scr[...] = jnp.zeros_like(dst_scr)
            dlb_ref[...] = jnp.zeros_like(dlb_ref)

        pos, cid, amask = _hgrn_masks(n, rev)
        order = list(range(nc))[::-1] if rev else list(range(nc))
        for hh in range(2):
            sl = slice(hh * LANE, (hh + 1) * LANE)
            hq, lbv = hq_ref[:, sl], lb_ref[0:1, sl]
            q, sq, sig, f, g = _hgrn_gates(hq, z_ref[:, sl], lbv)
            v = hi_ref[:, sl]
            c_ = _hgrn_local(q, f, g, v, pos, cid, amask, rev, nc)
            dvec, ut_all = c_["dvec"], c_["ut_all"]
            drow = lambda c: dvec[c * HG_CHUNK:c * HG_CHUNK + 1, :]
            st = ssave_ref[0, sl, :]
            cols = [None] * nc
            for c in order:
                cols[c] = st
                st = st * drow(c) + ut_all[:, c * LANE:(c + 1) * LANE]
            dob, vb = _bf(do_ref[:, sl]), _bf(v)
            gt_all = _dot(dob, c_["qi_m"], 0, 0)
            dcur = dst_scr[hh]
            dnext = [None] * nc
            ddrow = [None] * nc
            for c in order[::-1]:
                dnext[c] = dcur
                ddrow[c] = jnp.sum(cols[c] * dcur, axis=0, keepdims=True) * drow(c)
                dcur = dcur * drow(c) + gt_all[:, c * LANE:(c + 1) * LANE]
            dst_scr[hh] = dcur
            dsn_all = _bf(jnp.concatenate(dnext, axis=1))
            st_all = _bf(jnp.concatenate(cols, axis=1))
            da = _bf(jnp.where(amask, _dot(dob, vb, 1, 1), 0.0))
            dv = _dot(_bf(c_["a"]), dob, 0, 0) + _dot(c_["kd_m"], dsn_all, 1, 1)
            dqb = _dot(da, _bf(c_["kb"]), 1, 0)
            dkb = _dot(da, _bf(c_["qb"]), 0, 0)
            dqi = _chunk_pick(_dot(dob, st_all, 1, 0), cid, nc)
            dkd = _chunk_pick(_dot(vb, dsn_all, 1, 0), cid, nc)
            dq = dqb * c_["ex_q"] + dqi * c_["ex_i"]
            dk = dkb * c_["ex_k"] + dkd * c_["ex_d"]
            e = dqb * c_["qb"] - dkb * c_["kb"] + dqi * c_["qi"]
            w = dkd * c_["kd"]
            dtot = jnp.where(cid == 0, ddrow[0], 0.0)
            for c in range(1, nc):
                dtot = dtot + jnp.where(cid == c, ddrow[c], 0.0)
            dg = _chunk_scan(e, pos, rev) + (_chunk_scan(w, pos, not rev) - w) + dtot
            df = dg / f - dk
            dz_ref[:, sl] = df * (1.0 - lbv) * sig * (1.0 - sig)
            dlb_ref[0:1, sl] += jnp.sum(df * (1.0 - sig), axis=0, keepdims=True)
            dhq_ref[:, sl] = dq * (sq * (1.0 + hq * (1.0 - sq)))
            dhi_ref[:, sl] = dv

    out = pl.BlockSpec((n, 2 * LANE), lambda hp, i: (blk(i), hp))
    return _call(
        body, (p, p, p, lb, do, ssave), name="hgrn_bwd_rev" if rev else "hgrn_bwd", grid=(2, nb),
        in_specs=[col(3), col(z_base), col(9), pl.BlockSpec((1, 2 * LANE), lambda hp, i: (0, hp)), out,
                  pl.BlockSpec((1, 2 * LANE, LANE), lambda hp, i: (blk(i), hp, 0))],
        out_specs=[out, out, out, pl.BlockSpec((1, 2 * LANE), lambda hp, i: (0, hp))],
        out_shape=[jax.ShapeDtypeStruct((t, HG_DIM), F32)] * 3 + [jax.ShapeDtypeStruct((1, HG_DIM), F32)],
        scratch_shapes=[pltpu.VMEM((2, LANE, LANE), F32)], after=after)


def _mix_out(att, of, ob, p, gout, w_out, g_post, x, tm=256):
    t, d = x.shape
    tm = min(tm, t)

    def body(att_ref, of_ref, ob_ref, hg0_ref, hg1_ref, gout_ref, w_ref, g_ref, x_ref, cat_ref, y_ref, x1_ref):
        cat_ref[:, :ATT_Q_DIM] = att_ref[...]
        for h in range(HG_HEADS):
            sl = slice(h * LANE, (h + 1) * LANE)
            hg_ref = hg0_ref if h < 2 else hg1_ref
            hg = hg_ref[:, (h % 2) * LANE:(h % 2 + 1) * LANE]
            nrm = _rms_fwd(of_ref[:, sl] + ob_ref[:, sl], gout_ref[...])
            cat_ref[:, ATT_Q_DIM + h * LANE:ATT_Q_DIM + (h + 1) * LANE] = _bf(nrm * hg * _sigmoid(hg))
        y = _dot(cat_ref[...], w_ref[...], 1, 0)
        y_ref[...] = y
        x1_ref[...] = x_ref[...] + _rms_fwd(y, g_ref[...])

    row = lambda width: pl.BlockSpec((tm, width), lambda i: (i, 0))
    return _call(
        body, (att, of, ob, p, p, gout, w_out, g_post, x), name="mix_out", grid=(t // tm,),
        in_specs=[row(ATT_Q_DIM), row(HG_DIM), row(HG_DIM), pl.BlockSpec((tm, 2 * LANE), lambda i: (i, 11)),
                  pl.BlockSpec((tm, 2 * LANE), lambda i: (i, 12)), _full((1, LANE)), _full((d, d)), _full((1, d)), row(d)],
        out_specs=[row(d), row(d), row(d)],
        out_shape=[jax.ShapeDtypeStruct((t, d), BF16), jax.ShapeDtypeStruct((t, d), F32), jax.ShapeDtypeStruct((t, d), F32)])


def _rec_bwd(dcat, of, ob, p, gout, tm=256):
    t = of.shape[0]
    tm = min(tm, t)

    def body(dc_ref, of_ref, ob_ref, hg0_ref, hg1_ref, gout_ref, do_ref, dhg_ref, dgo_ref):
        @pl.when(pl.program_id(0) == 0)
        def _():
            dgo_ref[...] = jnp.zeros_like(dgo_ref)

        dgo = jnp.zeros((1, LANE), F32)
        for h in range(HG_HEADS):
            sl = slice(h * LANE, (h + 1) * LANE)
            hg_ref = hg0_ref if h < 2 else hg1_ref
            hg = hg_ref[:, (h % 2) * LANE:(h % 2 + 1) * LANE]
            o = of_ref[:, sl] + ob_ref[:, sl]
            sg = _sigmoid(hg)
            drec = dc_ref[:, sl]
            dhg_ref[:, sl] = drec * _rms_fwd(o, gout_ref[...]) * (sg * (1.0 + hg * (1.0 - sg)))
            do, dg = _rms_bwd(drec * hg * sg, o, gout_ref[...])
            do_ref[:, sl] = do
            dgo = dgo + dg
        dgo_ref[...] += dgo

    row = lambda width: pl.BlockSpec((tm, width), lambda i: (i, 0))
    return pl.pallas_call(
        body, name="rec_bwd", grid=(t // tm,),
        in_specs=[pl.BlockSpec((tm, HG_DIM), lambda i: (i, 1)), row(HG_DIM), row(HG_DIM),
                  pl.BlockSpec((tm, 2 * LANE), lambda i: (i, 11)), pl.BlockSpec((tm, 2 * LANE), lambda i: (i, 12)),
                  _full((1, LANE))],
        out_specs=[row(HG_DIM), row(HG_DIM), _full((1, LANE))],
        out_shape=[jax.ShapeDtypeStruct((t, HG_DIM), F32), jax.ShapeDtypeStruct((t, HG_DIM), F32),
                   jax.ShapeDtypeStruct((1, LANE), F32)],
        compiler_params=_params(),
    )(dcat, of, ob, p, p, gout)


def _assemble_dp(dp_qk, dv, dhq_f, dhq_b, dz_f, dz_b, dhi_f, dhi_b, dhg, tm=512):
    t = dv.shape[0]
    tm = min(tm, t)
    qk_w = ATT_Q_DIM + LANE

    def body(qk_ref, dv_ref, hqf_ref, hqb_ref, zf_ref, zb_ref, hif_ref, hib_ref, hg_ref, dp_ref):
        o = 0
        dp_ref[:, o:o + qk_w] = qk_ref[...]
        o += qk_w
        dp_ref[:, o:o + LANE] = _bf(dv_ref[...])
        o += LANE
        for val in (hqf_ref[...] + hqb_ref[...], zf_ref[...], zb_ref[...], hif_ref[...] + hib_ref[...], hg_ref[...]):
            dp_ref[:, o:o + HG_DIM] = _bf(val)
            o += HG_DIM

    row = lambda width: pl.BlockSpec((tm, width), lambda i: (i, 0))
    return pl.pallas_call(
        body, name="assemble_dp", grid=(t // tm,),
        in_specs=[row(qk_w), row(LANE)] + [row(HG_DIM)] * 7,
        out_specs=row(N_IN),
        out_shape=jax.ShapeDtypeStruct((t, N_IN), BF16),
        compiler_params=_params(),
    )(dp_qk, dv, dhq_f, dhq_b, dz_f, dz_b, dhi_f, dhi_b, dhg)


def _xattn_fwd(q, kv, tq=512):
    t, d = q.shape
    tq = min(tq, t)
    nm = kv.shape[0]

    def body(q_ref, kv_ref, o_ref):
        for h in range(X_HEADS):
            sl = slice(h * X_HEAD_DIM, (h + 1) * X_HEAD_DIM)
            s = _dot(_bf(q_ref[:, sl]), _bf(kv_ref[:, sl]), 1, 1) * (X_HEAD_DIM ** -0.5)
            e = jnp.exp(s - jnp.max(s, axis=-1, keepdims=True))
            pr = e * (1.0 / jnp.sum(e, axis=-1, keepdims=True))
            o_ref[:, sl] = _bf(_dot(_bf(pr), _bf(kv_ref[:, d + h * X_HEAD_DIM:d + (h + 1) * X_HEAD_DIM]), 1, 0))

    return pl.pallas_call(
        body, name="xattn_fwd", grid=(t // tq,),
        in_specs=[pl.BlockSpec((tq, d), lambda i: (i, 0)), _full((nm, 2 * d))],
        out_specs=pl.BlockSpec((tq, d), lambda i: (i, 0)),
        out_shape=jax.ShapeDtypeStruct((t, d), BF16),
        compiler_params=_params(),
    )(q, kv)


def _xattn_bwd(q, kv, do, tq=512):
    t, d = q.shape
    tq = min(tq, t)
    nm = kv.shape[0]

    def body(q_ref, kv_ref, do_ref, dq_ref, dkv_ref):
        @pl.when(pl.program_id(0) == 0)
        def _():
            dkv_ref[...] = jnp.zeros_like(dkv_ref)

        for h in range(X_HEADS):
            sl = slice(h * X_HEAD_DIM, (h + 1) * X_HEAD_DIM)
            slv = slice(d + h * X_HEAD_DIM, d + (h + 1) * X_HEAD_DIM)
            qb, kb, vb, dob = _bf(q_ref[:, sl]), _bf(kv_ref[:, sl]), _bf(kv_ref[:, slv]), _bf(do_ref[:, sl])
            s = _dot(qb, kb, 1, 1) * (X_HEAD_DIM ** -0.5)
            e = jnp.exp(s - jnp.max(s, axis=-1, keepdims=True))
            pr = e * (1.0 / jnp.sum(e, axis=-1, keepdims=True))
            dpr = _dot(dob, vb, 1, 1)
            ds = _bf(pr * (dpr - jnp.sum(pr * dpr, axis=-1, keepdims=True)) * (X_HEAD_DIM ** -0.5))
            dq_ref[:, sl] = _bf(_dot(ds, kb, 1, 0))
            dkv_ref[:, sl] += _dot(ds, qb, 0, 0)
            dkv_ref[:, slv] += _dot(_bf(pr), dob, 0, 0)

    return pl.pallas_call(
        body, name="xattn_bwd", grid=(t // tq,),
        in_specs=[pl.BlockSpec((tq, d), lambda i: (i, 0)), _full((nm, 2 * d)), pl.BlockSpec((tq, d), lambda i: (i, 0))],
        out_specs=[pl.BlockSpec((tq, d), lambda i: (i, 0)), _full((nm, 2 * d))],
        out_shape=[jax.ShapeDtypeStruct((t, d), BF16), jax.ShapeDtypeStruct((nm, 2 * d), F32)],
        compiler_params=_params(),
    )(q, kv, do)


CONV_TN = 256


def _shift_rows(u, row, t, delta):
    if delta < 0:
        return jnp.where(row == 0, 0.0, pltpu.roll(u, 1, 0))
    return jnp.where(row == t - 1, 0.0, pltpu.roll(u, t - 1, 0))


def _conv_gate_fwd(u, conv_w, conv_b):
    t = u.shape[0]
    nt = D_FF // CONV_TN

    def body(ug_ref, uv_ref, wg_ref, wv_ref, bg_ref, bv_ref, a_ref):
        row = lax.broadcasted_iota(jnp.int32, (t, CONV_TN), 0)

        def conv(u_ref, w_ref, b_ref):
            uu = u_ref[...]
            return (b_ref[...] + _shift_rows(uu, row, t, -1) * w_ref[0:1, :] + uu * w_ref[1:2, :]
                    + _shift_rows(uu, row, t, 1) * w_ref[2:3, :])

        gate = conv(ug_ref, wg_ref, bg_ref)
        a_ref[...] = _bf(gate * _sigmoid(gate) * conv(uv_ref, wv_ref, bv_ref))

    col = lambda rows, off: pl.BlockSpec((rows, CONV_TN), lambda j: (0, j + off))
    return pl.pallas_call(
        body, name="conv_gate_fwd", grid=(nt,),
        in_specs=[col(t, 0), col(t, nt), col(3, 0), col(3, nt), col(1, 0), col(1, nt)],
        out_specs=col(t, 0),
        out_shape=jax.ShapeDtypeStruct((t, D_FF), BF16),
        compiler_params=_params(),
    )(u, u, conv_w, conv_w, conv_b, conv_b)


def _conv_gate_bwd(u, conv_w, conv_b, da, after=()):
    t = u.shape[0]
    nt = D_FF // CONV_TN

    def body(ug_ref, uv_ref, wg_ref, wv_ref, bg_ref, bv_ref, da_ref, dug_ref, dwg_ref, dbg_ref, duv_ref, dwv_ref, dbv_ref):
        row = lax.broadcasted_iota(jnp.int32, (t, CONV_TN), 0)
        ug, uv = ug_ref[...], uv_ref[...]
        ug_m, ug_p = _shift_rows(ug, row, t, -1), _shift_rows(ug, row, t, 1)
        uv_m, uv_p = _shift_rows(uv, row, t, -1), _shift_rows(uv, row, t, 1)
        gate = bg_ref[...] + ug_m * wg_ref[0:1, :] + ug * wg_ref[1:2, :] + ug_p * wg_ref[2:3, :]
        val = bv_ref[...] + uv_m * wv_ref[0:1, :] + uv * wv_ref[1:2, :] + uv_p * wv_ref[2:3, :]
        sg = _sigmoid(gate)
        da_ = da_ref[...].astype(F32)

        def emit(dacc, um, uu, up, w_ref, du_ref, dw_ref, db_ref):
            du_ref[...] = _bf(_shift_rows(dacc, row, t, 1) * w_ref[0:1, :] + dacc * w_ref[1:2, :]
                              + _shift_rows(dacc, row, t, -1) * w_ref[2:3, :])
            dw_ref[0:1, :] = jnp.sum(dacc * um, axis=0, keepdims=True)
            dw_ref[1:2, :] = jnp.sum(dacc * uu, axis=0, keepdims=True)
            dw_ref[2:3, :] = jnp.sum(dacc * up, axis=0, keepdims=True)
            db_ref[...] = jnp.sum(dacc, axis=0, keepdims=True)

        emit(da_ * val * (sg * (1.0 + gate * (1.0 - sg))), ug_m, ug, ug_p, wg_ref, dug_ref, dwg_ref, dbg_ref)
        emit(da_ * gate * sg, uv_m, uv, uv_p, wv_ref, duv_ref, dwv_ref, dbv_ref)

    col = lambda rows, off: pl.BlockSpec((rows, CONV_TN), lambda j: (0, j + off))
    half_shapes = [jax.ShapeDtypeStruct((t, D_FF), BF16), jax.ShapeDtypeStruct((3, D_FF), F32),
                   jax.ShapeDtypeStruct((1, D_FF), F32)]
    outs = _call(
        body, (u, u, conv_w, conv_w, conv_b, conv_b, da), name="conv_gate_bwd", grid=(nt,),
        in_specs=[col(t, 0), col(t, nt), col(3, 0), col(3, nt), col(1, 0), col(1, nt), col(t, 0)],
        out_specs=[col(t, 0), col(3, 0), col(1, 0)] * 2, out_shape=half_shapes * 2, after=after)
    return outs[:3], outs[3:]


def _row_tile(r, cap):
    best = r
    for cand in range(16, cap + 1, 16):
        if r % cand == 0:
            best = cand
    return best


def _sum_parts(own, got, me, name, tr=256):
    _, r, c = got.shape
    tr = _row_tile(r, tr)

    def body(me_ref, own_ref, got_ref, o_ref):
        mine = own_ref[...].astype(F32)
        acc = None
        for i in range(N_DEV):
            term = jnp.where(me_ref[0] == i, mine, got_ref[i].astype(F32))
            acc = term if acc is None else acc + term
        o_ref[...] = acc

    return pl.pallas_call(
        body, name=name,
        grid_spec=pltpu.PrefetchScalarGridSpec(
            num_scalar_prefetch=1, grid=(r // tr,),
            in_specs=[pl.BlockSpec((None, tr, c), lambda i, me_ref: (me_ref[0], i, 0)),
                      pl.BlockSpec((N_DEV, tr, c), lambda i, me_ref: (0, i, 0))],
            out_specs=pl.BlockSpec((tr, c), lambda i, me_ref: (i, 0))),
        out_shape=jax.ShapeDtypeStruct((r, c), F32),
        compiler_params=_params(),
    )(me, own, got)


def _adamw_math(w_, g_, m_, v_):
    m_ = ADAM_B1 * m_ + (1.0 - ADAM_B1) * g_
    v_ = ADAM_B2 * v_ + (1.0 - ADAM_B2) * (g_ * g_)
    m_hat = m_ / (1.0 - ADAM_B1 ** ADAM_STEP)
    v_hat = v_ / (1.0 - ADAM_B2 ** ADAM_STEP)
    return -ADAM_LR * (m_hat / (jnp.sqrt(v_hat) + ADAM_EPS) + ADAM_WD * w_), m_, v_


def _sum_adamw(own, got, me, w, m, v, name, tr=256):
    _, r, c = got.shape
    tr = _row_tile(r, tr)

    def body(me_ref, own_ref, got_ref, w_ref, m_ref, v_ref, g_ref, d_ref, nm_ref, nv_ref):
        mine = own_ref[...].astype(F32)
        acc = None
        for i in range(N_DEV):
            term = jnp.where(me_ref[0] == i, mine, got_ref[i].astype(F32))
            acc = term if acc is None else acc + term
        g_ref[...] = acc
        d_ref[...], nm_ref[...], nv_ref[...] = _adamw_math(w_ref[...], acc, m_ref[...], v_ref[...])

    blk = pl.BlockSpec((tr, c), lambda i, me_ref: (i, 0))
    return pl.pallas_call(
        body, name=name,
        grid_spec=pltpu.PrefetchScalarGridSpec(
            num_scalar_prefetch=1, grid=(r // tr,),
            in_specs=[pl.BlockSpec((None, tr, c), lambda i, me_ref: (me_ref[0], i, 0)),
                      pl.BlockSpec((N_DEV, tr, c), lambda i, me_ref: (0, i, 0)), blk, blk, blk],
            out_specs=[blk] * 4),
        out_shape=[jax.ShapeDtypeStruct((r, c), F32)] * 4,
        compiler_params=_params(),
    )(me, own, got, w, m, v)


def _adamw(w, g, m, v, name, tr=256):
    r, c = w.shape
    tr = _row_tile(r, tr)

    def body(w_ref, g_ref, m_ref, v_ref, d_ref, nm_ref, nv_ref):
        d_ref[...], nm_ref[...], nv_ref[...] = _adamw_math(w_ref[...], g_ref[...], m_ref[...], v_ref[...])

    blk = pl.BlockSpec((tr, c), lambda i: (i, 0))
    return pl.pallas_call(
        body, name=name, grid=(r // tr,),
        in_specs=[blk] * 4, out_specs=[blk] * 3,
        out_shape=[jax.ShapeDtypeStruct((r, c), F32)] * 3,
        compiler_params=_params(),
    )(w, g, m, v)


def _mesh_pos():
    return lax.axis_index("x"), lax.axis_index("y"), lax.axis_index("c")


def _dev_index(px, py, pc):
    return 4 * px + 2 * py + pc


class _Gather:
    def __init__(self, arrs):
        self.arrs = list(arrs)
        n = len(self.arrs)
        self.out_shape = [jax.ShapeDtypeStruct((N_DEV,) + a.shape, a.dtype) for a in self.arrs]
        self.sems = [pltpu.SemaphoreType.DMA((7, n)), pltpu.SemaphoreType.DMA((7, n)), pltpu.SemaphoreType.DMA((n,))]

    def _ctx(self, ins, outs, sems):
        send_sems, recv_sems, local_sems = sems
        x, y, c = _mesh_pos()
        chips = [(1 - x, y), (x, 1 - y), (1 - x, 1 - y)]

        def copy(k, a, block, to, src=None):
            dst = outs[a].at[_dev_index(*block)]
            return pltpu.make_async_remote_copy(
                src_ref=dst if src is None else src, dst_ref=dst, send_sem=send_sems.at[k, a], recv_sem=recv_sems.at[k, a],
                device_id=to, device_id_type=MESH_T)

        n = len(ins)
        me, sibling = (x, y, c), (x, y, 1 - c)
        mine = [pltpu.make_async_copy(ins[a], outs[a].at[_dev_index(*me)], local_sems.at[a]) for a in range(n)]
        first = [copy(0, a, me, sibling, src=ins[a]) for a in range(n)]
        first += [copy(1 + j, a, me, (*chip, c), src=ins[a]) for j, chip in enumerate(chips) for a in range(n)]
        passed = [copy(4 + j, a, (*chip, c), sibling) for j, chip in enumerate(chips) for a in range(n)]
        return n, c, me, sibling, chips, copy, mine, first, passed

    def start(self, ins, outs, sems):
        _, _, _, _, _, _, mine, first, _ = self._ctx(ins, outs, sems)
        for cp in mine + first:
            cp.start()

    def forward(self, ins, outs, sems):
        n, c, me, _, chips, copy, _, _, passed = self._ctx(ins, outs, sems)
        for j, chip in enumerate(chips):
            for a in range(n):
                copy(1 + j, a, (*chip, c), me).wait_recv()
                passed[j * n + a].start()

    def finish(self, ins, outs, sems):
        n, c, me, sibling, chips, copy, mine, first, passed = self._ctx(ins, outs, sems)
        for a in range(n):
            copy(0, a, sibling, me).wait_recv()
        for j, chip in enumerate(chips):
            for a in range(n):
                copy(4 + j, a, (*chip, 1 - c), me).wait_recv()
        for cp in first + passed:
            cp.wait_send()
        for cp in mine:
            cp.wait()


def _comm_alone(comm, name):
    n = len(comm.arrs)

    def body(*refs):
        ins, outs, sems = refs[:n], refs[n:2 * n], refs[2 * n:]
        comm.start(ins, outs, sems)
        if comm.forward is not None:
            comm.forward(ins, outs, sems)
        comm.finish(ins, outs, sems)

    any_spec = pl.BlockSpec(memory_space=pl.ANY)
    return pl.pallas_call(body, name=name, in_specs=[any_spec] * n, out_specs=[any_spec] * n, out_shape=comm.out_shape,
                          scratch_shapes=comm.sems)(*comm.arrs)


def _peers(x, y, c):
    out = []
    for k in range(1, N_DEV):
        pos = (1 - x if k & 4 else x, 1 - y if k & 2 else y, 1 - c if k & 1 else c)
        out.append((k, pos, _dev_index(*pos)))
    return out


def _exchange_begin(arrs, tag):
    n = len(arrs)
    lands = [lax.empty(a.shape, a.dtype) for a in arrs]

    def start_body(*refs):
        ins, land = refs[:n], refs[n:2 * n]
        send_sems, recv_sems, token = refs[2 * n], refs[2 * n + 1], refs[-1]
        x, y, c = _mesh_pos()
        me_i = _dev_index(x, y, c)
        for k, pos, peer_i in _peers(x, y, c):
            for a in range(n):
                pltpu.make_async_remote_copy(
                    src_ref=ins[a].at[peer_i], dst_ref=land[a].at[me_i], send_sem=send_sems.at[(k - 1) * n + a],
                    recv_sem=recv_sems.at[(k - 1) * n + a], device_id=pos, device_id_type=MESH_T).start()
        token[...] = jnp.zeros_like(token)

    hbm = pl.BlockSpec(memory_space=pltpu.HBM)
    sem = pl.BlockSpec(memory_space=pltpu.SEMAPHORE)
    thru = [pltpu.HBM(a.shape, a.dtype) for a in arrs]
    outs = pl.pallas_call(
        start_body, name="exchange_start_" + tag,
        out_shape=[pltpu.SemaphoreType.DMA((7 * n,)), pltpu.SemaphoreType.DMA((7 * n,))] + thru + thru
        + [jax.ShapeDtypeStruct((8, LANE), F32)],
        in_specs=[hbm] * (2 * n), out_specs=[sem, sem] + [hbm] * (2 * n) + [pl.BlockSpec(memory_space=pltpu.VMEM)],
        input_output_aliases={i: 2 + i for i in range(2 * n)},
        compiler_params=pltpu.CompilerParams(has_side_effects=pltpu.SideEffectType.DATAFLOW_SIDE_EFFECTING),
    )(*[pltpu.with_memory_space_constraint(a, pltpu.HBM) for a in list(arrs) + list(lands)])
    return (tag, n, outs[0], outs[1], outs[2:2 + n], outs[2 + n:2 + 2 * n]), outs[-1]


def _exchange_end(handle, after):
    tag, n, send_sems, recv_sems, srcs, lands = handle

    def body(*refs):
        ins, land = refs[:n], refs[n:2 * n]
        send_sems_, recv_sems_ = refs[2 * n], refs[2 * n + 1]
        x, y, c = _mesh_pos()
        me_i = _dev_index(x, y, c)
        for k, pos, peer_i in _peers(x, y, c):
            for a in range(n):
                cp = pltpu.make_async_remote_copy(
                    src_ref=ins[a].at[peer_i], dst_ref=land[a].at[me_i], send_sem=send_sems_.at[(k - 1) * n + a],
                    recv_sem=recv_sems_.at[(k - 1) * n + a], device_id=pos, device_id_type=MESH_T)
                cp.wait_send()
                cp.wait_recv()

    hbm = pl.BlockSpec(memory_space=pltpu.HBM)
    sem = pl.BlockSpec(memory_space=pltpu.SEMAPHORE)
    outs = pl.pallas_call(
        body, name="exchange_end_" + tag, out_shape=[pltpu.HBM(a.shape, a.dtype) for a in list(srcs) + list(lands)],
        in_specs=[hbm] * (2 * n) + [sem, sem, pl.BlockSpec(memory_space=pl.ANY)], out_specs=[hbm] * (2 * n),
        input_output_aliases={i: i for i in range(2 * n)},
        compiler_params=pltpu.CompilerParams(has_side_effects=pltpu.SideEffectType.DATAFLOW_SIDE_EFFECTING),
    )(*srcs, *lands, send_sems, recv_sems, after)
    return list(zip(outs[:n], outs[n:]))


def _gather_begin(arrs, tag):
    n = len(arrs)
    me = _dev_index(*_mesh_pos())
    lands = [lax.dynamic_update_slice(lax.empty((N_DEV,) + a.shape, a.dtype), a[None], (me,) + (0,) * a.ndim) for a in arrs]

    def start_body(*refs):
        ins, land = refs[:n], refs[n:2 * n]
        send_sems, recv_sems, token = refs[2 * n], refs[2 * n + 1], refs[-1]
        x, y, c = _mesh_pos()
        me_i = _dev_index(x, y, c)
        for a in range(n):
            for k, pos, _ in _peers(x, y, c):
                pltpu.make_async_remote_copy(
                    src_ref=ins[a], dst_ref=land[a].at[me_i], send_sem=send_sems.at[(k - 1) * n + a],
                    recv_sem=recv_sems.at[(k - 1) * n + a], device_id=pos, device_id_type=MESH_T).start()
        token[...] = jnp.zeros_like(token)

    hbm = pl.BlockSpec(memory_space=pltpu.HBM)
    sem = pl.BlockSpec(memory_space=pltpu.SEMAPHORE)
    thru = [pltpu.HBM(a.shape, a.dtype) for a in list(arrs) + lands]
    outs = pl.pallas_call(
        start_body, name="gather_start_" + tag,
        out_shape=[pltpu.SemaphoreType.DMA((7 * n,)), pltpu.SemaphoreType.DMA((7 * n,))] + thru
        + [jax.ShapeDtypeStruct((8, LANE), F32)],
        in_specs=[hbm] * (2 * n), out_specs=[sem, sem] + [hbm] * (2 * n) + [pl.BlockSpec(memory_space=pltpu.VMEM)],
        input_output_aliases={i: 2 + i for i in range(2 * n)},
        compiler_params=pltpu.CompilerParams(has_side_effects=pltpu.SideEffectType.DATAFLOW_SIDE_EFFECTING),
    )(*[pltpu.with_memory_space_constraint(a, pltpu.HBM) for a in list(arrs) + lands])
    return (tag, n, outs[0], outs[1], outs[2:2 + n], outs[2 + n:2 + 2 * n]), outs[-1]


def _gather_end(handle, which, after):
    tag, n, send_sems, recv_sems, srcs, lands = handle
    m = len(which)

    def body(*refs):
        ins, land = refs[:m], refs[m:2 * m]
        send_sems_, recv_sems_ = refs[2 * m], refs[2 * m + 1]
        x, y, c = _mesh_pos()
        me_i = _dev_index(x, y, c)
        for j, a in enumerate(which):
            for k, pos, _ in _peers(x, y, c):
                cp = pltpu.make_async_remote_copy(
                    src_ref=ins[j], dst_ref=land[j].at[me_i], send_sem=send_sems_.at[(k - 1) * n + a],
                    recv_sem=recv_sems_.at[(k - 1) * n + a], device_id=pos, device_id_type=MESH_T)
                cp.wait_send()
                cp.wait_recv()

    hbm = pl.BlockSpec(memory_space=pltpu.HBM)
    sem = pl.BlockSpec(memory_space=pltpu.SEMAPHORE)
    ops = [srcs[a] for a in which] + [lands[a] for a in which]
    outs = pl.pallas_call(
        body, name="gather_end_%s_%s" % (tag, "_".join(str(a) for a in which)),
        out_shape=[pltpu.HBM(a.shape, a.dtype) for a in ops],
        in_specs=[hbm] * (2 * m) + [sem, sem, pl.BlockSpec(memory_space=pl.ANY)], out_specs=[hbm] * (2 * m),
        input_output_aliases={i: i for i in range(2 * m)},
        compiler_params=pltpu.CompilerParams(has_side_effects=pltpu.SideEffectType.DATAFLOW_SIDE_EFFECTING),
    )(*ops, send_sems, recv_sems, after)
    return list(outs[m:])


def _call(body, operands, *, name, grid, in_specs, out_specs, out_shape, scratch_shapes=(), after=()):
    ni, nd = len(in_specs), len(after)

    def ordered(*refs):
        body(*refs[:ni], *refs[ni + nd:])

    outs = pl.pallas_call(
        ordered, name=name, grid=grid, in_specs=list(in_specs) + [pl.BlockSpec(memory_space=pl.ANY)] * nd,
        out_specs=out_specs, out_shape=out_shape, scratch_shapes=list(scratch_shapes), compiler_params=_params(),
    )(*operands, *after)
    return list(outs)


def _rows(a):
    return a.reshape(-1, a.shape[-1])


def _slots(a):
    return a.reshape(N_DEV, -1, a.shape[-1])


def _local_step(x, mem, tgt, w, wire):
    tables = _rope_tables(x.shape[0])
    bd = _head_sum_matrix()
    two = lambda g: jnp.tile(g, (1, 2))
    gq2, gk2 = two(w["q_norm_g"]), two(w["k_norm_g"])

    w_in_t, hg_lb, conv_w = _comm_alone(_Gather([wire["w_in"], w["hg_lb"].reshape(4, -1), w["conv_w"][0]]), "gather_w_in")
    w_in_t = _rows(w_in_t)
    hg_lb = jnp.transpose(hg_lb.reshape(N_DEV, 2, 2, -1), (1, 2, 0, 3)).reshape(2, 2, HG_DIM)
    conv_w = jnp.transpose(conv_w, (1, 0, 2)).reshape(3, 2 * D_FF)
    lb_a0, lb_a1 = hg_lb[:, 0, :], hg_lb[:, 1, :]

    order = ("w_out", "w_xq", "w_xkv", "w_xo", "w_up", "w_down")
    w_in_t, *later = lax.optimization_barrier((w_in_t, *[wire[n] for n in order]))
    fetch, started = _gather_begin(later, "w")
    take = lambda n, after: _rows(_gather_end(fetch, [order.index(n)], after)[0])

    h1, p = _norm_mm(x, w["pre_mix_g"], w_in_t, True, N_IN, "in_proj", after=[started])
    qr, kr = _qk_prep(p, gq2, gk2, tables, bd)
    att, att32, lse = _attn_fwd(qr, kr, p)
    lb = _lower_bounds(lb_a0, lb_a1)
    of, s_f = _hgrn_fwd(p, lb[0:1], False)
    ob, s_b = _hgrn_fwd(p, lb[1:2], True)
    w_out = take("w_out", ob)
    cat, mixed, x1 = _mix_out(att, of, ob, p, w["hg_out_norm_g"], w_out, w["post_mix_g"], x)

    w_xq = take("w_xq", x1)
    h2, q2 = _norm_mm(x1, w["pre_x_g"], w_xq, False, 1024, "xq_proj")
    w_xkv_t = take("w_xkv", q2)
    mn, kv = _norm_mm(mem, w["mem_norm_g"], w_xkv_t, True, 2 * D_MODEL, "xkv_proj")
    o2 = _xattn_fwd(q2, kv)
    w_xo = take("w_xo", o2)
    y2, x2 = _mm_postnorm_res(o2, w_xo, w["post_x_g"], x1, "xo_proj")

    w_up_t = take("w_up", x2)
    h3, u = _norm_mm(x2, w["pre_ffn_g"], w_up_t, True, 2 * D_FF, "up_proj", tm=256)
    a = _conv_gate_fwd(u, conv_w, w["conv_b"])
    w_down = take("w_down", a)
    y3, dx3, loss = _mm_postnorm_res_loss(a, w_down, w["post_ffn_g"], x2, tgt, "down_proj")

    g, pending = {}, {}
    dy3, da, g["post_ffn_g"] = _postnorm_bwd_mm(dx3, y3, w["post_ffn_g"], w_down, "down_bwd", BF16)
    pending["down"], started = _exchange_begin([_slots(_dw(a, dy3, "dw_down"))], "down")
    (du_g, dcw_g, dcb_g), (du_v, dcw_v, dcb_v) = _conv_gate_bwd(u, conv_w, w["conv_b"], da, after=[started])
    g["conv_w"] = jnp.concatenate([dcw_g, dcw_v], axis=1)
    g["conv_b"] = jnp.concatenate([dcb_g, dcb_v], axis=1)
    pending["up"], started = _exchange_begin([_slots(_dw([du_g, du_v], h3, "dw_up"))], "up")
    dx2, g["pre_ffn_g"] = _mm_prenorm_bwd([du_g, du_v], w_up_t, True, x2, w["pre_ffn_g"], dx3, "up_bwd", after=[started], tm=256)

    dy2, do2, g["post_x_g"] = _postnorm_bwd_mm(dx2, y2, w["post_x_g"], w_xo, "xo_bwd", BF16)
    dw_xo = _dw(o2, dy2, "dw_xo")
    dq2, dkv = _xattn_bwd(q2, kv, do2)
    dw_xq = _dw(h2, dq2, "dw_xq")
    dkvb = _bf(dkv)
    dw_xkv = _dw(dkvb, mn, "dw_xkv")
    pending["x"], started = _exchange_begin([_slots(dw_xo), _slots(dw_xq), _slots(dw_xkv)], "x")
    _, g["mem_norm_g"] = _mm_prenorm_bwd(dkvb, w_xkv_t, True, mem, w["mem_norm_g"], jnp.zeros_like(mem), "xkv_bwd")
    dx1, g["pre_x_g"] = _mm_prenorm_bwd(dq2, w_xq, False, x1, w["pre_x_g"], dx2, "xq_bwd", after=[started])

    dmixed, dcat, g["post_mix_g"] = _postnorm_bwd_mm(dx1, mixed, w["post_mix_g"], w_out, "out_bwd")
    pending["out"], started = _exchange_begin([_slots(_dw(cat, dmixed, "dw_out"))], "out")
    do, dhg, g["hg_out_norm_g"] = _rec_bwd(dcat, of, ob, p, w["hg_out_norm_g"])
    dhq_f, dz_f, dhi_f, dlb_f = _hgrn_bwd(p, lb[0:1], do, s_f, False, after=[started])
    dhq_b, dz_b, dhi_b, dlb_b = _hgrn_bwd(p, lb[1:2], do, s_b, True)
    d_a0, d_a1 = _lower_bounds_bwd(lb_a0, lb_a1, jnp.concatenate([dlb_f, dlb_b], axis=0))
    g["hg_lb"] = jnp.stack([d_a0, d_a1], axis=1)
    dqr, dkr, dv = _attn_bwd(qr, kr, p, dcat, att32, lse)
    dp_qk, dgq, dgk = _qk_prep_bwd(p, dqr, dkr, gq2, gk2, tables, bd)
    g["q_norm_g"], g["k_norm_g"] = dgq, dgk
    dp = _assemble_dp(dp_qk, dv, dhq_f, dhq_b, dz_f, dz_b, dhi_f, dhi_b, dhg)
    pending["in"], started = _exchange_begin([_slots(_dw(dp, h1, "dw_in"))], "in")
    dx, g["pre_mix_g"] = _mm_prenorm_bwd(dp, w_in_t, True, x, w["pre_mix_g"], dx1, "in_bwd", after=[started])
    return loss, dx, g, pending


_COL_SHARDED = ("w_in", "w_xkv", "w_up")
_ROW_SHARDED = ("w_out", "w_xq", "w_xo", "w_down")
_REPLICATED = ("pre_mix_g", "q_norm_g", "k_norm_g", "hg_out_norm_g", "post_mix_g", "pre_x_g", "mem_norm_g", "post_x_g",
               "pre_ffn_g", "conv_b", "post_ffn_g")
_WEIGHTS = ("pre_mix_g", "w_in", "q_norm_g", "k_norm_g", "hg_lb", "hg_out_norm_g", "w_out", "post_mix_g", "pre_x_g",
            "mem_norm_g", "w_xq", "w_xkv", "w_xo", "post_x_g", "pre_ffn_g", "w_up", "conv_w", "conv_b", "w_down",
            "post_ffn_g")
_ADAM_TRANSPOSED = ("w_in", "w_up")
PACK_W = 1024


def _small_plan(shapes):
    plan, r = [], 0
    for vi, (rows, cols) in enumerate(shapes):
        for i in range(rows):
            for c0 in range(0, cols, PACK_W):
                plan.append((vi, i, c0, min(PACK_W, cols - c0), r))
                r += 1
    return plan, -(-r // 8) * 8


def _pack_small(vals):
    plan, nrows = _small_plan([val.shape for val in vals])

    def body(*refs):
        ins, out = refs[:-1], refs[-1]
        out[...] = jnp.zeros_like(out)
        for vi, i, c0, width, r in plan:
            out[r:r + 1, 0:width] = ins[vi][i:i + 1, c0:c0 + width]

    return pl.pallas_call(body, name="pack_small", out_shape=jax.ShapeDtypeStruct((nrows, PACK_W), F32))(*vals)


def _sum_unpack_small(packs, shapes):
    plan, _ = _small_plan(shapes)

    def body(*refs):
        p_ref, outs = refs[0], refs[1:]
        acc = p_ref[0]
        for i in range(1, N_DEV):
            acc = acc + p_ref[i]
        for vi, i, c0, width, r in plan:
            outs[vi][i:i + 1, c0:c0 + width] = acc[r:r + 1, 0:width]

    return pl.pallas_call(body, name="sum_unpack_small", out_shape=[jax.ShapeDtypeStruct(s, F32) for s in shapes])(packs)


def _adamw_many(ws, gs, ms, vs):
    n = len(ws)

    def body(*refs):
        w_refs, g_refs, m_refs, v_refs = (refs[k * n:(k + 1) * n] for k in range(4))
        d_refs, nm_refs, nv_refs = (refs[(4 + k) * n:(5 + k) * n] for k in range(3))
        for k in range(n):
            g_ = g_refs[k][...]
            m_ = ADAM_B1 * m_refs[k][...] + (1.0 - ADAM_B1) * g_
            v_ = ADAM_B2 * v_refs[k][...] + (1.0 - ADAM_B2) * (g_ * g_)
            m_hat = m_ / (1.0 - ADAM_B1 ** ADAM_STEP)
            v_hat = v_ / (1.0 - ADAM_B2 ** ADAM_STEP)
            d_refs[k][...] = -ADAM_LR * (m_hat / (jnp.sqrt(v_hat) + ADAM_EPS) + ADAM_WD * w_refs[k][...])
            nm_refs[k][...] = m_
            nv_refs[k][...] = v_

    shapes = [jax.ShapeDtypeStruct(a.shape, F32) for a in ws]
    outs = pl.pallas_call(body, name="adamw_small", out_shape=shapes * 3)(*ws, *gs, *ms, *vs)
    return outs[:n], outs[n:2 * n], outs[2 * n:]


def kernel(x, mem, pre_mix_g, w_in, q_norm_g, k_norm_g, hg_lb, hg_out_norm_g, w_out, post_mix_g, pre_x_g, mem_norm_g, w_xq, w_xkv, w_xo, post_x_g, pre_ffn_g, w_up, conv_w, conv_b, w_down, post_ffn_g, loss_target, m_pre_mix_g, m_w_in, m_q_norm_g, m_k_norm_g, m_hg_lb, m_hg_out_norm_g, m_w_out, m_post_mix_g, m_pre_x_g, m_mem_norm_g, m_w_xq, m_w_xkv, m_w_xo, m_post_x_g, m_pre_ffn_g, m_w_up, m_conv_w, m_conv_b, m_w_down, m_post_ffn_g, v_pre_mix_g, v_w_in, v_q_norm_g, v_k_norm_g, v_hg_lb, v_hg_out_norm_g, v_w_out, v_post_mix_g, v_pre_x_g, v_mem_norm_g, v_w_xq, v_w_xkv, v_w_xo, v_post_x_g, v_pre_ffn_g, v_w_up, v_conv_w, v_conv_b, v_w_down, v_post_ffn_g):
    args = dict(locals())
    w = {n: args[n] for n in _WEIGHTS}
    m = {n: args["m_" + n] for n in _WEIGHTS}
    v = {n: args["v_" + n] for n in _WEIGHTS}
    me = _dev_index(*_mesh_pos())

    wire = {n: _bf(w[n][0].T) for n in _COL_SHARDED}
    wire.update({n: _bf(w[n][0]) for n in _ROW_SHARDED})

    loss, grad_x, g, pending = _local_step(x[0], mem[0], loss_target[0], w, wire)

    grads, delta, new_m, new_v = {}, {}, {}, {}

    me_arr = jnp.reshape(me, (1,)).astype(jnp.int32)

    def update(n, parts):
        if n in _ADAM_TRANSPOSED:
            outs = _sum_adamw(*parts, me_arr, w[n][0].T, m[n][0].T, v[n][0].T, "update_" + n)
            grads[n], delta[n], new_m[n], new_v[n] = (a.T[None] for a in outs)
        elif n in _COL_SHARDED:
            gsum = _sum_parts(*parts, me_arr, "sum_" + n).T
            grads[n] = gsum[None]
            delta[n], new_m[n], new_v[n] = (a[None] for a in _adamw(w[n][0], gsum, m[n][0], v[n][0], "adamw_" + n))
        else:
            outs = _sum_adamw(*parts, me_arr, w[n][0], m[n][0], v[n][0], "update_" + n)
            grads[n], delta[n], new_m[n], new_v[n] = (a[None] for a in outs)

    small = list(_REPLICATED) + ["hg_lb", "conv_w"]
    vals = [g[n] for n in _REPLICATED] + [g["hg_lb"].reshape(4, HG_DIM), g["conv_w"], jnp.pad(loss, ((0, 0), (0, LANE - 1)))]
    shapes = [val.shape for val in vals]
    fetch_small, _ = _gather_begin([_pack_small(vals)], "small")

    after = grad_x
    for tag, names in (("down", ["w_down"]), ("up", ["w_up"]), ("x", ["w_xo", "w_xq", "w_xkv"]), ("out", ["w_out"]),
                       ("in", ["w_in"])):
        for n, parts in zip(names, _exchange_end(pending[tag], after)):
            update(n, parts)
            after = new_v[n]

    (packs,) = _gather_end(fetch_small, [0], after)
    summed = _sum_unpack_small(packs, shapes)
    loss = summed[-1][0, 0]
    for n, s in zip(small, summed[:-1]):
        grads[n] = s
    fold = lambda v2: v2[:, :ATT_HEAD_DIM] + v2[:, ATT_HEAD_DIM:]
    grads["q_norm_g"], grads["k_norm_g"] = fold(grads["q_norm_g"]), fold(grads["k_norm_g"])
    grads["hg_lb"] = lax.dynamic_slice_in_dim(grads["hg_lb"].reshape(2, 2, HG_DIM), me * (HG_DIM // N_DEV),
                                              HG_DIM // N_DEV, axis=2)
    grads["conv_w"] = lax.dynamic_slice_in_dim(grads["conv_w"], me * (2 * D_FF // N_DEV), 2 * D_FF // N_DEV, axis=1)[None]

    flat2 = lambda a: a.reshape(-1, a.shape[-1])
    outs = _adamw_many(*[[flat2(d[n]) for n in small] for d in (w, grads, m, v)])
    for dst, vals in zip((delta, new_m, new_v), outs):
        for n, val in zip(small, vals):
            dst[n] = val.reshape(w[n].shape)

    return (loss, grad_x[None], *[grads[n] for n in _WEIGHTS], *[delta[n] for n in _WEIGHTS],
            *[new_m[n] for n in _WEIGHTS], *[new_v[n] for n in _WEIGHTS])
```

```python
import jax
import jax.numpy as jnp
from jax import lax
from jax.experimental import pallas as pl
from jax.experimental.pallas import tpu as pltpu

F32 = jnp.float32
BF16 = jnp.bfloat16

D_MODEL = 1024
GRID_W = 64
EPS = 1e-6
ATT_HEADS = 8
ATT_HEAD_DIM = 64
ATT_Q_DIM = 512
ATT_KV_DIM = 128
ROPE_THETA = 10000.0
HG_HEADS = 4
HG_DIM = 512
HG_CHUNK = 32
HG_CHUNK_LOG2 = 5
HG_BLOCK_FWD = 256
HG_BLOCK_BWD = 128
N_IN = 3328
X_HEADS = 4
X_HEAD_DIM = 256
D_FF = 2816
N_DEV = 8
LANE = 128
ADAM_LR = 0.001
ADAM_B1 = 0.9
ADAM_B2 = 0.999
ADAM_EPS = 1e-08
ADAM_WD = 0.01
ADAM_STEP = 10
VMEM_LIMIT = 56 * 1024 * 1024

MESH_T = pl.DeviceIdType.MESH


def _params(**kw):
    return pltpu.CompilerParams(vmem_limit_bytes=VMEM_LIMIT, **kw)


def _dot(a, b, ca, cb):
    return lax.dot_general(a, b, (((ca,), (cb,)), ((), ())), preferred_element_type=F32)


def _bf(x):
    return x.astype(BF16)


def _sigmoid(x):
    return 1.0 / (1.0 + jnp.exp(-x))


def _rms_fwd(x, g):
    r = lax.rsqrt(jnp.mean(x * x, axis=-1, keepdims=True) + EPS)
    return x * r * g


def _rms_bwd(dy, x, g):
    r = lax.rsqrt(jnp.mean(x * x, axis=-1, keepdims=True) + EPS)
    xh = x * r
    dg = jnp.sum(dy * xh, axis=0, keepdims=True)
    t = dy * g
    dx = r * (t - xh * jnp.mean(t * xh, axis=-1, keepdims=True))
    return dx, dg


def _full(shape):
    nd = len(shape)
    return pl.BlockSpec(shape, lambda *a: (0,) * nd)


def _norm_mm(x, g, w, trans, tn, name, after=(), tm=512):
    t, d = x.shape
    n = w.shape[0] if trans else w.shape[1]
    tm = min(tm, t)

    def body(x_ref, g_ref, w_ref, h_ref, p_ref):
        h = _bf(_rms_fwd(x_ref[...], g_ref[...]))
        h_ref[...] = h
        p_ref[...] = _dot(h, w_ref[...], 1, 1 if trans else 0)

    w_spec = pl.BlockSpec((tn, d), lambda i, j: (j, 0)) if trans else pl.BlockSpec((d, tn), lambda i, j: (0, j))
    return _call(
        body, (x, g, w), name=name, grid=(t // tm, n // tn),
        in_specs=[pl.BlockSpec((tm, d), lambda i, j: (i, 0)), _full((1, d)), w_spec],
        out_specs=[pl.BlockSpec((tm, d), lambda i, j: (i, 0)), pl.BlockSpec((tm, tn), lambda i, j: (i, j))],
        out_shape=[jax.ShapeDtypeStruct((t, d), BF16), jax.ShapeDtypeStruct((t, n), F32)], after=after)


def _mm_postnorm_res(a, w, g, res, name, tm=256):
    t, k = a.shape
    d = w.shape[1]

    def body(a_ref, w_ref, g_ref, res_ref, y_ref, o_ref):
        y = _dot(a_ref[...], w_ref[...], 1, 0)
        y_ref[...] = y
        o_ref[...] = res_ref[...] + _rms_fwd(y, g_ref[...])

    row = lambda width: pl.BlockSpec((tm, width), lambda i: (i, 0))
    return pl.pallas_call(
        body, name=name, grid=(t // tm,),
        in_specs=[row(k), _full((k, d)), _full((1, d)), row(d)],
        out_specs=[row(d), row(d)],
        out_shape=[jax.ShapeDtypeStruct((t, d), F32)] * 2,
        compiler_params=_params(),
    )(a, w, g, res)


def _mm_postnorm_res_loss(a, w, g, res, tgt, name, tm=256):
    t, k = a.shape
    d = w.shape[1]

    def body(a_ref, w_ref, g_ref, res_ref, tgt_ref, y_ref, dout_ref, loss_ref):
        @pl.when(pl.program_id(0) == 0)
        def _():
            loss_ref[...] = jnp.zeros_like(loss_ref)

        y = _dot(a_ref[...], w_ref[...], 1, 0)
        y_ref[...] = y
        diff = res_ref[...] + _rms_fwd(y, g_ref[...]) - tgt_ref[...]
        dout_ref[...] = diff * (1.0 / d)
        part = jnp.sum(jnp.sum(diff * diff, axis=-1, keepdims=True), axis=0, keepdims=True)
        loss_ref[...] += (0.5 / d) * part

    row = lambda width: pl.BlockSpec((tm, width), lambda i: (i, 0))
    return pl.pallas_call(
        body, name=name, grid=(t // tm,),
        in_specs=[row(k), _full((k, d)), _full((1, d)), row(d), row(d)],
        out_specs=[row(d), row(d), _full((1, 1))],
        out_shape=[jax.ShapeDtypeStruct((t, d), F32)] * 2 + [jax.ShapeDtypeStruct((1, 1), F32)],
        compiler_params=_params(),
    )(a, w, g, res, tgt)


def _postnorm_bwd_mm(dout, y, g, w, name, da_dtype=F32, tm=256):
    t, d = y.shape
    k = w.shape[0]

    def body(dout_ref, y_ref, g_ref, w_ref, dy_ref, da_ref, dg_ref):
        @pl.when(pl.program_id(0) == 0)
        def _():
            dg_ref[...] = jnp.zeros_like(dg_ref)

        dy, dg = _rms_bwd(dout_ref[...], y_ref[...], g_ref[...])
        dg_ref[...] += dg
        dyb = _bf(dy)
        dy_ref[...] = dyb
        da_ref[...] = _dot(dyb, w_ref[...], 1, 1).astype(da_dtype)

    row = lambda width: pl.BlockSpec((tm, width), lambda i: (i, 0))
    return pl.pallas_call(
        body, name=name, grid=(t // tm,),
        in_specs=[row(d), row(d), _full((1, d)), _full((k, d))],
        out_specs=[row(d), row(k), _full((1, d))],
        out_shape=[jax.ShapeDtypeStruct((t, d), BF16), jax.ShapeDtypeStruct((t, k), da_dtype), jax.ShapeDtypeStruct((1, d), F32)],
        compiler_params=_params(),
    )(dout, y, g, w)


def _mm_prenorm_bwd(dp, w, trans, x, g, dres, name, after=(), tm=512):
    dps = list(dp) if isinstance(dp, (list, tuple)) else [dp]
    nparts = len(dps)
    t, n_each = dps[0].shape
    d = x.shape[1]
    tm = min(tm, t)

    def body(*refs):
        dp_refs = refs[:nparts]
        w_ref, x_ref, g_ref, dres_ref, dx_ref, dg_ref = refs[nparts:]

        @pl.when(pl.program_id(0) == 0)
        def _():
            dg_ref[...] = jnp.zeros_like(dg_ref)

        dh = None
        for part, dp_ref in enumerate(dp_refs):
            cols = slice(part * n_each, (part + 1) * n_each)
            term = _dot(dp_ref[...], w_ref[cols, :], 1, 0) if trans else _dot(dp_ref[...], w_ref[:, cols], 1, 1)
            dh = term if dh is None else dh + term
        dx, dg = _rms_bwd(dh, x_ref[...], g_ref[...])
        dg_ref[...] += dg
        dx_ref[...] = dres_ref[...] + dx

    row = lambda width: pl.BlockSpec((tm, width), lambda i: (i, 0))
    return _call(
        body, (*dps, w, x, g, dres), name=name, grid=(t // tm,),
        in_specs=[row(n_each)] * nparts + [_full(w.shape), row(d), _full((1, d)), row(d)],
        out_specs=[row(d), _full((1, d))],
        out_shape=[jax.ShapeDtypeStruct((t, d), F32), jax.ShapeDtypeStruct((1, d), F32)], after=after)


def _dw(a, b, name, bcol=None, tka=256):
    parts = list(a) if isinstance(a, (list, tuple)) else [a]
    nparts = len(parts)
    t, ka_each = parts[0].shape
    bci, nb = (0, b.shape[1]) if bcol is None else bcol
    nt_each = ka_each // tka

    def body(*refs):
        a_refs, b_ref, o_ref = refs[:nparts], refs[nparts], refs[nparts + 1]
        i = pl.program_id(0)
        for part, a_ref in enumerate(a_refs):
            @pl.when((i >= part * nt_each) & (i < (part + 1) * nt_each))
            def _():
                o_ref[...] = _bf(_dot(a_ref[...], b_ref[...], 0, 0))

    a_spec = lambda part: pl.BlockSpec((t, tka), lambda i: (0, jnp.clip(i - part * nt_each, 0, nt_each - 1)))
    return pl.pallas_call(
        body, name=name, grid=(nt_each * nparts,),
        in_specs=[a_spec(part) for part in range(nparts)] + [pl.BlockSpec((t, nb), lambda i: (0, bci))],
        out_specs=pl.BlockSpec((tka, nb), lambda i: (i, 0)),
        out_shape=jax.ShapeDtypeStruct((nparts * ka_each, nb), BF16),
        compiler_params=_params(),
    )(*parts, b)


def _rope_tables(t):
    pos = jnp.arange(t)
    r = (pos // GRID_W).astype(F32)
    c = (pos % GRID_W).astype(F32)
    npair = ATT_HEAD_DIM // 4
    inv = jnp.power(ROPE_THETA, -jnp.arange(npair, dtype=F32) / npair)
    ang = jnp.concatenate([r[:, None] * inv, c[:, None] * inv], axis=-1)
    cos = jnp.repeat(jnp.cos(ang), 2, axis=-1)
    sin = jnp.repeat(jnp.sin(ang), 2, axis=-1)
    even = (jnp.arange(ATT_HEAD_DIM) % 2) == 0
    sa = jnp.where(even, -sin, 0.0)
    sb = jnp.where(even, 0.0, sin)
    two = lambda a: jnp.tile(a, (1, 2))
    return two(cos), two(sa), two(sb)


def _head_sum_matrix():
    a = jnp.arange(LANE) // ATT_HEAD_DIM
    return (a[:, None] == a[None, :]).astype(BF16)


def _head_mean(v, bd):
    hi = _bf(v)
    lo = _bf(v - hi.astype(F32))
    return (_dot(hi, bd, 1, 0) + _dot(lo, bd, 1, 0)) * (1.0 / ATT_HEAD_DIM)


def _qk_prep(p, gq, gk, tables, bd, tm=512):
    t = p.shape[0]
    tm = min(tm, t)
    cc, sa, sb = tables

    def body(p_ref, gq_ref, gk_ref, cc_ref, sa_ref, sb_ref, bd_ref, q_ref, k_ref):
        cc_, sa_, sb_, bd_ = cc_ref[...], sa_ref[...], sb_ref[...], bd_ref[...]
        low = lax.broadcasted_iota(jnp.int32, (tm, LANE), 1) < ATT_HEAD_DIM

        def normrope(xs, g):
            xn = xs * lax.rsqrt(_head_mean(xs * xs, bd_) + EPS) * g
            return xn * cc_ + pltpu.roll(xn, LANE - 1, 1) * sa_ + pltpu.roll(xn, 1, 1) * sb_

        for j in range(4):
            y = normrope(p_ref[:, j * LANE:(j + 1) * LANE], gq_ref[...]) * (ATT_HEAD_DIM ** -0.5)
            yr = pltpu.roll(y, ATT_HEAD_DIM, 1)
            if j // 2 == 0:
                h0, h1 = jnp.where(low, y, 0.0), jnp.where(low, yr, 0.0)
            else:
                h0, h1 = jnp.where(low, 0.0, yr), jnp.where(low, 0.0, y)
            q_ref[:, (2 * j) * LANE:(2 * j + 1) * LANE] = _bf(h0)
            q_ref[:, (2 * j + 1) * LANE:(2 * j + 2) * LANE] = _bf(h1)
        k_ref[...] = _bf(normrope(p_ref[:, ATT_Q_DIM:ATT_Q_DIM + LANE], gk_ref[...]))

    row = lambda width: pl.BlockSpec((tm, width), lambda i: (i, 0))
    return pl.pallas_call(
        body, name="qk_prep", grid=(t // tm,),
        in_specs=[row(ATT_Q_DIM + LANE), _full((1, LANE)), _full((1, LANE)), row(LANE), row(LANE), row(LANE),
                  _full((LANE, LANE))],
        out_specs=[row(ATT_HEADS * LANE), row(LANE)],
        out_shape=[jax.ShapeDtypeStruct((t, ATT_HEADS * LANE), BF16), jax.ShapeDtypeStruct((t, LANE), BF16)],
        compiler_params=_params(),
    )(p, gq, gk, cc, sa, sb, bd)


def _qk_prep_bwd(p, dq, dk, gq, gk, tables, bd, tm=512):
    t = p.shape[0]
    tm = min(tm, t)
    cc, sa, sb = tables

    def body(p_ref, dq_ref, dk_ref, gq_ref, gk_ref, cc_ref, sa_ref, sb_ref, bd_ref, dp_ref, dgq_ref, dgk_ref):
        @pl.when(pl.program_id(0) == 0)
        def _():
            dgq_ref[...] = jnp.zeros_like(dgq_ref)
            dgk_ref[...] = jnp.zeros_like(dgk_ref)

        cc_, sa_, sb_, bd_ = cc_ref[...], sa_ref[...], sb_ref[...], bd_ref[...]
        low = lax.broadcasted_iota(jnp.int32, (tm, LANE), 1) < ATT_HEAD_DIM

        def bwd(xs, g, dy):
            r = lax.rsqrt(_head_mean(xs * xs, bd_) + EPS)
            xh = xs * r
            dxn = dy * cc_ + pltpu.roll(dy * sa_, 1, 1) + pltpu.roll(dy * sb_, LANE - 1, 1)
            dg = jnp.sum(dxn * xh, axis=0, keepdims=True)
            tt = dxn * g
            return r * (tt - xh * _head_mean(tt * xh, bd_)), dg

        dgq = jnp.zeros((1, LANE), F32)
        for j in range(4):
            d0 = dq_ref[:, (2 * j) * LANE:(2 * j + 1) * LANE]
            d1 = dq_ref[:, (2 * j + 1) * LANE:(2 * j + 2) * LANE]
            if j // 2 == 0:
                dy = jnp.where(low, d0, pltpu.roll(d1, ATT_HEAD_DIM, 1))
            else:
                dy = jnp.where(low, pltpu.roll(d0, ATT_HEAD_DIM, 1), d1)
            dx, dg = bwd(p_ref[:, j * LANE:(j + 1) * LANE], gq_ref[...], dy * (ATT_HEAD_DIM ** -0.5))
            dp_ref[:, j * LANE:(j + 1) * LANE] = _bf(dx)
            dgq = dgq + dg
        dgq_ref[...] += dgq
        dx, dg = bwd(p_ref[:, ATT_Q_DIM:ATT_Q_DIM + LANE], gk_ref[...], dk_ref[...])
        dp_ref[:, ATT_Q_DIM:ATT_Q_DIM + LANE] = _bf(dx)
        dgk_ref[...] += dg

    row = lambda width: pl.BlockSpec((tm, width), lambda i: (i, 0))
    return pl.pallas_call(
        body, name="qk_prep_bwd", grid=(t // tm,),
        in_specs=[row(ATT_Q_DIM + LANE), row(ATT_HEADS * LANE), row(LANE), _full((1, LANE)), _full((1, LANE)),
                  row(LANE), row(LANE), row(LANE), _full((LANE, LANE))],
        out_specs=[row(ATT_Q_DIM + LANE), _full((1, LANE)), _full((1, LANE))],
        out_shape=[jax.ShapeDtypeStruct((t, ATT_Q_DIM + LANE), BF16), jax.ShapeDtypeStruct((1, LANE), F32),
                   jax.ShapeDtypeStruct((1, LANE), F32)],
        compiler_params=_params(),
    )(p, dq, dk, gq, gk, cc, sa, sb, bd)


def _attn_fwd(q, k, p, tq=256):
    t = k.shape[0]
    tq = min(tq, t)
    v_blk = (ATT_Q_DIM + ATT_KV_DIM) // LANE

    def body(q_ref, k_ref, v_ref, o_ref, o32_ref, lse_ref):
        k_ = k_ref[...]
        v = v_ref[...]
        lane_k = lax.broadcasted_iota(jnp.int32, (t, LANE), 1)
        lane_q = lax.broadcasted_iota(jnp.int32, (tq, LANE), 1)
        lowk, lowq = lane_k < ATT_HEAD_DIM, lane_q < ATT_HEAD_DIM
        ones_lane = (ATT_HEAD_DIM, 0)
        vm = (_bf(jnp.where(lowk, v, jnp.where(lane_k == ones_lane[0], 1.0, 0.0))),
              _bf(jnp.where(lowk, jnp.where(lane_k == ones_lane[1], 1.0, 0.0), v)))
        for j in range(4):
            kvh = j // 2
            acc = None
            for sub in range(2):
                h = 2 * j + sub
                s = _dot(q_ref[:, h * LANE:(h + 1) * LANE], k_, 1, 1)
                mx = jnp.max(s, axis=-1, keepdims=True)
                ov = _dot(jnp.exp(_bf(s - mx)), vm[kvh], 1, 0)
                l = jnp.sum(jnp.where(lane_q == ones_lane[kvh], ov, 0.0), axis=-1, keepdims=True)
                lse_ref[h] = mx + jnp.log(l)
                o = jnp.where(lowq if kvh == 0 else ~lowq, ov, 0.0) * (1.0 / l)
                if sub != kvh:
                    o = pltpu.roll(o, ATT_HEAD_DIM, 1)
                acc = o if acc is None else acc + o
            o32_ref[:, j * LANE:(j + 1) * LANE] = acc
            o_ref[:, j * LANE:(j + 1) * LANE] = _bf(acc)

    row = pl.BlockSpec((tq, ATT_Q_DIM), lambda i: (i, 0))
    return _call(
        body, (q, k, p), name="attn_fwd", grid=(t // tq,),
        in_specs=[pl.BlockSpec((tq, ATT_HEADS * LANE), lambda i: (i, 0)), _full((t, LANE)),
                  pl.BlockSpec((t, LANE), lambda i: (0, v_blk))],
        out_specs=[row, row, pl.BlockSpec((ATT_HEADS, tq, 1), lambda i: (0, i, 0))],
        out_shape=[jax.ShapeDtypeStruct((t, ATT_Q_DIM), BF16), jax.ShapeDtypeStruct((t, ATT_Q_DIM), F32),
                   jax.ShapeDtypeStruct((ATT_HEADS, t, 1), F32)])


def _attn_bwd(q, k, p, dcat, o32, lse, tq=256):
    t = k.shape[0]
    tq = min(tq, t)
    v_blk = (ATT_Q_DIM + ATT_KV_DIM) // LANE

    def body(q_ref, k_ref, v_ref, do_ref, o_ref, lse_ref, dq_ref, dk_ref, dv_ref):
        @pl.when(pl.program_id(0) == 0)
        def _():
            dk_ref[...] = jnp.zeros_like(dk_ref)
            dv_ref[...] = jnp.zeros_like(dv_ref)

        k_ = k_ref[...]
        vb = _bf(v_ref[...])
        lowq = lax.broadcasted_iota(jnp.int32, (tq, LANE), 1) < ATT_HEAD_DIM
        dk_acc = jnp.zeros((t, LANE), F32)
        dv_acc = jnp.zeros((t, LANE), F32)
        for j in range(4):
            kvh = j // 2
            dop = do_ref[:, j * LANE:(j + 1) * LANE]
            prod = dop * o_ref[:, j * LANE:(j + 1) * LANE]
            d_low = jnp.sum(jnp.where(lowq, prod, 0.0), axis=-1, keepdims=True)
            d_sub = (d_low, jnp.sum(prod, axis=-1, keepdims=True) - d_low)
            for sub in range(2):
                h = 2 * j + sub
                src = dop if sub == kvh else pltpu.roll(dop, ATT_HEAD_DIM, 1)
                do_h = _bf(jnp.where(lowq, src, 0.0) if kvh == 0 else jnp.where(lowq, 0.0, src))
                qh = q_ref[:, h * LANE:(h + 1) * LANE]
                pr = jnp.exp(_bf(_dot(qh, k_, 1, 1) - lse_ref[h]))
                ds = pr * _bf(_dot(do_h, vb, 1, 1) - d_sub[sub])
                dq_ref[:, h * LANE:(h + 1) * LANE] = _dot(ds, k_, 1, 0)
                dk_acc = dk_acc + _dot(ds, qh, 0, 0)
                dv_acc = dv_acc + _dot(pr, do_h, 0, 0)
        dk_ref[...] += dk_acc
        dv_ref[...] += dv_acc

    row = pl.BlockSpec((tq, ATT_Q_DIM), lambda i: (i, 0))
    return _call(
        body, (q, k, p, dcat, o32, lse), name="attn_bwd", grid=(t // tq,),
        in_specs=[pl.BlockSpec((tq, ATT_HEADS * LANE), lambda i: (i, 0)), _full((t, LANE)),
                  pl.BlockSpec((t, LANE), lambda i: (0, v_blk)), row, row,
                  pl.BlockSpec((ATT_HEADS, tq, 1), lambda i: (0, i, 0))],
        out_specs=[pl.BlockSpec((tq, ATT_HEADS * LANE), lambda i: (i, 0)), _full((t, LANE)), _full((t, LANE))],
        out_shape=[jax.ShapeDtypeStruct((t, ATT_HEADS * LANE), F32), jax.ShapeDtypeStruct((t, LANE), F32),
                   jax.ShapeDtypeStruct((t, LANE), F32)])


def _lower_bounds(a0, a1):
    def body(a0_ref, a1_ref, lb_ref):
        m = jnp.maximum(a0_ref[...], a1_ref[...])
        e0, e1 = jnp.exp(a0_ref[...] - m), jnp.exp(a1_ref[...] - m)
        lb_ref[...] = e0 / (e0 + e1)

    return pl.pallas_call(body, name="lower_bounds", out_shape=jax.ShapeDtypeStruct(a0.shape, F32))(a0, a1)


def _lower_bounds_bwd(a0, a1, dlb):
    def body(a0_ref, a1_ref, dlb_ref, d0_ref, d1_ref):
        m = jnp.maximum(a0_ref[...], a1_ref[...])
        e0, e1 = jnp.exp(a0_ref[...] - m), jnp.exp(a1_ref[...] - m)
        lb = e0 / (e0 + e1)
        d0 = dlb_ref[...] * lb * (1.0 - lb)
        d0_ref[...] = d0
        d1_ref[...] = -d0

    return pl.pallas_call(body, name="lower_bounds_bwd", out_shape=[jax.ShapeDtypeStruct(a0.shape, F32)] * 2)(a0, a1, dlb)


def _chunk_sums(x, mat, terms):
    acc, rest = None, x
    for i in range(terms):
        part = _bf(rest)
        if i + 1 < terms:
            rest = rest - part.astype(F32)
        term = _dot(mat, part, 1, 0)
        acc = term if acc is None else acc + term
    return acc


def _chunk_concat(x, cid, nc):
    return jnp.concatenate([_bf(jnp.where(cid == c, x, 0.0)) for c in range(nc)], axis=1)


def _chunk_pick(x, cid, nc):
    out = jnp.where(cid == 0, x[:, :LANE], 0.0)
    for c in range(1, nc):
        out = out + jnp.where(cid == c, x[:, c * LANE:(c + 1) * LANE], 0.0)
    return out


def _hgrn_gates(hq, z, lb):
    sq = _sigmoid(hq)
    sig = _sigmoid(z)
    f = lb + (1.0 - lb) * sig
    return hq * sq, sq, sig, f, jnp.log(f)


def _hgrn_local(q, f, g, v, cid, amask, sums, rev, nc):
    k = 1.0 - f
    n = g.shape[0]
    both = _chunk_sums(g, sums, 3)
    ba, bb = (both[n:], both[:n]) if rev else (both[:n], both[n:])
    eq = 0.5 * (ba - bb + g)
    ex_q, ex_k, ex_i, ex_d = jnp.exp(eq), jnp.exp(-eq), jnp.exp(ba), jnp.exp(bb - g)
    qb, kb, qi, kd = q * ex_q, k * ex_k, q * ex_i, k * ex_d
    dvec = jnp.exp(ba + bb - g)
    a = jnp.where(amask, _dot(_bf(qb), _bf(kb), 1, 1), 0.0)
    kd_m = _chunk_concat(kd, cid, nc)
    qi_m = _chunk_concat(qi, cid, nc)
    ut_all = _dot(_bf(v), kd_m, 0, 0)
    return dict(k=k, ex_q=ex_q, ex_k=ex_k, ex_i=ex_i, ex_d=ex_d, qb=qb, kb=kb, qi=qi, kd=kd, dvec=dvec, a=a,
                kd_m=kd_m, qi_m=qi_m, ut_all=ut_all)


def _hgrn_masks(n, rev):
    row = lax.broadcasted_iota(jnp.int32, (n, LANE), 0)
    ti = lax.broadcasted_iota(jnp.int32, (n, n), 0)
    si = lax.broadcasted_iota(jnp.int32, (n, n), 1)
    same = jnp.right_shift(ti, HG_CHUNK_LOG2) == jnp.right_shift(si, HG_CHUNK_LOG2)
    down, up = same & (si <= ti), same & (si >= ti)
    sums = jnp.concatenate([_bf(jnp.where(down, 1.0, 0.0)), _bf(jnp.where(up, 1.0, 0.0))], axis=0)
    return jnp.right_shift(row, HG_CHUNK_LOG2), (up if rev else down), sums


def _hgrn_specs(t, rev, bwd):
    n = min(HG_BLOCK_BWD if bwd else HG_BLOCK_FWD, t)
    nb = t // n
    if rev != bwd:
        blk = lambda i: nb - 1 - i
    else:
        blk = lambda i: i
    col = lambda base: pl.BlockSpec((n, 2 * LANE), lambda hp, i: (blk(i), base + hp))
    return n, nb, blk, col


def _hgrn_fwd(p, lb, rev):
    t = p.shape[0]
    n, nb, blk, col = _hgrn_specs(t, rev, False)
    nc = n // HG_CHUNK
    sub = n // min(HG_BLOCK_BWD, t)
    z_base = 7 if rev else 5

    def body(hq_ref, z_ref, hi_ref, lb_ref, o_ref, ssave_ref, st_scr):
        @pl.when(pl.program_id(1) == 0)
        def _():
            st_scr[...] = jnp.zeros_like(st_scr)

        cid, amask, sums = _hgrn_masks(n, rev)
        order = list(range(nc))[::-1] if rev else list(range(nc))
        for hh in range(2):
            sl = slice(hh * LANE, (hh + 1) * LANE)
            q, _, _, f, g = _hgrn_gates(hq_ref[:, sl], z_ref[:, sl], lb_ref[0:1, sl])
            v = hi_ref[:, sl]
            c_ = _hgrn_local(q, f, g, v, cid, amask, sums, rev, nc)
            st = st_scr[hh]
            cols = [None] * nc
            for c in order:
                cols[c] = st
                st = st * c_["dvec"][c * HG_CHUNK:c * HG_CHUNK + 1, :] + c_["ut_all"][:, c * LANE:(c + 1) * LANE]
            st_scr[hh] = st
            for s in range(sub):
                ssave_ref[sub - 1 - s if rev else s, sl, :] = cols[order[s * (nc // sub)]]
            st_all = _bf(jnp.concatenate(cols, axis=1))
            o_ref[:, sl] = _dot(_bf(c_["a"]), _bf(v), 1, 0) + _dot(c_["qi_m"], st_all, 1, 1)

    return _call(
        body, (p, p, p, lb), name="hgrn_fwd_rev" if rev else "hgrn_fwd", grid=(2, nb),
        in_specs=[col(3), col(z_base), col(9), pl.BlockSpec((1, 2 * LANE), lambda hp, i: (0, hp))],
        out_specs=[pl.BlockSpec((n, 2 * LANE), lambda hp, i: (blk(i), hp)),
                   pl.BlockSpec((sub, 2 * LANE, LANE), lambda hp, i: (blk(i), hp, 0))],
        out_shape=[jax.ShapeDtypeStruct((t, HG_DIM), F32), jax.ShapeDtypeStruct((nb * sub, HG_DIM, LANE), F32)],
        scratch_shapes=[pltpu.VMEM((2, LANE, LANE), F32)])


def _hgrn_bwd(p, lb, do, ssave, rev, after=()):
    t = p.shape[0]
    n, nb, blk, col = _hgrn_specs(t, rev, True)
    nc = n // HG_CHUNK
    z_base = 7 if rev else 5

    def body(hq_ref, z_ref, hi_ref, lb_ref, do_ref, ssave_ref, dhq_ref, dz_ref, dhi_ref, dlb_ref, dst_scr):
        @pl.when(pl.program_id(1) == 0)
        def _():
            dst_scr[...] = jnp.zeros_like(dst_scr)
            dlb_ref[...] = jnp.zeros_like(dlb_ref)

        cid, amask, sums = _hgrn_masks(n, rev)
        order = list(range(nc))[::-1] if rev else list(range(nc))
        for hh in range(2):
            sl = slice(hh * LANE, (hh + 1) * LANE)
            hq, lbv = hq_ref[:, sl], lb_ref[0:1, sl]
            q, sq, sig, f, g = _hgrn_gates(hq, z_ref[:, sl], lbv)
            v = hi_ref[:, sl]
            c_ = _hgrn_local(q, f, g, v, cid, amask, sums, rev, nc)
            dvec, ut_all = c_["dvec"], c_["ut_all"]
            drow = lambda c: dvec[c * HG_CHUNK:c * HG_CHUNK + 1, :]
            st = ssave_ref[0, sl, :]
            cols = [None] * nc
            for c in order:
                cols[c] = st
                st = st * drow(c) + ut_all[:, c * LANE:(c + 1) * LANE]
            dob, vb = _bf(do_ref[:, sl]), _bf(v)
            gt_all = _dot(dob, c_["qi_m"], 0, 0)
            dcur = dst_scr[hh]
            dnext = [None] * nc
            ddrow = [None] * nc
            for c in order[::-1]:
                dnext[c] = dcur
                ddrow[c] = jnp.sum(cols[c] * dcur, axis=0, keepdims=True) * drow(c)
                dcur = dcur * drow(c) + gt_all[:, c * LANE:(c + 1) * LANE]
            dst_scr[hh] = dcur
            dsn_all = _bf(jnp.concatenate(dnext, axis=1))
            st_all = _bf(jnp.concatenate(cols, axis=1))
            da = _bf(jnp.where(amask, _dot(dob, vb, 1, 1), 0.0))
            dv = _dot(_bf(c_["a"]), dob, 0, 0) + _dot(c_["kd_m"], dsn_all, 1, 1)
            dqb = _dot(da, _bf(c_["kb"]), 1, 0)
            dkb = _dot(da, _bf(c_["qb"]), 0, 0)
            dqi = _chunk_pick(_dot(dob, st_all, 1, 0), cid, nc)
            dkd = _chunk_pick(_dot(vb, dsn_all, 1, 0), cid, nc)
            dq = dqb * c_["ex_q"] + dqi * c_["ex_i"]
            dk = dkb * c_["ex_k"] + dkd * c_["ex_d"]
            e = dqb * c_["qb"] - dkb * c_["kb"] + dqi * c_["qi"]
            w = dkd * c_["kd"]
            dtot = jnp.where(cid == 0, ddrow[0], 0.0)
            for c in range(1, nc):
                dtot = dtot + jnp.where(cid == c, ddrow[c], 0.0)
            against, along = (sums[:n], sums[n:]) if rev else (sums[n:], sums[:n])
            dg = _chunk_sums(e, against, 2) + (_chunk_sums(w, along, 2) - w) + dtot
            df = dg / f - dk
            dz_ref[:, sl] = df * (1.0 - lbv) * sig * (1.0 - sig)
            dlb_ref[0:1, sl] += jnp.sum(df * (1.0 - sig), axis=0, keepdims=True)
            dhq_ref[:, sl] = dq * (sq * (1.0 + hq * (1.0 - sq)))
            dhi_ref[:, sl] = dv

    out = pl.BlockSpec((n, 2 * LANE), lambda hp, i: (blk(i), hp))
    return _call(
        body, (p, p, p, lb, do, ssave), name="hgrn_bwd_rev" if rev else "hgrn_bwd", grid=(2, nb),
        in_specs=[col(3), col(z_base), col(9), pl.BlockSpec((1, 2 * LANE), lambda hp, i: (0, hp)), out,
                  pl.BlockSpec((1, 2 * LANE, LANE), lambda hp, i: (blk(i), hp, 0))],
        out_specs=[out, out, out, pl.BlockSpec((1, 2 * LANE), lambda hp, i: (0, hp))],
        out_shape=[jax.ShapeDtypeStruct((t, HG_DIM), F32)] * 3 + [jax.ShapeDtypeStruct((1, HG_DIM), F32)],
        scratch_shapes=[pltpu.VMEM((2, LANE, LANE), F32)], after=after)


def _mix_out(att, of, ob, p, gout, w_out, g_post, x, tm=256):
    t, d = x.shape
    tm = min(tm, t)

    def body(att_ref, of_ref, ob_ref, hg0_ref, hg1_ref, gout_ref, w_ref, g_ref, x_ref, cat_ref, y_ref, x1_ref):
        cat_ref[:, :ATT_Q_DIM] = att_ref[...]
        for h in range(HG_HEADS):
            sl = slice(h * LANE, (h + 1) * LANE)
            hg_ref = hg0_ref if h < 2 else hg1_ref
            hg = hg_ref[:, (h % 2) * LANE:(h % 2 + 1) * LANE]
            nrm = _rms_fwd(of_ref[:, sl] + ob_ref[:, sl], gout_ref[...])
            cat_ref[:, ATT_Q_DIM + h * LANE:ATT_Q_DIM + (h + 1) * LANE] = _bf(nrm * hg * _sigmoid(hg))
        y = _dot(cat_ref[...], w_ref[...], 1, 0)
        y_ref[...] = y
        x1_ref[...] = x_ref[...] + _rms_fwd(y, g_ref[...])

    row = lambda width: pl.BlockSpec((tm, width), lambda i: (i, 0))
    return _call(
        body, (att, of, ob, p, p, gout, w_out, g_post, x), name="mix_out", grid=(t // tm,),
        in_specs=[row(ATT_Q_DIM), row(HG_DIM), row(HG_DIM), pl.BlockSpec((tm, 2 * LANE), lambda i: (i, 11)),
                  pl.BlockSpec((tm, 2 * LANE), lambda i: (i, 12)), _full((1, LANE)), _full((d, d)), _full((1, d)), row(d)],
        out_specs=[row(d), row(d), row(d)],
        out_shape=[jax.ShapeDtypeStruct((t, d), BF16), jax.ShapeDtypeStruct((t, d), F32), jax.ShapeDtypeStruct((t, d), F32)])


def _rec_bwd(dcat, of, ob, p, gout, tm=256):
    t = of.shape[0]
    tm = min(tm, t)

    def body(dc_ref, of_ref, ob_ref, hg0_ref, hg1_ref, gout_ref, do_ref, dhg_ref, dgo_ref):
        @pl.when(pl.program_id(0) == 0)
        def _():
            dgo_ref[...] = jnp.zeros_like(dgo_ref)

        dgo = jnp.zeros((1, LANE), F32)
        for h in range(HG_HEADS):
            sl = slice(h * LANE, (h + 1) * LANE)
            hg_ref = hg0_ref if h < 2 else hg1_ref
            hg = hg_ref[:, (h % 2) * LANE:(h % 2 + 1) * LANE]
            o = of_ref[:, sl] + ob_ref[:, sl]
            sg = _sigmoid(hg)
            drec = dc_ref[:, sl]
            dhg_ref[:, sl] = drec * _rms_fwd(o, gout_ref[...]) * (sg * (1.0 + hg * (1.0 - sg)))
            do, dg = _rms_bwd(drec * hg * sg, o, gout_ref[...])
            do_ref[:, sl] = do
            dgo = dgo + dg
        dgo_ref[...] += dgo

    row = lambda width: pl.BlockSpec((tm, width), lambda i: (i, 0))
    return pl.pallas_call(
        body, name="rec_bwd", grid=(t // tm,),
        in_specs=[pl.BlockSpec((tm, HG_DIM), lambda i: (i, 1)), row(HG_DIM), row(HG_DIM),
                  pl.BlockSpec((tm, 2 * LANE), lambda i: (i, 11)), pl.BlockSpec((tm, 2 * LANE), lambda i: (i, 12)),
                  _full((1, LANE))],
        out_specs=[row(HG_DIM), row(HG_DIM), _full((1, LANE))],
        out_shape=[jax.ShapeDtypeStruct((t, HG_DIM), F32), jax.ShapeDtypeStruct((t, HG_DIM), F32),
                   jax.ShapeDtypeStruct((1, LANE), F32)],
        compiler_params=_params(),
    )(dcat, of, ob, p, p, gout)


def _assemble_dp(dp_qk, dv, dhq_f, dhq_b, dz_f, dz_b, dhi_f, dhi_b, dhg, tm=512):
    t = dv.shape[0]
    tm = min(tm, t)
    qk_w = ATT_Q_DIM + LANE

    def body(qk_ref, dv_ref, hqf_ref, hqb_ref, zf_ref, zb_ref, hif_ref, hib_ref, hg_ref, dp_ref):
        o = 0
        dp_ref[:, o:o + qk_w] = qk_ref[...]
        o += qk_w
        dp_ref[:, o:o + LANE] = _bf(dv_ref[...])
        o += LANE
        for val in (hqf_ref[...] + hqb_ref[...], zf_ref[...], zb_ref[...], hif_ref[...] + hib_ref[...], hg_ref[...]):
            dp_ref[:, o:o + HG_DIM] = _bf(val)
            o += HG_DIM

    row = lambda width: pl.BlockSpec((tm, width), lambda i: (i, 0))
    return pl.pallas_call(
        body, name="assemble_dp", grid=(t // tm,),
        in_specs=[row(qk_w), row(LANE)] + [row(HG_DIM)] * 7,
        out_specs=row(N_IN),
        out_shape=jax.ShapeDtypeStruct((t, N_IN), BF16),
        compiler_params=_params(),
    )(dp_qk, dv, dhq_f, dhq_b, dz_f, dz_b, dhi_f, dhi_b, dhg)


def _xattn_fwd(q, kv, tq=512):
    t, d = q.shape
    tq = min(tq, t)
    nm = kv.shape[0]

    def body(q_ref, kv_ref, o_ref):
        for h in range(X_HEADS):
            sl = slice(h * X_HEAD_DIM, (h + 1) * X_HEAD_DIM)
            s = _dot(_bf(q_ref[:, sl]), _bf(kv_ref[:, sl]), 1, 1) * (X_HEAD_DIM ** -0.5)
            e = jnp.exp(s - jnp.max(s, axis=-1, keepdims=True))
            pr = e * (1.0 / jnp.sum(e, axis=-1, keepdims=True))
            o_ref[:, sl] = _bf(_dot(_bf(pr), _bf(kv_ref[:, d + h * X_HEAD_DIM:d + (h + 1) * X_HEAD_DIM]), 1, 0))

    return pl.pallas_call(
        body, name="xattn_fwd", grid=(t // tq,),
        in_specs=[pl.BlockSpec((tq, d), lambda i: (i, 0)), _full((nm, 2 * d))],
        out_specs=pl.BlockSpec((tq, d), lambda i: (i, 0)),
        out_shape=jax.ShapeDtypeStruct((t, d), BF16),
        compiler_params=_params(),
    )(q, kv)


def _xattn_bwd(q, kv, do, tq=512):
    t, d = q.shape
    tq = min(tq, t)
    nm = kv.shape[0]

    def body(q_ref, kv_ref, do_ref, dq_ref, dkv_ref):
        @pl.when(pl.program_id(0) == 0)
        def _():
            dkv_ref[...] = jnp.zeros_like(dkv_ref)

        for h in range(X_HEADS):
            sl = slice(h * X_HEAD_DIM, (h + 1) * X_HEAD_DIM)
            slv = slice(d + h * X_HEAD_DIM, d + (h + 1) * X_HEAD_DIM)
            qb, kb, vb, dob = _bf(q_ref[:, sl]), _bf(kv_ref[:, sl]), _bf(kv_ref[:, slv]), _bf(do_ref[:, sl])
            s = _dot(qb, kb, 1, 1) * (X_HEAD_DIM ** -0.5)
            e = jnp.exp(s - jnp.max(s, axis=-1, keepdims=True))
            pr = e * (1.0 / jnp.sum(e, axis=-1, keepdims=True))
            dpr = _dot(dob, vb, 1, 1)
            ds = _bf(pr * (dpr - jnp.sum(pr * dpr, axis=-1, keepdims=True)) * (X_HEAD_DIM ** -0.5))
            dq_ref[:, sl] = _bf(_dot(ds, kb, 1, 0))
            dkv_ref[:, sl] += _dot(ds, qb, 0, 0)
            dkv_ref[:, slv] += _dot(_bf(pr), dob, 0, 0)

    return pl.pallas_call(
        body, name="xattn_bwd", grid=(t // tq,),
        in_specs=[pl.BlockSpec((tq, d), lambda i: (i, 0)), _full((nm, 2 * d)), pl.BlockSpec((tq, d), lambda i: (i, 0))],
        out_specs=[pl.BlockSpec((tq, d), lambda i: (i, 0)), _full((nm, 2 * d))],
        out_shape=[jax.ShapeDtypeStruct((t, d), BF16), jax.ShapeDtypeStruct((nm, 2 * d), F32)],
        compiler_params=_params(),
    )(q, kv, do)


CONV_TN = 256


def _shift_rows(u, row, t, delta):
    if delta < 0:
        return jnp.where(row == 0, 0.0, pltpu.roll(u, 1, 0))
    return jnp.where(row == t - 1, 0.0, pltpu.roll(u, t - 1, 0))


def _conv_gate_fwd(u, conv_w, conv_b):
    t = u.shape[0]
    nt = D_FF // CONV_TN

    def body(ug_ref, uv_ref, wg_ref, wv_ref, bg_ref, bv_ref, a_ref):
        row = lax.broadcasted_iota(jnp.int32, (t, CONV_TN), 0)

        def conv(u_ref, w_ref, b_ref):
            uu = u_ref[...]
            return (b_ref[...] + _shift_rows(uu, row, t, -1) * w_ref[0:1, :] + uu * w_ref[1:2, :]
                    + _shift_rows(uu, row, t, 1) * w_ref[2:3, :])

        gate = conv(ug_ref, wg_ref, bg_ref)
        a_ref[...] = _bf(gate * _sigmoid(gate) * conv(uv_ref, wv_ref, bv_ref))

    col = lambda rows, off: pl.BlockSpec((rows, CONV_TN), lambda j: (0, j + off))
    return pl.pallas_call(
        body, name="conv_gate_fwd", grid=(nt,),
        in_specs=[col(t, 0), col(t, nt), col(3, 0), col(3, nt), col(1, 0), col(1, nt)],
        out_specs=col(t, 0),
        out_shape=jax.ShapeDtypeStruct((t, D_FF), BF16),
        compiler_params=_params(),
    )(u, u, conv_w, conv_w, conv_b, conv_b)


def _conv_gate_bwd(u, conv_w, conv_b, da, after=()):
    t = u.shape[0]
    nt = D_FF // CONV_TN

    def body(ug_ref, uv_ref, wg_ref, wv_ref, bg_ref, bv_ref, da_ref, dug_ref, dwg_ref, dbg_ref, duv_ref, dwv_ref, dbv_ref):
        row = lax.broadcasted_iota(jnp.int32, (t, CONV_TN), 0)
        ug, uv = ug_ref[...], uv_ref[...]
        ug_m, ug_p = _shift_rows(ug, row, t, -1), _shift_rows(ug, row, t, 1)
        uv_m, uv_p = _shift_rows(uv, row, t, -1), _shift_rows(uv, row, t, 1)
        gate = bg_ref[...] + ug_m * wg_ref[0:1, :] + ug * wg_ref[1:2, :] + ug_p * wg_ref[2:3, :]
        val = bv_ref[...] + uv_m * wv_ref[0:1, :] + uv * wv_ref[1:2, :] + uv_p * wv_ref[2:3, :]
        sg = _sigmoid(gate)
        da_ = da_ref[...].astype(F32)

        def emit(dacc, um, uu, up, w_ref, du_ref, dw_ref, db_ref):
            du_ref[...] = _bf(_shift_rows(dacc, row, t, 1) * w_ref[0:1, :] + dacc * w_ref[1:2, :]
                              + _shift_rows(dacc, row, t, -1) * w_ref[2:3, :])
            dw_ref[0:1, :] = jnp.sum(dacc * um, axis=0, keepdims=True)
            dw_ref[1:2, :] = jnp.sum(dacc * uu, axis=0, keepdims=True)
            dw_ref[2:3, :] = jnp.sum(dacc * up, axis=0, keepdims=True)
            db_ref[...] = jnp.sum(dacc, axis=0, keepdims=True)

        emit(da_ * val * (sg * (1.0 + gate * (1.0 - sg))), ug_m, ug, ug_p, wg_ref, dug_ref, dwg_ref, dbg_ref)
        emit(da_ * gate * sg, uv_m, uv, uv_p, wv_ref, duv_ref, dwv_ref, dbv_ref)

    col = lambda rows, off: pl.BlockSpec((rows, CONV_TN), lambda j: (0, j + off))
    half_shapes = [jax.ShapeDtypeStruct((t, D_FF), BF16), jax.ShapeDtypeStruct((3, D_FF), F32),
                   jax.ShapeDtypeStruct((1, D_FF), F32)]
    outs = _call(
        body, (u, u, conv_w, conv_w, conv_b, conv_b, da), name="conv_gate_bwd", grid=(nt,),
        in_specs=[col(t, 0), col(t, nt), col(3, 0), col(3, nt), col(1, 0), col(1, nt), col(t, 0)],
        out_specs=[col(t, 0), col(3, 0), col(1, 0)] * 2, out_shape=half_shapes * 2, after=after)
    return outs[:3], outs[3:]


def _row_tile(r, cap):
    best = r
    for cand in range(16, cap + 1, 16):
        if r % cand == 0:
            best = cand
    return best


def _sum_parts(own, got, me, name, tr=256):
    _, r, c = got.shape
    tr = _row_tile(r, tr)

    def body(me_ref, own_ref, got_ref, o_ref):
        mine = own_ref[...].astype(F32)
        acc = None
        for i in range(N_DEV):
            term = jnp.where(me_ref[0] == i, mine, got_ref[i].astype(F32))
            acc = term if acc is None else acc + term
        o_ref[...] = acc

    return pl.pallas_call(
        body, name=name,
        grid_spec=pltpu.PrefetchScalarGridSpec(
            num_scalar_prefetch=1, grid=(r // tr,),
            in_specs=[pl.BlockSpec((None, tr, c), lambda i, me_ref: (me_ref[0], i, 0)),
                      pl.BlockSpec((N_DEV, tr, c), lambda i, me_ref: (0, i, 0))],
            out_specs=pl.BlockSpec((tr, c), lambda i, me_ref: (i, 0))),
        out_shape=jax.ShapeDtypeStruct((r, c), F32),
        compiler_params=_params(),
    )(me, own, got)


def _adamw_math(w_, g_, m_, v_):
    m_ = ADAM_B1 * m_ + (1.0 - ADAM_B1) * g_
    v_ = ADAM_B2 * v_ + (1.0 - ADAM_B2) * (g_ * g_)
    m_hat = m_ / (1.0 - ADAM_B1 ** ADAM_STEP)
    v_hat = v_ / (1.0 - ADAM_B2 ** ADAM_STEP)
    return -ADAM_LR * (m_hat / (jnp.sqrt(v_hat) + ADAM_EPS) + ADAM_WD * w_), m_, v_


def _sum_adamw(own, got, me, w, m, v, name, tr=256):
    _, r, c = got.shape
    tr = _row_tile(r, tr)

    def body(me_ref, own_ref, got_ref, w_ref, m_ref, v_ref, g_ref, d_ref, nm_ref, nv_ref):
        mine = own_ref[...].astype(F32)
        acc = None
        for i in range(N_DEV):
            term = jnp.where(me_ref[0] == i, mine, got_ref[i].astype(F32))
            acc = term if acc is None else acc + term
        g_ref[...] = acc
        d_ref[...], nm_ref[...], nv_ref[...] = _adamw_math(w_ref[...], acc, m_ref[...], v_ref[...])

    blk = pl.BlockSpec((tr, c), lambda i, me_ref: (i, 0))
    return pl.pallas_call(
        body, name=name,
        grid_spec=pltpu.PrefetchScalarGridSpec(
            num_scalar_prefetch=1, grid=(r // tr,),
            in_specs=[pl.BlockSpec((None, tr, c), lambda i, me_ref: (me_ref[0], i, 0)),
                      pl.BlockSpec((N_DEV, tr, c), lambda i, me_ref: (0, i, 0)), blk, blk, blk],
            out_specs=[blk] * 4),
        out_shape=[jax.ShapeDtypeStruct((r, c), F32)] * 4,
        compiler_params=_params(),
    )(me, own, got, w, m, v)


def _adamw(w, g, m, v, name, tr=256):
    r, c = w.shape
    tr = _row_tile(r, tr)

    def body(w_ref, g_ref, m_ref, v_ref, d_ref, nm_ref, nv_ref):
        d_ref[...], nm_ref[...], nv_ref[...] = _adamw_math(w_ref[...], g_ref[...], m_ref[...], v_ref[...])

    blk = pl.BlockSpec((tr, c), lambda i: (i, 0))
    return pl.pallas_call(
        body, name=name, grid=(r // tr,),
        in_specs=[blk] * 4, out_specs=[blk] * 3,
        out_shape=[jax.ShapeDtypeStruct((r, c), F32)] * 3,
        compiler_params=_params(),
    )(w, g, m, v)


def _mesh_pos():
    return lax.axis_index("x"), lax.axis_index("y"), lax.axis_index("c")


def _dev_index(px, py, pc):
    return 4 * px + 2 * py + pc


class _Gather:
    def __init__(self, arrs):
        self.arrs = list(arrs)
        n = len(self.arrs)
        self.out_shape = [jax.ShapeDtypeStruct((N_DEV,) + a.shape, a.dtype) for a in self.arrs]
        self.sems = [pltpu.SemaphoreType.DMA((7, n)), pltpu.SemaphoreType.DMA((7, n)), pltpu.SemaphoreType.DMA((n,))]

    def _ctx(self, ins, outs, sems):
        send_sems, recv_sems, local_sems = sems
        x, y, c = _mesh_pos()
        chips = [(1 - x, y), (x, 1 - y), (1 - x, 1 - y)]

        def copy(k, a, block, to, src=None):
            dst = outs[a].at[_dev_index(*block)]
            return pltpu.make_async_remote_copy(
                src_ref=dst if src is None else src, dst_ref=dst, send_sem=send_sems.at[k, a], recv_sem=recv_sems.at[k, a],
                device_id=to, device_id_type=MESH_T)

        n = len(ins)
        me, sibling = (x, y, c), (x, y, 1 - c)
        mine = [pltpu.make_async_copy(ins[a], outs[a].at[_dev_index(*me)], local_sems.at[a]) for a in range(n)]
        first = [copy(0, a, me, sibling, src=ins[a]) for a in range(n)]
        first += [copy(1 + j, a, me, (*chip, c), src=ins[a]) for j, chip in enumerate(chips) for a in range(n)]
        passed = [copy(4 + j, a, (*chip, c), sibling) for j, chip in enumerate(chips) for a in range(n)]
        return n, c, me, sibling, chips, copy, mine, first, passed

    def start(self, ins, outs, sems):
        _, _, _, _, _, _, mine, first, _ = self._ctx(ins, outs, sems)
        for cp in mine + first:
            cp.start()

    def forward(self, ins, outs, sems):
        n, c, me, _, chips, copy, _, _, passed = self._ctx(ins, outs, sems)
        for j, chip in enumerate(chips):
            for a in range(n):
                copy(1 + j, a, (*chip, c), me).wait_recv()
                passed[j * n + a].start()

    def finish(self, ins, outs, sems):
        n, c, me, sibling, chips, copy, mine, first, passed = self._ctx(ins, outs, sems)
        for a in range(n):
            copy(0, a, sibling, me).wait_recv()
        for j, chip in enumerate(chips):
            for a in range(n):
                copy(4 + j, a, (*chip, 1 - c), me).wait_recv()
        for cp in first + passed:
            cp.wait_send()
        for cp in mine:
            cp.wait()


def _comm_alone(comm, name):
    n = len(comm.arrs)

    def body(*refs):
        ins, outs, sems = refs[:n], refs[n:2 * n], refs[2 * n:]
        comm.start(ins, outs, sems)
        if comm.forward is not None:
            comm.forward(ins, outs, sems)
        comm.finish(ins, outs, sems)

    any_spec = pl.BlockSpec(memory_space=pl.ANY)
    return pl.pallas_call(body, name=name, in_specs=[any_spec] * n, out_specs=[any_spec] * n, out_shape=comm.out_shape,
                          scratch_shapes=comm.sems)(*comm.arrs)


def _peers(x, y, c):
    out = []
    for k in range(1, N_DEV):
        pos = (1 - x if k & 4 else x, 1 - y if k & 2 else y, 1 - c if k & 1 else c)
        out.append((k, pos, _dev_index(*pos)))
    return out


def _exchange_begin(arrs, tag):
    n = len(arrs)
    lands = [lax.empty(a.shape, a.dtype) for a in arrs]

    def start_body(*refs):
        ins, land = refs[:n], refs[n:2 * n]
        send_sems, recv_sems, token = refs[2 * n], refs[2 * n + 1], refs[-1]
        x, y, c = _mesh_pos()
        me_i = _dev_index(x, y, c)
        for k, pos, peer_i in _peers(x, y, c):
            for a in range(n):
                pltpu.make_async_remote_copy(
                    src_ref=ins[a].at[peer_i], dst_ref=land[a].at[me_i], send_sem=send_sems.at[(k - 1) * n + a],
                    recv_sem=recv_sems.at[(k - 1) * n + a], device_id=pos, device_id_type=MESH_T).start()
        token[...] = jnp.zeros_like(token)

    hbm = pl.BlockSpec(memory_space=pltpu.HBM)
    sem = pl.BlockSpec(memory_space=pltpu.SEMAPHORE)
    thru = [pltpu.HBM(a.shape, a.dtype) for a in arrs]
    outs = pl.pallas_call(
        start_body, name="exchange_start_" + tag,
        out_shape=[pltpu.SemaphoreType.DMA((7 * n,)), pltpu.SemaphoreType.DMA((7 * n,))] + thru + thru
        + [jax.ShapeDtypeStruct((8, LANE), F32)],
        in_specs=[hbm] * (2 * n), out_specs=[sem, sem] + [hbm] * (2 * n) + [pl.BlockSpec(memory_space=pltpu.VMEM)],
        input_output_aliases={i: 2 + i for i in range(2 * n)},
        compiler_params=pltpu.CompilerParams(has_side_effects=pltpu.SideEffectType.DATAFLOW_SIDE_EFFECTING),
    )(*[pltpu.with_memory_space_constraint(a, pltpu.HBM) for a in list(arrs) + list(lands)])
    return (tag, n, outs[0], outs[1], outs[2:2 + n], outs[2 + n:2 + 2 * n]), outs[-1]


def _exchange_end(handle, after):
    tag, n, send_sems, recv_sems, srcs, lands = handle

    def body(*refs):
        ins, land = refs[:n], refs[n:2 * n]
        send_sems_, recv_sems_ = refs[2 * n], refs[2 * n + 1]
        x, y, c = _mesh_pos()
        me_i = _dev_index(x, y, c)
        for k, pos, peer_i in _peers(x, y, c):
            for a in range(n):
                cp = pltpu.make_async_remote_copy(
                    src_ref=ins[a].at[peer_i], dst_ref=land[a].at[me_i], send_sem=send_sems_.at[(k - 1) * n + a],
                    recv_sem=recv_sems_.at[(k - 1) * n + a], device_id=pos, device_id_type=MESH_T)
                cp.wait_send()
                cp.wait_recv()

    hbm = pl.BlockSpec(memory_space=pltpu.HBM)
    sem = pl.BlockSpec(memory_space=pltpu.SEMAPHORE)
    outs = pl.pallas_call(
        body, name="exchange_end_" + tag, out_shape=[pltpu.HBM(a.shape, a.dtype) for a in list(srcs) + list(lands)],
        in_specs=[hbm] * (2 * n) + [sem, sem, pl.BlockSpec(memory_space=pl.ANY)], out_specs=[hbm] * (2 * n),
        input_output_aliases={i: i for i in range(2 * n)},
        compiler_params=pltpu.CompilerParams(has_side_effects=pltpu.SideEffectType.DATAFLOW_SIDE_EFFECTING),
    )(*srcs, *lands, send_sems, recv_sems, after)
    return list(zip(outs[:n], outs[n:]))


def _gather_begin(arrs, tag):
    n = len(arrs)
    me = _dev_index(*_mesh_pos())
    lands = [lax.dynamic_update_slice(lax.empty((N_DEV,) + a.shape, a.dtype), a[None], (me,) + (0,) * a.ndim) for a in arrs]

    def start_body(*refs):
        ins, land = refs[:n], refs[n:2 * n]
        send_sems, recv_sems, token = refs[2 * n], refs[2 * n + 1], refs[-1]
        x, y, c = _mesh_pos()
        me_i = _dev_index(x, y, c)
        for a in range(n):
            for k, pos, _ in _peers(x, y, c):
                pltpu.make_async_remote_copy(
                    src_ref=ins[a], dst_ref=land[a].at[me_i], send_sem=send_sems.at[(k - 1) * n + a],
                    recv_sem=recv_sems.at[(k - 1) * n + a], device_id=pos, device_id_type=MESH_T).start()
        token[...] = jnp.zeros_like(token)

    hbm = pl.BlockSpec(memory_space=pltpu.HBM)
    sem = pl.BlockSpec(memory_space=pltpu.SEMAPHORE)
    thru = [pltpu.HBM(a.shape, a.dtype) for a in list(arrs) + lands]
    outs = pl.pallas_call(
        start_body, name="gather_start_" + tag,
        out_shape=[pltpu.SemaphoreType.DMA((7 * n,)), pltpu.SemaphoreType.DMA((7 * n,))] + thru
        + [jax.ShapeDtypeStruct((8, LANE), F32)],
        in_specs=[hbm] * (2 * n), out_specs=[sem, sem] + [hbm] * (2 * n) + [pl.BlockSpec(memory_space=pltpu.VMEM)],
        input_output_aliases={i: 2 + i for i in range(2 * n)},
        compiler_params=pltpu.CompilerParams(has_side_effects=pltpu.SideEffectType.DATAFLOW_SIDE_EFFECTING),
    )(*[pltpu.with_memory_space_constraint(a, pltpu.HBM) for a in list(arrs) + lands])
    return (tag, n, outs[0], outs[1], outs[2:2 + n], outs[2 + n:2 + 2 * n]), outs[-1]


def _gather_end(handle, which, after):
    tag, n, send_sems, recv_sems, srcs, lands = handle
    m = len(which)

    def body(*refs):
        ins, land = refs[:m], refs[m:2 * m]
        send_sems_, recv_sems_ = refs[2 * m], refs[2 * m + 1]
        x, y, c = _mesh_pos()
        me_i = _dev_index(x, y, c)
        for j, a in enumerate(which):
            for k, pos, _ in _peers(x, y, c):
                cp = pltpu.make_async_remote_copy(
                    src_ref=ins[j], dst_ref=land[j].at[me_i], send_sem=send_sems_.at[(k - 1) * n + a],
                    recv_sem=recv_sems_.at[(k - 1) * n + a], device_id=pos, device_id_type=MESH_T)
                cp.wait_send()
                cp.wait_recv()

    hbm = pl.BlockSpec(memory_space=pltpu.HBM)
    sem = pl.BlockSpec(memory_space=pltpu.SEMAPHORE)
    ops = [srcs[a] for a in which] + [lands[a] for a in which]
    outs = pl.pallas_call(
        body, name="gather_end_%s_%s" % (tag, "_".join(str(a) for a in which)),
        out_shape=[pltpu.HBM(a.shape, a.dtype) for a in ops],
        in_specs=[hbm] * (2 * m) + [sem, sem, pl.BlockSpec(memory_space=pl.ANY)], out_specs=[hbm] * (2 * m),
        input_output_aliases={i: i for i in range(2 * m)},
        compiler_params=pltpu.CompilerParams(has_side_effects=pltpu.SideEffectType.DATAFLOW_SIDE_EFFECTING),
    )(*ops, send_sems, recv_sems, after)
    return list(outs[m:])


def _call(body, operands, *, name, grid, in_specs, out_specs, out_shape, scratch_shapes=(), after=()):
    ni, nd = len(in_specs), len(after)

    def ordered(*refs):
        body(*refs[:ni], *refs[ni + nd:])

    outs = pl.pallas_call(
        ordered, name=name, grid=grid, in_specs=list(in_specs) + [pl.BlockSpec(memory_space=pl.ANY)] * nd,
        out_specs=out_specs, out_shape=out_shape, scratch_shapes=list(scratch_shapes), compiler_params=_params(),
    )(*operands, *after)
    return list(outs)


def _rows(a):
    return a.reshape(-1, a.shape[-1])


def _slots(a):
    return a.reshape(N_DEV, -1, a.shape[-1])


def _local_step(x, mem, tgt, w, wire):
    tables = _rope_tables(x.shape[0])
    bd = _head_sum_matrix()
    two = lambda g: jnp.tile(g, (1, 2))
    gq2, gk2 = two(w["q_norm_g"]), two(w["k_norm_g"])

    w_in_t, hg_lb, conv_w = _comm_alone(_Gather([wire["w_in"], w["hg_lb"].reshape(4, -1), w["conv_w"][0]]), "gather_w_in")
    w_in_t = _rows(w_in_t)
    hg_lb = jnp.transpose(hg_lb.reshape(N_DEV, 2, 2, -1), (1, 2, 0, 3)).reshape(2, 2, HG_DIM)
    conv_w = jnp.transpose(conv_w, (1, 0, 2)).reshape(3, 2 * D_FF)
    lb_a0, lb_a1 = hg_lb[:, 0, :], hg_lb[:, 1, :]

    order = ("w_out", "w_xq", "w_xkv", "w_xo", "w_up", "w_down")
    w_in_t, *later = lax.optimization_barrier((w_in_t, *[wire[n] for n in order]))
    fetch, started = _gather_begin(later, "w")
    take = lambda n, after: _rows(_gather_end(fetch, [order.index(n)], after)[0])

    h1, p = _norm_mm(x, w["pre_mix_g"], w_in_t, True, N_IN, "in_proj", after=[started])
    qr, kr = _qk_prep(p, gq2, gk2, tables, bd)
    att, att32, lse = _attn_fwd(qr, kr, p)
    lb = _lower_bounds(lb_a0, lb_a1)
    of, s_f = _hgrn_fwd(p, lb[0:1], False)
    ob, s_b = _hgrn_fwd(p, lb[1:2], True)
    w_out = take("w_out", ob)
    cat, mixed, x1 = _mix_out(att, of, ob, p, w["hg_out_norm_g"], w_out, w["post_mix_g"], x)

    w_xq = take("w_xq", x1)
    h2, q2 = _norm_mm(x1, w["pre_x_g"], w_xq, False, 1024, "xq_proj")
    w_xkv_t = take("w_xkv", q2)
    mn, kv = _norm_mm(mem, w["mem_norm_g"], w_xkv_t, True, 2 * D_MODEL, "xkv_proj")
    o2 = _xattn_fwd(q2, kv)
    w_xo = take("w_xo", o2)
    y2, x2 = _mm_postnorm_res(o2, w_xo, w["post_x_g"], x1, "xo_proj")

    w_up_t = take("w_up", x2)
    h3, u = _norm_mm(x2, w["pre_ffn_g"], w_up_t, True, 2 * D_FF, "up_proj", tm=256)
    a = _conv_gate_fwd(u, conv_w, w["conv_b"])
    w_down = take("w_down", a)
    y3, dx3, loss = _mm_postnorm_res_loss(a, w_down, w["post_ffn_g"], x2, tgt, "down_proj")

    g, pending = {}, {}
    dy3, da, g["post_ffn_g"] = _postnorm_bwd_mm(dx3, y3, w["post_ffn_g"], w_down, "down_bwd", BF16)
    pending["down"], started = _exchange_begin([_slots(_dw(a, dy3, "dw_down"))], "down")
    (du_g, dcw_g, dcb_g), (du_v, dcw_v, dcb_v) = _conv_gate_bwd(u, conv_w, w["conv_b"], da, after=[started])
    g["conv_w"] = jnp.concatenate([dcw_g, dcw_v], axis=1)
    g["conv_b"] = jnp.concatenate([dcb_g, dcb_v], axis=1)
    pending["up"], started = _exchange_begin([_slots(_dw([du_g, du_v], h3, "dw_up"))], "up")
    dx2, g["pre_ffn_g"] = _mm_prenorm_bwd([du_g, du_v], w_up_t, True, x2, w["pre_ffn_g"], dx3, "up_bwd", after=[started], tm=256)

    dy2, do2, g["post_x_g"] = _postnorm_bwd_mm(dx2, y2, w["post_x_g"], w_xo, "xo_bwd", BF16)
    dw_xo = _dw(o2, dy2, "dw_xo")
    dq2, dkv = _xattn_bwd(q2, kv, do2)
    dw_xq = _dw(h2, dq2, "dw_xq")
    dkvb = _bf(dkv)
    dw_xkv = _dw(dkvb, mn, "dw_xkv")
    pending["x"], started = _exchange_begin([_slots(dw_xo), _slots(dw_xq), _slots(dw_xkv)], "x")
    _, g["mem_norm_g"] = _mm_prenorm_bwd(dkvb, w_xkv_t, True, mem, w["mem_norm_g"], jnp.zeros_like(mem), "xkv_bwd")
    dx1, g["pre_x_g"] = _mm_prenorm_bwd(dq2, w_xq, False, x1, w["pre_x_g"], dx2, "xq_bwd", after=[started])

    dmixed, dcat, g["post_mix_g"] = _postnorm_bwd_mm(dx1, mixed, w["post_mix_g"], w_out, "out_bwd")
    pending["out"], started = _exchange_begin([_slots(_dw(cat, dmixed, "dw_out"))], "out")
    do, dhg, g["hg_out_norm_g"] = _rec_bwd(dcat, of, ob, p, w["hg_out_norm_g"])
    dhq_f, dz_f, dhi_f, dlb_f = _hgrn_bwd(p, lb[0:1], do, s_f, False, after=[started])
    dhq_b, dz_b, dhi_b, dlb_b = _hgrn_bwd(p, lb[1:2], do, s_b, True)
    d_a0, d_a1 = _lower_bounds_bwd(lb_a0, lb_a1, jnp.concatenate([dlb_f, dlb_b], axis=0))
    g["hg_lb"] = jnp.stack([d_a0, d_a1], axis=1)
    dqr, dkr, dv = _attn_bwd(qr, kr, p, dcat, att32, lse)
    dp_qk, dgq, dgk = _qk_prep_bwd(p, dqr, dkr, gq2, gk2, tables, bd)
    g["q_norm_g"], g["k_norm_g"] = dgq, dgk
    dp = _assemble_dp(dp_qk, dv, dhq_f, dhq_b, dz_f, dz_b, dhi_f, dhi_b, dhg)
    pending["in"], started = _exchange_begin([_slots(_dw(dp, h1, "dw_in"))], "in")
    dx, g["pre_mix_g"] = _mm_prenorm_bwd(dp, w_in_t, True, x, w["pre_mix_g"], dx1, "in_bwd", after=[started])
    return loss, dx, g, pending


_COL_SHARDED = ("w_in", "w_xkv", "w_up")
_ROW_SHARDED = ("w_out", "w_xq", "w_xo", "w_down")
_REPLICATED = ("pre_mix_g", "q_norm_g", "k_norm_g", "hg_out_norm_g", "post_mix_g", "pre_x_g", "mem_norm_g", "post_x_g",
               "pre_ffn_g", "conv_b", "post_ffn_g")
_WEIGHTS = ("pre_mix_g", "w_in", "q_norm_g", "k_norm_g", "hg_lb", "hg_out_norm_g", "w_out", "post_mix_g", "pre_x_g",
            "mem_norm_g", "w_xq", "w_xkv", "w_xo", "post_x_g", "pre_ffn_g", "w_up", "conv_w", "conv_b", "w_down",
            "post_ffn_g")
_ADAM_TRANSPOSED = ("w_in", "w_up")
PACK_W = 1024


def _small_plan(shapes):
    plan, r = [], 0
    for vi, (rows, cols) in enumerate(shapes):
        for i in range(rows):
            for c0 in range(0, cols, PACK_W):
                plan.append((vi, i, c0, min(PACK_W, cols - c0), r))
                r += 1
    return plan, -(-r // 8) * 8


def _pack_small(vals):
    plan, nrows = _small_plan([val.shape for val in vals])

    def body(*refs):
        ins, out = refs[:-1], refs[-1]
        out[...] = jnp.zeros_like(out)
        for vi, i, c0, width, r in plan:
            out[r:r + 1, 0:width] = ins[vi][i:i + 1, c0:c0 + width]

    return pl.pallas_call(body, name="pack_small", out_shape=jax.ShapeDtypeStruct((nrows, PACK_W), F32))(*vals)


def _sum_unpack_small(packs, shapes):
    plan, _ = _small_plan(shapes)

    def body(*refs):
        p_ref, outs = refs[0], refs[1:]
        acc = p_ref[0]
        for i in range(1, N_DEV):
            acc = acc + p_ref[i]
        for vi, i, c0, width, r in plan:
            outs[vi][i:i + 1, c0:c0 + width] = acc[r:r + 1, 0:width]

    return pl.pallas_call(body, name="sum_unpack_small", out_shape=[jax.ShapeDtypeStruct(s, F32) for s in shapes])(packs)


def _adamw_many(ws, gs, ms, vs):
    n = len(ws)

    def body(*refs):
        w_refs, g_refs, m_refs, v_refs = (refs[k * n:(k + 1) * n] for k in range(4))
        d_refs, nm_refs, nv_refs = (refs[(4 + k) * n:(5 + k) * n] for k in range(3))
        for k in range(n):
            g_ = g_refs[k][...]
            m_ = ADAM_B1 * m_refs[k][...] + (1.0 - ADAM_B1) * g_
            v_ = ADAM_B2 * v_refs[k][...] + (1.0 - ADAM_B2) * (g_ * g_)
            m_hat = m_ / (1.0 - ADAM_B1 ** ADAM_STEP)
            v_hat = v_ / (1.0 - ADAM_B2 ** ADAM_STEP)
            d_refs[k][...] = -ADAM_LR * (m_hat / (jnp.sqrt(v_hat) + ADAM_EPS) + ADAM_WD * w_refs[k][...])
            nm_refs[k][...] = m_
            nv_refs[k][...] = v_

    shapes = [jax.ShapeDtypeStruct(a.shape, F32) for a in ws]
    outs = pl.pallas_call(body, name="adamw_small", out_shape=shapes * 3)(*ws, *gs, *ms, *vs)
    return outs[:n], outs[n:2 * n], outs[2 * n:]


def kernel(x, mem, pre_mix_g, w_in, q_norm_g, k_norm_g, hg_lb, hg_out_norm_g, w_out, post_mix_g, pre_x_g, mem_norm_g, w_xq, w_xkv, w_xo, post_x_g, pre_ffn_g, w_up, conv_w, conv_b, w_down, post_ffn_g, loss_target, m_pre_mix_g, m_w_in, m_q_norm_g, m_k_norm_g, m_hg_lb, m_hg_out_norm_g, m_w_out, m_post_mix_g, m_pre_x_g, m_mem_norm_g, m_w_xq, m_w_xkv, m_w_xo, m_post_x_g, m_pre_ffn_g, m_w_up, m_conv_w, m_conv_b, m_w_down, m_post_ffn_g, v_pre_mix_g, v_w_in, v_q_norm_g, v_k_norm_g, v_hg_lb, v_hg_out_norm_g, v_w_out, v_post_mix_g, v_pre_x_g, v_mem_norm_g, v_w_xq, v_w_xkv, v_w_xo, v_post_x_g, v_pre_ffn_g, v_w_up, v_conv_w, v_conv_b, v_w_down, v_post_ffn_g):
    args = dict(locals())
    w = {n: args[n] for n in _WEIGHTS}
    m = {n: args["m_" + n] for n in _WEIGHTS}
    v = {n: args["v_" + n] for n in _WEIGHTS}
    me = _dev_index(*_mesh_pos())

    wire = {n: _bf(w[n][0].T) for n in _COL_SHARDED}
    wire.update({n: _bf(w[n][0]) for n in _ROW_SHARDED})

    loss, grad_x, g, pending = _local_step(x[0], mem[0], loss_target[0], w, wire)

    grads, delta, new_m, new_v = {}, {}, {}, {}

    me_arr = jnp.reshape(me, (1,)).astype(jnp.int32)

    def update(n, parts):
        if n in _ADAM_TRANSPOSED:
            outs = _sum_adamw(*parts, me_arr, w[n][0].T, m[n][0].T, v[n][0].T, "update_" + n)
            grads[n], delta[n], new_m[n], new_v[n] = (a.T[None] for a in outs)
        elif n in _COL_SHARDED:
            gsum = _sum_parts(*parts, me_arr, "sum_" + n).T
            grads[n] = gsum[None]
            delta[n], new_m[n], new_v[n] = (a[None] for a in _adamw(w[n][0], gsum, m[n][0], v[n][0], "adamw_" + n))
        else:
            outs = _sum_adamw(*parts, me_arr, w[n][0], m[n][0], v[n][0], "update_" + n)
            grads[n], delta[n], new_m[n], new_v[n] = (a[None] for a in outs)

    small = list(_REPLICATED) + ["hg_lb", "conv_w"]
    vals = [g[n] for n in _REPLICATED] + [g["hg_lb"].reshape(4, HG_DIM), g["conv_w"], jnp.pad(loss, ((0, 0), (0, LANE - 1)))]
    shapes = [val.shape for val in vals]
    fetch_small, _ = _gather_begin([_pack_small(vals)], "small")

    after = grad_x
    for tag, names in (("down", ["w_down"]), ("up", ["w_up"]), ("x", ["w_xo", "w_xq", "w_xkv"]), ("out", ["w_out"]),
                       ("in", ["w_in"])):
        for n, parts in zip(names, _exchange_end(pending[tag], after)):
            update(n, parts)
            after = new_v[n]

    (packs,) = _gather_end(fetch_small, [0], after)
    summed = _sum_unpack_small(packs, shapes)
    loss = summed[-1][0, 0]
    for n, s in zip(small, summed[:-1]):
        grads[n] = s
    fold = lambda v2: v2[:, :ATT_HEAD_DIM] + v2[:, ATT_HEAD_DIM:]
    grads["q_norm_g"], grads["k_norm_g"] = fold(grads["q_norm_g"]), fold(grads["k_norm_g"])
    grads["hg_lb"] = lax.dynamic_slice_in_dim(grads["hg_lb"].reshape(2, 2, HG_DIM), me * (HG_DIM // N_DEV),
                                              HG_DIM // N_DEV, axis=2)
    grads["conv_w"] = lax.dynamic_slice_in_dim(grads["conv_w"], me * (2 * D_FF // N_DEV), 2 * D_FF // N_DEV, axis=1)[None]

    flat2 = lambda a: a.reshape(-1, a.shape[-1])
    outs = _adamw_many(*[[flat2(d[n]) for n in small] for d in (w, grads, m, v)])
    for dst, vals in zip((delta, new_m, new_v), outs):
        for n, val in zip(small, vals):
            dst[n] = val.reshape(w[n].shape)

    return (loss, grad_x[None], *[grads[n] for n in _WEIGHTS], *[delta[n] for n in _WEIGHTS],
            *[new_m[n] for n in _WEIGHTS], *[new_v[n] for n in _WEIGHTS])
```

```python
import jax
import jax.numpy as jnp
from jax import lax
from jax.experimental import pallas as pl
from jax.experimental.pallas import tpu as pltpu

F32 = jnp.float32
BF16 = jnp.bfloat16

D_MODEL = 1024
GRID_W = 64
EPS = 1e-6
ATT_HEADS = 8
ATT_HEAD_DIM = 64
ATT_Q_DIM = 512
ATT_KV_DIM = 128
ROPE_THETA = 10000.0
HG_HEADS = 4
HG_DIM = 512
HG_CHUNK = 32
HG_CHUNK_LOG2 = 5
HG_BLOCK_FWD = 256
HG_BLOCK_BWD = 128
N_IN = 3328
X_HEADS = 4
X_HEAD_DIM = 256
D_FF = 2816
N_DEV = 8
LANE = 128
ADAM_LR = 0.001
ADAM_B1 = 0.9
ADAM_B2 = 0.999
ADAM_EPS = 1e-08
ADAM_WD = 0.01
ADAM_STEP = 10
VMEM_LIMIT = 56 * 1024 * 1024

MESH_T = pl.DeviceIdType.MESH


def _params(**kw):
    return pltpu.CompilerParams(vmem_limit_bytes=VMEM_LIMIT, **kw)


def _dot(a, b, ca, cb):
    return lax.dot_general(a, b, (((ca,), (cb,)), ((), ())), preferred_element_type=F32)


def _bf(x):
    return x.astype(BF16)


def _sigmoid(x):
    return 1.0 / (1.0 + jnp.exp(-x))


def _rms_fwd(x, g):
    r = lax.rsqrt(jnp.mean(x * x, axis=-1, keepdims=True) + EPS)
    return x * r * g


def _rms_bwd(dy, x, g):
    r = lax.rsqrt(jnp.mean(x * x, axis=-1, keepdims=True) + EPS)
    xh = x * r
    dg = jnp.sum(dy * xh, axis=0, keepdims=True)
    t = dy * g
    dx = r * (t - xh * jnp.mean(t * xh, axis=-1, keepdims=True))
    return dx, dg


def _full(shape):
    nd = len(shape)
    return pl.BlockSpec(shape, lambda *a: (0,) * nd)


def _norm_mm(x, g, w, trans, tn, name, after=(), tm=512):
    t, d = x.shape
    n = w.shape[0] if trans else w.shape[1]
    tm = min(tm, t)

    def body(x_ref, g_ref, w_ref, h_ref, p_ref):
        h = _bf(_rms_fwd(x_ref[...], g_ref[...]))
        h_ref[...] = h
        p_ref[...] = _dot(h, w_ref[...], 1, 1 if trans else 0)

    w_spec = pl.BlockSpec((tn, d), lambda i, j: (j, 0)) if trans else pl.BlockSpec((d, tn), lambda i, j: (0, j))
    return _call(
        body, (x, g, w), name=name, grid=(t // tm, n // tn),
        in_specs=[pl.BlockSpec((tm, d), lambda i, j: (i, 0)), _full((1, d)), w_spec],
        out_specs=[pl.BlockSpec((tm, d), lambda i, j: (i, 0)), pl.BlockSpec((tm, tn), lambda i, j: (i, j))],
        out_shape=[jax.ShapeDtypeStruct((t, d), BF16), jax.ShapeDtypeStruct((t, n), F32)], after=after)


def _mm_postnorm_res(a, w, g, res, name, tm=256):
    t, k = a.shape
    d = w.shape[1]

    def body(a_ref, w_ref, g_ref, res_ref, y_ref, o_ref):
        y = _dot(a_ref[...], w_ref[...], 1, 0)
        y_ref[...] = y
        o_ref[...] = res_ref[...] + _rms_fwd(y, g_ref[...])

    row = lambda width: pl.BlockSpec((tm, width), lambda i: (i, 0))
    return pl.pallas_call(
        body, name=name, grid=(t // tm,),
        in_specs=[row(k), _full((k, d)), _full((1, d)), row(d)],
        out_specs=[row(d), row(d)],
        out_shape=[jax.ShapeDtypeStruct((t, d), F32)] * 2,
        compiler_params=_params(),
    )(a, w, g, res)


def _mm_postnorm_res_loss(a, w, g, res, tgt, name, tm=256):
    t, k = a.shape
    d = w.shape[1]

    def body(a_ref, w_ref, g_ref, res_ref, tgt_ref, y_ref, dout_ref, loss_ref):
        @pl.when(pl.program_id(0) == 0)
        def _():
            loss_ref[...] = jnp.zeros_like(loss_ref)

        y = _dot(a_ref[...], w_ref[...], 1, 0)
        y_ref[...] = y
        diff = res_ref[...] + _rms_fwd(y, g_ref[...]) - tgt_ref[...]
        dout_ref[...] = diff * (1.0 / d)
        part = jnp.sum(jnp.sum(diff * diff, axis=-1, keepdims=True), axis=0, keepdims=True)
        loss_ref[...] += (0.5 / d) * part

    row = lambda width: pl.BlockSpec((tm, width), lambda i: (i, 0))
    return pl.pallas_call(
        body, name=name, grid=(t // tm,),
        in_specs=[row(k), _full((k, d)), _full((1, d)), row(d), row(d)],
        out_specs=[row(d), row(d), _full((1, 1))],
        out_shape=[jax.ShapeDtypeStruct((t, d), F32)] * 2 + [jax.ShapeDtypeStruct((1, 1), F32)],
        compiler_params=_params(),
    )(a, w, g, res, tgt)


def _postnorm_bwd_mm(dout, y, g, w, name, da_dtype=F32, tm=256):
    t, d = y.shape
    k = w.shape[0]

    def body(dout_ref, y_ref, g_ref, w_ref, dy_ref, da_ref, dg_ref):
        @pl.when(pl.program_id(0) == 0)
        def _():
            dg_ref[...] = jnp.zeros_like(dg_ref)

        dy, dg = _rms_bwd(dout_ref[...], y_ref[...], g_ref[...])
        dg_ref[...] += dg
        dyb = _bf(dy)
        dy_ref[...] = dyb
        da_ref[...] = _dot(dyb, w_ref[...], 1, 1).astype(da_dtype)

    row = lambda width: pl.BlockSpec((tm, width), lambda i: (i, 0))
    return pl.pallas_call(
        body, name=name, grid=(t // tm,),
        in_specs=[row(d), row(d), _full((1, d)), _full((k, d))],
        out_specs=[row(d), row(k), _full((1, d))],
        out_shape=[jax.ShapeDtypeStruct((t, d), BF16), jax.ShapeDtypeStruct((t, k), da_dtype), jax.ShapeDtypeStruct((1, d), F32)],
        compiler_params=_params(),
    )(dout, y, g, w)


def _mm_prenorm_bwd(dp, w, trans, x, g, dres, name, after=(), tm=512):
    dps = list(dp) if isinstance(dp, (list, tuple)) else [dp]
    nparts = len(dps)
    t, n_each = dps[0].shape
    d = x.shape[1]
    tm = min(tm, t)

    def body(*refs):
        dp_refs = refs[:nparts]
        w_ref, x_ref, g_ref, dres_ref, dx_ref, dg_ref = refs[nparts:]

        @pl.when(pl.program_id(0) == 0)
        def _():
            dg_ref[...] = jnp.zeros_like(dg_ref)

        dh = None
        for part, dp_ref in enumerate(dp_refs):
            cols = slice(part * n_each, (part + 1) * n_each)
            term = _dot(dp_ref[...], w_ref[cols, :], 1, 0) if trans else _dot(dp_ref[...], w_ref[:, cols], 1, 1)
            dh = term if dh is None else dh + term
        dx, dg = _rms_bwd(dh, x_ref[...], g_ref[...])
        dg_ref[...] += dg
        dx_ref[...] = dres_ref[...] + dx

    row = lambda width: pl.BlockSpec((tm, width), lambda i: (i, 0))
    return _call(
        body, (*dps, w, x, g, dres), name=name, grid=(t // tm,),
        in_specs=[row(n_each)] * nparts + [_full(w.shape), row(d), _full((1, d)), row(d)],
        out_specs=[row(d), _full((1, d))],
        out_shape=[jax.ShapeDtypeStruct((t, d), F32), jax.ShapeDtypeStruct((1, d), F32)], after=after)


def _dw(a, b, name, bcol=None, tka=256):
    parts = list(a) if isinstance(a, (list, tuple)) else [a]
    nparts = len(parts)
    t, ka_each = parts[0].shape
    bci, nb = (0, b.shape[1]) if bcol is None else bcol
    nt_each = ka_each // tka

    def body(*refs):
        a_refs, b_ref, o_ref = refs[:nparts], refs[nparts], refs[nparts + 1]
        i = pl.program_id(0)
        for part, a_ref in enumerate(a_refs):
            @pl.when((i >= part * nt_each) & (i < (part + 1) * nt_each))
            def _():
                o_ref[...] = _bf(_dot(a_ref[...], b_ref[...], 0, 0))

    a_spec = lambda part: pl.BlockSpec((t, tka), lambda i: (0, jnp.clip(i - part * nt_each, 0, nt_each - 1)))
    return pl.pallas_call(
        body, name=name, grid=(nt_each * nparts,),
        in_specs=[a_spec(part) for part in range(nparts)] + [pl.BlockSpec((t, nb), lambda i: (0, bci))],
        out_specs=pl.BlockSpec((tka, nb), lambda i: (i, 0)),
        out_shape=jax.ShapeDtypeStruct((nparts * ka_each, nb), BF16),
        compiler_params=_params(),
    )(*parts, b)


def _rope_tables(t):
    pos = jnp.arange(t)
    r = (pos // GRID_W).astype(F32)
    c = (pos % GRID_W).astype(F32)
    npair = ATT_HEAD_DIM // 4
    inv = jnp.power(ROPE_THETA, -jnp.arange(npair, dtype=F32) / npair)
    ang = jnp.concatenate([r[:, None] * inv, c[:, None] * inv], axis=-1)
    cos = jnp.repeat(jnp.cos(ang), 2, axis=-1)
    sin = jnp.repeat(jnp.sin(ang), 2, axis=-1)
    even = (jnp.arange(ATT_HEAD_DIM) % 2) == 0
    sa = jnp.where(even, -sin, 0.0)
    sb = jnp.where(even, 0.0, sin)
    two = lambda a: jnp.tile(a, (1, 2))
    return two(cos), two(sa), two(sb)


def _head_sum_matrix():
    a = jnp.arange(LANE) // ATT_HEAD_DIM
    return (a[:, None] == a[None, :]).astype(BF16)


def _head_mean(v, bd):
    hi = _bf(v)
    lo = _bf(v - hi.astype(F32))
    return (_dot(hi, bd, 1, 0) + _dot(lo, bd, 1, 0)) * (1.0 / ATT_HEAD_DIM)


def _qk_prep(p, gq, gk, tables, bd, tm=512):
    t = p.shape[0]
    tm = min(tm, t)
    cc, sa, sb = tables

    def body(p_ref, gq_ref, gk_ref, cc_ref, sa_ref, sb_ref, bd_ref, q_ref, k_ref):
        cc_, sa_, sb_, bd_ = cc_ref[...], sa_ref[...], sb_ref[...], bd_ref[...]
        low = lax.broadcasted_iota(jnp.int32, (tm, LANE), 1) < ATT_HEAD_DIM

        def normrope(xs, g):
            xn = xs * lax.rsqrt(_head_mean(xs * xs, bd_) + EPS) * g
            return xn * cc_ + pltpu.roll(xn, LANE - 1, 1) * sa_ + pltpu.roll(xn, 1, 1) * sb_

        for j in range(4):
            y = normrope(p_ref[:, j * LANE:(j + 1) * LANE], gq_ref[...]) * (ATT_HEAD_DIM ** -0.5)
            yr = pltpu.roll(y, ATT_HEAD_DIM, 1)
            if j // 2 == 0:
                h0, h1 = jnp.where(low, y, 0.0), jnp.where(low, yr, 0.0)
            else:
                h0, h1 = jnp.where(low, 0.0, yr), jnp.where(low, 0.0, y)
            q_ref[:, (2 * j) * LANE:(2 * j + 1) * LANE] = _bf(h0)
            q_ref[:, (2 * j + 1) * LANE:(2 * j + 2) * LANE] = _bf(h1)
        k_ref[...] = _bf(normrope(p_ref[:, ATT_Q_DIM:ATT_Q_DIM + LANE], gk_ref[...]))

    row = lambda width: pl.BlockSpec((tm, width), lambda i: (i, 0))
    return pl.pallas_call(
        body, name="qk_prep", grid=(t // tm,),
        in_specs=[row(ATT_Q_DIM + LANE), _full((1, LANE)), _full((1, LANE)), row(LANE), row(LANE), row(LANE),
                  _full((LANE, LANE))],
        out_specs=[row(ATT_HEADS * LANE), row(LANE)],
        out_shape=[jax.ShapeDtypeStruct((t, ATT_HEADS * LANE), BF16), jax.ShapeDtypeStruct((t, LANE), BF16)],
        compiler_params=_params(),
    )(p, gq, gk, cc, sa, sb, bd)


def _qk_prep_bwd(p, dq, dk, gq, gk, tables, bd, tm=512):
    t = p.shape[0]
    tm = min(tm, t)
    cc, sa, sb = tables

    def body(p_ref, dq_ref, dk_ref, gq_ref, gk_ref, cc_ref, sa_ref, sb_ref, bd_ref, dp_ref, dgq_ref, dgk_ref):
        @pl.when(pl.program_id(0) == 0)
        def _():
            dgq_ref[...] = jnp.zeros_like(dgq_ref)
            dgk_ref[...] = jnp.zeros_like(dgk_ref)

        cc_, sa_, sb_, bd_ = cc_ref[...], sa_ref[...], sb_ref[...], bd_ref[...]
        low = lax.broadcasted_iota(jnp.int32, (tm, LANE), 1) < ATT_HEAD_DIM

        def bwd(xs, g, dy):
            r = lax.rsqrt(_head_mean(xs * xs, bd_) + EPS)
            xh = xs * r
            dxn = dy * cc_ + pltpu.roll(dy * sa_, 1, 1) + pltpu.roll(dy * sb_, LANE - 1, 1)
            dg = jnp.sum(dxn * xh, axis=0, keepdims=True)
            tt = dxn * g
            return r * (tt - xh * _head_mean(tt * xh, bd_)), dg

        dgq = jnp.zeros((1, LANE), F32)
        for j in range(4):
            d0 = dq_ref[:, (2 * j) * LANE:(2 * j + 1) * LANE]
            d1 = dq_ref[:, (2 * j + 1) * LANE:(2 * j + 2) * LANE]
            if j // 2 == 0:
                dy = jnp.where(low, d0, pltpu.roll(d1, ATT_HEAD_DIM, 1))
            else:
                dy = jnp.where(low, pltpu.roll(d0, ATT_HEAD_DIM, 1), d1)
            dx, dg = bwd(p_ref[:, j * LANE:(j + 1) * LANE], gq_ref[...], dy * (ATT_HEAD_DIM ** -0.5))
            dp_ref[:, j * LANE:(j + 1) * LANE] = _bf(dx)
            dgq = dgq + dg
        dgq_ref[...] += dgq
        dx, dg = bwd(p_ref[:, ATT_Q_DIM:ATT_Q_DIM + LANE], gk_ref[...], dk_ref[...])
        dp_ref[:, ATT_Q_DIM:ATT_Q_DIM + LANE] = _bf(dx)
        dgk_ref[...] += dg

    row = lambda width: pl.BlockSpec((tm, width), lambda i: (i, 0))
    return pl.pallas_call(
        body, name="qk_prep_bwd", grid=(t // tm,),
        in_specs=[row(ATT_Q_DIM + LANE), row(ATT_HEADS * LANE), row(LANE), _full((1, LANE)), _full((1, LANE)),
                  row(LANE), row(LANE), row(LANE), _full((LANE, LANE))],
        out_specs=[row(ATT_Q_DIM + LANE), _full((1, LANE)), _full((1, LANE))],
        out_shape=[jax.ShapeDtypeStruct((t, ATT_Q_DIM + LANE), BF16), jax.ShapeDtypeStruct((1, LANE), F32),
                   jax.ShapeDtypeStruct((1, LANE), F32)],
        compiler_params=_params(),
    )(p, dq, dk, gq, gk, cc, sa, sb, bd)


def _attn_fwd(q, k, p, tq=256):
    t = k.shape[0]
    tq = min(tq, t)
    v_blk = (ATT_Q_DIM + ATT_KV_DIM) // LANE

    def body(q_ref, k_ref, v_ref, o_ref, o32_ref, lse_ref):
        k_ = k_ref[...]
        v = v_ref[...]
        lane_k = lax.broadcasted_iota(jnp.int32, (t, LANE), 1)
        lane_q = lax.broadcasted_iota(jnp.int32, (tq, LANE), 1)
        lowk, lowq = lane_k < ATT_HEAD_DIM, lane_q < ATT_HEAD_DIM
        ones_lane = (ATT_HEAD_DIM, 0)
        vm = (_bf(jnp.where(lowk, v, jnp.where(lane_k == ones_lane[0], 1.0, 0.0))),
              _bf(jnp.where(lowk, jnp.where(lane_k == ones_lane[1], 1.0, 0.0), v)))
        for j in range(4):
            kvh = j // 2
            acc = None
            for sub in range(2):
                h = 2 * j + sub
                s = _dot(q_ref[:, h * LANE:(h + 1) * LANE], k_, 1, 1)
                mx = jnp.max(s, axis=-1, keepdims=True)
                ov = _dot(jnp.exp(_bf(s - mx)), vm[kvh], 1, 0)
                l = jnp.sum(jnp.where(lane_q == ones_lane[kvh], ov, 0.0), axis=-1, keepdims=True)
                lse_ref[h] = mx + jnp.log(l)
                o = jnp.where(lowq if kvh == 0 else ~lowq, ov, 0.0) * (1.0 / l)
                if sub != kvh:
                    o = pltpu.roll(o, ATT_HEAD_DIM, 1)
                acc = o if acc is None else acc + o
            o32_ref[:, j * LANE:(j + 1) * LANE] = acc
            o_ref[:, j * LANE:(j + 1) * LANE] = _bf(acc)

    row = pl.BlockSpec((tq, ATT_Q_DIM), lambda i: (i, 0))
    return _call(
        body, (q, k, p), name="attn_fwd", grid=(t // tq,),
        in_specs=[pl.BlockSpec((tq, ATT_HEADS * LANE), lambda i: (i, 0)), _full((t, LANE)),
                  pl.BlockSpec((t, LANE), lambda i: (0, v_blk))],
        out_specs=[row, row, pl.BlockSpec((ATT_HEADS, tq, 1), lambda i: (0, i, 0))],
        out_shape=[jax.ShapeDtypeStruct((t, ATT_Q_DIM), BF16), jax.ShapeDtypeStruct((t, ATT_Q_DIM), F32),
                   jax.ShapeDtypeStruct((ATT_HEADS, t, 1), F32)])


def _attn_bwd(q, k, p, dcat, o32, lse, tq=256):
    t = k.shape[0]
    tq = min(tq, t)
    v_blk = (ATT_Q_DIM + ATT_KV_DIM) // LANE

    def body(q_ref, k_ref, v_ref, do_ref, o_ref, lse_ref, dq_ref, dk_ref, dv_ref):
        @pl.when(pl.program_id(0) == 0)
        def _():
            dk_ref[...] = jnp.zeros_like(dk_ref)
            dv_ref[...] = jnp.zeros_like(dv_ref)

        k_ = k_ref[...]
        vb = _bf(v_ref[...])
        lowq = lax.broadcasted_iota(jnp.int32, (tq, LANE), 1) < ATT_HEAD_DIM
        dk_acc = jnp.zeros((t, LANE), F32)
        dv_acc = jnp.zeros((t, LANE), F32)
        for j in range(4):
            kvh = j // 2
            dop = do_ref[:, j * LANE:(j + 1) * LANE]
            prod = dop * o_ref[:, j * LANE:(j + 1) * LANE]
            d_low = jnp.sum(jnp.where(lowq, prod, 0.0), axis=-1, keepdims=True)
            d_sub = (d_low, jnp.sum(prod, axis=-1, keepdims=True) - d_low)
            for sub in range(2):
                h = 2 * j + sub
                src = dop if sub == kvh else pltpu.roll(dop, ATT_HEAD_DIM, 1)
                do_h = _bf(jnp.where(lowq, src, 0.0) if kvh == 0 else jnp.where(lowq, 0.0, src))
                qh = q_ref[:, h * LANE:(h + 1) * LANE]
                pr = jnp.exp(_bf(_dot(qh, k_, 1, 1) - lse_ref[h]))
                ds = pr * _bf(_dot(do_h, vb, 1, 1) - d_sub[sub])
                dq_ref[:, h * LANE:(h + 1) * LANE] = _dot(ds, k_, 1, 0)
                dk_acc = dk_acc + _dot(ds, qh, 0, 0)
                dv_acc = dv_acc + _dot(pr, do_h, 0, 0)
        dk_ref[...] += dk_acc
        dv_ref[...] += dv_acc

    row = pl.BlockSpec((tq, ATT_Q_DIM), lambda i: (i, 0))
    return _call(
        body, (q, k, p, dcat, o32, lse), name="attn_bwd", grid=(t // tq,),
        in_specs=[pl.BlockSpec((tq, ATT_HEADS * LANE), lambda i: (i, 0)), _full((t, LANE)),
                  pl.BlockSpec((t, LANE), lambda i: (0, v_blk)), row, row,
                  pl.BlockSpec((ATT_HEADS, tq, 1), lambda i: (0, i, 0))],
        out_specs=[pl.BlockSpec((tq, ATT_HEADS * LANE), lambda i: (i, 0)), _full((t, LANE)), _full((t, LANE))],
        out_shape=[jax.ShapeDtypeStruct((t, ATT_HEADS * LANE), F32), jax.ShapeDtypeStruct((t, LANE), F32),
                   jax.ShapeDtypeStruct((t, LANE), F32)])


def _lower_bounds(a0, a1):
    def body(a0_ref, a1_ref, lb_ref):
        m = jnp.maximum(a0_ref[...], a1_ref[...])
        e0, e1 = jnp.exp(a0_ref[...] - m), jnp.exp(a1_ref[...] - m)
        lb_ref[...] = e0 / (e0 + e1)

    return pl.pallas_call(body, name="lower_bounds", out_shape=jax.ShapeDtypeStruct(a0.shape, F32))(a0, a1)


def _lower_bounds_bwd(a0, a1, dlb):
    def body(a0_ref, a1_ref, dlb_ref, d0_ref, d1_ref):
        m = jnp.maximum(a0_ref[...], a1_ref[...])
        e0, e1 = jnp.exp(a0_ref[...] - m), jnp.exp(a1_ref[...] - m)
        lb = e0 / (e0 + e1)
        d0 = dlb_ref[...] * lb * (1.0 - lb)
        d0_ref[...] = d0
        d1_ref[...] = -d0

    return pl.pallas_call(body, name="lower_bounds_bwd", out_shape=[jax.ShapeDtypeStruct(a0.shape, F32)] * 2)(a0, a1, dlb)


def _chunk_scan(x, pos, down):
    n = x.shape[0]
    s = 1
    while s < HG_CHUNK:
        if down:
            x = x + jnp.where(pos >= s, pltpu.roll(x, s, 0), 0.0)
        else:
            x = x + jnp.where(pos < HG_CHUNK - s, pltpu.roll(x, n - s, 0), 0.0)
        s *= 2
    return x


def _chunk_concat(x, nc):
    xb = _bf(x)
    rows = []
    for c in range(nc):
        pieces = [xb[c * HG_CHUNK:(c + 1) * HG_CHUNK, :]]
        if c:
            pieces.insert(0, jnp.zeros((HG_CHUNK, c * LANE), BF16))
        if c < nc - 1:
            pieces.append(jnp.zeros((HG_CHUNK, (nc - 1 - c) * LANE), BF16))
        rows.append(jnp.concatenate(pieces, axis=1) if len(pieces) > 1 else pieces[0])
    return jnp.concatenate(rows, axis=0) if nc > 1 else rows[0]


def _chunk_pick(x, nc):
    rows = [x[c * HG_CHUNK:(c + 1) * HG_CHUNK, c * LANE:(c + 1) * LANE] for c in range(nc)]
    return jnp.concatenate(rows, axis=0) if nc > 1 else rows[0]


def _chunk_rows(vals):
    rows = [jnp.broadcast_to(v, (HG_CHUNK, LANE)) for v in vals]
    return jnp.concatenate(rows, axis=0) if len(rows) > 1 else rows[0]


def _hgrn_gates(hq, z, lb):
    sq = _sigmoid(hq)
    sig = _sigmoid(z)
    f = lb + (1.0 - lb) * sig
    return hq * sq, sq, sig, f, jnp.log(f)


def _hgrn_local(q, f, g, v, pos, cid, amask, rev, nc):
    k = 1.0 - f
    ba = _chunk_scan(g, pos, not rev)
    bb = _chunk_scan(g, pos, rev)
    eq = 0.5 * (ba - bb + g)
    ex_q, ex_k, ex_i, ex_d = jnp.exp(eq), jnp.exp(-eq), jnp.exp(ba), jnp.exp(bb - g)
    qb, kb, qi, kd = q * ex_q, k * ex_k, q * ex_i, k * ex_d
    dvec = jnp.exp(ba + bb - g)
    a = jnp.where(amask, _dot(_bf(qb), _bf(kb), 1, 1), 0.0)
    kd_m = _chunk_concat(kd, nc)
    qi_m = _chunk_concat(qi, nc)
    ut_all = _dot(_bf(v), kd_m, 0, 0)
    return dict(k=k, ex_q=ex_q, ex_k=ex_k, ex_i=ex_i, ex_d=ex_d, qb=qb, kb=kb, qi=qi, kd=kd, dvec=dvec, a=a,
                kd_m=kd_m, qi_m=qi_m, ut_all=ut_all)


def _hgrn_masks(n, rev):
    row = lax.broadcasted_iota(jnp.int32, (n, LANE), 0)
    ti = lax.broadcasted_iota(jnp.int32, (n, n), 0)
    si = lax.broadcasted_iota(jnp.int32, (n, n), 1)
    tri = (si >= ti) if rev else (si <= ti)
    same = jnp.right_shift(ti, HG_CHUNK_LOG2) == jnp.right_shift(si, HG_CHUNK_LOG2)
    return jnp.bitwise_and(row, HG_CHUNK - 1), jnp.right_shift(row, HG_CHUNK_LOG2), same & tri


def _hgrn_specs(t, rev, bwd):
    n = min(HG_BLOCK_BWD if bwd else HG_BLOCK_FWD, t)
    nb = t // n
    if rev != bwd:
        blk = lambda i: nb - 1 - i
    else:
        blk = lambda i: i
    col = lambda base: pl.BlockSpec((n, 2 * LANE), lambda hp, i: (blk(i), base + hp))
    return n, nb, blk, col


def _hgrn_fwd(p, lb, rev):
    t = p.shape[0]
    n, nb, blk, col = _hgrn_specs(t, rev, False)
    nc = n // HG_CHUNK
    sub = n // min(HG_BLOCK_BWD, t)
    z_base = 7 if rev else 5

    def body(hq_ref, z_ref, hi_ref, lb_ref, o_ref, ssave_ref, st_scr):
        @pl.when(pl.program_id(1) == 0)
        def _():
            st_scr[...] = jnp.zeros_like(st_scr)

        pos, cid, amask = _hgrn_masks(n, rev)
        order = list(range(nc))[::-1] if rev else list(range(nc))
        for hh in range(2):
            sl = slice(hh * LANE, (hh + 1) * LANE)
            q, _, _, f, g = _hgrn_gates(hq_ref[:, sl], z_ref[:, sl], lb_ref[0:1, sl])
            v = hi_ref[:, sl]
            c_ = _hgrn_local(q, f, g, v, pos, cid, amask, rev, nc)
            st = st_scr[hh]
            cols = [None] * nc
            for c in order:
                cols[c] = st
                st = st * c_["dvec"][c * HG_CHUNK:c * HG_CHUNK + 1, :] + c_["ut_all"][:, c * LANE:(c + 1) * LANE]
            st_scr[hh] = st
            for s in range(sub):
                ssave_ref[sub - 1 - s if rev else s, sl, :] = cols[order[s * (nc // sub)]]
            st_all = _bf(jnp.concatenate(cols, axis=1))
            o_ref[:, sl] = _dot(_bf(c_["a"]), _bf(v), 1, 0) + _dot(c_["qi_m"], st_all, 1, 1)

    return _call(
        body, (p, p, p, lb), name="hgrn_fwd_rev" if rev else "hgrn_fwd", grid=(2, nb),
        in_specs=[col(3), col(z_base), col(9), pl.BlockSpec((1, 2 * LANE), lambda hp, i: (0, hp))],
        out_specs=[pl.BlockSpec((n, 2 * LANE), lambda hp, i: (blk(i), hp)),
                   pl.BlockSpec((sub, 2 * LANE, LANE), lambda hp, i: (blk(i), hp, 0))],
        out_shape=[jax.ShapeDtypeStruct((t, HG_DIM), F32), jax.ShapeDtypeStruct((nb * sub, HG_DIM, LANE), F32)],
        scratch_shapes=[pltpu.VMEM((2, LANE, LANE), F32)])


def _hgrn_bwd(p, lb, do, ssave, rev, after=()):
    t = p.shape[0]
    n, nb, blk, col = _hgrn_specs(t, rev, True)
    nc = n // HG_CHUNK
    z_base = 7 if rev else 5

    def body(hq_ref, z_ref, hi_ref, lb_ref, do_ref, ssave_ref, dhq_ref, dz_ref, dhi_ref, dlb_ref, dst_scr):
        @pl.when(pl.program_id(1) == 0)
        def _():
            dst_scr[...] = jnp.zeros_like(dst_scr)
            dlb_ref[...] = jnp.zeros_like(dlb_ref)

        pos, cid, amask = _hgrn_masks(n, rev)
        order = list(range(nc))[::-1] if rev else list(range(nc))
        for hh in range(2):
            sl = slice(hh * LANE, (hh + 1) * LANE)
            hq, lbv = hq_ref[:, sl], lb_ref[0:1, sl]
            q, sq, sig, f, g = _hgrn_gates(hq, z_ref[:, sl], lbv)
            v = hi_ref[:, sl]
            c_ = _hgrn_local(q, f, g, v, pos, cid, amask, rev, nc)
            dvec, ut_all = c_["dvec"], c_["ut_all"]
            drow = lambda c: dvec[c * HG_CHUNK:c * HG_CHUNK + 1, :]
            st = ssave_ref[0, sl, :]
            cols = [None] * nc
            for c in order:
                cols[c] = st
                st = st * drow(c) + ut_all[:, c * LANE:(c + 1) * LANE]
            dob, vb = _bf(do_ref[:, sl]), _bf(v)
            gt_all = _dot(dob, c_["qi_m"], 0, 0)
            dcur = dst_scr[hh]
            dnext = [None] * nc
            ddrow = [None] * nc
            for c in order[::-1]:
                dnext[c] = dcur
                ddrow[c] = jnp.sum(cols[c] * dcur, axis=0, keepdims=True) * drow(c)
                dcur = dcur * drow(c) + gt_all[:, c * LANE:(c + 1) * LANE]
            dst_scr[hh] = dcur
            dsn_all = _bf(jnp.concatenate(dnext, axis=1))
            st_all = _bf(jnp.concatenate(cols, axis=1))
            da = _bf(jnp.where(amask, _dot(dob, vb, 1, 1), 0.0))
            dv = _dot(_bf(c_["a"]), dob, 0, 0) + _dot(c_["kd_m"], dsn_all, 1, 1)
            dqb = _dot(da, _bf(c_["kb"]), 1, 0)
            dkb = _dot(da, _bf(c_["qb"]), 0, 0)
            dqi = _chunk_pick(_dot(dob, st_all, 1, 0), nc)
            dkd = _chunk_pick(_dot(vb, dsn_all, 1, 0), nc)
            dq = dqb * c_["ex_q"] + dqi * c_["ex_i"]
            dk = dkb * c_["ex_k"] + dkd * c_["ex_d"]
            e = dqb * c_["qb"] - dkb * c_["kb"] + dqi * c_["qi"]
            w = dkd * c_["kd"]
            dg = _chunk_scan(e, pos, rev) + (_chunk_scan(w, pos, not rev) - w) + _chunk_rows(ddrow)
            df = dg / f - dk
            dz_ref[:, sl] = df * (1.0 - lbv) * sig * (1.0 - sig)
            dlb_ref[0:1, sl] += jnp.sum(df * (1.0 - sig), axis=0, keepdims=True)
            dhq_ref[:, sl] = dq * (sq * (1.0 + hq * (1.0 - sq)))
            dhi_ref[:, sl] = dv

    out = pl.BlockSpec((n, 2 * LANE), lambda hp, i: (blk(i), hp))
    return _call(
        body, (p, p, p, lb, do, ssave), name="hgrn_bwd_rev" if rev else "hgrn_bwd", grid=(2, nb),
        in_specs=[col(3), col(z_base), col(9), pl.BlockSpec((1, 2 * LANE), lambda hp, i: (0, hp)), out,
                  pl.BlockSpec((1, 2 * LANE, LANE), lambda hp, i: (blk(i), hp, 0))],
        out_specs=[out, out, out, pl.BlockSpec((1, 2 * LANE), lambda hp, i: (0, hp))],
        out_shape=[jax.ShapeDtypeStruct((t, HG_DIM), F32)] * 3 + [jax.ShapeDtypeStruct((1, HG_DIM), F32)],
        scratch_shapes=[pltpu.VMEM((2, LANE, LANE), F32)], after=after)


def _mix_out(att, of, ob, p, gout, w_out, g_post, x, tm=256):
    t, d = x.shape
    tm = min(tm, t)

    def body(att_ref, of_ref, ob_ref, hg0_ref, hg1_ref, gout_ref, w_ref, g_ref, x_ref, cat_ref, y_ref, x1_ref):
        cat_ref[:, :ATT_Q_DIM] = att_ref[...]
        for h in range(HG_HEADS):
            sl = slice(h * LANE, (h + 1) * LANE)
            hg_ref = hg0_ref if h < 2 else hg1_ref
            hg = hg_ref[:, (h % 2) * LANE:(h % 2 + 1) * LANE]
            nrm = _rms_fwd(of_ref[:, sl] + ob_ref[:, sl], gout_ref[...])
            cat_ref[:, ATT_Q_DIM + h * LANE:ATT_Q_DIM + (h + 1) * LANE] = _bf(nrm * hg * _sigmoid(hg))
        y = _dot(cat_ref[...], w_ref[...], 1, 0)
        y_ref[...] = y
        x1_ref[...] = x_ref[...] + _rms_fwd(y, g_ref[...])

    row = lambda width: pl.BlockSpec((tm, width), lambda i: (i, 0))
    return _call(
        body, (att, of, ob, p, p, gout, w_out, g_post, x), name="mix_out", grid=(t // tm,),
        in_specs=[row(ATT_Q_DIM), row(HG_DIM), row(HG_DIM), pl.BlockSpec((tm, 2 * LANE), lambda i: (i, 11)),
                  pl.BlockSpec((tm, 2 * LANE), lambda i: (i, 12)), _full((1, LANE)), _full((d, d)), _full((1, d)), row(d)],
        out_specs=[row(d), row(d), row(d)],
        out_shape=[jax.ShapeDtypeStruct((t, d), BF16), jax.ShapeDtypeStruct((t, d), F32), jax.ShapeDtypeStruct((t, d), F32)])


def _rec_bwd(dcat, of, ob, p, gout, tm=256):
    t = of.shape[0]
    tm = min(tm, t)

    def body(dc_ref, of_ref, ob_ref, hg0_ref, hg1_ref, gout_ref, do_ref, dhg_ref, dgo_ref):
        @pl.when(pl.program_id(0) == 0)
        def _():
            dgo_ref[...] = jnp.zeros_like(dgo_ref)

        dgo = jnp.zeros((1, LANE), F32)
        for h in range(HG_HEADS):
            sl = slice(h * LANE, (h + 1) * LANE)
            hg_ref = hg0_ref if h < 2 else hg1_ref
            hg = hg_ref[:, (h % 2) * LANE:(h % 2 + 1) * LANE]
            o = of_ref[:, sl] + ob_ref[:, sl]
            sg = _sigmoid(hg)
            drec = dc_ref[:, sl]
            dhg_ref[:, sl] = drec * _rms_fwd(o, gout_ref[...]) * (sg * (1.0 + hg * (1.0 - sg)))
            do, dg = _rms_bwd(drec * hg * sg, o, gout_ref[...])
            do_ref[:, sl] = do
            dgo = dgo + dg
        dgo_ref[...] += dgo

    row = lambda width: pl.BlockSpec((tm, width), lambda i: (i, 0))
    return pl.pallas_call(
        body, name="rec_bwd", grid=(t // tm,),
        in_specs=[pl.BlockSpec((tm, HG_DIM), lambda i: (i, 1)), row(HG_DIM), row(HG_DIM),
                  pl.BlockSpec((tm, 2 * LANE), lambda i: (i, 11)), pl.BlockSpec((tm, 2 * LANE), lambda i: (i, 12)),
                  _full((1, LANE))],
        out_specs=[row(HG_DIM), row(HG_DIM), _full((1, LANE))],
        out_shape=[jax.ShapeDtypeStruct((t, HG_DIM), F32), jax.ShapeDtypeStruct((t, HG_DIM), F32),
                   jax.ShapeDtypeStruct((1, LANE), F32)],
        compiler_params=_params(),
    )(dcat, of, ob, p, p, gout)


def _assemble_dp(dp_qk, dv, dhq_f, dhq_b, dz_f, dz_b, dhi_f, dhi_b, dhg, tm=512):
    t = dv.shape[0]
    tm = min(tm, t)
    qk_w = ATT_Q_DIM + LANE

    def body(qk_ref, dv_ref, hqf_ref, hqb_ref, zf_ref, zb_ref, hif_ref, hib_ref, hg_ref, dp_ref):
        o = 0
        dp_ref[:, o:o + qk_w] = qk_ref[...]
        o += qk_w
        dp_ref[:, o:o + LANE] = _bf(dv_ref[...])
        o += LANE
        for val in (hqf_ref[...] + hqb_ref[...], zf_ref[...], zb_ref[...], hif_ref[...] + hib_ref[...], hg_ref[...]):
            dp_ref[:, o:o + HG_DIM] = _bf(val)
            o += HG_DIM

    row = lambda width: pl.BlockSpec((tm, width), lambda i: (i, 0))
    return pl.pallas_call(
        body, name="assemble_dp", grid=(t // tm,),
        in_specs=[row(qk_w), row(LANE)] + [row(HG_DIM)] * 7,
        out_specs=row(N_IN),
        out_shape=jax.ShapeDtypeStruct((t, N_IN), BF16),
        compiler_params=_params(),
    )(dp_qk, dv, dhq_f, dhq_b, dz_f, dz_b, dhi_f, dhi_b, dhg)


def _xattn_fwd(q, kv, tq=512):
    t, d = q.shape
    tq = min(tq, t)
    nm = kv.shape[0]

    def body(q_ref, kv_ref, o_ref):
        for h in range(X_HEADS):
            sl = slice(h * X_HEAD_DIM, (h + 1) * X_HEAD_DIM)
            s = _dot(_bf(q_ref[:, sl]), _bf(kv_ref[:, sl]), 1, 1) * (X_HEAD_DIM ** -0.5)
            e = jnp.exp(s - jnp.max(s, axis=-1, keepdims=True))
            pr = e * (1.0 / jnp.sum(e, axis=-1, keepdims=True))
            o_ref[:, sl] = _bf(_dot(_bf(pr), _bf(kv_ref[:, d + h * X_HEAD_DIM:d + (h + 1) * X_HEAD_DIM]), 1, 0))

    return pl.pallas_call(
        body, name="xattn_fwd", grid=(t // tq,),
        in_specs=[pl.BlockSpec((tq, d), lambda i: (i, 0)), _full((nm, 2 * d))],
        out_specs=pl.BlockSpec((tq, d), lambda i: (i, 0)),
        out_shape=jax.ShapeDtypeStruct((t, d), BF16),
        compiler_params=_params(),
    )(q, kv)


def _xattn_bwd(q, kv, do, tq=512):
    t, d = q.shape
    tq = min(tq, t)
    nm = kv.shape[0]

    def body(q_ref, kv_ref, do_ref, dq_ref, dkv_ref):
        @pl.when(pl.program_id(0) == 0)
        def _():
            dkv_ref[...] = jnp.zeros_like(dkv_ref)

        for h in range(X_HEADS):
            sl = slice(h * X_HEAD_DIM, (h + 1) * X_HEAD_DIM)
            slv = slice(d + h * X_HEAD_DIM, d + (h + 1) * X_HEAD_DIM)
            qb, kb, vb, dob = _bf(q_ref[:, sl]), _bf(kv_ref[:, sl]), _bf(kv_ref[:, slv]), _bf(do_ref[:, sl])
            s = _dot(qb, kb, 1, 1) * (X_HEAD_DIM ** -0.5)
            e = jnp.exp(s - jnp.max(s, axis=-1, keepdims=True))
            pr = e * (1.0 / jnp.sum(e, axis=-1, keepdims=True))
            dpr = _dot(dob, vb, 1, 1)
            ds = _bf(pr * (dpr - jnp.sum(pr * dpr, axis=-1, keepdims=True)) * (X_HEAD_DIM ** -0.5))
            dq_ref[:, sl] = _bf(_dot(ds, kb, 1, 0))
            dkv_ref[:, sl] += _dot(ds, qb, 0, 0)
            dkv_ref[:, slv] += _dot(_bf(pr), dob, 0, 0)

    return pl.pallas_call(
        body, name="xattn_bwd", grid=(t // tq,),
        in_specs=[pl.BlockSpec((tq, d), lambda i: (i, 0)), _full((nm, 2 * d)), pl.BlockSpec((tq, d), lambda i: (i, 0))],
        out_specs=[pl.BlockSpec((tq, d), lambda i: (i, 0)), _full((nm, 2 * d))],
        out_shape=[jax.ShapeDtypeStruct((t, d), BF16), jax.ShapeDtypeStruct((nm, 2 * d), F32)],
        compiler_params=_params(),
    )(q, kv, do)


CONV_TN = 256


def _shift_rows(u, row, t, delta):
    if delta < 0:
        return jnp.where(row == 0, 0.0, pltpu.roll(u, 1, 0))
    return jnp.where(row == t - 1, 0.0, pltpu.roll(u, t - 1, 0))


def _conv_gate_fwd(u, conv_w, conv_b):
    t = u.shape[0]
    nt = D_FF // CONV_TN

    def body(ug_ref, uv_ref, wg_ref, wv_ref, bg_ref, bv_ref, a_ref):
        row = lax.broadcasted_iota(jnp.int32, (t, CONV_TN), 0)

        def conv(u_ref, w_ref, b_ref):
            uu = u_ref[...]
            return (b_ref[...] + _shift_rows(uu, row, t, -1) * w_ref[0:1, :] + uu * w_ref[1:2, :]
                    + _shift_rows(uu, row, t, 1) * w_ref[2:3, :])

        gate = conv(ug_ref, wg_ref, bg_ref)
        a_ref[...] = _bf(gate * _sigmoid(gate) * conv(uv_ref, wv_ref, bv_ref))

    col = lambda rows, off: pl.BlockSpec((rows, CONV_TN), lambda j: (0, j + off))
    return pl.pallas_call(
        body, name="conv_gate_fwd", grid=(nt,),
        in_specs=[col(t, 0), col(t, nt), col(3, 0), col(3, nt), col(1, 0), col(1, nt)],
        out_specs=col(t, 0),
        out_shape=jax.ShapeDtypeStruct((t, D_FF), BF16),
        compiler_params=_params(),
    )(u, u, conv_w, conv_w, conv_b, conv_b)


def _conv_gate_bwd(u, conv_w, conv_b, da, after=()):
    t = u.shape[0]
    nt = D_FF // CONV_TN

    def body(ug_ref, uv_ref, wg_ref, wv_ref, bg_ref, bv_ref, da_ref, dug_ref, dwg_ref, dbg_ref, duv_ref, dwv_ref, dbv_ref):
        row = lax.broadcasted_iota(jnp.int32, (t, CONV_TN), 0)
        ug, uv = ug_ref[...], uv_ref[...]
        ug_m, ug_p = _shift_rows(ug, row, t, -1), _shift_rows(ug, row, t, 1)
        uv_m, uv_p = _shift_rows(uv, row, t, -1), _shift_rows(uv, row, t, 1)
        gate = bg_ref[...] + ug_m * wg_ref[0:1, :] + ug * wg_ref[1:2, :] + ug_p * wg_ref[2:3, :]
        val = bv_ref[...] + uv_m * wv_ref[0:1, :] + uv * wv_ref[1:2, :] + uv_p * wv_ref[2:3, :]
        sg = _sigmoid(gate)
        da_ = da_ref[...].astype(F32)

        def emit(dacc, um, uu, up, w_ref, du_ref, dw_ref, db_ref):
            du_ref[...] = _bf(_shift_rows(dacc, row, t, 1) * w_ref[0:1, :] + dacc * w_ref[1:2, :]
                              + _shift_rows(dacc, row, t, -1) * w_ref[2:3, :])
            dw_ref[0:1, :] = jnp.sum(dacc * um, axis=0, keepdims=True)
            dw_ref[1:2, :] = jnp.sum(dacc * uu, axis=0, keepdims=True)
            dw_ref[2:3, :] = jnp.sum(dacc * up, axis=0, keepdims=True)
            db_ref[...] = jnp.sum(dacc, axis=0, keepdims=True)

        emit(da_ * val * (sg * (1.0 + gate * (1.0 - sg))), ug_m, ug, ug_p, wg_ref, dug_ref, dwg_ref, dbg_ref)
        emit(da_ * gate * sg, uv_m, uv, uv_p, wv_ref, duv_ref, dwv_ref, dbv_ref)

    col = lambda rows, off: pl.BlockSpec((rows, CONV_TN), lambda j: (0, j + off))
    half_shapes = [jax.ShapeDtypeStruct((t, D_FF), BF16), jax.ShapeDtypeStruct((3, D_FF), F32),
                   jax.ShapeDtypeStruct((1, D_FF), F32)]
    outs = _call(
        body, (u, u, conv_w, conv_w, conv_b, conv_b, da), name="conv_gate_bwd", grid=(nt,),
        in_specs=[col(t, 0), col(t, nt), col(3, 0), col(3, nt), col(1, 0), col(1, nt), col(t, 0)],
        out_specs=[col(t, 0), col(3, 0), col(1, 0)] * 2, out_shape=half_shapes * 2, after=after)
    return outs[:3], outs[3:]


def _row_tile(r, cap):
    best = r
    for cand in range(16, cap + 1, 16):
        if r % cand == 0:
            best = cand
    return best


def _sum_parts(own, got, me, name, tr=256):
    _, r, c = got.shape
    tr = _row_tile(r, tr)

    def body(me_ref, own_ref, got_ref, o_ref):
        mine = own_ref[...].astype(F32)
        acc = None
        for i in range(N_DEV):
            term = jnp.where(me_ref[0] == i, mine, got_ref[i].astype(F32))
            acc = term if acc is None else acc + term
        o_ref[...] = acc

    return pl.pallas_call(
        body, name=name,
        grid_spec=pltpu.PrefetchScalarGridSpec(
            num_scalar_prefetch=1, grid=(r // tr,),
            in_specs=[pl.BlockSpec((None, tr, c), lambda i, me_ref: (me_ref[0], i, 0)),
                      pl.BlockSpec((N_DEV, tr, c), lambda i, me_ref: (0, i, 0))],
            out_specs=pl.BlockSpec((tr, c), lambda i, me_ref: (i, 0))),
        out_shape=jax.ShapeDtypeStruct((r, c), F32),
        compiler_params=_params(),
    )(me, own, got)


def _adamw_math(w_, g_, m_, v_):
    m_ = ADAM_B1 * m_ + (1.0 - ADAM_B1) * g_
    v_ = ADAM_B2 * v_ + (1.0 - ADAM_B2) * (g_ * g_)
    m_hat = m_ / (1.0 - ADAM_B1 ** ADAM_STEP)
    v_hat = v_ / (1.0 - ADAM_B2 ** ADAM_STEP)
    return -ADAM_LR * (m_hat / (jnp.sqrt(v_hat) + ADAM_EPS) + ADAM_WD * w_), m_, v_


def _sum_adamw(own, got, me, w, m, v, name, tr=256):
    _, r, c = got.shape
    tr = _row_tile(r, tr)

    def body(me_ref, own_ref, got_ref, w_ref, m_ref, v_ref, g_ref, d_ref, nm_ref, nv_ref):
        mine = own_ref[...].astype(F32)
        acc = None
        for i in range(N_DEV):
            term = jnp.where(me_ref[0] == i, mine, got_ref[i].astype(F32))
            acc = term if acc is None else acc + term
        g_ref[...] = acc
        d_ref[...], nm_ref[...], nv_ref[...] = _adamw_math(w_ref[...], acc, m_ref[...], v_ref[...])

    blk = pl.BlockSpec((tr, c), lambda i, me_ref: (i, 0))
    return pl.pallas_call(
        body, name=name,
        grid_spec=pltpu.PrefetchScalarGridSpec(
            num_scalar_prefetch=1, grid=(r // tr,),
            in_specs=[pl.BlockSpec((None, tr, c), lambda i, me_ref: (me_ref[0], i, 0)),
                      pl.BlockSpec((N_DEV, tr, c), lambda i, me_ref: (0, i, 0)), blk, blk, blk],
            out_specs=[blk] * 4),
        out_shape=[jax.ShapeDtypeStruct((r, c), F32)] * 4,
        compiler_params=_params(),
    )(me, own, got, w, m, v)


def _adamw(w, g, m, v, name, tr=256):
    r, c = w.shape
    tr = _row_tile(r, tr)

    def body(w_ref, g_ref, m_ref, v_ref, d_ref, nm_ref, nv_ref):
        d_ref[...], nm_ref[...], nv_ref[...] = _adamw_math(w_ref[...], g_ref[...], m_ref[...], v_ref[...])

    blk = pl.BlockSpec((tr, c), lambda i: (i, 0))
    return pl.pallas_call(
        body, name=name, grid=(r // tr,),
        in_specs=[blk] * 4, out_specs=[blk] * 3,
        out_shape=[jax.ShapeDtypeStruct((r, c), F32)] * 3,
        compiler_params=_params(),
    )(w, g, m, v)


def _mesh_pos():
    return lax.axis_index("x"), lax.axis_index("y"), lax.axis_index("c")


def _dev_index(px, py, pc):
    return 4 * px + 2 * py + pc


class _Gather:
    def __init__(self, arrs):
        self.arrs = list(arrs)
        n = len(self.arrs)
        self.out_shape = [jax.ShapeDtypeStruct((N_DEV,) + a.shape, a.dtype) for a in self.arrs]
        self.sems = [pltpu.SemaphoreType.DMA((7, n)), pltpu.SemaphoreType.DMA((7, n)), pltpu.SemaphoreType.DMA((n,))]

    def _ctx(self, ins, outs, sems):
        send_sems, recv_sems, local_sems = sems
        x, y, c = _mesh_pos()
        chips = [(1 - x, y), (x, 1 - y), (1 - x, 1 - y)]

        def copy(k, a, block, to, src=None):
            dst = outs[a].at[_dev_index(*block)]
            return pltpu.make_async_remote_copy(
                src_ref=dst if src is None else src, dst_ref=dst, send_sem=send_sems.at[k, a], recv_sem=recv_sems.at[k, a],
                device_id=to, device_id_type=MESH_T)

        n = len(ins)
        me, sibling = (x, y, c), (x, y, 1 - c)
        mine = [pltpu.make_async_copy(ins[a], outs[a].at[_dev_index(*me)], local_sems.at[a]) for a in range(n)]
        first = [copy(0, a, me, sibling, src=ins[a]) for a in range(n)]
        first += [copy(1 + j, a, me, (*chip, c), src=ins[a]) for j, chip in enumerate(chips) for a in range(n)]
        passed = [copy(4 + j, a, (*chip, c), sibling) for j, chip in enumerate(chips) for a in range(n)]
        return n, c, me, sibling, chips, copy, mine, first, passed

    def start(self, ins, outs, sems):
        _, _, _, _, _, _, mine, first, _ = self._ctx(ins, outs, sems)
        for cp in mine + first:
            cp.start()

    def forward(self, ins, outs, sems):
        n, c, me, _, chips, copy, _, _, passed = self._ctx(ins, outs, sems)
        for j, chip in enumerate(chips):
            for a in range(n):
                copy(1 + j, a, (*chip, c), me).wait_recv()
                passed[j * n + a].start()

    def finish(self, ins, outs, sems):
        n, c, me, sibling, chips, copy, mine, first, passed = self._ctx(ins, outs, sems)
        for a in range(n):
            copy(0, a, sibling, me).wait_recv()
        for j, chip in enumerate(chips):
            for a in range(n):
                copy(4 + j, a, (*chip, 1 - c), me).wait_recv()
        for cp in first + passed:
            cp.wait_send()
        for cp in mine:
            cp.wait()


def _comm_alone(comm, name):
    n = len(comm.arrs)

    def body(*refs):
        ins, outs, sems = refs[:n], refs[n:2 * n], refs[2 * n:]
        comm.start(ins, outs, sems)
        if comm.forward is not None:
            comm.forward(ins, outs, sems)
        comm.finish(ins, outs, sems)

    any_spec = pl.BlockSpec(memory_space=pl.ANY)
    return pl.pallas_call(body, name=name, in_specs=[any_spec] * n, out_specs=[any_spec] * n, out_shape=comm.out_shape,
                          scratch_shapes=comm.sems)(*comm.arrs)


def _peers(x, y, c):
    out = []
    for k in range(1, N_DEV):
        pos = (1 - x if k & 4 else x, 1 - y if k & 2 else y, 1 - c if k & 1 else c)
        out.append((k, pos, _dev_index(*pos)))
    return out


def _exchange_begin(arrs, tag):
    n = len(arrs)
    lands = [lax.empty(a.shape, a.dtype) for a in arrs]

    def start_body(*refs):
        ins, land = refs[:n], refs[n:2 * n]
        send_sems, recv_sems, token = refs[2 * n], refs[2 * n + 1], refs[-1]
        x, y, c = _mesh_pos()
        me_i = _dev_index(x, y, c)
        for k, pos, peer_i in _peers(x, y, c):
            for a in range(n):
                pltpu.make_async_remote_copy(
                    src_ref=ins[a].at[peer_i], dst_ref=land[a].at[me_i], send_sem=send_sems.at[(k - 1) * n + a],
                    recv_sem=recv_sems.at[(k - 1) * n + a], device_id=pos, device_id_type=MESH_T).start()
        token[...] = jnp.zeros_like(token)

    hbm = pl.BlockSpec(memory_space=pltpu.HBM)
    sem = pl.BlockSpec(memory_space=pltpu.SEMAPHORE)
    thru = [pltpu.HBM(a.shape, a.dtype) for a in arrs]
    outs = pl.pallas_call(
        start_body, name="exchange_start_" + tag,
        out_shape=[pltpu.SemaphoreType.DMA((7 * n,)), pltpu.SemaphoreType.DMA((7 * n,))] + thru + thru
        + [jax.ShapeDtypeStruct((8, LANE), F32)],
        in_specs=[hbm] * (2 * n), out_specs=[sem, sem] + [hbm] * (2 * n) + [pl.BlockSpec(memory_space=pltpu.VMEM)],
        input_output_aliases={i: 2 + i for i in range(2 * n)},
        compiler_params=pltpu.CompilerParams(has_side_effects=pltpu.SideEffectType.DATAFLOW_SIDE_EFFECTING),
    )(*[pltpu.with_memory_space_constraint(a, pltpu.HBM) for a in list(arrs) + list(lands)])
    return (tag, n, outs[0], outs[1], outs[2:2 + n], outs[2 + n:2 + 2 * n]), outs[-1]


def _exchange_end(handle, after):
    tag, n, send_sems, recv_sems, srcs, lands = handle

    def body(*refs):
        ins, land = refs[:n], refs[n:2 * n]
        send_sems_, recv_sems_ = refs[2 * n], refs[2 * n + 1]
        x, y, c = _mesh_pos()
        me_i = _dev_index(x, y, c)
        for k, pos, peer_i in _peers(x, y, c):
            for a in range(n):
                cp = pltpu.make_async_remote_copy(
                    src_ref=ins[a].at[peer_i], dst_ref=land[a].at[me_i], send_sem=send_sems_.at[(k - 1) * n + a],
                    recv_sem=recv_sems_.at[(k - 1) * n + a], device_id=pos, device_id_type=MESH_T)
                cp.wait_send()
                cp.wait_recv()

    hbm = pl.BlockSpec(memory_space=pltpu.HBM)
    sem = pl.BlockSpec(memory_space=pltpu.SEMAPHORE)
    outs = pl.pallas_call(
        body, name="exchange_end_" + tag, out_shape=[pltpu.HBM(a.shape, a.dtype) for a in list(srcs) + list(lands)],
        in_specs=[hbm] * (2 * n) + [sem, sem, pl.BlockSpec(memory_space=pl.ANY)], out_specs=[hbm] * (2 * n),
        input_output_aliases={i: i for i in range(2 * n)},
        compiler_params=pltpu.CompilerParams(has_side_effects=pltpu.SideEffectType.DATAFLOW_SIDE_EFFECTING),
    )(*srcs, *lands, send_sems, recv_sems, after)
    return list(zip(outs[:n], outs[n:]))


def _gather_begin(arrs, tag):
    n = len(arrs)
    me = _dev_index(*_mesh_pos())
    lands = [lax.dynamic_update_slice(lax.empty((N_DEV,) + a.shape, a.dtype), a[None], (me,) + (0,) * a.ndim) for a in arrs]

    def start_body(*refs):
        ins, land = refs[:n], refs[n:2 * n]
        send_sems, recv_sems, token = refs[2 * n], refs[2 * n + 1], refs[-1]
        x, y, c = _mesh_pos()
        me_i = _dev_index(x, y, c)
        for a in range(n):
            for k, pos, _ in _peers(x, y, c):
                pltpu.make_async_remote_copy(
                    src_ref=ins[a], dst_ref=land[a].at[me_i], send_sem=send_sems.at[(k - 1) * n + a],
                    recv_sem=recv_sems.at[(k - 1) * n + a], device_id=pos, device_id_type=MESH_T).start()
        token[...] = jnp.zeros_like(token)

    hbm = pl.BlockSpec(memory_space=pltpu.HBM)
    sem = pl.BlockSpec(memory_space=pltpu.SEMAPHORE)
    thru = [pltpu.HBM(a.shape, a.dtype) for a in list(arrs) + lands]
    outs = pl.pallas_call(
        start_body, name="gather_start_" + tag,
        out_shape=[pltpu.SemaphoreType.DMA((7 * n,)), pltpu.SemaphoreType.DMA((7 * n,))] + thru
        + [jax.ShapeDtypeStruct((8, LANE), F32)],
        in_specs=[hbm] * (2 * n), out_specs=[sem, sem] + [hbm] * (2 * n) + [pl.BlockSpec(memory_space=pltpu.VMEM)],
        input_output_aliases={i: 2 + i for i in range(2 * n)},
        compiler_params=pltpu.CompilerParams(has_side_effects=pltpu.SideEffectType.DATAFLOW_SIDE_EFFECTING),
    )(*[pltpu.with_memory_space_constraint(a, pltpu.HBM) for a in list(arrs) + lands])
    return (tag, n, outs[0], outs[1], outs[2:2 + n], outs[2 + n:2 + 2 * n]), outs[-1]


def _gather_end(handle, which, after):
    tag, n, send_sems, recv_sems, srcs, lands = handle
    m = len(which)

    def body(*refs):
        ins, land = refs[:m], refs[m:2 * m]
        send_sems_, recv_sems_ = refs[2 * m], refs[2 * m + 1]
        x, y, c = _mesh_pos()
        me_i = _dev_index(x, y, c)
        for j, a in enumerate(which):
            for k, pos, _ in _peers(x, y, c):
                cp = pltpu.make_async_remote_copy(
                    src_ref=ins[j], dst_ref=land[j].at[me_i], send_sem=send_sems_.at[(k - 1) * n + a],
                    recv_sem=recv_sems_.at[(k - 1) * n + a], device_id=pos, device_id_type=MESH_T)
                cp.wait_send()
                cp.wait_recv()

    hbm = pl.BlockSpec(memory_space=pltpu.HBM)
    sem = pl.BlockSpec(memory_space=pltpu.SEMAPHORE)
    ops = [srcs[a] for a in which] + [lands[a] for a in which]
    outs = pl.pallas_call(
        body, name="gather_end_%s_%s" % (tag, "_".join(str(a) for a in which)),
        out_shape=[pltpu.HBM(a.shape, a.dtype) for a in ops],
        in_specs=[hbm] * (2 * m) + [sem, sem, pl.BlockSpec(memory_space=pl.ANY)], out_specs=[hbm] * (2 * m),
        input_output_aliases={i: i for i in range(2 * m)},
        compiler_params=pltpu.CompilerParams(has_side_effects=pltpu.SideEffectType.DATAFLOW_SIDE_EFFECTING),
    )(*ops, send_sems, recv_sems, after)
    return list(outs[m:])


def _call(body, operands, *, name, grid, in_specs, out_specs, out_shape, scratch_shapes=(), after=()):
    ni, nd = len(in_specs), len(after)

    def ordered(*refs):
        body(*refs[:ni], *refs[ni + nd:])

    outs = pl.pallas_call(
        ordered, name=name, grid=grid, in_specs=list(in_specs) + [pl.BlockSpec(memory_space=pl.ANY)] * nd,
        out_specs=out_specs, out_shape=out_shape, scratch_shapes=list(scratch_shapes), compiler_params=_params(),
    )(*operands, *after)
    return list(outs)


def _rows(a):
    return a.reshape(-1, a.shape[-1])


def _slots(a):
    return a.reshape(N_DEV, -1, a.shape[-1])


def _local_step(x, mem, tgt, w, wire):
    tables = _rope_tables(x.shape[0])
    bd = _head_sum_matrix()
    two = lambda g: jnp.tile(g, (1, 2))
    gq2, gk2 = two(w["q_norm_g"]), two(w["k_norm_g"])

    w_in_t, hg_lb, conv_w = _comm_alone(_Gather([wire["w_in"], w["hg_lb"].reshape(4, -1), w["conv_w"][0]]), "gather_w_in")
    w_in_t = _rows(w_in_t)
    hg_lb = jnp.transpose(hg_lb.reshape(N_DEV, 2, 2, -1), (1, 2, 0, 3)).reshape(2, 2, HG_DIM)
    conv_w = jnp.transpose(conv_w, (1, 0, 2)).reshape(3, 2 * D_FF)
    lb_a0, lb_a1 = hg_lb[:, 0, :], hg_lb[:, 1, :]

    order = ("w_out", "w_xq", "w_xkv", "w_xo", "w_up", "w_down")
    w_in_t, *later = lax.optimization_barrier((w_in_t, *[wire[n] for n in order]))
    fetch, started = _gather_begin(later, "w")
    take = lambda n, after: _rows(_gather_end(fetch, [order.index(n)], after)[0])

    h1, p = _norm_mm(x, w["pre_mix_g"], w_in_t, True, N_IN, "in_proj", after=[started])
    qr, kr = _qk_prep(p, gq2, gk2, tables, bd)
    att, att32, lse = _attn_fwd(qr, kr, p)
    lb = _lower_bounds(lb_a0, lb_a1)
    of, s_f = _hgrn_fwd(p, lb[0:1], False)
    ob, s_b = _hgrn_fwd(p, lb[1:2], True)
    w_out = take("w_out", ob)
    cat, mixed, x1 = _mix_out(att, of, ob, p, w["hg_out_norm_g"], w_out, w["post_mix_g"], x)

    w_xq = take("w_xq", x1)
    h2, q2 = _norm_mm(x1, w["pre_x_g"], w_xq, False, 1024, "xq_proj")
    w_xkv_t = take("w_xkv", q2)
    mn, kv = _norm_mm(mem, w["mem_norm_g"], w_xkv_t, True, 2 * D_MODEL, "xkv_proj")
    o2 = _xattn_fwd(q2, kv)
    w_xo = take("w_xo", o2)
    y2, x2 = _mm_postnorm_res(o2, w_xo, w["post_x_g"], x1, "xo_proj")

    w_up_t = take("w_up", x2)
    h3, u = _norm_mm(x2, w["pre_ffn_g"], w_up_t, True, 2 * D_FF, "up_proj", tm=256)
    a = _conv_gate_fwd(u, conv_w, w["conv_b"])
    w_down = take("w_down", a)
    y3, dx3, loss = _mm_postnorm_res_loss(a, w_down, w["post_ffn_g"], x2, tgt, "down_proj")

    g, pending = {}, {}
    dy3, da, g["post_ffn_g"] = _postnorm_bwd_mm(dx3, y3, w["post_ffn_g"], w_down, "down_bwd", BF16)
    pending["down"], started = _exchange_begin([_slots(_dw(a, dy3, "dw_down"))], "down")
    (du_g, dcw_g, dcb_g), (du_v, dcw_v, dcb_v) = _conv_gate_bwd(u, conv_w, w["conv_b"], da, after=[started])
    g["conv_w"] = jnp.concatenate([dcw_g, dcw_v], axis=1)
    g["conv_b"] = jnp.concatenate([dcb_g, dcb_v], axis=1)
    pending["up"], started = _exchange_begin([_slots(_dw([du_g, du_v], h3, "dw_up"))], "up")
    dx2, g["pre_ffn_g"] = _mm_prenorm_bwd([du_g, du_v], w_up_t, True, x2, w["pre_ffn_g"], dx3, "up_bwd", after=[started], tm=256)

    dy2, do2, g["post_x_g"] = _postnorm_bwd_mm(dx2, y2, w["post_x_g"], w_xo, "xo_bwd", BF16)
    dw_xo = _dw(o2, dy2, "dw_xo")
    dq2, dkv = _xattn_bwd(q2, kv, do2)
    dw_xq = _dw(h2, dq2, "dw_xq")
    dkvb = _bf(dkv)
    dw_xkv = _dw(dkvb, mn, "dw_xkv")
    pending["x"], started = _exchange_begin([_slots(dw_xo), _slots(dw_xq), _slots(dw_xkv)], "x")
    _, g["mem_norm_g"] = _mm_prenorm_bwd(dkvb, w_xkv_t, True, mem, w["mem_norm_g"], jnp.zeros_like(mem), "xkv_bwd")
    dx1, g["pre_x_g"] = _mm_prenorm_bwd(dq2, w_xq, False, x1, w["pre_x_g"], dx2, "xq_bwd", after=[started])

    dmixed, dcat, g["post_mix_g"] = _postnorm_bwd_mm(dx1, mixed, w["post_mix_g"], w_out, "out_bwd")
    pending["out"], started = _exchange_begin([_slots(_dw(cat, dmixed, "dw_out"))], "out")
    do, dhg, g["hg_out_norm_g"] = _rec_bwd(dcat, of, ob, p, w["hg_out_norm_g"])
    dhq_f, dz_f, dhi_f, dlb_f = _hgrn_bwd(p, lb[0:1], do, s_f, False, after=[started])
    dhq_b, dz_b, dhi_b, dlb_b = _hgrn_bwd(p, lb[1:2], do, s_b, True)
    d_a0, d_a1 = _lower_bounds_bwd(lb_a0, lb_a1, jnp.concatenate([dlb_f, dlb_b], axis=0))
    g["hg_lb"] = jnp.stack([d_a0, d_a1], axis=1)
    dqr, dkr, dv = _attn_bwd(qr, kr, p, dcat, att32, lse)
    dp_qk, dgq, dgk = _qk_prep_bwd(p, dqr, dkr, gq2, gk2, tables, bd)
    g["q_norm_g"], g["k_norm_g"] = dgq, dgk
    dp = _assemble_dp(dp_qk, dv, dhq_f, dhq_b, dz_f, dz_b, dhi_f, dhi_b, dhg)
    pending["in"], started = _exchange_begin([_slots(_dw(dp, h1, "dw_in"))], "in")
    dx, g["pre_mix_g"] = _mm_prenorm_bwd(dp, w_in_t, True, x, w["pre_mix_g"], dx1, "in_bwd", after=[started])
    return loss, dx, g, pending


_COL_SHARDED = ("w_in", "w_xkv", "w_up")
_ROW_SHARDED = ("w_out", "w_xq", "w_xo", "w_down")
_REPLICATED = ("pre_mix_g", "q_norm_g", "k_norm_g", "hg_out_norm_g", "post_mix_g", "pre_x_g", "mem_norm_g", "post_x_g",
               "pre_ffn_g", "conv_b", "post_ffn_g")
_WEIGHTS = ("pre_mix_g", "w_in", "q_norm_g", "k_norm_g", "hg_lb", "hg_out_norm_g", "w_out", "post_mix_g", "pre_x_g",
            "mem_norm_g", "w_xq", "w_xkv", "w_xo", "post_x_g", "pre_ffn_g", "w_up", "conv_w", "conv_b", "w_down",
            "post_ffn_g")
_ADAM_TRANSPOSED = ("w_in", "w_up")
PACK_W = 1024


def _small_plan(shapes):
    plan, r = [], 0
    for vi, (rows, cols) in enumerate(shapes):
        for i in range(rows):
            for c0 in range(0, cols, PACK_W):
                plan.append((vi, i, c0, min(PACK_W, cols - c0), r))
                r += 1
    return plan, -(-r // 8) * 8


def _pack_small(vals):
    plan, nrows = _small_plan([val.shape for val in vals])

    def body(*refs):
        ins, out = refs[:-1], refs[-1]
        out[...] = jnp.zeros_like(out)
        for vi, i, c0, width, r in plan:
            out[r:r + 1, 0:width] = ins[vi][i:i + 1, c0:c0 + width]

    return pl.pallas_call(body, name="pack_small", out_shape=jax.ShapeDtypeStruct((nrows, PACK_W), F32))(*vals)


def _sum_unpack_small(packs, shapes):
    plan, _ = _small_plan(shapes)

    def body(*refs):
        p_ref, outs = refs[0], refs[1:]
        acc = p_ref[0]
        for i in range(1, N_DEV):
            acc = acc + p_ref[i]
        for vi, i, c0, width, r in plan:
            outs[vi][i:i + 1, c0:c0 + width] = acc[r:r + 1, 0:width]

    return pl.pallas_call(body, name="sum_unpack_small", out_shape=[jax.ShapeDtypeStruct(s, F32) for s in shapes])(packs)


def _adamw_many(ws, gs, ms, vs):
    n = len(ws)

    def body(*refs):
        w_refs, g_refs, m_refs, v_refs = (refs[k * n:(k + 1) * n] for k in range(4))
        d_refs, nm_refs, nv_refs = (refs[(4 + k) * n:(5 + k) * n] for k in range(3))
        for k in range(n):
            g_ = g_refs[k][...]
            m_ = ADAM_B1 * m_refs[k][...] + (1.0 - ADAM_B1) * g_
            v_ = ADAM_B2 * v_refs[k][...] + (1.0 - ADAM_B2) * (g_ * g_)
            m_hat = m_ / (1.0 - ADAM_B1 ** ADAM_STEP)
            v_hat = v_ / (1.0 - ADAM_B2 ** ADAM_STEP)
            d_refs[k][...] = -ADAM_LR * (m_hat / (jnp.sqrt(v_hat) + ADAM_EPS) + ADAM_WD * w_refs[k][...])
            nm_refs[k][...] = m_
            nv_refs[k][...] = v_

    shapes = [jax.ShapeDtypeStruct(a.shape, F32) for a in ws]
    outs = pl.pallas_call(body, name="adamw_small", out_shape=shapes * 3)(*ws, *gs, *ms, *vs)
    return outs[:n], outs[n:2 * n], outs[2 * n:]


def kernel(x, mem, pre_mix_g, w_in, q_norm_g, k_norm_g, hg_lb, hg_out_norm_g, w_out, post_mix_g, pre_x_g, mem_norm_g, w_xq, w_xkv, w_xo, post_x_g, pre_ffn_g, w_up, conv_w, conv_b, w_down, post_ffn_g, loss_target, m_pre_mix_g, m_w_in, m_q_norm_g, m_k_norm_g, m_hg_lb, m_hg_out_norm_g, m_w_out, m_post_mix_g, m_pre_x_g, m_mem_norm_g, m_w_xq, m_w_xkv, m_w_xo, m_post_x_g, m_pre_ffn_g, m_w_up, m_conv_w, m_conv_b, m_w_down, m_post_ffn_g, v_pre_mix_g, v_w_in, v_q_norm_g, v_k_norm_g, v_hg_lb, v_hg_out_norm_g, v_w_out, v_post_mix_g, v_pre_x_g, v_mem_norm_g, v_w_xq, v_w_xkv, v_w_xo, v_post_x_g, v_pre_ffn_g, v_w_up, v_conv_w, v_conv_b, v_w_down, v_post_ffn_g):
    args = dict(locals())
    w = {n: args[n] for n in _WEIGHTS}
    m = {n: args["m_" + n] for n in _WEIGHTS}
    v = {n: args["v_" + n] for n in _WEIGHTS}
    me = _dev_index(*_mesh_pos())

    wire = {n: _bf(w[n][0].T) for n in _COL_SHARDED}
    wire.update({n: _bf(w[n][0]) for n in _ROW_SHARDED})

    loss, grad_x, g, pending = _local_step(x[0], mem[0], loss_target[0], w, wire)

    grads, delta, new_m, new_v = {}, {}, {}, {}

    me_arr = jnp.reshape(me, (1,)).astype(jnp.int32)

    def update(n, parts):
        if n in _ADAM_TRANSPOSED:
            outs = _sum_adamw(*parts, me_arr, w[n][0].T, m[n][0].T, v[n][0].T, "update_" + n)
            grads[n], delta[n], new_m[n], new_v[n] = (a.T[None] for a in outs)
        elif n in _COL_SHARDED:
            gsum = _sum_parts(*parts, me_arr, "sum_" + n).T
            grads[n] = gsum[None]
            delta[n], new_m[n], new_v[n] = (a[None] for a in _adamw(w[n][0], gsum, m[n][0], v[n][0], "adamw_" + n))
        else:
            outs = _sum_adamw(*parts, me_arr, w[n][0], m[n][0], v[n][0], "update_" + n)
            grads[n], delta[n], new_m[n], new_v[n] = (a[None] for a in outs)

    small = list(_REPLICATED) + ["hg_lb", "conv_w"]
    vals = [g[n] for n in _REPLICATED] + [g["hg_lb"].reshape(4, HG_DIM), g["conv_w"], jnp.pad(loss, ((0, 0), (0, LANE - 1)))]
    shapes = [val.shape for val in vals]
    fetch_small, _ = _gather_begin([_pack_small(vals)], "small")

    after = grad_x
    for tag, names in (("down", ["w_down"]), ("up", ["w_up"]), ("x", ["w_xo", "w_xq", "w_xkv"]), ("out", ["w_out"]),
                       ("in", ["w_in"])):
        for n, parts in zip(names, _exchange_end(pending[tag], after)):
            update(n, parts)
            after = new_v[n]

    (packs,) = _gather_end(fetch_small, [0], after)
    summed = _sum_unpack_small(packs, shapes)
    loss = summed[-1][0, 0]
    for n, s in zip(small, summed[:-1]):
        grads[n] = s
    fold = lambda v2: v2[:, :ATT_HEAD_DIM] + v2[:, ATT_HEAD_DIM:]
    grads["q_norm_g"], grads["k_norm_g"] = fold(grads["q_norm_g"]), fold(grads["k_norm_g"])
    grads["hg_lb"] = lax.dynamic_slice_in_dim(grads["hg_lb"].reshape(2, 2, HG_DIM), me * (HG_DIM // N_DEV),
                                              HG_DIM // N_DEV, axis=2)
    grads["conv_w"] = lax.dynamic_slice_in_dim(grads["conv_w"], me * (2 * D_FF // N_DEV), 2 * D_FF // N_DEV, axis=1)[None]

    flat2 = lambda a: a.reshape(-1, a.shape[-1])
    outs = _adamw_many(*[[flat2(d[n]) for n in small] for d in (w, grads, m, v)])
    for dst, vals in zip((delta, new_m, new_v), outs):
        for n, val in zip(small, vals):
            dst[n] = val.reshape(w[n].shape)

    return (loss, grad_x[None], *[grads[n] for n in _WEIGHTS], *[delta[n] for n in _WEIGHTS],
            *[new_m[n] for n in _WEIGHTS], *[new_v[n] for n in _WEIGHTS])
```

```python
import jax
import jax.numpy as jnp
from jax import lax
from jax.experimental import pallas as pl
from jax.experimental.pallas import tpu as pltpu

F32 = jnp.float32
BF16 = jnp.bfloat16

D_MODEL = 1024
GRID_W = 64
EPS = 1e-6
ATT_HEADS = 8
ATT_HEAD_DIM = 64
ATT_Q_DIM = 512
ATT_KV_DIM = 128
ROPE_THETA = 10000.0
HG_HEADS = 4
HG_DIM = 512
HG_CHUNK = 32
HG_CHUNK_LOG2 = 5
HG_BLOCK_FWD = 256
HG_BLOCK_BWD = 128
N_IN = 3328
X_HEADS = 4
X_HEAD_DIM = 256
D_FF = 2816
N_DEV = 8
LANE = 128
ADAM_LR = 0.001
ADAM_B1 = 0.9
ADAM_B2 = 0.999
ADAM_EPS = 1e-08
ADAM_WD = 0.01
ADAM_STEP = 10
VMEM_LIMIT = 56 * 1024 * 1024

MESH_T = pl.DeviceIdType.MESH


def _params(**kw):
    return pltpu.CompilerParams(vmem_limit_bytes=VMEM_LIMIT, **kw)


def _dot(a, b, ca, cb):
    return lax.dot_general(a, b, (((ca,), (cb,)), ((), ())), preferred_element_type=F32)


def _bf(x):
    return x.astype(BF16)


def _sigmoid(x):
    return 1.0 / (1.0 + jnp.exp(-x))


def _rms_fwd(x, g):
    r = lax.rsqrt(jnp.mean(x * x, axis=-1, keepdims=True) + EPS)
    return x * r * g


def _rms_bwd(dy, x, g):
    r = lax.rsqrt(jnp.mean(x * x, axis=-1, keepdims=True) + EPS)
    xh = x * r
    dg = jnp.sum(dy * xh, axis=0, keepdims=True)
    t = dy * g
    dx = r * (t - xh * jnp.mean(t * xh, axis=-1, keepdims=True))
    return dx, dg


def _full(shape):
    nd = len(shape)
    return pl.BlockSpec(shape, lambda *a: (0,) * nd)


def _norm_mm(x, g, w, trans, tn, name, after=(), tm=512):
    t, d = x.shape
    n = w.shape[0] if trans else w.shape[1]
    tm = min(tm, t)

    def body(x_ref, g_ref, w_ref, h_ref, p_ref):
        h = _bf(_rms_fwd(x_ref[...], g_ref[...]))
        h_ref[...] = h
        p_ref[...] = _dot(h, w_ref[...], 1, 1 if trans else 0)

    w_spec = pl.BlockSpec((tn, d), lambda i, j: (j, 0)) if trans else pl.BlockSpec((d, tn), lambda i, j: (0, j))
    return _call(
        body, (x, g, w), name=name, grid=(t // tm, n // tn),
        in_specs=[pl.BlockSpec((tm, d), lambda i, j: (i, 0)), _full((1, d)), w_spec],
        out_specs=[pl.BlockSpec((tm, d), lambda i, j: (i, 0)), pl.BlockSpec((tm, tn), lambda i, j: (i, j))],
        out_shape=[jax.ShapeDtypeStruct((t, d), BF16), jax.ShapeDtypeStruct((t, n), F32)], after=after)


def _mm_postnorm_res(a, w, g, res, name, tm=256):
    t, k = a.shape
    d = w.shape[1]

    def body(a_ref, w_ref, g_ref, res_ref, y_ref, o_ref):
        y = _dot(a_ref[...], w_ref[...], 1, 0)
        y_ref[...] = y
        o_ref[...] = res_ref[...] + _rms_fwd(y, g_ref[...])

    row = lambda width: pl.BlockSpec((tm, width), lambda i: (i, 0))
    return pl.pallas_call(
        body, name=name, grid=(t // tm,),
        in_specs=[row(k), _full((k, d)), _full((1, d)), row(d)],
        out_specs=[row(d), row(d)],
        out_shape=[jax.ShapeDtypeStruct((t, d), F32)] * 2,
        compiler_params=_params(),
    )(a, w, g, res)


def _mm_postnorm_res_loss(a, w, g, res, tgt, name, tm=256):
    t, k = a.shape
    d = w.shape[1]

    def body(a_ref, w_ref, g_ref, res_ref, tgt_ref, y_ref, dout_ref, loss_ref):
        @pl.when(pl.program_id(0) == 0)
        def _():
            loss_ref[...] = jnp.zeros_like(loss_ref)

        y = _dot(a_ref[...], w_ref[...], 1, 0)
        y_ref[...] = y
        diff = res_ref[...] + _rms_fwd(y, g_ref[...]) - tgt_ref[...]
        dout_ref[...] = diff * (1.0 / d)
        part = jnp.sum(jnp.sum(diff * diff, axis=-1, keepdims=True), axis=0, keepdims=True)
        loss_ref[...] += (0.5 / d) * part

    row = lambda width: pl.BlockSpec((tm, width), lambda i: (i, 0))
    return pl.pallas_call(
        body, name=name, grid=(t // tm,),
        in_specs=[row(k), _full((k, d)), _full((1, d)), row(d), row(d)],
        out_specs=[row(d), row(d), _full((1, 1))],
        out_shape=[jax.ShapeDtypeStruct((t, d), F32)] * 2 + [jax.ShapeDtypeStruct((1, 1), F32)],
        compiler_params=_params(),
    )(a, w, g, res, tgt)


def _postnorm_bwd_mm(dout, y, g, w, name, da_dtype=F32, tm=256):
    t, d = y.shape
    k = w.shape[0]

    def body(dout_ref, y_ref, g_ref, w_ref, dy_ref, da_ref, dg_ref):
        @pl.when(pl.program_id(0) == 0)
        def _():
            dg_ref[...] = jnp.zeros_like(dg_ref)

        dy, dg = _rms_bwd(dout_ref[...], y_ref[...], g_ref[...])
        dg_ref[...] += dg
        dyb = _bf(dy)
        dy_ref[...] = dyb
        da_ref[...] = _dot(dyb, w_ref[...], 1, 1).astype(da_dtype)

    row = lambda width: pl.BlockSpec((tm, width), lambda i: (i, 0))
    return pl.pallas_call(
        body, name=name, grid=(t // tm,),
        in_specs=[row(d), row(d), _full((1, d)), _full((k, d))],
        out_specs=[row(d), row(k), _full((1, d))],
        out_shape=[jax.ShapeDtypeStruct((t, d), BF16), jax.ShapeDtypeStruct((t, k), da_dtype), jax.ShapeDtypeStruct((1, d), F32)],
        compiler_params=_params(),
    )(dout, y, g, w)


def _mm_prenorm_bwd(dp, w, trans, x, g, dres, name, after=(), tm=512):
    dps = list(dp) if isinstance(dp, (list, tuple)) else [dp]
    nparts = len(dps)
    t, n_each = dps[0].shape
    d = x.shape[1]
    tm = min(tm, t)

    def body(*refs):
        dp_refs = refs[:nparts]
        w_ref, x_ref, g_ref, dres_ref, dx_ref, dg_ref = refs[nparts:]

        @pl.when(pl.program_id(0) == 0)
        def _():
            dg_ref[...] = jnp.zeros_like(dg_ref)

        dh = None
        for part, dp_ref in enumerate(dp_refs):
            cols = slice(part * n_each, (part + 1) * n_each)
            term = _dot(dp_ref[...], w_ref[cols, :], 1, 0) if trans else _dot(dp_ref[...], w_ref[:, cols], 1, 1)
            dh = term if dh is None else dh + term
        dx, dg = _rms_bwd(dh, x_ref[...], g_ref[...])
        dg_ref[...] += dg
        dx_ref[...] = dres_ref[...] + dx

    row = lambda width: pl.BlockSpec((tm, width), lambda i: (i, 0))
    return _call(
        body, (*dps, w, x, g, dres), name=name, grid=(t // tm,),
        in_specs=[row(n_each)] * nparts + [_full(w.shape), row(d), _full((1, d)), row(d)],
        out_specs=[row(d), _full((1, d))],
        out_shape=[jax.ShapeDtypeStruct((t, d), F32), jax.ShapeDtypeStruct((1, d), F32)], after=after)


def _dw(a, b, name, bcol=None, tka=256):
    parts = list(a) if isinstance(a, (list, tuple)) else [a]
    nparts = len(parts)
    t, ka_each = parts[0].shape
    bci, nb = (0, b.shape[1]) if bcol is None else bcol
    nt_each = ka_each // tka

    def body(*refs):
        a_refs, b_ref, o_ref = refs[:nparts], refs[nparts], refs[nparts + 1]
        i = pl.program_id(0)
        for part, a_ref in enumerate(a_refs):
            @pl.when((i >= part * nt_each) & (i < (part + 1) * nt_each))
            def _():
                o_ref[...] = _bf(_dot(a_ref[...], b_ref[...], 0, 0))

    a_spec = lambda part: pl.BlockSpec((t, tka), lambda i: (0, jnp.clip(i - part * nt_each, 0, nt_each - 1)))
    return pl.pallas_call(
        body, name=name, grid=(nt_each * nparts,),
        in_specs=[a_spec(part) for part in range(nparts)] + [pl.BlockSpec((t, nb), lambda i: (0, bci))],
        out_specs=pl.BlockSpec((tka, nb), lambda i: (i, 0)),
        out_shape=jax.ShapeDtypeStruct((nparts * ka_each, nb), BF16),
        compiler_params=_params(),
    )(*parts, b)


def _rope_tables(t):
    pos = jnp.arange(t)
    r = (pos // GRID_W).astype(F32)
    c = (pos % GRID_W).astype(F32)
    npair = ATT_HEAD_DIM // 4
    inv = jnp.power(ROPE_THETA, -jnp.arange(npair, dtype=F32) / npair)
    ang = jnp.concatenate([r[:, None] * inv, c[:, None] * inv], axis=-1)
    cos = jnp.repeat(jnp.cos(ang), 2, axis=-1)
    sin = jnp.repeat(jnp.sin(ang), 2, axis=-1)
    even = (jnp.arange(ATT_HEAD_DIM) % 2) == 0
    sa = jnp.where(even, -sin, 0.0)
    sb = jnp.where(even, 0.0, sin)
    two = lambda a: jnp.tile(a, (1, 2))
    return two(cos), two(sa), two(sb)


def _head_sum_matrix():
    a = jnp.arange(LANE) // ATT_HEAD_DIM
    return (a[:, None] == a[None, :]).astype(BF16)


def _head_mean(v, bd):
    hi = _bf(v)
    lo = _bf(v - hi.astype(F32))
    return (_dot(hi, bd, 1, 0) + _dot(lo, bd, 1, 0)) * (1.0 / ATT_HEAD_DIM)


def _qk_prep(p, gq, gk, tables, bd, tm=512):
    t = p.shape[0]
    tm = min(tm, t)
    cc, sa, sb = tables

    def body(p_ref, gq_ref, gk_ref, cc_ref, sa_ref, sb_ref, bd_ref, q_ref, k_ref):
        cc_, sa_, sb_, bd_ = cc_ref[...], sa_ref[...], sb_ref[...], bd_ref[...]
        low = lax.broadcasted_iota(jnp.int32, (tm, LANE), 1) < ATT_HEAD_DIM

        def normrope(xs, g):
            xn = xs * lax.rsqrt(_head_mean(xs * xs, bd_) + EPS) * g
            return xn * cc_ + pltpu.roll(xn, LANE - 1, 1) * sa_ + pltpu.roll(xn, 1, 1) * sb_

        for j in range(4):
            y = normrope(p_ref[:, j * LANE:(j + 1) * LANE], gq_ref[...]) * (ATT_HEAD_DIM ** -0.5)
            yr = pltpu.roll(y, ATT_HEAD_DIM, 1)
            if j // 2 == 0:
                h0, h1 = jnp.where(low, y, 0.0), jnp.where(low, yr, 0.0)
            else:
                h0, h1 = jnp.where(low, 0.0, yr), jnp.where(low, 0.0, y)
            q_ref[:, (2 * j) * LANE:(2 * j + 1) * LANE] = _bf(h0)
            q_ref[:, (2 * j + 1) * LANE:(2 * j + 2) * LANE] = _bf(h1)
        k_ref[...] = _bf(normrope(p_ref[:, ATT_Q_DIM:ATT_Q_DIM + LANE], gk_ref[...]))

    row = lambda width: pl.BlockSpec((tm, width), lambda i: (i, 0))
    return pl.pallas_call(
        body, name="qk_prep", grid=(t // tm,),
        in_specs=[row(ATT_Q_DIM + LANE), _full((1, LANE)), _full((1, LANE)), row(LANE), row(LANE), row(LANE),
                  _full((LANE, LANE))],
        out_specs=[row(ATT_HEADS * LANE), row(LANE)],
        out_shape=[jax.ShapeDtypeStruct((t, ATT_HEADS * LANE), BF16), jax.ShapeDtypeStruct((t, LANE), BF16)],
        compiler_params=_params(),
    )(p, gq, gk, cc, sa, sb, bd)


def _qk_prep_bwd(p, dq, dk, gq, gk, tables, bd, tm=512):
    t = p.shape[0]
    tm = min(tm, t)
    cc, sa, sb = tables

    def body(p_ref, dq_ref, dk_ref, gq_ref, gk_ref, cc_ref, sa_ref, sb_ref, bd_ref, dp_ref, dgq_ref, dgk_ref):
        @pl.when(pl.program_id(0) == 0)
        def _():
            dgq_ref[...] = jnp.zeros_like(dgq_ref)
            dgk_ref[...] = jnp.zeros_like(dgk_ref)

        cc_, sa_, sb_, bd_ = cc_ref[...], sa_ref[...], sb_ref[...], bd_ref[...]
        low = lax.broadcasted_iota(jnp.int32, (tm, LANE), 1) < ATT_HEAD_DIM

        def bwd(xs, g, dy):
            r = lax.rsqrt(_head_mean(xs * xs, bd_) + EPS)
            xh = xs * r
            dxn = dy * cc_ + pltpu.roll(dy * sa_, 1, 1) + pltpu.roll(dy * sb_, LANE - 1, 1)
            dg = jnp.sum(dxn * xh, axis=0, keepdims=True)
            tt = dxn * g
            return r * (tt - xh * _head_mean(tt * xh, bd_)), dg

        dgq = jnp.zeros((1, LANE), F32)
        for j in range(4):
            d0 = dq_ref[:, (2 * j) * LANE:(2 * j + 1) * LANE]
            d1 = dq_ref[:, (2 * j + 1) * LANE:(2 * j + 2) * LANE]
            if j // 2 == 0:
                dy = jnp.where(low, d0, pltpu.roll(d1, ATT_HEAD_DIM, 1))
            else:
                dy = jnp.where(low, pltpu.roll(d0, ATT_HEAD_DIM, 1), d1)
            dx, dg = bwd(p_ref[:, j * LANE:(j + 1) * LANE], gq_ref[...], dy * (ATT_HEAD_DIM ** -0.5))
            dp_ref[:, j * LANE:(j + 1) * LANE] = _bf(dx)
            dgq = dgq + dg
        dgq_ref[...] += dgq
        dx, dg = bwd(p_ref[:, ATT_Q_DIM:ATT_Q_DIM + LANE], gk_ref[...], dk_ref[...])
        dp_ref[:, ATT_Q_DIM:ATT_Q_DIM + LANE] = _bf(dx)
        dgk_ref[...] += dg

    row = lambda width: pl.BlockSpec((tm, width), lambda i: (i, 0))
    return pl.pallas_call(
        body, name="qk_prep_bwd", grid=(t // tm,),
        in_specs=[row(ATT_Q_DIM + LANE), row(ATT_HEADS * LANE), row(LANE), _full((1, LANE)), _full((1, LANE)),
                  row(LANE), row(LANE), row(LANE), _full((LANE, LANE))],
        out_specs=[row(ATT_Q_DIM + LANE), _full((1, LANE)), _full((1, LANE))],
        out_shape=[jax.ShapeDtypeStruct((t, ATT_Q_DIM + LANE), BF16), jax.ShapeDtypeStruct((1, LANE), F32),
                   jax.ShapeDtypeStruct((1, LANE), F32)],
        compiler_params=_params(),
    )(p, dq, dk, gq, gk, cc, sa, sb, bd)


def _attn_fwd(q, k, p, tq=256):
    t = k.shape[0]
    tq = min(tq, t)
    v_blk = (ATT_Q_DIM + ATT_KV_DIM) // LANE

    def body(q_ref, k_ref, v_ref, o_ref, o32_ref, lse_ref):
        k_ = k_ref[...]
        v = v_ref[...]
        lane_k = lax.broadcasted_iota(jnp.int32, (t, LANE), 1)
        lane_q = lax.broadcasted_iota(jnp.int32, (tq, LANE), 1)
        lowk, lowq = lane_k < ATT_HEAD_DIM, lane_q < ATT_HEAD_DIM
        ones_lane = (ATT_HEAD_DIM, 0)
        vm = (_bf(jnp.where(lowk, v, jnp.where(lane_k == ones_lane[0], 1.0, 0.0))),
              _bf(jnp.where(lowk, jnp.where(lane_k == ones_lane[1], 1.0, 0.0), v)))
        for j in range(4):
            kvh = j // 2
            acc = None
            for sub in range(2):
                h = 2 * j + sub
                s = _dot(q_ref[:, h * LANE:(h + 1) * LANE], k_, 1, 1)
                mx = jnp.max(s, axis=-1, keepdims=True)
                ov = _dot(jnp.exp(_bf(s - mx)), vm[kvh], 1, 0)
                l = jnp.sum(jnp.where(lane_q == ones_lane[kvh], ov, 0.0), axis=-1, keepdims=True)
                lse_ref[h] = mx + jnp.log(l)
                o = jnp.where(lowq if kvh == 0 else ~lowq, ov, 0.0) * (1.0 / l)
                if sub != kvh:
                    o = pltpu.roll(o, ATT_HEAD_DIM, 1)
                acc = o if acc is None else acc + o
            o32_ref[:, j * LANE:(j + 1) * LANE] = acc
            o_ref[:, j * LANE:(j + 1) * LANE] = _bf(acc)

    row = pl.BlockSpec((tq, ATT_Q_DIM), lambda i: (i, 0))
    return _call(
        body, (q, k, p), name="attn_fwd", grid=(t // tq,),
        in_specs=[pl.BlockSpec((tq, ATT_HEADS * LANE), lambda i: (i, 0)), _full((t, LANE)),
                  pl.BlockSpec((t, LANE), lambda i: (0, v_blk))],
        out_specs=[row, row, pl.BlockSpec((ATT_HEADS, tq, 1), lambda i: (0, i, 0))],
        out_shape=[jax.ShapeDtypeStruct((t, ATT_Q_DIM), BF16), jax.ShapeDtypeStruct((t, ATT_Q_DIM), F32),
                   jax.ShapeDtypeStruct((ATT_HEADS, t, 1), F32)])


def _attn_bwd(q, k, p, dcat, o32, lse, tq=256):
    t = k.shape[0]
    tq = min(tq, t)
    v_blk = (ATT_Q_DIM + ATT_KV_DIM) // LANE

    def body(q_ref, k_ref, v_ref, do_ref, o_ref, lse_ref, dq_ref, dk_ref, dv_ref):
        @pl.when(pl.program_id(0) == 0)
        def _():
            dk_ref[...] = jnp.zeros_like(dk_ref)
            dv_ref[...] = jnp.zeros_like(dv_ref)

        k_ = k_ref[...]
        vb = _bf(v_ref[...])
        lowq = lax.broadcasted_iota(jnp.int32, (tq, LANE), 1) < ATT_HEAD_DIM
        dk_acc = jnp.zeros((t, LANE), F32)
        dv_acc = jnp.zeros((t, LANE), F32)
        for j in range(4):
            kvh = j // 2
            dop = do_ref[:, j * LANE:(j + 1) * LANE]
            prod = dop * o_ref[:, j * LANE:(j + 1) * LANE]
            d_low = jnp.sum(jnp.where(lowq, prod, 0.0), axis=-1, keepdims=True)
            d_sub = (d_low, jnp.sum(prod, axis=-1, keepdims=True) - d_low)
            for sub in range(2):
                h = 2 * j + sub
                src = dop if sub == kvh else pltpu.roll(dop, ATT_HEAD_DIM, 1)
                do_h = _bf(jnp.where(lowq, src, 0.0) if kvh == 0 else jnp.where(lowq, 0.0, src))
                qh = q_ref[:, h * LANE:(h + 1) * LANE]
                pr = jnp.exp(_bf(_dot(qh, k_, 1, 1) - lse_ref[h]))
                ds = pr * _bf(_dot(do_h, vb, 1, 1) - d_sub[sub])
                dq_ref[:, h * LANE:(h + 1) * LANE] = _dot(ds, k_, 1, 0)
                dk_acc = dk_acc + _dot(ds, qh, 0, 0)
                dv_acc = dv_acc + _dot(pr, do_h, 0, 0)
        dk_ref[...] += dk_acc
        dv_ref[...] += dv_acc

    row = pl.BlockSpec((tq, ATT_Q_DIM), lambda i: (i, 0))
    return _call(
        body, (q, k, p, dcat, o32, lse), name="attn_bwd", grid=(t // tq,),
        in_specs=[pl.BlockSpec((tq, ATT_HEADS * LANE), lambda i: (i, 0)), _full((t, LANE)),
                  pl.BlockSpec((t, LANE), lambda i: (0, v_blk)), row, row,
                  pl.BlockSpec((ATT_HEADS, tq, 1), lambda i: (0, i, 0))],
        out_specs=[pl.BlockSpec((tq, ATT_HEADS * LANE), lambda i: (i, 0)), _full((t, LANE)), _full((t, LANE))],
        out_shape=[jax.ShapeDtypeStruct((t, ATT_HEADS * LANE), F32), jax.ShapeDtypeStruct((t, LANE), F32),
                   jax.ShapeDtypeStruct((t, LANE), F32)])


def _lower_bounds(a0, a1):
    def body(a0_ref, a1_ref, lb_ref):
        m = jnp.maximum(a0_ref[...], a1_ref[...])
        e0, e1 = jnp.exp(a0_ref[...] - m), jnp.exp(a1_ref[...] - m)
        lb_ref[...] = e0 / (e0 + e1)

    return pl.pallas_call(body, name="lower_bounds", out_shape=jax.ShapeDtypeStruct(a0.shape, F32))(a0, a1)


def _lower_bounds_bwd(a0, a1, dlb):
    def body(a0_ref, a1_ref, dlb_ref, d0_ref, d1_ref):
        m = jnp.maximum(a0_ref[...], a1_ref[...])
        e0, e1 = jnp.exp(a0_ref[...] - m), jnp.exp(a1_ref[...] - m)
        lb = e0 / (e0 + e1)
        d0 = dlb_ref[...] * lb * (1.0 - lb)
        d0_ref[...] = d0
        d1_ref[...] = -d0

    return pl.pallas_call(body, name="lower_bounds_bwd", out_shape=[jax.ShapeDtypeStruct(a0.shape, F32)] * 2)(a0, a1, dlb)


def _chunk_scan(x, pos, down):
    n = x.shape[0]
    s = 1
    while s < HG_CHUNK:
        if down:
            x = x + jnp.where(pos >= s, pltpu.roll(x, s, 0), 0.0)
        else:
            x = x + jnp.where(pos < HG_CHUNK - s, pltpu.roll(x, n - s, 0), 0.0)
        s *= 2
    return x


def _chunk_concat(x, nc):
    xb = _bf(x)
    rows = []
    for c in range(nc):
        pieces = [xb[c * HG_CHUNK:(c + 1) * HG_CHUNK, :]]
        if c:
            pieces.insert(0, jnp.zeros((HG_CHUNK, c * LANE), BF16))
        if c < nc - 1:
            pieces.append(jnp.zeros((HG_CHUNK, (nc - 1 - c) * LANE), BF16))
        rows.append(jnp.concatenate(pieces, axis=1) if len(pieces) > 1 else pieces[0])
    return jnp.concatenate(rows, axis=0) if nc > 1 else rows[0]


def _chunk_pick(x, nc):
    rows = [x[c * HG_CHUNK:(c + 1) * HG_CHUNK, c * LANE:(c + 1) * LANE] for c in range(nc)]
    return jnp.concatenate(rows, axis=0) if nc > 1 else rows[0]


def _chunk_rows(vals):
    rows = [jnp.broadcast_to(v, (HG_CHUNK, LANE)) for v in vals]
    return jnp.concatenate(rows, axis=0) if len(rows) > 1 else rows[0]


def _hgrn_gates(hq, z, lb):
    sq = _sigmoid(hq)
    sig = _sigmoid(z)
    f = lb + (1.0 - lb) * sig
    return hq * sq, sq, sig, f, jnp.log(f)


def _hgrn_local(q, f, g, v, pos, cid, amask, rev, nc):
    k = 1.0 - f
    ba = _chunk_scan(g, pos, not rev)
    bb = _chunk_scan(g, pos, rev)
    eq = 0.5 * (ba - bb + g)
    ex_q, ex_k, ex_i, ex_d = jnp.exp(eq), jnp.exp(-eq), jnp.exp(ba), jnp.exp(bb - g)
    qb, kb, qi, kd = q * ex_q, k * ex_k, q * ex_i, k * ex_d
    dvec = jnp.exp(ba + bb - g)
    a = jnp.where(amask, _dot(_bf(qb), _bf(kb), 1, 1), 0.0)
    kd_m = _chunk_concat(kd, nc)
    qi_m = _chunk_concat(qi, nc)
    ut_all = _dot(_bf(v), kd_m, 0, 0)
    return dict(k=k, ex_q=ex_q, ex_k=ex_k, ex_i=ex_i, ex_d=ex_d, qb=qb, kb=kb, qi=qi, kd=kd, dvec=dvec, a=a,
                kd_m=kd_m, qi_m=qi_m, ut_all=ut_all)


def _hgrn_masks(n, rev):
    row = lax.broadcasted_iota(jnp.int32, (n, LANE), 0)
    ti = lax.broadcasted_iota(jnp.int32, (n, n), 0)
    si = lax.broadcasted_iota(jnp.int32, (n, n), 1)
    tri = (si >= ti) if rev else (si <= ti)
    same = jnp.right_shift(ti, HG_CHUNK_LOG2) == jnp.right_shift(si, HG_CHUNK_LOG2)
    return jnp.bitwise_and(row, HG_CHUNK - 1), jnp.right_shift(row, HG_CHUNK_LOG2), same & tri


def _hgrn_specs(t, rev, bwd):
    n = min(HG_BLOCK_BWD if bwd else HG_BLOCK_FWD, t)
    nb = t // n
    if rev != bwd:
        blk = lambda i: nb - 1 - i
    else:
        blk = lambda i: i
    col = lambda base: pl.BlockSpec((n, 2 * LANE), lambda hp, i: (blk(i), base + hp))
    return n, nb, blk, col


def _hgrn_fwd(p, lb, rev, after=()):
    t = p.shape[0]
    n, nb, blk, col = _hgrn_specs(t, rev, False)
    nc = n // HG_CHUNK
    sub = n // min(HG_BLOCK_BWD, t)
    z_base = 7 if rev else 5

    def body(hq_ref, z_ref, hi_ref, lb_ref, o_ref, ssave_ref, st_scr):
        @pl.when(pl.program_id(1) == 0)
        def _():
            st_scr[...] = jnp.zeros_like(st_scr)

        pos, cid, amask = _hgrn_masks(n, rev)
        order = list(range(nc))[::-1] if rev else list(range(nc))
        for hh in range(2):
            sl = slice(hh * LANE, (hh + 1) * LANE)
            q, _, _, f, g = _hgrn_gates(hq_ref[:, sl], z_ref[:, sl], lb_ref[0:1, sl])
            v = hi_ref[:, sl]
            c_ = _hgrn_local(q, f, g, v, pos, cid, amask, rev, nc)
            st = st_scr[hh]
            cols = [None] * nc
            for c in order:
                cols[c] = st
                st = st * c_["dvec"][c * HG_CHUNK:c * HG_CHUNK + 1, :] + c_["ut_all"][:, c * LANE:(c + 1) * LANE]
            st_scr[hh] = st
            for s in range(sub):
                ssave_ref[sub - 1 - s if rev else s, sl, :] = cols[order[s * (nc // sub)]]
            st_all = _bf(jnp.concatenate(cols, axis=1))
            o_ref[:, sl] = _dot(_bf(c_["a"]), _bf(v), 1, 0) + _dot(c_["qi_m"], st_all, 1, 1)

    return _call(
        body, (p, p, p, lb), name="hgrn_fwd_rev" if rev else "hgrn_fwd", grid=(2, nb),
        in_specs=[col(3), col(z_base), col(9), pl.BlockSpec((1, 2 * LANE), lambda hp, i: (0, hp))],
        out_specs=[pl.BlockSpec((n, 2 * LANE), lambda hp, i: (blk(i), hp)),
                   pl.BlockSpec((sub, 2 * LANE, LANE), lambda hp, i: (blk(i), hp, 0))],
        out_shape=[jax.ShapeDtypeStruct((t, HG_DIM), F32), jax.ShapeDtypeStruct((nb * sub, HG_DIM, LANE), F32)],
        scratch_shapes=[pltpu.VMEM((2, LANE, LANE), F32)], after=after)


def _hgrn_bwd(p, lb, do, ssave, rev, after=()):
    t = p.shape[0]
    n, nb, blk, col = _hgrn_specs(t, rev, True)
    nc = n // HG_CHUNK
    z_base = 7 if rev else 5

    def body(hq_ref, z_ref, hi_ref, lb_ref, do_ref, ssave_ref, dhq_ref, dz_ref, dhi_ref, dlb_ref, dst_scr):
        @pl.when(pl.program_id(1) == 0)
        def _():
            dst_scr[...] = jnp.zeros_like(dst_scr)
            dlb_ref[...] = jnp.zeros_like(dlb_ref)

        pos, cid, amask = _hgrn_masks(n, rev)
        order = list(range(nc))[::-1] if rev else list(range(nc))
        for hh in range(2):
            sl = slice(hh * LANE, (hh + 1) * LANE)
            hq, lbv = hq_ref[:, sl], lb_ref[0:1, sl]
            q, sq, sig, f, g = _hgrn_gates(hq, z_ref[:, sl], lbv)
            v = hi_ref[:, sl]
            c_ = _hgrn_local(q, f, g, v, pos, cid, amask, rev, nc)
            dvec, ut_all = c_["dvec"], c_["ut_all"]
            drow = lambda c: dvec[c * HG_CHUNK:c * HG_CHUNK + 1, :]
            st = ssave_ref[0, sl, :]
            cols = [None] * nc
            for c in order:
                cols[c] = st
                st = st * drow(c) + ut_all[:, c * LANE:(c + 1) * LANE]
            dob, vb = _bf(do_ref[:, sl]), _bf(v)
            gt_all = _dot(dob, c_["qi_m"], 0, 0)
            dcur = dst_scr[hh]
            dnext = [None] * nc
            ddrow = [None] * nc
            for c in order[::-1]:
                dnext[c] = dcur
                ddrow[c] = jnp.sum(cols[c] * dcur, axis=0, keepdims=True) * drow(c)
                dcur = dcur * drow(c) + gt_all[:, c * LANE:(c + 1) * LANE]
            dst_scr[hh] = dcur
            dsn_all = _bf(jnp.concatenate(dnext, axis=1))
            st_all = _bf(jnp.concatenate(cols, axis=1))
            da = _bf(jnp.where(amask, _dot(dob, vb, 1, 1), 0.0))
            dv = _dot(_bf(c_["a"]), dob, 0, 0) + _dot(c_["kd_m"], dsn_all, 1, 1)
            dqb = _dot(da, _bf(c_["kb"]), 1, 0)
            dkb = _dot(da, _bf(c_["qb"]), 0, 0)
            dqi = _chunk_pick(_dot(dob, st_all, 1, 0), nc)
            dkd = _chunk_pick(_dot(vb, dsn_all, 1, 0), nc)
            dq = dqb * c_["ex_q"] + dqi * c_["ex_i"]
            dk = dkb * c_["ex_k"] + dkd * c_["ex_d"]
            e = dqb * c_["qb"] - dkb * c_["kb"] + dqi * c_["qi"]
            w = dkd * c_["kd"]
            dg = _chunk_scan(e, pos, rev) + (_chunk_scan(w, pos, not rev) - w) + _chunk_rows(ddrow)
            df = dg / f - dk
            dz_ref[:, sl] = df * (1.0 - lbv) * sig * (1.0 - sig)
            dlb_ref[0:1, sl] += jnp.sum(df * (1.0 - sig), axis=0, keepdims=True)
            dhq_ref[:, sl] = dq * (sq * (1.0 + hq * (1.0 - sq)))
            dhi_ref[:, sl] = dv

    out = pl.BlockSpec((n, 2 * LANE), lambda hp, i: (blk(i), hp))
    return _call(
        body, (p, p, p, lb, do, ssave), name="hgrn_bwd_rev" if rev else "hgrn_bwd", grid=(2, nb),
        in_specs=[col(3), col(z_base), col(9), pl.BlockSpec((1, 2 * LANE), lambda hp, i: (0, hp)), out,
                  pl.BlockSpec((1, 2 * LANE, LANE), lambda hp, i: (blk(i), hp, 0))],
        out_specs=[out, out, out, pl.BlockSpec((1, 2 * LANE), lambda hp, i: (0, hp))],
        out_shape=[jax.ShapeDtypeStruct((t, HG_DIM), F32)] * 3 + [jax.ShapeDtypeStruct((1, HG_DIM), F32)],
        scratch_shapes=[pltpu.VMEM((2, LANE, LANE), F32)], after=after)


def _mix_out(att, of, ob, p, gout, w_out, g_post, x, after=(), tm=256):
    t, d = x.shape
    tm = min(tm, t)

    def body(att_ref, of_ref, ob_ref, hg0_ref, hg1_ref, gout_ref, w_ref, g_ref, x_ref, cat_ref, y_ref, x1_ref):
        cat_ref[:, :ATT_Q_DIM] = att_ref[...]
        for h in range(HG_HEADS):
            sl = slice(h * LANE, (h + 1) * LANE)
            hg_ref = hg0_ref if h < 2 else hg1_ref
            hg = hg_ref[:, (h % 2) * LANE:(h % 2 + 1) * LANE]
            nrm = _rms_fwd(of_ref[:, sl] + ob_ref[:, sl], gout_ref[...])
            cat_ref[:, ATT_Q_DIM + h * LANE:ATT_Q_DIM + (h + 1) * LANE] = _bf(nrm * hg * _sigmoid(hg))
        y = _dot(cat_ref[...], w_ref[...], 1, 0)
        y_ref[...] = y
        x1_ref[...] = x_ref[...] + _rms_fwd(y, g_ref[...])

    row = lambda width: pl.BlockSpec((tm, width), lambda i: (i, 0))
    return _call(
        body, (att, of, ob, p, p, gout, w_out, g_post, x), name="mix_out", grid=(t // tm,),
        in_specs=[row(ATT_Q_DIM), row(HG_DIM), row(HG_DIM), pl.BlockSpec((tm, 2 * LANE), lambda i: (i, 11)),
                  pl.BlockSpec((tm, 2 * LANE), lambda i: (i, 12)), _full((1, LANE)), _full((d, d)), _full((1, d)), row(d)],
        out_specs=[row(d), row(d), row(d)],
        out_shape=[jax.ShapeDtypeStruct((t, d), BF16), jax.ShapeDtypeStruct((t, d), F32), jax.ShapeDtypeStruct((t, d), F32)],
        after=after)


def _rec_bwd(dcat, of, ob, p, gout, tm=256):
    t = of.shape[0]
    tm = min(tm, t)

    def body(dc_ref, of_ref, ob_ref, hg0_ref, hg1_ref, gout_ref, do_ref, dhg_ref, dgo_ref):
        @pl.when(pl.program_id(0) == 0)
        def _():
            dgo_ref[...] = jnp.zeros_like(dgo_ref)

        dgo = jnp.zeros((1, LANE), F32)
        for h in range(HG_HEADS):
            sl = slice(h * LANE, (h + 1) * LANE)
            hg_ref = hg0_ref if h < 2 else hg1_ref
            hg = hg_ref[:, (h % 2) * LANE:(h % 2 + 1) * LANE]
            o = of_ref[:, sl] + ob_ref[:, sl]
            sg = _sigmoid(hg)
            drec = dc_ref[:, sl]
            dhg_ref[:, sl] = drec * _rms_fwd(o, gout_ref[...]) * (sg * (1.0 + hg * (1.0 - sg)))
            do, dg = _rms_bwd(drec * hg * sg, o, gout_ref[...])
            do_ref[:, sl] = do
            dgo = dgo + dg
        dgo_ref[...] += dgo

    row = lambda width: pl.BlockSpec((tm, width), lambda i: (i, 0))
    return pl.pallas_call(
        body, name="rec_bwd", grid=(t // tm,),
        in_specs=[pl.BlockSpec((tm, HG_DIM), lambda i: (i, 1)), row(HG_DIM), row(HG_DIM),
                  pl.BlockSpec((tm, 2 * LANE), lambda i: (i, 11)), pl.BlockSpec((tm, 2 * LANE), lambda i: (i, 12)),
                  _full((1, LANE))],
        out_specs=[row(HG_DIM), row(HG_DIM), _full((1, LANE))],
        out_shape=[jax.ShapeDtypeStruct((t, HG_DIM), F32), jax.ShapeDtypeStruct((t, HG_DIM), F32),
                   jax.ShapeDtypeStruct((1, LANE), F32)],
        compiler_params=_params(),
    )(dcat, of, ob, p, p, gout)


def _assemble_dp(dp_qk, dv, dhq_f, dhq_b, dz_f, dz_b, dhi_f, dhi_b, dhg, tm=512):
    t = dv.shape[0]
    tm = min(tm, t)
    qk_w = ATT_Q_DIM + LANE

    def body(qk_ref, dv_ref, hqf_ref, hqb_ref, zf_ref, zb_ref, hif_ref, hib_ref, hg_ref, dp_ref):
        o = 0
        dp_ref[:, o:o + qk_w] = qk_ref[...]
        o += qk_w
        dp_ref[:, o:o + LANE] = _bf(dv_ref[...])
        o += LANE
        for val in (hqf_ref[...] + hqb_ref[...], zf_ref[...], zb_ref[...], hif_ref[...] + hib_ref[...], hg_ref[...]):
            dp_ref[:, o:o + HG_DIM] = _bf(val)
            o += HG_DIM

    row = lambda width: pl.BlockSpec((tm, width), lambda i: (i, 0))
    return pl.pallas_call(
        body, name="assemble_dp", grid=(t // tm,),
        in_specs=[row(qk_w), row(LANE)] + [row(HG_DIM)] * 7,
        out_specs=row(N_IN),
        out_shape=jax.ShapeDtypeStruct((t, N_IN), BF16),
        compiler_params=_params(),
    )(dp_qk, dv, dhq_f, dhq_b, dz_f, dz_b, dhi_f, dhi_b, dhg)


def _xattn_fwd(q, kv, tq=512):
    t, d = q.shape
    tq = min(tq, t)
    nm = kv.shape[0]

    def body(q_ref, kv_ref, o_ref):
        for h in range(X_HEADS):
            sl = slice(h * X_HEAD_DIM, (h + 1) * X_HEAD_DIM)
            s = _dot(_bf(q_ref[:, sl]), _bf(kv_ref[:, sl]), 1, 1) * (X_HEAD_DIM ** -0.5)
            e = jnp.exp(s - jnp.max(s, axis=-1, keepdims=True))
            pr = e * (1.0 / jnp.sum(e, axis=-1, keepdims=True))
            o_ref[:, sl] = _bf(_dot(_bf(pr), _bf(kv_ref[:, d + h * X_HEAD_DIM:d + (h + 1) * X_HEAD_DIM]), 1, 0))

    return pl.pallas_call(
        body, name="xattn_fwd", grid=(t // tq,),
        in_specs=[pl.BlockSpec((tq, d), lambda i: (i, 0)), _full((nm, 2 * d))],
        out_specs=pl.BlockSpec((tq, d), lambda i: (i, 0)),
        out_shape=jax.ShapeDtypeStruct((t, d), BF16),
        compiler_params=_params(),
    )(q, kv)


def _xattn_bwd(q, kv, do, tq=512):
    t, d = q.shape
    tq = min(tq, t)
    nm = kv.shape[0]

    def body(q_ref, kv_ref, do_ref, dq_ref, dkv_ref):
        @pl.when(pl.program_id(0) == 0)
        def _():
            dkv_ref[...] = jnp.zeros_like(dkv_ref)

        for h in range(X_HEADS):
            sl = slice(h * X_HEAD_DIM, (h + 1) * X_HEAD_DIM)
            slv = slice(d + h * X_HEAD_DIM, d + (h + 1) * X_HEAD_DIM)
            qb, kb, vb, dob = _bf(q_ref[:, sl]), _bf(kv_ref[:, sl]), _bf(kv_ref[:, slv]), _bf(do_ref[:, sl])
            s = _dot(qb, kb, 1, 1) * (X_HEAD_DIM ** -0.5)
            e = jnp.exp(s - jnp.max(s, axis=-1, keepdims=True))
            pr = e * (1.0 / jnp.sum(e, axis=-1, keepdims=True))
            dpr = _dot(dob, vb, 1, 1)
            ds = _bf(pr * (dpr - jnp.sum(pr * dpr, axis=-1, keepdims=True)) * (X_HEAD_DIM ** -0.5))
            dq_ref[:, sl] = _bf(_dot(ds, kb, 1, 0))
            dkv_ref[:, sl] += _dot(ds, qb, 0, 0)
            dkv_ref[:, slv] += _dot(_bf(pr), dob, 0, 0)

    return pl.pallas_call(
        body, name="xattn_bwd", grid=(t // tq,),
        in_specs=[pl.BlockSpec((tq, d), lambda i: (i, 0)), _full((nm, 2 * d)), pl.BlockSpec((tq, d), lambda i: (i, 0))],
        out_specs=[pl.BlockSpec((tq, d), lambda i: (i, 0)), _full((nm, 2 * d))],
        out_shape=[jax.ShapeDtypeStruct((t, d), BF16), jax.ShapeDtypeStruct((nm, 2 * d), F32)],
        compiler_params=_params(),
    )(q, kv, do)


CONV_TN = 256


def _shift_rows(u, row, t, delta):
    if delta < 0:
        return jnp.where(row == 0, 0.0, pltpu.roll(u, 1, 0))
    return jnp.where(row == t - 1, 0.0, pltpu.roll(u, t - 1, 0))


def _conv_gate_fwd(u, conv_w, conv_b):
    t = u.shape[0]
    nt = D_FF // CONV_TN

    def body(ug_ref, uv_ref, wg_ref, wv_ref, bg_ref, bv_ref, a_ref):
        row = lax.broadcasted_iota(jnp.int32, (t, CONV_TN), 0)

        def conv(u_ref, w_ref, b_ref):
            uu = u_ref[...]
            return (b_ref[...] + _shift_rows(uu, row, t, -1) * w_ref[0:1, :] + uu * w_ref[1:2, :]
                    + _shift_rows(uu, row, t, 1) * w_ref[2:3, :])

        gate = conv(ug_ref, wg_ref, bg_ref)
        a_ref[...] = _bf(gate * _sigmoid(gate) * conv(uv_ref, wv_ref, bv_ref))

    col = lambda rows, off: pl.BlockSpec((rows, CONV_TN), lambda j: (0, j + off))
    return pl.pallas_call(
        body, name="conv_gate_fwd", grid=(nt,),
        in_specs=[col(t, 0), col(t, nt), col(3, 0), col(3, nt), col(1, 0), col(1, nt)],
        out_specs=col(t, 0),
        out_shape=jax.ShapeDtypeStruct((t, D_FF), BF16),
        compiler_params=_params(),
    )(u, u, conv_w, conv_w, conv_b, conv_b)


def _conv_gate_bwd(u, conv_w, conv_b, da, after=()):
    t = u.shape[0]
    nt = D_FF // CONV_TN

    def body(ug_ref, uv_ref, wg_ref, wv_ref, bg_ref, bv_ref, da_ref, dug_ref, dwg_ref, dbg_ref, duv_ref, dwv_ref, dbv_ref):
        row = lax.broadcasted_iota(jnp.int32, (t, CONV_TN), 0)
        ug, uv = ug_ref[...], uv_ref[...]
        ug_m, ug_p = _shift_rows(ug, row, t, -1), _shift_rows(ug, row, t, 1)
        uv_m, uv_p = _shift_rows(uv, row, t, -1), _shift_rows(uv, row, t, 1)
        gate = bg_ref[...] + ug_m * wg_ref[0:1, :] + ug * wg_ref[1:2, :] + ug_p * wg_ref[2:3, :]
        val = bv_ref[...] + uv_m * wv_ref[0:1, :] + uv * wv_ref[1:2, :] + uv_p * wv_ref[2:3, :]
        sg = _sigmoid(gate)
        da_ = da_ref[...].astype(F32)

        def emit(dacc, um, uu, up, w_ref, du_ref, dw_ref, db_ref):
            du_ref[...] = _bf(_shift_rows(dacc, row, t, 1) * w_ref[0:1, :] + dacc * w_ref[1:2, :]
                              + _shift_rows(dacc, row, t, -1) * w_ref[2:3, :])
            dw_ref[0:1, :] = jnp.sum(dacc * um, axis=0, keepdims=True)
            dw_ref[1:2, :] = jnp.sum(dacc * uu, axis=0, keepdims=True)
            dw_ref[2:3, :] = jnp.sum(dacc * up, axis=0, keepdims=True)
            db_ref[...] = jnp.sum(dacc, axis=0, keepdims=True)

        emit(da_ * val * (sg * (1.0 + gate * (1.0 - sg))), ug_m, ug, ug_p, wg_ref, dug_ref, dwg_ref, dbg_ref)
        emit(da_ * gate * sg, uv_m, uv, uv_p, wv_ref, duv_ref, dwv_ref, dbv_ref)

    col = lambda rows, off: pl.BlockSpec((rows, CONV_TN), lambda j: (0, j + off))
    half_shapes = [jax.ShapeDtypeStruct((t, D_FF), BF16), jax.ShapeDtypeStruct((3, D_FF), F32),
                   jax.ShapeDtypeStruct((1, D_FF), F32)]
    outs = _call(
        body, (u, u, conv_w, conv_w, conv_b, conv_b, da), name="conv_gate_bwd", grid=(nt,),
        in_specs=[col(t, 0), col(t, nt), col(3, 0), col(3, nt), col(1, 0), col(1, nt), col(t, 0)],
        out_specs=[col(t, 0), col(3, 0), col(1, 0)] * 2, out_shape=half_shapes * 2, after=after)
    return outs[:3], outs[3:]


def _row_tile(r, cap):
    best = r
    for cand in range(16, cap + 1, 16):
        if r % cand == 0:
            best = cand
    return best


def _sum_parts(own, got, me, name, tr=256):
    _, r, c = got.shape
    tr = _row_tile(r, tr)

    def body(me_ref, own_ref, got_ref, o_ref):
        mine = own_ref[...].astype(F32)
        acc = None
        for i in range(N_DEV):
            term = jnp.where(me_ref[0] == i, mine, got_ref[i].astype(F32))
            acc = term if acc is None else acc + term
        o_ref[...] = acc

    return pl.pallas_call(
        body, name=name,
        grid_spec=pltpu.PrefetchScalarGridSpec(
            num_scalar_prefetch=1, grid=(r // tr,),
            in_specs=[pl.BlockSpec((None, tr, c), lambda i, me_ref: (me_ref[0], i, 0)),
                      pl.BlockSpec((N_DEV, tr, c), lambda i, me_ref: (0, i, 0))],
            out_specs=pl.BlockSpec((tr, c), lambda i, me_ref: (i, 0))),
        out_shape=jax.ShapeDtypeStruct((r, c), F32),
        compiler_params=_params(),
    )(me, own, got)


def _adamw_math(w_, g_, m_, v_):
    m_ = ADAM_B1 * m_ + (1.0 - ADAM_B1) * g_
    v_ = ADAM_B2 * v_ + (1.0 - ADAM_B2) * (g_ * g_)
    m_hat = m_ / (1.0 - ADAM_B1 ** ADAM_STEP)
    v_hat = v_ / (1.0 - ADAM_B2 ** ADAM_STEP)
    return -ADAM_LR * (m_hat / (jnp.sqrt(v_hat) + ADAM_EPS) + ADAM_WD * w_), m_, v_


def _sum_adamw(own, got, me, w, m, v, name, tr=256):
    _, r, c = got.shape
    tr = _row_tile(r, tr)

    def body(me_ref, own_ref, got_ref, w_ref, m_ref, v_ref, g_ref, d_ref, nm_ref, nv_ref):
        mine = own_ref[...].astype(F32)
        acc = None
        for i in range(N_DEV):
            term = jnp.where(me_ref[0] == i, mine, got_ref[i].astype(F32))
            acc = term if acc is None else acc + term
        g_ref[...] = acc
        d_ref[...], nm_ref[...], nv_ref[...] = _adamw_math(w_ref[...], acc, m_ref[...], v_ref[...])

    blk = pl.BlockSpec((tr, c), lambda i, me_ref: (i, 0))
    return pl.pallas_call(
        body, name=name,
        grid_spec=pltpu.PrefetchScalarGridSpec(
            num_scalar_prefetch=1, grid=(r // tr,),
            in_specs=[pl.BlockSpec((None, tr, c), lambda i, me_ref: (me_ref[0], i, 0)),
                      pl.BlockSpec((N_DEV, tr, c), lambda i, me_ref: (0, i, 0)), blk, blk, blk],
            out_specs=[blk] * 4),
        out_shape=[jax.ShapeDtypeStruct((r, c), F32)] * 4,
        compiler_params=_params(),
    )(me, own, got, w, m, v)


def _adamw(w, g, m, v, name, tr=256):
    r, c = w.shape
    tr = _row_tile(r, tr)

    def body(w_ref, g_ref, m_ref, v_ref, d_ref, nm_ref, nv_ref):
        d_ref[...], nm_ref[...], nv_ref[...] = _adamw_math(w_ref[...], g_ref[...], m_ref[...], v_ref[...])

    blk = pl.BlockSpec((tr, c), lambda i: (i, 0))
    return pl.pallas_call(
        body, name=name, grid=(r // tr,),
        in_specs=[blk] * 4, out_specs=[blk] * 3,
        out_shape=[jax.ShapeDtypeStruct((r, c), F32)] * 3,
        compiler_params=_params(),
    )(w, g, m, v)


def _mesh_pos():
    return lax.axis_index("x"), lax.axis_index("y"), lax.axis_index("c")


def _dev_index(px, py, pc):
    return 4 * px + 2 * py + pc


class _Gather:
    def __init__(self, arrs):
        self.arrs = list(arrs)
        n = len(self.arrs)
        self.out_shape = [jax.ShapeDtypeStruct((N_DEV,) + a.shape, a.dtype) for a in self.arrs]
        self.sems = [pltpu.SemaphoreType.DMA((7, n)), pltpu.SemaphoreType.DMA((7, n)), pltpu.SemaphoreType.DMA((n,))]

    def _ctx(self, ins, outs, sems):
        send_sems, recv_sems, local_sems = sems
        x, y, c = _mesh_pos()
        chips = [(1 - x, y), (x, 1 - y), (1 - x, 1 - y)]

        def copy(k, a, block, to, src=None):
            dst = outs[a].at[_dev_index(*block)]
            return pltpu.make_async_remote_copy(
                src_ref=dst if src is None else src, dst_ref=dst, send_sem=send_sems.at[k, a], recv_sem=recv_sems.at[k, a],
                device_id=to, device_id_type=MESH_T)

        n = len(ins)
        me, sibling = (x, y, c), (x, y, 1 - c)
        mine = [pltpu.make_async_copy(ins[a], outs[a].at[_dev_index(*me)], local_sems.at[a]) for a in range(n)]
        first = [copy(0, a, me, sibling, src=ins[a]) for a in range(n)]
        first += [copy(1 + j, a, me, (*chip, c), src=ins[a]) for j, chip in enumerate(chips) for a in range(n)]
        passed = [copy(4 + j, a, (*chip, c), sibling) for j, chip in enumerate(chips) for a in range(n)]
        return n, c, me, sibling, chips, copy, mine, first, passed

    def start(self, ins, outs, sems):
        _, _, _, _, _, _, mine, first, _ = self._ctx(ins, outs, sems)
        for cp in mine + first:
            cp.start()

    def forward(self, ins, outs, sems):
        n, c, me, _, chips, copy, _, _, passed = self._ctx(ins, outs, sems)
        for j, chip in enumerate(chips):
            for a in range(n):
                copy(1 + j, a, (*chip, c), me).wait_recv()
                passed[j * n + a].start()

    def finish(self, ins, outs, sems):
        n, c, me, sibling, chips, copy, mine, first, passed = self._ctx(ins, outs, sems)
        for a in range(n):
            copy(0, a, sibling, me).wait_recv()
        for j, chip in enumerate(chips):
            for a in range(n):
                copy(4 + j, a, (*chip, 1 - c), me).wait_recv()
        for cp in first + passed:
            cp.wait_send()
        for cp in mine:
            cp.wait()


def _comm_alone(comm, name):
    n = len(comm.arrs)

    def body(*refs):
        ins, outs, sems = refs[:n], refs[n:2 * n], refs[2 * n:]
        comm.start(ins, outs, sems)
        if comm.forward is not None:
            comm.forward(ins, outs, sems)
        comm.finish(ins, outs, sems)

    any_spec = pl.BlockSpec(memory_space=pl.ANY)
    return pl.pallas_call(body, name=name, in_specs=[any_spec] * n, out_specs=[any_spec] * n, out_shape=comm.out_shape,
                          scratch_shapes=comm.sems)(*comm.arrs)


def _peers(x, y, c):
    out = []
    for k in range(1, N_DEV):
        pos = (1 - x if k & 4 else x, 1 - y if k & 2 else y, 1 - c if k & 1 else c)
        out.append((k, pos, _dev_index(*pos)))
    return out


def _exchange_begin(arrs, tag):
    n = len(arrs)
    lands = [lax.empty(a.shape, a.dtype) for a in arrs]

    def start_body(*refs):
        ins, land = refs[:n], refs[n:2 * n]
        send_sems, recv_sems, token = refs[2 * n], refs[2 * n + 1], refs[-1]
        x, y, c = _mesh_pos()
        me_i = _dev_index(x, y, c)
        for k, pos, peer_i in _peers(x, y, c):
            for a in range(n):
                pltpu.make_async_remote_copy(
                    src_ref=ins[a].at[peer_i], dst_ref=land[a].at[me_i], send_sem=send_sems.at[(k - 1) * n + a],
                    recv_sem=recv_sems.at[(k - 1) * n + a], device_id=pos, device_id_type=MESH_T).start()
        token[...] = jnp.zeros_like(token)

    hbm = pl.BlockSpec(memory_space=pltpu.HBM)
    sem = pl.BlockSpec(memory_space=pltpu.SEMAPHORE)
    thru = [pltpu.HBM(a.shape, a.dtype) for a in arrs]
    outs = pl.pallas_call(
        start_body, name="exchange_start_" + tag,
        out_shape=[pltpu.SemaphoreType.DMA((7 * n,)), pltpu.SemaphoreType.DMA((7 * n,))] + thru + thru
        + [jax.ShapeDtypeStruct((8, LANE), F32)],
        in_specs=[hbm] * (2 * n), out_specs=[sem, sem] + [hbm] * (2 * n) + [pl.BlockSpec(memory_space=pltpu.VMEM)],
        input_output_aliases={i: 2 + i for i in range(2 * n)},
        compiler_params=pltpu.CompilerParams(has_side_effects=pltpu.SideEffectType.DATAFLOW_SIDE_EFFECTING),
    )(*[pltpu.with_memory_space_constraint(a, pltpu.HBM) for a in list(arrs) + list(lands)])
    return (tag, n, outs[0], outs[1], outs[2:2 + n], outs[2 + n:2 + 2 * n]), outs[-1]


def _exchange_end(handle, after):
    tag, n, send_sems, recv_sems, srcs, lands = handle

    def body(*refs):
        ins, land = refs[:n], refs[n:2 * n]
        send_sems_, recv_sems_ = refs[2 * n], refs[2 * n + 1]
        x, y, c = _mesh_pos()
        me_i = _dev_index(x, y, c)
        for k, pos, peer_i in _peers(x, y, c):
            for a in range(n):
                cp = pltpu.make_async_remote_copy(
                    src_ref=ins[a].at[peer_i], dst_ref=land[a].at[me_i], send_sem=send_sems_.at[(k - 1) * n + a],
                    recv_sem=recv_sems_.at[(k - 1) * n + a], device_id=pos, device_id_type=MESH_T)
                cp.wait_send()
                cp.wait_recv()

    hbm = pl.BlockSpec(memory_space=pltpu.HBM)
    sem = pl.BlockSpec(memory_space=pltpu.SEMAPHORE)
    outs = pl.pallas_call(
        body, name="exchange_end_" + tag, out_shape=[pltpu.HBM(a.shape, a.dtype) for a in list(srcs) + list(lands)],
        in_specs=[hbm] * (2 * n) + [sem, sem, pl.BlockSpec(memory_space=pl.ANY)], out_specs=[hbm] * (2 * n),
        input_output_aliases={i: i for i in range(2 * n)},
        compiler_params=pltpu.CompilerParams(has_side_effects=pltpu.SideEffectType.DATAFLOW_SIDE_EFFECTING),
    )(*srcs, *lands, send_sems, recv_sems, after)
    return list(zip(outs[:n], outs[n:]))


def _gather_begin(arrs, tag):
    n = len(arrs)
    me = _dev_index(*_mesh_pos())
    lands = [lax.dynamic_update_slice(lax.empty((N_DEV,) + a.shape, a.dtype), a[None], (me,) + (0,) * a.ndim) for a in arrs]

    def start_body(*refs):
        ins, land = refs[:n], refs[n:2 * n]
        send_sems, recv_sems, token = refs[2 * n], refs[2 * n + 1], refs[-1]
        x, y, c = _mesh_pos()
        me_i = _dev_index(x, y, c)
        for a in range(n):
            for k, pos, _ in _peers(x, y, c):
                pltpu.make_async_remote_copy(
                    src_ref=ins[a], dst_ref=land[a].at[me_i], send_sem=send_sems.at[(k - 1) * n + a],
                    recv_sem=recv_sems.at[(k - 1) * n + a], device_id=pos, device_id_type=MESH_T).start()
        token[...] = jnp.zeros_like(token)

    hbm = pl.BlockSpec(memory_space=pltpu.HBM)
    sem = pl.BlockSpec(memory_space=pltpu.SEMAPHORE)
    thru = [pltpu.HBM(a.shape, a.dtype) for a in list(arrs) + lands]
    outs = pl.pallas_call(
        start_body, name="gather_start_" + tag,
        out_shape=[pltpu.SemaphoreType.DMA((7 * n,)), pltpu.SemaphoreType.DMA((7 * n,))] + thru
        + [jax.ShapeDtypeStruct((8, LANE), F32)],
        in_specs=[hbm] * (2 * n), out_specs=[sem, sem] + [hbm] * (2 * n) + [pl.BlockSpec(memory_space=pltpu.VMEM)],
        input_output_aliases={i: 2 + i for i in range(2 * n)},
        compiler_params=pltpu.CompilerParams(has_side_effects=pltpu.SideEffectType.DATAFLOW_SIDE_EFFECTING),
    )(*[pltpu.with_memory_space_constraint(a, pltpu.HBM) for a in list(arrs) + lands])
    return (tag, n, outs[0], outs[1], outs[2:2 + n], outs[2 + n:2 + 2 * n]), outs[-1]


def _gather_end(handle, which, after):
    tag, n, send_sems, recv_sems, srcs, lands = handle
    m = len(which)

    def body(*refs):
        ins, land = refs[:m], refs[m:2 * m]
        send_sems_, recv_sems_ = refs[2 * m], refs[2 * m + 1]
        x, y, c = _mesh_pos()
        me_i = _dev_index(x, y, c)
        for j, a in enumerate(which):
            for k, pos, _ in _peers(x, y, c):
                cp = pltpu.make_async_remote_copy(
                    src_ref=ins[j], dst_ref=land[j].at[me_i], send_sem=send_sems_.at[(k - 1) * n + a],
                    recv_sem=recv_sems_.at[(k - 1) * n + a], device_id=pos, device_id_type=MESH_T)
                cp.wait_send()
                cp.wait_recv()

    hbm = pl.BlockSpec(memory_space=pltpu.HBM)
    sem = pl.BlockSpec(memory_space=pltpu.SEMAPHORE)
    ops = [srcs[a] for a in which] + [lands[a] for a in which]
    outs = pl.pallas_call(
        body, name="gather_end_%s_%s" % (tag, "_".join(str(a) for a in which)),
        out_shape=[pltpu.HBM(a.shape, a.dtype) for a in ops],
        in_specs=[hbm] * (2 * m) + [sem, sem, pl.BlockSpec(memory_space=pl.ANY)], out_specs=[hbm] * (2 * m),
        input_output_aliases={i: i for i in range(2 * m)},
        compiler_params=pltpu.CompilerParams(has_side_effects=pltpu.SideEffectType.DATAFLOW_SIDE_EFFECTING),
    )(*ops, send_sems, recv_sems, after)
    return list(outs[m:])


_SPLIT = dict(has_side_effects=pltpu.SideEffectType.DATAFLOW_SIDE_EFFECTING)


def _chips(x, y):
    return [(1 - x, y), (x, 1 - y), (1 - x, 1 - y)]


def _gather2_begin(arrs, tag):
    n = len(arrs)
    me = _dev_index(*_mesh_pos())
    lands = [lax.dynamic_update_slice(lax.empty((N_DEV,) + a.shape, a.dtype), a[None], (me,) + (0,) * a.ndim) for a in arrs]

    def body(*refs):
        ins, land = refs[:n], refs[n:2 * n]
        send1, recv1, token = refs[2 * n], refs[2 * n + 1], refs[-1]
        x, y, c = _mesh_pos()
        me_i = _dev_index(x, y, c)
        targets = [(x, y, 1 - c)] + [(*chip, c) for chip in _chips(x, y)]
        for a in range(n):
            for k, pos in enumerate(targets):
                pltpu.make_async_remote_copy(
                    src_ref=ins[a], dst_ref=land[a].at[me_i], send_sem=send1.at[k * n + a], recv_sem=recv1.at[k * n + a],
                    device_id=pos, device_id_type=MESH_T).start()
        token[...] = jnp.zeros_like(token)

    hbm = pl.BlockSpec(memory_space=pltpu.HBM)
    sem = pl.BlockSpec(memory_space=pltpu.SEMAPHORE)
    thru = [pltpu.HBM(a.shape, a.dtype) for a in list(arrs) + lands]
    outs = pl.pallas_call(
        body, name="gather_start_" + tag,
        out_shape=[pltpu.SemaphoreType.DMA((4 * n,)), pltpu.SemaphoreType.DMA((4 * n,))] + thru
        + [jax.ShapeDtypeStruct((8, LANE), F32)],
        in_specs=[hbm] * (2 * n), out_specs=[sem, sem] + [hbm] * (2 * n) + [pl.BlockSpec(memory_space=pltpu.VMEM)],
        input_output_aliases={i: 2 + i for i in range(2 * n)}, compiler_params=pltpu.CompilerParams(**_SPLIT),
    )(*[pltpu.with_memory_space_constraint(a, pltpu.HBM) for a in list(arrs) + lands])
    return dict(tag=tag, n=n, send1=outs[0], recv1=outs[1], srcs=list(outs[2:2 + n]), lands=list(outs[2 + n:2 + 2 * n])), outs[-1]


def _gather2_forward(handle, which, after, part):
    n, m = handle["n"], len(which)

    def body(*refs):
        land, recv1 = refs[:m], refs[m]
        send2, recv2, token = refs[m + 2], refs[m + 3], refs[-1]
        token[...] = jnp.zeros_like(token)
        x, y, c = _mesh_pos()
        for j, a in enumerate(which):
            for k, chip in enumerate(_chips(x, y)):
                blk = land[j].at[_dev_index(*chip, c)]
                copy = pltpu.make_async_remote_copy(
                    src_ref=blk, dst_ref=blk, send_sem=send2.at[k * m + j], recv_sem=recv2.at[k * m + j],
                    device_id=(x, y, 1 - c), device_id_type=MESH_T)
                pltpu.make_async_remote_copy(
                    src_ref=blk, dst_ref=blk, send_sem=send2.at[k * m + j], recv_sem=recv1.at[(1 + k) * n + a],
                    device_id=(*chip, c), device_id_type=MESH_T).wait_recv()
                copy.start()

    hbm = pl.BlockSpec(memory_space=pltpu.HBM)
    sem = pl.BlockSpec(memory_space=pltpu.SEMAPHORE)
    lands = [handle["lands"][a] for a in which]
    outs = pl.pallas_call(
        body, name="gather_forward_%s_%s" % (handle["tag"], part),
        out_shape=[pltpu.SemaphoreType.DMA((3 * m,)), pltpu.SemaphoreType.DMA((3 * m,))] + [pltpu.HBM(a.shape, a.dtype) for a in lands]
        + [jax.ShapeDtypeStruct((8, LANE), F32)],
        in_specs=[hbm] * m + [sem, pl.BlockSpec(memory_space=pl.ANY)],
        out_specs=[sem, sem] + [hbm] * m + [pl.BlockSpec(memory_space=pltpu.VMEM)],
        input_output_aliases={i: 2 + i for i in range(m)}, compiler_params=pltpu.CompilerParams(**_SPLIT),
    )(*lands, handle["recv1"], after)
    for j, a in enumerate(which):
        handle["lands"][a] = outs[2 + j]
        handle[("leg2", a)] = (outs[0], outs[1], j, m)
    return outs[-1]


def _gather2_end(handle, a, after):
    n = handle["n"]
    send2, recv2, j, m = handle[("leg2", a)]

    def body(src, land, send1, recv1, send2_, recv2_, after_ref, src_out, land_out):
        x, y, c = _mesh_pos()
        me_i = _dev_index(x, y, c)
        sibling = (x, y, 1 - c)
        for k, pos in enumerate([sibling] + [(*chip, c) for chip in _chips(x, y)]):
            first = pltpu.make_async_remote_copy(
                src_ref=src, dst_ref=land.at[me_i], send_sem=send1.at[k * n + a], recv_sem=recv1.at[k * n + a],
                device_id=pos, device_id_type=MESH_T)
            first.wait_send()
            if k == 0:
                first.wait_recv()
        for k in range(3):
            second = pltpu.make_async_remote_copy(
                src_ref=land.at[me_i], dst_ref=land.at[me_i], send_sem=send2_.at[k * m + j], recv_sem=recv2_.at[k * m + j],
                device_id=sibling, device_id_type=MESH_T)
            second.wait_send()
            second.wait_recv()

    hbm = pl.BlockSpec(memory_space=pltpu.HBM)
    sem = pl.BlockSpec(memory_space=pltpu.SEMAPHORE)
    src, land = handle["srcs"][a], handle["lands"][a]
    outs = pl.pallas_call(
        body, name="gather_end_%s_%d" % (handle["tag"], a),
        out_shape=[pltpu.HBM(src.shape, src.dtype), pltpu.HBM(land.shape, land.dtype)],
        in_specs=[hbm, hbm, sem, sem, sem, sem, pl.BlockSpec(memory_space=pl.ANY)], out_specs=[hbm, hbm],
        input_output_aliases={0: 0, 1: 1}, compiler_params=pltpu.CompilerParams(**_SPLIT),
    )(src, land, handle["send1"], handle["recv1"], send2, recv2, after)
    return outs[1]


def _call(body, operands, *, name, grid, in_specs, out_specs, out_shape, scratch_shapes=(), after=()):
    ni, nd = len(in_specs), len(after)

    def ordered(*refs):
        body(*refs[:ni], *refs[ni + nd:])

    outs = pl.pallas_call(
        ordered, name=name, grid=grid, in_specs=list(in_specs) + [pl.BlockSpec(memory_space=pl.ANY)] * nd,
        out_specs=out_specs, out_shape=out_shape, scratch_shapes=list(scratch_shapes), compiler_params=_params(),
    )(*operands, *after)
    return list(outs)


def _rows(a):
    return a.reshape(-1, a.shape[-1])


def _slots(a):
    return a.reshape(N_DEV, -1, a.shape[-1])


def _local_step(x, mem, tgt, w, wire):
    tables = _rope_tables(x.shape[0])
    bd = _head_sum_matrix()
    two = lambda g: jnp.tile(g, (1, 2))
    gq2, gk2 = two(w["q_norm_g"]), two(w["k_norm_g"])

    w_in_t, hg_lb, conv_w = _comm_alone(_Gather([wire["w_in"], w["hg_lb"].reshape(4, -1), w["conv_w"][0]]), "gather_w_in")
    w_in_t = _rows(w_in_t)
    hg_lb = jnp.transpose(hg_lb.reshape(N_DEV, 2, 2, -1), (1, 2, 0, 3)).reshape(2, 2, HG_DIM)
    conv_w = jnp.transpose(conv_w, (1, 0, 2)).reshape(3, 2 * D_FF)
    lb_a0, lb_a1 = hg_lb[:, 0, :], hg_lb[:, 1, :]

    order = ("w_out", "w_xq", "w_xkv", "w_xo", "w_up", "w_down")
    w_in_t, *later = lax.optimization_barrier((w_in_t, *[wire[n] for n in order]))
    fetch, started = _gather2_begin(later, "w")
    take = lambda n, after: _rows(_gather2_end(fetch, order.index(n), after))

    h1, p = _norm_mm(x, w["pre_mix_g"], w_in_t, True, N_IN, "in_proj", after=[started])
    qr, kr = _qk_prep(p, gq2, gk2, tables, bd)
    att, att32, lse = _attn_fwd(qr, kr, p)
    passed = _gather2_forward(fetch, [0, 1, 2, 3], att, "a")
    lb = _lower_bounds(lb_a0, lb_a1)
    of, s_f = _hgrn_fwd(p, lb[0:1], False, after=[passed])
    ob, s_b = _hgrn_fwd(p, lb[1:2], True)
    passed = _gather2_forward(fetch, [4, 5], ob, "b")
    w_out = take("w_out", ob)
    cat, mixed, x1 = _mix_out(att, of, ob, p, w["hg_out_norm_g"], w_out, w["post_mix_g"], x, after=[passed])

    w_xq = take("w_xq", x1)
    h2, q2 = _norm_mm(x1, w["pre_x_g"], w_xq, False, 1024, "xq_proj")
    w_xkv_t = take("w_xkv", q2)
    mn, kv = _norm_mm(mem, w["mem_norm_g"], w_xkv_t, True, 2 * D_MODEL, "xkv_proj")
    o2 = _xattn_fwd(q2, kv)
    w_xo = take("w_xo", o2)
    y2, x2 = _mm_postnorm_res(o2, w_xo, w["post_x_g"], x1, "xo_proj")

    w_up_t = take("w_up", x2)
    h3, u = _norm_mm(x2, w["pre_ffn_g"], w_up_t, True, 2 * D_FF, "up_proj", tm=256)
    a = _conv_gate_fwd(u, conv_w, w["conv_b"])
    w_down = take("w_down", a)
    y3, dx3, loss = _mm_postnorm_res_loss(a, w_down, w["post_ffn_g"], x2, tgt, "down_proj")

    g, pending = {}, {}
    dy3, da, g["post_ffn_g"] = _postnorm_bwd_mm(dx3, y3, w["post_ffn_g"], w_down, "down_bwd", BF16)
    pending["down"], started = _exchange_begin([_slots(_dw(a, dy3, "dw_down"))], "down")
    (du_g, dcw_g, dcb_g), (du_v, dcw_v, dcb_v) = _conv_gate_bwd(u, conv_w, w["conv_b"], da, after=[started])
    g["conv_w"] = jnp.concatenate([dcw_g, dcw_v], axis=1)
    g["conv_b"] = jnp.concatenate([dcb_g, dcb_v], axis=1)
    pending["up"], started = _exchange_begin([_slots(_dw([du_g, du_v], h3, "dw_up"))], "up")
    dx2, g["pre_ffn_g"] = _mm_prenorm_bwd([du_g, du_v], w_up_t, True, x2, w["pre_ffn_g"], dx3, "up_bwd", after=[started], tm=256)

    dy2, do2, g["post_x_g"] = _postnorm_bwd_mm(dx2, y2, w["post_x_g"], w_xo, "xo_bwd", BF16)
    dw_xo = _dw(o2, dy2, "dw_xo")
    dq2, dkv = _xattn_bwd(q2, kv, do2)
    dw_xq = _dw(h2, dq2, "dw_xq")
    dkvb = _bf(dkv)
    dw_xkv = _dw(dkvb, mn, "dw_xkv")
    pending["x"], started = _exchange_begin([_slots(dw_xo), _slots(dw_xq), _slots(dw_xkv)], "x")
    _, g["mem_norm_g"] = _mm_prenorm_bwd(dkvb, w_xkv_t, True, mem, w["mem_norm_g"], jnp.zeros_like(mem), "xkv_bwd")
    dx1, g["pre_x_g"] = _mm_prenorm_bwd(dq2, w_xq, False, x1, w["pre_x_g"], dx2, "xq_bwd", after=[started])

    dmixed, dcat, g["post_mix_g"] = _postnorm_bwd_mm(dx1, mixed, w["post_mix_g"], w_out, "out_bwd")
    pending["out"], started = _exchange_begin([_slots(_dw(cat, dmixed, "dw_out"))], "out")
    do, dhg, g["hg_out_norm_g"] = _rec_bwd(dcat, of, ob, p, w["hg_out_norm_g"])
    dhq_f, dz_f, dhi_f, dlb_f = _hgrn_bwd(p, lb[0:1], do, s_f, False, after=[started])
    dhq_b, dz_b, dhi_b, dlb_b = _hgrn_bwd(p, lb[1:2], do, s_b, True)
    d_a0, d_a1 = _lower_bounds_bwd(lb_a0, lb_a1, jnp.concatenate([dlb_f, dlb_b], axis=0))
    g["hg_lb"] = jnp.stack([d_a0, d_a1], axis=1)
    dqr, dkr, dv = _attn_bwd(qr, kr, p, dcat, att32, lse)
    dp_qk, dgq, dgk = _qk_prep_bwd(p, dqr, dkr, gq2, gk2, tables, bd)
    g["q_norm_g"], g["k_norm_g"] = dgq, dgk
    dp = _assemble_dp(dp_qk, dv, dhq_f, dhq_b, dz_f, dz_b, dhi_f, dhi_b, dhg)
    pending["in"], started = _exchange_begin([_slots(_dw(dp, h1, "dw_in"))], "in")
    dx, g["pre_mix_g"] = _mm_prenorm_bwd(dp, w_in_t, True, x, w["pre_mix_g"], dx1, "in_bwd", after=[started])
    return loss, dx, g, pending


_COL_SHARDED = ("w_in", "w_xkv", "w_up")
_ROW_SHARDED = ("w_out", "w_xq", "w_xo", "w_down")
_REPLICATED = ("pre_mix_g", "q_norm_g", "k_norm_g", "hg_out_norm_g", "post_mix_g", "pre_x_g", "mem_norm_g", "post_x_g",
               "pre_ffn_g", "conv_b", "post_ffn_g")
_WEIGHTS = ("pre_mix_g", "w_in", "q_norm_g", "k_norm_g", "hg_lb", "hg_out_norm_g", "w_out", "post_mix_g", "pre_x_g",
            "mem_norm_g", "w_xq", "w_xkv", "w_xo", "post_x_g", "pre_ffn_g", "w_up", "conv_w", "conv_b", "w_down",
            "post_ffn_g")
_ADAM_TRANSPOSED = ("w_in", "w_up")
PACK_W = 1024


def _small_plan(shapes):
    plan, r = [], 0
    for vi, (rows, cols) in enumerate(shapes):
        for i in range(rows):
            for c0 in range(0, cols, PACK_W):
                plan.append((vi, i, c0, min(PACK_W, cols - c0), r))
                r += 1
    return plan, -(-r // 8) * 8


def _pack_small(vals):
    plan, nrows = _small_plan([val.shape for val in vals])

    def body(*refs):
        ins, out = refs[:-1], refs[-1]
        out[...] = jnp.zeros_like(out)
        for vi, i, c0, width, r in plan:
            out[r:r + 1, 0:width] = ins[vi][i:i + 1, c0:c0 + width]

    return pl.pallas_call(body, name="pack_small", out_shape=jax.ShapeDtypeStruct((nrows, PACK_W), F32))(*vals)


def _sum_unpack_small(packs, shapes):
    plan, _ = _small_plan(shapes)

    def body(*refs):
        p_ref, outs = refs[0], refs[1:]
        acc = p_ref[0]
        for i in range(1, N_DEV):
            acc = acc + p_ref[i]
        for vi, i, c0, width, r in plan:
            outs[vi][i:i + 1, c0:c0 + width] = acc[r:r + 1, 0:width]

    return pl.pallas_call(body, name="sum_unpack_small", out_shape=[jax.ShapeDtypeStruct(s, F32) for s in shapes])(packs)


def _adamw_many(ws, gs, ms, vs):
    n = len(ws)

    def body(*refs):
        w_refs, g_refs, m_refs, v_refs = (refs[k * n:(k + 1) * n] for k in range(4))
        d_refs, nm_refs, nv_refs = (refs[(4 + k) * n:(5 + k) * n] for k in range(3))
        for k in range(n):
            g_ = g_refs[k][...]
            m_ = ADAM_B1 * m_refs[k][...] + (1.0 - ADAM_B1) * g_
            v_ = ADAM_B2 * v_refs[k][...] + (1.0 - ADAM_B2) * (g_ * g_)
            m_hat = m_ / (1.0 - ADAM_B1 ** ADAM_STEP)
            v_hat = v_ / (1.0 - ADAM_B2 ** ADAM_STEP)
            d_refs[k][...] = -ADAM_LR * (m_hat / (jnp.sqrt(v_hat) + ADAM_EPS) + ADAM_WD * w_refs[k][...])
            nm_refs[k][...] = m_
            nv_refs[k][...] = v_

    shapes = [jax.ShapeDtypeStruct(a.shape, F32) for a in ws]
    outs = pl.pallas_call(body, name="adamw_small", out_shape=shapes * 3)(*ws, *gs, *ms, *vs)
    return outs[:n], outs[n:2 * n], outs[2 * n:]


def kernel(x, mem, pre_mix_g, w_in, q_norm_g, k_norm_g, hg_lb, hg_out_norm_g, w_out, post_mix_g, pre_x_g, mem_norm_g, w_xq, w_xkv, w_xo, post_x_g, pre_ffn_g, w_up, conv_w, conv_b, w_down, post_ffn_g, loss_target, m_pre_mix_g, m_w_in, m_q_norm_g, m_k_norm_g, m_hg_lb, m_hg_out_norm_g, m_w_out, m_post_mix_g, m_pre_x_g, m_mem_norm_g, m_w_xq, m_w_xkv, m_w_xo, m_post_x_g, m_pre_ffn_g, m_w_up, m_conv_w, m_conv_b, m_w_down, m_post_ffn_g, v_pre_mix_g, v_w_in, v_q_norm_g, v_k_norm_g, v_hg_lb, v_hg_out_norm_g, v_w_out, v_post_mix_g, v_pre_x_g, v_mem_norm_g, v_w_xq, v_w_xkv, v_w_xo, v_post_x_g, v_pre_ffn_g, v_w_up, v_conv_w, v_conv_b, v_w_down, v_post_ffn_g):
    args = dict(locals())
    w = {n: args[n] for n in _WEIGHTS}
    m = {n: args["m_" + n] for n in _WEIGHTS}
    v = {n: args["v_" + n] for n in _WEIGHTS}
    me = _dev_index(*_mesh_pos())

    wire = {n: _bf(w[n][0].T) for n in _COL_SHARDED}
    wire.update({n: _bf(w[n][0]) for n in _ROW_SHARDED})

    loss, grad_x, g, pending = _local_step(x[0], mem[0], loss_target[0], w, wire)

    grads, delta, new_m, new_v = {}, {}, {}, {}

    me_arr = jnp.reshape(me, (1,)).astype(jnp.int32)

    def update(n, parts):
        if n in _ADAM_TRANSPOSED:
            outs = _sum_adamw(*parts, me_arr, w[n][0].T, m[n][0].T, v[n][0].T, "update_" + n)
            grads[n], delta[n], new_m[n], new_v[n] = (a.T[None] for a in outs)
        elif n in _COL_SHARDED:
            gsum = _sum_parts(*parts, me_arr, "sum_" + n).T
            grads[n] = gsum[None]
            delta[n], new_m[n], new_v[n] = (a[None] for a in _adamw(w[n][0], gsum, m[n][0], v[n][0], "adamw_" + n))
        else:
            outs = _sum_adamw(*parts, me_arr, w[n][0], m[n][0], v[n][0], "update_" + n)
            grads[n], delta[n], new_m[n], new_v[n] = (a[None] for a in outs)

    small = list(_REPLICATED) + ["hg_lb", "conv_w"]
    vals = [g[n] for n in _REPLICATED] + [g["hg_lb"].reshape(4, HG_DIM), g["conv_w"], jnp.pad(loss, ((0, 0), (0, LANE - 1)))]
    shapes = [val.shape for val in vals]
    fetch_small, _ = _gather_begin([_pack_small(vals)], "small")

    after = grad_x
    for tag, names in (("down", ["w_down"]), ("up", ["w_up"]), ("x", ["w_xo", "w_xq", "w_xkv"]), ("out", ["w_out"]),
                       ("in", ["w_in"])):
        for n, parts in zip(names, _exchange_end(pending[tag], after)):
            update(n, parts)
            after = new_v[n]

    (packs,) = _gather_end(fetch_small, [0], after)
    summed = _sum_unpack_small(packs, shapes)
    loss = summed[-1][0, 0]
    for n, s in zip(small, summed[:-1]):
        grads[n] = s
    fold = lambda v2: v2[:, :ATT_HEAD_DIM] + v2[:, ATT_HEAD_DIM:]
    grads["q_norm_g"], grads["k_norm_g"] = fold(grads["q_norm_g"]), fold(grads["k_norm_g"])
    grads["hg_lb"] = lax.dynamic_slice_in_dim(grads["hg_lb"].reshape(2, 2, HG_DIM), me * (HG_DIM // N_DEV),
                                              HG_DIM // N_DEV, axis=2)
    grads["conv_w"] = lax.dynamic_slice_in_dim(grads["conv_w"], me * (2 * D_FF // N_DEV), 2 * D_FF // N_DEV, axis=1)[None]

    flat2 = lambda a: a.reshape(-1, a.shape[-1])
    outs = _adamw_many(*[[flat2(d[n]) for n in small] for d in (w, grads, m, v)])
    for dst, vals in zip((delta, new_m, new_v), outs):
        for n, val in zip(small, vals):
            dst[n] = val.reshape(w[n].shape)

    return (loss, grad_x[None], *[grads[n] for n in _WEIGHTS], *[delta[n] for n in _WEIGHTS],
            *[new_m[n] for n in _WEIGHTS], *[new_v[n] for n in _WEIGHTS])
```

```python
import jax
import jax.numpy as jnp
from jax import lax
from jax.experimental import pallas as pl
from jax.experimental.pallas import tpu as pltpu

F32 = jnp.float32
BF16 = jnp.bfloat16

D_MODEL = 1024
GRID_W = 64
EPS = 1e-6
ATT_HEADS = 8
ATT_HEAD_DIM = 64
ATT_Q_DIM = 512
ATT_KV_DIM = 128
ROPE_THETA = 10000.0
HG_HEADS = 4
HG_DIM = 512
HG_CHUNK = 32
HG_CHUNK_LOG2 = 5
HG_BLOCK_FWD = 256
HG_BLOCK_BWD = 128
N_IN = 3328
X_HEADS = 4
X_HEAD_DIM = 256
D_FF = 2816
N_DEV = 8
LANE = 128
ADAM_LR = 0.001
ADAM_B1 = 0.9
ADAM_B2 = 0.999
ADAM_EPS = 1e-08
ADAM_WD = 0.01
ADAM_STEP = 10
VMEM_LIMIT = 56 * 1024 * 1024

MESH_T = pl.DeviceIdType.MESH


def _params(**kw):
    return pltpu.CompilerParams(vmem_limit_bytes=VMEM_LIMIT, **kw)


def _dot(a, b, ca, cb):
    return lax.dot_general(a, b, (((ca,), (cb,)), ((), ())), preferred_element_type=F32)


def _bf(x):
    return x.astype(BF16)


def _sigmoid(x):
    return 1.0 / (1.0 + jnp.exp(-x))


def _rms_fwd(x, g):
    r = lax.rsqrt(jnp.mean(x * x, axis=-1, keepdims=True) + EPS)
    return x * r * g


def _rms_bwd(dy, x, g):
    r = lax.rsqrt(jnp.mean(x * x, axis=-1, keepdims=True) + EPS)
    xh = x * r
    dg = jnp.sum(dy * xh, axis=0, keepdims=True)
    t = dy * g
    dx = r * (t - xh * jnp.mean(t * xh, axis=-1, keepdims=True))
    return dx, dg


def _full(shape):
    nd = len(shape)
    return pl.BlockSpec(shape, lambda *a: (0,) * nd)


def _norm_mm(x, g, w, trans, tn, name, after=(), tm=512):
    t, d = x.shape
    n = w.shape[0] if trans else w.shape[1]
    tm = min(tm, t)

    def body(x_ref, g_ref, w_ref, h_ref, p_ref):
        h = _bf(_rms_fwd(x_ref[...], g_ref[...]))
        h_ref[...] = h
        p_ref[...] = _dot(h, w_ref[...], 1, 1 if trans else 0)

    w_spec = pl.BlockSpec((tn, d), lambda i, j: (j, 0)) if trans else pl.BlockSpec((d, tn), lambda i, j: (0, j))
    return _call(
        body, (x, g, w), name=name, grid=(t // tm, n // tn),
        in_specs=[pl.BlockSpec((tm, d), lambda i, j: (i, 0)), _full((1, d)), w_spec],
        out_specs=[pl.BlockSpec((tm, d), lambda i, j: (i, 0)), pl.BlockSpec((tm, tn), lambda i, j: (i, j))],
        out_shape=[jax.ShapeDtypeStruct((t, d), BF16), jax.ShapeDtypeStruct((t, n), F32)], after=after)


def _mm_postnorm_res(a, w, g, res, name, tm=256):
    t, k = a.shape
    d = w.shape[1]

    def body(a_ref, w_ref, g_ref, res_ref, y_ref, o_ref):
        y = _dot(a_ref[...], w_ref[...], 1, 0)
        y_ref[...] = y
        o_ref[...] = res_ref[...] + _rms_fwd(y, g_ref[...])

    row = lambda width: pl.BlockSpec((tm, width), lambda i: (i, 0))
    return pl.pallas_call(
        body, name=name, grid=(t // tm,),
        in_specs=[row(k), _full((k, d)), _full((1, d)), row(d)],
        out_specs=[row(d), row(d)],
        out_shape=[jax.ShapeDtypeStruct((t, d), F32)] * 2,
        compiler_params=_params(),
    )(a, w, g, res)


def _mm_postnorm_res_loss(a, w, g, res, tgt, name, tm=256):
    t, k = a.shape
    d = w.shape[1]

    def body(a_ref, w_ref, g_ref, res_ref, tgt_ref, y_ref, dout_ref, loss_ref):
        @pl.when(pl.program_id(0) == 0)
        def _():
            loss_ref[...] = jnp.zeros_like(loss_ref)

        y = _dot(a_ref[...], w_ref[...], 1, 0)
        y_ref[...] = y
        diff = res_ref[...] + _rms_fwd(y, g_ref[...]) - tgt_ref[...]
        dout_ref[...] = diff * (1.0 / d)
        part = jnp.sum(jnp.sum(diff * diff, axis=-1, keepdims=True), axis=0, keepdims=True)
        loss_ref[...] += (0.5 / d) * part

    row = lambda width: pl.BlockSpec((tm, width), lambda i: (i, 0))
    return pl.pallas_call(
        body, name=name, grid=(t // tm,),
        in_specs=[row(k), _full((k, d)), _full((1, d)), row(d), row(d)],
        out_specs=[row(d), row(d), _full((1, 1))],
        out_shape=[jax.ShapeDtypeStruct((t, d), F32)] * 2 + [jax.ShapeDtypeStruct((1, 1), F32)],
        compiler_params=_params(),
    )(a, w, g, res, tgt)


def _postnorm_bwd_mm(dout, y, g, w, name, da_dtype=F32, tm=256):
    t, d = y.shape
    k = w.shape[0]

    def body(dout_ref, y_ref, g_ref, w_ref, dy_ref, da_ref, dg_ref):
        @pl.when(pl.program_id(0) == 0)
        def _():
            dg_ref[...] = jnp.zeros_like(dg_ref)

        dy, dg = _rms_bwd(dout_ref[...], y_ref[...], g_ref[...])
        dg_ref[...] += dg
        dyb = _bf(dy)
        dy_ref[...] = dyb
        da_ref[...] = _dot(dyb, w_ref[...], 1, 1).astype(da_dtype)

    row = lambda width: pl.BlockSpec((tm, width), lambda i: (i, 0))
    return pl.pallas_call(
        body, name=name, grid=(t // tm,),
        in_specs=[row(d), row(d), _full((1, d)), _full((k, d))],
        out_specs=[row(d), row(k), _full((1, d))],
        out_shape=[jax.ShapeDtypeStruct((t, d), BF16), jax.ShapeDtypeStruct((t, k), da_dtype), jax.ShapeDtypeStruct((1, d), F32)],
        compiler_params=_params(),
    )(dout, y, g, w)


def _mm_prenorm_bwd(dp, w, trans, x, g, dres, name, after=(), tm=512):
    dps = list(dp) if isinstance(dp, (list, tuple)) else [dp]
    nparts = len(dps)
    t, n_each = dps[0].shape
    d = x.shape[1]
    tm = min(tm, t)

    def body(*refs):
        dp_refs = refs[:nparts]
        w_ref, x_ref, g_ref, dres_ref, dx_ref, dg_ref = refs[nparts:]

        @pl.when(pl.program_id(0) == 0)
        def _():
            dg_ref[...] = jnp.zeros_like(dg_ref)

        dh = None
        for part, dp_ref in enumerate(dp_refs):
            cols = slice(part * n_each, (part + 1) * n_each)
            term = _dot(dp_ref[...], w_ref[cols, :], 1, 0) if trans else _dot(dp_ref[...], w_ref[:, cols], 1, 1)
            dh = term if dh is None else dh + term
        dx, dg = _rms_bwd(dh, x_ref[...], g_ref[...])
        dg_ref[...] += dg
        dx_ref[...] = dres_ref[...] + dx

    row = lambda width: pl.BlockSpec((tm, width), lambda i: (i, 0))
    return _call(
        body, (*dps, w, x, g, dres), name=name, grid=(t // tm,),
        in_specs=[row(n_each)] * nparts + [_full(w.shape), row(d), _full((1, d)), row(d)],
        out_specs=[row(d), _full((1, d))],
        out_shape=[jax.ShapeDtypeStruct((t, d), F32), jax.ShapeDtypeStruct((1, d), F32)], after=after)


def _dw(a, b, name, bcol=None, tka=256):
    parts = list(a) if isinstance(a, (list, tuple)) else [a]
    nparts = len(parts)
    t, ka_each = parts[0].shape
    bci, nb = (0, b.shape[1]) if bcol is None else bcol
    nt_each = ka_each // tka

    def body(*refs):
        a_refs, b_ref, o_ref = refs[:nparts], refs[nparts], refs[nparts + 1]
        i = pl.program_id(0)
        for part, a_ref in enumerate(a_refs):
            @pl.when((i >= part * nt_each) & (i < (part + 1) * nt_each))
            def _():
                o_ref[...] = _bf(_dot(a_ref[...], b_ref[...], 0, 0))

    a_spec = lambda part: pl.BlockSpec((t, tka), lambda i: (0, jnp.clip(i - part * nt_each, 0, nt_each - 1)))
    return pl.pallas_call(
        body, name=name, grid=(nt_each * nparts,),
        in_specs=[a_spec(part) for part in range(nparts)] + [pl.BlockSpec((t, nb), lambda i: (0, bci))],
        out_specs=pl.BlockSpec((tka, nb), lambda i: (i, 0)),
        out_shape=jax.ShapeDtypeStruct((nparts * ka_each, nb), BF16),
        compiler_params=_params(),
    )(*parts, b)


def _rope_tables(t):
    pos = jnp.arange(t)
    r = (pos // GRID_W).astype(F32)
    c = (pos % GRID_W).astype(F32)
    npair = ATT_HEAD_DIM // 4
    inv = jnp.power(ROPE_THETA, -jnp.arange(npair, dtype=F32) / npair)
    ang = jnp.concatenate([r[:, None] * inv, c[:, None] * inv], axis=-1)
    cos = jnp.repeat(jnp.cos(ang), 2, axis=-1)
    sin = jnp.repeat(jnp.sin(ang), 2, axis=-1)
    even = (jnp.arange(ATT_HEAD_DIM) % 2) == 0
    sa = jnp.where(even, -sin, 0.0)
    sb = jnp.where(even, 0.0, sin)
    two = lambda a: jnp.tile(a, (1, 2))
    return two(cos), two(sa), two(sb)


def _head_sum_matrix():
    a = jnp.arange(LANE) // ATT_HEAD_DIM
    return (a[:, None] == a[None, :]).astype(BF16)


def _head_mean(v, bd):
    hi = _bf(v)
    lo = _bf(v - hi.astype(F32))
    return (_dot(hi, bd, 1, 0) + _dot(lo, bd, 1, 0)) * (1.0 / ATT_HEAD_DIM)


def _qk_prep(p, gq, gk, tables, bd, tm=512):
    t = p.shape[0]
    tm = min(tm, t)
    cc, sa, sb = tables

    def body(p_ref, gq_ref, gk_ref, cc_ref, sa_ref, sb_ref, bd_ref, q_ref, k_ref):
        cc_, sa_, sb_, bd_ = cc_ref[...], sa_ref[...], sb_ref[...], bd_ref[...]
        low = lax.broadcasted_iota(jnp.int32, (tm, LANE), 1) < ATT_HEAD_DIM

        def normrope(xs, g):
            xn = xs * lax.rsqrt(_head_mean(xs * xs, bd_) + EPS) * g
            return xn * cc_ + pltpu.roll(xn, LANE - 1, 1) * sa_ + pltpu.roll(xn, 1, 1) * sb_

        for j in range(4):
            y = normrope(p_ref[:, j * LANE:(j + 1) * LANE], gq_ref[...]) * (ATT_HEAD_DIM ** -0.5)
            yr = pltpu.roll(y, ATT_HEAD_DIM, 1)
            if j // 2 == 0:
                h0, h1 = jnp.where(low, y, 0.0), jnp.where(low, yr, 0.0)
            else:
                h0, h1 = jnp.where(low, 0.0, yr), jnp.where(low, 0.0, y)
            q_ref[:, (2 * j) * LANE:(2 * j + 1) * LANE] = _bf(h0)
            q_ref[:, (2 * j + 1) * LANE:(2 * j + 2) * LANE] = _bf(h1)
        k_ref[...] = _bf(normrope(p_ref[:, ATT_Q_DIM:ATT_Q_DIM + LANE], gk_ref[...]))

    row = lambda width: pl.BlockSpec((tm, width), lambda i: (i, 0))
    return pl.pallas_call(
        body, name="qk_prep", grid=(t // tm,),
        in_specs=[row(ATT_Q_DIM + LANE), _full((1, LANE)), _full((1, LANE)), row(LANE), row(LANE), row(LANE),
                  _full((LANE, LANE))],
        out_specs=[row(ATT_HEADS * LANE), row(LANE)],
        out_shape=[jax.ShapeDtypeStruct((t, ATT_HEADS * LANE), BF16), jax.ShapeDtypeStruct((t, LANE), BF16)],
        compiler_params=_params(),
    )(p, gq, gk, cc, sa, sb, bd)


def _qk_prep_bwd(p, dq, dk, gq, gk, tables, bd, tm=512):
    t = p.shape[0]
    tm = min(tm, t)
    cc, sa, sb = tables

    def body(p_ref, dq_ref, dk_ref, gq_ref, gk_ref, cc_ref, sa_ref, sb_ref, bd_ref, dp_ref, dgq_ref, dgk_ref):
        @pl.when(pl.program_id(0) == 0)
        def _():
            dgq_ref[...] = jnp.zeros_like(dgq_ref)
            dgk_ref[...] = jnp.zeros_like(dgk_ref)

        cc_, sa_, sb_, bd_ = cc_ref[...], sa_ref[...], sb_ref[...], bd_ref[...]
        low = lax.broadcasted_iota(jnp.int32, (tm, LANE), 1) < ATT_HEAD_DIM

        def bwd(xs, g, dy):
            r = lax.rsqrt(_head_mean(xs * xs, bd_) + EPS)
            xh = xs * r
            dxn = dy * cc_ + pltpu.roll(dy * sa_, 1, 1) + pltpu.roll(dy * sb_, LANE - 1, 1)
            dg = jnp.sum(dxn * xh, axis=0, keepdims=True)
            tt = dxn * g
            return r * (tt - xh * _head_mean(tt * xh, bd_)), dg

        dgq = jnp.zeros((1, LANE), F32)
        for j in range(4):
            d0 = dq_ref[:, (2 * j) * LANE:(2 * j + 1) * LANE]
            d1 = dq_ref[:, (2 * j + 1) * LANE:(2 * j + 2) * LANE]
            if j // 2 == 0:
                dy = jnp.where(low, d0, pltpu.roll(d1, ATT_HEAD_DIM, 1))
            else:
                dy = jnp.where(low, pltpu.roll(d0, ATT_HEAD_DIM, 1), d1)
            dx, dg = bwd(p_ref[:, j * LANE:(j + 1) * LANE], gq_ref[...], dy * (ATT_HEAD_DIM ** -0.5))
            dp_ref[:, j * LANE:(j + 1) * LANE] = _bf(dx)
            dgq = dgq + dg
        dgq_ref[...] += dgq
        dx, dg = bwd(p_ref[:, ATT_Q_DIM:ATT_Q_DIM + LANE], gk_ref[...], dk_ref[...])
        dp_ref[:, ATT_Q_DIM:ATT_Q_DIM + LANE] = _bf(dx)
        dgk_ref[...] += dg

    row = lambda width: pl.BlockSpec((tm, width), lambda i: (i, 0))
    return pl.pallas_call(
        body, name="qk_prep_bwd", grid=(t // tm,),
        in_specs=[row(ATT_Q_DIM + LANE), row(ATT_HEADS * LANE), row(LANE), _full((1, LANE)), _full((1, LANE)),
                  row(LANE), row(LANE), row(LANE), _full((LANE, LANE))],
        out_specs=[row(ATT_Q_DIM + LANE), _full((1, LANE)), _full((1, LANE))],
        out_shape=[jax.ShapeDtypeStruct((t, ATT_Q_DIM + LANE), BF16), jax.ShapeDtypeStruct((1, LANE), F32),
                   jax.ShapeDtypeStruct((1, LANE), F32)],
        compiler_params=_params(),
    )(p, dq, dk, gq, gk, cc, sa, sb, bd)


def _attn_fwd(q, k, p, tq=256):
    t = k.shape[0]
    tq = min(tq, t)
    v_blk = (ATT_Q_DIM + ATT_KV_DIM) // LANE

    def body(q_ref, k_ref, v_ref, o_ref, o32_ref, lse_ref):
        k_ = k_ref[...]
        v = v_ref[...]
        lane_k = lax.broadcasted_iota(jnp.int32, (t, LANE), 1)
        lane_q = lax.broadcasted_iota(jnp.int32, (tq, LANE), 1)
        lowk, lowq = lane_k < ATT_HEAD_DIM, lane_q < ATT_HEAD_DIM
        ones_lane = (ATT_HEAD_DIM, 0)
        vm = (_bf(jnp.where(lowk, v, jnp.where(lane_k == ones_lane[0], 1.0, 0.0))),
              _bf(jnp.where(lowk, jnp.where(lane_k == ones_lane[1], 1.0, 0.0), v)))
        for j in range(4):
            kvh = j // 2
            acc = None
            for sub in range(2):
                h = 2 * j + sub
                s = _dot(q_ref[:, h * LANE:(h + 1) * LANE], k_, 1, 1)
                mx = jnp.max(s, axis=-1, keepdims=True)
                ov = _dot(jnp.exp(_bf(s - mx)), vm[kvh], 1, 0)
                l = jnp.sum(jnp.where(lane_q == ones_lane[kvh], ov, 0.0), axis=-1, keepdims=True)
                lse_ref[h] = mx + jnp.log(l)
                o = jnp.where(lowq if kvh == 0 else ~lowq, ov, 0.0) * (1.0 / l)
                if sub != kvh:
                    o = pltpu.roll(o, ATT_HEAD_DIM, 1)
                acc = o if acc is None else acc + o
            o32_ref[:, j * LANE:(j + 1) * LANE] = acc
            o_ref[:, j * LANE:(j + 1) * LANE] = _bf(acc)

    row = pl.BlockSpec((tq, ATT_Q_DIM), lambda i: (i, 0))
    return _call(
        body, (q, k, p), name="attn_fwd", grid=(t // tq,),
        in_specs=[pl.BlockSpec((tq, ATT_HEADS * LANE), lambda i: (i, 0)), _full((t, LANE)),
                  pl.BlockSpec((t, LANE), lambda i: (0, v_blk))],
        out_specs=[row, row, pl.BlockSpec((ATT_HEADS, tq, 1), lambda i: (0, i, 0))],
        out_shape=[jax.ShapeDtypeStruct((t, ATT_Q_DIM), BF16), jax.ShapeDtypeStruct((t, ATT_Q_DIM), F32),
                   jax.ShapeDtypeStruct((ATT_HEADS, t, 1), F32)])


def _attn_bwd(q, k, p, dcat, o32, lse, tq=256):
    t = k.shape[0]
    tq = min(tq, t)
    v_blk = (ATT_Q_DIM + ATT_KV_DIM) // LANE

    def body(q_ref, k_ref, v_ref, do_ref, o_ref, lse_ref, dq_ref, dk_ref, dv_ref):
        @pl.when(pl.program_id(0) == 0)
        def _():
            dk_ref[...] = jnp.zeros_like(dk_ref)
            dv_ref[...] = jnp.zeros_like(dv_ref)

        k_ = k_ref[...]
        vb = _bf(v_ref[...])
        lowq = lax.broadcasted_iota(jnp.int32, (tq, LANE), 1) < ATT_HEAD_DIM
        dk_acc = jnp.zeros((t, LANE), F32)
        dv_acc = jnp.zeros((t, LANE), F32)
        for j in range(4):
            kvh = j // 2
            dop = do_ref[:, j * LANE:(j + 1) * LANE]
            prod = dop * o_ref[:, j * LANE:(j + 1) * LANE]
            d_low = jnp.sum(jnp.where(lowq, prod, 0.0), axis=-1, keepdims=True)
            d_sub = (d_low, jnp.sum(prod, axis=-1, keepdims=True) - d_low)
            for sub in range(2):
                h = 2 * j + sub
                src = dop if sub == kvh else pltpu.roll(dop, ATT_HEAD_DIM, 1)
                do_h = _bf(jnp.where(lowq, src, 0.0) if kvh == 0 else jnp.where(lowq, 0.0, src))
                qh = q_ref[:, h * LANE:(h + 1) * LANE]
                pr = jnp.exp(_bf(_dot(qh, k_, 1, 1) - lse_ref[h]))
                ds = pr * _bf(_dot(do_h, vb, 1, 1) - d_sub[sub])
                dq_ref[:, h * LANE:(h + 1) * LANE] = _dot(ds, k_, 1, 0)
                dk_acc = dk_acc + _dot(ds, qh, 0, 0)
                dv_acc = dv_acc + _dot(pr, do_h, 0, 0)
        dk_ref[...] += dk_acc
        dv_ref[...] += dv_acc

    row = pl.BlockSpec((tq, ATT_Q_DIM), lambda i: (i, 0))
    return _call(
        body, (q, k, p, dcat, o32, lse), name="attn_bwd", grid=(t // tq,),
        in_specs=[pl.BlockSpec((tq, ATT_HEADS * LANE), lambda i: (i, 0)), _full((t, LANE)),
                  pl.BlockSpec((t, LANE), lambda i: (0, v_blk)), row, row,
                  pl.BlockSpec((ATT_HEADS, tq, 1), lambda i: (0, i, 0))],
        out_specs=[pl.BlockSpec((tq, ATT_HEADS * LANE), lambda i: (i, 0)), _full((t, LANE)), _full((t, LANE))],
        out_shape=[jax.ShapeDtypeStruct((t, ATT_HEADS * LANE), F32), jax.ShapeDtypeStruct((t, LANE), F32),
                   jax.ShapeDtypeStruct((t, LANE), F32)])


def _lower_bounds(a0, a1):
    def body(a0_ref, a1_ref, lb_ref):
        m = jnp.maximum(a0_ref[...], a1_ref[...])
        e0, e1 = jnp.exp(a0_ref[...] - m), jnp.exp(a1_ref[...] - m)
        lb_ref[...] = e0 / (e0 + e1)

    return pl.pallas_call(body, name="lower_bounds", out_shape=jax.ShapeDtypeStruct(a0.shape, F32))(a0, a1)


def _lower_bounds_bwd(a0, a1, dlb):
    def body(a0_ref, a1_ref, dlb_ref, d0_ref, d1_ref):
        m = jnp.maximum(a0_ref[...], a1_ref[...])
        e0, e1 = jnp.exp(a0_ref[...] - m), jnp.exp(a1_ref[...] - m)
        lb = e0 / (e0 + e1)
        d0 = dlb_ref[...] * lb * (1.0 - lb)
        d0_ref[...] = d0
        d1_ref[...] = -d0

    return pl.pallas_call(body, name="lower_bounds_bwd", out_shape=[jax.ShapeDtypeStruct(a0.shape, F32)] * 2)(a0, a1, dlb)


def _chunk_scan(x, pos, down):
    n = x.shape[0]
    s = 1
    while s < HG_CHUNK:
        if down:
            x = x + jnp.where(pos >= s, pltpu.roll(x, s, 0), 0.0)
        else:
            x = x + jnp.where(pos < HG_CHUNK - s, pltpu.roll(x, n - s, 0), 0.0)
        s *= 2
    return x


def _chunk_concat(x, nc):
    xb = _bf(x)
    rows = []
    for c in range(nc):
        pieces = [xb[c * HG_CHUNK:(c + 1) * HG_CHUNK, :]]
        if c:
            pieces.insert(0, jnp.zeros((HG_CHUNK, c * LANE), BF16))
        if c < nc - 1:
            pieces.append(jnp.zeros((HG_CHUNK, (nc - 1 - c) * LANE), BF16))
        rows.append(jnp.concatenate(pieces, axis=1) if len(pieces) > 1 else pieces[0])
    return jnp.concatenate(rows, axis=0) if nc > 1 else rows[0]


def _chunk_pick(x, nc):
    rows = [x[c * HG_CHUNK:(c + 1) * HG_CHUNK, c * LANE:(c + 1) * LANE] for c in range(nc)]
    return jnp.concatenate(rows, axis=0) if nc > 1 else rows[0]


def _chunk_rows(vals):
    rows = [jnp.broadcast_to(v, (HG_CHUNK, LANE)) for v in vals]
    return jnp.concatenate(rows, axis=0) if len(rows) > 1 else rows[0]


def _hgrn_gates(hq, z, lb):
    sq = _sigmoid(hq)
    sig = _sigmoid(z)
    f = lb + (1.0 - lb) * sig
    return hq * sq, sq, sig, f, jnp.log(f)


def _hgrn_local(q, f, g, v, pos, cid, amask, rev, nc):
    k = 1.0 - f
    ba = _chunk_scan(g, pos, not rev)
    bb = _chunk_scan(g, pos, rev)
    eq = 0.5 * (ba - bb + g)
    ex_q, ex_k, ex_i, ex_d = jnp.exp(eq), jnp.exp(-eq), jnp.exp(ba), jnp.exp(bb - g)
    qb, kb, qi, kd = q * ex_q, k * ex_k, q * ex_i, k * ex_d
    dvec = jnp.exp(ba + bb - g)
    a = jnp.where(amask, _dot(_bf(qb), _bf(kb), 1, 1), 0.0)
    kd_m = _chunk_concat(kd, nc)
    qi_m = _chunk_concat(qi, nc)
    ut_all = _dot(_bf(v), kd_m, 0, 0)
    return dict(k=k, ex_q=ex_q, ex_k=ex_k, ex_i=ex_i, ex_d=ex_d, qb=qb, kb=kb, qi=qi, kd=kd, dvec=dvec, a=a,
                kd_m=kd_m, qi_m=qi_m, ut_all=ut_all)


def _hgrn_masks(n, rev):
    row = lax.broadcasted_iota(jnp.int32, (n, LANE), 0)
    ti = lax.broadcasted_iota(jnp.int32, (n, n), 0)
    si = lax.broadcasted_iota(jnp.int32, (n, n), 1)
    tri = (si >= ti) if rev else (si <= ti)
    same = jnp.right_shift(ti, HG_CHUNK_LOG2) == jnp.right_shift(si, HG_CHUNK_LOG2)
    return jnp.bitwise_and(row, HG_CHUNK - 1), jnp.right_shift(row, HG_CHUNK_LOG2), same & tri


def _hgrn_specs(t, rev, bwd):
    n = min(HG_BLOCK_BWD if bwd else HG_BLOCK_FWD, t)
    nb = t // n
    if rev != bwd:
        blk = lambda i: nb - 1 - i
    else:
        blk = lambda i: i
    col = lambda base: pl.BlockSpec((n, 2 * LANE), lambda hp, i: (blk(i), base + hp))
    return n, nb, blk, col


def _hgrn_fwd(p, lb, rev, after=()):
    t = p.shape[0]
    n, nb, blk, col = _hgrn_specs(t, rev, False)
    nc = n // HG_CHUNK
    sub = n // min(HG_BLOCK_BWD, t)
    z_base = 7 if rev else 5

    def body(hq_ref, z_ref, hi_ref, lb_ref, o_ref, ssave_ref, st_scr):
        @pl.when(pl.program_id(1) == 0)
        def _():
            st_scr[...] = jnp.zeros_like(st_scr)

        pos, cid, amask = _hgrn_masks(n, rev)
        order = list(range(nc))[::-1] if rev else list(range(nc))
        for hh in range(2):
            sl = slice(hh * LANE, (hh + 1) * LANE)
            q, _, _, f, g = _hgrn_gates(hq_ref[:, sl], z_ref[:, sl], lb_ref[0:1, sl])
            v = hi_ref[:, sl]
            c_ = _hgrn_local(q, f, g, v, pos, cid, amask, rev, nc)
            st = st_scr[hh]
            cols = [None] * nc
            for c in order:
                cols[c] = st
                st = st * c_["dvec"][c * HG_CHUNK:c * HG_CHUNK + 1, :] + c_["ut_all"][:, c * LANE:(c + 1) * LANE]
            st_scr[hh] = st
            for s in range(sub):
                ssave_ref[sub - 1 - s if rev else s, sl, :] = cols[order[s * (nc // sub)]]
            st_all = _bf(jnp.concatenate(cols, axis=1))
            o_ref[:, sl] = _dot(_bf(c_["a"]), _bf(v), 1, 0) + _dot(c_["qi_m"], st_all, 1, 1)

    return _call(
        body, (p, p, p, lb), name="hgrn_fwd_rev" if rev else "hgrn_fwd", grid=(2, nb),
        in_specs=[col(3), col(z_base), col(9), pl.BlockSpec((1, 2 * LANE), lambda hp, i: (0, hp))],
        out_specs=[pl.BlockSpec((n, 2 * LANE), lambda hp, i: (blk(i), hp)),
                   pl.BlockSpec((sub, 2 * LANE, LANE), lambda hp, i: (blk(i), hp, 0))],
        out_shape=[jax.ShapeDtypeStruct((t, HG_DIM), F32), jax.ShapeDtypeStruct((nb * sub, HG_DIM, LANE), F32)],
        scratch_shapes=[pltpu.VMEM((2, LANE, LANE), F32)], after=after)


def _hgrn_bwd(p, lb, do, ssave, rev, after=()):
    t = p.shape[0]
    n, nb, blk, col = _hgrn_specs(t, rev, True)
    nc = n // HG_CHUNK
    z_base = 7 if rev else 5

    def body(hq_ref, z_ref, hi_ref, lb_ref, do_ref, ssave_ref, dhq_ref, dz_ref, dhi_ref, dlb_ref, dst_scr):
        @pl.when(pl.program_id(1) == 0)
        def _():
            dst_scr[...] = jnp.zeros_like(dst_scr)
            dlb_ref[...] = jnp.zeros_like(dlb_ref)

        pos, cid, amask = _hgrn_masks(n, rev)
        order = list(range(nc))[::-1] if rev else list(range(nc))
        for hh in range(2):
            sl = slice(hh * LANE, (hh + 1) * LANE)
            hq, lbv = hq_ref[:, sl], lb_ref[0:1, sl]
            q, sq, sig, f, g = _hgrn_gates(hq, z_ref[:, sl], lbv)
            v = hi_ref[:, sl]
            c_ = _hgrn_local(q, f, g, v, pos, cid, amask, rev, nc)
            dvec, ut_all = c_["dvec"], c_["ut_all"]
            drow = lambda c: dvec[c * HG_CHUNK:c * HG_CHUNK + 1, :]
            st = ssave_ref[0, sl, :]
            cols = [None] * nc
            for c in order:
                cols[c] = st
                st = st * drow(c) + ut_all[:, c * LANE:(c + 1) * LANE]
            dob, vb = _bf(do_ref[:, sl]), _bf(v)
            gt_all = _dot(dob, c_["qi_m"], 0, 0)
            dcur = dst_scr[hh]
            dnext = [None] * nc
            ddrow = [None] * nc
            for c in order[::-1]:
                dnext[c] = dcur
                ddrow[c] = jnp.sum(cols[c] * dcur, axis=0, keepdims=True) * drow(c)
                dcur = dcur * drow(c) + gt_all[:, c * LANE:(c + 1) * LANE]
            dst_scr[hh] = dcur
            dsn_all = _bf(jnp.concatenate(dnext, axis=1))
            st_all = _bf(jnp.concatenate(cols, axis=1))
            da = _bf(jnp.where(amask, _dot(dob, vb, 1, 1), 0.0))
            dv = _dot(_bf(c_["a"]), dob, 0, 0) + _dot(c_["kd_m"], dsn_all, 1, 1)
            dqb = _dot(da, _bf(c_["kb"]), 1, 0)
            dkb = _dot(da, _bf(c_["qb"]), 0, 0)
            dqi = _chunk_pick(_dot(dob, st_all, 1, 0), nc)
            dkd = _chunk_pick(_dot(vb, dsn_all, 1, 0), nc)
            dq = dqb * c_["ex_q"] + dqi * c_["ex_i"]
            dk = dkb * c_["ex_k"] + dkd * c_["ex_d"]
            e = dqb * c_["qb"] - dkb * c_["kb"] + dqi * c_["qi"]
            w = dkd * c_["kd"]
            dg = _chunk_scan(e, pos, rev) + (_chunk_scan(w, pos, not rev) - w) + _chunk_rows(ddrow)
            df = dg / f - dk
            dz_ref[:, sl] = df * (1.0 - lbv) * sig * (1.0 - sig)
            dlb_ref[0:1, sl] += jnp.sum(df * (1.0 - sig), axis=0, keepdims=True)
            dhq_ref[:, sl] = dq * (sq * (1.0 + hq * (1.0 - sq)))
            dhi_ref[:, sl] = dv

    out = pl.BlockSpec((n, 2 * LANE), lambda hp, i: (blk(i), hp))
    return _call(
        body, (p, p, p, lb, do, ssave), name="hgrn_bwd_rev" if rev else "hgrn_bwd", grid=(2, nb),
        in_specs=[col(3), col(z_base), col(9), pl.BlockSpec((1, 2 * LANE), lambda hp, i: (0, hp)), out,
                  pl.BlockSpec((1, 2 * LANE, LANE), lambda hp, i: (blk(i), hp, 0))],
        out_specs=[out, out, out, pl.BlockSpec((1, 2 * LANE), lambda hp, i: (0, hp))],
        out_shape=[jax.ShapeDtypeStruct((t, HG_DIM), F32)] * 3 + [jax.ShapeDtypeStruct((1, HG_DIM), F32)],
        scratch_shapes=[pltpu.VMEM((2, LANE, LANE), F32)], after=after)


def _mix_out(att, of, ob, p, gout, w_out, g_post, x, after=(), tm=256):
    t, d = x.shape
    tm = min(tm, t)

    def body(att_ref, of_ref, ob_ref, hg0_ref, hg1_ref, gout_ref, w_ref, g_ref, x_ref, cat_ref, y_ref, x1_ref):
        cat_ref[:, :ATT_Q_DIM] = att_ref[...]
        for h in range(HG_HEADS):
            sl = slice(h * LANE, (h + 1) * LANE)
            hg_ref = hg0_ref if h < 2 else hg1_ref
            hg = hg_ref[:, (h % 2) * LANE:(h % 2 + 1) * LANE]
            nrm = _rms_fwd(of_ref[:, sl] + ob_ref[:, sl], gout_ref[...])
            cat_ref[:, ATT_Q_DIM + h * LANE:ATT_Q_DIM + (h + 1) * LANE] = _bf(nrm * hg * _sigmoid(hg))
        y = _dot(cat_ref[...], w_ref[...], 1, 0)
        y_ref[...] = y
        x1_ref[...] = x_ref[...] + _rms_fwd(y, g_ref[...])

    row = lambda width: pl.BlockSpec((tm, width), lambda i: (i, 0))
    return _call(
        body, (att, of, ob, p, p, gout, w_out, g_post, x), name="mix_out", grid=(t // tm,),
        in_specs=[row(ATT_Q_DIM), row(HG_DIM), row(HG_DIM), pl.BlockSpec((tm, 2 * LANE), lambda i: (i, 11)),
                  pl.BlockSpec((tm, 2 * LANE), lambda i: (i, 12)), _full((1, LANE)), _full((d, d)), _full((1, d)), row(d)],
        out_specs=[row(d), row(d), row(d)],
        out_shape=[jax.ShapeDtypeStruct((t, d), BF16), jax.ShapeDtypeStruct((t, d), F32), jax.ShapeDtypeStruct((t, d), F32)],
        after=after)


def _rec_bwd(dcat, of, ob, p, gout, tm=256):
    t = of.shape[0]
    tm = min(tm, t)

    def body(dc_ref, of_ref, ob_ref, hg0_ref, hg1_ref, gout_ref, do_ref, dhg_ref, dgo_ref):
        @pl.when(pl.program_id(0) == 0)
        def _():
            dgo_ref[...] = jnp.zeros_like(dgo_ref)

        dgo = jnp.zeros((1, LANE), F32)
        for h in range(HG_HEADS):
            sl = slice(h * LANE, (h + 1) * LANE)
            hg_ref = hg0_ref if h < 2 else hg1_ref
            hg = hg_ref[:, (h % 2) * LANE:(h % 2 + 1) * LANE]
            o = of_ref[:, sl] + ob_ref[:, sl]
            sg = _sigmoid(hg)
            drec = dc_ref[:, sl]
            dhg_ref[:, sl] = drec * _rms_fwd(o, gout_ref[...]) * (sg * (1.0 + hg * (1.0 - sg)))
            do, dg = _rms_bwd(drec * hg * sg, o, gout_ref[...])
            do_ref[:, sl] = do
            dgo = dgo + dg
        dgo_ref[...] += dgo

    row = lambda width: pl.BlockSpec((tm, width), lambda i: (i, 0))
    return pl.pallas_call(
        body, name="rec_bwd", grid=(t // tm,),
        in_specs=[pl.BlockSpec((tm, HG_DIM), lambda i: (i, 1)), row(HG_DIM), row(HG_DIM),
                  pl.BlockSpec((tm, 2 * LANE), lambda i: (i, 11)), pl.BlockSpec((tm, 2 * LANE), lambda i: (i, 12)),
                  _full((1, LANE))],
        out_specs=[row(HG_DIM), row(HG_DIM), _full((1, LANE))],
        out_shape=[jax.ShapeDtypeStruct((t, HG_DIM), F32), jax.ShapeDtypeStruct((t, HG_DIM), F32),
                   jax.ShapeDtypeStruct((1, LANE), F32)],
        compiler_params=_params(),
    )(dcat, of, ob, p, p, gout)


def _assemble_dp(dp_qk, dv, dhq_f, dhq_b, dz_f, dz_b, dhi_f, dhi_b, dhg, tm=512):
    t = dv.shape[0]
    tm = min(tm, t)
    qk_w = ATT_Q_DIM + LANE

    def body(qk_ref, dv_ref, hqf_ref, hqb_ref, zf_ref, zb_ref, hif_ref, hib_ref, hg_ref, dp_ref):
        o = 0
        dp_ref[:, o:o + qk_w] = qk_ref[...]
        o += qk_w
        dp_ref[:, o:o + LANE] = _bf(dv_ref[...])
        o += LANE
        for val in (hqf_ref[...] + hqb_ref[...], zf_ref[...], zb_ref[...], hif_ref[...] + hib_ref[...], hg_ref[...]):
            dp_ref[:, o:o + HG_DIM] = _bf(val)
            o += HG_DIM

    row = lambda width: pl.BlockSpec((tm, width), lambda i: (i, 0))
    return pl.pallas_call(
        body, name="assemble_dp", grid=(t // tm,),
        in_specs=[row(qk_w), row(LANE)] + [row(HG_DIM)] * 7,
        out_specs=row(N_IN),
        out_shape=jax.ShapeDtypeStruct((t, N_IN), BF16),
        compiler_params=_params(),
    )(dp_qk, dv, dhq_f, dhq_b, dz_f, dz_b, dhi_f, dhi_b, dhg)


def _xattn_fwd(q, kv, tq=512):
    t, d = q.shape
    tq = min(tq, t)
    nm = kv.shape[0]

    def body(q_ref, kv_ref, o_ref):
        for h in range(X_HEADS):
            sl = slice(h * X_HEAD_DIM, (h + 1) * X_HEAD_DIM)
            s = _dot(_bf(q_ref[:, sl]), _bf(kv_ref[:, sl]), 1, 1) * (X_HEAD_DIM ** -0.5)
            e = jnp.exp(s - jnp.max(s, axis=-1, keepdims=True))
            pr = e * (1.0 / jnp.sum(e, axis=-1, keepdims=True))
            o_ref[:, sl] = _bf(_dot(_bf(pr), _bf(kv_ref[:, d + h * X_HEAD_DIM:d + (h + 1) * X_HEAD_DIM]), 1, 0))

    return pl.pallas_call(
        body, name="xattn_fwd", grid=(t // tq,),
        in_specs=[pl.BlockSpec((tq, d), lambda i: (i, 0)), _full((nm, 2 * d))],
        out_specs=pl.BlockSpec((tq, d), lambda i: (i, 0)),
        out_shape=jax.ShapeDtypeStruct((t, d), BF16),
        compiler_params=_params(),
    )(q, kv)


def _xattn_bwd(q, kv, do, tq=512):
    t, d = q.shape
    tq = min(tq, t)
    nm = kv.shape[0]

    def body(q_ref, kv_ref, do_ref, dq_ref, dkv_ref):
        @pl.when(pl.program_id(0) == 0)
        def _():
            dkv_ref[...] = jnp.zeros_like(dkv_ref)

        for h in range(X_HEADS):
            sl = slice(h * X_HEAD_DIM, (h + 1) * X_HEAD_DIM)
            slv = slice(d + h * X_HEAD_DIM, d + (h + 1) * X_HEAD_DIM)
            qb, kb, vb, dob = _bf(q_ref[:, sl]), _bf(kv_ref[:, sl]), _bf(kv_ref[:, slv]), _bf(do_ref[:, sl])
            s = _dot(qb, kb, 1, 1) * (X_HEAD_DIM ** -0.5)
            e = jnp.exp(s - jnp.max(s, axis=-1, keepdims=True))
            pr = e * (1.0 / jnp.sum(e, axis=-1, keepdims=True))
            dpr = _dot(dob, vb, 1, 1)
            ds = _bf(pr * (dpr - jnp.sum(pr * dpr, axis=-1, keepdims=True)) * (X_HEAD_DIM ** -0.5))
            dq_ref[:, sl] = _bf(_dot(ds, kb, 1, 0))
            dkv_ref[:, sl] += _dot(ds, qb, 0, 0)
            dkv_ref[:, slv] += _dot(_bf(pr), dob, 0, 0)

    return pl.pallas_call(
        body, name="xattn_bwd", grid=(t // tq,),
        in_specs=[pl.BlockSpec((tq, d), lambda i: (i, 0)), _full((nm, 2 * d)), pl.BlockSpec((tq, d), lambda i: (i, 0))],
        out_specs=[pl.BlockSpec((tq, d), lambda i: (i, 0)), _full((nm, 2 * d))],
        out_shape=[jax.ShapeDtypeStruct((t, d), BF16), jax.ShapeDtypeStruct((nm, 2 * d), F32)],
        compiler_params=_params(),
    )(q, kv, do)


CONV_TN = 256


def _shift_rows(u, row, t, delta):
    if delta < 0:
        return jnp.where(row == 0, 0.0, pltpu.roll(u, 1, 0))
    return jnp.where(row == t - 1, 0.0, pltpu.roll(u, t - 1, 0))


def _conv_gate_fwd(u, conv_w, conv_b):
    t = u.shape[0]
    nt = D_FF // CONV_TN

    def body(ug_ref, uv_ref, wg_ref, wv_ref, bg_ref, bv_ref, a_ref):
        row = lax.broadcasted_iota(jnp.int32, (t, CONV_TN), 0)

        def conv(u_ref, w_ref, b_ref):
            uu = u_ref[...]
            return (b_ref[...] + _shift_rows(uu, row, t, -1) * w_ref[0:1, :] + uu * w_ref[1:2, :]
                    + _shift_rows(uu, row, t, 1) * w_ref[2:3, :])

        gate = conv(ug_ref, wg_ref, bg_ref)
        a_ref[...] = _bf(gate * _sigmoid(gate) * conv(uv_ref, wv_ref, bv_ref))

    col = lambda rows, off: pl.BlockSpec((rows, CONV_TN), lambda j: (0, j + off))
    return pl.pallas_call(
        body, name="conv_gate_fwd", grid=(nt,),
        in_specs=[col(t, 0), col(t, nt), col(3, 0), col(3, nt), col(1, 0), col(1, nt)],
        out_specs=col(t, 0),
        out_shape=jax.ShapeDtypeStruct((t, D_FF), BF16),
        compiler_params=_params(),
    )(u, u, conv_w, conv_w, conv_b, conv_b)


def _conv_gate_bwd(u, conv_w, conv_b, da, after=()):
    t = u.shape[0]
    nt = D_FF // CONV_TN

    def body(ug_ref, uv_ref, wg_ref, wv_ref, bg_ref, bv_ref, da_ref, dug_ref, dwg_ref, dbg_ref, duv_ref, dwv_ref, dbv_ref):
        row = lax.broadcasted_iota(jnp.int32, (t, CONV_TN), 0)
        ug, uv = ug_ref[...], uv_ref[...]
        ug_m, ug_p = _shift_rows(ug, row, t, -1), _shift_rows(ug, row, t, 1)
        uv_m, uv_p = _shift_rows(uv, row, t, -1), _shift_rows(uv, row, t, 1)
        gate = bg_ref[...] + ug_m * wg_ref[0:1, :] + ug * wg_ref[1:2, :] + ug_p * wg_ref[2:3, :]
        val = bv_ref[...] + uv_m * wv_ref[0:1, :] + uv * wv_ref[1:2, :] + uv_p * wv_ref[2:3, :]
        sg = _sigmoid(gate)
        da_ = da_ref[...].astype(F32)

        def emit(dacc, um, uu, up, w_ref, du_ref, dw_ref, db_ref):
            du_ref[...] = _bf(_shift_rows(dacc, row, t, 1) * w_ref[0:1, :] + dacc * w_ref[1:2, :]
                              + _shift_rows(dacc, row, t, -1) * w_ref[2:3, :])
            dw_ref[0:1, :] = jnp.sum(dacc * um, axis=0, keepdims=True)
            dw_ref[1:2, :] = jnp.sum(dacc * uu, axis=0, keepdims=True)
            dw_ref[2:3, :] = jnp.sum(dacc * up, axis=0, keepdims=True)
            db_ref[...] = jnp.sum(dacc, axis=0, keepdims=True)

        emit(da_ * val * (sg * (1.0 + gate * (1.0 - sg))), ug_m, ug, ug_p, wg_ref, dug_ref, dwg_ref, dbg_ref)
        emit(da_ * gate * sg, uv_m, uv, uv_p, wv_ref, duv_ref, dwv_ref, dbv_ref)

    col = lambda rows, off: pl.BlockSpec((rows, CONV_TN), lambda j: (0, j + off))
    half_shapes = [jax.ShapeDtypeStruct((t, D_FF), BF16), jax.ShapeDtypeStruct((3, D_FF), F32),
                   jax.ShapeDtypeStruct((1, D_FF), F32)]
    outs = _call(
        body, (u, u, conv_w, conv_w, conv_b, conv_b, da), name="conv_gate_bwd", grid=(nt,),
        in_specs=[col(t, 0), col(t, nt), col(3, 0), col(3, nt), col(1, 0), col(1, nt), col(t, 0)],
        out_specs=[col(t, 0), col(3, 0), col(1, 0)] * 2, out_shape=half_shapes * 2, after=after)
    return outs[:3], outs[3:]


def _row_tile(r, cap):
    best = r
    for cand in range(16, cap + 1, 16):
        if r % cand == 0:
            best = cand
    return best


def _sum_parts(own, got, me, name, tr=256):
    _, r, c = got.shape
    tr = _row_tile(r, tr)

    def body(me_ref, own_ref, got_ref, o_ref):
        mine = own_ref[...].astype(F32)
        acc = None
        for i in range(N_DEV):
            term = jnp.where(me_ref[0] == i, mine, got_ref[i].astype(F32))
            acc = term if acc is None else acc + term
        o_ref[...] = acc

    return pl.pallas_call(
        body, name=name,
        grid_spec=pltpu.PrefetchScalarGridSpec(
            num_scalar_prefetch=1, grid=(r // tr,),
            in_specs=[pl.BlockSpec((None, tr, c), lambda i, me_ref: (me_ref[0], i, 0)),
                      pl.BlockSpec((N_DEV, tr, c), lambda i, me_ref: (0, i, 0))],
            out_specs=pl.BlockSpec((tr, c), lambda i, me_ref: (i, 0))),
        out_shape=jax.ShapeDtypeStruct((r, c), F32),
        compiler_params=_params(),
    )(me, own, got)


def _adamw_math(w_, g_, m_, v_):
    m_ = ADAM_B1 * m_ + (1.0 - ADAM_B1) * g_
    v_ = ADAM_B2 * v_ + (1.0 - ADAM_B2) * (g_ * g_)
    m_hat = m_ / (1.0 - ADAM_B1 ** ADAM_STEP)
    v_hat = v_ / (1.0 - ADAM_B2 ** ADAM_STEP)
    return -ADAM_LR * (m_hat / (jnp.sqrt(v_hat) + ADAM_EPS) + ADAM_WD * w_), m_, v_


def _sum_adamw(own, got, me, w, m, v, name, tr=256):
    _, r, c = got.shape
    tr = _row_tile(r, tr)

    def body(me_ref, own_ref, got_ref, w_ref, m_ref, v_ref, g_ref, d_ref, nm_ref, nv_ref):
        mine = own_ref[...].astype(F32)
        acc = None
        for i in range(N_DEV):
            term = jnp.where(me_ref[0] == i, mine, got_ref[i].astype(F32))
            acc = term if acc is None else acc + term
        g_ref[...] = acc
        d_ref[...], nm_ref[...], nv_ref[...] = _adamw_math(w_ref[...], acc, m_ref[...], v_ref[...])

    blk = pl.BlockSpec((tr, c), lambda i, me_ref: (i, 0))
    return pl.pallas_call(
        body, name=name,
        grid_spec=pltpu.PrefetchScalarGridSpec(
            num_scalar_prefetch=1, grid=(r // tr,),
            in_specs=[pl.BlockSpec((None, tr, c), lambda i, me_ref: (me_ref[0], i, 0)),
                      pl.BlockSpec((N_DEV, tr, c), lambda i, me_ref: (0, i, 0)), blk, blk, blk],
            out_specs=[blk] * 4),
        out_shape=[jax.ShapeDtypeStruct((r, c), F32)] * 4,
        compiler_params=_params(),
    )(me, own, got, w, m, v)


def _adamw(w, g, m, v, name, tr=256):
    r, c = w.shape
    tr = _row_tile(r, tr)

    def body(w_ref, g_ref, m_ref, v_ref, d_ref, nm_ref, nv_ref):
        d_ref[...], nm_ref[...], nv_ref[...] = _adamw_math(w_ref[...], g_ref[...], m_ref[...], v_ref[...])

    blk = pl.BlockSpec((tr, c), lambda i: (i, 0))
    return pl.pallas_call(
        body, name=name, grid=(r // tr,),
        in_specs=[blk] * 4, out_specs=[blk] * 3,
        out_shape=[jax.ShapeDtypeStruct((r, c), F32)] * 3,
        compiler_params=_params(),
    )(w, g, m, v)


def _mesh_pos():
    return lax.axis_index("x"), lax.axis_index("y"), lax.axis_index("c")


def _dev_index(px, py, pc):
    return 4 * px + 2 * py + pc


class _Gather:
    def __init__(self, arrs):
        self.arrs = list(arrs)
        n = len(self.arrs)
        self.out_shape = [jax.ShapeDtypeStruct((N_DEV,) + a.shape, a.dtype) for a in self.arrs]
        self.sems = [pltpu.SemaphoreType.DMA((7, n)), pltpu.SemaphoreType.DMA((7, n)), pltpu.SemaphoreType.DMA((n,))]

    def _ctx(self, ins, outs, sems):
        send_sems, recv_sems, local_sems = sems
        x, y, c = _mesh_pos()
        chips = [(1 - x, y), (x, 1 - y), (1 - x, 1 - y)]

        def copy(k, a, block, to, src=None):
            dst = outs[a].at[_dev_index(*block)]
            return pltpu.make_async_remote_copy(
                src_ref=dst if src is None else src, dst_ref=dst, send_sem=send_sems.at[k, a], recv_sem=recv_sems.at[k, a],
                device_id=to, device_id_type=MESH_T)

        n = len(ins)
        me, sibling = (x, y, c), (x, y, 1 - c)
        mine = [pltpu.make_async_copy(ins[a], outs[a].at[_dev_index(*me)], local_sems.at[a]) for a in range(n)]
        first = [copy(0, a, me, sibling, src=ins[a]) for a in range(n)]
        first += [copy(1 + j, a, me, (*chip, c), src=ins[a]) for j, chip in enumerate(chips) for a in range(n)]
        passed = [copy(4 + j, a, (*chip, c), sibling) for j, chip in enumerate(chips) for a in range(n)]
        return n, c, me, sibling, chips, copy, mine, first, passed

    def start(self, ins, outs, sems):
        _, _, _, _, _, _, mine, first, _ = self._ctx(ins, outs, sems)
        for cp in mine + first:
            cp.start()

    def forward(self, ins, outs, sems):
        n, c, me, _, chips, copy, _, _, passed = self._ctx(ins, outs, sems)
        for j, chip in enumerate(chips):
            for a in range(n):
                copy(1 + j, a, (*chip, c), me).wait_recv()
                passed[j * n + a].start()

    def finish(self, ins, outs, sems):
        n, c, me, sibling, chips, copy, mine, first, passed = self._ctx(ins, outs, sems)
        for a in range(n):
            copy(0, a, sibling, me).wait_recv()
        for j, chip in enumerate(chips):
            for a in range(n):
                copy(4 + j, a, (*chip, 1 - c), me).wait_recv()
        for cp in first + passed:
            cp.wait_send()
        for cp in mine:
            cp.wait()


def _comm_alone(comm, name):
    n = len(comm.arrs)

    def body(*refs):
        ins, outs, sems = refs[:n], refs[n:2 * n], refs[2 * n:]
        comm.start(ins, outs, sems)
        if comm.forward is not None:
            comm.forward(ins, outs, sems)
        comm.finish(ins, outs, sems)

    any_spec = pl.BlockSpec(memory_space=pl.ANY)
    return pl.pallas_call(body, name=name, in_specs=[any_spec] * n, out_specs=[any_spec] * n, out_shape=comm.out_shape,
                          scratch_shapes=comm.sems)(*comm.arrs)


def _peers(x, y, c):
    out = []
    for k in range(1, N_DEV):
        pos = (1 - x if k & 4 else x, 1 - y if k & 2 else y, 1 - c if k & 1 else c)
        out.append((k, pos, _dev_index(*pos)))
    return out


def _exchange_begin(arrs, tag):
    n = len(arrs)
    lands = [lax.empty(a.shape, a.dtype) for a in arrs]

    def start_body(*refs):
        ins, land = refs[:n], refs[n:2 * n]
        send_sems, recv_sems, token = refs[2 * n], refs[2 * n + 1], refs[-1]
        x, y, c = _mesh_pos()
        me_i = _dev_index(x, y, c)
        for k, pos, peer_i in _peers(x, y, c):
            for a in range(n):
                pltpu.make_async_remote_copy(
                    src_ref=ins[a].at[peer_i], dst_ref=land[a].at[me_i], send_sem=send_sems.at[(k - 1) * n + a],
                    recv_sem=recv_sems.at[(k - 1) * n + a], device_id=pos, device_id_type=MESH_T).start()
        token[...] = jnp.zeros_like(token)

    hbm = pl.BlockSpec(memory_space=pltpu.HBM)
    sem = pl.BlockSpec(memory_space=pltpu.SEMAPHORE)
    thru = [pltpu.HBM(a.shape, a.dtype) for a in arrs]
    outs = pl.pallas_call(
        start_body, name="exchange_start_" + tag,
        out_shape=[pltpu.SemaphoreType.DMA((7 * n,)), pltpu.SemaphoreType.DMA((7 * n,))] + thru + thru
        + [jax.ShapeDtypeStruct((8, LANE), F32)],
        in_specs=[hbm] * (2 * n), out_specs=[sem, sem] + [hbm] * (2 * n) + [pl.BlockSpec(memory_space=pltpu.VMEM)],
        input_output_aliases={i: 2 + i for i in range(2 * n)},
        compiler_params=pltpu.CompilerParams(has_side_effects=pltpu.SideEffectType.DATAFLOW_SIDE_EFFECTING),
    )(*[pltpu.with_memory_space_constraint(a, pltpu.HBM) for a in list(arrs) + list(lands)])
    return (tag, n, outs[0], outs[1], outs[2:2 + n], outs[2 + n:2 + 2 * n]), outs[-1]


def _exchange_end(handle, after):
    tag, n, send_sems, recv_sems, srcs, lands = handle

    def body(*refs):
        ins, land = refs[:n], refs[n:2 * n]
        send_sems_, recv_sems_ = refs[2 * n], refs[2 * n + 1]
        x, y, c = _mesh_pos()
        me_i = _dev_index(x, y, c)
        for k, pos, peer_i in _peers(x, y, c):
            for a in range(n):
                cp = pltpu.make_async_remote_copy(
                    src_ref=ins[a].at[peer_i], dst_ref=land[a].at[me_i], send_sem=send_sems_.at[(k - 1) * n + a],
                    recv_sem=recv_sems_.at[(k - 1) * n + a], device_id=pos, device_id_type=MESH_T)
                cp.wait_send()
                cp.wait_recv()

    hbm = pl.BlockSpec(memory_space=pltpu.HBM)
    sem = pl.BlockSpec(memory_space=pltpu.SEMAPHORE)
    outs = pl.pallas_call(
        body, name="exchange_end_" + tag, out_shape=[pltpu.HBM(a.shape, a.dtype) for a in list(srcs) + list(lands)],
        in_specs=[hbm] * (2 * n) + [sem, sem, pl.BlockSpec(memory_space=pl.ANY)], out_specs=[hbm] * (2 * n),
        input_output_aliases={i: i for i in range(2 * n)},
        compiler_params=pltpu.CompilerParams(has_side_effects=pltpu.SideEffectType.DATAFLOW_SIDE_EFFECTING),
    )(*srcs, *lands, send_sems, recv_sems, after)
    return list(zip(outs[:n], outs[n:]))


def _gather_begin(arrs, tag):
    n = len(arrs)
    me = _dev_index(*_mesh_pos())
    lands = [lax.dynamic_update_slice(lax.empty((N_DEV,) + a.shape, a.dtype), a[None], (me,) + (0,) * a.ndim) for a in arrs]

    def start_body(*refs):
        ins, land = refs[:n], refs[n:2 * n]
        send_sems, recv_sems, token = refs[2 * n], refs[2 * n + 1], refs[-1]
        x, y, c = _mesh_pos()
        me_i = _dev_index(x, y, c)
        for a in range(n):
            for k, pos, _ in _peers(x, y, c):
                pltpu.make_async_remote_copy(
                    src_ref=ins[a], dst_ref=land[a].at[me_i], send_sem=send_sems.at[(k - 1) * n + a],
                    recv_sem=recv_sems.at[(k - 1) * n + a], device_id=pos, device_id_type=MESH_T).start()
        token[...] = jnp.zeros_like(token)

    hbm = pl.BlockSpec(memory_space=pltpu.HBM)
    sem = pl.BlockSpec(memory_space=pltpu.SEMAPHORE)
    thru = [pltpu.HBM(a.shape, a.dtype) for a in list(arrs) + lands]
    outs = pl.pallas_call(
        start_body, name="gather_start_" + tag,
        out_shape=[pltpu.SemaphoreType.DMA((7 * n,)), pltpu.SemaphoreType.DMA((7 * n,))] + thru
        + [jax.ShapeDtypeStruct((8, LANE), F32)],
        in_specs=[hbm] * (2 * n), out_specs=[sem, sem] + [hbm] * (2 * n) + [pl.BlockSpec(memory_space=pltpu.VMEM)],
        input_output_aliases={i: 2 + i for i in range(2 * n)},
        compiler_params=pltpu.CompilerParams(has_side_effects=pltpu.SideEffectType.DATAFLOW_SIDE_EFFECTING),
    )(*[pltpu.with_memory_space_constraint(a, pltpu.HBM) for a in list(arrs) + lands])
    return (tag, n, outs[0], outs[1], outs[2:2 + n], outs[2 + n:2 + 2 * n]), outs[-1]


def _gather_end(handle, which, after):
    tag, n, send_sems, recv_sems, srcs, lands = handle
    m = len(which)

    def body(*refs):
        ins, land = refs[:m], refs[m:2 * m]
        send_sems_, recv_sems_ = refs[2 * m], refs[2 * m + 1]
        x, y, c = _mesh_pos()
        me_i = _dev_index(x, y, c)
        for j, a in enumerate(which):
            for k, pos, _ in _peers(x, y, c):
                cp = pltpu.make_async_remote_copy(
                    src_ref=ins[j], dst_ref=land[j].at[me_i], send_sem=send_sems_.at[(k - 1) * n + a],
                    recv_sem=recv_sems_.at[(k - 1) * n + a], device_id=pos, device_id_type=MESH_T)
                cp.wait_send()
                cp.wait_recv()

    hbm = pl.BlockSpec(memory_space=pltpu.HBM)
    sem = pl.BlockSpec(memory_space=pltpu.SEMAPHORE)
    ops = [srcs[a] for a in which] + [lands[a] for a in which]
    outs = pl.pallas_call(
        body, name="gather_end_%s_%s" % (tag, "_".join(str(a) for a in which)),
        out_shape=[pltpu.HBM(a.shape, a.dtype) for a in ops],
        in_specs=[hbm] * (2 * m) + [sem, sem, pl.BlockSpec(memory_space=pl.ANY)], out_specs=[hbm] * (2 * m),
        input_output_aliases={i: i for i in range(2 * m)},
        compiler_params=pltpu.CompilerParams(has_side_effects=pltpu.SideEffectType.DATAFLOW_SIDE_EFFECTING),
    )(*ops, send_sems, recv_sems, after)
    return list(outs[m:])


_SPLIT = dict(has_side_effects=pltpu.SideEffectType.DATAFLOW_SIDE_EFFECTING)


def _chips(x, y):
    return [(1 - x, y), (x, 1 - y), (1 - x, 1 - y)]


def _gather2_begin(arrs, tag):
    n = len(arrs)
    me = _dev_index(*_mesh_pos())
    lands = [lax.dynamic_update_slice(lax.empty((N_DEV,) + a.shape, a.dtype), a[None], (me,) + (0,) * a.ndim) for a in arrs]

    def body(*refs):
        ins, land = refs[:n], refs[n:2 * n]
        send1, recv1, token = refs[2 * n], refs[2 * n + 1], refs[-1]
        x, y, c = _mesh_pos()
        me_i = _dev_index(x, y, c)
        targets = [(x, y, 1 - c)] + [(*chip, c) for chip in _chips(x, y)]
        for a in range(n):
            for k, pos in enumerate(targets):
                pltpu.make_async_remote_copy(
                    src_ref=ins[a], dst_ref=land[a].at[me_i], send_sem=send1.at[k * n + a], recv_sem=recv1.at[k * n + a],
                    device_id=pos, device_id_type=MESH_T).start()
        token[...] = jnp.zeros_like(token)

    hbm = pl.BlockSpec(memory_space=pltpu.HBM)
    sem = pl.BlockSpec(memory_space=pltpu.SEMAPHORE)
    thru = [pltpu.HBM(a.shape, a.dtype) for a in list(arrs) + lands]
    outs = pl.pallas_call(
        body, name="gather_start_" + tag,
        out_shape=[pltpu.SemaphoreType.DMA((4 * n,)), pltpu.SemaphoreType.DMA((4 * n,))] + thru
        + [jax.ShapeDtypeStruct((8, LANE), F32)],
        in_specs=[hbm] * (2 * n), out_specs=[sem, sem] + [hbm] * (2 * n) + [pl.BlockSpec(memory_space=pltpu.VMEM)],
        input_output_aliases={i: 2 + i for i in range(2 * n)}, compiler_params=pltpu.CompilerParams(**_SPLIT),
    )(*[pltpu.with_memory_space_constraint(a, pltpu.HBM) for a in list(arrs) + lands])
    return dict(tag=tag, n=n, send1=outs[0], recv1=outs[1], srcs=list(outs[2:2 + n]), lands=list(outs[2 + n:2 + 2 * n])), outs[-1]


def _gather2_forward(handle, which, after, part):
    n, m = handle["n"], len(which)

    def body(*refs):
        land, recv1 = refs[:m], refs[m]
        send2, recv2, token = refs[m + 2], refs[m + 3], refs[-1]
        token[...] = jnp.zeros_like(token)
        x, y, c = _mesh_pos()
        for j, a in enumerate(which):
            for k, chip in enumerate(_chips(x, y)):
                blk = land[j].at[_dev_index(*chip, c)]
                copy = pltpu.make_async_remote_copy(
                    src_ref=blk, dst_ref=blk, send_sem=send2.at[k * m + j], recv_sem=recv2.at[k * m + j],
                    device_id=(x, y, 1 - c), device_id_type=MESH_T)
                pltpu.make_async_remote_copy(
                    src_ref=blk, dst_ref=blk, send_sem=send2.at[k * m + j], recv_sem=recv1.at[(1 + k) * n + a],
                    device_id=(*chip, c), device_id_type=MESH_T).wait_recv()
                copy.start()

    hbm = pl.BlockSpec(memory_space=pltpu.HBM)
    sem = pl.BlockSpec(memory_space=pltpu.SEMAPHORE)
    lands = [handle["lands"][a] for a in which]
    outs = pl.pallas_call(
        body, name="gather_forward_%s_%s" % (handle["tag"], part),
        out_shape=[pltpu.SemaphoreType.DMA((3 * m,)), pltpu.SemaphoreType.DMA((3 * m,))] + [pltpu.HBM(a.shape, a.dtype) for a in lands]
        + [jax.ShapeDtypeStruct((8, LANE), F32)],
        in_specs=[hbm] * m + [sem, pl.BlockSpec(memory_space=pl.ANY)],
        out_specs=[sem, sem] + [hbm] * m + [pl.BlockSpec(memory_space=pltpu.VMEM)],
        input_output_aliases={i: 2 + i for i in range(m)}, compiler_params=pltpu.CompilerParams(**_SPLIT),
    )(*lands, handle["recv1"], after)
    for j, a in enumerate(which):
        handle["lands"][a] = outs[2 + j]
        handle[("leg2", a)] = (outs[0], outs[1], j, m)
    return outs[-1]


def _gather2_end(handle, a, after):
    n = handle["n"]
    send2, recv2, j, m = handle[("leg2", a)]

    def body(src, land, send1, recv1, send2_, recv2_, after_ref, src_out, land_out):
        x, y, c = _mesh_pos()
        me_i = _dev_index(x, y, c)
        sibling = (x, y, 1 - c)
        for k, pos in enumerate([sibling] + [(*chip, c) for chip in _chips(x, y)]):
            first = pltpu.make_async_remote_copy(
                src_ref=src, dst_ref=land.at[me_i], send_sem=send1.at[k * n + a], recv_sem=recv1.at[k * n + a],
                device_id=pos, device_id_type=MESH_T)
            first.wait_send()
            if k == 0:
                first.wait_recv()
        for k in range(3):
            second = pltpu.make_async_remote_copy(
                src_ref=land.at[me_i], dst_ref=land.at[me_i], send_sem=send2_.at[k * m + j], recv_sem=recv2_.at[k * m + j],
                device_id=sibling, device_id_type=MESH_T)
            second.wait_send()
            second.wait_recv()

    hbm = pl.BlockSpec(memory_space=pltpu.HBM)
    sem = pl.BlockSpec(memory_space=pltpu.SEMAPHORE)
    src, land = handle["srcs"][a], handle["lands"][a]
    outs = pl.pallas_call(
        body, name="gather_end_%s_%d" % (handle["tag"], a),
        out_shape=[pltpu.HBM(src.shape, src.dtype), pltpu.HBM(land.shape, land.dtype)],
        in_specs=[hbm, hbm, sem, sem, sem, sem, pl.BlockSpec(memory_space=pl.ANY)], out_specs=[hbm, hbm],
        input_output_aliases={0: 0, 1: 1}, compiler_params=pltpu.CompilerParams(**_SPLIT),
    )(src, land, handle["send1"], handle["recv1"], send2, recv2, after)
    return outs[1]


def _call(body, operands, *, name, grid, in_specs, out_specs, out_shape, scratch_shapes=(), after=()):
    ni, nd = len(in_specs), len(after)

    def ordered(*refs):
        body(*refs[:ni], *refs[ni + nd:])

    outs = pl.pallas_call(
        ordered, name=name, grid=grid, in_specs=list(in_specs) + [pl.BlockSpec(memory_space=pl.ANY)] * nd,
        out_specs=out_specs, out_shape=out_shape, scratch_shapes=list(scratch_shapes), compiler_params=_params(),
    )(*operands, *after)
    return list(outs)


def _rows(a):
    return a.reshape(-1, a.shape[-1])


def _slots(a):
    return a.reshape(N_DEV, -1, a.shape[-1])


def _local_step(x, mem, tgt, w, wire):
    tables = _rope_tables(x.shape[0])
    bd = _head_sum_matrix()
    two = lambda g: jnp.tile(g, (1, 2))
    gq2, gk2 = two(w["q_norm_g"]), two(w["k_norm_g"])

    w_in_t, hg_lb, conv_w = _comm_alone(_Gather([wire["w_in"], w["hg_lb"].reshape(4, -1), w["conv_w"][0]]), "gather_w_in")
    w_in_t = _rows(w_in_t)
    hg_lb = jnp.transpose(hg_lb.reshape(N_DEV, 2, 2, -1), (1, 2, 0, 3)).reshape(2, 2, HG_DIM)
    conv_w = jnp.transpose(conv_w, (1, 0, 2)).reshape(3, 2 * D_FF)
    lb_a0, lb_a1 = hg_lb[:, 0, :], hg_lb[:, 1, :]

    order = ("w_out", "w_xq", "w_xkv", "w_xo", "w_up", "w_down")
    w_in_t, *later = lax.optimization_barrier((w_in_t, *[wire[n] for n in order]))
    fetch, started = _gather2_begin(later, "w")
    take = lambda n, after: _rows(_gather2_end(fetch, order.index(n), after))

    h1, p = _norm_mm(x, w["pre_mix_g"], w_in_t, True, N_IN, "in_proj", after=[started])
    qr, kr = _qk_prep(p, gq2, gk2, tables, bd)
    att, att32, lse = _attn_fwd(qr, kr, p)
    passed = _gather2_forward(fetch, [0, 1, 2, 3], att, "a")
    lb = _lower_bounds(lb_a0, lb_a1)
    of, s_f = _hgrn_fwd(p, lb[0:1], False, after=[passed])
    ob, s_b = _hgrn_fwd(p, lb[1:2], True, after=[passed, of])
    passed = _gather2_forward(fetch, [4, 5], ob, "b")
    w_out = take("w_out", ob)
    cat, mixed, x1 = _mix_out(att, of, ob, p, w["hg_out_norm_g"], w_out, w["post_mix_g"], x, after=[passed])

    w_xq = take("w_xq", x1)
    h2, q2 = _norm_mm(x1, w["pre_x_g"], w_xq, False, 1024, "xq_proj")
    w_xkv_t = take("w_xkv", q2)
    mn, kv = _norm_mm(mem, w["mem_norm_g"], w_xkv_t, True, 2 * D_MODEL, "xkv_proj")
    o2 = _xattn_fwd(q2, kv)
    w_xo = take("w_xo", o2)
    y2, x2 = _mm_postnorm_res(o2, w_xo, w["post_x_g"], x1, "xo_proj")

    w_up_t = take("w_up", x2)
    h3, u = _norm_mm(x2, w["pre_ffn_g"], w_up_t, True, 2 * D_FF, "up_proj", tm=256)
    a = _conv_gate_fwd(u, conv_w, w["conv_b"])
    w_down = take("w_down", a)
    y3, dx3, loss = _mm_postnorm_res_loss(a, w_down, w["post_ffn_g"], x2, tgt, "down_proj")

    g, pending = {}, {}
    dy3, da, g["post_ffn_g"] = _postnorm_bwd_mm(dx3, y3, w["post_ffn_g"], w_down, "down_bwd", BF16)
    pending["down"], started = _exchange_begin([_slots(_dw(a, dy3, "dw_down"))], "down")
    (du_g, dcw_g, dcb_g), (du_v, dcw_v, dcb_v) = _conv_gate_bwd(u, conv_w, w["conv_b"], da, after=[started])
    g["conv_w"] = jnp.concatenate([dcw_g, dcw_v], axis=1)
    g["conv_b"] = jnp.concatenate([dcb_g, dcb_v], axis=1)
    pending["up"], started = _exchange_begin([_slots(_dw([du_g, du_v], h3, "dw_up"))], "up")
    dx2, g["pre_ffn_g"] = _mm_prenorm_bwd([du_g, du_v], w_up_t, True, x2, w["pre_ffn_g"], dx3, "up_bwd", after=[started], tm=256)

    dy2, do2, g["post_x_g"] = _postnorm_bwd_mm(dx2, y2, w["post_x_g"], w_xo, "xo_bwd", BF16)
    dw_xo = _dw(o2, dy2, "dw_xo")
    dq2, dkv = _xattn_bwd(q2, kv, do2)
    dw_xq = _dw(h2, dq2, "dw_xq")
    dkvb = _bf(dkv)
    dw_xkv = _dw(dkvb, mn, "dw_xkv")
    pending["x"], started = _exchange_begin([_slots(dw_xo), _slots(dw_xq), _slots(dw_xkv)], "x")
    _, g["mem_norm_g"] = _mm_prenorm_bwd(dkvb, w_xkv_t, True, mem, w["mem_norm_g"], jnp.zeros_like(mem), "xkv_bwd")
    dx1, g["pre_x_g"] = _mm_prenorm_bwd(dq2, w_xq, False, x1, w["pre_x_g"], dx2, "xq_bwd", after=[started])

    dmixed, dcat, g["post_mix_g"] = _postnorm_bwd_mm(dx1, mixed, w["post_mix_g"], w_out, "out_bwd")
    pending["out"], started = _exchange_begin([_slots(_dw(cat, dmixed, "dw_out"))], "out")
    do, dhg, g["hg_out_norm_g"] = _rec_bwd(dcat, of, ob, p, w["hg_out_norm_g"])
    dhq_f, dz_f, dhi_f, dlb_f = _hgrn_bwd(p, lb[0:1], do, s_f, False, after=[started])
    dhq_b, dz_b, dhi_b, dlb_b = _hgrn_bwd(p, lb[1:2], do, s_b, True)
    d_a0, d_a1 = _lower_bounds_bwd(lb_a0, lb_a1, jnp.concatenate([dlb_f, dlb_b], axis=0))
    g["hg_lb"] = jnp.stack([d_a0, d_a1], axis=1)
    dqr, dkr, dv = _attn_bwd(qr, kr, p, dcat, att32, lse)
    dp_qk, dgq, dgk = _qk_prep_bwd(p, dqr, dkr, gq2, gk2, tables, bd)
    g["q_norm_g"], g["k_norm_g"] = dgq, dgk
    dp = _assemble_dp(dp_qk, dv, dhq_f, dhq_b, dz_f, dz_b, dhi_f, dhi_b, dhg)
    pending["in"], started = _exchange_begin([_slots(_dw(dp, h1, "dw_in"))], "in")
    dx, g["pre_mix_g"] = _mm_prenorm_bwd(dp, w_in_t, True, x, w["pre_mix_g"], dx1, "in_bwd", after=[started])
    return loss, dx, g, pending


_COL_SHARDED = ("w_in", "w_xkv", "w_up")
_ROW_SHARDED = ("w_out", "w_xq", "w_xo", "w_down")
_REPLICATED = ("pre_mix_g", "q_norm_g", "k_norm_g", "hg_out_norm_g", "post_mix_g", "pre_x_g", "mem_norm_g", "post_x_g",
               "pre_ffn_g", "conv_b", "post_ffn_g")
_WEIGHTS = ("pre_mix_g", "w_in", "q_norm_g", "k_norm_g", "hg_lb", "hg_out_norm_g", "w_out", "post_mix_g", "pre_x_g",
            "mem_norm_g", "w_xq", "w_xkv", "w_xo", "post_x_g", "pre_ffn_g", "w_up", "conv_w", "conv_b", "w_down",
            "post_ffn_g")
_ADAM_TRANSPOSED = ("w_in", "w_up")
PACK_W = 1024


def _small_plan(shapes):
    plan, r = [], 0
    for vi, (rows, cols) in enumerate(shapes):
        for i in range(rows):
            for c0 in range(0, cols, PACK_W):
                plan.append((vi, i, c0, min(PACK_W, cols - c0), r))
                r += 1
    return plan, -(-r // 8) * 8


def _pack_small(vals):
    plan, nrows = _small_plan([val.shape for val in vals])

    def body(*refs):
        ins, out = refs[:-1], refs[-1]
        out[...] = jnp.zeros_like(out)
        for vi, i, c0, width, r in plan:
            out[r:r + 1, 0:width] = ins[vi][i:i + 1, c0:c0 + width]

    return pl.pallas_call(body, name="pack_small", out_shape=jax.ShapeDtypeStruct((nrows, PACK_W), F32))(*vals)


def _sum_unpack_small(packs, shapes):
    plan, _ = _small_plan(shapes)

    def body(*refs):
        p_ref, outs = refs[0], refs[1:]
        acc = p_ref[0]
        for i in range(1, N_DEV):
            acc = acc + p_ref[i]
        for vi, i, c0, width, r in plan:
            outs[vi][i:i + 1, c0:c0 + width] = acc[r:r + 1, 0:width]

    return pl.pallas_call(body, name="sum_unpack_small", out_shape=[jax.ShapeDtypeStruct(s, F32) for s in shapes])(packs)


def _adamw_many(ws, gs, ms, vs):
    n = len(ws)

    def body(*refs):
        w_refs, g_refs, m_refs, v_refs = (refs[k * n:(k + 1) * n] for k in range(4))
        d_refs, nm_refs, nv_refs = (refs[(4 + k) * n:(5 + k) * n] for k in range(3))
        for k in range(n):
            g_ = g_refs[k][...]
            m_ = ADAM_B1 * m_refs[k][...] + (1.0 - ADAM_B1) * g_
            v_ = ADAM_B2 * v_refs[k][...] + (1.0 - ADAM_B2) * (g_ * g_)
            m_hat = m_ / (1.0 - ADAM_B1 ** ADAM_STEP)
            v_hat = v_ / (1.0 - ADAM_B2 ** ADAM_STEP)
            d_refs[k][...] = -ADAM_LR * (m_hat / (jnp.sqrt(v_hat) + ADAM_EPS) + ADAM_WD * w_refs[k][...])
            nm_refs[k][...] = m_
            nv_refs[k][...] = v_

    shapes = [jax.ShapeDtypeStruct(a.shape, F32) for a in ws]
    outs = pl.pallas_call(body, name="adamw_small", out_shape=shapes * 3)(*ws, *gs, *ms, *vs)
    return outs[:n], outs[n:2 * n], outs[2 * n:]


def kernel(x, mem, pre_mix_g, w_in, q_norm_g, k_norm_g, hg_lb, hg_out_norm_g, w_out, post_mix_g, pre_x_g, mem_norm_g, w_xq, w_xkv, w_xo, post_x_g, pre_ffn_g, w_up, conv_w, conv_b, w_down, post_ffn_g, loss_target, m_pre_mix_g, m_w_in, m_q_norm_g, m_k_norm_g, m_hg_lb, m_hg_out_norm_g, m_w_out, m_post_mix_g, m_pre_x_g, m_mem_norm_g, m_w_xq, m_w_xkv, m_w_xo, m_post_x_g, m_pre_ffn_g, m_w_up, m_conv_w, m_conv_b, m_w_down, m_post_ffn_g, v_pre_mix_g, v_w_in, v_q_norm_g, v_k_norm_g, v_hg_lb, v_hg_out_norm_g, v_w_out, v_post_mix_g, v_pre_x_g, v_mem_norm_g, v_w_xq, v_w_xkv, v_w_xo, v_post_x_g, v_pre_ffn_g, v_w_up, v_conv_w, v_conv_b, v_w_down, v_post_ffn_g):
    args = dict(locals())
    w = {n: args[n] for n in _WEIGHTS}
    m = {n: args["m_" + n] for n in _WEIGHTS}
    v = {n: args["v_" + n] for n in _WEIGHTS}
    me = _dev_index(*_mesh_pos())

    wire = {n: _bf(w[n][0].T) for n in _COL_SHARDED}
    wire.update({n: _bf(w[n][0]) for n in _ROW_SHARDED})

    loss, grad_x, g, pending = _local_step(x[0], mem[0], loss_target[0], w, wire)

    grads, delta, new_m, new_v = {}, {}, {}, {}

    me_arr = jnp.reshape(me, (1,)).astype(jnp.int32)

    def update(n, parts):
        if n in _ADAM_TRANSPOSED:
            outs = _sum_adamw(*parts, me_arr, w[n][0].T, m[n][0].T, v[n][0].T, "update_" + n)
            grads[n], delta[n], new_m[n], new_v[n] = (a.T[None] for a in outs)
        elif n in _COL_SHARDED:
            gsum = _sum_parts(*parts, me_arr, "sum_" + n).T
            grads[n] = gsum[None]
            delta[n], new_m[n], new_v[n] = (a[None] for a in _adamw(w[n][0], gsum, m[n][0], v[n][0], "adamw_" + n))
        else:
            outs = _sum_adamw(*parts, me_arr, w[n][0], m[n][0], v[n][0], "update_" + n)
            grads[n], delta[n], new_m[n], new_v[n] = (a[None] for a in outs)

    small = list(_REPLICATED) + ["hg_lb", "conv_w"]
    vals = [g[n] for n in _REPLICATED] + [g["hg_lb"].reshape(4, HG_DIM), g["conv_w"], jnp.pad(loss, ((0, 0), (0, LANE - 1)))]
    shapes = [val.shape for val in vals]
    fetch_small, _ = _gather_begin([_pack_small(vals)], "small")

    after = grad_x
    for tag, names in (("down", ["w_down"]), ("up", ["w_up"]), ("x", ["w_xo", "w_xq", "w_xkv"]), ("out", ["w_out"]),
                       ("in", ["w_in"])):
        for n, parts in zip(names, _exchange_end(pending[tag], after)):
            update(n, parts)
            after = new_v[n]

    (packs,) = _gather_end(fetch_small, [0], after)
    summed = _sum_unpack_small(packs, shapes)
    loss = summed[-1][0, 0]
    for n, s in zip(small, summed[:-1]):
        grads[n] = s
    fold = lambda v2: v2[:, :ATT_HEAD_DIM] + v2[:, ATT_HEAD_DIM:]
    grads["q_norm_g"], grads["k_norm_g"] = fold(grads["q_norm_g"]), fold(grads["k_norm_g"])
    grads["hg_lb"] = lax.dynamic_slice_in_dim(grads["hg_lb"].reshape(2, 2, HG_DIM), me * (HG_DIM // N_DEV),
                                              HG_DIM // N_DEV, axis=2)
    grads["conv_w"] = lax.dynamic_slice_in_dim(grads["conv_w"], me * (2 * D_FF // N_DEV), 2 * D_FF // N_DEV, axis=1)[None]

    flat2 = lambda a: a.reshape(-1, a.shape[-1])
    outs = _adamw_many(*[[flat2(d[n]) for n in small] for d in (w, grads, m, v)])
    for dst, vals in zip((delta, new_m, new_v), outs):
        for n, val in zip(small, vals):
            dst[n] = val.reshape(w[n].shape)

    return (loss, grad_x[None], *[grads[n] for n in _WEIGHTS], *[delta[n] for n in _WEIGHTS],
            *[new_m[n] for n in _WEIGHTS], *[new_v[n] for n in _WEIGHTS])
```

```python
import jax
import jax.numpy as jnp
from jax import lax
from jax.experimental import pallas as pl
from jax.experimental.pallas import tpu as pltpu

F32 = jnp.float32
BF16 = jnp.bfloat16

D_MODEL = 1024
GRID_W = 64
EPS = 1e-6
ATT_HEADS = 8
ATT_HEAD_DIM = 64
ATT_Q_DIM = 512
ATT_KV_DIM = 128
ROPE_THETA = 10000.0
HG_HEADS = 4
HG_DIM = 512
HG_CHUNK = 32
HG_CHUNK_LOG2 = 5
HG_BLOCK_FWD = 256
HG_BLOCK_BWD = 128
N_IN = 3328
X_HEADS = 4
X_HEAD_DIM = 256
D_FF = 2816
N_DEV = 8
LANE = 128
ADAM_LR = 0.001
ADAM_B1 = 0.9
ADAM_B2 = 0.999
ADAM_EPS = 1e-08
ADAM_WD = 0.01
ADAM_STEP = 10
VMEM_LIMIT = 56 * 1024 * 1024

MESH_T = pl.DeviceIdType.MESH


def _params(**kw):
    return pltpu.CompilerParams(vmem_limit_bytes=VMEM_LIMIT, **kw)


def _dot(a, b, ca, cb):
    return lax.dot_general(a, b, (((ca,), (cb,)), ((), ())), preferred_element_type=F32)


def _bf(x):
    return x.astype(BF16)


def _sigmoid(x):
    return 1.0 / (1.0 + jnp.exp(-x))


def _rms_fwd(x, g):
    r = lax.rsqrt(jnp.mean(x * x, axis=-1, keepdims=True) + EPS)
    return x * r * g


def _rms_bwd(dy, x, g):
    r = lax.rsqrt(jnp.mean(x * x, axis=-1, keepdims=True) + EPS)
    xh = x * r
    dg = jnp.sum(dy * xh, axis=0, keepdims=True)
    t = dy * g
    dx = r * (t - xh * jnp.mean(t * xh, axis=-1, keepdims=True))
    return dx, dg


def _full(shape):
    nd = len(shape)
    return pl.BlockSpec(shape, lambda *a: (0,) * nd)


def _norm_mm(x, g, w, trans, tn, name, after=(), tm=512):
    t, d = x.shape
    n = w.shape[0] if trans else w.shape[1]
    tm = min(tm, t)

    def body(x_ref, g_ref, w_ref, h_ref, p_ref):
        h = _bf(_rms_fwd(x_ref[...], g_ref[...]))
        h_ref[...] = h
        p_ref[...] = _dot(h, w_ref[...], 1, 1 if trans else 0)

    w_spec = pl.BlockSpec((tn, d), lambda i, j: (j, 0)) if trans else pl.BlockSpec((d, tn), lambda i, j: (0, j))
    return _call(
        body, (x, g, w), name=name, grid=(t // tm, n // tn),
        in_specs=[pl.BlockSpec((tm, d), lambda i, j: (i, 0)), _full((1, d)), w_spec],
        out_specs=[pl.BlockSpec((tm, d), lambda i, j: (i, 0)), pl.BlockSpec((tm, tn), lambda i, j: (i, j))],
        out_shape=[jax.ShapeDtypeStruct((t, d), BF16), jax.ShapeDtypeStruct((t, n), F32)], after=after)


def _mm_postnorm_res(a, w, g, res, name, tm=256):
    t, k = a.shape
    d = w.shape[1]

    def body(a_ref, w_ref, g_ref, res_ref, y_ref, o_ref):
        y = _dot(a_ref[...], w_ref[...], 1, 0)
        y_ref[...] = y
        o_ref[...] = res_ref[...] + _rms_fwd(y, g_ref[...])

    row = lambda width: pl.BlockSpec((tm, width), lambda i: (i, 0))
    return pl.pallas_call(
        body, name=name, grid=(t // tm,),
        in_specs=[row(k), _full((k, d)), _full((1, d)), row(d)],
        out_specs=[row(d), row(d)],
        out_shape=[jax.ShapeDtypeStruct((t, d), F32)] * 2,
        compiler_params=_params(),
    )(a, w, g, res)


def _mm_postnorm_res_loss(a, w, g, res, tgt, name, tm=256):
    t, k = a.shape
    d = w.shape[1]

    def body(a_ref, w_ref, g_ref, res_ref, tgt_ref, y_ref, dout_ref, loss_ref):
        @pl.when(pl.program_id(0) == 0)
        def _():
            loss_ref[...] = jnp.zeros_like(loss_ref)

        y = _dot(a_ref[...], w_ref[...], 1, 0)
        y_ref[...] = y
        diff = res_ref[...] + _rms_fwd(y, g_ref[...]) - tgt_ref[...]
        dout_ref[...] = diff * (1.0 / d)
        part = jnp.sum(jnp.sum(diff * diff, axis=-1, keepdims=True), axis=0, keepdims=True)
        loss_ref[...] += (0.5 / d) * part

    row = lambda width: pl.BlockSpec((tm, width), lambda i: (i, 0))
    return pl.pallas_call(
        body, name=name, grid=(t // tm,),
        in_specs=[row(k), _full((k, d)), _full((1, d)), row(d), row(d)],
        out_specs=[row(d), row(d), _full((1, 1))],
        out_shape=[jax.ShapeDtypeStruct((t, d), F32)] * 2 + [jax.ShapeDtypeStruct((1, 1), F32)],
        compiler_params=_params(),
    )(a, w, g, res, tgt)


def _postnorm_bwd_mm(dout, y, g, w, name, da_dtype=F32, tm=256):
    t, d = y.shape
    k = w.shape[0]

    def body(dout_ref, y_ref, g_ref, w_ref, dy_ref, da_ref, dg_ref):
        @pl.when(pl.program_id(0) == 0)
        def _():
            dg_ref[...] = jnp.zeros_like(dg_ref)

        dy, dg = _rms_bwd(dout_ref[...], y_ref[...], g_ref[...])
        dg_ref[...] += dg
        dyb = _bf(dy)
        dy_ref[...] = dyb
        da_ref[...] = _dot(dyb, w_ref[...], 1, 1).astype(da_dtype)

    row = lambda width: pl.BlockSpec((tm, width), lambda i: (i, 0))
    return pl.pallas_call(
        body, name=name, grid=(t // tm,),
        in_specs=[row(d), row(d), _full((1, d)), _full((k, d))],
        out_specs=[row(d), row(k), _full((1, d))],
        out_shape=[jax.ShapeDtypeStruct((t, d), BF16), jax.ShapeDtypeStruct((t, k), da_dtype), jax.ShapeDtypeStruct((1, d), F32)],
        compiler_params=_params(),
    )(dout, y, g, w)


def _mm_prenorm_bwd(dp, w, trans, x, g, dres, name, after=(), tm=512):
    dps = list(dp) if isinstance(dp, (list, tuple)) else [dp]
    nparts = len(dps)
    t = dps[0].shape[0]
    widths = [part.shape[1] for part in dps]
    d = x.shape[1]
    tm = min(tm, t)

    def body(*refs):
        dp_refs = refs[:nparts]
        w_ref, x_ref, g_ref, dres_ref, dx_ref, dg_ref = refs[nparts:]

        @pl.when(pl.program_id(0) == 0)
        def _():
            dg_ref[...] = jnp.zeros_like(dg_ref)

        dh = None
        for k, dp_ref in enumerate(dp_refs):
            cols = slice(sum(widths[:k]), sum(widths[:k + 1]))
            term = _dot(dp_ref[...], w_ref[cols, :], 1, 0) if trans else _dot(dp_ref[...], w_ref[:, cols], 1, 1)
            dh = term if dh is None else dh + term
        dx, dg = _rms_bwd(dh, x_ref[...], g_ref[...])
        dg_ref[...] += dg
        dx_ref[...] = dres_ref[...] + dx

    row = lambda width: pl.BlockSpec((tm, width), lambda i: (i, 0))
    return _call(
        body, (*dps, w, x, g, dres), name=name, grid=(t // tm,),
        in_specs=[row(width) for width in widths] + [_full(w.shape), row(d), _full((1, d)), row(d)],
        out_specs=[row(d), _full((1, d))],
        out_shape=[jax.ShapeDtypeStruct((t, d), F32), jax.ShapeDtypeStruct((1, d), F32)], after=after)


def _dw(a, b, name, tka=256):
    parts = list(a) if isinstance(a, (list, tuple)) else [a]
    nparts = len(parts)
    t, nb = b.shape
    tiles = [part.shape[1] // tka for part in parts]
    first = [sum(tiles[:k]) for k in range(nparts)]

    def body(*refs):
        a_refs, b_ref, o_ref = refs[:nparts], refs[nparts], refs[nparts + 1]
        i = pl.program_id(0)
        for k, a_ref in enumerate(a_refs):
            @pl.when((i >= first[k]) & (i < first[k] + tiles[k]))
            def _():
                o_ref[...] = _bf(_dot(a_ref[...], b_ref[...], 0, 0))

    a_spec = lambda k: pl.BlockSpec((t, tka), lambda i: (0, jnp.clip(i - first[k], 0, tiles[k] - 1)))
    return pl.pallas_call(
        body, name=name, grid=(sum(tiles),),
        in_specs=[a_spec(k) for k in range(nparts)] + [_full((t, nb))],
        out_specs=pl.BlockSpec((tka, nb), lambda i: (i, 0)),
        out_shape=jax.ShapeDtypeStruct((sum(tiles) * tka, nb), BF16),
        compiler_params=_params(),
    )(*parts, b)


def _rope_tables(t):
    pos = jnp.arange(t)
    r = (pos // GRID_W).astype(F32)
    c = (pos % GRID_W).astype(F32)
    npair = ATT_HEAD_DIM // 4
    inv = jnp.power(ROPE_THETA, -jnp.arange(npair, dtype=F32) / npair)
    ang = jnp.concatenate([r[:, None] * inv, c[:, None] * inv], axis=-1)
    cos = jnp.repeat(jnp.cos(ang), 2, axis=-1)
    sin = jnp.repeat(jnp.sin(ang), 2, axis=-1)
    even = (jnp.arange(ATT_HEAD_DIM) % 2) == 0
    sa = jnp.where(even, -sin, 0.0)
    sb = jnp.where(even, 0.0, sin)
    two = lambda a: jnp.tile(a, (1, 2))
    return two(cos), two(sa), two(sb)


def _head_sum_matrix():
    a = jnp.arange(LANE) // ATT_HEAD_DIM
    return (a[:, None] == a[None, :]).astype(BF16)


def _head_mean(v, bd):
    hi = _bf(v)
    lo = _bf(v - hi.astype(F32))
    return (_dot(hi, bd, 1, 0) + _dot(lo, bd, 1, 0)) * (1.0 / ATT_HEAD_DIM)


def _qk_prep(p, gq, gk, tables, bd, tm=512):
    t = p.shape[0]
    tm = min(tm, t)
    cc, sa, sb = tables

    def body(p_ref, gq_ref, gk_ref, cc_ref, sa_ref, sb_ref, bd_ref, q_ref, k_ref):
        cc_, sa_, sb_, bd_ = cc_ref[...], sa_ref[...], sb_ref[...], bd_ref[...]
        low = lax.broadcasted_iota(jnp.int32, (tm, LANE), 1) < ATT_HEAD_DIM

        def normrope(xs, g):
            xn = xs * lax.rsqrt(_head_mean(xs * xs, bd_) + EPS) * g
            return xn * cc_ + pltpu.roll(xn, LANE - 1, 1) * sa_ + pltpu.roll(xn, 1, 1) * sb_

        for j in range(4):
            y = normrope(p_ref[:, j * LANE:(j + 1) * LANE], gq_ref[...]) * (ATT_HEAD_DIM ** -0.5)
            yr = pltpu.roll(y, ATT_HEAD_DIM, 1)
            if j // 2 == 0:
                h0, h1 = jnp.where(low, y, 0.0), jnp.where(low, yr, 0.0)
            else:
                h0, h1 = jnp.where(low, 0.0, yr), jnp.where(low, 0.0, y)
            q_ref[:, (2 * j) * LANE:(2 * j + 1) * LANE] = _bf(h0)
            q_ref[:, (2 * j + 1) * LANE:(2 * j + 2) * LANE] = _bf(h1)
        k_ref[...] = _bf(normrope(p_ref[:, ATT_Q_DIM:ATT_Q_DIM + LANE], gk_ref[...]))

    row = lambda width: pl.BlockSpec((tm, width), lambda i: (i, 0))
    return pl.pallas_call(
        body, name="qk_prep", grid=(t // tm,),
        in_specs=[row(ATT_Q_DIM + LANE), _full((1, LANE)), _full((1, LANE)), row(LANE), row(LANE), row(LANE),
                  _full((LANE, LANE))],
        out_specs=[row(ATT_HEADS * LANE), row(LANE)],
        out_shape=[jax.ShapeDtypeStruct((t, ATT_HEADS * LANE), BF16), jax.ShapeDtypeStruct((t, LANE), BF16)],
        compiler_params=_params(),
    )(p, gq, gk, cc, sa, sb, bd)


def _qk_prep_bwd(p, dq, dk, dv, gq, gk, tables, bd, tm=512):
    t = p.shape[0]
    tm = min(tm, t)
    cc, sa, sb = tables
    att_w = ATT_Q_DIM + 2 * ATT_KV_DIM

    def body(p_ref, dq_ref, dk_ref, dv_ref, gq_ref, gk_ref, cc_ref, sa_ref, sb_ref, bd_ref, dp_ref, dgq_ref, dgk_ref):
        dp_ref[:, ATT_Q_DIM + LANE:att_w] = _bf(dv_ref[...])

        @pl.when(pl.program_id(0) == 0)
        def _():
            dgq_ref[...] = jnp.zeros_like(dgq_ref)
            dgk_ref[...] = jnp.zeros_like(dgk_ref)

        cc_, sa_, sb_, bd_ = cc_ref[...], sa_ref[...], sb_ref[...], bd_ref[...]
        low = lax.broadcasted_iota(jnp.int32, (tm, LANE), 1) < ATT_HEAD_DIM

        def bwd(xs, g, dy):
            r = lax.rsqrt(_head_mean(xs * xs, bd_) + EPS)
            xh = xs * r
            dxn = dy * cc_ + pltpu.roll(dy * sa_, 1, 1) + pltpu.roll(dy * sb_, LANE - 1, 1)
            dg = jnp.sum(dxn * xh, axis=0, keepdims=True)
            tt = dxn * g
            return r * (tt - xh * _head_mean(tt * xh, bd_)), dg

        dgq = jnp.zeros((1, LANE), F32)
        for j in range(4):
            d0 = dq_ref[:, (2 * j) * LANE:(2 * j + 1) * LANE]
            d1 = dq_ref[:, (2 * j + 1) * LANE:(2 * j + 2) * LANE]
            if j // 2 == 0:
                dy = jnp.where(low, d0, pltpu.roll(d1, ATT_HEAD_DIM, 1))
            else:
                dy = jnp.where(low, pltpu.roll(d0, ATT_HEAD_DIM, 1), d1)
            dx, dg = bwd(p_ref[:, j * LANE:(j + 1) * LANE], gq_ref[...], dy * (ATT_HEAD_DIM ** -0.5))
            dp_ref[:, j * LANE:(j + 1) * LANE] = _bf(dx)
            dgq = dgq + dg
        dgq_ref[...] += dgq
        dx, dg = bwd(p_ref[:, ATT_Q_DIM:ATT_Q_DIM + LANE], gk_ref[...], dk_ref[...])
        dp_ref[:, ATT_Q_DIM:ATT_Q_DIM + LANE] = _bf(dx)
        dgk_ref[...] += dg

    row = lambda width: pl.BlockSpec((tm, width), lambda i: (i, 0))
    return pl.pallas_call(
        body, name="qk_prep_bwd", grid=(t // tm,),
        in_specs=[row(ATT_Q_DIM + LANE), row(ATT_HEADS * LANE), row(LANE), row(LANE), _full((1, LANE)), _full((1, LANE)),
                  row(LANE), row(LANE), row(LANE), _full((LANE, LANE))],
        out_specs=[row(att_w), _full((1, LANE)), _full((1, LANE))],
        out_shape=[jax.ShapeDtypeStruct((t, att_w), BF16), jax.ShapeDtypeStruct((1, LANE), F32),
                   jax.ShapeDtypeStruct((1, LANE), F32)],
        compiler_params=_params(),
    )(p, dq, dk, dv, gq, gk, cc, sa, sb, bd)


def _attn_fwd(q, k, p, tq=256):
    t = k.shape[0]
    tq = min(tq, t)
    v_blk = (ATT_Q_DIM + ATT_KV_DIM) // LANE

    def body(q_ref, k_ref, v_ref, o_ref, o32_ref, lse_ref):
        k_ = k_ref[...]
        v = v_ref[...]
        lane_k = lax.broadcasted_iota(jnp.int32, (t, LANE), 1)
        lane_q = lax.broadcasted_iota(jnp.int32, (tq, LANE), 1)
        lowk, lowq = lane_k < ATT_HEAD_DIM, lane_q < ATT_HEAD_DIM
        ones_lane = (ATT_HEAD_DIM, 0)
        vm = (_bf(jnp.where(lowk, v, jnp.where(lane_k == ones_lane[0], 1.0, 0.0))),
              _bf(jnp.where(lowk, jnp.where(lane_k == ones_lane[1], 1.0, 0.0), v)))
        for j in range(4):
            kvh = j // 2
            acc = None
            for sub in range(2):
                h = 2 * j + sub
                s = _dot(q_ref[:, h * LANE:(h + 1) * LANE], k_, 1, 1)
                mx = jnp.max(s, axis=-1, keepdims=True)
                ov = _dot(jnp.exp(_bf(s - mx)), vm[kvh], 1, 0)
                l = jnp.sum(jnp.where(lane_q == ones_lane[kvh], ov, 0.0), axis=-1, keepdims=True)
                lse_ref[h] = mx + jnp.log(l)
                o = jnp.where(lowq if kvh == 0 else ~lowq, ov, 0.0) * (1.0 / l)
                if sub != kvh:
                    o = pltpu.roll(o, ATT_HEAD_DIM, 1)
                acc = o if acc is None else acc + o
            o32_ref[:, j * LANE:(j + 1) * LANE] = acc
            o_ref[:, j * LANE:(j + 1) * LANE] = _bf(acc)

    row = pl.BlockSpec((tq, ATT_Q_DIM), lambda i: (i, 0))
    return _call(
        body, (q, k, p), name="attn_fwd", grid=(t // tq,),
        in_specs=[pl.BlockSpec((tq, ATT_HEADS * LANE), lambda i: (i, 0)), _full((t, LANE)),
                  pl.BlockSpec((t, LANE), lambda i: (0, v_blk))],
        out_specs=[row, row, pl.BlockSpec((ATT_HEADS, tq, 1), lambda i: (0, i, 0))],
        out_shape=[jax.ShapeDtypeStruct((t, ATT_Q_DIM), BF16), jax.ShapeDtypeStruct((t, ATT_Q_DIM), F32),
                   jax.ShapeDtypeStruct((ATT_HEADS, t, 1), F32)])


def _attn_bwd(q, k, p, dcat, o32, lse, tq=256):
    t = k.shape[0]
    tq = min(tq, t)
    v_blk = (ATT_Q_DIM + ATT_KV_DIM) // LANE

    def body(q_ref, k_ref, v_ref, do_ref, o_ref, lse_ref, dq_ref, dk_ref, dv_ref):
        @pl.when(pl.program_id(0) == 0)
        def _():
            dk_ref[...] = jnp.zeros_like(dk_ref)
            dv_ref[...] = jnp.zeros_like(dv_ref)

        k_ = k_ref[...]
        vb = _bf(v_ref[...])
        lowq = lax.broadcasted_iota(jnp.int32, (tq, LANE), 1) < ATT_HEAD_DIM
        dk_acc = jnp.zeros((t, LANE), F32)
        dv_acc = jnp.zeros((t, LANE), F32)
        for j in range(4):
            kvh = j // 2
            dop = do_ref[:, j * LANE:(j + 1) * LANE]
            prod = dop * o_ref[:, j * LANE:(j + 1) * LANE]
            d_low = jnp.sum(jnp.where(lowq, prod, 0.0), axis=-1, keepdims=True)
            d_sub = (d_low, jnp.sum(prod, axis=-1, keepdims=True) - d_low)
            for sub in range(2):
                h = 2 * j + sub
                src = dop if sub == kvh else pltpu.roll(dop, ATT_HEAD_DIM, 1)
                do_h = _bf(jnp.where(lowq, src, 0.0) if kvh == 0 else jnp.where(lowq, 0.0, src))
                qh = q_ref[:, h * LANE:(h + 1) * LANE]
                pr = jnp.exp(_bf(_dot(qh, k_, 1, 1) - lse_ref[h]))
                ds = pr * _bf(_dot(do_h, vb, 1, 1) - d_sub[sub])
                dq_ref[:, h * LANE:(h + 1) * LANE] = _dot(ds, k_, 1, 0)
                dk_acc = dk_acc + _dot(ds, qh, 0, 0)
                dv_acc = dv_acc + _dot(pr, do_h, 0, 0)
        dk_ref[...] += dk_acc
        dv_ref[...] += dv_acc

    row = pl.BlockSpec((tq, ATT_Q_DIM), lambda i: (i, 0))
    return _call(
        body, (q, k, p, dcat, o32, lse), name="attn_bwd", grid=(t // tq,),
        in_specs=[pl.BlockSpec((tq, ATT_HEADS * LANE), lambda i: (i, 0)), _full((t, LANE)),
                  pl.BlockSpec((t, LANE), lambda i: (0, v_blk)), row, row,
                  pl.BlockSpec((ATT_HEADS, tq, 1), lambda i: (0, i, 0))],
        out_specs=[pl.BlockSpec((tq, ATT_HEADS * LANE), lambda i: (i, 0)), _full((t, LANE)), _full((t, LANE))],
        out_shape=[jax.ShapeDtypeStruct((t, ATT_HEADS * LANE), F32), jax.ShapeDtypeStruct((t, LANE), F32),
                   jax.ShapeDtypeStruct((t, LANE), F32)])


def _lower_bounds(a0, a1):
    def body(a0_ref, a1_ref, lb_ref):
        m = jnp.maximum(a0_ref[...], a1_ref[...])
        e0, e1 = jnp.exp(a0_ref[...] - m), jnp.exp(a1_ref[...] - m)
        lb_ref[...] = e0 / (e0 + e1)

    return pl.pallas_call(body, name="lower_bounds", out_shape=jax.ShapeDtypeStruct(a0.shape, F32))(a0, a1)


def _lower_bounds_bwd(a0, a1, dlb):
    def body(a0_ref, a1_ref, dlb_ref, d0_ref, d1_ref):
        m = jnp.maximum(a0_ref[...], a1_ref[...])
        e0, e1 = jnp.exp(a0_ref[...] - m), jnp.exp(a1_ref[...] - m)
        lb = e0 / (e0 + e1)
        d0 = dlb_ref[...] * lb * (1.0 - lb)
        d0_ref[...] = d0
        d1_ref[...] = -d0

    return pl.pallas_call(body, name="lower_bounds_bwd", out_shape=[jax.ShapeDtypeStruct(a0.shape, F32)] * 2)(a0, a1, dlb)


def _chunk_scan(x, pos, down):
    n = x.shape[0]
    s = 1
    while s < HG_CHUNK:
        if down:
            x = x + jnp.where(pos >= s, pltpu.roll(x, s, 0), 0.0)
        else:
            x = x + jnp.where(pos < HG_CHUNK - s, pltpu.roll(x, n - s, 0), 0.0)
        s *= 2
    return x


def _chunk_concat(x, nc):
    xb = _bf(x)
    rows = []
    for c in range(nc):
        pieces = [xb[c * HG_CHUNK:(c + 1) * HG_CHUNK, :]]
        if c:
            pieces.insert(0, jnp.zeros((HG_CHUNK, c * LANE), BF16))
        if c < nc - 1:
            pieces.append(jnp.zeros((HG_CHUNK, (nc - 1 - c) * LANE), BF16))
        rows.append(jnp.concatenate(pieces, axis=1) if len(pieces) > 1 else pieces[0])
    return jnp.concatenate(rows, axis=0) if nc > 1 else rows[0]


def _chunk_pick(x, nc):
    rows = [x[c * HG_CHUNK:(c + 1) * HG_CHUNK, c * LANE:(c + 1) * LANE] for c in range(nc)]
    return jnp.concatenate(rows, axis=0) if nc > 1 else rows[0]


def _chunk_rows(vals):
    rows = [jnp.broadcast_to(v, (HG_CHUNK, LANE)) for v in vals]
    return jnp.concatenate(rows, axis=0) if len(rows) > 1 else rows[0]


def _hgrn_gates(hq, z, lb):
    sq = _sigmoid(hq)
    sig = _sigmoid(z)
    f = lb + (1.0 - lb) * sig
    return hq * sq, sq, sig, f, jnp.log(f)


def _hgrn_local(q, f, g, v, pos, cid, amask, rev, nc):
    k = 1.0 - f
    ba = _chunk_scan(g, pos, not rev)
    bb = _chunk_scan(g, pos, rev)
    eq = 0.5 * (ba - bb + g)
    ex_q, ex_k, ex_i, ex_d = jnp.exp(eq), jnp.exp(-eq), jnp.exp(ba), jnp.exp(bb - g)
    qb, kb, qi, kd = q * ex_q, k * ex_k, q * ex_i, k * ex_d
    dvec = jnp.exp(ba + bb - g)
    a = jnp.where(amask, _dot(_bf(qb), _bf(kb), 1, 1), 0.0)
    kd_m = _chunk_concat(kd, nc)
    qi_m = _chunk_concat(qi, nc)
    ut_all = _dot(_bf(v), kd_m, 0, 0)
    return dict(k=k, ex_q=ex_q, ex_k=ex_k, ex_i=ex_i, ex_d=ex_d, qb=qb, kb=kb, qi=qi, kd=kd, dvec=dvec, a=a,
                kd_m=kd_m, qi_m=qi_m, ut_all=ut_all)


def _hgrn_masks(n, rev):
    row = lax.broadcasted_iota(jnp.int32, (n, LANE), 0)
    ti = lax.broadcasted_iota(jnp.int32, (n, n), 0)
    si = lax.broadcasted_iota(jnp.int32, (n, n), 1)
    tri = (si >= ti) if rev else (si <= ti)
    same = jnp.right_shift(ti, HG_CHUNK_LOG2) == jnp.right_shift(si, HG_CHUNK_LOG2)
    return jnp.bitwise_and(row, HG_CHUNK - 1), jnp.right_shift(row, HG_CHUNK_LOG2), same & tri


def _hgrn_specs(t, rev, bwd):
    n = min(HG_BLOCK_BWD if bwd else HG_BLOCK_FWD, t)
    nb = t // n
    if rev != bwd:
        blk = lambda i: nb - 1 - i
    else:
        blk = lambda i: i
    col = lambda base: pl.BlockSpec((n, 2 * LANE), lambda hp, i: (blk(i), base + hp))
    return n, nb, blk, col


def _hgrn_fwd(p, lb, rev, after=()):
    t = p.shape[0]
    n, nb, blk, col = _hgrn_specs(t, rev, False)
    nc = n // HG_CHUNK
    sub = n // min(HG_BLOCK_BWD, t)
    z_base = 7 if rev else 5

    def body(hq_ref, z_ref, hi_ref, lb_ref, o_ref, ssave_ref, st_scr):
        @pl.when(pl.program_id(1) == 0)
        def _():
            st_scr[...] = jnp.zeros_like(st_scr)

        pos, cid, amask = _hgrn_masks(n, rev)
        order = list(range(nc))[::-1] if rev else list(range(nc))
        for hh in range(2):
            sl = slice(hh * LANE, (hh + 1) * LANE)
            q, _, _, f, g = _hgrn_gates(hq_ref[:, sl], z_ref[:, sl], lb_ref[0:1, sl])
            v = hi_ref[:, sl]
            c_ = _hgrn_local(q, f, g, v, pos, cid, amask, rev, nc)
            st = st_scr[hh]
            cols = [None] * nc
            for c in order:
                cols[c] = st
                st = st * c_["dvec"][c * HG_CHUNK:c * HG_CHUNK + 1, :] + c_["ut_all"][:, c * LANE:(c + 1) * LANE]
            st_scr[hh] = st
            for s in range(sub):
                ssave_ref[sub - 1 - s if rev else s, sl, :] = cols[order[s * (nc // sub)]]
            st_all = _bf(jnp.concatenate(cols, axis=1))
            o_ref[:, sl] = _dot(_bf(c_["a"]), _bf(v), 1, 0) + _dot(c_["qi_m"], st_all, 1, 1)

    return _call(
        body, (p, p, p, lb), name="hgrn_fwd_rev" if rev else "hgrn_fwd", grid=(2, nb),
        in_specs=[col(3), col(z_base), col(9), pl.BlockSpec((1, 2 * LANE), lambda hp, i: (0, hp))],
        out_specs=[pl.BlockSpec((n, 2 * LANE), lambda hp, i: (blk(i), hp)),
                   pl.BlockSpec((sub, 2 * LANE, LANE), lambda hp, i: (blk(i), hp, 0))],
        out_shape=[jax.ShapeDtypeStruct((t, HG_DIM), F32), jax.ShapeDtypeStruct((nb * sub, HG_DIM, LANE), F32)],
        scratch_shapes=[pltpu.VMEM((2, LANE, LANE), F32)], after=after)


def _hgrn_bwd(p, lb, do, ssave, rev, other=None, after=()):
    t = p.shape[0]
    n, nb, blk, col = _hgrn_specs(t, rev, True)
    nc = n // HG_CHUNK
    z_base = 7 if rev else 5
    n_other = 0 if other is None else 2

    def body(*refs):
        hq_ref, z_ref, hi_ref, lb_ref, do_ref, ssave_ref = refs[:6]
        dhq_ref, dz_ref, dhi_ref, dlb_ref, dst_scr = refs[6 + n_other:]

        @pl.when(pl.program_id(1) == 0)
        def _():
            dst_scr[...] = jnp.zeros_like(dst_scr)
            dlb_ref[...] = jnp.zeros_like(dlb_ref)

        pos, cid, amask = _hgrn_masks(n, rev)
        order = list(range(nc))[::-1] if rev else list(range(nc))
        for hh in range(2):
            sl = slice(hh * LANE, (hh + 1) * LANE)
            hq, lbv = hq_ref[:, sl], lb_ref[0:1, sl]
            q, sq, sig, f, g = _hgrn_gates(hq, z_ref[:, sl], lbv)
            v = hi_ref[:, sl]
            c_ = _hgrn_local(q, f, g, v, pos, cid, amask, rev, nc)
            dvec, ut_all = c_["dvec"], c_["ut_all"]
            drow = lambda c: dvec[c * HG_CHUNK:c * HG_CHUNK + 1, :]
            st = ssave_ref[0, sl, :]
            cols = [None] * nc
            for c in order:
                cols[c] = st
                st = st * drow(c) + ut_all[:, c * LANE:(c + 1) * LANE]
            dob, vb = _bf(do_ref[:, sl]), _bf(v)
            gt_all = _dot(dob, c_["qi_m"], 0, 0)
            dcur = dst_scr[hh]
            dnext = [None] * nc
            ddrow = [None] * nc
            for c in order[::-1]:
                dnext[c] = dcur
                ddrow[c] = jnp.sum(cols[c] * dcur, axis=0, keepdims=True) * drow(c)
                dcur = dcur * drow(c) + gt_all[:, c * LANE:(c + 1) * LANE]
            dst_scr[hh] = dcur
            dsn_all = _bf(jnp.concatenate(dnext, axis=1))
            st_all = _bf(jnp.concatenate(cols, axis=1))
            da = _bf(jnp.where(amask, _dot(dob, vb, 1, 1), 0.0))
            dv = _dot(_bf(c_["a"]), dob, 0, 0) + _dot(c_["kd_m"], dsn_all, 1, 1)
            dqb = _dot(da, _bf(c_["kb"]), 1, 0)
            dkb = _dot(da, _bf(c_["qb"]), 0, 0)
            dqi = _chunk_pick(_dot(dob, st_all, 1, 0), nc)
            dkd = _chunk_pick(_dot(vb, dsn_all, 1, 0), nc)
            dq = dqb * c_["ex_q"] + dqi * c_["ex_i"]
            dk = dkb * c_["ex_k"] + dkd * c_["ex_d"]
            e = dqb * c_["qb"] - dkb * c_["kb"] + dqi * c_["qi"]
            w = dkd * c_["kd"]
            dg = _chunk_scan(e, pos, rev) + (_chunk_scan(w, pos, not rev) - w) + _chunk_rows(ddrow)
            df = dg / f - dk
            dz_ref[:, sl] = _bf(df * (1.0 - lbv) * sig * (1.0 - sig))
            dlb_ref[0:1, sl] += jnp.sum(df * (1.0 - sig), axis=0, keepdims=True)
            dhq = dq * (sq * (1.0 + hq * (1.0 - sq)))
            if other is None:
                dhq_ref[:, sl], dhi_ref[:, sl] = dhq, dv
            else:
                dhq_ref[:, sl], dhi_ref[:, sl] = _bf(dhq + refs[6][:, sl]), _bf(dv + refs[7][:, sl])

    out = pl.BlockSpec((n, 2 * LANE), lambda hp, i: (blk(i), hp))
    sum_dtype = F32 if other is None else BF16
    return _call(
        body, (p, p, p, lb, do, ssave, *(other or ())), name="hgrn_bwd_rev" if rev else "hgrn_bwd", grid=(2, nb),
        in_specs=[col(3), col(z_base), col(9), pl.BlockSpec((1, 2 * LANE), lambda hp, i: (0, hp)), out,
                  pl.BlockSpec((1, 2 * LANE, LANE), lambda hp, i: (blk(i), hp, 0))] + [out] * n_other,
        out_specs=[out, out, out, pl.BlockSpec((1, 2 * LANE), lambda hp, i: (0, hp))],
        out_shape=[jax.ShapeDtypeStruct((t, HG_DIM), sum_dtype), jax.ShapeDtypeStruct((t, HG_DIM), BF16),
                   jax.ShapeDtypeStruct((t, HG_DIM), sum_dtype), jax.ShapeDtypeStruct((1, HG_DIM), F32)],
        scratch_shapes=[pltpu.VMEM((2, LANE, LANE), F32)], after=after)


def _mix_out(att, of, ob, p, gout, w_out, g_post, x, after=(), tm=256):
    t, d = x.shape
    tm = min(tm, t)

    def body(att_ref, of_ref, ob_ref, hg0_ref, hg1_ref, gout_ref, w_ref, g_ref, x_ref, cat_ref, y_ref, x1_ref):
        cat_ref[:, :ATT_Q_DIM] = att_ref[...]
        for h in range(HG_HEADS):
            sl = slice(h * LANE, (h + 1) * LANE)
            hg_ref = hg0_ref if h < 2 else hg1_ref
            hg = hg_ref[:, (h % 2) * LANE:(h % 2 + 1) * LANE]
            nrm = _rms_fwd(of_ref[:, sl] + ob_ref[:, sl], gout_ref[...])
            cat_ref[:, ATT_Q_DIM + h * LANE:ATT_Q_DIM + (h + 1) * LANE] = _bf(nrm * hg * _sigmoid(hg))
        y = _dot(cat_ref[...], w_ref[...], 1, 0)
        y_ref[...] = y
        x1_ref[...] = x_ref[...] + _rms_fwd(y, g_ref[...])

    row = lambda width: pl.BlockSpec((tm, width), lambda i: (i, 0))
    return _call(
        body, (att, of, ob, p, p, gout, w_out, g_post, x), name="mix_out", grid=(t // tm,),
        in_specs=[row(ATT_Q_DIM), row(HG_DIM), row(HG_DIM), pl.BlockSpec((tm, 2 * LANE), lambda i: (i, 11)),
                  pl.BlockSpec((tm, 2 * LANE), lambda i: (i, 12)), _full((1, LANE)), _full((d, d)), _full((1, d)), row(d)],
        out_specs=[row(d), row(d), row(d)],
        out_shape=[jax.ShapeDtypeStruct((t, d), BF16), jax.ShapeDtypeStruct((t, d), F32), jax.ShapeDtypeStruct((t, d), F32)],
        after=after)


def _rec_bwd(dcat, of, ob, p, gout, tm=256):
    t = of.shape[0]
    tm = min(tm, t)

    def body(dc_ref, of_ref, ob_ref, hg0_ref, hg1_ref, gout_ref, do_ref, dhg_ref, dgo_ref):
        @pl.when(pl.program_id(0) == 0)
        def _():
            dgo_ref[...] = jnp.zeros_like(dgo_ref)

        dgo = jnp.zeros((1, LANE), F32)
        for h in range(HG_HEADS):
            sl = slice(h * LANE, (h + 1) * LANE)
            hg_ref = hg0_ref if h < 2 else hg1_ref
            hg = hg_ref[:, (h % 2) * LANE:(h % 2 + 1) * LANE]
            o = of_ref[:, sl] + ob_ref[:, sl]
            sg = _sigmoid(hg)
            drec = dc_ref[:, sl]
            dhg_ref[:, sl] = _bf(drec * _rms_fwd(o, gout_ref[...]) * (sg * (1.0 + hg * (1.0 - sg))))
            do, dg = _rms_bwd(drec * hg * sg, o, gout_ref[...])
            do_ref[:, sl] = do
            dgo = dgo + dg
        dgo_ref[...] += dgo

    row = lambda width: pl.BlockSpec((tm, width), lambda i: (i, 0))
    return pl.pallas_call(
        body, name="rec_bwd", grid=(t // tm,),
        in_specs=[pl.BlockSpec((tm, HG_DIM), lambda i: (i, 1)), row(HG_DIM), row(HG_DIM),
                  pl.BlockSpec((tm, 2 * LANE), lambda i: (i, 11)), pl.BlockSpec((tm, 2 * LANE), lambda i: (i, 12)),
                  _full((1, LANE))],
        out_specs=[row(HG_DIM), row(HG_DIM), _full((1, LANE))],
        out_shape=[jax.ShapeDtypeStruct((t, HG_DIM), F32), jax.ShapeDtypeStruct((t, HG_DIM), BF16),
                   jax.ShapeDtypeStruct((1, LANE), F32)],
        compiler_params=_params(),
    )(dcat, of, ob, p, p, gout)


def _xattn_fwd(q, kv, tq=512):
    t, d = q.shape
    tq = min(tq, t)
    nm = kv.shape[0]

    def body(q_ref, kv_ref, o_ref):
        for h in range(X_HEADS):
            sl = slice(h * X_HEAD_DIM, (h + 1) * X_HEAD_DIM)
            s = _dot(_bf(q_ref[:, sl]), _bf(kv_ref[:, sl]), 1, 1) * (X_HEAD_DIM ** -0.5)
            e = jnp.exp(s - jnp.max(s, axis=-1, keepdims=True))
            pr = e * (1.0 / jnp.sum(e, axis=-1, keepdims=True))
            o_ref[:, sl] = _bf(_dot(_bf(pr), _bf(kv_ref[:, d + h * X_HEAD_DIM:d + (h + 1) * X_HEAD_DIM]), 1, 0))

    return pl.pallas_call(
        body, name="xattn_fwd", grid=(t // tq,),
        in_specs=[pl.BlockSpec((tq, d), lambda i: (i, 0)), _full((nm, 2 * d))],
        out_specs=pl.BlockSpec((tq, d), lambda i: (i, 0)),
        out_shape=jax.ShapeDtypeStruct((t, d), BF16),
        compiler_params=_params(),
    )(q, kv)


def _xattn_bwd(q, kv, do, tq=512):
    t, d = q.shape
    tq = min(tq, t)
    nm = kv.shape[0]

    def body(q_ref, kv_ref, do_ref, dq_ref, dkv_ref):
        @pl.when(pl.program_id(0) == 0)
        def _():
            dkv_ref[...] = jnp.zeros_like(dkv_ref)

        for h in range(X_HEADS):
            sl = slice(h * X_HEAD_DIM, (h + 1) * X_HEAD_DIM)
            slv = slice(d + h * X_HEAD_DIM, d + (h + 1) * X_HEAD_DIM)
            qb, kb, vb, dob = _bf(q_ref[:, sl]), _bf(kv_ref[:, sl]), _bf(kv_ref[:, slv]), _bf(do_ref[:, sl])
            s = _dot(qb, kb, 1, 1) * (X_HEAD_DIM ** -0.5)
            e = jnp.exp(s - jnp.max(s, axis=-1, keepdims=True))
            pr = e * (1.0 / jnp.sum(e, axis=-1, keepdims=True))
            dpr = _dot(dob, vb, 1, 1)
            ds = _bf(pr * (dpr - jnp.sum(pr * dpr, axis=-1, keepdims=True)) * (X_HEAD_DIM ** -0.5))
            dq_ref[:, sl] = _bf(_dot(ds, kb, 1, 0))
            dkv_ref[:, sl] += _dot(ds, qb, 0, 0)
            dkv_ref[:, slv] += _dot(_bf(pr), dob, 0, 0)

    return pl.pallas_call(
        body, name="xattn_bwd", grid=(t // tq,),
        in_specs=[pl.BlockSpec((tq, d), lambda i: (i, 0)), _full((nm, 2 * d)), pl.BlockSpec((tq, d), lambda i: (i, 0))],
        out_specs=[pl.BlockSpec((tq, d), lambda i: (i, 0)), _full((nm, 2 * d))],
        out_shape=[jax.ShapeDtypeStruct((t, d), BF16), jax.ShapeDtypeStruct((nm, 2 * d), F32)],
        compiler_params=_params(),
    )(q, kv, do)


CONV_TN = 256


def _shift_rows(u, row, t, delta):
    if delta < 0:
        return jnp.where(row == 0, 0.0, pltpu.roll(u, 1, 0))
    return jnp.where(row == t - 1, 0.0, pltpu.roll(u, t - 1, 0))


def _conv_gate_fwd(u, conv_w, conv_b):
    t = u.shape[0]
    nt = D_FF // CONV_TN

    def body(ug_ref, uv_ref, wg_ref, wv_ref, bg_ref, bv_ref, a_ref):
        row = lax.broadcasted_iota(jnp.int32, (t, CONV_TN), 0)

        def conv(u_ref, w_ref, b_ref):
            uu = u_ref[...]
            return (b_ref[...] + _shift_rows(uu, row, t, -1) * w_ref[0:1, :] + uu * w_ref[1:2, :]
                    + _shift_rows(uu, row, t, 1) * w_ref[2:3, :])

        gate = conv(ug_ref, wg_ref, bg_ref)
        a_ref[...] = _bf(gate * _sigmoid(gate) * conv(uv_ref, wv_ref, bv_ref))

    col = lambda rows, off: pl.BlockSpec((rows, CONV_TN), lambda j: (0, j + off))
    return pl.pallas_call(
        body, name="conv_gate_fwd", grid=(nt,),
        in_specs=[col(t, 0), col(t, nt), col(3, 0), col(3, nt), col(1, 0), col(1, nt)],
        out_specs=col(t, 0),
        out_shape=jax.ShapeDtypeStruct((t, D_FF), BF16),
        compiler_params=_params(),
    )(u, u, conv_w, conv_w, conv_b, conv_b)


def _conv_gate_bwd(u, conv_w, conv_b, da, after=()):
    t = u.shape[0]
    nt = D_FF // CONV_TN

    def body(ug_ref, uv_ref, wg_ref, wv_ref, bg_ref, bv_ref, da_ref, dug_ref, dwg_ref, dbg_ref, duv_ref, dwv_ref, dbv_ref):
        row = lax.broadcasted_iota(jnp.int32, (t, CONV_TN), 0)
        ug, uv = ug_ref[...], uv_ref[...]
        ug_m, ug_p = _shift_rows(ug, row, t, -1), _shift_rows(ug, row, t, 1)
        uv_m, uv_p = _shift_rows(uv, row, t, -1), _shift_rows(uv, row, t, 1)
        gate = bg_ref[...] + ug_m * wg_ref[0:1, :] + ug * wg_ref[1:2, :] + ug_p * wg_ref[2:3, :]
        val = bv_ref[...] + uv_m * wv_ref[0:1, :] + uv * wv_ref[1:2, :] + uv_p * wv_ref[2:3, :]
        sg = _sigmoid(gate)
        da_ = da_ref[...].astype(F32)

        def emit(dacc, um, uu, up, w_ref, du_ref, dw_ref, db_ref):
            du_ref[...] = _bf(_shift_rows(dacc, row, t, 1) * w_ref[0:1, :] + dacc * w_ref[1:2, :]
                              + _shift_rows(dacc, row, t, -1) * w_ref[2:3, :])
            dw_ref[0:1, :] = jnp.sum(dacc * um, axis=0, keepdims=True)
            dw_ref[1:2, :] = jnp.sum(dacc * uu, axis=0, keepdims=True)
            dw_ref[2:3, :] = jnp.sum(dacc * up, axis=0, keepdims=True)
            db_ref[...] = jnp.sum(dacc, axis=0, keepdims=True)

        emit(da_ * val * (sg * (1.0 + gate * (1.0 - sg))), ug_m, ug, ug_p, wg_ref, dug_ref, dwg_ref, dbg_ref)
        emit(da_ * gate * sg, uv_m, uv, uv_p, wv_ref, duv_ref, dwv_ref, dbv_ref)

    col = lambda rows, off: pl.BlockSpec((rows, CONV_TN), lambda j: (0, j + off))
    half_shapes = [jax.ShapeDtypeStruct((t, D_FF), BF16), jax.ShapeDtypeStruct((3, D_FF), F32),
                   jax.ShapeDtypeStruct((1, D_FF), F32)]
    outs = _call(
        body, (u, u, conv_w, conv_w, conv_b, conv_b, da), name="conv_gate_bwd", grid=(nt,),
        in_specs=[col(t, 0), col(t, nt), col(3, 0), col(3, nt), col(1, 0), col(1, nt), col(t, 0)],
        out_specs=[col(t, 0), col(3, 0), col(1, 0)] * 2, out_shape=half_shapes * 2, after=after)
    return outs[:3], outs[3:]


def _row_tile(r, cap):
    best = r
    for cand in range(16, cap + 1, 16):
        if r % cand == 0:
            best = cand
    return best


def _sum_parts(own, got, me, name, tr=256):
    _, r, c = got.shape
    tr = _row_tile(r, tr)

    def body(me_ref, own_ref, got_ref, o_ref):
        mine = own_ref[...].astype(F32)
        acc = None
        for i in range(N_DEV):
            term = jnp.where(me_ref[0] == i, mine, got_ref[i].astype(F32))
            acc = term if acc is None else acc + term
        o_ref[...] = acc

    return pl.pallas_call(
        body, name=name,
        grid_spec=pltpu.PrefetchScalarGridSpec(
            num_scalar_prefetch=1, grid=(r // tr,),
            in_specs=[pl.BlockSpec((None, tr, c), lambda i, me_ref: (me_ref[0], i, 0)),
                      pl.BlockSpec((N_DEV, tr, c), lambda i, me_ref: (0, i, 0))],
            out_specs=pl.BlockSpec((tr, c), lambda i, me_ref: (i, 0))),
        out_shape=jax.ShapeDtypeStruct((r, c), F32),
        compiler_params=_params(),
    )(me, own, got)


def _adamw_math(w_, g_, m_, v_):
    m_ = ADAM_B1 * m_ + (1.0 - ADAM_B1) * g_
    v_ = ADAM_B2 * v_ + (1.0 - ADAM_B2) * (g_ * g_)
    m_hat = m_ / (1.0 - ADAM_B1 ** ADAM_STEP)
    v_hat = v_ / (1.0 - ADAM_B2 ** ADAM_STEP)
    return -ADAM_LR * (m_hat / (jnp.sqrt(v_hat) + ADAM_EPS) + ADAM_WD * w_), m_, v_


def _sum_adamw(own, got, me, w, m, v, name, tr=256):
    _, r, c = got.shape
    tr = _row_tile(r, tr)

    def body(me_ref, own_ref, got_ref, w_ref, m_ref, v_ref, g_ref, d_ref, nm_ref, nv_ref):
        mine = own_ref[...].astype(F32)
        acc = None
        for i in range(N_DEV):
            term = jnp.where(me_ref[0] == i, mine, got_ref[i].astype(F32))
            acc = term if acc is None else acc + term
        g_ref[...] = acc
        d_ref[...], nm_ref[...], nv_ref[...] = _adamw_math(w_ref[...], acc, m_ref[...], v_ref[...])

    blk = pl.BlockSpec((tr, c), lambda i, me_ref: (i, 0))
    return pl.pallas_call(
        body, name=name,
        grid_spec=pltpu.PrefetchScalarGridSpec(
            num_scalar_prefetch=1, grid=(r // tr,),
            in_specs=[pl.BlockSpec((None, tr, c), lambda i, me_ref: (me_ref[0], i, 0)),
                      pl.BlockSpec((N_DEV, tr, c), lambda i, me_ref: (0, i, 0)), blk, blk, blk],
            out_specs=[blk] * 4),
        out_shape=[jax.ShapeDtypeStruct((r, c), F32)] * 4,
        compiler_params=_params(),
    )(me, own, got, w, m, v)


def _adamw(w, g, m, v, name, tr=256):
    r, c = w.shape
    tr = _row_tile(r, tr)

    def body(w_ref, g_ref, m_ref, v_ref, d_ref, nm_ref, nv_ref):
        d_ref[...], nm_ref[...], nv_ref[...] = _adamw_math(w_ref[...], g_ref[...], m_ref[...], v_ref[...])

    blk = pl.BlockSpec((tr, c), lambda i: (i, 0))
    return pl.pallas_call(
        body, name=name, grid=(r // tr,),
        in_specs=[blk] * 4, out_specs=[blk] * 3,
        out_shape=[jax.ShapeDtypeStruct((r, c), F32)] * 3,
        compiler_params=_params(),
    )(w, g, m, v)


def _mesh_pos():
    return lax.axis_index("x"), lax.axis_index("y"), lax.axis_index("c")


def _dev_index(px, py, pc):
    return 4 * px + 2 * py + pc


class _Gather:
    def __init__(self, arrs):
        self.arrs = list(arrs)
        n = len(self.arrs)
        self.out_shape = [jax.ShapeDtypeStruct((N_DEV,) + a.shape, a.dtype) for a in self.arrs]
        self.sems = [pltpu.SemaphoreType.DMA((7, n)), pltpu.SemaphoreType.DMA((7, n)), pltpu.SemaphoreType.DMA((n,))]

    def _ctx(self, ins, outs, sems):
        send_sems, recv_sems, local_sems = sems
        x, y, c = _mesh_pos()
        chips = [(1 - x, y), (x, 1 - y), (1 - x, 1 - y)]

        def copy(k, a, block, to, src=None):
            dst = outs[a].at[_dev_index(*block)]
            return pltpu.make_async_remote_copy(
                src_ref=dst if src is None else src, dst_ref=dst, send_sem=send_sems.at[k, a], recv_sem=recv_sems.at[k, a],
                device_id=to, device_id_type=MESH_T)

        n = len(ins)
        me, sibling = (x, y, c), (x, y, 1 - c)
        mine = [pltpu.make_async_copy(ins[a], outs[a].at[_dev_index(*me)], local_sems.at[a]) for a in range(n)]
        first = [copy(0, a, me, sibling, src=ins[a]) for a in range(n)]
        first += [copy(1 + j, a, me, (*chip, c), src=ins[a]) for j, chip in enumerate(chips) for a in range(n)]
        passed = [copy(4 + j, a, (*chip, c), sibling) for j, chip in enumerate(chips) for a in range(n)]
        return n, c, me, sibling, chips, copy, mine, first, passed

    def start(self, ins, outs, sems):
        _, _, _, _, _, _, mine, first, _ = self._ctx(ins, outs, sems)
        for cp in mine + first:
            cp.start()

    def forward(self, ins, outs, sems):
        n, c, me, _, chips, copy, _, _, passed = self._ctx(ins, outs, sems)
        for j, chip in enumerate(chips):
            for a in range(n):
                copy(1 + j, a, (*chip, c), me).wait_recv()
                passed[j * n + a].start()

    def finish(self, ins, outs, sems):
        n, c, me, sibling, chips, copy, mine, first, passed = self._ctx(ins, outs, sems)
        for a in range(n):
            copy(0, a, sibling, me).wait_recv()
        for j, chip in enumerate(chips):
            for a in range(n):
                copy(4 + j, a, (*chip, 1 - c), me).wait_recv()
        for cp in first + passed:
            cp.wait_send()
        for cp in mine:
            cp.wait()


def _comm_alone(comm, name):
    n = len(comm.arrs)

    def body(*refs):
        ins, outs, sems = refs[:n], refs[n:2 * n], refs[2 * n:]
        comm.start(ins, outs, sems)
        if comm.forward is not None:
            comm.forward(ins, outs, sems)
        comm.finish(ins, outs, sems)

    any_spec = pl.BlockSpec(memory_space=pl.ANY)
    return pl.pallas_call(body, name=name, in_specs=[any_spec] * n, out_specs=[any_spec] * n, out_shape=comm.out_shape,
                          scratch_shapes=comm.sems)(*comm.arrs)


def _peers(x, y, c):
    out = []
    for k in range(1, N_DEV):
        pos = (1 - x if k & 4 else x, 1 - y if k & 2 else y, 1 - c if k & 1 else c)
        out.append((k, pos, _dev_index(*pos)))
    return out


def _exchange_begin(arrs, tag):
    n = len(arrs)
    lands = [lax.empty(a.shape, a.dtype) for a in arrs]

    def start_body(*refs):
        ins, land = refs[:n], refs[n:2 * n]
        send_sems, recv_sems, token = refs[2 * n], refs[2 * n + 1], refs[-1]
        x, y, c = _mesh_pos()
        me_i = _dev_index(x, y, c)
        for k, pos, peer_i in _peers(x, y, c):
            for a in range(n):
                pltpu.make_async_remote_copy(
                    src_ref=ins[a].at[peer_i], dst_ref=land[a].at[me_i], send_sem=send_sems.at[(k - 1) * n + a],
                    recv_sem=recv_sems.at[(k - 1) * n + a], device_id=pos, device_id_type=MESH_T).start()
        token[...] = jnp.zeros_like(token)

    hbm = pl.BlockSpec(memory_space=pltpu.HBM)
    sem = pl.BlockSpec(memory_space=pltpu.SEMAPHORE)
    thru = [pltpu.HBM(a.shape, a.dtype) for a in arrs]
    outs = pl.pallas_call(
        start_body, name="exchange_start_" + tag,
        out_shape=[pltpu.SemaphoreType.DMA((7 * n,)), pltpu.SemaphoreType.DMA((7 * n,))] + thru + thru
        + [jax.ShapeDtypeStruct((8, LANE), F32)],
        in_specs=[hbm] * (2 * n), out_specs=[sem, sem] + [hbm] * (2 * n) + [pl.BlockSpec(memory_space=pltpu.VMEM)],
        input_output_aliases={i: 2 + i for i in range(2 * n)},
        compiler_params=pltpu.CompilerParams(has_side_effects=pltpu.SideEffectType.DATAFLOW_SIDE_EFFECTING),
    )(*[pltpu.with_memory_space_constraint(a, pltpu.HBM) for a in list(arrs) + list(lands)])
    return (tag, n, outs[0], outs[1], outs[2:2 + n], outs[2 + n:2 + 2 * n]), outs[-1]


def _exchange_end(handle, after):
    tag, n, send_sems, recv_sems, srcs, lands = handle

    def body(*refs):
        ins, land = refs[:n], refs[n:2 * n]
        send_sems_, recv_sems_ = refs[2 * n], refs[2 * n + 1]
        x, y, c = _mesh_pos()
        me_i = _dev_index(x, y, c)
        for k, pos, peer_i in _peers(x, y, c):
            for a in range(n):
                cp = pltpu.make_async_remote_copy(
                    src_ref=ins[a].at[peer_i], dst_ref=land[a].at[me_i], send_sem=send_sems_.at[(k - 1) * n + a],
                    recv_sem=recv_sems_.at[(k - 1) * n + a], device_id=pos, device_id_type=MESH_T)
                cp.wait_send()
                cp.wait_recv()

    hbm = pl.BlockSpec(memory_space=pltpu.HBM)
    sem = pl.BlockSpec(memory_space=pltpu.SEMAPHORE)
    outs = pl.pallas_call(
        body, name="exchange_end_" + tag, out_shape=[pltpu.HBM(a.shape, a.dtype) for a in list(srcs) + list(lands)],
        in_specs=[hbm] * (2 * n) + [sem, sem, pl.BlockSpec(memory_space=pl.ANY)], out_specs=[hbm] * (2 * n),
        input_output_aliases={i: i for i in range(2 * n)},
        compiler_params=pltpu.CompilerParams(has_side_effects=pltpu.SideEffectType.DATAFLOW_SIDE_EFFECTING),
    )(*srcs, *lands, send_sems, recv_sems, after)
    return list(zip(outs[:n], outs[n:]))


def _gather_begin(arrs, tag):
    n = len(arrs)
    me = _dev_index(*_mesh_pos())
    lands = [lax.dynamic_update_slice(lax.empty((N_DEV,) + a.shape, a.dtype), a[None], (me,) + (0,) * a.ndim) for a in arrs]

    def start_body(*refs):
        ins, land = refs[:n], refs[n:2 * n]
        send_sems, recv_sems, token = refs[2 * n], refs[2 * n + 1], refs[-1]
        x, y, c = _mesh_pos()
        me_i = _dev_index(x, y, c)
        for a in range(n):
            for k, pos, _ in _peers(x, y, c):
                pltpu.make_async_remote_copy(
                    src_ref=ins[a], dst_ref=land[a].at[me_i], send_sem=send_sems.at[(k - 1) * n + a],
                    recv_sem=recv_sems.at[(k - 1) * n + a], device_id=pos, device_id_type=MESH_T).start()
        token[...] = jnp.zeros_like(token)

    hbm = pl.BlockSpec(memory_space=pltpu.HBM)
    sem = pl.BlockSpec(memory_space=pltpu.SEMAPHORE)
    thru = [pltpu.HBM(a.shape, a.dtype) for a in list(arrs) + lands]
    outs = pl.pallas_call(
        start_body, name="gather_start_" + tag,
        out_shape=[pltpu.SemaphoreType.DMA((7 * n,)), pltpu.SemaphoreType.DMA((7 * n,))] + thru
        + [jax.ShapeDtypeStruct((8, LANE), F32)],
        in_specs=[hbm] * (2 * n), out_specs=[sem, sem] + [hbm] * (2 * n) + [pl.BlockSpec(memory_space=pltpu.VMEM)],
        input_output_aliases={i: 2 + i for i in range(2 * n)},
        compiler_params=pltpu.CompilerParams(has_side_effects=pltpu.SideEffectType.DATAFLOW_SIDE_EFFECTING),
    )(*[pltpu.with_memory_space_constraint(a, pltpu.HBM) for a in list(arrs) + lands])
    return (tag, n, outs[0], outs[1], outs[2:2 + n], outs[2 + n:2 + 2 * n]), outs[-1]


def _gather_end(handle, which, after):
    tag, n, send_sems, recv_sems, srcs, lands = handle
    m = len(which)

    def body(*refs):
        ins, land = refs[:m], refs[m:2 * m]
        send_sems_, recv_sems_ = refs[2 * m], refs[2 * m + 1]
        x, y, c = _mesh_pos()
        me_i = _dev_index(x, y, c)
        for j, a in enumerate(which):
            for k, pos, _ in _peers(x, y, c):
                cp = pltpu.make_async_remote_copy(
                    src_ref=ins[j], dst_ref=land[j].at[me_i], send_sem=send_sems_.at[(k - 1) * n + a],
                    recv_sem=recv_sems_.at[(k - 1) * n + a], device_id=pos, device_id_type=MESH_T)
                cp.wait_send()
                cp.wait_recv()

    hbm = pl.BlockSpec(memory_space=pltpu.HBM)
    sem = pl.BlockSpec(memory_space=pltpu.SEMAPHORE)
    ops = [srcs[a] for a in which] + [lands[a] for a in which]
    outs = pl.pallas_call(
        body, name="gather_end_%s_%s" % (tag, "_".join(str(a) for a in which)),
        out_shape=[pltpu.HBM(a.shape, a.dtype) for a in ops],
        in_specs=[hbm] * (2 * m) + [sem, sem, pl.BlockSpec(memory_space=pl.ANY)], out_specs=[hbm] * (2 * m),
        input_output_aliases={i: i for i in range(2 * m)},
        compiler_params=pltpu.CompilerParams(has_side_effects=pltpu.SideEffectType.DATAFLOW_SIDE_EFFECTING),
    )(*ops, send_sems, recv_sems, after)
    return list(outs[m:])


_SPLIT = dict(has_side_effects=pltpu.SideEffectType.DATAFLOW_SIDE_EFFECTING)


def _chips(x, y):
    return [(1 - x, y), (x, 1 - y), (1 - x, 1 - y)]


def _gather2_begin(arrs, tag):
    n = len(arrs)
    me = _dev_index(*_mesh_pos())
    lands = [lax.dynamic_update_slice(lax.empty((N_DEV,) + a.shape, a.dtype), a[None], (me,) + (0,) * a.ndim) for a in arrs]

    def body(*refs):
        ins, land = refs[:n], refs[n:2 * n]
        send1, recv1, token = refs[2 * n], refs[2 * n + 1], refs[-1]
        x, y, c = _mesh_pos()
        me_i = _dev_index(x, y, c)
        targets = [(x, y, 1 - c)] + [(*chip, c) for chip in _chips(x, y)]
        for a in range(n):
            for k, pos in enumerate(targets):
                pltpu.make_async_remote_copy(
                    src_ref=ins[a], dst_ref=land[a].at[me_i], send_sem=send1.at[k * n + a], recv_sem=recv1.at[k * n + a],
                    device_id=pos, device_id_type=MESH_T).start()
        token[...] = jnp.zeros_like(token)

    hbm = pl.BlockSpec(memory_space=pltpu.HBM)
    sem = pl.BlockSpec(memory_space=pltpu.SEMAPHORE)
    thru = [pltpu.HBM(a.shape, a.dtype) for a in list(arrs) + lands]
    outs = pl.pallas_call(
        body, name="gather_start_" + tag,
        out_shape=[pltpu.SemaphoreType.DMA((4 * n,)), pltpu.SemaphoreType.DMA((4 * n,))] + thru
        + [jax.ShapeDtypeStruct((8, LANE), F32)],
        in_specs=[hbm] * (2 * n), out_specs=[sem, sem] + [hbm] * (2 * n) + [pl.BlockSpec(memory_space=pltpu.VMEM)],
        input_output_aliases={i: 2 + i for i in range(2 * n)}, compiler_params=pltpu.CompilerParams(**_SPLIT),
    )(*[pltpu.with_memory_space_constraint(a, pltpu.HBM) for a in list(arrs) + lands])
    return dict(tag=tag, n=n, send1=outs[0], recv1=outs[1], srcs=list(outs[2:2 + n]), lands=list(outs[2 + n:2 + 2 * n])), outs[-1]


def _gather2_forward(handle, which, after, part):
    n, m = handle["n"], len(which)

    def body(*refs):
        land, recv1 = refs[:m], refs[m]
        send2, recv2, token = refs[m + 2], refs[m + 3], refs[-1]
        token[...] = jnp.zeros_like(token)
        x, y, c = _mesh_pos()
        for j, a in enumerate(which):
            for k, chip in enumerate(_chips(x, y)):
                blk = land[j].at[_dev_index(*chip, c)]
                copy = pltpu.make_async_remote_copy(
                    src_ref=blk, dst_ref=blk, send_sem=send2.at[k * m + j], recv_sem=recv2.at[k * m + j],
                    device_id=(x, y, 1 - c), device_id_type=MESH_T)
                pltpu.make_async_remote_copy(
                    src_ref=blk, dst_ref=blk, send_sem=send2.at[k * m + j], recv_sem=recv1.at[(1 + k) * n + a],
                    device_id=(*chip, c), device_id_type=MESH_T).wait_recv()
                copy.start()

    hbm = pl.BlockSpec(memory_space=pltpu.HBM)
    sem = pl.BlockSpec(memory_space=pltpu.SEMAPHORE)
    lands = [handle["lands"][a] for a in which]
    outs = pl.pallas_call(
        body, name="gather_forward_%s_%s" % (handle["tag"], part),
        out_shape=[pltpu.SemaphoreType.DMA((3 * m,)), pltpu.SemaphoreType.DMA((3 * m,))] + [pltpu.HBM(a.shape, a.dtype) for a in lands]
        + [jax.ShapeDtypeStruct((8, LANE), F32)],
        in_specs=[hbm] * m + [sem, pl.BlockSpec(memory_space=pl.ANY)],
        out_specs=[sem, sem] + [hbm] * m + [pl.BlockSpec(memory_space=pltpu.VMEM)],
        input_output_aliases={i: 2 + i for i in range(m)}, compiler_params=pltpu.CompilerParams(**_SPLIT),
    )(*lands, handle["recv1"], after)
    for j, a in enumerate(which):
        handle["lands"][a] = outs[2 + j]
        handle[("leg2", a)] = (outs[0], outs[1], j, m)
    return outs[-1]


def _gather2_end(handle, a, after):
    n = handle["n"]
    send2, recv2, j, m = handle[("leg2", a)]

    def body(src, land, send1, recv1, send2_, recv2_, after_ref, src_out, land_out):
        x, y, c = _mesh_pos()
        me_i = _dev_index(x, y, c)
        sibling = (x, y, 1 - c)
        for k, pos in enumerate([sibling] + [(*chip, c) for chip in _chips(x, y)]):
            first = pltpu.make_async_remote_copy(
                src_ref=src, dst_ref=land.at[me_i], send_sem=send1.at[k * n + a], recv_sem=recv1.at[k * n + a],
                device_id=pos, device_id_type=MESH_T)
            first.wait_send()
            if k == 0:
                first.wait_recv()
        for k in range(3):
            second = pltpu.make_async_remote_copy(
                src_ref=land.at[me_i], dst_ref=land.at[me_i], send_sem=send2_.at[k * m + j], recv_sem=recv2_.at[k * m + j],
                device_id=sibling, device_id_type=MESH_T)
            second.wait_send()
            second.wait_recv()

    hbm = pl.BlockSpec(memory_space=pltpu.HBM)
    sem = pl.BlockSpec(memory_space=pltpu.SEMAPHORE)
    src, land = handle["srcs"][a], handle["lands"][a]
    outs = pl.pallas_call(
        body, name="gather_end_%s_%d" % (handle["tag"], a),
        out_shape=[pltpu.HBM(src.shape, src.dtype), pltpu.HBM(land.shape, land.dtype)],
        in_specs=[hbm, hbm, sem, sem, sem, sem, pl.BlockSpec(memory_space=pl.ANY)], out_specs=[hbm, hbm],
        input_output_aliases={0: 0, 1: 1}, compiler_params=pltpu.CompilerParams(**_SPLIT),
    )(src, land, handle["send1"], handle["recv1"], send2, recv2, after)
    return outs[1]


def _call(body, operands, *, name, grid, in_specs, out_specs, out_shape, scratch_shapes=(), after=()):
    ni, nd = len(in_specs), len(after)

    def ordered(*refs):
        body(*refs[:ni], *refs[ni + nd:])

    outs = pl.pallas_call(
        ordered, name=name, grid=grid, in_specs=list(in_specs) + [pl.BlockSpec(memory_space=pl.ANY)] * nd,
        out_specs=out_specs, out_shape=out_shape, scratch_shapes=list(scratch_shapes), compiler_params=_params(),
    )(*operands, *after)
    return list(outs)


def _rows(a):
    return a.reshape(-1, a.shape[-1])


def _slots(a):
    return a.reshape(N_DEV, -1, a.shape[-1])


def _local_step(x, mem, tgt, w, wire):
    tables = _rope_tables(x.shape[0])
    bd = _head_sum_matrix()
    two = lambda g: jnp.tile(g, (1, 2))
    gq2, gk2 = two(w["q_norm_g"]), two(w["k_norm_g"])

    w_in_t, hg_lb, conv_w = _comm_alone(_Gather([wire["w_in"], w["hg_lb"].reshape(4, -1), w["conv_w"][0]]), "gather_w_in")
    w_in_t = _rows(w_in_t)
    hg_lb = jnp.transpose(hg_lb.reshape(N_DEV, 2, 2, -1), (1, 2, 0, 3)).reshape(2, 2, HG_DIM)
    conv_w = jnp.transpose(conv_w, (1, 0, 2)).reshape(3, 2 * D_FF)
    lb_a0, lb_a1 = hg_lb[:, 0, :], hg_lb[:, 1, :]

    order = ("w_out", "w_xq", "w_xkv", "w_xo", "w_up", "w_down")
    w_in_t, *later = lax.optimization_barrier((w_in_t, *[wire[n] for n in order]))
    fetch, started = _gather2_begin(later, "w")
    take = lambda n, after: _rows(_gather2_end(fetch, order.index(n), after))

    h1, p = _norm_mm(x, w["pre_mix_g"], w_in_t, True, N_IN, "in_proj", after=[started])
    qr, kr = _qk_prep(p, gq2, gk2, tables, bd)
    att, att32, lse = _attn_fwd(qr, kr, p)
    passed = _gather2_forward(fetch, [0, 1, 2, 3], att, "a")
    lb = _lower_bounds(lb_a0, lb_a1)
    of, s_f = _hgrn_fwd(p, lb[0:1], False, after=[passed])
    ob, s_b = _hgrn_fwd(p, lb[1:2], True, after=[passed, of])
    passed = _gather2_forward(fetch, [4, 5], ob, "b")
    w_out = take("w_out", ob)
    cat, mixed, x1 = _mix_out(att, of, ob, p, w["hg_out_norm_g"], w_out, w["post_mix_g"], x, after=[passed])

    w_xq = take("w_xq", x1)
    h2, q2 = _norm_mm(x1, w["pre_x_g"], w_xq, False, 1024, "xq_proj")
    w_xkv_t = take("w_xkv", q2)
    mn, kv = _norm_mm(mem, w["mem_norm_g"], w_xkv_t, True, 2 * D_MODEL, "xkv_proj")
    o2 = _xattn_fwd(q2, kv)
    w_xo = take("w_xo", o2)
    y2, x2 = _mm_postnorm_res(o2, w_xo, w["post_x_g"], x1, "xo_proj")

    w_up_t = take("w_up", x2)
    h3, u = _norm_mm(x2, w["pre_ffn_g"], w_up_t, True, 2 * D_FF, "up_proj", tm=256)
    a = _conv_gate_fwd(u, conv_w, w["conv_b"])
    w_down = take("w_down", a)
    y3, dx3, loss = _mm_postnorm_res_loss(a, w_down, w["post_ffn_g"], x2, tgt, "down_proj")

    g, pending = {}, {}
    dy3, da, g["post_ffn_g"] = _postnorm_bwd_mm(dx3, y3, w["post_ffn_g"], w_down, "down_bwd", BF16)
    pending["down"], started = _exchange_begin([_slots(_dw(a, dy3, "dw_down"))], "down")
    (du_g, dcw_g, dcb_g), (du_v, dcw_v, dcb_v) = _conv_gate_bwd(u, conv_w, w["conv_b"], da, after=[started])
    g["conv_w"] = jnp.concatenate([dcw_g, dcw_v], axis=1)
    g["conv_b"] = jnp.concatenate([dcb_g, dcb_v], axis=1)
    pending["up"], started = _exchange_begin([_slots(_dw([du_g, du_v], h3, "dw_up"))], "up")
    dx2, g["pre_ffn_g"] = _mm_prenorm_bwd([du_g, du_v], w_up_t, True, x2, w["pre_ffn_g"], dx3, "up_bwd", after=[started], tm=256)

    dy2, do2, g["post_x_g"] = _postnorm_bwd_mm(dx2, y2, w["post_x_g"], w_xo, "xo_bwd", BF16)
    dw_xo = _dw(o2, dy2, "dw_xo")
    dq2, dkv = _xattn_bwd(q2, kv, do2)
    dw_xq = _dw(h2, dq2, "dw_xq")
    dkvb = _bf(dkv)
    dw_xkv = _dw(dkvb, mn, "dw_xkv")
    pending["x"], started = _exchange_begin([_slots(dw_xo), _slots(dw_xq), _slots(dw_xkv)], "x")
    _, g["mem_norm_g"] = _mm_prenorm_bwd(dkvb, w_xkv_t, True, mem, w["mem_norm_g"], jnp.zeros_like(mem), "xkv_bwd")
    dx1, g["pre_x_g"] = _mm_prenorm_bwd(dq2, w_xq, False, x1, w["pre_x_g"], dx2, "xq_bwd", after=[started])

    dmixed, dcat, g["post_mix_g"] = _postnorm_bwd_mm(dx1, mixed, w["post_mix_g"], w_out, "out_bwd")
    pending["out"], started = _exchange_begin([_slots(_dw(cat, dmixed, "dw_out"))], "out")
    do, dhg, g["hg_out_norm_g"] = _rec_bwd(dcat, of, ob, p, w["hg_out_norm_g"])
    dhq_f, dz_f, dhi_f, dlb_f = _hgrn_bwd(p, lb[0:1], do, s_f, False, after=[started])
    dhq, dz_b, dhi, dlb_b = _hgrn_bwd(p, lb[1:2], do, s_b, True, other=(dhq_f, dhi_f))
    d_a0, d_a1 = _lower_bounds_bwd(lb_a0, lb_a1, jnp.concatenate([dlb_f, dlb_b], axis=0))
    g["hg_lb"] = jnp.stack([d_a0, d_a1], axis=1)
    dqr, dkr, dv = _attn_bwd(qr, kr, p, dcat, att32, lse)
    dp_att, dgq, dgk = _qk_prep_bwd(p, dqr, dkr, dv, gq2, gk2, tables, bd)
    g["q_norm_g"], g["k_norm_g"] = dgq, dgk
    dp = [dp_att, dhq, dz_f, dz_b, dhi, dhg]
    pending["in"], started = _exchange_begin([_slots(_dw(dp, h1, "dw_in"))], "in")
    dx, g["pre_mix_g"] = _mm_prenorm_bwd(dp, w_in_t, True, x, w["pre_mix_g"], dx1, "in_bwd", after=[started])
    return loss, dx, g, pending


_COL_SHARDED = ("w_in", "w_xkv", "w_up")
_ROW_SHARDED = ("w_out", "w_xq", "w_xo", "w_down")
_REPLICATED = ("pre_mix_g", "q_norm_g", "k_norm_g", "hg_out_norm_g", "post_mix_g", "pre_x_g", "mem_norm_g", "post_x_g",
               "pre_ffn_g", "conv_b", "post_ffn_g")
_WEIGHTS = ("pre_mix_g", "w_in", "q_norm_g", "k_norm_g", "hg_lb", "hg_out_norm_g", "w_out", "post_mix_g", "pre_x_g",
            "mem_norm_g", "w_xq", "w_xkv", "w_xo", "post_x_g", "pre_ffn_g", "w_up", "conv_w", "conv_b", "w_down",
            "post_ffn_g")
_ADAM_TRANSPOSED = ("w_in", "w_up")
PACK_W = 1024


def _small_plan(shapes):
    plan, r = [], 0
    for vi, (rows, cols) in enumerate(shapes):
        for i in range(rows):
            for c0 in range(0, cols, PACK_W):
                plan.append((vi, i, c0, min(PACK_W, cols - c0), r))
                r += 1
    return plan, -(-r // 8) * 8


def _pack_small(vals):
    plan, nrows = _small_plan([val.shape for val in vals])

    def body(*refs):
        ins, out = refs[:-1], refs[-1]
        out[...] = jnp.zeros_like(out)
        for vi, i, c0, width, r in plan:
            out[r:r + 1, 0:width] = ins[vi][i:i + 1, c0:c0 + width]

    return pl.pallas_call(body, name="pack_small", out_shape=jax.ShapeDtypeStruct((nrows, PACK_W), F32))(*vals)


def _sum_unpack_small(packs, shapes):
    plan, _ = _small_plan(shapes)

    def body(*refs):
        p_ref, outs = refs[0], refs[1:]
        acc = p_ref[0]
        for i in range(1, N_DEV):
            acc = acc + p_ref[i]
        for vi, i, c0, width, r in plan:
            outs[vi][i:i + 1, c0:c0 + width] = acc[r:r + 1, 0:width]

    return pl.pallas_call(body, name="sum_unpack_small", out_shape=[jax.ShapeDtypeStruct(s, F32) for s in shapes])(packs)


def _adamw_many(ws, gs, ms, vs):
    n = len(ws)

    def body(*refs):
        w_refs, g_refs, m_refs, v_refs = (refs[k * n:(k + 1) * n] for k in range(4))
        d_refs, nm_refs, nv_refs = (refs[(4 + k) * n:(5 + k) * n] for k in range(3))
        for k in range(n):
            g_ = g_refs[k][...]
            m_ = ADAM_B1 * m_refs[k][...] + (1.0 - ADAM_B1) * g_
            v_ = ADAM_B2 * v_refs[k][...] + (1.0 - ADAM_B2) * (g_ * g_)
            m_hat = m_ / (1.0 - ADAM_B1 ** ADAM_STEP)
            v_hat = v_ / (1.0 - ADAM_B2 ** ADAM_STEP)
            d_refs[k][...] = -ADAM_LR * (m_hat / (jnp.sqrt(v_hat) + ADAM_EPS) + ADAM_WD * w_refs[k][...])
            nm_refs[k][...] = m_
            nv_refs[k][...] = v_

    shapes = [jax.ShapeDtypeStruct(a.shape, F32) for a in ws]
    outs = pl.pallas_call(body, name="adamw_small", out_shape=shapes * 3)(*ws, *gs, *ms, *vs)
    return outs[:n], outs[n:2 * n], outs[2 * n:]


def kernel(x, mem, pre_mix_g, w_in, q_norm_g, k_norm_g, hg_lb, hg_out_norm_g, w_out, post_mix_g, pre_x_g, mem_norm_g, w_xq, w_xkv, w_xo, post_x_g, pre_ffn_g, w_up, conv_w, conv_b, w_down, post_ffn_g, loss_target, m_pre_mix_g, m_w_in, m_q_norm_g, m_k_norm_g, m_hg_lb, m_hg_out_norm_g, m_w_out, m_post_mix_g, m_pre_x_g, m_mem_norm_g, m_w_xq, m_w_xkv, m_w_xo, m_post_x_g, m_pre_ffn_g, m_w_up, m_conv_w, m_conv_b, m_w_down, m_post_ffn_g, v_pre_mix_g, v_w_in, v_q_norm_g, v_k_norm_g, v_hg_lb, v_hg_out_norm_g, v_w_out, v_post_mix_g, v_pre_x_g, v_mem_norm_g, v_w_xq, v_w_xkv, v_w_xo, v_post_x_g, v_pre_ffn_g, v_w_up, v_conv_w, v_conv_b, v_w_down, v_post_ffn_g):
    args = dict(locals())
    w = {n: args[n] for n in _WEIGHTS}
    m = {n: args["m_" + n] for n in _WEIGHTS}
    v = {n: args["v_" + n] for n in _WEIGHTS}
    me = _dev_index(*_mesh_pos())

    wire = {n: _bf(w[n][0].T) for n in _COL_SHARDED}
    wire.update({n: _bf(w[n][0]) for n in _ROW_SHARDED})

    loss, grad_x, g, pending = _local_step(x[0], mem[0], loss_target[0], w, wire)

    grads, delta, new_m, new_v = {}, {}, {}, {}

    me_arr = jnp.reshape(me, (1,)).astype(jnp.int32)

    def update(n, parts):
        if n in _ADAM_TRANSPOSED:
            outs = _sum_adamw(*parts, me_arr, w[n][0].T, m[n][0].T, v[n][0].T, "update_" + n)
            grads[n], delta[n], new_m[n], new_v[n] = (a.T[None] for a in outs)
        elif n in _COL_SHARDED:
            gsum = _sum_parts(*parts, me_arr, "sum_" + n).T
            grads[n] = gsum[None]
            delta[n], new_m[n], new_v[n] = (a[None] for a in _adamw(w[n][0], gsum, m[n][0], v[n][0], "adamw_" + n))
        else:
            outs = _sum_adamw(*parts, me_arr, w[n][0], m[n][0], v[n][0], "update_" + n)
            grads[n], delta[n], new_m[n], new_v[n] = (a[None] for a in outs)

    small = list(_REPLICATED) + ["hg_lb", "conv_w"]
    vals = [g[n] for n in _REPLICATED] + [g["hg_lb"].reshape(4, HG_DIM), g["conv_w"], jnp.pad(loss, ((0, 0), (0, LANE - 1)))]
    shapes = [val.shape for val in vals]
    fetch_small, _ = _gather_begin([_pack_small(vals)], "small")

    after = grad_x
    for tag, names in (("down", ["w_down"]), ("up", ["w_up"]), ("x", ["w_xo", "w_xq", "w_xkv"]), ("out", ["w_out"]),
                       ("in", ["w_in"])):
        for n, parts in zip(names, _exchange_end(pending[tag], after)):
            update(n, parts)
            after = new_v[n]

    (packs,) = _gather_end(fetch_small, [0], after)
    summed = _sum_unpack_small(packs, shapes)
    loss = summed[-1][0, 0]
    for n, s in zip(small, summed[:-1]):
        grads[n] = s
    fold = lambda v2: v2[:, :ATT_HEAD_DIM] + v2[:, ATT_HEAD_DIM:]
    grads["q_norm_g"], grads["k_norm_g"] = fold(grads["q_norm_g"]), fold(grads["k_norm_g"])
    grads["hg_lb"] = lax.dynamic_slice_in_dim(grads["hg_lb"].reshape(2, 2, HG_DIM), me * (HG_DIM // N_DEV),
                                              HG_DIM // N_DEV, axis=2)
    grads["conv_w"] = lax.dynamic_slice_in_dim(grads["conv_w"], me * (2 * D_FF // N_DEV), 2 * D_FF // N_DEV, axis=1)[None]

    flat2 = lambda a: a.reshape(-1, a.shape[-1])
    outs = _adamw_many(*[[flat2(d[n]) for n in small] for d in (w, grads, m, v)])
    for dst, vals in zip((delta, new_m, new_v), outs):
        for n, val in zip(small, vals):
            dst[n] = val.reshape(w[n].shape)

    return (loss, grad_x[None], *[grads[n] for n in _WEIGHTS], *[delta[n] for n in _WEIGHTS],
            *[new_m[n] for n in _WEIGHTS], *[new_v[n] for n in _WEIGHTS])
```

```python
import jax
import jax.numpy as jnp
from jax import lax
from jax.experimental import pallas as pl
from jax.experimental.pallas import tpu as pltpu

F32 = jnp.float32
BF16 = jnp.bfloat16

D_MODEL = 1024
GRID_W = 64
EPS = 1e-6
ATT_HEADS = 8
ATT_HEAD_DIM = 64
ATT_Q_DIM = 512
ATT_KV_DIM = 128
ROPE_THETA = 10000.0
HG_HEADS = 4
HG_DIM = 512
HG_CHUNK = 32
HG_CHUNK_LOG2 = 5
HG_BLOCK_FWD = 256
HG_BLOCK_BWD = 128
N_IN = 3328
X_HEADS = 4
X_HEAD_DIM = 256
D_FF = 2816
N_DEV = 8
LANE = 128
ADAM_LR = 0.001
ADAM_B1 = 0.9
ADAM_B2 = 0.999
ADAM_EPS = 1e-08
ADAM_WD = 0.01
ADAM_STEP = 10
VMEM_LIMIT = 56 * 1024 * 1024

MESH_T = pl.DeviceIdType.MESH


def _params(**kw):
    return pltpu.CompilerParams(vmem_limit_bytes=VMEM_LIMIT, **kw)


def _dot(a, b, ca, cb):
    return lax.dot_general(a, b, (((ca,), (cb,)), ((), ())), preferred_element_type=F32)


def _bf(x):
    return x.astype(BF16)


def _sigmoid(x):
    return 1.0 / (1.0 + jnp.exp(-x))


def _rms_fwd(x, g):
    r = lax.rsqrt(jnp.mean(x * x, axis=-1, keepdims=True) + EPS)
    return x * r * g


def _rms_bwd(dy, x, g):
    r = lax.rsqrt(jnp.mean(x * x, axis=-1, keepdims=True) + EPS)
    xh = x * r
    dg = jnp.sum(dy * xh, axis=0, keepdims=True)
    t = dy * g
    dx = r * (t - xh * jnp.mean(t * xh, axis=-1, keepdims=True))
    return dx, dg


def _full(shape):
    nd = len(shape)
    return pl.BlockSpec(shape, lambda *a: (0,) * nd)


def _norm_mm(x, g, w, trans, tn, name, after=(), tm=512):
    t, d = x.shape
    n = w.shape[0] if trans else w.shape[1]
    tm = min(tm, t)

    def body(x_ref, g_ref, w_ref, h_ref, p_ref):
        h = _bf(_rms_fwd(x_ref[...], g_ref[...]))
        h_ref[...] = h
        p_ref[...] = _dot(h, w_ref[...], 1, 1 if trans else 0)

    w_spec = pl.BlockSpec((tn, d), lambda i, j: (j, 0)) if trans else pl.BlockSpec((d, tn), lambda i, j: (0, j))
    return _call(
        body, (x, g, w), name=name, grid=(t // tm, n // tn),
        in_specs=[pl.BlockSpec((tm, d), lambda i, j: (i, 0)), _full((1, d)), w_spec],
        out_specs=[pl.BlockSpec((tm, d), lambda i, j: (i, 0)), pl.BlockSpec((tm, tn), lambda i, j: (i, j))],
        out_shape=[jax.ShapeDtypeStruct((t, d), BF16), jax.ShapeDtypeStruct((t, n), F32)], after=after)


def _mm_postnorm_res(a, w, g, res, name, tm=256):
    t, k = a.shape
    d = w.shape[1]

    def body(a_ref, w_ref, g_ref, res_ref, y_ref, o_ref):
        y = _dot(a_ref[...], w_ref[...], 1, 0)
        y_ref[...] = y
        o_ref[...] = res_ref[...] + _rms_fwd(y, g_ref[...])

    row = lambda width: pl.BlockSpec((tm, width), lambda i: (i, 0))
    return pl.pallas_call(
        body, name=name, grid=(t // tm,),
        in_specs=[row(k), _full((k, d)), _full((1, d)), row(d)],
        out_specs=[row(d), row(d)],
        out_shape=[jax.ShapeDtypeStruct((t, d), F32)] * 2,
        compiler_params=_params(),
    )(a, w, g, res)


def _mm_postnorm_res_loss(a, w, g, res, tgt, name, tm=256):
    t, k = a.shape
    d = w.shape[1]

    def body(a_ref, w_ref, g_ref, res_ref, tgt_ref, y_ref, dout_ref, loss_ref):
        @pl.when(pl.program_id(0) == 0)
        def _():
            loss_ref[...] = jnp.zeros_like(loss_ref)

        y = _dot(a_ref[...], w_ref[...], 1, 0)
        y_ref[...] = y
        diff = res_ref[...] + _rms_fwd(y, g_ref[...]) - tgt_ref[...]
        dout_ref[...] = diff * (1.0 / d)
        part = jnp.sum(jnp.sum(diff * diff, axis=-1, keepdims=True), axis=0, keepdims=True)
        loss_ref[...] += (0.5 / d) * part

    row = lambda width: pl.BlockSpec((tm, width), lambda i: (i, 0))
    return pl.pallas_call(
        body, name=name, grid=(t // tm,),
        in_specs=[row(k), _full((k, d)), _full((1, d)), row(d), row(d)],
        out_specs=[row(d), row(d), _full((1, 1))],
        out_shape=[jax.ShapeDtypeStruct((t, d), F32)] * 2 + [jax.ShapeDtypeStruct((1, 1), F32)],
        compiler_params=_params(),
    )(a, w, g, res, tgt)


def _postnorm_bwd_mm(dout, y, g, w, name, da_dtype=F32, tm=256):
    t, d = y.shape
    k = w.shape[0]

    def body(dout_ref, y_ref, g_ref, w_ref, dy_ref, da_ref, dg_ref):
        @pl.when(pl.program_id(0) == 0)
        def _():
            dg_ref[...] = jnp.zeros_like(dg_ref)

        dy, dg = _rms_bwd(dout_ref[...], y_ref[...], g_ref[...])
        dg_ref[...] += dg
        dyb = _bf(dy)
        dy_ref[...] = dyb
        da_ref[...] = _dot(dyb, w_ref[...], 1, 1).astype(da_dtype)

    row = lambda width: pl.BlockSpec((tm, width), lambda i: (i, 0))
    return pl.pallas_call(
        body, name=name, grid=(t // tm,),
        in_specs=[row(d), row(d), _full((1, d)), _full((k, d))],
        out_specs=[row(d), row(k), _full((1, d))],
        out_shape=[jax.ShapeDtypeStruct((t, d), BF16), jax.ShapeDtypeStruct((t, k), da_dtype), jax.ShapeDtypeStruct((1, d), F32)],
        compiler_params=_params(),
    )(dout, y, g, w)


def _mm_prenorm_bwd(dp, w, trans, x, g, dres, name, after=(), tm=512):
    dps = list(dp) if isinstance(dp, (list, tuple)) else [dp]
    nparts = len(dps)
    t = dps[0].shape[0]
    widths = [part.shape[1] for part in dps]
    d = x.shape[1]
    tm = min(tm, t)

    def body(*refs):
        dp_refs = refs[:nparts]
        w_ref, x_ref, g_ref, dres_ref, dx_ref, dg_ref = refs[nparts:]

        @pl.when(pl.program_id(0) == 0)
        def _():
            dg_ref[...] = jnp.zeros_like(dg_ref)

        dh = None
        for k, dp_ref in enumerate(dp_refs):
            cols = slice(sum(widths[:k]), sum(widths[:k + 1]))
            term = _dot(dp_ref[...], w_ref[cols, :], 1, 0) if trans else _dot(dp_ref[...], w_ref[:, cols], 1, 1)
            dh = term if dh is None else dh + term
        dx, dg = _rms_bwd(dh, x_ref[...], g_ref[...])
        dg_ref[...] += dg
        dx_ref[...] = dres_ref[...] + dx

    row = lambda width: pl.BlockSpec((tm, width), lambda i: (i, 0))
    return _call(
        body, (*dps, w, x, g, dres), name=name, grid=(t // tm,),
        in_specs=[row(width) for width in widths] + [_full(w.shape), row(d), _full((1, d)), row(d)],
        out_specs=[row(d), _full((1, d))],
        out_shape=[jax.ShapeDtypeStruct((t, d), F32), jax.ShapeDtypeStruct((1, d), F32)], after=after)


def _dw(a, b, name, tka=256):
    parts = list(a) if isinstance(a, (list, tuple)) else [a]
    nparts = len(parts)
    t, nb = b.shape
    tiles = [part.shape[1] // tka for part in parts]
    first = [sum(tiles[:k]) for k in range(nparts)]

    def body(*refs):
        a_refs, b_ref, o_ref = refs[:nparts], refs[nparts], refs[nparts + 1]
        i = pl.program_id(0)
        for k, a_ref in enumerate(a_refs):
            @pl.when((i >= first[k]) & (i < first[k] + tiles[k]))
            def _():
                o_ref[...] = _bf(_dot(a_ref[...], b_ref[...], 0, 0))

    a_spec = lambda k: pl.BlockSpec((t, tka), lambda i: (0, jnp.clip(i - first[k], 0, tiles[k] - 1)))
    return pl.pallas_call(
        body, name=name, grid=(sum(tiles),),
        in_specs=[a_spec(k) for k in range(nparts)] + [_full((t, nb))],
        out_specs=pl.BlockSpec((tka, nb), lambda i: (i, 0)),
        out_shape=jax.ShapeDtypeStruct((sum(tiles) * tka, nb), BF16),
        compiler_params=_params(),
    )(*parts, b)


def _rope_tables(t):
    pos = jnp.arange(t)
    r = (pos // GRID_W).astype(F32)
    c = (pos % GRID_W).astype(F32)
    npair = ATT_HEAD_DIM // 4
    inv = jnp.power(ROPE_THETA, -jnp.arange(npair, dtype=F32) / npair)
    ang = jnp.concatenate([r[:, None] * inv, c[:, None] * inv], axis=-1)
    cos = jnp.repeat(jnp.cos(ang), 2, axis=-1)
    sin = jnp.repeat(jnp.sin(ang), 2, axis=-1)
    even = (jnp.arange(ATT_HEAD_DIM) % 2) == 0
    sa = jnp.where(even, -sin, 0.0)
    sb = jnp.where(even, 0.0, sin)
    two = lambda a: jnp.tile(a, (1, 2))
    return two(cos), two(sa), two(sb)


def _head_sum_matrix():
    a = jnp.arange(LANE) // ATT_HEAD_DIM
    return (a[:, None] == a[None, :]).astype(BF16)


def _head_mean(v, bd):
    hi = _bf(v)
    lo = _bf(v - hi.astype(F32))
    return (_dot(hi, bd, 1, 0) + _dot(lo, bd, 1, 0)) * (1.0 / ATT_HEAD_DIM)


def _qk_prep(p, gq, gk, tables, bd, tm=512):
    t = p.shape[0]
    tm = min(tm, t)
    cc, sa, sb = tables

    def body(p_ref, gq_ref, gk_ref, cc_ref, sa_ref, sb_ref, bd_ref, q_ref, k_ref):
        cc_, sa_, sb_, bd_ = cc_ref[...], sa_ref[...], sb_ref[...], bd_ref[...]
        low = lax.broadcasted_iota(jnp.int32, (tm, LANE), 1) < ATT_HEAD_DIM

        def normrope(xs, g):
            xn = xs * lax.rsqrt(_head_mean(xs * xs, bd_) + EPS) * g
            return xn * cc_ + pltpu.roll(xn, LANE - 1, 1) * sa_ + pltpu.roll(xn, 1, 1) * sb_

        for j in range(4):
            y = normrope(p_ref[:, j * LANE:(j + 1) * LANE], gq_ref[...]) * (ATT_HEAD_DIM ** -0.5)
            yr = pltpu.roll(y, ATT_HEAD_DIM, 1)
            if j // 2 == 0:
                h0, h1 = jnp.where(low, y, 0.0), jnp.where(low, yr, 0.0)
            else:
                h0, h1 = jnp.where(low, 0.0, yr), jnp.where(low, 0.0, y)
            q_ref[:, (2 * j) * LANE:(2 * j + 1) * LANE] = _bf(h0)
            q_ref[:, (2 * j + 1) * LANE:(2 * j + 2) * LANE] = _bf(h1)
        k_ref[...] = _bf(normrope(p_ref[:, ATT_Q_DIM:ATT_Q_DIM + LANE], gk_ref[...]))

    row = lambda width: pl.BlockSpec((tm, width), lambda i: (i, 0))
    return pl.pallas_call(
        body, name="qk_prep", grid=(t // tm,),
        in_specs=[row(ATT_Q_DIM + LANE), _full((1, LANE)), _full((1, LANE)), row(LANE), row(LANE), row(LANE),
                  _full((LANE, LANE))],
        out_specs=[row(ATT_HEADS * LANE), row(LANE)],
        out_shape=[jax.ShapeDtypeStruct((t, ATT_HEADS * LANE), BF16), jax.ShapeDtypeStruct((t, LANE), BF16)],
        compiler_params=_params(),
    )(p, gq, gk, cc, sa, sb, bd)


def _qk_prep_bwd(p, dq, dk, dv, gq, gk, tables, bd, tm=512):
    t = p.shape[0]
    tm = min(tm, t)
    cc, sa, sb = tables
    att_w = ATT_Q_DIM + 2 * ATT_KV_DIM

    def body(p_ref, dq_ref, dk_ref, dv_ref, gq_ref, gk_ref, cc_ref, sa_ref, sb_ref, bd_ref, dp_ref, dgq_ref, dgk_ref):
        dp_ref[:, ATT_Q_DIM + LANE:att_w] = _bf(dv_ref[...])

        @pl.when(pl.program_id(0) == 0)
        def _():
            dgq_ref[...] = jnp.zeros_like(dgq_ref)
            dgk_ref[...] = jnp.zeros_like(dgk_ref)

        cc_, sa_, sb_, bd_ = cc_ref[...], sa_ref[...], sb_ref[...], bd_ref[...]
        low = lax.broadcasted_iota(jnp.int32, (tm, LANE), 1) < ATT_HEAD_DIM

        def bwd(xs, g, dy):
            r = lax.rsqrt(_head_mean(xs * xs, bd_) + EPS)
            xh = xs * r
            dxn = dy * cc_ + pltpu.roll(dy * sa_, 1, 1) + pltpu.roll(dy * sb_, LANE - 1, 1)
            dg = jnp.sum(dxn * xh, axis=0, keepdims=True)
            tt = dxn * g
            return r * (tt - xh * _head_mean(tt * xh, bd_)), dg

        dgq = jnp.zeros((1, LANE), F32)
        for j in range(4):
            d0 = dq_ref[:, (2 * j) * LANE:(2 * j + 1) * LANE]
            d1 = dq_ref[:, (2 * j + 1) * LANE:(2 * j + 2) * LANE]
            if j // 2 == 0:
                dy = jnp.where(low, d0, pltpu.roll(d1, ATT_HEAD_DIM, 1))
            else:
                dy = jnp.where(low, pltpu.roll(d0, ATT_HEAD_DIM, 1), d1)
            dx, dg = bwd(p_ref[:, j * LANE:(j + 1) * LANE], gq_ref[...], dy * (ATT_HEAD_DIM ** -0.5))
            dp_ref[:, j * LANE:(j + 1) * LANE] = _bf(dx)
            dgq = dgq + dg
        dgq_ref[...] += dgq
        dx, dg = bwd(p_ref[:, ATT_Q_DIM:ATT_Q_DIM + LANE], gk_ref[...], dk_ref[...])
        dp_ref[:, ATT_Q_DIM:ATT_Q_DIM + LANE] = _bf(dx)
        dgk_ref[...] += dg

    row = lambda width: pl.BlockSpec((tm, width), lambda i: (i, 0))
    return pl.pallas_call(
        body, name="qk_prep_bwd", grid=(t // tm,),
        in_specs=[row(ATT_Q_DIM + LANE), row(ATT_HEADS * LANE), row(LANE), row(LANE), _full((1, LANE)), _full((1, LANE)),
                  row(LANE), row(LANE), row(LANE), _full((LANE, LANE))],
        out_specs=[row(att_w), _full((1, LANE)), _full((1, LANE))],
        out_shape=[jax.ShapeDtypeStruct((t, att_w), BF16), jax.ShapeDtypeStruct((1, LANE), F32),
                   jax.ShapeDtypeStruct((1, LANE), F32)],
        compiler_params=_params(),
    )(p, dq, dk, dv, gq, gk, cc, sa, sb, bd)


def _attn_fwd(q, k, p, tq=256):
    t = k.shape[0]
    tq = min(tq, t)
    v_blk = (ATT_Q_DIM + ATT_KV_DIM) // LANE

    def body(q_ref, k_ref, v_ref, o_ref, o32_ref, lse_ref):
        k_ = k_ref[...]
        v = v_ref[...]
        lane_k = lax.broadcasted_iota(jnp.int32, (t, LANE), 1)
        lane_q = lax.broadcasted_iota(jnp.int32, (tq, LANE), 1)
        lowk, lowq = lane_k < ATT_HEAD_DIM, lane_q < ATT_HEAD_DIM
        ones_lane = (ATT_HEAD_DIM, 0)
        vm = (_bf(jnp.where(lowk, v, jnp.where(lane_k == ones_lane[0], 1.0, 0.0))),
              _bf(jnp.where(lowk, jnp.where(lane_k == ones_lane[1], 1.0, 0.0), v)))
        for j in range(4):
            kvh = j // 2
            acc = None
            for sub in range(2):
                h = 2 * j + sub
                s = _dot(q_ref[:, h * LANE:(h + 1) * LANE], k_, 1, 1)
                mx = jnp.max(s, axis=-1, keepdims=True)
                ov = _dot(jnp.exp(_bf(s - mx)), vm[kvh], 1, 0)
                l = jnp.sum(jnp.where(lane_q == ones_lane[kvh], ov, 0.0), axis=-1, keepdims=True)
                lse_ref[h] = mx + jnp.log(l)
                o = jnp.where(lowq if kvh == 0 else ~lowq, ov, 0.0) * (1.0 / l)
                if sub != kvh:
                    o = pltpu.roll(o, ATT_HEAD_DIM, 1)
                acc = o if acc is None else acc + o
            o32_ref[:, j * LANE:(j + 1) * LANE] = acc
            o_ref[:, j * LANE:(j + 1) * LANE] = _bf(acc)

    row = pl.BlockSpec((tq, ATT_Q_DIM), lambda i: (i, 0))
    return _call(
        body, (q, k, p), name="attn_fwd", grid=(t // tq,),
        in_specs=[pl.BlockSpec((tq, ATT_HEADS * LANE), lambda i: (i, 0)), _full((t, LANE)),
                  pl.BlockSpec((t, LANE), lambda i: (0, v_blk))],
        out_specs=[row, row, pl.BlockSpec((ATT_HEADS, tq, 1), lambda i: (0, i, 0))],
        out_shape=[jax.ShapeDtypeStruct((t, ATT_Q_DIM), BF16), jax.ShapeDtypeStruct((t, ATT_Q_DIM), F32),
                   jax.ShapeDtypeStruct((ATT_HEADS, t, 1), F32)])


def _attn_bwd(q, k, p, dcat, o32, lse, tq=256):
    t = k.shape[0]
    tq = min(tq, t)
    v_blk = (ATT_Q_DIM + ATT_KV_DIM) // LANE

    def body(q_ref, k_ref, v_ref, do_ref, o_ref, lse_ref, dq_ref, dk_ref, dv_ref):
        @pl.when(pl.program_id(0) == 0)
        def _():
            dk_ref[...] = jnp.zeros_like(dk_ref)
            dv_ref[...] = jnp.zeros_like(dv_ref)

        k_ = k_ref[...]
        vb = _bf(v_ref[...])
        lowq = lax.broadcasted_iota(jnp.int32, (tq, LANE), 1) < ATT_HEAD_DIM
        dk_acc = jnp.zeros((t, LANE), F32)
        dv_acc = jnp.zeros((t, LANE), F32)
        for j in range(4):
            kvh = j // 2
            dop = do_ref[:, j * LANE:(j + 1) * LANE]
            prod = dop * o_ref[:, j * LANE:(j + 1) * LANE]
            d_low = jnp.sum(jnp.where(lowq, prod, 0.0), axis=-1, keepdims=True)
            d_sub = (d_low, jnp.sum(prod, axis=-1, keepdims=True) - d_low)
            for sub in range(2):
                h = 2 * j + sub
                src = dop if sub == kvh else pltpu.roll(dop, ATT_HEAD_DIM, 1)
                do_h = _bf(jnp.where(lowq, src, 0.0) if kvh == 0 else jnp.where(lowq, 0.0, src))
                qh = q_ref[:, h * LANE:(h + 1) * LANE]
                pr = jnp.exp(_bf(_dot(qh, k_, 1, 1) - lse_ref[h]))
                ds = pr * _bf(_dot(do_h, vb, 1, 1) - d_sub[sub])
                dq_ref[:, h * LANE:(h + 1) * LANE] = _dot(ds, k_, 1, 0)
                dk_acc = dk_acc + _dot(ds, qh, 0, 0)
                dv_acc = dv_acc + _dot(pr, do_h, 0, 0)
        dk_ref[...] += dk_acc
        dv_ref[...] += dv_acc

    row = pl.BlockSpec((tq, ATT_Q_DIM), lambda i: (i, 0))
    return _call(
        body, (q, k, p, dcat, o32, lse), name="attn_bwd", grid=(t // tq,),
        in_specs=[pl.BlockSpec((tq, ATT_HEADS * LANE), lambda i: (i, 0)), _full((t, LANE)),
                  pl.BlockSpec((t, LANE), lambda i: (0, v_blk)), row, row,
                  pl.BlockSpec((ATT_HEADS, tq, 1), lambda i: (0, i, 0))],
        out_specs=[pl.BlockSpec((tq, ATT_HEADS * LANE), lambda i: (i, 0)), _full((t, LANE)), _full((t, LANE))],
        out_shape=[jax.ShapeDtypeStruct((t, ATT_HEADS * LANE), F32), jax.ShapeDtypeStruct((t, LANE), F32),
                   jax.ShapeDtypeStruct((t, LANE), F32)])


def _lower_bounds(a0, a1):
    def body(a0_ref, a1_ref, lb_ref):
        m = jnp.maximum(a0_ref[...], a1_ref[...])
        e0, e1 = jnp.exp(a0_ref[...] - m), jnp.exp(a1_ref[...] - m)
        lb_ref[...] = e0 / (e0 + e1)

    return pl.pallas_call(body, name="lower_bounds", out_shape=jax.ShapeDtypeStruct(a0.shape, F32))(a0, a1)


def _lower_bounds_bwd(a0, a1, dlb):
    def body(a0_ref, a1_ref, dlb_ref, d0_ref, d1_ref):
        m = jnp.maximum(a0_ref[...], a1_ref[...])
        e0, e1 = jnp.exp(a0_ref[...] - m), jnp.exp(a1_ref[...] - m)
        lb = e0 / (e0 + e1)
        d0 = dlb_ref[...] * lb * (1.0 - lb)
        d0_ref[...] = d0
        d1_ref[...] = -d0

    return pl.pallas_call(body, name="lower_bounds_bwd", out_shape=[jax.ShapeDtypeStruct(a0.shape, F32)] * 2)(a0, a1, dlb)


def _chunk_scan(x, pos, down):
    n = x.shape[0]
    s = 1
    while s < HG_CHUNK:
        if down:
            x = x + jnp.where(pos >= s, pltpu.roll(x, s, 0), 0.0)
        else:
            x = x + jnp.where(pos < HG_CHUNK - s, pltpu.roll(x, n - s, 0), 0.0)
        s *= 2
    return x


def _chunk_concat(x, nc):
    xb = _bf(x)
    rows = []
    for c in range(nc):
        pieces = [xb[c * HG_CHUNK:(c + 1) * HG_CHUNK, :]]
        if c:
            pieces.insert(0, jnp.zeros((HG_CHUNK, c * LANE), BF16))
        if c < nc - 1:
            pieces.append(jnp.zeros((HG_CHUNK, (nc - 1 - c) * LANE), BF16))
        rows.append(jnp.concatenate(pieces, axis=1) if len(pieces) > 1 else pieces[0])
    return jnp.concatenate(rows, axis=0) if nc > 1 else rows[0]


def _chunk_pick(x, nc):
    rows = [x[c * HG_CHUNK:(c + 1) * HG_CHUNK, c * LANE:(c + 1) * LANE] for c in range(nc)]
    return jnp.concatenate(rows, axis=0) if nc > 1 else rows[0]


def _chunk_rows(vals):
    rows = [jnp.broadcast_to(v, (HG_CHUNK, LANE)) for v in vals]
    return jnp.concatenate(rows, axis=0) if len(rows) > 1 else rows[0]


def _hgrn_gates(hq, z, lb):
    sq = _sigmoid(hq)
    sig = _sigmoid(z)
    f = lb + (1.0 - lb) * sig
    return hq * sq, sq, sig, f, jnp.log(f)


def _hgrn_local(q, f, g, v, pos, cid, amask, rev, nc):
    k = 1.0 - f
    ba = _chunk_scan(g, pos, not rev)
    bb = _chunk_scan(g, pos, rev)
    eq = 0.5 * (ba - bb + g)
    ex_q, ex_k, ex_i, ex_d = jnp.exp(eq), jnp.exp(-eq), jnp.exp(ba), jnp.exp(bb - g)
    qb, kb, qi, kd = q * ex_q, k * ex_k, q * ex_i, k * ex_d
    dvec = jnp.exp(ba + bb - g)
    a = jnp.where(amask, _dot(_bf(qb), _bf(kb), 1, 1), 0.0)
    kd_m = _chunk_concat(kd, nc)
    qi_m = _chunk_concat(qi, nc)
    ut_all = _dot(_bf(v), kd_m, 0, 0)
    return dict(k=k, ex_q=ex_q, ex_k=ex_k, ex_i=ex_i, ex_d=ex_d, qb=qb, kb=kb, qi=qi, kd=kd, dvec=dvec, a=a,
                kd_m=kd_m, qi_m=qi_m, ut_all=ut_all)


def _hgrn_masks(n, rev):
    row = lax.broadcasted_iota(jnp.int32, (n, LANE), 0)
    ti = lax.broadcasted_iota(jnp.int32, (n, n), 0)
    si = lax.broadcasted_iota(jnp.int32, (n, n), 1)
    tri = (si >= ti) if rev else (si <= ti)
    same = jnp.right_shift(ti, HG_CHUNK_LOG2) == jnp.right_shift(si, HG_CHUNK_LOG2)
    return jnp.bitwise_and(row, HG_CHUNK - 1), jnp.right_shift(row, HG_CHUNK_LOG2), same & tri


def _hgrn_specs(t, rev, bwd):
    n = min(HG_BLOCK_BWD if bwd else HG_BLOCK_FWD, t)
    nb = t // n
    if rev != bwd:
        blk = lambda i: nb - 1 - i
    else:
        blk = lambda i: i
    col = lambda base: pl.BlockSpec((n, 2 * LANE), lambda hp, i: (blk(i), base + hp))
    return n, nb, blk, col


def _hgrn_fwd(p, lb, rev, after=()):
    t = p.shape[0]
    n, nb, blk, col = _hgrn_specs(t, rev, False)
    nc = n // HG_CHUNK
    sub = n // min(HG_BLOCK_BWD, t)
    z_base = 7 if rev else 5

    def body(hq_ref, z_ref, hi_ref, lb_ref, o_ref, ssave_ref, st_scr):
        @pl.when(pl.program_id(1) == 0)
        def _():
            st_scr[...] = jnp.zeros_like(st_scr)

        pos, cid, amask = _hgrn_masks(n, rev)
        order = list(range(nc))[::-1] if rev else list(range(nc))
        for hh in range(2):
            sl = slice(hh * LANE, (hh + 1) * LANE)
            q, _, _, f, g = _hgrn_gates(hq_ref[:, sl], z_ref[:, sl], lb_ref[0:1, sl])
            v = hi_ref[:, sl]
            c_ = _hgrn_local(q, f, g, v, pos, cid, amask, rev, nc)
            st = st_scr[hh]
            cols = [None] * nc
            for c in order:
                cols[c] = st
                st = st * c_["dvec"][c * HG_CHUNK:c * HG_CHUNK + 1, :] + c_["ut_all"][:, c * LANE:(c + 1) * LANE]
            st_scr[hh] = st
            for s in range(sub):
                ssave_ref[sub - 1 - s if rev else s, sl, :] = cols[order[s * (nc // sub)]]
            st_all = _bf(jnp.concatenate(cols, axis=1))
            o_ref[:, sl] = _dot(_bf(c_["a"]), _bf(v), 1, 0) + _dot(c_["qi_m"], st_all, 1, 1)

    return _call(
        body, (p, p, p, lb), name="hgrn_fwd_rev" if rev else "hgrn_fwd", grid=(2, nb),
        in_specs=[col(3), col(z_base), col(9), pl.BlockSpec((1, 2 * LANE), lambda hp, i: (0, hp))],
        out_specs=[pl.BlockSpec((n, 2 * LANE), lambda hp, i: (blk(i), hp)),
                   pl.BlockSpec((sub, 2 * LANE, LANE), lambda hp, i: (blk(i), hp, 0))],
        out_shape=[jax.ShapeDtypeStruct((t, HG_DIM), F32), jax.ShapeDtypeStruct((nb * sub, HG_DIM, LANE), F32)],
        scratch_shapes=[pltpu.VMEM((2, LANE, LANE), F32)], after=after)


def _hgrn_bwd(p, lb, do, ssave, rev, other=None, after=()):
    t = p.shape[0]
    n, nb, blk, col = _hgrn_specs(t, rev, True)
    nc = n // HG_CHUNK
    z_base = 7 if rev else 5
    n_other = 0 if other is None else 2

    def body(*refs):
        hq_ref, z_ref, hi_ref, lb_ref, do_ref, ssave_ref = refs[:6]
        dhq_ref, dz_ref, dhi_ref, dlb_ref, dst_scr = refs[6 + n_other:]

        @pl.when(pl.program_id(1) == 0)
        def _():
            dst_scr[...] = jnp.zeros_like(dst_scr)
            dlb_ref[...] = jnp.zeros_like(dlb_ref)

        pos, cid, amask = _hgrn_masks(n, rev)
        order = list(range(nc))[::-1] if rev else list(range(nc))
        for hh in range(2):
            sl = slice(hh * LANE, (hh + 1) * LANE)
            hq, lbv = hq_ref[:, sl], lb_ref[0:1, sl]
            q, sq, sig, f, g = _hgrn_gates(hq, z_ref[:, sl], lbv)
            v = hi_ref[:, sl]
            c_ = _hgrn_local(q, f, g, v, pos, cid, amask, rev, nc)
            dvec, ut_all = c_["dvec"], c_["ut_all"]
            drow = lambda c: dvec[c * HG_CHUNK:c * HG_CHUNK + 1, :]
            st = ssave_ref[0, sl, :]
            cols = [None] * nc
            for c in order:
                cols[c] = st
                st = st * drow(c) + ut_all[:, c * LANE:(c + 1) * LANE]
            dob, vb = _bf(do_ref[:, sl]), _bf(v)
            gt_all = _dot(dob, c_["qi_m"], 0, 0)
            dcur = dst_scr[hh]
            dnext = [None] * nc
            ddrow = [None] * nc
            for c in order[::-1]:
                dnext[c] = dcur
                ddrow[c] = jnp.sum(cols[c] * dcur, axis=0, keepdims=True) * drow(c)
                dcur = dcur * drow(c) + gt_all[:, c * LANE:(c + 1) * LANE]
            dst_scr[hh] = dcur
            dsn_all = _bf(jnp.concatenate(dnext, axis=1))
            st_all = _bf(jnp.concatenate(cols, axis=1))
            da = _bf(jnp.where(amask, _dot(dob, vb, 1, 1), 0.0))
            dv = _dot(_bf(c_["a"]), dob, 0, 0) + _dot(c_["kd_m"], dsn_all, 1, 1)
            dqb = _dot(da, _bf(c_["kb"]), 1, 0)
            dkb = _dot(da, _bf(c_["qb"]), 0, 0)
            dqi = _chunk_pick(_dot(dob, st_all, 1, 0), nc)
            dkd = _chunk_pick(_dot(vb, dsn_all, 1, 0), nc)
            dq = dqb * c_["ex_q"] + dqi * c_["ex_i"]
            dk = dkb * c_["ex_k"] + dkd * c_["ex_d"]
            e = dqb * c_["qb"] - dkb * c_["kb"] + dqi * c_["qi"]
            w = dkd * c_["kd"]
            dg = _chunk_scan(e, pos, rev) + (_chunk_scan(w, pos, not rev) - w) + _chunk_rows(ddrow)
            df = dg / f - dk
            dz_ref[:, sl] = _bf(df * (1.0 - lbv) * sig * (1.0 - sig))
            dlb_ref[0:1, sl] += jnp.sum(df * (1.0 - sig), axis=0, keepdims=True)
            dhq = dq * (sq * (1.0 + hq * (1.0 - sq)))
            if other is None:
                dhq_ref[:, sl], dhi_ref[:, sl] = dhq, dv
            else:
                dhq_ref[:, sl], dhi_ref[:, sl] = _bf(dhq + refs[6][:, sl]), _bf(dv + refs[7][:, sl])

    out = pl.BlockSpec((n, 2 * LANE), lambda hp, i: (blk(i), hp))
    sum_dtype = F32 if other is None else BF16
    return _call(
        body, (p, p, p, lb, do, ssave, *(other or ())), name="hgrn_bwd_rev" if rev else "hgrn_bwd", grid=(2, nb),
        in_specs=[col(3), col(z_base), col(9), pl.BlockSpec((1, 2 * LANE), lambda hp, i: (0, hp)), out,
                  pl.BlockSpec((1, 2 * LANE, LANE), lambda hp, i: (blk(i), hp, 0))] + [out] * n_other,
        out_specs=[out, out, out, pl.BlockSpec((1, 2 * LANE), lambda hp, i: (0, hp))],
        out_shape=[jax.ShapeDtypeStruct((t, HG_DIM), sum_dtype), jax.ShapeDtypeStruct((t, HG_DIM), BF16),
                   jax.ShapeDtypeStruct((t, HG_DIM), sum_dtype), jax.ShapeDtypeStruct((1, HG_DIM), F32)],
        scratch_shapes=[pltpu.VMEM((2, LANE, LANE), F32)], after=after)


def _mix_out(att, of, ob, p, gout, w_out, g_post, x, after=(), tm=256):
    t, d = x.shape
    tm = min(tm, t)

    def body(att_ref, of_ref, ob_ref, hg0_ref, hg1_ref, gout_ref, w_ref, g_ref, x_ref, cat_ref, y_ref, x1_ref):
        cat_ref[:, :ATT_Q_DIM] = att_ref[...]
        for h in range(HG_HEADS):
            sl = slice(h * LANE, (h + 1) * LANE)
            hg_ref = hg0_ref if h < 2 else hg1_ref
            hg = hg_ref[:, (h % 2) * LANE:(h % 2 + 1) * LANE]
            nrm = _rms_fwd(of_ref[:, sl] + ob_ref[:, sl], gout_ref[...])
            cat_ref[:, ATT_Q_DIM + h * LANE:ATT_Q_DIM + (h + 1) * LANE] = _bf(nrm * hg * _sigmoid(hg))
        y = _dot(cat_ref[...], w_ref[...], 1, 0)
        y_ref[...] = y
        x1_ref[...] = x_ref[...] + _rms_fwd(y, g_ref[...])

    row = lambda width: pl.BlockSpec((tm, width), lambda i: (i, 0))
    return _call(
        body, (att, of, ob, p, p, gout, w_out, g_post, x), name="mix_out", grid=(t // tm,),
        in_specs=[row(ATT_Q_DIM), row(HG_DIM), row(HG_DIM), pl.BlockSpec((tm, 2 * LANE), lambda i: (i, 11)),
                  pl.BlockSpec((tm, 2 * LANE), lambda i: (i, 12)), _full((1, LANE)), _full((d, d)), _full((1, d)), row(d)],
        out_specs=[row(d), row(d), row(d)],
        out_shape=[jax.ShapeDtypeStruct((t, d), BF16), jax.ShapeDtypeStruct((t, d), F32), jax.ShapeDtypeStruct((t, d), F32)],
        after=after)


def _rec_bwd(dcat, of, ob, p, gout, tm=256):
    t = of.shape[0]
    tm = min(tm, t)

    def body(dc_ref, of_ref, ob_ref, hg0_ref, hg1_ref, gout_ref, do_ref, dhg_ref, dgo_ref):
        @pl.when(pl.program_id(0) == 0)
        def _():
            dgo_ref[...] = jnp.zeros_like(dgo_ref)

        dgo = jnp.zeros((1, LANE), F32)
        for h in range(HG_HEADS):
            sl = slice(h * LANE, (h + 1) * LANE)
            hg_ref = hg0_ref if h < 2 else hg1_ref
            hg = hg_ref[:, (h % 2) * LANE:(h % 2 + 1) * LANE]
            o = of_ref[:, sl] + ob_ref[:, sl]
            sg = _sigmoid(hg)
            drec = dc_ref[:, sl]
            dhg_ref[:, sl] = _bf(drec * _rms_fwd(o, gout_ref[...]) * (sg * (1.0 + hg * (1.0 - sg))))
            do, dg = _rms_bwd(drec * hg * sg, o, gout_ref[...])
            do_ref[:, sl] = do
            dgo = dgo + dg
        dgo_ref[...] += dgo

    row = lambda width: pl.BlockSpec((tm, width), lambda i: (i, 0))
    return pl.pallas_call(
        body, name="rec_bwd", grid=(t // tm,),
        in_specs=[pl.BlockSpec((tm, HG_DIM), lambda i: (i, 1)), row(HG_DIM), row(HG_DIM),
                  pl.BlockSpec((tm, 2 * LANE), lambda i: (i, 11)), pl.BlockSpec((tm, 2 * LANE), lambda i: (i, 12)),
                  _full((1, LANE))],
        out_specs=[row(HG_DIM), row(HG_DIM), _full((1, LANE))],
        out_shape=[jax.ShapeDtypeStruct((t, HG_DIM), F32), jax.ShapeDtypeStruct((t, HG_DIM), BF16),
                   jax.ShapeDtypeStruct((1, LANE), F32)],
        compiler_params=_params(),
    )(dcat, of, ob, p, p, gout)


def _xattn_fwd(q, kv, tq=512):
    t, d = q.shape
    tq = min(tq, t)
    nm = kv.shape[0]

    def body(q_ref, kv_ref, o_ref):
        for h in range(X_HEADS):
            sl = slice(h * X_HEAD_DIM, (h + 1) * X_HEAD_DIM)
            s = _dot(_bf(q_ref[:, sl]), _bf(kv_ref[:, sl]), 1, 1) * (X_HEAD_DIM ** -0.5)
            e = jnp.exp(s - jnp.max(s, axis=-1, keepdims=True))
            pr = e * (1.0 / jnp.sum(e, axis=-1, keepdims=True))
            o_ref[:, sl] = _bf(_dot(_bf(pr), _bf(kv_ref[:, d + h * X_HEAD_DIM:d + (h + 1) * X_HEAD_DIM]), 1, 0))

    return pl.pallas_call(
        body, name="xattn_fwd", grid=(t // tq,),
        in_specs=[pl.BlockSpec((tq, d), lambda i: (i, 0)), _full((nm, 2 * d))],
        out_specs=pl.BlockSpec((tq, d), lambda i: (i, 0)),
        out_shape=jax.ShapeDtypeStruct((t, d), BF16),
        compiler_params=_params(),
    )(q, kv)


def _xattn_bwd(q, kv, do, tq=512):
    t, d = q.shape
    tq = min(tq, t)
    nm = kv.shape[0]

    def body(q_ref, kv_ref, do_ref, dq_ref, dkv_ref):
        @pl.when(pl.program_id(0) == 0)
        def _():
            dkv_ref[...] = jnp.zeros_like(dkv_ref)

        for h in range(X_HEADS):
            sl = slice(h * X_HEAD_DIM, (h + 1) * X_HEAD_DIM)
            slv = slice(d + h * X_HEAD_DIM, d + (h + 1) * X_HEAD_DIM)
            qb, kb, vb, dob = _bf(q_ref[:, sl]), _bf(kv_ref[:, sl]), _bf(kv_ref[:, slv]), _bf(do_ref[:, sl])
            s = _dot(qb, kb, 1, 1) * (X_HEAD_DIM ** -0.5)
            e = jnp.exp(s - jnp.max(s, axis=-1, keepdims=True))
            pr = e * (1.0 / jnp.sum(e, axis=-1, keepdims=True))
            dpr = _dot(dob, vb, 1, 1)
            ds = _bf(pr * (dpr - jnp.sum(pr * dpr, axis=-1, keepdims=True)) * (X_HEAD_DIM ** -0.5))
            dq_ref[:, sl] = _bf(_dot(ds, kb, 1, 0))
            dkv_ref[:, sl] += _dot(ds, qb, 0, 0)
            dkv_ref[:, slv] += _dot(_bf(pr), dob, 0, 0)

    return pl.pallas_call(
        body, name="xattn_bwd", grid=(t // tq,),
        in_specs=[pl.BlockSpec((tq, d), lambda i: (i, 0)), _full((nm, 2 * d)), pl.BlockSpec((tq, d), lambda i: (i, 0))],
        out_specs=[pl.BlockSpec((tq, d), lambda i: (i, 0)), _full((nm, 2 * d))],
        out_shape=[jax.ShapeDtypeStruct((t, d), BF16), jax.ShapeDtypeStruct((nm, 2 * d), F32)],
        compiler_params=_params(),
    )(q, kv, do)


CONV_TN = 256


def _shift_rows(u, row, t, delta):
    if delta < 0:
        return jnp.where(row == 0, 0.0, pltpu.roll(u, 1, 0))
    return jnp.where(row == t - 1, 0.0, pltpu.roll(u, t - 1, 0))


def _conv_gate_fwd(u, conv_w, conv_b):
    t = u.shape[0]
    nt = D_FF // CONV_TN

    def body(ug_ref, uv_ref, wg_ref, wv_ref, bg_ref, bv_ref, a_ref):
        row = lax.broadcasted_iota(jnp.int32, (t, CONV_TN), 0)

        def conv(u_ref, w_ref, b_ref):
            uu = u_ref[...]
            return (b_ref[...] + _shift_rows(uu, row, t, -1) * w_ref[0:1, :] + uu * w_ref[1:2, :]
                    + _shift_rows(uu, row, t, 1) * w_ref[2:3, :])

        gate = conv(ug_ref, wg_ref, bg_ref)
        a_ref[...] = _bf(gate * _sigmoid(gate) * conv(uv_ref, wv_ref, bv_ref))

    col = lambda rows, off: pl.BlockSpec((rows, CONV_TN), lambda j: (0, j + off))
    return pl.pallas_call(
        body, name="conv_gate_fwd", grid=(nt,),
        in_specs=[col(t, 0), col(t, nt), col(3, 0), col(3, nt), col(1, 0), col(1, nt)],
        out_specs=col(t, 0),
        out_shape=jax.ShapeDtypeStruct((t, D_FF), BF16),
        compiler_params=_params(),
    )(u, u, conv_w, conv_w, conv_b, conv_b)


def _conv_gate_bwd(u, conv_w, conv_b, da, after=()):
    t = u.shape[0]
    nt = D_FF // CONV_TN

    def body(ug_ref, uv_ref, wg_ref, wv_ref, bg_ref, bv_ref, da_ref, dug_ref, dwg_ref, dbg_ref, duv_ref, dwv_ref, dbv_ref):
        row = lax.broadcasted_iota(jnp.int32, (t, CONV_TN), 0)
        ug, uv = ug_ref[...], uv_ref[...]
        ug_m, ug_p = _shift_rows(ug, row, t, -1), _shift_rows(ug, row, t, 1)
        uv_m, uv_p = _shift_rows(uv, row, t, -1), _shift_rows(uv, row, t, 1)
        gate = bg_ref[...] + ug_m * wg_ref[0:1, :] + ug * wg_ref[1:2, :] + ug_p * wg_ref[2:3, :]
        val = bv_ref[...] + uv_m * wv_ref[0:1, :] + uv * wv_ref[1:2, :] + uv_p * wv_ref[2:3, :]
        sg = _sigmoid(gate)
        da_ = da_ref[...].astype(F32)

        def emit(dacc, um, uu, up, w_ref, du_ref, dw_ref, db_ref):
            du_ref[...] = _bf(_shift_rows(dacc, row, t, 1) * w_ref[0:1, :] + dacc * w_ref[1:2, :]
                              + _shift_rows(dacc, row, t, -1) * w_ref[2:3, :])
            dw_ref[0:1, :] = jnp.sum(dacc * um, axis=0, keepdims=True)
            dw_ref[1:2, :] = jnp.sum(dacc * uu, axis=0, keepdims=True)
            dw_ref[2:3, :] = jnp.sum(dacc * up, axis=0, keepdims=True)
            db_ref[...] = jnp.sum(dacc, axis=0, keepdims=True)

        emit(da_ * val * (sg * (1.0 + gate * (1.0 - sg))), ug_m, ug, ug_p, wg_ref, dug_ref, dwg_ref, dbg_ref)
        emit(da_ * gate * sg, uv_m, uv, uv_p, wv_ref, duv_ref, dwv_ref, dbv_ref)

    col = lambda rows, off: pl.BlockSpec((rows, CONV_TN), lambda j: (0, j + off))
    half_shapes = [jax.ShapeDtypeStruct((t, D_FF), BF16), jax.ShapeDtypeStruct((3, D_FF), F32),
                   jax.ShapeDtypeStruct((1, D_FF), F32)]
    outs = _call(
        body, (u, u, conv_w, conv_w, conv_b, conv_b, da), name="conv_gate_bwd", grid=(nt,),
        in_specs=[col(t, 0), col(t, nt), col(3, 0), col(3, nt), col(1, 0), col(1, nt), col(t, 0)],
        out_specs=[col(t, 0), col(3, 0), col(1, 0)] * 2, out_shape=half_shapes * 2, after=after)
    return outs[:3], outs[3:]


def _row_tile(r, cap):
    best = r
    for cand in range(16, cap + 1, 16):
        if r % cand == 0:
            best = cand
    return best


def _sum_parts(own, got, me, name, tr=256):
    _, r, c = got.shape
    tr = _row_tile(r, tr)

    def body(me_ref, own_ref, got_ref, o_ref):
        mine = own_ref[...].astype(F32)
        acc = None
        for i in range(N_DEV):
            term = jnp.where(me_ref[0] == i, mine, got_ref[i].astype(F32))
            acc = term if acc is None else acc + term
        o_ref[...] = acc

    return pl.pallas_call(
        body, name=name,
        grid_spec=pltpu.PrefetchScalarGridSpec(
            num_scalar_prefetch=1, grid=(r // tr,),
            in_specs=[pl.BlockSpec((None, tr, c), lambda i, me_ref: (me_ref[0], i, 0)),
                      pl.BlockSpec((N_DEV, tr, c), lambda i, me_ref: (0, i, 0))],
            out_specs=pl.BlockSpec((tr, c), lambda i, me_ref: (i, 0))),
        out_shape=jax.ShapeDtypeStruct((r, c), F32),
        compiler_params=_params(),
    )(me, own, got)


def _adamw_math(w_, g_, m_, v_):
    m_ = ADAM_B1 * m_ + (1.0 - ADAM_B1) * g_
    v_ = ADAM_B2 * v_ + (1.0 - ADAM_B2) * (g_ * g_)
    m_hat = m_ / (1.0 - ADAM_B1 ** ADAM_STEP)
    v_hat = v_ / (1.0 - ADAM_B2 ** ADAM_STEP)
    return -ADAM_LR * (m_hat / (jnp.sqrt(v_hat) + ADAM_EPS) + ADAM_WD * w_), m_, v_


def _sum_adamw(own, got, me, w, m, v, name, tr=256):
    _, r, c = got.shape
    tr = _row_tile(r, tr)

    def body(me_ref, own_ref, got_ref, w_ref, m_ref, v_ref, g_ref, d_ref, nm_ref, nv_ref):
        mine = own_ref[...].astype(F32)
        acc = None
        for i in range(N_DEV):
            term = jnp.where(me_ref[0] == i, mine, got_ref[i].astype(F32))
            acc = term if acc is None else acc + term
        g_ref[...] = acc
        d_ref[...], nm_ref[...], nv_ref[...] = _adamw_math(w_ref[...], acc, m_ref[...], v_ref[...])

    blk = pl.BlockSpec((tr, c), lambda i, me_ref: (i, 0))
    return pl.pallas_call(
        body, name=name,
        grid_spec=pltpu.PrefetchScalarGridSpec(
            num_scalar_prefetch=1, grid=(r // tr,),
            in_specs=[pl.BlockSpec((None, tr, c), lambda i, me_ref: (me_ref[0], i, 0)),
                      pl.BlockSpec((N_DEV, tr, c), lambda i, me_ref: (0, i, 0)), blk, blk, blk],
            out_specs=[blk] * 4),
        out_shape=[jax.ShapeDtypeStruct((r, c), F32)] * 4,
        compiler_params=_params(),
    )(me, own, got, w, m, v)


def _adamw(w, g, m, v, name, tr=256):
    r, c = w.shape
    tr = _row_tile(r, tr)

    def body(w_ref, g_ref, m_ref, v_ref, d_ref, nm_ref, nv_ref):
        d_ref[...], nm_ref[...], nv_ref[...] = _adamw_math(w_ref[...], g_ref[...], m_ref[...], v_ref[...])

    blk = pl.BlockSpec((tr, c), lambda i: (i, 0))
    return pl.pallas_call(
        body, name=name, grid=(r // tr,),
        in_specs=[blk] * 4, out_specs=[blk] * 3,
        out_shape=[jax.ShapeDtypeStruct((r, c), F32)] * 3,
        compiler_params=_params(),
    )(w, g, m, v)


def _mesh_pos():
    return lax.axis_index("x"), lax.axis_index("y"), lax.axis_index("c")


def _dev_index(px, py, pc):
    return 4 * px + 2 * py + pc


class _Gather:
    def __init__(self, arrs):
        self.arrs = list(arrs)
        n = len(self.arrs)
        self.out_shape = [jax.ShapeDtypeStruct((N_DEV,) + a.shape, a.dtype) for a in self.arrs]
        self.sems = [pltpu.SemaphoreType.DMA((7, n)), pltpu.SemaphoreType.DMA((7, n)), pltpu.SemaphoreType.DMA((n,))]

    def _ctx(self, ins, outs, sems):
        send_sems, recv_sems, local_sems = sems
        x, y, c = _mesh_pos()
        chips = [(1 - x, y), (x, 1 - y), (1 - x, 1 - y)]

        def copy(k, a, block, to, src=None):
            dst = outs[a].at[_dev_index(*block)]
            return pltpu.make_async_remote_copy(
                src_ref=dst if src is None else src, dst_ref=dst, send_sem=send_sems.at[k, a], recv_sem=recv_sems.at[k, a],
                device_id=to, device_id_type=MESH_T)

        n = len(ins)
        me, sibling = (x, y, c), (x, y, 1 - c)
        mine = [pltpu.make_async_copy(ins[a], outs[a].at[_dev_index(*me)], local_sems.at[a]) for a in range(n)]
        first = [copy(0, a, me, sibling, src=ins[a]) for a in range(n)]
        first += [copy(1 + j, a, me, (*chip, c), src=ins[a]) for j, chip in enumerate(chips) for a in range(n)]
        passed = [copy(4 + j, a, (*chip, c), sibling) for j, chip in enumerate(chips) for a in range(n)]
        return n, c, me, sibling, chips, copy, mine, first, passed

    def start(self, ins, outs, sems):
        _, _, _, _, _, _, mine, first, _ = self._ctx(ins, outs, sems)
        for cp in mine + first:
            cp.start()

    def forward(self, ins, outs, sems):
        n, c, me, _, chips, copy, _, _, passed = self._ctx(ins, outs, sems)
        for j, chip in enumerate(chips):
            for a in range(n):
                copy(1 + j, a, (*chip, c), me).wait_recv()
                passed[j * n + a].start()

    def finish(self, ins, outs, sems):
        n, c, me, sibling, chips, copy, mine, first, passed = self._ctx(ins, outs, sems)
        for a in range(n):
            copy(0, a, sibling, me).wait_recv()
        for j, chip in enumerate(chips):
            for a in range(n):
                copy(4 + j, a, (*chip, 1 - c), me).wait_recv()
        for cp in first + passed:
            cp.wait_send()
        for cp in mine:
            cp.wait()


def _comm_alone(comm, name):
    n = len(comm.arrs)

    def body(*refs):
        ins, outs, sems = refs[:n], refs[n:2 * n], refs[2 * n:]
        comm.start(ins, outs, sems)
        if comm.forward is not None:
            comm.forward(ins, outs, sems)
        comm.finish(ins, outs, sems)

    any_spec = pl.BlockSpec(memory_space=pl.ANY)
    return pl.pallas_call(body, name=name, in_specs=[any_spec] * n, out_specs=[any_spec] * n, out_shape=comm.out_shape,
                          scratch_shapes=comm.sems)(*comm.arrs)


def _peers(x, y, c):
    out = []
    for k in range(1, N_DEV):
        pos = (1 - x if k & 4 else x, 1 - y if k & 2 else y, 1 - c if k & 1 else c)
        out.append((k, pos, _dev_index(*pos)))
    return out


def _exchange_begin(arrs, tag):
    n = len(arrs)
    lands = [lax.empty(a.shape, a.dtype) for a in arrs]

    def start_body(*refs):
        ins, land = refs[:n], refs[n:2 * n]
        send_sems, recv_sems, token = refs[2 * n], refs[2 * n + 1], refs[-1]
        x, y, c = _mesh_pos()
        me_i = _dev_index(x, y, c)
        for k, pos, peer_i in _peers(x, y, c):
            for a in range(n):
                pltpu.make_async_remote_copy(
                    src_ref=ins[a].at[peer_i], dst_ref=land[a].at[me_i], send_sem=send_sems.at[(k - 1) * n + a],
                    recv_sem=recv_sems.at[(k - 1) * n + a], device_id=pos, device_id_type=MESH_T).start()
        token[...] = jnp.zeros_like(token)

    hbm = pl.BlockSpec(memory_space=pltpu.HBM)
    sem = pl.BlockSpec(memory_space=pltpu.SEMAPHORE)
    thru = [pltpu.HBM(a.shape, a.dtype) for a in arrs]
    outs = pl.pallas_call(
        start_body, name="exchange_start_" + tag,
        out_shape=[pltpu.SemaphoreType.DMA((7 * n,)), pltpu.SemaphoreType.DMA((7 * n,))] + thru + thru
        + [jax.ShapeDtypeStruct((8, LANE), F32)],
        in_specs=[hbm] * (2 * n), out_specs=[sem, sem] + [hbm] * (2 * n) + [pl.BlockSpec(memory_space=pltpu.VMEM)],
        input_output_aliases={i: 2 + i for i in range(2 * n)},
        compiler_params=pltpu.CompilerParams(has_side_effects=pltpu.SideEffectType.DATAFLOW_SIDE_EFFECTING),
    )(*[pltpu.with_memory_space_constraint(a, pltpu.HBM) for a in list(arrs) + list(lands)])
    return (tag, n, outs[0], outs[1], outs[2:2 + n], outs[2 + n:2 + 2 * n]), outs[-1]


def _exchange_end(handle, after):
    tag, n, send_sems, recv_sems, srcs, lands = handle

    def body(*refs):
        ins, land = refs[:n], refs[n:2 * n]
        send_sems_, recv_sems_ = refs[2 * n], refs[2 * n + 1]
        x, y, c = _mesh_pos()
        me_i = _dev_index(x, y, c)
        for k, pos, peer_i in _peers(x, y, c):
            for a in range(n):
                cp = pltpu.make_async_remote_copy(
                    src_ref=ins[a].at[peer_i], dst_ref=land[a].at[me_i], send_sem=send_sems_.at[(k - 1) * n + a],
                    recv_sem=recv_sems_.at[(k - 1) * n + a], device_id=pos, device_id_type=MESH_T)
                cp.wait_send()
                cp.wait_recv()

    hbm = pl.BlockSpec(memory_space=pltpu.HBM)
    sem = pl.BlockSpec(memory_space=pltpu.SEMAPHORE)
    outs = pl.pallas_call(
        body, name="exchange_end_" + tag, out_shape=[pltpu.HBM(a.shape, a.dtype) for a in list(srcs) + list(lands)],
        in_specs=[hbm] * (2 * n) + [sem, sem, pl.BlockSpec(memory_space=pl.ANY)], out_specs=[hbm] * (2 * n),
        input_output_aliases={i: i for i in range(2 * n)},
        compiler_params=pltpu.CompilerParams(has_side_effects=pltpu.SideEffectType.DATAFLOW_SIDE_EFFECTING),
    )(*srcs, *lands, send_sems, recv_sems, after)
    return list(zip(outs[:n], outs[n:]))


def _gather_begin(arrs, tag):
    n = len(arrs)
    me = _dev_index(*_mesh_pos())
    lands = [lax.dynamic_update_slice(lax.empty((N_DEV,) + a.shape, a.dtype), a[None], (me,) + (0,) * a.ndim) for a in arrs]

    def start_body(*refs):
        ins, land = refs[:n], refs[n:2 * n]
        send_sems, recv_sems, token = refs[2 * n], refs[2 * n + 1], refs[-1]
        x, y, c = _mesh_pos()
        me_i = _dev_index(x, y, c)
        for a in range(n):
            for k, pos, _ in _peers(x, y, c):
                pltpu.make_async_remote_copy(
                    src_ref=ins[a], dst_ref=land[a].at[me_i], send_sem=send_sems.at[(k - 1) * n + a],
                    recv_sem=recv_sems.at[(k - 1) * n + a], device_id=pos, device_id_type=MESH_T).start()
        token[...] = jnp.zeros_like(token)

    hbm = pl.BlockSpec(memory_space=pltpu.HBM)
    sem = pl.BlockSpec(memory_space=pltpu.SEMAPHORE)
    thru = [pltpu.HBM(a.shape, a.dtype) for a in list(arrs) + lands]
    outs = pl.pallas_call(
        start_body, name="gather_start_" + tag,
        out_shape=[pltpu.SemaphoreType.DMA((7 * n,)), pltpu.SemaphoreType.DMA((7 * n,))] + thru
        + [jax.ShapeDtypeStruct((8, LANE), F32)],
        in_specs=[hbm] * (2 * n), out_specs=[sem, sem] + [hbm] * (2 * n) + [pl.BlockSpec(memory_space=pltpu.VMEM)],
        input_output_aliases={i: 2 + i for i in range(2 * n)},
        compiler_params=pltpu.CompilerParams(has_side_effects=pltpu.SideEffectType.DATAFLOW_SIDE_EFFECTING),
    )(*[pltpu.with_memory_space_constraint(a, pltpu.HBM) for a in list(arrs) + lands])
    return (tag, n, outs[0], outs[1], outs[2:2 + n], outs[2 + n:2 + 2 * n]), outs[-1]


def _gather_end(handle, which, after):
    tag, n, send_sems, recv_sems, srcs, lands = handle
    m = len(which)

    def body(*refs):
        ins, land = refs[:m], refs[m:2 * m]
        send_sems_, recv_sems_ = refs[2 * m], refs[2 * m + 1]
        x, y, c = _mesh_pos()
        me_i = _dev_index(x, y, c)
        for j, a in enumerate(which):
            for k, pos, _ in _peers(x, y, c):
                cp = pltpu.make_async_remote_copy(
                    src_ref=ins[j], dst_ref=land[j].at[me_i], send_sem=send_sems_.at[(k - 1) * n + a],
                    recv_sem=recv_sems_.at[(k - 1) * n + a], device_id=pos, device_id_type=MESH_T)
                cp.wait_send()
                cp.wait_recv()

    hbm = pl.BlockSpec(memory_space=pltpu.HBM)
    sem = pl.BlockSpec(memory_space=pltpu.SEMAPHORE)
    ops = [srcs[a] for a in which] + [lands[a] for a in which]
    outs = pl.pallas_call(
        body, name="gather_end_%s_%s" % (tag, "_".join(str(a) for a in which)),
        out_shape=[pltpu.HBM(a.shape, a.dtype) for a in ops],
        in_specs=[hbm] * (2 * m) + [sem, sem, pl.BlockSpec(memory_space=pl.ANY)], out_specs=[hbm] * (2 * m),
        input_output_aliases={i: i for i in range(2 * m)},
        compiler_params=pltpu.CompilerParams(has_side_effects=pltpu.SideEffectType.DATAFLOW_SIDE_EFFECTING),
    )(*ops, send_sems, recv_sems, after)
    return list(outs[m:])


_SPLIT = dict(has_side_effects=pltpu.SideEffectType.DATAFLOW_SIDE_EFFECTING)


def _chips(x, y):
    return [(1 - x, y), (x, 1 - y), (1 - x, 1 - y)]


def _gather2_begin(arrs, tag):
    n = len(arrs)
    me = _dev_index(*_mesh_pos())
    lands = [lax.dynamic_update_slice(lax.empty((N_DEV,) + a.shape, a.dtype), a[None], (me,) + (0,) * a.ndim) for a in arrs]

    def body(*refs):
        ins, land = refs[:n], refs[n:2 * n]
        send1, recv1, token = refs[2 * n], refs[2 * n + 1], refs[-1]
        x, y, c = _mesh_pos()
        me_i = _dev_index(x, y, c)
        targets = [(x, y, 1 - c)] + [(*chip, c) for chip in _chips(x, y)]
        for a in range(n):
            for k, pos in enumerate(targets):
                pltpu.make_async_remote_copy(
                    src_ref=ins[a], dst_ref=land[a].at[me_i], send_sem=send1.at[k * n + a], recv_sem=recv1.at[k * n + a],
                    device_id=pos, device_id_type=MESH_T).start()
        token[...] = jnp.zeros_like(token)

    hbm = pl.BlockSpec(memory_space=pltpu.HBM)
    sem = pl.BlockSpec(memory_space=pltpu.SEMAPHORE)
    thru = [pltpu.HBM(a.shape, a.dtype) for a in list(arrs) + lands]
    outs = pl.pallas_call(
        body, name="gather_start_" + tag,
        out_shape=[pltpu.SemaphoreType.DMA((4 * n,)), pltpu.SemaphoreType.DMA((4 * n,))] + thru
        + [jax.ShapeDtypeStruct((8, LANE), F32)],
        in_specs=[hbm] * (2 * n), out_specs=[sem, sem] + [hbm] * (2 * n) + [pl.BlockSpec(memory_space=pltpu.VMEM)],
        input_output_aliases={i: 2 + i for i in range(2 * n)}, compiler_params=pltpu.CompilerParams(**_SPLIT),
    )(*[pltpu.with_memory_space_constraint(a, pltpu.HBM) for a in list(arrs) + lands])
    return dict(tag=tag, n=n, send1=outs[0], recv1=outs[1], srcs=list(outs[2:2 + n]), lands=list(outs[2 + n:2 + 2 * n])), outs[-1]


def _gather2_forward(handle, which, after, part):
    n, m = handle["n"], len(which)

    def body(*refs):
        land, recv1 = refs[:m], refs[m]
        send2, recv2, token = refs[m + 2], refs[m + 3], refs[-1]
        token[...] = jnp.zeros_like(token)
        x, y, c = _mesh_pos()
        for j, a in enumerate(which):
            for k, chip in enumerate(_chips(x, y)):
                blk = land[j].at[_dev_index(*chip, c)]
                copy = pltpu.make_async_remote_copy(
                    src_ref=blk, dst_ref=blk, send_sem=send2.at[k * m + j], recv_sem=recv2.at[k * m + j],
                    device_id=(x, y, 1 - c), device_id_type=MESH_T)
                pltpu.make_async_remote_copy(
                    src_ref=blk, dst_ref=blk, send_sem=send2.at[k * m + j], recv_sem=recv1.at[(1 + k) * n + a],
                    device_id=(*chip, c), device_id_type=MESH_T).wait_recv()
                copy.start()

    hbm = pl.BlockSpec(memory_space=pltpu.HBM)
    sem = pl.BlockSpec(memory_space=pltpu.SEMAPHORE)
    lands = [handle["lands"][a] for a in which]
    outs = pl.pallas_call(
        body, name="gather_forward_%s_%s" % (handle["tag"], part),
        out_shape=[pltpu.SemaphoreType.DMA((3 * m,)), pltpu.SemaphoreType.DMA((3 * m,))] + [pltpu.HBM(a.shape, a.dtype) for a in lands]
        + [jax.ShapeDtypeStruct((8, LANE), F32)],
        in_specs=[hbm] * m + [sem, pl.BlockSpec(memory_space=pl.ANY)],
        out_specs=[sem, sem] + [hbm] * m + [pl.BlockSpec(memory_space=pltpu.VMEM)],
        input_output_aliases={i: 2 + i for i in range(m)}, compiler_params=pltpu.CompilerParams(**_SPLIT),
    )(*lands, handle["recv1"], after)
    for j, a in enumerate(which):
        handle["lands"][a] = outs[2 + j]
        handle[("leg2", a)] = (outs[0], outs[1], j, m)
    return outs[-1]


def _gather2_end(handle, a, after):
    n = handle["n"]
    send2, recv2, j, m = handle[("leg2", a)]

    def body(src, land, send1, recv1, send2_, recv2_, after_ref, src_out, land_out):
        x, y, c = _mesh_pos()
        me_i = _dev_index(x, y, c)
        sibling = (x, y, 1 - c)
        for k, pos in enumerate([sibling] + [(*chip, c) for chip in _chips(x, y)]):
            first = pltpu.make_async_remote_copy(
                src_ref=src, dst_ref=land.at[me_i], send_sem=send1.at[k * n + a], recv_sem=recv1.at[k * n + a],
                device_id=pos, device_id_type=MESH_T)
            first.wait_send()
            if k == 0:
                first.wait_recv()
        for k in range(3):
            second = pltpu.make_async_remote_copy(
                src_ref=land.at[me_i], dst_ref=land.at[me_i], send_sem=send2_.at[k * m + j], recv_sem=recv2_.at[k * m + j],
                device_id=sibling, device_id_type=MESH_T)
            second.wait_send()
            second.wait_recv()

    hbm = pl.BlockSpec(memory_space=pltpu.HBM)
    sem = pl.BlockSpec(memory_space=pltpu.SEMAPHORE)
    src, land = handle["srcs"][a], handle["lands"][a]
    outs = pl.pallas_call(
        body, name="gather_end_%s_%d" % (handle["tag"], a),
        out_shape=[pltpu.HBM(src.shape, src.dtype), pltpu.HBM(land.shape, land.dtype)],
        in_specs=[hbm, hbm, sem, sem, sem, sem, pl.BlockSpec(memory_space=pl.ANY)], out_specs=[hbm, hbm],
        input_output_aliases={0: 0, 1: 1}, compiler_params=pltpu.CompilerParams(**_SPLIT),
    )(src, land, handle["send1"], handle["recv1"], send2, recv2, after)
    return outs[1]


def _call(body, operands, *, name, grid, in_specs, out_specs, out_shape, scratch_shapes=(), after=()):
    ni, nd = len(in_specs), len(after)

    def ordered(*refs):
        body(*refs[:ni], *refs[ni + nd:])

    outs = pl.pallas_call(
        ordered, name=name, grid=grid, in_specs=list(in_specs) + [pl.BlockSpec(memory_space=pl.ANY)] * nd,
        out_specs=out_specs, out_shape=out_shape, scratch_shapes=list(scratch_shapes), compiler_params=_params(),
    )(*operands, *after)
    return list(outs)


def _rows(a):
    return a.reshape(-1, a.shape[-1])


def _slots(a):
    return a.reshape(N_DEV, -1, a.shape[-1])


def _local_step(x, mem, tgt, w, wire):
    tables = _rope_tables(x.shape[0])
    bd = _head_sum_matrix()
    two = lambda g: jnp.tile(g, (1, 2))
    gq2, gk2 = two(w["q_norm_g"]), two(w["k_norm_g"])

    w_in_t, hg_lb, conv_w = _comm_alone(_Gather([wire["w_in"], w["hg_lb"].reshape(4, -1), w["conv_w"][0]]), "gather_w_in")
    w_in_t = _rows(w_in_t)
    hg_lb = jnp.transpose(hg_lb.reshape(N_DEV, 2, 2, -1), (1, 2, 0, 3)).reshape(2, 2, HG_DIM)
    conv_w = jnp.transpose(conv_w, (1, 0, 2)).reshape(3, 2 * D_FF)
    lb_a0, lb_a1 = hg_lb[:, 0, :], hg_lb[:, 1, :]

    order = ("w_out", "w_xq", "w_xkv", "w_xo", "w_up", "w_down")
    w_in_t, *later = lax.optimization_barrier((w_in_t, *[wire[n] for n in order]))
    fetch, started = _gather2_begin(later, "w")
    take = lambda n, after: _rows(_gather2_end(fetch, order.index(n), after))

    h1, p = _norm_mm(x, w["pre_mix_g"], w_in_t, True, N_IN, "in_proj", after=[started])
    qr, kr = _qk_prep(p, gq2, gk2, tables, bd)
    att, att32, lse = _attn_fwd(qr, kr, p)
    passed = _gather2_forward(fetch, [0, 1, 2, 3], att, "a")
    lb = _lower_bounds(lb_a0, lb_a1)
    of, s_f = _hgrn_fwd(p, lb[0:1], False, after=[passed])
    ob, s_b = _hgrn_fwd(p, lb[1:2], True, after=[passed, of])
    passed = _gather2_forward(fetch, [4, 5], ob, "b")
    w_out = take("w_out", ob)
    cat, mixed, x1 = _mix_out(att, of, ob, p, w["hg_out_norm_g"], w_out, w["post_mix_g"], x, after=[passed])

    w_xq = take("w_xq", x1)
    h2, q2 = _norm_mm(x1, w["pre_x_g"], w_xq, False, 1024, "xq_proj")
    w_xkv_t = take("w_xkv", q2)
    mn, kv = _norm_mm(mem, w["mem_norm_g"], w_xkv_t, True, 2 * D_MODEL, "xkv_proj")
    o2 = _xattn_fwd(q2, kv)
    w_xo = take("w_xo", o2)
    y2, x2 = _mm_postnorm_res(o2, w_xo, w["post_x_g"], x1, "xo_proj")

    w_up_t = take("w_up", x2)
    h3, u = _norm_mm(x2, w["pre_ffn_g"], w_up_t, True, 2 * D_FF, "up_proj", tm=256)
    a = _conv_gate_fwd(u, conv_w, w["conv_b"])
    w_down = take("w_down", a)
    y3, dx3, loss = _mm_postnorm_res_loss(a, w_down, w["post_ffn_g"], x2, tgt, "down_proj")

    g, pending = {}, {}
    dy3, da, g["post_ffn_g"] = _postnorm_bwd_mm(dx3, y3, w["post_ffn_g"], w_down, "down_bwd", BF16)
    pending["down"], started = _exchange_begin([_slots(_dw(a, dy3, "dw_down", tka=D_FF // 2))], "down")
    (du_g, dcw_g, dcb_g), (du_v, dcw_v, dcb_v) = _conv_gate_bwd(u, conv_w, w["conv_b"], da, after=[started])
    g["conv_w"] = jnp.concatenate([dcw_g, dcw_v], axis=1)
    g["conv_b"] = jnp.concatenate([dcb_g, dcb_v], axis=1)
    pending["up"], started = _exchange_begin([_slots(_dw([du_g, du_v], h3, "dw_up", tka=D_FF // 2))], "up")
    dx2, g["pre_ffn_g"] = _mm_prenorm_bwd([du_g, du_v], w_up_t, True, x2, w["pre_ffn_g"], dx3, "up_bwd", after=[started], tm=256)

    dy2, do2, g["post_x_g"] = _postnorm_bwd_mm(dx2, y2, w["post_x_g"], w_xo, "xo_bwd", BF16)
    dw_xo = _dw(o2, dy2, "dw_xo", tka=512)
    dq2, dkv = _xattn_bwd(q2, kv, do2)
    dw_xq = _dw(h2, dq2, "dw_xq", tka=512)
    dkvb = _bf(dkv)
    dw_xkv = _dw(dkvb, mn, "dw_xkv")
    pending["x"], started = _exchange_begin([_slots(dw_xo), _slots(dw_xq), _slots(dw_xkv)], "x")
    _, g["mem_norm_g"] = _mm_prenorm_bwd(dkvb, w_xkv_t, True, mem, w["mem_norm_g"], jnp.zeros_like(mem), "xkv_bwd")
    dx1, g["pre_x_g"] = _mm_prenorm_bwd(dq2, w_xq, False, x1, w["pre_x_g"], dx2, "xq_bwd", after=[started])

    dmixed, dcat, g["post_mix_g"] = _postnorm_bwd_mm(dx1, mixed, w["post_mix_g"], w_out, "out_bwd")
    pending["out"], started = _exchange_begin([_slots(_dw(cat, dmixed, "dw_out", tka=512))], "out")
    do, dhg, g["hg_out_norm_g"] = _rec_bwd(dcat, of, ob, p, w["hg_out_norm_g"])
    dhq_f, dz_f, dhi_f, dlb_f = _hgrn_bwd(p, lb[0:1], do, s_f, False, after=[started])
    dhq, dz_b, dhi, dlb_b = _hgrn_bwd(p, lb[1:2], do, s_b, True, other=(dhq_f, dhi_f))
    d_a0, d_a1 = _lower_bounds_bwd(lb_a0, lb_a1, jnp.concatenate([dlb_f, dlb_b], axis=0))
    g["hg_lb"] = jnp.stack([d_a0, d_a1], axis=1)
    dqr, dkr, dv = _attn_bwd(qr, kr, p, dcat, att32, lse)
    dp_att, dgq, dgk = _qk_prep_bwd(p, dqr, dkr, dv, gq2, gk2, tables, bd)
    g["q_norm_g"], g["k_norm_g"] = dgq, dgk
    dp = [dp_att, dhq, dz_f, dz_b, dhi, dhg]
    pending["in"], started = _exchange_begin([_slots(_dw(dp, h1, "dw_in"))], "in")
    dx, g["pre_mix_g"] = _mm_prenorm_bwd(dp, w_in_t, True, x, w["pre_mix_g"], dx1, "in_bwd", after=[started])
    return loss, dx, g, pending


_COL_SHARDED = ("w_in", "w_xkv", "w_up")
_ROW_SHARDED = ("w_out", "w_xq", "w_xo", "w_down")
_REPLICATED = ("pre_mix_g", "q_norm_g", "k_norm_g", "hg_out_norm_g", "post_mix_g", "pre_x_g", "mem_norm_g", "post_x_g",
               "pre_ffn_g", "conv_b", "post_ffn_g")
_WEIGHTS = ("pre_mix_g", "w_in", "q_norm_g", "k_norm_g", "hg_lb", "hg_out_norm_g", "w_out", "post_mix_g", "pre_x_g",
            "mem_norm_g", "w_xq", "w_xkv", "w_xo", "post_x_g", "pre_ffn_g", "w_up", "conv_w", "conv_b", "w_down",
            "post_ffn_g")
_ADAM_TRANSPOSED = ("w_in", "w_up")
PACK_W = 1024


def _small_plan(shapes):
    plan, r = [], 0
    for vi, (rows, cols) in enumerate(shapes):
        for i in range(rows):
            for c0 in range(0, cols, PACK_W):
                plan.append((vi, i, c0, min(PACK_W, cols - c0), r))
                r += 1
    return plan, -(-r // 8) * 8


def _pack_small(vals):
    plan, nrows = _small_plan([val.shape for val in vals])

    def body(*refs):
        ins, out = refs[:-1], refs[-1]
        out[...] = jnp.zeros_like(out)
        for vi, i, c0, width, r in plan:
            out[r:r + 1, 0:width] = ins[vi][i:i + 1, c0:c0 + width]

    return pl.pallas_call(body, name="pack_small", out_shape=jax.ShapeDtypeStruct((nrows, PACK_W), F32))(*vals)


def _sum_unpack_small(packs, shapes):
    plan, _ = _small_plan(shapes)

    def body(*refs):
        p_ref, outs = refs[0], refs[1:]
        acc = p_ref[0]
        for i in range(1, N_DEV):
            acc = acc + p_ref[i]
        for vi, i, c0, width, r in plan:
            outs[vi][i:i + 1, c0:c0 + width] = acc[r:r + 1, 0:width]

    return pl.pallas_call(body, name="sum_unpack_small", out_shape=[jax.ShapeDtypeStruct(s, F32) for s in shapes])(packs)


def _adamw_many(ws, gs, ms, vs):
    n = len(ws)

    def body(*refs):
        w_refs, g_refs, m_refs, v_refs = (refs[k * n:(k + 1) * n] for k in range(4))
        d_refs, nm_refs, nv_refs = (refs[(4 + k) * n:(5 + k) * n] for k in range(3))
        for k in range(n):
            g_ = g_refs[k][...]
            m_ = ADAM_B1 * m_refs[k][...] + (1.0 - ADAM_B1) * g_
            v_ = ADAM_B2 * v_refs[k][...] + (1.0 - ADAM_B2) * (g_ * g_)
            m_hat = m_ / (1.0 - ADAM_B1 ** ADAM_STEP)
            v_hat = v_ / (1.0 - ADAM_B2 ** ADAM_STEP)
            d_refs[k][...] = -ADAM_LR * (m_hat / (jnp.sqrt(v_hat) + ADAM_EPS) + ADAM_WD * w_refs[k][...])
            nm_refs[k][...] = m_
            nv_refs[k][...] = v_

    shapes = [jax.ShapeDtypeStruct(a.shape, F32) for a in ws]
    outs = pl.pallas_call(body, name="adamw_small", out_shape=shapes * 3)(*ws, *gs, *ms, *vs)
    return outs[:n], outs[n:2 * n], outs[2 * n:]


def kernel(x, mem, pre_mix_g, w_in, q_norm_g, k_norm_g, hg_lb, hg_out_norm_g, w_out, post_mix_g, pre_x_g, mem_norm_g, w_xq, w_xkv, w_xo, post_x_g, pre_ffn_g, w_up, conv_w, conv_b, w_down, post_ffn_g, loss_target, m_pre_mix_g, m_w_in, m_q_norm_g, m_k_norm_g, m_hg_lb, m_hg_out_norm_g, m_w_out, m_post_mix_g, m_pre_x_g, m_mem_norm_g, m_w_xq, m_w_xkv, m_w_xo, m_post_x_g, m_pre_ffn_g, m_w_up, m_conv_w, m_conv_b, m_w_down, m_post_ffn_g, v_pre_mix_g, v_w_in, v_q_norm_g, v_k_norm_g, v_hg_lb, v_hg_out_norm_g, v_w_out, v_post_mix_g, v_pre_x_g, v_mem_norm_g, v_w_xq, v_w_xkv, v_w_xo, v_post_x_g, v_pre_ffn_g, v_w_up, v_conv_w, v_conv_b, v_w_down, v_post_ffn_g):
    args = dict(locals())
    w = {n: args[n] for n in _WEIGHTS}
    m = {n: args["m_" + n] for n in _WEIGHTS}
    v = {n: args["v_" + n] for n in _WEIGHTS}
    me = _dev_index(*_mesh_pos())

    wire = {n: _bf(w[n][0].T) for n in _COL_SHARDED}
    wire.update({n: _bf(w[n][0]) for n in _ROW_SHARDED})

    loss, grad_x, g, pending = _local_step(x[0], mem[0], loss_target[0], w, wire)

    grads, delta, new_m, new_v = {}, {}, {}, {}

    me_arr = jnp.reshape(me, (1,)).astype(jnp.int32)

    def update(n, parts):
        if n in _ADAM_TRANSPOSED:
            outs = _sum_adamw(*parts, me_arr, w[n][0].T, m[n][0].T, v[n][0].T, "update_" + n)
            grads[n], delta[n], new_m[n], new_v[n] = (a.T[None] for a in outs)
        elif n in _COL_SHARDED:
            gsum = _sum_parts(*parts, me_arr, "sum_" + n).T
            grads[n] = gsum[None]
            delta[n], new_m[n], new_v[n] = (a[None] for a in _adamw(w[n][0], gsum, m[n][0], v[n][0], "adamw_" + n))
        else:
            outs = _sum_adamw(*parts, me_arr, w[n][0], m[n][0], v[n][0], "update_" + n)
            grads[n], delta[n], new_m[n], new_v[n] = (a[None] for a in outs)

    small = list(_REPLICATED) + ["hg_lb", "conv_w"]
    vals = [g[n] for n in _REPLICATED] + [g["hg_lb"].reshape(4, HG_DIM), g["conv_w"], jnp.pad(loss, ((0, 0), (0, LANE - 1)))]
    shapes = [val.shape for val in vals]
    fetch_small, _ = _gather_begin([_pack_small(vals)], "small")

    after = grad_x
    for tag, names in (("down", ["w_down"]), ("up", ["w_up"]), ("x", ["w_xo", "w_xq", "w_xkv"]), ("out", ["w_out"]),
                       ("in", ["w_in"])):
        for n, parts in zip(names, _exchange_end(pending[tag], after)):
            update(n, parts)
            after = new_v[n]

    (packs,) = _gather_end(fetch_small, [0], after)
    summed = _sum_unpack_small(packs, shapes)
    loss = summed[-1][0, 0]
    for n, s in zip(small, summed[:-1]):
        grads[n] = s
    fold = lambda v2: v2[:, :ATT_HEAD_DIM] + v2[:, ATT_HEAD_DIM:]
    grads["q_norm_g"], grads["k_norm_g"] = fold(grads["q_norm_g"]), fold(grads["k_norm_g"])
    grads["hg_lb"] = lax.dynamic_slice_in_dim(grads["hg_lb"].reshape(2, 2, HG_DIM), me * (HG_DIM // N_DEV),
                                              HG_DIM // N_DEV, axis=2)
    grads["conv_w"] = lax.dynamic_slice_in_dim(grads["conv_w"], me * (2 * D_FF // N_DEV), 2 * D_FF // N_DEV, axis=1)[None]

    flat2 = lambda a: a.reshape(-1, a.shape[-1])
    outs = _adamw_many(*[[flat2(d[n]) for n in small] for d in (w, grads, m, v)])
    for dst, vals in zip((delta, new_m, new_v), outs):
        for n, val in zip(small, vals):
            dst[n] = val.reshape(w[n].shape)

    return (loss, grad_x[None], *[grads[n] for n in _WEIGHTS], *[delta[n] for n in _WEIGHTS],
            *[new_m[n] for n in _WEIGHTS], *[new_v[n] for n in _WEIGHTS])
```

```python
import jax
import jax.numpy as jnp
from jax import lax
from jax.experimental import pallas as pl
from jax.experimental.pallas import tpu as pltpu

F32 = jnp.float32
BF16 = jnp.bfloat16

D_MODEL = 1024
GRID_W = 64
EPS = 1e-6
ATT_HEADS = 8
ATT_HEAD_DIM = 64
ATT_Q_DIM = 512
ATT_KV_DIM = 128
ROPE_THETA = 10000.0
HG_HEADS = 4
HG_DIM = 512
HG_CHUNK = 32
HG_CHUNK_LOG2 = 5
HG_BLOCK_FWD = 256
HG_BLOCK_BWD = 128
N_IN = 3328
X_HEADS = 4
X_HEAD_DIM = 256
D_FF = 2816
N_DEV = 8
LANE = 128
ADAM_LR = 0.001
ADAM_B1 = 0.9
ADAM_B2 = 0.999
ADAM_EPS = 1e-08
ADAM_WD = 0.01
ADAM_STEP = 10
VMEM_LIMIT = 56 * 1024 * 1024

MESH_T = pl.DeviceIdType.MESH


def _params(**kw):
    return pltpu.CompilerParams(vmem_limit_bytes=VMEM_LIMIT, **kw)


def _dot(a, b, ca, cb):
    return lax.dot_general(a, b, (((ca,), (cb,)), ((), ())), preferred_element_type=F32)


def _bf(x):
    return x.astype(BF16)


def _sigmoid(x):
    return 1.0 / (1.0 + jnp.exp(-x))


def _rms_fwd(x, g):
    r = lax.rsqrt(jnp.mean(x * x, axis=-1, keepdims=True) + EPS)
    return x * r * g


def _rms_bwd(dy, x, g):
    r = lax.rsqrt(jnp.mean(x * x, axis=-1, keepdims=True) + EPS)
    xh = x * r
    dg = jnp.sum(dy * xh, axis=0, keepdims=True)
    t = dy * g
    dx = r * (t - xh * jnp.mean(t * xh, axis=-1, keepdims=True))
    return dx, dg


def _full(shape):
    nd = len(shape)
    return pl.BlockSpec(shape, lambda *a: (0,) * nd)


def _norm_mm(x, g, w, trans, tn, name, after=(), tm=512):
    t, d = x.shape
    n = w.shape[0] if trans else w.shape[1]
    tm = min(tm, t)

    def body(x_ref, g_ref, w_ref, h_ref, p_ref):
        h = _bf(_rms_fwd(x_ref[...], g_ref[...]))
        h_ref[...] = h
        p_ref[...] = _dot(h, w_ref[...], 1, 1 if trans else 0)

    w_spec = pl.BlockSpec((tn, d), lambda i, j: (j, 0)) if trans else pl.BlockSpec((d, tn), lambda i, j: (0, j))
    return _call(
        body, (x, g, w), name=name, grid=(t // tm, n // tn),
        in_specs=[pl.BlockSpec((tm, d), lambda i, j: (i, 0)), _full((1, d)), w_spec],
        out_specs=[pl.BlockSpec((tm, d), lambda i, j: (i, 0)), pl.BlockSpec((tm, tn), lambda i, j: (i, j))],
        out_shape=[jax.ShapeDtypeStruct((t, d), BF16), jax.ShapeDtypeStruct((t, n), F32)], after=after)


def _mm_postnorm_res(a, w, g, res, name, tm=256):
    t, k = a.shape
    d = w.shape[1]

    def body(a_ref, w_ref, g_ref, res_ref, y_ref, o_ref):
        y = _dot(a_ref[...], w_ref[...], 1, 0)
        y_ref[...] = y
        o_ref[...] = res_ref[...] + _rms_fwd(y, g_ref[...])

    row = lambda width: pl.BlockSpec((tm, width), lambda i: (i, 0))
    return pl.pallas_call(
        body, name=name, grid=(t // tm,),
        in_specs=[row(k), _full((k, d)), _full((1, d)), row(d)],
        out_specs=[row(d), row(d)],
        out_shape=[jax.ShapeDtypeStruct((t, d), F32)] * 2,
        compiler_params=_params(),
    )(a, w, g, res)


def _mm_postnorm_res_loss(a, w, g, res, tgt, name, tm=256):
    t, k = a.shape
    d = w.shape[1]

    def body(a_ref, w_ref, g_ref, res_ref, tgt_ref, y_ref, dout_ref, loss_ref):
        @pl.when(pl.program_id(0) == 0)
        def _():
            loss_ref[...] = jnp.zeros_like(loss_ref)

        y = _dot(a_ref[...], w_ref[...], 1, 0)
        y_ref[...] = y
        diff = res_ref[...] + _rms_fwd(y, g_ref[...]) - tgt_ref[...]
        dout_ref[...] = diff * (1.0 / d)
        part = jnp.sum(jnp.sum(diff * diff, axis=-1, keepdims=True), axis=0, keepdims=True)
        loss_ref[...] += (0.5 / d) * part

    row = lambda width: pl.BlockSpec((tm, width), lambda i: (i, 0))
    return pl.pallas_call(
        body, name=name, grid=(t // tm,),
        in_specs=[row(k), _full((k, d)), _full((1, d)), row(d), row(d)],
        out_specs=[row(d), row(d), _full((1, 1))],
        out_shape=[jax.ShapeDtypeStruct((t, d), F32)] * 2 + [jax.ShapeDtypeStruct((1, 1), F32)],
        compiler_params=_params(),
    )(a, w, g, res, tgt)


def _postnorm_bwd_mm(dout, y, g, w, name, da_dtype=F32, tm=256):
    t, d = y.shape
    k = w.shape[0]

    def body(dout_ref, y_ref, g_ref, w_ref, dy_ref, da_ref, dg_ref):
        @pl.when(pl.program_id(0) == 0)
        def _():
            dg_ref[...] = jnp.zeros_like(dg_ref)

        dy, dg = _rms_bwd(dout_ref[...], y_ref[...], g_ref[...])
        dg_ref[...] += dg
        dyb = _bf(dy)
        dy_ref[...] = dyb
        da_ref[...] = _dot(dyb, w_ref[...], 1, 1).astype(da_dtype)

    row = lambda width: pl.BlockSpec((tm, width), lambda i: (i, 0))
    return pl.pallas_call(
        body, name=name, grid=(t // tm,),
        in_specs=[row(d), row(d), _full((1, d)), _full((k, d))],
        out_specs=[row(d), row(k), _full((1, d))],
        out_shape=[jax.ShapeDtypeStruct((t, d), BF16), jax.ShapeDtypeStruct((t, k), da_dtype), jax.ShapeDtypeStruct((1, d), F32)],
        compiler_params=_params(),
    )(dout, y, g, w)


def _mm_prenorm_bwd(dp, w, trans, x, g, dres, name, after=(), tm=512):
    dps = list(dp) if isinstance(dp, (list, tuple)) else [dp]
    nparts = len(dps)
    t = dps[0].shape[0]
    widths = [part.shape[1] for part in dps]
    d = x.shape[1]
    tm = min(tm, t)

    def body(*refs):
        dp_refs = refs[:nparts]
        w_ref, x_ref, g_ref, dres_ref, dx_ref, dg_ref = refs[nparts:]

        @pl.when(pl.program_id(0) == 0)
        def _():
            dg_ref[...] = jnp.zeros_like(dg_ref)

        dh = None
        for k, dp_ref in enumerate(dp_refs):
            cols = slice(sum(widths[:k]), sum(widths[:k + 1]))
            term = _dot(dp_ref[...], w_ref[cols, :], 1, 0) if trans else _dot(dp_ref[...], w_ref[:, cols], 1, 1)
            dh = term if dh is None else dh + term
        dx, dg = _rms_bwd(dh, x_ref[...], g_ref[...])
        dg_ref[...] += dg
        dx_ref[...] = dres_ref[...] + dx

    row = lambda width: pl.BlockSpec((tm, width), lambda i: (i, 0))
    return _call(
        body, (*dps, w, x, g, dres), name=name, grid=(t // tm,),
        in_specs=[row(width) for width in widths] + [_full(w.shape), row(d), _full((1, d)), row(d)],
        out_specs=[row(d), _full((1, d))],
        out_shape=[jax.ShapeDtypeStruct((t, d), F32), jax.ShapeDtypeStruct((1, d), F32)], after=after)


def _dw(a, b, name, tka=256):
    parts = list(a) if isinstance(a, (list, tuple)) else [a]
    nparts = len(parts)
    t, nb = b.shape
    tiles = [part.shape[1] // tka for part in parts]
    first = [sum(tiles[:k]) for k in range(nparts)]

    def body(*refs):
        a_refs, b_ref, o_ref = refs[:nparts], refs[nparts], refs[nparts + 1]
        i = pl.program_id(0)
        for k, a_ref in enumerate(a_refs):
            @pl.when((i >= first[k]) & (i < first[k] + tiles[k]))
            def _():
                o_ref[...] = _bf(_dot(a_ref[...], b_ref[...], 0, 0))

    a_spec = lambda k: pl.BlockSpec((t, tka), lambda i: (0, jnp.clip(i - first[k], 0, tiles[k] - 1)))
    return pl.pallas_call(
        body, name=name, grid=(sum(tiles),),
        in_specs=[a_spec(k) for k in range(nparts)] + [_full((t, nb))],
        out_specs=pl.BlockSpec((tka, nb), lambda i: (i, 0)),
        out_shape=jax.ShapeDtypeStruct((sum(tiles) * tka, nb), BF16),
        compiler_params=_params(),
    )(*parts, b)


def _rope_tables(t):
    pos = jnp.arange(t)
    r = (pos // GRID_W).astype(F32)
    c = (pos % GRID_W).astype(F32)
    npair = ATT_HEAD_DIM // 4
    inv = jnp.power(ROPE_THETA, -jnp.arange(npair, dtype=F32) / npair)
    ang = jnp.concatenate([r[:, None] * inv, c[:, None] * inv], axis=-1)
    cos = jnp.repeat(jnp.cos(ang), 2, axis=-1)
    sin = jnp.repeat(jnp.sin(ang), 2, axis=-1)
    even = (jnp.arange(ATT_HEAD_DIM) % 2) == 0
    sa = jnp.where(even, -sin, 0.0)
    sb = jnp.where(even, 0.0, sin)
    two = lambda a: jnp.tile(a, (1, 2))
    return two(cos), two(sa), two(sb)


def _head_sum_matrix():
    a = jnp.arange(LANE) // ATT_HEAD_DIM
    return (a[:, None] == a[None, :]).astype(BF16)


def _head_mean(v, bd):
    hi = _bf(v)
    lo = _bf(v - hi.astype(F32))
    return (_dot(hi, bd, 1, 0) + _dot(lo, bd, 1, 0)) * (1.0 / ATT_HEAD_DIM)


def _qk_prep(p, gq, gk, tables, bd, tm=512):
    t = p.shape[0]
    tm = min(tm, t)
    cc, sa, sb = tables

    def body(p_ref, gq_ref, gk_ref, cc_ref, sa_ref, sb_ref, bd_ref, q_ref, k_ref):
        cc_, sa_, sb_, bd_ = cc_ref[...], sa_ref[...], sb_ref[...], bd_ref[...]
        low = lax.broadcasted_iota(jnp.int32, (tm, LANE), 1) < ATT_HEAD_DIM

        def normrope(xs, g):
            xn = xs * lax.rsqrt(_head_mean(xs * xs, bd_) + EPS) * g
            return xn * cc_ + pltpu.roll(xn, LANE - 1, 1) * sa_ + pltpu.roll(xn, 1, 1) * sb_

        for j in range(4):
            y = normrope(p_ref[:, j * LANE:(j + 1) * LANE], gq_ref[...]) * (ATT_HEAD_DIM ** -0.5)
            yr = pltpu.roll(y, ATT_HEAD_DIM, 1)
            if j // 2 == 0:
                h0, h1 = jnp.where(low, y, 0.0), jnp.where(low, yr, 0.0)
            else:
                h0, h1 = jnp.where(low, 0.0, yr), jnp.where(low, 0.0, y)
            q_ref[:, (2 * j) * LANE:(2 * j + 1) * LANE] = _bf(h0)
            q_ref[:, (2 * j + 1) * LANE:(2 * j + 2) * LANE] = _bf(h1)
        k_ref[...] = _bf(normrope(p_ref[:, ATT_Q_DIM:ATT_Q_DIM + LANE], gk_ref[...]))

    row = lambda width: pl.BlockSpec((tm, width), lambda i: (i, 0))
    return pl.pallas_call(
        body, name="qk_prep", grid=(t // tm,),
        in_specs=[row(ATT_Q_DIM + LANE), _full((1, LANE)), _full((1, LANE)), row(LANE), row(LANE), row(LANE),
                  _full((LANE, LANE))],
        out_specs=[row(ATT_HEADS * LANE), row(LANE)],
        out_shape=[jax.ShapeDtypeStruct((t, ATT_HEADS * LANE), BF16), jax.ShapeDtypeStruct((t, LANE), BF16)],
        compiler_params=_params(),
    )(p, gq, gk, cc, sa, sb, bd)


def _qk_prep_bwd(p, dq, dk, dv, gq, gk, tables, bd, tm=512):
    t = p.shape[0]
    tm = min(tm, t)
    cc, sa, sb = tables
    att_w = ATT_Q_DIM + 2 * ATT_KV_DIM

    def body(p_ref, dq_ref, dk_ref, dv_ref, gq_ref, gk_ref, cc_ref, sa_ref, sb_ref, bd_ref, dp_ref, dgq_ref, dgk_ref):
        dp_ref[:, ATT_Q_DIM + LANE:att_w] = _bf(dv_ref[...])

        @pl.when(pl.program_id(0) == 0)
        def _():
            dgq_ref[...] = jnp.zeros_like(dgq_ref)
            dgk_ref[...] = jnp.zeros_like(dgk_ref)

        cc_, sa_, sb_, bd_ = cc_ref[...], sa_ref[...], sb_ref[...], bd_ref[...]
        low = lax.broadcasted_iota(jnp.int32, (tm, LANE), 1) < ATT_HEAD_DIM

        def bwd(xs, g, dy):
            r = lax.rsqrt(_head_mean(xs * xs, bd_) + EPS)
            xh = xs * r
            dxn = dy * cc_ + pltpu.roll(dy * sa_, 1, 1) + pltpu.roll(dy * sb_, LANE - 1, 1)
            dg = jnp.sum(dxn * xh, axis=0, keepdims=True)
            tt = dxn * g
            return r * (tt - xh * _head_mean(tt * xh, bd_)), dg

        dgq = jnp.zeros((1, LANE), F32)
        for j in range(4):
            d0 = dq_ref[:, (2 * j) * LANE:(2 * j + 1) * LANE]
            d1 = dq_ref[:, (2 * j + 1) * LANE:(2 * j + 2) * LANE]
            if j // 2 == 0:
                dy = jnp.where(low, d0, pltpu.roll(d1, ATT_HEAD_DIM, 1))
            else:
                dy = jnp.where(low, pltpu.roll(d0, ATT_HEAD_DIM, 1), d1)
            dx, dg = bwd(p_ref[:, j * LANE:(j + 1) * LANE], gq_ref[...], dy * (ATT_HEAD_DIM ** -0.5))
            dp_ref[:, j * LANE:(j + 1) * LANE] = _bf(dx)
            dgq = dgq + dg
        dgq_ref[...] += dgq
        dx, dg = bwd(p_ref[:, ATT_Q_DIM:ATT_Q_DIM + LANE], gk_ref[...], dk_ref[...])
        dp_ref[:, ATT_Q_DIM:ATT_Q_DIM + LANE] = _bf(dx)
        dgk_ref[...] += dg

    row = lambda width: pl.BlockSpec((tm, width), lambda i: (i, 0))
    return pl.pallas_call(
        body, name="qk_prep_bwd", grid=(t // tm,),
        in_specs=[row(ATT_Q_DIM + LANE), row(ATT_HEADS * LANE), row(LANE), row(LANE), _full((1, LANE)), _full((1, LANE)),
                  row(LANE), row(LANE), row(LANE), _full((LANE, LANE))],
        out_specs=[row(att_w), _full((1, LANE)), _full((1, LANE))],
        out_shape=[jax.ShapeDtypeStruct((t, att_w), BF16), jax.ShapeDtypeStruct((1, LANE), F32),
                   jax.ShapeDtypeStruct((1, LANE), F32)],
        compiler_params=_params(),
    )(p, dq, dk, dv, gq, gk, cc, sa, sb, bd)


def _attn_fwd(q, k, p, tq=256):
    t = k.shape[0]
    tq = min(tq, t)
    v_blk = (ATT_Q_DIM + ATT_KV_DIM) // LANE

    def body(q_ref, k_ref, v_ref, o_ref, o32_ref, lse_ref):
        k_ = k_ref[...]
        v = v_ref[...]
        lane_k = lax.broadcasted_iota(jnp.int32, (t, LANE), 1)
        lane_q = lax.broadcasted_iota(jnp.int32, (tq, LANE), 1)
        lowk, lowq = lane_k < ATT_HEAD_DIM, lane_q < ATT_HEAD_DIM
        ones_lane = (ATT_HEAD_DIM, 0)
        vm = (_bf(jnp.where(lowk, v, jnp.where(lane_k == ones_lane[0], 1.0, 0.0))),
              _bf(jnp.where(lowk, jnp.where(lane_k == ones_lane[1], 1.0, 0.0), v)))
        for j in range(4):
            kvh = j // 2
            acc = None
            for sub in range(2):
                h = 2 * j + sub
                s = _dot(q_ref[:, h * LANE:(h + 1) * LANE], k_, 1, 1)
                mx = jnp.max(s, axis=-1, keepdims=True)
                ov = _dot(jnp.exp(_bf(s - mx)), vm[kvh], 1, 0)
                l = jnp.sum(jnp.where(lane_q == ones_lane[kvh], ov, 0.0), axis=-1, keepdims=True)
                lse_ref[h] = mx + jnp.log(l)
                o = jnp.where(lowq if kvh == 0 else ~lowq, ov, 0.0) * (1.0 / l)
                if sub != kvh:
                    o = pltpu.roll(o, ATT_HEAD_DIM, 1)
                acc = o if acc is None else acc + o
            o32_ref[:, j * LANE:(j + 1) * LANE] = acc
            o_ref[:, j * LANE:(j + 1) * LANE] = _bf(acc)

    row = pl.BlockSpec((tq, ATT_Q_DIM), lambda i: (i, 0))
    return _call(
        body, (q, k, p), name="attn_fwd", grid=(t // tq,),
        in_specs=[pl.BlockSpec((tq, ATT_HEADS * LANE), lambda i: (i, 0)), _full((t, LANE)),
                  pl.BlockSpec((t, LANE), lambda i: (0, v_blk))],
        out_specs=[row, row, pl.BlockSpec((ATT_HEADS, tq, 1), lambda i: (0, i, 0))],
        out_shape=[jax.ShapeDtypeStruct((t, ATT_Q_DIM), BF16), jax.ShapeDtypeStruct((t, ATT_Q_DIM), F32),
                   jax.ShapeDtypeStruct((ATT_HEADS, t, 1), F32)])


def _attn_bwd(q, k, p, dcat, o32, lse, tq=256):
    t = k.shape[0]
    tq = min(tq, t)
    v_blk = (ATT_Q_DIM + ATT_KV_DIM) // LANE

    def body(q_ref, k_ref, v_ref, do_ref, o_ref, lse_ref, dq_ref, dk_ref, dv_ref):
        @pl.when(pl.program_id(0) == 0)
        def _():
            dk_ref[...] = jnp.zeros_like(dk_ref)
            dv_ref[...] = jnp.zeros_like(dv_ref)

        k_ = k_ref[...]
        vb = _bf(v_ref[...])
        lowq = lax.broadcasted_iota(jnp.int32, (tq, LANE), 1) < ATT_HEAD_DIM
        dk_acc = jnp.zeros((t, LANE), F32)
        dv_acc = jnp.zeros((t, LANE), F32)
        for j in range(4):
            kvh = j // 2
            dop = do_ref[:, j * LANE:(j + 1) * LANE]
            prod = dop * o_ref[:, j * LANE:(j + 1) * LANE]
            d_low = jnp.sum(jnp.where(lowq, prod, 0.0), axis=-1, keepdims=True)
            d_sub = (d_low, jnp.sum(prod, axis=-1, keepdims=True) - d_low)
            for sub in range(2):
                h = 2 * j + sub
                src = dop if sub == kvh else pltpu.roll(dop, ATT_HEAD_DIM, 1)
                do_h = _bf(jnp.where(lowq, src, 0.0) if kvh == 0 else jnp.where(lowq, 0.0, src))
                qh = q_ref[:, h * LANE:(h + 1) * LANE]
                pr = jnp.exp(_bf(_dot(qh, k_, 1, 1) - lse_ref[h]))
                ds = pr * _bf(_dot(do_h, vb, 1, 1) - d_sub[sub])
                dq_ref[:, h * LANE:(h + 1) * LANE] = _dot(ds, k_, 1, 0)
                dk_acc = dk_acc + _dot(ds, qh, 0, 0)
                dv_acc = dv_acc + _dot(pr, do_h, 0, 0)
        dk_ref[...] += dk_acc
        dv_ref[...] += dv_acc

    row = pl.BlockSpec((tq, ATT_Q_DIM), lambda i: (i, 0))
    return _call(
        body, (q, k, p, dcat, o32, lse), name="attn_bwd", grid=(t // tq,),
        in_specs=[pl.BlockSpec((tq, ATT_HEADS * LANE), lambda i: (i, 0)), _full((t, LANE)),
                  pl.BlockSpec((t, LANE), lambda i: (0, v_blk)), row, row,
                  pl.BlockSpec((ATT_HEADS, tq, 1), lambda i: (0, i, 0))],
        out_specs=[pl.BlockSpec((tq, ATT_HEADS * LANE), lambda i: (i, 0)), _full((t, LANE)), _full((t, LANE))],
        out_shape=[jax.ShapeDtypeStruct((t, ATT_HEADS * LANE), F32), jax.ShapeDtypeStruct((t, LANE), F32),
                   jax.ShapeDtypeStruct((t, LANE), F32)])


def _lower_bounds(a0, a1):
    def body(a0_ref, a1_ref, lb_ref):
        m = jnp.maximum(a0_ref[...], a1_ref[...])
        e0, e1 = jnp.exp(a0_ref[...] - m), jnp.exp(a1_ref[...] - m)
        lb_ref[...] = e0 / (e0 + e1)

    return pl.pallas_call(body, name="lower_bounds", out_shape=jax.ShapeDtypeStruct(a0.shape, F32))(a0, a1)


def _lower_bounds_bwd(a0, a1, dlb):
    def body(a0_ref, a1_ref, dlb_ref, d0_ref, d1_ref):
        m = jnp.maximum(a0_ref[...], a1_ref[...])
        e0, e1 = jnp.exp(a0_ref[...] - m), jnp.exp(a1_ref[...] - m)
        lb = e0 / (e0 + e1)
        d0 = dlb_ref[...] * lb * (1.0 - lb)
        d0_ref[...] = d0
        d1_ref[...] = -d0

    return pl.pallas_call(body, name="lower_bounds_bwd", out_shape=[jax.ShapeDtypeStruct(a0.shape, F32)] * 2)(a0, a1, dlb)


def _chunk_scan(x, pos, down):
    n = x.shape[0]
    s = 1
    while s < HG_CHUNK:
        if down:
            x = x + jnp.where(pos >= s, pltpu.roll(x, s, 0), 0.0)
        else:
            x = x + jnp.where(pos < HG_CHUNK - s, pltpu.roll(x, n - s, 0), 0.0)
        s *= 2
    return x


def _chunk_concat(x, nc):
    xb = _bf(x)
    rows = []
    for c in range(nc):
        pieces = [xb[c * HG_CHUNK:(c + 1) * HG_CHUNK, :]]
        if c:
            pieces.insert(0, jnp.zeros((HG_CHUNK, c * LANE), BF16))
        if c < nc - 1:
            pieces.append(jnp.zeros((HG_CHUNK, (nc - 1 - c) * LANE), BF16))
        rows.append(jnp.concatenate(pieces, axis=1) if len(pieces) > 1 else pieces[0])
    return jnp.concatenate(rows, axis=0) if nc > 1 else rows[0]


def _chunk_pick(x, nc):
    rows = [x[c * HG_CHUNK:(c + 1) * HG_CHUNK, c * LANE:(c + 1) * LANE] for c in range(nc)]
    return jnp.concatenate(rows, axis=0) if nc > 1 else rows[0]


def _chunk_rows(vals):
    rows = [jnp.broadcast_to(v, (HG_CHUNK, LANE)) for v in vals]
    return jnp.concatenate(rows, axis=0) if len(rows) > 1 else rows[0]


def _hgrn_gates(hq, z, lb):
    sq = _sigmoid(hq)
    sig = _sigmoid(z)
    f = lb + (1.0 - lb) * sig
    return hq * sq, sq, sig, f, jnp.log(f)


def _hgrn_local(q, f, g, v, pos, cid, amask, rev, nc):
    k = 1.0 - f
    ba = _chunk_scan(g, pos, not rev)
    bb = _chunk_scan(g, pos, rev)
    eq = 0.5 * (ba - bb + g)
    ex_q, ex_k, ex_i, ex_d = jnp.exp(eq), jnp.exp(-eq), jnp.exp(ba), jnp.exp(bb - g)
    qb, kb, qi, kd = q * ex_q, k * ex_k, q * ex_i, k * ex_d
    dvec = jnp.exp(ba + bb - g)
    a = jnp.where(amask, _dot(_bf(qb), _bf(kb), 1, 1), 0.0)
    kd_m = _chunk_concat(kd, nc)
    qi_m = _chunk_concat(qi, nc)
    ut_all = _dot(_bf(v), kd_m, 0, 0)
    return dict(k=k, ex_q=ex_q, ex_k=ex_k, ex_i=ex_i, ex_d=ex_d, qb=qb, kb=kb, qi=qi, kd=kd, dvec=dvec, a=a,
                kd_m=kd_m, qi_m=qi_m, ut_all=ut_all)


def _hgrn_masks(n, rev):
    row = lax.broadcasted_iota(jnp.int32, (n, LANE), 0)
    ti = lax.broadcasted_iota(jnp.int32, (n, n), 0)
    si = lax.broadcasted_iota(jnp.int32, (n, n), 1)
    tri = (si >= ti) if rev else (si <= ti)
    same = jnp.right_shift(ti, HG_CHUNK_LOG2) == jnp.right_shift(si, HG_CHUNK_LOG2)
    return jnp.bitwise_and(row, HG_CHUNK - 1), jnp.right_shift(row, HG_CHUNK_LOG2), same & tri


def _hgrn_specs(t, rev, bwd):
    n = min(HG_BLOCK_BWD if bwd else HG_BLOCK_FWD, t)
    nb = t // n
    if rev != bwd:
        blk = lambda i: nb - 1 - i
    else:
        blk = lambda i: i
    col = lambda base: pl.BlockSpec((n, 2 * LANE), lambda hp, i: (blk(i), base + hp))
    return n, nb, blk, col


def _hgrn_fwd(p, lb, rev, after=()):
    t = p.shape[0]
    n, nb, blk, col = _hgrn_specs(t, rev, False)
    nc = n // HG_CHUNK
    sub = n // min(HG_BLOCK_BWD, t)
    z_base = 7 if rev else 5

    def body(hq_ref, z_ref, hi_ref, lb_ref, o_ref, ssave_ref, st_scr):
        @pl.when(pl.program_id(1) == 0)
        def _():
            st_scr[...] = jnp.zeros_like(st_scr)

        pos, cid, amask = _hgrn_masks(n, rev)
        order = list(range(nc))[::-1] if rev else list(range(nc))
        for hh in range(2):
            sl = slice(hh * LANE, (hh + 1) * LANE)
            q, _, _, f, g = _hgrn_gates(hq_ref[:, sl], z_ref[:, sl], lb_ref[0:1, sl])
            v = hi_ref[:, sl]
            c_ = _hgrn_local(q, f, g, v, pos, cid, amask, rev, nc)
            st = st_scr[hh]
            cols = [None] * nc
            for c in order:
                cols[c] = st
                st = st * c_["dvec"][c * HG_CHUNK:c * HG_CHUNK + 1, :] + c_["ut_all"][:, c * LANE:(c + 1) * LANE]
            st_scr[hh] = st
            for s in range(sub):
                ssave_ref[sub - 1 - s if rev else s, sl, :] = cols[order[s * (nc // sub)]]
            st_all = _bf(jnp.concatenate(cols, axis=1))
            o_ref[:, sl] = _dot(_bf(c_["a"]), _bf(v), 1, 0) + _dot(c_["qi_m"], st_all, 1, 1)

    return _call(
        body, (p, p, p, lb), name="hgrn_fwd_rev" if rev else "hgrn_fwd", grid=(2, nb),
        in_specs=[col(3), col(z_base), col(9), pl.BlockSpec((1, 2 * LANE), lambda hp, i: (0, hp))],
        out_specs=[pl.BlockSpec((n, 2 * LANE), lambda hp, i: (blk(i), hp)),
                   pl.BlockSpec((sub, 2 * LANE, LANE), lambda hp, i: (blk(i), hp, 0))],
        out_shape=[jax.ShapeDtypeStruct((t, HG_DIM), F32), jax.ShapeDtypeStruct((nb * sub, HG_DIM, LANE), F32)],
        scratch_shapes=[pltpu.VMEM((2, LANE, LANE), F32)], after=after)


def _hgrn_bwd(p, lb, do, ssave, rev, other=None, after=()):
    t = p.shape[0]
    n, nb, blk, col = _hgrn_specs(t, rev, True)
    nc = n // HG_CHUNK
    z_base = 7 if rev else 5
    n_other = 0 if other is None else 2

    def body(*refs):
        hq_ref, z_ref, hi_ref, lb_ref, do_ref, ssave_ref = refs[:6]
        dhq_ref, dz_ref, dhi_ref, dlb_ref, dst_scr = refs[6 + n_other:]

        @pl.when(pl.program_id(1) == 0)
        def _():
            dst_scr[...] = jnp.zeros_like(dst_scr)
            dlb_ref[...] = jnp.zeros_like(dlb_ref)

        pos, cid, amask = _hgrn_masks(n, rev)
        order = list(range(nc))[::-1] if rev else list(range(nc))
        for hh in range(2):
            sl = slice(hh * LANE, (hh + 1) * LANE)
            hq, lbv = hq_ref[:, sl], lb_ref[0:1, sl]
            q, sq, sig, f, g = _hgrn_gates(hq, z_ref[:, sl], lbv)
            v = hi_ref[:, sl]
            c_ = _hgrn_local(q, f, g, v, pos, cid, amask, rev, nc)
            dvec, ut_all = c_["dvec"], c_["ut_all"]
            drow = lambda c: dvec[c * HG_CHUNK:c * HG_CHUNK + 1, :]
            st = ssave_ref[0, sl, :]
            cols = [None] * nc
            for c in order:
                cols[c] = st
                st = st * drow(c) + ut_all[:, c * LANE:(c + 1) * LANE]
            dob, vb = _bf(do_ref[:, sl]), _bf(v)
            gt_all = _dot(dob, c_["qi_m"], 0, 0)
            dcur = dst_scr[hh]
            dnext = [None] * nc
            ddrow = [None] * nc
            for c in order[::-1]:
                dnext[c] = dcur
                ddrow[c] = jnp.sum(cols[c] * dcur, axis=0, keepdims=True) * drow(c)
                dcur = dcur * drow(c) + gt_all[:, c * LANE:(c + 1) * LANE]
            dst_scr[hh] = dcur
            dsn_all = _bf(jnp.concatenate(dnext, axis=1))
            st_all = _bf(jnp.concatenate(cols, axis=1))
            da = _bf(jnp.where(amask, _dot(dob, vb, 1, 1), 0.0))
            dv = _dot(_bf(c_["a"]), dob, 0, 0) + _dot(c_["kd_m"], dsn_all, 1, 1)
            dqb = _dot(da, _bf(c_["kb"]), 1, 0)
            dkb = _dot(da, _bf(c_["qb"]), 0, 0)
            dqi = _chunk_pick(_dot(dob, st_all, 1, 0), nc)
            dkd = _chunk_pick(_dot(vb, dsn_all, 1, 0), nc)
            dq = dqb * c_["ex_q"] + dqi * c_["ex_i"]
            dk = dkb * c_["ex_k"] + dkd * c_["ex_d"]
            e = dqb * c_["qb"] - dkb * c_["kb"] + dqi * c_["qi"]
            w = dkd * c_["kd"]
            dg = _chunk_scan(e, pos, rev) + (_chunk_scan(w, pos, not rev) - w) + _chunk_rows(ddrow)
            df = dg / f - dk
            dz_ref[:, sl] = _bf(df * (1.0 - lbv) * sig * (1.0 - sig))
            dlb_ref[0:1, sl] += jnp.sum(df * (1.0 - sig), axis=0, keepdims=True)
            dhq = dq * (sq * (1.0 + hq * (1.0 - sq)))
            if other is None:
                dhq_ref[:, sl], dhi_ref[:, sl] = dhq, dv
            else:
                dhq_ref[:, sl], dhi_ref[:, sl] = _bf(dhq + refs[6][:, sl]), _bf(dv + refs[7][:, sl])

    out = pl.BlockSpec((n, 2 * LANE), lambda hp, i: (blk(i), hp))
    sum_dtype = F32 if other is None else BF16
    return _call(
        body, (p, p, p, lb, do, ssave, *(other or ())), name="hgrn_bwd_rev" if rev else "hgrn_bwd", grid=(2, nb),
        in_specs=[col(3), col(z_base), col(9), pl.BlockSpec((1, 2 * LANE), lambda hp, i: (0, hp)), out,
                  pl.BlockSpec((1, 2 * LANE, LANE), lambda hp, i: (blk(i), hp, 0))] + [out] * n_other,
        out_specs=[out, out, out, pl.BlockSpec((1, 2 * LANE), lambda hp, i: (0, hp))],
        out_shape=[jax.ShapeDtypeStruct((t, HG_DIM), sum_dtype), jax.ShapeDtypeStruct((t, HG_DIM), BF16),
                   jax.ShapeDtypeStruct((t, HG_DIM), sum_dtype), jax.ShapeDtypeStruct((1, HG_DIM), F32)],
        scratch_shapes=[pltpu.VMEM((2, LANE, LANE), F32)], after=after)


def _mix_out(att, of, ob, p, gout, w_out, g_post, x, after=(), tm=256):
    t, d = x.shape
    tm = min(tm, t)

    def body(att_ref, of_ref, ob_ref, hg0_ref, hg1_ref, gout_ref, w_ref, g_ref, x_ref, cat_ref, y_ref, x1_ref):
        cat_ref[:, :ATT_Q_DIM] = att_ref[...]
        for h in range(HG_HEADS):
            sl = slice(h * LANE, (h + 1) * LANE)
            hg_ref = hg0_ref if h < 2 else hg1_ref
            hg = hg_ref[:, (h % 2) * LANE:(h % 2 + 1) * LANE]
            nrm = _rms_fwd(of_ref[:, sl] + ob_ref[:, sl], gout_ref[...])
            cat_ref[:, ATT_Q_DIM + h * LANE:ATT_Q_DIM + (h + 1) * LANE] = _bf(nrm * hg * _sigmoid(hg))
        y = _dot(cat_ref[...], w_ref[...], 1, 0)
        y_ref[...] = y
        x1_ref[...] = x_ref[...] + _rms_fwd(y, g_ref[...])

    row = lambda width: pl.BlockSpec((tm, width), lambda i: (i, 0))
    return _call(
        body, (att, of, ob, p, p, gout, w_out, g_post, x), name="mix_out", grid=(t // tm,),
        in_specs=[row(ATT_Q_DIM), row(HG_DIM), row(HG_DIM), pl.BlockSpec((tm, 2 * LANE), lambda i: (i, 11)),
                  pl.BlockSpec((tm, 2 * LANE), lambda i: (i, 12)), _full((1, LANE)), _full((d, d)), _full((1, d)), row(d)],
        out_specs=[row(d), row(d), row(d)],
        out_shape=[jax.ShapeDtypeStruct((t, d), BF16), jax.ShapeDtypeStruct((t, d), F32), jax.ShapeDtypeStruct((t, d), F32)],
        after=after)


def _rec_bwd(dcat, of, ob, p, gout, tm=256):
    t = of.shape[0]
    tm = min(tm, t)

    def body(dc_ref, of_ref, ob_ref, hg0_ref, hg1_ref, gout_ref, do_ref, dhg_ref, dgo_ref):
        @pl.when(pl.program_id(0) == 0)
        def _():
            dgo_ref[...] = jnp.zeros_like(dgo_ref)

        dgo = jnp.zeros((1, LANE), F32)
        for h in range(HG_HEADS):
            sl = slice(h * LANE, (h + 1) * LANE)
            hg_ref = hg0_ref if h < 2 else hg1_ref
            hg = hg_ref[:, (h % 2) * LANE:(h % 2 + 1) * LANE]
            o = of_ref[:, sl] + ob_ref[:, sl]
            sg = _sigmoid(hg)
            drec = dc_ref[:, sl]
            dhg_ref[:, sl] = _bf(drec * _rms_fwd(o, gout_ref[...]) * (sg * (1.0 + hg * (1.0 - sg))))
            do, dg = _rms_bwd(drec * hg * sg, o, gout_ref[...])
            do_ref[:, sl] = do
            dgo = dgo + dg
        dgo_ref[...] += dgo

    row = lambda width: pl.BlockSpec((tm, width), lambda i: (i, 0))
    return pl.pallas_call(
        body, name="rec_bwd", grid=(t // tm,),
        in_specs=[pl.BlockSpec((tm, HG_DIM), lambda i: (i, 1)), row(HG_DIM), row(HG_DIM),
                  pl.BlockSpec((tm, 2 * LANE), lambda i: (i, 11)), pl.BlockSpec((tm, 2 * LANE), lambda i: (i, 12)),
                  _full((1, LANE))],
        out_specs=[row(HG_DIM), row(HG_DIM), _full((1, LANE))],
        out_shape=[jax.ShapeDtypeStruct((t, HG_DIM), F32), jax.ShapeDtypeStruct((t, HG_DIM), BF16),
                   jax.ShapeDtypeStruct((1, LANE), F32)],
        compiler_params=_params(),
    )(dcat, of, ob, p, p, gout)


def _xattn_fwd(q, kv, tq=512):
    t, d = q.shape
    tq = min(tq, t)
    nm = kv.shape[0]

    def body(q_ref, kv_ref, o_ref):
        for h in range(X_HEADS):
            sl = slice(h * X_HEAD_DIM, (h + 1) * X_HEAD_DIM)
            s = _dot(_bf(q_ref[:, sl]), _bf(kv_ref[:, sl]), 1, 1) * (X_HEAD_DIM ** -0.5)
            e = jnp.exp(s - jnp.max(s, axis=-1, keepdims=True))
            pr = e * (1.0 / jnp.sum(e, axis=-1, keepdims=True))
            o_ref[:, sl] = _bf(_dot(_bf(pr), _bf(kv_ref[:, d + h * X_HEAD_DIM:d + (h + 1) * X_HEAD_DIM]), 1, 0))

    return pl.pallas_call(
        body, name="xattn_fwd", grid=(t // tq,),
        in_specs=[pl.BlockSpec((tq, d), lambda i: (i, 0)), _full((nm, 2 * d))],
        out_specs=pl.BlockSpec((tq, d), lambda i: (i, 0)),
        out_shape=jax.ShapeDtypeStruct((t, d), BF16),
        compiler_params=_params(),
    )(q, kv)


def _xattn_bwd(q, kv, do, tq=512):
    t, d = q.shape
    tq = min(tq, t)
    nm = kv.shape[0]

    def body(q_ref, kv_ref, do_ref, dq_ref, dkv_ref):
        @pl.when(pl.program_id(0) == 0)
        def _():
            dkv_ref[...] = jnp.zeros_like(dkv_ref)

        for h in range(X_HEADS):
            sl = slice(h * X_HEAD_DIM, (h + 1) * X_HEAD_DIM)
            slv = slice(d + h * X_HEAD_DIM, d + (h + 1) * X_HEAD_DIM)
            qb, kb, vb, dob = _bf(q_ref[:, sl]), _bf(kv_ref[:, sl]), _bf(kv_ref[:, slv]), _bf(do_ref[:, sl])
            s = _dot(qb, kb, 1, 1) * (X_HEAD_DIM ** -0.5)
            e = jnp.exp(s - jnp.max(s, axis=-1, keepdims=True))
            pr = e * (1.0 / jnp.sum(e, axis=-1, keepdims=True))
            dpr = _dot(dob, vb, 1, 1)
            ds = _bf(pr * (dpr - jnp.sum(pr * dpr, axis=-1, keepdims=True)) * (X_HEAD_DIM ** -0.5))
            dq_ref[:, sl] = _bf(_dot(ds, kb, 1, 0))
            dkv_ref[:, sl] += _dot(ds, qb, 0, 0)
            dkv_ref[:, slv] += _dot(_bf(pr), dob, 0, 0)

    return pl.pallas_call(
        body, name="xattn_bwd", grid=(t // tq,),
        in_specs=[pl.BlockSpec((tq, d), lambda i: (i, 0)), _full((nm, 2 * d)), pl.BlockSpec((tq, d), lambda i: (i, 0))],
        out_specs=[pl.BlockSpec((tq, d), lambda i: (i, 0)), _full((nm, 2 * d))],
        out_shape=[jax.ShapeDtypeStruct((t, d), BF16), jax.ShapeDtypeStruct((nm, 2 * d), F32)],
        compiler_params=_params(),
    )(q, kv, do)


CONV_TN = 256


def _shift_rows(u, row, t, delta):
    if delta < 0:
        return jnp.where(row == 0, 0.0, pltpu.roll(u, 1, 0))
    return jnp.where(row == t - 1, 0.0, pltpu.roll(u, t - 1, 0))


def _conv_gate_fwd(u, conv_w, conv_b):
    t = u.shape[0]
    nt = D_FF // CONV_TN

    def body(ug_ref, uv_ref, wg_ref, wv_ref, bg_ref, bv_ref, a_ref):
        row = lax.broadcasted_iota(jnp.int32, (t, CONV_TN), 0)

        def conv(u_ref, w_ref, b_ref):
            uu = u_ref[...]
            return (b_ref[...] + _shift_rows(uu, row, t, -1) * w_ref[0:1, :] + uu * w_ref[1:2, :]
                    + _shift_rows(uu, row, t, 1) * w_ref[2:3, :])

        gate = conv(ug_ref, wg_ref, bg_ref)
        a_ref[...] = _bf(gate * _sigmoid(gate) * conv(uv_ref, wv_ref, bv_ref))

    col = lambda rows, off: pl.BlockSpec((rows, CONV_TN), lambda j: (0, j + off))
    return pl.pallas_call(
        body, name="conv_gate_fwd", grid=(nt,),
        in_specs=[col(t, 0), col(t, nt), col(3, 0), col(3, nt), col(1, 0), col(1, nt)],
        out_specs=col(t, 0),
        out_shape=jax.ShapeDtypeStruct((t, D_FF), BF16),
        compiler_params=_params(),
    )(u, u, conv_w, conv_w, conv_b, conv_b)


def _conv_gate_bwd(u, conv_w, conv_b, da, after=()):
    t = u.shape[0]
    nt = D_FF // CONV_TN

    def body(ug_ref, uv_ref, wg_ref, wv_ref, bg_ref, bv_ref, da_ref, dug_ref, dwg_ref, dbg_ref, duv_ref, dwv_ref, dbv_ref):
        row = lax.broadcasted_iota(jnp.int32, (t, CONV_TN), 0)
        ug, uv = ug_ref[...], uv_ref[...]
        ug_m, ug_p = _shift_rows(ug, row, t, -1), _shift_rows(ug, row, t, 1)
        uv_m, uv_p = _shift_rows(uv, row, t, -1), _shift_rows(uv, row, t, 1)
        gate = bg_ref[...] + ug_m * wg_ref[0:1, :] + ug * wg_ref[1:2, :] + ug_p * wg_ref[2:3, :]
        val = bv_ref[...] + uv_m * wv_ref[0:1, :] + uv * wv_ref[1:2, :] + uv_p * wv_ref[2:3, :]
        sg = _sigmoid(gate)
        da_ = da_ref[...].astype(F32)

        def emit(dacc, um, uu, up, w_ref, du_ref, dw_ref, db_ref):
            du_ref[...] = _bf(_shift_rows(dacc, row, t, 1) * w_ref[0:1, :] + dacc * w_ref[1:2, :]
                              + _shift_rows(dacc, row, t, -1) * w_ref[2:3, :])
            dw_ref[0:1, :] = jnp.sum(dacc * um, axis=0, keepdims=True)
            dw_ref[1:2, :] = jnp.sum(dacc * uu, axis=0, keepdims=True)
            dw_ref[2:3, :] = jnp.sum(dacc * up, axis=0, keepdims=True)
            db_ref[...] = jnp.sum(dacc, axis=0, keepdims=True)

        emit(da_ * val * (sg * (1.0 + gate * (1.0 - sg))), ug_m, ug, ug_p, wg_ref, dug_ref, dwg_ref, dbg_ref)
        emit(da_ * gate * sg, uv_m, uv, uv_p, wv_ref, duv_ref, dwv_ref, dbv_ref)

    col = lambda rows, off: pl.BlockSpec((rows, CONV_TN), lambda j: (0, j + off))
    half_shapes = [jax.ShapeDtypeStruct((t, D_FF), BF16), jax.ShapeDtypeStruct((3, D_FF), F32),
                   jax.ShapeDtypeStruct((1, D_FF), F32)]
    outs = _call(
        body, (u, u, conv_w, conv_w, conv_b, conv_b, da), name="conv_gate_bwd", grid=(nt,),
        in_specs=[col(t, 0), col(t, nt), col(3, 0), col(3, nt), col(1, 0), col(1, nt), col(t, 0)],
        out_specs=[col(t, 0), col(3, 0), col(1, 0)] * 2, out_shape=half_shapes * 2, after=after)
    return outs[:3], outs[3:]


def _row_tile(r, cap):
    best = r
    for cand in range(16, cap + 1, 16):
        if r % cand == 0:
            best = cand
    return best


def _sum_parts(own, got, me, name, tr=256):
    _, r, c = got.shape
    tr = _row_tile(r, tr)

    def body(me_ref, own_ref, got_ref, o_ref):
        mine = own_ref[...].astype(F32)
        acc = None
        for i in range(N_DEV):
            term = jnp.where(me_ref[0] == i, mine, got_ref[i].astype(F32))
            acc = term if acc is None else acc + term
        o_ref[...] = acc

    return pl.pallas_call(
        body, name=name,
        grid_spec=pltpu.PrefetchScalarGridSpec(
            num_scalar_prefetch=1, grid=(r // tr,),
            in_specs=[pl.BlockSpec((None, tr, c), lambda i, me_ref: (me_ref[0], i, 0)),
                      pl.BlockSpec((N_DEV, tr, c), lambda i, me_ref: (0, i, 0))],
            out_specs=pl.BlockSpec((tr, c), lambda i, me_ref: (i, 0))),
        out_shape=jax.ShapeDtypeStruct((r, c), F32),
        compiler_params=_params(),
    )(me, own, got)


def _adamw_math(w_, g_, m_, v_):
    m_ = ADAM_B1 * m_ + (1.0 - ADAM_B1) * g_
    v_ = ADAM_B2 * v_ + (1.0 - ADAM_B2) * (g_ * g_)
    m_hat = m_ / (1.0 - ADAM_B1 ** ADAM_STEP)
    v_hat = v_ / (1.0 - ADAM_B2 ** ADAM_STEP)
    return -ADAM_LR * (m_hat / (jnp.sqrt(v_hat) + ADAM_EPS) + ADAM_WD * w_), m_, v_


def _sum_adamw(own, got, me, w, m, v, name, tr=256):
    _, r, c = got.shape
    tr = _row_tile(r, tr)

    def body(me_ref, own_ref, got_ref, w_ref, m_ref, v_ref, g_ref, d_ref, nm_ref, nv_ref):
        mine = own_ref[...].astype(F32)
        acc = None
        for i in range(N_DEV):
            term = jnp.where(me_ref[0] == i, mine, got_ref[i].astype(F32))
            acc = term if acc is None else acc + term
        g_ref[...] = acc
        d_ref[...], nm_ref[...], nv_ref[...] = _adamw_math(w_ref[...], acc, m_ref[...], v_ref[...])

    blk = pl.BlockSpec((tr, c), lambda i, me_ref: (i, 0))
    return pl.pallas_call(
        body, name=name,
        grid_spec=pltpu.PrefetchScalarGridSpec(
            num_scalar_prefetch=1, grid=(r // tr,),
            in_specs=[pl.BlockSpec((None, tr, c), lambda i, me_ref: (me_ref[0], i, 0)),
                      pl.BlockSpec((N_DEV, tr, c), lambda i, me_ref: (0, i, 0)), blk, blk, blk],
            out_specs=[blk] * 4),
        out_shape=[jax.ShapeDtypeStruct((r, c), F32)] * 4,
        compiler_params=_params(),
    )(me, own, got, w, m, v)


def _adamw(w, g, m, v, name, tr=256):
    r, c = w.shape
    tr = _row_tile(r, tr)

    def body(w_ref, g_ref, m_ref, v_ref, d_ref, nm_ref, nv_ref):
        d_ref[...], nm_ref[...], nv_ref[...] = _adamw_math(w_ref[...], g_ref[...], m_ref[...], v_ref[...])

    blk = pl.BlockSpec((tr, c), lambda i: (i, 0))
    return pl.pallas_call(
        body, name=name, grid=(r // tr,),
        in_specs=[blk] * 4, out_specs=[blk] * 3,
        out_shape=[jax.ShapeDtypeStruct((r, c), F32)] * 3,
        compiler_params=_params(),
    )(w, g, m, v)


def _mesh_pos():
    return lax.axis_index("x"), lax.axis_index("y"), lax.axis_index("c")


def _dev_index(px, py, pc):
    return 4 * px + 2 * py + pc


class _Gather:
    def __init__(self, arrs):
        self.arrs = list(arrs)
        n = len(self.arrs)
        self.out_shape = [jax.ShapeDtypeStruct((N_DEV,) + a.shape, a.dtype) for a in self.arrs]
        self.sems = [pltpu.SemaphoreType.DMA((7, n)), pltpu.SemaphoreType.DMA((7, n)), pltpu.SemaphoreType.DMA((n,))]

    def _ctx(self, ins, outs, sems):
        send_sems, recv_sems, local_sems = sems
        x, y, c = _mesh_pos()
        chips = [(1 - x, y), (x, 1 - y), (1 - x, 1 - y)]

        def copy(k, a, block, to, src=None):
            dst = outs[a].at[_dev_index(*block)]
            return pltpu.make_async_remote_copy(
                src_ref=dst if src is None else src, dst_ref=dst, send_sem=send_sems.at[k, a], recv_sem=recv_sems.at[k, a],
                device_id=to, device_id_type=MESH_T)

        n = len(ins)
        me, sibling = (x, y, c), (x, y, 1 - c)
        mine = [pltpu.make_async_copy(ins[a], outs[a].at[_dev_index(*me)], local_sems.at[a]) for a in range(n)]
        first = [copy(0, a, me, sibling, src=ins[a]) for a in range(n)]
        first += [copy(1 + j, a, me, (*chip, c), src=ins[a]) for j, chip in enumerate(chips) for a in range(n)]
        passed = [copy(4 + j, a, (*chip, c), sibling) for j, chip in enumerate(chips) for a in range(n)]
        return n, c, me, sibling, chips, copy, mine, first, passed

    def start(self, ins, outs, sems):
        _, _, _, _, _, _, mine, first, _ = self._ctx(ins, outs, sems)
        for cp in mine + first:
            cp.start()

    def forward(self, ins, outs, sems):
        n, c, me, _, chips, copy, _, _, passed = self._ctx(ins, outs, sems)
        for j, chip in enumerate(chips):
            for a in range(n):
                copy(1 + j, a, (*chip, c), me).wait_recv()
                passed[j * n + a].start()

    def finish(self, ins, outs, sems):
        n, c, me, sibling, chips, copy, mine, first, passed = self._ctx(ins, outs, sems)
        for a in range(n):
            copy(0, a, sibling, me).wait_recv()
        for j, chip in enumerate(chips):
            for a in range(n):
                copy(4 + j, a, (*chip, 1 - c), me).wait_recv()
        for cp in first + passed:
            cp.wait_send()
        for cp in mine:
            cp.wait()


def _comm_alone(comm, name):
    n = len(comm.arrs)

    def body(*refs):
        ins, outs, sems = refs[:n], refs[n:2 * n], refs[2 * n:]
        comm.start(ins, outs, sems)
        if comm.forward is not None:
            comm.forward(ins, outs, sems)
        comm.finish(ins, outs, sems)

    any_spec = pl.BlockSpec(memory_space=pl.ANY)
    return pl.pallas_call(body, name=name, in_specs=[any_spec] * n, out_specs=[any_spec] * n, out_shape=comm.out_shape,
                          scratch_shapes=comm.sems)(*comm.arrs)


def _peers(x, y, c):
    out = []
    for k in range(1, N_DEV):
        pos = (1 - x if k & 4 else x, 1 - y if k & 2 else y, 1 - c if k & 1 else c)
        out.append((k, pos, _dev_index(*pos)))
    return out


def _exchange_begin(arrs, tag):
    n = len(arrs)
    lands = [lax.empty(a.shape, a.dtype) for a in arrs]

    def start_body(*refs):
        ins, land = refs[:n], refs[n:2 * n]
        send_sems, recv_sems, token = refs[2 * n], refs[2 * n + 1], refs[-1]
        x, y, c = _mesh_pos()
        me_i = _dev_index(x, y, c)
        for k, pos, peer_i in _peers(x, y, c):
            for a in range(n):
                pltpu.make_async_remote_copy(
                    src_ref=ins[a].at[peer_i], dst_ref=land[a].at[me_i], send_sem=send_sems.at[(k - 1) * n + a],
                    recv_sem=recv_sems.at[(k - 1) * n + a], device_id=pos, device_id_type=MESH_T).start()
        token[...] = jnp.zeros_like(token)

    hbm = pl.BlockSpec(memory_space=pltpu.HBM)
    sem = pl.BlockSpec(memory_space=pltpu.SEMAPHORE)
    thru = [pltpu.HBM(a.shape, a.dtype) for a in arrs]
    outs = pl.pallas_call(
        start_body, name="exchange_start_" + tag,
        out_shape=[pltpu.SemaphoreType.DMA((7 * n,)), pltpu.SemaphoreType.DMA((7 * n,))] + thru + thru
        + [jax.ShapeDtypeStruct((8, LANE), F32)],
        in_specs=[hbm] * (2 * n), out_specs=[sem, sem] + [hbm] * (2 * n) + [pl.BlockSpec(memory_space=pltpu.VMEM)],
        input_output_aliases={i: 2 + i for i in range(2 * n)},
        compiler_params=pltpu.CompilerParams(has_side_effects=pltpu.SideEffectType.DATAFLOW_SIDE_EFFECTING),
    )(*[pltpu.with_memory_space_constraint(a, pltpu.HBM) for a in list(arrs) + list(lands)])
    return (tag, n, outs[0], outs[1], outs[2:2 + n], outs[2 + n:2 + 2 * n]), outs[-1]


def _exchange_end(handle, after):
    tag, n, send_sems, recv_sems, srcs, lands = handle

    def body(*refs):
        ins, land = refs[:n], refs[n:2 * n]
        send_sems_, recv_sems_ = refs[2 * n], refs[2 * n + 1]
        x, y, c = _mesh_pos()
        me_i = _dev_index(x, y, c)
        for k, pos, peer_i in _peers(x, y, c):
            for a in range(n):
                cp = pltpu.make_async_remote_copy(
                    src_ref=ins[a].at[peer_i], dst_ref=land[a].at[me_i], send_sem=send_sems_.at[(k - 1) * n + a],
                    recv_sem=recv_sems_.at[(k - 1) * n + a], device_id=pos, device_id_type=MESH_T)
                cp.wait_send()
                cp.wait_recv()

    hbm = pl.BlockSpec(memory_space=pltpu.HBM)
    sem = pl.BlockSpec(memory_space=pltpu.SEMAPHORE)
    outs = pl.pallas_call(
        body, name="exchange_end_" + tag, out_shape=[pltpu.HBM(a.shape, a.dtype) for a in list(srcs) + list(lands)],
        in_specs=[hbm] * (2 * n) + [sem, sem, pl.BlockSpec(memory_space=pl.ANY)], out_specs=[hbm] * (2 * n),
        input_output_aliases={i: i for i in range(2 * n)},
        compiler_params=pltpu.CompilerParams(has_side_effects=pltpu.SideEffectType.DATAFLOW_SIDE_EFFECTING),
    )(*srcs, *lands, send_sems, recv_sems, after)
    return list(zip(outs[:n], outs[n:]))


def _gather_begin(arrs, tag):
    n = len(arrs)
    me = _dev_index(*_mesh_pos())
    lands = [lax.dynamic_update_slice(lax.empty((N_DEV,) + a.shape, a.dtype), a[None], (me,) + (0,) * a.ndim) for a in arrs]

    def start_body(*refs):
        ins, land = refs[:n], refs[n:2 * n]
        send_sems, recv_sems, token = refs[2 * n], refs[2 * n + 1], refs[-1]
        x, y, c = _mesh_pos()
        me_i = _dev_index(x, y, c)
        for a in range(n):
            for k, pos, _ in _peers(x, y, c):
                pltpu.make_async_remote_copy(
                    src_ref=ins[a], dst_ref=land[a].at[me_i], send_sem=send_sems.at[(k - 1) * n + a],
                    recv_sem=recv_sems.at[(k - 1) * n + a], device_id=pos, device_id_type=MESH_T).start()
        token[...] = jnp.zeros_like(token)

    hbm = pl.BlockSpec(memory_space=pltpu.HBM)
    sem = pl.BlockSpec(memory_space=pltpu.SEMAPHORE)
    thru = [pltpu.HBM(a.shape, a.dtype) for a in list(arrs) + lands]
    outs = pl.pallas_call(
        start_body, name="gather_start_" + tag,
        out_shape=[pltpu.SemaphoreType.DMA((7 * n,)), pltpu.SemaphoreType.DMA((7 * n,))] + thru
        + [jax.ShapeDtypeStruct((8, LANE), F32)],
        in_specs=[hbm] * (2 * n), out_specs=[sem, sem] + [hbm] * (2 * n) + [pl.BlockSpec(memory_space=pltpu.VMEM)],
        input_output_aliases={i: 2 + i for i in range(2 * n)},
        compiler_params=pltpu.CompilerParams(has_side_effects=pltpu.SideEffectType.DATAFLOW_SIDE_EFFECTING),
    )(*[pltpu.with_memory_space_constraint(a, pltpu.HBM) for a in list(arrs) + lands])
    return (tag, n, outs[0], outs[1], outs[2:2 + n], outs[2 + n:2 + 2 * n]), outs[-1]


def _gather_end(handle, which, after):
    tag, n, send_sems, recv_sems, srcs, lands = handle
    m = len(which)

    def body(*refs):
        ins, land = refs[:m], refs[m:2 * m]
        send_sems_, recv_sems_ = refs[2 * m], refs[2 * m + 1]
        x, y, c = _mesh_pos()
        me_i = _dev_index(x, y, c)
        for j, a in enumerate(which):
            for k, pos, _ in _peers(x, y, c):
                cp = pltpu.make_async_remote_copy(
                    src_ref=ins[j], dst_ref=land[j].at[me_i], send_sem=send_sems_.at[(k - 1) * n + a],
                    recv_sem=recv_sems_.at[(k - 1) * n + a], device_id=pos, device_id_type=MESH_T)
                cp.wait_send()
                cp.wait_recv()

    hbm = pl.BlockSpec(memory_space=pltpu.HBM)
    sem = pl.BlockSpec(memory_space=pltpu.SEMAPHORE)
    ops = [srcs[a] for a in which] + [lands[a] for a in which]
    outs = pl.pallas_call(
        body, name="gather_end_%s_%s" % (tag, "_".join(str(a) for a in which)),
        out_shape=[pltpu.HBM(a.shape, a.dtype) for a in ops],
        in_specs=[hbm] * (2 * m) + [sem, sem, pl.BlockSpec(memory_space=pl.ANY)], out_specs=[hbm] * (2 * m),
        input_output_aliases={i: i for i in range(2 * m)},
        compiler_params=pltpu.CompilerParams(has_side_effects=pltpu.SideEffectType.DATAFLOW_SIDE_EFFECTING),
    )(*ops, send_sems, recv_sems, after)
    return list(outs[m:])


_SPLIT = dict(has_side_effects=pltpu.SideEffectType.DATAFLOW_SIDE_EFFECTING)


def _chips(x, y):
    return [(1 - x, y), (x, 1 - y), (1 - x, 1 - y)]


def _gather2_begin(arrs, tag):
    n = len(arrs)
    me = _dev_index(*_mesh_pos())
    lands = [lax.dynamic_update_slice(lax.empty((N_DEV,) + a.shape, a.dtype), a[None], (me,) + (0,) * a.ndim) for a in arrs]

    def body(*refs):
        ins, land = refs[:n], refs[n:2 * n]
        send1, recv1, token = refs[2 * n], refs[2 * n + 1], refs[-1]
        x, y, c = _mesh_pos()
        me_i = _dev_index(x, y, c)
        targets = [(x, y, 1 - c)] + [(*chip, c) for chip in _chips(x, y)]
        for a in range(n):
            for k, pos in enumerate(targets):
                pltpu.make_async_remote_copy(
                    src_ref=ins[a], dst_ref=land[a].at[me_i], send_sem=send1.at[k * n + a], recv_sem=recv1.at[k * n + a],
                    device_id=pos, device_id_type=MESH_T).start()
        token[...] = jnp.zeros_like(token)

    hbm = pl.BlockSpec(memory_space=pltpu.HBM)
    sem = pl.BlockSpec(memory_space=pltpu.SEMAPHORE)
    thru = [pltpu.HBM(a.shape, a.dtype) for a in list(arrs) + lands]
    outs = pl.pallas_call(
        body, name="gather_start_" + tag,
        out_shape=[pltpu.SemaphoreType.DMA((4 * n,)), pltpu.SemaphoreType.DMA((4 * n,))] + thru
        + [jax.ShapeDtypeStruct((8, LANE), F32)],
        in_specs=[hbm] * (2 * n), out_specs=[sem, sem] + [hbm] * (2 * n) + [pl.BlockSpec(memory_space=pltpu.VMEM)],
        input_output_aliases={i: 2 + i for i in range(2 * n)}, compiler_params=pltpu.CompilerParams(**_SPLIT),
    )(*[pltpu.with_memory_space_constraint(a, pltpu.HBM) for a in list(arrs) + lands])
    return dict(tag=tag, n=n, send1=outs[0], recv1=outs[1], srcs=list(outs[2:2 + n]), lands=list(outs[2 + n:2 + 2 * n])), outs[-1]


def _gather2_forward(handle, which, after, part):
    n, m = handle["n"], len(which)

    def body(*refs):
        land, recv1 = refs[:m], refs[m]
        send2, recv2, token = refs[m + 2], refs[m + 3], refs[-1]
        token[...] = jnp.zeros_like(token)
        x, y, c = _mesh_pos()
        for j, a in enumerate(which):
            for k, chip in enumerate(_chips(x, y)):
                blk = land[j].at[_dev_index(*chip, c)]
                copy = pltpu.make_async_remote_copy(
                    src_ref=blk, dst_ref=blk, send_sem=send2.at[k * m + j], recv_sem=recv2.at[k * m + j],
                    device_id=(x, y, 1 - c), device_id_type=MESH_T)
                pltpu.make_async_remote_copy(
                    src_ref=blk, dst_ref=blk, send_sem=send2.at[k * m + j], recv_sem=recv1.at[(1 + k) * n + a],
                    device_id=(*chip, c), device_id_type=MESH_T).wait_recv()
                copy.start()

    hbm = pl.BlockSpec(memory_space=pltpu.HBM)
    sem = pl.BlockSpec(memory_space=pltpu.SEMAPHORE)
    lands = [handle["lands"][a] for a in which]
    outs = pl.pallas_call(
        body, name="gather_forward_%s_%s" % (handle["tag"], part),
        out_shape=[pltpu.SemaphoreType.DMA((3 * m,)), pltpu.SemaphoreType.DMA((3 * m,))] + [pltpu.HBM(a.shape, a.dtype) for a in lands]
        + [jax.ShapeDtypeStruct((8, LANE), F32)],
        in_specs=[hbm] * m + [sem, pl.BlockSpec(memory_space=pl.ANY)],
        out_specs=[sem, sem] + [hbm] * m + [pl.BlockSpec(memory_space=pltpu.VMEM)],
        input_output_aliases={i: 2 + i for i in range(m)}, compiler_params=pltpu.CompilerParams(**_SPLIT),
    )(*lands, handle["recv1"], after)
    for j, a in enumerate(which):
        handle["lands"][a] = outs[2 + j]
        handle[("leg2", a)] = (outs[0], outs[1], j, m)
    return outs[-1]


def _gather2_end(handle, a, after):
    n = handle["n"]
    send2, recv2, j, m = handle[("leg2", a)]

    def body(src, land, send1, recv1, send2_, recv2_, after_ref, src_out, land_out):
        x, y, c = _mesh_pos()
        me_i = _dev_index(x, y, c)
        sibling = (x, y, 1 - c)
        for k, pos in enumerate([sibling] + [(*chip, c) for chip in _chips(x, y)]):
            first = pltpu.make_async_remote_copy(
                src_ref=src, dst_ref=land.at[me_i], send_sem=send1.at[k * n + a], recv_sem=recv1.at[k * n + a],
                device_id=pos, device_id_type=MESH_T)
            first.wait_send()
            if k == 0:
                first.wait_recv()
        for k in range(3):
            second = pltpu.make_async_remote_copy(
                src_ref=land.at[me_i], dst_ref=land.at[me_i], send_sem=send2_.at[k * m + j], recv_sem=recv2_.at[k * m + j],
                device_id=sibling, device_id_type=MESH_T)
            second.wait_send()
            second.wait_recv()

    hbm = pl.BlockSpec(memory_space=pltpu.HBM)
    sem = pl.BlockSpec(memory_space=pltpu.SEMAPHORE)
    src, land = handle["srcs"][a], handle["lands"][a]
    outs = pl.pallas_call(
        body, name="gather_end_%s_%d" % (handle["tag"], a),
        out_shape=[pltpu.HBM(src.shape, src.dtype), pltpu.HBM(land.shape, land.dtype)],
        in_specs=[hbm, hbm, sem, sem, sem, sem, pl.BlockSpec(memory_space=pl.ANY)], out_specs=[hbm, hbm],
        input_output_aliases={0: 0, 1: 1}, compiler_params=pltpu.CompilerParams(**_SPLIT),
    )(src, land, handle["send1"], handle["recv1"], send2, recv2, after)
    return outs[1]


def _call(body, operands, *, name, grid, in_specs, out_specs, out_shape, scratch_shapes=(), after=()):
    ni, nd = len(in_specs), len(after)

    def ordered(*refs):
        body(*refs[:ni], *refs[ni + nd:])

    outs = pl.pallas_call(
        ordered, name=name, grid=grid, in_specs=list(in_specs) + [pl.BlockSpec(memory_space=pl.ANY)] * nd,
        out_specs=out_specs, out_shape=out_shape, scratch_shapes=list(scratch_shapes), compiler_params=_params(),
    )(*operands, *after)
    return list(outs)


def _rows(a):
    return a.reshape(-1, a.shape[-1])


def _slots(a):
    return a.reshape(N_DEV, -1, a.shape[-1])


def _local_step(x, mem, tgt, w, wire):
    tables = _rope_tables(x.shape[0])
    bd = _head_sum_matrix()
    two = lambda g: jnp.tile(g, (1, 2))
    gq2, gk2 = two(w["q_norm_g"]), two(w["k_norm_g"])

    w_in_t, hg_lb, conv_w = _comm_alone(_Gather([wire["w_in"], w["hg_lb"].reshape(4, -1), w["conv_w"][0]]), "gather_w_in")
    w_in_t = _rows(w_in_t)
    hg_lb = jnp.transpose(hg_lb.reshape(N_DEV, 2, 2, -1), (1, 2, 0, 3)).reshape(2, 2, HG_DIM)
    conv_w = jnp.transpose(conv_w, (1, 0, 2)).reshape(3, 2 * D_FF)
    lb_a0, lb_a1 = hg_lb[:, 0, :], hg_lb[:, 1, :]

    order = ("w_out", "w_xq", "w_xkv", "w_xo", "w_up", "w_down")
    w_in_t, *later = lax.optimization_barrier((w_in_t, *[wire[n] for n in order]))
    fetch, started = _gather2_begin(later, "w")
    take = lambda n, after: _rows(_gather2_end(fetch, order.index(n), after))

    h1, p = _norm_mm(x, w["pre_mix_g"], w_in_t, True, N_IN, "in_proj", after=[started])
    qr, kr = _qk_prep(p, gq2, gk2, tables, bd)
    att, att32, lse = _attn_fwd(qr, kr, p)
    passed = _gather2_forward(fetch, [0, 1, 2, 3], att, "a")
    lb = _lower_bounds(lb_a0, lb_a1)
    of, s_f = _hgrn_fwd(p, lb[0:1], False, after=[passed])
    ob, s_b = _hgrn_fwd(p, lb[1:2], True, after=[passed, of])
    passed = _gather2_forward(fetch, [4, 5], ob, "b")
    w_out = take("w_out", ob)
    cat, mixed, x1 = _mix_out(att, of, ob, p, w["hg_out_norm_g"], w_out, w["post_mix_g"], x, after=[passed])

    w_xq = take("w_xq", x1)
    h2, q2 = _norm_mm(x1, w["pre_x_g"], w_xq, False, 1024, "xq_proj")
    w_xkv_t = take("w_xkv", q2)
    mn, kv = _norm_mm(mem, w["mem_norm_g"], w_xkv_t, True, 2 * D_MODEL, "xkv_proj")
    o2 = _xattn_fwd(q2, kv)
    w_xo = take("w_xo", o2)
    y2, x2 = _mm_postnorm_res(o2, w_xo, w["post_x_g"], x1, "xo_proj")

    w_up_t = take("w_up", x2)
    h3, u = _norm_mm(x2, w["pre_ffn_g"], w_up_t, True, 2 * D_FF, "up_proj", tm=256)
    a = _conv_gate_fwd(u, conv_w, w["conv_b"])
    w_down = take("w_down", a)
    y3, dx3, loss = _mm_postnorm_res_loss(a, w_down, w["post_ffn_g"], x2, tgt, "down_proj")

    g, pending = {}, {}
    dy3, da, g["post_ffn_g"] = _postnorm_bwd_mm(dx3, y3, w["post_ffn_g"], w_down, "down_bwd", BF16)
    pending["down"], started = _exchange_begin([_slots(_dw(a, dy3, "dw_down", tka=D_FF // 2))], "down")
    (du_g, dcw_g, dcb_g), (du_v, dcw_v, dcb_v) = _conv_gate_bwd(u, conv_w, w["conv_b"], da, after=[started])
    g["conv_w"] = jnp.concatenate([dcw_g, dcw_v], axis=1)
    g["conv_b"] = jnp.concatenate([dcb_g, dcb_v], axis=1)
    pending["up"], started = _exchange_begin([_slots(_dw([du_g, du_v], h3, "dw_up", tka=D_FF // 2))], "up")
    dx2, g["pre_ffn_g"] = _mm_prenorm_bwd([du_g, du_v], w_up_t, True, x2, w["pre_ffn_g"], dx3, "up_bwd", after=[started], tm=256)

    dy2, do2, g["post_x_g"] = _postnorm_bwd_mm(dx2, y2, w["post_x_g"], w_xo, "xo_bwd", BF16)
    dw_xo = _dw(o2, dy2, "dw_xo", tka=512)
    dq2, dkv = _xattn_bwd(q2, kv, do2)
    dw_xq = _dw(h2, dq2, "dw_xq", tka=512)
    dkvb = _bf(dkv)
    dw_xkv = _dw(dkvb, mn, "dw_xkv")
    _, g["mem_norm_g"] = _mm_prenorm_bwd(dkvb, w_xkv_t, True, mem, w["mem_norm_g"], jnp.zeros_like(mem), "xkv_bwd")
    dx1, g["pre_x_g"] = _mm_prenorm_bwd(dq2, w_xq, False, x1, w["pre_x_g"], dx2, "xq_bwd")

    dmixed, dcat, g["post_mix_g"] = _postnorm_bwd_mm(dx1, mixed, w["post_mix_g"], w_out, "out_bwd")
    dw_out = _dw(cat, dmixed, "dw_out", tka=512)
    pending["x"], started = _exchange_begin([_slots(dw_xo), _slots(dw_xq), _slots(dw_xkv), _slots(dw_out)], "x")
    do, dhg, g["hg_out_norm_g"] = _rec_bwd(dcat, of, ob, p, w["hg_out_norm_g"])
    dhq_f, dz_f, dhi_f, dlb_f = _hgrn_bwd(p, lb[0:1], do, s_f, False, after=[started])
    dhq, dz_b, dhi, dlb_b = _hgrn_bwd(p, lb[1:2], do, s_b, True, other=(dhq_f, dhi_f))
    d_a0, d_a1 = _lower_bounds_bwd(lb_a0, lb_a1, jnp.concatenate([dlb_f, dlb_b], axis=0))
    g["hg_lb"] = jnp.stack([d_a0, d_a1], axis=1)
    dqr, dkr, dv = _attn_bwd(qr, kr, p, dcat, att32, lse)
    dp_att, dgq, dgk = _qk_prep_bwd(p, dqr, dkr, dv, gq2, gk2, tables, bd)
    g["q_norm_g"], g["k_norm_g"] = dgq, dgk
    dp = [dp_att, dhq, dz_f, dz_b, dhi, dhg]
    pending["in"], started = _exchange_begin([_slots(_dw(dp, h1, "dw_in"))], "in")
    dx, g["pre_mix_g"] = _mm_prenorm_bwd(dp, w_in_t, True, x, w["pre_mix_g"], dx1, "in_bwd", after=[started])
    return loss, dx, g, pending


_COL_SHARDED = ("w_in", "w_xkv", "w_up")
_ROW_SHARDED = ("w_out", "w_xq", "w_xo", "w_down")
_REPLICATED = ("pre_mix_g", "q_norm_g", "k_norm_g", "hg_out_norm_g", "post_mix_g", "pre_x_g", "mem_norm_g", "post_x_g",
               "pre_ffn_g", "conv_b", "post_ffn_g")
_WEIGHTS = ("pre_mix_g", "w_in", "q_norm_g", "k_norm_g", "hg_lb", "hg_out_norm_g", "w_out", "post_mix_g", "pre_x_g",
            "mem_norm_g", "w_xq", "w_xkv", "w_xo", "post_x_g", "pre_ffn_g", "w_up", "conv_w", "conv_b", "w_down",
            "post_ffn_g")
_ADAM_TRANSPOSED = ("w_in", "w_up")
PACK_W = 1024


def _small_plan(shapes):
    plan, r = [], 0
    for vi, (rows, cols) in enumerate(shapes):
        for i in range(rows):
            for c0 in range(0, cols, PACK_W):
                plan.append((vi, i, c0, min(PACK_W, cols - c0), r))
                r += 1
    return plan, -(-r // 8) * 8


def _pack_small(vals):
    plan, nrows = _small_plan([val.shape for val in vals])

    def body(*refs):
        ins, out = refs[:-1], refs[-1]
        out[...] = jnp.zeros_like(out)
        for vi, i, c0, width, r in plan:
            out[r:r + 1, 0:width] = ins[vi][i:i + 1, c0:c0 + width]

    return pl.pallas_call(body, name="pack_small", out_shape=jax.ShapeDtypeStruct((nrows, PACK_W), F32))(*vals)


def _sum_unpack_small(packs, shapes):
    plan, _ = _small_plan(shapes)

    def body(*refs):
        p_ref, outs = refs[0], refs[1:]
        acc = p_ref[0]
        for i in range(1, N_DEV):
            acc = acc + p_ref[i]
        for vi, i, c0, width, r in plan:
            outs[vi][i:i + 1, c0:c0 + width] = acc[r:r + 1, 0:width]

    return pl.pallas_call(body, name="sum_unpack_small", out_shape=[jax.ShapeDtypeStruct(s, F32) for s in shapes])(packs)


def _adamw_many(ws, gs, ms, vs):
    n = len(ws)

    def body(*refs):
        w_refs, g_refs, m_refs, v_refs = (refs[k * n:(k + 1) * n] for k in range(4))
        d_refs, nm_refs, nv_refs = (refs[(4 + k) * n:(5 + k) * n] for k in range(3))
        for k in range(n):
            g_ = g_refs[k][...]
            m_ = ADAM_B1 * m_refs[k][...] + (1.0 - ADAM_B1) * g_
            v_ = ADAM_B2 * v_refs[k][...] + (1.0 - ADAM_B2) * (g_ * g_)
            m_hat = m_ / (1.0 - ADAM_B1 ** ADAM_STEP)
            v_hat = v_ / (1.0 - ADAM_B2 ** ADAM_STEP)
            d_refs[k][...] = -ADAM_LR * (m_hat / (jnp.sqrt(v_hat) + ADAM_EPS) + ADAM_WD * w_refs[k][...])
            nm_refs[k][...] = m_
            nv_refs[k][...] = v_

    shapes = [jax.ShapeDtypeStruct(a.shape, F32) for a in ws]
    outs = pl.pallas_call(body, name="adamw_small", out_shape=shapes * 3)(*ws, *gs, *ms, *vs)
    return outs[:n], outs[n:2 * n], outs[2 * n:]


def kernel(x, mem, pre_mix_g, w_in, q_norm_g, k_norm_g, hg_lb, hg_out_norm_g, w_out, post_mix_g, pre_x_g, mem_norm_g, w_xq, w_xkv, w_xo, post_x_g, pre_ffn_g, w_up, conv_w, conv_b, w_down, post_ffn_g, loss_target, m_pre_mix_g, m_w_in, m_q_norm_g, m_k_norm_g, m_hg_lb, m_hg_out_norm_g, m_w_out, m_post_mix_g, m_pre_x_g, m_mem_norm_g, m_w_xq, m_w_xkv, m_w_xo, m_post_x_g, m_pre_ffn_g, m_w_up, m_conv_w, m_conv_b, m_w_down, m_post_ffn_g, v_pre_mix_g, v_w_in, v_q_norm_g, v_k_norm_g, v_hg_lb, v_hg_out_norm_g, v_w_out, v_post_mix_g, v_pre_x_g, v_mem_norm_g, v_w_xq, v_w_xkv, v_w_xo, v_post_x_g, v_pre_ffn_g, v_w_up, v_conv_w, v_conv_b, v_w_down, v_post_ffn_g):
    args = dict(locals())
    w = {n: args[n] for n in _WEIGHTS}
    m = {n: args["m_" + n] for n in _WEIGHTS}
    v = {n: args["v_" + n] for n in _WEIGHTS}
    me = _dev_index(*_mesh_pos())

    wire = {n: _bf(w[n][0].T) for n in _COL_SHARDED}
    wire.update({n: _bf(w[n][0]) for n in _ROW_SHARDED})

    loss, grad_x, g, pending = _local_step(x[0], mem[0], loss_target[0], w, wire)

    grads, delta, new_m, new_v = {}, {}, {}, {}

    me_arr = jnp.reshape(me, (1,)).astype(jnp.int32)

    def update(n, parts):
        if n in _ADAM_TRANSPOSED:
            outs = _sum_adamw(*parts, me_arr, w[n][0].T, m[n][0].T, v[n][0].T, "update_" + n)
            grads[n], delta[n], new_m[n], new_v[n] = (a.T[None] for a in outs)
        elif n in _COL_SHARDED:
            gsum = _sum_parts(*parts, me_arr, "sum_" + n).T
            grads[n] = gsum[None]
            delta[n], new_m[n], new_v[n] = (a[None] for a in _adamw(w[n][0], gsum, m[n][0], v[n][0], "adamw_" + n))
        else:
            outs = _sum_adamw(*parts, me_arr, w[n][0], m[n][0], v[n][0], "update_" + n)
            grads[n], delta[n], new_m[n], new_v[n] = (a[None] for a in outs)

    small = list(_REPLICATED) + ["hg_lb", "conv_w"]
    vals = [g[n] for n in _REPLICATED] + [g["hg_lb"].reshape(4, HG_DIM), g["conv_w"], jnp.pad(loss, ((0, 0), (0, LANE - 1)))]
    shapes = [val.shape for val in vals]
    fetch_small, _ = _gather_begin([_pack_small(vals)], "small")

    after = grad_x
    for tag, names in (("down", ["w_down"]), ("up", ["w_up"]), ("x", ["w_xo", "w_xq", "w_xkv", "w_out"]),
                       ("in", ["w_in"])):
        for n, parts in zip(names, _exchange_end(pending[tag], after)):
            update(n, parts)
            after = new_v[n]

    (packs,) = _gather_end(fetch_small, [0], after)
    summed = _sum_unpack_small(packs, shapes)
    loss = summed[-1][0, 0]
    for n, s in zip(small, summed[:-1]):
        grads[n] = s
    fold = lambda v2: v2[:, :ATT_HEAD_DIM] + v2[:, ATT_HEAD_DIM:]
    grads["q_norm_g"], grads["k_norm_g"] = fold(grads["q_norm_g"]), fold(grads["k_norm_g"])
    grads["hg_lb"] = lax.dynamic_slice_in_dim(grads["hg_lb"].reshape(2, 2, HG_DIM), me * (HG_DIM // N_DEV),
                                              HG_DIM // N_DEV, axis=2)
    grads["conv_w"] = lax.dynamic_slice_in_dim(grads["conv_w"], me * (2 * D_FF // N_DEV), 2 * D_FF // N_DEV, axis=1)[None]

    flat2 = lambda a: a.reshape(-1, a.shape[-1])
    outs = _adamw_many(*[[flat2(d[n]) for n in small] for d in (w, grads, m, v)])
    for dst, vals in zip((delta, new_m, new_v), outs):
        for n, val in zip(small, vals):
            dst[n] = val.reshape(w[n].shape)

    return (loss, grad_x[None], *[grads[n] for n in _WEIGHTS], *[delta[n] for n in _WEIGHTS],
            *[new_m[n] for n in _WEIGHTS], *[new_v[n] for n in _WEIGHTS])
```

```python
import jax
import jax.numpy as jnp
from jax import lax
from jax.experimental import pallas as pl
from jax.experimental.pallas import tpu as pltpu

F32 = jnp.float32
BF16 = jnp.bfloat16

D_MODEL = 1024
GRID_W = 64
EPS = 1e-6
ATT_HEADS = 8
ATT_HEAD_DIM = 64
ATT_Q_DIM = 512
ATT_KV_DIM = 128
ROPE_THETA = 10000.0
HG_HEADS = 4
HG_DIM = 512
HG_CHUNK = 32
HG_CHUNK_LOG2 = 5
HG_BLOCK_FWD = 256
HG_BLOCK_BWD = 128
N_IN = 3328
X_HEADS = 4
X_HEAD_DIM = 256
D_FF = 2816
N_DEV = 8
LANE = 128
ADAM_LR = 0.001
ADAM_B1 = 0.9
ADAM_B2 = 0.999
ADAM_EPS = 1e-08
ADAM_WD = 0.01
ADAM_STEP = 10
VMEM_LIMIT = 56 * 1024 * 1024

MESH_T = pl.DeviceIdType.MESH


def _params(**kw):
    return pltpu.CompilerParams(vmem_limit_bytes=VMEM_LIMIT, **kw)


def _dot(a, b, ca, cb):
    return lax.dot_general(a, b, (((ca,), (cb,)), ((), ())), preferred_element_type=F32)


def _bf(x):
    return x.astype(BF16)


def _sigmoid(x):
    return 1.0 / (1.0 + jnp.exp(-x))


def _rms_fwd(x, g):
    r = lax.rsqrt(jnp.mean(x * x, axis=-1, keepdims=True) + EPS)
    return x * r * g


def _rms_bwd(dy, x, g):
    r = lax.rsqrt(jnp.mean(x * x, axis=-1, keepdims=True) + EPS)
    xh = x * r
    dg = jnp.sum(dy * xh, axis=0, keepdims=True)
    t = dy * g
    dx = r * (t - xh * jnp.mean(t * xh, axis=-1, keepdims=True))
    return dx, dg


def _full(shape):
    nd = len(shape)
    return pl.BlockSpec(shape, lambda *a: (0,) * nd)


def _norm_mm(x, g, w, trans, tn, name, after=(), tm=512):
    t, d = x.shape
    n = w.shape[0] if trans else w.shape[1]
    tm = min(tm, t)

    def body(x_ref, g_ref, w_ref, h_ref, p_ref):
        h = _bf(_rms_fwd(x_ref[...], g_ref[...]))
        h_ref[...] = h
        p_ref[...] = _dot(h, w_ref[...], 1, 1 if trans else 0)

    w_spec = pl.BlockSpec((tn, d), lambda i, j: (j, 0)) if trans else pl.BlockSpec((d, tn), lambda i, j: (0, j))
    return _call(
        body, (x, g, w), name=name, grid=(t // tm, n // tn),
        in_specs=[pl.BlockSpec((tm, d), lambda i, j: (i, 0)), _full((1, d)), w_spec],
        out_specs=[pl.BlockSpec((tm, d), lambda i, j: (i, 0)), pl.BlockSpec((tm, tn), lambda i, j: (i, j))],
        out_shape=[jax.ShapeDtypeStruct((t, d), BF16), jax.ShapeDtypeStruct((t, n), F32)], after=after)


def _mm_postnorm_res(a, w, g, res, name, tm=256):
    t, k = a.shape
    d = w.shape[1]

    def body(a_ref, w_ref, g_ref, res_ref, y_ref, o_ref):
        y = _dot(a_ref[...], w_ref[...], 1, 0)
        y_ref[...] = y
        o_ref[...] = res_ref[...] + _rms_fwd(y, g_ref[...])

    row = lambda width: pl.BlockSpec((tm, width), lambda i: (i, 0))
    return pl.pallas_call(
        body, name=name, grid=(t // tm,),
        in_specs=[row(k), _full((k, d)), _full((1, d)), row(d)],
        out_specs=[row(d), row(d)],
        out_shape=[jax.ShapeDtypeStruct((t, d), F32)] * 2,
        compiler_params=_params(),
    )(a, w, g, res)


def _mm_postnorm_res_loss(a, w, g, res, tgt, name, tm=256):
    t, k = a.shape
    d = w.shape[1]

    def body(a_ref, w_ref, g_ref, res_ref, tgt_ref, y_ref, dout_ref, loss_ref):
        @pl.when(pl.program_id(0) == 0)
        def _():
            loss_ref[...] = jnp.zeros_like(loss_ref)

        y = _dot(a_ref[...], w_ref[...], 1, 0)
        y_ref[...] = y
        diff = res_ref[...] + _rms_fwd(y, g_ref[...]) - tgt_ref[...]
        dout_ref[...] = diff * (1.0 / d)
        part = jnp.sum(jnp.sum(diff * diff, axis=-1, keepdims=True), axis=0, keepdims=True)
        loss_ref[...] += (0.5 / d) * part

    row = lambda width: pl.BlockSpec((tm, width), lambda i: (i, 0))
    return pl.pallas_call(
        body, name=name, grid=(t // tm,),
        in_specs=[row(k), _full((k, d)), _full((1, d)), row(d), row(d)],
        out_specs=[row(d), row(d), _full((1, 1))],
        out_shape=[jax.ShapeDtypeStruct((t, d), F32)] * 2 + [jax.ShapeDtypeStruct((1, 1), F32)],
        compiler_params=_params(),
    )(a, w, g, res, tgt)


def _postnorm_bwd_mm(dout, y, g, w, name, da_dtype=F32, tm=256):
    t, d = y.shape
    k = w.shape[0]

    def body(dout_ref, y_ref, g_ref, w_ref, dy_ref, da_ref, dg_ref):
        @pl.when(pl.program_id(0) == 0)
        def _():
            dg_ref[...] = jnp.zeros_like(dg_ref)

        dy, dg = _rms_bwd(dout_ref[...], y_ref[...], g_ref[...])
        dg_ref[...] += dg
        dyb = _bf(dy)
        dy_ref[...] = dyb
        da_ref[...] = _dot(dyb, w_ref[...], 1, 1).astype(da_dtype)

    row = lambda width: pl.BlockSpec((tm, width), lambda i: (i, 0))
    return pl.pallas_call(
        body, name=name, grid=(t // tm,),
        in_specs=[row(d), row(d), _full((1, d)), _full((k, d))],
        out_specs=[row(d), row(k), _full((1, d))],
        out_shape=[jax.ShapeDtypeStruct((t, d), BF16), jax.ShapeDtypeStruct((t, k), da_dtype), jax.ShapeDtypeStruct((1, d), F32)],
        compiler_params=_params(),
    )(dout, y, g, w)


def _mm_prenorm_bwd(dp, w, trans, x, g, dres, name, after=(), tm=512):
    dps = list(dp) if isinstance(dp, (list, tuple)) else [dp]
    nparts = len(dps)
    t = dps[0].shape[0]
    widths = [part.shape[1] for part in dps]
    d = x.shape[1]
    tm = min(tm, t)

    def body(*refs):
        dp_refs = refs[:nparts]
        w_ref, x_ref, g_ref, dres_ref, dx_ref, dg_ref = refs[nparts:]

        @pl.when(pl.program_id(0) == 0)
        def _():
            dg_ref[...] = jnp.zeros_like(dg_ref)

        dh = None
        for k, dp_ref in enumerate(dp_refs):
            cols = slice(sum(widths[:k]), sum(widths[:k + 1]))
            term = _dot(dp_ref[...], w_ref[cols, :], 1, 0) if trans else _dot(dp_ref[...], w_ref[:, cols], 1, 1)
            dh = term if dh is None else dh + term
        dx, dg = _rms_bwd(dh, x_ref[...], g_ref[...])
        dg_ref[...] += dg
        dx_ref[...] = dres_ref[...] + dx

    row = lambda width: pl.BlockSpec((tm, width), lambda i: (i, 0))
    return _call(
        body, (*dps, w, x, g, dres), name=name, grid=(t // tm,),
        in_specs=[row(width) for width in widths] + [_full(w.shape), row(d), _full((1, d)), row(d)],
        out_specs=[row(d), _full((1, d))],
        out_shape=[jax.ShapeDtypeStruct((t, d), F32), jax.ShapeDtypeStruct((1, d), F32)], after=after)


def _dw(a, b, name, tka=256, after=()):
    parts = list(a) if isinstance(a, (list, tuple)) else [a]
    nparts = len(parts)
    t, nb = b.shape
    tiles = [part.shape[1] // tka for part in parts]
    first = [sum(tiles[:k]) for k in range(nparts)]

    def body(*refs):
        a_refs, b_ref, o_ref = refs[:nparts], refs[nparts], refs[-1]
        i = pl.program_id(0)
        for k, a_ref in enumerate(a_refs):
            @pl.when((i >= first[k]) & (i < first[k] + tiles[k]))
            def _():
                o_ref[...] = _bf(_dot(a_ref[...], b_ref[...], 0, 0))

    a_spec = lambda k: pl.BlockSpec((t, tka), lambda i: (0, jnp.clip(i - first[k], 0, tiles[k] - 1)))
    return pl.pallas_call(
        body, name=name, grid=(sum(tiles),),
        in_specs=[a_spec(k) for k in range(nparts)] + [_full((t, nb))] + [pl.BlockSpec(memory_space=pl.ANY)] * len(after),
        out_specs=pl.BlockSpec((tka, nb), lambda i: (i, 0)),
        out_shape=jax.ShapeDtypeStruct((sum(tiles) * tka, nb), BF16),
        compiler_params=_params(),
    )(*parts, b, *after)


def _rope_tables(t):
    pos = jnp.arange(t)
    r = (pos // GRID_W).astype(F32)
    c = (pos % GRID_W).astype(F32)
    npair = ATT_HEAD_DIM // 4
    inv = jnp.power(ROPE_THETA, -jnp.arange(npair, dtype=F32) / npair)
    ang = jnp.concatenate([r[:, None] * inv, c[:, None] * inv], axis=-1)
    cos = jnp.repeat(jnp.cos(ang), 2, axis=-1)
    sin = jnp.repeat(jnp.sin(ang), 2, axis=-1)
    even = (jnp.arange(ATT_HEAD_DIM) % 2) == 0
    sa = jnp.where(even, -sin, 0.0)
    sb = jnp.where(even, 0.0, sin)
    two = lambda a: jnp.tile(a, (1, 2))
    return two(cos), two(sa), two(sb)


def _head_sum_matrix():
    a = jnp.arange(LANE) // ATT_HEAD_DIM
    return (a[:, None] == a[None, :]).astype(BF16)


def _head_mean(v, bd):
    hi = _bf(v)
    lo = _bf(v - hi.astype(F32))
    return (_dot(hi, bd, 1, 0) + _dot(lo, bd, 1, 0)) * (1.0 / ATT_HEAD_DIM)


def _qk_prep(p, gq, gk, tables, bd, tm=512):
    t = p.shape[0]
    tm = min(tm, t)
    cc, sa, sb = tables

    def body(p_ref, gq_ref, gk_ref, cc_ref, sa_ref, sb_ref, bd_ref, q_ref, k_ref):
        cc_, sa_, sb_, bd_ = cc_ref[...], sa_ref[...], sb_ref[...], bd_ref[...]
        low = lax.broadcasted_iota(jnp.int32, (tm, LANE), 1) < ATT_HEAD_DIM

        def normrope(xs, g):
            xn = xs * lax.rsqrt(_head_mean(xs * xs, bd_) + EPS) * g
            return xn * cc_ + pltpu.roll(xn, LANE - 1, 1) * sa_ + pltpu.roll(xn, 1, 1) * sb_

        for j in range(4):
            y = normrope(p_ref[:, j * LANE:(j + 1) * LANE], gq_ref[...]) * (ATT_HEAD_DIM ** -0.5)
            yr = pltpu.roll(y, ATT_HEAD_DIM, 1)
            if j // 2 == 0:
                h0, h1 = jnp.where(low, y, 0.0), jnp.where(low, yr, 0.0)
            else:
                h0, h1 = jnp.where(low, 0.0, yr), jnp.where(low, 0.0, y)
            q_ref[:, (2 * j) * LANE:(2 * j + 1) * LANE] = _bf(h0)
            q_ref[:, (2 * j + 1) * LANE:(2 * j + 2) * LANE] = _bf(h1)
        k_ref[...] = _bf(normrope(p_ref[:, ATT_Q_DIM:ATT_Q_DIM + LANE], gk_ref[...]))

    row = lambda width: pl.BlockSpec((tm, width), lambda i: (i, 0))
    return pl.pallas_call(
        body, name="qk_prep", grid=(t // tm,),
        in_specs=[row(ATT_Q_DIM + LANE), _full((1, LANE)), _full((1, LANE)), row(LANE), row(LANE), row(LANE),
                  _full((LANE, LANE))],
        out_specs=[row(ATT_HEADS * LANE), row(LANE)],
        out_shape=[jax.ShapeDtypeStruct((t, ATT_HEADS * LANE), BF16), jax.ShapeDtypeStruct((t, LANE), BF16)],
        compiler_params=_params(),
    )(p, gq, gk, cc, sa, sb, bd)


def _qk_prep_bwd(p, dq, dk, dv, gq, gk, tables, bd, tm=512):
    t = p.shape[0]
    tm = min(tm, t)
    cc, sa, sb = tables
    att_w = ATT_Q_DIM + 2 * ATT_KV_DIM

    def body(p_ref, dq_ref, dk_ref, dv_ref, gq_ref, gk_ref, cc_ref, sa_ref, sb_ref, bd_ref, dp_ref, dgq_ref, dgk_ref):
        dp_ref[:, ATT_Q_DIM + LANE:att_w] = _bf(dv_ref[...])

        @pl.when(pl.program_id(0) == 0)
        def _():
            dgq_ref[...] = jnp.zeros_like(dgq_ref)
            dgk_ref[...] = jnp.zeros_like(dgk_ref)

        cc_, sa_, sb_, bd_ = cc_ref[...], sa_ref[...], sb_ref[...], bd_ref[...]
        low = lax.broadcasted_iota(jnp.int32, (tm, LANE), 1) < ATT_HEAD_DIM

        def bwd(xs, g, dy):
            r = lax.rsqrt(_head_mean(xs * xs, bd_) + EPS)
            xh = xs * r
            dxn = dy * cc_ + pltpu.roll(dy * sa_, 1, 1) + pltpu.roll(dy * sb_, LANE - 1, 1)
            dg = jnp.sum(dxn * xh, axis=0, keepdims=True)
            tt = dxn * g
            return r * (tt - xh * _head_mean(tt * xh, bd_)), dg

        dgq = jnp.zeros((1, LANE), F32)
        for j in range(4):
            d0 = dq_ref[:, (2 * j) * LANE:(2 * j + 1) * LANE]
            d1 = dq_ref[:, (2 * j + 1) * LANE:(2 * j + 2) * LANE]
            if j // 2 == 0:
                dy = jnp.where(low, d0, pltpu.roll(d1, ATT_HEAD_DIM, 1))
            else:
                dy = jnp.where(low, pltpu.roll(d0, ATT_HEAD_DIM, 1), d1)
            dx, dg = bwd(p_ref[:, j * LANE:(j + 1) * LANE], gq_ref[...], dy * (ATT_HEAD_DIM ** -0.5))
            dp_ref[:, j * LANE:(j + 1) * LANE] = _bf(dx)
            dgq = dgq + dg
        dgq_ref[...] += dgq
        dx, dg = bwd(p_ref[:, ATT_Q_DIM:ATT_Q_DIM + LANE], gk_ref[...], dk_ref[...])
        dp_ref[:, ATT_Q_DIM:ATT_Q_DIM + LANE] = _bf(dx)
        dgk_ref[...] += dg

    row = lambda width: pl.BlockSpec((tm, width), lambda i: (i, 0))
    return pl.pallas_call(
        body, name="qk_prep_bwd", grid=(t // tm,),
        in_specs=[row(ATT_Q_DIM + LANE), row(ATT_HEADS * LANE), row(LANE), row(LANE), _full((1, LANE)), _full((1, LANE)),
                  row(LANE), row(LANE), row(LANE), _full((LANE, LANE))],
        out_specs=[row(att_w), _full((1, LANE)), _full((1, LANE))],
        out_shape=[jax.ShapeDtypeStruct((t, att_w), BF16), jax.ShapeDtypeStruct((1, LANE), F32),
                   jax.ShapeDtypeStruct((1, LANE), F32)],
        compiler_params=_params(),
    )(p, dq, dk, dv, gq, gk, cc, sa, sb, bd)


def _attn_fwd(q, k, p, tq=256):
    t = k.shape[0]
    tq = min(tq, t)
    v_blk = (ATT_Q_DIM + ATT_KV_DIM) // LANE

    def body(q_ref, k_ref, v_ref, o_ref, o32_ref, lse_ref):
        k_ = k_ref[...]
        v = v_ref[...]
        lane_k = lax.broadcasted_iota(jnp.int32, (t, LANE), 1)
        lane_q = lax.broadcasted_iota(jnp.int32, (tq, LANE), 1)
        lowk, lowq = lane_k < ATT_HEAD_DIM, lane_q < ATT_HEAD_DIM
        ones_lane = (ATT_HEAD_DIM, 0)
        vm = (_bf(jnp.where(lowk, v, jnp.where(lane_k == ones_lane[0], 1.0, 0.0))),
              _bf(jnp.where(lowk, jnp.where(lane_k == ones_lane[1], 1.0, 0.0), v)))
        for j in range(4):
            kvh = j // 2
            acc = None
            for sub in range(2):
                h = 2 * j + sub
                s = _dot(q_ref[:, h * LANE:(h + 1) * LANE], k_, 1, 1)
                mx = jnp.max(s, axis=-1, keepdims=True)
                ov = _dot(jnp.exp(_bf(s - mx)), vm[kvh], 1, 0)
                l = jnp.sum(jnp.where(lane_q == ones_lane[kvh], ov, 0.0), axis=-1, keepdims=True)
                lse_ref[h] = mx + jnp.log(l)
                o = jnp.where(lowq if kvh == 0 else ~lowq, ov, 0.0) * (1.0 / l)
                if sub != kvh:
                    o = pltpu.roll(o, ATT_HEAD_DIM, 1)
                acc = o if acc is None else acc + o
            o32_ref[:, j * LANE:(j + 1) * LANE] = acc
            o_ref[:, j * LANE:(j + 1) * LANE] = _bf(acc)

    row = pl.BlockSpec((tq, ATT_Q_DIM), lambda i: (i, 0))
    return _call(
        body, (q, k, p), name="attn_fwd", grid=(t // tq,),
        in_specs=[pl.BlockSpec((tq, ATT_HEADS * LANE), lambda i: (i, 0)), _full((t, LANE)),
                  pl.BlockSpec((t, LANE), lambda i: (0, v_blk))],
        out_specs=[row, row, pl.BlockSpec((ATT_HEADS, tq, 1), lambda i: (0, i, 0))],
        out_shape=[jax.ShapeDtypeStruct((t, ATT_Q_DIM), BF16), jax.ShapeDtypeStruct((t, ATT_Q_DIM), F32),
                   jax.ShapeDtypeStruct((ATT_HEADS, t, 1), F32)])


def _attn_bwd(q, k, p, dcat, o32, lse, tq=256):
    t = k.shape[0]
    tq = min(tq, t)
    v_blk = (ATT_Q_DIM + ATT_KV_DIM) // LANE

    def body(q_ref, k_ref, v_ref, do_ref, o_ref, lse_ref, dq_ref, dk_ref, dv_ref):
        @pl.when(pl.program_id(0) == 0)
        def _():
            dk_ref[...] = jnp.zeros_like(dk_ref)
            dv_ref[...] = jnp.zeros_like(dv_ref)

        k_ = k_ref[...]
        vb = _bf(v_ref[...])
        lowq = lax.broadcasted_iota(jnp.int32, (tq, LANE), 1) < ATT_HEAD_DIM
        dk_acc = jnp.zeros((t, LANE), F32)
        dv_acc = jnp.zeros((t, LANE), F32)
        for j in range(4):
            kvh = j // 2
            dop = do_ref[:, j * LANE:(j + 1) * LANE]
            prod = dop * o_ref[:, j * LANE:(j + 1) * LANE]
            d_low = jnp.sum(jnp.where(lowq, prod, 0.0), axis=-1, keepdims=True)
            d_sub = (d_low, jnp.sum(prod, axis=-1, keepdims=True) - d_low)
            for sub in range(2):
                h = 2 * j + sub
                src = dop if sub == kvh else pltpu.roll(dop, ATT_HEAD_DIM, 1)
                do_h = _bf(jnp.where(lowq, src, 0.0) if kvh == 0 else jnp.where(lowq, 0.0, src))
                qh = q_ref[:, h * LANE:(h + 1) * LANE]
                pr = jnp.exp(_bf(_dot(qh, k_, 1, 1) - lse_ref[h]))
                ds = pr * _bf(_dot(do_h, vb, 1, 1) - d_sub[sub])
                dq_ref[:, h * LANE:(h + 1) * LANE] = _dot(ds, k_, 1, 0)
                dk_acc = dk_acc + _dot(ds, qh, 0, 0)
                dv_acc = dv_acc + _dot(pr, do_h, 0, 0)
        dk_ref[...] += dk_acc
        dv_ref[...] += dv_acc

    row = pl.BlockSpec((tq, ATT_Q_DIM), lambda i: (i, 0))
    return _call(
        body, (q, k, p, dcat, o32, lse), name="attn_bwd", grid=(t // tq,),
        in_specs=[pl.BlockSpec((tq, ATT_HEADS * LANE), lambda i: (i, 0)), _full((t, LANE)),
                  pl.BlockSpec((t, LANE), lambda i: (0, v_blk)), row, row,
                  pl.BlockSpec((ATT_HEADS, tq, 1), lambda i: (0, i, 0))],
        out_specs=[pl.BlockSpec((tq, ATT_HEADS * LANE), lambda i: (i, 0)), _full((t, LANE)), _full((t, LANE))],
        out_shape=[jax.ShapeDtypeStruct((t, ATT_HEADS * LANE), F32), jax.ShapeDtypeStruct((t, LANE), F32),
                   jax.ShapeDtypeStruct((t, LANE), F32)])


def _lower_bounds(a0, a1):
    def body(a0_ref, a1_ref, lb_ref):
        m = jnp.maximum(a0_ref[...], a1_ref[...])
        e0, e1 = jnp.exp(a0_ref[...] - m), jnp.exp(a1_ref[...] - m)
        lb_ref[...] = e0 / (e0 + e1)

    return pl.pallas_call(body, name="lower_bounds", out_shape=jax.ShapeDtypeStruct(a0.shape, F32))(a0, a1)


def _lower_bounds_bwd(a0, a1, dlb):
    def body(a0_ref, a1_ref, dlb_ref, d0_ref, d1_ref):
        m = jnp.maximum(a0_ref[...], a1_ref[...])
        e0, e1 = jnp.exp(a0_ref[...] - m), jnp.exp(a1_ref[...] - m)
        lb = e0 / (e0 + e1)
        d0 = dlb_ref[...] * lb * (1.0 - lb)
        d0_ref[...] = d0
        d1_ref[...] = -d0

    return pl.pallas_call(body, name="lower_bounds_bwd", out_shape=[jax.ShapeDtypeStruct(a0.shape, F32)] * 2)(a0, a1, dlb)


def _chunk_scan(x, pos, down):
    n = x.shape[0]
    s = 1
    while s < HG_CHUNK:
        if down:
            x = x + jnp.where(pos >= s, pltpu.roll(x, s, 0), 0.0)
        else:
            x = x + jnp.where(pos < HG_CHUNK - s, pltpu.roll(x, n - s, 0), 0.0)
        s *= 2
    return x


def _chunk_concat(x, nc):
    xb = _bf(x)
    rows = []
    for c in range(nc):
        pieces = [xb[c * HG_CHUNK:(c + 1) * HG_CHUNK, :]]
        if c:
            pieces.insert(0, jnp.zeros((HG_CHUNK, c * LANE), BF16))
        if c < nc - 1:
            pieces.append(jnp.zeros((HG_CHUNK, (nc - 1 - c) * LANE), BF16))
        rows.append(jnp.concatenate(pieces, axis=1) if len(pieces) > 1 else pieces[0])
    return jnp.concatenate(rows, axis=0) if nc > 1 else rows[0]


def _chunk_pick(x, nc):
    rows = [x[c * HG_CHUNK:(c + 1) * HG_CHUNK, c * LANE:(c + 1) * LANE] for c in range(nc)]
    return jnp.concatenate(rows, axis=0) if nc > 1 else rows[0]


def _chunk_rows(vals):
    rows = [jnp.broadcast_to(v, (HG_CHUNK, LANE)) for v in vals]
    return jnp.concatenate(rows, axis=0) if len(rows) > 1 else rows[0]


def _hgrn_gates(hq, z, lb):
    sq = _sigmoid(hq)
    sig = _sigmoid(z)
    f = lb + (1.0 - lb) * sig
    return hq * sq, sq, sig, f, jnp.log(f)


def _hgrn_local(q, f, g, v, pos, cid, amask, rev, nc):
    k = 1.0 - f
    ba = _chunk_scan(g, pos, not rev)
    bb = _chunk_scan(g, pos, rev)
    eq = 0.5 * (ba - bb + g)
    ex_q, ex_k, ex_i, ex_d = jnp.exp(eq), jnp.exp(-eq), jnp.exp(ba), jnp.exp(bb - g)
    qb, kb, qi, kd = q * ex_q, k * ex_k, q * ex_i, k * ex_d
    dvec = jnp.exp(ba + bb - g)
    a = jnp.where(amask, _dot(_bf(qb), _bf(kb), 1, 1), 0.0)
    kd_m = _chunk_concat(kd, nc)
    qi_m = _chunk_concat(qi, nc)
    ut_all = _dot(_bf(v), kd_m, 0, 0)
    return dict(k=k, ex_q=ex_q, ex_k=ex_k, ex_i=ex_i, ex_d=ex_d, qb=qb, kb=kb, qi=qi, kd=kd, dvec=dvec, a=a,
                kd_m=kd_m, qi_m=qi_m, ut_all=ut_all)


def _hgrn_masks(n, rev):
    row = lax.broadcasted_iota(jnp.int32, (n, LANE), 0)
    ti = lax.broadcasted_iota(jnp.int32, (n, n), 0)
    si = lax.broadcasted_iota(jnp.int32, (n, n), 1)
    tri = (si >= ti) if rev else (si <= ti)
    same = jnp.right_shift(ti, HG_CHUNK_LOG2) == jnp.right_shift(si, HG_CHUNK_LOG2)
    return jnp.bitwise_and(row, HG_CHUNK - 1), jnp.right_shift(row, HG_CHUNK_LOG2), same & tri


def _hgrn_specs(t, rev, bwd):
    n = min(HG_BLOCK_BWD if bwd else HG_BLOCK_FWD, t)
    nb = t // n
    if rev != bwd:
        blk = lambda i: nb - 1 - i
    else:
        blk = lambda i: i
    col = lambda base: pl.BlockSpec((n, 2 * LANE), lambda hp, i: (blk(i), base + hp))
    return n, nb, blk, col


def _hgrn_fwd(p, lb, rev, after=()):
    t = p.shape[0]
    n, nb, blk, col = _hgrn_specs(t, rev, False)
    nc = n // HG_CHUNK
    sub = n // min(HG_BLOCK_BWD, t)
    z_base = 7 if rev else 5

    def body(hq_ref, z_ref, hi_ref, lb_ref, o_ref, ssave_ref, st_scr):
        @pl.when(pl.program_id(1) == 0)
        def _():
            st_scr[...] = jnp.zeros_like(st_scr)

        pos, cid, amask = _hgrn_masks(n, rev)
        order = list(range(nc))[::-1] if rev else list(range(nc))
        for hh in range(2):
            sl = slice(hh * LANE, (hh + 1) * LANE)
            q, _, _, f, g = _hgrn_gates(hq_ref[:, sl], z_ref[:, sl], lb_ref[0:1, sl])
            v = hi_ref[:, sl]
            c_ = _hgrn_local(q, f, g, v, pos, cid, amask, rev, nc)
            st = st_scr[hh]
            cols = [None] * nc
            for c in order:
                cols[c] = st
                st = st * c_["dvec"][c * HG_CHUNK:c * HG_CHUNK + 1, :] + c_["ut_all"][:, c * LANE:(c + 1) * LANE]
            st_scr[hh] = st
            for s in range(sub):
                ssave_ref[sub - 1 - s if rev else s, sl, :] = cols[order[s * (nc // sub)]]
            st_all = _bf(jnp.concatenate(cols, axis=1))
            o_ref[:, sl] = _dot(_bf(c_["a"]), _bf(v), 1, 0) + _dot(c_["qi_m"], st_all, 1, 1)

    return _call(
        body, (p, p, p, lb), name="hgrn_fwd_rev" if rev else "hgrn_fwd", grid=(2, nb),
        in_specs=[col(3), col(z_base), col(9), pl.BlockSpec((1, 2 * LANE), lambda hp, i: (0, hp))],
        out_specs=[pl.BlockSpec((n, 2 * LANE), lambda hp, i: (blk(i), hp)),
                   pl.BlockSpec((sub, 2 * LANE, LANE), lambda hp, i: (blk(i), hp, 0))],
        out_shape=[jax.ShapeDtypeStruct((t, HG_DIM), F32), jax.ShapeDtypeStruct((nb * sub, HG_DIM, LANE), F32)],
        scratch_shapes=[pltpu.VMEM((2, LANE, LANE), F32)], after=after)


def _hgrn_bwd(p, lb, do, ssave, rev, other=None, after=()):
    t = p.shape[0]
    n, nb, blk, col = _hgrn_specs(t, rev, True)
    nc = n // HG_CHUNK
    z_base = 7 if rev else 5
    n_other = 0 if other is None else 2

    def body(*refs):
        hq_ref, z_ref, hi_ref, lb_ref, do_ref, ssave_ref = refs[:6]
        dhq_ref, dz_ref, dhi_ref, dlb_ref, dst_scr = refs[6 + n_other:]

        @pl.when(pl.program_id(1) == 0)
        def _():
            dst_scr[...] = jnp.zeros_like(dst_scr)
            dlb_ref[...] = jnp.zeros_like(dlb_ref)

        pos, cid, amask = _hgrn_masks(n, rev)
        order = list(range(nc))[::-1] if rev else list(range(nc))
        for hh in range(2):
            sl = slice(hh * LANE, (hh + 1) * LANE)
            hq, lbv = hq_ref[:, sl], lb_ref[0:1, sl]
            q, sq, sig, f, g = _hgrn_gates(hq, z_ref[:, sl], lbv)
            v = hi_ref[:, sl]
            c_ = _hgrn_local(q, f, g, v, pos, cid, amask, rev, nc)
            dvec, ut_all = c_["dvec"], c_["ut_all"]
            drow = lambda c: dvec[c * HG_CHUNK:c * HG_CHUNK + 1, :]
            st = ssave_ref[0, sl, :]
            cols = [None] * nc
            for c in order:
                cols[c] = st
                st = st * drow(c) + ut_all[:, c * LANE:(c + 1) * LANE]
            dob, vb = _bf(do_ref[:, sl]), _bf(v)
            gt_all = _dot(dob, c_["qi_m"], 0, 0)
            dcur = dst_scr[hh]
            dnext = [None] * nc
            ddrow = [None] * nc
            for c in order[::-1]:
                dnext[c] = dcur
                ddrow[c] = jnp.sum(cols[c] * dcur, axis=0, keepdims=True) * drow(c)
                dcur = dcur * drow(c) + gt_all[:, c * LANE:(c + 1) * LANE]
            dst_scr[hh] = dcur
            dsn_all = _bf(jnp.concatenate(dnext, axis=1))
            st_all = _bf(jnp.concatenate(cols, axis=1))
            da = _bf(jnp.where(amask, _dot(dob, vb, 1, 1), 0.0))
            dv = _dot(_bf(c_["a"]), dob, 0, 0) + _dot(c_["kd_m"], dsn_all, 1, 1)
            dqb = _dot(da, _bf(c_["kb"]), 1, 0)
            dkb = _dot(da, _bf(c_["qb"]), 0, 0)
            dqi = _chunk_pick(_dot(dob, st_all, 1, 0), nc)
            dkd = _chunk_pick(_dot(vb, dsn_all, 1, 0), nc)
            dq = dqb * c_["ex_q"] + dqi * c_["ex_i"]
            dk = dkb * c_["ex_k"] + dkd * c_["ex_d"]
            e = dqb * c_["qb"] - dkb * c_["kb"] + dqi * c_["qi"]
            w = dkd * c_["kd"]
            dg = _chunk_scan(e, pos, rev) + (_chunk_scan(w, pos, not rev) - w) + _chunk_rows(ddrow)
            df = dg / f - dk
            dz_ref[:, sl] = _bf(df * (1.0 - lbv) * sig * (1.0 - sig))
            dlb_ref[0:1, sl] += jnp.sum(df * (1.0 - sig), axis=0, keepdims=True)
            dhq = dq * (sq * (1.0 + hq * (1.0 - sq)))
            if other is None:
                dhq_ref[:, sl], dhi_ref[:, sl] = dhq, dv
            else:
                dhq_ref[:, sl], dhi_ref[:, sl] = _bf(dhq + refs[6][:, sl]), _bf(dv + refs[7][:, sl])

    out = pl.BlockSpec((n, 2 * LANE), lambda hp, i: (blk(i), hp))
    sum_dtype = F32 if other is None else BF16
    return _call(
        body, (p, p, p, lb, do, ssave, *(other or ())), name="hgrn_bwd_rev" if rev else "hgrn_bwd", grid=(2, nb),
        in_specs=[col(3), col(z_base), col(9), pl.BlockSpec((1, 2 * LANE), lambda hp, i: (0, hp)), out,
                  pl.BlockSpec((1, 2 * LANE, LANE), lambda hp, i: (blk(i), hp, 0))] + [out] * n_other,
        out_specs=[out, out, out, pl.BlockSpec((1, 2 * LANE), lambda hp, i: (0, hp))],
        out_shape=[jax.ShapeDtypeStruct((t, HG_DIM), sum_dtype), jax.ShapeDtypeStruct((t, HG_DIM), BF16),
                   jax.ShapeDtypeStruct((t, HG_DIM), sum_dtype), jax.ShapeDtypeStruct((1, HG_DIM), F32)],
        scratch_shapes=[pltpu.VMEM((2, LANE, LANE), F32)], after=after)


def _mix_out(att, of, ob, p, gout, w_out, g_post, x, after=(), tm=256):
    t, d = x.shape
    tm = min(tm, t)

    def body(att_ref, of_ref, ob_ref, hg0_ref, hg1_ref, gout_ref, w_ref, g_ref, x_ref, cat_ref, y_ref, x1_ref):
        cat_ref[:, :ATT_Q_DIM] = att_ref[...]
        for h in range(HG_HEADS):
            sl = slice(h * LANE, (h + 1) * LANE)
            hg_ref = hg0_ref if h < 2 else hg1_ref
            hg = hg_ref[:, (h % 2) * LANE:(h % 2 + 1) * LANE]
            nrm = _rms_fwd(of_ref[:, sl] + ob_ref[:, sl], gout_ref[...])
            cat_ref[:, ATT_Q_DIM + h * LANE:ATT_Q_DIM + (h + 1) * LANE] = _bf(nrm * hg * _sigmoid(hg))
        y = _dot(cat_ref[...], w_ref[...], 1, 0)
        y_ref[...] = y
        x1_ref[...] = x_ref[...] + _rms_fwd(y, g_ref[...])

    row = lambda width: pl.BlockSpec((tm, width), lambda i: (i, 0))
    return _call(
        body, (att, of, ob, p, p, gout, w_out, g_post, x), name="mix_out", grid=(t // tm,),
        in_specs=[row(ATT_Q_DIM), row(HG_DIM), row(HG_DIM), pl.BlockSpec((tm, 2 * LANE), lambda i: (i, 11)),
                  pl.BlockSpec((tm, 2 * LANE), lambda i: (i, 12)), _full((1, LANE)), _full((d, d)), _full((1, d)), row(d)],
        out_specs=[row(d), row(d), row(d)],
        out_shape=[jax.ShapeDtypeStruct((t, d), BF16), jax.ShapeDtypeStruct((t, d), F32), jax.ShapeDtypeStruct((t, d), F32)],
        after=after)


def _rec_bwd(dcat, of, ob, p, gout, tm=256):
    t = of.shape[0]
    tm = min(tm, t)

    def body(dc_ref, of_ref, ob_ref, hg0_ref, hg1_ref, gout_ref, do_ref, dhg_ref, dgo_ref):
        @pl.when(pl.program_id(0) == 0)
        def _():
            dgo_ref[...] = jnp.zeros_like(dgo_ref)

        dgo = jnp.zeros((1, LANE), F32)
        for h in range(HG_HEADS):
            sl = slice(h * LANE, (h + 1) * LANE)
            hg_ref = hg0_ref if h < 2 else hg1_ref
            hg = hg_ref[:, (h % 2) * LANE:(h % 2 + 1) * LANE]
            o = of_ref[:, sl] + ob_ref[:, sl]
            sg = _sigmoid(hg)
            drec = dc_ref[:, sl]
            dhg_ref[:, sl] = _bf(drec * _rms_fwd(o, gout_ref[...]) * (sg * (1.0 + hg * (1.0 - sg))))
            do, dg = _rms_bwd(drec * hg * sg, o, gout_ref[...])
            do_ref[:, sl] = do
            dgo = dgo + dg
        dgo_ref[...] += dgo

    row = lambda width: pl.BlockSpec((tm, width), lambda i: (i, 0))
    return pl.pallas_call(
        body, name="rec_bwd", grid=(t // tm,),
        in_specs=[pl.BlockSpec((tm, HG_DIM), lambda i: (i, 1)), row(HG_DIM), row(HG_DIM),
                  pl.BlockSpec((tm, 2 * LANE), lambda i: (i, 11)), pl.BlockSpec((tm, 2 * LANE), lambda i: (i, 12)),
                  _full((1, LANE))],
        out_specs=[row(HG_DIM), row(HG_DIM), _full((1, LANE))],
        out_shape=[jax.ShapeDtypeStruct((t, HG_DIM), F32), jax.ShapeDtypeStruct((t, HG_DIM), BF16),
                   jax.ShapeDtypeStruct((1, LANE), F32)],
        compiler_params=_params(),
    )(dcat, of, ob, p, p, gout)


def _xattn_fwd(q, kv, tq=512):
    t, d = q.shape
    tq = min(tq, t)
    nm = kv.shape[0]

    def body(q_ref, kv_ref, o_ref):
        for h in range(X_HEADS):
            sl = slice(h * X_HEAD_DIM, (h + 1) * X_HEAD_DIM)
            s = _dot(_bf(q_ref[:, sl]), _bf(kv_ref[:, sl]), 1, 1) * (X_HEAD_DIM ** -0.5)
            e = jnp.exp(s - jnp.max(s, axis=-1, keepdims=True))
            pr = e * (1.0 / jnp.sum(e, axis=-1, keepdims=True))
            o_ref[:, sl] = _bf(_dot(_bf(pr), _bf(kv_ref[:, d + h * X_HEAD_DIM:d + (h + 1) * X_HEAD_DIM]), 1, 0))

    return pl.pallas_call(
        body, name="xattn_fwd", grid=(t // tq,),
        in_specs=[pl.BlockSpec((tq, d), lambda i: (i, 0)), _full((nm, 2 * d))],
        out_specs=pl.BlockSpec((tq, d), lambda i: (i, 0)),
        out_shape=jax.ShapeDtypeStruct((t, d), BF16),
        compiler_params=_params(),
    )(q, kv)


def _xattn_bwd(q, kv, do, tq=512):
    t, d = q.shape
    tq = min(tq, t)
    nm = kv.shape[0]

    def body(q_ref, kv_ref, do_ref, dq_ref, dkv_ref):
        @pl.when(pl.program_id(0) == 0)
        def _():
            dkv_ref[...] = jnp.zeros_like(dkv_ref)

        for h in range(X_HEADS):
            sl = slice(h * X_HEAD_DIM, (h + 1) * X_HEAD_DIM)
            slv = slice(d + h * X_HEAD_DIM, d + (h + 1) * X_HEAD_DIM)
            qb, kb, vb, dob = _bf(q_ref[:, sl]), _bf(kv_ref[:, sl]), _bf(kv_ref[:, slv]), _bf(do_ref[:, sl])
            s = _dot(qb, kb, 1, 1) * (X_HEAD_DIM ** -0.5)
            e = jnp.exp(s - jnp.max(s, axis=-1, keepdims=True))
            pr = e * (1.0 / jnp.sum(e, axis=-1, keepdims=True))
            dpr = _dot(dob, vb, 1, 1)
            ds = _bf(pr * (dpr - jnp.sum(pr * dpr, axis=-1, keepdims=True)) * (X_HEAD_DIM ** -0.5))
            dq_ref[:, sl] = _bf(_dot(ds, kb, 1, 0))
            dkv_ref[:, sl] += _dot(ds, qb, 0, 0)
            dkv_ref[:, slv] += _dot(_bf(pr), dob, 0, 0)

    return pl.pallas_call(
        body, name="xattn_bwd", grid=(t // tq,),
        in_specs=[pl.BlockSpec((tq, d), lambda i: (i, 0)), _full((nm, 2 * d)), pl.BlockSpec((tq, d), lambda i: (i, 0))],
        out_specs=[pl.BlockSpec((tq, d), lambda i: (i, 0)), _full((nm, 2 * d))],
        out_shape=[jax.ShapeDtypeStruct((t, d), BF16), jax.ShapeDtypeStruct((nm, 2 * d), F32)],
        compiler_params=_params(),
    )(q, kv, do)


CONV_TN = 256


def _shift_rows(u, row, t, delta):
    if delta < 0:
        return jnp.where(row == 0, 0.0, pltpu.roll(u, 1, 0))
    return jnp.where(row == t - 1, 0.0, pltpu.roll(u, t - 1, 0))


def _conv_gate_fwd(u, conv_w, conv_b):
    t = u.shape[0]
    nt = D_FF // CONV_TN

    def body(ug_ref, uv_ref, wg_ref, wv_ref, bg_ref, bv_ref, a_ref):
        row = lax.broadcasted_iota(jnp.int32, (t, CONV_TN), 0)

        def conv(u_ref, w_ref, b_ref):
            uu = u_ref[...]
            return (b_ref[...] + _shift_rows(uu, row, t, -1) * w_ref[0:1, :] + uu * w_ref[1:2, :]
                    + _shift_rows(uu, row, t, 1) * w_ref[2:3, :])

        gate = conv(ug_ref, wg_ref, bg_ref)
        a_ref[...] = _bf(gate * _sigmoid(gate) * conv(uv_ref, wv_ref, bv_ref))

    col = lambda rows, off: pl.BlockSpec((rows, CONV_TN), lambda j: (0, j + off))
    return pl.pallas_call(
        body, name="conv_gate_fwd", grid=(nt,),
        in_specs=[col(t, 0), col(t, nt), col(3, 0), col(3, nt), col(1, 0), col(1, nt)],
        out_specs=col(t, 0),
        out_shape=jax.ShapeDtypeStruct((t, D_FF), BF16),
        compiler_params=_params(),
    )(u, u, conv_w, conv_w, conv_b, conv_b)


def _conv_gate_bwd(u, conv_w, conv_b, da, after=()):
    t = u.shape[0]
    nt = D_FF // CONV_TN

    def body(ug_ref, uv_ref, wg_ref, wv_ref, bg_ref, bv_ref, da_ref, dug_ref, dwg_ref, dbg_ref, duv_ref, dwv_ref, dbv_ref):
        row = lax.broadcasted_iota(jnp.int32, (t, CONV_TN), 0)
        ug, uv = ug_ref[...], uv_ref[...]
        ug_m, ug_p = _shift_rows(ug, row, t, -1), _shift_rows(ug, row, t, 1)
        uv_m, uv_p = _shift_rows(uv, row, t, -1), _shift_rows(uv, row, t, 1)
        gate = bg_ref[...] + ug_m * wg_ref[0:1, :] + ug * wg_ref[1:2, :] + ug_p * wg_ref[2:3, :]
        val = bv_ref[...] + uv_m * wv_ref[0:1, :] + uv * wv_ref[1:2, :] + uv_p * wv_ref[2:3, :]
        sg = _sigmoid(gate)
        da_ = da_ref[...].astype(F32)

        def emit(dacc, um, uu, up, w_ref, du_ref, dw_ref, db_ref):
            du_ref[...] = _bf(_shift_rows(dacc, row, t, 1) * w_ref[0:1, :] + dacc * w_ref[1:2, :]
                              + _shift_rows(dacc, row, t, -1) * w_ref[2:3, :])
            dw_ref[0:1, :] = jnp.sum(dacc * um, axis=0, keepdims=True)
            dw_ref[1:2, :] = jnp.sum(dacc * uu, axis=0, keepdims=True)
            dw_ref[2:3, :] = jnp.sum(dacc * up, axis=0, keepdims=True)
            db_ref[...] = jnp.sum(dacc, axis=0, keepdims=True)

        emit(da_ * val * (sg * (1.0 + gate * (1.0 - sg))), ug_m, ug, ug_p, wg_ref, dug_ref, dwg_ref, dbg_ref)
        emit(da_ * gate * sg, uv_m, uv, uv_p, wv_ref, duv_ref, dwv_ref, dbv_ref)

    col = lambda rows, off: pl.BlockSpec((rows, CONV_TN), lambda j: (0, j + off))
    half_shapes = [jax.ShapeDtypeStruct((t, D_FF), BF16), jax.ShapeDtypeStruct((3, D_FF), F32),
                   jax.ShapeDtypeStruct((1, D_FF), F32)]
    outs = _call(
        body, (u, u, conv_w, conv_w, conv_b, conv_b, da), name="conv_gate_bwd", grid=(nt,),
        in_specs=[col(t, 0), col(t, nt), col(3, 0), col(3, nt), col(1, 0), col(1, nt), col(t, 0)],
        out_specs=[col(t, 0), col(3, 0), col(1, 0)] * 2, out_shape=half_shapes * 2, after=after)
    return outs[:3], outs[3:]


def _row_tile(r, cap):
    best = r
    for cand in range(16, cap + 1, 16):
        if r % cand == 0:
            best = cand
    return best


def _sum_parts(own, got, me, name, tr=256):
    _, r, c = got.shape
    tr = _row_tile(r, tr)

    def body(me_ref, own_ref, got_ref, o_ref):
        mine = own_ref[...].astype(F32)
        acc = None
        for i in range(N_DEV):
            term = jnp.where(me_ref[0] == i, mine, got_ref[i].astype(F32))
            acc = term if acc is None else acc + term
        o_ref[...] = acc

    return pl.pallas_call(
        body, name=name,
        grid_spec=pltpu.PrefetchScalarGridSpec(
            num_scalar_prefetch=1, grid=(r // tr,),
            in_specs=[pl.BlockSpec((None, tr, c), lambda i, me_ref: (me_ref[0], i, 0)),
                      pl.BlockSpec((N_DEV, tr, c), lambda i, me_ref: (0, i, 0))],
            out_specs=pl.BlockSpec((tr, c), lambda i, me_ref: (i, 0))),
        out_shape=jax.ShapeDtypeStruct((r, c), F32),
        compiler_params=_params(),
    )(me, own, got)


def _adamw_math(w_, g_, m_, v_):
    m_ = ADAM_B1 * m_ + (1.0 - ADAM_B1) * g_
    v_ = ADAM_B2 * v_ + (1.0 - ADAM_B2) * (g_ * g_)
    m_hat = m_ / (1.0 - ADAM_B1 ** ADAM_STEP)
    v_hat = v_ / (1.0 - ADAM_B2 ** ADAM_STEP)
    return -ADAM_LR * (m_hat / (jnp.sqrt(v_hat) + ADAM_EPS) + ADAM_WD * w_), m_, v_


def _sum_adamw(own, got, me, w, m, v, name, tr=256):
    _, r, c = got.shape
    tr = _row_tile(r, tr)

    def body(me_ref, own_ref, got_ref, w_ref, m_ref, v_ref, g_ref, d_ref, nm_ref, nv_ref):
        mine = own_ref[...].astype(F32)
        acc = None
        for i in range(N_DEV):
            term = jnp.where(me_ref[0] == i, mine, got_ref[i].astype(F32))
            acc = term if acc is None else acc + term
        g_ref[...] = acc
        d_ref[...], nm_ref[...], nv_ref[...] = _adamw_math(w_ref[...], acc, m_ref[...], v_ref[...])

    blk = pl.BlockSpec((tr, c), lambda i, me_ref: (i, 0))
    return pl.pallas_call(
        body, name=name,
        grid_spec=pltpu.PrefetchScalarGridSpec(
            num_scalar_prefetch=1, grid=(r // tr,),
            in_specs=[pl.BlockSpec((None, tr, c), lambda i, me_ref: (me_ref[0], i, 0)),
                      pl.BlockSpec((N_DEV, tr, c), lambda i, me_ref: (0, i, 0)), blk, blk, blk],
            out_specs=[blk] * 4),
        out_shape=[jax.ShapeDtypeStruct((r, c), F32)] * 4,
        compiler_params=_params(),
    )(me, own, got, w, m, v)


def _adamw(w, g, m, v, name, tr=256):
    r, c = w.shape
    tr = _row_tile(r, tr)

    def body(w_ref, g_ref, m_ref, v_ref, d_ref, nm_ref, nv_ref):
        d_ref[...], nm_ref[...], nv_ref[...] = _adamw_math(w_ref[...], g_ref[...], m_ref[...], v_ref[...])

    blk = pl.BlockSpec((tr, c), lambda i: (i, 0))
    return pl.pallas_call(
        body, name=name, grid=(r // tr,),
        in_specs=[blk] * 4, out_specs=[blk] * 3,
        out_shape=[jax.ShapeDtypeStruct((r, c), F32)] * 3,
        compiler_params=_params(),
    )(w, g, m, v)


def _mesh_pos():
    return lax.axis_index("x"), lax.axis_index("y"), lax.axis_index("c")


def _dev_index(px, py, pc):
    return 4 * px + 2 * py + pc


class _Gather:
    def __init__(self, arrs):
        self.arrs = list(arrs)
        n = len(self.arrs)
        self.out_shape = [jax.ShapeDtypeStruct((N_DEV,) + a.shape, a.dtype) for a in self.arrs]
        self.sems = [pltpu.SemaphoreType.DMA((7, n)), pltpu.SemaphoreType.DMA((7, n)), pltpu.SemaphoreType.DMA((n,))]

    def _ctx(self, ins, outs, sems):
        send_sems, recv_sems, local_sems = sems
        x, y, c = _mesh_pos()
        chips = [(1 - x, y), (x, 1 - y), (1 - x, 1 - y)]

        def copy(k, a, block, to, src=None):
            dst = outs[a].at[_dev_index(*block)]
            return pltpu.make_async_remote_copy(
                src_ref=dst if src is None else src, dst_ref=dst, send_sem=send_sems.at[k, a], recv_sem=recv_sems.at[k, a],
                device_id=to, device_id_type=MESH_T)

        n = len(ins)
        me, sibling = (x, y, c), (x, y, 1 - c)
        mine = [pltpu.make_async_copy(ins[a], outs[a].at[_dev_index(*me)], local_sems.at[a]) for a in range(n)]
        first = [copy(0, a, me, sibling, src=ins[a]) for a in range(n)]
        first += [copy(1 + j, a, me, (*chip, c), src=ins[a]) for j, chip in enumerate(chips) for a in range(n)]
        passed = [copy(4 + j, a, (*chip, c), sibling) for j, chip in enumerate(chips) for a in range(n)]
        return n, c, me, sibling, chips, copy, mine, first, passed

    def start(self, ins, outs, sems):
        _, _, _, _, _, _, mine, first, _ = self._ctx(ins, outs, sems)
        for cp in mine + first:
            cp.start()

    def forward(self, ins, outs, sems):
        n, c, me, _, chips, copy, _, _, passed = self._ctx(ins, outs, sems)
        for j, chip in enumerate(chips):
            for a in range(n):
                copy(1 + j, a, (*chip, c), me).wait_recv()
                passed[j * n + a].start()

    def finish(self, ins, outs, sems):
        n, c, me, sibling, chips, copy, mine, first, passed = self._ctx(ins, outs, sems)
        for a in range(n):
            copy(0, a, sibling, me).wait_recv()
        for j, chip in enumerate(chips):
            for a in range(n):
                copy(4 + j, a, (*chip, 1 - c), me).wait_recv()
        for cp in first + passed:
            cp.wait_send()
        for cp in mine:
            cp.wait()


def _comm_alone(comm, name):
    n = len(comm.arrs)

    def body(*refs):
        ins, outs, sems = refs[:n], refs[n:2 * n], refs[2 * n:]
        comm.start(ins, outs, sems)
        if comm.forward is not None:
            comm.forward(ins, outs, sems)
        comm.finish(ins, outs, sems)

    any_spec = pl.BlockSpec(memory_space=pl.ANY)
    return pl.pallas_call(body, name=name, in_specs=[any_spec] * n, out_specs=[any_spec] * n, out_shape=comm.out_shape,
                          scratch_shapes=comm.sems)(*comm.arrs)


def _peers(x, y, c):
    out = []
    for k in range(1, N_DEV):
        pos = (1 - x if k & 4 else x, 1 - y if k & 2 else y, 1 - c if k & 1 else c)
        out.append((k, pos, _dev_index(*pos)))
    return out


def _exchange_begin(arrs, tag):
    n = len(arrs)
    lands = [lax.empty(a.shape, a.dtype) for a in arrs]

    def start_body(*refs):
        ins, land = refs[:n], refs[n:2 * n]
        send_sems, recv_sems, token = refs[2 * n], refs[2 * n + 1], refs[-1]
        x, y, c = _mesh_pos()
        me_i = _dev_index(x, y, c)
        for k, pos, peer_i in _peers(x, y, c):
            for a in range(n):
                pltpu.make_async_remote_copy(
                    src_ref=ins[a].at[peer_i], dst_ref=land[a].at[me_i], send_sem=send_sems.at[(k - 1) * n + a],
                    recv_sem=recv_sems.at[(k - 1) * n + a], device_id=pos, device_id_type=MESH_T).start()
        token[...] = jnp.zeros_like(token)

    hbm = pl.BlockSpec(memory_space=pltpu.HBM)
    sem = pl.BlockSpec(memory_space=pltpu.SEMAPHORE)
    thru = [pltpu.HBM(a.shape, a.dtype) for a in arrs]
    outs = pl.pallas_call(
        start_body, name="exchange_start_" + tag,
        out_shape=[pltpu.SemaphoreType.DMA((7 * n,)), pltpu.SemaphoreType.DMA((7 * n,))] + thru + thru
        + [jax.ShapeDtypeStruct((8, LANE), F32)],
        in_specs=[hbm] * (2 * n), out_specs=[sem, sem] + [hbm] * (2 * n) + [pl.BlockSpec(memory_space=pltpu.VMEM)],
        input_output_aliases={i: 2 + i for i in range(2 * n)},
        compiler_params=pltpu.CompilerParams(has_side_effects=pltpu.SideEffectType.DATAFLOW_SIDE_EFFECTING),
    )(*[pltpu.with_memory_space_constraint(a, pltpu.HBM) for a in list(arrs) + list(lands)])
    return (tag, n, outs[0], outs[1], outs[2:2 + n], outs[2 + n:2 + 2 * n]), outs[-1]


def _exchange_end(handle, after):
    tag, n, send_sems, recv_sems, srcs, lands = handle

    def body(*refs):
        ins, land = refs[:n], refs[n:2 * n]
        send_sems_, recv_sems_ = refs[2 * n], refs[2 * n + 1]
        x, y, c = _mesh_pos()
        me_i = _dev_index(x, y, c)
        for k, pos, peer_i in _peers(x, y, c):
            for a in range(n):
                cp = pltpu.make_async_remote_copy(
                    src_ref=ins[a].at[peer_i], dst_ref=land[a].at[me_i], send_sem=send_sems_.at[(k - 1) * n + a],
                    recv_sem=recv_sems_.at[(k - 1) * n + a], device_id=pos, device_id_type=MESH_T)
                cp.wait_send()
                cp.wait_recv()

    hbm = pl.BlockSpec(memory_space=pltpu.HBM)
    sem = pl.BlockSpec(memory_space=pltpu.SEMAPHORE)
    outs = pl.pallas_call(
        body, name="exchange_end_" + tag, out_shape=[pltpu.HBM(a.shape, a.dtype) for a in list(srcs) + list(lands)],
        in_specs=[hbm] * (2 * n) + [sem, sem, pl.BlockSpec(memory_space=pl.ANY)], out_specs=[hbm] * (2 * n),
        input_output_aliases={i: i for i in range(2 * n)},
        compiler_params=pltpu.CompilerParams(has_side_effects=pltpu.SideEffectType.DATAFLOW_SIDE_EFFECTING),
    )(*srcs, *lands, send_sems, recv_sems, after)
    return list(zip(outs[:n], outs[n:]))


def _gather_begin(arrs, tag):
    n = len(arrs)
    me = _dev_index(*_mesh_pos())
    lands = [lax.dynamic_update_slice(lax.empty((N_DEV,) + a.shape, a.dtype), a[None], (me,) + (0,) * a.ndim) for a in arrs]

    def start_body(*refs):
        ins, land = refs[:n], refs[n:2 * n]
        send_sems, recv_sems, token = refs[2 * n], refs[2 * n + 1], refs[-1]
        x, y, c = _mesh_pos()
        me_i = _dev_index(x, y, c)
        for a in range(n):
            for k, pos, _ in _peers(x, y, c):
                pltpu.make_async_remote_copy(
                    src_ref=ins[a], dst_ref=land[a].at[me_i], send_sem=send_sems.at[(k - 1) * n + a],
                    recv_sem=recv_sems.at[(k - 1) * n + a], device_id=pos, device_id_type=MESH_T).start()
        token[...] = jnp.zeros_like(token)

    hbm = pl.BlockSpec(memory_space=pltpu.HBM)
    sem = pl.BlockSpec(memory_space=pltpu.SEMAPHORE)
    thru = [pltpu.HBM(a.shape, a.dtype) for a in list(arrs) + lands]
    outs = pl.pallas_call(
        start_body, name="gather_start_" + tag,
        out_shape=[pltpu.SemaphoreType.DMA((7 * n,)), pltpu.SemaphoreType.DMA((7 * n,))] + thru
        + [jax.ShapeDtypeStruct((8, LANE), F32)],
        in_specs=[hbm] * (2 * n), out_specs=[sem, sem] + [hbm] * (2 * n) + [pl.BlockSpec(memory_space=pltpu.VMEM)],
        input_output_aliases={i: 2 + i for i in range(2 * n)},
        compiler_params=pltpu.CompilerParams(has_side_effects=pltpu.SideEffectType.DATAFLOW_SIDE_EFFECTING),
    )(*[pltpu.with_memory_space_constraint(a, pltpu.HBM) for a in list(arrs) + lands])
    return (tag, n, outs[0], outs[1], outs[2:2 + n], outs[2 + n:2 + 2 * n]), outs[-1]


def _gather_end(handle, which, after):
    tag, n, send_sems, recv_sems, srcs, lands = handle
    m = len(which)

    def body(*refs):
        ins, land = refs[:m], refs[m:2 * m]
        send_sems_, recv_sems_ = refs[2 * m], refs[2 * m + 1]
        x, y, c = _mesh_pos()
        me_i = _dev_index(x, y, c)
        for j, a in enumerate(which):
            for k, pos, _ in _peers(x, y, c):
                cp = pltpu.make_async_remote_copy(
                    src_ref=ins[j], dst_ref=land[j].at[me_i], send_sem=send_sems_.at[(k - 1) * n + a],
                    recv_sem=recv_sems_.at[(k - 1) * n + a], device_id=pos, device_id_type=MESH_T)
                cp.wait_send()
                cp.wait_recv()

    hbm = pl.BlockSpec(memory_space=pltpu.HBM)
    sem = pl.BlockSpec(memory_space=pltpu.SEMAPHORE)
    ops = [srcs[a] for a in which] + [lands[a] for a in which]
    outs = pl.pallas_call(
        body, name="gather_end_%s_%s" % (tag, "_".join(str(a) for a in which)),
        out_shape=[pltpu.HBM(a.shape, a.dtype) for a in ops],
        in_specs=[hbm] * (2 * m) + [sem, sem, pl.BlockSpec(memory_space=pl.ANY)], out_specs=[hbm] * (2 * m),
        input_output_aliases={i: i for i in range(2 * m)},
        compiler_params=pltpu.CompilerParams(has_side_effects=pltpu.SideEffectType.DATAFLOW_SIDE_EFFECTING),
    )(*ops, send_sems, recv_sems, after)
    return list(outs[m:])


_SPLIT = dict(has_side_effects=pltpu.SideEffectType.DATAFLOW_SIDE_EFFECTING)


def _chips(x, y):
    return [(1 - x, y), (x, 1 - y), (1 - x, 1 - y)]


def _gather2_begin(arrs, tag):
    n = len(arrs)
    me = _dev_index(*_mesh_pos())
    lands = [lax.dynamic_update_slice(lax.empty((N_DEV,) + a.shape, a.dtype), a[None], (me,) + (0,) * a.ndim) for a in arrs]

    def body(*refs):
        ins, land = refs[:n], refs[n:2 * n]
        send1, recv1, token = refs[2 * n], refs[2 * n + 1], refs[-1]
        x, y, c = _mesh_pos()
        me_i = _dev_index(x, y, c)
        targets = [(x, y, 1 - c)] + [(*chip, c) for chip in _chips(x, y)]
        for a in range(n):
            for k, pos in enumerate(targets):
                pltpu.make_async_remote_copy(
                    src_ref=ins[a], dst_ref=land[a].at[me_i], send_sem=send1.at[k * n + a], recv_sem=recv1.at[k * n + a],
                    device_id=pos, device_id_type=MESH_T).start()
        token[...] = jnp.zeros_like(token)

    hbm = pl.BlockSpec(memory_space=pltpu.HBM)
    sem = pl.BlockSpec(memory_space=pltpu.SEMAPHORE)
    thru = [pltpu.HBM(a.shape, a.dtype) for a in list(arrs) + lands]
    outs = pl.pallas_call(
        body, name="gather_start_" + tag,
        out_shape=[pltpu.SemaphoreType.DMA((4 * n,)), pltpu.SemaphoreType.DMA((4 * n,))] + thru
        + [jax.ShapeDtypeStruct((8, LANE), F32)],
        in_specs=[hbm] * (2 * n), out_specs=[sem, sem] + [hbm] * (2 * n) + [pl.BlockSpec(memory_space=pltpu.VMEM)],
        input_output_aliases={i: 2 + i for i in range(2 * n)}, compiler_params=pltpu.CompilerParams(**_SPLIT),
    )(*[pltpu.with_memory_space_constraint(a, pltpu.HBM) for a in list(arrs) + lands])
    return dict(tag=tag, n=n, send1=outs[0], recv1=outs[1], srcs=list(outs[2:2 + n]), lands=list(outs[2 + n:2 + 2 * n])), outs[-1]


def _gather2_forward(handle, which, after, part):
    n, m = handle["n"], len(which)

    def body(*refs):
        land, recv1 = refs[:m], refs[m]
        send2, recv2, token = refs[m + 2], refs[m + 3], refs[-1]
        token[...] = jnp.zeros_like(token)
        x, y, c = _mesh_pos()
        for j, a in enumerate(which):
            for k, chip in enumerate(_chips(x, y)):
                blk = land[j].at[_dev_index(*chip, c)]
                copy = pltpu.make_async_remote_copy(
                    src_ref=blk, dst_ref=blk, send_sem=send2.at[k * m + j], recv_sem=recv2.at[k * m + j],
                    device_id=(x, y, 1 - c), device_id_type=MESH_T)
                pltpu.make_async_remote_copy(
                    src_ref=blk, dst_ref=blk, send_sem=send2.at[k * m + j], recv_sem=recv1.at[(1 + k) * n + a],
                    device_id=(*chip, c), device_id_type=MESH_T).wait_recv()
                copy.start()

    hbm = pl.BlockSpec(memory_space=pltpu.HBM)
    sem = pl.BlockSpec(memory_space=pltpu.SEMAPHORE)
    lands = [handle["lands"][a] for a in which]
    outs = pl.pallas_call(
        body, name="gather_forward_%s_%s" % (handle["tag"], part),
        out_shape=[pltpu.SemaphoreType.DMA((3 * m,)), pltpu.SemaphoreType.DMA((3 * m,))] + [pltpu.HBM(a.shape, a.dtype) for a in lands]
        + [jax.ShapeDtypeStruct((8, LANE), F32)],
        in_specs=[hbm] * m + [sem, pl.BlockSpec(memory_space=pl.ANY)],
        out_specs=[sem, sem] + [hbm] * m + [pl.BlockSpec(memory_space=pltpu.VMEM)],
        input_output_aliases={i: 2 + i for i in range(m)}, compiler_params=pltpu.CompilerParams(**_SPLIT),
    )(*lands, handle["recv1"], after)
    for j, a in enumerate(which):
        handle["lands"][a] = outs[2 + j]
        handle[("leg2", a)] = (outs[0], outs[1], j, m)
    return outs[-1]


def _gather2_end(handle, a, after):
    n = handle["n"]
    send2, recv2, j, m = handle[("leg2", a)]

    def body(src, land, send1, recv1, send2_, recv2_, after_ref, src_out, land_out):
        x, y, c = _mesh_pos()
        me_i = _dev_index(x, y, c)
        sibling = (x, y, 1 - c)
        for k, pos in enumerate([sibling] + [(*chip, c) for chip in _chips(x, y)]):
            first = pltpu.make_async_remote_copy(
                src_ref=src, dst_ref=land.at[me_i], send_sem=send1.at[k * n + a], recv_sem=recv1.at[k * n + a],
                device_id=pos, device_id_type=MESH_T)
            first.wait_send()
            if k == 0:
                first.wait_recv()
        for k in range(3):
            second = pltpu.make_async_remote_copy(
                src_ref=land.at[me_i], dst_ref=land.at[me_i], send_sem=send2_.at[k * m + j], recv_sem=recv2_.at[k * m + j],
                device_id=sibling, device_id_type=MESH_T)
            second.wait_send()
            second.wait_recv()

    hbm = pl.BlockSpec(memory_space=pltpu.HBM)
    sem = pl.BlockSpec(memory_space=pltpu.SEMAPHORE)
    src, land = handle["srcs"][a], handle["lands"][a]
    outs = pl.pallas_call(
        body, name="gather_end_%s_%d" % (handle["tag"], a),
        out_shape=[pltpu.HBM(src.shape, src.dtype), pltpu.HBM(land.shape, land.dtype)],
        in_specs=[hbm, hbm, sem, sem, sem, sem, pl.BlockSpec(memory_space=pl.ANY)], out_specs=[hbm, hbm],
        input_output_aliases={0: 0, 1: 1}, compiler_params=pltpu.CompilerParams(**_SPLIT),
    )(src, land, handle["send1"], handle["recv1"], send2, recv2, after)
    return outs[1]


def _call(body, operands, *, name, grid, in_specs, out_specs, out_shape, scratch_shapes=(), after=()):
    ni, nd = len(in_specs), len(after)

    def ordered(*refs):
        body(*refs[:ni], *refs[ni + nd:])

    outs = pl.pallas_call(
        ordered, name=name, grid=grid, in_specs=list(in_specs) + [pl.BlockSpec(memory_space=pl.ANY)] * nd,
        out_specs=out_specs, out_shape=out_shape, scratch_shapes=list(scratch_shapes), compiler_params=_params(),
    )(*operands, *after)
    return list(outs)


def _rows(a):
    return a.reshape(-1, a.shape[-1])


def _slots(a):
    return a.reshape(N_DEV, -1, a.shape[-1])


def _local_step(x, mem, tgt, w, wire):
    tables = _rope_tables(x.shape[0])
    bd = _head_sum_matrix()
    two = lambda g: jnp.tile(g, (1, 2))
    gq2, gk2 = two(w["q_norm_g"]), two(w["k_norm_g"])

    w_in_t, hg_lb, conv_w = _comm_alone(_Gather([wire["w_in"], w["hg_lb"].reshape(4, -1), w["conv_w"][0]]), "gather_w_in")
    w_in_t = _rows(w_in_t)
    hg_lb = jnp.transpose(hg_lb.reshape(N_DEV, 2, 2, -1), (1, 2, 0, 3)).reshape(2, 2, HG_DIM)
    conv_w = jnp.transpose(conv_w, (1, 0, 2)).reshape(3, 2 * D_FF)
    lb_a0, lb_a1 = hg_lb[:, 0, :], hg_lb[:, 1, :]

    order = ("w_out", "w_xq", "w_xkv", "w_xo", "w_up", "w_down")
    w_in_t, *later = lax.optimization_barrier((w_in_t, *[wire[n] for n in order]))
    fetch, started = _gather2_begin(later, "w")
    take = lambda n, after: _rows(_gather2_end(fetch, order.index(n), after))

    h1, p = _norm_mm(x, w["pre_mix_g"], w_in_t, True, N_IN, "in_proj", after=[started])
    qr, kr = _qk_prep(p, gq2, gk2, tables, bd)
    att, att32, lse = _attn_fwd(qr, kr, p)
    passed = _gather2_forward(fetch, [0, 1, 2, 3], att, "a")
    lb = _lower_bounds(lb_a0, lb_a1)
    of, s_f = _hgrn_fwd(p, lb[0:1], False, after=[passed])
    ob, s_b = _hgrn_fwd(p, lb[1:2], True, after=[passed, of])
    passed = _gather2_forward(fetch, [4, 5], ob, "b")
    w_out = take("w_out", ob)
    cat, mixed, x1 = _mix_out(att, of, ob, p, w["hg_out_norm_g"], w_out, w["post_mix_g"], x, after=[passed])

    w_xq = take("w_xq", x1)
    h2, q2 = _norm_mm(x1, w["pre_x_g"], w_xq, False, 1024, "xq_proj")
    w_xkv_t = take("w_xkv", q2)
    mn, kv = _norm_mm(mem, w["mem_norm_g"], w_xkv_t, True, 2 * D_MODEL, "xkv_proj")
    o2 = _xattn_fwd(q2, kv)
    w_xo = take("w_xo", o2)
    y2, x2 = _mm_postnorm_res(o2, w_xo, w["post_x_g"], x1, "xo_proj")

    w_up_t = take("w_up", x2)
    h3, u = _norm_mm(x2, w["pre_ffn_g"], w_up_t, True, 2 * D_FF, "up_proj", tm=256)
    a = _conv_gate_fwd(u, conv_w, w["conv_b"])
    w_down = take("w_down", a)
    y3, dx3, loss = _mm_postnorm_res_loss(a, w_down, w["post_ffn_g"], x2, tgt, "down_proj")

    g, pending = {}, {}
    dy3, da, g["post_ffn_g"] = _postnorm_bwd_mm(dx3, y3, w["post_ffn_g"], w_down, "down_bwd", BF16)
    pending["down"], started = _exchange_begin([_slots(_dw(a, dy3, "dw_down", tka=D_FF // 2))], "down")
    (du_g, dcw_g, dcb_g), (du_v, dcw_v, dcb_v) = _conv_gate_bwd(u, conv_w, w["conv_b"], da, after=[started])
    g["conv_w"] = jnp.concatenate([dcw_g, dcw_v], axis=1)
    g["conv_b"] = jnp.concatenate([dcb_g, dcb_v], axis=1)
    pending["up"], started = _exchange_begin([_slots(_dw([du_g, du_v], h3, "dw_up", tka=D_FF // 2))], "up")
    dx2, g["pre_ffn_g"] = _mm_prenorm_bwd([du_g, du_v], w_up_t, True, x2, w["pre_ffn_g"], dx3, "up_bwd", after=[started], tm=256)

    dy2, do2, g["post_x_g"] = _postnorm_bwd_mm(dx2, y2, w["post_x_g"], w_xo, "xo_bwd", BF16)
    dw_xo = _dw(o2, dy2, "dw_xo", tka=512)
    dq2, dkv = _xattn_bwd(q2, kv, do2)
    dw_xq = _dw(h2, dq2, "dw_xq", tka=512)
    dkvb = _bf(dkv)
    dw_xkv = _dw(dkvb, mn, "dw_xkv")
    _, g["mem_norm_g"] = _mm_prenorm_bwd(dkvb, w_xkv_t, True, mem, w["mem_norm_g"], jnp.zeros_like(mem), "xkv_bwd")
    dx1, g["pre_x_g"] = _mm_prenorm_bwd(dq2, w_xq, False, x1, w["pre_x_g"], dx2, "xq_bwd")

    dmixed, dcat, g["post_mix_g"] = _postnorm_bwd_mm(dx1, mixed, w["post_mix_g"], w_out, "out_bwd")
    dw_out = _dw(cat, dmixed, "dw_out", tka=512)
    pending["x"], started = _exchange_begin([_slots(dw_xo), _slots(dw_xq), _slots(dw_xkv), _slots(dw_out)], "x")
    do, dhg, g["hg_out_norm_g"] = _rec_bwd(dcat, of, ob, p, w["hg_out_norm_g"])
    dhq_f, dz_f, dhi_f, dlb_f = _hgrn_bwd(p, lb[0:1], do, s_f, False, after=[started])
    dhq, dz_b, dhi, dlb_b = _hgrn_bwd(p, lb[1:2], do, s_b, True, other=(dhq_f, dhi_f))
    d_a0, d_a1 = _lower_bounds_bwd(lb_a0, lb_a1, jnp.concatenate([dlb_f, dlb_b], axis=0))
    g["hg_lb"] = jnp.stack([d_a0, d_a1], axis=1)
    dqr, dkr, dv = _attn_bwd(qr, kr, p, dcat, att32, lse)
    dp_att, dgq, dgk = _qk_prep_bwd(p, dqr, dkr, dv, gq2, gk2, tables, bd)
    g["q_norm_g"], g["k_norm_g"] = dgq, dgk
    dp = [dp_att, dhq, dz_f, dz_b, dhi, dhg]

    early = [n for n in _SMALL if n != "pre_mix_g"]
    vals = [g[n].reshape(-1, g[n].shape[-1]) for n in early] + [jnp.pad(loss, ((0, 0), (0, LANE - 1)))]
    small = dict(early=early, shapes=[val.shape for val in vals])
    small["fetch"], started = _gather_begin([_pack_small(vals, "pack_small")], "small")
    pending["in"], started = _exchange_begin([_slots(_dw(dp, h1, "dw_in", after=[started]))], "in")
    dx, g["pre_mix_g"] = _mm_prenorm_bwd(dp, w_in_t, True, x, w["pre_mix_g"], dx1, "in_bwd", after=[started])
    small["fetch_last"], _ = _gather_begin([_pack_small([g["pre_mix_g"]], "pack_last")], "last")
    return dx, g, pending, small


_COL_SHARDED = ("w_in", "w_xkv", "w_up")
_ROW_SHARDED = ("w_out", "w_xq", "w_xo", "w_down")
_REPLICATED = ("pre_mix_g", "q_norm_g", "k_norm_g", "hg_out_norm_g", "post_mix_g", "pre_x_g", "mem_norm_g", "post_x_g",
               "pre_ffn_g", "conv_b", "post_ffn_g")
_WEIGHTS = ("pre_mix_g", "w_in", "q_norm_g", "k_norm_g", "hg_lb", "hg_out_norm_g", "w_out", "post_mix_g", "pre_x_g",
            "mem_norm_g", "w_xq", "w_xkv", "w_xo", "post_x_g", "pre_ffn_g", "w_up", "conv_w", "conv_b", "w_down",
            "post_ffn_g")
_SMALL = _REPLICATED + ("hg_lb", "conv_w")
_ADAM_TRANSPOSED = ("w_in", "w_up")
PACK_W = 1024


def _small_plan(shapes):
    plan, r = [], 0
    for vi, (rows, cols) in enumerate(shapes):
        for i in range(rows):
            for c0 in range(0, cols, PACK_W):
                plan.append((vi, i, c0, min(PACK_W, cols - c0), r))
                r += 1
    return plan, -(-r // 8) * 8


def _pack_small(vals, name):
    plan, nrows = _small_plan([val.shape for val in vals])

    def body(*refs):
        ins, out = refs[:-1], refs[-1]
        out[...] = jnp.zeros_like(out)
        for vi, i, c0, width, r in plan:
            out[r:r + 1, 0:width] = ins[vi][i:i + 1, c0:c0 + width]

    return pl.pallas_call(body, name=name, out_shape=jax.ShapeDtypeStruct((nrows, PACK_W), F32))(*vals)


def _sum_unpack_small(packs, shapes, name):
    plan, _ = _small_plan(shapes)

    def body(*refs):
        p_ref, outs = refs[0], refs[1:]
        acc = p_ref[0]
        for i in range(1, N_DEV):
            acc = acc + p_ref[i]
        for vi, i, c0, width, r in plan:
            outs[vi][i:i + 1, c0:c0 + width] = acc[r:r + 1, 0:width]

    return pl.pallas_call(body, name=name, out_shape=[jax.ShapeDtypeStruct(s, F32) for s in shapes])(packs)


def _adamw_many(ws, gs, ms, vs):
    n = len(ws)

    def body(*refs):
        w_refs, g_refs, m_refs, v_refs = (refs[k * n:(k + 1) * n] for k in range(4))
        d_refs, nm_refs, nv_refs = (refs[(4 + k) * n:(5 + k) * n] for k in range(3))
        for k in range(n):
            g_ = g_refs[k][...]
            m_ = ADAM_B1 * m_refs[k][...] + (1.0 - ADAM_B1) * g_
            v_ = ADAM_B2 * v_refs[k][...] + (1.0 - ADAM_B2) * (g_ * g_)
            m_hat = m_ / (1.0 - ADAM_B1 ** ADAM_STEP)
            v_hat = v_ / (1.0 - ADAM_B2 ** ADAM_STEP)
            d_refs[k][...] = -ADAM_LR * (m_hat / (jnp.sqrt(v_hat) + ADAM_EPS) + ADAM_WD * w_refs[k][...])
            nm_refs[k][...] = m_
            nv_refs[k][...] = v_

    shapes = [jax.ShapeDtypeStruct(a.shape, F32) for a in ws]
    outs = pl.pallas_call(body, name="adamw_small", out_shape=shapes * 3)(*ws, *gs, *ms, *vs)
    return outs[:n], outs[n:2 * n], outs[2 * n:]


def kernel(x, mem, pre_mix_g, w_in, q_norm_g, k_norm_g, hg_lb, hg_out_norm_g, w_out, post_mix_g, pre_x_g, mem_norm_g, w_xq, w_xkv, w_xo, post_x_g, pre_ffn_g, w_up, conv_w, conv_b, w_down, post_ffn_g, loss_target, m_pre_mix_g, m_w_in, m_q_norm_g, m_k_norm_g, m_hg_lb, m_hg_out_norm_g, m_w_out, m_post_mix_g, m_pre_x_g, m_mem_norm_g, m_w_xq, m_w_xkv, m_w_xo, m_post_x_g, m_pre_ffn_g, m_w_up, m_conv_w, m_conv_b, m_w_down, m_post_ffn_g, v_pre_mix_g, v_w_in, v_q_norm_g, v_k_norm_g, v_hg_lb, v_hg_out_norm_g, v_w_out, v_post_mix_g, v_pre_x_g, v_mem_norm_g, v_w_xq, v_w_xkv, v_w_xo, v_post_x_g, v_pre_ffn_g, v_w_up, v_conv_w, v_conv_b, v_w_down, v_post_ffn_g):
    args = dict(locals())
    w = {n: args[n] for n in _WEIGHTS}
    m = {n: args["m_" + n] for n in _WEIGHTS}
    v = {n: args["v_" + n] for n in _WEIGHTS}
    me = _dev_index(*_mesh_pos())

    wire = {n: _bf(w[n][0].T) for n in _COL_SHARDED}
    wire.update({n: _bf(w[n][0]) for n in _ROW_SHARDED})

    grad_x, g, pending, small = _local_step(x[0], mem[0], loss_target[0], w, wire)

    grads, delta, new_m, new_v = {}, {}, {}, {}

    me_arr = jnp.reshape(me, (1,)).astype(jnp.int32)

    def update(n, parts):
        if n in _ADAM_TRANSPOSED:
            outs = _sum_adamw(*parts, me_arr, w[n][0].T, m[n][0].T, v[n][0].T, "update_" + n)
            grads[n], delta[n], new_m[n], new_v[n] = (a.T[None] for a in outs)
        elif n in _COL_SHARDED:
            gsum = _sum_parts(*parts, me_arr, "sum_" + n).T
            grads[n] = gsum[None]
            delta[n], new_m[n], new_v[n] = (a[None] for a in _adamw(w[n][0], gsum, m[n][0], v[n][0], "adamw_" + n))
        else:
            outs = _sum_adamw(*parts, me_arr, w[n][0], m[n][0], v[n][0], "update_" + n)
            grads[n], delta[n], new_m[n], new_v[n] = (a[None] for a in outs)

    after = grad_x
    for tag, names in (("down", ["w_down"]), ("up", ["w_up"]), ("x", ["w_xo", "w_xq", "w_xkv", "w_out"]),
                       ("in", ["w_in"])):
        for n, parts in zip(names, _exchange_end(pending[tag], after)):
            update(n, parts)
            after = new_v[n]

    (packs,) = _gather_end(small["fetch"], [0], after)
    summed = _sum_unpack_small(packs, small["shapes"], "sum_unpack_small")
    loss = summed[-1][0, 0]
    for n, s in zip(small["early"], summed[:-1]):
        grads[n] = s
    (packs,) = _gather_end(small["fetch_last"], [0], after)
    (grads["pre_mix_g"],) = _sum_unpack_small(packs, [w["pre_mix_g"].shape], "sum_unpack_last")
    small = _SMALL
    fold = lambda v2: v2[:, :ATT_HEAD_DIM] + v2[:, ATT_HEAD_DIM:]
    grads["q_norm_g"], grads["k_norm_g"] = fold(grads["q_norm_g"]), fold(grads["k_norm_g"])
    grads["hg_lb"] = lax.dynamic_slice_in_dim(grads["hg_lb"].reshape(2, 2, HG_DIM), me * (HG_DIM // N_DEV),
                                              HG_DIM // N_DEV, axis=2)
    grads["conv_w"] = lax.dynamic_slice_in_dim(grads["conv_w"], me * (2 * D_FF // N_DEV), 2 * D_FF // N_DEV, axis=1)[None]

    flat2 = lambda a: a.reshape(-1, a.shape[-1])
    outs = _adamw_many(*[[flat2(d[n]) for n in small] for d in (w, grads, m, v)])
    for dst, vals in zip((delta, new_m, new_v), outs):
        for n, val in zip(small, vals):
            dst[n] = val.reshape(w[n].shape)

    return (loss, grad_x[None], *[grads[n] for n in _WEIGHTS], *[delta[n] for n in _WEIGHTS],
            *[new_m[n] for n in _WEIGHTS], *[new_v[n] for n in _WEIGHTS])
```

```python
import jax
import jax.numpy as jnp
from jax import lax
from jax.experimental import pallas as pl
from jax.experimental.pallas import tpu as pltpu

F32 = jnp.float32
BF16 = jnp.bfloat16

D_MODEL = 1024
GRID_W = 64
EPS = 1e-6
ATT_HEADS = 8
ATT_HEAD_DIM = 64
ATT_Q_DIM = 512
ATT_KV_DIM = 128
ROPE_THETA = 10000.0
HG_HEADS = 4
HG_DIM = 512
HG_CHUNK = 32
HG_CHUNK_LOG2 = 5
HG_BLOCK_FWD = 256
HG_BLOCK_BWD = 128
N_IN = 3328
X_HEADS = 4
X_HEAD_DIM = 256
D_FF = 2816
N_DEV = 8
LANE = 128
ADAM_LR = 0.001
ADAM_B1 = 0.9
ADAM_B2 = 0.999
ADAM_EPS = 1e-08
ADAM_WD = 0.01
ADAM_STEP = 10
VMEM_LIMIT = 56 * 1024 * 1024

MESH_T = pl.DeviceIdType.MESH


def _params(**kw):
    return pltpu.CompilerParams(vmem_limit_bytes=VMEM_LIMIT, **kw)


def _dot(a, b, ca, cb):
    return lax.dot_general(a, b, (((ca,), (cb,)), ((), ())), preferred_element_type=F32)


def _bf(x):
    return x.astype(BF16)


def _sigmoid(x):
    return 1.0 / (1.0 + jnp.exp(-x))


def _rms_fwd(x, g):
    r = lax.rsqrt(jnp.mean(x * x, axis=-1, keepdims=True) + EPS)
    return x * r * g


def _rms_bwd(dy, x, g):
    r = lax.rsqrt(jnp.mean(x * x, axis=-1, keepdims=True) + EPS)
    xh = x * r
    dg = jnp.sum(dy * xh, axis=0, keepdims=True)
    t = dy * g
    dx = r * (t - xh * jnp.mean(t * xh, axis=-1, keepdims=True))
    return dx, dg


def _full(shape):
    nd = len(shape)
    return pl.BlockSpec(shape, lambda *a: (0,) * nd)


def _norm_mm(x, g, w, trans, tn, name, after=(), tm=512):
    t, d = x.shape
    n = w.shape[0] if trans else w.shape[1]
    tm = min(tm, t)

    def body(x_ref, g_ref, w_ref, h_ref, p_ref):
        h = _bf(_rms_fwd(x_ref[...], g_ref[...]))
        h_ref[...] = h
        p_ref[...] = _dot(h, w_ref[...], 1, 1 if trans else 0)

    w_spec = pl.BlockSpec((tn, d), lambda i, j: (j, 0)) if trans else pl.BlockSpec((d, tn), lambda i, j: (0, j))
    return _call(
        body, (x, g, w), name=name, grid=(t // tm, n // tn),
        in_specs=[pl.BlockSpec((tm, d), lambda i, j: (i, 0)), _full((1, d)), w_spec],
        out_specs=[pl.BlockSpec((tm, d), lambda i, j: (i, 0)), pl.BlockSpec((tm, tn), lambda i, j: (i, j))],
        out_shape=[jax.ShapeDtypeStruct((t, d), BF16), jax.ShapeDtypeStruct((t, n), F32)], after=after)


def _mm_postnorm_res(a, w, g, res, name, tm=256):
    t, k = a.shape
    d = w.shape[1]

    def body(a_ref, w_ref, g_ref, res_ref, y_ref, o_ref):
        y = _dot(a_ref[...], w_ref[...], 1, 0)
        y_ref[...] = _bf(y)
        o_ref[...] = res_ref[...] + _rms_fwd(y, g_ref[...])

    row = lambda width: pl.BlockSpec((tm, width), lambda i: (i, 0))
    return pl.pallas_call(
        body, name=name, grid=(t // tm,),
        in_specs=[row(k), _full((k, d)), _full((1, d)), row(d)],
        out_specs=[row(d), row(d)],
        out_shape=[jax.ShapeDtypeStruct((t, d), BF16), jax.ShapeDtypeStruct((t, d), F32)],
        compiler_params=_params(),
    )(a, w, g, res)


def _mm_postnorm_res_loss(a, w, g, res, tgt, name, tm=256):
    t, k = a.shape
    d = w.shape[1]

    def body(a_ref, w_ref, g_ref, res_ref, tgt_ref, y_ref, dout_ref, loss_ref):
        @pl.when(pl.program_id(0) == 0)
        def _():
            loss_ref[...] = jnp.zeros_like(loss_ref)

        y = _dot(a_ref[...], w_ref[...], 1, 0)
        y_ref[...] = _bf(y)
        diff = res_ref[...] + _rms_fwd(y, g_ref[...]) - tgt_ref[...]
        dout_ref[...] = diff * (1.0 / d)
        part = jnp.sum(jnp.sum(diff * diff, axis=-1, keepdims=True), axis=0, keepdims=True)
        loss_ref[...] += (0.5 / d) * part

    row = lambda width: pl.BlockSpec((tm, width), lambda i: (i, 0))
    return pl.pallas_call(
        body, name=name, grid=(t // tm,),
        in_specs=[row(k), _full((k, d)), _full((1, d)), row(d), row(d)],
        out_specs=[row(d), row(d), _full((1, 1))],
        out_shape=[jax.ShapeDtypeStruct((t, d), BF16), jax.ShapeDtypeStruct((t, d), F32), jax.ShapeDtypeStruct((1, 1), F32)],
        compiler_params=_params(),
    )(a, w, g, res, tgt)


def _postnorm_bwd_mm(dout, y, g, w, name, da_dtype=F32, tm=256):
    t, d = y.shape
    k = w.shape[0]

    def body(dout_ref, y_ref, g_ref, w_ref, dy_ref, da_ref, dg_ref):
        @pl.when(pl.program_id(0) == 0)
        def _():
            dg_ref[...] = jnp.zeros_like(dg_ref)

        dy, dg = _rms_bwd(dout_ref[...], y_ref[...].astype(F32), g_ref[...])
        dg_ref[...] += dg
        dyb = _bf(dy)
        dy_ref[...] = dyb
        da_ref[...] = _dot(dyb, w_ref[...], 1, 1).astype(da_dtype)

    row = lambda width: pl.BlockSpec((tm, width), lambda i: (i, 0))
    return pl.pallas_call(
        body, name=name, grid=(t // tm,),
        in_specs=[row(d), row(d), _full((1, d)), _full((k, d))],
        out_specs=[row(d), row(k), _full((1, d))],
        out_shape=[jax.ShapeDtypeStruct((t, d), BF16), jax.ShapeDtypeStruct((t, k), da_dtype), jax.ShapeDtypeStruct((1, d), F32)],
        compiler_params=_params(),
    )(dout, y, g, w)


def _mm_prenorm_bwd(dp, w, trans, x, g, dres, name, after=(), tm=512):
    dps = list(dp) if isinstance(dp, (list, tuple)) else [dp]
    nparts = len(dps)
    t = dps[0].shape[0]
    widths = [part.shape[1] for part in dps]
    d = x.shape[1]
    tm = min(tm, t)

    def body(*refs):
        dp_refs = refs[:nparts]
        w_ref, x_ref, g_ref, dres_ref, dx_ref, dg_ref = refs[nparts:]

        @pl.when(pl.program_id(0) == 0)
        def _():
            dg_ref[...] = jnp.zeros_like(dg_ref)

        dh = None
        for k, dp_ref in enumerate(dp_refs):
            cols = slice(sum(widths[:k]), sum(widths[:k + 1]))
            term = _dot(dp_ref[...], w_ref[cols, :], 1, 0) if trans else _dot(dp_ref[...], w_ref[:, cols], 1, 1)
            dh = term if dh is None else dh + term
        dx, dg = _rms_bwd(dh, x_ref[...], g_ref[...])
        dg_ref[...] += dg
        dx_ref[...] = dres_ref[...] + dx

    row = lambda width: pl.BlockSpec((tm, width), lambda i: (i, 0))
    return _call(
        body, (*dps, w, x, g, dres), name=name, grid=(t // tm,),
        in_specs=[row(width) for width in widths] + [_full(w.shape), row(d), _full((1, d)), row(d)],
        out_specs=[row(d), _full((1, d))],
        out_shape=[jax.ShapeDtypeStruct((t, d), F32), jax.ShapeDtypeStruct((1, d), F32)], after=after)


def _dw(a, b, name, tka=256, after=()):
    parts = list(a) if isinstance(a, (list, tuple)) else [a]
    nparts = len(parts)
    t, nb = b.shape
    tiles = [part.shape[1] // tka for part in parts]
    first = [sum(tiles[:k]) for k in range(nparts)]

    def body(*refs):
        a_refs, b_ref, o_ref = refs[:nparts], refs[nparts], refs[-1]
        i = pl.program_id(0)
        for k, a_ref in enumerate(a_refs):
            @pl.when((i >= first[k]) & (i < first[k] + tiles[k]))
            def _():
                o_ref[...] = _bf(_dot(a_ref[...], b_ref[...], 0, 0))

    a_spec = lambda k: pl.BlockSpec((t, tka), lambda i: (0, jnp.clip(i - first[k], 0, tiles[k] - 1)))
    return pl.pallas_call(
        body, name=name, grid=(sum(tiles),),
        in_specs=[a_spec(k) for k in range(nparts)] + [_full((t, nb))] + [pl.BlockSpec(memory_space=pl.ANY)] * len(after),
        out_specs=pl.BlockSpec((tka, nb), lambda i: (i, 0)),
        out_shape=jax.ShapeDtypeStruct((sum(tiles) * tka, nb), BF16),
        compiler_params=_params(),
    )(*parts, b, *after)


def _rope_tables(t):
    pos = jnp.arange(t)
    r = (pos // GRID_W).astype(F32)
    c = (pos % GRID_W).astype(F32)
    npair = ATT_HEAD_DIM // 4
    inv = jnp.power(ROPE_THETA, -jnp.arange(npair, dtype=F32) / npair)
    ang = jnp.concatenate([r[:, None] * inv, c[:, None] * inv], axis=-1)
    cos = jnp.repeat(jnp.cos(ang), 2, axis=-1)
    sin = jnp.repeat(jnp.sin(ang), 2, axis=-1)
    even = (jnp.arange(ATT_HEAD_DIM) % 2) == 0
    sa = jnp.where(even, -sin, 0.0)
    sb = jnp.where(even, 0.0, sin)
    two = lambda a: jnp.tile(a, (1, 2))
    return two(cos), two(sa), two(sb)


def _head_sum_matrix():
    a = jnp.arange(LANE) // ATT_HEAD_DIM
    return (a[:, None] == a[None, :]).astype(BF16)


def _head_mean(v, bd):
    hi = _bf(v)
    lo = _bf(v - hi.astype(F32))
    return (_dot(hi, bd, 1, 0) + _dot(lo, bd, 1, 0)) * (1.0 / ATT_HEAD_DIM)


def _qk_prep(p, gq, gk, tables, bd, tm=512):
    t = p.shape[0]
    tm = min(tm, t)
    cc, sa, sb = tables

    def body(p_ref, gq_ref, gk_ref, cc_ref, sa_ref, sb_ref, bd_ref, q_ref, k_ref):
        cc_, sa_, sb_, bd_ = cc_ref[...], sa_ref[...], sb_ref[...], bd_ref[...]
        low = lax.broadcasted_iota(jnp.int32, (tm, LANE), 1) < ATT_HEAD_DIM

        def normrope(xs, g):
            xn = xs * lax.rsqrt(_head_mean(xs * xs, bd_) + EPS) * g
            return xn * cc_ + pltpu.roll(xn, LANE - 1, 1) * sa_ + pltpu.roll(xn, 1, 1) * sb_

        for j in range(4):
            y = normrope(p_ref[:, j * LANE:(j + 1) * LANE], gq_ref[...]) * (ATT_HEAD_DIM ** -0.5)
            yr = pltpu.roll(y, ATT_HEAD_DIM, 1)
            if j // 2 == 0:
                h0, h1 = jnp.where(low, y, 0.0), jnp.where(low, yr, 0.0)
            else:
                h0, h1 = jnp.where(low, 0.0, yr), jnp.where(low, 0.0, y)
            q_ref[:, (2 * j) * LANE:(2 * j + 1) * LANE] = _bf(h0)
            q_ref[:, (2 * j + 1) * LANE:(2 * j + 2) * LANE] = _bf(h1)
        k_ref[...] = _bf(normrope(p_ref[:, ATT_Q_DIM:ATT_Q_DIM + LANE], gk_ref[...]))

    row = lambda width: pl.BlockSpec((tm, width), lambda i: (i, 0))
    return pl.pallas_call(
        body, name="qk_prep", grid=(t // tm,),
        in_specs=[row(ATT_Q_DIM + LANE), _full((1, LANE)), _full((1, LANE)), row(LANE), row(LANE), row(LANE),
                  _full((LANE, LANE))],
        out_specs=[row(ATT_HEADS * LANE), row(LANE)],
        out_shape=[jax.ShapeDtypeStruct((t, ATT_HEADS * LANE), BF16), jax.ShapeDtypeStruct((t, LANE), BF16)],
        compiler_params=_params(),
    )(p, gq, gk, cc, sa, sb, bd)


def _qk_prep_bwd(p, dq, dk, dv, gq, gk, tables, bd, tm=512):
    t = p.shape[0]
    tm = min(tm, t)
    cc, sa, sb = tables
    att_w = ATT_Q_DIM + 2 * ATT_KV_DIM

    def body(p_ref, dq_ref, dk_ref, dv_ref, gq_ref, gk_ref, cc_ref, sa_ref, sb_ref, bd_ref, dp_ref, dgq_ref, dgk_ref):
        dp_ref[:, ATT_Q_DIM + LANE:att_w] = _bf(dv_ref[...])

        @pl.when(pl.program_id(0) == 0)
        def _():
            dgq_ref[...] = jnp.zeros_like(dgq_ref)
            dgk_ref[...] = jnp.zeros_like(dgk_ref)

        cc_, sa_, sb_, bd_ = cc_ref[...], sa_ref[...], sb_ref[...], bd_ref[...]
        low = lax.broadcasted_iota(jnp.int32, (tm, LANE), 1) < ATT_HEAD_DIM

        def bwd(xs, g, dy):
            r = lax.rsqrt(_head_mean(xs * xs, bd_) + EPS)
            xh = xs * r
            dxn = dy * cc_ + pltpu.roll(dy * sa_, 1, 1) + pltpu.roll(dy * sb_, LANE - 1, 1)
            dg = jnp.sum(dxn * xh, axis=0, keepdims=True)
            tt = dxn * g
            return r * (tt - xh * _head_mean(tt * xh, bd_)), dg

        dgq = jnp.zeros((1, LANE), F32)
        for j in range(4):
            d0 = dq_ref[:, (2 * j) * LANE:(2 * j + 1) * LANE]
            d1 = dq_ref[:, (2 * j + 1) * LANE:(2 * j + 2) * LANE]
            if j // 2 == 0:
                dy = jnp.where(low, d0, pltpu.roll(d1, ATT_HEAD_DIM, 1))
            else:
                dy = jnp.where(low, pltpu.roll(d0, ATT_HEAD_DIM, 1), d1)
            dx, dg = bwd(p_ref[:, j * LANE:(j + 1) * LANE], gq_ref[...], dy * (ATT_HEAD_DIM ** -0.5))
            dp_ref[:, j * LANE:(j + 1) * LANE] = _bf(dx)
            dgq = dgq + dg
        dgq_ref[...] += dgq
        dx, dg = bwd(p_ref[:, ATT_Q_DIM:ATT_Q_DIM + LANE], gk_ref[...], dk_ref[...])
        dp_ref[:, ATT_Q_DIM:ATT_Q_DIM + LANE] = _bf(dx)
        dgk_ref[...] += dg

    row = lambda width: pl.BlockSpec((tm, width), lambda i: (i, 0))
    return pl.pallas_call(
        body, name="qk_prep_bwd", grid=(t // tm,),
        in_specs=[row(ATT_Q_DIM + LANE), row(ATT_HEADS * LANE), row(LANE), row(LANE), _full((1, LANE)), _full((1, LANE)),
                  row(LANE), row(LANE), row(LANE), _full((LANE, LANE))],
        out_specs=[row(att_w), _full((1, LANE)), _full((1, LANE))],
        out_shape=[jax.ShapeDtypeStruct((t, att_w), BF16), jax.ShapeDtypeStruct((1, LANE), F32),
                   jax.ShapeDtypeStruct((1, LANE), F32)],
        compiler_params=_params(),
    )(p, dq, dk, dv, gq, gk, cc, sa, sb, bd)


def _attn_fwd(q, k, p, tq=256):
    t = k.shape[0]
    tq = min(tq, t)
    v_blk = (ATT_Q_DIM + ATT_KV_DIM) // LANE

    def body(q_ref, k_ref, v_ref, o_ref, o32_ref, lse_ref):
        k_ = k_ref[...]
        v = v_ref[...]
        lane_k = lax.broadcasted_iota(jnp.int32, (t, LANE), 1)
        lane_q = lax.broadcasted_iota(jnp.int32, (tq, LANE), 1)
        lowk, lowq = lane_k < ATT_HEAD_DIM, lane_q < ATT_HEAD_DIM
        ones_lane = (ATT_HEAD_DIM, 0)
        vm = (_bf(jnp.where(lowk, v, jnp.where(lane_k == ones_lane[0], 1.0, 0.0))),
              _bf(jnp.where(lowk, jnp.where(lane_k == ones_lane[1], 1.0, 0.0), v)))
        for j in range(4):
            kvh = j // 2
            acc = None
            for sub in range(2):
                h = 2 * j + sub
                s = _dot(q_ref[:, h * LANE:(h + 1) * LANE], k_, 1, 1)
                mx = jnp.max(s, axis=-1, keepdims=True)
                ov = _dot(jnp.exp(_bf(s - mx)), vm[kvh], 1, 0)
                l = jnp.sum(jnp.where(lane_q == ones_lane[kvh], ov, 0.0), axis=-1, keepdims=True)
                lse_ref[h] = mx + jnp.log(l)
                o = jnp.where(lowq if kvh == 0 else ~lowq, ov, 0.0) * (1.0 / l)
                if sub != kvh:
                    o = pltpu.roll(o, ATT_HEAD_DIM, 1)
                acc = o if acc is None else acc + o
            o32_ref[:, j * LANE:(j + 1) * LANE] = acc
            o_ref[:, j * LANE:(j + 1) * LANE] = _bf(acc)

    row = pl.BlockSpec((tq, ATT_Q_DIM), lambda i: (i, 0))
    return _call(
        body, (q, k, p), name="attn_fwd", grid=(t // tq,),
        in_specs=[pl.BlockSpec((tq, ATT_HEADS * LANE), lambda i: (i, 0)), _full((t, LANE)),
                  pl.BlockSpec((t, LANE), lambda i: (0, v_blk))],
        out_specs=[row, row, pl.BlockSpec((ATT_HEADS, tq, 1), lambda i: (0, i, 0))],
        out_shape=[jax.ShapeDtypeStruct((t, ATT_Q_DIM), BF16), jax.ShapeDtypeStruct((t, ATT_Q_DIM), F32),
                   jax.ShapeDtypeStruct((ATT_HEADS, t, 1), F32)])


def _attn_bwd(q, k, p, dcat, o32, lse, tq=256):
    t = k.shape[0]
    tq = min(tq, t)
    v_blk = (ATT_Q_DIM + ATT_KV_DIM) // LANE

    def body(q_ref, k_ref, v_ref, do_ref, o_ref, lse_ref, dq_ref, dk_ref, dv_ref):
        @pl.when(pl.program_id(0) == 0)
        def _():
            dk_ref[...] = jnp.zeros_like(dk_ref)
            dv_ref[...] = jnp.zeros_like(dv_ref)

        k_ = k_ref[...]
        vb = _bf(v_ref[...])
        lowq = lax.broadcasted_iota(jnp.int32, (tq, LANE), 1) < ATT_HEAD_DIM
        dk_acc = jnp.zeros((t, LANE), F32)
        dv_acc = jnp.zeros((t, LANE), F32)
        for j in range(4):
            kvh = j // 2
            dop = do_ref[:, j * LANE:(j + 1) * LANE]
            prod = dop * o_ref[:, j * LANE:(j + 1) * LANE]
            d_low = jnp.sum(jnp.where(lowq, prod, 0.0), axis=-1, keepdims=True)
            d_sub = (d_low, jnp.sum(prod, axis=-1, keepdims=True) - d_low)
            for sub in range(2):
                h = 2 * j + sub
                src = dop if sub == kvh else pltpu.roll(dop, ATT_HEAD_DIM, 1)
                do_h = _bf(jnp.where(lowq, src, 0.0) if kvh == 0 else jnp.where(lowq, 0.0, src))
                qh = q_ref[:, h * LANE:(h + 1) * LANE]
                pr = jnp.exp(_bf(_dot(qh, k_, 1, 1) - lse_ref[h]))
                ds = pr * _bf(_dot(do_h, vb, 1, 1) - d_sub[sub])
                dq_ref[:, h * LANE:(h + 1) * LANE] = _dot(ds, k_, 1, 0)
                dk_acc = dk_acc + _dot(ds, qh, 0, 0)
                dv_acc = dv_acc + _dot(pr, do_h, 0, 0)
        dk_ref[...] += dk_acc
        dv_ref[...] += dv_acc

    row = pl.BlockSpec((tq, ATT_Q_DIM), lambda i: (i, 0))
    return _call(
        body, (q, k, p, dcat, o32, lse), name="attn_bwd", grid=(t // tq,),
        in_specs=[pl.BlockSpec((tq, ATT_HEADS * LANE), lambda i: (i, 0)), _full((t, LANE)),
                  pl.BlockSpec((t, LANE), lambda i: (0, v_blk)), row, row,
                  pl.BlockSpec((ATT_HEADS, tq, 1), lambda i: (0, i, 0))],
        out_specs=[pl.BlockSpec((tq, ATT_HEADS * LANE), lambda i: (i, 0)), _full((t, LANE)), _full((t, LANE))],
        out_shape=[jax.ShapeDtypeStruct((t, ATT_HEADS * LANE), F32), jax.ShapeDtypeStruct((t, LANE), F32),
                   jax.ShapeDtypeStruct((t, LANE), F32)])


def _lower_bounds(a0, a1):
    def body(a0_ref, a1_ref, lb_ref):
        m = jnp.maximum(a0_ref[...], a1_ref[...])
        e0, e1 = jnp.exp(a0_ref[...] - m), jnp.exp(a1_ref[...] - m)
        lb_ref[...] = e0 / (e0 + e1)

    return pl.pallas_call(body, name="lower_bounds", out_shape=jax.ShapeDtypeStruct(a0.shape, F32))(a0, a1)


def _lower_bounds_bwd(a0, a1, dlb):
    def body(a0_ref, a1_ref, dlb_ref, d0_ref, d1_ref):
        m = jnp.maximum(a0_ref[...], a1_ref[...])
        e0, e1 = jnp.exp(a0_ref[...] - m), jnp.exp(a1_ref[...] - m)
        lb = e0 / (e0 + e1)
        d0 = dlb_ref[...] * lb * (1.0 - lb)
        d0_ref[...] = d0
        d1_ref[...] = -d0

    return pl.pallas_call(body, name="lower_bounds_bwd", out_shape=[jax.ShapeDtypeStruct(a0.shape, F32)] * 2)(a0, a1, dlb)


def _chunk_scan(x, pos, down):
    n = x.shape[0]
    s = 1
    while s < HG_CHUNK:
        if down:
            x = x + jnp.where(pos >= s, pltpu.roll(x, s, 0), 0.0)
        else:
            x = x + jnp.where(pos < HG_CHUNK - s, pltpu.roll(x, n - s, 0), 0.0)
        s *= 2
    return x


def _chunk_concat(x, nc):
    xb = _bf(x)
    rows = []
    for c in range(nc):
        pieces = [xb[c * HG_CHUNK:(c + 1) * HG_CHUNK, :]]
        if c:
            pieces.insert(0, jnp.zeros((HG_CHUNK, c * LANE), BF16))
        if c < nc - 1:
            pieces.append(jnp.zeros((HG_CHUNK, (nc - 1 - c) * LANE), BF16))
        rows.append(jnp.concatenate(pieces, axis=1) if len(pieces) > 1 else pieces[0])
    return jnp.concatenate(rows, axis=0) if nc > 1 else rows[0]


def _chunk_pick(x, nc):
    rows = [x[c * HG_CHUNK:(c + 1) * HG_CHUNK, c * LANE:(c + 1) * LANE] for c in range(nc)]
    return jnp.concatenate(rows, axis=0) if nc > 1 else rows[0]


def _chunk_rows(vals):
    rows = [jnp.broadcast_to(v, (HG_CHUNK, LANE)) for v in vals]
    return jnp.concatenate(rows, axis=0) if len(rows) > 1 else rows[0]


def _hgrn_gates(hq, z, lb):
    sq = _sigmoid(hq)
    sig = _sigmoid(z)
    f = lb + (1.0 - lb) * sig
    return hq * sq, sq, sig, f, jnp.log(f)


def _hgrn_local(q, f, g, v, pos, amask, rev, nc):
    k = 1.0 - f
    ba = _chunk_scan(g, pos, not rev)
    bb = _chunk_scan(g, pos, rev)
    eq = 0.5 * (ba - bb + g)
    ex_q, ex_k, ex_i, ex_d = jnp.exp(eq), jnp.exp(-eq), jnp.exp(ba), jnp.exp(bb - g)
    qb, kb, qi, kd = q * ex_q, k * ex_k, q * ex_i, k * ex_d
    dvec = jnp.exp(ba + bb - g)
    a = jnp.where(amask, _dot(_bf(qb), _bf(kb), 1, 1), 0.0)
    kd_m = _chunk_concat(kd, nc)
    qi_m = _chunk_concat(qi, nc)
    ut_all = _dot(_bf(v), kd_m, 0, 0)
    return dict(k=k, ex_q=ex_q, ex_k=ex_k, ex_i=ex_i, ex_d=ex_d, qb=qb, kb=kb, qi=qi, kd=kd, dvec=dvec, a=a,
                kd_m=kd_m, qi_m=qi_m, ut_all=ut_all)


def _hgrn_masks(n, rev):
    row = lax.broadcasted_iota(jnp.int32, (n, LANE), 0)
    ti = lax.broadcasted_iota(jnp.int32, (n, n), 0)
    si = lax.broadcasted_iota(jnp.int32, (n, n), 1)
    tri = (si >= ti) if rev else (si <= ti)
    same = jnp.right_shift(ti, HG_CHUNK_LOG2) == jnp.right_shift(si, HG_CHUNK_LOG2)
    return jnp.bitwise_and(row, HG_CHUNK - 1), same & tri


def _hgrn_specs(t, rev, bwd):
    n = min(HG_BLOCK_BWD if bwd else HG_BLOCK_FWD, t)
    nb = t // n
    if rev != bwd:
        blk = lambda i: nb - 1 - i
    else:
        blk = lambda i: i
    col = lambda base: pl.BlockSpec((n, 2 * LANE), lambda hp, i: (blk(i), base + hp))
    return n, nb, blk, col


def _hgrn_fwd(p, lb, rev, after=()):
    t = p.shape[0]
    n, nb, blk, col = _hgrn_specs(t, rev, False)
    nc = n // HG_CHUNK
    sub = n // min(HG_BLOCK_BWD, t)
    z_base = 7 if rev else 5

    def body(hq_ref, z_ref, hi_ref, lb_ref, o_ref, ssave_ref, st_scr):
        @pl.when(pl.program_id(1) == 0)
        def _():
            st_scr[...] = jnp.zeros_like(st_scr)

        pos, amask = _hgrn_masks(n, rev)
        order = list(range(nc))[::-1] if rev else list(range(nc))
        for hh in range(2):
            sl = slice(hh * LANE, (hh + 1) * LANE)
            q, _, _, f, g = _hgrn_gates(hq_ref[:, sl], z_ref[:, sl], lb_ref[0:1, sl])
            v = hi_ref[:, sl]
            c_ = _hgrn_local(q, f, g, v, pos, amask, rev, nc)
            st = st_scr[hh]
            cols = [None] * nc
            for c in order:
                cols[c] = st
                st = st * c_["dvec"][c * HG_CHUNK:c * HG_CHUNK + 1, :] + c_["ut_all"][:, c * LANE:(c + 1) * LANE]
            st_scr[hh] = st
            for s in range(sub):
                ssave_ref[sub - 1 - s if rev else s, sl, :] = cols[order[s * (nc // sub)]]
            st_all = _bf(jnp.concatenate(cols, axis=1))
            o_ref[:, sl] = _dot(_bf(c_["a"]), _bf(v), 1, 0) + _dot(c_["qi_m"], st_all, 1, 1)

    return _call(
        body, (p, p, p, lb), name="hgrn_fwd_rev" if rev else "hgrn_fwd", grid=(2, nb),
        in_specs=[col(3), col(z_base), col(9), pl.BlockSpec((1, 2 * LANE), lambda hp, i: (0, hp))],
        out_specs=[pl.BlockSpec((n, 2 * LANE), lambda hp, i: (blk(i), hp)),
                   pl.BlockSpec((sub, 2 * LANE, LANE), lambda hp, i: (blk(i), hp, 0))],
        out_shape=[jax.ShapeDtypeStruct((t, HG_DIM), F32), jax.ShapeDtypeStruct((nb * sub, HG_DIM, LANE), F32)],
        scratch_shapes=[pltpu.VMEM((2, LANE, LANE), F32)], after=after)


def _hgrn_bwd(p, lb, do, ssave, rev, other=None, after=()):
    t = p.shape[0]
    n, nb, blk, col = _hgrn_specs(t, rev, True)
    nc = n // HG_CHUNK
    z_base = 7 if rev else 5
    n_other = 0 if other is None else 2

    def body(*refs):
        hq_ref, z_ref, hi_ref, lb_ref, do_ref, ssave_ref = refs[:6]
        dhq_ref, dz_ref, dhi_ref, dlb_ref, dst_scr = refs[6 + n_other:]

        @pl.when(pl.program_id(1) == 0)
        def _():
            dst_scr[...] = jnp.zeros_like(dst_scr)
            dlb_ref[...] = jnp.zeros_like(dlb_ref)

        pos, amask = _hgrn_masks(n, rev)
        order = list(range(nc))[::-1] if rev else list(range(nc))
        for hh in range(2):
            sl = slice(hh * LANE, (hh + 1) * LANE)
            hq, lbv = hq_ref[:, sl], lb_ref[0:1, sl]
            q, sq, sig, f, g = _hgrn_gates(hq, z_ref[:, sl], lbv)
            v = hi_ref[:, sl]
            c_ = _hgrn_local(q, f, g, v, pos, amask, rev, nc)
            dvec, ut_all = c_["dvec"], c_["ut_all"]
            drow = lambda c: dvec[c * HG_CHUNK:c * HG_CHUNK + 1, :]
            st = ssave_ref[0, sl, :]
            cols = [None] * nc
            for c in order:
                cols[c] = st
                st = st * drow(c) + ut_all[:, c * LANE:(c + 1) * LANE]
            dob, vb = _bf(do_ref[:, sl]), _bf(v)
            gt_all = _dot(dob, c_["qi_m"], 0, 0)
            dcur = dst_scr[hh]
            dnext = [None] * nc
            ddrow = [None] * nc
            for c in order[::-1]:
                dnext[c] = dcur
                ddrow[c] = jnp.sum(cols[c] * dcur, axis=0, keepdims=True) * drow(c)
                dcur = dcur * drow(c) + gt_all[:, c * LANE:(c + 1) * LANE]
            dst_scr[hh] = dcur
            dsn_all = _bf(jnp.concatenate(dnext, axis=1))
            st_all = _bf(jnp.concatenate(cols, axis=1))
            da = _bf(jnp.where(amask, _dot(dob, vb, 1, 1), 0.0))
            dv = _dot(_bf(c_["a"]), dob, 0, 0) + _dot(c_["kd_m"], dsn_all, 1, 1)
            dqb = _dot(da, _bf(c_["kb"]), 1, 0)
            dkb = _dot(da, _bf(c_["qb"]), 0, 0)
            dqi = _chunk_pick(_dot(dob, st_all, 1, 0), nc)
            dkd = _chunk_pick(_dot(vb, dsn_all, 1, 0), nc)
            dq = dqb * c_["ex_q"] + dqi * c_["ex_i"]
            dk = dkb * c_["ex_k"] + dkd * c_["ex_d"]
            e = dqb * c_["qb"] - dkb * c_["kb"] + dqi * c_["qi"]
            w = dkd * c_["kd"]
            dg = _chunk_scan(e, pos, rev) + (_chunk_scan(w, pos, not rev) - w) + _chunk_rows(ddrow)
            df = dg / f - dk
            dz_ref[:, sl] = _bf(df * (1.0 - lbv) * sig * (1.0 - sig))
            dlb_ref[0:1, sl] += jnp.sum(df * (1.0 - sig), axis=0, keepdims=True)
            dhq = dq * (sq * (1.0 + hq * (1.0 - sq)))
            if other is None:
                dhq_ref[:, sl], dhi_ref[:, sl] = dhq, dv
            else:
                dhq_ref[:, sl], dhi_ref[:, sl] = _bf(dhq + refs[6][:, sl]), _bf(dv + refs[7][:, sl])

    out = pl.BlockSpec((n, 2 * LANE), lambda hp, i: (blk(i), hp))
    sum_dtype = F32 if other is None else BF16
    return _call(
        body, (p, p, p, lb, do, ssave, *(other or ())), name="hgrn_bwd_rev" if rev else "hgrn_bwd", grid=(2, nb),
        in_specs=[col(3), col(z_base), col(9), pl.BlockSpec((1, 2 * LANE), lambda hp, i: (0, hp)), out,
                  pl.BlockSpec((1, 2 * LANE, LANE), lambda hp, i: (blk(i), hp, 0))] + [out] * n_other,
        out_specs=[out, out, out, pl.BlockSpec((1, 2 * LANE), lambda hp, i: (0, hp))],
        out_shape=[jax.ShapeDtypeStruct((t, HG_DIM), sum_dtype), jax.ShapeDtypeStruct((t, HG_DIM), BF16),
                   jax.ShapeDtypeStruct((t, HG_DIM), sum_dtype), jax.ShapeDtypeStruct((1, HG_DIM), F32)],
        scratch_shapes=[pltpu.VMEM((2, LANE, LANE), F32)], after=after)


def _mix_out(att, of, ob, p, gout, w_out, g_post, x, after=(), tm=256):
    t, d = x.shape
    tm = min(tm, t)

    def body(att_ref, of_ref, ob_ref, hg0_ref, hg1_ref, gout_ref, w_ref, g_ref, x_ref, cat_ref, y_ref, x1_ref):
        cat_ref[:, :ATT_Q_DIM] = att_ref[...]
        for h in range(HG_HEADS):
            sl = slice(h * LANE, (h + 1) * LANE)
            hg_ref = hg0_ref if h < 2 else hg1_ref
            hg = hg_ref[:, (h % 2) * LANE:(h % 2 + 1) * LANE]
            nrm = _rms_fwd(of_ref[:, sl] + ob_ref[:, sl], gout_ref[...])
            cat_ref[:, ATT_Q_DIM + h * LANE:ATT_Q_DIM + (h + 1) * LANE] = _bf(nrm * hg * _sigmoid(hg))
        y = _dot(cat_ref[...], w_ref[...], 1, 0)
        y_ref[...] = _bf(y)
        x1_ref[...] = x_ref[...] + _rms_fwd(y, g_ref[...])

    row = lambda width: pl.BlockSpec((tm, width), lambda i: (i, 0))
    return _call(
        body, (att, of, ob, p, p, gout, w_out, g_post, x), name="mix_out", grid=(t // tm,),
        in_specs=[row(ATT_Q_DIM), row(HG_DIM), row(HG_DIM), pl.BlockSpec((tm, 2 * LANE), lambda i: (i, 11)),
                  pl.BlockSpec((tm, 2 * LANE), lambda i: (i, 12)), _full((1, LANE)), _full((d, d)), _full((1, d)), row(d)],
        out_specs=[row(d), row(d), row(d)],
        out_shape=[jax.ShapeDtypeStruct((t, d), BF16), jax.ShapeDtypeStruct((t, d), BF16), jax.ShapeDtypeStruct((t, d), F32)],
        after=after)


def _rec_bwd(dcat, of, ob, p, gout, tm=256):
    t = of.shape[0]
    tm = min(tm, t)

    def body(dc_ref, of_ref, ob_ref, hg0_ref, hg1_ref, gout_ref, do_ref, dhg_ref, dgo_ref):
        @pl.when(pl.program_id(0) == 0)
        def _():
            dgo_ref[...] = jnp.zeros_like(dgo_ref)

        dgo = jnp.zeros((1, LANE), F32)
        for h in range(HG_HEADS):
            sl = slice(h * LANE, (h + 1) * LANE)
            hg_ref = hg0_ref if h < 2 else hg1_ref
            hg = hg_ref[:, (h % 2) * LANE:(h % 2 + 1) * LANE]
            o = of_ref[:, sl] + ob_ref[:, sl]
            sg = _sigmoid(hg)
            drec = dc_ref[:, sl]
            dhg_ref[:, sl] = _bf(drec * _rms_fwd(o, gout_ref[...]) * (sg * (1.0 + hg * (1.0 - sg))))
            do, dg = _rms_bwd(drec * hg * sg, o, gout_ref[...])
            do_ref[:, sl] = do
            dgo = dgo + dg
        dgo_ref[...] += dgo

    row = lambda width: pl.BlockSpec((tm, width), lambda i: (i, 0))
    return pl.pallas_call(
        body, name="rec_bwd", grid=(t // tm,),
        in_specs=[pl.BlockSpec((tm, HG_DIM), lambda i: (i, 1)), row(HG_DIM), row(HG_DIM),
                  pl.BlockSpec((tm, 2 * LANE), lambda i: (i, 11)), pl.BlockSpec((tm, 2 * LANE), lambda i: (i, 12)),
                  _full((1, LANE))],
        out_specs=[row(HG_DIM), row(HG_DIM), _full((1, LANE))],
        out_shape=[jax.ShapeDtypeStruct((t, HG_DIM), F32), jax.ShapeDtypeStruct((t, HG_DIM), BF16),
                   jax.ShapeDtypeStruct((1, LANE), F32)],
        compiler_params=_params(),
    )(dcat, of, ob, p, p, gout)


def _xattn_fwd(q, kv, tq=512):
    t, d = q.shape
    tq = min(tq, t)
    nm = kv.shape[0]

    def body(q_ref, kv_ref, o_ref):
        for h in range(X_HEADS):
            sl = slice(h * X_HEAD_DIM, (h + 1) * X_HEAD_DIM)
            s = _dot(_bf(q_ref[:, sl]), _bf(kv_ref[:, sl]), 1, 1) * (X_HEAD_DIM ** -0.5)
            e = jnp.exp(s - jnp.max(s, axis=-1, keepdims=True))
            pr = e * (1.0 / jnp.sum(e, axis=-1, keepdims=True))
            o_ref[:, sl] = _bf(_dot(_bf(pr), _bf(kv_ref[:, d + h * X_HEAD_DIM:d + (h + 1) * X_HEAD_DIM]), 1, 0))

    return pl.pallas_call(
        body, name="xattn_fwd", grid=(t // tq,),
        in_specs=[pl.BlockSpec((tq, d), lambda i: (i, 0)), _full((nm, 2 * d))],
        out_specs=pl.BlockSpec((tq, d), lambda i: (i, 0)),
        out_shape=jax.ShapeDtypeStruct((t, d), BF16),
        compiler_params=_params(),
    )(q, kv)


def _xattn_bwd(q, kv, do, tq=512):
    t, d = q.shape
    tq = min(tq, t)
    nm = kv.shape[0]

    def body(q_ref, kv_ref, do_ref, dq_ref, dkv_ref):
        @pl.when(pl.program_id(0) == 0)
        def _():
            dkv_ref[...] = jnp.zeros_like(dkv_ref)

        for h in range(X_HEADS):
            sl = slice(h * X_HEAD_DIM, (h + 1) * X_HEAD_DIM)
            slv = slice(d + h * X_HEAD_DIM, d + (h + 1) * X_HEAD_DIM)
            qb, kb, vb, dob = _bf(q_ref[:, sl]), _bf(kv_ref[:, sl]), _bf(kv_ref[:, slv]), _bf(do_ref[:, sl])
            s = _dot(qb, kb, 1, 1) * (X_HEAD_DIM ** -0.5)
            e = jnp.exp(s - jnp.max(s, axis=-1, keepdims=True))
            pr = e * (1.0 / jnp.sum(e, axis=-1, keepdims=True))
            dpr = _dot(dob, vb, 1, 1)
            ds = _bf(pr * (dpr - jnp.sum(pr * dpr, axis=-1, keepdims=True)) * (X_HEAD_DIM ** -0.5))
            dq_ref[:, sl] = _bf(_dot(ds, kb, 1, 0))
            dkv_ref[:, sl] += _dot(ds, qb, 0, 0)
            dkv_ref[:, slv] += _dot(_bf(pr), dob, 0, 0)

    return pl.pallas_call(
        body, name="xattn_bwd", grid=(t // tq,),
        in_specs=[pl.BlockSpec((tq, d), lambda i: (i, 0)), _full((nm, 2 * d)), pl.BlockSpec((tq, d), lambda i: (i, 0))],
        out_specs=[pl.BlockSpec((tq, d), lambda i: (i, 0)), _full((nm, 2 * d))],
        out_shape=[jax.ShapeDtypeStruct((t, d), BF16), jax.ShapeDtypeStruct((nm, 2 * d), F32)],
        compiler_params=_params(),
    )(q, kv, do)


CONV_TN = 256


def _shift_rows(u, row, t, delta):
    if delta < 0:
        return jnp.where(row == 0, 0.0, pltpu.roll(u, 1, 0))
    return jnp.where(row == t - 1, 0.0, pltpu.roll(u, t - 1, 0))


def _conv_gate_fwd(u, conv_w, conv_b):
    t = u.shape[0]
    nt = D_FF // CONV_TN

    def body(ug_ref, uv_ref, wg_ref, wv_ref, bg_ref, bv_ref, a_ref):
        row = lax.broadcasted_iota(jnp.int32, (t, CONV_TN), 0)

        def conv(u_ref, w_ref, b_ref):
            uu = u_ref[...]
            return (b_ref[...] + _shift_rows(uu, row, t, -1) * w_ref[0:1, :] + uu * w_ref[1:2, :]
                    + _shift_rows(uu, row, t, 1) * w_ref[2:3, :])

        gate = conv(ug_ref, wg_ref, bg_ref)
        a_ref[...] = _bf(gate * _sigmoid(gate) * conv(uv_ref, wv_ref, bv_ref))

    col = lambda rows, off: pl.BlockSpec((rows, CONV_TN), lambda j: (0, j + off))
    return pl.pallas_call(
        body, name="conv_gate_fwd", grid=(nt,),
        in_specs=[col(t, 0), col(t, nt), col(3, 0), col(3, nt), col(1, 0), col(1, nt)],
        out_specs=col(t, 0),
        out_shape=jax.ShapeDtypeStruct((t, D_FF), BF16),
        compiler_params=_params(),
    )(u, u, conv_w, conv_w, conv_b, conv_b)


def _conv_gate_bwd(u, conv_w, conv_b, da, after=()):
    t = u.shape[0]
    nt = D_FF // CONV_TN

    def body(ug_ref, uv_ref, wg_ref, wv_ref, bg_ref, bv_ref, da_ref, dug_ref, dwg_ref, dbg_ref, duv_ref, dwv_ref, dbv_ref):
        row = lax.broadcasted_iota(jnp.int32, (t, CONV_TN), 0)
        ug, uv = ug_ref[...], uv_ref[...]
        ug_m, ug_p = _shift_rows(ug, row, t, -1), _shift_rows(ug, row, t, 1)
        uv_m, uv_p = _shift_rows(uv, row, t, -1), _shift_rows(uv, row, t, 1)
        gate = bg_ref[...] + ug_m * wg_ref[0:1, :] + ug * wg_ref[1:2, :] + ug_p * wg_ref[2:3, :]
        val = bv_ref[...] + uv_m * wv_ref[0:1, :] + uv * wv_ref[1:2, :] + uv_p * wv_ref[2:3, :]
        sg = _sigmoid(gate)
        da_ = da_ref[...].astype(F32)

        def emit(dacc, um, uu, up, w_ref, du_ref, dw_ref, db_ref):
            du_ref[...] = _bf(_shift_rows(dacc, row, t, 1) * w_ref[0:1, :] + dacc * w_ref[1:2, :]
                              + _shift_rows(dacc, row, t, -1) * w_ref[2:3, :])
            dw_ref[0:1, :] = jnp.sum(dacc * um, axis=0, keepdims=True)
            dw_ref[1:2, :] = jnp.sum(dacc * uu, axis=0, keepdims=True)
            dw_ref[2:3, :] = jnp.sum(dacc * up, axis=0, keepdims=True)
            db_ref[...] = jnp.sum(dacc, axis=0, keepdims=True)

        emit(da_ * val * (sg * (1.0 + gate * (1.0 - sg))), ug_m, ug, ug_p, wg_ref, dug_ref, dwg_ref, dbg_ref)
        emit(da_ * gate * sg, uv_m, uv, uv_p, wv_ref, duv_ref, dwv_ref, dbv_ref)

    col = lambda rows, off: pl.BlockSpec((rows, CONV_TN), lambda j: (0, j + off))
    half_shapes = [jax.ShapeDtypeStruct((t, D_FF), BF16), jax.ShapeDtypeStruct((3, D_FF), F32),
                   jax.ShapeDtypeStruct((1, D_FF), F32)]
    outs = _call(
        body, (u, u, conv_w, conv_w, conv_b, conv_b, da), name="conv_gate_bwd", grid=(nt,),
        in_specs=[col(t, 0), col(t, nt), col(3, 0), col(3, nt), col(1, 0), col(1, nt), col(t, 0)],
        out_specs=[col(t, 0), col(3, 0), col(1, 0)] * 2, out_shape=half_shapes * 2, after=after)
    return outs[:3], outs[3:]


def _row_tile(r, cap):
    best = r
    for cand in range(16, cap + 1, 16):
        if r % cand == 0:
            best = cand
    return best


def _sum_parts(own, got, me, name, tr=256):
    _, r, c = got.shape
    tr = _row_tile(r, tr)

    def body(me_ref, own_ref, got_ref, o_ref):
        mine = own_ref[...].astype(F32)
        acc = None
        for i in range(N_DEV):
            term = jnp.where(me_ref[0] == i, mine, got_ref[i].astype(F32))
            acc = term if acc is None else acc + term
        o_ref[...] = acc

    return pl.pallas_call(
        body, name=name,
        grid_spec=pltpu.PrefetchScalarGridSpec(
            num_scalar_prefetch=1, grid=(r // tr,),
            in_specs=[pl.BlockSpec((None, tr, c), lambda i, me_ref: (me_ref[0], i, 0)),
                      pl.BlockSpec((N_DEV, tr, c), lambda i, me_ref: (0, i, 0))],
            out_specs=pl.BlockSpec((tr, c), lambda i, me_ref: (i, 0))),
        out_shape=jax.ShapeDtypeStruct((r, c), F32),
        compiler_params=_params(),
    )(me, own, got)


def _adamw_math(w_, g_, m_, v_):
    m_ = ADAM_B1 * m_ + (1.0 - ADAM_B1) * g_
    v_ = ADAM_B2 * v_ + (1.0 - ADAM_B2) * (g_ * g_)
    m_hat = m_ / (1.0 - ADAM_B1 ** ADAM_STEP)
    v_hat = v_ / (1.0 - ADAM_B2 ** ADAM_STEP)
    return -ADAM_LR * (m_hat / (jnp.sqrt(v_hat) + ADAM_EPS) + ADAM_WD * w_), m_, v_


def _sum_adamw(own, got, me, w, m, v, name, tr=256):
    _, r, c = got.shape
    tr = _row_tile(r, tr)

    def body(me_ref, own_ref, got_ref, w_ref, m_ref, v_ref, g_ref, d_ref, nm_ref, nv_ref):
        mine = own_ref[...].astype(F32)
        acc = None
        for i in range(N_DEV):
            term = jnp.where(me_ref[0] == i, mine, got_ref[i].astype(F32))
            acc = term if acc is None else acc + term
        g_ref[...] = acc
        d_ref[...], nm_ref[...], nv_ref[...] = _adamw_math(w_ref[...], acc, m_ref[...], v_ref[...])

    blk = pl.BlockSpec((tr, c), lambda i, me_ref: (i, 0))
    return pl.pallas_call(
        body, name=name,
        grid_spec=pltpu.PrefetchScalarGridSpec(
            num_scalar_prefetch=1, grid=(r // tr,),
            in_specs=[pl.BlockSpec((None, tr, c), lambda i, me_ref: (me_ref[0], i, 0)),
                      pl.BlockSpec((N_DEV, tr, c), lambda i, me_ref: (0, i, 0)), blk, blk, blk],
            out_specs=[blk] * 4),
        out_shape=[jax.ShapeDtypeStruct((r, c), F32)] * 4,
        compiler_params=_params(),
    )(me, own, got, w, m, v)


def _adamw(w, g, m, v, name, tr=256):
    r, c = w.shape
    tr = _row_tile(r, tr)

    def body(w_ref, g_ref, m_ref, v_ref, d_ref, nm_ref, nv_ref):
        d_ref[...], nm_ref[...], nv_ref[...] = _adamw_math(w_ref[...], g_ref[...], m_ref[...], v_ref[...])

    blk = pl.BlockSpec((tr, c), lambda i: (i, 0))
    return pl.pallas_call(
        body, name=name, grid=(r // tr,),
        in_specs=[blk] * 4, out_specs=[blk] * 3,
        out_shape=[jax.ShapeDtypeStruct((r, c), F32)] * 3,
        compiler_params=_params(),
    )(w, g, m, v)


def _mesh_pos():
    return lax.axis_index("x"), lax.axis_index("y"), lax.axis_index("c")


def _dev_index(px, py, pc):
    return 4 * px + 2 * py + pc


class _Gather:
    def __init__(self, arrs):
        self.arrs = list(arrs)
        n = len(self.arrs)
        self.out_shape = [jax.ShapeDtypeStruct((N_DEV,) + a.shape, a.dtype) for a in self.arrs]
        self.sems = [pltpu.SemaphoreType.DMA((7, n)), pltpu.SemaphoreType.DMA((7, n)), pltpu.SemaphoreType.DMA((n,))]

    def _ctx(self, ins, outs, sems):
        send_sems, recv_sems, local_sems = sems
        x, y, c = _mesh_pos()
        chips = [(1 - x, y), (x, 1 - y), (1 - x, 1 - y)]

        def copy(k, a, block, to, src=None):
            dst = outs[a].at[_dev_index(*block)]
            return pltpu.make_async_remote_copy(
                src_ref=dst if src is None else src, dst_ref=dst, send_sem=send_sems.at[k, a], recv_sem=recv_sems.at[k, a],
                device_id=to, device_id_type=MESH_T)

        n = len(ins)
        me, sibling = (x, y, c), (x, y, 1 - c)
        mine = [pltpu.make_async_copy(ins[a], outs[a].at[_dev_index(*me)], local_sems.at[a]) for a in range(n)]
        first = [copy(0, a, me, sibling, src=ins[a]) for a in range(n)]
        first += [copy(1 + j, a, me, (*chip, c), src=ins[a]) for j, chip in enumerate(chips) for a in range(n)]
        passed = [copy(4 + j, a, (*chip, c), sibling) for j, chip in enumerate(chips) for a in range(n)]
        return n, c, me, sibling, chips, copy, mine, first, passed

    def start(self, ins, outs, sems):
        _, _, _, _, _, _, mine, first, _ = self._ctx(ins, outs, sems)
        for cp in mine + first:
            cp.start()

    def forward(self, ins, outs, sems):
        n, c, me, _, chips, copy, _, _, passed = self._ctx(ins, outs, sems)
        for j, chip in enumerate(chips):
            for a in range(n):
                copy(1 + j, a, (*chip, c), me).wait_recv()
                passed[j * n + a].start()

    def finish(self, ins, outs, sems):
        n, c, me, sibling, chips, copy, mine, first, passed = self._ctx(ins, outs, sems)
        for a in range(n):
            copy(0, a, sibling, me).wait_recv()
        for j, chip in enumerate(chips):
            for a in range(n):
                copy(4 + j, a, (*chip, 1 - c), me).wait_recv()
        for cp in first + passed:
            cp.wait_send()
        for cp in mine:
            cp.wait()


def _comm_alone(comm, name):
    n = len(comm.arrs)

    def body(*refs):
        ins, outs, sems = refs[:n], refs[n:2 * n], refs[2 * n:]
        comm.start(ins, outs, sems)
        if comm.forward is not None:
            comm.forward(ins, outs, sems)
        comm.finish(ins, outs, sems)

    any_spec = pl.BlockSpec(memory_space=pl.ANY)
    return pl.pallas_call(body, name=name, in_specs=[any_spec] * n, out_specs=[any_spec] * n, out_shape=comm.out_shape,
                          scratch_shapes=comm.sems)(*comm.arrs)


def _peers(x, y, c):
    out = []
    for k in range(1, N_DEV):
        pos = (1 - x if k & 4 else x, 1 - y if k & 2 else y, 1 - c if k & 1 else c)
        out.append((k, pos, _dev_index(*pos)))
    return out


def _exchange_begin(arrs, tag):
    n = len(arrs)
    lands = [lax.empty(a.shape, a.dtype) for a in arrs]

    def start_body(*refs):
        ins, land = refs[:n], refs[n:2 * n]
        send_sems, recv_sems, token = refs[2 * n], refs[2 * n + 1], refs[-1]
        x, y, c = _mesh_pos()
        me_i = _dev_index(x, y, c)
        for k, pos, peer_i in _peers(x, y, c):
            for a in range(n):
                pltpu.make_async_remote_copy(
                    src_ref=ins[a].at[peer_i], dst_ref=land[a].at[me_i], send_sem=send_sems.at[(k - 1) * n + a],
                    recv_sem=recv_sems.at[(k - 1) * n + a], device_id=pos, device_id_type=MESH_T).start()
        token[...] = jnp.zeros_like(token)

    hbm = pl.BlockSpec(memory_space=pltpu.HBM)
    sem = pl.BlockSpec(memory_space=pltpu.SEMAPHORE)
    thru = [pltpu.HBM(a.shape, a.dtype) for a in arrs]
    outs = pl.pallas_call(
        start_body, name="exchange_start_" + tag,
        out_shape=[pltpu.SemaphoreType.DMA((7 * n,)), pltpu.SemaphoreType.DMA((7 * n,))] + thru + thru
        + [jax.ShapeDtypeStruct((8, LANE), F32)],
        in_specs=[hbm] * (2 * n), out_specs=[sem, sem] + [hbm] * (2 * n) + [pl.BlockSpec(memory_space=pltpu.VMEM)],
        input_output_aliases={i: 2 + i for i in range(2 * n)},
        compiler_params=pltpu.CompilerParams(has_side_effects=pltpu.SideEffectType.DATAFLOW_SIDE_EFFECTING),
    )(*[pltpu.with_memory_space_constraint(a, pltpu.HBM) for a in list(arrs) + list(lands)])
    return (tag, n, outs[0], outs[1], outs[2:2 + n], outs[2 + n:2 + 2 * n]), outs[-1]


def _exchange_end(handle, after):
    tag, n, send_sems, recv_sems, srcs, lands = handle

    def body(*refs):
        ins, land = refs[:n], refs[n:2 * n]
        send_sems_, recv_sems_ = refs[2 * n], refs[2 * n + 1]
        x, y, c = _mesh_pos()
        me_i = _dev_index(x, y, c)
        for k, pos, peer_i in _peers(x, y, c):
            for a in range(n):
                cp = pltpu.make_async_remote_copy(
                    src_ref=ins[a].at[peer_i], dst_ref=land[a].at[me_i], send_sem=send_sems_.at[(k - 1) * n + a],
                    recv_sem=recv_sems_.at[(k - 1) * n + a], device_id=pos, device_id_type=MESH_T)
                cp.wait_send()
                cp.wait_recv()

    hbm = pl.BlockSpec(memory_space=pltpu.HBM)
    sem = pl.BlockSpec(memory_space=pltpu.SEMAPHORE)
    outs = pl.pallas_call(
        body, name="exchange_end_" + tag, out_shape=[pltpu.HBM(a.shape, a.dtype) for a in list(srcs) + list(lands)],
        in_specs=[hbm] * (2 * n) + [sem, sem, pl.BlockSpec(memory_space=pl.ANY)], out_specs=[hbm] * (2 * n),
        input_output_aliases={i: i for i in range(2 * n)},
        compiler_params=pltpu.CompilerParams(has_side_effects=pltpu.SideEffectType.DATAFLOW_SIDE_EFFECTING),
    )(*srcs, *lands, send_sems, recv_sems, after)
    return list(zip(outs[:n], outs[n:]))


def _gather_begin(arrs, tag):
    n = len(arrs)
    me = _dev_index(*_mesh_pos())
    lands = [lax.dynamic_update_slice(lax.empty((N_DEV,) + a.shape, a.dtype), a[None], (me,) + (0,) * a.ndim) for a in arrs]

    def start_body(*refs):
        ins, land = refs[:n], refs[n:2 * n]
        send_sems, recv_sems, token = refs[2 * n], refs[2 * n + 1], refs[-1]
        x, y, c = _mesh_pos()
        me_i = _dev_index(x, y, c)
        for a in range(n):
            for k, pos, _ in _peers(x, y, c):
                pltpu.make_async_remote_copy(
                    src_ref=ins[a], dst_ref=land[a].at[me_i], send_sem=send_sems.at[(k - 1) * n + a],
                    recv_sem=recv_sems.at[(k - 1) * n + a], device_id=pos, device_id_type=MESH_T).start()
        token[...] = jnp.zeros_like(token)

    hbm = pl.BlockSpec(memory_space=pltpu.HBM)
    sem = pl.BlockSpec(memory_space=pltpu.SEMAPHORE)
    thru = [pltpu.HBM(a.shape, a.dtype) for a in list(arrs) + lands]
    outs = pl.pallas_call(
        start_body, name="gather_start_" + tag,
        out_shape=[pltpu.SemaphoreType.DMA((7 * n,)), pltpu.SemaphoreType.DMA((7 * n,))] + thru
        + [jax.ShapeDtypeStruct((8, LANE), F32)],
        in_specs=[hbm] * (2 * n), out_specs=[sem, sem] + [hbm] * (2 * n) + [pl.BlockSpec(memory_space=pltpu.VMEM)],
        input_output_aliases={i: 2 + i for i in range(2 * n)},
        compiler_params=pltpu.CompilerParams(has_side_effects=pltpu.SideEffectType.DATAFLOW_SIDE_EFFECTING),
    )(*[pltpu.with_memory_space_constraint(a, pltpu.HBM) for a in list(arrs) + lands])
    return (tag, n, outs[0], outs[1], outs[2:2 + n], outs[2 + n:2 + 2 * n]), outs[-1]


def _gather_end(handle, which, after):
    tag, n, send_sems, recv_sems, srcs, lands = handle
    m = len(which)

    def body(*refs):
        ins, land = refs[:m], refs[m:2 * m]
        send_sems_, recv_sems_ = refs[2 * m], refs[2 * m + 1]
        x, y, c = _mesh_pos()
        me_i = _dev_index(x, y, c)
        for j, a in enumerate(which):
            for k, pos, _ in _peers(x, y, c):
                cp = pltpu.make_async_remote_copy(
                    src_ref=ins[j], dst_ref=land[j].at[me_i], send_sem=send_sems_.at[(k - 1) * n + a],
                    recv_sem=recv_sems_.at[(k - 1) * n + a], device_id=pos, device_id_type=MESH_T)
                cp.wait_send()
                cp.wait_recv()

    hbm = pl.BlockSpec(memory_space=pltpu.HBM)
    sem = pl.BlockSpec(memory_space=pltpu.SEMAPHORE)
    ops = [srcs[a] for a in which] + [lands[a] for a in which]
    outs = pl.pallas_call(
        body, name="gather_end_%s_%s" % (tag, "_".join(str(a) for a in which)),
        out_shape=[pltpu.HBM(a.shape, a.dtype) for a in ops],
        in_specs=[hbm] * (2 * m) + [sem, sem, pl.BlockSpec(memory_space=pl.ANY)], out_specs=[hbm] * (2 * m),
        input_output_aliases={i: i for i in range(2 * m)},
        compiler_params=pltpu.CompilerParams(has_side_effects=pltpu.SideEffectType.DATAFLOW_SIDE_EFFECTING),
    )(*ops, send_sems, recv_sems, after)
    return list(outs[m:])


_SPLIT = dict(has_side_effects=pltpu.SideEffectType.DATAFLOW_SIDE_EFFECTING)


def _chips(x, y):
    return [(1 - x, y), (x, 1 - y), (1 - x, 1 - y)]


def _gather2_begin(arrs, tag):
    n = len(arrs)
    me = _dev_index(*_mesh_pos())
    lands = [lax.dynamic_update_slice(lax.empty((N_DEV,) + a.shape, a.dtype), a[None], (me,) + (0,) * a.ndim) for a in arrs]

    def body(*refs):
        ins, land = refs[:n], refs[n:2 * n]
        send1, recv1, token = refs[2 * n], refs[2 * n + 1], refs[-1]
        x, y, c = _mesh_pos()
        me_i = _dev_index(x, y, c)
        targets = [(x, y, 1 - c)] + [(*chip, c) for chip in _chips(x, y)]
        for a in range(n):
            for k, pos in enumerate(targets):
                pltpu.make_async_remote_copy(
                    src_ref=ins[a], dst_ref=land[a].at[me_i], send_sem=send1.at[k * n + a], recv_sem=recv1.at[k * n + a],
                    device_id=pos, device_id_type=MESH_T).start()
        token[...] = jnp.zeros_like(token)

    hbm = pl.BlockSpec(memory_space=pltpu.HBM)
    sem = pl.BlockSpec(memory_space=pltpu.SEMAPHORE)
    thru = [pltpu.HBM(a.shape, a.dtype) for a in list(arrs) + lands]
    outs = pl.pallas_call(
        body, name="gather_start_" + tag,
        out_shape=[pltpu.SemaphoreType.DMA((4 * n,)), pltpu.SemaphoreType.DMA((4 * n,))] + thru
        + [jax.ShapeDtypeStruct((8, LANE), F32)],
        in_specs=[hbm] * (2 * n), out_specs=[sem, sem] + [hbm] * (2 * n) + [pl.BlockSpec(memory_space=pltpu.VMEM)],
        input_output_aliases={i: 2 + i for i in range(2 * n)}, compiler_params=pltpu.CompilerParams(**_SPLIT),
    )(*[pltpu.with_memory_space_constraint(a, pltpu.HBM) for a in list(arrs) + lands])
    return dict(tag=tag, n=n, send1=outs[0], recv1=outs[1], srcs=list(outs[2:2 + n]), lands=list(outs[2 + n:2 + 2 * n])), outs[-1]


def _gather2_forward(handle, which, after, part):
    n, m = handle["n"], len(which)

    def body(*refs):
        land, recv1 = refs[:m], refs[m]
        send2, recv2, token = refs[m + 2], refs[m + 3], refs[-1]
        token[...] = jnp.zeros_like(token)
        x, y, c = _mesh_pos()
        for j, a in enumerate(which):
            for k, chip in enumerate(_chips(x, y)):
                blk = land[j].at[_dev_index(*chip, c)]
                copy = pltpu.make_async_remote_copy(
                    src_ref=blk, dst_ref=blk, send_sem=send2.at[k * m + j], recv_sem=recv2.at[k * m + j],
                    device_id=(x, y, 1 - c), device_id_type=MESH_T)
                pltpu.make_async_remote_copy(
                    src_ref=blk, dst_ref=blk, send_sem=send2.at[k * m + j], recv_sem=recv1.at[(1 + k) * n + a],
                    device_id=(*chip, c), device_id_type=MESH_T).wait_recv()
                copy.start()

    hbm = pl.BlockSpec(memory_space=pltpu.HBM)
    sem = pl.BlockSpec(memory_space=pltpu.SEMAPHORE)
    lands = [handle["lands"][a] for a in which]
    outs = pl.pallas_call(
        body, name="gather_forward_%s_%s" % (handle["tag"], part),
        out_shape=[pltpu.SemaphoreType.DMA((3 * m,)), pltpu.SemaphoreType.DMA((3 * m,))] + [pltpu.HBM(a.shape, a.dtype) for a in lands]
        + [jax.ShapeDtypeStruct((8, LANE), F32)],
        in_specs=[hbm] * m + [sem, pl.BlockSpec(memory_space=pl.ANY)],
        out_specs=[sem, sem] + [hbm] * m + [pl.BlockSpec(memory_space=pltpu.VMEM)],
        input_output_aliases={i: 2 + i for i in range(m)}, compiler_params=pltpu.CompilerParams(**_SPLIT),
    )(*lands, handle["recv1"], after)
    for j, a in enumerate(which):
        handle["lands"][a] = outs[2 + j]
        handle[("leg2", a)] = (outs[0], outs[1], j, m)
    return outs[-1]


def _gather2_end(handle, a, after):
    n = handle["n"]
    send2, recv2, j, m = handle[("leg2", a)]

    def body(src, land, send1, recv1, send2_, recv2_, after_ref, src_out, land_out):
        x, y, c = _mesh_pos()
        me_i = _dev_index(x, y, c)
        sibling = (x, y, 1 - c)
        for k, pos in enumerate([sibling] + [(*chip, c) for chip in _chips(x, y)]):
            first = pltpu.make_async_remote_copy(
                src_ref=src, dst_ref=land.at[me_i], send_sem=send1.at[k * n + a], recv_sem=recv1.at[k * n + a],
                device_id=pos, device_id_type=MESH_T)
            first.wait_send()
            if k == 0:
                first.wait_recv()
        for k in range(3):
            second = pltpu.make_async_remote_copy(
                src_ref=land.at[me_i], dst_ref=land.at[me_i], send_sem=send2_.at[k * m + j], recv_sem=recv2_.at[k * m + j],
                device_id=sibling, device_id_type=MESH_T)
            second.wait_send()
            second.wait_recv()

    hbm = pl.BlockSpec(memory_space=pltpu.HBM)
    sem = pl.BlockSpec(memory_space=pltpu.SEMAPHORE)
    src, land = handle["srcs"][a], handle["lands"][a]
    outs = pl.pallas_call(
        body, name="gather_end_%s_%d" % (handle["tag"], a),
        out_shape=[pltpu.HBM(src.shape, src.dtype), pltpu.HBM(land.shape, land.dtype)],
        in_specs=[hbm, hbm, sem, sem, sem, sem, pl.BlockSpec(memory_space=pl.ANY)], out_specs=[hbm, hbm],
        input_output_aliases={0: 0, 1: 1}, compiler_params=pltpu.CompilerParams(**_SPLIT),
    )(src, land, handle["send1"], handle["recv1"], send2, recv2, after)
    return outs[1]


def _call(body, operands, *, name, grid, in_specs, out_specs, out_shape, scratch_shapes=(), after=()):
    ni, nd = len(in_specs), len(after)

    def ordered(*refs):
        body(*refs[:ni], *refs[ni + nd:])

    outs = pl.pallas_call(
        ordered, name=name, grid=grid, in_specs=list(in_specs) + [pl.BlockSpec(memory_space=pl.ANY)] * nd,
        out_specs=out_specs, out_shape=out_shape, scratch_shapes=list(scratch_shapes), compiler_params=_params(),
    )(*operands, *after)
    return list(outs)


def _rows(a):
    return a.reshape(-1, a.shape[-1])


def _slots(a):
    return a.reshape(N_DEV, -1, a.shape[-1])


def _local_step(x, mem, tgt, w, wire):
    tables = _rope_tables(x.shape[0])
    bd = _head_sum_matrix()
    two = lambda g: jnp.tile(g, (1, 2))
    gq2, gk2 = two(w["q_norm_g"]), two(w["k_norm_g"])

    w_in_t, hg_lb, conv_w = _comm_alone(_Gather([wire["w_in"], w["hg_lb"].reshape(4, -1), w["conv_w"][0]]), "gather_w_in")
    w_in_t = _rows(w_in_t)
    hg_lb = jnp.transpose(hg_lb.reshape(N_DEV, 2, 2, -1), (1, 2, 0, 3)).reshape(2, 2, HG_DIM)
    conv_w = jnp.transpose(conv_w, (1, 0, 2)).reshape(3, 2 * D_FF)
    lb_a0, lb_a1 = hg_lb[:, 0, :], hg_lb[:, 1, :]

    order = ("w_out", "w_xq", "w_xkv", "w_xo", "w_up", "w_down")
    w_in_t, *later = lax.optimization_barrier((w_in_t, *[wire[n] for n in order]))
    fetch, started = _gather2_begin(later, "w")
    take = lambda n, after: _rows(_gather2_end(fetch, order.index(n), after))

    h1, p = _norm_mm(x, w["pre_mix_g"], w_in_t, True, N_IN, "in_proj", after=[started])
    qr, kr = _qk_prep(p, gq2, gk2, tables, bd)
    att, att32, lse = _attn_fwd(qr, kr, p)
    passed = _gather2_forward(fetch, [0, 1, 2, 3], att, "a")
    lb = _lower_bounds(lb_a0, lb_a1)
    of, s_f = _hgrn_fwd(p, lb[0:1], False, after=[passed])
    ob, s_b = _hgrn_fwd(p, lb[1:2], True, after=[passed, of])
    passed = _gather2_forward(fetch, [4, 5], ob, "b")
    w_out = take("w_out", ob)
    cat, mixed, x1 = _mix_out(att, of, ob, p, w["hg_out_norm_g"], w_out, w["post_mix_g"], x, after=[passed])

    w_xq = take("w_xq", x1)
    h2, q2 = _norm_mm(x1, w["pre_x_g"], w_xq, False, 1024, "xq_proj")
    w_xkv_t = take("w_xkv", q2)
    mn, kv = _norm_mm(mem, w["mem_norm_g"], w_xkv_t, True, 2 * D_MODEL, "xkv_proj")
    o2 = _xattn_fwd(q2, kv)
    w_xo = take("w_xo", o2)
    y2, x2 = _mm_postnorm_res(o2, w_xo, w["post_x_g"], x1, "xo_proj")

    w_up_t = take("w_up", x2)
    h3, u = _norm_mm(x2, w["pre_ffn_g"], w_up_t, True, 2 * D_FF, "up_proj", tm=256)
    a = _conv_gate_fwd(u, conv_w, w["conv_b"])
    w_down = take("w_down", a)
    y3, dx3, loss = _mm_postnorm_res_loss(a, w_down, w["post_ffn_g"], x2, tgt, "down_proj")

    g, pending = {}, {}
    dy3, da, g["post_ffn_g"] = _postnorm_bwd_mm(dx3, y3, w["post_ffn_g"], w_down, "down_bwd", BF16)
    pending["down"], started = _exchange_begin([_slots(_dw(a, dy3, "dw_down", tka=D_FF // 2))], "down")
    (du_g, dcw_g, dcb_g), (du_v, dcw_v, dcb_v) = _conv_gate_bwd(u, conv_w, w["conv_b"], da, after=[started])
    g["conv_w"] = jnp.concatenate([dcw_g, dcw_v], axis=1)
    g["conv_b"] = jnp.concatenate([dcb_g, dcb_v], axis=1)
    pending["up"], started = _exchange_begin([_slots(_dw([du_g, du_v], h3, "dw_up", tka=D_FF // 2))], "up")
    dx2, g["pre_ffn_g"] = _mm_prenorm_bwd([du_g, du_v], w_up_t, True, x2, w["pre_ffn_g"], dx3, "up_bwd", after=[started], tm=256)

    dy2, do2, g["post_x_g"] = _postnorm_bwd_mm(dx2, y2, w["post_x_g"], w_xo, "xo_bwd", BF16)
    dw_xo = _dw(o2, dy2, "dw_xo", tka=512)
    dq2, dkv = _xattn_bwd(q2, kv, do2)
    dw_xq = _dw(h2, dq2, "dw_xq", tka=512)
    dkvb = _bf(dkv)
    dw_xkv = _dw(dkvb, mn, "dw_xkv")
    _, g["mem_norm_g"] = _mm_prenorm_bwd(dkvb, w_xkv_t, True, mem, w["mem_norm_g"], jnp.zeros_like(mem), "xkv_bwd")
    dx1, g["pre_x_g"] = _mm_prenorm_bwd(dq2, w_xq, False, x1, w["pre_x_g"], dx2, "xq_bwd")

    dmixed, dcat, g["post_mix_g"] = _postnorm_bwd_mm(dx1, mixed, w["post_mix_g"], w_out, "out_bwd")
    dw_out = _dw(cat, dmixed, "dw_out", tka=512)
    pending["x"], started = _exchange_begin([_slots(dw_xo), _slots(dw_xq), _slots(dw_xkv), _slots(dw_out)], "x")
    do, dhg, g["hg_out_norm_g"] = _rec_bwd(dcat, of, ob, p, w["hg_out_norm_g"])
    dhq_f, dz_f, dhi_f, dlb_f = _hgrn_bwd(p, lb[0:1], do, s_f, False, after=[started])
    dhq, dz_b, dhi, dlb_b = _hgrn_bwd(p, lb[1:2], do, s_b, True, other=(dhq_f, dhi_f))
    d_a0, d_a1 = _lower_bounds_bwd(lb_a0, lb_a1, jnp.concatenate([dlb_f, dlb_b], axis=0))
    g["hg_lb"] = jnp.stack([d_a0, d_a1], axis=1)
    dqr, dkr, dv = _attn_bwd(qr, kr, p, dcat, att32, lse)
    dp_att, dgq, dgk = _qk_prep_bwd(p, dqr, dkr, dv, gq2, gk2, tables, bd)
    g["q_norm_g"], g["k_norm_g"] = dgq, dgk
    dp = [dp_att, dhq, dz_f, dz_b, dhi, dhg]

    early = [n for n in _SMALL if n != "pre_mix_g"]
    vals = [g[n].reshape(-1, g[n].shape[-1]) for n in early] + [jnp.pad(loss, ((0, 0), (0, LANE - 1)))]
    small = dict(early=early, shapes=[val.shape for val in vals])
    small["fetch"], started = _gather_begin([_pack_small(vals, "pack_small")], "small")
    pending["in"], started = _exchange_begin([_slots(_dw(dp, h1, "dw_in", after=[started]))], "in")
    dx, g["pre_mix_g"] = _mm_prenorm_bwd(dp, w_in_t, True, x, w["pre_mix_g"], dx1, "in_bwd", after=[started])
    small["fetch_last"], _ = _gather_begin([_pack_small([g["pre_mix_g"]], "pack_last")], "last")
    return dx, g, pending, small


_COL_SHARDED = ("w_in", "w_xkv", "w_up")
_ROW_SHARDED = ("w_out", "w_xq", "w_xo", "w_down")
_REPLICATED = ("pre_mix_g", "q_norm_g", "k_norm_g", "hg_out_norm_g", "post_mix_g", "pre_x_g", "mem_norm_g", "post_x_g",
               "pre_ffn_g", "conv_b", "post_ffn_g")
_WEIGHTS = ("pre_mix_g", "w_in", "q_norm_g", "k_norm_g", "hg_lb", "hg_out_norm_g", "w_out", "post_mix_g", "pre_x_g",
            "mem_norm_g", "w_xq", "w_xkv", "w_xo", "post_x_g", "pre_ffn_g", "w_up", "conv_w", "conv_b", "w_down",
            "post_ffn_g")
_SMALL = _REPLICATED + ("hg_lb", "conv_w")
_ADAM_TRANSPOSED = ("w_in", "w_up")
PACK_W = 1024


def _small_plan(shapes):
    plan, r = [], 0
    for vi, (rows, cols) in enumerate(shapes):
        for i in range(rows):
            for c0 in range(0, cols, PACK_W):
                plan.append((vi, i, c0, min(PACK_W, cols - c0), r))
                r += 1
    return plan, -(-r // 8) * 8


def _pack_small(vals, name):
    plan, nrows = _small_plan([val.shape for val in vals])

    def body(*refs):
        ins, out = refs[:-1], refs[-1]
        out[...] = jnp.zeros_like(out)
        for vi, i, c0, width, r in plan:
            out[r:r + 1, 0:width] = ins[vi][i:i + 1, c0:c0 + width]

    return pl.pallas_call(body, name=name, out_shape=jax.ShapeDtypeStruct((nrows, PACK_W), F32))(*vals)


def _sum_unpack_small(packs, shapes, name):
    plan, _ = _small_plan(shapes)

    def body(*refs):
        p_ref, outs = refs[0], refs[1:]
        acc = p_ref[0]
        for i in range(1, N_DEV):
            acc = acc + p_ref[i]
        for vi, i, c0, width, r in plan:
            outs[vi][i:i + 1, c0:c0 + width] = acc[r:r + 1, 0:width]

    return pl.pallas_call(body, name=name, out_shape=[jax.ShapeDtypeStruct(s, F32) for s in shapes])(packs)


def _adamw_many(ws, gs, ms, vs):
    n = len(ws)

    def body(*refs):
        w_refs, g_refs, m_refs, v_refs = (refs[k * n:(k + 1) * n] for k in range(4))
        d_refs, nm_refs, nv_refs = (refs[(4 + k) * n:(5 + k) * n] for k in range(3))
        for k in range(n):
            g_ = g_refs[k][...]
            m_ = ADAM_B1 * m_refs[k][...] + (1.0 - ADAM_B1) * g_
            v_ = ADAM_B2 * v_refs[k][...] + (1.0 - ADAM_B2) * (g_ * g_)
            m_hat = m_ / (1.0 - ADAM_B1 ** ADAM_STEP)
            v_hat = v_ / (1.0 - ADAM_B2 ** ADAM_STEP)
            d_refs[k][...] = -ADAM_LR * (m_hat / (jnp.sqrt(v_hat) + ADAM_EPS) + ADAM_WD * w_refs[k][...])
            nm_refs[k][...] = m_
            nv_refs[k][...] = v_

    shapes = [jax.ShapeDtypeStruct(a.shape, F32) for a in ws]
    outs = pl.pallas_call(body, name="adamw_small", out_shape=shapes * 3)(*ws, *gs, *ms, *vs)
    return outs[:n], outs[n:2 * n], outs[2 * n:]


def kernel(x, mem, pre_mix_g, w_in, q_norm_g, k_norm_g, hg_lb, hg_out_norm_g, w_out, post_mix_g, pre_x_g, mem_norm_g, w_xq, w_xkv, w_xo, post_x_g, pre_ffn_g, w_up, conv_w, conv_b, w_down, post_ffn_g, loss_target, m_pre_mix_g, m_w_in, m_q_norm_g, m_k_norm_g, m_hg_lb, m_hg_out_norm_g, m_w_out, m_post_mix_g, m_pre_x_g, m_mem_norm_g, m_w_xq, m_w_xkv, m_w_xo, m_post_x_g, m_pre_ffn_g, m_w_up, m_conv_w, m_conv_b, m_w_down, m_post_ffn_g, v_pre_mix_g, v_w_in, v_q_norm_g, v_k_norm_g, v_hg_lb, v_hg_out_norm_g, v_w_out, v_post_mix_g, v_pre_x_g, v_mem_norm_g, v_w_xq, v_w_xkv, v_w_xo, v_post_x_g, v_pre_ffn_g, v_w_up, v_conv_w, v_conv_b, v_w_down, v_post_ffn_g):
    args = dict(locals())
    w = {n: args[n] for n in _WEIGHTS}
    m = {n: args["m_" + n] for n in _WEIGHTS}
    v = {n: args["v_" + n] for n in _WEIGHTS}
    me = _dev_index(*_mesh_pos())

    wire = {n: _bf(w[n][0].T) for n in _COL_SHARDED}
    wire.update({n: _bf(w[n][0]) for n in _ROW_SHARDED})

    grad_x, g, pending, small = _local_step(x[0], mem[0], loss_target[0], w, wire)

    grads, delta, new_m, new_v = {}, {}, {}, {}

    me_arr = jnp.reshape(me, (1,)).astype(jnp.int32)

    def update(n, parts):
        if n in _ADAM_TRANSPOSED:
            outs = _sum_adamw(*parts, me_arr, w[n][0].T, m[n][0].T, v[n][0].T, "update_" + n)
            grads[n], delta[n], new_m[n], new_v[n] = (a.T[None] for a in outs)
        elif n in _COL_SHARDED:
            gsum = _sum_parts(*parts, me_arr, "sum_" + n).T
            grads[n] = gsum[None]
            delta[n], new_m[n], new_v[n] = (a[None] for a in _adamw(w[n][0], gsum, m[n][0], v[n][0], "adamw_" + n))
        else:
            outs = _sum_adamw(*parts, me_arr, w[n][0], m[n][0], v[n][0], "update_" + n)
            grads[n], delta[n], new_m[n], new_v[n] = (a[None] for a in outs)

    after = grad_x
    for tag, names in (("down", ["w_down"]), ("up", ["w_up"]), ("x", ["w_xo", "w_xq", "w_xkv", "w_out"]),
                       ("in", ["w_in"])):
        for n, parts in zip(names, _exchange_end(pending[tag], after)):
            update(n, parts)
            after = new_v[n]

    (packs,) = _gather_end(small["fetch"], [0], after)
    summed = _sum_unpack_small(packs, small["shapes"], "sum_unpack_small")
    loss = summed[-1][0, 0]
    for n, s in zip(small["early"], summed[:-1]):
        grads[n] = s
    (packs,) = _gather_end(small["fetch_last"], [0], after)
    (grads["pre_mix_g"],) = _sum_unpack_small(packs, [w["pre_mix_g"].shape], "sum_unpack_last")
    small = _SMALL
    fold = lambda v2: v2[:, :ATT_HEAD_DIM] + v2[:, ATT_HEAD_DIM:]
    grads["q_norm_g"], grads["k_norm_g"] = fold(grads["q_norm_g"]), fold(grads["k_norm_g"])
    grads["hg_lb"] = lax.dynamic_slice_in_dim(grads["hg_lb"].reshape(2, 2, HG_DIM), me * (HG_DIM // N_DEV),
                                              HG_DIM // N_DEV, axis=2)
    grads["conv_w"] = lax.dynamic_slice_in_dim(grads["conv_w"], me * (2 * D_FF // N_DEV), 2 * D_FF // N_DEV, axis=1)[None]

    flat2 = lambda a: a.reshape(-1, a.shape[-1])
    outs = _adamw_many(*[[flat2(d[n]) for n in small] for d in (w, grads, m, v)])
    for dst, vals in zip((delta, new_m, new_v), outs):
        for n, val in zip(small, vals):
            dst[n] = val.reshape(w[n].shape)

    return (loss, grad_x[None], *[grads[n] for n in _WEIGHTS], *[delta[n] for n in _WEIGHTS],
            *[new_m[n] for n in _WEIGHTS], *[new_v[n] for n in _WEIGHTS])
```

```python
import jax
import jax.numpy as jnp
from jax import lax
from jax.experimental import pallas as pl
from jax.experimental.pallas import tpu as pltpu

F32 = jnp.float32
BF16 = jnp.bfloat16

D_MODEL = 1024
GRID_W = 64
EPS = 1e-6
ATT_HEADS = 8
ATT_HEAD_DIM = 64
ATT_Q_DIM = 512
ATT_KV_DIM = 128
ROPE_THETA = 10000.0
HG_HEADS = 4
HG_DIM = 512
HG_CHUNK = 32
HG_CHUNK_LOG2 = 5
HG_BLOCK_FWD = 256
HG_BLOCK_BWD = 128
N_IN = 3328
X_HEADS = 4
X_HEAD_DIM = 256
D_FF = 2816
N_DEV = 8
LANE = 128
ADAM_LR = 0.001
ADAM_B1 = 0.9
ADAM_B2 = 0.999
ADAM_EPS = 1e-08
ADAM_WD = 0.01
ADAM_STEP = 10
VMEM_LIMIT = 56 * 1024 * 1024

MESH_T = pl.DeviceIdType.MESH


def _params(**kw):
    return pltpu.CompilerParams(vmem_limit_bytes=VMEM_LIMIT, **kw)


def _dot(a, b, ca, cb):
    return lax.dot_general(a, b, (((ca,), (cb,)), ((), ())), preferred_element_type=F32)


def _bf(x):
    return x.astype(BF16)


def _sigmoid(x):
    return 1.0 / (1.0 + jnp.exp(-x))


def _rms_fwd(x, g):
    r = lax.rsqrt(jnp.mean(x * x, axis=-1, keepdims=True) + EPS)
    return x * r * g


def _rms_bwd(dy, x, g):
    r = lax.rsqrt(jnp.mean(x * x, axis=-1, keepdims=True) + EPS)
    xh = x * r
    dg = jnp.sum(dy * xh, axis=0, keepdims=True)
    t = dy * g
    dx = r * (t - xh * jnp.mean(t * xh, axis=-1, keepdims=True))
    return dx, dg


def _full(shape):
    nd = len(shape)
    return pl.BlockSpec(shape, lambda *a: (0,) * nd)


def _norm_mm(x, g, w, trans, tn, name, after=(), tm=512, p_dtype=F32):
    t, d = x.shape
    n = w.shape[0] if trans else w.shape[1]
    tm = min(tm, t)

    def body(x_ref, g_ref, w_ref, h_ref, p_ref):
        h = _bf(_rms_fwd(x_ref[...], g_ref[...]))
        h_ref[...] = h
        p_ref[...] = _dot(h, w_ref[...], 1, 1 if trans else 0).astype(p_dtype)

    w_spec = pl.BlockSpec((tn, d), lambda i, j: (j, 0)) if trans else pl.BlockSpec((d, tn), lambda i, j: (0, j))
    return _call(
        body, (x, g, w), name=name, grid=(t // tm, n // tn),
        in_specs=[pl.BlockSpec((tm, d), lambda i, j: (i, 0)), _full((1, d)), w_spec],
        out_specs=[pl.BlockSpec((tm, d), lambda i, j: (i, 0)), pl.BlockSpec((tm, tn), lambda i, j: (i, j))],
        out_shape=[jax.ShapeDtypeStruct((t, d), BF16), jax.ShapeDtypeStruct((t, n), p_dtype)], after=after)


def _mm_postnorm_res(a, w, g, res, name, tm=256):
    t, k = a.shape
    d = w.shape[1]

    def body(a_ref, w_ref, g_ref, res_ref, y_ref, o_ref):
        y = _dot(a_ref[...], w_ref[...], 1, 0)
        y_ref[...] = _bf(y)
        o_ref[...] = res_ref[...] + _rms_fwd(y, g_ref[...])

    row = lambda width: pl.BlockSpec((tm, width), lambda i: (i, 0))
    return pl.pallas_call(
        body, name=name, grid=(t // tm,),
        in_specs=[row(k), _full((k, d)), _full((1, d)), row(d)],
        out_specs=[row(d), row(d)],
        out_shape=[jax.ShapeDtypeStruct((t, d), BF16), jax.ShapeDtypeStruct((t, d), F32)],
        compiler_params=_params(),
    )(a, w, g, res)


def _mm_postnorm_res_loss(a, w, g, res, tgt, name, tm=256):
    t, k = a.shape
    d = w.shape[1]

    def body(a_ref, w_ref, g_ref, res_ref, tgt_ref, y_ref, dout_ref, loss_ref):
        @pl.when(pl.program_id(0) == 0)
        def _():
            loss_ref[...] = jnp.zeros_like(loss_ref)

        y = _dot(a_ref[...], w_ref[...], 1, 0)
        y_ref[...] = _bf(y)
        diff = res_ref[...] + _rms_fwd(y, g_ref[...]) - tgt_ref[...]
        dout_ref[...] = diff * (1.0 / d)
        part = jnp.sum(jnp.sum(diff * diff, axis=-1, keepdims=True), axis=0, keepdims=True)
        loss_ref[...] += (0.5 / d) * part

    row = lambda width: pl.BlockSpec((tm, width), lambda i: (i, 0))
    return pl.pallas_call(
        body, name=name, grid=(t // tm,),
        in_specs=[row(k), _full((k, d)), _full((1, d)), row(d), row(d)],
        out_specs=[row(d), row(d), _full((1, 1))],
        out_shape=[jax.ShapeDtypeStruct((t, d), BF16), jax.ShapeDtypeStruct((t, d), F32), jax.ShapeDtypeStruct((1, 1), F32)],
        compiler_params=_params(),
    )(a, w, g, res, tgt)


def _postnorm_bwd_mm(dout, y, g, w, name, da_dtype=F32, tm=256):
    t, d = y.shape
    k = w.shape[0]

    def body(dout_ref, y_ref, g_ref, w_ref, dy_ref, da_ref, dg_ref):
        @pl.when(pl.program_id(0) == 0)
        def _():
            dg_ref[...] = jnp.zeros_like(dg_ref)

        dy, dg = _rms_bwd(dout_ref[...], y_ref[...].astype(F32), g_ref[...])
        dg_ref[...] += dg
        dyb = _bf(dy)
        dy_ref[...] = dyb
        da_ref[...] = _dot(dyb, w_ref[...], 1, 1).astype(da_dtype)

    row = lambda width: pl.BlockSpec((tm, width), lambda i: (i, 0))
    return pl.pallas_call(
        body, name=name, grid=(t // tm,),
        in_specs=[row(d), row(d), _full((1, d)), _full((k, d))],
        out_specs=[row(d), row(k), _full((1, d))],
        out_shape=[jax.ShapeDtypeStruct((t, d), BF16), jax.ShapeDtypeStruct((t, k), da_dtype), jax.ShapeDtypeStruct((1, d), F32)],
        compiler_params=_params(),
    )(dout, y, g, w)


def _mm_prenorm_bwd(dp, w, trans, x, g, dres, name, after=(), tm=512):
    dps = list(dp) if isinstance(dp, (list, tuple)) else [dp]
    nparts = len(dps)
    t = dps[0].shape[0]
    widths = [part.shape[1] for part in dps]
    d = x.shape[1]
    tm = min(tm, t)

    def body(*refs):
        dp_refs = refs[:nparts]
        w_ref, x_ref, g_ref, dres_ref, dx_ref, dg_ref = refs[nparts:]

        @pl.when(pl.program_id(0) == 0)
        def _():
            dg_ref[...] = jnp.zeros_like(dg_ref)

        dh = None
        for k, dp_ref in enumerate(dp_refs):
            cols = slice(sum(widths[:k]), sum(widths[:k + 1]))
            term = _dot(dp_ref[...], w_ref[cols, :], 1, 0) if trans else _dot(dp_ref[...], w_ref[:, cols], 1, 1)
            dh = term if dh is None else dh + term
        dx, dg = _rms_bwd(dh, x_ref[...], g_ref[...])
        dg_ref[...] += dg
        dx_ref[...] = dres_ref[...] + dx

    row = lambda width: pl.BlockSpec((tm, width), lambda i: (i, 0))
    return _call(
        body, (*dps, w, x, g, dres), name=name, grid=(t // tm,),
        in_specs=[row(width) for width in widths] + [_full(w.shape), row(d), _full((1, d)), row(d)],
        out_specs=[row(d), _full((1, d))],
        out_shape=[jax.ShapeDtypeStruct((t, d), F32), jax.ShapeDtypeStruct((1, d), F32)], after=after)


def _dw(a, b, name, tka=256, after=()):
    parts = list(a) if isinstance(a, (list, tuple)) else [a]
    nparts = len(parts)
    t, nb = b.shape
    tiles = [part.shape[1] // tka for part in parts]
    first = [sum(tiles[:k]) for k in range(nparts)]

    def body(*refs):
        a_refs, b_ref, o_ref = refs[:nparts], refs[nparts], refs[-1]
        i = pl.program_id(0)
        for k, a_ref in enumerate(a_refs):
            @pl.when((i >= first[k]) & (i < first[k] + tiles[k]))
            def _():
                o_ref[...] = _bf(_dot(a_ref[...], b_ref[...], 0, 0))

    a_spec = lambda k: pl.BlockSpec((t, tka), lambda i: (0, jnp.clip(i - first[k], 0, tiles[k] - 1)))
    return pl.pallas_call(
        body, name=name, grid=(sum(tiles),),
        in_specs=[a_spec(k) for k in range(nparts)] + [_full((t, nb))] + [pl.BlockSpec(memory_space=pl.ANY)] * len(after),
        out_specs=pl.BlockSpec((tka, nb), lambda i: (i, 0)),
        out_shape=jax.ShapeDtypeStruct((sum(tiles) * tka, nb), BF16),
        compiler_params=_params(),
    )(*parts, b, *after)


def _rope_tables(t):
    pos = jnp.arange(t)
    r = (pos // GRID_W).astype(F32)
    c = (pos % GRID_W).astype(F32)
    npair = ATT_HEAD_DIM // 4
    inv = jnp.power(ROPE_THETA, -jnp.arange(npair, dtype=F32) / npair)
    ang = jnp.concatenate([r[:, None] * inv, c[:, None] * inv], axis=-1)
    cos = jnp.repeat(jnp.cos(ang), 2, axis=-1)
    sin = jnp.repeat(jnp.sin(ang), 2, axis=-1)
    even = (jnp.arange(ATT_HEAD_DIM) % 2) == 0
    sa = jnp.where(even, -sin, 0.0)
    sb = jnp.where(even, 0.0, sin)
    two = lambda a: jnp.tile(a, (1, 2))
    return two(cos), two(sa), two(sb)


def _head_sum_matrix():
    a = jnp.arange(LANE) // ATT_HEAD_DIM
    return (a[:, None] == a[None, :]).astype(BF16)


def _head_mean(v, bd):
    hi = _bf(v)
    lo = _bf(v - hi.astype(F32))
    return (_dot(hi, bd, 1, 0) + _dot(lo, bd, 1, 0)) * (1.0 / ATT_HEAD_DIM)


def _qk_prep(p, gq, gk, tables, bd, tm=512):
    t = p.shape[0]
    tm = min(tm, t)
    cc, sa, sb = tables

    def body(p_ref, gq_ref, gk_ref, cc_ref, sa_ref, sb_ref, bd_ref, q_ref, k_ref):
        cc_, sa_, sb_, bd_ = cc_ref[...], sa_ref[...], sb_ref[...], bd_ref[...]
        low = lax.broadcasted_iota(jnp.int32, (tm, LANE), 1) < ATT_HEAD_DIM

        def normrope(xs, g):
            xn = xs * lax.rsqrt(_head_mean(xs * xs, bd_) + EPS) * g
            return xn * cc_ + pltpu.roll(xn, LANE - 1, 1) * sa_ + pltpu.roll(xn, 1, 1) * sb_

        for j in range(4):
            y = normrope(p_ref[:, j * LANE:(j + 1) * LANE], gq_ref[...]) * (ATT_HEAD_DIM ** -0.5)
            yr = pltpu.roll(y, ATT_HEAD_DIM, 1)
            if j // 2 == 0:
                h0, h1 = jnp.where(low, y, 0.0), jnp.where(low, yr, 0.0)
            else:
                h0, h1 = jnp.where(low, 0.0, yr), jnp.where(low, 0.0, y)
            q_ref[:, (2 * j) * LANE:(2 * j + 1) * LANE] = _bf(h0)
            q_ref[:, (2 * j + 1) * LANE:(2 * j + 2) * LANE] = _bf(h1)
        k_ref[...] = _bf(normrope(p_ref[:, ATT_Q_DIM:ATT_Q_DIM + LANE], gk_ref[...]))

    row = lambda width: pl.BlockSpec((tm, width), lambda i: (i, 0))
    return pl.pallas_call(
        body, name="qk_prep", grid=(t // tm,),
        in_specs=[row(ATT_Q_DIM + LANE), _full((1, LANE)), _full((1, LANE)), row(LANE), row(LANE), row(LANE),
                  _full((LANE, LANE))],
        out_specs=[row(ATT_HEADS * LANE), row(LANE)],
        out_shape=[jax.ShapeDtypeStruct((t, ATT_HEADS * LANE), BF16), jax.ShapeDtypeStruct((t, LANE), BF16)],
        compiler_params=_params(),
    )(p, gq, gk, cc, sa, sb, bd)


def _qk_prep_bwd(p, dq, dk, dv, gq, gk, tables, bd, tm=512):
    t = p.shape[0]
    tm = min(tm, t)
    cc, sa, sb = tables
    att_w = ATT_Q_DIM + 2 * ATT_KV_DIM

    def body(p_ref, dq_ref, dk_ref, dv_ref, gq_ref, gk_ref, cc_ref, sa_ref, sb_ref, bd_ref, dp_ref, dgq_ref, dgk_ref):
        dp_ref[:, ATT_Q_DIM + LANE:att_w] = _bf(dv_ref[...])

        @pl.when(pl.program_id(0) == 0)
        def _():
            dgq_ref[...] = jnp.zeros_like(dgq_ref)
            dgk_ref[...] = jnp.zeros_like(dgk_ref)

        cc_, sa_, sb_, bd_ = cc_ref[...], sa_ref[...], sb_ref[...], bd_ref[...]
        low = lax.broadcasted_iota(jnp.int32, (tm, LANE), 1) < ATT_HEAD_DIM

        def bwd(xs, g, dy):
            r = lax.rsqrt(_head_mean(xs * xs, bd_) + EPS)
            xh = xs * r
            dxn = dy * cc_ + pltpu.roll(dy * sa_, 1, 1) + pltpu.roll(dy * sb_, LANE - 1, 1)
            dg = jnp.sum(dxn * xh, axis=0, keepdims=True)
            tt = dxn * g
            return r * (tt - xh * _head_mean(tt * xh, bd_)), dg

        dgq = jnp.zeros((1, LANE), F32)
        for j in range(4):
            d0 = dq_ref[:, (2 * j) * LANE:(2 * j + 1) * LANE]
            d1 = dq_ref[:, (2 * j + 1) * LANE:(2 * j + 2) * LANE]
            if j // 2 == 0:
                dy = jnp.where(low, d0, pltpu.roll(d1, ATT_HEAD_DIM, 1))
            else:
                dy = jnp.where(low, pltpu.roll(d0, ATT_HEAD_DIM, 1), d1)
            dx, dg = bwd(p_ref[:, j * LANE:(j + 1) * LANE], gq_ref[...], dy * (ATT_HEAD_DIM ** -0.5))
            dp_ref[:, j * LANE:(j + 1) * LANE] = _bf(dx)
            dgq = dgq + dg
        dgq_ref[...] += dgq
        dx, dg = bwd(p_ref[:, ATT_Q_DIM:ATT_Q_DIM + LANE], gk_ref[...], dk_ref[...])
        dp_ref[:, ATT_Q_DIM:ATT_Q_DIM + LANE] = _bf(dx)
        dgk_ref[...] += dg

    row = lambda width: pl.BlockSpec((tm, width), lambda i: (i, 0))
    return pl.pallas_call(
        body, name="qk_prep_bwd", grid=(t // tm,),
        in_specs=[row(ATT_Q_DIM + LANE), row(ATT_HEADS * LANE), row(LANE), row(LANE), _full((1, LANE)), _full((1, LANE)),
                  row(LANE), row(LANE), row(LANE), _full((LANE, LANE))],
        out_specs=[row(att_w), _full((1, LANE)), _full((1, LANE))],
        out_shape=[jax.ShapeDtypeStruct((t, att_w), BF16), jax.ShapeDtypeStruct((1, LANE), F32),
                   jax.ShapeDtypeStruct((1, LANE), F32)],
        compiler_params=_params(),
    )(p, dq, dk, dv, gq, gk, cc, sa, sb, bd)


def _attn_fwd(q, k, p, tq=256):
    t = k.shape[0]
    tq = min(tq, t)
    v_blk = (ATT_Q_DIM + ATT_KV_DIM) // LANE

    def body(q_ref, k_ref, v_ref, o_ref, o32_ref, lse_ref):
        k_ = k_ref[...]
        v = v_ref[...]
        lane_k = lax.broadcasted_iota(jnp.int32, (t, LANE), 1)
        lane_q = lax.broadcasted_iota(jnp.int32, (tq, LANE), 1)
        lowk, lowq = lane_k < ATT_HEAD_DIM, lane_q < ATT_HEAD_DIM
        ones_lane = (ATT_HEAD_DIM, 0)
        vm = (_bf(jnp.where(lowk, v, jnp.where(lane_k == ones_lane[0], 1.0, 0.0))),
              _bf(jnp.where(lowk, jnp.where(lane_k == ones_lane[1], 1.0, 0.0), v)))
        for j in range(4):
            kvh = j // 2
            acc = None
            for sub in range(2):
                h = 2 * j + sub
                s = _dot(q_ref[:, h * LANE:(h + 1) * LANE], k_, 1, 1)
                mx = jnp.max(s, axis=-1, keepdims=True)
                ov = _dot(jnp.exp(_bf(s - mx)), vm[kvh], 1, 0)
                l = jnp.sum(jnp.where(lane_q == ones_lane[kvh], ov, 0.0), axis=-1, keepdims=True)
                lse_ref[h] = mx + jnp.log(l)
                o = jnp.where(lowq if kvh == 0 else ~lowq, ov, 0.0) * (1.0 / l)
                if sub != kvh:
                    o = pltpu.roll(o, ATT_HEAD_DIM, 1)
                acc = o if acc is None else acc + o
            o32_ref[:, j * LANE:(j + 1) * LANE] = acc
            o_ref[:, j * LANE:(j + 1) * LANE] = _bf(acc)

    row = pl.BlockSpec((tq, ATT_Q_DIM), lambda i: (i, 0))
    return _call(
        body, (q, k, p), name="attn_fwd", grid=(t // tq,),
        in_specs=[pl.BlockSpec((tq, ATT_HEADS * LANE), lambda i: (i, 0)), _full((t, LANE)),
                  pl.BlockSpec((t, LANE), lambda i: (0, v_blk))],
        out_specs=[row, row, pl.BlockSpec((ATT_HEADS, tq, 1), lambda i: (0, i, 0))],
        out_shape=[jax.ShapeDtypeStruct((t, ATT_Q_DIM), BF16), jax.ShapeDtypeStruct((t, ATT_Q_DIM), F32),
                   jax.ShapeDtypeStruct((ATT_HEADS, t, 1), F32)])


def _attn_bwd(q, k, p, dcat, o32, lse, tq=256):
    t = k.shape[0]
    tq = min(tq, t)
    v_blk = (ATT_Q_DIM + ATT_KV_DIM) // LANE

    def body(q_ref, k_ref, v_ref, do_ref, o_ref, lse_ref, dq_ref, dk_ref, dv_ref):
        @pl.when(pl.program_id(0) == 0)
        def _():
            dk_ref[...] = jnp.zeros_like(dk_ref)
            dv_ref[...] = jnp.zeros_like(dv_ref)

        k_ = k_ref[...]
        vb = _bf(v_ref[...])
        lowq = lax.broadcasted_iota(jnp.int32, (tq, LANE), 1) < ATT_HEAD_DIM
        dk_acc = jnp.zeros((t, LANE), F32)
        dv_acc = jnp.zeros((t, LANE), F32)
        for j in range(4):
            kvh = j // 2
            dop = do_ref[:, j * LANE:(j + 1) * LANE]
            prod = dop * o_ref[:, j * LANE:(j + 1) * LANE]
            d_low = jnp.sum(jnp.where(lowq, prod, 0.0), axis=-1, keepdims=True)
            d_sub = (d_low, jnp.sum(prod, axis=-1, keepdims=True) - d_low)
            for sub in range(2):
                h = 2 * j + sub
                src = dop if sub == kvh else pltpu.roll(dop, ATT_HEAD_DIM, 1)
                do_h = _bf(jnp.where(lowq, src, 0.0) if kvh == 0 else jnp.where(lowq, 0.0, src))
                qh = q_ref[:, h * LANE:(h + 1) * LANE]
                pr = jnp.exp(_bf(_dot(qh, k_, 1, 1) - lse_ref[h]))
                ds = pr * _bf(_dot(do_h, vb, 1, 1) - d_sub[sub])
                dq_ref[:, h * LANE:(h + 1) * LANE] = _dot(ds, k_, 1, 0)
                dk_acc = dk_acc + _dot(ds, qh, 0, 0)
                dv_acc = dv_acc + _dot(pr, do_h, 0, 0)
        dk_ref[...] += dk_acc
        dv_ref[...] += dv_acc

    row = pl.BlockSpec((tq, ATT_Q_DIM), lambda i: (i, 0))
    return _call(
        body, (q, k, p, dcat, o32, lse), name="attn_bwd", grid=(t // tq,),
        in_specs=[pl.BlockSpec((tq, ATT_HEADS * LANE), lambda i: (i, 0)), _full((t, LANE)),
                  pl.BlockSpec((t, LANE), lambda i: (0, v_blk)), row, row,
                  pl.BlockSpec((ATT_HEADS, tq, 1), lambda i: (0, i, 0))],
        out_specs=[pl.BlockSpec((tq, ATT_HEADS * LANE), lambda i: (i, 0)), _full((t, LANE)), _full((t, LANE))],
        out_shape=[jax.ShapeDtypeStruct((t, ATT_HEADS * LANE), F32), jax.ShapeDtypeStruct((t, LANE), F32),
                   jax.ShapeDtypeStruct((t, LANE), F32)])


def _lower_bounds(a0, a1):
    def body(a0_ref, a1_ref, lb_ref):
        m = jnp.maximum(a0_ref[...], a1_ref[...])
        e0, e1 = jnp.exp(a0_ref[...] - m), jnp.exp(a1_ref[...] - m)
        lb_ref[...] = e0 / (e0 + e1)

    return pl.pallas_call(body, name="lower_bounds", out_shape=jax.ShapeDtypeStruct(a0.shape, F32))(a0, a1)


def _lower_bounds_bwd(a0, a1, dlb):
    def body(a0_ref, a1_ref, dlb_ref, d0_ref, d1_ref):
        m = jnp.maximum(a0_ref[...], a1_ref[...])
        e0, e1 = jnp.exp(a0_ref[...] - m), jnp.exp(a1_ref[...] - m)
        lb = e0 / (e0 + e1)
        d0 = dlb_ref[...] * lb * (1.0 - lb)
        d0_ref[...] = d0
        d1_ref[...] = -d0

    return pl.pallas_call(body, name="lower_bounds_bwd", out_shape=[jax.ShapeDtypeStruct(a0.shape, F32)] * 2)(a0, a1, dlb)


def _chunk_scan(x, pos, down):
    n = x.shape[0]
    s = 1
    while s < HG_CHUNK:
        if down:
            x = x + jnp.where(pos >= s, pltpu.roll(x, s, 0), 0.0)
        else:
            x = x + jnp.where(pos < HG_CHUNK - s, pltpu.roll(x, n - s, 0), 0.0)
        s *= 2
    return x


def _chunk_concat(x, nc):
    xb = _bf(x)
    rows = []
    for c in range(nc):
        pieces = [xb[c * HG_CHUNK:(c + 1) * HG_CHUNK, :]]
        if c:
            pieces.insert(0, jnp.zeros((HG_CHUNK, c * LANE), BF16))
        if c < nc - 1:
            pieces.append(jnp.zeros((HG_CHUNK, (nc - 1 - c) * LANE), BF16))
        rows.append(jnp.concatenate(pieces, axis=1) if len(pieces) > 1 else pieces[0])
    return jnp.concatenate(rows, axis=0) if nc > 1 else rows[0]


def _chunk_pick(x, nc):
    rows = [x[c * HG_CHUNK:(c + 1) * HG_CHUNK, c * LANE:(c + 1) * LANE] for c in range(nc)]
    return jnp.concatenate(rows, axis=0) if nc > 1 else rows[0]


def _chunk_rows(vals):
    rows = [jnp.broadcast_to(v, (HG_CHUNK, LANE)) for v in vals]
    return jnp.concatenate(rows, axis=0) if len(rows) > 1 else rows[0]


def _hgrn_gates(hq, z, lb):
    sq = _sigmoid(hq)
    sig = _sigmoid(z)
    f = lb + (1.0 - lb) * sig
    return hq * sq, sq, sig, f, jnp.log(f)


def _hgrn_local(q, f, g, v, pos, amask, rev, nc):
    k = 1.0 - f
    ba = _chunk_scan(g, pos, not rev)
    bb = _chunk_scan(g, pos, rev)
    eq = 0.5 * (ba - bb + g)
    ex_q, ex_k, ex_i, ex_d = jnp.exp(eq), jnp.exp(-eq), jnp.exp(ba), jnp.exp(bb - g)
    qb, kb, qi, kd = q * ex_q, k * ex_k, q * ex_i, k * ex_d
    dvec = jnp.exp(ba + bb - g)
    a = jnp.where(amask, _dot(_bf(qb), _bf(kb), 1, 1), 0.0)
    kd_m = _chunk_concat(kd, nc)
    qi_m = _chunk_concat(qi, nc)
    ut_all = _dot(_bf(v), kd_m, 0, 0)
    return dict(k=k, ex_q=ex_q, ex_k=ex_k, ex_i=ex_i, ex_d=ex_d, qb=qb, kb=kb, qi=qi, kd=kd, dvec=dvec, a=a,
                kd_m=kd_m, qi_m=qi_m, ut_all=ut_all)


def _hgrn_masks(n, rev):
    row = lax.broadcasted_iota(jnp.int32, (n, LANE), 0)
    ti = lax.broadcasted_iota(jnp.int32, (n, n), 0)
    si = lax.broadcasted_iota(jnp.int32, (n, n), 1)
    tri = (si >= ti) if rev else (si <= ti)
    same = jnp.right_shift(ti, HG_CHUNK_LOG2) == jnp.right_shift(si, HG_CHUNK_LOG2)
    return jnp.bitwise_and(row, HG_CHUNK - 1), same & tri


def _hgrn_specs(t, rev, bwd):
    n = min(HG_BLOCK_BWD if bwd else HG_BLOCK_FWD, t)
    nb = t // n
    if rev != bwd:
        blk = lambda i: nb - 1 - i
    else:
        blk = lambda i: i
    col = lambda base: pl.BlockSpec((n, 2 * LANE), lambda hp, i: (blk(i), base + hp))
    return n, nb, blk, col


def _hgrn_fwd(p, lb, rev, after=()):
    t = p.shape[0]
    n, nb, blk, col = _hgrn_specs(t, rev, False)
    nc = n // HG_CHUNK
    sub = n // min(HG_BLOCK_BWD, t)
    z_base = 7 if rev else 5

    def body(hq_ref, z_ref, hi_ref, lb_ref, o_ref, ssave_ref, st_scr):
        @pl.when(pl.program_id(1) == 0)
        def _():
            st_scr[...] = jnp.zeros_like(st_scr)

        pos, amask = _hgrn_masks(n, rev)
        order = list(range(nc))[::-1] if rev else list(range(nc))
        for hh in range(2):
            sl = slice(hh * LANE, (hh + 1) * LANE)
            q, _, _, f, g = _hgrn_gates(hq_ref[:, sl], z_ref[:, sl], lb_ref[0:1, sl])
            v = hi_ref[:, sl]
            c_ = _hgrn_local(q, f, g, v, pos, amask, rev, nc)
            st = st_scr[hh]
            cols = [None] * nc
            for c in order:
                cols[c] = st
                st = st * c_["dvec"][c * HG_CHUNK:c * HG_CHUNK + 1, :] + c_["ut_all"][:, c * LANE:(c + 1) * LANE]
            st_scr[hh] = st
            for s in range(sub):
                ssave_ref[sub - 1 - s if rev else s, sl, :] = cols[order[s * (nc // sub)]]
            st_all = _bf(jnp.concatenate(cols, axis=1))
            o_ref[:, sl] = _dot(_bf(c_["a"]), _bf(v), 1, 0) + _dot(c_["qi_m"], st_all, 1, 1)

    return _call(
        body, (p, p, p, lb), name="hgrn_fwd_rev" if rev else "hgrn_fwd", grid=(2, nb),
        in_specs=[col(3), col(z_base), col(9), pl.BlockSpec((1, 2 * LANE), lambda hp, i: (0, hp))],
        out_specs=[pl.BlockSpec((n, 2 * LANE), lambda hp, i: (blk(i), hp)),
                   pl.BlockSpec((sub, 2 * LANE, LANE), lambda hp, i: (blk(i), hp, 0))],
        out_shape=[jax.ShapeDtypeStruct((t, HG_DIM), F32), jax.ShapeDtypeStruct((nb * sub, HG_DIM, LANE), F32)],
        scratch_shapes=[pltpu.VMEM((2, LANE, LANE), F32)], after=after)


def _hgrn_bwd(p, lb, do, ssave, rev, other=None, after=()):
    t = p.shape[0]
    n, nb, blk, col = _hgrn_specs(t, rev, True)
    nc = n // HG_CHUNK
    z_base = 7 if rev else 5
    n_other = 0 if other is None else 2

    def body(*refs):
        hq_ref, z_ref, hi_ref, lb_ref, do_ref, ssave_ref = refs[:6]
        dhq_ref, dz_ref, dhi_ref, dlb_ref, dst_scr = refs[6 + n_other:]

        @pl.when(pl.program_id(1) == 0)
        def _():
            dst_scr[...] = jnp.zeros_like(dst_scr)
            dlb_ref[...] = jnp.zeros_like(dlb_ref)

        pos, amask = _hgrn_masks(n, rev)
        order = list(range(nc))[::-1] if rev else list(range(nc))
        for hh in range(2):
            sl = slice(hh * LANE, (hh + 1) * LANE)
            hq, lbv = hq_ref[:, sl], lb_ref[0:1, sl]
            q, sq, sig, f, g = _hgrn_gates(hq, z_ref[:, sl], lbv)
            v = hi_ref[:, sl]
            c_ = _hgrn_local(q, f, g, v, pos, amask, rev, nc)
            dvec, ut_all = c_["dvec"], c_["ut_all"]
            drow = lambda c: dvec[c * HG_CHUNK:c * HG_CHUNK + 1, :]
            st = ssave_ref[0, sl, :]
            cols = [None] * nc
            for c in order:
                cols[c] = st
                st = st * drow(c) + ut_all[:, c * LANE:(c + 1) * LANE]
            dob, vb = _bf(do_ref[:, sl]), _bf(v)
            gt_all = _dot(dob, c_["qi_m"], 0, 0)
            dcur = dst_scr[hh]
            dnext = [None] * nc
            ddrow = [None] * nc
            for c in order[::-1]:
                dnext[c] = dcur
                ddrow[c] = jnp.sum(cols[c] * dcur, axis=0, keepdims=True) * drow(c)
                dcur = dcur * drow(c) + gt_all[:, c * LANE:(c + 1) * LANE]
            dst_scr[hh] = dcur
            dsn_all = _bf(jnp.concatenate(dnext, axis=1))
            st_all = _bf(jnp.concatenate(cols, axis=1))
            da = _bf(jnp.where(amask, _dot(dob, vb, 1, 1), 0.0))
            dv = _dot(_bf(c_["a"]), dob, 0, 0) + _dot(c_["kd_m"], dsn_all, 1, 1)
            dqb = _dot(da, _bf(c_["kb"]), 1, 0)
            dkb = _dot(da, _bf(c_["qb"]), 0, 0)
            dqi = _chunk_pick(_dot(dob, st_all, 1, 0), nc)
            dkd = _chunk_pick(_dot(vb, dsn_all, 1, 0), nc)
            dq = dqb * c_["ex_q"] + dqi * c_["ex_i"]
            dk = dkb * c_["ex_k"] + dkd * c_["ex_d"]
            e = dqb * c_["qb"] - dkb * c_["kb"] + dqi * c_["qi"]
            w = dkd * c_["kd"]
            dg = _chunk_scan(e, pos, rev) + (_chunk_scan(w, pos, not rev) - w) + _chunk_rows(ddrow)
            df = dg / f - dk
            dz_ref[:, sl] = _bf(df * (1.0 - lbv) * sig * (1.0 - sig))
            dlb_ref[0:1, sl] += jnp.sum(df * (1.0 - sig), axis=0, keepdims=True)
            dhq = dq * (sq * (1.0 + hq * (1.0 - sq)))
            if other is None:
                dhq_ref[:, sl], dhi_ref[:, sl] = dhq, dv
            else:
                dhq_ref[:, sl], dhi_ref[:, sl] = _bf(dhq + refs[6][:, sl]), _bf(dv + refs[7][:, sl])

    out = pl.BlockSpec((n, 2 * LANE), lambda hp, i: (blk(i), hp))
    sum_dtype = F32 if other is None else BF16
    return _call(
        body, (p, p, p, lb, do, ssave, *(other or ())), name="hgrn_bwd_rev" if rev else "hgrn_bwd", grid=(2, nb),
        in_specs=[col(3), col(z_base), col(9), pl.BlockSpec((1, 2 * LANE), lambda hp, i: (0, hp)), out,
                  pl.BlockSpec((1, 2 * LANE, LANE), lambda hp, i: (blk(i), hp, 0))] + [out] * n_other,
        out_specs=[out, out, out, pl.BlockSpec((1, 2 * LANE), lambda hp, i: (0, hp))],
        out_shape=[jax.ShapeDtypeStruct((t, HG_DIM), sum_dtype), jax.ShapeDtypeStruct((t, HG_DIM), BF16),
                   jax.ShapeDtypeStruct((t, HG_DIM), sum_dtype), jax.ShapeDtypeStruct((1, HG_DIM), F32)],
        scratch_shapes=[pltpu.VMEM((2, LANE, LANE), F32)], after=after)


def _mix_out(att, of, ob, p, gout, w_out, g_post, x, after=(), tm=256):
    t, d = x.shape
    tm = min(tm, t)

    def body(att_ref, of_ref, ob_ref, hg0_ref, hg1_ref, gout_ref, w_ref, g_ref, x_ref, cat_ref, y_ref, x1_ref):
        cat_ref[:, :ATT_Q_DIM] = att_ref[...]
        for h in range(HG_HEADS):
            sl = slice(h * LANE, (h + 1) * LANE)
            hg_ref = hg0_ref if h < 2 else hg1_ref
            hg = hg_ref[:, (h % 2) * LANE:(h % 2 + 1) * LANE]
            nrm = _rms_fwd(of_ref[:, sl] + ob_ref[:, sl], gout_ref[...])
            cat_ref[:, ATT_Q_DIM + h * LANE:ATT_Q_DIM + (h + 1) * LANE] = _bf(nrm * hg * _sigmoid(hg))
        y = _dot(cat_ref[...], w_ref[...], 1, 0)
        y_ref[...] = _bf(y)
        x1_ref[...] = x_ref[...] + _rms_fwd(y, g_ref[...])

    row = lambda width: pl.BlockSpec((tm, width), lambda i: (i, 0))
    return _call(
        body, (att, of, ob, p, p, gout, w_out, g_post, x), name="mix_out", grid=(t // tm,),
        in_specs=[row(ATT_Q_DIM), row(HG_DIM), row(HG_DIM), pl.BlockSpec((tm, 2 * LANE), lambda i: (i, 11)),
                  pl.BlockSpec((tm, 2 * LANE), lambda i: (i, 12)), _full((1, LANE)), _full((d, d)), _full((1, d)), row(d)],
        out_specs=[row(d), row(d), row(d)],
        out_shape=[jax.ShapeDtypeStruct((t, d), BF16), jax.ShapeDtypeStruct((t, d), BF16), jax.ShapeDtypeStruct((t, d), F32)],
        after=after)


def _rec_bwd(dcat, of, ob, p, gout, tm=256):
    t = of.shape[0]
    tm = min(tm, t)

    def body(dc_ref, of_ref, ob_ref, hg0_ref, hg1_ref, gout_ref, do_ref, dhg_ref, dgo_ref):
        @pl.when(pl.program_id(0) == 0)
        def _():
            dgo_ref[...] = jnp.zeros_like(dgo_ref)

        dgo = jnp.zeros((1, LANE), F32)
        for h in range(HG_HEADS):
            sl = slice(h * LANE, (h + 1) * LANE)
            hg_ref = hg0_ref if h < 2 else hg1_ref
            hg = hg_ref[:, (h % 2) * LANE:(h % 2 + 1) * LANE]
            o = of_ref[:, sl] + ob_ref[:, sl]
            sg = _sigmoid(hg)
            drec = dc_ref[:, sl]
            dhg_ref[:, sl] = _bf(drec * _rms_fwd(o, gout_ref[...]) * (sg * (1.0 + hg * (1.0 - sg))))
            do, dg = _rms_bwd(drec * hg * sg, o, gout_ref[...])
            do_ref[:, sl] = do
            dgo = dgo + dg
        dgo_ref[...] += dgo

    row = lambda width: pl.BlockSpec((tm, width), lambda i: (i, 0))
    return pl.pallas_call(
        body, name="rec_bwd", grid=(t // tm,),
        in_specs=[pl.BlockSpec((tm, HG_DIM), lambda i: (i, 1)), row(HG_DIM), row(HG_DIM),
                  pl.BlockSpec((tm, 2 * LANE), lambda i: (i, 11)), pl.BlockSpec((tm, 2 * LANE), lambda i: (i, 12)),
                  _full((1, LANE))],
        out_specs=[row(HG_DIM), row(HG_DIM), _full((1, LANE))],
        out_shape=[jax.ShapeDtypeStruct((t, HG_DIM), F32), jax.ShapeDtypeStruct((t, HG_DIM), BF16),
                   jax.ShapeDtypeStruct((1, LANE), F32)],
        compiler_params=_params(),
    )(dcat, of, ob, p, p, gout)


def _xattn_fwd(q, kv, tq=512):
    t, d = q.shape
    tq = min(tq, t)
    nm = kv.shape[0]

    def body(q_ref, kv_ref, o_ref):
        for h in range(X_HEADS):
            sl = slice(h * X_HEAD_DIM, (h + 1) * X_HEAD_DIM)
            s = _dot(_bf(q_ref[:, sl]), _bf(kv_ref[:, sl]), 1, 1) * (X_HEAD_DIM ** -0.5)
            e = jnp.exp(s - jnp.max(s, axis=-1, keepdims=True))
            pr = e * (1.0 / jnp.sum(e, axis=-1, keepdims=True))
            o_ref[:, sl] = _bf(_dot(_bf(pr), _bf(kv_ref[:, d + h * X_HEAD_DIM:d + (h + 1) * X_HEAD_DIM]), 1, 0))

    return pl.pallas_call(
        body, name="xattn_fwd", grid=(t // tq,),
        in_specs=[pl.BlockSpec((tq, d), lambda i: (i, 0)), _full((nm, 2 * d))],
        out_specs=pl.BlockSpec((tq, d), lambda i: (i, 0)),
        out_shape=jax.ShapeDtypeStruct((t, d), BF16),
        compiler_params=_params(),
    )(q, kv)


def _xattn_bwd(q, kv, do, tq=512):
    t, d = q.shape
    tq = min(tq, t)
    nm = kv.shape[0]

    def body(q_ref, kv_ref, do_ref, dq_ref, dkv_ref):
        @pl.when(pl.program_id(0) == 0)
        def _():
            dkv_ref[...] = jnp.zeros_like(dkv_ref)

        for h in range(X_HEADS):
            sl = slice(h * X_HEAD_DIM, (h + 1) * X_HEAD_DIM)
            slv = slice(d + h * X_HEAD_DIM, d + (h + 1) * X_HEAD_DIM)
            qb, kb, vb, dob = _bf(q_ref[:, sl]), _bf(kv_ref[:, sl]), _bf(kv_ref[:, slv]), _bf(do_ref[:, sl])
            s = _dot(qb, kb, 1, 1) * (X_HEAD_DIM ** -0.5)
            e = jnp.exp(s - jnp.max(s, axis=-1, keepdims=True))
            pr = e * (1.0 / jnp.sum(e, axis=-1, keepdims=True))
            dpr = _dot(dob, vb, 1, 1)
            ds = _bf(pr * (dpr - jnp.sum(pr * dpr, axis=-1, keepdims=True)) * (X_HEAD_DIM ** -0.5))
            dq_ref[:, sl] = _bf(_dot(ds, kb, 1, 0))
            dkv_ref[:, sl] += _dot(ds, qb, 0, 0)
            dkv_ref[:, slv] += _dot(_bf(pr), dob, 0, 0)

    return pl.pallas_call(
        body, name="xattn_bwd", grid=(t // tq,),
        in_specs=[pl.BlockSpec((tq, d), lambda i: (i, 0)), _full((nm, 2 * d)), pl.BlockSpec((tq, d), lambda i: (i, 0))],
        out_specs=[pl.BlockSpec((tq, d), lambda i: (i, 0)), _full((nm, 2 * d))],
        out_shape=[jax.ShapeDtypeStruct((t, d), BF16), jax.ShapeDtypeStruct((nm, 2 * d), F32)],
        compiler_params=_params(),
    )(q, kv, do)


CONV_TN = 256


def _shift_rows(u, row, t, delta):
    if delta < 0:
        return jnp.where(row == 0, 0.0, pltpu.roll(u, 1, 0))
    return jnp.where(row == t - 1, 0.0, pltpu.roll(u, t - 1, 0))


def _conv_gate_fwd(u, conv_w, conv_b):
    t = u.shape[0]
    nt = D_FF // CONV_TN

    def body(ug_ref, uv_ref, wg_ref, wv_ref, bg_ref, bv_ref, a_ref):
        row = lax.broadcasted_iota(jnp.int32, (t, CONV_TN), 0)

        def conv(u_ref, w_ref, b_ref):
            uu = u_ref[...]
            return (b_ref[...] + _shift_rows(uu, row, t, -1) * w_ref[0:1, :] + uu * w_ref[1:2, :]
                    + _shift_rows(uu, row, t, 1) * w_ref[2:3, :])

        gate = conv(ug_ref, wg_ref, bg_ref)
        a_ref[...] = _bf(gate * _sigmoid(gate) * conv(uv_ref, wv_ref, bv_ref))

    col = lambda rows, off: pl.BlockSpec((rows, CONV_TN), lambda j: (0, j + off))
    return pl.pallas_call(
        body, name="conv_gate_fwd", grid=(nt,),
        in_specs=[col(t, 0), col(t, nt), col(3, 0), col(3, nt), col(1, 0), col(1, nt)],
        out_specs=col(t, 0),
        out_shape=jax.ShapeDtypeStruct((t, D_FF), BF16),
        compiler_params=_params(),
    )(u, u, conv_w, conv_w, conv_b, conv_b)


def _conv_gate_bwd(u, conv_w, conv_b, da, after=()):
    t = u.shape[0]
    nt = D_FF // CONV_TN

    def body(ug_ref, uv_ref, wg_ref, wv_ref, bg_ref, bv_ref, da_ref, dug_ref, dwg_ref, dbg_ref, duv_ref, dwv_ref, dbv_ref):
        row = lax.broadcasted_iota(jnp.int32, (t, CONV_TN), 0)
        ug, uv = ug_ref[...], uv_ref[...]
        ug_m, ug_p = _shift_rows(ug, row, t, -1), _shift_rows(ug, row, t, 1)
        uv_m, uv_p = _shift_rows(uv, row, t, -1), _shift_rows(uv, row, t, 1)
        gate = bg_ref[...] + ug_m * wg_ref[0:1, :] + ug * wg_ref[1:2, :] + ug_p * wg_ref[2:3, :]
        val = bv_ref[...] + uv_m * wv_ref[0:1, :] + uv * wv_ref[1:2, :] + uv_p * wv_ref[2:3, :]
        sg = _sigmoid(gate)
        da_ = da_ref[...].astype(F32)

        def emit(dacc, um, uu, up, w_ref, du_ref, dw_ref, db_ref):
            du_ref[...] = _bf(_shift_rows(dacc, row, t, 1) * w_ref[0:1, :] + dacc * w_ref[1:2, :]
                              + _shift_rows(dacc, row, t, -1) * w_ref[2:3, :])
            dw_ref[0:1, :] = jnp.sum(dacc * um, axis=0, keepdims=True)
            dw_ref[1:2, :] = jnp.sum(dacc * uu, axis=0, keepdims=True)
            dw_ref[2:3, :] = jnp.sum(dacc * up, axis=0, keepdims=True)
            db_ref[...] = jnp.sum(dacc, axis=0, keepdims=True)

        emit(da_ * val * (sg * (1.0 + gate * (1.0 - sg))), ug_m, ug, ug_p, wg_ref, dug_ref, dwg_ref, dbg_ref)
        emit(da_ * gate * sg, uv_m, uv, uv_p, wv_ref, duv_ref, dwv_ref, dbv_ref)

    col = lambda rows, off: pl.BlockSpec((rows, CONV_TN), lambda j: (0, j + off))
    half_shapes = [jax.ShapeDtypeStruct((t, D_FF), BF16), jax.ShapeDtypeStruct((3, D_FF), F32),
                   jax.ShapeDtypeStruct((1, D_FF), F32)]
    outs = _call(
        body, (u, u, conv_w, conv_w, conv_b, conv_b, da), name="conv_gate_bwd", grid=(nt,),
        in_specs=[col(t, 0), col(t, nt), col(3, 0), col(3, nt), col(1, 0), col(1, nt), col(t, 0)],
        out_specs=[col(t, 0), col(3, 0), col(1, 0)] * 2, out_shape=half_shapes * 2, after=after)
    return outs[:3], outs[3:]


def _row_tile(r, cap):
    best = r
    for cand in range(16, cap + 1, 16):
        if r % cand == 0:
            best = cand
    return best


def _sum_parts(own, got, me, name, tr=256):
    _, r, c = got.shape
    tr = _row_tile(r, tr)

    def body(me_ref, own_ref, got_ref, o_ref):
        mine = own_ref[...].astype(F32)
        acc = None
        for i in range(N_DEV):
            term = jnp.where(me_ref[0] == i, mine, got_ref[i].astype(F32))
            acc = term if acc is None else acc + term
        o_ref[...] = acc

    return pl.pallas_call(
        body, name=name,
        grid_spec=pltpu.PrefetchScalarGridSpec(
            num_scalar_prefetch=1, grid=(r // tr,),
            in_specs=[pl.BlockSpec((None, tr, c), lambda i, me_ref: (me_ref[0], i, 0)),
                      pl.BlockSpec((N_DEV, tr, c), lambda i, me_ref: (0, i, 0))],
            out_specs=pl.BlockSpec((tr, c), lambda i, me_ref: (i, 0))),
        out_shape=jax.ShapeDtypeStruct((r, c), F32),
        compiler_params=_params(),
    )(me, own, got)


def _adamw_math(w_, g_, m_, v_):
    m_ = ADAM_B1 * m_ + (1.0 - ADAM_B1) * g_
    v_ = ADAM_B2 * v_ + (1.0 - ADAM_B2) * (g_ * g_)
    m_hat = m_ / (1.0 - ADAM_B1 ** ADAM_STEP)
    v_hat = v_ / (1.0 - ADAM_B2 ** ADAM_STEP)
    return -ADAM_LR * (m_hat / (jnp.sqrt(v_hat) + ADAM_EPS) + ADAM_WD * w_), m_, v_


def _sum_adamw(own, got, me, w, m, v, name, tr=256):
    _, r, c = got.shape
    tr = _row_tile(r, tr)

    def body(me_ref, own_ref, got_ref, w_ref, m_ref, v_ref, g_ref, d_ref, nm_ref, nv_ref):
        mine = own_ref[...].astype(F32)
        acc = None
        for i in range(N_DEV):
            term = jnp.where(me_ref[0] == i, mine, got_ref[i].astype(F32))
            acc = term if acc is None else acc + term
        g_ref[...] = acc
        d_ref[...], nm_ref[...], nv_ref[...] = _adamw_math(w_ref[...], acc, m_ref[...], v_ref[...])

    blk = pl.BlockSpec((tr, c), lambda i, me_ref: (i, 0))
    return pl.pallas_call(
        body, name=name,
        grid_spec=pltpu.PrefetchScalarGridSpec(
            num_scalar_prefetch=1, grid=(r // tr,),
            in_specs=[pl.BlockSpec((None, tr, c), lambda i, me_ref: (me_ref[0], i, 0)),
                      pl.BlockSpec((N_DEV, tr, c), lambda i, me_ref: (0, i, 0)), blk, blk, blk],
            out_specs=[blk] * 4),
        out_shape=[jax.ShapeDtypeStruct((r, c), F32)] * 4,
        compiler_params=_params(),
    )(me, own, got, w, m, v)


def _adamw(w, g, m, v, name, tr=256):
    r, c = w.shape
    tr = _row_tile(r, tr)

    def body(w_ref, g_ref, m_ref, v_ref, d_ref, nm_ref, nv_ref):
        d_ref[...], nm_ref[...], nv_ref[...] = _adamw_math(w_ref[...], g_ref[...], m_ref[...], v_ref[...])

    blk = pl.BlockSpec((tr, c), lambda i: (i, 0))
    return pl.pallas_call(
        body, name=name, grid=(r // tr,),
        in_specs=[blk] * 4, out_specs=[blk] * 3,
        out_shape=[jax.ShapeDtypeStruct((r, c), F32)] * 3,
        compiler_params=_params(),
    )(w, g, m, v)


def _mesh_pos():
    return lax.axis_index("x"), lax.axis_index("y"), lax.axis_index("c")


def _dev_index(px, py, pc):
    return 4 * px + 2 * py + pc


class _Gather:
    def __init__(self, arrs):
        self.arrs = list(arrs)
        n = len(self.arrs)
        self.out_shape = [jax.ShapeDtypeStruct((N_DEV,) + a.shape, a.dtype) for a in self.arrs]
        self.sems = [pltpu.SemaphoreType.DMA((7, n)), pltpu.SemaphoreType.DMA((7, n)), pltpu.SemaphoreType.DMA((n,))]

    def _ctx(self, ins, outs, sems):
        send_sems, recv_sems, local_sems = sems
        x, y, c = _mesh_pos()
        chips = [(1 - x, y), (x, 1 - y), (1 - x, 1 - y)]

        def copy(k, a, block, to, src=None):
            dst = outs[a].at[_dev_index(*block)]
            return pltpu.make_async_remote_copy(
                src_ref=dst if src is None else src, dst_ref=dst, send_sem=send_sems.at[k, a], recv_sem=recv_sems.at[k, a],
                device_id=to, device_id_type=MESH_T)

        n = len(ins)
        me, sibling = (x, y, c), (x, y, 1 - c)
        mine = [pltpu.make_async_copy(ins[a], outs[a].at[_dev_index(*me)], local_sems.at[a]) for a in range(n)]
        first = [copy(0, a, me, sibling, src=ins[a]) for a in range(n)]
        first += [copy(1 + j, a, me, (*chip, c), src=ins[a]) for j, chip in enumerate(chips) for a in range(n)]
        passed = [copy(4 + j, a, (*chip, c), sibling) for j, chip in enumerate(chips) for a in range(n)]
        return n, c, me, sibling, chips, copy, mine, first, passed

    def start(self, ins, outs, sems):
        _, _, _, _, _, _, mine, first, _ = self._ctx(ins, outs, sems)
        for cp in mine + first:
            cp.start()

    def forward(self, ins, outs, sems):
        n, c, me, _, chips, copy, _, _, passed = self._ctx(ins, outs, sems)
        for j, chip in enumerate(chips):
            for a in range(n):
                copy(1 + j, a, (*chip, c), me).wait_recv()
                passed[j * n + a].start()

    def finish(self, ins, outs, sems):
        n, c, me, sibling, chips, copy, mine, first, passed = self._ctx(ins, outs, sems)
        for a in range(n):
            copy(0, a, sibling, me).wait_recv()
        for j, chip in enumerate(chips):
            for a in range(n):
                copy(4 + j, a, (*chip, 1 - c), me).wait_recv()
        for cp in first + passed:
            cp.wait_send()
        for cp in mine:
            cp.wait()


def _comm_alone(comm, name):
    n = len(comm.arrs)

    def body(*refs):
        ins, outs, sems = refs[:n], refs[n:2 * n], refs[2 * n:]
        comm.start(ins, outs, sems)
        if comm.forward is not None:
            comm.forward(ins, outs, sems)
        comm.finish(ins, outs, sems)

    any_spec = pl.BlockSpec(memory_space=pl.ANY)
    return pl.pallas_call(body, name=name, in_specs=[any_spec] * n, out_specs=[any_spec] * n, out_shape=comm.out_shape,
                          scratch_shapes=comm.sems)(*comm.arrs)


def _peers(x, y, c):
    out = []
    for k in range(1, N_DEV):
        pos = (1 - x if k & 4 else x, 1 - y if k & 2 else y, 1 - c if k & 1 else c)
        out.append((k, pos, _dev_index(*pos)))
    return out


def _exchange_begin(arrs, tag):
    n = len(arrs)
    lands = [lax.empty(a.shape, a.dtype) for a in arrs]

    def start_body(*refs):
        ins, land = refs[:n], refs[n:2 * n]
        send_sems, recv_sems, token = refs[2 * n], refs[2 * n + 1], refs[-1]
        x, y, c = _mesh_pos()
        me_i = _dev_index(x, y, c)
        for k, pos, peer_i in _peers(x, y, c):
            for a in range(n):
                pltpu.make_async_remote_copy(
                    src_ref=ins[a].at[peer_i], dst_ref=land[a].at[me_i], send_sem=send_sems.at[(k - 1) * n + a],
                    recv_sem=recv_sems.at[(k - 1) * n + a], device_id=pos, device_id_type=MESH_T).start()
        token[...] = jnp.zeros_like(token)

    hbm = pl.BlockSpec(memory_space=pltpu.HBM)
    sem = pl.BlockSpec(memory_space=pltpu.SEMAPHORE)
    thru = [pltpu.HBM(a.shape, a.dtype) for a in arrs]
    outs = pl.pallas_call(
        start_body, name="exchange_start_" + tag,
        out_shape=[pltpu.SemaphoreType.DMA((7 * n,)), pltpu.SemaphoreType.DMA((7 * n,))] + thru + thru
        + [jax.ShapeDtypeStruct((8, LANE), F32)],
        in_specs=[hbm] * (2 * n), out_specs=[sem, sem] + [hbm] * (2 * n) + [pl.BlockSpec(memory_space=pltpu.VMEM)],
        input_output_aliases={i: 2 + i for i in range(2 * n)},
        compiler_params=pltpu.CompilerParams(has_side_effects=pltpu.SideEffectType.DATAFLOW_SIDE_EFFECTING),
    )(*[pltpu.with_memory_space_constraint(a, pltpu.HBM) for a in list(arrs) + list(lands)])
    return (tag, n, outs[0], outs[1], outs[2:2 + n], outs[2 + n:2 + 2 * n]), outs[-1]


def _exchange_end(handle, after):
    tag, n, send_sems, recv_sems, srcs, lands = handle

    def body(*refs):
        ins, land = refs[:n], refs[n:2 * n]
        send_sems_, recv_sems_ = refs[2 * n], refs[2 * n + 1]
        x, y, c = _mesh_pos()
        me_i = _dev_index(x, y, c)
        for k, pos, peer_i in _peers(x, y, c):
            for a in range(n):
                cp = pltpu.make_async_remote_copy(
                    src_ref=ins[a].at[peer_i], dst_ref=land[a].at[me_i], send_sem=send_sems_.at[(k - 1) * n + a],
                    recv_sem=recv_sems_.at[(k - 1) * n + a], device_id=pos, device_id_type=MESH_T)
                cp.wait_send()
                cp.wait_recv()

    hbm = pl.BlockSpec(memory_space=pltpu.HBM)
    sem = pl.BlockSpec(memory_space=pltpu.SEMAPHORE)
    outs = pl.pallas_call(
        body, name="exchange_end_" + tag, out_shape=[pltpu.HBM(a.shape, a.dtype) for a in list(srcs) + list(lands)],
        in_specs=[hbm] * (2 * n) + [sem, sem, pl.BlockSpec(memory_space=pl.ANY)], out_specs=[hbm] * (2 * n),
        input_output_aliases={i: i for i in range(2 * n)},
        compiler_params=pltpu.CompilerParams(has_side_effects=pltpu.SideEffectType.DATAFLOW_SIDE_EFFECTING),
    )(*srcs, *lands, send_sems, recv_sems, after)
    return list(zip(outs[:n], outs[n:]))


def _gather_begin(arrs, tag):
    n = len(arrs)
    me = _dev_index(*_mesh_pos())
    lands = [lax.dynamic_update_slice(lax.empty((N_DEV,) + a.shape, a.dtype), a[None], (me,) + (0,) * a.ndim) for a in arrs]

    def start_body(*refs):
        ins, land = refs[:n], refs[n:2 * n]
        send_sems, recv_sems, token = refs[2 * n], refs[2 * n + 1], refs[-1]
        x, y, c = _mesh_pos()
        me_i = _dev_index(x, y, c)
        for a in range(n):
            for k, pos, _ in _peers(x, y, c):
                pltpu.make_async_remote_copy(
                    src_ref=ins[a], dst_ref=land[a].at[me_i], send_sem=send_sems.at[(k - 1) * n + a],
                    recv_sem=recv_sems.at[(k - 1) * n + a], device_id=pos, device_id_type=MESH_T).start()
        token[...] = jnp.zeros_like(token)

    hbm = pl.BlockSpec(memory_space=pltpu.HBM)
    sem = pl.BlockSpec(memory_space=pltpu.SEMAPHORE)
    thru = [pltpu.HBM(a.shape, a.dtype) for a in list(arrs) + lands]
    outs = pl.pallas_call(
        start_body, name="gather_start_" + tag,
        out_shape=[pltpu.SemaphoreType.DMA((7 * n,)), pltpu.SemaphoreType.DMA((7 * n,))] + thru
        + [jax.ShapeDtypeStruct((8, LANE), F32)],
        in_specs=[hbm] * (2 * n), out_specs=[sem, sem] + [hbm] * (2 * n) + [pl.BlockSpec(memory_space=pltpu.VMEM)],
        input_output_aliases={i: 2 + i for i in range(2 * n)},
        compiler_params=pltpu.CompilerParams(has_side_effects=pltpu.SideEffectType.DATAFLOW_SIDE_EFFECTING),
    )(*[pltpu.with_memory_space_constraint(a, pltpu.HBM) for a in list(arrs) + lands])
    return (tag, n, outs[0], outs[1], outs[2:2 + n], outs[2 + n:2 + 2 * n]), outs[-1]


def _gather_end(handle, which, after):
    tag, n, send_sems, recv_sems, srcs, lands = handle
    m = len(which)

    def body(*refs):
        ins, land = refs[:m], refs[m:2 * m]
        send_sems_, recv_sems_ = refs[2 * m], refs[2 * m + 1]
        x, y, c = _mesh_pos()
        me_i = _dev_index(x, y, c)
        for j, a in enumerate(which):
            for k, pos, _ in _peers(x, y, c):
                cp = pltpu.make_async_remote_copy(
                    src_ref=ins[j], dst_ref=land[j].at[me_i], send_sem=send_sems_.at[(k - 1) * n + a],
                    recv_sem=recv_sems_.at[(k - 1) * n + a], device_id=pos, device_id_type=MESH_T)
                cp.wait_send()
                cp.wait_recv()

    hbm = pl.BlockSpec(memory_space=pltpu.HBM)
    sem = pl.BlockSpec(memory_space=pltpu.SEMAPHORE)
    ops = [srcs[a] for a in which] + [lands[a] for a in which]
    outs = pl.pallas_call(
        body, name="gather_end_%s_%s" % (tag, "_".join(str(a) for a in which)),
        out_shape=[pltpu.HBM(a.shape, a.dtype) for a in ops],
        in_specs=[hbm] * (2 * m) + [sem, sem, pl.BlockSpec(memory_space=pl.ANY)], out_specs=[hbm] * (2 * m),
        input_output_aliases={i: i for i in range(2 * m)},
        compiler_params=pltpu.CompilerParams(has_side_effects=pltpu.SideEffectType.DATAFLOW_SIDE_EFFECTING),
    )(*ops, send_sems, recv_sems, after)
    return list(outs[m:])


_SPLIT = dict(has_side_effects=pltpu.SideEffectType.DATAFLOW_SIDE_EFFECTING)


def _chips(x, y):
    return [(1 - x, y), (x, 1 - y), (1 - x, 1 - y)]


def _gather2_begin(arrs, tag):
    n = len(arrs)
    me = _dev_index(*_mesh_pos())
    lands = [lax.dynamic_update_slice(lax.empty((N_DEV,) + a.shape, a.dtype), a[None], (me,) + (0,) * a.ndim) for a in arrs]

    def body(*refs):
        ins, land = refs[:n], refs[n:2 * n]
        send1, recv1, token = refs[2 * n], refs[2 * n + 1], refs[-1]
        x, y, c = _mesh_pos()
        me_i = _dev_index(x, y, c)
        targets = [(x, y, 1 - c)] + [(*chip, c) for chip in _chips(x, y)]
        for a in range(n):
            for k, pos in enumerate(targets):
                pltpu.make_async_remote_copy(
                    src_ref=ins[a], dst_ref=land[a].at[me_i], send_sem=send1.at[k * n + a], recv_sem=recv1.at[k * n + a],
                    device_id=pos, device_id_type=MESH_T).start()
        token[...] = jnp.zeros_like(token)

    hbm = pl.BlockSpec(memory_space=pltpu.HBM)
    sem = pl.BlockSpec(memory_space=pltpu.SEMAPHORE)
    thru = [pltpu.HBM(a.shape, a.dtype) for a in list(arrs) + lands]
    outs = pl.pallas_call(
        body, name="gather_start_" + tag,
        out_shape=[pltpu.SemaphoreType.DMA((4 * n,)), pltpu.SemaphoreType.DMA((4 * n,))] + thru
        + [jax.ShapeDtypeStruct((8, LANE), F32)],
        in_specs=[hbm] * (2 * n), out_specs=[sem, sem] + [hbm] * (2 * n) + [pl.BlockSpec(memory_space=pltpu.VMEM)],
        input_output_aliases={i: 2 + i for i in range(2 * n)}, compiler_params=pltpu.CompilerParams(**_SPLIT),
    )(*[pltpu.with_memory_space_constraint(a, pltpu.HBM) for a in list(arrs) + lands])
    return dict(tag=tag, n=n, send1=outs[0], recv1=outs[1], srcs=list(outs[2:2 + n]), lands=list(outs[2 + n:2 + 2 * n])), outs[-1]


def _gather2_forward(handle, which, after, part):
    n, m = handle["n"], len(which)

    def body(*refs):
        land, recv1 = refs[:m], refs[m]
        send2, recv2, token = refs[m + 2], refs[m + 3], refs[-1]
        token[...] = jnp.zeros_like(token)
        x, y, c = _mesh_pos()
        for j, a in enumerate(which):
            for k, chip in enumerate(_chips(x, y)):
                blk = land[j].at[_dev_index(*chip, c)]
                copy = pltpu.make_async_remote_copy(
                    src_ref=blk, dst_ref=blk, send_sem=send2.at[k * m + j], recv_sem=recv2.at[k * m + j],
                    device_id=(x, y, 1 - c), device_id_type=MESH_T)
                pltpu.make_async_remote_copy(
                    src_ref=blk, dst_ref=blk, send_sem=send2.at[k * m + j], recv_sem=recv1.at[(1 + k) * n + a],
                    device_id=(*chip, c), device_id_type=MESH_T).wait_recv()
                copy.start()

    hbm = pl.BlockSpec(memory_space=pltpu.HBM)
    sem = pl.BlockSpec(memory_space=pltpu.SEMAPHORE)
    lands = [handle["lands"][a] for a in which]
    outs = pl.pallas_call(
        body, name="gather_forward_%s_%s" % (handle["tag"], part),
        out_shape=[pltpu.SemaphoreType.DMA((3 * m,)), pltpu.SemaphoreType.DMA((3 * m,))] + [pltpu.HBM(a.shape, a.dtype) for a in lands]
        + [jax.ShapeDtypeStruct((8, LANE), F32)],
        in_specs=[hbm] * m + [sem, pl.BlockSpec(memory_space=pl.ANY)],
        out_specs=[sem, sem] + [hbm] * m + [pl.BlockSpec(memory_space=pltpu.VMEM)],
        input_output_aliases={i: 2 + i for i in range(m)}, compiler_params=pltpu.CompilerParams(**_SPLIT),
    )(*lands, handle["recv1"], after)
    for j, a in enumerate(which):
        handle["lands"][a] = outs[2 + j]
        handle[("leg2", a)] = (outs[0], outs[1], j, m)
    return outs[-1]


def _gather2_end(handle, a, after):
    n = handle["n"]
    send2, recv2, j, m = handle[("leg2", a)]

    def body(src, land, send1, recv1, send2_, recv2_, after_ref, src_out, land_out):
        x, y, c = _mesh_pos()
        me_i = _dev_index(x, y, c)
        sibling = (x, y, 1 - c)
        for k, pos in enumerate([sibling] + [(*chip, c) for chip in _chips(x, y)]):
            first = pltpu.make_async_remote_copy(
                src_ref=src, dst_ref=land.at[me_i], send_sem=send1.at[k * n + a], recv_sem=recv1.at[k * n + a],
                device_id=pos, device_id_type=MESH_T)
            first.wait_send()
            if k == 0:
                first.wait_recv()
        for k in range(3):
            second = pltpu.make_async_remote_copy(
                src_ref=land.at[me_i], dst_ref=land.at[me_i], send_sem=send2_.at[k * m + j], recv_sem=recv2_.at[k * m + j],
                device_id=sibling, device_id_type=MESH_T)
            second.wait_send()
            second.wait_recv()

    hbm = pl.BlockSpec(memory_space=pltpu.HBM)
    sem = pl.BlockSpec(memory_space=pltpu.SEMAPHORE)
    src, land = handle["srcs"][a], handle["lands"][a]
    outs = pl.pallas_call(
        body, name="gather_end_%s_%d" % (handle["tag"], a),
        out_shape=[pltpu.HBM(src.shape, src.dtype), pltpu.HBM(land.shape, land.dtype)],
        in_specs=[hbm, hbm, sem, sem, sem, sem, pl.BlockSpec(memory_space=pl.ANY)], out_specs=[hbm, hbm],
        input_output_aliases={0: 0, 1: 1}, compiler_params=pltpu.CompilerParams(**_SPLIT),
    )(src, land, handle["send1"], handle["recv1"], send2, recv2, after)
    return outs[1]


def _call(body, operands, *, name, grid, in_specs, out_specs, out_shape, scratch_shapes=(), after=()):
    ni, nd = len(in_specs), len(after)

    def ordered(*refs):
        body(*refs[:ni], *refs[ni + nd:])

    outs = pl.pallas_call(
        ordered, name=name, grid=grid, in_specs=list(in_specs) + [pl.BlockSpec(memory_space=pl.ANY)] * nd,
        out_specs=out_specs, out_shape=out_shape, scratch_shapes=list(scratch_shapes), compiler_params=_params(),
    )(*operands, *after)
    return list(outs)


def _rows(a):
    return a.reshape(-1, a.shape[-1])


def _slots(a):
    return a.reshape(N_DEV, -1, a.shape[-1])


def _local_step(x, mem, tgt, w, wire):
    tables = _rope_tables(x.shape[0])
    bd = _head_sum_matrix()
    two = lambda g: jnp.tile(g, (1, 2))
    gq2, gk2 = two(w["q_norm_g"]), two(w["k_norm_g"])

    w_in_t, hg_lb, conv_w = _comm_alone(_Gather([wire["w_in"], w["hg_lb"].reshape(4, -1), w["conv_w"][0]]), "gather_w_in")
    w_in_t = _rows(w_in_t)
    hg_lb = jnp.transpose(hg_lb.reshape(N_DEV, 2, 2, -1), (1, 2, 0, 3)).reshape(2, 2, HG_DIM)
    conv_w = jnp.transpose(conv_w, (1, 0, 2)).reshape(3, 2 * D_FF)
    lb_a0, lb_a1 = hg_lb[:, 0, :], hg_lb[:, 1, :]

    order = ("w_out", "w_xq", "w_xkv", "w_xo", "w_up", "w_down")
    w_in_t, *later = lax.optimization_barrier((w_in_t, *[wire[n] for n in order]))
    fetch, started = _gather2_begin(later, "w")
    take = lambda n, after: _rows(_gather2_end(fetch, order.index(n), after))

    h1, p = _norm_mm(x, w["pre_mix_g"], w_in_t, True, N_IN, "in_proj", after=[started])
    qr, kr = _qk_prep(p, gq2, gk2, tables, bd)
    att, att32, lse = _attn_fwd(qr, kr, p)
    passed = _gather2_forward(fetch, [0, 1, 2, 3], att, "a")
    lb = _lower_bounds(lb_a0, lb_a1)
    of, s_f = _hgrn_fwd(p, lb[0:1], False, after=[passed])
    ob, s_b = _hgrn_fwd(p, lb[1:2], True, after=[passed, of])
    passed = _gather2_forward(fetch, [4, 5], ob, "b")
    w_out = take("w_out", ob)
    cat, mixed, x1 = _mix_out(att, of, ob, p, w["hg_out_norm_g"], w_out, w["post_mix_g"], x, after=[passed])

    w_xq = take("w_xq", x1)
    h2, q2 = _norm_mm(x1, w["pre_x_g"], w_xq, False, 1024, "xq_proj", p_dtype=BF16)
    w_xkv_t = take("w_xkv", q2)
    mn, kv = _norm_mm(mem, w["mem_norm_g"], w_xkv_t, True, 2 * D_MODEL, "xkv_proj")
    o2 = _xattn_fwd(q2, kv)
    w_xo = take("w_xo", o2)
    y2, x2 = _mm_postnorm_res(o2, w_xo, w["post_x_g"], x1, "xo_proj")

    w_up_t = take("w_up", x2)
    h3, u = _norm_mm(x2, w["pre_ffn_g"], w_up_t, True, 2 * D_FF, "up_proj", tm=256)
    a = _conv_gate_fwd(u, conv_w, w["conv_b"])
    w_down = take("w_down", a)
    y3, dx3, loss = _mm_postnorm_res_loss(a, w_down, w["post_ffn_g"], x2, tgt, "down_proj")

    g, pending = {}, {}
    dy3, da, g["post_ffn_g"] = _postnorm_bwd_mm(dx3, y3, w["post_ffn_g"], w_down, "down_bwd", BF16)
    pending["down"], started = _exchange_begin([_slots(_dw(a, dy3, "dw_down", tka=D_FF // 2))], "down")
    (du_g, dcw_g, dcb_g), (du_v, dcw_v, dcb_v) = _conv_gate_bwd(u, conv_w, w["conv_b"], da, after=[started])
    g["conv_w"] = jnp.concatenate([dcw_g, dcw_v], axis=1)
    g["conv_b"] = jnp.concatenate([dcb_g, dcb_v], axis=1)
    pending["up"], started = _exchange_begin([_slots(_dw([du_g, du_v], h3, "dw_up", tka=D_FF // 2))], "up")
    dx2, g["pre_ffn_g"] = _mm_prenorm_bwd([du_g, du_v], w_up_t, True, x2, w["pre_ffn_g"], dx3, "up_bwd", after=[started], tm=256)

    dy2, do2, g["post_x_g"] = _postnorm_bwd_mm(dx2, y2, w["post_x_g"], w_xo, "xo_bwd", BF16)
    dw_xo = _dw(o2, dy2, "dw_xo", tka=512)
    dq2, dkv = _xattn_bwd(q2, kv, do2)
    dw_xq = _dw(h2, dq2, "dw_xq", tka=512)
    dkvb = _bf(dkv)
    dw_xkv = _dw(dkvb, mn, "dw_xkv")
    _, g["mem_norm_g"] = _mm_prenorm_bwd(dkvb, w_xkv_t, True, mem, w["mem_norm_g"], jnp.zeros_like(mem), "xkv_bwd")
    dx1, g["pre_x_g"] = _mm_prenorm_bwd(dq2, w_xq, False, x1, w["pre_x_g"], dx2, "xq_bwd")

    dmixed, dcat, g["post_mix_g"] = _postnorm_bwd_mm(dx1, mixed, w["post_mix_g"], w_out, "out_bwd")
    dw_out = _dw(cat, dmixed, "dw_out", tka=512)
    pending["x"], started = _exchange_begin([_slots(dw_xo), _slots(dw_xq), _slots(dw_xkv), _slots(dw_out)], "x")
    do, dhg, g["hg_out_norm_g"] = _rec_bwd(dcat, of, ob, p, w["hg_out_norm_g"])
    dhq_f, dz_f, dhi_f, dlb_f = _hgrn_bwd(p, lb[0:1], do, s_f, False, after=[started])
    dhq, dz_b, dhi, dlb_b = _hgrn_bwd(p, lb[1:2], do, s_b, True, other=(dhq_f, dhi_f))
    d_a0, d_a1 = _lower_bounds_bwd(lb_a0, lb_a1, jnp.concatenate([dlb_f, dlb_b], axis=0))
    g["hg_lb"] = jnp.stack([d_a0, d_a1], axis=1)
    dqr, dkr, dv = _attn_bwd(qr, kr, p, dcat, att32, lse)
    dp_att, dgq, dgk = _qk_prep_bwd(p, dqr, dkr, dv, gq2, gk2, tables, bd)
    g["q_norm_g"], g["k_norm_g"] = dgq, dgk
    dp = [dp_att, dhq, dz_f, dz_b, dhi, dhg]

    early = [n for n in _SMALL if n != "pre_mix_g"]
    vals = [g[n].reshape(-1, g[n].shape[-1]) for n in early] + [jnp.pad(loss, ((0, 0), (0, LANE - 1)))]
    small = dict(early=early, shapes=[val.shape for val in vals])
    small["fetch"], started = _gather_begin([_pack_small(vals, "pack_small")], "small")
    pending["in"], started = _exchange_begin([_slots(_dw(dp, h1, "dw_in", after=[started]))], "in")
    dx, g["pre_mix_g"] = _mm_prenorm_bwd(dp, w_in_t, True, x, w["pre_mix_g"], dx1, "in_bwd", after=[started])
    small["fetch_last"], _ = _gather_begin([_pack_small([g["pre_mix_g"]], "pack_last")], "last")
    return dx, g, pending, small


_COL_SHARDED = ("w_in", "w_xkv", "w_up")
_ROW_SHARDED = ("w_out", "w_xq", "w_xo", "w_down")
_REPLICATED = ("pre_mix_g", "q_norm_g", "k_norm_g", "hg_out_norm_g", "post_mix_g", "pre_x_g", "mem_norm_g", "post_x_g",
               "pre_ffn_g", "conv_b", "post_ffn_g")
_WEIGHTS = ("pre_mix_g", "w_in", "q_norm_g", "k_norm_g", "hg_lb", "hg_out_norm_g", "w_out", "post_mix_g", "pre_x_g",
            "mem_norm_g", "w_xq", "w_xkv", "w_xo", "post_x_g", "pre_ffn_g", "w_up", "conv_w", "conv_b", "w_down",
            "post_ffn_g")
_SMALL = _REPLICATED + ("hg_lb", "conv_w")
_ADAM_TRANSPOSED = ("w_in", "w_up")
PACK_W = 1024


def _small_plan(shapes):
    plan, r = [], 0
    for vi, (rows, cols) in enumerate(shapes):
        for i in range(rows):
            for c0 in range(0, cols, PACK_W):
                plan.append((vi, i, c0, min(PACK_W, cols - c0), r))
                r += 1
    return plan, -(-r // 8) * 8


def _pack_small(vals, name):
    plan, nrows = _small_plan([val.shape for val in vals])

    def body(*refs):
        ins, out = refs[:-1], refs[-1]
        out[...] = jnp.zeros_like(out)
        for vi, i, c0, width, r in plan:
            out[r:r + 1, 0:width] = ins[vi][i:i + 1, c0:c0 + width]

    return pl.pallas_call(body, name=name, out_shape=jax.ShapeDtypeStruct((nrows, PACK_W), F32))(*vals)


def _sum_unpack_small(packs, shapes, name):
    plan, _ = _small_plan(shapes)

    def body(*refs):
        p_ref, outs = refs[0], refs[1:]
        acc = p_ref[0]
        for i in range(1, N_DEV):
            acc = acc + p_ref[i]
        for vi, i, c0, width, r in plan:
            outs[vi][i:i + 1, c0:c0 + width] = acc[r:r + 1, 0:width]

    return pl.pallas_call(body, name=name, out_shape=[jax.ShapeDtypeStruct(s, F32) for s in shapes])(packs)


def _adamw_many(ws, gs, ms, vs):
    n = len(ws)

    def body(*refs):
        w_refs, g_refs, m_refs, v_refs = (refs[k * n:(k + 1) * n] for k in range(4))
        d_refs, nm_refs, nv_refs = (refs[(4 + k) * n:(5 + k) * n] for k in range(3))
        for k in range(n):
            g_ = g_refs[k][...]
            m_ = ADAM_B1 * m_refs[k][...] + (1.0 - ADAM_B1) * g_
            v_ = ADAM_B2 * v_refs[k][...] + (1.0 - ADAM_B2) * (g_ * g_)
            m_hat = m_ / (1.0 - ADAM_B1 ** ADAM_STEP)
            v_hat = v_ / (1.0 - ADAM_B2 ** ADAM_STEP)
            d_refs[k][...] = -ADAM_LR * (m_hat / (jnp.sqrt(v_hat) + ADAM_EPS) + ADAM_WD * w_refs[k][...])
            nm_refs[k][...] = m_
            nv_refs[k][...] = v_

    shapes = [jax.ShapeDtypeStruct(a.shape, F32) for a in ws]
    outs = pl.pallas_call(body, name="adamw_small", out_shape=shapes * 3)(*ws, *gs, *ms, *vs)
    return outs[:n], outs[n:2 * n], outs[2 * n:]


def kernel(x, mem, pre_mix_g, w_in, q_norm_g, k_norm_g, hg_lb, hg_out_norm_g, w_out, post_mix_g, pre_x_g, mem_norm_g, w_xq, w_xkv, w_xo, post_x_g, pre_ffn_g, w_up, conv_w, conv_b, w_down, post_ffn_g, loss_target, m_pre_mix_g, m_w_in, m_q_norm_g, m_k_norm_g, m_hg_lb, m_hg_out_norm_g, m_w_out, m_post_mix_g, m_pre_x_g, m_mem_norm_g, m_w_xq, m_w_xkv, m_w_xo, m_post_x_g, m_pre_ffn_g, m_w_up, m_conv_w, m_conv_b, m_w_down, m_post_ffn_g, v_pre_mix_g, v_w_in, v_q_norm_g, v_k_norm_g, v_hg_lb, v_hg_out_norm_g, v_w_out, v_post_mix_g, v_pre_x_g, v_mem_norm_g, v_w_xq, v_w_xkv, v_w_xo, v_post_x_g, v_pre_ffn_g, v_w_up, v_conv_w, v_conv_b, v_w_down, v_post_ffn_g):
    args = dict(locals())
    w = {n: args[n] for n in _WEIGHTS}
    m = {n: args["m_" + n] for n in _WEIGHTS}
    v = {n: args["v_" + n] for n in _WEIGHTS}
    me = _dev_index(*_mesh_pos())

    wire = {n: _bf(w[n][0].T) for n in _COL_SHARDED}
    wire.update({n: _bf(w[n][0]) for n in _ROW_SHARDED})

    grad_x, g, pending, small = _local_step(x[0], mem[0], loss_target[0], w, wire)

    grads, delta, new_m, new_v = {}, {}, {}, {}

    me_arr = jnp.reshape(me, (1,)).astype(jnp.int32)

    def update(n, parts):
        if n in _ADAM_TRANSPOSED:
            outs = _sum_adamw(*parts, me_arr, w[n][0].T, m[n][0].T, v[n][0].T, "update_" + n)
            grads[n], delta[n], new_m[n], new_v[n] = (a.T[None] for a in outs)
        elif n in _COL_SHARDED:
            gsum = _sum_parts(*parts, me_arr, "sum_" + n).T
            grads[n] = gsum[None]
            delta[n], new_m[n], new_v[n] = (a[None] for a in _adamw(w[n][0], gsum, m[n][0], v[n][0], "adamw_" + n))
        else:
            outs = _sum_adamw(*parts, me_arr, w[n][0], m[n][0], v[n][0], "update_" + n)
            grads[n], delta[n], new_m[n], new_v[n] = (a[None] for a in outs)

    after = grad_x
    for tag, names in (("down", ["w_down"]), ("up", ["w_up"]), ("x", ["w_xo", "w_xq", "w_xkv", "w_out"]),
                       ("in", ["w_in"])):
        for n, parts in zip(names, _exchange_end(pending[tag], after)):
            update(n, parts)
            after = new_v[n]

    (packs,) = _gather_end(small["fetch"], [0], after)
    summed = _sum_unpack_small(packs, small["shapes"], "sum_unpack_small")
    loss = summed[-1][0, 0]
    for n, s in zip(small["early"], summed[:-1]):
        grads[n] = s
    (packs,) = _gather_end(small["fetch_last"], [0], after)
    (grads["pre_mix_g"],) = _sum_unpack_small(packs, [w["pre_mix_g"].shape], "sum_unpack_last")
    small = _SMALL
    fold = lambda v2: v2[:, :ATT_HEAD_DIM] + v2[:, ATT_HEAD_DIM:]
    grads["q_norm_g"], grads["k_norm_g"] = fold(grads["q_norm_g"]), fold(grads["k_norm_g"])
    grads["hg_lb"] = lax.dynamic_slice_in_dim(grads["hg_lb"].reshape(2, 2, HG_DIM), me * (HG_DIM // N_DEV),
                                              HG_DIM // N_DEV, axis=2)
    grads["conv_w"] = lax.dynamic_slice_in_dim(grads["conv_w"], me * (2 * D_FF // N_DEV), 2 * D_FF // N_DEV, axis=1)[None]

    flat2 = lambda a: a.reshape(-1, a.shape[-1])
    outs = _adamw_many(*[[flat2(d[n]) for n in small] for d in (w, grads, m, v)])
    for dst, vals in zip((delta, new_m, new_v), outs):
        for n, val in zip(small, vals):
            dst[n] = val.reshape(w[n].shape)

    return (loss, grad_x[None], *[grads[n] for n in _WEIGHTS], *[delta[n] for n in _WEIGHTS],
            *[new_m[n] for n in _WEIGHTS], *[new_v[n] for n in _WEIGHTS])
```

```python
import jax
import jax.numpy as jnp
from jax import lax
from jax.experimental import pallas as pl
from jax.experimental.pallas import tpu as pltpu

F32 = jnp.float32
BF16 = jnp.bfloat16

D_MODEL = 1024
GRID_W = 64
EPS = 1e-6
ATT_HEADS = 8
ATT_HEAD_DIM = 64
ATT_Q_DIM = 512
ATT_KV_DIM = 128
ROPE_THETA = 10000.0
HG_HEADS = 4
HG_DIM = 512
HG_CHUNK = 32
HG_CHUNK_LOG2 = 5
HG_BLOCK_FWD = 256
HG_BLOCK_BWD = 128
N_IN = 3328
X_HEADS = 4
X_HEAD_DIM = 256
D_FF = 2816
N_DEV = 8
LANE = 128
ADAM_LR = 0.001
ADAM_B1 = 0.9
ADAM_B2 = 0.999
ADAM_EPS = 1e-08
ADAM_WD = 0.01
ADAM_STEP = 10
VMEM_LIMIT = 56 * 1024 * 1024

MESH_T = pl.DeviceIdType.MESH


def _params(**kw):
    return pltpu.CompilerParams(vmem_limit_bytes=VMEM_LIMIT, **kw)


def _dot(a, b, ca, cb):
    return lax.dot_general(a, b, (((ca,), (cb,)), ((), ())), preferred_element_type=F32)


def _bf(x):
    return x.astype(BF16)


def _sigmoid(x):
    return 1.0 / (1.0 + jnp.exp(-x))


def _rms_fwd(x, g):
    r = lax.rsqrt(jnp.mean(x * x, axis=-1, keepdims=True) + EPS)
    return x * r * g


def _rms_bwd(dy, x, g):
    r = lax.rsqrt(jnp.mean(x * x, axis=-1, keepdims=True) + EPS)
    xh = x * r
    dg = jnp.sum(dy * xh, axis=0, keepdims=True)
    t = dy * g
    dx = r * (t - xh * jnp.mean(t * xh, axis=-1, keepdims=True))
    return dx, dg


def _full(shape):
    nd = len(shape)
    return pl.BlockSpec(shape, lambda *a: (0,) * nd)


def _norm_mm(x, g, w, trans, tn, name, after=(), tm=512, p_dtype=F32):
    t, d = x.shape
    n = w.shape[0] if trans else w.shape[1]
    tm = min(tm, t)

    def body(x_ref, g_ref, w_ref, h_ref, p_ref):
        h = _bf(_rms_fwd(x_ref[...], g_ref[...]))
        h_ref[...] = h
        p_ref[...] = _dot(h, w_ref[...], 1, 1 if trans else 0).astype(p_dtype)

    w_spec = pl.BlockSpec((tn, d), lambda i, j: (j, 0)) if trans else pl.BlockSpec((d, tn), lambda i, j: (0, j))
    return _call(
        body, (x, g, w), name=name, grid=(t // tm, n // tn),
        in_specs=[pl.BlockSpec((tm, d), lambda i, j: (i, 0)), _full((1, d)), w_spec],
        out_specs=[pl.BlockSpec((tm, d), lambda i, j: (i, 0)), pl.BlockSpec((tm, tn), lambda i, j: (i, j))],
        out_shape=[jax.ShapeDtypeStruct((t, d), BF16), jax.ShapeDtypeStruct((t, n), p_dtype)], after=after)


def _mm_postnorm_res(a, w, g, res, name, tm=256):
    t, k = a.shape
    d = w.shape[1]

    def body(a_ref, w_ref, g_ref, res_ref, y_ref, o_ref):
        y = _dot(a_ref[...], w_ref[...], 1, 0)
        y_ref[...] = _bf(y)
        o_ref[...] = res_ref[...] + _rms_fwd(y, g_ref[...])

    row = lambda width: pl.BlockSpec((tm, width), lambda i: (i, 0))
    return pl.pallas_call(
        body, name=name, grid=(t // tm,),
        in_specs=[row(k), _full((k, d)), _full((1, d)), row(d)],
        out_specs=[row(d), row(d)],
        out_shape=[jax.ShapeDtypeStruct((t, d), BF16), jax.ShapeDtypeStruct((t, d), F32)],
        compiler_params=_params(),
    )(a, w, g, res)


def _mm_postnorm_res_loss(a, w, g, res, tgt, name, tm=256):
    t, k = a.shape
    d = w.shape[1]

    def body(a_ref, w_ref, g_ref, res_ref, tgt_ref, y_ref, dout_ref, loss_ref):
        @pl.when(pl.program_id(0) == 0)
        def _():
            loss_ref[...] = jnp.zeros_like(loss_ref)

        y = _dot(a_ref[...], w_ref[...], 1, 0)
        y_ref[...] = _bf(y)
        diff = res_ref[...] + _rms_fwd(y, g_ref[...]) - tgt_ref[...]
        dout_ref[...] = diff * (1.0 / d)
        part = jnp.sum(jnp.sum(diff * diff, axis=-1, keepdims=True), axis=0, keepdims=True)
        loss_ref[...] += (0.5 / d) * part

    row = lambda width: pl.BlockSpec((tm, width), lambda i: (i, 0))
    return pl.pallas_call(
        body, name=name, grid=(t // tm,),
        in_specs=[row(k), _full((k, d)), _full((1, d)), row(d), row(d)],
        out_specs=[row(d), row(d), _full((1, 1))],
        out_shape=[jax.ShapeDtypeStruct((t, d), BF16), jax.ShapeDtypeStruct((t, d), F32), jax.ShapeDtypeStruct((1, 1), F32)],
        compiler_params=_params(),
    )(a, w, g, res, tgt)


def _postnorm_bwd_mm(dout, y, g, w, name, da_dtype=F32, tm=256):
    t, d = y.shape
    k = w.shape[0]

    def body(dout_ref, y_ref, g_ref, w_ref, dy_ref, da_ref, dg_ref):
        @pl.when(pl.program_id(0) == 0)
        def _():
            dg_ref[...] = jnp.zeros_like(dg_ref)

        dy, dg = _rms_bwd(dout_ref[...], y_ref[...].astype(F32), g_ref[...])
        dg_ref[...] += dg
        dyb = _bf(dy)
        dy_ref[...] = dyb
        da_ref[...] = _dot(dyb, w_ref[...], 1, 1).astype(da_dtype)

    row = lambda width: pl.BlockSpec((tm, width), lambda i: (i, 0))
    return pl.pallas_call(
        body, name=name, grid=(t // tm,),
        in_specs=[row(d), row(d), _full((1, d)), _full((k, d))],
        out_specs=[row(d), row(k), _full((1, d))],
        out_shape=[jax.ShapeDtypeStruct((t, d), BF16), jax.ShapeDtypeStruct((t, k), da_dtype), jax.ShapeDtypeStruct((1, d), F32)],
        compiler_params=_params(),
    )(dout, y, g, w)


def _mm_prenorm_bwd(dp, w, trans, x, g, dres, name, after=(), tm=512):
    dps = list(dp) if isinstance(dp, (list, tuple)) else [dp]
    nparts = len(dps)
    t = dps[0].shape[0]
    widths = [part.shape[1] for part in dps]
    d = x.shape[1]
    tm = min(tm, t)

    def body(*refs):
        dp_refs = refs[:nparts]
        w_ref, x_ref, g_ref, dres_ref, dx_ref, dg_ref = refs[nparts:]

        @pl.when(pl.program_id(0) == 0)
        def _():
            dg_ref[...] = jnp.zeros_like(dg_ref)

        dh = None
        for k, dp_ref in enumerate(dp_refs):
            cols = slice(sum(widths[:k]), sum(widths[:k + 1]))
            term = _dot(dp_ref[...], w_ref[cols, :], 1, 0) if trans else _dot(dp_ref[...], w_ref[:, cols], 1, 1)
            dh = term if dh is None else dh + term
        dx, dg = _rms_bwd(dh, x_ref[...], g_ref[...])
        dg_ref[...] += dg
        dx_ref[...] = dres_ref[...] + dx

    row = lambda width: pl.BlockSpec((tm, width), lambda i: (i, 0))
    return _call(
        body, (*dps, w, x, g, dres), name=name, grid=(t // tm,),
        in_specs=[row(width) for width in widths] + [_full(w.shape), row(d), _full((1, d)), row(d)],
        out_specs=[row(d), _full((1, d))],
        out_shape=[jax.ShapeDtypeStruct((t, d), F32), jax.ShapeDtypeStruct((1, d), F32)], after=after)


def _dw(a, b, name, tka=256, after=()):
    parts = list(a) if isinstance(a, (list, tuple)) else [a]
    nparts = len(parts)
    t, nb = b.shape
    tiles = [part.shape[1] // tka for part in parts]
    first = [sum(tiles[:k]) for k in range(nparts)]

    def body(*refs):
        a_refs, b_ref, o_ref = refs[:nparts], refs[nparts], refs[-1]
        i = pl.program_id(0)
        for k, a_ref in enumerate(a_refs):
            @pl.when((i >= first[k]) & (i < first[k] + tiles[k]))
            def _():
                o_ref[...] = _bf(_dot(a_ref[...], b_ref[...], 0, 0))

    a_spec = lambda k: pl.BlockSpec((t, tka), lambda i: (0, jnp.clip(i - first[k], 0, tiles[k] - 1)))
    return pl.pallas_call(
        body, name=name, grid=(sum(tiles),),
        in_specs=[a_spec(k) for k in range(nparts)] + [_full((t, nb))] + [pl.BlockSpec(memory_space=pl.ANY)] * len(after),
        out_specs=pl.BlockSpec((tka, nb), lambda i: (i, 0)),
        out_shape=jax.ShapeDtypeStruct((sum(tiles) * tka, nb), BF16),
        compiler_params=_params(),
    )(*parts, b, *after)


def _rope_tables(t):
    pos = jnp.arange(t)
    r = (pos // GRID_W).astype(F32)
    c = (pos % GRID_W).astype(F32)
    npair = ATT_HEAD_DIM // 4
    inv = jnp.power(ROPE_THETA, -jnp.arange(npair, dtype=F32) / npair)
    ang = jnp.concatenate([r[:, None] * inv, c[:, None] * inv], axis=-1)
    cos = jnp.repeat(jnp.cos(ang), 2, axis=-1)
    sin = jnp.repeat(jnp.sin(ang), 2, axis=-1)
    even = (jnp.arange(ATT_HEAD_DIM) % 2) == 0
    sa = jnp.where(even, -sin, 0.0)
    sb = jnp.where(even, 0.0, sin)
    two = lambda a: jnp.tile(a, (1, 2))
    return two(cos), two(sa), two(sb)


def _head_sum_matrix():
    a = jnp.arange(LANE) // ATT_HEAD_DIM
    return (a[:, None] == a[None, :]).astype(BF16)


def _head_mean(v, bd):
    hi = _bf(v)
    lo = _bf(v - hi.astype(F32))
    return (_dot(hi, bd, 1, 0) + _dot(lo, bd, 1, 0)) * (1.0 / ATT_HEAD_DIM)


def _qk_prep(p, gq, gk, tables, bd, tm=512):
    t = p.shape[0]
    tm = min(tm, t)
    cc, sa, sb = tables

    def body(p_ref, gq_ref, gk_ref, cc_ref, sa_ref, sb_ref, bd_ref, q_ref, k_ref):
        cc_, sa_, sb_, bd_ = cc_ref[...], sa_ref[...], sb_ref[...], bd_ref[...]
        low = lax.broadcasted_iota(jnp.int32, (tm, LANE), 1) < ATT_HEAD_DIM

        def normrope(xs, g):
            xn = xs * lax.rsqrt(_head_mean(xs * xs, bd_) + EPS) * g
            return xn * cc_ + pltpu.roll(xn, LANE - 1, 1) * sa_ + pltpu.roll(xn, 1, 1) * sb_

        for j in range(4):
            y = normrope(p_ref[:, j * LANE:(j + 1) * LANE], gq_ref[...]) * (ATT_HEAD_DIM ** -0.5)
            yr = pltpu.roll(y, ATT_HEAD_DIM, 1)
            if j // 2 == 0:
                h0, h1 = jnp.where(low, y, 0.0), jnp.where(low, yr, 0.0)
            else:
                h0, h1 = jnp.where(low, 0.0, yr), jnp.where(low, 0.0, y)
            q_ref[:, (2 * j) * LANE:(2 * j + 1) * LANE] = _bf(h0)
            q_ref[:, (2 * j + 1) * LANE:(2 * j + 2) * LANE] = _bf(h1)
        k_ref[...] = _bf(normrope(p_ref[:, ATT_Q_DIM:ATT_Q_DIM + LANE], gk_ref[...]))

    row = lambda width: pl.BlockSpec((tm, width), lambda i: (i, 0))
    return pl.pallas_call(
        body, name="qk_prep", grid=(t // tm,),
        in_specs=[row(ATT_Q_DIM + LANE), _full((1, LANE)), _full((1, LANE)), row(LANE), row(LANE), row(LANE),
                  _full((LANE, LANE))],
        out_specs=[row(ATT_HEADS * LANE), row(LANE)],
        out_shape=[jax.ShapeDtypeStruct((t, ATT_HEADS * LANE), BF16), jax.ShapeDtypeStruct((t, LANE), BF16)],
        compiler_params=_params(),
    )(p, gq, gk, cc, sa, sb, bd)


def _qk_prep_bwd(p, dq, dk, dv, gq, gk, tables, bd, tm=512):
    t = p.shape[0]
    tm = min(tm, t)
    cc, sa, sb = tables
    att_w = ATT_Q_DIM + 2 * ATT_KV_DIM

    def body(p_ref, dq_ref, dk_ref, dv_ref, gq_ref, gk_ref, cc_ref, sa_ref, sb_ref, bd_ref, dp_ref, dgq_ref, dgk_ref):
        dp_ref[:, ATT_Q_DIM + LANE:att_w] = _bf(dv_ref[...])

        @pl.when(pl.program_id(0) == 0)
        def _():
            dgq_ref[...] = jnp.zeros_like(dgq_ref)
            dgk_ref[...] = jnp.zeros_like(dgk_ref)

        cc_, sa_, sb_, bd_ = cc_ref[...], sa_ref[...], sb_ref[...], bd_ref[...]
        low = lax.broadcasted_iota(jnp.int32, (tm, LANE), 1) < ATT_HEAD_DIM

        def bwd(xs, g, dy):
            r = lax.rsqrt(_head_mean(xs * xs, bd_) + EPS)
            xh = xs * r
            dxn = dy * cc_ + pltpu.roll(dy * sa_, 1, 1) + pltpu.roll(dy * sb_, LANE - 1, 1)
            dg = jnp.sum(dxn * xh, axis=0, keepdims=True)
            tt = dxn * g
            return r * (tt - xh * _head_mean(tt * xh, bd_)), dg

        dgq = jnp.zeros((1, LANE), F32)
        for j in range(4):
            d0 = dq_ref[:, (2 * j) * LANE:(2 * j + 1) * LANE]
            d1 = dq_ref[:, (2 * j + 1) * LANE:(2 * j + 2) * LANE]
            if j // 2 == 0:
                dy = jnp.where(low, d0, pltpu.roll(d1, ATT_HEAD_DIM, 1))
            else:
                dy = jnp.where(low, pltpu.roll(d0, ATT_HEAD_DIM, 1), d1)
            dx, dg = bwd(p_ref[:, j * LANE:(j + 1) * LANE], gq_ref[...], dy * (ATT_HEAD_DIM ** -0.5))
            dp_ref[:, j * LANE:(j + 1) * LANE] = _bf(dx)
            dgq = dgq + dg
        dgq_ref[...] += dgq
        dx, dg = bwd(p_ref[:, ATT_Q_DIM:ATT_Q_DIM + LANE], gk_ref[...], dk_ref[...])
        dp_ref[:, ATT_Q_DIM:ATT_Q_DIM + LANE] = _bf(dx)
        dgk_ref[...] += dg

    row = lambda width: pl.BlockSpec((tm, width), lambda i: (i, 0))
    return pl.pallas_call(
        body, name="qk_prep_bwd", grid=(t // tm,),
        in_specs=[row(ATT_Q_DIM + LANE), row(ATT_HEADS * LANE), row(LANE), row(LANE), _full((1, LANE)), _full((1, LANE)),
                  row(LANE), row(LANE), row(LANE), _full((LANE, LANE))],
        out_specs=[row(att_w), _full((1, LANE)), _full((1, LANE))],
        out_shape=[jax.ShapeDtypeStruct((t, att_w), BF16), jax.ShapeDtypeStruct((1, LANE), F32),
                   jax.ShapeDtypeStruct((1, LANE), F32)],
        compiler_params=_params(),
    )(p, dq, dk, dv, gq, gk, cc, sa, sb, bd)


def _attn_fwd(q, k, p, tq=256):
    t = k.shape[0]
    tq = min(tq, t)
    v_blk = (ATT_Q_DIM + ATT_KV_DIM) // LANE

    def body(q_ref, k_ref, v_ref, o_ref, o32_ref, lse_ref):
        k_ = k_ref[...]
        v = v_ref[...]
        lane_k = lax.broadcasted_iota(jnp.int32, (t, LANE), 1)
        lane_q = lax.broadcasted_iota(jnp.int32, (tq, LANE), 1)
        lowk, lowq = lane_k < ATT_HEAD_DIM, lane_q < ATT_HEAD_DIM
        ones_lane = (ATT_HEAD_DIM, 0)
        vm = (_bf(jnp.where(lowk, v, jnp.where(lane_k == ones_lane[0], 1.0, 0.0))),
              _bf(jnp.where(lowk, jnp.where(lane_k == ones_lane[1], 1.0, 0.0), v)))
        for j in range(4):
            kvh = j // 2
            acc = None
            for sub in range(2):
                h = 2 * j + sub
                s = _dot(q_ref[:, h * LANE:(h + 1) * LANE], k_, 1, 1)
                mx = jnp.max(s, axis=-1, keepdims=True)
                ov = _dot(jnp.exp(_bf(s - mx)), vm[kvh], 1, 0)
                l = jnp.sum(jnp.where(lane_q == ones_lane[kvh], ov, 0.0), axis=-1, keepdims=True)
                lse_ref[h] = mx + jnp.log(l)
                o = jnp.where(lowq if kvh == 0 else ~lowq, ov, 0.0) * (1.0 / l)
                if sub != kvh:
                    o = pltpu.roll(o, ATT_HEAD_DIM, 1)
                acc = o if acc is None else acc + o
            o32_ref[:, j * LANE:(j + 1) * LANE] = acc
            o_ref[:, j * LANE:(j + 1) * LANE] = _bf(acc)

    row = pl.BlockSpec((tq, ATT_Q_DIM), lambda i: (i, 0))
    return _call(
        body, (q, k, p), name="attn_fwd", grid=(t // tq,),
        in_specs=[pl.BlockSpec((tq, ATT_HEADS * LANE), lambda i: (i, 0)), _full((t, LANE)),
                  pl.BlockSpec((t, LANE), lambda i: (0, v_blk))],
        out_specs=[row, row, pl.BlockSpec((ATT_HEADS, tq, 1), lambda i: (0, i, 0))],
        out_shape=[jax.ShapeDtypeStruct((t, ATT_Q_DIM), BF16), jax.ShapeDtypeStruct((t, ATT_Q_DIM), F32),
                   jax.ShapeDtypeStruct((ATT_HEADS, t, 1), F32)])


def _attn_bwd(q, k, p, dcat, o32, lse, tq=256):
    t = k.shape[0]
    tq = min(tq, t)
    v_blk = (ATT_Q_DIM + ATT_KV_DIM) // LANE

    def body(q_ref, k_ref, v_ref, do_ref, o_ref, lse_ref, dq_ref, dk_ref, dv_ref):
        @pl.when(pl.program_id(0) == 0)
        def _():
            dk_ref[...] = jnp.zeros_like(dk_ref)
            dv_ref[...] = jnp.zeros_like(dv_ref)

        k_ = k_ref[...]
        vb = _bf(v_ref[...])
        lowq = lax.broadcasted_iota(jnp.int32, (tq, LANE), 1) < ATT_HEAD_DIM
        dk_acc = jnp.zeros((t, LANE), F32)
        dv_acc = jnp.zeros((t, LANE), F32)
        for j in range(4):
            kvh = j // 2
            dop = do_ref[:, j * LANE:(j + 1) * LANE].astype(F32)
            prod = dop * o_ref[:, j * LANE:(j + 1) * LANE]
            d_low = jnp.sum(jnp.where(lowq, prod, 0.0), axis=-1, keepdims=True)
            d_sub = (d_low, jnp.sum(prod, axis=-1, keepdims=True) - d_low)
            for sub in range(2):
                h = 2 * j + sub
                src = dop if sub == kvh else pltpu.roll(dop, ATT_HEAD_DIM, 1)
                do_h = _bf(jnp.where(lowq, src, 0.0) if kvh == 0 else jnp.where(lowq, 0.0, src))
                qh = q_ref[:, h * LANE:(h + 1) * LANE]
                pr = jnp.exp(_bf(_dot(qh, k_, 1, 1) - lse_ref[h]))
                ds = pr * _bf(_dot(do_h, vb, 1, 1) - d_sub[sub])
                dq_ref[:, h * LANE:(h + 1) * LANE] = _dot(ds, k_, 1, 0)
                dk_acc = dk_acc + _dot(ds, qh, 0, 0)
                dv_acc = dv_acc + _dot(pr, do_h, 0, 0)
        dk_ref[...] += dk_acc
        dv_ref[...] += dv_acc

    row = pl.BlockSpec((tq, ATT_Q_DIM), lambda i: (i, 0))
    return _call(
        body, (q, k, p, dcat, o32, lse), name="attn_bwd", grid=(t // tq,),
        in_specs=[pl.BlockSpec((tq, ATT_HEADS * LANE), lambda i: (i, 0)), _full((t, LANE)),
                  pl.BlockSpec((t, LANE), lambda i: (0, v_blk)), row, row,
                  pl.BlockSpec((ATT_HEADS, tq, 1), lambda i: (0, i, 0))],
        out_specs=[pl.BlockSpec((tq, ATT_HEADS * LANE), lambda i: (i, 0)), _full((t, LANE)), _full((t, LANE))],
        out_shape=[jax.ShapeDtypeStruct((t, ATT_HEADS * LANE), F32), jax.ShapeDtypeStruct((t, LANE), F32),
                   jax.ShapeDtypeStruct((t, LANE), F32)])


def _lower_bounds(a0, a1):
    def body(a0_ref, a1_ref, lb_ref):
        m = jnp.maximum(a0_ref[...], a1_ref[...])
        e0, e1 = jnp.exp(a0_ref[...] - m), jnp.exp(a1_ref[...] - m)
        lb_ref[...] = e0 / (e0 + e1)

    return pl.pallas_call(body, name="lower_bounds", out_shape=jax.ShapeDtypeStruct(a0.shape, F32))(a0, a1)


def _lower_bounds_bwd(a0, a1, dlb):
    def body(a0_ref, a1_ref, dlb_ref, d0_ref, d1_ref):
        m = jnp.maximum(a0_ref[...], a1_ref[...])
        e0, e1 = jnp.exp(a0_ref[...] - m), jnp.exp(a1_ref[...] - m)
        lb = e0 / (e0 + e1)
        d0 = dlb_ref[...] * lb * (1.0 - lb)
        d0_ref[...] = d0
        d1_ref[...] = -d0

    return pl.pallas_call(body, name="lower_bounds_bwd", out_shape=[jax.ShapeDtypeStruct(a0.shape, F32)] * 2)(a0, a1, dlb)


def _chunk_scan(x, pos, down):
    n = x.shape[0]
    s = 1
    while s < HG_CHUNK:
        if down:
            x = x + jnp.where(pos >= s, pltpu.roll(x, s, 0), 0.0)
        else:
            x = x + jnp.where(pos < HG_CHUNK - s, pltpu.roll(x, n - s, 0), 0.0)
        s *= 2
    return x


def _chunk_concat(x, nc):
    xb = _bf(x)
    rows = []
    for c in range(nc):
        pieces = [xb[c * HG_CHUNK:(c + 1) * HG_CHUNK, :]]
        if c:
            pieces.insert(0, jnp.zeros((HG_CHUNK, c * LANE), BF16))
        if c < nc - 1:
            pieces.append(jnp.zeros((HG_CHUNK, (nc - 1 - c) * LANE), BF16))
        rows.append(jnp.concatenate(pieces, axis=1) if len(pieces) > 1 else pieces[0])
    return jnp.concatenate(rows, axis=0) if nc > 1 else rows[0]


def _chunk_pick(x, nc):
    rows = [x[c * HG_CHUNK:(c + 1) * HG_CHUNK, c * LANE:(c + 1) * LANE] for c in range(nc)]
    return jnp.concatenate(rows, axis=0) if nc > 1 else rows[0]


def _chunk_rows(vals):
    rows = [jnp.broadcast_to(v, (HG_CHUNK, LANE)) for v in vals]
    return jnp.concatenate(rows, axis=0) if len(rows) > 1 else rows[0]


def _hgrn_gates(hq, z, lb):
    sq = _sigmoid(hq)
    sig = _sigmoid(z)
    f = lb + (1.0 - lb) * sig
    return hq * sq, sq, sig, f, jnp.log(f)


def _hgrn_local(q, f, g, v, pos, amask, rev, nc):
    k = 1.0 - f
    ba = _chunk_scan(g, pos, not rev)
    bb = _chunk_scan(g, pos, rev)
    eq = 0.5 * (ba - bb + g)
    ex_q, ex_k, ex_i, ex_d = jnp.exp(eq), jnp.exp(-eq), jnp.exp(ba), jnp.exp(bb - g)
    qb, kb, qi, kd = q * ex_q, k * ex_k, q * ex_i, k * ex_d
    dvec = jnp.exp(ba + bb - g)
    a = jnp.where(amask, _dot(_bf(qb), _bf(kb), 1, 1), 0.0)
    kd_m = _chunk_concat(kd, nc)
    qi_m = _chunk_concat(qi, nc)
    ut_all = _dot(_bf(v), kd_m, 0, 0)
    return dict(k=k, ex_q=ex_q, ex_k=ex_k, ex_i=ex_i, ex_d=ex_d, qb=qb, kb=kb, qi=qi, kd=kd, dvec=dvec, a=a,
                kd_m=kd_m, qi_m=qi_m, ut_all=ut_all)


def _hgrn_masks(n, rev):
    row = lax.broadcasted_iota(jnp.int32, (n, LANE), 0)
    ti = lax.broadcasted_iota(jnp.int32, (n, n), 0)
    si = lax.broadcasted_iota(jnp.int32, (n, n), 1)
    tri = (si >= ti) if rev else (si <= ti)
    same = jnp.right_shift(ti, HG_CHUNK_LOG2) == jnp.right_shift(si, HG_CHUNK_LOG2)
    return jnp.bitwise_and(row, HG_CHUNK - 1), same & tri


def _hgrn_specs(t, rev, bwd):
    n = min(HG_BLOCK_BWD if bwd else HG_BLOCK_FWD, t)
    nb = t // n
    if rev != bwd:
        blk = lambda i: nb - 1 - i
    else:
        blk = lambda i: i
    col = lambda base: pl.BlockSpec((n, 2 * LANE), lambda hp, i: (blk(i), base + hp))
    return n, nb, blk, col


def _hgrn_fwd(p, lb, rev, after=()):
    t = p.shape[0]
    n, nb, blk, col = _hgrn_specs(t, rev, False)
    nc = n // HG_CHUNK
    sub = n // min(HG_BLOCK_BWD, t)
    z_base = 7 if rev else 5

    def body(hq_ref, z_ref, hi_ref, lb_ref, o_ref, ssave_ref, st_scr):
        @pl.when(pl.program_id(1) == 0)
        def _():
            st_scr[...] = jnp.zeros_like(st_scr)

        pos, amask = _hgrn_masks(n, rev)
        order = list(range(nc))[::-1] if rev else list(range(nc))
        for hh in range(2):
            sl = slice(hh * LANE, (hh + 1) * LANE)
            q, _, _, f, g = _hgrn_gates(hq_ref[:, sl], z_ref[:, sl], lb_ref[0:1, sl])
            v = hi_ref[:, sl]
            c_ = _hgrn_local(q, f, g, v, pos, amask, rev, nc)
            st = st_scr[hh]
            cols = [None] * nc
            for c in order:
                cols[c] = st
                st = st * c_["dvec"][c * HG_CHUNK:c * HG_CHUNK + 1, :] + c_["ut_all"][:, c * LANE:(c + 1) * LANE]
            st_scr[hh] = st
            for s in range(sub):
                ssave_ref[sub - 1 - s if rev else s, sl, :] = cols[order[s * (nc // sub)]]
            st_all = _bf(jnp.concatenate(cols, axis=1))
            o_ref[:, sl] = _dot(_bf(c_["a"]), _bf(v), 1, 0) + _dot(c_["qi_m"], st_all, 1, 1)

    return _call(
        body, (p, p, p, lb), name="hgrn_fwd_rev" if rev else "hgrn_fwd", grid=(2, nb),
        in_specs=[col(3), col(z_base), col(9), pl.BlockSpec((1, 2 * LANE), lambda hp, i: (0, hp))],
        out_specs=[pl.BlockSpec((n, 2 * LANE), lambda hp, i: (blk(i), hp)),
                   pl.BlockSpec((sub, 2 * LANE, LANE), lambda hp, i: (blk(i), hp, 0))],
        out_shape=[jax.ShapeDtypeStruct((t, HG_DIM), F32), jax.ShapeDtypeStruct((nb * sub, HG_DIM, LANE), F32)],
        scratch_shapes=[pltpu.VMEM((2, LANE, LANE), F32)], after=after)


def _hgrn_bwd(p, lb, do, ssave, rev, other=None, after=()):
    t = p.shape[0]
    n, nb, blk, col = _hgrn_specs(t, rev, True)
    nc = n // HG_CHUNK
    z_base = 7 if rev else 5
    n_other = 0 if other is None else 2

    def body(*refs):
        hq_ref, z_ref, hi_ref, lb_ref, do_ref, ssave_ref = refs[:6]
        dhq_ref, dz_ref, dhi_ref, dlb_ref, dst_scr = refs[6 + n_other:]

        @pl.when(pl.program_id(1) == 0)
        def _():
            dst_scr[...] = jnp.zeros_like(dst_scr)
            dlb_ref[...] = jnp.zeros_like(dlb_ref)

        pos, amask = _hgrn_masks(n, rev)
        order = list(range(nc))[::-1] if rev else list(range(nc))
        for hh in range(2):
            sl = slice(hh * LANE, (hh + 1) * LANE)
            hq, lbv = hq_ref[:, sl], lb_ref[0:1, sl]
            q, sq, sig, f, g = _hgrn_gates(hq, z_ref[:, sl], lbv)
            v = hi_ref[:, sl]
            c_ = _hgrn_local(q, f, g, v, pos, amask, rev, nc)
            dvec, ut_all = c_["dvec"], c_["ut_all"]
            drow = lambda c: dvec[c * HG_CHUNK:c * HG_CHUNK + 1, :]
            st = ssave_ref[0, sl, :]
            cols = [None] * nc
            for c in order:
                cols[c] = st
                st = st * drow(c) + ut_all[:, c * LANE:(c + 1) * LANE]
            dob, vb = _bf(do_ref[:, sl]), _bf(v)
            gt_all = _dot(dob, c_["qi_m"], 0, 0)
            dcur = dst_scr[hh]
            dnext = [None] * nc
            ddrow = [None] * nc
            for c in order[::-1]:
                dnext[c] = dcur
                ddrow[c] = jnp.sum(cols[c] * dcur, axis=0, keepdims=True) * drow(c)
                dcur = dcur * drow(c) + gt_all[:, c * LANE:(c + 1) * LANE]
            dst_scr[hh] = dcur
            dsn_all = _bf(jnp.concatenate(dnext, axis=1))
            st_all = _bf(jnp.concatenate(cols, axis=1))
            da = _bf(jnp.where(amask, _dot(dob, vb, 1, 1), 0.0))
            dv = _dot(_bf(c_["a"]), dob, 0, 0) + _dot(c_["kd_m"], dsn_all, 1, 1)
            dqb = _dot(da, _bf(c_["kb"]), 1, 0)
            dkb = _dot(da, _bf(c_["qb"]), 0, 0)
            dqi = _chunk_pick(_dot(dob, st_all, 1, 0), nc)
            dkd = _chunk_pick(_dot(vb, dsn_all, 1, 0), nc)
            dq = dqb * c_["ex_q"] + dqi * c_["ex_i"]
            dk = dkb * c_["ex_k"] + dkd * c_["ex_d"]
            e = dqb * c_["qb"] - dkb * c_["kb"] + dqi * c_["qi"]
            w = dkd * c_["kd"]
            dg = _chunk_scan(e, pos, rev) + (_chunk_scan(w, pos, not rev) - w) + _chunk_rows(ddrow)
            df = dg / f - dk
            dz_ref[:, sl] = _bf(df * (1.0 - lbv) * sig * (1.0 - sig))
            dlb_ref[0:1, sl] += jnp.sum(df * (1.0 - sig), axis=0, keepdims=True)
            dhq = dq * (sq * (1.0 + hq * (1.0 - sq)))
            if other is None:
                dhq_ref[:, sl], dhi_ref[:, sl] = dhq, dv
            else:
                dhq_ref[:, sl], dhi_ref[:, sl] = _bf(dhq + refs[6][:, sl]), _bf(dv + refs[7][:, sl])

    out = pl.BlockSpec((n, 2 * LANE), lambda hp, i: (blk(i), hp))
    sum_dtype = F32 if other is None else BF16
    return _call(
        body, (p, p, p, lb, do, ssave, *(other or ())), name="hgrn_bwd_rev" if rev else "hgrn_bwd", grid=(2, nb),
        in_specs=[col(3), col(z_base), col(9), pl.BlockSpec((1, 2 * LANE), lambda hp, i: (0, hp)), out,
                  pl.BlockSpec((1, 2 * LANE, LANE), lambda hp, i: (blk(i), hp, 0))] + [out] * n_other,
        out_specs=[out, out, out, pl.BlockSpec((1, 2 * LANE), lambda hp, i: (0, hp))],
        out_shape=[jax.ShapeDtypeStruct((t, HG_DIM), sum_dtype), jax.ShapeDtypeStruct((t, HG_DIM), BF16),
                   jax.ShapeDtypeStruct((t, HG_DIM), sum_dtype), jax.ShapeDtypeStruct((1, HG_DIM), F32)],
        scratch_shapes=[pltpu.VMEM((2, LANE, LANE), F32)], after=after)


def _mix_out(att, of, ob, p, gout, w_out, g_post, x, after=(), tm=256):
    t, d = x.shape
    tm = min(tm, t)

    def body(att_ref, of_ref, ob_ref, hg0_ref, hg1_ref, gout_ref, w_ref, g_ref, x_ref, cat_ref, y_ref, x1_ref):
        cat_ref[:, :ATT_Q_DIM] = att_ref[...]
        for h in range(HG_HEADS):
            sl = slice(h * LANE, (h + 1) * LANE)
            hg_ref = hg0_ref if h < 2 else hg1_ref
            hg = hg_ref[:, (h % 2) * LANE:(h % 2 + 1) * LANE]
            nrm = _rms_fwd(of_ref[:, sl] + ob_ref[:, sl], gout_ref[...])
            cat_ref[:, ATT_Q_DIM + h * LANE:ATT_Q_DIM + (h + 1) * LANE] = _bf(nrm * hg * _sigmoid(hg))
        y = _dot(cat_ref[...], w_ref[...], 1, 0)
        y_ref[...] = _bf(y)
        x1_ref[...] = x_ref[...] + _rms_fwd(y, g_ref[...])

    row = lambda width: pl.BlockSpec((tm, width), lambda i: (i, 0))
    return _call(
        body, (att, of, ob, p, p, gout, w_out, g_post, x), name="mix_out", grid=(t // tm,),
        in_specs=[row(ATT_Q_DIM), row(HG_DIM), row(HG_DIM), pl.BlockSpec((tm, 2 * LANE), lambda i: (i, 11)),
                  pl.BlockSpec((tm, 2 * LANE), lambda i: (i, 12)), _full((1, LANE)), _full((d, d)), _full((1, d)), row(d)],
        out_specs=[row(d), row(d), row(d)],
        out_shape=[jax.ShapeDtypeStruct((t, d), BF16), jax.ShapeDtypeStruct((t, d), BF16), jax.ShapeDtypeStruct((t, d), F32)],
        after=after)


def _rec_bwd(dcat, of, ob, p, gout, tm=256):
    t = of.shape[0]
    tm = min(tm, t)

    def body(dc_ref, of_ref, ob_ref, hg0_ref, hg1_ref, gout_ref, do_ref, dhg_ref, dgo_ref):
        @pl.when(pl.program_id(0) == 0)
        def _():
            dgo_ref[...] = jnp.zeros_like(dgo_ref)

        dgo = jnp.zeros((1, LANE), F32)
        for h in range(HG_HEADS):
            sl = slice(h * LANE, (h + 1) * LANE)
            hg_ref = hg0_ref if h < 2 else hg1_ref
            hg = hg_ref[:, (h % 2) * LANE:(h % 2 + 1) * LANE]
            o = of_ref[:, sl] + ob_ref[:, sl]
            sg = _sigmoid(hg)
            drec = dc_ref[:, sl].astype(F32)
            dhg_ref[:, sl] = _bf(drec * _rms_fwd(o, gout_ref[...]) * (sg * (1.0 + hg * (1.0 - sg))))
            do, dg = _rms_bwd(drec * hg * sg, o, gout_ref[...])
            do_ref[:, sl] = do
            dgo = dgo + dg
        dgo_ref[...] += dgo

    row = lambda width: pl.BlockSpec((tm, width), lambda i: (i, 0))
    return pl.pallas_call(
        body, name="rec_bwd", grid=(t // tm,),
        in_specs=[pl.BlockSpec((tm, HG_DIM), lambda i: (i, 1)), row(HG_DIM), row(HG_DIM),
                  pl.BlockSpec((tm, 2 * LANE), lambda i: (i, 11)), pl.BlockSpec((tm, 2 * LANE), lambda i: (i, 12)),
                  _full((1, LANE))],
        out_specs=[row(HG_DIM), row(HG_DIM), _full((1, LANE))],
        out_shape=[jax.ShapeDtypeStruct((t, HG_DIM), F32), jax.ShapeDtypeStruct((t, HG_DIM), BF16),
                   jax.ShapeDtypeStruct((1, LANE), F32)],
        compiler_params=_params(),
    )(dcat, of, ob, p, p, gout)


def _xattn_fwd(q, kv, tq=512):
    t, d = q.shape
    tq = min(tq, t)
    nm = kv.shape[0]

    def body(q_ref, kv_ref, o_ref):
        for h in range(X_HEADS):
            sl = slice(h * X_HEAD_DIM, (h + 1) * X_HEAD_DIM)
            s = _dot(_bf(q_ref[:, sl]), _bf(kv_ref[:, sl]), 1, 1) * (X_HEAD_DIM ** -0.5)
            e = jnp.exp(s - jnp.max(s, axis=-1, keepdims=True))
            pr = e * (1.0 / jnp.sum(e, axis=-1, keepdims=True))
            o_ref[:, sl] = _bf(_dot(_bf(pr), _bf(kv_ref[:, d + h * X_HEAD_DIM:d + (h + 1) * X_HEAD_DIM]), 1, 0))

    return pl.pallas_call(
        body, name="xattn_fwd", grid=(t // tq,),
        in_specs=[pl.BlockSpec((tq, d), lambda i: (i, 0)), _full((nm, 2 * d))],
        out_specs=pl.BlockSpec((tq, d), lambda i: (i, 0)),
        out_shape=jax.ShapeDtypeStruct((t, d), BF16),
        compiler_params=_params(),
    )(q, kv)


def _xattn_bwd(q, kv, do, tq=512):
    t, d = q.shape
    tq = min(tq, t)
    nm = kv.shape[0]

    def body(q_ref, kv_ref, do_ref, dq_ref, dkv_ref):
        @pl.when(pl.program_id(0) == 0)
        def _():
            dkv_ref[...] = jnp.zeros_like(dkv_ref)

        for h in range(X_HEADS):
            sl = slice(h * X_HEAD_DIM, (h + 1) * X_HEAD_DIM)
            slv = slice(d + h * X_HEAD_DIM, d + (h + 1) * X_HEAD_DIM)
            qb, kb, vb, dob = _bf(q_ref[:, sl]), _bf(kv_ref[:, sl]), _bf(kv_ref[:, slv]), _bf(do_ref[:, sl])
            s = _dot(qb, kb, 1, 1) * (X_HEAD_DIM ** -0.5)
            e = jnp.exp(s - jnp.max(s, axis=-1, keepdims=True))
            pr = e * (1.0 / jnp.sum(e, axis=-1, keepdims=True))
            dpr = _dot(dob, vb, 1, 1)
            ds = _bf(pr * (dpr - jnp.sum(pr * dpr, axis=-1, keepdims=True)) * (X_HEAD_DIM ** -0.5))
            dq_ref[:, sl] = _bf(_dot(ds, kb, 1, 0))
            dkv_ref[:, sl] += _dot(ds, qb, 0, 0)
            dkv_ref[:, slv] += _dot(_bf(pr), dob, 0, 0)

    return pl.pallas_call(
        body, name="xattn_bwd", grid=(t // tq,),
        in_specs=[pl.BlockSpec((tq, d), lambda i: (i, 0)), _full((nm, 2 * d)), pl.BlockSpec((tq, d), lambda i: (i, 0))],
        out_specs=[pl.BlockSpec((tq, d), lambda i: (i, 0)), _full((nm, 2 * d))],
        out_shape=[jax.ShapeDtypeStruct((t, d), BF16), jax.ShapeDtypeStruct((nm, 2 * d), F32)],
        compiler_params=_params(),
    )(q, kv, do)


CONV_TN = 256


def _shift_rows(u, row, t, delta):
    if delta < 0:
        return jnp.where(row == 0, 0.0, pltpu.roll(u, 1, 0))
    return jnp.where(row == t - 1, 0.0, pltpu.roll(u, t - 1, 0))


def _conv_gate_fwd(u, conv_w, conv_b):
    t = u.shape[0]
    nt = D_FF // CONV_TN

    def body(ug_ref, uv_ref, wg_ref, wv_ref, bg_ref, bv_ref, a_ref):
        row = lax.broadcasted_iota(jnp.int32, (t, CONV_TN), 0)

        def conv(u_ref, w_ref, b_ref):
            uu = u_ref[...]
            return (b_ref[...] + _shift_rows(uu, row, t, -1) * w_ref[0:1, :] + uu * w_ref[1:2, :]
                    + _shift_rows(uu, row, t, 1) * w_ref[2:3, :])

        gate = conv(ug_ref, wg_ref, bg_ref)
        a_ref[...] = _bf(gate * _sigmoid(gate) * conv(uv_ref, wv_ref, bv_ref))

    col = lambda rows, off: pl.BlockSpec((rows, CONV_TN), lambda j: (0, j + off))
    return pl.pallas_call(
        body, name="conv_gate_fwd", grid=(nt,),
        in_specs=[col(t, 0), col(t, nt), col(3, 0), col(3, nt), col(1, 0), col(1, nt)],
        out_specs=col(t, 0),
        out_shape=jax.ShapeDtypeStruct((t, D_FF), BF16),
        compiler_params=_params(),
    )(u, u, conv_w, conv_w, conv_b, conv_b)


def _conv_gate_bwd(u, conv_w, conv_b, da, after=()):
    t = u.shape[0]
    nt = D_FF // CONV_TN

    def body(ug_ref, uv_ref, wg_ref, wv_ref, bg_ref, bv_ref, da_ref, dug_ref, dwg_ref, dbg_ref, duv_ref, dwv_ref, dbv_ref):
        row = lax.broadcasted_iota(jnp.int32, (t, CONV_TN), 0)
        ug, uv = ug_ref[...], uv_ref[...]
        ug_m, ug_p = _shift_rows(ug, row, t, -1), _shift_rows(ug, row, t, 1)
        uv_m, uv_p = _shift_rows(uv, row, t, -1), _shift_rows(uv, row, t, 1)
        gate = bg_ref[...] + ug_m * wg_ref[0:1, :] + ug * wg_ref[1:2, :] + ug_p * wg_ref[2:3, :]
        val = bv_ref[...] + uv_m * wv_ref[0:1, :] + uv * wv_ref[1:2, :] + uv_p * wv_ref[2:3, :]
        sg = _sigmoid(gate)
        da_ = da_ref[...].astype(F32)

        def emit(dacc, um, uu, up, w_ref, du_ref, dw_ref, db_ref):
            du_ref[...] = _bf(_shift_rows(dacc, row, t, 1) * w_ref[0:1, :] + dacc * w_ref[1:2, :]
                              + _shift_rows(dacc, row, t, -1) * w_ref[2:3, :])
            dw_ref[0:1, :] = jnp.sum(dacc * um, axis=0, keepdims=True)
            dw_ref[1:2, :] = jnp.sum(dacc * uu, axis=0, keepdims=True)
            dw_ref[2:3, :] = jnp.sum(dacc * up, axis=0, keepdims=True)
            db_ref[...] = jnp.sum(dacc, axis=0, keepdims=True)

        emit(da_ * val * (sg * (1.0 + gate * (1.0 - sg))), ug_m, ug, ug_p, wg_ref, dug_ref, dwg_ref, dbg_ref)
        emit(da_ * gate * sg, uv_m, uv, uv_p, wv_ref, duv_ref, dwv_ref, dbv_ref)

    col = lambda rows, off: pl.BlockSpec((rows, CONV_TN), lambda j: (0, j + off))
    half_shapes = [jax.ShapeDtypeStruct((t, D_FF), BF16), jax.ShapeDtypeStruct((3, D_FF), F32),
                   jax.ShapeDtypeStruct((1, D_FF), F32)]
    outs = _call(
        body, (u, u, conv_w, conv_w, conv_b, conv_b, da), name="conv_gate_bwd", grid=(nt,),
        in_specs=[col(t, 0), col(t, nt), col(3, 0), col(3, nt), col(1, 0), col(1, nt), col(t, 0)],
        out_specs=[col(t, 0), col(3, 0), col(1, 0)] * 2, out_shape=half_shapes * 2, after=after)
    return outs[:3], outs[3:]


def _row_tile(r, cap):
    best = r
    for cand in range(16, cap + 1, 16):
        if r % cand == 0:
            best = cand
    return best


def _sum_parts(own, got, me, name, tr=256):
    _, r, c = got.shape
    tr = _row_tile(r, tr)

    def body(me_ref, own_ref, got_ref, o_ref):
        mine = own_ref[...].astype(F32)
        acc = None
        for i in range(N_DEV):
            term = jnp.where(me_ref[0] == i, mine, got_ref[i].astype(F32))
            acc = term if acc is None else acc + term
        o_ref[...] = acc

    return pl.pallas_call(
        body, name=name,
        grid_spec=pltpu.PrefetchScalarGridSpec(
            num_scalar_prefetch=1, grid=(r // tr,),
            in_specs=[pl.BlockSpec((None, tr, c), lambda i, me_ref: (me_ref[0], i, 0)),
                      pl.BlockSpec((N_DEV, tr, c), lambda i, me_ref: (0, i, 0))],
            out_specs=pl.BlockSpec((tr, c), lambda i, me_ref: (i, 0))),
        out_shape=jax.ShapeDtypeStruct((r, c), F32),
        compiler_params=_params(),
    )(me, own, got)


def _adamw_math(w_, g_, m_, v_):
    m_ = ADAM_B1 * m_ + (1.0 - ADAM_B1) * g_
    v_ = ADAM_B2 * v_ + (1.0 - ADAM_B2) * (g_ * g_)
    m_hat = m_ / (1.0 - ADAM_B1 ** ADAM_STEP)
    v_hat = v_ / (1.0 - ADAM_B2 ** ADAM_STEP)
    return -ADAM_LR * (m_hat / (jnp.sqrt(v_hat) + ADAM_EPS) + ADAM_WD * w_), m_, v_


def _sum_adamw(own, got, me, w, m, v, name, tr=256):
    _, r, c = got.shape
    tr = _row_tile(r, tr)

    def body(me_ref, own_ref, got_ref, w_ref, m_ref, v_ref, g_ref, d_ref, nm_ref, nv_ref):
        mine = own_ref[...].astype(F32)
        acc = None
        for i in range(N_DEV):
            term = jnp.where(me_ref[0] == i, mine, got_ref[i].astype(F32))
            acc = term if acc is None else acc + term
        g_ref[...] = acc
        d_ref[...], nm_ref[...], nv_ref[...] = _adamw_math(w_ref[...], acc, m_ref[...], v_ref[...])

    blk = pl.BlockSpec((tr, c), lambda i, me_ref: (i, 0))
    return pl.pallas_call(
        body, name=name,
        grid_spec=pltpu.PrefetchScalarGridSpec(
            num_scalar_prefetch=1, grid=(r // tr,),
            in_specs=[pl.BlockSpec((None, tr, c), lambda i, me_ref: (me_ref[0], i, 0)),
                      pl.BlockSpec((N_DEV, tr, c), lambda i, me_ref: (0, i, 0)), blk, blk, blk],
            out_specs=[blk] * 4),
        out_shape=[jax.ShapeDtypeStruct((r, c), F32)] * 4,
        compiler_params=_params(),
    )(me, own, got, w, m, v)


def _adamw(w, g, m, v, name, tr=256):
    r, c = w.shape
    tr = _row_tile(r, tr)

    def body(w_ref, g_ref, m_ref, v_ref, d_ref, nm_ref, nv_ref):
        d_ref[...], nm_ref[...], nv_ref[...] = _adamw_math(w_ref[...], g_ref[...], m_ref[...], v_ref[...])

    blk = pl.BlockSpec((tr, c), lambda i: (i, 0))
    return pl.pallas_call(
        body, name=name, grid=(r // tr,),
        in_specs=[blk] * 4, out_specs=[blk] * 3,
        out_shape=[jax.ShapeDtypeStruct((r, c), F32)] * 3,
        compiler_params=_params(),
    )(w, g, m, v)


def _mesh_pos():
    return lax.axis_index("x"), lax.axis_index("y"), lax.axis_index("c")


def _dev_index(px, py, pc):
    return 4 * px + 2 * py + pc


class _Gather:
    def __init__(self, arrs):
        self.arrs = list(arrs)
        n = len(self.arrs)
        self.out_shape = [jax.ShapeDtypeStruct((N_DEV,) + a.shape, a.dtype) for a in self.arrs]
        self.sems = [pltpu.SemaphoreType.DMA((7, n)), pltpu.SemaphoreType.DMA((7, n)), pltpu.SemaphoreType.DMA((n,))]

    def _ctx(self, ins, outs, sems):
        send_sems, recv_sems, local_sems = sems
        x, y, c = _mesh_pos()
        chips = [(1 - x, y), (x, 1 - y), (1 - x, 1 - y)]

        def copy(k, a, block, to, src=None):
            dst = outs[a].at[_dev_index(*block)]
            return pltpu.make_async_remote_copy(
                src_ref=dst if src is None else src, dst_ref=dst, send_sem=send_sems.at[k, a], recv_sem=recv_sems.at[k, a],
                device_id=to, device_id_type=MESH_T)

        n = len(ins)
        me, sibling = (x, y, c), (x, y, 1 - c)
        mine = [pltpu.make_async_copy(ins[a], outs[a].at[_dev_index(*me)], local_sems.at[a]) for a in range(n)]
        first = [copy(0, a, me, sibling, src=ins[a]) for a in range(n)]
        first += [copy(1 + j, a, me, (*chip, c), src=ins[a]) for j, chip in enumerate(chips) for a in range(n)]
        passed = [copy(4 + j, a, (*chip, c), sibling) for j, chip in enumerate(chips) for a in range(n)]
        return n, c, me, sibling, chips, copy, mine, first, passed

    def start(self, ins, outs, sems):
        _, _, _, _, _, _, mine, first, _ = self._ctx(ins, outs, sems)
        for cp in mine + first:
            cp.start()

    def forward(self, ins, outs, sems):
        n, c, me, _, chips, copy, _, _, passed = self._ctx(ins, outs, sems)
        for j, chip in enumerate(chips):
            for a in range(n):
                copy(1 + j, a, (*chip, c), me).wait_recv()
                passed[j * n + a].start()

    def finish(self, ins, outs, sems):
        n, c, me, sibling, chips, copy, mine, first, passed = self._ctx(ins, outs, sems)
        for a in range(n):
            copy(0, a, sibling, me).wait_recv()
        for j, chip in enumerate(chips):
            for a in range(n):
                copy(4 + j, a, (*chip, 1 - c), me).wait_recv()
        for cp in first + passed:
            cp.wait_send()
        for cp in mine:
            cp.wait()


def _comm_alone(comm, name):
    n = len(comm.arrs)

    def body(*refs):
        ins, outs, sems = refs[:n], refs[n:2 * n], refs[2 * n:]
        comm.start(ins, outs, sems)
        if comm.forward is not None:
            comm.forward(ins, outs, sems)
        comm.finish(ins, outs, sems)

    any_spec = pl.BlockSpec(memory_space=pl.ANY)
    return pl.pallas_call(body, name=name, in_specs=[any_spec] * n, out_specs=[any_spec] * n, out_shape=comm.out_shape,
                          scratch_shapes=comm.sems)(*comm.arrs)


def _peers(x, y, c):
    out = []
    for k in range(1, N_DEV):
        pos = (1 - x if k & 4 else x, 1 - y if k & 2 else y, 1 - c if k & 1 else c)
        out.append((k, pos, _dev_index(*pos)))
    return out


def _exchange_begin(arrs, tag):
    n = len(arrs)
    lands = [lax.empty(a.shape, a.dtype) for a in arrs]

    def start_body(*refs):
        ins, land = refs[:n], refs[n:2 * n]
        send_sems, recv_sems, token = refs[2 * n], refs[2 * n + 1], refs[-1]
        x, y, c = _mesh_pos()
        me_i = _dev_index(x, y, c)
        for k, pos, peer_i in _peers(x, y, c):
            for a in range(n):
                pltpu.make_async_remote_copy(
                    src_ref=ins[a].at[peer_i], dst_ref=land[a].at[me_i], send_sem=send_sems.at[(k - 1) * n + a],
                    recv_sem=recv_sems.at[(k - 1) * n + a], device_id=pos, device_id_type=MESH_T).start()
        token[...] = jnp.zeros_like(token)

    hbm = pl.BlockSpec(memory_space=pltpu.HBM)
    sem = pl.BlockSpec(memory_space=pltpu.SEMAPHORE)
    thru = [pltpu.HBM(a.shape, a.dtype) for a in arrs]
    outs = pl.pallas_call(
        start_body, name="exchange_start_" + tag,
        out_shape=[pltpu.SemaphoreType.DMA((7 * n,)), pltpu.SemaphoreType.DMA((7 * n,))] + thru + thru
        + [jax.ShapeDtypeStruct((8, LANE), F32)],
        in_specs=[hbm] * (2 * n), out_specs=[sem, sem] + [hbm] * (2 * n) + [pl.BlockSpec(memory_space=pltpu.VMEM)],
        input_output_aliases={i: 2 + i for i in range(2 * n)},
        compiler_params=pltpu.CompilerParams(has_side_effects=pltpu.SideEffectType.DATAFLOW_SIDE_EFFECTING),
    )(*[pltpu.with_memory_space_constraint(a, pltpu.HBM) for a in list(arrs) + list(lands)])
    return (tag, n, outs[0], outs[1], outs[2:2 + n], outs[2 + n:2 + 2 * n]), outs[-1]


def _exchange_end(handle, after):
    tag, n, send_sems, recv_sems, srcs, lands = handle

    def body(*refs):
        ins, land = refs[:n], refs[n:2 * n]
        send_sems_, recv_sems_ = refs[2 * n], refs[2 * n + 1]
        x, y, c = _mesh_pos()
        me_i = _dev_index(x, y, c)
        for k, pos, peer_i in _peers(x, y, c):
            for a in range(n):
                cp = pltpu.make_async_remote_copy(
                    src_ref=ins[a].at[peer_i], dst_ref=land[a].at[me_i], send_sem=send_sems_.at[(k - 1) * n + a],
                    recv_sem=recv_sems_.at[(k - 1) * n + a], device_id=pos, device_id_type=MESH_T)
                cp.wait_send()
                cp.wait_recv()

    hbm = pl.BlockSpec(memory_space=pltpu.HBM)
    sem = pl.BlockSpec(memory_space=pltpu.SEMAPHORE)
    outs = pl.pallas_call(
        body, name="exchange_end_" + tag, out_shape=[pltpu.HBM(a.shape, a.dtype) for a in list(srcs) + list(lands)],
        in_specs=[hbm] * (2 * n) + [sem, sem, pl.BlockSpec(memory_space=pl.ANY)], out_specs=[hbm] * (2 * n),
        input_output_aliases={i: i for i in range(2 * n)},
        compiler_params=pltpu.CompilerParams(has_side_effects=pltpu.SideEffectType.DATAFLOW_SIDE_EFFECTING),
    )(*srcs, *lands, send_sems, recv_sems, after)
    return list(zip(outs[:n], outs[n:]))


def _gather_begin(arrs, tag):
    n = len(arrs)
    me = _dev_index(*_mesh_pos())
    lands = [lax.dynamic_update_slice(lax.empty((N_DEV,) + a.shape, a.dtype), a[None], (me,) + (0,) * a.ndim) for a in arrs]

    def start_body(*refs):
        ins, land = refs[:n], refs[n:2 * n]
        send_sems, recv_sems, token = refs[2 * n], refs[2 * n + 1], refs[-1]
        x, y, c = _mesh_pos()
        me_i = _dev_index(x, y, c)
        for a in range(n):
            for k, pos, _ in _peers(x, y, c):
                pltpu.make_async_remote_copy(
                    src_ref=ins[a], dst_ref=land[a].at[me_i], send_sem=send_sems.at[(k - 1) * n + a],
                    recv_sem=recv_sems.at[(k - 1) * n + a], device_id=pos, device_id_type=MESH_T).start()
        token[...] = jnp.zeros_like(token)

    hbm = pl.BlockSpec(memory_space=pltpu.HBM)
    sem = pl.BlockSpec(memory_space=pltpu.SEMAPHORE)
    thru = [pltpu.HBM(a.shape, a.dtype) for a in list(arrs) + lands]
    outs = pl.pallas_call(
        start_body, name="gather_start_" + tag,
        out_shape=[pltpu.SemaphoreType.DMA((7 * n,)), pltpu.SemaphoreType.DMA((7 * n,))] + thru
        + [jax.ShapeDtypeStruct((8, LANE), F32)],
        in_specs=[hbm] * (2 * n), out_specs=[sem, sem] + [hbm] * (2 * n) + [pl.BlockSpec(memory_space=pltpu.VMEM)],
        input_output_aliases={i: 2 + i for i in range(2 * n)},
        compiler_params=pltpu.CompilerParams(has_side_effects=pltpu.SideEffectType.DATAFLOW_SIDE_EFFECTING),
    )(*[pltpu.with_memory_space_constraint(a, pltpu.HBM) for a in list(arrs) + lands])
    return (tag, n, outs[0], outs[1], outs[2:2 + n], outs[2 + n:2 + 2 * n]), outs[-1]


def _gather_end(handle, which, after):
    tag, n, send_sems, recv_sems, srcs, lands = handle
    m = len(which)

    def body(*refs):
        ins, land = refs[:m], refs[m:2 * m]
        send_sems_, recv_sems_ = refs[2 * m], refs[2 * m + 1]
        x, y, c = _mesh_pos()
        me_i = _dev_index(x, y, c)
        for j, a in enumerate(which):
            for k, pos, _ in _peers(x, y, c):
                cp = pltpu.make_async_remote_copy(
                    src_ref=ins[j], dst_ref=land[j].at[me_i], send_sem=send_sems_.at[(k - 1) * n + a],
                    recv_sem=recv_sems_.at[(k - 1) * n + a], device_id=pos, device_id_type=MESH_T)
                cp.wait_send()
                cp.wait_recv()

    hbm = pl.BlockSpec(memory_space=pltpu.HBM)
    sem = pl.BlockSpec(memory_space=pltpu.SEMAPHORE)
    ops = [srcs[a] for a in which] + [lands[a] for a in which]
    outs = pl.pallas_call(
        body, name="gather_end_%s_%s" % (tag, "_".join(str(a) for a in which)),
        out_shape=[pltpu.HBM(a.shape, a.dtype) for a in ops],
        in_specs=[hbm] * (2 * m) + [sem, sem, pl.BlockSpec(memory_space=pl.ANY)], out_specs=[hbm] * (2 * m),
        input_output_aliases={i: i for i in range(2 * m)},
        compiler_params=pltpu.CompilerParams(has_side_effects=pltpu.SideEffectType.DATAFLOW_SIDE_EFFECTING),
    )(*ops, send_sems, recv_sems, after)
    return list(outs[m:])


_SPLIT = dict(has_side_effects=pltpu.SideEffectType.DATAFLOW_SIDE_EFFECTING)


def _chips(x, y):
    return [(1 - x, y), (x, 1 - y), (1 - x, 1 - y)]


def _gather2_begin(arrs, tag):
    n = len(arrs)
    me = _dev_index(*_mesh_pos())
    lands = [lax.dynamic_update_slice(lax.empty((N_DEV,) + a.shape, a.dtype), a[None], (me,) + (0,) * a.ndim) for a in arrs]

    def body(*refs):
        ins, land = refs[:n], refs[n:2 * n]
        send1, recv1, token = refs[2 * n], refs[2 * n + 1], refs[-1]
        x, y, c = _mesh_pos()
        me_i = _dev_index(x, y, c)
        targets = [(x, y, 1 - c)] + [(*chip, c) for chip in _chips(x, y)]
        for a in range(n):
            for k, pos in enumerate(targets):
                pltpu.make_async_remote_copy(
                    src_ref=ins[a], dst_ref=land[a].at[me_i], send_sem=send1.at[k * n + a], recv_sem=recv1.at[k * n + a],
                    device_id=pos, device_id_type=MESH_T).start()
        token[...] = jnp.zeros_like(token)

    hbm = pl.BlockSpec(memory_space=pltpu.HBM)
    sem = pl.BlockSpec(memory_space=pltpu.SEMAPHORE)
    thru = [pltpu.HBM(a.shape, a.dtype) for a in list(arrs) + lands]
    outs = pl.pallas_call(
        body, name="gather_start_" + tag,
        out_shape=[pltpu.SemaphoreType.DMA((4 * n,)), pltpu.SemaphoreType.DMA((4 * n,))] + thru
        + [jax.ShapeDtypeStruct((8, LANE), F32)],
        in_specs=[hbm] * (2 * n), out_specs=[sem, sem] + [hbm] * (2 * n) + [pl.BlockSpec(memory_space=pltpu.VMEM)],
        input_output_aliases={i: 2 + i for i in range(2 * n)}, compiler_params=pltpu.CompilerParams(**_SPLIT),
    )(*[pltpu.with_memory_space_constraint(a, pltpu.HBM) for a in list(arrs) + lands])
    return dict(tag=tag, n=n, send1=outs[0], recv1=outs[1], srcs=list(outs[2:2 + n]), lands=list(outs[2 + n:2 + 2 * n])), outs[-1]


def _gather2_forward(handle, which, after, part):
    n, m = handle["n"], len(which)

    def body(*refs):
        land, recv1 = refs[:m], refs[m]
        send2, recv2, token = refs[m + 2], refs[m + 3], refs[-1]
        token[...] = jnp.zeros_like(token)
        x, y, c = _mesh_pos()
        for j, a in enumerate(which):
            for k, chip in enumerate(_chips(x, y)):
                blk = land[j].at[_dev_index(*chip, c)]
                copy = pltpu.make_async_remote_copy(
                    src_ref=blk, dst_ref=blk, send_sem=send2.at[k * m + j], recv_sem=recv2.at[k * m + j],
                    device_id=(x, y, 1 - c), device_id_type=MESH_T)
                pltpu.make_async_remote_copy(
                    src_ref=blk, dst_ref=blk, send_sem=send2.at[k * m + j], recv_sem=recv1.at[(1 + k) * n + a],
                    device_id=(*chip, c), device_id_type=MESH_T).wait_recv()
                copy.start()

    hbm = pl.BlockSpec(memory_space=pltpu.HBM)
    sem = pl.BlockSpec(memory_space=pltpu.SEMAPHORE)
    lands = [handle["lands"][a] for a in which]
    outs = pl.pallas_call(
        body, name="gather_forward_%s_%s" % (handle["tag"], part),
        out_shape=[pltpu.SemaphoreType.DMA((3 * m,)), pltpu.SemaphoreType.DMA((3 * m,))] + [pltpu.HBM(a.shape, a.dtype) for a in lands]
        + [jax.ShapeDtypeStruct((8, LANE), F32)],
        in_specs=[hbm] * m + [sem, pl.BlockSpec(memory_space=pl.ANY)],
        out_specs=[sem, sem] + [hbm] * m + [pl.BlockSpec(memory_space=pltpu.VMEM)],
        input_output_aliases={i: 2 + i for i in range(m)}, compiler_params=pltpu.CompilerParams(**_SPLIT),
    )(*lands, handle["recv1"], after)
    for j, a in enumerate(which):
        handle["lands"][a] = outs[2 + j]
        handle[("leg2", a)] = (outs[0], outs[1], j, m)
    return outs[-1]


def _gather2_end(handle, a, after):
    n = handle["n"]
    send2, recv2, j, m = handle[("leg2", a)]

    def body(src, land, send1, recv1, send2_, recv2_, after_ref, src_out, land_out):
        x, y, c = _mesh_pos()
        me_i = _dev_index(x, y, c)
        sibling = (x, y, 1 - c)
        for k, pos in enumerate([sibling] + [(*chip, c) for chip in _chips(x, y)]):
            first = pltpu.make_async_remote_copy(
                src_ref=src, dst_ref=land.at[me_i], send_sem=send1.at[k * n + a], recv_sem=recv1.at[k * n + a],
                device_id=pos, device_id_type=MESH_T)
            first.wait_send()
            if k == 0:
                first.wait_recv()
        for k in range(3):
            second = pltpu.make_async_remote_copy(
                src_ref=land.at[me_i], dst_ref=land.at[me_i], send_sem=send2_.at[k * m + j], recv_sem=recv2_.at[k * m + j],
                device_id=sibling, device_id_type=MESH_T)
            second.wait_send()
            second.wait_recv()

    hbm = pl.BlockSpec(memory_space=pltpu.HBM)
    sem = pl.BlockSpec(memory_space=pltpu.SEMAPHORE)
    src, land = handle["srcs"][a], handle["lands"][a]
    outs = pl.pallas_call(
        body, name="gather_end_%s_%d" % (handle["tag"], a),
        out_shape=[pltpu.HBM(src.shape, src.dtype), pltpu.HBM(land.shape, land.dtype)],
        in_specs=[hbm, hbm, sem, sem, sem, sem, pl.BlockSpec(memory_space=pl.ANY)], out_specs=[hbm, hbm],
        input_output_aliases={0: 0, 1: 1}, compiler_params=pltpu.CompilerParams(**_SPLIT),
    )(src, land, handle["send1"], handle["recv1"], send2, recv2, after)
    return outs[1]


def _call(body, operands, *, name, grid, in_specs, out_specs, out_shape, scratch_shapes=(), after=()):
    ni, nd = len(in_specs), len(after)

    def ordered(*refs):
        body(*refs[:ni], *refs[ni + nd:])

    outs = pl.pallas_call(
        ordered, name=name, grid=grid, in_specs=list(in_specs) + [pl.BlockSpec(memory_space=pl.ANY)] * nd,
        out_specs=out_specs, out_shape=out_shape, scratch_shapes=list(scratch_shapes), compiler_params=_params(),
    )(*operands, *after)
    return list(outs)


def _rows(a):
    return a.reshape(-1, a.shape[-1])


def _slots(a):
    return a.reshape(N_DEV, -1, a.shape[-1])


def _local_step(x, mem, tgt, w, wire):
    tables = _rope_tables(x.shape[0])
    bd = _head_sum_matrix()
    two = lambda g: jnp.tile(g, (1, 2))
    gq2, gk2 = two(w["q_norm_g"]), two(w["k_norm_g"])

    w_in_t, hg_lb, conv_w = _comm_alone(_Gather([wire["w_in"], w["hg_lb"].reshape(4, -1), w["conv_w"][0]]), "gather_w_in")
    w_in_t = _rows(w_in_t)
    hg_lb = jnp.transpose(hg_lb.reshape(N_DEV, 2, 2, -1), (1, 2, 0, 3)).reshape(2, 2, HG_DIM)
    conv_w = jnp.transpose(conv_w, (1, 0, 2)).reshape(3, 2 * D_FF)
    lb_a0, lb_a1 = hg_lb[:, 0, :], hg_lb[:, 1, :]

    order = ("w_out", "w_xq", "w_xkv", "w_xo", "w_up", "w_down")
    w_in_t, *later = lax.optimization_barrier((w_in_t, *[wire[n] for n in order]))
    fetch, started = _gather2_begin(later, "w")
    take = lambda n, after: _rows(_gather2_end(fetch, order.index(n), after))

    h1, p = _norm_mm(x, w["pre_mix_g"], w_in_t, True, N_IN, "in_proj", after=[started])
    qr, kr = _qk_prep(p, gq2, gk2, tables, bd)
    att, att32, lse = _attn_fwd(qr, kr, p)
    passed = _gather2_forward(fetch, [0, 1, 2, 3], att, "a")
    lb = _lower_bounds(lb_a0, lb_a1)
    of, s_f = _hgrn_fwd(p, lb[0:1], False, after=[passed])
    ob, s_b = _hgrn_fwd(p, lb[1:2], True, after=[passed, of])
    passed = _gather2_forward(fetch, [4, 5], ob, "b")
    w_out = take("w_out", ob)
    cat, mixed, x1 = _mix_out(att, of, ob, p, w["hg_out_norm_g"], w_out, w["post_mix_g"], x, after=[passed])

    w_xq = take("w_xq", x1)
    h2, q2 = _norm_mm(x1, w["pre_x_g"], w_xq, False, 1024, "xq_proj", p_dtype=BF16)
    w_xkv_t = take("w_xkv", q2)
    mn, kv = _norm_mm(mem, w["mem_norm_g"], w_xkv_t, True, 2 * D_MODEL, "xkv_proj")
    o2 = _xattn_fwd(q2, kv)
    w_xo = take("w_xo", o2)
    y2, x2 = _mm_postnorm_res(o2, w_xo, w["post_x_g"], x1, "xo_proj")

    w_up_t = take("w_up", x2)
    h3, u = _norm_mm(x2, w["pre_ffn_g"], w_up_t, True, 2 * D_FF, "up_proj", tm=256)
    a = _conv_gate_fwd(u, conv_w, w["conv_b"])
    w_down = take("w_down", a)
    y3, dx3, loss = _mm_postnorm_res_loss(a, w_down, w["post_ffn_g"], x2, tgt, "down_proj")

    g, pending = {}, {}
    dy3, da, g["post_ffn_g"] = _postnorm_bwd_mm(dx3, y3, w["post_ffn_g"], w_down, "down_bwd", BF16)
    pending["down"], started = _exchange_begin([_slots(_dw(a, dy3, "dw_down", tka=D_FF // 2))], "down")
    (du_g, dcw_g, dcb_g), (du_v, dcw_v, dcb_v) = _conv_gate_bwd(u, conv_w, w["conv_b"], da, after=[started])
    g["conv_w"] = jnp.concatenate([dcw_g, dcw_v], axis=1)
    g["conv_b"] = jnp.concatenate([dcb_g, dcb_v], axis=1)
    pending["up"], started = _exchange_begin([_slots(_dw([du_g, du_v], h3, "dw_up", tka=D_FF // 2))], "up")
    dx2, g["pre_ffn_g"] = _mm_prenorm_bwd([du_g, du_v], w_up_t, True, x2, w["pre_ffn_g"], dx3, "up_bwd", after=[started], tm=256)

    dy2, do2, g["post_x_g"] = _postnorm_bwd_mm(dx2, y2, w["post_x_g"], w_xo, "xo_bwd", BF16)
    dw_xo = _dw(o2, dy2, "dw_xo", tka=512)
    dq2, dkv = _xattn_bwd(q2, kv, do2)
    dw_xq = _dw(h2, dq2, "dw_xq", tka=512)
    dkvb = _bf(dkv)
    dw_xkv = _dw(dkvb, mn, "dw_xkv")
    _, g["mem_norm_g"] = _mm_prenorm_bwd(dkvb, w_xkv_t, True, mem, w["mem_norm_g"], jnp.zeros_like(mem), "xkv_bwd")
    dx1, g["pre_x_g"] = _mm_prenorm_bwd(dq2, w_xq, False, x1, w["pre_x_g"], dx2, "xq_bwd")

    dmixed, dcat, g["post_mix_g"] = _postnorm_bwd_mm(dx1, mixed, w["post_mix_g"], w_out, "out_bwd", BF16)
    dw_out = _dw(cat, dmixed, "dw_out", tka=512)
    pending["x"], started = _exchange_begin([_slots(dw_xo), _slots(dw_xq), _slots(dw_xkv), _slots(dw_out)], "x")
    do, dhg, g["hg_out_norm_g"] = _rec_bwd(dcat, of, ob, p, w["hg_out_norm_g"])
    dhq_f, dz_f, dhi_f, dlb_f = _hgrn_bwd(p, lb[0:1], do, s_f, False, after=[started])
    dhq, dz_b, dhi, dlb_b = _hgrn_bwd(p, lb[1:2], do, s_b, True, other=(dhq_f, dhi_f))
    d_a0, d_a1 = _lower_bounds_bwd(lb_a0, lb_a1, jnp.concatenate([dlb_f, dlb_b], axis=0))
    g["hg_lb"] = jnp.stack([d_a0, d_a1], axis=1)
    dqr, dkr, dv = _attn_bwd(qr, kr, p, dcat, att32, lse)
    dp_att, dgq, dgk = _qk_prep_bwd(p, dqr, dkr, dv, gq2, gk2, tables, bd)
    g["q_norm_g"], g["k_norm_g"] = dgq, dgk
    dp = [dp_att, dhq, dz_f, dz_b, dhi, dhg]

    early = [n for n in _SMALL if n != "pre_mix_g"]
    vals = [g[n].reshape(-1, g[n].shape[-1]) for n in early] + [jnp.pad(loss, ((0, 0), (0, LANE - 1)))]
    small = dict(early=early, shapes=[val.shape for val in vals])
    small["fetch"], started = _gather_begin([_pack_small(vals, "pack_small")], "small")
    pending["in"], started = _exchange_begin([_slots(_dw(dp, h1, "dw_in", after=[started]))], "in")
    dx, g["pre_mix_g"] = _mm_prenorm_bwd(dp, w_in_t, True, x, w["pre_mix_g"], dx1, "in_bwd", after=[started])
    small["fetch_last"], _ = _gather_begin([_pack_small([g["pre_mix_g"]], "pack_last")], "last")
    return dx, g, pending, small


_COL_SHARDED = ("w_in", "w_xkv", "w_up")
_ROW_SHARDED = ("w_out", "w_xq", "w_xo", "w_down")
_REPLICATED = ("pre_mix_g", "q_norm_g", "k_norm_g", "hg_out_norm_g", "post_mix_g", "pre_x_g", "mem_norm_g", "post_x_g",
               "pre_ffn_g", "conv_b", "post_ffn_g")
_WEIGHTS = ("pre_mix_g", "w_in", "q_norm_g", "k_norm_g", "hg_lb", "hg_out_norm_g", "w_out", "post_mix_g", "pre_x_g",
            "mem_norm_g", "w_xq", "w_xkv", "w_xo", "post_x_g", "pre_ffn_g", "w_up", "conv_w", "conv_b", "w_down",
            "post_ffn_g")
_SMALL = _REPLICATED + ("hg_lb", "conv_w")
_ADAM_TRANSPOSED = ("w_in", "w_up")
PACK_W = 1024


def _small_plan(shapes):
    plan, r = [], 0
    for vi, (rows, cols) in enumerate(shapes):
        for i in range(rows):
            for c0 in range(0, cols, PACK_W):
                plan.append((vi, i, c0, min(PACK_W, cols - c0), r))
                r += 1
    return plan, -(-r // 8) * 8


def _pack_small(vals, name):
    plan, nrows = _small_plan([val.shape for val in vals])

    def body(*refs):
        ins, out = refs[:-1], refs[-1]
        out[...] = jnp.zeros_like(out)
        for vi, i, c0, width, r in plan:
            out[r:r + 1, 0:width] = ins[vi][i:i + 1, c0:c0 + width]

    return pl.pallas_call(body, name=name, out_shape=jax.ShapeDtypeStruct((nrows, PACK_W), F32))(*vals)


def _sum_unpack_small(packs, shapes, name):
    plan, _ = _small_plan(shapes)

    def body(*refs):
        p_ref, outs = refs[0], refs[1:]
        acc = p_ref[0]
        for i in range(1, N_DEV):
            acc = acc + p_ref[i]
        for vi, i, c0, width, r in plan:
            outs[vi][i:i + 1, c0:c0 + width] = acc[r:r + 1, 0:width]

    return pl.pallas_call(body, name=name, out_shape=[jax.ShapeDtypeStruct(s, F32) for s in shapes])(packs)


def _adamw_many(ws, gs, ms, vs):
    n = len(ws)

    def body(*refs):
        w_refs, g_refs, m_refs, v_refs = (refs[k * n:(k + 1) * n] for k in range(4))
        d_refs, nm_refs, nv_refs = (refs[(4 + k) * n:(5 + k) * n] for k in range(3))
        for k in range(n):
            g_ = g_refs[k][...]
            m_ = ADAM_B1 * m_refs[k][...] + (1.0 - ADAM_B1) * g_
            v_ = ADAM_B2 * v_refs[k][...] + (1.0 - ADAM_B2) * (g_ * g_)
            m_hat = m_ / (1.0 - ADAM_B1 ** ADAM_STEP)
            v_hat = v_ / (1.0 - ADAM_B2 ** ADAM_STEP)
            d_refs[k][...] = -ADAM_LR * (m_hat / (jnp.sqrt(v_hat) + ADAM_EPS) + ADAM_WD * w_refs[k][...])
            nm_refs[k][...] = m_
            nv_refs[k][...] = v_

    shapes = [jax.ShapeDtypeStruct(a.shape, F32) for a in ws]
    outs = pl.pallas_call(body, name="adamw_small", out_shape=shapes * 3)(*ws, *gs, *ms, *vs)
    return outs[:n], outs[n:2 * n], outs[2 * n:]


def kernel(x, mem, pre_mix_g, w_in, q_norm_g, k_norm_g, hg_lb, hg_out_norm_g, w_out, post_mix_g, pre_x_g, mem_norm_g, w_xq, w_xkv, w_xo, post_x_g, pre_ffn_g, w_up, conv_w, conv_b, w_down, post_ffn_g, loss_target, m_pre_mix_g, m_w_in, m_q_norm_g, m_k_norm_g, m_hg_lb, m_hg_out_norm_g, m_w_out, m_post_mix_g, m_pre_x_g, m_mem_norm_g, m_w_xq, m_w_xkv, m_w_xo, m_post_x_g, m_pre_ffn_g, m_w_up, m_conv_w, m_conv_b, m_w_down, m_post_ffn_g, v_pre_mix_g, v_w_in, v_q_norm_g, v_k_norm_g, v_hg_lb, v_hg_out_norm_g, v_w_out, v_post_mix_g, v_pre_x_g, v_mem_norm_g, v_w_xq, v_w_xkv, v_w_xo, v_post_x_g, v_pre_ffn_g, v_w_up, v_conv_w, v_conv_b, v_w_down, v_post_ffn_g):
    args = dict(locals())
    w = {n: args[n] for n in _WEIGHTS}
    m = {n: args["m_" + n] for n in _WEIGHTS}
    v = {n: args["v_" + n] for n in _WEIGHTS}
    me = _dev_index(*_mesh_pos())

    wire = {n: _bf(w[n][0].T) for n in _COL_SHARDED}
    wire.update({n: _bf(w[n][0]) for n in _ROW_SHARDED})

    grad_x, g, pending, small = _local_step(x[0], mem[0], loss_target[0], w, wire)

    grads, delta, new_m, new_v = {}, {}, {}, {}

    me_arr = jnp.reshape(me, (1,)).astype(jnp.int32)

    def update(n, parts):
        if n in _ADAM_TRANSPOSED:
            outs = _sum_adamw(*parts, me_arr, w[n][0].T, m[n][0].T, v[n][0].T, "update_" + n)
            grads[n], delta[n], new_m[n], new_v[n] = (a.T[None] for a in outs)
        elif n in _COL_SHARDED:
            gsum = _sum_parts(*parts, me_arr, "sum_" + n).T
            grads[n] = gsum[None]
            delta[n], new_m[n], new_v[n] = (a[None] for a in _adamw(w[n][0], gsum, m[n][0], v[n][0], "adamw_" + n))
        else:
            outs = _sum_adamw(*parts, me_arr, w[n][0], m[n][0], v[n][0], "update_" + n)
            grads[n], delta[n], new_m[n], new_v[n] = (a[None] for a in outs)

    after = grad_x
    for tag, names in (("down", ["w_down"]), ("up", ["w_up"]), ("x", ["w_xo", "w_xq", "w_xkv", "w_out"]),
                       ("in", ["w_in"])):
        for n, parts in zip(names, _exchange_end(pending[tag], after)):
            update(n, parts)
            after = new_v[n]

    (packs,) = _gather_end(small["fetch"], [0], after)
    summed = _sum_unpack_small(packs, small["shapes"], "sum_unpack_small")
    loss = summed[-1][0, 0]
    for n, s in zip(small["early"], summed[:-1]):
        grads[n] = s
    (packs,) = _gather_end(small["fetch_last"], [0], after)
    (grads["pre_mix_g"],) = _sum_unpack_small(packs, [w["pre_mix_g"].shape], "sum_unpack_last")
    small = _SMALL
    fold = lambda v2: v2[:, :ATT_HEAD_DIM] + v2[:, ATT_HEAD_DIM:]
    grads["q_norm_g"], grads["k_norm_g"] = fold(grads["q_norm_g"]), fold(grads["k_norm_g"])
    grads["hg_lb"] = lax.dynamic_slice_in_dim(grads["hg_lb"].reshape(2, 2, HG_DIM), me * (HG_DIM // N_DEV),
                                              HG_DIM // N_DEV, axis=2)
    grads["conv_w"] = lax.dynamic_slice_in_dim(grads["conv_w"], me * (2 * D_FF // N_DEV), 2 * D_FF // N_DEV, axis=1)[None]

    flat2 = lambda a: a.reshape(-1, a.shape[-1])
    outs = _adamw_many(*[[flat2(d[n]) for n in small] for d in (w, grads, m, v)])
    for dst, vals in zip((delta, new_m, new_v), outs):
        for n, val in zip(small, vals):
            dst[n] = val.reshape(w[n].shape)

    return (loss, grad_x[None], *[grads[n] for n in _WEIGHTS], *[delta[n] for n in _WEIGHTS],
            *[new_m[n] for n in _WEIGHTS], *[new_v[n] for n in _WEIGHTS])
```

```python
import jax
import jax.numpy as jnp
from jax import lax
from jax.experimental import pallas as pl
from jax.experimental.pallas import tpu as pltpu

F32 = jnp.float32
BF16 = jnp.bfloat16

D_MODEL = 1024
GRID_W = 64
EPS = 1e-6
ATT_HEADS = 8
ATT_HEAD_DIM = 64
ATT_Q_DIM = 512
ATT_KV_DIM = 128
ROPE_THETA = 10000.0
HG_HEADS = 4
HG_DIM = 512
HG_CHUNK = 32
HG_CHUNK_LOG2 = 5
HG_BLOCK_FWD = 256
HG_BLOCK_BWD = 128
N_IN = 3328
X_HEADS = 4
X_HEAD_DIM = 256
D_FF = 2816
N_DEV = 8
LANE = 128
ADAM_LR = 0.001
ADAM_B1 = 0.9
ADAM_B2 = 0.999
ADAM_EPS = 1e-08
ADAM_WD = 0.01
ADAM_STEP = 10
VMEM_LIMIT = 56 * 1024 * 1024

MESH_T = pl.DeviceIdType.MESH


def _params(**kw):
    return pltpu.CompilerParams(vmem_limit_bytes=VMEM_LIMIT, **kw)


def _dot(a, b, ca, cb):
    return lax.dot_general(a, b, (((ca,), (cb,)), ((), ())), preferred_element_type=F32)


def _bf(x):
    return x.astype(BF16)


def _sigmoid(x):
    return 1.0 / (1.0 + jnp.exp(-x))


def _rms_fwd(x, g):
    r = lax.rsqrt(jnp.mean(x * x, axis=-1, keepdims=True) + EPS)
    return x * r * g


def _rms_bwd(dy, x, g):
    r = lax.rsqrt(jnp.mean(x * x, axis=-1, keepdims=True) + EPS)
    xh = x * r
    dg = jnp.sum(dy * xh, axis=0, keepdims=True)
    t = dy * g
    dx = r * (t - xh * jnp.mean(t * xh, axis=-1, keepdims=True))
    return dx, dg


def _full(shape):
    nd = len(shape)
    return pl.BlockSpec(shape, lambda *a: (0,) * nd)


def _norm_mm(x, g, w, trans, tn, name, after=(), tm=512, p_dtype=F32):
    t, d = x.shape
    n = w.shape[0] if trans else w.shape[1]
    tm = min(tm, t)

    def body(x_ref, g_ref, w_ref, h_ref, p_ref):
        h = _bf(_rms_fwd(x_ref[...], g_ref[...]))
        h_ref[...] = h
        p_ref[...] = _dot(h, w_ref[...], 1, 1 if trans else 0).astype(p_dtype)

    w_spec = pl.BlockSpec((tn, d), lambda i, j: (j, 0)) if trans else pl.BlockSpec((d, tn), lambda i, j: (0, j))
    return _call(
        body, (x, g, w), name=name, grid=(t // tm, n // tn),
        in_specs=[pl.BlockSpec((tm, d), lambda i, j: (i, 0)), _full((1, d)), w_spec],
        out_specs=[pl.BlockSpec((tm, d), lambda i, j: (i, 0)), pl.BlockSpec((tm, tn), lambda i, j: (i, j))],
        out_shape=[jax.ShapeDtypeStruct((t, d), BF16), jax.ShapeDtypeStruct((t, n), p_dtype)], after=after)


def _mm_postnorm_res(a, w, g, res, name, tm=256):
    t, k = a.shape
    d = w.shape[1]

    def body(a_ref, w_ref, g_ref, res_ref, y_ref, o_ref):
        y = _dot(a_ref[...], w_ref[...], 1, 0)
        y_ref[...] = _bf(y)
        o_ref[...] = res_ref[...] + _rms_fwd(y, g_ref[...])

    row = lambda width: pl.BlockSpec((tm, width), lambda i: (i, 0))
    return pl.pallas_call(
        body, name=name, grid=(t // tm,),
        in_specs=[row(k), _full((k, d)), _full((1, d)), row(d)],
        out_specs=[row(d), row(d)],
        out_shape=[jax.ShapeDtypeStruct((t, d), BF16), jax.ShapeDtypeStruct((t, d), F32)],
        compiler_params=_params(),
    )(a, w, g, res)


def _mm_postnorm_res_loss(a, w, g, res, tgt, name, tm=256):
    t, k = a.shape
    d = w.shape[1]

    def body(a_ref, w_ref, g_ref, res_ref, tgt_ref, y_ref, dout_ref, loss_ref):
        @pl.when(pl.program_id(0) == 0)
        def _():
            loss_ref[...] = jnp.zeros_like(loss_ref)

        y = _dot(a_ref[...], w_ref[...], 1, 0)
        y_ref[...] = _bf(y)
        diff = res_ref[...] + _rms_fwd(y, g_ref[...]) - tgt_ref[...]
        dout_ref[...] = diff * (1.0 / d)
        part = jnp.sum(jnp.sum(diff * diff, axis=-1, keepdims=True), axis=0, keepdims=True)
        loss_ref[...] += (0.5 / d) * part

    row = lambda width: pl.BlockSpec((tm, width), lambda i: (i, 0))
    return pl.pallas_call(
        body, name=name, grid=(t // tm,),
        in_specs=[row(k), _full((k, d)), _full((1, d)), row(d), row(d)],
        out_specs=[row(d), row(d), _full((1, 1))],
        out_shape=[jax.ShapeDtypeStruct((t, d), BF16), jax.ShapeDtypeStruct((t, d), F32), jax.ShapeDtypeStruct((1, 1), F32)],
        compiler_params=_params(),
    )(a, w, g, res, tgt)


def _postnorm_bwd_mm(dout, y, g, w, name, da_dtype=F32, tm=256):
    t, d = y.shape
    k = w.shape[0]

    def body(dout_ref, y_ref, g_ref, w_ref, dy_ref, da_ref, dg_ref):
        @pl.when(pl.program_id(0) == 0)
        def _():
            dg_ref[...] = jnp.zeros_like(dg_ref)

        dy, dg = _rms_bwd(dout_ref[...], y_ref[...].astype(F32), g_ref[...])
        dg_ref[...] += dg
        dyb = _bf(dy)
        dy_ref[...] = dyb
        da_ref[...] = _dot(dyb, w_ref[...], 1, 1).astype(da_dtype)

    row = lambda width: pl.BlockSpec((tm, width), lambda i: (i, 0))
    return pl.pallas_call(
        body, name=name, grid=(t // tm,),
        in_specs=[row(d), row(d), _full((1, d)), _full((k, d))],
        out_specs=[row(d), row(k), _full((1, d))],
        out_shape=[jax.ShapeDtypeStruct((t, d), BF16), jax.ShapeDtypeStruct((t, k), da_dtype), jax.ShapeDtypeStruct((1, d), F32)],
        compiler_params=_params(),
    )(dout, y, g, w)


def _mm_prenorm_bwd(dp, w, trans, x, g, dres, name, after=(), tm=512):
    dps = list(dp) if isinstance(dp, (list, tuple)) else [dp]
    nparts = len(dps)
    t = dps[0].shape[0]
    widths = [part.shape[1] for part in dps]
    d = x.shape[1]
    tm = min(tm, t)

    def body(*refs):
        dp_refs = refs[:nparts]
        w_ref, x_ref, g_ref, dres_ref, dx_ref, dg_ref = refs[nparts:]

        @pl.when(pl.program_id(0) == 0)
        def _():
            dg_ref[...] = jnp.zeros_like(dg_ref)

        dh = None
        for k, dp_ref in enumerate(dp_refs):
            cols = slice(sum(widths[:k]), sum(widths[:k + 1]))
            term = _dot(dp_ref[...], w_ref[cols, :], 1, 0) if trans else _dot(dp_ref[...], w_ref[:, cols], 1, 1)
            dh = term if dh is None else dh + term
        dx, dg = _rms_bwd(dh, x_ref[...], g_ref[...])
        dg_ref[...] += dg
        dx_ref[...] = dres_ref[...] + dx

    row = lambda width: pl.BlockSpec((tm, width), lambda i: (i, 0))
    return _call(
        body, (*dps, w, x, g, dres), name=name, grid=(t // tm,),
        in_specs=[row(width) for width in widths] + [_full(w.shape), row(d), _full((1, d)), row(d)],
        out_specs=[row(d), _full((1, d))],
        out_shape=[jax.ShapeDtypeStruct((t, d), F32), jax.ShapeDtypeStruct((1, d), F32)], after=after)


def _dw(a, b, name, tka=256, after=()):
    parts = list(a) if isinstance(a, (list, tuple)) else [a]
    nparts = len(parts)
    t, nb = b.shape
    tiles = [part.shape[1] // tka for part in parts]
    first = [sum(tiles[:k]) for k in range(nparts)]

    def body(*refs):
        a_refs, b_ref, o_ref = refs[:nparts], refs[nparts], refs[-1]
        i = pl.program_id(0)
        for k, a_ref in enumerate(a_refs):
            @pl.when((i >= first[k]) & (i < first[k] + tiles[k]))
            def _():
                o_ref[...] = _bf(_dot(a_ref[...], b_ref[...], 0, 0))

    a_spec = lambda k: pl.BlockSpec((t, tka), lambda i: (0, jnp.clip(i - first[k], 0, tiles[k] - 1)))
    return pl.pallas_call(
        body, name=name, grid=(sum(tiles),),
        in_specs=[a_spec(k) for k in range(nparts)] + [_full((t, nb))] + [pl.BlockSpec(memory_space=pl.ANY)] * len(after),
        out_specs=pl.BlockSpec((tka, nb), lambda i: (i, 0)),
        out_shape=jax.ShapeDtypeStruct((sum(tiles) * tka, nb), BF16),
        compiler_params=_params(),
    )(*parts, b, *after)


def _rope_tables(t):
    pos = jnp.arange(t)
    r = (pos // GRID_W).astype(F32)
    c = (pos % GRID_W).astype(F32)
    npair = ATT_HEAD_DIM // 4
    inv = jnp.power(ROPE_THETA, -jnp.arange(npair, dtype=F32) / npair)
    ang = jnp.concatenate([r[:, None] * inv, c[:, None] * inv], axis=-1)
    cos = jnp.repeat(jnp.cos(ang), 2, axis=-1)
    sin = jnp.repeat(jnp.sin(ang), 2, axis=-1)
    even = (jnp.arange(ATT_HEAD_DIM) % 2) == 0
    sa = jnp.where(even, -sin, 0.0)
    sb = jnp.where(even, 0.0, sin)
    two = lambda a: jnp.tile(a, (1, 2))
    return two(cos), two(sa), two(sb)


def _head_sum_matrix():
    a = jnp.arange(LANE) // ATT_HEAD_DIM
    return (a[:, None] == a[None, :]).astype(BF16)


def _head_mean(v, bd):
    hi = _bf(v)
    lo = _bf(v - hi.astype(F32))
    return (_dot(hi, bd, 1, 0) + _dot(lo, bd, 1, 0)) * (1.0 / ATT_HEAD_DIM)


def _qk_prep(p, gq, gk, tables, bd, tm=512):
    t = p.shape[0]
    tm = min(tm, t)
    cc, sa, sb = tables

    def body(p_ref, gq_ref, gk_ref, cc_ref, sa_ref, sb_ref, bd_ref, q_ref, k_ref):
        cc_, sa_, sb_, bd_ = cc_ref[...], sa_ref[...], sb_ref[...], bd_ref[...]
        low = lax.broadcasted_iota(jnp.int32, (tm, LANE), 1) < ATT_HEAD_DIM

        def normrope(xs, g):
            xn = xs * lax.rsqrt(_head_mean(xs * xs, bd_) + EPS) * g
            return xn * cc_ + pltpu.roll(xn, LANE - 1, 1) * sa_ + pltpu.roll(xn, 1, 1) * sb_

        for j in range(4):
            y = normrope(p_ref[:, j * LANE:(j + 1) * LANE].astype(F32), gq_ref[...]) * (ATT_HEAD_DIM ** -0.5)
            yr = pltpu.roll(y, ATT_HEAD_DIM, 1)
            if j // 2 == 0:
                h0, h1 = jnp.where(low, y, 0.0), jnp.where(low, yr, 0.0)
            else:
                h0, h1 = jnp.where(low, 0.0, yr), jnp.where(low, 0.0, y)
            q_ref[:, (2 * j) * LANE:(2 * j + 1) * LANE] = _bf(h0)
            q_ref[:, (2 * j + 1) * LANE:(2 * j + 2) * LANE] = _bf(h1)
        k_ref[...] = _bf(normrope(p_ref[:, ATT_Q_DIM:ATT_Q_DIM + LANE].astype(F32), gk_ref[...]))

    row = lambda width: pl.BlockSpec((tm, width), lambda i: (i, 0))
    return pl.pallas_call(
        body, name="qk_prep", grid=(t // tm,),
        in_specs=[row(ATT_Q_DIM + LANE), _full((1, LANE)), _full((1, LANE)), row(LANE), row(LANE), row(LANE),
                  _full((LANE, LANE))],
        out_specs=[row(ATT_HEADS * LANE), row(LANE)],
        out_shape=[jax.ShapeDtypeStruct((t, ATT_HEADS * LANE), BF16), jax.ShapeDtypeStruct((t, LANE), BF16)],
        compiler_params=_params(),
    )(p, gq, gk, cc, sa, sb, bd)


def _qk_prep_bwd(p, dq, dk, dv, gq, gk, tables, bd, tm=512):
    t = p.shape[0]
    tm = min(tm, t)
    cc, sa, sb = tables
    att_w = ATT_Q_DIM + 2 * ATT_KV_DIM

    def body(p_ref, dq_ref, dk_ref, dv_ref, gq_ref, gk_ref, cc_ref, sa_ref, sb_ref, bd_ref, dp_ref, dgq_ref, dgk_ref):
        dp_ref[:, ATT_Q_DIM + LANE:att_w] = _bf(dv_ref[...])

        @pl.when(pl.program_id(0) == 0)
        def _():
            dgq_ref[...] = jnp.zeros_like(dgq_ref)
            dgk_ref[...] = jnp.zeros_like(dgk_ref)

        cc_, sa_, sb_, bd_ = cc_ref[...], sa_ref[...], sb_ref[...], bd_ref[...]
        low = lax.broadcasted_iota(jnp.int32, (tm, LANE), 1) < ATT_HEAD_DIM

        def bwd(xs, g, dy):
            r = lax.rsqrt(_head_mean(xs * xs, bd_) + EPS)
            xh = xs * r
            dxn = dy * cc_ + pltpu.roll(dy * sa_, 1, 1) + pltpu.roll(dy * sb_, LANE - 1, 1)
            dg = jnp.sum(dxn * xh, axis=0, keepdims=True)
            tt = dxn * g
            return r * (tt - xh * _head_mean(tt * xh, bd_)), dg

        dgq = jnp.zeros((1, LANE), F32)
        for j in range(4):
            d0 = dq_ref[:, (2 * j) * LANE:(2 * j + 1) * LANE]
            d1 = dq_ref[:, (2 * j + 1) * LANE:(2 * j + 2) * LANE]
            if j // 2 == 0:
                dy = jnp.where(low, d0, pltpu.roll(d1, ATT_HEAD_DIM, 1))
            else:
                dy = jnp.where(low, pltpu.roll(d0, ATT_HEAD_DIM, 1), d1)
            dx, dg = bwd(p_ref[:, j * LANE:(j + 1) * LANE].astype(F32), gq_ref[...], dy * (ATT_HEAD_DIM ** -0.5))
            dp_ref[:, j * LANE:(j + 1) * LANE] = _bf(dx)
            dgq = dgq + dg
        dgq_ref[...] += dgq
        dx, dg = bwd(p_ref[:, ATT_Q_DIM:ATT_Q_DIM + LANE].astype(F32), gk_ref[...], dk_ref[...])
        dp_ref[:, ATT_Q_DIM:ATT_Q_DIM + LANE] = _bf(dx)
        dgk_ref[...] += dg

    row = lambda width: pl.BlockSpec((tm, width), lambda i: (i, 0))
    return pl.pallas_call(
        body, name="qk_prep_bwd", grid=(t // tm,),
        in_specs=[row(ATT_Q_DIM + LANE), row(ATT_HEADS * LANE), row(LANE), row(LANE), _full((1, LANE)), _full((1, LANE)),
                  row(LANE), row(LANE), row(LANE), _full((LANE, LANE))],
        out_specs=[row(att_w), _full((1, LANE)), _full((1, LANE))],
        out_shape=[jax.ShapeDtypeStruct((t, att_w), BF16), jax.ShapeDtypeStruct((1, LANE), F32),
                   jax.ShapeDtypeStruct((1, LANE), F32)],
        compiler_params=_params(),
    )(p, dq, dk, dv, gq, gk, cc, sa, sb, bd)


def _attn_fwd(q, k, p, tq=256):
    t = k.shape[0]
    tq = min(tq, t)
    v_blk = (ATT_Q_DIM + ATT_KV_DIM) // LANE

    def body(q_ref, k_ref, v_ref, o_ref, o32_ref, lse_ref):
        k_ = k_ref[...]
        v = v_ref[...].astype(F32)
        lane_k =lax.broadcasted_iota(jnp.int32, (t, LANE), 1)
        lane_q = lax.broadcasted_iota(jnp.int32, (tq, LANE), 1)
        lowk, lowq = lane_k < ATT_HEAD_DIM, lane_q < ATT_HEAD_DIM
        ones_lane = (ATT_HEAD_DIM, 0)
        vm = (_bf(jnp.where(lowk, v, jnp.where(lane_k == ones_lane[0], 1.0, 0.0))),
              _bf(jnp.where(lowk, jnp.where(lane_k == ones_lane[1], 1.0, 0.0), v)))
        for j in range(4):
            kvh = j // 2
            acc = None
            for sub in range(2):
                h = 2 * j + sub
                s = _dot(q_ref[:, h * LANE:(h + 1) * LANE], k_, 1, 1)
                mx = jnp.max(s, axis=-1, keepdims=True)
                ov = _dot(jnp.exp(_bf(s - mx)), vm[kvh], 1, 0)
                l = jnp.sum(jnp.where(lane_q == ones_lane[kvh], ov, 0.0), axis=-1, keepdims=True)
                lse_ref[h] = mx + jnp.log(l)
                o = jnp.where(lowq if kvh == 0 else ~lowq, ov, 0.0) * (1.0 / l)
                if sub != kvh:
                    o = pltpu.roll(o, ATT_HEAD_DIM, 1)
                acc = o if acc is None else acc + o
            o32_ref[:, j * LANE:(j + 1) * LANE] = acc
            o_ref[:, j * LANE:(j + 1) * LANE] = _bf(acc)

    row = pl.BlockSpec((tq, ATT_Q_DIM), lambda i: (i, 0))
    return _call(
        body, (q, k, p), name="attn_fwd", grid=(t // tq,),
        in_specs=[pl.BlockSpec((tq, ATT_HEADS * LANE), lambda i: (i, 0)), _full((t, LANE)),
                  pl.BlockSpec((t, LANE), lambda i: (0, v_blk))],
        out_specs=[row, row, pl.BlockSpec((ATT_HEADS, tq, 1), lambda i: (0, i, 0))],
        out_shape=[jax.ShapeDtypeStruct((t, ATT_Q_DIM), BF16), jax.ShapeDtypeStruct((t, ATT_Q_DIM), F32),
                   jax.ShapeDtypeStruct((ATT_HEADS, t, 1), F32)])


def _attn_bwd(q, k, p, dcat, o32, lse, tq=256):
    t = k.shape[0]
    tq = min(tq, t)
    v_blk = (ATT_Q_DIM + ATT_KV_DIM) // LANE

    def body(q_ref, k_ref, v_ref, do_ref, o_ref, lse_ref, dq_ref, dk_ref, dv_ref):
        @pl.when(pl.program_id(0) == 0)
        def _():
            dk_ref[...] = jnp.zeros_like(dk_ref)
            dv_ref[...] = jnp.zeros_like(dv_ref)

        k_ = k_ref[...]
        vb = _bf(v_ref[...])
        lowq = lax.broadcasted_iota(jnp.int32, (tq, LANE), 1) < ATT_HEAD_DIM
        dk_acc = jnp.zeros((t, LANE), F32)
        dv_acc = jnp.zeros((t, LANE), F32)
        for j in range(4):
            kvh = j // 2
            dop = do_ref[:, j * LANE:(j + 1) * LANE].astype(F32)
            prod = dop * o_ref[:, j * LANE:(j + 1) * LANE]
            d_low = jnp.sum(jnp.where(lowq, prod, 0.0), axis=-1, keepdims=True)
            d_sub = (d_low, jnp.sum(prod, axis=-1, keepdims=True) - d_low)
            for sub in range(2):
                h = 2 * j + sub
                src = dop if sub == kvh else pltpu.roll(dop, ATT_HEAD_DIM, 1)
                do_h = _bf(jnp.where(lowq, src, 0.0) if kvh == 0 else jnp.where(lowq, 0.0, src))
                qh = q_ref[:, h * LANE:(h + 1) * LANE]
                pr = jnp.exp(_bf(_dot(qh, k_, 1, 1) - lse_ref[h]))
                ds = pr * _bf(_dot(do_h, vb, 1, 1) - d_sub[sub])
                dq_ref[:, h * LANE:(h + 1) * LANE] = _dot(ds, k_, 1, 0)
                dk_acc = dk_acc + _dot(ds, qh, 0, 0)
                dv_acc = dv_acc + _dot(pr, do_h, 0, 0)
        dk_ref[...] += dk_acc
        dv_ref[...] += dv_acc

    row = pl.BlockSpec((tq, ATT_Q_DIM), lambda i: (i, 0))
    return _call(
        body, (q, k, p, dcat, o32, lse), name="attn_bwd", grid=(t // tq,),
        in_specs=[pl.BlockSpec((tq, ATT_HEADS * LANE), lambda i: (i, 0)), _full((t, LANE)),
                  pl.BlockSpec((t, LANE), lambda i: (0, v_blk)), row, row,
                  pl.BlockSpec((ATT_HEADS, tq, 1), lambda i: (0, i, 0))],
        out_specs=[pl.BlockSpec((tq, ATT_HEADS * LANE), lambda i: (i, 0)), _full((t, LANE)), _full((t, LANE))],
        out_shape=[jax.ShapeDtypeStruct((t, ATT_HEADS * LANE), F32), jax.ShapeDtypeStruct((t, LANE), F32),
                   jax.ShapeDtypeStruct((t, LANE), F32)])


def _lower_bounds(a0, a1):
    def body(a0_ref, a1_ref, lb_ref):
        m = jnp.maximum(a0_ref[...], a1_ref[...])
        e0, e1 = jnp.exp(a0_ref[...] - m), jnp.exp(a1_ref[...] - m)
        lb_ref[...] = e0 / (e0 + e1)

    return pl.pallas_call(body, name="lower_bounds", out_shape=jax.ShapeDtypeStruct(a0.shape, F32))(a0, a1)


def _lower_bounds_bwd(a0, a1, dlb):
    def body(a0_ref, a1_ref, dlb_ref, d0_ref, d1_ref):
        m = jnp.maximum(a0_ref[...], a1_ref[...])
        e0, e1 = jnp.exp(a0_ref[...] - m), jnp.exp(a1_ref[...] - m)
        lb = e0 / (e0 + e1)
        d0 = dlb_ref[...] * lb * (1.0 - lb)
        d0_ref[...] = d0
        d1_ref[...] = -d0

    return pl.pallas_call(body, name="lower_bounds_bwd", out_shape=[jax.ShapeDtypeStruct(a0.shape, F32)] * 2)(a0, a1, dlb)


def _chunk_scan(x, pos, down):
    n = x.shape[0]
    s = 1
    while s < HG_CHUNK:
        if down:
            x = x + jnp.where(pos >= s, pltpu.roll(x, s, 0), 0.0)
        else:
            x = x + jnp.where(pos < HG_CHUNK - s, pltpu.roll(x, n - s, 0), 0.0)
        s *= 2
    return x


def _chunk_concat(x, nc):
    xb = _bf(x)
    rows = []
    for c in range(nc):
        pieces = [xb[c * HG_CHUNK:(c + 1) * HG_CHUNK, :]]
        if c:
            pieces.insert(0, jnp.zeros((HG_CHUNK, c * LANE), BF16))
        if c < nc - 1:
            pieces.append(jnp.zeros((HG_CHUNK, (nc - 1 - c) * LANE), BF16))
        rows.append(jnp.concatenate(pieces, axis=1) if len(pieces) > 1 else pieces[0])
    return jnp.concatenate(rows, axis=0) if nc > 1 else rows[0]


def _chunk_pick(x, nc):
    rows = [x[c * HG_CHUNK:(c + 1) * HG_CHUNK, c * LANE:(c + 1) * LANE] for c in range(nc)]
    return jnp.concatenate(rows, axis=0) if nc > 1 else rows[0]


def _chunk_rows(vals):
    rows = [jnp.broadcast_to(v, (HG_CHUNK, LANE)) for v in vals]
    return jnp.concatenate(rows, axis=0) if len(rows) > 1 else rows[0]


def _hgrn_gates(hq, z, lb):
    sq = _sigmoid(hq)
    sig = _sigmoid(z)
    f = lb + (1.0 - lb) * sig
    return hq * sq, sq, sig, f, jnp.log(f)


def _hgrn_local(q, f, g, v, pos, amask, rev, nc):
    k = 1.0 - f
    ba = _chunk_scan(g, pos, not rev)
    bb = _chunk_scan(g, pos, rev)
    eq = 0.5 * (ba - bb + g)
    ex_q, ex_k, ex_i, ex_d = jnp.exp(eq), jnp.exp(-eq), jnp.exp(ba), jnp.exp(bb - g)
    qb, kb, qi, kd = q * ex_q, k * ex_k, q * ex_i, k * ex_d
    dvec = jnp.exp(ba + bb - g)
    a = jnp.where(amask, _dot(_bf(qb), _bf(kb), 1, 1), 0.0)
    kd_m = _chunk_concat(kd, nc)
    qi_m = _chunk_concat(qi, nc)
    ut_all = _dot(_bf(v), kd_m, 0, 0)
    return dict(k=k, ex_q=ex_q, ex_k=ex_k, ex_i=ex_i, ex_d=ex_d, qb=qb, kb=kb, qi=qi, kd=kd, dvec=dvec, a=a,
                kd_m=kd_m, qi_m=qi_m, ut_all=ut_all)


def _hgrn_masks(n, rev):
    row = lax.broadcasted_iota(jnp.int32, (n, LANE), 0)
    ti = lax.broadcasted_iota(jnp.int32, (n, n), 0)
    si = lax.broadcasted_iota(jnp.int32, (n, n), 1)
    tri = (si >= ti) if rev else (si <= ti)
    same = jnp.right_shift(ti, HG_CHUNK_LOG2) == jnp.right_shift(si, HG_CHUNK_LOG2)
    return jnp.bitwise_and(row, HG_CHUNK - 1), same & tri


def _hgrn_specs(t, rev, bwd):
    n = min(HG_BLOCK_BWD if bwd else HG_BLOCK_FWD, t)
    nb = t // n
    if rev != bwd:
        blk = lambda i: nb - 1 - i
    else:
        blk = lambda i: i
    col = lambda base: pl.BlockSpec((n, 2 * LANE), lambda hp, i: (blk(i), base + hp))
    return n, nb, blk, col


def _hgrn_fwd(p, lb, rev, after=()):
    t = p.shape[0]
    n, nb, blk, col = _hgrn_specs(t, rev, False)
    nc = n // HG_CHUNK
    sub = n // min(HG_BLOCK_BWD, t)
    z_base = 7 if rev else 5

    def body(hq_ref, z_ref, hi_ref, lb_ref, o_ref, ssave_ref, st_scr):
        @pl.when(pl.program_id(1) == 0)
        def _():
            st_scr[...] = jnp.zeros_like(st_scr)

        pos, amask = _hgrn_masks(n, rev)
        order = list(range(nc))[::-1] if rev else list(range(nc))
        for hh in range(2):
            sl = slice(hh * LANE, (hh + 1) * LANE)
            q, _, _, f, g = _hgrn_gates(hq_ref[:, sl].astype(F32), z_ref[:, sl].astype(F32), lb_ref[0:1, sl])
            v = hi_ref[:, sl].astype(F32)
            c_ = _hgrn_local(q, f, g, v, pos, amask, rev, nc)
            st = st_scr[hh]
            cols = [None] * nc
            for c in order:
                cols[c] = st
                st = st * c_["dvec"][c * HG_CHUNK:c * HG_CHUNK + 1, :] + c_["ut_all"][:, c * LANE:(c + 1) * LANE]
            st_scr[hh] = st
            for s in range(sub):
                ssave_ref[sub - 1 - s if rev else s, sl, :] = cols[order[s * (nc // sub)]]
            st_all = _bf(jnp.concatenate(cols, axis=1))
            o_ref[:, sl] = _dot(_bf(c_["a"]), _bf(v), 1, 0) + _dot(c_["qi_m"], st_all, 1, 1)

    return _call(
        body, (p, p, p, lb), name="hgrn_fwd_rev" if rev else "hgrn_fwd", grid=(2, nb),
        in_specs=[col(3), col(z_base), col(9), pl.BlockSpec((1, 2 * LANE), lambda hp, i: (0, hp))],
        out_specs=[pl.BlockSpec((n, 2 * LANE), lambda hp, i: (blk(i), hp)),
                   pl.BlockSpec((sub, 2 * LANE, LANE), lambda hp, i: (blk(i), hp, 0))],
        out_shape=[jax.ShapeDtypeStruct((t, HG_DIM), F32), jax.ShapeDtypeStruct((nb * sub, HG_DIM, LANE), F32)],
        scratch_shapes=[pltpu.VMEM((2, LANE, LANE), F32)], after=after)


def _hgrn_bwd(p, lb, do, ssave, rev, other=None, after=()):
    t = p.shape[0]
    n, nb, blk, col = _hgrn_specs(t, rev, True)
    nc = n // HG_CHUNK
    z_base = 7 if rev else 5
    n_other = 0 if other is None else 2

    def body(*refs):
        hq_ref, z_ref, hi_ref, lb_ref, do_ref, ssave_ref = refs[:6]
        dhq_ref, dz_ref, dhi_ref, dlb_ref, dst_scr = refs[6 + n_other:]

        @pl.when(pl.program_id(1) == 0)
        def _():
            dst_scr[...] = jnp.zeros_like(dst_scr)
            dlb_ref[...] = jnp.zeros_like(dlb_ref)

        pos, amask = _hgrn_masks(n, rev)
        order = list(range(nc))[::-1] if rev else list(range(nc))
        for hh in range(2):
            sl = slice(hh * LANE, (hh + 1) * LANE)
            hq, lbv = hq_ref[:, sl].astype(F32), lb_ref[0:1, sl]
            q, sq, sig, f, g = _hgrn_gates(hq, z_ref[:, sl].astype(F32), lbv)
            v = hi_ref[:, sl].astype(F32)
            c_ = _hgrn_local(q, f, g, v, pos, amask, rev, nc)
            dvec, ut_all = c_["dvec"], c_["ut_all"]
            drow = lambda c: dvec[c * HG_CHUNK:c * HG_CHUNK + 1, :]
            st = ssave_ref[0, sl, :]
            cols = [None] * nc
            for c in order:
                cols[c] = st
                st = st * drow(c) + ut_all[:, c * LANE:(c + 1) * LANE]
            dob, vb = _bf(do_ref[:, sl]), _bf(v)
            gt_all = _dot(dob, c_["qi_m"], 0, 0)
            dcur = dst_scr[hh]
            dnext = [None] * nc
            ddrow = [None] * nc
            for c in order[::-1]:
                dnext[c] = dcur
                ddrow[c] = jnp.sum(cols[c] * dcur, axis=0, keepdims=True) * drow(c)
                dcur = dcur * drow(c) + gt_all[:, c * LANE:(c + 1) * LANE]
            dst_scr[hh] = dcur
            dsn_all = _bf(jnp.concatenate(dnext, axis=1))
            st_all = _bf(jnp.concatenate(cols, axis=1))
            da = _bf(jnp.where(amask, _dot(dob, vb, 1, 1), 0.0))
            dv = _dot(_bf(c_["a"]), dob, 0, 0) + _dot(c_["kd_m"], dsn_all, 1, 1)
            dqb = _dot(da, _bf(c_["kb"]), 1, 0)
            dkb = _dot(da, _bf(c_["qb"]), 0, 0)
            dqi = _chunk_pick(_dot(dob, st_all, 1, 0), nc)
            dkd = _chunk_pick(_dot(vb, dsn_all, 1, 0), nc)
            dq = dqb * c_["ex_q"] + dqi * c_["ex_i"]
            dk = dkb * c_["ex_k"] + dkd * c_["ex_d"]
            e = dqb * c_["qb"] - dkb * c_["kb"] + dqi * c_["qi"]
            w = dkd * c_["kd"]
            dg = _chunk_scan(e, pos, rev) + (_chunk_scan(w, pos, not rev) - w) + _chunk_rows(ddrow)
            df = dg / f - dk
            dz_ref[:, sl] = _bf(df * (1.0 - lbv) * sig * (1.0 - sig))
            dlb_ref[0:1, sl] += jnp.sum(df * (1.0 - sig), axis=0, keepdims=True)
            dhq = dq * (sq * (1.0 + hq * (1.0 - sq)))
            if other is None:
                dhq_ref[:, sl], dhi_ref[:, sl] = dhq, dv
            else:
                dhq_ref[:, sl], dhi_ref[:, sl] = _bf(dhq + refs[6][:, sl]), _bf(dv + refs[7][:, sl])

    out = pl.BlockSpec((n, 2 * LANE), lambda hp, i: (blk(i), hp))
    sum_dtype = F32 if other is None else BF16
    return _call(
        body, (p, p, p, lb, do, ssave, *(other or ())), name="hgrn_bwd_rev" if rev else "hgrn_bwd", grid=(2, nb),
        in_specs=[col(3), col(z_base), col(9), pl.BlockSpec((1, 2 * LANE), lambda hp, i: (0, hp)), out,
                  pl.BlockSpec((1, 2 * LANE, LANE), lambda hp, i: (blk(i), hp, 0))] + [out] * n_other,
        out_specs=[out, out, out, pl.BlockSpec((1, 2 * LANE), lambda hp, i: (0, hp))],
        out_shape=[jax.ShapeDtypeStruct((t, HG_DIM), sum_dtype), jax.ShapeDtypeStruct((t, HG_DIM), BF16),
                   jax.ShapeDtypeStruct((t, HG_DIM), sum_dtype), jax.ShapeDtypeStruct((1, HG_DIM), F32)],
        scratch_shapes=[pltpu.VMEM((2, LANE, LANE), F32)], after=after)


def _mix_out(att, of, ob, p, gout, w_out, g_post, x, after=(), tm=256):
    t, d = x.shape
    tm = min(tm, t)

    def body(att_ref, of_ref, ob_ref, hg0_ref, hg1_ref, gout_ref, w_ref, g_ref, x_ref, cat_ref, y_ref, x1_ref):
        cat_ref[:, :ATT_Q_DIM] = att_ref[...]
        for h in range(HG_HEADS):
            sl = slice(h * LANE, (h + 1) * LANE)
            hg_ref = hg0_ref if h < 2 else hg1_ref
            hg = hg_ref[:, (h % 2) * LANE:(h % 2 + 1) * LANE].astype(F32)
            nrm = _rms_fwd(of_ref[:, sl] + ob_ref[:, sl], gout_ref[...])
            cat_ref[:, ATT_Q_DIM + h * LANE:ATT_Q_DIM + (h + 1) * LANE] = _bf(nrm * hg * _sigmoid(hg))
        y = _dot(cat_ref[...], w_ref[...], 1, 0)
        y_ref[...] = _bf(y)
        x1_ref[...] = x_ref[...] + _rms_fwd(y, g_ref[...])

    row = lambda width: pl.BlockSpec((tm, width), lambda i: (i, 0))
    return _call(
        body, (att, of, ob, p, p, gout, w_out, g_post, x), name="mix_out", grid=(t // tm,),
        in_specs=[row(ATT_Q_DIM), row(HG_DIM), row(HG_DIM), pl.BlockSpec((tm, 2 * LANE), lambda i: (i, 11)),
                  pl.BlockSpec((tm, 2 * LANE), lambda i: (i, 12)), _full((1, LANE)), _full((d, d)), _full((1, d)), row(d)],
        out_specs=[row(d), row(d), row(d)],
        out_shape=[jax.ShapeDtypeStruct((t, d), BF16), jax.ShapeDtypeStruct((t, d), BF16), jax.ShapeDtypeStruct((t, d), F32)],
        after=after)


def _rec_bwd(dcat, of, ob, p, gout, tm=256):
    t = of.shape[0]
    tm = min(tm, t)

    def body(dc_ref, of_ref, ob_ref, hg0_ref, hg1_ref, gout_ref, do_ref, dhg_ref, dgo_ref):
        @pl.when(pl.program_id(0) == 0)
        def _():
            dgo_ref[...] = jnp.zeros_like(dgo_ref)

        dgo = jnp.zeros((1, LANE), F32)
        for h in range(HG_HEADS):
            sl = slice(h * LANE, (h + 1) * LANE)
            hg_ref = hg0_ref if h < 2 else hg1_ref
            hg = hg_ref[:, (h % 2) * LANE:(h % 2 + 1) * LANE].astype(F32)
            o = of_ref[:, sl] + ob_ref[:, sl]
            sg = _sigmoid(hg)
            drec = dc_ref[:, sl].astype(F32)
            dhg_ref[:, sl] = _bf(drec * _rms_fwd(o, gout_ref[...]) * (sg * (1.0 + hg * (1.0 - sg))))
            do, dg = _rms_bwd(drec * hg * sg, o, gout_ref[...])
            do_ref[:, sl] = do
            dgo = dgo + dg
        dgo_ref[...] += dgo

    row = lambda width: pl.BlockSpec((tm, width), lambda i: (i, 0))
    return pl.pallas_call(
        body, name="rec_bwd", grid=(t // tm,),
        in_specs=[pl.BlockSpec((tm, HG_DIM), lambda i: (i, 1)), row(HG_DIM), row(HG_DIM),
                  pl.BlockSpec((tm, 2 * LANE), lambda i: (i, 11)), pl.BlockSpec((tm, 2 * LANE), lambda i: (i, 12)),
                  _full((1, LANE))],
        out_specs=[row(HG_DIM), row(HG_DIM), _full((1, LANE))],
        out_shape=[jax.ShapeDtypeStruct((t, HG_DIM), F32), jax.ShapeDtypeStruct((t, HG_DIM), BF16),
                   jax.ShapeDtypeStruct((1, LANE), F32)],
        compiler_params=_params(),
    )(dcat, of, ob, p, p, gout)


def _xattn_fwd(q, kv, tq=512):
    t, d = q.shape
    tq = min(tq, t)
    nm = kv.shape[0]

    def body(q_ref, kv_ref, o_ref):
        for h in range(X_HEADS):
            sl = slice(h * X_HEAD_DIM, (h + 1) * X_HEAD_DIM)
            s = _dot(_bf(q_ref[:, sl]), _bf(kv_ref[:, sl]), 1, 1) * (X_HEAD_DIM ** -0.5)
            e = jnp.exp(s - jnp.max(s, axis=-1, keepdims=True))
            pr = e * (1.0 / jnp.sum(e, axis=-1, keepdims=True))
            o_ref[:, sl] = _bf(_dot(_bf(pr), _bf(kv_ref[:, d + h * X_HEAD_DIM:d + (h + 1) * X_HEAD_DIM]), 1, 0))

    return pl.pallas_call(
        body, name="xattn_fwd", grid=(t // tq,),
        in_specs=[pl.BlockSpec((tq, d), lambda i: (i, 0)), _full((nm, 2 * d))],
        out_specs=pl.BlockSpec((tq, d), lambda i: (i, 0)),
        out_shape=jax.ShapeDtypeStruct((t, d), BF16),
        compiler_params=_params(),
    )(q, kv)


def _xattn_bwd(q, kv, do, tq=512):
    t, d = q.shape
    tq = min(tq, t)
    nm = kv.shape[0]

    def body(q_ref, kv_ref, do_ref, dq_ref, dkv_ref):
        @pl.when(pl.program_id(0) == 0)
        def _():
            dkv_ref[...] = jnp.zeros_like(dkv_ref)

        for h in range(X_HEADS):
            sl = slice(h * X_HEAD_DIM, (h + 1) * X_HEAD_DIM)
            slv = slice(d + h * X_HEAD_DIM, d + (h + 1) * X_HEAD_DIM)
            qb, kb, vb, dob = _bf(q_ref[:, sl]), _bf(kv_ref[:, sl]), _bf(kv_ref[:, slv]), _bf(do_ref[:, sl])
            s = _dot(qb, kb, 1, 1) * (X_HEAD_DIM ** -0.5)
            e = jnp.exp(s - jnp.max(s, axis=-1, keepdims=True))
            pr = e * (1.0 / jnp.sum(e, axis=-1, keepdims=True))
            dpr = _dot(dob, vb, 1, 1)
            ds = _bf(pr * (dpr - jnp.sum(pr * dpr, axis=-1, keepdims=True)) * (X_HEAD_DIM ** -0.5))
            dq_ref[:, sl] = _bf(_dot(ds, kb, 1, 0))
            dkv_ref[:, sl] += _dot(ds, qb, 0, 0)
            dkv_ref[:, slv] += _dot(_bf(pr), dob, 0, 0)

    return pl.pallas_call(
        body, name="xattn_bwd", grid=(t // tq,),
        in_specs=[pl.BlockSpec((tq, d), lambda i: (i, 0)), _full((nm, 2 * d)), pl.BlockSpec((tq, d), lambda i: (i, 0))],
        out_specs=[pl.BlockSpec((tq, d), lambda i: (i, 0)), _full((nm, 2 * d))],
        out_shape=[jax.ShapeDtypeStruct((t, d), BF16), jax.ShapeDtypeStruct((nm, 2 * d), F32)],
        compiler_params=_params(),
    )(q, kv, do)


CONV_TN = 256


def _shift_rows(u, row, t, delta):
    if delta < 0:
        return jnp.where(row == 0, 0.0, pltpu.roll(u, 1, 0))
    return jnp.where(row == t - 1, 0.0, pltpu.roll(u, t - 1, 0))


def _conv_gate_fwd(u, conv_w, conv_b):
    t = u.shape[0]
    nt = D_FF // CONV_TN

    def body(ug_ref, uv_ref, wg_ref, wv_ref, bg_ref, bv_ref, a_ref):
        row = lax.broadcasted_iota(jnp.int32, (t, CONV_TN), 0)

        def conv(u_ref, w_ref, b_ref):
            uu = u_ref[...]
            return (b_ref[...] + _shift_rows(uu, row, t, -1) * w_ref[0:1, :] + uu * w_ref[1:2, :]
                    + _shift_rows(uu, row, t, 1) * w_ref[2:3, :])

        gate = conv(ug_ref, wg_ref, bg_ref)
        a_ref[...] = _bf(gate * _sigmoid(gate) * conv(uv_ref, wv_ref, bv_ref))

    col = lambda rows, off: pl.BlockSpec((rows, CONV_TN), lambda j: (0, j + off))
    return pl.pallas_call(
        body, name="conv_gate_fwd", grid=(nt,),
        in_specs=[col(t, 0), col(t, nt), col(3, 0), col(3, nt), col(1, 0), col(1, nt)],
        out_specs=col(t, 0),
        out_shape=jax.ShapeDtypeStruct((t, D_FF), BF16),
        compiler_params=_params(),
    )(u, u, conv_w, conv_w, conv_b, conv_b)


def _conv_gate_bwd(u, conv_w, conv_b, da, after=()):
    t = u.shape[0]
    nt = D_FF // CONV_TN

    def body(ug_ref, uv_ref, wg_ref, wv_ref, bg_ref, bv_ref, da_ref, dug_ref, dwg_ref, dbg_ref, duv_ref, dwv_ref, dbv_ref):
        row = lax.broadcasted_iota(jnp.int32, (t, CONV_TN), 0)
        ug, uv = ug_ref[...], uv_ref[...]
        ug_m, ug_p = _shift_rows(ug, row, t, -1), _shift_rows(ug, row, t, 1)
        uv_m, uv_p = _shift_rows(uv, row, t, -1), _shift_rows(uv, row, t, 1)
        gate = bg_ref[...] + ug_m * wg_ref[0:1, :] + ug * wg_ref[1:2, :] + ug_p * wg_ref[2:3, :]
        val = bv_ref[...] + uv_m * wv_ref[0:1, :] + uv * wv_ref[1:2, :] + uv_p * wv_ref[2:3, :]
        sg = _sigmoid(gate)
        da_ = da_ref[...].astype(F32)

        def emit(dacc, um, uu, up, w_ref, du_ref, dw_ref, db_ref):
            du_ref[...] = _bf(_shift_rows(dacc, row, t, 1) * w_ref[0:1, :] + dacc * w_ref[1:2, :]
                              + _shift_rows(dacc, row, t, -1) * w_ref[2:3, :])
            dw_ref[0:1, :] = jnp.sum(dacc * um, axis=0, keepdims=True)
            dw_ref[1:2, :] = jnp.sum(dacc * uu, axis=0, keepdims=True)
            dw_ref[2:3, :] = jnp.sum(dacc * up, axis=0, keepdims=True)
            db_ref[...] = jnp.sum(dacc, axis=0, keepdims=True)

        emit(da_ * val * (sg * (1.0 + gate * (1.0 - sg))), ug_m, ug, ug_p, wg_ref, dug_ref, dwg_ref, dbg_ref)
        emit(da_ * gate * sg, uv_m, uv, uv_p, wv_ref, duv_ref, dwv_ref, dbv_ref)

    col = lambda rows, off: pl.BlockSpec((rows, CONV_TN), lambda j: (0, j + off))
    half_shapes = [jax.ShapeDtypeStruct((t, D_FF), BF16), jax.ShapeDtypeStruct((3, D_FF), F32),
                   jax.ShapeDtypeStruct((1, D_FF), F32)]
    outs = _call(
        body, (u, u, conv_w, conv_w, conv_b, conv_b, da), name="conv_gate_bwd", grid=(nt,),
        in_specs=[col(t, 0), col(t, nt), col(3, 0), col(3, nt), col(1, 0), col(1, nt), col(t, 0)],
        out_specs=[col(t, 0), col(3, 0), col(1, 0)] * 2, out_shape=half_shapes * 2, after=after)
    return outs[:3], outs[3:]


def _row_tile(r, cap):
    best = r
    for cand in range(16, cap + 1, 16):
        if r % cand == 0:
            best = cand
    return best


def _sum_parts(own, got, me, name, tr=256):
    _, r, c = got.shape
    tr = _row_tile(r, tr)

    def body(me_ref, own_ref, got_ref, o_ref):
        mine = own_ref[...].astype(F32)
        acc = None
        for i in range(N_DEV):
            term = jnp.where(me_ref[0] == i, mine, got_ref[i].astype(F32))
            acc = term if acc is None else acc + term
        o_ref[...] = acc

    return pl.pallas_call(
        body, name=name,
        grid_spec=pltpu.PrefetchScalarGridSpec(
            num_scalar_prefetch=1, grid=(r // tr,),
            in_specs=[pl.BlockSpec((None, tr, c), lambda i, me_ref: (me_ref[0], i, 0)),
                      pl.BlockSpec((N_DEV, tr, c), lambda i, me_ref: (0, i, 0))],
            out_specs=pl.BlockSpec((tr, c), lambda i, me_ref: (i, 0))),
        out_shape=jax.ShapeDtypeStruct((r, c), F32),
        compiler_params=_params(),
    )(me, own, got)


def _adamw_math(w_, g_, m_, v_):
    m_ = ADAM_B1 * m_ + (1.0 - ADAM_B1) * g_
    v_ = ADAM_B2 * v_ + (1.0 - ADAM_B2) * (g_ * g_)
    m_hat = m_ / (1.0 - ADAM_B1 ** ADAM_STEP)
    v_hat = v_ / (1.0 - ADAM_B2 ** ADAM_STEP)
    return -ADAM_LR * (m_hat / (jnp.sqrt(v_hat) + ADAM_EPS) + ADAM_WD * w_), m_, v_


def _sum_adamw(own, got, me, w, m, v, name, tr=256):
    _, r, c = got.shape
    tr = _row_tile(r, tr)

    def body(me_ref, own_ref, got_ref, w_ref, m_ref, v_ref, g_ref, d_ref, nm_ref, nv_ref):
        mine = own_ref[...].astype(F32)
        acc = None
        for i in range(N_DEV):
            term = jnp.where(me_ref[0] == i, mine, got_ref[i].astype(F32))
            acc = term if acc is None else acc + term
        g_ref[...] = acc
        d_ref[...], nm_ref[...], nv_ref[...] = _adamw_math(w_ref[...], acc, m_ref[...], v_ref[...])

    blk = pl.BlockSpec((tr, c), lambda i, me_ref: (i, 0))
    return pl.pallas_call(
        body, name=name,
        grid_spec=pltpu.PrefetchScalarGridSpec(
            num_scalar_prefetch=1, grid=(r // tr,),
            in_specs=[pl.BlockSpec((None, tr, c), lambda i, me_ref: (me_ref[0], i, 0)),
                      pl.BlockSpec((N_DEV, tr, c), lambda i, me_ref: (0, i, 0)), blk, blk, blk],
            out_specs=[blk] * 4),
        out_shape=[jax.ShapeDtypeStruct((r, c), F32)] * 4,
        compiler_params=_params(),
    )(me, own, got, w, m, v)


def _adamw(w, g, m, v, name, tr=256):
    r, c = w.shape
    tr = _row_tile(r, tr)

    def body(w_ref, g_ref, m_ref, v_ref, d_ref, nm_ref, nv_ref):
        d_ref[...], nm_ref[...], nv_ref[...] = _adamw_math(w_ref[...], g_ref[...], m_ref[...], v_ref[...])

    blk = pl.BlockSpec((tr, c), lambda i: (i, 0))
    return pl.pallas_call(
        body, name=name, grid=(r // tr,),
        in_specs=[blk] * 4, out_specs=[blk] * 3,
        out_shape=[jax.ShapeDtypeStruct((r, c), F32)] * 3,
        compiler_params=_params(),
    )(w, g, m, v)


def _mesh_pos():
    return lax.axis_index("x"), lax.axis_index("y"), lax.axis_index("c")


def _dev_index(px, py, pc):
    return 4 * px + 2 * py + pc


class _Gather:
    def __init__(self, arrs):
        self.arrs = list(arrs)
        n = len(self.arrs)
        self.out_shape = [jax.ShapeDtypeStruct((N_DEV,) + a.shape, a.dtype) for a in self.arrs]
        self.sems = [pltpu.SemaphoreType.DMA((7, n)), pltpu.SemaphoreType.DMA((7, n)), pltpu.SemaphoreType.DMA((n,))]

    def _ctx(self, ins, outs, sems):
        send_sems, recv_sems, local_sems = sems
        x, y, c = _mesh_pos()
        chips = [(1 - x, y), (x, 1 - y), (1 - x, 1 - y)]

        def copy(k, a, block, to, src=None):
            dst = outs[a].at[_dev_index(*block)]
            return pltpu.make_async_remote_copy(
                src_ref=dst if src is None else src, dst_ref=dst, send_sem=send_sems.at[k, a], recv_sem=recv_sems.at[k, a],
                device_id=to, device_id_type=MESH_T)

        n = len(ins)
        me, sibling = (x, y, c), (x, y, 1 - c)
        mine = [pltpu.make_async_copy(ins[a], outs[a].at[_dev_index(*me)], local_sems.at[a]) for a in range(n)]
        first = [copy(0, a, me, sibling, src=ins[a]) for a in range(n)]
        first += [copy(1 + j, a, me, (*chip, c), src=ins[a]) for j, chip in enumerate(chips) for a in range(n)]
        passed = [copy(4 + j, a, (*chip, c), sibling) for j, chip in enumerate(chips) for a in range(n)]
        return n, c, me, sibling, chips, copy, mine, first, passed

    def start(self, ins, outs, sems):
        _, _, _, _, _, _, mine, first, _ = self._ctx(ins, outs, sems)
        for cp in mine + first:
            cp.start()

    def forward(self, ins, outs, sems):
        n, c, me, _, chips, copy, _, _, passed = self._ctx(ins, outs, sems)
        for j, chip in enumerate(chips):
            for a in range(n):
                copy(1 + j, a, (*chip, c), me).wait_recv()
                passed[j * n + a].start()

    def finish(self, ins, outs, sems):
        n, c, me, sibling, chips, copy, mine, first, passed = self._ctx(ins, outs, sems)
        for a in range(n):
            copy(0, a, sibling, me).wait_recv()
        for j, chip in enumerate(chips):
            for a in range(n):
                copy(4 + j, a, (*chip, 1 - c), me).wait_recv()
        for cp in first + passed:
            cp.wait_send()
        for cp in mine:
            cp.wait()


def _comm_alone(comm, name):
    n = len(comm.arrs)

    def body(*refs):
        ins, outs, sems = refs[:n], refs[n:2 * n], refs[2 * n:]
        comm.start(ins, outs, sems)
        if comm.forward is not None:
            comm.forward(ins, outs, sems)
        comm.finish(ins, outs, sems)

    any_spec = pl.BlockSpec(memory_space=pl.ANY)
    return pl.pallas_call(body, name=name, in_specs=[any_spec] * n, out_specs=[any_spec] * n, out_shape=comm.out_shape,
                          scratch_shapes=comm.sems)(*comm.arrs)


def _peers(x, y, c):
    out = []
    for k in range(1, N_DEV):
        pos = (1 - x if k & 4 else x, 1 - y if k & 2 else y, 1 - c if k & 1 else c)
        out.append((k, pos, _dev_index(*pos)))
    return out


def _exchange_begin(arrs, tag):
    n = len(arrs)
    lands = [lax.empty(a.shape, a.dtype) for a in arrs]

    def start_body(*refs):
        ins, land = refs[:n], refs[n:2 * n]
        send_sems, recv_sems, token = refs[2 * n], refs[2 * n + 1], refs[-1]
        x, y, c = _mesh_pos()
        me_i = _dev_index(x, y, c)
        for k, pos, peer_i in _peers(x, y, c):
            for a in range(n):
                pltpu.make_async_remote_copy(
                    src_ref=ins[a].at[peer_i], dst_ref=land[a].at[me_i], send_sem=send_sems.at[(k - 1) * n + a],
                    recv_sem=recv_sems.at[(k - 1) * n + a], device_id=pos, device_id_type=MESH_T).start()
        token[...] = jnp.zeros_like(token)

    hbm = pl.BlockSpec(memory_space=pltpu.HBM)
    sem = pl.BlockSpec(memory_space=pltpu.SEMAPHORE)
    thru = [pltpu.HBM(a.shape, a.dtype) for a in arrs]
    outs = pl.pallas_call(
        start_body, name="exchange_start_" + tag,
        out_shape=[pltpu.SemaphoreType.DMA((7 * n,)), pltpu.SemaphoreType.DMA((7 * n,))] + thru + thru
        + [jax.ShapeDtypeStruct((8, LANE), F32)],
        in_specs=[hbm] * (2 * n), out_specs=[sem, sem] + [hbm] * (2 * n) + [pl.BlockSpec(memory_space=pltpu.VMEM)],
        input_output_aliases={i: 2 + i for i in range(2 * n)},
        compiler_params=pltpu.CompilerParams(has_side_effects=pltpu.SideEffectType.DATAFLOW_SIDE_EFFECTING),
    )(*[pltpu.with_memory_space_constraint(a, pltpu.HBM) for a in list(arrs) + list(lands)])
    return (tag, n, outs[0], outs[1], outs[2:2 + n], outs[2 + n:2 + 2 * n]), outs[-1]


def _exchange_end(handle, after):
    tag, n, send_sems, recv_sems, srcs, lands = handle

    def body(*refs):
        ins, land = refs[:n], refs[n:2 * n]
        send_sems_, recv_sems_ = refs[2 * n], refs[2 * n + 1]
        x, y, c = _mesh_pos()
        me_i = _dev_index(x, y, c)
        for k, pos, peer_i in _peers(x, y, c):
            for a in range(n):
                cp = pltpu.make_async_remote_copy(
                    src_ref=ins[a].at[peer_i], dst_ref=land[a].at[me_i], send_sem=send_sems_.at[(k - 1) * n + a],
                    recv_sem=recv_sems_.at[(k - 1) * n + a], device_id=pos, device_id_type=MESH_T)
                cp.wait_send()
                cp.wait_recv()

    hbm = pl.BlockSpec(memory_space=pltpu.HBM)
    sem = pl.BlockSpec(memory_space=pltpu.SEMAPHORE)
    outs = pl.pallas_call(
        body, name="exchange_end_" + tag, out_shape=[pltpu.HBM(a.shape, a.dtype) for a in list(srcs) + list(lands)],
        in_specs=[hbm] * (2 * n) + [sem, sem, pl.BlockSpec(memory_space=pl.ANY)], out_specs=[hbm] * (2 * n),
        input_output_aliases={i: i for i in range(2 * n)},
        compiler_params=pltpu.CompilerParams(has_side_effects=pltpu.SideEffectType.DATAFLOW_SIDE_EFFECTING),
    )(*srcs, *lands, send_sems, recv_sems, after)
    return list(zip(outs[:n], outs[n:]))


def _gather_begin(arrs, tag):
    n = len(arrs)
    me = _dev_index(*_mesh_pos())
    lands = [lax.dynamic_update_slice(lax.empty((N_DEV,) + a.shape, a.dtype), a[None], (me,) + (0,) * a.ndim) for a in arrs]

    def start_body(*refs):
        ins, land = refs[:n], refs[n:2 * n]
        send_sems, recv_sems, token = refs[2 * n], refs[2 * n + 1], refs[-1]
        x, y, c = _mesh_pos()
        me_i = _dev_index(x, y, c)
        for a in range(n):
            for k, pos, _ in _peers(x, y, c):
                pltpu.make_async_remote_copy(
                    src_ref=ins[a], dst_ref=land[a].at[me_i], send_sem=send_sems.at[(k - 1) * n + a],
                    recv_sem=recv_sems.at[(k - 1) * n + a], device_id=pos, device_id_type=MESH_T).start()
        token[...] = jnp.zeros_like(token)

    hbm = pl.BlockSpec(memory_space=pltpu.HBM)
    sem = pl.BlockSpec(memory_space=pltpu.SEMAPHORE)
    thru = [pltpu.HBM(a.shape, a.dtype) for a in list(arrs) + lands]
    outs = pl.pallas_call(
        start_body, name="gather_start_" + tag,
        out_shape=[pltpu.SemaphoreType.DMA((7 * n,)), pltpu.SemaphoreType.DMA((7 * n,))] + thru
        + [jax.ShapeDtypeStruct((8, LANE), F32)],
        in_specs=[hbm] * (2 * n), out_specs=[sem, sem] + [hbm] * (2 * n) + [pl.BlockSpec(memory_space=pltpu.VMEM)],
        input_output_aliases={i: 2 + i for i in range(2 * n)},
        compiler_params=pltpu.CompilerParams(has_side_effects=pltpu.SideEffectType.DATAFLOW_SIDE_EFFECTING),
    )(*[pltpu.with_memory_space_constraint(a, pltpu.HBM) for a in list(arrs) + lands])
    return (tag, n, outs[0], outs[1], outs[2:2 + n], outs[2 + n:2 + 2 * n]), outs[-1]


def _gather_end(handle, which, after):
    tag, n, send_sems, recv_sems, srcs, lands = handle
    m = len(which)

    def body(*refs):
        ins, land = refs[:m], refs[m:2 * m]
        send_sems_, recv_sems_ = refs[2 * m], refs[2 * m + 1]
        x, y, c = _mesh_pos()
        me_i = _dev_index(x, y, c)
        for j, a in enumerate(which):
            for k, pos, _ in _peers(x, y, c):
                cp = pltpu.make_async_remote_copy(
                    src_ref=ins[j], dst_ref=land[j].at[me_i], send_sem=send_sems_.at[(k - 1) * n + a],
                    recv_sem=recv_sems_.at[(k - 1) * n + a], device_id=pos, device_id_type=MESH_T)
                cp.wait_send()
                cp.wait_recv()

    hbm = pl.BlockSpec(memory_space=pltpu.HBM)
    sem = pl.BlockSpec(memory_space=pltpu.SEMAPHORE)
    ops = [srcs[a] for a in which] + [lands[a] for a in which]
    outs = pl.pallas_call(
        body, name="gather_end_%s_%s" % (tag, "_".join(str(a) for a in which)),
        out_shape=[pltpu.HBM(a.shape, a.dtype) for a in ops],
        in_specs=[hbm] * (2 * m) + [sem, sem, pl.BlockSpec(memory_space=pl.ANY)], out_specs=[hbm] * (2 * m),
        input_output_aliases={i: i for i in range(2 * m)},
        compiler_params=pltpu.CompilerParams(has_side_effects=pltpu.SideEffectType.DATAFLOW_SIDE_EFFECTING),
    )(*ops, send_sems, recv_sems, after)
    return list(outs[m:])


_SPLIT = dict(has_side_effects=pltpu.SideEffectType.DATAFLOW_SIDE_EFFECTING)


def _chips(x, y):
    return [(1 - x, y), (x, 1 - y), (1 - x, 1 - y)]


def _gather2_begin(arrs, tag):
    n = len(arrs)
    me = _dev_index(*_mesh_pos())
    lands = [lax.dynamic_update_slice(lax.empty((N_DEV,) + a.shape, a.dtype), a[None], (me,) + (0,) * a.ndim) for a in arrs]

    def body(*refs):
        ins, land = refs[:n], refs[n:2 * n]
        send1, recv1, token = refs[2 * n], refs[2 * n + 1], refs[-1]
        x, y, c = _mesh_pos()
        me_i = _dev_index(x, y, c)
        targets = [(x, y, 1 - c)] + [(*chip, c) for chip in _chips(x, y)]
        for a in range(n):
            for k, pos in enumerate(targets):
                pltpu.make_async_remote_copy(
                    src_ref=ins[a], dst_ref=land[a].at[me_i], send_sem=send1.at[k * n + a], recv_sem=recv1.at[k * n + a],
                    device_id=pos, device_id_type=MESH_T).start()
        token[...] = jnp.zeros_like(token)

    hbm = pl.BlockSpec(memory_space=pltpu.HBM)
    sem = pl.BlockSpec(memory_space=pltpu.SEMAPHORE)
    thru = [pltpu.HBM(a.shape, a.dtype) for a in list(arrs) + lands]
    outs = pl.pallas_call(
        body, name="gather_start_" + tag,
        out_shape=[pltpu.SemaphoreType.DMA((4 * n,)), pltpu.SemaphoreType.DMA((4 * n,))] + thru
        + [jax.ShapeDtypeStruct((8, LANE), F32)],
        in_specs=[hbm] * (2 * n), out_specs=[sem, sem] + [hbm] * (2 * n) + [pl.BlockSpec(memory_space=pltpu.VMEM)],
        input_output_aliases={i: 2 + i for i in range(2 * n)}, compiler_params=pltpu.CompilerParams(**_SPLIT),
    )(*[pltpu.with_memory_space_constraint(a, pltpu.HBM) for a in list(arrs) + lands])
    return dict(tag=tag, n=n, send1=outs[0], recv1=outs[1], srcs=list(outs[2:2 + n]), lands=list(outs[2 + n:2 + 2 * n])), outs[-1]


def _gather2_forward(handle, which, after, part):
    n, m = handle["n"], len(which)

    def body(*refs):
        land, recv1 = refs[:m], refs[m]
        send2, recv2, token = refs[m + 2], refs[m + 3], refs[-1]
        token[...] = jnp.zeros_like(token)
        x, y, c = _mesh_pos()
        for j, a in enumerate(which):
            for k, chip in enumerate(_chips(x, y)):
                blk = land[j].at[_dev_index(*chip, c)]
                copy = pltpu.make_async_remote_copy(
                    src_ref=blk, dst_ref=blk, send_sem=send2.at[k * m + j], recv_sem=recv2.at[k * m + j],
                    device_id=(x, y, 1 - c), device_id_type=MESH_T)
                pltpu.make_async_remote_copy(
                    src_ref=blk, dst_ref=blk, send_sem=send2.at[k * m + j], recv_sem=recv1.at[(1 + k) * n + a],
                    device_id=(*chip, c), device_id_type=MESH_T).wait_recv()
                copy.start()

    hbm = pl.BlockSpec(memory_space=pltpu.HBM)
    sem = pl.BlockSpec(memory_space=pltpu.SEMAPHORE)
    lands = [handle["lands"][a] for a in which]
    outs = pl.pallas_call(
        body, name="gather_forward_%s_%s" % (handle["tag"], part),
        out_shape=[pltpu.SemaphoreType.DMA((3 * m,)), pltpu.SemaphoreType.DMA((3 * m,))] + [pltpu.HBM(a.shape, a.dtype) for a in lands]
        + [jax.ShapeDtypeStruct((8, LANE), F32)],
        in_specs=[hbm] * m + [sem, pl.BlockSpec(memory_space=pl.ANY)],
        out_specs=[sem, sem] + [hbm] * m + [pl.BlockSpec(memory_space=pltpu.VMEM)],
        input_output_aliases={i: 2 + i for i in range(m)}, compiler_params=pltpu.CompilerParams(**_SPLIT),
    )(*lands, handle["recv1"], after)
    for j, a in enumerate(which):
        handle["lands"][a] = outs[2 + j]
        handle[("leg2", a)] = (outs[0], outs[1], j, m)
    return outs[-1]


def _gather2_end(handle, a, after):
    n = handle["n"]
    send2, recv2, j, m = handle[("leg2", a)]

    def body(src, land, send1, recv1, send2_, recv2_, after_ref, src_out, land_out):
        x, y, c = _mesh_pos()
        me_i = _dev_index(x, y, c)
        sibling = (x, y, 1 - c)
        for k, pos in enumerate([sibling] + [(*chip, c) for chip in _chips(x, y)]):
            first = pltpu.make_async_remote_copy(
                src_ref=src, dst_ref=land.at[me_i], send_sem=send1.at[k * n + a], recv_sem=recv1.at[k * n + a],
                device_id=pos, device_id_type=MESH_T)
            first.wait_send()
            if k == 0:
                first.wait_recv()
        for k in range(3):
            second = pltpu.make_async_remote_copy(
                src_ref=land.at[me_i], dst_ref=land.at[me_i], send_sem=send2_.at[k * m + j], recv_sem=recv2_.at[k * m + j],
                device_id=sibling, device_id_type=MESH_T)
            second.wait_send()
            second.wait_recv()

    hbm = pl.BlockSpec(memory_space=pltpu.HBM)
    sem = pl.BlockSpec(memory_space=pltpu.SEMAPHORE)
    src, land = handle["srcs"][a], handle["lands"][a]
    outs = pl.pallas_call(
        body, name="gather_end_%s_%d" % (handle["tag"], a),
        out_shape=[pltpu.HBM(src.shape, src.dtype), pltpu.HBM(land.shape, land.dtype)],
        in_specs=[hbm, hbm, sem, sem, sem, sem, pl.BlockSpec(memory_space=pl.ANY)], out_specs=[hbm, hbm],
        input_output_aliases={0: 0, 1: 1}, compiler_params=pltpu.CompilerParams(**_SPLIT),
    )(src, land, handle["send1"], handle["recv1"], send2, recv2, after)
    return outs[1]


def _call(body, operands, *, name, grid, in_specs, out_specs, out_shape, scratch_shapes=(), after=()):
    ni, nd = len(in_specs), len(after)

    def ordered(*refs):
        body(*refs[:ni], *refs[ni + nd:])

    outs = pl.pallas_call(
        ordered, name=name, grid=grid, in_specs=list(in_specs) + [pl.BlockSpec(memory_space=pl.ANY)] * nd,
        out_specs=out_specs, out_shape=out_shape, scratch_shapes=list(scratch_shapes), compiler_params=_params(),
    )(*operands, *after)
    return list(outs)


def _rows(a):
    return a.reshape(-1, a.shape[-1])


def _slots(a):
    return a.reshape(N_DEV, -1, a.shape[-1])


def _local_step(x, mem, tgt, w, wire):
    tables = _rope_tables(x.shape[0])
    bd = _head_sum_matrix()
    two = lambda g: jnp.tile(g, (1, 2))
    gq2, gk2 = two(w["q_norm_g"]), two(w["k_norm_g"])

    w_in_t, hg_lb, conv_w = _comm_alone(_Gather([wire["w_in"], w["hg_lb"].reshape(4, -1), w["conv_w"][0]]), "gather_w_in")
    w_in_t = _rows(w_in_t)
    hg_lb = jnp.transpose(hg_lb.reshape(N_DEV, 2, 2, -1), (1, 2, 0, 3)).reshape(2, 2, HG_DIM)
    conv_w = jnp.transpose(conv_w, (1, 0, 2)).reshape(3, 2 * D_FF)
    lb_a0, lb_a1 = hg_lb[:, 0, :], hg_lb[:, 1, :]

    order = ("w_out", "w_xq", "w_xkv", "w_xo", "w_up", "w_down")
    w_in_t, *later = lax.optimization_barrier((w_in_t, *[wire[n] for n in order]))
    fetch, started = _gather2_begin(later, "w")
    take = lambda n, after: _rows(_gather2_end(fetch, order.index(n), after))

    h1, p = _norm_mm(x, w["pre_mix_g"], w_in_t, True, N_IN, "in_proj", after=[started], p_dtype=BF16)
    qr, kr = _qk_prep(p, gq2, gk2, tables, bd)
    att, att32, lse = _attn_fwd(qr, kr, p)
    passed = _gather2_forward(fetch, [0, 1, 2, 3], att, "a")
    lb = _lower_bounds(lb_a0, lb_a1)
    of, s_f = _hgrn_fwd(p, lb[0:1], False, after=[passed])
    ob, s_b = _hgrn_fwd(p, lb[1:2], True, after=[passed, of])
    passed = _gather2_forward(fetch, [4, 5], ob, "b")
    w_out = take("w_out", ob)
    cat, mixed, x1 = _mix_out(att, of, ob, p, w["hg_out_norm_g"], w_out, w["post_mix_g"], x, after=[passed])

    w_xq = take("w_xq", x1)
    h2, q2 = _norm_mm(x1, w["pre_x_g"], w_xq, False, 1024, "xq_proj", p_dtype=BF16)
    w_xkv_t = take("w_xkv", q2)
    mn, kv = _norm_mm(mem, w["mem_norm_g"], w_xkv_t, True, 2 * D_MODEL, "xkv_proj")
    o2 = _xattn_fwd(q2, kv)
    w_xo = take("w_xo", o2)
    y2, x2 = _mm_postnorm_res(o2, w_xo, w["post_x_g"], x1, "xo_proj")

    w_up_t = take("w_up", x2)
    h3, u = _norm_mm(x2, w["pre_ffn_g"], w_up_t, True, 2 * D_FF, "up_proj", tm=256)
    a = _conv_gate_fwd(u, conv_w, w["conv_b"])
    w_down = take("w_down", a)
    y3, dx3, loss = _mm_postnorm_res_loss(a, w_down, w["post_ffn_g"], x2, tgt, "down_proj")

    g, pending = {}, {}
    dy3, da, g["post_ffn_g"] = _postnorm_bwd_mm(dx3, y3, w["post_ffn_g"], w_down, "down_bwd", BF16)
    pending["down"], started = _exchange_begin([_slots(_dw(a, dy3, "dw_down", tka=D_FF // 2))], "down")
    (du_g, dcw_g, dcb_g), (du_v, dcw_v, dcb_v) = _conv_gate_bwd(u, conv_w, w["conv_b"], da, after=[started])
    g["conv_w"] = jnp.concatenate([dcw_g, dcw_v], axis=1)
    g["conv_b"] = jnp.concatenate([dcb_g, dcb_v], axis=1)
    pending["up"], started = _exchange_begin([_slots(_dw([du_g, du_v], h3, "dw_up", tka=D_FF // 2))], "up")
    dx2, g["pre_ffn_g"] = _mm_prenorm_bwd([du_g, du_v], w_up_t, True, x2, w["pre_ffn_g"], dx3, "up_bwd", after=[started], tm=256)

    dy2, do2, g["post_x_g"] = _postnorm_bwd_mm(dx2, y2, w["post_x_g"], w_xo, "xo_bwd", BF16)
    dw_xo = _dw(o2, dy2, "dw_xo", tka=512)
    dq2, dkv = _xattn_bwd(q2, kv, do2)
    dw_xq = _dw(h2, dq2, "dw_xq", tka=512)
    dkvb = _bf(dkv)
    dw_xkv = _dw(dkvb, mn, "dw_xkv")
    _, g["mem_norm_g"] = _mm_prenorm_bwd(dkvb, w_xkv_t, True, mem, w["mem_norm_g"], jnp.zeros_like(mem), "xkv_bwd")
    dx1, g["pre_x_g"] = _mm_prenorm_bwd(dq2, w_xq, False, x1, w["pre_x_g"], dx2, "xq_bwd")

    dmixed, dcat, g["post_mix_g"] = _postnorm_bwd_mm(dx1, mixed, w["post_mix_g"], w_out, "out_bwd", BF16)
    dw_out = _dw(cat, dmixed, "dw_out", tka=512)
    pending["x"], started = _exchange_begin([_slots(dw_xo), _slots(dw_xq), _slots(dw_xkv), _slots(dw_out)], "x")
    do, dhg, g["hg_out_norm_g"] = _rec_bwd(dcat, of, ob, p, w["hg_out_norm_g"])
    dhq_f, dz_f, dhi_f, dlb_f = _hgrn_bwd(p, lb[0:1], do, s_f, False, after=[started])
    dhq, dz_b, dhi, dlb_b = _hgrn_bwd(p, lb[1:2], do, s_b, True, other=(dhq_f, dhi_f))
    d_a0, d_a1 = _lower_bounds_bwd(lb_a0, lb_a1, jnp.concatenate([dlb_f, dlb_b], axis=0))
    g["hg_lb"] = jnp.stack([d_a0, d_a1], axis=1)
    dqr, dkr, dv = _attn_bwd(qr, kr, p, dcat, att32, lse)
    dp_att, dgq, dgk = _qk_prep_bwd(p, dqr, dkr, dv, gq2, gk2, tables, bd)
    g["q_norm_g"], g["k_norm_g"] = dgq, dgk
    dp = [dp_att, dhq, dz_f, dz_b, dhi, dhg]

    early = [n for n in _SMALL if n != "pre_mix_g"]
    vals = [g[n].reshape(-1, g[n].shape[-1]) for n in early] + [jnp.pad(loss, ((0, 0), (0, LANE - 1)))]
    small = dict(early=early, shapes=[val.shape for val in vals])
    small["fetch"], started = _gather_begin([_pack_small(vals, "pack_small")], "small")
    pending["in"], started = _exchange_begin([_slots(_dw(dp, h1, "dw_in", after=[started]))], "in")
    dx, g["pre_mix_g"] = _mm_prenorm_bwd(dp, w_in_t, True, x, w["pre_mix_g"], dx1, "in_bwd", after=[started])
    small["fetch_last"], _ = _gather_begin([_pack_small([g["pre_mix_g"]], "pack_last")], "last")
    return dx, g, pending, small


_COL_SHARDED = ("w_in", "w_xkv", "w_up")
_ROW_SHARDED = ("w_out", "w_xq", "w_xo", "w_down")
_REPLICATED = ("pre_mix_g", "q_norm_g", "k_norm_g", "hg_out_norm_g", "post_mix_g", "pre_x_g", "mem_norm_g", "post_x_g",
               "pre_ffn_g", "conv_b", "post_ffn_g")
_WEIGHTS = ("pre_mix_g", "w_in", "q_norm_g", "k_norm_g", "hg_lb", "hg_out_norm_g", "w_out", "post_mix_g", "pre_x_g",
            "mem_norm_g", "w_xq", "w_xkv", "w_xo", "post_x_g", "pre_ffn_g", "w_up", "conv_w", "conv_b", "w_down",
            "post_ffn_g")
_SMALL = _REPLICATED + ("hg_lb", "conv_w")
_ADAM_TRANSPOSED = ("w_in", "w_up")
PACK_W = 1024


def _small_plan(shapes):
    plan, r = [], 0
    for vi, (rows, cols) in enumerate(shapes):
        for i in range(rows):
            for c0 in range(0, cols, PACK_W):
                plan.append((vi, i, c0, min(PACK_W, cols - c0), r))
                r += 1
    return plan, -(-r // 8) * 8


def _pack_small(vals, name):
    plan, nrows = _small_plan([val.shape for val in vals])

    def body(*refs):
        ins, out = refs[:-1], refs[-1]
        out[...] = jnp.zeros_like(out)
        for vi, i, c0, width, r in plan:
            out[r:r + 1, 0:width] = ins[vi][i:i + 1, c0:c0 + width]

    return pl.pallas_call(body, name=name, out_shape=jax.ShapeDtypeStruct((nrows, PACK_W), F32))(*vals)


def _sum_unpack_small(packs, shapes, name):
    plan, _ = _small_plan(shapes)

    def body(*refs):
        p_ref, outs = refs[0], refs[1:]
        acc = p_ref[0]
        for i in range(1, N_DEV):
            acc = acc + p_ref[i]
        for vi, i, c0, width, r in plan:
            outs[vi][i:i + 1, c0:c0 + width] = acc[r:r + 1, 0:width]

    return pl.pallas_call(body, name=name, out_shape=[jax.ShapeDtypeStruct(s, F32) for s in shapes])(packs)


def _adamw_many(ws, gs, ms, vs):
    n = len(ws)

    def body(*refs):
        w_refs, g_refs, m_refs, v_refs = (refs[k * n:(k + 1) * n] for k in range(4))
        d_refs, nm_refs, nv_refs = (refs[(4 + k) * n:(5 + k) * n] for k in range(3))
        for k in range(n):
            g_ = g_refs[k][...]
            m_ = ADAM_B1 * m_refs[k][...] + (1.0 - ADAM_B1) * g_
            v_ = ADAM_B2 * v_refs[k][...] + (1.0 - ADAM_B2) * (g_ * g_)
            m_hat = m_ / (1.0 - ADAM_B1 ** ADAM_STEP)
            v_hat = v_ / (1.0 - ADAM_B2 ** ADAM_STEP)
            d_refs[k][...] = -ADAM_LR * (m_hat / (jnp.sqrt(v_hat) + ADAM_EPS) + ADAM_WD * w_refs[k][...])
            nm_refs[k][...] = m_
            nv_refs[k][...] = v_

    shapes = [jax.ShapeDtypeStruct(a.shape, F32) for a in ws]
    outs = pl.pallas_call(body, name="adamw_small", out_shape=shapes * 3)(*ws, *gs, *ms, *vs)
    return outs[:n], outs[n:2 * n], outs[2 * n:]


def kernel(x, mem, pre_mix_g, w_in, q_norm_g, k_norm_g, hg_lb, hg_out_norm_g, w_out, post_mix_g, pre_x_g, mem_norm_g, w_xq, w_xkv, w_xo, post_x_g, pre_ffn_g, w_up, conv_w, conv_b, w_down, post_ffn_g, loss_target, m_pre_mix_g, m_w_in, m_q_norm_g, m_k_norm_g, m_hg_lb, m_hg_out_norm_g, m_w_out, m_post_mix_g, m_pre_x_g, m_mem_norm_g, m_w_xq, m_w_xkv, m_w_xo, m_post_x_g, m_pre_ffn_g, m_w_up, m_conv_w, m_conv_b, m_w_down, m_post_ffn_g, v_pre_mix_g, v_w_in, v_q_norm_g, v_k_norm_g, v_hg_lb, v_hg_out_norm_g, v_w_out, v_post_mix_g, v_pre_x_g, v_mem_norm_g, v_w_xq, v_w_xkv, v_w_xo, v_post_x_g, v_pre_ffn_g, v_w_up, v_conv_w, v_conv_b, v_w_down, v_post_ffn_g):
    args = dict(locals())
    w = {n: args[n] for n in _WEIGHTS}
    m = {n: args["m_" + n] for n in _WEIGHTS}
    v = {n: args["v_" + n] for n in _WEIGHTS}
    me = _dev_index(*_mesh_pos())

    wire = {n: _bf(w[n][0].T) for n in _COL_SHARDED}
    wire.update({n: _bf(w[n][0]) for n in _ROW_SHARDED})

    grad_x, g, pending, small = _local_step(x[0], mem[0], loss_target[0], w, wire)

    grads, delta, new_m, new_v = {}, {}, {}, {}

    me_arr = jnp.reshape(me, (1,)).astype(jnp.int32)

    def update(n, parts):
        if n in _ADAM_TRANSPOSED:
            outs = _sum_adamw(*parts, me_arr, w[n][0].T, m[n][0].T, v[n][0].T, "update_" + n)
            grads[n], delta[n], new_m[n], new_v[n] = (a.T[None] for a in outs)
        elif n in _COL_SHARDED:
            gsum = _sum_parts(*parts, me_arr, "sum_" + n).T
            grads[n] = gsum[None]
            delta[n], new_m[n], new_v[n] = (a[None] for a in _adamw(w[n][0], gsum, m[n][0], v[n][0], "adamw_" + n))
        else:
            outs = _sum_adamw(*parts, me_arr, w[n][0], m[n][0], v[n][0], "update_" + n)
            grads[n], delta[n], new_m[n], new_v[n] = (a[None] for a in outs)

    after = grad_x
    for tag, names in (("down", ["w_down"]), ("up", ["w_up"]), ("x", ["w_xo", "w_xq", "w_xkv", "w_out"]),
                       ("in", ["w_in"])):
        for n, parts in zip(names, _exchange_end(pending[tag], after)):
            update(n, parts)
            after = new_v[n]

    (packs,) = _gather_end(small["fetch"], [0], after)
    summed = _sum_unpack_small(packs, small["shapes"], "sum_unpack_small")
    loss = summed[-1][0, 0]
    for n, s in zip(small["early"], summed[:-1]):
        grads[n] = s
    (packs,) = _gather_end(small["fetch_last"], [0], after)
    (grads["pre_mix_g"],) = _sum_unpack_small(packs, [w["pre_mix_g"].shape], "sum_unpack_last")
    small = _SMALL
    fold = lambda v2: v2[:, :ATT_HEAD_DIM] + v2[:, ATT_HEAD_DIM:]
    grads["q_norm_g"], grads["k_norm_g"] = fold(grads["q_norm_g"]), fold(grads["k_norm_g"])
    grads["hg_lb"] = lax.dynamic_slice_in_dim(grads["hg_lb"].reshape(2, 2, HG_DIM), me * (HG_DIM // N_DEV),
                                              HG_DIM // N_DEV, axis=2)
    grads["conv_w"] = lax.dynamic_slice_in_dim(grads["conv_w"], me * (2 * D_FF // N_DEV), 2 * D_FF // N_DEV, axis=1)[None]

    flat2 = lambda a: a.reshape(-1, a.shape[-1])
    outs = _adamw_many(*[[flat2(d[n]) for n in small] for d in (w, grads, m, v)])
    for dst, vals in zip((delta, new_m, new_v), outs):
        for n, val in zip(small, vals):
            dst[n] = val.reshape(w[n].shape)

    return (loss, grad_x[None], *[grads[n] for n in _WEIGHTS], *[delta[n] for n in _WEIGHTS],
            *[new_m[n] for n in _WEIGHTS], *[new_v[n] for n in _WEIGHTS])
```

```python
import jax
import jax.numpy as jnp
from jax import lax
from jax.experimental import pallas as pl
from jax.experimental.pallas import tpu as pltpu

F32 = jnp.float32
BF16 = jnp.bfloat16

D_MODEL = 1024
GRID_W = 64
EPS = 1e-6
ATT_HEADS = 8
ATT_HEAD_DIM = 64
ATT_Q_DIM = 512
ATT_KV_DIM = 128
ROPE_THETA = 10000.0
HG_HEADS = 4
HG_DIM = 512
HG_CHUNK = 32
HG_CHUNK_LOG2 = 5
HG_BLOCK_FWD = 256
HG_BLOCK_BWD = 128
N_IN = 3328
X_HEADS = 4
X_HEAD_DIM = 256
D_FF = 2816
N_DEV = 8
LANE = 128
ADAM_LR = 0.001
ADAM_B1 = 0.9
ADAM_B2 = 0.999
ADAM_EPS = 1e-08
ADAM_WD = 0.01
ADAM_STEP = 10
VMEM_LIMIT = 56 * 1024 * 1024

MESH_T = pl.DeviceIdType.MESH


def _params(**kw):
    return pltpu.CompilerParams(vmem_limit_bytes=VMEM_LIMIT, **kw)


def _dot(a, b, ca, cb):
    return lax.dot_general(a, b, (((ca,), (cb,)), ((), ())), preferred_element_type=F32)


def _bf(x):
    return x.astype(BF16)


def _sigmoid(x):
    return 1.0 / (1.0 + jnp.exp(-x))


def _rms_fwd(x, g):
    r = lax.rsqrt(jnp.mean(x * x, axis=-1, keepdims=True) + EPS)
    return x * r * g


def _rms_bwd(dy, x, g):
    r = lax.rsqrt(jnp.mean(x * x, axis=-1, keepdims=True) + EPS)
    xh = x * r
    dg = jnp.sum(dy * xh, axis=0, keepdims=True)
    t = dy * g
    dx = r * (t - xh * jnp.mean(t * xh, axis=-1, keepdims=True))
    return dx, dg


def _full(shape):
    nd = len(shape)
    return pl.BlockSpec(shape, lambda *a: (0,) * nd)


def _norm_mm(x, g, w, trans, tn, name, after=(), tm=512, p_dtype=F32):
    t, d = x.shape
    n = w.shape[0] if trans else w.shape[1]
    tm = min(tm, t)

    def body(x_ref, g_ref, w_ref, h_ref, p_ref):
        h = _bf(_rms_fwd(x_ref[...], g_ref[...]))
        h_ref[...] = h
        p_ref[...] = _dot(h, w_ref[...], 1, 1 if trans else 0).astype(p_dtype)

    w_spec = pl.BlockSpec((tn, d), lambda i, j: (j, 0)) if trans else pl.BlockSpec((d, tn), lambda i, j: (0, j))
    return _call(
        body, (x, g, w), name=name, grid=(t // tm, n // tn),
        in_specs=[pl.BlockSpec((tm, d), lambda i, j: (i, 0)), _full((1, d)), w_spec],
        out_specs=[pl.BlockSpec((tm, d), lambda i, j: (i, 0)), pl.BlockSpec((tm, tn), lambda i, j: (i, j))],
        out_shape=[jax.ShapeDtypeStruct((t, d), BF16), jax.ShapeDtypeStruct((t, n), p_dtype)], after=after)


def _mm_postnorm_res(a, w, g, res, name, tm=256):
    t, k = a.shape
    d = w.shape[1]

    def body(a_ref, w_ref, g_ref, res_ref, y_ref, o_ref):
        y = _dot(a_ref[...], w_ref[...], 1, 0)
        y_ref[...] = _bf(y)
        o_ref[...] = res_ref[...] + _rms_fwd(y, g_ref[...])

    row = lambda width: pl.BlockSpec((tm, width), lambda i: (i, 0))
    return pl.pallas_call(
        body, name=name, grid=(t // tm,),
        in_specs=[row(k), _full((k, d)), _full((1, d)), row(d)],
        out_specs=[row(d), row(d)],
        out_shape=[jax.ShapeDtypeStruct((t, d), BF16), jax.ShapeDtypeStruct((t, d), F32)],
        compiler_params=_params(),
    )(a, w, g, res)


def _mm_postnorm_res_loss(a, w, g, res, tgt, name, tm=256):
    t, k = a.shape
    d = w.shape[1]

    def body(a_ref, w_ref, g_ref, res_ref, tgt_ref, y_ref, dout_ref, loss_ref):
        @pl.when(pl.program_id(0) == 0)
        def _():
            loss_ref[...] = jnp.zeros_like(loss_ref)

        y = _dot(a_ref[...], w_ref[...], 1, 0)
        y_ref[...] = _bf(y)
        diff = res_ref[...] + _rms_fwd(y, g_ref[...]) - tgt_ref[...]
        dout_ref[...] = diff * (1.0 / d)
        part = jnp.sum(jnp.sum(diff * diff, axis=-1, keepdims=True), axis=0, keepdims=True)
        loss_ref[...] += (0.5 / d) * part

    row = lambda width: pl.BlockSpec((tm, width), lambda i: (i, 0))
    return pl.pallas_call(
        body, name=name, grid=(t // tm,),
        in_specs=[row(k), _full((k, d)), _full((1, d)), row(d), row(d)],
        out_specs=[row(d), row(d), _full((1, 1))],
        out_shape=[jax.ShapeDtypeStruct((t, d), BF16), jax.ShapeDtypeStruct((t, d), F32), jax.ShapeDtypeStruct((1, 1), F32)],
        compiler_params=_params(),
    )(a, w, g, res, tgt)


def _postnorm_bwd_mm(dout, y, g, w, name, da_dtype=F32, tm=256):
    t, d = y.shape
    k = w.shape[0]

    def body(dout_ref, y_ref, g_ref, w_ref, dy_ref, da_ref, dg_ref):
        @pl.when(pl.program_id(0) == 0)
        def _():
            dg_ref[...] = jnp.zeros_like(dg_ref)

        dy, dg = _rms_bwd(dout_ref[...], y_ref[...].astype(F32), g_ref[...])
        dg_ref[...] += dg
        dyb = _bf(dy)
        dy_ref[...] = dyb
        da_ref[...] = _dot(dyb, w_ref[...], 1, 1).astype(da_dtype)

    row = lambda width: pl.BlockSpec((tm, width), lambda i: (i, 0))
    return pl.pallas_call(
        body, name=name, grid=(t // tm,),
        in_specs=[row(d), row(d), _full((1, d)), _full((k, d))],
        out_specs=[row(d), row(k), _full((1, d))],
        out_shape=[jax.ShapeDtypeStruct((t, d), BF16), jax.ShapeDtypeStruct((t, k), da_dtype), jax.ShapeDtypeStruct((1, d), F32)],
        compiler_params=_params(),
    )(dout, y, g, w)


def _mm_prenorm_bwd(dp, w, trans, x, g, dres, name, after=(), tm=512):
    dps = list(dp) if isinstance(dp, (list, tuple)) else [dp]
    nparts = len(dps)
    t = dps[0].shape[0]
    widths = [part.shape[1] for part in dps]
    d = x.shape[1]
    tm = min(tm, t)

    def body(*refs):
        dp_refs = refs[:nparts]
        w_ref, x_ref, g_ref, dres_ref, dx_ref, dg_ref = refs[nparts:]

        @pl.when(pl.program_id(0) == 0)
        def _():
            dg_ref[...] = jnp.zeros_like(dg_ref)

        dh = None
        for k, dp_ref in enumerate(dp_refs):
            cols = slice(sum(widths[:k]), sum(widths[:k + 1]))
            term = _dot(dp_ref[...], w_ref[cols, :], 1, 0) if trans else _dot(dp_ref[...], w_ref[:, cols], 1, 1)
            dh = term if dh is None else dh + term
        dx, dg = _rms_bwd(dh, x_ref[...], g_ref[...])
        dg_ref[...] += dg
        dx_ref[...] = dres_ref[...] + dx

    row = lambda width: pl.BlockSpec((tm, width), lambda i: (i, 0))
    return _call(
        body, (*dps, w, x, g, dres), name=name, grid=(t // tm,),
        in_specs=[row(width) for width in widths] + [_full(w.shape), row(d), _full((1, d)), row(d)],
        out_specs=[row(d), _full((1, d))],
        out_shape=[jax.ShapeDtypeStruct((t, d), F32), jax.ShapeDtypeStruct((1, d), F32)], after=after)


def _dw(a, b, name, tka=256, after=()):
    parts = list(a) if isinstance(a, (list, tuple)) else [a]
    nparts = len(parts)
    t, nb = b.shape
    tiles = [part.shape[1] // tka for part in parts]
    first = [sum(tiles[:k]) for k in range(nparts)]

    def body(*refs):
        a_refs, b_ref, o_ref = refs[:nparts], refs[nparts], refs[-1]
        i = pl.program_id(0)
        for k, a_ref in enumerate(a_refs):
            @pl.when((i >= first[k]) & (i < first[k] + tiles[k]))
            def _():
                o_ref[...] = _bf(_dot(a_ref[...], b_ref[...], 0, 0))

    a_spec = lambda k: pl.BlockSpec((t, tka), lambda i: (0, jnp.clip(i - first[k], 0, tiles[k] - 1)))
    return pl.pallas_call(
        body, name=name, grid=(sum(tiles),),
        in_specs=[a_spec(k) for k in range(nparts)] + [_full((t, nb))] + [pl.BlockSpec(memory_space=pl.ANY)] * len(after),
        out_specs=pl.BlockSpec((tka, nb), lambda i: (i, 0)),
        out_shape=jax.ShapeDtypeStruct((sum(tiles) * tka, nb), BF16),
        compiler_params=_params(),
    )(*parts, b, *after)


def _rope_tables(t):
    pos = jnp.arange(t)
    r = (pos // GRID_W).astype(F32)
    c = (pos % GRID_W).astype(F32)
    npair = ATT_HEAD_DIM // 4
    inv = jnp.power(ROPE_THETA, -jnp.arange(npair, dtype=F32) / npair)
    ang = jnp.concatenate([r[:, None] * inv, c[:, None] * inv], axis=-1)
    cos = jnp.repeat(jnp.cos(ang), 2, axis=-1)
    sin = jnp.repeat(jnp.sin(ang), 2, axis=-1)
    even = (jnp.arange(ATT_HEAD_DIM) % 2) == 0
    sa = jnp.where(even, -sin, 0.0)
    sb = jnp.where(even, 0.0, sin)
    two = lambda a: jnp.tile(a, (1, 2))
    return two(cos), two(sa), two(sb)


def _head_sum_matrix():
    a = jnp.arange(LANE) // ATT_HEAD_DIM
    return (a[:, None] == a[None, :]).astype(BF16)


def _head_mean(v, bd):
    hi = _bf(v)
    lo = _bf(v - hi.astype(F32))
    return (_dot(hi, bd, 1, 0) + _dot(lo, bd, 1, 0)) * (1.0 / ATT_HEAD_DIM)


def _qk_prep(p, gq, gk, tables, bd, tm=512):
    t = p.shape[0]
    tm = min(tm, t)
    cc, sa, sb = tables

    def body(p_ref, gq_ref, gk_ref, cc_ref, sa_ref, sb_ref, bd_ref, q_ref, k_ref):
        cc_, sa_, sb_, bd_ = cc_ref[...], sa_ref[...], sb_ref[...], bd_ref[...]
        low = lax.broadcasted_iota(jnp.int32, (tm, LANE), 1) < ATT_HEAD_DIM

        def normrope(xs, g):
            xn = xs * lax.rsqrt(_head_mean(xs * xs, bd_) + EPS) * g
            return xn * cc_ + pltpu.roll(xn, LANE - 1, 1) * sa_ + pltpu.roll(xn, 1, 1) * sb_

        for j in range(4):
            y = normrope(p_ref[:, j * LANE:(j + 1) * LANE].astype(F32), gq_ref[...]) * (ATT_HEAD_DIM ** -0.5)
            yr = pltpu.roll(y, ATT_HEAD_DIM, 1)
            if j // 2 == 0:
                h0, h1 = jnp.where(low, y, 0.0), jnp.where(low, yr, 0.0)
            else:
                h0, h1 = jnp.where(low, 0.0, yr), jnp.where(low, 0.0, y)
            q_ref[:, (2 * j) * LANE:(2 * j + 1) * LANE] = _bf(h0)
            q_ref[:, (2 * j + 1) * LANE:(2 * j + 2) * LANE] = _bf(h1)
        k_ref[...] = _bf(normrope(p_ref[:, ATT_Q_DIM:ATT_Q_DIM + LANE].astype(F32), gk_ref[...]))

    row = lambda width: pl.BlockSpec((tm, width), lambda i: (i, 0))
    return pl.pallas_call(
        body, name="qk_prep", grid=(t // tm,),
        in_specs=[row(ATT_Q_DIM + LANE), _full((1, LANE)), _full((1, LANE)), row(LANE), row(LANE), row(LANE),
                  _full((LANE, LANE))],
        out_specs=[row(ATT_HEADS * LANE), row(LANE)],
        out_shape=[jax.ShapeDtypeStruct((t, ATT_HEADS * LANE), BF16), jax.ShapeDtypeStruct((t, LANE), BF16)],
        compiler_params=_params(),
    )(p, gq, gk, cc, sa, sb, bd)


def _qk_prep_bwd(p, dq, dk, dv, gq, gk, tables, bd, tm=512):
    t = p.shape[0]
    tm = min(tm, t)
    cc, sa, sb = tables
    att_w = ATT_Q_DIM + 2 * ATT_KV_DIM

    def body(p_ref, dq_ref, dk_ref, dv_ref, gq_ref, gk_ref, cc_ref, sa_ref, sb_ref, bd_ref, dp_ref, dgq_ref, dgk_ref):
        dp_ref[:, ATT_Q_DIM + LANE:att_w] = _bf(dv_ref[...])

        @pl.when(pl.program_id(0) == 0)
        def _():
            dgq_ref[...] = jnp.zeros_like(dgq_ref)
            dgk_ref[...] = jnp.zeros_like(dgk_ref)

        cc_, sa_, sb_, bd_ = cc_ref[...], sa_ref[...], sb_ref[...], bd_ref[...]
        low = lax.broadcasted_iota(jnp.int32, (tm, LANE), 1) < ATT_HEAD_DIM

        def bwd(xs, g, dy):
            r = lax.rsqrt(_head_mean(xs * xs, bd_) + EPS)
            xh = xs * r
            dxn = dy * cc_ + pltpu.roll(dy * sa_, 1, 1) + pltpu.roll(dy * sb_, LANE - 1, 1)
            dg = jnp.sum(dxn * xh, axis=0, keepdims=True)
            tt = dxn * g
            return r * (tt - xh * _head_mean(tt * xh, bd_)), dg

        dgq = jnp.zeros((1, LANE), F32)
        for j in range(4):
            d0 = dq_ref[:, (2 * j) * LANE:(2 * j + 1) * LANE]
            d1 = dq_ref[:, (2 * j + 1) * LANE:(2 * j + 2) * LANE]
            if j // 2 == 0:
                dy = jnp.where(low, d0, pltpu.roll(d1, ATT_HEAD_DIM, 1))
            else:
                dy = jnp.where(low, pltpu.roll(d0, ATT_HEAD_DIM, 1), d1)
            dx, dg = bwd(p_ref[:, j * LANE:(j + 1) * LANE].astype(F32), gq_ref[...], dy * (ATT_HEAD_DIM ** -0.5))
            dp_ref[:, j * LANE:(j + 1) * LANE] = _bf(dx)
            dgq = dgq + dg
        dgq_ref[...] += dgq
        dx, dg = bwd(p_ref[:, ATT_Q_DIM:ATT_Q_DIM + LANE].astype(F32), gk_ref[...], dk_ref[...])
        dp_ref[:, ATT_Q_DIM:ATT_Q_DIM + LANE] = _bf(dx)
        dgk_ref[...] += dg

    row = lambda width: pl.BlockSpec((tm, width), lambda i: (i, 0))
    return pl.pallas_call(
        body, name="qk_prep_bwd", grid=(t // tm,),
        in_specs=[row(ATT_Q_DIM + LANE), row(ATT_HEADS * LANE), row(LANE), row(LANE), _full((1, LANE)), _full((1, LANE)),
                  row(LANE), row(LANE), row(LANE), _full((LANE, LANE))],
        out_specs=[row(att_w), _full((1, LANE)), _full((1, LANE))],
        out_shape=[jax.ShapeDtypeStruct((t, att_w), BF16), jax.ShapeDtypeStruct((1, LANE), F32),
                   jax.ShapeDtypeStruct((1, LANE), F32)],
        compiler_params=_params(),
    )(p, dq, dk, dv, gq, gk, cc, sa, sb, bd)


def _attn_fwd(q, k, p, tq=256):
    t = k.shape[0]
    tq = min(tq, t)
    v_blk = (ATT_Q_DIM + ATT_KV_DIM) // LANE

    def body(q_ref, k_ref, v_ref, o_ref, o32_ref, lse_ref):
        k_ = k_ref[...]
        v = v_ref[...].astype(F32)
        lane_k =lax.broadcasted_iota(jnp.int32, (t, LANE), 1)
        lane_q = lax.broadcasted_iota(jnp.int32, (tq, LANE), 1)
        lowk, lowq = lane_k < ATT_HEAD_DIM, lane_q < ATT_HEAD_DIM
        ones_lane = (ATT_HEAD_DIM, 0)
        vm = (_bf(jnp.where(lowk, v, jnp.where(lane_k == ones_lane[0], 1.0, 0.0))),
              _bf(jnp.where(lowk, jnp.where(lane_k == ones_lane[1], 1.0, 0.0), v)))
        for j in range(4):
            kvh = j // 2
            acc = None
            for sub in range(2):
                h = 2 * j + sub
                s = _dot(q_ref[:, h * LANE:(h + 1) * LANE], k_, 1, 1)
                mx = jnp.max(s, axis=-1, keepdims=True)
                ov = _dot(jnp.exp(_bf(s - mx)), vm[kvh], 1, 0)
                l = jnp.sum(jnp.where(lane_q == ones_lane[kvh], ov, 0.0), axis=-1, keepdims=True)
                lse_ref[h] = mx + jnp.log(l)
                o = jnp.where(lowq if kvh == 0 else ~lowq, ov, 0.0) * (1.0 / l)
                if sub != kvh:
                    o = pltpu.roll(o, ATT_HEAD_DIM, 1)
                acc = o if acc is None else acc + o
            o32_ref[:, j * LANE:(j + 1) * LANE] = acc
            o_ref[:, j * LANE:(j + 1) * LANE] = _bf(acc)

    row = pl.BlockSpec((tq, ATT_Q_DIM), lambda i: (i, 0))
    return _call(
        body, (q, k, p), name="attn_fwd", grid=(t // tq,),
        in_specs=[pl.BlockSpec((tq, ATT_HEADS * LANE), lambda i: (i, 0)), _full((t, LANE)),
                  pl.BlockSpec((t, LANE), lambda i: (0, v_blk))],
        out_specs=[row, row, pl.BlockSpec((ATT_HEADS, tq, 1), lambda i: (0, i, 0))],
        out_shape=[jax.ShapeDtypeStruct((t, ATT_Q_DIM), BF16), jax.ShapeDtypeStruct((t, ATT_Q_DIM), F32),
                   jax.ShapeDtypeStruct((ATT_HEADS, t, 1), F32)])


def _attn_bwd(q, k, p, dcat, o32, lse, tq=256):
    t = k.shape[0]
    tq = min(tq, t)
    v_blk = (ATT_Q_DIM + ATT_KV_DIM) // LANE

    def body(q_ref, k_ref, v_ref, do_ref, o_ref, lse_ref, dq_ref, dk_ref, dv_ref):
        @pl.when(pl.program_id(0) == 0)
        def _():
            dk_ref[...] = jnp.zeros_like(dk_ref)
            dv_ref[...] = jnp.zeros_like(dv_ref)

        k_ = k_ref[...]
        vb = _bf(v_ref[...])
        lowq = lax.broadcasted_iota(jnp.int32, (tq, LANE), 1) < ATT_HEAD_DIM
        dk_acc = jnp.zeros((t, LANE), F32)
        dv_acc = jnp.zeros((t, LANE), F32)
        for j in range(4):
            kvh = j // 2
            dop = do_ref[:, j * LANE:(j + 1) * LANE].astype(F32)
            prod = dop * o_ref[:, j * LANE:(j + 1) * LANE]
            d_low = jnp.sum(jnp.where(lowq, prod, 0.0), axis=-1, keepdims=True)
            d_sub = (d_low, jnp.sum(prod, axis=-1, keepdims=True) - d_low)
            for sub in range(2):
                h = 2 * j + sub
                src = dop if sub == kvh else pltpu.roll(dop, ATT_HEAD_DIM, 1)
                do_h = _bf(jnp.where(lowq, src, 0.0) if kvh == 0 else jnp.where(lowq, 0.0, src))
                qh = q_ref[:, h * LANE:(h + 1) * LANE]
                pr = jnp.exp(_bf(_dot(qh, k_, 1, 1) - lse_ref[h]))
                ds = pr * _bf(_dot(do_h, vb, 1, 1) - d_sub[sub])
                dq_ref[:, h * LANE:(h + 1) * LANE] = _dot(ds, k_, 1, 0)
                dk_acc = dk_acc + _dot(ds, qh, 0, 0)
                dv_acc = dv_acc + _dot(pr, do_h, 0, 0)
        dk_ref[...] += dk_acc
        dv_ref[...] += dv_acc

    row = pl.BlockSpec((tq, ATT_Q_DIM), lambda i: (i, 0))
    return _call(
        body, (q, k, p, dcat, o32, lse), name="attn_bwd", grid=(t // tq,),
        in_specs=[pl.BlockSpec((tq, ATT_HEADS * LANE), lambda i: (i, 0)), _full((t, LANE)),
                  pl.BlockSpec((t, LANE), lambda i: (0, v_blk)), row, row,
                  pl.BlockSpec((ATT_HEADS, tq, 1), lambda i: (0, i, 0))],
        out_specs=[pl.BlockSpec((tq, ATT_HEADS * LANE), lambda i: (i, 0)), _full((t, LANE)), _full((t, LANE))],
        out_shape=[jax.ShapeDtypeStruct((t, ATT_HEADS * LANE), F32), jax.ShapeDtypeStruct((t, LANE), F32),
                   jax.ShapeDtypeStruct((t, LANE), F32)])


def _lower_bounds(a0, a1):
    def body(a0_ref, a1_ref, lb_ref):
        m = jnp.maximum(a0_ref[...], a1_ref[...])
        e0, e1 = jnp.exp(a0_ref[...] - m), jnp.exp(a1_ref[...] - m)
        lb_ref[...] = e0 / (e0 + e1)

    return pl.pallas_call(body, name="lower_bounds", out_shape=jax.ShapeDtypeStruct(a0.shape, F32))(a0, a1)


def _lower_bounds_bwd(a0, a1, dlb):
    def body(a0_ref, a1_ref, dlb_ref, d0_ref, d1_ref):
        m = jnp.maximum(a0_ref[...], a1_ref[...])
        e0, e1 = jnp.exp(a0_ref[...] - m), jnp.exp(a1_ref[...] - m)
        lb = e0 / (e0 + e1)
        d0 = dlb_ref[...] * lb * (1.0 - lb)
        d0_ref[...] = d0
        d1_ref[...] = -d0

    return pl.pallas_call(body, name="lower_bounds_bwd", out_shape=[jax.ShapeDtypeStruct(a0.shape, F32)] * 2)(a0, a1, dlb)


def _chunk_scan(x, pos, down):
    n = x.shape[0]
    s = 1
    while s < HG_CHUNK:
        if down:
            x = x + jnp.where(pos >= s, pltpu.roll(x, s, 0), 0.0)
        else:
            x = x + jnp.where(pos < HG_CHUNK - s, pltpu.roll(x, n - s, 0), 0.0)
        s *= 2
    return x


def _chunk_concat(x, nc):
    xb = _bf(x)
    rows = []
    for c in range(nc):
        pieces = [xb[c * HG_CHUNK:(c + 1) * HG_CHUNK, :]]
        if c:
            pieces.insert(0, jnp.zeros((HG_CHUNK, c * LANE), BF16))
        if c < nc - 1:
            pieces.append(jnp.zeros((HG_CHUNK, (nc - 1 - c) * LANE), BF16))
        rows.append(jnp.concatenate(pieces, axis=1) if len(pieces) > 1 else pieces[0])
    return jnp.concatenate(rows, axis=0) if nc > 1 else rows[0]


def _chunk_pick(x, nc):
    rows = [x[c * HG_CHUNK:(c + 1) * HG_CHUNK, c * LANE:(c + 1) * LANE] for c in range(nc)]
    return jnp.concatenate(rows, axis=0) if nc > 1 else rows[0]


def _chunk_rows(vals):
    rows = [jnp.broadcast_to(v, (HG_CHUNK, LANE)) for v in vals]
    return jnp.concatenate(rows, axis=0) if len(rows) > 1 else rows[0]


def _hgrn_gates(hq, z, lb):
    sq = _sigmoid(hq)
    sig = _sigmoid(z)
    f = lb + (1.0 - lb) * sig
    return hq * sq, sq, sig, f, jnp.log(f)


def _hgrn_local(q, f, g, v, pos, amask, rev, nc):
    k = 1.0 - f
    ba = _chunk_scan(g, pos, not rev)
    bb = _chunk_scan(g, pos, rev)
    eq = 0.5 * (ba - bb + g)
    ex_q, ex_k, ex_i, ex_d = jnp.exp(eq), jnp.exp(-eq), jnp.exp(ba), jnp.exp(bb - g)
    qb, kb, qi, kd = q * ex_q, k * ex_k, q * ex_i, k * ex_d
    dvec = jnp.exp(ba + bb - g)
    a = jnp.where(amask, _dot(_bf(qb), _bf(kb), 1, 1), 0.0)
    kd_m = _chunk_concat(kd, nc)
    qi_m = _chunk_concat(qi, nc)
    ut_all = _dot(_bf(v), kd_m, 0, 0)
    return dict(k=k, ex_q=ex_q, ex_k=ex_k, ex_i=ex_i, ex_d=ex_d, qb=qb, kb=kb, qi=qi, kd=kd, dvec=dvec, a=a,
                kd_m=kd_m, qi_m=qi_m, ut_all=ut_all)


def _hgrn_masks(n, rev):
    row = lax.broadcasted_iota(jnp.int32, (n, LANE), 0)
    ti = lax.broadcasted_iota(jnp.int32, (n, n), 0)
    si = lax.broadcasted_iota(jnp.int32, (n, n), 1)
    tri = (si >= ti) if rev else (si <= ti)
    same = jnp.right_shift(ti, HG_CHUNK_LOG2) == jnp.right_shift(si, HG_CHUNK_LOG2)
    return jnp.bitwise_and(row, HG_CHUNK - 1), same & tri


def _hgrn_specs(t, rev, bwd):
    n = min(HG_BLOCK_BWD if bwd else HG_BLOCK_FWD, t)
    nb = t // n
    if rev != bwd:
        blk = lambda i: nb - 1 - i
    else:
        blk = lambda i: i
    col = lambda base: pl.BlockSpec((n, 2 * LANE), lambda hp, i: (blk(i), base + hp))
    return n, nb, blk, col


def _hgrn_fwd(p, lb, rev, after=()):
    t = p.shape[0]
    n, nb, blk, col = _hgrn_specs(t, rev, False)
    nc = n // HG_CHUNK
    sub = n // min(HG_BLOCK_BWD, t)
    z_base = 7 if rev else 5

    def body(hq_ref, z_ref, hi_ref, lb_ref, o_ref, ssave_ref, st_scr):
        @pl.when(pl.program_id(1) == 0)
        def _():
            st_scr[...] = jnp.zeros_like(st_scr)

        pos, amask = _hgrn_masks(n, rev)
        order = list(range(nc))[::-1] if rev else list(range(nc))
        for hh in range(2):
            sl = slice(hh * LANE, (hh + 1) * LANE)
            q, _, _, f, g = _hgrn_gates(hq_ref[:, sl].astype(F32), z_ref[:, sl].astype(F32), lb_ref[0:1, sl])
            v = hi_ref[:, sl].astype(F32)
            c_ = _hgrn_local(q, f, g, v, pos, amask, rev, nc)
            st = st_scr[hh]
            cols = [None] * nc
            for c in order:
                cols[c] = st
                st = st * c_["dvec"][c * HG_CHUNK:c * HG_CHUNK + 1, :] + c_["ut_all"][:, c * LANE:(c + 1) * LANE]
            st_scr[hh] = st
            for s in range(sub):
                ssave_ref[sub - 1 - s if rev else s, sl, :] = cols[order[s * (nc // sub)]]
            st_all = _bf(jnp.concatenate(cols, axis=1))
            o_ref[:, sl] = _dot(_bf(c_["a"]), _bf(v), 1, 0) + _dot(c_["qi_m"], st_all, 1, 1)

    return _call(
        body, (p, p, p, lb), name="hgrn_fwd_rev" if rev else "hgrn_fwd", grid=(2, nb),
        in_specs=[col(3), col(z_base), col(9), pl.BlockSpec((1, 2 * LANE), lambda hp, i: (0, hp))],
        out_specs=[pl.BlockSpec((n, 2 * LANE), lambda hp, i: (blk(i), hp)),
                   pl.BlockSpec((sub, 2 * LANE, LANE), lambda hp, i: (blk(i), hp, 0))],
        out_shape=[jax.ShapeDtypeStruct((t, HG_DIM), F32), jax.ShapeDtypeStruct((nb * sub, HG_DIM, LANE), F32)],
        scratch_shapes=[pltpu.VMEM((2, LANE, LANE), F32)], after=after)


def _hgrn_bwd(p, lb, do, ssave, rev, other=None, after=()):
    t = p.shape[0]
    n, nb, blk, col = _hgrn_specs(t, rev, True)
    nc = n // HG_CHUNK
    z_base = 7 if rev else 5
    n_other = 0 if other is None else 2

    def body(*refs):
        hq_ref, z_ref, hi_ref, lb_ref, do_ref, ssave_ref = refs[:6]
        dhq_ref, dz_ref, dhi_ref, dlb_ref, dst_scr = refs[6 + n_other:]

        @pl.when(pl.program_id(1) == 0)
        def _():
            dst_scr[...] = jnp.zeros_like(dst_scr)
            dlb_ref[...] = jnp.zeros_like(dlb_ref)

        pos, amask = _hgrn_masks(n, rev)
        order = list(range(nc))[::-1] if rev else list(range(nc))
        for hh in range(2):
            sl = slice(hh * LANE, (hh + 1) * LANE)
            hq, lbv = hq_ref[:, sl].astype(F32), lb_ref[0:1, sl]
            q, sq, sig, f, g = _hgrn_gates(hq, z_ref[:, sl].astype(F32), lbv)
            v = hi_ref[:, sl].astype(F32)
            c_ = _hgrn_local(q, f, g, v, pos, amask, rev, nc)
            dvec, ut_all = c_["dvec"], c_["ut_all"]
            drow = lambda c: dvec[c * HG_CHUNK:c * HG_CHUNK + 1, :]
            st = ssave_ref[0, sl, :]
            cols = [None] * nc
            for c in order:
                cols[c] = st
                st = st * drow(c) + ut_all[:, c * LANE:(c + 1) * LANE]
            dob, vb = _bf(do_ref[:, sl]), _bf(v)
            gt_all = _dot(dob, c_["qi_m"], 0, 0)
            dcur = dst_scr[hh]
            dnext = [None] * nc
            ddrow = [None] * nc
            for c in order[::-1]:
                dnext[c] = dcur
                ddrow[c] = jnp.sum(cols[c] * dcur, axis=0, keepdims=True) * drow(c)
                dcur = dcur * drow(c) + gt_all[:, c * LANE:(c + 1) * LANE]
            dst_scr[hh] = dcur
            dsn_all = _bf(jnp.concatenate(dnext, axis=1))
            st_all = _bf(jnp.concatenate(cols, axis=1))
            da = _bf(jnp.where(amask, _dot(dob, vb, 1, 1), 0.0))
            dv = _dot(_bf(c_["a"]), dob, 0, 0) + _dot(c_["kd_m"], dsn_all, 1, 1)
            dqb = _dot(da, _bf(c_["kb"]), 1, 0)
            dkb = _dot(da, _bf(c_["qb"]), 0, 0)
            dqi = _chunk_pick(_dot(dob, st_all, 1, 0), nc)
            dkd = _chunk_pick(_dot(vb, dsn_all, 1, 0), nc)
            dq = dqb * c_["ex_q"] + dqi * c_["ex_i"]
            dk = dkb * c_["ex_k"] + dkd * c_["ex_d"]
            e = dqb * c_["qb"] - dkb * c_["kb"] + dqi * c_["qi"]
            w = dkd * c_["kd"]
            dg = _chunk_scan(e, pos, rev) + (_chunk_scan(w, pos, not rev) - w) + _chunk_rows(ddrow)
            df = dg / f - dk
            dz_ref[:, sl] = _bf(df * (1.0 - lbv) * sig * (1.0 - sig))
            dlb_ref[0:1, sl] += jnp.sum(df * (1.0 - sig), axis=0, keepdims=True)
            dhq = dq * (sq * (1.0 + hq * (1.0 - sq)))
            if other is None:
                dhq_ref[:, sl], dhi_ref[:, sl] = dhq, dv
            else:
                dhq_ref[:, sl], dhi_ref[:, sl] = _bf(dhq + refs[6][:, sl]), _bf(dv + refs[7][:, sl])

    out = pl.BlockSpec((n, 2 * LANE), lambda hp, i: (blk(i), hp))
    sum_dtype = F32 if other is None else BF16
    return _call(
        body, (p, p, p, lb, do, ssave, *(other or ())), name="hgrn_bwd_rev" if rev else "hgrn_bwd", grid=(2, nb),
        in_specs=[col(3), col(z_base), col(9), pl.BlockSpec((1, 2 * LANE), lambda hp, i: (0, hp)), out,
                  pl.BlockSpec((1, 2 * LANE, LANE), lambda hp, i: (blk(i), hp, 0))] + [out] * n_other,
        out_specs=[out, out, out, pl.BlockSpec((1, 2 * LANE), lambda hp, i: (0, hp))],
        out_shape=[jax.ShapeDtypeStruct((t, HG_DIM), sum_dtype), jax.ShapeDtypeStruct((t, HG_DIM), BF16),
                   jax.ShapeDtypeStruct((t, HG_DIM), sum_dtype), jax.ShapeDtypeStruct((1, HG_DIM), F32)],
        scratch_shapes=[pltpu.VMEM((2, LANE, LANE), F32)], after=after)


def _mix_out(att, of, ob, p, gout, w_out, g_post, x, after=(), tm=256):
    t, d = x.shape
    tm = min(tm, t)

    def body(att_ref, of_ref, ob_ref, hg0_ref, hg1_ref, gout_ref, w_ref, g_ref, x_ref, cat_ref, y_ref, x1_ref):
        cat_ref[:, :ATT_Q_DIM] = att_ref[...]
        for h in range(HG_HEADS):
            sl = slice(h * LANE, (h + 1) * LANE)
            hg_ref = hg0_ref if h < 2 else hg1_ref
            hg = hg_ref[:, (h % 2) * LANE:(h % 2 + 1) * LANE].astype(F32)
            nrm = _rms_fwd(of_ref[:, sl] + ob_ref[:, sl], gout_ref[...])
            cat_ref[:, ATT_Q_DIM + h * LANE:ATT_Q_DIM + (h + 1) * LANE] = _bf(nrm * hg * _sigmoid(hg))
        y = _dot(cat_ref[...], w_ref[...], 1, 0)
        y_ref[...] = _bf(y)
        x1_ref[...] = x_ref[...] + _rms_fwd(y, g_ref[...])

    row = lambda width: pl.BlockSpec((tm, width), lambda i: (i, 0))
    return _call(
        body, (att, of, ob, p, p, gout, w_out, g_post, x), name="mix_out", grid=(t // tm,),
        in_specs=[row(ATT_Q_DIM), row(HG_DIM), row(HG_DIM), pl.BlockSpec((tm, 2 * LANE), lambda i: (i, 11)),
                  pl.BlockSpec((tm, 2 * LANE), lambda i: (i, 12)), _full((1, LANE)), _full((d, d)), _full((1, d)), row(d)],
        out_specs=[row(d), row(d), row(d)],
        out_shape=[jax.ShapeDtypeStruct((t, d), BF16), jax.ShapeDtypeStruct((t, d), BF16), jax.ShapeDtypeStruct((t, d), F32)],
        after=after)


def _rec_bwd(dcat, of, ob, p, gout, tm=256):
    t = of.shape[0]
    tm = min(tm, t)

    def body(dc_ref, of_ref, ob_ref, hg0_ref, hg1_ref, gout_ref, do_ref, dhg_ref, dgo_ref):
        @pl.when(pl.program_id(0) == 0)
        def _():
            dgo_ref[...] = jnp.zeros_like(dgo_ref)

        dgo = jnp.zeros((1, LANE), F32)
        for h in range(HG_HEADS):
            sl = slice(h * LANE, (h + 1) * LANE)
            hg_ref = hg0_ref if h < 2 else hg1_ref
            hg = hg_ref[:, (h % 2) * LANE:(h % 2 + 1) * LANE].astype(F32)
            o = of_ref[:, sl] + ob_ref[:, sl]
            sg = _sigmoid(hg)
            drec = dc_ref[:, sl].astype(F32)
            dhg_ref[:, sl] = _bf(drec * _rms_fwd(o, gout_ref[...]) * (sg * (1.0 + hg * (1.0 - sg))))
            do, dg = _rms_bwd(drec * hg * sg, o, gout_ref[...])
            do_ref[:, sl] = do
            dgo = dgo + dg
        dgo_ref[...] += dgo

    row = lambda width: pl.BlockSpec((tm, width), lambda i: (i, 0))
    return pl.pallas_call(
        body, name="rec_bwd", grid=(t // tm,),
        in_specs=[pl.BlockSpec((tm, HG_DIM), lambda i: (i, 1)), row(HG_DIM), row(HG_DIM),
                  pl.BlockSpec((tm, 2 * LANE), lambda i: (i, 11)), pl.BlockSpec((tm, 2 * LANE), lambda i: (i, 12)),
                  _full((1, LANE))],
        out_specs=[row(HG_DIM), row(HG_DIM), _full((1, LANE))],
        out_shape=[jax.ShapeDtypeStruct((t, HG_DIM), F32), jax.ShapeDtypeStruct((t, HG_DIM), BF16),
                   jax.ShapeDtypeStruct((1, LANE), F32)],
        compiler_params=_params(),
    )(dcat, of, ob, p, p, gout)


def _xattn_fwd(q, kv, tq=512):
    t, d = q.shape
    tq = min(tq, t)
    nm = kv.shape[0]

    def body(q_ref, kv_ref, o_ref):
        for h in range(X_HEADS):
            sl = slice(h * X_HEAD_DIM, (h + 1) * X_HEAD_DIM)
            s = _dot(_bf(q_ref[:, sl]), _bf(kv_ref[:, sl]), 1, 1) * (X_HEAD_DIM ** -0.5)
            e = jnp.exp(s - jnp.max(s, axis=-1, keepdims=True))
            pr = e * (1.0 / jnp.sum(e, axis=-1, keepdims=True))
            o_ref[:, sl] = _bf(_dot(_bf(pr), _bf(kv_ref[:, d + h * X_HEAD_DIM:d + (h + 1) * X_HEAD_DIM]), 1, 0))

    return pl.pallas_call(
        body, name="xattn_fwd", grid=(t // tq,),
        in_specs=[pl.BlockSpec((tq, d), lambda i: (i, 0)), _full((nm, 2 * d))],
        out_specs=pl.BlockSpec((tq, d), lambda i: (i, 0)),
        out_shape=jax.ShapeDtypeStruct((t, d), BF16),
        compiler_params=_params(),
    )(q, kv)


def _xattn_bwd(q, kv, do, tq=512):
    t, d = q.shape
    tq = min(tq, t)
    nm = kv.shape[0]

    def body(q_ref, kv_ref, do_ref, dq_ref, dkv_ref):
        @pl.when(pl.program_id(0) == 0)
        def _():
            dkv_ref[...] = jnp.zeros_like(dkv_ref)

        for h in range(X_HEADS):
            sl = slice(h * X_HEAD_DIM, (h + 1) * X_HEAD_DIM)
            slv = slice(d + h * X_HEAD_DIM, d + (h + 1) * X_HEAD_DIM)
            qb, kb, vb, dob = _bf(q_ref[:, sl]), _bf(kv_ref[:, sl]), _bf(kv_ref[:, slv]), _bf(do_ref[:, sl])
            s = _dot(qb, kb, 1, 1) * (X_HEAD_DIM ** -0.5)
            e = jnp.exp(s - jnp.max(s, axis=-1, keepdims=True))
            pr = e * (1.0 / jnp.sum(e, axis=-1, keepdims=True))
            dpr = _dot(dob, vb, 1, 1)
            ds = _bf(pr * (dpr - jnp.sum(pr * dpr, axis=-1, keepdims=True)) * (X_HEAD_DIM ** -0.5))
            dq_ref[:, sl] = _bf(_dot(ds, kb, 1, 0))
            dkv_ref[:, sl] += _dot(ds, qb, 0, 0)
            dkv_ref[:, slv] += _dot(_bf(pr), dob, 0, 0)

    return pl.pallas_call(
        body, name="xattn_bwd", grid=(t // tq,),
        in_specs=[pl.BlockSpec((tq, d), lambda i: (i, 0)), _full((nm, 2 * d)), pl.BlockSpec((tq, d), lambda i: (i, 0))],
        out_specs=[pl.BlockSpec((tq, d), lambda i: (i, 0)), _full((nm, 2 * d))],
        out_shape=[jax.ShapeDtypeStruct((t, d), BF16), jax.ShapeDtypeStruct((nm, 2 * d), F32)],
        compiler_params=_params(),
    )(q, kv, do)


CONV_TN = 256


def _shift_rows(u, row, t, delta):
    if delta < 0:
        return jnp.where(row == 0, 0.0, pltpu.roll(u, 1, 0))
    return jnp.where(row == t - 1, 0.0, pltpu.roll(u, t - 1, 0))


def _conv_gate_fwd(u, conv_w, conv_b):
    t = u.shape[0]
    nt = D_FF // CONV_TN

    def body(ug_ref, uv_ref, wg_ref, wv_ref, bg_ref, bv_ref, a_ref):
        row = lax.broadcasted_iota(jnp.int32, (t, CONV_TN), 0)

        def conv(u_ref, w_ref, b_ref):
            uu = u_ref[...].astype(F32)
            return (b_ref[...] + _shift_rows(uu, row, t, -1) * w_ref[0:1, :] + uu * w_ref[1:2, :]
                    + _shift_rows(uu, row, t, 1) * w_ref[2:3, :])

        gate = conv(ug_ref, wg_ref, bg_ref)
        a_ref[...] = _bf(gate * _sigmoid(gate) * conv(uv_ref, wv_ref, bv_ref))

    col = lambda rows, off: pl.BlockSpec((rows, CONV_TN), lambda j: (0, j + off))
    return pl.pallas_call(
        body, name="conv_gate_fwd", grid=(nt,),
        in_specs=[col(t, 0), col(t, nt), col(3, 0), col(3, nt), col(1, 0), col(1, nt)],
        out_specs=col(t, 0),
        out_shape=jax.ShapeDtypeStruct((t, D_FF), BF16),
        compiler_params=_params(),
    )(u, u, conv_w, conv_w, conv_b, conv_b)


def _conv_gate_bwd(u, conv_w, conv_b, da, after=()):
    t = u.shape[0]
    nt = D_FF // CONV_TN

    def body(ug_ref, uv_ref, wg_ref, wv_ref, bg_ref, bv_ref, da_ref, dug_ref, dwg_ref, dbg_ref, duv_ref, dwv_ref, dbv_ref):
        row = lax.broadcasted_iota(jnp.int32, (t, CONV_TN), 0)
        ug, uv = ug_ref[...].astype(F32), uv_ref[...].astype(F32)
        ug_m, ug_p = _shift_rows(ug, row, t, -1), _shift_rows(ug, row, t, 1)
        uv_m, uv_p = _shift_rows(uv, row, t, -1), _shift_rows(uv, row, t, 1)
        gate = bg_ref[...] + ug_m * wg_ref[0:1, :] + ug * wg_ref[1:2, :] + ug_p * wg_ref[2:3, :]
        val = bv_ref[...] + uv_m * wv_ref[0:1, :] + uv * wv_ref[1:2, :] + uv_p * wv_ref[2:3, :]
        sg = _sigmoid(gate)
        da_ = da_ref[...].astype(F32)

        def emit(dacc, um, uu, up, w_ref, du_ref, dw_ref, db_ref):
            du_ref[...] = _bf(_shift_rows(dacc, row, t, 1) * w_ref[0:1, :] + dacc * w_ref[1:2, :]
                              + _shift_rows(dacc, row, t, -1) * w_ref[2:3, :])
            dw_ref[0:1, :] = jnp.sum(dacc * um, axis=0, keepdims=True)
            dw_ref[1:2, :] = jnp.sum(dacc * uu, axis=0, keepdims=True)
            dw_ref[2:3, :] = jnp.sum(dacc * up, axis=0, keepdims=True)
            db_ref[...] = jnp.sum(dacc, axis=0, keepdims=True)

        emit(da_ * val * (sg * (1.0 + gate * (1.0 - sg))), ug_m, ug, ug_p, wg_ref, dug_ref, dwg_ref, dbg_ref)
        emit(da_ * gate * sg, uv_m, uv, uv_p, wv_ref, duv_ref, dwv_ref, dbv_ref)

    col = lambda rows, off: pl.BlockSpec((rows, CONV_TN), lambda j: (0, j + off))
    half_shapes = [jax.ShapeDtypeStruct((t, D_FF), BF16), jax.ShapeDtypeStruct((3, D_FF), F32),
                   jax.ShapeDtypeStruct((1, D_FF), F32)]
    outs = _call(
        body, (u, u, conv_w, conv_w, conv_b, conv_b, da), name="conv_gate_bwd", grid=(nt,),
        in_specs=[col(t, 0), col(t, nt), col(3, 0), col(3, nt), col(1, 0), col(1, nt), col(t, 0)],
        out_specs=[col(t, 0), col(3, 0), col(1, 0)] * 2, out_shape=half_shapes * 2, after=after)
    return outs[:3], outs[3:]


def _row_tile(r, cap):
    best = r
    for cand in range(16, cap + 1, 16):
        if r % cand == 0:
            best = cand
    return best


def _sum_parts(own, got, me, name, tr=256):
    _, r, c = got.shape
    tr = _row_tile(r, tr)

    def body(me_ref, own_ref, got_ref, o_ref):
        mine = own_ref[...].astype(F32)
        acc = None
        for i in range(N_DEV):
            term = jnp.where(me_ref[0] == i, mine, got_ref[i].astype(F32))
            acc = term if acc is None else acc + term
        o_ref[...] = acc

    return pl.pallas_call(
        body, name=name,
        grid_spec=pltpu.PrefetchScalarGridSpec(
            num_scalar_prefetch=1, grid=(r // tr,),
            in_specs=[pl.BlockSpec((None, tr, c), lambda i, me_ref: (me_ref[0], i, 0)),
                      pl.BlockSpec((N_DEV, tr, c), lambda i, me_ref: (0, i, 0))],
            out_specs=pl.BlockSpec((tr, c), lambda i, me_ref: (i, 0))),
        out_shape=jax.ShapeDtypeStruct((r, c), F32),
        compiler_params=_params(),
    )(me, own, got)


def _adamw_math(w_, g_, m_, v_):
    m_ = ADAM_B1 * m_ + (1.0 - ADAM_B1) * g_
    v_ = ADAM_B2 * v_ + (1.0 - ADAM_B2) * (g_ * g_)
    m_hat = m_ / (1.0 - ADAM_B1 ** ADAM_STEP)
    v_hat = v_ / (1.0 - ADAM_B2 ** ADAM_STEP)
    return -ADAM_LR * (m_hat / (jnp.sqrt(v_hat) + ADAM_EPS) + ADAM_WD * w_), m_, v_


def _sum_adamw(own, got, me, w, m, v, name, tr=256):
    _, r, c = got.shape
    tr = _row_tile(r, tr)

    def body(me_ref, own_ref, got_ref, w_ref, m_ref, v_ref, g_ref, d_ref, nm_ref, nv_ref):
        mine = own_ref[...].astype(F32)
        acc = None
        for i in range(N_DEV):
            term = jnp.where(me_ref[0] == i, mine, got_ref[i].astype(F32))
            acc = term if acc is None else acc + term
        g_ref[...] = acc
        d_ref[...], nm_ref[...], nv_ref[...] = _adamw_math(w_ref[...], acc, m_ref[...], v_ref[...])

    blk = pl.BlockSpec((tr, c), lambda i, me_ref: (i, 0))
    return pl.pallas_call(
        body, name=name,
        grid_spec=pltpu.PrefetchScalarGridSpec(
            num_scalar_prefetch=1, grid=(r // tr,),
            in_specs=[pl.BlockSpec((None, tr, c), lambda i, me_ref: (me_ref[0], i, 0)),
                      pl.BlockSpec((N_DEV, tr, c), lambda i, me_ref: (0, i, 0)), blk, blk, blk],
            out_specs=[blk] * 4),
        out_shape=[jax.ShapeDtypeStruct((r, c), F32)] * 4,
        compiler_params=_params(),
    )(me, own, got, w, m, v)


def _adamw(w, g, m, v, name, tr=256):
    r, c = w.shape
    tr = _row_tile(r, tr)

    def body(w_ref, g_ref, m_ref, v_ref, d_ref, nm_ref, nv_ref):
        d_ref[...], nm_ref[...], nv_ref[...] = _adamw_math(w_ref[...], g_ref[...], m_ref[...], v_ref[...])

    blk = pl.BlockSpec((tr, c), lambda i: (i, 0))
    return pl.pallas_call(
        body, name=name, grid=(r // tr,),
        in_specs=[blk] * 4, out_specs=[blk] * 3,
        out_shape=[jax.ShapeDtypeStruct((r, c), F32)] * 3,
        compiler_params=_params(),
    )(w, g, m, v)


def _mesh_pos():
    return lax.axis_index("x"), lax.axis_index("y"), lax.axis_index("c")


def _dev_index(px, py, pc):
    return 4 * px + 2 * py + pc


class _Gather:
    def __init__(self, arrs):
        self.arrs = list(arrs)
        n = len(self.arrs)
        self.out_shape = [jax.ShapeDtypeStruct((N_DEV,) + a.shape, a.dtype) for a in self.arrs]
        self.sems = [pltpu.SemaphoreType.DMA((7, n)), pltpu.SemaphoreType.DMA((7, n)), pltpu.SemaphoreType.DMA((n,))]

    def _ctx(self, ins, outs, sems):
        send_sems, recv_sems, local_sems = sems
        x, y, c = _mesh_pos()
        chips = [(1 - x, y), (x, 1 - y), (1 - x, 1 - y)]

        def copy(k, a, block, to, src=None):
            dst = outs[a].at[_dev_index(*block)]
            return pltpu.make_async_remote_copy(
                src_ref=dst if src is None else src, dst_ref=dst, send_sem=send_sems.at[k, a], recv_sem=recv_sems.at[k, a],
                device_id=to, device_id_type=MESH_T)

        n = len(ins)
        me, sibling = (x, y, c), (x, y, 1 - c)
        mine = [pltpu.make_async_copy(ins[a], outs[a].at[_dev_index(*me)], local_sems.at[a]) for a in range(n)]
        first = [copy(0, a, me, sibling, src=ins[a]) for a in range(n)]
        first += [copy(1 + j, a, me, (*chip, c), src=ins[a]) for j, chip in enumerate(chips) for a in range(n)]
        passed = [copy(4 + j, a, (*chip, c), sibling) for j, chip in enumerate(chips) for a in range(n)]
        return n, c, me, sibling, chips, copy, mine, first, passed

    def start(self, ins, outs, sems):
        _, _, _, _, _, _, mine, first, _ = self._ctx(ins, outs, sems)
        for cp in mine + first:
            cp.start()

    def forward(self, ins, outs, sems):
        n, c, me, _, chips, copy, _, _, passed = self._ctx(ins, outs, sems)
        for j, chip in enumerate(chips):
            for a in range(n):
                copy(1 + j, a, (*chip, c), me).wait_recv()
                passed[j * n + a].start()

    def finish(self, ins, outs, sems):
        n, c, me, sibling, chips, copy, mine, first, passed = self._ctx(ins, outs, sems)
        for a in range(n):
            copy(0, a, sibling, me).wait_recv()
        for j, chip in enumerate(chips):
            for a in range(n):
                copy(4 + j, a, (*chip, 1 - c), me).wait_recv()
        for cp in first + passed:
            cp.wait_send()
        for cp in mine:
            cp.wait()


def _comm_alone(comm, name):
    n = len(comm.arrs)

    def body(*refs):
        ins, outs, sems = refs[:n], refs[n:2 * n], refs[2 * n:]
        comm.start(ins, outs, sems)
        if comm.forward is not None:
            comm.forward(ins, outs, sems)
        comm.finish(ins, outs, sems)

    any_spec = pl.BlockSpec(memory_space=pl.ANY)
    return pl.pallas_call(body, name=name, in_specs=[any_spec] * n, out_specs=[any_spec] * n, out_shape=comm.out_shape,
                          scratch_shapes=comm.sems)(*comm.arrs)


def _peers(x, y, c):
    out = []
    for k in range(1, N_DEV):
        pos = (1 - x if k & 4 else x, 1 - y if k & 2 else y, 1 - c if k & 1 else c)
        out.append((k, pos, _dev_index(*pos)))
    return out


def _exchange_begin(arrs, tag):
    n = len(arrs)
    lands = [lax.empty(a.shape, a.dtype) for a in arrs]

    def start_body(*refs):
        ins, land = refs[:n], refs[n:2 * n]
        send_sems, recv_sems, token = refs[2 * n], refs[2 * n + 1], refs[-1]
        x, y, c = _mesh_pos()
        me_i = _dev_index(x, y, c)
        for k, pos, peer_i in _peers(x, y, c):
            for a in range(n):
                pltpu.make_async_remote_copy(
                    src_ref=ins[a].at[peer_i], dst_ref=land[a].at[me_i], send_sem=send_sems.at[(k - 1) * n + a],
                    recv_sem=recv_sems.at[(k - 1) * n + a], device_id=pos, device_id_type=MESH_T).start()
        token[...] = jnp.zeros_like(token)

    hbm = pl.BlockSpec(memory_space=pltpu.HBM)
    sem = pl.BlockSpec(memory_space=pltpu.SEMAPHORE)
    thru = [pltpu.HBM(a.shape, a.dtype) for a in arrs]
    outs = pl.pallas_call(
        start_body, name="exchange_start_" + tag,
        out_shape=[pltpu.SemaphoreType.DMA((7 * n,)), pltpu.SemaphoreType.DMA((7 * n,))] + thru + thru
        + [jax.ShapeDtypeStruct((8, LANE), F32)],
        in_specs=[hbm] * (2 * n), out_specs=[sem, sem] + [hbm] * (2 * n) + [pl.BlockSpec(memory_space=pltpu.VMEM)],
        input_output_aliases={i: 2 + i for i in range(2 * n)},
        compiler_params=pltpu.CompilerParams(has_side_effects=pltpu.SideEffectType.DATAFLOW_SIDE_EFFECTING),
    )(*[pltpu.with_memory_space_constraint(a, pltpu.HBM) for a in list(arrs) + list(lands)])
    return (tag, n, outs[0], outs[1], outs[2:2 + n], outs[2 + n:2 + 2 * n]), outs[-1]


def _exchange_end(handle, after):
    tag, n, send_sems, recv_sems, srcs, lands = handle

    def body(*refs):
        ins, land = refs[:n], refs[n:2 * n]
        send_sems_, recv_sems_ = refs[2 * n], refs[2 * n + 1]
        x, y, c = _mesh_pos()
        me_i = _dev_index(x, y, c)
        for k, pos, peer_i in _peers(x, y, c):
            for a in range(n):
                cp = pltpu.make_async_remote_copy(
                    src_ref=ins[a].at[peer_i], dst_ref=land[a].at[me_i], send_sem=send_sems_.at[(k - 1) * n + a],
                    recv_sem=recv_sems_.at[(k - 1) * n + a], device_id=pos, device_id_type=MESH_T)
                cp.wait_send()
                cp.wait_recv()

    hbm = pl.BlockSpec(memory_space=pltpu.HBM)
    sem = pl.BlockSpec(memory_space=pltpu.SEMAPHORE)
    outs = pl.pallas_call(
        body, name="exchange_end_" + tag, out_shape=[pltpu.HBM(a.shape, a.dtype) for a in list(srcs) + list(lands)],
        in_specs=[hbm] * (2 * n) + [sem, sem, pl.BlockSpec(memory_space=pl.ANY)], out_specs=[hbm] * (2 * n),
        input_output_aliases={i: i for i in range(2 * n)},
        compiler_params=pltpu.CompilerParams(has_side_effects=pltpu.SideEffectType.DATAFLOW_SIDE_EFFECTING),
    )(*srcs, *lands, send_sems, recv_sems, after)
    return list(zip(outs[:n], outs[n:]))


def _gather_begin(arrs, tag):
    n = len(arrs)
    me = _dev_index(*_mesh_pos())
    lands = [lax.dynamic_update_slice(lax.empty((N_DEV,) + a.shape, a.dtype), a[None], (me,) + (0,) * a.ndim) for a in arrs]

    def start_body(*refs):
        ins, land = refs[:n], refs[n:2 * n]
        send_sems, recv_sems, token = refs[2 * n], refs[2 * n + 1], refs[-1]
        x, y, c = _mesh_pos()
        me_i = _dev_index(x, y, c)
        for a in range(n):
            for k, pos, _ in _peers(x, y, c):
                pltpu.make_async_remote_copy(
                    src_ref=ins[a], dst_ref=land[a].at[me_i], send_sem=send_sems.at[(k - 1) * n + a],
                    recv_sem=recv_sems.at[(k - 1) * n + a], device_id=pos, device_id_type=MESH_T).start()
        token[...] = jnp.zeros_like(token)

    hbm = pl.BlockSpec(memory_space=pltpu.HBM)
    sem = pl.BlockSpec(memory_space=pltpu.SEMAPHORE)
    thru = [pltpu.HBM(a.shape, a.dtype) for a in list(arrs) + lands]
    outs = pl.pallas_call(
        start_body, name="gather_start_" + tag,
        out_shape=[pltpu.SemaphoreType.DMA((7 * n,)), pltpu.SemaphoreType.DMA((7 * n,))] + thru
        + [jax.ShapeDtypeStruct((8, LANE), F32)],
        in_specs=[hbm] * (2 * n), out_specs=[sem, sem] + [hbm] * (2 * n) + [pl.BlockSpec(memory_space=pltpu.VMEM)],
        input_output_aliases={i: 2 + i for i in range(2 * n)},
        compiler_params=pltpu.CompilerParams(has_side_effects=pltpu.SideEffectType.DATAFLOW_SIDE_EFFECTING),
    )(*[pltpu.with_memory_space_constraint(a, pltpu.HBM) for a in list(arrs) + lands])
    return (tag, n, outs[0], outs[1], outs[2:2 + n], outs[2 + n:2 + 2 * n]), outs[-1]


def _gather_end(handle, which, after):
    tag, n, send_sems, recv_sems, srcs, lands = handle
    m = len(which)

    def body(*refs):
        ins, land = refs[:m], refs[m:2 * m]
        send_sems_, recv_sems_ = refs[2 * m], refs[2 * m + 1]
        x, y, c = _mesh_pos()
        me_i = _dev_index(x, y, c)
        for j, a in enumerate(which):
            for k, pos, _ in _peers(x, y, c):
                cp = pltpu.make_async_remote_copy(
                    src_ref=ins[j], dst_ref=land[j].at[me_i], send_sem=send_sems_.at[(k - 1) * n + a],
                    recv_sem=recv_sems_.at[(k - 1) * n + a], device_id=pos, device_id_type=MESH_T)
                cp.wait_send()
                cp.wait_recv()

    hbm = pl.BlockSpec(memory_space=pltpu.HBM)
    sem = pl.BlockSpec(memory_space=pltpu.SEMAPHORE)
    ops = [srcs[a] for a in which] + [lands[a] for a in which]
    outs = pl.pallas_call(
        body, name="gather_end_%s_%s" % (tag, "_".join(str(a) for a in which)),
        out_shape=[pltpu.HBM(a.shape, a.dtype) for a in ops],
        in_specs=[hbm] * (2 * m) + [sem, sem, pl.BlockSpec(memory_space=pl.ANY)], out_specs=[hbm] * (2 * m),
        input_output_aliases={i: i for i in range(2 * m)},
        compiler_params=pltpu.CompilerParams(has_side_effects=pltpu.SideEffectType.DATAFLOW_SIDE_EFFECTING),
    )(*ops, send_sems, recv_sems, after)
    return list(outs[m:])


_SPLIT = dict(has_side_effects=pltpu.SideEffectType.DATAFLOW_SIDE_EFFECTING)


def _chips(x, y):
    return [(1 - x, y), (x, 1 - y), (1 - x, 1 - y)]


def _gather2_begin(arrs, tag):
    n = len(arrs)
    me = _dev_index(*_mesh_pos())
    lands = [lax.dynamic_update_slice(lax.empty((N_DEV,) + a.shape, a.dtype), a[None], (me,) + (0,) * a.ndim) for a in arrs]

    def body(*refs):
        ins, land = refs[:n], refs[n:2 * n]
        send1, recv1, token = refs[2 * n], refs[2 * n + 1], refs[-1]
        x, y, c = _mesh_pos()
        me_i = _dev_index(x, y, c)
        targets = [(x, y, 1 - c)] + [(*chip, c) for chip in _chips(x, y)]
        for a in range(n):
            for k, pos in enumerate(targets):
                pltpu.make_async_remote_copy(
                    src_ref=ins[a], dst_ref=land[a].at[me_i], send_sem=send1.at[k * n + a], recv_sem=recv1.at[k * n + a],
                    device_id=pos, device_id_type=MESH_T).start()
        token[...] = jnp.zeros_like(token)

    hbm = pl.BlockSpec(memory_space=pltpu.HBM)
    sem = pl.BlockSpec(memory_space=pltpu.SEMAPHORE)
    thru = [pltpu.HBM(a.shape, a.dtype) for a in list(arrs) + lands]
    outs = pl.pallas_call(
        body, name="gather_start_" + tag,
        out_shape=[pltpu.SemaphoreType.DMA((4 * n,)), pltpu.SemaphoreType.DMA((4 * n,))] + thru
        + [jax.ShapeDtypeStruct((8, LANE), F32)],
        in_specs=[hbm] * (2 * n), out_specs=[sem, sem] + [hbm] * (2 * n) + [pl.BlockSpec(memory_space=pltpu.VMEM)],
        input_output_aliases={i: 2 + i for i in range(2 * n)}, compiler_params=pltpu.CompilerParams(**_SPLIT),
    )(*[pltpu.with_memory_space_constraint(a, pltpu.HBM) for a in list(arrs) + lands])
    return dict(tag=tag, n=n, send1=outs[0], recv1=outs[1], srcs=list(outs[2:2 + n]), lands=list(outs[2 + n:2 + 2 * n])), outs[-1]


def _gather2_forward(handle, which, after, part):
    n, m = handle["n"], len(which)

    def body(*refs):
        land, recv1 = refs[:m], refs[m]
        send2, recv2, token = refs[m + 2], refs[m + 3], refs[-1]
        token[...] = jnp.zeros_like(token)
        x, y, c = _mesh_pos()
        for j, a in enumerate(which):
            for k, chip in enumerate(_chips(x, y)):
                blk = land[j].at[_dev_index(*chip, c)]
                copy = pltpu.make_async_remote_copy(
                    src_ref=blk, dst_ref=blk, send_sem=send2.at[k * m + j], recv_sem=recv2.at[k * m + j],
                    device_id=(x, y, 1 - c), device_id_type=MESH_T)
                pltpu.make_async_remote_copy(
                    src_ref=blk, dst_ref=blk, send_sem=send2.at[k * m + j], recv_sem=recv1.at[(1 + k) * n + a],
                    device_id=(*chip, c), device_id_type=MESH_T).wait_recv()
                copy.start()

    hbm = pl.BlockSpec(memory_space=pltpu.HBM)
    sem = pl.BlockSpec(memory_space=pltpu.SEMAPHORE)
    lands = [handle["lands"][a] for a in which]
    outs = pl.pallas_call(
        body, name="gather_forward_%s_%s" % (handle["tag"], part),
        out_shape=[pltpu.SemaphoreType.DMA((3 * m,)), pltpu.SemaphoreType.DMA((3 * m,))] + [pltpu.HBM(a.shape, a.dtype) for a in lands]
        + [jax.ShapeDtypeStruct((8, LANE), F32)],
        in_specs=[hbm] * m + [sem, pl.BlockSpec(memory_space=pl.ANY)],
        out_specs=[sem, sem] + [hbm] * m + [pl.BlockSpec(memory_space=pltpu.VMEM)],
        input_output_aliases={i: 2 + i for i in range(m)}, compiler_params=pltpu.CompilerParams(**_SPLIT),
    )(*lands, handle["recv1"], after)
    for j, a in enumerate(which):
        handle["lands"][a] = outs[2 + j]
        handle[("leg2", a)] = (outs[0], outs[1], j, m)
    return outs[-1]


def _gather2_end(handle, a, after):
    n = handle["n"]
    send2, recv2, j, m = handle[("leg2", a)]

    def body(src, land, send1, recv1, send2_, recv2_, after_ref, src_out, land_out):
        x, y, c = _mesh_pos()
        me_i = _dev_index(x, y, c)
        sibling = (x, y, 1 - c)
        for k, pos in enumerate([sibling] + [(*chip, c) for chip in _chips(x, y)]):
            first = pltpu.make_async_remote_copy(
                src_ref=src, dst_ref=land.at[me_i], send_sem=send1.at[k * n + a], recv_sem=recv1.at[k * n + a],
                device_id=pos, device_id_type=MESH_T)
            first.wait_send()
            if k == 0:
                first.wait_recv()
        for k in range(3):
            second = pltpu.make_async_remote_copy(
                src_ref=land.at[me_i], dst_ref=land.at[me_i], send_sem=send2_.at[k * m + j], recv_sem=recv2_.at[k * m + j],
                device_id=sibling, device_id_type=MESH_T)
            second.wait_send()
            second.wait_recv()

    hbm = pl.BlockSpec(memory_space=pltpu.HBM)
    sem = pl.BlockSpec(memory_space=pltpu.SEMAPHORE)
    src, land = handle["srcs"][a], handle["lands"][a]
    outs = pl.pallas_call(
        body, name="gather_end_%s_%d" % (handle["tag"], a),
        out_shape=[pltpu.HBM(src.shape, src.dtype), pltpu.HBM(land.shape, land.dtype)],
        in_specs=[hbm, hbm, sem, sem, sem, sem, pl.BlockSpec(memory_space=pl.ANY)], out_specs=[hbm, hbm],
        input_output_aliases={0: 0, 1: 1}, compiler_params=pltpu.CompilerParams(**_SPLIT),
    )(src, land, handle["send1"], handle["recv1"], send2, recv2, after)
    return outs[1]


def _call(body, operands, *, name, grid, in_specs, out_specs, out_shape, scratch_shapes=(), after=()):
    ni, nd = len(in_specs), len(after)

    def ordered(*refs):
        body(*refs[:ni], *refs[ni + nd:])

    outs = pl.pallas_call(
        ordered, name=name, grid=grid, in_specs=list(in_specs) + [pl.BlockSpec(memory_space=pl.ANY)] * nd,
        out_specs=out_specs, out_shape=out_shape, scratch_shapes=list(scratch_shapes), compiler_params=_params(),
    )(*operands, *after)
    return list(outs)


def _rows(a):
    return a.reshape(-1, a.shape[-1])


def _slots(a):
    return a.reshape(N_DEV, -1, a.shape[-1])


def _local_step(x, mem, tgt, w, wire):
    tables = _rope_tables(x.shape[0])
    bd = _head_sum_matrix()
    two = lambda g: jnp.tile(g, (1, 2))
    gq2, gk2 = two(w["q_norm_g"]), two(w["k_norm_g"])

    w_in_t, hg_lb, conv_w = _comm_alone(_Gather([wire["w_in"], w["hg_lb"].reshape(4, -1), w["conv_w"][0]]), "gather_w_in")
    w_in_t = _rows(w_in_t)
    hg_lb = jnp.transpose(hg_lb.reshape(N_DEV, 2, 2, -1), (1, 2, 0, 3)).reshape(2, 2, HG_DIM)
    conv_w = jnp.transpose(conv_w, (1, 0, 2)).reshape(3, 2 * D_FF)
    lb_a0, lb_a1 = hg_lb[:, 0, :], hg_lb[:, 1, :]

    order = ("w_out", "w_xq", "w_xkv", "w_xo", "w_up", "w_down")
    w_in_t, *later = lax.optimization_barrier((w_in_t, *[wire[n] for n in order]))
    fetch, started = _gather2_begin(later, "w")
    take = lambda n, after: _rows(_gather2_end(fetch, order.index(n), after))

    h1, p = _norm_mm(x, w["pre_mix_g"], w_in_t, True, N_IN, "in_proj", after=[started], p_dtype=BF16)
    qr, kr = _qk_prep(p, gq2, gk2, tables, bd)
    att, att32, lse = _attn_fwd(qr, kr, p)
    passed = _gather2_forward(fetch, [0, 1, 2, 3], att, "a")
    lb = _lower_bounds(lb_a0, lb_a1)
    of, s_f = _hgrn_fwd(p, lb[0:1], False, after=[passed])
    ob, s_b = _hgrn_fwd(p, lb[1:2], True, after=[passed, of])
    passed = _gather2_forward(fetch, [4, 5], ob, "b")
    w_out = take("w_out", ob)
    cat, mixed, x1 = _mix_out(att, of, ob, p, w["hg_out_norm_g"], w_out, w["post_mix_g"], x, after=[passed])

    w_xq = take("w_xq", x1)
    h2, q2 = _norm_mm(x1, w["pre_x_g"], w_xq, False, 1024, "xq_proj", p_dtype=BF16)
    w_xkv_t = take("w_xkv", q2)
    mn, kv = _norm_mm(mem, w["mem_norm_g"], w_xkv_t, True, 2 * D_MODEL, "xkv_proj")
    o2 = _xattn_fwd(q2, kv)
    w_xo = take("w_xo", o2)
    y2, x2 = _mm_postnorm_res(o2, w_xo, w["post_x_g"], x1, "xo_proj")

    w_up_t = take("w_up", x2)
    h3, u = _norm_mm(x2, w["pre_ffn_g"], w_up_t, True, 2 * D_FF, "up_proj", tm=256, p_dtype=BF16)
    a = _conv_gate_fwd(u, conv_w, w["conv_b"])
    w_down = take("w_down", a)
    y3, dx3, loss = _mm_postnorm_res_loss(a, w_down, w["post_ffn_g"], x2, tgt, "down_proj")

    g, pending = {}, {}
    dy3, da, g["post_ffn_g"] = _postnorm_bwd_mm(dx3, y3, w["post_ffn_g"], w_down, "down_bwd", BF16)
    pending["down"], started = _exchange_begin([_slots(_dw(a, dy3, "dw_down", tka=D_FF // 2))], "down")
    (du_g, dcw_g, dcb_g), (du_v, dcw_v, dcb_v) = _conv_gate_bwd(u, conv_w, w["conv_b"], da, after=[started])
    g["conv_w"] = jnp.concatenate([dcw_g, dcw_v], axis=1)
    g["conv_b"] = jnp.concatenate([dcb_g, dcb_v], axis=1)
    pending["up"], started = _exchange_begin([_slots(_dw([du_g, du_v], h3, "dw_up", tka=D_FF // 2))], "up")
    dx2, g["pre_ffn_g"] = _mm_prenorm_bwd([du_g, du_v], w_up_t, True, x2, w["pre_ffn_g"], dx3, "up_bwd", after=[started], tm=256)

    dy2, do2, g["post_x_g"] = _postnorm_bwd_mm(dx2, y2, w["post_x_g"], w_xo, "xo_bwd", BF16)
    dw_xo = _dw(o2, dy2, "dw_xo", tka=512)
    dq2, dkv = _xattn_bwd(q2, kv, do2)
    dw_xq = _dw(h2, dq2, "dw_xq", tka=512)
    dkvb = _bf(dkv)
    dw_xkv = _dw(dkvb, mn, "dw_xkv")
    _, g["mem_norm_g"] = _mm_prenorm_bwd(dkvb, w_xkv_t, True, mem, w["mem_norm_g"], jnp.zeros_like(mem), "xkv_bwd")
    dx1, g["pre_x_g"] = _mm_prenorm_bwd(dq2, w_xq, False, x1, w["pre_x_g"], dx2, "xq_bwd")

    dmixed, dcat, g["post_mix_g"] = _postnorm_bwd_mm(dx1, mixed, w["post_mix_g"], w_out, "out_bwd", BF16)
    dw_out = _dw(cat, dmixed, "dw_out", tka=512)
    pending["x"], started = _exchange_begin([_slots(dw_xo), _slots(dw_xq), _slots(dw_xkv), _slots(dw_out)], "x")
    do, dhg, g["hg_out_norm_g"] = _rec_bwd(dcat, of, ob, p, w["hg_out_norm_g"])
    dhq_f, dz_f, dhi_f, dlb_f = _hgrn_bwd(p, lb[0:1], do, s_f, False, after=[started])
    dhq, dz_b, dhi, dlb_b = _hgrn_bwd(p, lb[1:2], do, s_b, True, other=(dhq_f, dhi_f))
    d_a0, d_a1 = _lower_bounds_bwd(lb_a0, lb_a1, jnp.concatenate([dlb_f, dlb_b], axis=0))
    g["hg_lb"] = jnp.stack([d_a0, d_a1], axis=1)
    dqr, dkr, dv = _attn_bwd(qr, kr, p, dcat, att32, lse)
    dp_att, dgq, dgk = _qk_prep_bwd(p, dqr, dkr, dv, gq2, gk2, tables, bd)
    g["q_norm_g"], g["k_norm_g"] = dgq, dgk
    dp = [dp_att, dhq, dz_f, dz_b, dhi, dhg]

    early = [n for n in _SMALL if n != "pre_mix_g"]
    vals = [g[n].reshape(-1, g[n].shape[-1]) for n in early] + [jnp.pad(loss, ((0, 0), (0, LANE - 1)))]
    small = dict(early=early, shapes=[val.shape for val in vals])
    small["fetch"], started = _gather_begin([_pack_small(vals, "pack_small")], "small")
    pending["in"], started = _exchange_begin([_slots(_dw(dp, h1, "dw_in", after=[started]))], "in")
    dx, g["pre_mix_g"] = _mm_prenorm_bwd(dp, w_in_t, True, x, w["pre_mix_g"], dx1, "in_bwd", after=[started])
    small["fetch_last"], _ = _gather_begin([_pack_small([g["pre_mix_g"]], "pack_last")], "last")
    return dx, g, pending, small


_COL_SHARDED = ("w_in", "w_xkv", "w_up")
_ROW_SHARDED = ("w_out", "w_xq", "w_xo", "w_down")
_REPLICATED = ("pre_mix_g", "q_norm_g", "k_norm_g", "hg_out_norm_g", "post_mix_g", "pre_x_g", "mem_norm_g", "post_x_g",
               "pre_ffn_g", "conv_b", "post_ffn_g")
_WEIGHTS = ("pre_mix_g", "w_in", "q_norm_g", "k_norm_g", "hg_lb", "hg_out_norm_g", "w_out", "post_mix_g", "pre_x_g",
            "mem_norm_g", "w_xq", "w_xkv", "w_xo", "post_x_g", "pre_ffn_g", "w_up", "conv_w", "conv_b", "w_down",
            "post_ffn_g")
_SMALL = _REPLICATED + ("hg_lb", "conv_w")
_ADAM_TRANSPOSED = ("w_in", "w_up")
PACK_W = 1024


def _small_plan(shapes):
    plan, r = [], 0
    for vi, (rows, cols) in enumerate(shapes):
        for i in range(rows):
            for c0 in range(0, cols, PACK_W):
                plan.append((vi, i, c0, min(PACK_W, cols - c0), r))
                r += 1
    return plan, -(-r // 8) * 8


def _pack_small(vals, name):
    plan, nrows = _small_plan([val.shape for val in vals])

    def body(*refs):
        ins, out = refs[:-1], refs[-1]
        out[...] = jnp.zeros_like(out)
        for vi, i, c0, width, r in plan:
            out[r:r + 1, 0:width] = ins[vi][i:i + 1, c0:c0 + width]

    return pl.pallas_call(body, name=name, out_shape=jax.ShapeDtypeStruct((nrows, PACK_W), F32))(*vals)


def _sum_unpack_small(packs, shapes, name):
    plan, _ = _small_plan(shapes)

    def body(*refs):
        p_ref, outs = refs[0], refs[1:]
        acc = p_ref[0]
        for i in range(1, N_DEV):
            acc = acc + p_ref[i]
        for vi, i, c0, width, r in plan:
            outs[vi][i:i + 1, c0:c0 + width] = acc[r:r + 1, 0:width]

    return pl.pallas_call(body, name=name, out_shape=[jax.ShapeDtypeStruct(s, F32) for s in shapes])(packs)


def _adamw_many(ws, gs, ms, vs):
    n = len(ws)

    def body(*refs):
        w_refs, g_refs, m_refs, v_refs = (refs[k * n:(k + 1) * n] for k in range(4))
        d_refs, nm_refs, nv_refs = (refs[(4 + k) * n:(5 + k) * n] for k in range(3))
        for k in range(n):
            g_ = g_refs[k][...]
            m_ = ADAM_B1 * m_refs[k][...] + (1.0 - ADAM_B1) * g_
            v_ = ADAM_B2 * v_refs[k][...] + (1.0 - ADAM_B2) * (g_ * g_)
            m_hat = m_ / (1.0 - ADAM_B1 ** ADAM_STEP)
            v_hat = v_ / (1.0 - ADAM_B2 ** ADAM_STEP)
            d_refs[k][...] = -ADAM_LR * (m_hat / (jnp.sqrt(v_hat) + ADAM_EPS) + ADAM_WD * w_refs[k][...])
            nm_refs[k][...] = m_
            nv_refs[k][...] = v_

    shapes = [jax.ShapeDtypeStruct(a.shape, F32) for a in ws]
    outs = pl.pallas_call(body, name="adamw_small", out_shape=shapes * 3)(*ws, *gs, *ms, *vs)
    return outs[:n], outs[n:2 * n], outs[2 * n:]


def kernel(x, mem, pre_mix_g, w_in, q_norm_g, k_norm_g, hg_lb, hg_out_norm_g, w_out, post_mix_g, pre_x_g, mem_norm_g, w_xq, w_xkv, w_xo, post_x_g, pre_ffn_g, w_up, conv_w, conv_b, w_down, post_ffn_g, loss_target, m_pre_mix_g, m_w_in, m_q_norm_g, m_k_norm_g, m_hg_lb, m_hg_out_norm_g, m_w_out, m_post_mix_g, m_pre_x_g, m_mem_norm_g, m_w_xq, m_w_xkv, m_w_xo, m_post_x_g, m_pre_ffn_g, m_w_up, m_conv_w, m_conv_b, m_w_down, m_post_ffn_g, v_pre_mix_g, v_w_in, v_q_norm_g, v_k_norm_g, v_hg_lb, v_hg_out_norm_g, v_w_out, v_post_mix_g, v_pre_x_g, v_mem_norm_g, v_w_xq, v_w_xkv, v_w_xo, v_post_x_g, v_pre_ffn_g, v_w_up, v_conv_w, v_conv_b, v_w_down, v_post_ffn_g):
    args = dict(locals())
    w = {n: args[n] for n in _WEIGHTS}
    m = {n: args["m_" + n] for n in _WEIGHTS}
    v = {n: args["v_" + n] for n in _WEIGHTS}
    me = _dev_index(*_mesh_pos())

    wire = {n: _bf(w[n][0].T) for n in _COL_SHARDED}
    wire.update({n: _bf(w[n][0]) for n in _ROW_SHARDED})

    grad_x, g, pending, small = _local_step(x[0], mem[0], loss_target[0], w, wire)

    grads, delta, new_m, new_v = {}, {}, {}, {}

    me_arr = jnp.reshape(me, (1,)).astype(jnp.int32)

    def update(n, parts):
        if n in _ADAM_TRANSPOSED:
            outs = _sum_adamw(*parts, me_arr, w[n][0].T, m[n][0].T, v[n][0].T, "update_" + n)
            grads[n], delta[n], new_m[n], new_v[n] = (a.T[None] for a in outs)
        elif n in _COL_SHARDED:
            gsum = _sum_parts(*parts, me_arr, "sum_" + n).T
            grads[n] = gsum[None]
            delta[n], new_m[n], new_v[n] = (a[None] for a in _adamw(w[n][0], gsum, m[n][0], v[n][0], "adamw_" + n))
        else:
            outs = _sum_adamw(*parts, me_arr, w[n][0], m[n][0], v[n][0], "update_" + n)
            grads[n], delta[n], new_m[n], new_v[n] = (a[None] for a in outs)

    after = grad_x
    for tag, names in (("down", ["w_down"]), ("up", ["w_up"]), ("x", ["w_xo", "w_xq", "w_xkv", "w_out"]),
                       ("in", ["w_in"])):
        for n, parts in zip(names, _exchange_end(pending[tag], after)):
            update(n, parts)
            after = new_v[n]

    (packs,) = _gather_end(small["fetch"], [0], after)
    summed = _sum_unpack_small(packs, small["shapes"], "sum_unpack_small")
    loss = summed[-1][0, 0]
    for n, s in zip(small["early"], summed[:-1]):
        grads[n] = s
    (packs,) = _gather_end(small["fetch_last"], [0], after)
    (grads["pre_mix_g"],) = _sum_unpack_small(packs, [w["pre_mix_g"].shape], "sum_unpack_last")
    small = _SMALL
    fold = lambda v2: v2[:, :ATT_HEAD_DIM] + v2[:, ATT_HEAD_DIM:]
    grads["q_norm_g"], grads["k_norm_g"] = fold(grads["q_norm_g"]), fold(grads["k_norm_g"])
    grads["hg_lb"] = lax.dynamic_slice_in_dim(grads["hg_lb"].reshape(2, 2, HG_DIM), me * (HG_DIM // N_DEV),
                                              HG_DIM // N_DEV, axis=2)
    grads["conv_w"] = lax.dynamic_slice_in_dim(grads["conv_w"], me * (2 * D_FF // N_DEV), 2 * D_FF // N_DEV, axis=1)[None]

    flat2 = lambda a: a.reshape(-1, a.shape[-1])
    outs = _adamw_many(*[[flat2(d[n]) for n in small] for d in (w, grads, m, v)])
    for dst, vals in zip((delta, new_m, new_v), outs):
        for n, val in zip(small, vals):
            dst[n] = val.reshape(w[n].shape)

    return (loss, grad_x[None], *[grads[n] for n in _WEIGHTS], *[delta[n] for n in _WEIGHTS],
            *[new_m[n] for n in _WEIGHTS], *[new_v[n] for n in _WEIGHTS])
```

```python
import jax
import jax.numpy as jnp
from jax import lax
from jax.experimental import pallas as pl
from jax.experimental.pallas import tpu as pltpu

F32 = jnp.float32
BF16 = jnp.bfloat16

D_MODEL = 1024
GRID_W = 64
EPS = 1e-6
ATT_HEADS = 8
ATT_HEAD_DIM = 64
ATT_Q_DIM = 512
ATT_KV_DIM = 128
ROPE_THETA = 10000.0
HG_HEADS = 4
HG_DIM = 512
HG_CHUNK = 32
HG_CHUNK_LOG2 = 5
HG_BLOCK_FWD = 256
HG_BLOCK_BWD = 128
N_IN = 3328
X_HEADS = 4
X_HEAD_DIM = 256
D_FF = 2816
N_DEV = 8
LANE = 128
ADAM_LR = 0.001
ADAM_B1 = 0.9
ADAM_B2 = 0.999
ADAM_EPS = 1e-08
ADAM_WD = 0.01
ADAM_STEP = 10
VMEM_LIMIT = 56 * 1024 * 1024

MESH_T = pl.DeviceIdType.MESH


def _params(**kw):
    return pltpu.CompilerParams(vmem_limit_bytes=VMEM_LIMIT, **kw)


def _dot(a, b, ca, cb):
    return lax.dot_general(a, b, (((ca,), (cb,)), ((), ())), preferred_element_type=F32)


def _bf(x):
    return x.astype(BF16)


def _sigmoid(x):
    return 1.0 / (1.0 + jnp.exp(-x))


def _rms_fwd(x, g):
    r = lax.rsqrt(jnp.mean(x * x, axis=-1, keepdims=True) + EPS)
    return x * r * g


def _rms_bwd(dy, x, g):
    r = lax.rsqrt(jnp.mean(x * x, axis=-1, keepdims=True) + EPS)
    xh = x * r
    dg = jnp.sum(dy * xh, axis=0, keepdims=True)
    t = dy * g
    dx = r * (t - xh * jnp.mean(t * xh, axis=-1, keepdims=True))
    return dx, dg


def _full(shape):
    nd = len(shape)
    return pl.BlockSpec(shape, lambda *a: (0,) * nd)


def _norm_mm(x, g, w, trans, tn, name, after=(), tm=512, p_dtype=F32):
    t, d = x.shape
    n = w.shape[0] if trans else w.shape[1]
    tm = min(tm, t)

    def body(x_ref, g_ref, w_ref, h_ref, p_ref):
        h = _bf(_rms_fwd(x_ref[...], g_ref[...]))
        h_ref[...] = h
        p_ref[...] = _dot(h, w_ref[...], 1, 1 if trans else 0).astype(p_dtype)

    w_spec = pl.BlockSpec((tn, d), lambda i, j: (j, 0)) if trans else pl.BlockSpec((d, tn), lambda i, j: (0, j))
    return _call(
        body, (x, g, w), name=name, grid=(t // tm, n // tn),
        in_specs=[pl.BlockSpec((tm, d), lambda i, j: (i, 0)), _full((1, d)), w_spec],
        out_specs=[pl.BlockSpec((tm, d), lambda i, j: (i, 0)), pl.BlockSpec((tm, tn), lambda i, j: (i, j))],
        out_shape=[jax.ShapeDtypeStruct((t, d), BF16), jax.ShapeDtypeStruct((t, n), p_dtype)], after=after)


def _mm_postnorm_res(a, w, g, res, name, tm=256):
    t, k = a.shape
    d = w.shape[1]

    def body(a_ref, w_ref, g_ref, res_ref, y_ref, o_ref):
        y = _dot(a_ref[...], w_ref[...], 1, 0)
        y_ref[...] = _bf(y)
        o_ref[...] = res_ref[...] + _rms_fwd(y, g_ref[...])

    row = lambda width: pl.BlockSpec((tm, width), lambda i: (i, 0))
    return pl.pallas_call(
        body, name=name, grid=(t // tm,),
        in_specs=[row(k), _full((k, d)), _full((1, d)), row(d)],
        out_specs=[row(d), row(d)],
        out_shape=[jax.ShapeDtypeStruct((t, d), BF16), jax.ShapeDtypeStruct((t, d), F32)],
        compiler_params=_params(),
    )(a, w, g, res)


def _mm_postnorm_res_loss(a, w, g, res, tgt, name, tm=256):
    t, k = a.shape
    d = w.shape[1]

    def body(a_ref, w_ref, g_ref, res_ref, tgt_ref, y_ref, dout_ref, loss_ref):
        @pl.when(pl.program_id(0) == 0)
        def _():
            loss_ref[...] = jnp.zeros_like(loss_ref)

        y = _dot(a_ref[...], w_ref[...], 1, 0)
        y_ref[...] = _bf(y)
        diff = res_ref[...] + _rms_fwd(y, g_ref[...]) - tgt_ref[...]
        dout_ref[...] = diff * (1.0 / d)
        part = jnp.sum(jnp.sum(diff * diff, axis=-1, keepdims=True), axis=0, keepdims=True)
        loss_ref[...] += (0.5 / d) * part

    row = lambda width: pl.BlockSpec((tm, width), lambda i: (i, 0))
    return pl.pallas_call(
        body, name=name, grid=(t // tm,),
        in_specs=[row(k), _full((k, d)), _full((1, d)), row(d), row(d)],
        out_specs=[row(d), row(d), _full((1, 1))],
        out_shape=[jax.ShapeDtypeStruct((t, d), BF16), jax.ShapeDtypeStruct((t, d), F32), jax.ShapeDtypeStruct((1, 1), F32)],
        compiler_params=_params(),
    )(a, w, g, res, tgt)


def _postnorm_bwd_mm(dout, y, g, w, name, da_dtype=F32, tm=256):
    t, d = y.shape
    k = w.shape[0]

    def body(dout_ref, y_ref, g_ref, w_ref, dy_ref, da_ref, dg_ref):
        @pl.when(pl.program_id(0) == 0)
        def _():
            dg_ref[...] = jnp.zeros_like(dg_ref)

        dy, dg = _rms_bwd(dout_ref[...], y_ref[...].astype(F32), g_ref[...])
        dg_ref[...] += dg
        dyb = _bf(dy)
        dy_ref[...] = dyb
        da_ref[...] = _dot(dyb, w_ref[...], 1, 1).astype(da_dtype)

    row = lambda width: pl.BlockSpec((tm, width), lambda i: (i, 0))
    return pl.pallas_call(
        body, name=name, grid=(t // tm,),
        in_specs=[row(d), row(d), _full((1, d)), _full((k, d))],
        out_specs=[row(d), row(k), _full((1, d))],
        out_shape=[jax.ShapeDtypeStruct((t, d), BF16), jax.ShapeDtypeStruct((t, k), da_dtype), jax.ShapeDtypeStruct((1, d), F32)],
        compiler_params=_params(),
    )(dout, y, g, w)


def _mm_prenorm_bwd(dp, w, trans, x, g, dres, name, after=(), tm=512):
    dps = list(dp) if isinstance(dp, (list, tuple)) else [dp]
    nparts = len(dps)
    t = dps[0].shape[0]
    widths = [part.shape[1] for part in dps]
    d = x.shape[1]
    tm = min(tm, t)

    def body(*refs):
        dp_refs = refs[:nparts]
        w_ref, x_ref, g_ref, dres_ref, dx_ref, dg_ref = refs[nparts:]

        @pl.when(pl.program_id(0) == 0)
        def _():
            dg_ref[...] = jnp.zeros_like(dg_ref)

        dh = None
        for k, dp_ref in enumerate(dp_refs):
            cols = slice(sum(widths[:k]), sum(widths[:k + 1]))
            term = _dot(dp_ref[...], w_ref[cols, :], 1, 0) if trans else _dot(dp_ref[...], w_ref[:, cols], 1, 1)
            dh = term if dh is None else dh + term
        dx, dg = _rms_bwd(dh, x_ref[...], g_ref[...])
        dg_ref[...] += dg
        dx_ref[...] = dres_ref[...] + dx

    row = lambda width: pl.BlockSpec((tm, width), lambda i: (i, 0))
    return _call(
        body, (*dps, w, x, g, dres), name=name, grid=(t // tm,),
        in_specs=[row(width) for width in widths] + [_full(w.shape), row(d), _full((1, d)), row(d)],
        out_specs=[row(d), _full((1, d))],
        out_shape=[jax.ShapeDtypeStruct((t, d), F32), jax.ShapeDtypeStruct((1, d), F32)], after=after)


def _dw(a, b, name, tka=256, after=()):
    parts = list(a) if isinstance(a, (list, tuple)) else [a]
    nparts = len(parts)
    t, nb = b.shape
    tiles = [part.shape[1] // tka for part in parts]
    first = [sum(tiles[:k]) for k in range(nparts)]

    def body(*refs):
        a_refs, b_ref, o_ref = refs[:nparts], refs[nparts], refs[-1]
        i = pl.program_id(0)
        for k, a_ref in enumerate(a_refs):
            @pl.when((i >= first[k]) & (i < first[k] + tiles[k]))
            def _():
                o_ref[...] = _bf(_dot(a_ref[...], b_ref[...], 0, 0))

    a_spec = lambda k: pl.BlockSpec((t, tka), lambda i: (0, jnp.clip(i - first[k], 0, tiles[k] - 1)))
    return pl.pallas_call(
        body, name=name, grid=(sum(tiles),),
        in_specs=[a_spec(k) for k in range(nparts)] + [_full((t, nb))] + [pl.BlockSpec(memory_space=pl.ANY)] * len(after),
        out_specs=pl.BlockSpec((tka, nb), lambda i: (i, 0)),
        out_shape=jax.ShapeDtypeStruct((sum(tiles) * tka, nb), BF16),
        compiler_params=_params(),
    )(*parts, b, *after)


def _rope_tables(t):
    pos = jnp.arange(t)
    r = (pos // GRID_W).astype(F32)
    c = (pos % GRID_W).astype(F32)
    npair = ATT_HEAD_DIM // 4
    inv = jnp.power(ROPE_THETA, -jnp.arange(npair, dtype=F32) / npair)
    ang = jnp.concatenate([r[:, None] * inv, c[:, None] * inv], axis=-1)
    cos = jnp.repeat(jnp.cos(ang), 2, axis=-1)
    sin = jnp.repeat(jnp.sin(ang), 2, axis=-1)
    even = (jnp.arange(ATT_HEAD_DIM) % 2) == 0
    sa = jnp.where(even, -sin, 0.0)
    sb = jnp.where(even, 0.0, sin)
    two = lambda a: jnp.tile(a, (1, 2))
    return two(cos), two(sa), two(sb)


def _head_sum_matrix():
    a = jnp.arange(LANE) // ATT_HEAD_DIM
    return (a[:, None] == a[None, :]).astype(BF16)


def _head_mean(v, bd):
    hi = _bf(v)
    lo = _bf(v - hi.astype(F32))
    return (_dot(hi, bd, 1, 0) + _dot(lo, bd, 1, 0)) * (1.0 / ATT_HEAD_DIM)


def _qk_prep(p, gq, gk, tables, bd, tm=512):
    t = p.shape[0]
    tm = min(tm, t)
    cc, sa, sb = tables

    def body(p_ref, gq_ref, gk_ref, cc_ref, sa_ref, sb_ref, bd_ref, q_ref, k_ref):
        cc_, sa_, sb_, bd_ = cc_ref[...], sa_ref[...], sb_ref[...], bd_ref[...]
        low = lax.broadcasted_iota(jnp.int32, (tm, LANE), 1) < ATT_HEAD_DIM

        def normrope(xs, g):
            xn = xs * lax.rsqrt(_head_mean(xs * xs, bd_) + EPS) * g
            return xn * cc_ + pltpu.roll(xn, LANE - 1, 1) * sa_ + pltpu.roll(xn, 1, 1) * sb_

        for j in range(4):
            y = normrope(p_ref[:, j * LANE:(j + 1) * LANE].astype(F32), gq_ref[...]) * (ATT_HEAD_DIM ** -0.5)
            yr = pltpu.roll(y, ATT_HEAD_DIM, 1)
            if j // 2 == 0:
                h0, h1 = jnp.where(low, y, 0.0), jnp.where(low, yr, 0.0)
            else:
                h0, h1 = jnp.where(low, 0.0, yr), jnp.where(low, 0.0, y)
            q_ref[:, (2 * j) * LANE:(2 * j + 1) * LANE] = _bf(h0)
            q_ref[:, (2 * j + 1) * LANE:(2 * j + 2) * LANE] = _bf(h1)
        k_ref[...] = _bf(normrope(p_ref[:, ATT_Q_DIM:ATT_Q_DIM + LANE].astype(F32), gk_ref[...]))

    row = lambda width: pl.BlockSpec((tm, width), lambda i: (i, 0))
    return pl.pallas_call(
        body, name="qk_prep", grid=(t // tm,),
        in_specs=[row(ATT_Q_DIM + LANE), _full((1, LANE)), _full((1, LANE)), row(LANE), row(LANE), row(LANE),
                  _full((LANE, LANE))],
        out_specs=[row(ATT_HEADS * LANE), row(LANE)],
        out_shape=[jax.ShapeDtypeStruct((t, ATT_HEADS * LANE), BF16), jax.ShapeDtypeStruct((t, LANE), BF16)],
        compiler_params=_params(),
    )(p, gq, gk, cc, sa, sb, bd)


def _qk_prep_bwd(p, dq, dk, dv, gq, gk, tables, bd, tm=512):
    t = p.shape[0]
    tm = min(tm, t)
    cc, sa, sb = tables
    att_w = ATT_Q_DIM + 2 * ATT_KV_DIM

    def body(p_ref, dq_ref, dk_ref, dv_ref, gq_ref, gk_ref, cc_ref, sa_ref, sb_ref, bd_ref, dp_ref, dgq_ref, dgk_ref):
        dp_ref[:, ATT_Q_DIM + LANE:att_w] = _bf(dv_ref[...])

        @pl.when(pl.program_id(0) == 0)
        def _():
            dgq_ref[...] = jnp.zeros_like(dgq_ref)
            dgk_ref[...] = jnp.zeros_like(dgk_ref)

        cc_, sa_, sb_, bd_ = cc_ref[...], sa_ref[...], sb_ref[...], bd_ref[...]
        low = lax.broadcasted_iota(jnp.int32, (tm, LANE), 1) < ATT_HEAD_DIM

        def bwd(xs, g, dy):
            r = lax.rsqrt(_head_mean(xs * xs, bd_) + EPS)
            xh = xs * r
            dxn = dy * cc_ + pltpu.roll(dy * sa_, 1, 1) + pltpu.roll(dy * sb_, LANE - 1, 1)
            dg = jnp.sum(dxn * xh, axis=0, keepdims=True)
            tt = dxn * g
            return r * (tt - xh * _head_mean(tt * xh, bd_)), dg

        dgq = jnp.zeros((1, LANE), F32)
        for j in range(4):
            d0 = dq_ref[:, (2 * j) * LANE:(2 * j + 1) * LANE]
            d1 = dq_ref[:, (2 * j + 1) * LANE:(2 * j + 2) * LANE]
            if j // 2 == 0:
                dy = jnp.where(low, d0, pltpu.roll(d1, ATT_HEAD_DIM, 1))
            else:
                dy = jnp.where(low, pltpu.roll(d0, ATT_HEAD_DIM, 1), d1)
            dx, dg = bwd(p_ref[:, j * LANE:(j + 1) * LANE].astype(F32), gq_ref[...], dy * (ATT_HEAD_DIM ** -0.5))
            dp_ref[:, j * LANE:(j + 1) * LANE] = _bf(dx)
            dgq = dgq + dg
        dgq_ref[...] += dgq
        dx, dg = bwd(p_ref[:, ATT_Q_DIM:ATT_Q_DIM + LANE].astype(F32), gk_ref[...], dk_ref[...])
        dp_ref[:, ATT_Q_DIM:ATT_Q_DIM + LANE] = _bf(dx)
        dgk_ref[...] += dg

    row = lambda width: pl.BlockSpec((tm, width), lambda i: (i, 0))
    return pl.pallas_call(
        body, name="qk_prep_bwd", grid=(t // tm,),
        in_specs=[row(ATT_Q_DIM + LANE), row(ATT_HEADS * LANE), row(LANE), row(LANE), _full((1, LANE)), _full((1, LANE)),
                  row(LANE), row(LANE), row(LANE), _full((LANE, LANE))],
        out_specs=[row(att_w), _full((1, LANE)), _full((1, LANE))],
        out_shape=[jax.ShapeDtypeStruct((t, att_w), BF16), jax.ShapeDtypeStruct((1, LANE), F32),
                   jax.ShapeDtypeStruct((1, LANE), F32)],
        compiler_params=_params(),
    )(p, dq, dk, dv, gq, gk, cc, sa, sb, bd)


def _attn_fwd(q, k, p, tq=256):
    t = k.shape[0]
    tq = min(tq, t)
    v_blk = (ATT_Q_DIM + ATT_KV_DIM) // LANE

    def body(q_ref, k_ref, v_ref, o_ref, o32_ref, lse_ref):
        k_ = k_ref[...]
        v = v_ref[...].astype(F32)
        lane_k =lax.broadcasted_iota(jnp.int32, (t, LANE), 1)
        lane_q = lax.broadcasted_iota(jnp.int32, (tq, LANE), 1)
        lowk, lowq = lane_k < ATT_HEAD_DIM, lane_q < ATT_HEAD_DIM
        ones_lane = (ATT_HEAD_DIM, 0)
        vm = (_bf(jnp.where(lowk, v, jnp.where(lane_k == ones_lane[0], 1.0, 0.0))),
              _bf(jnp.where(lowk, jnp.where(lane_k == ones_lane[1], 1.0, 0.0), v)))
        for j in range(4):
            kvh = j // 2
            acc = None
            for sub in range(2):
                h = 2 * j + sub
                s = _dot(q_ref[:, h * LANE:(h + 1) * LANE], k_, 1, 1)
                mx = jnp.max(s, axis=-1, keepdims=True)
                ov = _dot(jnp.exp(_bf(s - mx)), vm[kvh], 1, 0)
                l = jnp.sum(jnp.where(lane_q == ones_lane[kvh], ov, 0.0), axis=-1, keepdims=True)
                lse_ref[h] = mx + jnp.log(l)
                o = jnp.where(lowq if kvh == 0 else ~lowq, ov, 0.0) * (1.0 / l)
                if sub != kvh:
                    o = pltpu.roll(o, ATT_HEAD_DIM, 1)
                acc = o if acc is None else acc + o
            o32_ref[:, j * LANE:(j + 1) * LANE] = acc
            o_ref[:, j * LANE:(j + 1) * LANE] = _bf(acc)

    row = pl.BlockSpec((tq, ATT_Q_DIM), lambda i: (i, 0))
    return _call(
        body, (q, k, p), name="attn_fwd", grid=(t // tq,),
        in_specs=[pl.BlockSpec((tq, ATT_HEADS * LANE), lambda i: (i, 0)), _full((t, LANE)),
                  pl.BlockSpec((t, LANE), lambda i: (0, v_blk))],
        out_specs=[row, row, pl.BlockSpec((ATT_HEADS, tq, 1), lambda i: (0, i, 0))],
        out_shape=[jax.ShapeDtypeStruct((t, ATT_Q_DIM), BF16), jax.ShapeDtypeStruct((t, ATT_Q_DIM), F32),
                   jax.ShapeDtypeStruct((ATT_HEADS, t, 1), F32)])


def _attn_bwd(q, k, p, dcat, o32, lse, tq=256):
    t = k.shape[0]
    tq = min(tq, t)
    v_blk = (ATT_Q_DIM + ATT_KV_DIM) // LANE

    def body(q_ref, k_ref, v_ref, do_ref, o_ref, lse_ref, dq_ref, dk_ref, dv_ref):
        @pl.when(pl.program_id(0) == 0)
        def _():
            dk_ref[...] = jnp.zeros_like(dk_ref)
            dv_ref[...] = jnp.zeros_like(dv_ref)

        k_ = k_ref[...]
        vb = _bf(v_ref[...])
        lowq = lax.broadcasted_iota(jnp.int32, (tq, LANE), 1) < ATT_HEAD_DIM
        dk_acc = jnp.zeros((t, LANE), F32)
        dv_acc = jnp.zeros((t, LANE), F32)
        for j in range(4):
            kvh = j // 2
            dop = do_ref[:, j * LANE:(j + 1) * LANE].astype(F32)
            prod = dop * o_ref[:, j * LANE:(j + 1) * LANE]
            d_low = jnp.sum(jnp.where(lowq, prod, 0.0), axis=-1, keepdims=True)
            d_sub = (d_low, jnp.sum(prod, axis=-1, keepdims=True) - d_low)
            for sub in range(2):
                h = 2 * j + sub
                src = dop if sub == kvh else pltpu.roll(dop, ATT_HEAD_DIM, 1)
                do_h = _bf(jnp.where(lowq, src, 0.0) if kvh == 0 else jnp.where(lowq, 0.0, src))
                qh = q_ref[:, h * LANE:(h + 1) * LANE]
                pr = jnp.exp(_bf(_dot(qh, k_, 1, 1) - lse_ref[h]))
                ds = pr * _bf(_dot(do_h, vb, 1, 1) - d_sub[sub])
                dq_ref[:, h * LANE:(h + 1) * LANE] = _dot(ds, k_, 1, 0)
                dk_acc = dk_acc + _dot(ds, qh, 0, 0)
                dv_acc = dv_acc + _dot(pr, do_h, 0, 0)
        dk_ref[...] += dk_acc
        dv_ref[...] += dv_acc

    row = pl.BlockSpec((tq, ATT_Q_DIM), lambda i: (i, 0))
    return _call(
        body, (q, k, p, dcat, o32, lse), name="attn_bwd", grid=(t // tq,),
        in_specs=[pl.BlockSpec((tq, ATT_HEADS * LANE), lambda i: (i, 0)), _full((t, LANE)),
                  pl.BlockSpec((t, LANE), lambda i: (0, v_blk)), row, row,
                  pl.BlockSpec((ATT_HEADS, tq, 1), lambda i: (0, i, 0))],
        out_specs=[pl.BlockSpec((tq, ATT_HEADS * LANE), lambda i: (i, 0)), _full((t, LANE)), _full((t, LANE))],
        out_shape=[jax.ShapeDtypeStruct((t, ATT_HEADS * LANE), F32), jax.ShapeDtypeStruct((t, LANE), F32),
                   jax.ShapeDtypeStruct((t, LANE), F32)])


def _lower_bounds(a0, a1):
    def body(a0_ref, a1_ref, lb_ref):
        m = jnp.maximum(a0_ref[...], a1_ref[...])
        e0, e1 = jnp.exp(a0_ref[...] - m), jnp.exp(a1_ref[...] - m)
        lb_ref[...] = e0 / (e0 + e1)

    return pl.pallas_call(body, name="lower_bounds", out_shape=jax.ShapeDtypeStruct(a0.shape, F32))(a0, a1)


def _lower_bounds_bwd(a0, a1, dlb):
    def body(a0_ref, a1_ref, dlb_ref, d0_ref, d1_ref):
        m = jnp.maximum(a0_ref[...], a1_ref[...])
        e0, e1 = jnp.exp(a0_ref[...] - m), jnp.exp(a1_ref[...] - m)
        lb = e0 / (e0 + e1)
        d0 = dlb_ref[...] * lb * (1.0 - lb)
        d0_ref[...] = d0
        d1_ref[...] = -d0

    return pl.pallas_call(body, name="lower_bounds_bwd", out_shape=[jax.ShapeDtypeStruct(a0.shape, F32)] * 2)(a0, a1, dlb)


def _chunk_scan(x, pos, down):
    n = x.shape[0]
    s = 1
    while s < HG_CHUNK:
        if down:
            x = x + jnp.where(pos >= s, pltpu.roll(x, s, 0), 0.0)
        else:
            x = x + jnp.where(pos < HG_CHUNK - s, pltpu.roll(x, n - s, 0), 0.0)
        s *= 2
    return x


def _chunk_concat(x, nc):
    xb = _bf(x)
    rows = []
    for c in range(nc):
        pieces = [xb[c * HG_CHUNK:(c + 1) * HG_CHUNK, :]]
        if c:
            pieces.insert(0, jnp.zeros((HG_CHUNK, c * LANE), BF16))
        if c < nc - 1:
            pieces.append(jnp.zeros((HG_CHUNK, (nc - 1 - c) * LANE), BF16))
        rows.append(jnp.concatenate(pieces, axis=1) if len(pieces) > 1 else pieces[0])
    return jnp.concatenate(rows, axis=0) if nc > 1 else rows[0]


def _chunk_pick(x, nc):
    rows = [x[c * HG_CHUNK:(c + 1) * HG_CHUNK, c * LANE:(c + 1) * LANE] for c in range(nc)]
    return jnp.concatenate(rows, axis=0) if nc > 1 else rows[0]


def _chunk_rows(vals):
    rows = [jnp.broadcast_to(v, (HG_CHUNK, LANE)) for v in vals]
    return jnp.concatenate(rows, axis=0) if len(rows) > 1 else rows[0]


def _hgrn_gates(hq, z, lb):
    sq = _sigmoid(hq)
    sig = _sigmoid(z)
    f = lb + (1.0 - lb) * sig
    return hq * sq, sq, sig, f, jnp.log(f)


def _hgrn_local(q, f, g, v, pos, amask, rev, nc):
    k = 1.0 - f
    ba = _chunk_scan(g, pos, not rev)
    bb = _chunk_scan(g, pos, rev)
    eq = 0.5 * (ba - bb + g)
    ex_q, ex_k, ex_i, ex_d = jnp.exp(eq), jnp.exp(-eq), jnp.exp(ba), jnp.exp(bb - g)
    qb, kb, qi, kd = q * ex_q, k * ex_k, q * ex_i, k * ex_d
    dvec = jnp.exp(ba + bb - g)
    a = jnp.where(amask, _dot(_bf(qb), _bf(kb), 1, 1), 0.0)
    kd_m = _chunk_concat(kd, nc)
    qi_m = _chunk_concat(qi, nc)
    ut_all = _dot(_bf(v), kd_m, 0, 0)
    return dict(k=k, ex_q=ex_q, ex_k=ex_k, ex_i=ex_i, ex_d=ex_d, qb=qb, kb=kb, qi=qi, kd=kd, dvec=dvec, a=a,
                kd_m=kd_m, qi_m=qi_m, ut_all=ut_all)


def _hgrn_masks(n, rev):
    row = lax.broadcasted_iota(jnp.int32, (n, LANE), 0)
    ti = lax.broadcasted_iota(jnp.int32, (n, n), 0)
    si = lax.broadcasted_iota(jnp.int32, (n, n), 1)
    tri = (si >= ti) if rev else (si <= ti)
    same = jnp.right_shift(ti, HG_CHUNK_LOG2) == jnp.right_shift(si, HG_CHUNK_LOG2)
    return jnp.bitwise_and(row, HG_CHUNK - 1), same & tri


def _hgrn_specs(t, rev, bwd):
    n = min(HG_BLOCK_BWD if bwd else HG_BLOCK_FWD, t)
    nb = t // n
    if rev != bwd:
        blk = lambda i: nb - 1 - i
    else:
        blk = lambda i: i
    col = lambda base: pl.BlockSpec((n, 2 * LANE), lambda hp, i: (blk(i), base + hp))
    return n, nb, blk, col


def _hgrn_fwd(p, lb, rev, after=()):
    t = p.shape[0]
    n, nb, blk, col = _hgrn_specs(t, rev, False)
    nc = n // HG_CHUNK
    sub = n // min(HG_BLOCK_BWD, t)
    z_base = 7 if rev else 5

    def body(hq_ref, z_ref, hi_ref, lb_ref, o_ref, ssave_ref, st_scr):
        @pl.when(pl.program_id(1) == 0)
        def _():
            st_scr[...] = jnp.zeros_like(st_scr)

        pos, amask = _hgrn_masks(n, rev)
        order = list(range(nc))[::-1] if rev else list(range(nc))
        for hh in range(2):
            sl = slice(hh * LANE, (hh + 1) * LANE)
            q, _, _, f, g = _hgrn_gates(hq_ref[:, sl].astype(F32), z_ref[:, sl].astype(F32), lb_ref[0:1, sl])
            v = hi_ref[:, sl].astype(F32)
            c_ = _hgrn_local(q, f, g, v, pos, amask, rev, nc)
            st = st_scr[hh]
            cols = [None] * nc
            for c in order:
                cols[c] = st
                st = st * c_["dvec"][c * HG_CHUNK:c * HG_CHUNK + 1, :] + c_["ut_all"][:, c * LANE:(c + 1) * LANE]
            st_scr[hh] = st
            for s in range(sub):
                ssave_ref[sub - 1 - s if rev else s, sl, :] = cols[order[s * (nc // sub)]]
            st_all = _bf(jnp.concatenate(cols, axis=1))
            o_ref[:, sl] = _dot(_bf(c_["a"]), _bf(v), 1, 0) + _dot(c_["qi_m"], st_all, 1, 1)

    return _call(
        body, (p, p, p, lb), name="hgrn_fwd_rev" if rev else "hgrn_fwd", grid=(2, nb),
        in_specs=[col(3), col(z_base), col(9), pl.BlockSpec((1, 2 * LANE), lambda hp, i: (0, hp))],
        out_specs=[pl.BlockSpec((n, 2 * LANE), lambda hp, i: (blk(i), hp)),
                   pl.BlockSpec((sub, 2 * LANE, LANE), lambda hp, i: (blk(i), hp, 0))],
        out_shape=[jax.ShapeDtypeStruct((t, HG_DIM), F32), jax.ShapeDtypeStruct((nb * sub, HG_DIM, LANE), F32)],
        scratch_shapes=[pltpu.VMEM((2, LANE, LANE), F32)], after=after)


def _hgrn_bwd(p, lb, do, ssave, rev, other=None, after=()):
    t = p.shape[0]
    n, nb, blk, col = _hgrn_specs(t, rev, True)
    nc = n // HG_CHUNK
    z_base = 7 if rev else 5
    n_other = 0 if other is None else 2

    def body(*refs):
        hq_ref, z_ref, hi_ref, lb_ref, do_ref, ssave_ref = refs[:6]
        dhq_ref, dz_ref, dhi_ref, dlb_ref, dst_scr = refs[6 + n_other:]

        @pl.when(pl.program_id(1) == 0)
        def _():
            dst_scr[...] = jnp.zeros_like(dst_scr)
            dlb_ref[...] = jnp.zeros_like(dlb_ref)

        pos, amask = _hgrn_masks(n, rev)
        order = list(range(nc))[::-1] if rev else list(range(nc))
        for hh in range(2):
            sl = slice(hh * LANE, (hh + 1) * LANE)
            hq, lbv = hq_ref[:, sl].astype(F32), lb_ref[0:1, sl]
            q, sq, sig, f, g = _hgrn_gates(hq, z_ref[:, sl].astype(F32), lbv)
            v = hi_ref[:, sl].astype(F32)
            c_ = _hgrn_local(q, f, g, v, pos, amask, rev, nc)
            dvec, ut_all = c_["dvec"], c_["ut_all"]
            drow = lambda c: dvec[c * HG_CHUNK:c * HG_CHUNK + 1, :]
            st = ssave_ref[0, sl, :]
            cols = [None] * nc
            for c in order:
                cols[c] = st
                st = st * drow(c) + ut_all[:, c * LANE:(c + 1) * LANE]
            dob, vb = _bf(do_ref[:, sl]), _bf(v)
            gt_all = _dot(dob, c_["qi_m"], 0, 0)
            dcur = dst_scr[hh]
            dnext = [None] * nc
            ddrow = [None] * nc
            for c in order[::-1]:
                dnext[c] = dcur
                ddrow[c] = jnp.sum(cols[c] * dcur, axis=0, keepdims=True) * drow(c)
                dcur = dcur * drow(c) + gt_all[:, c * LANE:(c + 1) * LANE]
            dst_scr[hh] = dcur
            dsn_all = _bf(jnp.concatenate(dnext, axis=1))
            st_all = _bf(jnp.concatenate(cols, axis=1))
            da = _bf(jnp.where(amask, _dot(dob, vb, 1, 1), 0.0))
            dv = _dot(_bf(c_["a"]), dob, 0, 0) + _dot(c_["kd_m"], dsn_all, 1, 1)
            dqb = _dot(da, _bf(c_["kb"]), 1, 0)
            dkb = _dot(da, _bf(c_["qb"]), 0, 0)
            dqi = _chunk_pick(_dot(dob, st_all, 1, 0), nc)
            dkd = _chunk_pick(_dot(vb, dsn_all, 1, 0), nc)
            dq = dqb * c_["ex_q"] + dqi * c_["ex_i"]
            dk = dkb * c_["ex_k"] + dkd * c_["ex_d"]
            e = dqb * c_["qb"] - dkb * c_["kb"] + dqi * c_["qi"]
            w = dkd * c_["kd"]
            dg = _chunk_scan(e, pos, rev) + (_chunk_scan(w, pos, not rev) - w) + _chunk_rows(ddrow)
            df = dg / f - dk
            dz_ref[:, sl] = _bf(df * (1.0 - lbv) * sig * (1.0 - sig))
            dlb_ref[0:1, sl] += jnp.sum(df * (1.0 - sig), axis=0, keepdims=True)
            dhq = dq * (sq * (1.0 + hq * (1.0 - sq)))
            if other is None:
                dhq_ref[:, sl], dhi_ref[:, sl] = dhq, dv
            else:
                dhq_ref[:, sl], dhi_ref[:, sl] = _bf(dhq + refs[6][:, sl]), _bf(dv + refs[7][:, sl])

    out = pl.BlockSpec((n, 2 * LANE), lambda hp, i: (blk(i), hp))
    sum_dtype = F32 if other is None else BF16
    return _call(
        body, (p, p, p, lb, do, ssave, *(other or ())), name="hgrn_bwd_rev" if rev else "hgrn_bwd", grid=(2, nb),
        in_specs=[col(3), col(z_base), col(9), pl.BlockSpec((1, 2 * LANE), lambda hp, i: (0, hp)), out,
                  pl.BlockSpec((1, 2 * LANE, LANE), lambda hp, i: (blk(i), hp, 0))] + [out] * n_other,
        out_specs=[out, out, out, pl.BlockSpec((1, 2 * LANE), lambda hp, i: (0, hp))],
        out_shape=[jax.ShapeDtypeStruct((t, HG_DIM), sum_dtype), jax.ShapeDtypeStruct((t, HG_DIM), BF16),
                   jax.ShapeDtypeStruct((t, HG_DIM), sum_dtype), jax.ShapeDtypeStruct((1, HG_DIM), F32)],
        scratch_shapes=[pltpu.VMEM((2, LANE, LANE), F32)], after=after)


def _mix_out(att, of, ob, p, gout, w_out, g_post, x, after=(), tm=256):
    t, d = x.shape
    tm = min(tm, t)

    def body(att_ref, of_ref, ob_ref, hg0_ref, hg1_ref, gout_ref, w_ref, g_ref, x_ref, cat_ref, y_ref, x1_ref):
        cat_ref[:, :ATT_Q_DIM] = att_ref[...]
        for h in range(HG_HEADS):
            sl = slice(h * LANE, (h + 1) * LANE)
            hg_ref = hg0_ref if h < 2 else hg1_ref
            hg = hg_ref[:, (h % 2) * LANE:(h % 2 + 1) * LANE].astype(F32)
            nrm = _rms_fwd(of_ref[:, sl] + ob_ref[:, sl], gout_ref[...])
            cat_ref[:, ATT_Q_DIM + h * LANE:ATT_Q_DIM + (h + 1) * LANE] = _bf(nrm * hg * _sigmoid(hg))
        y = _dot(cat_ref[...], w_ref[...], 1, 0)
        y_ref[...] = _bf(y)
        x1_ref[...] = x_ref[...] + _rms_fwd(y, g_ref[...])

    row = lambda width: pl.BlockSpec((tm, width), lambda i: (i, 0))
    return _call(
        body, (att, of, ob, p, p, gout, w_out, g_post, x), name="mix_out", grid=(t // tm,),
        in_specs=[row(ATT_Q_DIM), row(HG_DIM), row(HG_DIM), pl.BlockSpec((tm, 2 * LANE), lambda i: (i, 11)),
                  pl.BlockSpec((tm, 2 * LANE), lambda i: (i, 12)), _full((1, LANE)), _full((d, d)), _full((1, d)), row(d)],
        out_specs=[row(d), row(d), row(d)],
        out_shape=[jax.ShapeDtypeStruct((t, d), BF16), jax.ShapeDtypeStruct((t, d), BF16), jax.ShapeDtypeStruct((t, d), F32)],
        after=after)


def _rec_bwd(dcat, of, ob, p, gout, tm=256):
    t = of.shape[0]
    tm = min(tm, t)

    def body(dc_ref, of_ref, ob_ref, hg0_ref, hg1_ref, gout_ref, do_ref, dhg_ref, dgo_ref):
        @pl.when(pl.program_id(0) == 0)
        def _():
            dgo_ref[...] = jnp.zeros_like(dgo_ref)

        dgo = jnp.zeros((1, LANE), F32)
        for h in range(HG_HEADS):
            sl = slice(h * LANE, (h + 1) * LANE)
            hg_ref = hg0_ref if h < 2 else hg1_ref
            hg = hg_ref[:, (h % 2) * LANE:(h % 2 + 1) * LANE].astype(F32)
            o = of_ref[:, sl] + ob_ref[:, sl]
            sg = _sigmoid(hg)
            drec = dc_ref[:, sl].astype(F32)
            dhg_ref[:, sl] = _bf(drec * _rms_fwd(o, gout_ref[...]) * (sg * (1.0 + hg * (1.0 - sg))))
            do, dg = _rms_bwd(drec * hg * sg, o, gout_ref[...])
            do_ref[:, sl] = do
            dgo = dgo + dg
        dgo_ref[...] += dgo

    row = lambda width: pl.BlockSpec((tm, width), lambda i: (i, 0))
    return pl.pallas_call(
        body, name="rec_bwd", grid=(t // tm,),
        in_specs=[pl.BlockSpec((tm, HG_DIM), lambda i: (i, 1)), row(HG_DIM), row(HG_DIM),
                  pl.BlockSpec((tm, 2 * LANE), lambda i: (i, 11)), pl.BlockSpec((tm, 2 * LANE), lambda i: (i, 12)),
                  _full((1, LANE))],
        out_specs=[row(HG_DIM), row(HG_DIM), _full((1, LANE))],
        out_shape=[jax.ShapeDtypeStruct((t, HG_DIM), F32), jax.ShapeDtypeStruct((t, HG_DIM), BF16),
                   jax.ShapeDtypeStruct((1, LANE), F32)],
        compiler_params=_params(),
    )(dcat, of, ob, p, p, gout)


def _xattn_fwd(q, kv, tq=512):
    t, d = q.shape
    tq = min(tq, t)
    nm = kv.shape[0]

    def body(q_ref, kv_ref, o_ref):
        for h in range(X_HEADS):
            sl = slice(h * X_HEAD_DIM, (h + 1) * X_HEAD_DIM)
            s = _dot(_bf(q_ref[:, sl]), _bf(kv_ref[:, sl]), 1, 1) * (X_HEAD_DIM ** -0.5)
            e = jnp.exp(s - jnp.max(s, axis=-1, keepdims=True))
            pr = e * (1.0 / jnp.sum(e, axis=-1, keepdims=True))
            o_ref[:, sl] = _bf(_dot(_bf(pr), _bf(kv_ref[:, d + h * X_HEAD_DIM:d + (h + 1) * X_HEAD_DIM]), 1, 0))

    return pl.pallas_call(
        body, name="xattn_fwd", grid=(t // tq,),
        in_specs=[pl.BlockSpec((tq, d), lambda i: (i, 0)), _full((nm, 2 * d))],
        out_specs=pl.BlockSpec((tq, d), lambda i: (i, 0)),
        out_shape=jax.ShapeDtypeStruct((t, d), BF16),
        compiler_params=_params(),
    )(q, kv)


def _xattn_bwd(q, kv, do, tq=512):
    t, d = q.shape
    tq = min(tq, t)
    nm = kv.shape[0]

    def body(q_ref, kv_ref, do_ref, dq_ref, dkv_ref):
        @pl.when(pl.program_id(0) == 0)
        def _():
            dkv_ref[...] = jnp.zeros_like(dkv_ref)

        for h in range(X_HEADS):
            sl = slice(h * X_HEAD_DIM, (h + 1) * X_HEAD_DIM)
            slv = slice(d + h * X_HEAD_DIM, d + (h + 1) * X_HEAD_DIM)
            qb, kb, vb, dob = _bf(q_ref[:, sl]), _bf(kv_ref[:, sl]), _bf(kv_ref[:, slv]), _bf(do_ref[:, sl])
            s = _dot(qb, kb, 1, 1) * (X_HEAD_DIM ** -0.5)
            e = jnp.exp(s - jnp.max(s, axis=-1, keepdims=True))
            pr = e * (1.0 / jnp.sum(e, axis=-1, keepdims=True))
            dpr = _dot(dob, vb, 1, 1)
            ds = _bf(pr * (dpr - jnp.sum(pr * dpr, axis=-1, keepdims=True)) * (X_HEAD_DIM ** -0.5))
            dq_ref[:, sl] = _bf(_dot(ds, kb, 1, 0))
            dkv_ref[:, sl] += _dot(ds, qb, 0, 0)
            dkv_ref[:, slv] += _dot(_bf(pr), dob, 0, 0)

    return pl.pallas_call(
        body, name="xattn_bwd", grid=(t // tq,),
        in_specs=[pl.BlockSpec((tq, d), lambda i: (i, 0)), _full((nm, 2 * d)), pl.BlockSpec((tq, d), lambda i: (i, 0))],
        out_specs=[pl.BlockSpec((tq, d), lambda i: (i, 0)), _full((nm, 2 * d))],
        out_shape=[jax.ShapeDtypeStruct((t, d), BF16), jax.ShapeDtypeStruct((nm, 2 * d), F32)],
        compiler_params=_params(),
    )(q, kv, do)


CONV_TN = 256


def _shift_rows(u, row, t, delta):
    if delta < 0:
        return jnp.where(row == 0, 0.0, pltpu.roll(u, 1, 0))
    return jnp.where(row == t - 1, 0.0, pltpu.roll(u, t - 1, 0))


def _conv_gate_fwd(u, conv_w, conv_b):
    t = u.shape[0]
    nt = D_FF // CONV_TN

    def body(ug_ref, uv_ref, wg_ref, wv_ref, bg_ref, bv_ref, a_ref):
        row = lax.broadcasted_iota(jnp.int32, (t, CONV_TN), 0)

        def conv(u_ref, w_ref, b_ref):
            uu = u_ref[...].astype(F32)
            return (b_ref[...] + _shift_rows(uu, row, t, -1) * w_ref[0:1, :] + uu * w_ref[1:2, :]
                    + _shift_rows(uu, row, t, 1) * w_ref[2:3, :])

        gate = conv(ug_ref, wg_ref, bg_ref)
        a_ref[...] = _bf(gate * _sigmoid(gate) * conv(uv_ref, wv_ref, bv_ref))

    col = lambda rows, off: pl.BlockSpec((rows, CONV_TN), lambda j: (0, j + off))
    return pl.pallas_call(
        body, name="conv_gate_fwd", grid=(nt,),
        in_specs=[col(t, 0), col(t, nt), col(3, 0), col(3, nt), col(1, 0), col(1, nt)],
        out_specs=col(t, 0),
        out_shape=jax.ShapeDtypeStruct((t, D_FF), BF16),
        compiler_params=_params(),
    )(u, u, conv_w, conv_w, conv_b, conv_b)


def _conv_gate_bwd(u, conv_w, conv_b, da, after=()):
    t = u.shape[0]
    nt = D_FF // CONV_TN

    def body(ug_ref, uv_ref, wg_ref, wv_ref, bg_ref, bv_ref, da_ref, dug_ref, dwg_ref, dbg_ref, duv_ref, dwv_ref, dbv_ref):
        row = lax.broadcasted_iota(jnp.int32, (t, CONV_TN), 0)
        ug, uv = ug_ref[...].astype(F32), uv_ref[...].astype(F32)
        ug_m, ug_p = _shift_rows(ug, row, t, -1), _shift_rows(ug, row, t, 1)
        uv_m, uv_p = _shift_rows(uv, row, t, -1), _shift_rows(uv, row, t, 1)
        gate = bg_ref[...] + ug_m * wg_ref[0:1, :] + ug * wg_ref[1:2, :] + ug_p * wg_ref[2:3, :]
        val = bv_ref[...] + uv_m * wv_ref[0:1, :] + uv * wv_ref[1:2, :] + uv_p * wv_ref[2:3, :]
        sg = _sigmoid(gate)
        da_ = da_ref[...].astype(F32)

        ones = jnp.ones((8, t), BF16)

        def emit(dacc, um, uu, up, w_ref, du_ref, dw_ref, db_ref):
            du_ref[...] = _bf(_shift_rows(dacc, row, t, 1) * w_ref[0:1, :] + dacc * w_ref[1:2, :]
                              + _shift_rows(dacc, row, t, -1) * w_ref[2:3, :])
            dw_ref[0:1, :] = _dot(ones, _bf(dacc * um), 1, 0)[0:1, :]
            dw_ref[1:2, :] = _dot(ones, _bf(dacc * uu), 1, 0)[0:1, :]
            dw_ref[2:3, :] = _dot(ones, _bf(dacc * up), 1, 0)[0:1, :]
            db_ref[...] = _dot(ones, _bf(dacc), 1, 0)[0:1, :]

        emit(da_ * val * (sg * (1.0 + gate * (1.0 - sg))), ug_m, ug, ug_p, wg_ref, dug_ref, dwg_ref, dbg_ref)
        emit(da_ * gate * sg, uv_m, uv, uv_p, wv_ref, duv_ref, dwv_ref, dbv_ref)

    col = lambda rows, off: pl.BlockSpec((rows, CONV_TN), lambda j: (0, j + off))
    half_shapes = [jax.ShapeDtypeStruct((t, D_FF), BF16), jax.ShapeDtypeStruct((3, D_FF), F32),
                   jax.ShapeDtypeStruct((1, D_FF), F32)]
    outs = _call(
        body, (u, u, conv_w, conv_w, conv_b, conv_b, da), name="conv_gate_bwd", grid=(nt,),
        in_specs=[col(t, 0), col(t, nt), col(3, 0), col(3, nt), col(1, 0), col(1, nt), col(t, 0)],
        out_specs=[col(t, 0), col(3, 0), col(1, 0)] * 2, out_shape=half_shapes * 2, after=after)
    return outs[:3], outs[3:]


def _row_tile(r, cap):
    best = r
    for cand in range(16, cap + 1, 16):
        if r % cand == 0:
            best = cand
    return best


def _sum_parts(own, got, me, name, tr=256):
    _, r, c = got.shape
    tr = _row_tile(r, tr)

    def body(me_ref, own_ref, got_ref, o_ref):
        mine = own_ref[...].astype(F32)
        acc = None
        for i in range(N_DEV):
            term = jnp.where(me_ref[0] == i, mine, got_ref[i].astype(F32))
            acc = term if acc is None else acc + term
        o_ref[...] = acc

    return pl.pallas_call(
        body, name=name,
        grid_spec=pltpu.PrefetchScalarGridSpec(
            num_scalar_prefetch=1, grid=(r // tr,),
            in_specs=[pl.BlockSpec((None, tr, c), lambda i, me_ref: (me_ref[0], i, 0)),
                      pl.BlockSpec((N_DEV, tr, c), lambda i, me_ref: (0, i, 0))],
            out_specs=pl.BlockSpec((tr, c), lambda i, me_ref: (i, 0))),
        out_shape=jax.ShapeDtypeStruct((r, c), F32),
        compiler_params=_params(),
    )(me, own, got)


def _adamw_math(w_, g_, m_, v_):
    m_ = ADAM_B1 * m_ + (1.0 - ADAM_B1) * g_
    v_ = ADAM_B2 * v_ + (1.0 - ADAM_B2) * (g_ * g_)
    m_hat = m_ / (1.0 - ADAM_B1 ** ADAM_STEP)
    v_hat = v_ / (1.0 - ADAM_B2 ** ADAM_STEP)
    return -ADAM_LR * (m_hat / (jnp.sqrt(v_hat) + ADAM_EPS) + ADAM_WD * w_), m_, v_


def _sum_adamw(own, got, me, w, m, v, name, tr=256):
    _, r, c = got.shape
    tr = _row_tile(r, tr)

    def body(me_ref, own_ref, got_ref, w_ref, m_ref, v_ref, g_ref, d_ref, nm_ref, nv_ref):
        mine = own_ref[...].astype(F32)
        acc = None
        for i in range(N_DEV):
            term = jnp.where(me_ref[0] == i, mine, got_ref[i].astype(F32))
            acc = term if acc is None else acc + term
        g_ref[...] = acc
        d_ref[...], nm_ref[...], nv_ref[...] = _adamw_math(w_ref[...], acc, m_ref[...], v_ref[...])

    blk = pl.BlockSpec((tr, c), lambda i, me_ref: (i, 0))
    return pl.pallas_call(
        body, name=name,
        grid_spec=pltpu.PrefetchScalarGridSpec(
            num_scalar_prefetch=1, grid=(r // tr,),
            in_specs=[pl.BlockSpec((None, tr, c), lambda i, me_ref: (me_ref[0], i, 0)),
                      pl.BlockSpec((N_DEV, tr, c), lambda i, me_ref: (0, i, 0)), blk, blk, blk],
            out_specs=[blk] * 4),
        out_shape=[jax.ShapeDtypeStruct((r, c), F32)] * 4,
        compiler_params=_params(),
    )(me, own, got, w, m, v)


def _adamw(w, g, m, v, name, tr=256):
    r, c = w.shape
    tr = _row_tile(r, tr)

    def body(w_ref, g_ref, m_ref, v_ref, d_ref, nm_ref, nv_ref):
        d_ref[...], nm_ref[...], nv_ref[...] = _adamw_math(w_ref[...], g_ref[...], m_ref[...], v_ref[...])

    blk = pl.BlockSpec((tr, c), lambda i: (i, 0))
    return pl.pallas_call(
        body, name=name, grid=(r // tr,),
        in_specs=[blk] * 4, out_specs=[blk] * 3,
        out_shape=[jax.ShapeDtypeStruct((r, c), F32)] * 3,
        compiler_params=_params(),
    )(w, g, m, v)


def _mesh_pos():
    return lax.axis_index("x"), lax.axis_index("y"), lax.axis_index("c")


def _dev_index(px, py, pc):
    return 4 * px + 2 * py + pc


class _Gather:
    def __init__(self, arrs):
        self.arrs = list(arrs)
        n = len(self.arrs)
        self.out_shape = [jax.ShapeDtypeStruct((N_DEV,) + a.shape, a.dtype) for a in self.arrs]
        self.sems = [pltpu.SemaphoreType.DMA((7, n)), pltpu.SemaphoreType.DMA((7, n)), pltpu.SemaphoreType.DMA((n,))]

    def _ctx(self, ins, outs, sems):
        send_sems, recv_sems, local_sems = sems
        x, y, c = _mesh_pos()
        chips = [(1 - x, y), (x, 1 - y), (1 - x, 1 - y)]

        def copy(k, a, block, to, src=None):
            dst = outs[a].at[_dev_index(*block)]
            return pltpu.make_async_remote_copy(
                src_ref=dst if src is None else src, dst_ref=dst, send_sem=send_sems.at[k, a], recv_sem=recv_sems.at[k, a],
                device_id=to, device_id_type=MESH_T)

        n = len(ins)
        me, sibling = (x, y, c), (x, y, 1 - c)
        mine = [pltpu.make_async_copy(ins[a], outs[a].at[_dev_index(*me)], local_sems.at[a]) for a in range(n)]
        first = [copy(0, a, me, sibling, src=ins[a]) for a in range(n)]
        first += [copy(1 + j, a, me, (*chip, c), src=ins[a]) for j, chip in enumerate(chips) for a in range(n)]
        passed = [copy(4 + j, a, (*chip, c), sibling) for j, chip in enumerate(chips) for a in range(n)]
        return n, c, me, sibling, chips, copy, mine, first, passed

    def start(self, ins, outs, sems):
        _, _, _, _, _, _, mine, first, _ = self._ctx(ins, outs, sems)
        for cp in mine + first:
            cp.start()

    def forward(self, ins, outs, sems):
        n, c, me, _, chips, copy, _, _, passed = self._ctx(ins, outs, sems)
        for j, chip in enumerate(chips):
            for a in range(n):
                copy(1 + j, a, (*chip, c), me).wait_recv()
                passed[j * n + a].start()

    def finish(self, ins, outs, sems):
        n, c, me, sibling, chips, copy, mine, first, passed = self._ctx(ins, outs, sems)
        for a in range(n):
            copy(0, a, sibling, me).wait_recv()
        for j, chip in enumerate(chips):
            for a in range(n):
                copy(4 + j, a, (*chip, 1 - c), me).wait_recv()
        for cp in first + passed:
            cp.wait_send()
        for cp in mine:
            cp.wait()


def _comm_alone(comm, name):
    n = len(comm.arrs)

    def body(*refs):
        ins, outs, sems = refs[:n], refs[n:2 * n], refs[2 * n:]
        comm.start(ins, outs, sems)
        if comm.forward is not None:
            comm.forward(ins, outs, sems)
        comm.finish(ins, outs, sems)

    any_spec = pl.BlockSpec(memory_space=pl.ANY)
    return pl.pallas_call(body, name=name, in_specs=[any_spec] * n, out_specs=[any_spec] * n, out_shape=comm.out_shape,
                          scratch_shapes=comm.sems)(*comm.arrs)


def _peers(x, y, c):
    out = []
    for k in range(1, N_DEV):
        pos = (1 - x if k & 4 else x, 1 - y if k & 2 else y, 1 - c if k & 1 else c)
        out.append((k, pos, _dev_index(*pos)))
    return out


def _exchange_begin(arrs, tag):
    n = len(arrs)
    lands = [lax.empty(a.shape, a.dtype) for a in arrs]

    def start_body(*refs):
        ins, land = refs[:n], refs[n:2 * n]
        send_sems, recv_sems, token = refs[2 * n], refs[2 * n + 1], refs[-1]
        x, y, c = _mesh_pos()
        me_i = _dev_index(x, y, c)
        for k, pos, peer_i in _peers(x, y, c):
            for a in range(n):
                pltpu.make_async_remote_copy(
                    src_ref=ins[a].at[peer_i], dst_ref=land[a].at[me_i], send_sem=send_sems.at[(k - 1) * n + a],
                    recv_sem=recv_sems.at[(k - 1) * n + a], device_id=pos, device_id_type=MESH_T).start()
        token[...] = jnp.zeros_like(token)

    hbm = pl.BlockSpec(memory_space=pltpu.HBM)
    sem = pl.BlockSpec(memory_space=pltpu.SEMAPHORE)
    thru = [pltpu.HBM(a.shape, a.dtype) for a in arrs]
    outs = pl.pallas_call(
        start_body, name="exchange_start_" + tag,
        out_shape=[pltpu.SemaphoreType.DMA((7 * n,)), pltpu.SemaphoreType.DMA((7 * n,))] + thru + thru
        + [jax.ShapeDtypeStruct((8, LANE), F32)],
        in_specs=[hbm] * (2 * n), out_specs=[sem, sem] + [hbm] * (2 * n) + [pl.BlockSpec(memory_space=pltpu.VMEM)],
        input_output_aliases={i: 2 + i for i in range(2 * n)},
        compiler_params=pltpu.CompilerParams(has_side_effects=pltpu.SideEffectType.DATAFLOW_SIDE_EFFECTING),
    )(*[pltpu.with_memory_space_constraint(a, pltpu.HBM) for a in list(arrs) + list(lands)])
    return (tag, n, outs[0], outs[1], outs[2:2 + n], outs[2 + n:2 + 2 * n]), outs[-1]


def _exchange_end(handle, after):
    tag, n, send_sems, recv_sems, srcs, lands = handle

    def body(*refs):
        ins, land = refs[:n], refs[n:2 * n]
        send_sems_, recv_sems_ = refs[2 * n], refs[2 * n + 1]
        x, y, c = _mesh_pos()
        me_i = _dev_index(x, y, c)
        for k, pos, peer_i in _peers(x, y, c):
            for a in range(n):
                cp = pltpu.make_async_remote_copy(
                    src_ref=ins[a].at[peer_i], dst_ref=land[a].at[me_i], send_sem=send_sems_.at[(k - 1) * n + a],
                    recv_sem=recv_sems_.at[(k - 1) * n + a], device_id=pos, device_id_type=MESH_T)
                cp.wait_send()
                cp.wait_recv()

    hbm = pl.BlockSpec(memory_space=pltpu.HBM)
    sem = pl.BlockSpec(memory_space=pltpu.SEMAPHORE)
    outs = pl.pallas_call(
        body, name="exchange_end_" + tag, out_shape=[pltpu.HBM(a.shape, a.dtype) for a in list(srcs) + list(lands)],
        in_specs=[hbm] * (2 * n) + [sem, sem, pl.BlockSpec(memory_space=pl.ANY)], out_specs=[hbm] * (2 * n),
        input_output_aliases={i: i for i in range(2 * n)},
        compiler_params=pltpu.CompilerParams(has_side_effects=pltpu.SideEffectType.DATAFLOW_SIDE_EFFECTING),
    )(*srcs, *lands, send_sems, recv_sems, after)
    return list(zip(outs[:n], outs[n:]))


def _gather_begin(arrs, tag):
    n = len(arrs)
    me = _dev_index(*_mesh_pos())
    lands = [lax.dynamic_update_slice(lax.empty((N_DEV,) + a.shape, a.dtype), a[None], (me,) + (0,) * a.ndim) for a in arrs]

    def start_body(*refs):
        ins, land = refs[:n], refs[n:2 * n]
        send_sems, recv_sems, token = refs[2 * n], refs[2 * n + 1], refs[-1]
        x, y, c = _mesh_pos()
        me_i = _dev_index(x, y, c)
        for a in range(n):
            for k, pos, _ in _peers(x, y, c):
                pltpu.make_async_remote_copy(
                    src_ref=ins[a], dst_ref=land[a].at[me_i], send_sem=send_sems.at[(k - 1) * n + a],
                    recv_sem=recv_sems.at[(k - 1) * n + a], device_id=pos, device_id_type=MESH_T).start()
        token[...] = jnp.zeros_like(token)

    hbm = pl.BlockSpec(memory_space=pltpu.HBM)
    sem = pl.BlockSpec(memory_space=pltpu.SEMAPHORE)
    thru = [pltpu.HBM(a.shape, a.dtype) for a in list(arrs) + lands]
    outs = pl.pallas_call(
        start_body, name="gather_start_" + tag,
        out_shape=[pltpu.SemaphoreType.DMA((7 * n,)), pltpu.SemaphoreType.DMA((7 * n,))] + thru
        + [jax.ShapeDtypeStruct((8, LANE), F32)],
        in_specs=[hbm] * (2 * n), out_specs=[sem, sem] + [hbm] * (2 * n) + [pl.BlockSpec(memory_space=pltpu.VMEM)],
        input_output_aliases={i: 2 + i for i in range(2 * n)},
        compiler_params=pltpu.CompilerParams(has_side_effects=pltpu.SideEffectType.DATAFLOW_SIDE_EFFECTING),
    )(*[pltpu.with_memory_space_constraint(a, pltpu.HBM) for a in list(arrs) + lands])
    return (tag, n, outs[0], outs[1], outs[2:2 + n], outs[2 + n:2 + 2 * n]), outs[-1]


def _gather_end(handle, which, after):
    tag, n, send_sems, recv_sems, srcs, lands = handle
    m = len(which)

    def body(*refs):
        ins, land = refs[:m], refs[m:2 * m]
        send_sems_, recv_sems_ = refs[2 * m], refs[2 * m + 1]
        x, y, c = _mesh_pos()
        me_i = _dev_index(x, y, c)
        for j, a in enumerate(which):
            for k, pos, _ in _peers(x, y, c):
                cp = pltpu.make_async_remote_copy(
                    src_ref=ins[j], dst_ref=land[j].at[me_i], send_sem=send_sems_.at[(k - 1) * n + a],
                    recv_sem=recv_sems_.at[(k - 1) * n + a], device_id=pos, device_id_type=MESH_T)
                cp.wait_send()
                cp.wait_recv()

    hbm = pl.BlockSpec(memory_space=pltpu.HBM)
    sem = pl.BlockSpec(memory_space=pltpu.SEMAPHORE)
    ops = [srcs[a] for a in which] + [lands[a] for a in which]
    outs = pl.pallas_call(
        body, name="gather_end_%s_%s" % (tag, "_".join(str(a) for a in which)),
        out_shape=[pltpu.HBM(a.shape, a.dtype) for a in ops],
        in_specs=[hbm] * (2 * m) + [sem, sem, pl.BlockSpec(memory_space=pl.ANY)], out_specs=[hbm] * (2 * m),
        input_output_aliases={i: i for i in range(2 * m)},
        compiler_params=pltpu.CompilerParams(has_side_effects=pltpu.SideEffectType.DATAFLOW_SIDE_EFFECTING),
    )(*ops, send_sems, recv_sems, after)
    return list(outs[m:])


_SPLIT = dict(has_side_effects=pltpu.SideEffectType.DATAFLOW_SIDE_EFFECTING)


def _chips(x, y):
    return [(1 - x, y), (x, 1 - y), (1 - x, 1 - y)]


def _gather2_begin(arrs, tag):
    n = len(arrs)
    me = _dev_index(*_mesh_pos())
    lands = [lax.dynamic_update_slice(lax.empty((N_DEV,) + a.shape, a.dtype), a[None], (me,) + (0,) * a.ndim) for a in arrs]

    def body(*refs):
        ins, land = refs[:n], refs[n:2 * n]
        send1, recv1, token = refs[2 * n], refs[2 * n + 1], refs[-1]
        x, y, c = _mesh_pos()
        me_i = _dev_index(x, y, c)
        targets = [(x, y, 1 - c)] + [(*chip, c) for chip in _chips(x, y)]
        for a in range(n):
            for k, pos in enumerate(targets):
                pltpu.make_async_remote_copy(
                    src_ref=ins[a], dst_ref=land[a].at[me_i], send_sem=send1.at[k * n + a], recv_sem=recv1.at[k * n + a],
                    device_id=pos, device_id_type=MESH_T).start()
        token[...] = jnp.zeros_like(token)

    hbm = pl.BlockSpec(memory_space=pltpu.HBM)
    sem = pl.BlockSpec(memory_space=pltpu.SEMAPHORE)
    thru = [pltpu.HBM(a.shape, a.dtype) for a in list(arrs) + lands]
    outs = pl.pallas_call(
        body, name="gather_start_" + tag,
        out_shape=[pltpu.SemaphoreType.DMA((4 * n,)), pltpu.SemaphoreType.DMA((4 * n,))] + thru
        + [jax.ShapeDtypeStruct((8, LANE), F32)],
        in_specs=[hbm] * (2 * n), out_specs=[sem, sem] + [hbm] * (2 * n) + [pl.BlockSpec(memory_space=pltpu.VMEM)],
        input_output_aliases={i: 2 + i for i in range(2 * n)}, compiler_params=pltpu.CompilerParams(**_SPLIT),
    )(*[pltpu.with_memory_space_constraint(a, pltpu.HBM) for a in list(arrs) + lands])
    return dict(tag=tag, n=n, send1=outs[0], recv1=outs[1], srcs=list(outs[2:2 + n]), lands=list(outs[2 + n:2 + 2 * n])), outs[-1]


def _gather2_forward(handle, which, after, part):
    n, m = handle["n"], len(which)

    def body(*refs):
        land, recv1 = refs[:m], refs[m]
        send2, recv2, token = refs[m + 2], refs[m + 3], refs[-1]
        token[...] = jnp.zeros_like(token)
        x, y, c = _mesh_pos()
        for j, a in enumerate(which):
            for k, chip in enumerate(_chips(x, y)):
                blk = land[j].at[_dev_index(*chip, c)]
                copy = pltpu.make_async_remote_copy(
                    src_ref=blk, dst_ref=blk, send_sem=send2.at[k * m + j], recv_sem=recv2.at[k * m + j],
                    device_id=(x, y, 1 - c), device_id_type=MESH_T)
                pltpu.make_async_remote_copy(
                    src_ref=blk, dst_ref=blk, send_sem=send2.at[k * m + j], recv_sem=recv1.at[(1 + k) * n + a],
                    device_id=(*chip, c), device_id_type=MESH_T).wait_recv()
                copy.start()

    hbm = pl.BlockSpec(memory_space=pltpu.HBM)
    sem = pl.BlockSpec(memory_space=pltpu.SEMAPHORE)
    lands = [handle["lands"][a] for a in which]
    outs = pl.pallas_call(
        body, name="gather_forward_%s_%s" % (handle["tag"], part),
        out_shape=[pltpu.SemaphoreType.DMA((3 * m,)), pltpu.SemaphoreType.DMA((3 * m,))] + [pltpu.HBM(a.shape, a.dtype) for a in lands]
        + [jax.ShapeDtypeStruct((8, LANE), F32)],
        in_specs=[hbm] * m + [sem, pl.BlockSpec(memory_space=pl.ANY)],
        out_specs=[sem, sem] + [hbm] * m + [pl.BlockSpec(memory_space=pltpu.VMEM)],
        input_output_aliases={i: 2 + i for i in range(m)}, compiler_params=pltpu.CompilerParams(**_SPLIT),
    )(*lands, handle["recv1"], after)
    for j, a in enumerate(which):
        handle["lands"][a] = outs[2 + j]
        handle[("leg2", a)] = (outs[0], outs[1], j, m)
    return outs[-1]


def _gather2_end(handle, a, after):
    n = handle["n"]
    send2, recv2, j, m = handle[("leg2", a)]

    def body(src, land, send1, recv1, send2_, recv2_, after_ref, src_out, land_out):
        x, y, c = _mesh_pos()
        me_i = _dev_index(x, y, c)
        sibling = (x, y, 1 - c)
        for k, pos in enumerate([sibling] + [(*chip, c) for chip in _chips(x, y)]):
            first = pltpu.make_async_remote_copy(
                src_ref=src, dst_ref=land.at[me_i], send_sem=send1.at[k * n + a], recv_sem=recv1.at[k * n + a],
                device_id=pos, device_id_type=MESH_T)
            first.wait_send()
            if k == 0:
                first.wait_recv()
        for k in range(3):
            second = pltpu.make_async_remote_copy(
                src_ref=land.at[me_i], dst_ref=land.at[me_i], send_sem=send2_.at[k * m + j], recv_sem=recv2_.at[k * m + j],
                device_id=sibling, device_id_type=MESH_T)
            second.wait_send()
            second.wait_recv()

    hbm = pl.BlockSpec(memory_space=pltpu.HBM)
    sem = pl.BlockSpec(memory_space=pltpu.SEMAPHORE)
    src, land = handle["srcs"][a], handle["lands"][a]
    outs = pl.pallas_call(
        body, name="gather_end_%s_%d" % (handle["tag"], a),
        out_shape=[pltpu.HBM(src.shape, src.dtype), pltpu.HBM(land.shape, land.dtype)],
        in_specs=[hbm, hbm, sem, sem, sem, sem, pl.BlockSpec(memory_space=pl.ANY)], out_specs=[hbm, hbm],
        input_output_aliases={0: 0, 1: 1}, compiler_params=pltpu.CompilerParams(**_SPLIT),
    )(src, land, handle["send1"], handle["recv1"], send2, recv2, after)
    return outs[1]


def _call(body, operands, *, name, grid, in_specs, out_specs, out_shape, scratch_shapes=(), after=()):
    ni, nd = len(in_specs), len(after)

    def ordered(*refs):
        body(*refs[:ni], *refs[ni + nd:])

    outs = pl.pallas_call(
        ordered, name=name, grid=grid, in_specs=list(in_specs) + [pl.BlockSpec(memory_space=pl.ANY)] * nd,
        out_specs=out_specs, out_shape=out_shape, scratch_shapes=list(scratch_shapes), compiler_params=_params(),
    )(*operands, *after)
    return list(outs)


def _rows(a):
    return a.reshape(-1, a.shape[-1])


def _slots(a):
    return a.reshape(N_DEV, -1, a.shape[-1])


def _local_step(x, mem, tgt, w, wire):
    tables = _rope_tables(x.shape[0])
    bd = _head_sum_matrix()
    two = lambda g: jnp.tile(g, (1, 2))
    gq2, gk2 = two(w["q_norm_g"]), two(w["k_norm_g"])

    w_in_t, hg_lb, conv_w = _comm_alone(_Gather([wire["w_in"], w["hg_lb"].reshape(4, -1), w["conv_w"][0]]), "gather_w_in")
    w_in_t = _rows(w_in_t)
    hg_lb = jnp.transpose(hg_lb.reshape(N_DEV, 2, 2, -1), (1, 2, 0, 3)).reshape(2, 2, HG_DIM)
    conv_w = jnp.transpose(conv_w, (1, 0, 2)).reshape(3, 2 * D_FF)
    lb_a0, lb_a1 = hg_lb[:, 0, :], hg_lb[:, 1, :]

    order = ("w_out", "w_xq", "w_xkv", "w_xo", "w_up", "w_down")
    w_in_t, *later = lax.optimization_barrier((w_in_t, *[wire[n] for n in order]))
    fetch, started = _gather2_begin(later, "w")
    take = lambda n, after: _rows(_gather2_end(fetch, order.index(n), after))

    h1, p = _norm_mm(x, w["pre_mix_g"], w_in_t, True, N_IN, "in_proj", after=[started], p_dtype=BF16)
    qr, kr = _qk_prep(p, gq2, gk2, tables, bd)
    att, att32, lse = _attn_fwd(qr, kr, p)
    passed = _gather2_forward(fetch, [0, 1, 2, 3], att, "a")
    lb = _lower_bounds(lb_a0, lb_a1)
    of, s_f = _hgrn_fwd(p, lb[0:1], False, after=[passed])
    ob, s_b = _hgrn_fwd(p, lb[1:2], True, after=[passed, of])
    passed = _gather2_forward(fetch, [4, 5], ob, "b")
    w_out = take("w_out", ob)
    cat, mixed, x1 = _mix_out(att, of, ob, p, w["hg_out_norm_g"], w_out, w["post_mix_g"], x, after=[passed])

    w_xq = take("w_xq", x1)
    h2, q2 = _norm_mm(x1, w["pre_x_g"], w_xq, False, 1024, "xq_proj", p_dtype=BF16)
    w_xkv_t = take("w_xkv", q2)
    mn, kv = _norm_mm(mem, w["mem_norm_g"], w_xkv_t, True, 2 * D_MODEL, "xkv_proj")
    o2 = _xattn_fwd(q2, kv)
    w_xo = take("w_xo", o2)
    y2, x2 = _mm_postnorm_res(o2, w_xo, w["post_x_g"], x1, "xo_proj")

    w_up_t = take("w_up", x2)
    h3, u = _norm_mm(x2, w["pre_ffn_g"], w_up_t, True, 2 * D_FF, "up_proj", tm=256, p_dtype=BF16)
    a = _conv_gate_fwd(u, conv_w, w["conv_b"])
    w_down = take("w_down", a)
    y3, dx3, loss = _mm_postnorm_res_loss(a, w_down, w["post_ffn_g"], x2, tgt, "down_proj")

    g, pending = {}, {}
    dy3, da, g["post_ffn_g"] = _postnorm_bwd_mm(dx3, y3, w["post_ffn_g"], w_down, "down_bwd", BF16)
    pending["down"], started = _exchange_begin([_slots(_dw(a, dy3, "dw_down", tka=D_FF // 2))], "down")
    (du_g, dcw_g, dcb_g), (du_v, dcw_v, dcb_v) = _conv_gate_bwd(u, conv_w, w["conv_b"], da, after=[started])
    g["conv_w"] = jnp.concatenate([dcw_g, dcw_v], axis=1)
    g["conv_b"] = jnp.concatenate([dcb_g, dcb_v], axis=1)
    pending["up"], started = _exchange_begin([_slots(_dw([du_g, du_v], h3, "dw_up", tka=D_FF // 2))], "up")
    dx2, g["pre_ffn_g"] = _mm_prenorm_bwd([du_g, du_v], w_up_t, True, x2, w["pre_ffn_g"], dx3, "up_bwd", after=[started], tm=256)

    dy2, do2, g["post_x_g"] = _postnorm_bwd_mm(dx2, y2, w["post_x_g"], w_xo, "xo_bwd", BF16)
    dw_xo = _dw(o2, dy2, "dw_xo", tka=512)
    dq2, dkv = _xattn_bwd(q2, kv, do2)
    dw_xq = _dw(h2, dq2, "dw_xq", tka=512)
    dkvb = _bf(dkv)
    dw_xkv = _dw(dkvb, mn, "dw_xkv")
    _, g["mem_norm_g"] = _mm_prenorm_bwd(dkvb, w_xkv_t, True, mem, w["mem_norm_g"], jnp.zeros_like(mem), "xkv_bwd")
    dx1, g["pre_x_g"] = _mm_prenorm_bwd(dq2, w_xq, False, x1, w["pre_x_g"], dx2, "xq_bwd")

    dmixed, dcat, g["post_mix_g"] = _postnorm_bwd_mm(dx1, mixed, w["post_mix_g"], w_out, "out_bwd", BF16)
    dw_out = _dw(cat, dmixed, "dw_out", tka=512)
    pending["x"], started = _exchange_begin([_slots(dw_xo), _slots(dw_xq), _slots(dw_xkv), _slots(dw_out)], "x")
    do, dhg, g["hg_out_norm_g"] = _rec_bwd(dcat, of, ob, p, w["hg_out_norm_g"])
    dhq_f, dz_f, dhi_f, dlb_f = _hgrn_bwd(p, lb[0:1], do, s_f, False, after=[started])
    dhq, dz_b, dhi, dlb_b = _hgrn_bwd(p, lb[1:2], do, s_b, True, other=(dhq_f, dhi_f))
    d_a0, d_a1 = _lower_bounds_bwd(lb_a0, lb_a1, jnp.concatenate([dlb_f, dlb_b], axis=0))
    g["hg_lb"] = jnp.stack([d_a0, d_a1], axis=1)
    dqr, dkr, dv = _attn_bwd(qr, kr, p, dcat, att32, lse)
    dp_att, dgq, dgk = _qk_prep_bwd(p, dqr, dkr, dv, gq2, gk2, tables, bd)
    g["q_norm_g"], g["k_norm_g"] = dgq, dgk
    dp = [dp_att, dhq, dz_f, dz_b, dhi, dhg]

    early = [n for n in _SMALL if n != "pre_mix_g"]
    vals = [g[n].reshape(-1, g[n].shape[-1]) for n in early] + [jnp.pad(loss, ((0, 0), (0, LANE - 1)))]
    small = dict(early=early, shapes=[val.shape for val in vals])
    small["fetch"], started = _gather_begin([_pack_small(vals, "pack_small")], "small")
    pending["in"], started = _exchange_begin([_slots(_dw(dp, h1, "dw_in", after=[started]))], "in")
    dx, g["pre_mix_g"] = _mm_prenorm_bwd(dp, w_in_t, True, x, w["pre_mix_g"], dx1, "in_bwd", after=[started])
    small["fetch_last"], _ = _gather_begin([_pack_small([g["pre_mix_g"]], "pack_last")], "last")
    return dx, g, pending, small


_COL_SHARDED = ("w_in", "w_xkv", "w_up")
_ROW_SHARDED = ("w_out", "w_xq", "w_xo", "w_down")
_REPLICATED = ("pre_mix_g", "q_norm_g", "k_norm_g", "hg_out_norm_g", "post_mix_g", "pre_x_g", "mem_norm_g", "post_x_g",
               "pre_ffn_g", "conv_b", "post_ffn_g")
_WEIGHTS = ("pre_mix_g", "w_in", "q_norm_g", "k_norm_g", "hg_lb", "hg_out_norm_g", "w_out", "post_mix_g", "pre_x_g",
            "mem_norm_g", "w_xq", "w_xkv", "w_xo", "post_x_g", "pre_ffn_g", "w_up", "conv_w", "conv_b", "w_down",
            "post_ffn_g")
_SMALL = _REPLICATED + ("hg_lb", "conv_w")
_ADAM_TRANSPOSED = ("w_in", "w_up")
PACK_W = 1024


def _small_plan(shapes):
    plan, r = [], 0
    for vi, (rows, cols) in enumerate(shapes):
        for i in range(rows):
            for c0 in range(0, cols, PACK_W):
                plan.append((vi, i, c0, min(PACK_W, cols - c0), r))
                r += 1
    return plan, -(-r // 8) * 8


def _pack_small(vals, name):
    plan, nrows = _small_plan([val.shape for val in vals])

    def body(*refs):
        ins, out = refs[:-1], refs[-1]
        out[...] = jnp.zeros_like(out)
        for vi, i, c0, width, r in plan:
            out[r:r + 1, 0:width] = ins[vi][i:i + 1, c0:c0 + width]

    return pl.pallas_call(body, name=name, out_shape=jax.ShapeDtypeStruct((nrows, PACK_W), F32))(*vals)


def _sum_unpack_small(packs, shapes, name):
    plan, _ = _small_plan(shapes)

    def body(*refs):
        p_ref, outs = refs[0], refs[1:]
        acc = p_ref[0]
        for i in range(1, N_DEV):
            acc = acc + p_ref[i]
        for vi, i, c0, width, r in plan:
            outs[vi][i:i + 1, c0:c0 + width] = acc[r:r + 1, 0:width]

    return pl.pallas_call(body, name=name, out_shape=[jax.ShapeDtypeStruct(s, F32) for s in shapes])(packs)


def _adamw_many(ws, gs, ms, vs):
    n = len(ws)

    def body(*refs):
        w_refs, g_refs, m_refs, v_refs = (refs[k * n:(k + 1) * n] for k in range(4))
        d_refs, nm_refs, nv_refs = (refs[(4 + k) * n:(5 + k) * n] for k in range(3))
        for k in range(n):
            g_ = g_refs[k][...]
            m_ = ADAM_B1 * m_refs[k][...] + (1.0 - ADAM_B1) * g_
            v_ = ADAM_B2 * v_refs[k][...] + (1.0 - ADAM_B2) * (g_ * g_)
            m_hat = m_ / (1.0 - ADAM_B1 ** ADAM_STEP)
            v_hat = v_ / (1.0 - ADAM_B2 ** ADAM_STEP)
            d_refs[k][...] = -ADAM_LR * (m_hat / (jnp.sqrt(v_hat) + ADAM_EPS) + ADAM_WD * w_refs[k][...])
            nm_refs[k][...] = m_
            nv_refs[k][...] = v_

    shapes = [jax.ShapeDtypeStruct(a.shape, F32) for a in ws]
    outs = pl.pallas_call(body, name="adamw_small", out_shape=shapes * 3)(*ws, *gs, *ms, *vs)
    return outs[:n], outs[n:2 * n], outs[2 * n:]


def kernel(x, mem, pre_mix_g, w_in, q_norm_g, k_norm_g, hg_lb, hg_out_norm_g, w_out, post_mix_g, pre_x_g, mem_norm_g, w_xq, w_xkv, w_xo, post_x_g, pre_ffn_g, w_up, conv_w, conv_b, w_down, post_ffn_g, loss_target, m_pre_mix_g, m_w_in, m_q_norm_g, m_k_norm_g, m_hg_lb, m_hg_out_norm_g, m_w_out, m_post_mix_g, m_pre_x_g, m_mem_norm_g, m_w_xq, m_w_xkv, m_w_xo, m_post_x_g, m_pre_ffn_g, m_w_up, m_conv_w, m_conv_b, m_w_down, m_post_ffn_g, v_pre_mix_g, v_w_in, v_q_norm_g, v_k_norm_g, v_hg_lb, v_hg_out_norm_g, v_w_out, v_post_mix_g, v_pre_x_g, v_mem_norm_g, v_w_xq, v_w_xkv, v_w_xo, v_post_x_g, v_pre_ffn_g, v_w_up, v_conv_w, v_conv_b, v_w_down, v_post_ffn_g):
    args = dict(locals())
    w = {n: args[n] for n in _WEIGHTS}
    m = {n: args["m_" + n] for n in _WEIGHTS}
    v = {n: args["v_" + n] for n in _WEIGHTS}
    me = _dev_index(*_mesh_pos())

    wire = {n: _bf(w[n][0].T) for n in _COL_SHARDED}
    wire.update({n: _bf(w[n][0]) for n in _ROW_SHARDED})

    grad_x, g, pending, small = _local_step(x[0], mem[0], loss_target[0], w, wire)

    grads, delta, new_m, new_v = {}, {}, {}, {}

    me_arr = jnp.reshape(me, (1,)).astype(jnp.int32)

    def update(n, parts):
        if n in _ADAM_TRANSPOSED:
            outs = _sum_adamw(*parts, me_arr, w[n][0].T, m[n][0].T, v[n][0].T, "update_" + n)
            grads[n], delta[n], new_m[n], new_v[n] = (a.T[None] for a in outs)
        elif n in _COL_SHARDED:
            gsum = _sum_parts(*parts, me_arr, "sum_" + n).T
            grads[n] = gsum[None]
            delta[n], new_m[n], new_v[n] = (a[None] for a in _adamw(w[n][0], gsum, m[n][0], v[n][0], "adamw_" + n))
        else:
            outs = _sum_adamw(*parts, me_arr, w[n][0], m[n][0], v[n][0], "update_" + n)
            grads[n], delta[n], new_m[n], new_v[n] = (a[None] for a in outs)

    after = grad_x
    for tag, names in (("down", ["w_down"]), ("up", ["w_up"]), ("x", ["w_xo", "w_xq", "w_xkv", "w_out"]),
                       ("in", ["w_in"])):
        for n, parts in zip(names, _exchange_end(pending[tag], after)):
            update(n, parts)
            after = new_v[n]

    (packs,) = _gather_end(small["fetch"], [0], after)
    summed = _sum_unpack_small(packs, small["shapes"], "sum_unpack_small")
    loss = summed[-1][0, 0]
    for n, s in zip(small["early"], summed[:-1]):
        grads[n] = s
    (packs,) = _gather_end(small["fetch_last"], [0], after)
    (grads["pre_mix_g"],) = _sum_unpack_small(packs, [w["pre_mix_g"].shape], "sum_unpack_last")
    small = _SMALL
    fold = lambda v2: v2[:, :ATT_HEAD_DIM] + v2[:, ATT_HEAD_DIM:]
    grads["q_norm_g"], grads["k_norm_g"] = fold(grads["q_norm_g"]), fold(grads["k_norm_g"])
    grads["hg_lb"] = lax.dynamic_slice_in_dim(grads["hg_lb"].reshape(2, 2, HG_DIM), me * (HG_DIM // N_DEV),
                                              HG_DIM // N_DEV, axis=2)
    grads["conv_w"] = lax.dynamic_slice_in_dim(grads["conv_w"], me * (2 * D_FF // N_DEV), 2 * D_FF // N_DEV, axis=1)[None]

    flat2 = lambda a: a.reshape(-1, a.shape[-1])
    outs = _adamw_many(*[[flat2(d[n]) for n in small] for d in (w, grads, m, v)])
    for dst, vals in zip((delta, new_m, new_v), outs):
        for n, val in zip(small, vals):
            dst[n] = val.reshape(w[n].shape)

    return (loss, grad_x[None], *[grads[n] for n in _WEIGHTS], *[delta[n] for n in _WEIGHTS],
            *[new_m[n] for n in _WEIGHTS], *[new_v[n] for n in _WEIGHTS])
```
